```python
import math
import jax, jax.numpy as jnp
from jax import lax
import numpy as np

D_MODEL = 2048
BATCH = 8
SEQ = 2048
DEPTH = 1

HEAD_DIM = 128
MEM_LEN = 256
ATT_GROUPS = ((128, 1), (512, 4), (2048, 16))
ATT_HEADS_PER_GROUP = 4
N_ATT_GROUPS = len(ATT_GROUPS)
ATT_WIDTH = N_ATT_GROUPS * ATT_HEADS_PER_GROUP * HEAD_DIM
ATT_OUT_WIDTH = ATT_HEADS_PER_GROUP * HEAD_DIM
ATT_BLOCK = 128
HG_HEADS = 8
HG_KEY = 128
HG_VAL = 128
HG_WIDTH = HG_HEADS * HG_KEY
HG_CHUNK = 64
N_BRANCH = 2
IN_WIDTH = 3 * ATT_WIDTH + 4 * HG_WIDTH + N_BRANCH * D_MODEL
CROSS_HEADS = 4
CROSS_WIDTH = CROSS_HEADS * HEAD_DIM
D_FF = int(math.ceil(8 * D_MODEL / 3 / 256) * 256)
RMS_EPS = 1e-6

kernel_name = "hybrid_dilated_attn_hgrn2_gated_block"


def rms_norm(x, w):
    xf = x.astype(jnp.float32)
    y = xf * lax.rsqrt(jnp.mean(xf * xf, axis=-1, keepdims=True) + RMS_EPS)
    return (y * w.astype(jnp.float32)).astype(x.dtype)


def banded_window_attention(q, k, v, window):
    assert window <= ATT_BLOCK
    lead = q.shape[:-2]
    L, dh = q.shape[-2], q.shape[-1]
    nb = -(-L // ATT_BLOCK)
    Lp = nb * ATT_BLOCK
    if Lp != L:
        pad = [(0, 0)] * len(lead) + [(0, Lp - L), (0, 0)]
        q, k, v = jnp.pad(q, pad), jnp.pad(k, pad), jnp.pad(v, pad)
    qb = q.reshape(*lead, nb, ATT_BLOCK, dh)
    kb = k.reshape(*lead, nb, ATT_BLOCK, dh)
    vb = v.reshape(*lead, nb, ATT_BLOCK, dh)
    kk = jnp.concatenate([jnp.concatenate([jnp.zeros_like(kb[..., :1, :, :]), kb[..., :-1, :, :]], axis=-3), kb], axis=-2)
    vv = jnp.concatenate([jnp.concatenate([jnp.zeros_like(vb[..., :1, :, :]), vb[..., :-1, :, :]], axis=-3), vb], axis=-2)
    blk = jnp.arange(nb)[:, None, None] * ATT_BLOCK
    qpos = blk + jnp.arange(ATT_BLOCK)[None, :, None]
    kpos = blk - ATT_BLOCK + jnp.arange(2 * ATT_BLOCK)[None, None, :]
    dist = qpos - kpos
    mask = (dist >= 0) & (dist <= window) & (kpos >= 0)
    s = jnp.einsum('...nqd,...nkd->...nqk', qb, kk).astype(jnp.float32) * (dh ** -0.5)
    s = jnp.where(mask, s, -jnp.inf)
    m = jnp.max(s, axis=-1, keepdims=True)
    p = jnp.exp(s - m)
    l = jnp.sum(p, axis=-1, keepdims=True)
    o = jnp.einsum('...nqk,...nkd->...nqd', (p / l).astype(v.dtype), vv)
    lse = (m + jnp.log(l))[..., 0]
    o = o.reshape(*lead, Lp, dh)[..., :L, :]
    lse = lse.reshape(*lead, Lp)[..., :L]
    return o, lse


def dilated_window_attention(q, k, v, window, dilation):
    B, H, S, dh = q.shape
    L = S // dilation

    def to_residue(t):
        return t.reshape(B, H, L, dilation, dh).transpose(0, 1, 3, 2, 4)

    o, lse = banded_window_attention(to_residue(q), to_residue(k), to_residue(v), window // dilation)
    o = o.transpose(0, 1, 3, 2, 4).reshape(B, H, S, dh)
    lse = lse.transpose(0, 1, 3, 2).reshape(B, H, S)
    return o, lse


def hgrn2_chunked(q, log_f, k, v):
    B, H, S, K = q.shape
    V = v.shape[-1]
    C = HG_CHUNK
    N = S // C
    q, log_f, k = (t.reshape(B, H, N, C, K) for t in (q, log_f, k))
    v = v.reshape(B, H, N, C, V)
    b = jnp.cumsum(log_f, axis=-2)
    b_last = b[..., -1:, :]
    chunk_kv = jnp.einsum('bhnck,bhncv->bhnkv', k * jnp.exp(b_last - b), v)
    decay = jnp.exp(b_last[..., 0, :])

    def step(state, inp):
        dec, kv = inp
        return dec[..., None] * state + kv, state

    _, states = lax.scan(step, jnp.zeros((B, H, K, V), jnp.float32),
                         (jnp.moveaxis(decay, 2, 0), jnp.moveaxis(chunk_kv, 2, 0)))
    states = jnp.moveaxis(states, 0, 2)
    inter = jnp.einsum('bhnck,bhnkv->bhncv', q * jnp.exp(b), states)
    b_ref = b[..., C // 2:C // 2 + 1, :]
    a = jnp.einsum('bhnck,bhnsk->bhncs', q * jnp.exp(b - b_ref), k * jnp.exp(b_ref - b))
    causal = jnp.arange(C)[:, None] >= jnp.arange(C)[None, :]
    a = jnp.where(causal, a, 0.0)
    intra = jnp.einsum('bhncs,bhnsv->bhncv', a, v)
    return (inter + intra).reshape(B, H, S, V)


def _fwd_setup_inputs(seed: int = 0) -> dict:
    key = jax.random.key(seed)
    ks = jax.random.split(key, 20)
    nrm = lambda k, shape, fan_in: jax.random.normal(k, shape, jnp.float32) * (fan_in ** -0.5)
    gain = lambda k, shape: 1.0 + 0.02 * jax.random.normal(k, shape, jnp.float32)
    return {
        "x": jax.random.normal(ks[0], (BATCH, SEQ, D_MODEL), jnp.float32),
        "mem": jax.random.normal(ks[1], (BATCH, MEM_LEN, D_MODEL), jnp.float32),
        "ln_mix_w": gain(ks[2], (DEPTH, D_MODEL)),
        "w_in": nrm(ks[3], (DEPTH, D_MODEL, IN_WIDTH), D_MODEL),
        "hg_norm_w": gain(ks[4], (DEPTH, HG_VAL)),
        "hg_lower_bounds": 0.1 * jax.random.normal(ks[5], (DEPTH + 1, HG_WIDTH), jnp.float32),
        "w_branch_a": nrm(ks[6], (DEPTH, ATT_OUT_WIDTH, D_MODEL), ATT_OUT_WIDTH),
        "w_branch_b": nrm(ks[7], (DEPTH, HG_WIDTH, D_MODEL), HG_WIDTH),
        "w_out": nrm(ks[8], (DEPTH, D_MODEL, D_MODEL), D_MODEL),
        "ln_cross_w": gain(ks[9], (DEPTH, D_MODEL)),
        "ln_mem_w": gain(ks[10], (DEPTH, D_MODEL)),
        "wq_cross": nrm(ks[11], (DEPTH, D_MODEL, CROSS_WIDTH), D_MODEL),
        "wkv_cross": nrm(ks[12], (DEPTH, D_MODEL, 2 * CROSS_WIDTH), D_MODEL),
        "wo_cross": nrm(ks[13], (DEPTH, CROSS_WIDTH, D_MODEL), CROSS_WIDTH),
        "ln_ffn_w": gain(ks[14], (DEPTH, D_MODEL)),
        "w1": nrm(ks[15], (DEPTH, D_MODEL, D_FF), D_MODEL),
        "w3": nrm(ks[16], (DEPTH, D_MODEL, D_FF), D_MODEL),
        "w2": nrm(ks[17], (DEPTH, D_FF, D_MODEL), D_FF),
        "ln_final_w": gain(ks[18], (D_MODEL,)),
    }


def _fwd_reference(x, mem, ln_mix_w, w_in, hg_norm_w, hg_lower_bounds, w_branch_a, w_branch_b, w_out,
              ln_cross_w, ln_mem_w, wq_cross, wkv_cross, wo_cross, ln_ffn_w, w1, w3, w2, ln_final_w):
    B, S, D = x.shape
    M = mem.shape[1]
    lower_bounds = jnp.cumsum(jax.nn.softmax(hg_lower_bounds.astype(jnp.float32), axis=0), axis=0)
    for l in range(DEPTH):
        h = rms_norm(x, ln_mix_w[l])
        proj = h @ w_in[l]
        q_a, k_a, v_a, q_h, f_h, i_h, g_h, gates = jnp.split(
            proj, np.cumsum([ATT_WIDTH] * 3 + [HG_WIDTH] * 4).tolist(), axis=-1)

        def att_heads(t):
            return t.reshape(B, S, N_ATT_GROUPS, ATT_HEADS_PER_GROUP, HEAD_DIM).transpose(0, 2, 3, 1, 4)
        qa, ka, va = att_heads(q_a), att_heads(k_a), att_heads(v_a)
        outs, lses = [], []
        for g, (window, dilation) in enumerate(ATT_GROUPS):
            o_g, lse_g = dilated_window_attention(qa[:, g], ka[:, g], va[:, g], window, dilation)
            outs.append(o_g)
            lses.append(lse_g)
        outs = jnp.stack(outs, axis=1)
        alpha = jax.nn.softmax(jnp.stack(lses, axis=1), axis=1)
        o_att = jnp.sum(alpha[..., None].astype(outs.dtype) * outs, axis=1)
        o_att = o_att.transpose(0, 2, 1, 3).reshape(B, S, ATT_OUT_WIDTH)

        def hg_heads(t):
            return t.reshape(B, S, HG_HEADS, HG_KEY).transpose(0, 2, 1, 3).astype(jnp.float32)
        lb = lower_bounds[l].reshape(HG_HEADS, HG_KEY)[None, :, None, :]
        f = lb + (1.0 - lb) * jax.nn.sigmoid(hg_heads(f_h))
        o_hg = hgrn2_chunked(jax.nn.silu(hg_heads(q_h)), jnp.log(f), 1.0 - f, hg_heads(i_h))
        o_hg = o_hg * lax.rsqrt(jnp.mean(o_hg * o_hg, axis=-1, keepdims=True) + RMS_EPS) * hg_norm_w[l].astype(jnp.float32)
        o_hg = (o_hg * jax.nn.silu(hg_heads(g_h))).transpose(0, 2, 1, 3).reshape(B, S, HG_WIDTH).astype(x.dtype)

        gate_a, gate_b = jnp.split(jax.nn.sigmoid(gates), N_BRANCH, axis=-1)
        merged = gate_a * (o_att @ w_branch_a[l]) + gate_b * (o_hg @ w_branch_b[l])
        x = x + merged @ w_out[l]

        hc = rms_norm(x, ln_cross_w[l])
        mn = rms_norm(mem, ln_mem_w[l])
        qc = (hc @ wq_cross[l]).reshape(B, S, CROSS_HEADS, HEAD_DIM)
        kvc = (mn @ wkv_cross[l]).reshape(B, M, 2, CROSS_HEADS, HEAD_DIM)
        sc = jnp.einsum('bshd,bmhd->bhsm', qc, kvc[:, :, 0]).astype(jnp.float32) * (HEAD_DIM ** -0.5)
        pc = jax.nn.softmax(sc, axis=-1).astype(x.dtype)
        oc = jnp.einsum('bhsm,bmhd->bshd', pc, kvc[:, :, 1]).reshape(B, S, CROSS_WIDTH)
        x = x + oc @ wo_cross[l]

        hf = rms_norm(x, ln_ffn_w[l])
        x = x + (jax.nn.silu(hf @ w1[l]) * (hf @ w3[l])) @ w2[l]
    return rms_norm(x, ln_final_w)


import jax as _jax
import jax.numpy as _jnp

TWIN_FORMAT = 'train_step'
FWD_PARAMS = ['x', 'mem', 'ln_mix_w', 'w_in', 'hg_norm_w', 'hg_lower_bounds', 'w_branch_a', 'w_branch_b', 'w_out', 'ln_cross_w', 'ln_mem_w', 'wq_cross', 'wkv_cross', 'wo_cross', 'ln_ffn_w', 'w1', 'w3', 'w2', 'ln_final_w']
TWIN_WEIGHTS = ['ln_mix_w', 'w_in', 'hg_norm_w', 'hg_lower_bounds', 'w_branch_a', 'w_branch_b', 'w_out', 'ln_cross_w', 'ln_mem_w', 'wq_cross', 'wkv_cross', 'wo_cross', 'ln_ffn_w', 'w1', 'w3', 'w2', 'ln_final_w']
TWIN_DIFF_INPUT = 'x'
TWIN_INPUTS = ['x', 'mem', 'ln_mix_w', 'w_in', 'hg_norm_w', 'hg_lower_bounds', 'w_branch_a', 'w_branch_b', 'w_out', 'ln_cross_w', 'ln_mem_w', 'wq_cross', 'wkv_cross', 'wo_cross', 'ln_ffn_w', 'w1', 'w3', 'w2', 'ln_final_w', 'loss_target', 'm_ln_mix_w', 'm_w_in', 'm_hg_norm_w', 'm_hg_lower_bounds', 'm_w_branch_a', 'm_w_branch_b', 'm_w_out', 'm_ln_cross_w', 'm_ln_mem_w', 'm_wq_cross', 'm_wkv_cross', 'm_wo_cross', 'm_ln_ffn_w', 'm_w1', 'm_w3', 'm_w2', 'm_ln_final_w', 'v_ln_mix_w', 'v_w_in', 'v_hg_norm_w', 'v_hg_lower_bounds', 'v_w_branch_a', 'v_w_branch_b', 'v_w_out', 'v_ln_cross_w', 'v_ln_mem_w', 'v_wq_cross', 'v_wkv_cross', 'v_wo_cross', 'v_ln_ffn_w', 'v_w1', 'v_w3', 'v_w2', 'v_ln_final_w']
TWIN_OUTPUTS = ['loss', 'grad_x', 'grad_ln_mix_w', 'grad_w_in', 'grad_hg_norm_w', 'grad_hg_lower_bounds', 'grad_w_branch_a', 'grad_w_branch_b', 'grad_w_out', 'grad_ln_cross_w', 'grad_ln_mem_w', 'grad_wq_cross', 'grad_wkv_cross', 'grad_wo_cross', 'grad_ln_ffn_w', 'grad_w1', 'grad_w3', 'grad_w2', 'grad_ln_final_w', 'delta_ln_mix_w', 'delta_w_in', 'delta_hg_norm_w', 'delta_hg_lower_bounds', 'delta_w_branch_a', 'delta_w_branch_b', 'delta_w_out', 'delta_ln_cross_w', 'delta_ln_mem_w', 'delta_wq_cross', 'delta_wkv_cross', 'delta_wo_cross', 'delta_ln_ffn_w', 'delta_w1', 'delta_w3', 'delta_w2', 'delta_ln_final_w', 'new_m_ln_mix_w', 'new_m_w_in', 'new_m_hg_norm_w', 'new_m_hg_lower_bounds', 'new_m_w_branch_a', 'new_m_w_branch_b', 'new_m_w_out', 'new_m_ln_cross_w', 'new_m_ln_mem_w', 'new_m_wq_cross', 'new_m_wkv_cross', 'new_m_wo_cross', 'new_m_ln_ffn_w', 'new_m_w1', 'new_m_w3', 'new_m_w2', 'new_m_ln_final_w', 'new_v_ln_mix_w', 'new_v_w_in', 'new_v_hg_norm_w', 'new_v_hg_lower_bounds', 'new_v_w_branch_a', 'new_v_w_branch_b', 'new_v_w_out', 'new_v_ln_cross_w', 'new_v_ln_mem_w', 'new_v_wq_cross', 'new_v_wkv_cross', 'new_v_wo_cross', 'new_v_ln_ffn_w', 'new_v_w1', 'new_v_w3', 'new_v_w2', 'new_v_ln_final_w']
TWIN_LEAF_KINDS = {'loss': 'loss', 'grad_x': 'grad_x', 'grad_ln_mix_w': 'grad_w', 'grad_w_in': 'grad_w', 'grad_hg_norm_w': 'grad_w', 'grad_hg_lower_bounds': 'grad_w', 'grad_w_branch_a': 'grad_w', 'grad_w_branch_b': 'grad_w', 'grad_w_out': 'grad_w', 'grad_ln_cross_w': 'grad_w', 'grad_ln_mem_w': 'grad_w', 'grad_wq_cross': 'grad_w', 'grad_wkv_cross': 'grad_w', 'grad_wo_cross': 'grad_w', 'grad_ln_ffn_w': 'grad_w', 'grad_w1': 'grad_w', 'grad_w3': 'grad_w', 'grad_w2': 'grad_w', 'grad_ln_final_w': 'grad_w', 'delta_ln_mix_w': 'delta_w', 'delta_w_in': 'delta_w', 'delta_hg_norm_w': 'delta_w', 'delta_hg_lower_bounds': 'delta_w', 'delta_w_branch_a': 'delta_w', 'delta_w_branch_b': 'delta_w', 'delta_w_out': 'delta_w', 'delta_ln_cross_w': 'delta_w', 'delta_ln_mem_w': 'delta_w', 'delta_wq_cross': 'delta_w', 'delta_wkv_cross': 'delta_w', 'delta_wo_cross': 'delta_w', 'delta_ln_ffn_w': 'delta_w', 'delta_w1': 'delta_w', 'delta_w3': 'delta_w', 'delta_w2': 'delta_w', 'delta_ln_final_w': 'delta_w', 'new_m_ln_mix_w': 'new_m', 'new_m_w_in': 'new_m', 'new_m_hg_norm_w': 'new_m', 'new_m_hg_lower_bounds': 'new_m', 'new_m_w_branch_a': 'new_m', 'new_m_w_branch_b': 'new_m', 'new_m_w_out': 'new_m', 'new_m_ln_cross_w': 'new_m', 'new_m_ln_mem_w': 'new_m', 'new_m_wq_cross': 'new_m', 'new_m_wkv_cross': 'new_m', 'new_m_wo_cross': 'new_m', 'new_m_ln_ffn_w': 'new_m', 'new_m_w1': 'new_m', 'new_m_w3': 'new_m', 'new_m_w2': 'new_m', 'new_m_ln_final_w': 'new_m', 'new_v_ln_mix_w': 'new_v', 'new_v_w_in': 'new_v', 'new_v_hg_norm_w': 'new_v', 'new_v_hg_lower_bounds': 'new_v', 'new_v_w_branch_a': 'new_v', 'new_v_w_branch_b': 'new_v', 'new_v_w_out': 'new_v', 'new_v_ln_cross_w': 'new_v', 'new_v_ln_mem_w': 'new_v', 'new_v_wq_cross': 'new_v', 'new_v_wkv_cross': 'new_v', 'new_v_wo_cross': 'new_v', 'new_v_ln_ffn_w': 'new_v', 'new_v_w1': 'new_v', 'new_v_w3': 'new_v', 'new_v_w2': 'new_v', 'new_v_ln_final_w': 'new_v'}


def _forward(args):
    return _fwd_reference(*[args[k] for k in FWD_PARAMS])


def _output_shape():
    out = _jax.eval_shape(lambda: _forward(_fwd_setup_inputs(0)))
    return out.shape, out.dtype

N_MICROBATCH = 1
ADAM_LR = 0.001
ADAM_B1 = 0.9
ADAM_B2 = 0.999
ADAM_EPS = 1e-08
ADAM_WD = 0.01
ADAM_STEP = 10
PER_EXAMPLE_BATCH_AXIS = {'x': 0, 'mem': 0, 'loss_target': 0}
SHARED_INPUTS = []
_WEIGHT_DTYPES = {'ln_mix_w': _jnp.float32, 'w_in': _jnp.float32, 'hg_norm_w': _jnp.float32, 'hg_lower_bounds': _jnp.float32, 'w_branch_a': _jnp.float32, 'w_branch_b': _jnp.float32, 'w_out': _jnp.float32, 'ln_cross_w': _jnp.float32, 'ln_mem_w': _jnp.float32, 'wq_cross': _jnp.float32, 'wkv_cross': _jnp.float32, 'wo_cross': _jnp.float32, 'ln_ffn_w': _jnp.float32, 'w1': _jnp.float32, 'w3': _jnp.float32, 'w2': _jnp.float32, 'ln_final_w': _jnp.float32}
MOMENT_SCALE = {'ln_mix_w': 3.355408e-02, 'w_in': 1.362111e-02, 'hg_norm_w': 9.366059e-02, 'hg_lower_bounds': 2.833352e-03, 'w_branch_a': 6.629352e-03, 'w_branch_b': 2.157073e-02, 'w_out': 2.244979e-02, 'ln_cross_w': 6.836962e-03, 'ln_mem_w': 9.957538e-03, 'wq_cross': 1.368496e-02, 'wkv_cross': 1.392973e-02, 'wo_cross': 6.897854e-03, 'ln_ffn_w': 4.368308e-02, 'w1': 1.915532e-02, 'w3': 1.856738e-02, 'w2': 3.079145e-02, 'ln_final_w': 8.003426e+00}


def _to_microbatches(a, axis):
    t = _jnp.moveaxis(a, axis, 0)
    t = t.reshape((N_MICROBATCH, t.shape[0] // N_MICROBATCH) + t.shape[1:])
    return _jnp.moveaxis(t, 1, axis + 1)


def setup_inputs(seed: int = 0) -> dict:
    inp = _fwd_setup_inputs(seed)
    key = _jax.random.fold_in(_jax.random.key(seed), 7919)
    shape, _ = _output_shape()
    out = dict(inp)
    out["loss_target"] = _jax.random.normal(_jax.random.fold_in(key, 0), shape, _jnp.float32)
    for i, name in enumerate(TWIN_WEIGHTS):
        w = inp[name].astype(_jnp.float32)
        if MOMENT_SCALE is None:
            s = _jnp.sqrt(_jnp.mean(_jnp.square(w)) + 1e-30)
        else:
            s = MOMENT_SCALE[name]
        km, kv = _jax.random.split(_jax.random.fold_in(key, i + 1))
        out[name] = w
        out["m_" + name] = s * _jax.random.normal(km, w.shape, _jnp.float32)
        out["v_" + name] = (s * s) * _jax.random.uniform(kv, w.shape, _jnp.float32, 0.5, 1.5)
    if N_MICROBATCH > 1:
        for name, axis in PER_EXAMPLE_BATCH_AXIS.items():
            out[name] = _to_microbatches(out[name], axis)
    return {'x': out['x'], 'mem': out['mem'], 'ln_mix_w': out['ln_mix_w'], 'w_in': out['w_in'], 'hg_norm_w': out['hg_norm_w'], 'hg_lower_bounds': out['hg_lower_bounds'], 'w_branch_a': out['w_branch_a'], 'w_branch_b': out['w_branch_b'], 'w_out': out['w_out'], 'ln_cross_w': out['ln_cross_w'], 'ln_mem_w': out['ln_mem_w'], 'wq_cross': out['wq_cross'], 'wkv_cross': out['wkv_cross'], 'wo_cross': out['wo_cross'], 'ln_ffn_w': out['ln_ffn_w'], 'w1': out['w1'], 'w3': out['w3'], 'w2': out['w2'], 'ln_final_w': out['ln_final_w'], 'loss_target': out['loss_target'], 'm_ln_mix_w': out['m_ln_mix_w'], 'm_w_in': out['m_w_in'], 'm_hg_norm_w': out['m_hg_norm_w'], 'm_hg_lower_bounds': out['m_hg_lower_bounds'], 'm_w_branch_a': out['m_w_branch_a'], 'm_w_branch_b': out['m_w_branch_b'], 'm_w_out': out['m_w_out'], 'm_ln_cross_w': out['m_ln_cross_w'], 'm_ln_mem_w': out['m_ln_mem_w'], 'm_wq_cross': out['m_wq_cross'], 'm_wkv_cross': out['m_wkv_cross'], 'm_wo_cross': out['m_wo_cross'], 'm_ln_ffn_w': out['m_ln_ffn_w'], 'm_w1': out['m_w1'], 'm_w3': out['m_w3'], 'm_w2': out['m_w2'], 'm_ln_final_w': out['m_ln_final_w'], 'v_ln_mix_w': out['v_ln_mix_w'], 'v_w_in': out['v_w_in'], 'v_hg_norm_w': out['v_hg_norm_w'], 'v_hg_lower_bounds': out['v_hg_lower_bounds'], 'v_w_branch_a': out['v_w_branch_a'], 'v_w_branch_b': out['v_w_branch_b'], 'v_w_out': out['v_w_out'], 'v_ln_cross_w': out['v_ln_cross_w'], 'v_ln_mem_w': out['v_ln_mem_w'], 'v_wq_cross': out['v_wq_cross'], 'v_wkv_cross': out['v_wkv_cross'], 'v_wo_cross': out['v_wo_cross'], 'v_ln_ffn_w': out['v_ln_ffn_w'], 'v_w1': out['v_w1'], 'v_w3': out['v_w3'], 'v_w2': out['v_w2'], 'v_ln_final_w': out['v_ln_final_w']}


def _loss(weights, diff, rest, loss_target):
    with _jax.named_scope("forward"):
        args = {**rest, TWIN_DIFF_INPUT: diff, **{k: w.astype(_WEIGHT_DTYPES[k]) for k, w in weights.items()}}
        y = _forward(args)
    with _jax.named_scope("loss_head"):
        err = _jnp.square(y.astype(_jnp.float32) - loss_target)
        return 0.5 * _jnp.sum(_jnp.mean(err, axis=-1)) if err.ndim else 0.5 * err


def _adamw(w, g, m, v):
    m = ADAM_B1 * m + (1.0 - ADAM_B1) * g
    v = ADAM_B2 * v + (1.0 - ADAM_B2) * _jnp.square(g)
    m_hat = m / (1.0 - ADAM_B1 ** ADAM_STEP)
    v_hat = v / (1.0 - ADAM_B2 ** ADAM_STEP)
    delta = -ADAM_LR * (m_hat / (_jnp.sqrt(v_hat) + ADAM_EPS) + ADAM_WD * w)
    return delta, m, v


def reference(x, mem, ln_mix_w, w_in, hg_norm_w, hg_lower_bounds, w_branch_a, w_branch_b, w_out, ln_cross_w, ln_mem_w, wq_cross, wkv_cross, wo_cross, ln_ffn_w, w1, w3, w2, ln_final_w, loss_target, m_ln_mix_w, m_w_in, m_hg_norm_w, m_hg_lower_bounds, m_w_branch_a, m_w_branch_b, m_w_out, m_ln_cross_w, m_ln_mem_w, m_wq_cross, m_wkv_cross, m_wo_cross, m_ln_ffn_w, m_w1, m_w3, m_w2, m_ln_final_w, v_ln_mix_w, v_w_in, v_hg_norm_w, v_hg_lower_bounds, v_w_branch_a, v_w_branch_b, v_w_out, v_ln_cross_w, v_ln_mem_w, v_wq_cross, v_wkv_cross, v_wo_cross, v_ln_ffn_w, v_w1, v_w3, v_w2, v_ln_final_w):
    given = dict(x=x, mem=mem, ln_mix_w=ln_mix_w, w_in=w_in, hg_norm_w=hg_norm_w, hg_lower_bounds=hg_lower_bounds, w_branch_a=w_branch_a, w_branch_b=w_branch_b, w_out=w_out, ln_cross_w=ln_cross_w, ln_mem_w=ln_mem_w, wq_cross=wq_cross, wkv_cross=wkv_cross, wo_cross=wo_cross, ln_ffn_w=ln_ffn_w, w1=w1, w3=w3, w2=w2, ln_final_w=ln_final_w, loss_target=loss_target, m_ln_mix_w=m_ln_mix_w, m_w_in=m_w_in, m_hg_norm_w=m_hg_norm_w, m_hg_lower_bounds=m_hg_lower_bounds, m_w_branch_a=m_w_branch_a, m_w_branch_b=m_w_branch_b, m_w_out=m_w_out, m_ln_cross_w=m_ln_cross_w, m_ln_mem_w=m_ln_mem_w, m_wq_cross=m_wq_cross, m_wkv_cross=m_wkv_cross, m_wo_cross=m_wo_cross, m_ln_ffn_w=m_ln_ffn_w, m_w1=m_w1, m_w3=m_w3, m_w2=m_w2, m_ln_final_w=m_ln_final_w, v_ln_mix_w=v_ln_mix_w, v_w_in=v_w_in, v_hg_norm_w=v_hg_norm_w, v_hg_lower_bounds=v_hg_lower_bounds, v_w_branch_a=v_w_branch_a, v_w_branch_b=v_w_branch_b, v_w_out=v_w_out, v_ln_cross_w=v_ln_cross_w, v_ln_mem_w=v_ln_mem_w, v_wq_cross=v_wq_cross, v_wkv_cross=v_wkv_cross, v_wo_cross=v_wo_cross, v_ln_ffn_w=v_ln_ffn_w, v_w1=v_w1, v_w3=v_w3, v_w2=v_w2, v_ln_final_w=v_ln_final_w)
    weights = {n: given[n] for n in TWIN_WEIGHTS}
    shared = {n: given[n] for n in SHARED_INPUTS}
    per_example = {n: given[n] for n in ['x', 'mem']}
    grad_fn = _jax.value_and_grad(_loss, argnums=(0, 1))

    def one_microbatch(ex, loss_target):
        ex = dict(ex)
        diff = ex.pop(TWIN_DIFF_INPUT)
        return grad_fn(weights, diff, {**shared, **ex}, loss_target)

    if N_MICROBATCH == 1:
        loss, (grad_w, grad_x) = one_microbatch(per_example, given["loss_target"])
    else:
        def body(carry, xs):
            loss_sum, grad_sum = carry
            l_k, (gw_k, gx_k) = one_microbatch(xs[0], xs[1])
            with _jax.named_scope("update"):
                return (loss_sum + l_k, _jax.tree.map(_jnp.add, grad_sum, gw_k)), gx_k

        init = (_jnp.zeros((), _jnp.float32), _jax.tree.map(_jnp.zeros_like, weights))
        (loss, grad_w), grad_x = _jax.lax.scan(body, init, (per_example, given["loss_target"]))
    with _jax.named_scope("update"):
        delta_w, new_m, new_v = {}, {}, {}
        for n in TWIN_WEIGHTS:
            delta_w[n], new_m[n], new_v[n] = _adamw(weights[n], grad_w[n], given["m_" + n], given["v_" + n])
    return (loss, grad_x, *[grad_w[n] for n in TWIN_WEIGHTS], *[delta_w[n] for n in TWIN_WEIGHTS],
            *[new_m[n] for n in TWIN_WEIGHTS], *[new_v[n] for n in TWIN_WEIGHTS])
```

```python
import functools
import math

import jax
import jax.numpy as jnp
from jax import lax
from jax.experimental import pallas as pl
from jax.experimental.pallas import tpu as pltpu

F32 = jnp.float32
BF16 = jnp.bfloat16
MESH = pl.DeviceIdType.MESH

D_MODEL = 2048
SEQ = 2048
HEAD_DIM = 128
MEM_LEN = 256
ATT_GROUPS = ((128, 1), (512, 4), (2048, 16))
ATT_HEADS = 4
ATT_WIDTH = 3 * ATT_HEADS * HEAD_DIM
ATT_OUT = ATT_HEADS * HEAD_DIM
ATT_BLOCK = 128
HG_HEADS = 8
HG_WIDTH = HG_HEADS * HEAD_DIM
HG_CHUNK = 64
IN_WIDTH = 3 * ATT_WIDTH + 4 * HG_WIDTH + 2 * D_MODEL
CROSS_HEADS = 4
CROSS_WIDTH = CROSS_HEADS * HEAD_DIM
D_FF = 5632
RMS_EPS = 1e-6
ADAM_LR = 0.001
ADAM_B1 = 0.9
ADAM_B2 = 0.999
ADAM_EPS = 1e-08
ADAM_WD = 0.01
ADAM_STEP = 10
N_CHIPS = 4
N_DEV = 8

VMEM_LIMIT_BYTES = 48 * 1024 * 1024
LANE = 128


def _cparams(sem=None):
    return pltpu.CompilerParams(dimension_semantics=sem, vmem_limit_bytes=VMEM_LIMIT_BYTES)


def _div(n, cap, mult):
    best = None
    for d in range(mult, min(n, cap) + 1, mult):
        if n % d == 0:
            best = d
    assert best is not None, (n, cap, mult)
    return best


def _sigmoid(x):
    return 1.0 / (1.0 + jnp.exp(-x))


def _dot(a, b):
    return jnp.dot(a.astype(BF16), b.astype(BF16), preferred_element_type=F32)


def _dot_nt(a, b):
    return lax.dot_general(a.astype(BF16), b.astype(BF16), (((1,), (1,)), ((), ())),
                           preferred_element_type=F32)


def _dot_tn(a, b):
    return jnp.dot(a.astype(F32).T.astype(BF16), b.astype(BF16), preferred_element_type=F32)


def _dot_exact(a, b):
    return jnp.dot(a, b, precision=lax.Precision.HIGHEST, preferred_element_type=F32)


def _mm(a, b, *, mode, out_dtype, name, res=None):
    if mode == "nn":
        (m, k), (k2, n) = a.shape, b.shape
    elif mode == "nt":
        (m, k), (n, k2) = a.shape, b.shape
    else:
        (k, m), (k2, n) = a.shape, b.shape
    assert k == k2, (name, a.shape, b.shape)
    tm = min(512, m)
    tn = min(512, n)
    assert m % tm == 0 and n % tn == 0
    out_shape = jax.ShapeDtypeStruct((m, n), out_dtype)

    if mode == "tn":
        assert res is None

        def kern_tn(a_ref, b_ref, o_ref, at_ref):
            @pl.when(pl.program_id(1) == 0)
            def _():
                at_ref[...] = a_ref[...].astype(F32).T.astype(BF16)

            o_ref[...] = jnp.dot(at_ref[...], b_ref[...].astype(BF16),
                                 preferred_element_type=F32).astype(o_ref.dtype)

        return pl.pallas_call(
            kern_tn, name=name, grid=(m // tm, n // tn),
            in_specs=[pl.BlockSpec((k, tm), lambda i, j: (0, i)),
                      pl.BlockSpec((k, tn), lambda i, j: (0, j))],
            out_specs=pl.BlockSpec((tm, tn), lambda i, j: (i, j)),
            out_shape=out_shape,
            scratch_shapes=[pltpu.VMEM((tm, k), BF16)],
            compiler_params=_cparams(("parallel", "arbitrary")),
        )(a, b)

    tk = k if k <= 2048 else _div(k, 3072, LANE)
    nk = k // tk
    a_spec = pl.BlockSpec((tm, tk), lambda i, j, kk: (i, kk))
    if mode == "nn":
        b_spec = pl.BlockSpec((tk, tn), lambda i, j, kk: (kk, j))
        dot = _dot
    else:
        b_spec = pl.BlockSpec((tn, tk), lambda i, j, kk: (j, kk))
        dot = _dot_nt
    o_spec = pl.BlockSpec((tm, tn), lambda i, j, kk: (i, j))
    in_specs = [a_spec, b_spec]
    args = [a, b]
    if res is not None:
        in_specs.append(o_spec)
        args.append(res)
    has_res = res is not None

    def kern(*refs):
        a_ref, b_ref = refs[0], refs[1]
        r_ref = refs[2] if has_res else None
        o_ref = refs[3] if has_res else refs[2]
        part = dot(a_ref[...], b_ref[...])
        if nk == 1:
            if has_res:
                part = part + r_ref[...]
            o_ref[...] = part.astype(o_ref.dtype)
            return
        acc_ref = refs[-1]
        kk = pl.program_id(2)

        @pl.when(kk == 0)
        def _():
            acc_ref[...] = part

        @pl.when(kk > 0)
        def _():
            acc_ref[...] += part

        @pl.when(kk == nk - 1)
        def _():
            tot = acc_ref[...]
            if has_res:
                tot = tot + r_ref[...]
            o_ref[...] = tot.astype(o_ref.dtype)

    return pl.pallas_call(
        kern, name=name, grid=(m // tm, n // tn, nk),
        in_specs=in_specs, out_specs=o_spec, out_shape=out_shape,
        scratch_shapes=[pltpu.VMEM((tm, tn), F32)] if nk > 1 else [],
        compiler_params=_cparams(("parallel", "parallel", "arbitrary")),
    )(*args)


ROW_BLOCK = 256


def _rms_fwd(x, g, name):
    t, d = x.shape
    tr = min(ROW_BLOCK, t)

    def kern(x_ref, g_ref, o_ref):
        xf = x_ref[...]
        r = lax.rsqrt(jnp.mean(xf * xf, axis=-1, keepdims=True) + RMS_EPS)
        o_ref[...] = (xf * r * g_ref[...]).astype(o_ref.dtype)

    return pl.pallas_call(
        kern, name=name, grid=(t // tr,),
        in_specs=[pl.BlockSpec((tr, d), lambda i: (i, 0)), pl.BlockSpec((1, d), lambda i: (0, 0))],
        out_specs=pl.BlockSpec((tr, d), lambda i: (i, 0)),
        out_shape=jax.ShapeDtypeStruct((t, d), BF16),
        compiler_params=_cparams(("parallel",)),
    )(x, g)


def _rms_bwd(x, g, dh, res, name):
    t, d = x.shape
    tr = min(ROW_BLOCK, t)
    has_res = res is not None

    def kern(*refs):
        x_ref, g_ref, dh_ref = refs[:3]
        r_ref = refs[3] if has_res else None
        dx_ref, dg_ref = refs[-2], refs[-1]
        xf = x_ref[...]
        r = lax.rsqrt(jnp.mean(xf * xf, axis=-1, keepdims=True) + RMS_EPS)
        xn = xf * r
        dh_ = dh_ref[...]
        dhg = dh_ * g_ref[...]
        dx = r * (dhg - xn * jnp.mean(dhg * xn, axis=-1, keepdims=True))
        if has_res:
            dx = dx + r_ref[...]
        dx_ref[...] = dx
        part = jnp.sum(dh_ * xn, axis=0, keepdims=True)

        @pl.when(pl.program_id(0) == 0)
        def _():
            dg_ref[...] = part

        @pl.when(pl.program_id(0) > 0)
        def _():
            dg_ref[...] += part

    row = pl.BlockSpec((tr, d), lambda i: (i, 0))
    vec = pl.BlockSpec((1, d), lambda i: (0, 0))
    in_specs = [row, vec, row] + ([row] if has_res else [])
    args = [x, g, dh] + ([res] if has_res else [])
    return pl.pallas_call(
        kern, name=name, grid=(t // tr,), in_specs=in_specs, out_specs=[row, vec],
        out_shape=[jax.ShapeDtypeStruct((t, d), F32), jax.ShapeDtypeStruct((1, d), F32)],
        compiler_params=_cparams(("arbitrary",)),
    )(*args)


def _loss_head(x3, g, target, name):
    t, d = x3.shape
    tr = ROW_BLOCK

    def kern(x_ref, g_ref, t_ref, dx_ref, dg_ref, loss_ref):
        xf = x_ref[...]
        r = lax.rsqrt(jnp.mean(xf * xf, axis=-1, keepdims=True) + RMS_EPS)
        xn = xf * r
        gg = g_ref[...]
        err = xn * gg - t_ref[...]
        lpart = 0.5 * jnp.sum(jnp.mean(err * err, axis=-1, keepdims=True), axis=0, keepdims=True)
        dy = err * (1.0 / d)
        dyg = dy * gg
        dx_ref[...] = r * (dyg - xn * jnp.mean(dyg * xn, axis=-1, keepdims=True))
        gpart = jnp.sum(dy * xn, axis=0, keepdims=True)
        lrow = jnp.broadcast_to(lpart, (1, LANE))

        @pl.when(pl.program_id(0) == 0)
        def _():
            dg_ref[...] = gpart
            loss_ref[...] = lrow

        @pl.when(pl.program_id(0) > 0)
        def _():
            dg_ref[...] += gpart
            loss_ref[...] += lrow

    row = pl.BlockSpec((tr, d), lambda i: (i, 0))
    vec = pl.BlockSpec((1, d), lambda i: (0, 0))
    return pl.pallas_call(
        kern, name=name, grid=(t // tr,), in_specs=[row, vec, row],
        out_specs=[row, vec, pl.BlockSpec((1, LANE), lambda i: (0, 0))],
        out_shape=[jax.ShapeDtypeStruct((t, d), F32), jax.ShapeDtypeStruct((1, d), F32),
                   jax.ShapeDtypeStruct((1, LANE), F32)],
        compiler_params=_cparams(("arbitrary",)),
    )(x3, g, target)


ATT_SCALE = HEAD_DIM ** -0.5
PROJ_BLOCKS = IN_WIDTH // HEAD_DIM
Q_BLOCK0, K_BLOCK0, V_BLOCK0 = 0, ATT_WIDTH // HEAD_DIM, 2 * ATT_WIDTH // HEAD_DIM


def _band_mask(n):
    iq = lax.broadcasted_iota(jnp.int32, (ATT_BLOCK, 2 * ATT_BLOCK), 0)
    ik = lax.broadcasted_iota(jnp.int32, (ATT_BLOCK, 2 * ATT_BLOCK), 1)
    last_prev = jnp.where(n == 0, -1, ATT_BLOCK - 1)
    prev = (ik <= last_prev) & (iq <= ik)
    own = (ik >= ATT_BLOCK) & ((ik - ATT_BLOCK) <= iq)
    return prev | own


def _attn_fwd(proj, grp, name):
    _, dil = ATT_GROUPS[grp]
    ln = SEQ // dil
    nb = ln // ATT_BLOCK
    pv = proj.reshape(ln, dil * IN_WIDTH)
    hb = grp * ATT_HEADS

    def blk(base, prev):
        def imap(h, r, n):
            row = jnp.maximum(n - 1, 0) if prev else n
            return (row, r * PROJ_BLOCKS + base + hb + h)
        return pl.BlockSpec((ATT_BLOCK, HEAD_DIM), imap)

    out_spec = pl.BlockSpec((ATT_BLOCK, HEAD_DIM), lambda h, r, n: (n, r * ATT_HEADS + h))

    def kern(q_ref, kp_ref, ko_ref, vp_ref, vo_ref, o_ref, lse_ref):
        n = pl.program_id(2)
        k2 = jnp.concatenate([kp_ref[...], ko_ref[...]], axis=0)
        v2 = jnp.concatenate([vp_ref[...], vo_ref[...]], axis=0)
        s = _dot_nt(q_ref[...], k2) * ATT_SCALE
        s = jnp.where(_band_mask(n), s, -jnp.inf)
        m = jnp.max(s, axis=-1, keepdims=True)
        p = jnp.exp(s - m)
        l = jnp.sum(p, axis=-1, keepdims=True)
        o_ref[...] = _dot(p / l, v2)
        lse_ref[...] = jnp.broadcast_to(m + jnp.log(l), (ATT_BLOCK, HEAD_DIM))

    o, lse = pl.pallas_call(
        kern, name=name, grid=(ATT_HEADS, dil, nb),
        in_specs=[blk(Q_BLOCK0, False), blk(K_BLOCK0, True), blk(K_BLOCK0, False),
                  blk(V_BLOCK0, True), blk(V_BLOCK0, False)],
        out_specs=[out_spec, out_spec],
        out_shape=[jax.ShapeDtypeStruct((ln, dil * ATT_OUT), F32)] * 2,
        compiler_params=_cparams(("parallel", "parallel", "arbitrary")),
    )(pv, pv, pv, pv, pv)
    return o.reshape(SEQ, ATT_OUT), lse.reshape(SEQ, ATT_OUT)


def _attn_weights(l0, l1, l2):
    mx = jnp.maximum(jnp.maximum(l0, l1), l2)
    e0, e1, e2 = jnp.exp(l0 - mx), jnp.exp(l1 - mx), jnp.exp(l2 - mx)
    den = e0 + e1 + e2
    return e0 / den, e1 / den, e2 / den


def _attn_merge_fwd(outs, lses, name):
    tr = ROW_BLOCK

    def kern(o0, o1, o2, l0, l1, l2, out_ref):
        a0, a1, a2 = _attn_weights(l0[...], l1[...], l2[...])
        out_ref[...] = (a0 * o0[...] + a1 * o1[...] + a2 * o2[...]).astype(out_ref.dtype)

    spec = pl.BlockSpec((tr, ATT_OUT), lambda i: (i, 0))
    return pl.pallas_call(
        kern, name=name, grid=(SEQ // tr,), in_specs=[spec] * 6, out_specs=spec,
        out_shape=jax.ShapeDtypeStruct((SEQ, ATT_OUT), BF16),
        compiler_params=_cparams(("parallel",)),
    )(*outs, *lses)


def _attn_merge_bwd(outs, lses, do_att, name):
    tr = ROW_BLOCK

    def kern(o0, o1, o2, l0, l1, l2, do_ref, d0, d1, d2, t0, t1, t2):
        alphas = _attn_weights(l0[...], l1[...], l2[...])
        do = do_ref[...]
        o_att = alphas[0] * o0[...] + alphas[1] * o1[...] + alphas[2] * o2[...]
        prod = do * o_att
        parts = []
        for h in range(ATT_HEADS):
            sl = slice(h * HEAD_DIM, (h + 1) * HEAD_DIM)
            tot = jnp.sum(prod[:, sl], axis=-1, keepdims=True)
            parts.append(jnp.broadcast_to(tot, (tr, HEAD_DIM)))
        dd = jnp.concatenate(parts, axis=1)
        for a, d_ref, t_ref in zip(alphas, (d0, d1, d2), (t0, t1, t2)):
            d_ref[...] = (a * do).astype(d_ref.dtype)
            t_ref[...] = -a * dd

    spec = pl.BlockSpec((tr, ATT_OUT), lambda i: (i, 0))
    res = pl.pallas_call(
        kern, name=name, grid=(SEQ // tr,), in_specs=[spec] * 7, out_specs=[spec] * 6,
        out_shape=[jax.ShapeDtypeStruct((SEQ, ATT_OUT), BF16)] * 3
        + [jax.ShapeDtypeStruct((SEQ, ATT_OUT), F32)] * 3,
        compiler_params=_cparams(("parallel",)),
    )(*outs, *lses, do_att)
    return res[:3], res[3:]


def _attn_bwd(proj, grp, lse, do_g, dl_g, name):
    _, dil = ATT_GROUPS[grp]
    ln = SEQ // dil
    nb = ln // ATT_BLOCK
    pv = proj.reshape(ln, dil * IN_WIDTH)
    hb = grp * ATT_HEADS
    view = lambda t: t.reshape(ln, dil * ATT_OUT)

    def pblk(base, shift):
        def imap(h, r, n):
            return (jnp.clip(n + shift, 0, nb - 1), r * PROJ_BLOCKS + base + hb + h)
        return pl.BlockSpec((ATT_BLOCK, HEAD_DIM), imap)

    def oblk(shift):
        def imap(h, r, n):
            return (jnp.clip(n + shift, 0, nb - 1), r * ATT_HEADS + h)
        return pl.BlockSpec((ATT_BLOCK, HEAD_DIM), imap)

    def kern(q_ref, qn_ref, kp_ref, ko_ref, vp_ref, vo_ref, do_ref, don_ref, lse_ref, lsen_ref,
             dl_ref, dln_ref, dq_ref, dk_ref, dv_ref):
        n = pl.program_id(2)
        q = q_ref[...]
        ko, vo = ko_ref[...], vo_ref[...]
        k2 = jnp.concatenate([kp_ref[...], ko], axis=0)
        v2 = jnp.concatenate([vp_ref[...], vo], axis=0)
        do = do_ref[...]
        s = _dot_nt(q, k2) * ATT_SCALE
        p = jnp.where(_band_mask(n), jnp.exp(s - lse_ref[:, :1]), 0.0)
        ds = p * (_dot_nt(do, v2) + dl_ref[:, :1])
        dq_ref[...] = (_dot(ds, k2) * ATT_SCALE).astype(dq_ref.dtype)
        dk = _dot_tn(ds[:, ATT_BLOCK:], q)
        dv = _dot_tn(p[:, ATT_BLOCK:], do)
        qn, don = qn_ref[...], don_ref[...]
        iq = lax.broadcasted_iota(jnp.int32, (ATT_BLOCK, ATT_BLOCK), 0)
        ik = lax.broadcasted_iota(jnp.int32, (ATT_BLOCK, ATT_BLOCK), 1)
        last_key = jnp.where(n + 1 < nb, ATT_BLOCK - 1, -1)
        valid = (iq <= ik) & (ik <= last_key)
        s2 = _dot_nt(qn, ko) * ATT_SCALE
        p2 = jnp.where(valid, jnp.exp(s2 - lsen_ref[:, :1]), 0.0)
        ds2 = p2 * (_dot_nt(don, vo) + dln_ref[:, :1])
        dk = dk + _dot_tn(ds2, qn)
        dv = dv + _dot_tn(p2, don)
        dk_ref[...] = (dk * ATT_SCALE).astype(dk_ref.dtype)
        dv_ref[...] = dv.astype(dv_ref.dtype)

    out_spec = oblk(0)
    dq, dk, dv = pl.pallas_call(
        kern, name=name, grid=(ATT_HEADS, dil, nb),
        in_specs=[pblk(Q_BLOCK0, 0), pblk(Q_BLOCK0, 1), pblk(K_BLOCK0, -1), pblk(K_BLOCK0, 0),
                  pblk(V_BLOCK0, -1), pblk(V_BLOCK0, 0), oblk(0), oblk(1), oblk(0), oblk(1),
                  oblk(0), oblk(1)],
        out_specs=[out_spec] * 3,
        out_shape=[jax.ShapeDtypeStruct((ln, dil * ATT_OUT), BF16)] * 3,
        compiler_params=_cparams(("parallel", "parallel", "arbitrary")),
    )(pv, pv, pv, pv, pv, pv, view(do_g), view(do_g), view(lse), view(lse), view(dl_g), view(dl_g))
    return dq.reshape(SEQ, ATT_OUT), dk.reshape(SEQ, ATT_OUT), dv.reshape(SEQ, ATT_OUT)


HG_HEADS_PER_STEP = 4
HG_BLOCK_W = HG_HEADS_PER_STEP * HEAD_DIM
HG_Q_BLK = (3 * ATT_WIDTH) // HG_BLOCK_W
HG_N_CHUNKS = SEQ // HG_CHUNK
HG_MID = HG_CHUNK // 2


def _lower_bound(lb_ref, sl):
    l0, l1 = lb_ref[0:1, sl], lb_ref[1:2, sl]
    mx = jnp.maximum(l0, l1)
    e0, e1 = jnp.exp(l0 - mx), jnp.exp(l1 - mx)
    return e0 / (e0 + e1)


def _tri(lower):
    i = lax.broadcasted_iota(jnp.int32, (HG_CHUNK, HG_CHUNK), 0)
    j = lax.broadcasted_iota(jnp.int32, (HG_CHUNK, HG_CHUNK), 1)
    return (i >= j) if lower else (i <= j)


def _hg_chunk_terms(qh, fh, lb):
    sig = _sigmoid(fh)
    f = lb + (1.0 - lb) * sig
    k = 1.0 - f
    b = _dot_exact(_tri(True).astype(F32), jnp.log(f))
    bl = b[HG_CHUNK - 1:HG_CHUNK, :]
    br = b[HG_MID:HG_MID + 1, :]
    sq = _sigmoid(qh)
    q = qh * sq
    return dict(sig=sig, f=f, k=k, b=b, bl=bl, br=br, sq=sq, q=q,
                e1=jnp.exp(bl - b), e2=jnp.exp(b), e3=jnp.exp(b - br), e4=jnp.exp(br - b))


def _hg_fwd(proj, lbw, normw, name):
    def in_blk(off):
        return pl.BlockSpec((HG_CHUNK, HG_BLOCK_W), lambda hp, n: (n, HG_Q_BLK + off + hp))

    def kern(q_ref, f_ref, i_ref, g_ref, lb_ref, nw_ref, oraw_ref, ohg_ref, st_ref, state):
        @pl.when(pl.program_id(1) == 0)
        def _():
            state[...] = jnp.zeros_like(state)

        causal = _tri(True)
        for hd in range(HG_HEADS_PER_STEP):
            sl = slice(hd * HEAD_DIM, (hd + 1) * HEAD_DIM)
            t = _hg_chunk_terms(q_ref[:, sl], f_ref[:, sl], _lower_bound(lb_ref, sl))
            v = i_ref[:, sl]
            st = state[hd]
            st_ref[0, hd] = st
            kd = t["k"] * t["e1"]
            inter = _dot_nt(t["q"] * t["e2"], st)
            a = jnp.where(causal, _dot_nt(t["q"] * t["e3"], t["k"] * t["e4"]), 0.0)
            o = inter + _dot(a, v)
            state[hd] = st * jnp.exp(t["bl"]) + _dot_tn(v, kd)
            oraw_ref[:, sl] = o
            r = lax.rsqrt(jnp.mean(o * o, axis=-1, keepdims=True) + RMS_EPS)
            gh = g_ref[:, sl]
            ohg_ref[:, sl] = (o * r * nw_ref[...] * (gh * _sigmoid(gh))).astype(ohg_ref.dtype)

    out_blk = pl.BlockSpec((HG_CHUNK, HG_BLOCK_W), lambda hp, n: (n, hp))
    return pl.pallas_call(
        kern, name=name, grid=(HG_HEADS // HG_HEADS_PER_STEP, HG_N_CHUNKS),
        in_specs=[in_blk(0), in_blk(2), in_blk(4), in_blk(6),
                  pl.BlockSpec((2, HG_BLOCK_W), lambda hp, n: (0, hp)),
                  pl.BlockSpec((1, HEAD_DIM), lambda hp, n: (0, 0))],
        out_specs=[out_blk, out_blk,
                   pl.BlockSpec((1, HG_HEADS_PER_STEP, HEAD_DIM, HEAD_DIM), lambda hp, n: (n, hp, 0, 0))],
        out_shape=[jax.ShapeDtypeStruct((SEQ, HG_WIDTH), F32), jax.ShapeDtypeStruct((SEQ, HG_WIDTH), BF16),
                   jax.ShapeDtypeStruct((HG_N_CHUNKS, HG_HEADS, HEAD_DIM, HEAD_DIM), F32)],
        scratch_shapes=[pltpu.VMEM((HG_HEADS_PER_STEP, HEAD_DIM, HEAD_DIM), F32)],
        compiler_params=_cparams(("parallel", "arbitrary")),
    )(proj, proj, proj, proj, lbw, normw)


def _hg_bwd(proj, lbw, normw, oraw, states, do_hg, name):
    last = HG_N_CHUNKS - 1

    def in_blk(off):
        return pl.BlockSpec((HG_CHUNK, HG_BLOCK_W), lambda hp, n: (last - n, HG_Q_BLK + off + hp))

    blk = pl.BlockSpec((HG_CHUNK, HG_BLOCK_W), lambda hp, n: (last - n, hp))

    def kern(q_ref, f_ref, i_ref, g_ref, lb_ref, nw_ref, oraw_ref, st_ref, do_ref,
             dq_ref, df_ref, di_ref, dg_ref, dlb_ref, dnw_ref, dstate):
        first = pl.program_id(1) == 0

        @pl.when(first)
        def _():
            dstate[...] = jnp.zeros_like(dstate)

        causal = _tri(True)
        rows = lax.broadcasted_iota(jnp.int32, (HG_CHUNK, HEAD_DIM), 0)
        nw = nw_ref[...]
        dnw_tot = jnp.zeros((1, HEAD_DIM), F32)
        dlb_parts = []
        for hd in range(HG_HEADS_PER_STEP):
            sl = slice(hd * HEAD_DIM, (hd + 1) * HEAD_DIM)
            qh, fh, v, gh = q_ref[:, sl], f_ref[:, sl], i_ref[:, sl], g_ref[:, sl]
            o, dout = oraw_ref[:, sl], do_ref[:, sl]
            sgg = _sigmoid(gh)
            r = lax.rsqrt(jnp.mean(o * o, axis=-1, keepdims=True) + RMS_EPS)
            xn = o * r
            dg_ref[:, sl] = (dout * xn * nw * (sgg * (1.0 + gh * (1.0 - sgg)))).astype(dg_ref.dtype)
            don = dout * (gh * sgg)
            dnw_tot = dnw_tot + jnp.sum(don * xn, axis=0, keepdims=True)
            tt = don * nw
            do = r * (tt - xn * jnp.mean(tt * xn, axis=-1, keepdims=True))
            lb = _lower_bound(lb_ref, sl)
            t = _hg_chunk_terms(qh, fh, lb)
            k, q = t["k"], t["q"]
            kd, qb, qr, kr = k * t["e1"], q * t["e2"], q * t["e3"], k * t["e4"]
            a = jnp.where(causal, _dot_nt(qr, kr), 0.0)
            st = st_ref[0, hd]
            dstn = dstate[hd]
            dqb = _dot(do, st)
            da = jnp.where(causal, _dot_nt(do, v), 0.0)
            dv = _dot_tn(a, do) + _dot_nt(kd, dstn)
            dqr = _dot(da, kr)
            dkr = _dot_tn(da, qr)
            dkd = _dot(v, dstn)
            decay = jnp.exp(t["bl"])
            ddecay = jnp.sum(dstn * st, axis=0, keepdims=True)
            dstate[hd] = dstn * decay + _dot_tn(do, qb)
            dq = dqb * t["e2"] + dqr * t["e3"]
            dk = dkd * t["e1"] + dkr * t["e4"]
            db = dqb * qb + dqr * qr - dkr * kr - dkd * kd
            dbl = jnp.sum(dkd * kd, axis=0, keepdims=True) + ddecay * decay
            dbr = jnp.sum(dkr * kr - dqr * qr, axis=0, keepdims=True)
            dlf = _dot_exact(_tri(False).astype(F32), db) + dbl + jnp.where(rows <= HG_MID, dbr, 0.0)
            df = dlf / t["f"] - dk
            sig, sq = t["sig"], t["sq"]
            df_ref[:, sl] = (df * (1.0 - lb) * sig * (1.0 - sig)).astype(df_ref.dtype)
            dlb_parts.append(jnp.sum(df * (1.0 - sig), axis=0, keepdims=True))
            dq_ref[:, sl] = (dq * (sq * (1.0 + qh * (1.0 - sq)))).astype(dq_ref.dtype)
            di_ref[:, sl] = dv.astype(di_ref.dtype)
        dlb_row = jnp.concatenate(dlb_parts, axis=1)
        dnw_blk = jnp.broadcast_to(dnw_tot, (8, HEAD_DIM))

        @pl.when(first)
        def _():
            dlb_ref[...] = dlb_row
            dnw_ref[...] = dnw_blk

        @pl.when(jnp.logical_not(first))
        def _():
            dlb_ref[...] += dlb_row
            dnw_ref[...] += dnw_blk

    n_hp = HG_HEADS // HG_HEADS_PER_STEP
    outs = pl.pallas_call(
        kern, name=name, grid=(n_hp, HG_N_CHUNKS),
        in_specs=[in_blk(0), in_blk(2), in_blk(4), in_blk(6),
                  pl.BlockSpec((2, HG_BLOCK_W), lambda hp, n: (0, hp)),
                  pl.BlockSpec((1, HEAD_DIM), lambda hp, n: (0, 0)),
                  blk,
                  pl.BlockSpec((1, HG_HEADS_PER_STEP, HEAD_DIM, HEAD_DIM), lambda hp, n: (last - n, hp, 0, 0)),
                  blk],
        out_specs=[blk, blk, blk, blk,
                   pl.BlockSpec((1, HG_BLOCK_W), lambda hp, n: (0, hp)),
                   pl.BlockSpec((8, HEAD_DIM), lambda hp, n: (hp, 0))],
        out_shape=[jax.ShapeDtypeStruct((SEQ, HG_WIDTH), BF16)] * 4
        + [jax.ShapeDtypeStruct((1, HG_WIDTH), F32), jax.ShapeDtypeStruct((8 * n_hp, HEAD_DIM), F32)],
        scratch_shapes=[pltpu.VMEM((HG_HEADS_PER_STEP, HEAD_DIM, HEAD_DIM), F32)],
        compiler_params=_cparams(("parallel", "arbitrary")),
    )(proj, proj, proj, proj, lbw, normw, oraw, states, do_hg)
    dqh, dfh, dih, dgh, dlb, dnw = outs
    return dqh, dfh, dih, dgh, dlb, (dnw[0:1], dnw[8:9])


GATE_BLOCK_W = 512
GATE_A_BLK = (3 * ATT_WIDTH + 4 * HG_WIDTH) // GATE_BLOCK_W
GATE_B_BLK = GATE_A_BLK + D_MODEL // GATE_BLOCK_W


def _gate_specs():
    tr = ROW_BLOCK
    blk = pl.BlockSpec((tr, GATE_BLOCK_W), lambda i, j: (i, j))
    ga = pl.BlockSpec((tr, GATE_BLOCK_W), lambda i, j: (i, GATE_A_BLK + j))
    gb = pl.BlockSpec((tr, GATE_BLOCK_W), lambda i, j: (i, GATE_B_BLK + j))
    return (SEQ // tr, D_MODEL // GATE_BLOCK_W), blk, ga, gb


def _gate_fwd(proj, ya, yb, name):
    grid, blk, ga, gb = _gate_specs()

    def kern(ga_ref, gb_ref, ya_ref, yb_ref, o_ref):
        o_ref[...] = (_sigmoid(ga_ref[...]) * ya_ref[...] + _sigmoid(gb_ref[...]) * yb_ref[...]).astype(o_ref.dtype)

    return pl.pallas_call(
        kern, name=name, grid=grid, in_specs=[ga, gb, blk, blk], out_specs=blk,
        out_shape=jax.ShapeDtypeStruct((SEQ, D_MODEL), BF16),
        compiler_params=_cparams(("parallel", "parallel")),
    )(proj, proj, ya, yb)


def _gate_bwd(proj, ya, yb, dmerged, name):
    grid, blk, ga, gb = _gate_specs()

    def kern(ga_ref, gb_ref, ya_ref, yb_ref, dm_ref, dya_ref, dyb_ref, dga_ref, dgb_ref):
        dm = dm_ref[...]
        sa, sb = _sigmoid(ga_ref[...]), _sigmoid(gb_ref[...])
        dya_ref[...] = (dm * sa).astype(dya_ref.dtype)
        dyb_ref[...] = (dm * sb).astype(dyb_ref.dtype)
        dga_ref[...] = (dm * ya_ref[...] * sa * (1.0 - sa)).astype(dga_ref.dtype)
        dgb_ref[...] = (dm * yb_ref[...] * sb * (1.0 - sb)).astype(dgb_ref.dtype)

    return pl.pallas_call(
        kern, name=name, grid=grid, in_specs=[ga, gb, blk, blk, blk], out_specs=[blk] * 4,
        out_shape=[jax.ShapeDtypeStruct((SEQ, D_MODEL), BF16)] * 4,
        compiler_params=_cparams(("parallel", "parallel")),
    )(proj, proj, ya, yb, dmerged)


FF_SHARD = D_FF // N_CHIPS


def _swiglu_fwd(ab, name):
    tr = ROW_BLOCK

    def kern(ab_ref, u_ref):
        a, b = ab_ref[:, :FF_SHARD], ab_ref[:, FF_SHARD:]
        u_ref[...] = (a * _sigmoid(a) * b).astype(u_ref.dtype)

    return pl.pallas_call(
        kern, name=name, grid=(SEQ // tr, N_CHIPS),
        in_specs=[pl.BlockSpec((tr, 2 * FF_SHARD), lambda i, j: (i, j))],
        out_specs=pl.BlockSpec((tr, FF_SHARD), lambda i, j: (i, j)),
        out_shape=jax.ShapeDtypeStruct((SEQ, D_FF), BF16),
        compiler_params=_cparams(("parallel", "parallel")),
    )(ab)


def _swiglu_bwd(ab, du, name):
    tr = ROW_BLOCK

    def kern(ab_ref, du_ref, dab_ref):
        a, b = ab_ref[:, :FF_SHARD], ab_ref[:, FF_SHARD:]
        du_ = du_ref[...]
        sg = _sigmoid(a)
        dab_ref[:, :FF_SHARD] = (du_ * b * (sg * (1.0 + a * (1.0 - sg)))).astype(dab_ref.dtype)
        dab_ref[:, FF_SHARD:] = (du_ * (a * sg)).astype(dab_ref.dtype)

    wide = pl.BlockSpec((tr, 2 * FF_SHARD), lambda i, j: (i, j))
    return pl.pallas_call(
        kern, name=name, grid=(SEQ // tr, N_CHIPS),
        in_specs=[wide, pl.BlockSpec((tr, FF_SHARD), lambda i, j: (i, j))],
        out_specs=wide, out_shape=jax.ShapeDtypeStruct((SEQ, 2 * D_FF), BF16),
        compiler_params=_cparams(("parallel", "parallel")),
    )(ab, du)


CROSS_ROWS = 512


def _cross_fwd(qc, kvc, name):
    def kern(q_ref, k_ref, v_ref, o_ref):
        s = _dot_nt(q_ref[...], k_ref[...]) * ATT_SCALE
        m = jnp.max(s, axis=-1, keepdims=True)
        e = jnp.exp(s - m)
        p = e / jnp.sum(e, axis=-1, keepdims=True)
        o_ref[...] = _dot(p, v_ref[...]).astype(o_ref.dtype)

    qblk = pl.BlockSpec((CROSS_ROWS, HEAD_DIM), lambda h, i: (i, h))
    return pl.pallas_call(
        kern, name=name, grid=(CROSS_HEADS, SEQ // CROSS_ROWS),
        in_specs=[qblk, pl.BlockSpec((MEM_LEN, HEAD_DIM), lambda h, i: (0, h)),
                  pl.BlockSpec((MEM_LEN, HEAD_DIM), lambda h, i: (0, CROSS_HEADS + h))],
        out_specs=qblk, out_shape=jax.ShapeDtypeStruct((SEQ, CROSS_WIDTH), BF16),
        compiler_params=_cparams(("parallel", "parallel")),
    )(qc, kvc, kvc)


def _cross_bwd(qc, kvc, doc, name):
    def kern(q_ref, k_ref, v_ref, do_ref, dq_ref, dk_ref, dv_ref):
        q, k, v, do = q_ref[...], k_ref[...], v_ref[...], do_ref[...]
        s = _dot_nt(q, k) * ATT_SCALE
        m = jnp.max(s, axis=-1, keepdims=True)
        e = jnp.exp(s - m)
        p = e / jnp.sum(e, axis=-1, keepdims=True)
        dp = _dot_nt(do, v)
        ds = p * (dp - jnp.sum(dp * p, axis=-1, keepdims=True))
        dq_ref[...] = (_dot(ds, k) * ATT_SCALE).astype(dq_ref.dtype)
        dk = _dot_tn(ds, q) * ATT_SCALE
        dv = _dot_tn(p, do)

        @pl.when(pl.program_id(1) == 0)
        def _():
            dk_ref[...] = dk
            dv_ref[...] = dv

        @pl.when(pl.program_id(1) > 0)
        def _():
            dk_ref[...] += dk
            dv_ref[...] += dv

    qblk = pl.BlockSpec((CROSS_ROWS, HEAD_DIM), lambda h, i: (i, h))
    kblk = pl.BlockSpec((MEM_LEN, HEAD_DIM), lambda h, i: (0, h))
    dq, dk, dv = pl.pallas_call(
        kern, name=name, grid=(CROSS_HEADS, SEQ // CROSS_ROWS),
        in_specs=[qblk, kblk, pl.BlockSpec((MEM_LEN, HEAD_DIM), lambda h, i: (0, CROSS_HEADS + h)), qblk],
        out_specs=[qblk, kblk, kblk],
        out_shape=[jax.ShapeDtypeStruct((SEQ, CROSS_WIDTH), BF16),
                   jax.ShapeDtypeStruct((MEM_LEN, CROSS_WIDTH), F32),
                   jax.ShapeDtypeStruct((MEM_LEN, CROSS_WIDTH), F32)],
        compiler_params=_cparams(("parallel", "arbitrary")),
    )(qc, kvc, kvc, doc)
    return dq, jnp.concatenate([dk, dv], axis=1)


def _local_step(x, mem, target, small, full):
    h = _rms_fwd(x, small["ln_mix_w"], "rms_mix")
    proj = _mm(h, full["w_in"], mode="nn", out_dtype=F32, name="mm_proj")
    att = [_attn_fwd(proj, g, f"attn_fwd_g{g}") for g in range(3)]
    outs, lses = [a[0] for a in att], [a[1] for a in att]
    o_att = _attn_merge_fwd(outs, lses, "attn_merge")
    oraw, o_hg, states = _hg_fwd(proj, small["hg_lower_bounds"], small["hg_norm_w"], "hgrn_fwd")
    ya = _mm(o_att, full["w_branch_a"], mode="nn", out_dtype=F32, name="mm_branch_a")
    yb = _mm(o_hg, full["w_branch_b"], mode="nn", out_dtype=F32, name="mm_branch_b")
    merged = _gate_fwd(proj, ya, yb, "gate_fwd")
    x1 = _mm(merged, full["w_out"], mode="nn", out_dtype=F32, name="mm_out", res=x)

    hc = _rms_fwd(x1, small["ln_cross_w"], "rms_cross")
    mn = _rms_fwd(mem, small["ln_mem_w"], "rms_mem")
    qc = _mm(hc, full["wq_cross"], mode="nn", out_dtype=F32, name="mm_q")
    kvc = _mm(mn, full["wkv_cross"], mode="nn", out_dtype=F32, name="mm_kv")
    oc = _cross_fwd(qc, kvc, "cross_fwd")
    x2 = _mm(oc, full["wo_cross"], mode="nn", out_dtype=F32, name="mm_o", res=x1)

    hf = _rms_fwd(x2, small["ln_ffn_w"], "rms_ffn")
    ab = _mm(hf, full["w13"], mode="nn", out_dtype=F32, name="mm_w13")
    u = _swiglu_fwd(ab, "swiglu_fwd")
    x3 = _mm(u, full["w2"], mode="nn", out_dtype=F32, name="mm_w2", res=x2)

    dx3, dg_final, loss = _loss_head(x3, small["ln_final_w"], target, "loss_head")

    gw, gs = {}, {"ln_final_w": dg_final}
    du = _mm(dx3, full["w2"], mode="nt", out_dtype=F32, name="mm_du")
    gw["w2"] = _mm(u, dx3, mode="tn", out_dtype=BF16, name="mm_dw2")
    dab = _swiglu_bwd(ab, du, "swiglu_bwd")
    gw["w13"] = _mm(hf, dab, mode="tn", out_dtype=BF16, name="mm_dw13")
    dhf = _mm(dab, full["w13"], mode="nt", out_dtype=F32, name="mm_dhf")
    dx2, gs["ln_ffn_w"] = _rms_bwd(x2, small["ln_ffn_w"], dhf, dx3, "rms_ffn_bwd")
    doc = _mm(dx2, full["wo_cross"], mode="nt", out_dtype=BF16, name="mm_doc")
    gw["wo_cross"] = _mm(oc, dx2, mode="tn", out_dtype=BF16, name="mm_dwo")
    dqc, dkvc = _cross_bwd(qc, kvc, doc, "cross_bwd")
    gw["wq_cross"] = _mm(hc, dqc, mode="tn", out_dtype=BF16, name="mm_dwq")
    dhc = _mm(dqc, full["wq_cross"], mode="nt", out_dtype=F32, name="mm_dhc")
    gw["wkv_cross"] = _mm(mn, dkvc, mode="tn", out_dtype=BF16, name="mm_dwkv")
    dmn = _mm(dkvc, full["wkv_cross"], mode="nt", out_dtype=F32, name="mm_dmn")
    _, gs["ln_mem_w"] = _rms_bwd(mem, small["ln_mem_w"], dmn, None, "rms_mem_bwd")
    dx1, gs["ln_cross_w"] = _rms_bwd(x1, small["ln_cross_w"], dhc, dx2, "rms_cross_bwd")
    dmerged = _mm(dx1, full["w_out"], mode="nt", out_dtype=F32, name="mm_dmerged")
    gw["w_out"] = _mm(merged, dx1, mode="tn", out_dtype=BF16, name="mm_dwout")
    dya, dyb, dga, dgb = _gate_bwd(proj, ya, yb, dmerged, "gate_bwd")
    gw["w_branch_a"] = _mm(o_att, dya, mode="tn", out_dtype=BF16, name="mm_dwa")
    do_att = _mm(dya, full["w_branch_a"], mode="nt", out_dtype=F32, name="mm_doatt")
    gw["w_branch_b"] = _mm(o_hg, dyb, mode="tn", out_dtype=BF16, name="mm_dwb")
    do_hg = _mm(dyb, full["w_branch_b"], mode="nt", out_dtype=F32, name="mm_dohg")
    dqh, dfh, dih, dgh, dlb, gs["hg_norm_w"] = _hg_bwd(
        proj, small["hg_lower_bounds"], small["hg_norm_w"], oraw, states, do_hg, "hgrn_bwd")
    gs["hg_lb"] = dlb
    do_gs, dl_gs = _attn_merge_bwd(outs, lses, do_att, "attn_merge_bwd")
    dqs, dks, dvs = zip(*[_attn_bwd(proj, g, lses[g], do_gs[g], dl_gs[g], f"attn_bwd_g{g}") for g in range(3)])
    dproj = jnp.concatenate([*dqs, *dks, *dvs, dqh, dfh, dih, dgh, dga, dgb], axis=1)
    gw["w_in"] = _mm(h, dproj, mode="tn", out_dtype=BF16, name="mm_dwin")
    dh = _mm(dproj, full["w_in"], mode="nt", out_dtype=F32, name="mm_dh")
    dx, gs["ln_mix_w"] = _rms_bwd(x, small["ln_mix_w"], dh, dx1, "rms_mix_bwd")
    return loss, dx, gw, gs


FULL_SPECS = {
    "w_in": ("col", D_MODEL, IN_WIDTH),
    "w_branch_a": ("col", ATT_OUT, D_MODEL),
    "w_branch_b": ("col", HG_WIDTH, D_MODEL),
    "w_out": ("row", D_MODEL, D_MODEL),
    "wq_cross": ("row", D_MODEL, CROSS_WIDTH),
    "wkv_cross": ("row", D_MODEL, 2 * CROSS_WIDTH),
    "wo_cross": ("col", CROSS_WIDTH, D_MODEL),
    "w13": ("col", D_MODEL, 2 * D_FF),
    "w2": ("row", D_FF, D_MODEL),
}
FULL_ORDER = tuple(FULL_SPECS)
WEIGHT_PLACE = {
    "w_in": ("w_in", 0), "w_branch_a": ("w_branch_a", 0), "w_branch_b": ("w_branch_b", 0),
    "w_out": ("w_out", 0), "wq_cross": ("wq_cross", 0), "wkv_cross": ("wkv_cross", 0),
    "wo_cross": ("wo_cross", 0), "w1": ("w13", 0), "w3": ("w13", FF_SHARD), "w2": ("w2", 0),
}
BIG_WEIGHTS = tuple(WEIGHT_PLACE)
ANY = pl.BlockSpec(memory_space=pl.ANY)


def _position():
    return lax.axis_index("x"), lax.axis_index("y"), lax.axis_index("c")


def _other_chips(x, y):
    return [(1 - x, y), (x, 1 - y), (1 - x, 1 - y)]


def _half(ref, kind, h, lead=None):
    r, c = ref.shape[-2], ref.shape[-1]
    if kind == "col":
        idx = (pl.ds(h * (r // 2), r // 2), slice(None))
    else:
        idx = (slice(None), pl.ds(h * (c // 2), c // 2))
    return ref.at[idx] if lead is None else ref.at[(lead,) + idx]


def _shard_of(ref, kind, start, size):
    return ref.at[:, pl.ds(start, size)] if kind == "col" else ref.at[pl.ds(start, size), :]


def _slice_width(name):
    kind, rows, cols = FULL_SPECS[name]
    return (cols if kind == "col" else rows) // N_CHIPS


def _gather_weights(shards, specs=FULL_SPECS, place=WEIGHT_PLACE, name="gather_weights"):
    wnames = list(place)
    fnames = list(specs)
    n_w, n_f = len(wnames), len(fnames)

    def body(*refs):
        src = dict(zip(wnames, refs[:n_w]))
        dst = dict(zip(fnames, refs[n_w:n_w + n_f]))
        ici_send, ici_recv, d2d_send, d2d_recv, loc_sem = refs[n_w + n_f:]
        x, y, c = _position()
        j = 2 * x + y
        chips = _other_chips(x, y)

        def landing(w, chip, h):
            fname, off = place[w]
            kind = specs[fname][0]
            stride = (specs[fname][2] if kind == "col" else specs[fname][1]) // N_CHIPS
            s = src[w].shape
            size = s[1] if kind == "col" else s[0]
            return _half(_shard_of(dst[fname], kind, chip * stride + off, size), kind, h)

        def mine(w, h):
            return _half(src[w], specs[place[w][0]][0], h)

        local = [pltpu.make_async_copy(mine(w, c), landing(w, j, c), loc_sem.at[i])
                 for i, w in enumerate(wnames)]
        for cp in local:
            cp.start()
        sends = []
        for i, w in enumerate(wnames):
            for p, (px, py) in enumerate(chips):
                cp = pltpu.make_async_remote_copy(
                    src_ref=mine(w, c), dst_ref=landing(w, j, c),
                    send_sem=ici_send.at[3 * i + p], recv_sem=ici_recv.at[3 * i + p],
                    device_id=(px, py, c), device_id_type=MESH)
                cp.start()
                sends.append(cp)
        for cp in local:
            cp.wait()
        for f, fname in enumerate(fnames):
            kind = specs[fname][0]
            for i, w in enumerate(wnames):
                if place[w][0] != fname:
                    continue
                for p, (px, py) in enumerate(chips):
                    pltpu.make_async_remote_copy(
                        src_ref=mine(w, c), dst_ref=landing(w, 2 * px + py, c),
                        send_sem=ici_send.at[3 * i + p], recv_sem=ici_recv.at[3 * i + p],
                        device_id=(px, py, c), device_id_type=MESH).wait_recv()
            cp = pltpu.make_async_remote_copy(
                src_ref=_half(dst[fname], kind, c), dst_ref=_half(dst[fname], kind, c),
                send_sem=d2d_send.at[f], recv_sem=d2d_recv.at[f],
                device_id=(x, y, 1 - c), device_id_type=MESH)
            cp.start()
            sends.append(cp)
        for f, fname in enumerate(fnames):
            kind = specs[fname][0]
            pltpu.make_async_remote_copy(
                src_ref=_half(dst[fname], kind, 1 - c), dst_ref=_half(dst[fname], kind, 1 - c),
                send_sem=d2d_send.at[f], recv_sem=d2d_recv.at[f],
                device_id=(x, y, 1 - c), device_id_type=MESH).wait_recv()
        for cp in sends:
            cp.wait_send()

    outs = pl.pallas_call(
        body, name=name,
        in_specs=[ANY] * n_w, out_specs=[ANY] * n_f,
        out_shape=[jax.ShapeDtypeStruct(specs[f][1:], BF16) for f in fnames],
        scratch_shapes=[pltpu.SemaphoreType.DMA((3 * n_w,)), pltpu.SemaphoreType.DMA((3 * n_w,)),
                        pltpu.SemaphoreType.DMA((n_f,)), pltpu.SemaphoreType.DMA((n_f,)),
                        pltpu.SemaphoreType.DMA((n_w,))],
    )(*[shards[w] for w in wnames])
    return dict(zip(fnames, outs))


def _half_shape(kind, rows, cols):
    return (rows // 2, cols) if kind == "col" else (rows, cols // 2)


def _pair_exchange(grads, specs=FULL_SPECS, name="rs_pair_exchange"):
    fnames = list(specs)
    n_f = len(fnames)

    def body(*refs):
        g = dict(zip(fnames, refs[:n_f]))
        out = dict(zip(fnames, refs[n_f:2 * n_f]))
        send_sem, recv_sem, loc_sem = refs[2 * n_f:]
        x, y, c = _position()
        local, sends = [], []
        for f, fname in enumerate(fnames):
            kind = specs[fname][0]
            cp = pltpu.make_async_copy(_half(g[fname], kind, c), out[fname].at[c], loc_sem.at[f])
            cp.start()
            local.append(cp)
            cp = pltpu.make_async_remote_copy(
                src_ref=_half(g[fname], kind, 1 - c), dst_ref=out[fname].at[c],
                send_sem=send_sem.at[f], recv_sem=recv_sem.at[f],
                device_id=(x, y, 1 - c), device_id_type=MESH)
            cp.start()
            sends.append(cp)
        for f, fname in enumerate(fnames):
            kind = specs[fname][0]
            pltpu.make_async_remote_copy(
                src_ref=_half(g[fname], kind, c), dst_ref=out[fname].at[1 - c],
                send_sem=send_sem.at[f], recv_sem=recv_sem.at[f],
                device_id=(x, y, 1 - c), device_id_type=MESH).wait_recv()
        for cp in sends:
            cp.wait_send()
        for cp in local:
            cp.wait()

    outs = pl.pallas_call(
        body, name=name, in_specs=[ANY] * n_f, out_specs=[ANY] * n_f,
        out_shape=[jax.ShapeDtypeStruct((2,) + _half_shape(*specs[f]), BF16) for f in fnames],
        scratch_shapes=[pltpu.SemaphoreType.DMA((n_f,)), pltpu.SemaphoreType.DMA((n_f,)),
                        pltpu.SemaphoreType.DMA((n_f,))],
    )(*[grads[f] for f in fnames])
    return dict(zip(fnames, outs))


def _chip_exchange(pair_sums, specs=FULL_SPECS, name="rs_chip_exchange"):
    fnames = list(specs)
    n_f = len(fnames)

    def slot_shape(fname):
        kind, rows, cols = specs[fname]
        hr, hc = _half_shape(kind, rows, cols)
        return (hr, hc // N_CHIPS) if kind == "col" else (hr // N_CHIPS, hc)

    def body(*refs):
        ps = dict(zip(fnames, refs[:n_f]))
        out = dict(zip(fnames, refs[n_f:2 * n_f]))
        send_sem, recv_sem, loc_sem = refs[2 * n_f:]
        x, y, c = _position()
        j = 2 * x + y
        chips = _other_chips(x, y)

        def part(fname, chip):
            kind = specs[fname][0]
            width = slot_shape(fname)[1 if kind == "col" else 0]
            return _shard_of(ps[fname], kind, chip * width, width)

        local, sends = [], []
        for f, fname in enumerate(fnames):
            cp = pltpu.make_async_copy(part(fname, j), out[fname].at[j], loc_sem.at[f])
            cp.start()
            local.append(cp)
            for p, (px, py) in enumerate(chips):
                cp = pltpu.make_async_remote_copy(
                    src_ref=part(fname, 2 * px + py), dst_ref=out[fname].at[j],
                    send_sem=send_sem.at[3 * f + p], recv_sem=recv_sem.at[3 * f + p],
                    device_id=(px, py, c), device_id_type=MESH)
                cp.start()
                sends.append(cp)
        for f, fname in enumerate(fnames):
            for p, (px, py) in enumerate(chips):
                pltpu.make_async_remote_copy(
                    src_ref=part(fname, j), dst_ref=out[fname].at[2 * px + py],
                    send_sem=send_sem.at[3 * f + p], recv_sem=recv_sem.at[3 * f + p],
                    device_id=(px, py, c), device_id_type=MESH).wait_recv()
        for cp in sends:
            cp.wait_send()
        for cp in local:
            cp.wait()

    outs = pl.pallas_call(
        body, name=name, in_specs=[ANY] * n_f, out_specs=[ANY] * n_f,
        out_shape=[jax.ShapeDtypeStruct((N_CHIPS,) + slot_shape(f), BF16) for f in fnames],
        scratch_shapes=[pltpu.SemaphoreType.DMA((3 * n_f,)), pltpu.SemaphoreType.DMA((3 * n_f,)),
                        pltpu.SemaphoreType.DMA((n_f,))],
    )(*[pair_sums[f] for f in fnames])
    return dict(zip(fnames, outs))


def _sibling_share(half_grads, shard_shapes, specs=FULL_SPECS, place=WEIGHT_PLACE, name="rs_sibling_share"):
    fnames = list(specs)
    wnames = list(place)
    n_f, n_w = len(fnames), len(wnames)

    def body(*refs):
        hg = dict(zip(fnames, refs[:n_f]))
        out = dict(zip(wnames, refs[n_f:n_f + n_w]))
        send_sem, recv_sem, loc_sem = refs[n_f + n_w:]
        x, y, c = _position()
        local, sends = [], []
        for i, w in enumerate(wnames):
            fname, off = place[w]
            kind = specs[fname][0]
            shp = shard_shapes[w]
            size = shp[1] if kind == "col" else shp[0]
            piece = _shard_of(hg[fname], kind, off, size)
            cp = pltpu.make_async_copy(piece, _half(out[w], kind, c), loc_sem.at[i])
            cp.start()
            local.append(cp)
            cp = pltpu.make_async_remote_copy(
                src_ref=piece, dst_ref=_half(out[w], kind, c),
                send_sem=send_sem.at[i], recv_sem=recv_sem.at[i],
                device_id=(x, y, 1 - c), device_id_type=MESH)
            cp.start()
            sends.append(cp)
        for i, w in enumerate(wnames):
            fname, off = place[w]
            kind = specs[fname][0]
            shp = shard_shapes[w]
            size = shp[1] if kind == "col" else shp[0]
            pltpu.make_async_remote_copy(
                src_ref=_shard_of(hg[fname], kind, off, size), dst_ref=_half(out[w], kind, 1 - c),
                send_sem=send_sem.at[i], recv_sem=recv_sem.at[i],
                device_id=(x, y, 1 - c), device_id_type=MESH).wait_recv()
        for cp in sends:
            cp.wait_send()
        for cp in local:
            cp.wait()

    outs = pl.pallas_call(
        body, name=name, in_specs=[ANY] * n_f, out_specs=[ANY] * n_w,
        out_shape=[jax.ShapeDtypeStruct(shard_shapes[w], F32) for w in wnames],
        scratch_shapes=[pltpu.SemaphoreType.DMA((n_w,)), pltpu.SemaphoreType.DMA((n_w,)),
                        pltpu.SemaphoreType.DMA((n_w,))],
    )(*[half_grads[f] for f in fnames])
    return dict(zip(wnames, outs))


def _gather_rows(v, name="gather_small"):
    shape = v.shape

    def body(v_ref, out_ref, send_sem, recv_sem, loc_sem):
        x, y, c = _position()
        me = 4 * x + 2 * y + c
        flips = [(fx, fy, fc) for fx in (0, 1) for fy in (0, 1) for fc in (0, 1)][1:]

        def peer(fl):
            return tuple(1 - a if f else a for a, f in zip((x, y, c), fl))

        loc = pltpu.make_async_copy(v_ref, out_ref.at[me], loc_sem)
        loc.start()
        sends = []
        for i, fl in enumerate(flips):
            cp = pltpu.make_async_remote_copy(
                src_ref=v_ref, dst_ref=out_ref.at[me], send_sem=send_sem.at[i], recv_sem=recv_sem.at[i],
                device_id=peer(fl), device_id_type=MESH)
            cp.start()
            sends.append(cp)
        for i, fl in enumerate(flips):
            px, py, pc = peer(fl)
            pltpu.make_async_remote_copy(
                src_ref=v_ref, dst_ref=out_ref.at[4 * px + 2 * py + pc],
                send_sem=send_sem.at[i], recv_sem=recv_sem.at[i],
                device_id=peer(fl), device_id_type=MESH).wait_recv()
        for cp in sends:
            cp.wait_send()
        loc.wait()

    return pl.pallas_call(
        body, name=name, in_specs=[ANY], out_specs=ANY,
        out_shape=jax.ShapeDtypeStruct((N_DEV,) + shape, F32),
        scratch_shapes=[pltpu.SemaphoreType.DMA((N_DEV - 1,)), pltpu.SemaphoreType.DMA((N_DEV - 1,)),
                        pltpu.SemaphoreType.DMA],
    )(v)


EW_BLOCK_ELEMS = 256 * 1024


def _ew_block(rows, cols):
    tc = cols if cols <= 4096 else _div(cols, 2048, LANE)
    tr = _div(rows, max(16, EW_BLOCK_ELEMS // tc), 16)
    return tr, tc


def _sum_slots(buf, out_dtype, name):
    n, rows, cols = buf.shape
    tr, tc = _ew_block(rows, cols)

    def kern(b_ref, o_ref):
        tot = b_ref[0].astype(F32)
        for s in range(1, n):
            tot = tot + b_ref[s].astype(F32)
        o_ref[...] = tot.astype(o_ref.dtype)

    return pl.pallas_call(
        kern, name=name, grid=(rows // tr, cols // tc),
        in_specs=[pl.BlockSpec((n, tr, tc), lambda i, j: (0, i, j))],
        out_specs=pl.BlockSpec((tr, tc), lambda i, j: (i, j)),
        out_shape=jax.ShapeDtypeStruct((rows, cols), out_dtype),
        compiler_params=_cparams(("parallel", "parallel")),
    )(buf)


def _adam_math(w, g, m, v):
    m2 = ADAM_B1 * m + (1.0 - ADAM_B1) * g
    v2 = ADAM_B2 * v + (1.0 - ADAM_B2) * (g * g)
    m_hat = m2 / (1.0 - ADAM_B1 ** ADAM_STEP)
    v_hat = v2 / (1.0 - ADAM_B2 ** ADAM_STEP)
    delta = -ADAM_LR * (m_hat / (jnp.sqrt(v_hat) + ADAM_EPS) + ADAM_WD * w)
    return delta, m2, v2


def _adamw(w, g, m, v, name):
    rows, cols = w.shape
    tr, tc = _ew_block(rows, cols)

    def kern(w_ref, g_ref, m_ref, v_ref, d_ref, m2_ref, v2_ref):
        d_ref[...], m2_ref[...], v2_ref[...] = _adam_math(w_ref[...], g_ref[...], m_ref[...], v_ref[...])

    blk = pl.BlockSpec((tr, tc), lambda i, j: (i, j))
    return pl.pallas_call(
        kern, name=name, grid=(rows // tr, cols // tc), in_specs=[blk] * 4, out_specs=[blk] * 3,
        out_shape=[jax.ShapeDtypeStruct((rows, cols), F32)] * 3,
        compiler_params=_cparams(("parallel", "parallel")),
    )(w, g, m, v)


SMALL_ROWS = ("ln_mix_w", "ln_cross_w", "ln_mem_w", "ln_ffn_w", "ln_final_w")
ROW_HG_NORM, ROW_LB0, ROW_LB1 = 5, 6, 7
LOSS_LANE0 = HEAD_DIM


def _pack_small(vals):
    rows = [vals[n].reshape(1, D_MODEL) for n in SMALL_ROWS]
    pad = lambda a: jnp.pad(a, ((0, 0), (0, D_MODEL - a.shape[1])))
    rows.append(pad(vals["hg_norm_w"].reshape(1, HEAD_DIM)))
    rows.append(pad(vals["hg_lower_bounds"].reshape(2, HG_WIDTH)))
    return jnp.concatenate(rows, axis=0)


def _small_update(gathered, w, m, v, name="small_update"):
    def kern(g_ref, w_ref, m_ref, v_ref, grad_ref, d_ref, m2_ref, v2_ref, loss_ref):
        tot = g_ref[0]
        for s in range(1, N_DEV):
            tot = tot + g_ref[s]
        wv = w_ref[...]
        row = lax.broadcasted_iota(jnp.int32, (8, D_MODEL), 0)
        lane = lax.broadcasted_iota(jnp.int32, (8, D_MODEL), 1)
        l0, l1 = wv[ROW_LB0:ROW_LB0 + 1], wv[ROW_LB1:ROW_LB1 + 1]
        mx = jnp.maximum(l0, l1)
        e0, e1 = jnp.exp(l0 - mx), jnp.exp(l1 - mx)
        p0 = e0 / (e0 + e1)
        dlog = tot[ROW_LB0:ROW_LB0 + 1] * p0 * (1.0 - p0)
        tot = jnp.where(row == ROW_HG_NORM, tot + tot[ROW_LB1:ROW_LB1 + 1], tot)
        grad = jnp.where(row == ROW_LB0, dlog, jnp.where(row == ROW_LB1, -dlog, tot))
        grad = jnp.where((row == ROW_HG_NORM) & (lane >= HEAD_DIM), 0.0, grad)
        grad = jnp.where((row >= ROW_LB0) & (lane >= HG_WIDTH), 0.0, grad)
        grad_ref[...] = grad
        d_ref[...], m2_ref[...], v2_ref[...] = _adam_math(wv, grad, m_ref[...], v_ref[...])
        loss_ref[...] = tot[ROW_HG_NORM:ROW_HG_NORM + 1, LOSS_LANE0:LOSS_LANE0 + LANE]

    full = pl.BlockSpec((8, D_MODEL), lambda: (0, 0))
    return pl.pallas_call(
        kern, name=name,
        in_specs=[pl.BlockSpec((N_DEV, 8, D_MODEL), lambda: (0, 0, 0)), full, full, full],
        out_specs=[full, full, full, full, pl.BlockSpec((1, LANE), lambda: (0, 0))],
        out_shape=[jax.ShapeDtypeStruct((8, D_MODEL), F32)] * 4 + [jax.ShapeDtypeStruct((1, LANE), F32)],
        compiler_params=_cparams(),
    )(gathered, w, m, v)


def _unpack_small(p, shapes):
    out = {n: p[i].reshape(shapes[n]) for i, n in enumerate(SMALL_ROWS)}
    out["hg_norm_w"] = p[ROW_HG_NORM, :HEAD_DIM].reshape(shapes["hg_norm_w"])
    out["hg_lower_bounds"] = p[ROW_LB0:ROW_LB1 + 1, :HG_WIDTH].reshape(shapes["hg_lower_bounds"])
    return out


def _reduce_scatter(gw, shard_shapes):
    pairs = _pair_exchange(gw)
    pair_sums = {f: _sum_slots(pairs[f], BF16, f"rs_pair_sum_{f}") for f in FULL_ORDER}
    slots = _chip_exchange(pair_sums)
    halves = {f: _sum_slots(slots[f], F32, f"rs_chip_sum_{f}") for f in FULL_ORDER}
    return _sibling_share(halves, shard_shapes)


WEIGHT_ORDER = ("ln_mix_w", "w_in", "hg_norm_w", "hg_lower_bounds", "w_branch_a", "w_branch_b", "w_out",
                "ln_cross_w", "ln_mem_w", "wq_cross", "wkv_cross", "wo_cross", "ln_ffn_w", "w1", "w3", "w2",
                "ln_final_w")
SMALL_WEIGHTS = tuple(n for n in WEIGHT_ORDER if n not in BIG_WEIGHTS)


def kernel(x, mem, ln_mix_w, w_in, hg_norm_w, hg_lower_bounds, w_branch_a, w_branch_b, w_out, ln_cross_w, ln_mem_w, wq_cross, wkv_cross, wo_cross, ln_ffn_w, w1, w3, w2, ln_final_w, loss_target, m_ln_mix_w, m_w_in, m_hg_norm_w, m_hg_lower_bounds, m_w_branch_a, m_w_branch_b, m_w_out, m_ln_cross_w, m_ln_mem_w, m_wq_cross, m_wkv_cross, m_wo_cross, m_ln_ffn_w, m_w1, m_w3, m_w2, m_ln_final_w, v_ln_mix_w, v_w_in, v_hg_norm_w, v_hg_lower_bounds, v_w_branch_a, v_w_branch_b, v_w_out, v_ln_cross_w, v_ln_mem_w, v_wq_cross, v_wkv_cross, v_wo_cross, v_ln_ffn_w, v_w1, v_w3, v_w2, v_ln_final_w):
    args = dict(locals())
    w = {n: args[n] for n in WEIGHT_ORDER}
    m = {n: args["m_" + n] for n in WEIGHT_ORDER}
    v = {n: args["v_" + n] for n in WEIGHT_ORDER}
    shapes = {n: w[n].shape for n in WEIGHT_ORDER}
    mat = lambda a: a.reshape(a.shape[-2:])
    shard_shapes = {n: shapes[n][-2:] for n in BIG_WEIGHTS}

    full = _gather_weights({n: mat(w[n]).astype(BF16) for n in BIG_WEIGHTS})
    small = {n: w[n].reshape(1, -1) for n in SMALL_ROWS}
    small["hg_norm_w"] = w["hg_norm_w"].reshape(1, HEAD_DIM)
    small["hg_lower_bounds"] = w["hg_lower_bounds"]
    loss, dx, gw, gs = _local_step(x.reshape(SEQ, D_MODEL), mem.reshape(MEM_LEN, D_MODEL),
                                   loss_target.reshape(SEQ, D_MODEL), small, full)

    grads = _reduce_scatter(gw, shard_shapes)
    out_g, out_d, out_m, out_v = {}, {}, {}, {}
    for n in BIG_WEIGHTS:
        out_g[n] = grads[n]
        out_d[n], out_m[n], out_v[n] = _adamw(mat(w[n]), grads[n], mat(m[n]), mat(v[n]), f"adamw_{n}")

    pad = lambda a: jnp.pad(a, ((0, 0), (0, D_MODEL - a.shape[1])))
    part = jnp.concatenate(
        [gs[n] for n in SMALL_ROWS]
        + [pad(jnp.concatenate([gs["hg_norm_w"][0], loss], axis=1)), pad(gs["hg_lb"]),
           pad(gs["hg_norm_w"][1])], axis=0)
    sg, sd, sm, sv, loss_tot = _small_update(_gather_rows(part), _pack_small(w), _pack_small(m), _pack_small(v))
    for dst, packed in ((out_g, sg), (out_d, sd), (out_m, sm), (out_v, sv)):
        dst.update(_unpack_small(packed, shapes))

    result = [loss_tot[0, 0], dx.reshape(x.shape)]
    for group in (out_g, out_d, out_m, out_v):
        result += [group[n].reshape(shapes[n]) for n in WEIGHT_ORDER]
    return tuple(result)
```

```python
import functools
import math

import jax
import jax.numpy as jnp
from jax import lax
from jax.experimental import pallas as pl
from jax.experimental.pallas import tpu as pltpu

F32 = jnp.float32
BF16 = jnp.bfloat16
MESH = pl.DeviceIdType.MESH

D_MODEL = 2048
SEQ = 2048
HEAD_DIM = 128
MEM_LEN = 256
ATT_GROUPS = ((128, 1), (512, 4), (2048, 16))
ATT_HEADS = 4
ATT_WIDTH = 3 * ATT_HEADS * HEAD_DIM
ATT_OUT = ATT_HEADS * HEAD_DIM
ATT_BLOCK = 128
HG_HEADS = 8
HG_WIDTH = HG_HEADS * HEAD_DIM
HG_CHUNK = 64
IN_WIDTH = 3 * ATT_WIDTH + 4 * HG_WIDTH + 2 * D_MODEL
CROSS_HEADS = 4
CROSS_WIDTH = CROSS_HEADS * HEAD_DIM
D_FF = 5632
RMS_EPS = 1e-6
ADAM_LR = 0.001
ADAM_B1 = 0.9
ADAM_B2 = 0.999
ADAM_EPS = 1e-08
ADAM_WD = 0.01
ADAM_STEP = 10
N_CHIPS = 4
N_DEV = 8

VMEM_LIMIT_BYTES = 48 * 1024 * 1024
LANE = 128


def _cparams(sem=None):
    return pltpu.CompilerParams(dimension_semantics=sem, vmem_limit_bytes=VMEM_LIMIT_BYTES)


def _div(n, cap, mult):
    best = None
    for d in range(mult, min(n, cap) + 1, mult):
        if n % d == 0:
            best = d
    assert best is not None, (n, cap, mult)
    return best


def _sigmoid(x):
    return 1.0 / (1.0 + jnp.exp(-x))


def _dot(a, b):
    return jnp.dot(a.astype(BF16), b.astype(BF16), preferred_element_type=F32)


def _dot_nt(a, b):
    return lax.dot_general(a.astype(BF16), b.astype(BF16), (((1,), (1,)), ((), ())),
                           preferred_element_type=F32)


def _dot_tn(a, b):
    return jnp.dot(a.astype(F32).T.astype(BF16), b.astype(BF16), preferred_element_type=F32)


def _dot_exact(a, b):
    return jnp.dot(a, b, precision=lax.Precision.HIGHEST, preferred_element_type=F32)


def _mm(a, b, *, mode, out_dtype, name, res=None):
    if mode == "nn":
        (m, k), (k2, n) = a.shape, b.shape
    elif mode == "nt":
        (m, k), (n, k2) = a.shape, b.shape
    else:
        (k, m), (k2, n) = a.shape, b.shape
    assert k == k2, (name, a.shape, b.shape)
    tm = min(512, m)
    tn = min(512, n)
    assert m % tm == 0 and n % tn == 0
    out_shape = jax.ShapeDtypeStruct((m, n), out_dtype)

    if mode == "tn":
        assert res is None

        def kern_tn(a_ref, b_ref, o_ref, at_ref):
            @pl.when(pl.program_id(1) == 0)
            def _():
                at_ref[...] = a_ref[...].astype(F32).T.astype(BF16)

            o_ref[...] = jnp.dot(at_ref[...], b_ref[...].astype(BF16),
                                 preferred_element_type=F32).astype(o_ref.dtype)

        return pl.pallas_call(
            kern_tn, name=name, grid=(m // tm, n // tn),
            in_specs=[pl.BlockSpec((k, tm), lambda i, j: (0, i)),
                      pl.BlockSpec((k, tn), lambda i, j: (0, j))],
            out_specs=pl.BlockSpec((tm, tn), lambda i, j: (i, j)),
            out_shape=out_shape,
            scratch_shapes=[pltpu.VMEM((tm, k), BF16)],
            compiler_params=_cparams(("parallel", "arbitrary")),
        )(a, b)

    tk = k if k <= 2048 else _div(k, 3072, LANE)
    nk = k // tk
    a_spec = pl.BlockSpec((tm, tk), lambda i, j, kk: (i, kk))
    if mode == "nn":
        b_spec = pl.BlockSpec((tk, tn), lambda i, j, kk: (kk, j))
        dot = _dot
    else:
        b_spec = pl.BlockSpec((tn, tk), lambda i, j, kk: (j, kk))
        dot = _dot_nt
    o_spec = pl.BlockSpec((tm, tn), lambda i, j, kk: (i, j))
    in_specs = [a_spec, b_spec]
    args = [a, b]
    if res is not None:
        in_specs.append(o_spec)
        args.append(res)
    has_res = res is not None

    def kern(*refs):
        a_ref, b_ref = refs[0], refs[1]
        r_ref = refs[2] if has_res else None
        o_ref = refs[3] if has_res else refs[2]
        part = dot(a_ref[...], b_ref[...])
        if nk == 1:
            if has_res:
                part = part + r_ref[...]
            o_ref[...] = part.astype(o_ref.dtype)
            return
        acc_ref = refs[-1]
        kk = pl.program_id(2)

        @pl.when(kk == 0)
        def _():
            acc_ref[...] = part

        @pl.when(kk > 0)
        def _():
            acc_ref[...] += part

        @pl.when(kk == nk - 1)
        def _():
            tot = acc_ref[...]
            if has_res:
                tot = tot + r_ref[...]
            o_ref[...] = tot.astype(o_ref.dtype)

    return pl.pallas_call(
        kern, name=name, grid=(m // tm, n // tn, nk),
        in_specs=in_specs, out_specs=o_spec, out_shape=out_shape,
        scratch_shapes=[pltpu.VMEM((tm, tn), F32)] if nk > 1 else [],
        compiler_params=_cparams(("parallel", "parallel", "arbitrary")),
    )(*args)


ROW_BLOCK = 256


def _rms_fwd(x, g, name):
    t, d = x.shape
    tr = min(ROW_BLOCK, t)

    def kern(x_ref, g_ref, o_ref):
        xf = x_ref[...]
        r = lax.rsqrt(jnp.mean(xf * xf, axis=-1, keepdims=True) + RMS_EPS)
        o_ref[...] = (xf * r * g_ref[...]).astype(o_ref.dtype)

    return pl.pallas_call(
        kern, name=name, grid=(t // tr,),
        in_specs=[pl.BlockSpec((tr, d), lambda i: (i, 0)), pl.BlockSpec((1, d), lambda i: (0, 0))],
        out_specs=pl.BlockSpec((tr, d), lambda i: (i, 0)),
        out_shape=jax.ShapeDtypeStruct((t, d), BF16),
        compiler_params=_cparams(("parallel",)),
    )(x, g)


def _rms_bwd(x, g, dh, res, name):
    t, d = x.shape
    tr = min(ROW_BLOCK, t)
    has_res = res is not None

    def kern(*refs):
        x_ref, g_ref, dh_ref = refs[:3]
        r_ref = refs[3] if has_res else None
        dx_ref, dg_ref = refs[-2], refs[-1]
        xf = x_ref[...]
        r = lax.rsqrt(jnp.mean(xf * xf, axis=-1, keepdims=True) + RMS_EPS)
        xn = xf * r
        dh_ = dh_ref[...]
        dhg = dh_ * g_ref[...]
        dx = r * (dhg - xn * jnp.mean(dhg * xn, axis=-1, keepdims=True))
        if has_res:
            dx = dx + r_ref[...]
        dx_ref[...] = dx
        part = jnp.sum(dh_ * xn, axis=0, keepdims=True)

        @pl.when(pl.program_id(0) == 0)
        def _():
            dg_ref[...] = part

        @pl.when(pl.program_id(0) > 0)
        def _():
            dg_ref[...] += part

    row = pl.BlockSpec((tr, d), lambda i: (i, 0))
    vec = pl.BlockSpec((1, d), lambda i: (0, 0))
    in_specs = [row, vec, row] + ([row] if has_res else [])
    args = [x, g, dh] + ([res] if has_res else [])
    return pl.pallas_call(
        kern, name=name, grid=(t // tr,), in_specs=in_specs, out_specs=[row, vec],
        out_shape=[jax.ShapeDtypeStruct((t, d), F32), jax.ShapeDtypeStruct((1, d), F32)],
        compiler_params=_cparams(("arbitrary",)),
    )(*args)


def _loss_head(x3, g, target, name):
    t, d = x3.shape
    tr = ROW_BLOCK

    def kern(x_ref, g_ref, t_ref, dx_ref, dg_ref, loss_ref):
        xf = x_ref[...]
        r = lax.rsqrt(jnp.mean(xf * xf, axis=-1, keepdims=True) + RMS_EPS)
        xn = xf * r
        gg = g_ref[...]
        err = xn * gg - t_ref[...]
        lpart = 0.5 * jnp.sum(jnp.mean(err * err, axis=-1, keepdims=True), axis=0, keepdims=True)
        dy = err * (1.0 / d)
        dyg = dy * gg
        dx_ref[...] = r * (dyg - xn * jnp.mean(dyg * xn, axis=-1, keepdims=True))
        gpart = jnp.sum(dy * xn, axis=0, keepdims=True)
        lrow = jnp.broadcast_to(lpart, (1, LANE))

        @pl.when(pl.program_id(0) == 0)
        def _():
            dg_ref[...] = gpart
            loss_ref[...] = lrow

        @pl.when(pl.program_id(0) > 0)
        def _():
            dg_ref[...] += gpart
            loss_ref[...] += lrow

    row = pl.BlockSpec((tr, d), lambda i: (i, 0))
    vec = pl.BlockSpec((1, d), lambda i: (0, 0))
    return pl.pallas_call(
        kern, name=name, grid=(t // tr,), in_specs=[row, vec, row],
        out_specs=[row, vec, pl.BlockSpec((1, LANE), lambda i: (0, 0))],
        out_shape=[jax.ShapeDtypeStruct((t, d), F32), jax.ShapeDtypeStruct((1, d), F32),
                   jax.ShapeDtypeStruct((1, LANE), F32)],
        compiler_params=_cparams(("arbitrary",)),
    )(x3, g, target)


ATT_SCALE = HEAD_DIM ** -0.5
Q_BLOCK0, K_BLOCK0, V_BLOCK0 = 0, ATT_WIDTH // HEAD_DIM, 2 * ATT_WIDTH // HEAD_DIM


def _residue_rows(dil, r, n):
    if dil == 1:
        return pl.ds(n * ATT_BLOCK, ATT_BLOCK)
    return pl.ds(n * ATT_BLOCK * dil + r, ATT_BLOCK, stride=dil)


def _band_mask(with_prev):
    width = 2 * ATT_BLOCK if with_prev else ATT_BLOCK
    iq = lax.broadcasted_iota(jnp.int32, (ATT_BLOCK, width), 0)
    ik = lax.broadcasted_iota(jnp.int32, (ATT_BLOCK, width), 1)
    if not with_prev:
        return ik <= iq
    return ((ik < ATT_BLOCK) & (iq <= ik)) | ((ik >= ATT_BLOCK) & ((ik - ATT_BLOCK) <= iq))


def _band_keys(ref, dil, r, n):
    own = ref[_residue_rows(dil, r, n), :]
    if n == 0:
        return own
    return jnp.concatenate([ref[_residue_rows(dil, r, n - 1), :], own], axis=0)


def _attn_col_spec(base, grp):
    return pl.BlockSpec((SEQ, HEAD_DIM), lambda h: (0, base + grp * ATT_HEADS + h))


def _attn_fwd(proj, grp, name):
    _, dil = ATT_GROUPS[grp]
    nb = SEQ // dil // ATT_BLOCK

    def kern(q_ref, k_ref, v_ref, o_ref, lse_ref):
        for r in range(dil):
            for n in range(nb):
                rows = _residue_rows(dil, r, n)
                s = _dot_nt(q_ref[rows, :], _band_keys(k_ref, dil, r, n)) * ATT_SCALE
                s = jnp.where(_band_mask(n > 0), s, -jnp.inf)
                m = jnp.max(s, axis=-1, keepdims=True)
                p = jnp.exp(s - m)
                l = jnp.sum(p, axis=-1, keepdims=True)
                o_ref[rows, :] = _dot(p / l, _band_keys(v_ref, dil, r, n))
                lse_ref[rows, :] = jnp.broadcast_to(m + jnp.log(l), (ATT_BLOCK, HEAD_DIM))

    out_spec = pl.BlockSpec((SEQ, HEAD_DIM), lambda h: (0, h))
    return pl.pallas_call(
        kern, name=name, grid=(ATT_HEADS,),
        in_specs=[_attn_col_spec(Q_BLOCK0, grp), _attn_col_spec(K_BLOCK0, grp), _attn_col_spec(V_BLOCK0, grp)],
        out_specs=[out_spec, out_spec],
        out_shape=[jax.ShapeDtypeStruct((SEQ, ATT_OUT), F32)] * 2,
        compiler_params=_cparams(("parallel",)),
    )(proj, proj, proj)


def _attn_weights(l0, l1, l2):
    mx = jnp.maximum(jnp.maximum(l0, l1), l2)
    e0, e1, e2 = jnp.exp(l0 - mx), jnp.exp(l1 - mx), jnp.exp(l2 - mx)
    den = e0 + e1 + e2
    return e0 / den, e1 / den, e2 / den


def _attn_merge_fwd(outs, lses, name):
    tr = ROW_BLOCK

    def kern(o0, o1, o2, l0, l1, l2, out_ref):
        a0, a1, a2 = _attn_weights(l0[...], l1[...], l2[...])
        out_ref[...] = (a0 * o0[...] + a1 * o1[...] + a2 * o2[...]).astype(out_ref.dtype)

    spec = pl.BlockSpec((tr, ATT_OUT), lambda i: (i, 0))
    return pl.pallas_call(
        kern, name=name, grid=(SEQ // tr,), in_specs=[spec] * 6, out_specs=spec,
        out_shape=jax.ShapeDtypeStruct((SEQ, ATT_OUT), BF16),
        compiler_params=_cparams(("parallel",)),
    )(*outs, *lses)


def _attn_merge_bwd(outs, lses, do_att, name):
    tr = ROW_BLOCK

    def kern(o0, o1, o2, l0, l1, l2, do_ref, d0, d1, d2, t0, t1, t2):
        alphas = _attn_weights(l0[...], l1[...], l2[...])
        do = do_ref[...]
        o_att = alphas[0] * o0[...] + alphas[1] * o1[...] + alphas[2] * o2[...]
        prod = do * o_att
        parts = []
        for h in range(ATT_HEADS):
            sl = slice(h * HEAD_DIM, (h + 1) * HEAD_DIM)
            tot = jnp.sum(prod[:, sl], axis=-1, keepdims=True)
            parts.append(jnp.broadcast_to(tot, (tr, HEAD_DIM)))
        dd = jnp.concatenate(parts, axis=1)
        for a, d_ref, t_ref in zip(alphas, (d0, d1, d2), (t0, t1, t2)):
            d_ref[...] = (a * do).astype(d_ref.dtype)
            t_ref[...] = -a * dd

    spec = pl.BlockSpec((tr, ATT_OUT), lambda i: (i, 0))
    res = pl.pallas_call(
        kern, name=name, grid=(SEQ // tr,), in_specs=[spec] * 7, out_specs=[spec] * 6,
        out_shape=[jax.ShapeDtypeStruct((SEQ, ATT_OUT), F32)] * 6,
        compiler_params=_cparams(("parallel",)),
    )(*outs, *lses, do_att)
    return res[:3], res[3:]


def _attn_bwd(proj, grp, lse, do_g, dl_g, name):
    _, dil = ATT_GROUPS[grp]
    nb = SEQ // dil // ATT_BLOCK

    def kern(q_ref, k_ref, v_ref, do_ref, lse_ref, dl_ref, dq_ref, dk_ref, dv_ref, dq_acc, dk_acc, dv_acc):
        dk_acc[...] = jnp.zeros_like(dk_acc)
        dv_acc[...] = jnp.zeros_like(dv_acc)
        for r in range(dil):
            for n in range(nb):
                rows = _residue_rows(dil, r, n)
                q, do = q_ref[rows, :], do_ref[rows, :]
                kk, vv = _band_keys(k_ref, dil, r, n), _band_keys(v_ref, dil, r, n)
                s = _dot_nt(q, kk) * ATT_SCALE
                p = jnp.where(_band_mask(n > 0), jnp.exp(s - lse_ref[rows, :][:, :1]), 0.0)
                ds = p * (_dot_nt(do, vv) + dl_ref[rows, :][:, :1])
                dq_acc[rows, :] = _dot(ds, kk) * ATT_SCALE
                dk = _dot_tn(ds, q) * ATT_SCALE
                dv = _dot_tn(p, do)
                if n > 0:
                    prev = _residue_rows(dil, r, n - 1)
                    dk_acc[prev, :] += dk[:ATT_BLOCK]
                    dv_acc[prev, :] += dv[:ATT_BLOCK]
                    dk, dv = dk[ATT_BLOCK:], dv[ATT_BLOCK:]
                dk_acc[rows, :] += dk
                dv_acc[rows, :] += dv
        dq_ref[...] = dq_acc[...].astype(dq_ref.dtype)
        dk_ref[...] = dk_acc[...].astype(dk_ref.dtype)
        dv_ref[...] = dv_acc[...].astype(dv_ref.dtype)

    spec = pl.BlockSpec((SEQ, HEAD_DIM), lambda h: (0, h))
    return pl.pallas_call(
        kern, name=name, grid=(ATT_HEADS,),
        in_specs=[_attn_col_spec(Q_BLOCK0, grp), _attn_col_spec(K_BLOCK0, grp), _attn_col_spec(V_BLOCK0, grp),
                  spec, spec, spec],
        out_specs=[spec] * 3,
        out_shape=[jax.ShapeDtypeStruct((SEQ, ATT_OUT), BF16)] * 3,
        scratch_shapes=[pltpu.VMEM((SEQ, HEAD_DIM), F32)] * 3,
        compiler_params=_cparams(("parallel",)),
    )(proj, proj, proj, do_g, lse, dl_g)


HG_HEADS_PER_STEP = 4
HG_BLOCK_W = HG_HEADS_PER_STEP * HEAD_DIM
HG_Q_BLK = (3 * ATT_WIDTH) // HG_BLOCK_W
HG_N_CHUNKS = SEQ // HG_CHUNK
HG_MID = HG_CHUNK // 2


def _lower_bound(lb_ref, sl):
    l0, l1 = lb_ref[0:1, sl], lb_ref[1:2, sl]
    mx = jnp.maximum(l0, l1)
    e0, e1 = jnp.exp(l0 - mx), jnp.exp(l1 - mx)
    return e0 / (e0 + e1)


def _tri(lower):
    i = lax.broadcasted_iota(jnp.int32, (HG_CHUNK, HG_CHUNK), 0)
    j = lax.broadcasted_iota(jnp.int32, (HG_CHUNK, HG_CHUNK), 1)
    return (i >= j) if lower else (i <= j)


def _hg_chunk_terms(qh, fh, lb):
    sig = _sigmoid(fh)
    f = lb + (1.0 - lb) * sig
    k = 1.0 - f
    b = _dot_exact(_tri(True).astype(F32), jnp.log(f))
    bl = b[HG_CHUNK - 1:HG_CHUNK, :]
    br = b[HG_MID:HG_MID + 1, :]
    sq = _sigmoid(qh)
    q = qh * sq
    return dict(sig=sig, f=f, k=k, b=b, bl=bl, br=br, sq=sq, q=q,
                e1=jnp.exp(bl - b), e2=jnp.exp(b), e3=jnp.exp(b - br), e4=jnp.exp(br - b))


def _hg_fwd(proj, lbw, normw, name):
    def in_blk(off):
        return pl.BlockSpec((HG_CHUNK, HG_BLOCK_W), lambda hp, n: (n, HG_Q_BLK + off + hp))

    def kern(q_ref, f_ref, i_ref, g_ref, lb_ref, nw_ref, oraw_ref, ohg_ref, st_ref, state):
        @pl.when(pl.program_id(1) == 0)
        def _():
            state[...] = jnp.zeros_like(state)

        causal = _tri(True)
        for hd in range(HG_HEADS_PER_STEP):
            sl = slice(hd * HEAD_DIM, (hd + 1) * HEAD_DIM)
            t = _hg_chunk_terms(q_ref[:, sl], f_ref[:, sl], _lower_bound(lb_ref, sl))
            v = i_ref[:, sl]
            st = state[hd]
            st_ref[0, hd] = st
            kd = t["k"] * t["e1"]
            inter = _dot_nt(t["q"] * t["e2"], st)
            a = jnp.where(causal, _dot_nt(t["q"] * t["e3"], t["k"] * t["e4"]), 0.0)
            o = inter + _dot(a, v)
            state[hd] = st * jnp.exp(t["bl"]) + _dot_tn(v, kd)
            oraw_ref[:, sl] = o
            r = lax.rsqrt(jnp.mean(o * o, axis=-1, keepdims=True) + RMS_EPS)
            gh = g_ref[:, sl]
            ohg_ref[:, sl] = (o * r * nw_ref[...] * (gh * _sigmoid(gh))).astype(ohg_ref.dtype)

    out_blk = pl.BlockSpec((HG_CHUNK, HG_BLOCK_W), lambda hp, n: (n, hp))
    return pl.pallas_call(
        kern, name=name, grid=(HG_HEADS // HG_HEADS_PER_STEP, HG_N_CHUNKS),
        in_specs=[in_blk(0), in_blk(2), in_blk(4), in_blk(6),
                  pl.BlockSpec((2, HG_BLOCK_W), lambda hp, n: (0, hp)),
                  pl.BlockSpec((1, HEAD_DIM), lambda hp, n: (0, 0))],
        out_specs=[out_blk, out_blk,
                   pl.BlockSpec((1, HG_HEADS_PER_STEP, HEAD_DIM, HEAD_DIM), lambda hp, n: (n, hp, 0, 0))],
        out_shape=[jax.ShapeDtypeStruct((SEQ, HG_WIDTH), F32), jax.ShapeDtypeStruct((SEQ, HG_WIDTH), BF16),
                   jax.ShapeDtypeStruct((HG_N_CHUNKS, HG_HEADS, HEAD_DIM, HEAD_DIM), F32)],
        scratch_shapes=[pltpu.VMEM((HG_HEADS_PER_STEP, HEAD_DIM, HEAD_DIM), F32)],
        compiler_params=_cparams(("parallel", "arbitrary")),
    )(proj, proj, proj, proj, lbw, normw)


def _hg_bwd(proj, lbw, normw, oraw, states, do_hg, name):
    last = HG_N_CHUNKS - 1

    def in_blk(off):
        return pl.BlockSpec((HG_CHUNK, HG_BLOCK_W), lambda hp, n: (last - n, HG_Q_BLK + off + hp))

    blk = pl.BlockSpec((HG_CHUNK, HG_BLOCK_W), lambda hp, n: (last - n, hp))

    def kern(q_ref, f_ref, i_ref, g_ref, lb_ref, nw_ref, oraw_ref, st_ref, do_ref,
             dq_ref, df_ref, di_ref, dg_ref, dlb_ref, dnw_ref, dstate):
        first = pl.program_id(1) == 0

        @pl.when(first)
        def _():
            dstate[...] = jnp.zeros_like(dstate)

        causal = _tri(True)
        rows = lax.broadcasted_iota(jnp.int32, (HG_CHUNK, HEAD_DIM), 0)
        nw = nw_ref[...]
        dnw_tot = jnp.zeros((1, HEAD_DIM), F32)
        dlb_parts = []
        for hd in range(HG_HEADS_PER_STEP):
            sl = slice(hd * HEAD_DIM, (hd + 1) * HEAD_DIM)
            qh, fh, v, gh = q_ref[:, sl], f_ref[:, sl], i_ref[:, sl], g_ref[:, sl]
            o, dout = oraw_ref[:, sl], do_ref[:, sl]
            sgg = _sigmoid(gh)
            r = lax.rsqrt(jnp.mean(o * o, axis=-1, keepdims=True) + RMS_EPS)
            xn = o * r
            dg_ref[:, sl] = (dout * xn * nw * (sgg * (1.0 + gh * (1.0 - sgg)))).astype(dg_ref.dtype)
            don = dout * (gh * sgg)
            dnw_tot = dnw_tot + jnp.sum(don * xn, axis=0, keepdims=True)
            tt = don * nw
            do = r * (tt - xn * jnp.mean(tt * xn, axis=-1, keepdims=True))
            lb = _lower_bound(lb_ref, sl)
            t = _hg_chunk_terms(qh, fh, lb)
            k, q = t["k"], t["q"]
            kd, qb, qr, kr = k * t["e1"], q * t["e2"], q * t["e3"], k * t["e4"]
            a = jnp.where(causal, _dot_nt(qr, kr), 0.0)
            st = st_ref[0, hd]
            dstn = dstate[hd]
            dqb = _dot(do, st)
            da = jnp.where(causal, _dot_nt(do, v), 0.0)
            dv = _dot_tn(a, do) + _dot_nt(kd, dstn)
            dqr = _dot(da, kr)
            dkr = _dot_tn(da, qr)
            dkd = _dot(v, dstn)
            decay = jnp.exp(t["bl"])
            ddecay = jnp.sum(dstn * st, axis=0, keepdims=True)
            dstate[hd] = dstn * decay + _dot_tn(do, qb)
            dq = dqb * t["e2"] + dqr * t["e3"]
            dk = dkd * t["e1"] + dkr * t["e4"]
            db = dqb * qb + dqr * qr - dkr * kr - dkd * kd
            dbl = jnp.sum(dkd * kd, axis=0, keepdims=True) + ddecay * decay
            dbr = jnp.sum(dkr * kr - dqr * qr, axis=0, keepdims=True)
            dlf = _dot_exact(_tri(False).astype(F32), db) + dbl + jnp.where(rows <= HG_MID, dbr, 0.0)
            df = dlf / t["f"] - dk
            sig, sq = t["sig"], t["sq"]
            df_ref[:, sl] = (df * (1.0 - lb) * sig * (1.0 - sig)).astype(df_ref.dtype)
            dlb_parts.append(jnp.sum(df * (1.0 - sig), axis=0, keepdims=True))
            dq_ref[:, sl] = (dq * (sq * (1.0 + qh * (1.0 - sq)))).astype(dq_ref.dtype)
            di_ref[:, sl] = dv.astype(di_ref.dtype)
        dlb_row = jnp.concatenate(dlb_parts, axis=1)
        dnw_blk = jnp.broadcast_to(dnw_tot, (8, HEAD_DIM))

        @pl.when(first)
        def _():
            dlb_ref[...] = dlb_row
            dnw_ref[...] = dnw_blk

        @pl.when(jnp.logical_not(first))
        def _():
            dlb_ref[...] += dlb_row
            dnw_ref[...] += dnw_blk

    n_hp = HG_HEADS // HG_HEADS_PER_STEP
    outs = pl.pallas_call(
        kern, name=name, grid=(n_hp, HG_N_CHUNKS),
        in_specs=[in_blk(0), in_blk(2), in_blk(4), in_blk(6),
                  pl.BlockSpec((2, HG_BLOCK_W), lambda hp, n: (0, hp)),
                  pl.BlockSpec((1, HEAD_DIM), lambda hp, n: (0, 0)),
                  blk,
                  pl.BlockSpec((1, HG_HEADS_PER_STEP, HEAD_DIM, HEAD_DIM), lambda hp, n: (last - n, hp, 0, 0)),
                  blk],
        out_specs=[blk, blk, blk, blk,
                   pl.BlockSpec((1, HG_BLOCK_W), lambda hp, n: (0, hp)),
                   pl.BlockSpec((8, HEAD_DIM), lambda hp, n: (hp, 0))],
        out_shape=[jax.ShapeDtypeStruct((SEQ, HG_WIDTH), BF16)] * 4
        + [jax.ShapeDtypeStruct((1, HG_WIDTH), F32), jax.ShapeDtypeStruct((8 * n_hp, HEAD_DIM), F32)],
        scratch_shapes=[pltpu.VMEM((HG_HEADS_PER_STEP, HEAD_DIM, HEAD_DIM), F32)],
        compiler_params=_cparams(("parallel", "arbitrary")),
    )(proj, proj, proj, proj, lbw, normw, oraw, states, do_hg)
    dqh, dfh, dih, dgh, dlb, dnw = outs
    return dqh, dfh, dih, dgh, dlb, (dnw[0:1], dnw[8:9])


GATE_BLOCK_W = 512
GATE_A_BLK = (3 * ATT_WIDTH + 4 * HG_WIDTH) // GATE_BLOCK_W
GATE_B_BLK = GATE_A_BLK + D_MODEL // GATE_BLOCK_W


def _gate_specs():
    tr = ROW_BLOCK
    blk = pl.BlockSpec((tr, GATE_BLOCK_W), lambda i, j: (i, j))
    ga = pl.BlockSpec((tr, GATE_BLOCK_W), lambda i, j: (i, GATE_A_BLK + j))
    gb = pl.BlockSpec((tr, GATE_BLOCK_W), lambda i, j: (i, GATE_B_BLK + j))
    return (SEQ // tr, D_MODEL // GATE_BLOCK_W), blk, ga, gb


def _gate_fwd(proj, ya, yb, name):
    grid, blk, ga, gb = _gate_specs()

    def kern(ga_ref, gb_ref, ya_ref, yb_ref, o_ref):
        o_ref[...] = (_sigmoid(ga_ref[...]) * ya_ref[...] + _sigmoid(gb_ref[...]) * yb_ref[...]).astype(o_ref.dtype)

    return pl.pallas_call(
        kern, name=name, grid=grid, in_specs=[ga, gb, blk, blk], out_specs=blk,
        out_shape=jax.ShapeDtypeStruct((SEQ, D_MODEL), BF16),
        compiler_params=_cparams(("parallel", "parallel")),
    )(proj, proj, ya, yb)


def _gate_bwd(proj, ya, yb, dmerged, name):
    grid, blk, ga, gb = _gate_specs()

    def kern(ga_ref, gb_ref, ya_ref, yb_ref, dm_ref, dya_ref, dyb_ref, dga_ref, dgb_ref):
        dm = dm_ref[...]
        sa, sb = _sigmoid(ga_ref[...]), _sigmoid(gb_ref[...])
        dya_ref[...] = (dm * sa).astype(dya_ref.dtype)
        dyb_ref[...] = (dm * sb).astype(dyb_ref.dtype)
        dga_ref[...] = (dm * ya_ref[...] * sa * (1.0 - sa)).astype(dga_ref.dtype)
        dgb_ref[...] = (dm * yb_ref[...] * sb * (1.0 - sb)).astype(dgb_ref.dtype)

    return pl.pallas_call(
        kern, name=name, grid=grid, in_specs=[ga, gb, blk, blk, blk], out_specs=[blk] * 4,
        out_shape=[jax.ShapeDtypeStruct((SEQ, D_MODEL), BF16)] * 4,
        compiler_params=_cparams(("parallel", "parallel")),
    )(proj, proj, ya, yb, dmerged)


FF_SHARD = D_FF // N_CHIPS


def _swiglu_fwd(ab, name):
    tr = ROW_BLOCK

    def kern(ab_ref, u_ref):
        a, b = ab_ref[:, :FF_SHARD], ab_ref[:, FF_SHARD:]
        u_ref[...] = (a * _sigmoid(a) * b).astype(u_ref.dtype)

    return pl.pallas_call(
        kern, name=name, grid=(SEQ // tr, N_CHIPS),
        in_specs=[pl.BlockSpec((tr, 2 * FF_SHARD), lambda i, j: (i, j))],
        out_specs=pl.BlockSpec((tr, FF_SHARD), lambda i, j: (i, j)),
        out_shape=jax.ShapeDtypeStruct((SEQ, D_FF), BF16),
        compiler_params=_cparams(("parallel", "parallel")),
    )(ab)


def _swiglu_bwd(ab, du, name):
    tr = ROW_BLOCK

    def kern(ab_ref, du_ref, dab_ref):
        a, b = ab_ref[:, :FF_SHARD], ab_ref[:, FF_SHARD:]
        du_ = du_ref[...]
        sg = _sigmoid(a)
        dab_ref[:, :FF_SHARD] = (du_ * b * (sg * (1.0 + a * (1.0 - sg)))).astype(dab_ref.dtype)
        dab_ref[:, FF_SHARD:] = (du_ * (a * sg)).astype(dab_ref.dtype)

    wide = pl.BlockSpec((tr, 2 * FF_SHARD), lambda i, j: (i, j))
    return pl.pallas_call(
        kern, name=name, grid=(SEQ // tr, N_CHIPS),
        in_specs=[wide, pl.BlockSpec((tr, FF_SHARD), lambda i, j: (i, j))],
        out_specs=wide, out_shape=jax.ShapeDtypeStruct((SEQ, 2 * D_FF), BF16),
        compiler_params=_cparams(("parallel", "parallel")),
    )(ab, du)


CROSS_ROWS = 512


def _cross_fwd(qc, kvc, name):
    def kern(q_ref, k_ref, v_ref, o_ref):
        s = _dot_nt(q_ref[...], k_ref[...]) * ATT_SCALE
        m = jnp.max(s, axis=-1, keepdims=True)
        e = jnp.exp(s - m)
        p = e / jnp.sum(e, axis=-1, keepdims=True)
        o_ref[...] = _dot(p, v_ref[...]).astype(o_ref.dtype)

    qblk = pl.BlockSpec((CROSS_ROWS, HEAD_DIM), lambda h, i: (i, h))
    return pl.pallas_call(
        kern, name=name, grid=(CROSS_HEADS, SEQ // CROSS_ROWS),
        in_specs=[qblk, pl.BlockSpec((MEM_LEN, HEAD_DIM), lambda h, i: (0, h)),
                  pl.BlockSpec((MEM_LEN, HEAD_DIM), lambda h, i: (0, CROSS_HEADS + h))],
        out_specs=qblk, out_shape=jax.ShapeDtypeStruct((SEQ, CROSS_WIDTH), BF16),
        compiler_params=_cparams(("parallel", "parallel")),
    )(qc, kvc, kvc)


def _cross_bwd(qc, kvc, doc, name):
    def kern(q_ref, k_ref, v_ref, do_ref, dq_ref, dk_ref, dv_ref):
        q, k, v, do = q_ref[...], k_ref[...], v_ref[...], do_ref[...]
        s = _dot_nt(q, k) * ATT_SCALE
        m = jnp.max(s, axis=-1, keepdims=True)
        e = jnp.exp(s - m)
        p = e / jnp.sum(e, axis=-1, keepdims=True)
        dp = _dot_nt(do, v)
        ds = p * (dp - jnp.sum(dp * p, axis=-1, keepdims=True))
        dq_ref[...] = (_dot(ds, k) * ATT_SCALE).astype(dq_ref.dtype)
        dk = _dot_tn(ds, q) * ATT_SCALE
        dv = _dot_tn(p, do)

        @pl.when(pl.program_id(1) == 0)
        def _():
            dk_ref[...] = dk
            dv_ref[...] = dv

        @pl.when(pl.program_id(1) > 0)
        def _():
            dk_ref[...] += dk
            dv_ref[...] += dv

    qblk = pl.BlockSpec((CROSS_ROWS, HEAD_DIM), lambda h, i: (i, h))
    kblk = pl.BlockSpec((MEM_LEN, HEAD_DIM), lambda h, i: (0, h))
    dq, dk, dv = pl.pallas_call(
        kern, name=name, grid=(CROSS_HEADS, SEQ // CROSS_ROWS),
        in_specs=[qblk, kblk, pl.BlockSpec((MEM_LEN, HEAD_DIM), lambda h, i: (0, CROSS_HEADS + h)), qblk],
        out_specs=[qblk, kblk, kblk],
        out_shape=[jax.ShapeDtypeStruct((SEQ, CROSS_WIDTH), BF16),
                   jax.ShapeDtypeStruct((MEM_LEN, CROSS_WIDTH), F32),
                   jax.ShapeDtypeStruct((MEM_LEN, CROSS_WIDTH), F32)],
        compiler_params=_cparams(("parallel", "arbitrary")),
    )(qc, kvc, kvc, doc)
    return dq, jnp.concatenate([dk, dv], axis=1)


def _local_step(x, mem, target, small, full):
    h = _rms_fwd(x, small["ln_mix_w"], "rms_mix")
    proj = _mm(h, full["w_in"], mode="nn", out_dtype=F32, name="mm_proj")
    att = [_attn_fwd(proj, g, f"attn_fwd_g{g}") for g in range(3)]
    outs, lses = [a[0] for a in att], [a[1] for a in att]
    o_att = _attn_merge_fwd(outs, lses, "attn_merge")
    oraw, o_hg, states = _hg_fwd(proj, small["hg_lower_bounds"], small["hg_norm_w"], "hgrn_fwd")
    ya = _mm(o_att, full["w_branch_a"], mode="nn", out_dtype=F32, name="mm_branch_a")
    yb = _mm(o_hg, full["w_branch_b"], mode="nn", out_dtype=F32, name="mm_branch_b")
    merged = _gate_fwd(proj, ya, yb, "gate_fwd")
    x1 = _mm(merged, full["w_out"], mode="nn", out_dtype=F32, name="mm_out", res=x)

    hc = _rms_fwd(x1, small["ln_cross_w"], "rms_cross")
    mn = _rms_fwd(mem, small["ln_mem_w"], "rms_mem")
    qc = _mm(hc, full["wq_cross"], mode="nn", out_dtype=F32, name="mm_q")
    kvc = _mm(mn, full["wkv_cross"], mode="nn", out_dtype=F32, name="mm_kv")
    oc = _cross_fwd(qc, kvc, "cross_fwd")
    x2 = _mm(oc, full["wo_cross"], mode="nn", out_dtype=F32, name="mm_o", res=x1)

    hf = _rms_fwd(x2, small["ln_ffn_w"], "rms_ffn")
    ab = _mm(hf, full["w13"], mode="nn", out_dtype=F32, name="mm_w13")
    u = _swiglu_fwd(ab, "swiglu_fwd")
    x3 = _mm(u, full["w2"], mode="nn", out_dtype=F32, name="mm_w2", res=x2)

    dx3, dg_final, loss = _loss_head(x3, small["ln_final_w"], target, "loss_head")

    gw, gs = {}, {"ln_final_w": dg_final}
    du = _mm(dx3, full["w2"], mode="nt", out_dtype=F32, name="mm_du")
    gw["w2"] = _mm(u, dx3, mode="tn", out_dtype=BF16, name="mm_dw2")
    dab = _swiglu_bwd(ab, du, "swiglu_bwd")
    gw["w13"] = _mm(hf, dab, mode="tn", out_dtype=BF16, name="mm_dw13")
    dhf = _mm(dab, full["w13"], mode="nt", out_dtype=F32, name="mm_dhf")
    dx2, gs["ln_ffn_w"] = _rms_bwd(x2, small["ln_ffn_w"], dhf, dx3, "rms_ffn_bwd")
    doc = _mm(dx2, full["wo_cross"], mode="nt", out_dtype=BF16, name="mm_doc")
    gw["wo_cross"] = _mm(oc, dx2, mode="tn", out_dtype=BF16, name="mm_dwo")
    dqc, dkvc = _cross_bwd(qc, kvc, doc, "cross_bwd")
    gw["wq_cross"] = _mm(hc, dqc, mode="tn", out_dtype=BF16, name="mm_dwq")
    dhc = _mm(dqc, full["wq_cross"], mode="nt", out_dtype=F32, name="mm_dhc")
    gw["wkv_cross"] = _mm(mn, dkvc, mode="tn", out_dtype=BF16, name="mm_dwkv")
    dmn = _mm(dkvc, full["wkv_cross"], mode="nt", out_dtype=F32, name="mm_dmn")
    _, gs["ln_mem_w"] = _rms_bwd(mem, small["ln_mem_w"], dmn, None, "rms_mem_bwd")
    dx1, gs["ln_cross_w"] = _rms_bwd(x1, small["ln_cross_w"], dhc, dx2, "rms_cross_bwd")
    dmerged = _mm(dx1, full["w_out"], mode="nt", out_dtype=F32, name="mm_dmerged")
    gw["w_out"] = _mm(merged, dx1, mode="tn", out_dtype=BF16, name="mm_dwout")
    dya, dyb, dga, dgb = _gate_bwd(proj, ya, yb, dmerged, "gate_bwd")
    gw["w_branch_a"] = _mm(o_att, dya, mode="tn", out_dtype=BF16, name="mm_dwa")
    do_att = _mm(dya, full["w_branch_a"], mode="nt", out_dtype=F32, name="mm_doatt")
    gw["w_branch_b"] = _mm(o_hg, dyb, mode="tn", out_dtype=BF16, name="mm_dwb")
    do_hg = _mm(dyb, full["w_branch_b"], mode="nt", out_dtype=F32, name="mm_dohg")
    dqh, dfh, dih, dgh, dlb, gs["hg_norm_w"] = _hg_bwd(
        proj, small["hg_lower_bounds"], small["hg_norm_w"], oraw, states, do_hg, "hgrn_bwd")
    gs["hg_lb"] = dlb
    do_gs, dl_gs = _attn_merge_bwd(outs, lses, do_att, "attn_merge_bwd")
    dqs, dks, dvs = zip(*[_attn_bwd(proj, g, lses[g], do_gs[g], dl_gs[g], f"attn_bwd_g{g}") for g in range(3)])
    dproj = jnp.concatenate([*dqs, *dks, *dvs, dqh, dfh, dih, dgh, dga, dgb], axis=1)
    gw["w_in"] = _mm(h, dproj, mode="tn", out_dtype=BF16, name="mm_dwin")
    dh = _mm(dproj, full["w_in"], mode="nt", out_dtype=F32, name="mm_dh")
    dx, gs["ln_mix_w"] = _rms_bwd(x, small["ln_mix_w"], dh, dx1, "rms_mix_bwd")
    return loss, dx, gw, gs


FULL_SPECS = {
    "w_in": ("col", D_MODEL, IN_WIDTH),
    "w_branch_a": ("col", ATT_OUT, D_MODEL),
    "w_branch_b": ("col", HG_WIDTH, D_MODEL),
    "w_out": ("row", D_MODEL, D_MODEL),
    "wq_cross": ("row", D_MODEL, CROSS_WIDTH),
    "wkv_cross": ("row", D_MODEL, 2 * CROSS_WIDTH),
    "wo_cross": ("col", CROSS_WIDTH, D_MODEL),
    "w13": ("col", D_MODEL, 2 * D_FF),
    "w2": ("row", D_FF, D_MODEL),
}
FULL_ORDER = tuple(FULL_SPECS)
WEIGHT_PLACE = {
    "w_in": ("w_in", 0), "w_branch_a": ("w_branch_a", 0), "w_branch_b": ("w_branch_b", 0),
    "w_out": ("w_out", 0), "wq_cross": ("wq_cross", 0), "wkv_cross": ("wkv_cross", 0),
    "wo_cross": ("wo_cross", 0), "w1": ("w13", 0), "w3": ("w13", FF_SHARD), "w2": ("w2", 0),
}
BIG_WEIGHTS = tuple(WEIGHT_PLACE)
ANY = pl.BlockSpec(memory_space=pl.ANY)


def _position():
    return lax.axis_index("x"), lax.axis_index("y"), lax.axis_index("c")


def _other_chips(x, y):
    return [(1 - x, y), (x, 1 - y), (1 - x, 1 - y)]


def _half(ref, kind, h, lead=None):
    r, c = ref.shape[-2], ref.shape[-1]
    if kind == "col":
        idx = (pl.ds(h * (r // 2), r // 2), slice(None))
    else:
        idx = (slice(None), pl.ds(h * (c // 2), c // 2))
    return ref.at[idx] if lead is None else ref.at[(lead,) + idx]


def _shard_of(ref, kind, start, size):
    return ref.at[:, pl.ds(start, size)] if kind == "col" else ref.at[pl.ds(start, size), :]


def _slice_width(name):
    kind, rows, cols = FULL_SPECS[name]
    return (cols if kind == "col" else rows) // N_CHIPS


def _half_shape(kind, rows, cols):
    return (rows // 2, cols) if kind == "col" else (rows, cols // 2)


def _mesh_scalars():
    x, y, c = _position()
    return jnp.stack([c, 2 * x + y]).astype(jnp.int32)


def _grid_spec(grid, in_specs, out_specs):
    return pltpu.PrefetchScalarGridSpec(num_scalar_prefetch=1, grid=grid, in_specs=in_specs, out_specs=out_specs)


def _cast_into_full(parts, fname, pos, specs, place, name):
    kind, rows, cols = specs[fname]
    ws = [w for w in place if place[w][0] == fname]
    if kind == "col":
        stride = cols // N_CHIPS
        hr = rows // 2
        tr = _div(hr, max(16, EW_BLOCK_ELEMS // stride), 16)
        nrb = hr // tr
        grid = (nrb,)
        in_specs = [pl.BlockSpec((tr, parts[w].shape[1]), lambda i, pos_ref: (i + pos_ref[0] * nrb, 0)) for w in ws]
        out_spec = pl.BlockSpec((tr, stride), lambda i, pos_ref: (i + pos_ref[0] * nrb, pos_ref[1]))
    else:
        stride = rows // N_CHIPS
        hc = cols // 2
        tr = _div(stride, max(16, EW_BLOCK_ELEMS // hc), 16)
        nrb = stride // tr
        grid = (nrb,)
        in_specs = [pl.BlockSpec((tr, hc), lambda i, pos_ref: (i, pos_ref[0])) for w in ws]
        out_spec = pl.BlockSpec((tr, hc), lambda i, pos_ref: (i + pos_ref[1] * nrb, pos_ref[0]))

    def kern(pos_ref, *refs):
        o_ref = refs[-1]
        for w, r in zip(ws, refs[:-1]):
            off = place[w][1] if kind == "col" else 0
            o_ref[:, off:off + r.shape[1]] = r[...].astype(o_ref.dtype)

    return pl.pallas_call(
        kern, name=name, grid_spec=_grid_spec(grid, in_specs, out_spec),
        out_shape=jax.ShapeDtypeStruct((rows, cols), BF16),
        compiler_params=_cparams(("parallel",)),
    )(pos, *[parts[w] for w in ws])


def _gather_weights(fulls, specs=FULL_SPECS, name="gather_weights"):
    fnames = list(specs)
    n_f = len(fnames)

    def body(*refs):
        dst = dict(zip(fnames, refs[n_f:2 * n_f]))
        ici_send, ici_recv, d2d_send, d2d_recv = refs[2 * n_f:]
        x, y, c = _position()
        j = 2 * x + y
        chips = _other_chips(x, y)

        def region(fname, chip):
            kind, rows, cols = specs[fname]
            stride = (cols if kind == "col" else rows) // N_CHIPS
            return _half(_shard_of(dst[fname], kind, chip * stride, stride), kind, c)

        sends = []
        for f, fname in enumerate(fnames):
            for p, (px, py) in enumerate(chips):
                cp = pltpu.make_async_remote_copy(
                    src_ref=region(fname, j), dst_ref=region(fname, j),
                    send_sem=ici_send.at[3 * f + p], recv_sem=ici_recv.at[3 * f + p],
                    device_id=(px, py, c), device_id_type=MESH)
                cp.start()
                sends.append(cp)
        for f, fname in enumerate(fnames):
            kind = specs[fname][0]
            for p, (px, py) in enumerate(chips):
                pltpu.make_async_remote_copy(
                    src_ref=region(fname, j), dst_ref=region(fname, 2 * px + py),
                    send_sem=ici_send.at[3 * f + p], recv_sem=ici_recv.at[3 * f + p],
                    device_id=(px, py, c), device_id_type=MESH).wait_recv()
            cp = pltpu.make_async_remote_copy(
                src_ref=_half(dst[fname], kind, c), dst_ref=_half(dst[fname], kind, c),
                send_sem=d2d_send.at[f], recv_sem=d2d_recv.at[f],
                device_id=(x, y, 1 - c), device_id_type=MESH)
            cp.start()
            sends.append(cp)
        for f, fname in enumerate(fnames):
            kind = specs[fname][0]
            pltpu.make_async_remote_copy(
                src_ref=_half(dst[fname], kind, 1 - c), dst_ref=_half(dst[fname], kind, 1 - c),
                send_sem=d2d_send.at[f], recv_sem=d2d_recv.at[f],
                device_id=(x, y, 1 - c), device_id_type=MESH).wait_recv()
        for cp in sends:
            cp.wait_send()

    outs = pl.pallas_call(
        body, name=name,
        in_specs=[ANY] * n_f, out_specs=[ANY] * n_f,
        out_shape=[jax.ShapeDtypeStruct(specs[f][1:], BF16) for f in fnames],
        input_output_aliases={i: i for i in range(n_f)},
        scratch_shapes=[pltpu.SemaphoreType.DMA((3 * n_f,)), pltpu.SemaphoreType.DMA((3 * n_f,)),
                        pltpu.SemaphoreType.DMA((n_f,)), pltpu.SemaphoreType.DMA((n_f,))],
    )(*[fulls[f] for f in fnames])
    return dict(zip(fnames, outs))


def _pair_exchange(grads, specs=FULL_SPECS, name="rs_pair_exchange"):
    fnames = list(specs)
    n_f = len(fnames)

    def body(*refs):
        g = dict(zip(fnames, refs[:n_f]))
        out = dict(zip(fnames, refs[n_f:2 * n_f]))
        send_sem, recv_sem = refs[2 * n_f:]
        x, y, c = _position()
        sends = []
        for f, fname in enumerate(fnames):
            cp = pltpu.make_async_remote_copy(
                src_ref=_half(g[fname], specs[fname][0], 1 - c), dst_ref=out[fname],
                send_sem=send_sem.at[f], recv_sem=recv_sem.at[f],
                device_id=(x, y, 1 - c), device_id_type=MESH)
            cp.start()
            sends.append(cp)
        for cp in sends:
            cp.wait_recv()
        for cp in sends:
            cp.wait_send()

    outs = pl.pallas_call(
        body, name=name, in_specs=[ANY] * n_f, out_specs=[ANY] * n_f,
        out_shape=[jax.ShapeDtypeStruct(_half_shape(*specs[f]), BF16) for f in fnames],
        scratch_shapes=[pltpu.SemaphoreType.DMA((n_f,)), pltpu.SemaphoreType.DMA((n_f,))],
    )(*[grads[f] for f in fnames])
    return dict(zip(fnames, outs))


def _pair_sum(grad, recv, pos, spec, name):
    kind, rows, cols = spec
    hr, hc = _half_shape(kind, rows, cols)
    tr, tc = _ew_block(hr, hc)
    nrb, ncb = hr // tr, hc // tc
    blk = pl.BlockSpec((tr, tc), lambda i, jj, pos_ref: (i, jj))
    if kind == "col":
        mine = pl.BlockSpec((tr, tc), lambda i, jj, pos_ref: (i + pos_ref[0] * nrb, jj))
    else:
        mine = pl.BlockSpec((tr, tc), lambda i, jj, pos_ref: (i, jj + pos_ref[0] * ncb))

    def kern(pos_ref, g_ref, r_ref, o_ref):
        o_ref[...] = (g_ref[...].astype(F32) + r_ref[...].astype(F32)).astype(o_ref.dtype)

    return pl.pallas_call(
        kern, name=name, grid_spec=_grid_spec((nrb, ncb), [mine, blk], blk),
        out_shape=jax.ShapeDtypeStruct((hr, hc), BF16),
        compiler_params=_cparams(("parallel", "parallel")),
    )(pos, grad, recv)


def _slot_shape(spec):
    kind, rows, cols = spec
    hr, hc = _half_shape(kind, rows, cols)
    return (hr, hc // N_CHIPS) if kind == "col" else (hr // N_CHIPS, hc)


def _chip_exchange(pair_sums, specs=FULL_SPECS, name="rs_chip_exchange"):
    fnames = list(specs)
    n_f = len(fnames)

    def body(*refs):
        ps = dict(zip(fnames, refs[:n_f]))
        out = dict(zip(fnames, refs[n_f:2 * n_f]))
        send_sem, recv_sem = refs[2 * n_f:]
        x, y, c = _position()
        sends = []
        for f, fname in enumerate(fnames):
            kind = specs[fname][0]
            width = _slot_shape(specs[fname])[1 if kind == "col" else 0]
            for p, (px, py) in enumerate(_other_chips(x, y)):
                cp = pltpu.make_async_remote_copy(
                    src_ref=_shard_of(ps[fname], kind, (2 * px + py) * width, width), dst_ref=out[fname].at[p],
                    send_sem=send_sem.at[3 * f + p], recv_sem=recv_sem.at[3 * f + p],
                    device_id=(px, py, c), device_id_type=MESH)
                cp.start()
                sends.append(cp)
        for cp in sends:
            cp.wait_recv()
        for cp in sends:
            cp.wait_send()

    outs = pl.pallas_call(
        body, name=name, in_specs=[ANY] * n_f, out_specs=[ANY] * n_f,
        out_shape=[jax.ShapeDtypeStruct((N_CHIPS - 1,) + _slot_shape(specs[f]), BF16) for f in fnames],
        scratch_shapes=[pltpu.SemaphoreType.DMA((3 * n_f,)), pltpu.SemaphoreType.DMA((3 * n_f,))],
    )(*[pair_sums[f] for f in fnames])
    return dict(zip(fnames, outs))


def _chip_sum(pair_sum, slots, pos, fname, shard_shapes, specs, place, name):
    kind, rows, cols = specs[fname]
    sr, sc = _slot_shape(specs[fname])
    ws = [w for w in place if place[w][0] == fname]
    n_slots = N_CHIPS - 1
    if kind == "col":
        tr = _div(sr, max(16, EW_BLOCK_ELEMS // sc), 16)
        nrb = sr // tr
        grid = (nrb,)
        own = pl.BlockSpec((tr, sc), lambda i, pos_ref: (i, pos_ref[1]))
        slot = pl.BlockSpec((n_slots, tr, sc), lambda i, pos_ref: (0, i, 0))
        out_specs = [pl.BlockSpec((tr, shard_shapes[w][1]), lambda i, pos_ref: (i + pos_ref[0] * nrb, 0)) for w in ws]
    else:
        tr = _div(sr, max(16, EW_BLOCK_ELEMS // sc), 16)
        nrb = sr // tr
        grid = (nrb,)
        own = pl.BlockSpec((tr, sc), lambda i, pos_ref: (i + pos_ref[1] * nrb, 0))
        slot = pl.BlockSpec((n_slots, tr, sc), lambda i, pos_ref: (0, i, 0))
        out_specs = [pl.BlockSpec((tr, sc), lambda i, pos_ref: (i, pos_ref[0])) for w in ws]

    def kern(pos_ref, own_ref, slot_ref, *out_refs):
        tot = own_ref[...].astype(F32)
        for s in range(n_slots):
            tot = tot + slot_ref[s].astype(F32)
        for w, o_ref in zip(ws, out_refs):
            off = place[w][1] if kind == "col" else 0
            o_ref[...] = tot[:, off:off + o_ref.shape[1]]

    outs = pl.pallas_call(
        kern, name=name, grid_spec=_grid_spec(grid, [own, slot], out_specs),
        out_shape=[jax.ShapeDtypeStruct(shard_shapes[w], F32) for w in ws],
        compiler_params=_cparams(("parallel",)),
    )(pos, pair_sum, slots)
    return dict(zip(ws, outs))


def _sibling_share(grads, specs=FULL_SPECS, place=WEIGHT_PLACE, name="rs_sibling_share"):
    wnames = list(place)
    n_w = len(wnames)

    def body(*refs):
        out = dict(zip(wnames, refs[n_w:2 * n_w]))
        send_sem, recv_sem = refs[2 * n_w:]
        x, y, c = _position()
        sends = []
        for i, w in enumerate(wnames):
            kind = specs[place[w][0]][0]
            cp = pltpu.make_async_remote_copy(
                src_ref=_half(out[w], kind, c), dst_ref=_half(out[w], kind, c),
                send_sem=send_sem.at[i], recv_sem=recv_sem.at[i],
                device_id=(x, y, 1 - c), device_id_type=MESH)
            cp.start()
            sends.append(cp)
        for i, w in enumerate(wnames):
            kind = specs[place[w][0]][0]
            pltpu.make_async_remote_copy(
                src_ref=_half(out[w], kind, 1 - c), dst_ref=_half(out[w], kind, 1 - c),
                send_sem=send_sem.at[i], recv_sem=recv_sem.at[i],
                device_id=(x, y, 1 - c), device_id_type=MESH).wait_recv()
        for cp in sends:
            cp.wait_send()

    outs = pl.pallas_call(
        body, name=name, in_specs=[ANY] * n_w, out_specs=[ANY] * n_w,
        out_shape=[jax.ShapeDtypeStruct(grads[w].shape, F32) for w in wnames],
        input_output_aliases={i: i for i in range(n_w)},
        scratch_shapes=[pltpu.SemaphoreType.DMA((n_w,)), pltpu.SemaphoreType.DMA((n_w,))],
    )(*[grads[w] for w in wnames])
    return dict(zip(wnames, outs))


def _gather_rows(v, name="gather_small"):
    shape = v.shape

    def body(v_ref, out_ref, send_sem, recv_sem, loc_sem):
        x, y, c = _position()
        me = 4 * x + 2 * y + c
        flips = [(fx, fy, fc) for fx in (0, 1) for fy in (0, 1) for fc in (0, 1)][1:]

        def peer(fl):
            return tuple(1 - a if f else a for a, f in zip((x, y, c), fl))

        loc = pltpu.make_async_copy(v_ref, out_ref.at[me], loc_sem)
        loc.start()
        sends = []
        for i, fl in enumerate(flips):
            cp = pltpu.make_async_remote_copy(
                src_ref=v_ref, dst_ref=out_ref.at[me], send_sem=send_sem.at[i], recv_sem=recv_sem.at[i],
                device_id=peer(fl), device_id_type=MESH)
            cp.start()
            sends.append(cp)
        for i, fl in enumerate(flips):
            px, py, pc = peer(fl)
            pltpu.make_async_remote_copy(
                src_ref=v_ref, dst_ref=out_ref.at[4 * px + 2 * py + pc],
                send_sem=send_sem.at[i], recv_sem=recv_sem.at[i],
                device_id=peer(fl), device_id_type=MESH).wait_recv()
        for cp in sends:
            cp.wait_send()
        loc.wait()

    return pl.pallas_call(
        body, name=name, in_specs=[ANY], out_specs=ANY,
        out_shape=jax.ShapeDtypeStruct((N_DEV,) + shape, F32),
        scratch_shapes=[pltpu.SemaphoreType.DMA((N_DEV - 1,)), pltpu.SemaphoreType.DMA((N_DEV - 1,)),
                        pltpu.SemaphoreType.DMA],
    )(v)


EW_BLOCK_ELEMS = 256 * 1024


def _ew_block(rows, cols):
    tc = cols if cols <= 4096 else _div(cols, 2048, LANE)
    tr = _div(rows, max(16, EW_BLOCK_ELEMS // tc), 16)
    return tr, tc


def _adam_math(w, g, m, v):
    m2 = ADAM_B1 * m + (1.0 - ADAM_B1) * g
    v2 = ADAM_B2 * v + (1.0 - ADAM_B2) * (g * g)
    m_hat = m2 / (1.0 - ADAM_B1 ** ADAM_STEP)
    v_hat = v2 / (1.0 - ADAM_B2 ** ADAM_STEP)
    delta = -ADAM_LR * (m_hat / (jnp.sqrt(v_hat) + ADAM_EPS) + ADAM_WD * w)
    return delta, m2, v2


def _adamw(w, g, m, v, name):
    rows, cols = w.shape
    tr, tc = _ew_block(rows, cols)

    def kern(w_ref, g_ref, m_ref, v_ref, d_ref, m2_ref, v2_ref):
        d_ref[...], m2_ref[...], v2_ref[...] = _adam_math(w_ref[...], g_ref[...], m_ref[...], v_ref[...])

    blk = pl.BlockSpec((tr, tc), lambda i, j: (i, j))
    return pl.pallas_call(
        kern, name=name, grid=(rows // tr, cols // tc), in_specs=[blk] * 4, out_specs=[blk] * 3,
        out_shape=[jax.ShapeDtypeStruct((rows, cols), F32)] * 3,
        compiler_params=_cparams(("parallel", "parallel")),
    )(w, g, m, v)


SMALL_ROWS = ("ln_mix_w", "ln_cross_w", "ln_mem_w", "ln_ffn_w", "ln_final_w")
ROW_HG_NORM, ROW_LB0, ROW_LB1 = 5, 6, 7
LOSS_LANE0 = HEAD_DIM


def _pack_small(vals):
    rows = [vals[n].reshape(1, D_MODEL) for n in SMALL_ROWS]
    pad = lambda a: jnp.pad(a, ((0, 0), (0, D_MODEL - a.shape[1])))
    rows.append(pad(vals["hg_norm_w"].reshape(1, HEAD_DIM)))
    rows.append(pad(vals["hg_lower_bounds"].reshape(2, HG_WIDTH)))
    return jnp.concatenate(rows, axis=0)


def _small_update(gathered, w, m, v, name="small_update"):
    def kern(g_ref, w_ref, m_ref, v_ref, grad_ref, d_ref, m2_ref, v2_ref, loss_ref):
        tot = g_ref[0]
        for s in range(1, N_DEV):
            tot = tot + g_ref[s]
        wv = w_ref[...]
        row = lax.broadcasted_iota(jnp.int32, (8, D_MODEL), 0)
        lane = lax.broadcasted_iota(jnp.int32, (8, D_MODEL), 1)
        l0, l1 = wv[ROW_LB0:ROW_LB0 + 1], wv[ROW_LB1:ROW_LB1 + 1]
        mx = jnp.maximum(l0, l1)
        e0, e1 = jnp.exp(l0 - mx), jnp.exp(l1 - mx)
        p0 = e0 / (e0 + e1)
        dlog = tot[ROW_LB0:ROW_LB0 + 1] * p0 * (1.0 - p0)
        tot = jnp.where(row == ROW_HG_NORM, tot + tot[ROW_LB1:ROW_LB1 + 1], tot)
        grad = jnp.where(row == ROW_LB0, dlog, jnp.where(row == ROW_LB1, -dlog, tot))
        grad = jnp.where((row == ROW_HG_NORM) & (lane >= HEAD_DIM), 0.0, grad)
        grad = jnp.where((row >= ROW_LB0) & (lane >= HG_WIDTH), 0.0, grad)
        grad_ref[...] = grad
        d_ref[...], m2_ref[...], v2_ref[...] = _adam_math(wv, grad, m_ref[...], v_ref[...])
        loss_ref[...] = tot[ROW_HG_NORM:ROW_HG_NORM + 1, LOSS_LANE0:LOSS_LANE0 + LANE]

    full = pl.BlockSpec((8, D_MODEL), lambda: (0, 0))
    return pl.pallas_call(
        kern, name=name,
        in_specs=[pl.BlockSpec((N_DEV, 8, D_MODEL), lambda: (0, 0, 0)), full, full, full],
        out_specs=[full, full, full, full, pl.BlockSpec((1, LANE), lambda: (0, 0))],
        out_shape=[jax.ShapeDtypeStruct((8, D_MODEL), F32)] * 4 + [jax.ShapeDtypeStruct((1, LANE), F32)],
        compiler_params=_cparams(),
    )(gathered, w, m, v)


def _unpack_small(p, shapes):
    out = {n: p[i].reshape(shapes[n]) for i, n in enumerate(SMALL_ROWS)}
    out["hg_norm_w"] = p[ROW_HG_NORM, :HEAD_DIM].reshape(shapes["hg_norm_w"])
    out["hg_lower_bounds"] = p[ROW_LB0:ROW_LB1 + 1, :HG_WIDTH].reshape(shapes["hg_lower_bounds"])
    return out


def _reduce_scatter(gw, shard_shapes, pos, specs=FULL_SPECS, place=WEIGHT_PLACE):
    recv = _pair_exchange(gw, specs)
    pair_sums = {f: _pair_sum(gw[f], recv[f], pos, specs[f], f"rs_pair_sum_{f}") for f in specs}
    slots = _chip_exchange(pair_sums, specs)
    halves = {}
    for f in specs:
        halves.update(_chip_sum(pair_sums[f], slots[f], pos, f, shard_shapes, specs, place, f"rs_chip_sum_{f}"))
    return _sibling_share(halves, specs, place)


def _gather_all(shards, pos, specs=FULL_SPECS, place=WEIGHT_PLACE):
    fulls = {f: _cast_into_full({w: shards[w] for w in place if place[w][0] == f}, f, pos, specs, place,
                                f"cast_{f}") for f in specs}
    return _gather_weights(fulls, specs)


WEIGHT_ORDER = ("ln_mix_w", "w_in", "hg_norm_w", "hg_lower_bounds", "w_branch_a", "w_branch_b", "w_out",
                "ln_cross_w", "ln_mem_w", "wq_cross", "wkv_cross", "wo_cross", "ln_ffn_w", "w1", "w3", "w2",
                "ln_final_w")
SMALL_WEIGHTS = tuple(n for n in WEIGHT_ORDER if n not in BIG_WEIGHTS)


def kernel(x, mem, ln_mix_w, w_in, hg_norm_w, hg_lower_bounds, w_branch_a, w_branch_b, w_out, ln_cross_w, ln_mem_w, wq_cross, wkv_cross, wo_cross, ln_ffn_w, w1, w3, w2, ln_final_w, loss_target, m_ln_mix_w, m_w_in, m_hg_norm_w, m_hg_lower_bounds, m_w_branch_a, m_w_branch_b, m_w_out, m_ln_cross_w, m_ln_mem_w, m_wq_cross, m_wkv_cross, m_wo_cross, m_ln_ffn_w, m_w1, m_w3, m_w2, m_ln_final_w, v_ln_mix_w, v_w_in, v_hg_norm_w, v_hg_lower_bounds, v_w_branch_a, v_w_branch_b, v_w_out, v_ln_cross_w, v_ln_mem_w, v_wq_cross, v_wkv_cross, v_wo_cross, v_ln_ffn_w, v_w1, v_w3, v_w2, v_ln_final_w):
    args = dict(locals())
    w = {n: args[n] for n in WEIGHT_ORDER}
    m = {n: args["m_" + n] for n in WEIGHT_ORDER}
    v = {n: args["v_" + n] for n in WEIGHT_ORDER}
    shapes = {n: w[n].shape for n in WEIGHT_ORDER}
    mat = lambda a: a.reshape(a.shape[-2:])
    shard_shapes = {n: shapes[n][-2:] for n in BIG_WEIGHTS}

    pos = _mesh_scalars()
    full = _gather_all({n: mat(w[n]) for n in BIG_WEIGHTS}, pos)
    small = {n: w[n].reshape(1, -1) for n in SMALL_ROWS}
    small["hg_norm_w"] = w["hg_norm_w"].reshape(1, HEAD_DIM)
    small["hg_lower_bounds"] = w["hg_lower_bounds"]
    loss, dx, gw, gs = _local_step(x.reshape(SEQ, D_MODEL), mem.reshape(MEM_LEN, D_MODEL),
                                   loss_target.reshape(SEQ, D_MODEL), small, full)

    grads = _reduce_scatter(gw, shard_shapes, pos)
    out_g, out_d, out_m, out_v = {}, {}, {}, {}
    for n in BIG_WEIGHTS:
        out_g[n] = grads[n]
        out_d[n], out_m[n], out_v[n] = _adamw(mat(w[n]), grads[n], mat(m[n]), mat(v[n]), f"adamw_{n}")

    pad = lambda a: jnp.pad(a, ((0, 0), (0, D_MODEL - a.shape[1])))
    part = jnp.concatenate(
        [gs[n] for n in SMALL_ROWS]
        + [pad(jnp.concatenate([gs["hg_norm_w"][0], loss], axis=1)), pad(gs["hg_lb"]),
           pad(gs["hg_norm_w"][1])], axis=0)
    sg, sd, sm, sv, loss_tot = _small_update(_gather_rows(part), _pack_small(w), _pack_small(m), _pack_small(v))
    for dst, packed in ((out_g, sg), (out_d, sd), (out_m, sm), (out_v, sv)):
        dst.update(_unpack_small(packed, shapes))

    result = [loss_tot[0, 0], dx.reshape(x.shape)]
    for group in (out_g, out_d, out_m, out_v):
        result += [group[n].reshape(shapes[n]) for n in WEIGHT_ORDER]
    return tuple(result)
```

```python
import math

import jax
import jax.numpy as jnp
from jax import lax
from jax.experimental import pallas as pl
from jax.experimental.pallas import tpu as pltpu

F32 = jnp.float32
BF16 = jnp.bfloat16
MESH = pl.DeviceIdType.MESH

D_MODEL = 2048
SEQ = 2048
HEAD_DIM = 128
MEM_LEN = 256
ATT_GROUPS = ((128, 1), (512, 4), (2048, 16))
ATT_HEADS = 4
ATT_WIDTH = 3 * ATT_HEADS * HEAD_DIM
ATT_OUT = ATT_HEADS * HEAD_DIM
ATT_BLOCK = 128
HG_HEADS = 8
HG_WIDTH = HG_HEADS * HEAD_DIM
HG_CHUNK = 64
IN_WIDTH = 3 * ATT_WIDTH + 4 * HG_WIDTH + 2 * D_MODEL
CROSS_HEADS = 4
CROSS_WIDTH = CROSS_HEADS * HEAD_DIM
D_FF = 5632
RMS_EPS = 1e-6
ADAM_LR = 0.001
ADAM_B1 = 0.9
ADAM_B2 = 0.999
ADAM_EPS = 1e-08
ADAM_WD = 0.01
ADAM_STEP = 10
N_CHIPS = 4
N_DEV = 8

VMEM_LIMIT_BYTES = 48 * 1024 * 1024
LANE = 128
ANY = pl.BlockSpec(memory_space=pl.ANY)


def _cparams(sem=None):
    return pltpu.CompilerParams(dimension_semantics=sem, vmem_limit_bytes=VMEM_LIMIT_BYTES)


def _div(n, cap, mult):
    best = None
    for d in range(mult, min(n, cap) + 1, mult):
        if n % d == 0:
            best = d
    assert best is not None, (n, cap, mult)
    return best


def _sigmoid(x):
    return 1.0 / (1.0 + jnp.exp(-x))


def _dot(a, b):
    return jnp.dot(a.astype(BF16), b.astype(BF16), preferred_element_type=F32)


def _dot_nt(a, b):
    return lax.dot_general(a.astype(BF16), b.astype(BF16), (((1,), (1,)), ((), ())),
                           preferred_element_type=F32)


def _dot_tn(a, b):
    return jnp.dot(a.astype(F32).T.astype(BF16), b.astype(BF16), preferred_element_type=F32)


def _dot_exact(a, b):
    return jnp.dot(a, b, precision=lax.Precision.HIGHEST, preferred_element_type=F32)


class _Carried:
    def __init__(self, arrays, fresh, n_sems, start, finish, mid=None):
        self.arrays, self.fresh, self.n_sems = arrays, fresh, n_sems
        self.start, self.mid, self.finish = start, mid, finish


def _carried_layout(carried):
    akeys = [(ci, k) for ci, cm in enumerate(carried) for k in cm.arrays]
    fkeys = [(ci, k) for ci, cm in enumerate(carried) for k in cm.fresh]
    arrays = [carried[ci].arrays[k] for ci, k in akeys]
    shapes = [jax.ShapeDtypeStruct(a.shape, a.dtype) for a in arrays] + [carried[ci].fresh[k] for ci, k in fkeys]
    sems = []
    for cm in carried:
        sems += [pltpu.SemaphoreType.DMA((cm.n_sems,)), pltpu.SemaphoreType.DMA((cm.n_sems,))]
    return akeys, fkeys, arrays, shapes, sems


def _carried_results(carried, akeys, fkeys, outs):
    res = [dict() for _ in carried]
    for (ci, k), o in zip(akeys + fkeys, outs):
        res[ci][k] = o
    return res


def _pcall(kern, *, name, grid, in_specs, out_specs, out_shape, args, scratch_shapes=(), semantics=None,
           carried=()):
    carried = list(carried)
    single = not isinstance(out_shape, (list, tuple))
    out_specs = [out_specs] if single else list(out_specs)
    out_shape = [out_shape] if single else list(out_shape)
    n_in, n_out, n_scr = len(in_specs), len(out_shape), len(scratch_shapes)
    if not carried:
        outs = pl.pallas_call(kern, name=name, grid=grid, in_specs=list(in_specs), out_specs=out_specs,
                              out_shape=out_shape, scratch_shapes=list(scratch_shapes),
                              compiler_params=_cparams(semantics))(*args)
        return (outs[0] if single else list(outs)), []
    akeys, fkeys, arrays, shapes, sems = _carried_layout(carried)
    n_a, n_f = len(akeys), len(fkeys)
    total = math.prod(grid)
    mid_step = min(total - 1, (3 * total) // 5)

    def wrapped(*refs):
        ins = refs[:n_in]
        o0 = n_in + n_a
        outs = refs[o0:o0 + n_out]
        a0 = o0 + n_out
        s0 = a0 + n_a + n_f
        per = [dict() for _ in carried]
        for (ci, k), r in zip(akeys + fkeys, refs[a0:s0]):
            per[ci][k] = r
        scratch = refs[s0:s0 + n_scr]
        sem = refs[s0 + n_scr:]
        step = 0
        for d, g in enumerate(grid):
            step = step * g + pl.program_id(d)

        @pl.when(step == 0)
        def _():
            for ci, cm in enumerate(carried):
                cm.start(per[ci], sem[2 * ci], sem[2 * ci + 1])

        kern(*ins, *outs, *scratch)

        @pl.when(step == mid_step)
        def _():
            for ci, cm in enumerate(carried):
                if cm.mid is not None:
                    cm.mid(per[ci], sem[2 * ci], sem[2 * ci + 1])

        @pl.when(step == total - 1)
        def _():
            for ci, cm in enumerate(carried):
                cm.finish(per[ci], sem[2 * ci], sem[2 * ci + 1])

    outs = pl.pallas_call(
        wrapped, name=name, grid=grid,
        in_specs=list(in_specs) + [ANY] * n_a, out_specs=out_specs + [ANY] * (n_a + n_f),
        out_shape=out_shape + shapes,
        input_output_aliases={n_in + i: n_out + i for i in range(n_a)},
        scratch_shapes=list(scratch_shapes) + sems,
        compiler_params=_cparams(("arbitrary",) * len(grid)),
    )(*args, *arrays)
    res = _carried_results(carried, akeys, fkeys, outs[n_out:])
    return (outs[0] if single else list(outs[:n_out])), res


def _run_comm(carried, name):
    carried = list(carried)
    akeys, fkeys, arrays, shapes, sems = _carried_layout(carried)
    n_a, n_f = len(akeys), len(fkeys)

    def body(*refs):
        per = [dict() for _ in carried]
        for (ci, k), r in zip(akeys + fkeys, refs[n_a:2 * n_a + n_f]):
            per[ci][k] = r
        sem = refs[2 * n_a + n_f:]
        for hook in ("start", "mid", "finish"):
            for ci, cm in enumerate(carried):
                fn = getattr(cm, hook)
                if fn is not None:
                    fn(per[ci], sem[2 * ci], sem[2 * ci + 1])

    outs = pl.pallas_call(
        body, name=name, in_specs=[ANY] * n_a, out_specs=[ANY] * (n_a + n_f), out_shape=shapes,
        input_output_aliases={i: i for i in range(n_a)}, scratch_shapes=sems,
    )(*arrays)
    return _carried_results(carried, akeys, fkeys, outs)


def _mm(a, b, *, mode, out_dtype, name, res=None, carried=()):
    if mode == "nn":
        (m, k), (k2, n) = a.shape, b.shape
    elif mode == "nt":
        (m, k), (n, k2) = a.shape, b.shape
    else:
        (k, m), (k2, n) = a.shape, b.shape
    assert k == k2, (name, a.shape, b.shape)
    tm = min(512, m)
    tn = min(512, n)
    assert m % tm == 0 and n % tn == 0
    out_shape = jax.ShapeDtypeStruct((m, n), out_dtype)

    if mode == "tn":
        assert res is None

        def kern_tn(a_ref, b_ref, o_ref, at_ref):
            @pl.when(pl.program_id(1) == 0)
            def _():
                at_ref[...] = a_ref[...].astype(F32).T.astype(BF16)

            o_ref[...] = jnp.dot(at_ref[...], b_ref[...].astype(BF16),
                                 preferred_element_type=F32).astype(o_ref.dtype)

        return _pcall(
            kern_tn, name=name, grid=(m // tm, n // tn),
            in_specs=[pl.BlockSpec((k, tm), lambda i, j: (0, i)),
                      pl.BlockSpec((k, tn), lambda i, j: (0, j))],
            out_specs=pl.BlockSpec((tm, tn), lambda i, j: (i, j)),
            out_shape=out_shape, args=(a, b),
            scratch_shapes=[pltpu.VMEM((tm, k), BF16)],
            semantics=("parallel", "arbitrary"), carried=carried)

    tk = k if k <= 2048 else _div(k, 3072, LANE)
    nk = k // tk
    a_spec = pl.BlockSpec((tm, tk), lambda i, j, kk: (i, kk))
    if mode == "nn":
        b_spec = pl.BlockSpec((tk, tn), lambda i, j, kk: (kk, j))
        dot = _dot
    else:
        b_spec = pl.BlockSpec((tn, tk), lambda i, j, kk: (j, kk))
        dot = _dot_nt
    o_spec = pl.BlockSpec((tm, tn), lambda i, j, kk: (i, j))
    in_specs = [a_spec, b_spec]
    args = [a, b]
    if res is not None:
        in_specs.append(o_spec)
        args.append(res)
    has_res = res is not None

    def kern(*refs):
        a_ref, b_ref = refs[0], refs[1]
        r_ref = refs[2] if has_res else None
        o_ref = refs[3] if has_res else refs[2]
        part = dot(a_ref[...], b_ref[...])
        if nk == 1:
            if has_res:
                part = part + r_ref[...]
            o_ref[...] = part.astype(o_ref.dtype)
            return
        acc_ref = refs[-1]
        kk = pl.program_id(2)

        @pl.when(kk == 0)
        def _():
            acc_ref[...] = part

        @pl.when(kk > 0)
        def _():
            acc_ref[...] += part

        @pl.when(kk == nk - 1)
        def _():
            tot = acc_ref[...]
            if has_res:
                tot = tot + r_ref[...]
            o_ref[...] = tot.astype(o_ref.dtype)

    return _pcall(
        kern, name=name, grid=(m // tm, n // tn, nk),
        in_specs=in_specs, out_specs=o_spec, out_shape=out_shape, args=args,
        scratch_shapes=[pltpu.VMEM((tm, tn), F32)] if nk > 1 else [],
        semantics=("parallel", "parallel", "arbitrary"), carried=carried)


ROW_BLOCK = 256


def _rms_fwd(x, g, name):
    t, d = x.shape
    tr = min(ROW_BLOCK, t)

    def kern(x_ref, g_ref, o_ref):
        xf = x_ref[...]
        r = lax.rsqrt(jnp.mean(xf * xf, axis=-1, keepdims=True) + RMS_EPS)
        o_ref[...] = (xf * r * g_ref[...]).astype(o_ref.dtype)

    return pl.pallas_call(
        kern, name=name, grid=(t // tr,),
        in_specs=[pl.BlockSpec((tr, d), lambda i: (i, 0)), pl.BlockSpec((1, d), lambda i: (0, 0))],
        out_specs=pl.BlockSpec((tr, d), lambda i: (i, 0)),
        out_shape=jax.ShapeDtypeStruct((t, d), BF16),
        compiler_params=_cparams(("parallel",)),
    )(x, g)


def _rms_bwd(x, g, dh, res, name):
    t, d = x.shape
    tr = min(ROW_BLOCK, t)
    has_res = res is not None

    def kern(*refs):
        x_ref, g_ref, dh_ref = refs[:3]
        r_ref = refs[3] if has_res else None
        dx_ref, dg_ref = refs[-2], refs[-1]
        xf = x_ref[...]
        r = lax.rsqrt(jnp.mean(xf * xf, axis=-1, keepdims=True) + RMS_EPS)
        xn = xf * r
        dh_ = dh_ref[...]
        dhg = dh_ * g_ref[...]
        dx = r * (dhg - xn * jnp.mean(dhg * xn, axis=-1, keepdims=True))
        if has_res:
            dx = dx + r_ref[...]
        dx_ref[...] = dx
        part = jnp.sum(dh_ * xn, axis=0, keepdims=True)

        @pl.when(pl.program_id(0) == 0)
        def _():
            dg_ref[...] = part

        @pl.when(pl.program_id(0) > 0)
        def _():
            dg_ref[...] += part

    row = pl.BlockSpec((tr, d), lambda i: (i, 0))
    vec = pl.BlockSpec((1, d), lambda i: (0, 0))
    in_specs = [row, vec, row] + ([row] if has_res else [])
    args = [x, g, dh] + ([res] if has_res else [])
    return pl.pallas_call(
        kern, name=name, grid=(t // tr,), in_specs=in_specs, out_specs=[row, vec],
        out_shape=[jax.ShapeDtypeStruct((t, d), F32), jax.ShapeDtypeStruct((1, d), F32)],
        compiler_params=_cparams(("arbitrary",)),
    )(*args)


def _loss_head(x3, g, target, name):
    t, d = x3.shape
    tr = ROW_BLOCK

    def kern(x_ref, g_ref, t_ref, dx_ref, dg_ref, loss_ref):
        xf = x_ref[...]
        r = lax.rsqrt(jnp.mean(xf * xf, axis=-1, keepdims=True) + RMS_EPS)
        xn = xf * r
        gg = g_ref[...]
        err = xn * gg - t_ref[...]
        lpart = 0.5 * jnp.sum(jnp.mean(err * err, axis=-1, keepdims=True), axis=0, keepdims=True)
        dy = err * (1.0 / d)
        dyg = dy * gg
        dx_ref[...] = r * (dyg - xn * jnp.mean(dyg * xn, axis=-1, keepdims=True))
        gpart = jnp.sum(dy * xn, axis=0, keepdims=True)
        lrow = jnp.broadcast_to(lpart, (1, LANE))

        @pl.when(pl.program_id(0) == 0)
        def _():
            dg_ref[...] = gpart
            loss_ref[...] = lrow

        @pl.when(pl.program_id(0) > 0)
        def _():
            dg_ref[...] += gpart
            loss_ref[...] += lrow

    row = pl.BlockSpec((tr, d), lambda i: (i, 0))
    vec = pl.BlockSpec((1, d), lambda i: (0, 0))
    return pl.pallas_call(
        kern, name=name, grid=(t // tr,), in_specs=[row, vec, row],
        out_specs=[row, vec, pl.BlockSpec((1, LANE), lambda i: (0, 0))],
        out_shape=[jax.ShapeDtypeStruct((t, d), F32), jax.ShapeDtypeStruct((1, d), F32),
                   jax.ShapeDtypeStruct((1, LANE), F32)],
        compiler_params=_cparams(("arbitrary",)),
    )(x3, g, target)


ATT_SCALE = HEAD_DIM ** -0.5
Q_BLOCK0, K_BLOCK0, V_BLOCK0 = 0, ATT_WIDTH // HEAD_DIM, 2 * ATT_WIDTH // HEAD_DIM


def _residue_rows(dil, r, n):
    if dil == 1:
        return pl.ds(n * ATT_BLOCK, ATT_BLOCK)
    return pl.ds(n * ATT_BLOCK * dil + r, ATT_BLOCK, stride=dil)


def _band_mask(with_prev):
    width = 2 * ATT_BLOCK if with_prev else ATT_BLOCK
    iq = lax.broadcasted_iota(jnp.int32, (ATT_BLOCK, width), 0)
    ik = lax.broadcasted_iota(jnp.int32, (ATT_BLOCK, width), 1)
    if not with_prev:
        return ik <= iq
    return ((ik < ATT_BLOCK) & (iq <= ik)) | ((ik >= ATT_BLOCK) & ((ik - ATT_BLOCK) <= iq))


def _band_keys(ref, dil, r, n):
    own = ref[_residue_rows(dil, r, n), :]
    if n == 0:
        return own
    return jnp.concatenate([ref[_residue_rows(dil, r, n - 1), :], own], axis=0)


def _attn_col_spec(base, grp):
    return pl.BlockSpec((SEQ, HEAD_DIM), lambda h: (0, base + grp * ATT_HEADS + h))


def _attn_fwd(proj, grp, name):
    _, dil = ATT_GROUPS[grp]
    nb = SEQ // dil // ATT_BLOCK

    def kern(q_ref, k_ref, v_ref, o_ref, lse_ref):
        for r in range(dil):
            for n in range(nb):
                rows = _residue_rows(dil, r, n)
                s = _dot_nt(q_ref[rows, :], _band_keys(k_ref, dil, r, n)) * ATT_SCALE
                s = jnp.where(_band_mask(n > 0), s, -jnp.inf)
                m = jnp.max(s, axis=-1, keepdims=True)
                p = jnp.exp(s - m)
                l = jnp.sum(p, axis=-1, keepdims=True)
                o_ref[rows, :] = _dot(p / l, _band_keys(v_ref, dil, r, n))
                lse_ref[rows, :] = jnp.broadcast_to(m + jnp.log(l), (ATT_BLOCK, HEAD_DIM))

    out_spec = pl.BlockSpec((SEQ, HEAD_DIM), lambda h: (0, h))
    return pl.pallas_call(
        kern, name=name, grid=(ATT_HEADS,),
        in_specs=[_attn_col_spec(Q_BLOCK0, grp), _attn_col_spec(K_BLOCK0, grp), _attn_col_spec(V_BLOCK0, grp)],
        out_specs=[out_spec, out_spec],
        out_shape=[jax.ShapeDtypeStruct((SEQ, ATT_OUT), F32)] * 2,
        compiler_params=_cparams(("parallel",)),
    )(proj, proj, proj)


def _attn_weights(l0, l1, l2):
    mx = jnp.maximum(jnp.maximum(l0, l1), l2)
    e0, e1, e2 = jnp.exp(l0 - mx), jnp.exp(l1 - mx), jnp.exp(l2 - mx)
    den = e0 + e1 + e2
    return e0 / den, e1 / den, e2 / den


def _attn_merge_fwd(outs, lses, name):
    tr = ROW_BLOCK

    def kern(o0, o1, o2, l0, l1, l2, out_ref):
        a0, a1, a2 = _attn_weights(l0[...], l1[...], l2[...])
        out_ref[...] = (a0 * o0[...] + a1 * o1[...] + a2 * o2[...]).astype(out_ref.dtype)

    spec = pl.BlockSpec((tr, ATT_OUT), lambda i: (i, 0))
    return pl.pallas_call(
        kern, name=name, grid=(SEQ // tr,), in_specs=[spec] * 6, out_specs=spec,
        out_shape=jax.ShapeDtypeStruct((SEQ, ATT_OUT), BF16),
        compiler_params=_cparams(("parallel",)),
    )(*outs, *lses)


def _attn_merge_bwd(outs, lses, do_att, name, carried=()):
    tr = ROW_BLOCK

    def kern(o0, o1, o2, l0, l1, l2, do_ref, d0, d1, d2, t0, t1, t2):
        alphas = _attn_weights(l0[...], l1[...], l2[...])
        do = do_ref[...]
        o_att = alphas[0] * o0[...] + alphas[1] * o1[...] + alphas[2] * o2[...]
        prod = do * o_att
        parts = []
        for h in range(ATT_HEADS):
            sl = slice(h * HEAD_DIM, (h + 1) * HEAD_DIM)
            tot = jnp.sum(prod[:, sl], axis=-1, keepdims=True)
            parts.append(jnp.broadcast_to(tot, (tr, HEAD_DIM)))
        dd = jnp.concatenate(parts, axis=1)
        for a, d_ref, t_ref in zip(alphas, (d0, d1, d2), (t0, t1, t2)):
            d_ref[...] = a * do
            t_ref[...] = -a * dd

    spec = pl.BlockSpec((tr, ATT_OUT), lambda i: (i, 0))
    res, cres = _pcall(
        kern, name=name, grid=(SEQ // tr,), in_specs=[spec] * 7, out_specs=[spec] * 6,
        out_shape=[jax.ShapeDtypeStruct((SEQ, ATT_OUT), F32)] * 6, args=(*outs, *lses, do_att),
        semantics=("parallel",), carried=carried)
    return (res[:3], res[3:]), cres


def _attn_bwd(proj, grp, lse, do_g, dl_g, name):
    _, dil = ATT_GROUPS[grp]
    nb = SEQ // dil // ATT_BLOCK

    def kern(q_ref, k_ref, v_ref, do_ref, lse_ref, dl_ref, dq_ref, dk_ref, dv_ref, dq_acc, dk_acc, dv_acc):
        dk_acc[...] = jnp.zeros_like(dk_acc)
        dv_acc[...] = jnp.zeros_like(dv_acc)
        for r in range(dil):
            for n in range(nb):
                rows = _residue_rows(dil, r, n)
                q, do = q_ref[rows, :], do_ref[rows, :]
                kk, vv = _band_keys(k_ref, dil, r, n), _band_keys(v_ref, dil, r, n)
                s = _dot_nt(q, kk) * ATT_SCALE
                p = jnp.where(_band_mask(n > 0), jnp.exp(s - lse_ref[rows, :][:, :1]), 0.0)
                ds = p * (_dot_nt(do, vv) + dl_ref[rows, :][:, :1])
                dq_acc[rows, :] = _dot(ds, kk) * ATT_SCALE
                dk = _dot_tn(ds, q) * ATT_SCALE
                dv = _dot_tn(p, do)
                if n > 0:
                    prev = _residue_rows(dil, r, n - 1)
                    dk_acc[prev, :] += dk[:ATT_BLOCK]
                    dv_acc[prev, :] += dv[:ATT_BLOCK]
                    dk, dv = dk[ATT_BLOCK:], dv[ATT_BLOCK:]
                dk_acc[rows, :] += dk
                dv_acc[rows, :] += dv
        dq_ref[...] = dq_acc[...].astype(dq_ref.dtype)
        dk_ref[...] = dk_acc[...].astype(dk_ref.dtype)
        dv_ref[...] = dv_acc[...].astype(dv_ref.dtype)

    spec = pl.BlockSpec((SEQ, HEAD_DIM), lambda h: (0, h))
    return pl.pallas_call(
        kern, name=name, grid=(ATT_HEADS,),
        in_specs=[_attn_col_spec(Q_BLOCK0, grp), _attn_col_spec(K_BLOCK0, grp), _attn_col_spec(V_BLOCK0, grp),
                  spec, spec, spec],
        out_specs=[spec] * 3,
        out_shape=[jax.ShapeDtypeStruct((SEQ, ATT_OUT), BF16)] * 3,
        scratch_shapes=[pltpu.VMEM((SEQ, HEAD_DIM), F32)] * 3,
        compiler_params=_cparams(("parallel",)),
    )(proj, proj, proj, do_g, lse, dl_g)


HG_HEADS_PER_STEP = 4
HG_BLOCK_W = HG_HEADS_PER_STEP * HEAD_DIM
HG_Q_BLK = (3 * ATT_WIDTH) // HG_BLOCK_W
HG_N_CHUNKS = SEQ // HG_CHUNK
HG_MID = HG_CHUNK // 2


def _lower_bound(lb_ref, sl):
    l0, l1 = lb_ref[0:1, sl], lb_ref[1:2, sl]
    mx = jnp.maximum(l0, l1)
    e0, e1 = jnp.exp(l0 - mx), jnp.exp(l1 - mx)
    return e0 / (e0 + e1)


def _tri(lower):
    i = lax.broadcasted_iota(jnp.int32, (HG_CHUNK, HG_CHUNK), 0)
    j = lax.broadcasted_iota(jnp.int32, (HG_CHUNK, HG_CHUNK), 1)
    return (i >= j) if lower else (i <= j)


def _hg_chunk_terms(qh, fh, lb):
    sig = _sigmoid(fh)
    f = lb + (1.0 - lb) * sig
    k = 1.0 - f
    b = _dot_exact(_tri(True).astype(F32), jnp.log(f))
    bl = b[HG_CHUNK - 1:HG_CHUNK, :]
    br = b[HG_MID:HG_MID + 1, :]
    sq = _sigmoid(qh)
    q = qh * sq
    return dict(sig=sig, f=f, k=k, b=b, bl=bl, br=br, sq=sq, q=q,
                e1=jnp.exp(bl - b), e2=jnp.exp(b), e3=jnp.exp(b - br), e4=jnp.exp(br - b))


def _hg_fwd(proj, lbw, normw, name, carried=()):
    def in_blk(off):
        return pl.BlockSpec((HG_CHUNK, HG_BLOCK_W), lambda hp, n: (n, HG_Q_BLK + off + hp))

    def kern(q_ref, f_ref, i_ref, g_ref, lb_ref, nw_ref, oraw_ref, ohg_ref, st_ref, state):
        @pl.when(pl.program_id(1) == 0)
        def _():
            state[...] = jnp.zeros_like(state)

        causal = _tri(True)
        for hd in range(HG_HEADS_PER_STEP):
            sl = slice(hd * HEAD_DIM, (hd + 1) * HEAD_DIM)
            t = _hg_chunk_terms(q_ref[:, sl], f_ref[:, sl], _lower_bound(lb_ref, sl))
            v = i_ref[:, sl]
            st = state[hd]
            st_ref[0, hd] = st
            kd = t["k"] * t["e1"]
            inter = _dot_nt(t["q"] * t["e2"], st)
            a = jnp.where(causal, _dot_nt(t["q"] * t["e3"], t["k"] * t["e4"]), 0.0)
            o = inter + _dot(a, v)
            state[hd] = st * jnp.exp(t["bl"]) + _dot_tn(v, kd)
            oraw_ref[:, sl] = o
            r = lax.rsqrt(jnp.mean(o * o, axis=-1, keepdims=True) + RMS_EPS)
            gh = g_ref[:, sl]
            ohg_ref[:, sl] = (o * r * nw_ref[...] * (gh * _sigmoid(gh))).astype(ohg_ref.dtype)

    out_blk = pl.BlockSpec((HG_CHUNK, HG_BLOCK_W), lambda hp, n: (n, hp))
    return _pcall(
        kern, name=name, grid=(HG_HEADS // HG_HEADS_PER_STEP, HG_N_CHUNKS),
        in_specs=[in_blk(0), in_blk(2), in_blk(4), in_blk(6),
                  pl.BlockSpec((2, HG_BLOCK_W), lambda hp, n: (0, hp)),
                  pl.BlockSpec((1, HEAD_DIM), lambda hp, n: (0, 0))],
        out_specs=[out_blk, out_blk,
                   pl.BlockSpec((1, HG_HEADS_PER_STEP, HEAD_DIM, HEAD_DIM), lambda hp, n: (n, hp, 0, 0))],
        out_shape=[jax.ShapeDtypeStruct((SEQ, HG_WIDTH), F32), jax.ShapeDtypeStruct((SEQ, HG_WIDTH), BF16),
                   jax.ShapeDtypeStruct((HG_N_CHUNKS, HG_HEADS, HEAD_DIM, HEAD_DIM), F32)],
        args=(proj, proj, proj, proj, lbw, normw),
        scratch_shapes=[pltpu.VMEM((HG_HEADS_PER_STEP, HEAD_DIM, HEAD_DIM), F32)],
        semantics=("parallel", "arbitrary"), carried=carried)


def _hg_bwd(proj, lbw, normw, oraw, states, do_hg, name, carried=()):
    last = HG_N_CHUNKS - 1

    def in_blk(off):
        return pl.BlockSpec((HG_CHUNK, HG_BLOCK_W), lambda hp, n: (last - n, HG_Q_BLK + off + hp))

    blk = pl.BlockSpec((HG_CHUNK, HG_BLOCK_W), lambda hp, n: (last - n, hp))

    def kern(q_ref, f_ref, i_ref, g_ref, lb_ref, nw_ref, oraw_ref, st_ref, do_ref,
             dq_ref, df_ref, di_ref, dg_ref, dlb_ref, dnw_ref, dstate):
        first = pl.program_id(1) == 0

        @pl.when(first)
        def _():
            dstate[...] = jnp.zeros_like(dstate)

        causal = _tri(True)
        rows = lax.broadcasted_iota(jnp.int32, (HG_CHUNK, HEAD_DIM), 0)
        nw = nw_ref[...]
        dnw_tot = jnp.zeros((1, HEAD_DIM), F32)
        dlb_parts = []
        for hd in range(HG_HEADS_PER_STEP):
            sl = slice(hd * HEAD_DIM, (hd + 1) * HEAD_DIM)
            qh, fh, v, gh = q_ref[:, sl], f_ref[:, sl], i_ref[:, sl], g_ref[:, sl]
            o, dout = oraw_ref[:, sl], do_ref[:, sl]
            sgg = _sigmoid(gh)
            r = lax.rsqrt(jnp.mean(o * o, axis=-1, keepdims=True) + RMS_EPS)
            xn = o * r
            dg_ref[:, sl] = (dout * xn * nw * (sgg * (1.0 + gh * (1.0 - sgg)))).astype(dg_ref.dtype)
            don = dout * (gh * sgg)
            dnw_tot = dnw_tot + jnp.sum(don * xn, axis=0, keepdims=True)
            tt = don * nw
            do = r * (tt - xn * jnp.mean(tt * xn, axis=-1, keepdims=True))
            lb = _lower_bound(lb_ref, sl)
            t = _hg_chunk_terms(qh, fh, lb)
            k, q = t["k"], t["q"]
            kd, qb, qr, kr = k * t["e1"], q * t["e2"], q * t["e3"], k * t["e4"]
            a = jnp.where(causal, _dot_nt(qr, kr), 0.0)
            st = st_ref[0, hd]
            dstn = dstate[hd]
            dqb = _dot(do, st)
            da = jnp.where(causal, _dot_nt(do, v), 0.0)
            dv = _dot_tn(a, do) + _dot_nt(kd, dstn)
            dqr = _dot(da, kr)
            dkr = _dot_tn(da, qr)
            dkd = _dot(v, dstn)
            decay = jnp.exp(t["bl"])
            ddecay = jnp.sum(dstn * st, axis=0, keepdims=True)
            dstate[hd] = dstn * decay + _dot_tn(do, qb)
            dq = dqb * t["e2"] + dqr * t["e3"]
            dk = dkd * t["e1"] + dkr * t["e4"]
            db = dqb * qb + dqr * qr - dkr * kr - dkd * kd
            dbl = jnp.sum(dkd * kd, axis=0, keepdims=True) + ddecay * decay
            dbr = jnp.sum(dkr * kr - dqr * qr, axis=0, keepdims=True)
            dlf = _dot_exact(_tri(False).astype(F32), db) + dbl + jnp.where(rows <= HG_MID, dbr, 0.0)
            df = dlf / t["f"] - dk
            sig, sq = t["sig"], t["sq"]
            df_ref[:, sl] = (df * (1.0 - lb) * sig * (1.0 - sig)).astype(df_ref.dtype)
            dlb_parts.append(jnp.sum(df * (1.0 - sig), axis=0, keepdims=True))
            dq_ref[:, sl] = (dq * (sq * (1.0 + qh * (1.0 - sq)))).astype(dq_ref.dtype)
            di_ref[:, sl] = dv.astype(di_ref.dtype)
        dlb_row = jnp.concatenate(dlb_parts, axis=1)
        dnw_blk = jnp.broadcast_to(dnw_tot, (8, HEAD_DIM))

        @pl.when(first)
        def _():
            dlb_ref[...] = dlb_row
            dnw_ref[...] = dnw_blk

        @pl.when(jnp.logical_not(first))
        def _():
            dlb_ref[...] += dlb_row
            dnw_ref[...] += dnw_blk

    n_hp = HG_HEADS // HG_HEADS_PER_STEP
    outs, cres = _pcall(
        kern, name=name, grid=(n_hp, HG_N_CHUNKS),
        in_specs=[in_blk(0), in_blk(2), in_blk(4), in_blk(6),
                  pl.BlockSpec((2, HG_BLOCK_W), lambda hp, n: (0, hp)),
                  pl.BlockSpec((1, HEAD_DIM), lambda hp, n: (0, 0)),
                  blk,
                  pl.BlockSpec((1, HG_HEADS_PER_STEP, HEAD_DIM, HEAD_DIM), lambda hp, n: (last - n, hp, 0, 0)),
                  blk],
        out_specs=[blk, blk, blk, blk,
                   pl.BlockSpec((1, HG_BLOCK_W), lambda hp, n: (0, hp)),
                   pl.BlockSpec((8, HEAD_DIM), lambda hp, n: (hp, 0))],
        out_shape=[jax.ShapeDtypeStruct((SEQ, HG_WIDTH), BF16)] * 4
        + [jax.ShapeDtypeStruct((1, HG_WIDTH), F32), jax.ShapeDtypeStruct((8 * n_hp, HEAD_DIM), F32)],
        args=(proj, proj, proj, proj, lbw, normw, oraw, states, do_hg),
        scratch_shapes=[pltpu.VMEM((HG_HEADS_PER_STEP, HEAD_DIM, HEAD_DIM), F32)],
        semantics=("parallel", "arbitrary"), carried=carried)
    dqh, dfh, dih, dgh, dlb, dnw = outs
    return (dqh, dfh, dih, dgh, dlb, (dnw[0:1], dnw[8:9])), cres


GATE_BLOCK_W = 512
GATE_A_BLK = (3 * ATT_WIDTH + 4 * HG_WIDTH) // GATE_BLOCK_W
GATE_B_BLK = GATE_A_BLK + D_MODEL // GATE_BLOCK_W


def _gate_specs():
    tr = ROW_BLOCK
    blk = pl.BlockSpec((tr, GATE_BLOCK_W), lambda i, j: (i, j))
    ga = pl.BlockSpec((tr, GATE_BLOCK_W), lambda i, j: (i, GATE_A_BLK + j))
    gb = pl.BlockSpec((tr, GATE_BLOCK_W), lambda i, j: (i, GATE_B_BLK + j))
    return (SEQ // tr, D_MODEL // GATE_BLOCK_W), blk, ga, gb


def _gate_fwd(proj, ya, yb, name):
    grid, blk, ga, gb = _gate_specs()

    def kern(ga_ref, gb_ref, ya_ref, yb_ref, o_ref):
        o_ref[...] = (_sigmoid(ga_ref[...]) * ya_ref[...] + _sigmoid(gb_ref[...]) * yb_ref[...]).astype(o_ref.dtype)

    return pl.pallas_call(
        kern, name=name, grid=grid, in_specs=[ga, gb, blk, blk], out_specs=blk,
        out_shape=jax.ShapeDtypeStruct((SEQ, D_MODEL), BF16),
        compiler_params=_cparams(("parallel", "parallel")),
    )(proj, proj, ya, yb)


def _gate_bwd(proj, ya, yb, dmerged, name, carried=()):
    grid, blk, ga, gb = _gate_specs()

    def kern(ga_ref, gb_ref, ya_ref, yb_ref, dm_ref, dya_ref, dyb_ref, dga_ref, dgb_ref):
        dm = dm_ref[...]
        sa, sb = _sigmoid(ga_ref[...]), _sigmoid(gb_ref[...])
        dya_ref[...] = (dm * sa).astype(dya_ref.dtype)
        dyb_ref[...] = (dm * sb).astype(dyb_ref.dtype)
        dga_ref[...] = (dm * ya_ref[...] * sa * (1.0 - sa)).astype(dga_ref.dtype)
        dgb_ref[...] = (dm * yb_ref[...] * sb * (1.0 - sb)).astype(dgb_ref.dtype)

    return _pcall(
        kern, name=name, grid=grid, in_specs=[ga, gb, blk, blk, blk], out_specs=[blk] * 4,
        out_shape=[jax.ShapeDtypeStruct((SEQ, D_MODEL), BF16)] * 4, args=(proj, proj, ya, yb, dmerged),
        semantics=("parallel", "parallel"), carried=carried)


FF_SHARD = D_FF // N_CHIPS


def _swiglu_fwd(ab, name):
    tr = ROW_BLOCK

    def kern(ab_ref, u_ref):
        a, b = ab_ref[:, :FF_SHARD], ab_ref[:, FF_SHARD:]
        u_ref[...] = (a * _sigmoid(a) * b).astype(u_ref.dtype)

    return pl.pallas_call(
        kern, name=name, grid=(SEQ // tr, N_CHIPS),
        in_specs=[pl.BlockSpec((tr, 2 * FF_SHARD), lambda i, j: (i, j))],
        out_specs=pl.BlockSpec((tr, FF_SHARD), lambda i, j: (i, j)),
        out_shape=jax.ShapeDtypeStruct((SEQ, D_FF), BF16),
        compiler_params=_cparams(("parallel", "parallel")),
    )(ab)


def _swiglu_bwd(ab, du, name, carried=()):
    tr = ROW_BLOCK

    def kern(ab_ref, du_ref, dab_ref):
        a, b = ab_ref[:, :FF_SHARD], ab_ref[:, FF_SHARD:]
        du_ = du_ref[...]
        sg = _sigmoid(a)
        dab_ref[:, :FF_SHARD] = (du_ * b * (sg * (1.0 + a * (1.0 - sg)))).astype(dab_ref.dtype)
        dab_ref[:, FF_SHARD:] = (du_ * (a * sg)).astype(dab_ref.dtype)

    wide = pl.BlockSpec((tr, 2 * FF_SHARD), lambda i, j: (i, j))
    return _pcall(
        kern, name=name, grid=(SEQ // tr, N_CHIPS),
        in_specs=[wide, pl.BlockSpec((tr, FF_SHARD), lambda i, j: (i, j))],
        out_specs=wide, out_shape=jax.ShapeDtypeStruct((SEQ, 2 * D_FF), BF16), args=(ab, du),
        semantics=("parallel", "parallel"), carried=carried)


CROSS_ROWS = 512


def _cross_fwd(qc, kvc, name):
    def kern(q_ref, k_ref, v_ref, o_ref):
        s = _dot_nt(q_ref[...], k_ref[...]) * ATT_SCALE
        m = jnp.max(s, axis=-1, keepdims=True)
        e = jnp.exp(s - m)
        p = e / jnp.sum(e, axis=-1, keepdims=True)
        o_ref[...] = _dot(p, v_ref[...]).astype(o_ref.dtype)

    qblk = pl.BlockSpec((CROSS_ROWS, HEAD_DIM), lambda h, i: (i, h))
    return pl.pallas_call(
        kern, name=name, grid=(CROSS_HEADS, SEQ // CROSS_ROWS),
        in_specs=[qblk, pl.BlockSpec((MEM_LEN, HEAD_DIM), lambda h, i: (0, h)),
                  pl.BlockSpec((MEM_LEN, HEAD_DIM), lambda h, i: (0, CROSS_HEADS + h))],
        out_specs=qblk, out_shape=jax.ShapeDtypeStruct((SEQ, CROSS_WIDTH), BF16),
        compiler_params=_cparams(("parallel", "parallel")),
    )(qc, kvc, kvc)


def _cross_bwd(qc, kvc, doc, name):
    def kern(q_ref, k_ref, v_ref, do_ref, dq_ref, dk_ref, dv_ref):
        q, k, v, do = q_ref[...], k_ref[...], v_ref[...], do_ref[...]
        s = _dot_nt(q, k) * ATT_SCALE
        m = jnp.max(s, axis=-1, keepdims=True)
        e = jnp.exp(s - m)
        p = e / jnp.sum(e, axis=-1, keepdims=True)
        dp = _dot_nt(do, v)
        ds = p * (dp - jnp.sum(dp * p, axis=-1, keepdims=True))
        dq_ref[...] = (_dot(ds, k) * ATT_SCALE).astype(dq_ref.dtype)
        dk = _dot_tn(ds, q) * ATT_SCALE
        dv = _dot_tn(p, do)

        @pl.when(pl.program_id(1) == 0)
        def _():
            dk_ref[...] = dk
            dv_ref[...] = dv

        @pl.when(pl.program_id(1) > 0)
        def _():
            dk_ref[...] += dk
            dv_ref[...] += dv

    qblk = pl.BlockSpec((CROSS_ROWS, HEAD_DIM), lambda h, i: (i, h))
    kblk = pl.BlockSpec((MEM_LEN, HEAD_DIM), lambda h, i: (0, h))
    dq, dk, dv = pl.pallas_call(
        kern, name=name, grid=(CROSS_HEADS, SEQ // CROSS_ROWS),
        in_specs=[qblk, kblk, pl.BlockSpec((MEM_LEN, HEAD_DIM), lambda h, i: (0, CROSS_HEADS + h)), qblk],
        out_specs=[qblk, kblk, kblk],
        out_shape=[jax.ShapeDtypeStruct((SEQ, CROSS_WIDTH), BF16),
                   jax.ShapeDtypeStruct((MEM_LEN, CROSS_WIDTH), F32),
                   jax.ShapeDtypeStruct((MEM_LEN, CROSS_WIDTH), F32)],
        compiler_params=_cparams(("parallel", "arbitrary")),
    )(qc, kvc, kvc, doc)
    return dq, jnp.concatenate([dk, dv], axis=1)


FULL_SPECS = {
    "w_in": ("col", D_MODEL, IN_WIDTH),
    "w_branch_a": ("col", ATT_OUT, D_MODEL),
    "w_branch_b": ("col", HG_WIDTH, D_MODEL),
    "w_out": ("row", D_MODEL, D_MODEL),
    "wq_cross": ("row", D_MODEL, CROSS_WIDTH),
    "wkv_cross": ("row", D_MODEL, 2 * CROSS_WIDTH),
    "wo_cross": ("col", CROSS_WIDTH, D_MODEL),
    "w13": ("col", D_MODEL, 2 * D_FF),
    "w2": ("row", D_FF, D_MODEL),
}
WEIGHT_PLACE = {
    "w_in": ("w_in", 0), "w_branch_a": ("w_branch_a", 0), "w_branch_b": ("w_branch_b", 0),
    "w_out": ("w_out", 0), "wq_cross": ("wq_cross", 0), "wkv_cross": ("wkv_cross", 0),
    "wo_cross": ("wo_cross", 0), "w1": ("w13", 0), "w3": ("w13", FF_SHARD), "w2": ("w2", 0),
}
BIG_WEIGHTS = tuple(WEIGHT_PLACE)
EW_BLOCK_ELEMS = 256 * 1024


def _position():
    return lax.axis_index("x"), lax.axis_index("y"), lax.axis_index("c")


def _other_chips(x, y):
    return [(1 - x, y), (x, 1 - y), (1 - x, 1 - y)]


def _half(ref, kind, h):
    r, c = ref.shape
    if kind == "col":
        return ref.at[pl.ds(h * (r // 2), r // 2), :]
    return ref.at[:, pl.ds(h * (c // 2), c // 2)]


def _shard_of(ref, kind, start, size):
    return ref.at[:, pl.ds(start, size)] if kind == "col" else ref.at[pl.ds(start, size), :]


def _rows_of(ref, r0, nrows):
    return ref if nrows is None else ref.at[pl.ds(r0, nrows), :]


def _half_shape(kind, rows, cols):
    return (rows // 2, cols) if kind == "col" else (rows, cols // 2)


def _slot_shape(spec):
    kind, rows, cols = spec
    hr, hc = _half_shape(kind, rows, cols)
    return (hr, hc // N_CHIPS) if kind == "col" else (hr // N_CHIPS, hc)


def _remote(src, dst, send_sem, recv_sem, device):
    return pltpu.make_async_remote_copy(src_ref=src, dst_ref=dst, send_sem=send_sem, recv_sem=recv_sem,
                                        device_id=device, device_id_type=MESH)


def _gather_comm(fulls, jobs, specs):
    def piece(refs, job, chip, c):
        f, r0, nr = job
        kind, rows, cols = specs[f]
        stride = (cols if kind == "col" else rows) // N_CHIPS
        return _rows_of(_half(_shard_of(refs[f], kind, chip * stride, stride), kind, c), r0, nr)

    def rect(refs, job, h):
        f, r0, nr = job
        assert nr is None or specs[f][0] == "col"
        return _rows_of(_half(refs[f], specs[f][0], h), r0, nr)

    def start(refs, ss, rs):
        x, y, c = _position()
        j = 2 * x + y
        for q, job in enumerate(jobs):
            for p, (px, py) in enumerate(_other_chips(x, y)):
                _remote(piece(refs, job, j, c), piece(refs, job, j, c), ss.at[4 * q + p], rs.at[4 * q + p],
                        (px, py, c)).start()

    def mid(refs, ss, rs):
        x, y, c = _position()
        j = 2 * x + y
        for q, job in enumerate(jobs):
            for p, (px, py) in enumerate(_other_chips(x, y)):
                _remote(piece(refs, job, j, c), piece(refs, job, 2 * px + py, c), ss.at[4 * q + p],
                        rs.at[4 * q + p], (px, py, c)).wait_recv()
            _remote(rect(refs, job, c), rect(refs, job, c), ss.at[4 * q + 3], rs.at[4 * q + 3],
                    (x, y, 1 - c)).start()

    def finish(refs, ss, rs):
        x, y, c = _position()
        j = 2 * x + y
        for q, job in enumerate(jobs):
            _remote(rect(refs, job, 1 - c), rect(refs, job, 1 - c), ss.at[4 * q + 3], rs.at[4 * q + 3],
                    (x, y, 1 - c)).wait_recv()
        for q, job in enumerate(jobs):
            for p, (px, py) in enumerate(_other_chips(x, y)):
                _remote(piece(refs, job, j, c), piece(refs, job, j, c), ss.at[4 * q + p], rs.at[4 * q + p],
                        (px, py, c)).wait_send()
            _remote(rect(refs, job, c), rect(refs, job, c), ss.at[4 * q + 3], rs.at[4 * q + 3],
                    (x, y, 1 - c)).wait_send()

    names = list(dict.fromkeys(job[0] for job in jobs))
    return _Carried({f: fulls[f] for f in names}, {}, 4 * len(jobs), start, finish, mid)


def _pairx_comm(grads, names, specs):
    def copies(refs, ss, rs):
        x, y, c = _position()
        return [_remote(_half(refs[("g", f)], specs[f][0], 1 - c), refs[("r", f)], ss.at[i], rs.at[i], (x, y, 1 - c))
                for i, f in enumerate(names)]

    def start(refs, ss, rs):
        for cp in copies(refs, ss, rs):
            cp.start()

    def finish(refs, ss, rs):
        for cp in copies(refs, ss, rs):
            cp.wait_recv()
        for cp in copies(refs, ss, rs):
            cp.wait_send()

    fresh = {("r", f): jax.ShapeDtypeStruct(_half_shape(*specs[f]), BF16) for f in names}
    return _Carried({("g", f): grads[f] for f in names}, fresh, len(names), start, finish)


def _chipx_comm(pair_sums, slots, jobs, specs):
    def copies(refs, ss, rs):
        x, y, c = _position()
        out = []
        for q, (f, r0, nr) in enumerate(jobs):
            kind = specs[f][0]
            assert nr is None or kind == "col"
            width = _slot_shape(specs[f])[1 if kind == "col" else 0]
            for p, (px, py) in enumerate(_other_chips(x, y)):
                src = _rows_of(_shard_of(refs[("p", f)], kind, (2 * px + py) * width, width), r0, nr)
                dst = _rows_of(refs[("s", f)].at[p], r0, nr)
                out.append(_remote(src, dst, ss.at[3 * q + p], rs.at[3 * q + p], (px, py, c)))
        return out

    def start(refs, ss, rs):
        for cp in copies(refs, ss, rs):
            cp.start()

    def finish(refs, ss, rs):
        for cp in copies(refs, ss, rs):
            cp.wait_recv()
        for cp in copies(refs, ss, rs):
            cp.wait_send()

    names = list(dict.fromkeys(job[0] for job in jobs))
    arrays = {("p", f): pair_sums[f] for f in names}
    arrays.update({("s", f): slots[f] for f in names})
    return _Carried(arrays, {}, 3 * len(jobs), start, finish)


def _share_comm(grads, wnames, specs, place):
    def start(refs, ss, rs):
        x, y, c = _position()
        for i, w in enumerate(wnames):
            kind = specs[place[w][0]][0]
            _remote(_half(refs[w], kind, c), _half(refs[w], kind, c), ss.at[i], rs.at[i], (x, y, 1 - c)).start()

    def finish(refs, ss, rs):
        x, y, c = _position()
        for i, w in enumerate(wnames):
            kind = specs[place[w][0]][0]
            _remote(_half(refs[w], kind, 1 - c), _half(refs[w], kind, 1 - c), ss.at[i], rs.at[i],
                    (x, y, 1 - c)).wait_recv()
        for i, w in enumerate(wnames):
            kind = specs[place[w][0]][0]
            _remote(_half(refs[w], kind, c), _half(refs[w], kind, c), ss.at[i], rs.at[i], (x, y, 1 - c)).wait_send()

    return _Carried({w: grads[w] for w in wnames}, {}, len(wnames), start, finish)


def _gather_rows(v, name="gather_small"):
    shape = v.shape

    def body(v_ref, out_ref, send_sem, recv_sem, loc_sem):
        x, y, c = _position()
        me = 4 * x + 2 * y + c
        flips = [(fx, fy, fc) for fx in (0, 1) for fy in (0, 1) for fc in (0, 1)][1:]

        def peer(fl):
            return tuple(1 - a if f else a for a, f in zip((x, y, c), fl))

        loc = pltpu.make_async_copy(v_ref, out_ref.at[me], loc_sem)
        loc.start()
        sends = []
        for i, fl in enumerate(flips):
            cp = _remote(v_ref, out_ref.at[me], send_sem.at[i], recv_sem.at[i], peer(fl))
            cp.start()
            sends.append(cp)
        for i, fl in enumerate(flips):
            px, py, pc = peer(fl)
            _remote(v_ref, out_ref.at[4 * px + 2 * py + pc], send_sem.at[i], recv_sem.at[i], peer(fl)).wait_recv()
        for cp in sends:
            cp.wait_send()
        loc.wait()

    return pl.pallas_call(
        body, name=name, in_specs=[ANY], out_specs=ANY,
        out_shape=jax.ShapeDtypeStruct((N_DEV,) + shape, F32),
        scratch_shapes=[pltpu.SemaphoreType.DMA((N_DEV - 1,)), pltpu.SemaphoreType.DMA((N_DEV - 1,)),
                        pltpu.SemaphoreType.DMA],
    )(v)


def _ew_block(rows, cols):
    tc = cols if cols <= 4096 else _div(cols, 2048, LANE)
    tr = _div(rows, max(16, EW_BLOCK_ELEMS // tc), 16)
    return tr, tc


def _mesh_scalars():
    x, y, c = _position()
    return jnp.stack([c, 2 * x + y]).astype(jnp.int32)


def _grid_spec(grid, in_specs, out_specs):
    return pltpu.PrefetchScalarGridSpec(num_scalar_prefetch=1, grid=grid, in_specs=in_specs, out_specs=out_specs)


def _cast_into_full(parts, fname, pos, specs, place, name):
    kind, rows, cols = specs[fname]
    ws = [w for w in place if place[w][0] == fname]
    if kind == "col":
        stride = cols // N_CHIPS
        hr = rows // 2
        tr = _div(hr, max(16, EW_BLOCK_ELEMS // stride), 16)
        nrb = hr // tr
        in_specs = [pl.BlockSpec((tr, parts[w].shape[1]), lambda i, pos_ref: (i + pos_ref[0] * nrb, 0)) for w in ws]
        out_spec = pl.BlockSpec((tr, stride), lambda i, pos_ref: (i + pos_ref[0] * nrb, pos_ref[1]))
    else:
        stride = rows // N_CHIPS
        hc = cols // 2
        tr = _div(stride, max(16, EW_BLOCK_ELEMS // hc), 16)
        nrb = stride // tr
        in_specs = [pl.BlockSpec((tr, hc), lambda i, pos_ref: (i, pos_ref[0])) for w in ws]
        out_spec = pl.BlockSpec((tr, hc), lambda i, pos_ref: (i + pos_ref[1] * nrb, pos_ref[0]))

    def kern(pos_ref, *refs):
        o_ref = refs[-1]
        for w, r in zip(ws, refs[:-1]):
            off = place[w][1] if kind == "col" else 0
            o_ref[:, off:off + r.shape[1]] = r[...].astype(o_ref.dtype)

    return pl.pallas_call(
        kern, name=name, grid_spec=_grid_spec((nrb,), in_specs, out_spec),
        out_shape=jax.ShapeDtypeStruct((rows, cols), BF16),
        compiler_params=_cparams(("parallel",)),
    )(pos, *[parts[w] for w in ws])


def _pair_sum(grad, recv, pos, spec, name):
    kind, rows, cols = spec
    hr, hc = _half_shape(kind, rows, cols)
    tr, tc = _ew_block(hr, hc)
    nrb, ncb = hr // tr, hc // tc
    blk = pl.BlockSpec((tr, tc), lambda i, jj, pos_ref: (i, jj))
    if kind == "col":
        mine = pl.BlockSpec((tr, tc), lambda i, jj, pos_ref: (i + pos_ref[0] * nrb, jj))
    else:
        mine = pl.BlockSpec((tr, tc), lambda i, jj, pos_ref: (i, jj + pos_ref[0] * ncb))

    def kern(pos_ref, g_ref, r_ref, o_ref, slots_ref):
        o_ref[...] = (g_ref[...].astype(F32) + r_ref[...].astype(F32)).astype(o_ref.dtype)

    return pl.pallas_call(
        kern, name=name, grid_spec=_grid_spec((nrb, ncb), [mine, blk], [blk, ANY]),
        out_shape=[jax.ShapeDtypeStruct((hr, hc), BF16),
                   jax.ShapeDtypeStruct((N_CHIPS - 1,) + _slot_shape(spec), BF16)],
        compiler_params=_cparams(("parallel", "parallel")),
    )(pos, grad, recv)


def _chip_sum(pair_sum, slots, pos, fname, shard_shapes, specs, place, name):
    kind, rows, cols = specs[fname]
    sr, sc = _slot_shape(specs[fname])
    ws = [w for w in place if place[w][0] == fname]
    n_slots = N_CHIPS - 1
    tr = _div(sr, max(16, EW_BLOCK_ELEMS // sc), 16)
    nrb = sr // tr
    slot = pl.BlockSpec((n_slots, tr, sc), lambda i, pos_ref: (0, i, 0))
    if kind == "col":
        own = pl.BlockSpec((tr, sc), lambda i, pos_ref: (i, pos_ref[1]))
        out_specs = [pl.BlockSpec((tr, shard_shapes[w][1]), lambda i, pos_ref: (i + pos_ref[0] * nrb, 0)) for w in ws]
    else:
        own = pl.BlockSpec((tr, sc), lambda i, pos_ref: (i + pos_ref[1] * nrb, 0))
        out_specs = [pl.BlockSpec((tr, sc), lambda i, pos_ref: (i, pos_ref[0])) for w in ws]

    def kern(pos_ref, own_ref, slot_ref, *out_refs):
        tot = own_ref[...].astype(F32)
        for s in range(n_slots):
            tot = tot + slot_ref[s].astype(F32)
        for w, o_ref in zip(ws, out_refs):
            off = place[w][1] if kind == "col" else 0
            o_ref[...] = tot[:, off:off + o_ref.shape[1]]

    outs = pl.pallas_call(
        kern, name=name, grid_spec=_grid_spec((nrb,), [own, slot], out_specs),
        out_shape=[jax.ShapeDtypeStruct(shard_shapes[w], F32) for w in ws],
        compiler_params=_cparams(("parallel",)),
    )(pos, pair_sum, slots)
    return dict(zip(ws, outs))


def _adam_math(w, g, m, v):
    m2 = ADAM_B1 * m + (1.0 - ADAM_B1) * g
    v2 = ADAM_B2 * v + (1.0 - ADAM_B2) * (g * g)
    m_hat = m2 / (1.0 - ADAM_B1 ** ADAM_STEP)
    v_hat = v2 / (1.0 - ADAM_B2 ** ADAM_STEP)
    delta = -ADAM_LR * (m_hat / (jnp.sqrt(v_hat) + ADAM_EPS) + ADAM_WD * w)
    return delta, m2, v2


def _adamw(w, g, m, v, name, carried=()):
    rows, cols = w.shape
    tr, tc = _ew_block(rows, cols)

    def kern(w_ref, g_ref, m_ref, v_ref, d_ref, m2_ref, v2_ref):
        d_ref[...], m2_ref[...], v2_ref[...] = _adam_math(w_ref[...], g_ref[...], m_ref[...], v_ref[...])

    blk = pl.BlockSpec((tr, tc), lambda i, j: (i, j))
    return _pcall(
        kern, name=name, grid=(rows // tr, cols // tc), in_specs=[blk] * 4, out_specs=[blk] * 3,
        out_shape=[jax.ShapeDtypeStruct((rows, cols), F32)] * 3, args=(w, g, m, v),
        semantics=("parallel", "parallel"), carried=carried)


SMALL_ROWS = ("ln_mix_w", "ln_cross_w", "ln_mem_w", "ln_ffn_w", "ln_final_w")
ROW_HG_NORM, ROW_LB0, ROW_LB1 = 5, 6, 7
LOSS_LANE0 = HEAD_DIM


def _pack_small(vals):
    rows = [vals[n].reshape(1, D_MODEL) for n in SMALL_ROWS]
    pad = lambda a: jnp.pad(a, ((0, 0), (0, D_MODEL - a.shape[1])))
    rows.append(pad(vals["hg_norm_w"].reshape(1, HEAD_DIM)))
    rows.append(pad(vals["hg_lower_bounds"].reshape(2, HG_WIDTH)))
    return jnp.concatenate(rows, axis=0)


def _small_update(gathered, w, m, v, name="small_update"):
    def kern(g_ref, w_ref, m_ref, v_ref, grad_ref, d_ref, m2_ref, v2_ref, loss_ref):
        tot = g_ref[0]
        for s in range(1, N_DEV):
            tot = tot + g_ref[s]
        wv = w_ref[...]
        row = lax.broadcasted_iota(jnp.int32, (8, D_MODEL), 0)
        lane = lax.broadcasted_iota(jnp.int32, (8, D_MODEL), 1)
        l0, l1 = wv[ROW_LB0:ROW_LB0 + 1], wv[ROW_LB1:ROW_LB1 + 1]
        mx = jnp.maximum(l0, l1)
        e0, e1 = jnp.exp(l0 - mx), jnp.exp(l1 - mx)
        p0 = e0 / (e0 + e1)
        dlog = tot[ROW_LB0:ROW_LB0 + 1] * p0 * (1.0 - p0)
        tot = jnp.where(row == ROW_HG_NORM, tot + tot[ROW_LB1:ROW_LB1 + 1], tot)
        grad = jnp.where(row == ROW_LB0, dlog, jnp.where(row == ROW_LB1, -dlog, tot))
        grad = jnp.where((row == ROW_HG_NORM) & (lane >= HEAD_DIM), 0.0, grad)
        grad = jnp.where((row >= ROW_LB0) & (lane >= HG_WIDTH), 0.0, grad)
        grad_ref[...] = grad
        d_ref[...], m2_ref[...], v2_ref[...] = _adam_math(wv, grad, m_ref[...], v_ref[...])
        loss_ref[...] = tot[ROW_HG_NORM:ROW_HG_NORM + 1, LOSS_LANE0:LOSS_LANE0 + LANE]

    full = pl.BlockSpec((8, D_MODEL), lambda: (0, 0))
    return pl.pallas_call(
        kern, name=name,
        in_specs=[pl.BlockSpec((N_DEV, 8, D_MODEL), lambda: (0, 0, 0)), full, full, full],
        out_specs=[full, full, full, full, pl.BlockSpec((1, LANE), lambda: (0, 0))],
        out_shape=[jax.ShapeDtypeStruct((8, D_MODEL), F32)] * 4 + [jax.ShapeDtypeStruct((1, LANE), F32)],
        compiler_params=_cparams(),
    )(gathered, w, m, v)


def _unpack_small(p, shapes):
    out = {n: p[i].reshape(shapes[n]) for i, n in enumerate(SMALL_ROWS)}
    out["hg_norm_w"] = p[ROW_HG_NORM, :HEAD_DIM].reshape(shapes["hg_norm_w"])
    out["hg_lower_bounds"] = p[ROW_LB0:ROW_LB1 + 1, :HG_WIDTH].reshape(shapes["hg_lower_bounds"])
    return out


WHOLE = lambda f: (f, 0, None)
MID_MATRICES = ("w_branch_a", "w_branch_b", "w_out", "wq_cross", "wkv_cross", "wo_cross")
MID_WEIGHTS = MID_MATRICES
CARRY = {
    "mm_proj": [("gather", [WHOLE(f) for f in MID_MATRICES] + [("w13", 0, 448)])],
    "hgrn_fwd": [("gather", [("w13", 448, 416)])],
    "mm_out": [("gather", [("w13", 864, 160)])],
    "mm_w13": [("gather", [WHOLE("w2")])],
    "swiglu_bwd": [("pairx", ["w2"])],
    "mm_dw13": [("chipx", [WHOLE("w2")])],
    "mm_dhf": [("pairx", ["w13"])],
    "mm_dmerged": [("chipx", [("w13", 0, 144)])],
    "mm_dwout": [("chipx", [("w13", 144, 144)])],
    "gate_bwd": [("chipx", [("w13", 288, 176)])],
    "hgrn_bwd": [("chipx", [("w13", 464, 560)])],
    "attn_merge_bwd": [("pairx", list(MID_MATRICES))],
    "mm_dwin": [("chipx", [WHOLE(f) for f in MID_MATRICES])],
    "mm_dh": [("chipx", [("w_in", 0, 656)]), ("share", ["w1", "w3", "w2"] + list(MID_WEIGHTS))],
    "adamw_w1": [("chipx", [("w_in", 656, 112)])],
    "adamw_w3": [("chipx", [("w_in", 768, 112)])],
    "adamw_w2": [("chipx", [("w_in", 880, 144)])],
}
AFTER = {
    "swiglu_bwd": [("pair_sum", "w2")],
    "mm_dw13": [("chip_sum", "w2")],
    "mm_dhf": [("pair_sum", "w13")],
    "hgrn_bwd": [("chip_sum", "w13")],
    "attn_merge_bwd": [("pair_sum", f) for f in MID_MATRICES],
    "mm_dwin": [("chip_sum", f) for f in MID_MATRICES] + [("run", ("pairx", ["w_in"]), "rs_pair_exchange_w_in"),
                                                         ("pair_sum", "w_in")],
    "adamw_w2": [("chip_sum", "w_in"), ("run", ("share", ["w_in"]), "rs_sibling_share_w_in")],
}


class _Net:
    def __init__(self, full, pos=None, shard_shapes=None, comm=True, specs=FULL_SPECS, place=WEIGHT_PLACE):
        self.full, self.pos, self.shard_shapes, self.comm = dict(full), pos, shard_shapes, comm
        self.specs, self.place = specs, place
        self.gw, self.recv, self.psum, self.slots, self.grads = {}, {}, {}, {}, {}

    def _make(self, kind, arg):
        if kind == "gather":
            return _gather_comm(self.full, arg, self.specs)
        if kind == "pairx":
            return _pairx_comm(self.gw, arg, self.specs)
        if kind == "chipx":
            return _chipx_comm(self.psum, self.slots, arg, self.specs)
        assert kind == "share"
        return _share_comm(self.grads, arg, self.specs, self.place)

    def _store(self, kind, res):
        if kind == "gather":
            self.full.update(res)
        elif kind == "pairx":
            for (tag, f), a in res.items():
                (self.gw if tag == "g" else self.recv)[f] = a
        elif kind == "chipx":
            for (tag, f), a in res.items():
                (self.psum if tag == "p" else self.slots)[f] = a
        else:
            self.grads.update(res)

    def run_comm(self, item, name):
        kind, arg = item
        self._store(kind, _run_comm([self._make(kind, arg)], name)[0])

    def call(self, fn, name, *args, grad_of=None, **kw):
        items = CARRY.get(name, []) if self.comm else []
        out, res = fn(*args, name=name, carried=[self._make(k, a) for k, a in items], **kw)
        if grad_of is not None:
            self.gw[grad_of] = out
        for (kind, _), r in zip(items, res):
            self._store(kind, r)
        for step in (AFTER.get(name, []) if self.comm else []):
            if step[0] == "pair_sum":
                f = step[1]
                self.psum[f], self.slots[f] = _pair_sum(self.gw[f], self.recv[f], self.pos, self.specs[f],
                                                        f"rs_pair_sum_{f}")
            elif step[0] == "chip_sum":
                f = step[1]
                self.grads.update(_chip_sum(self.psum[f], self.slots[f], self.pos, f, self.shard_shapes,
                                            self.specs, self.place, f"rs_chip_sum_{f}"))
            else:
                self.run_comm(step[1], step[2])
        return out


def _local_step(net, x, mem, target, small):
    full, call = net.full, net.call
    h = _rms_fwd(x, small["ln_mix_w"], "rms_mix")
    proj = call(_mm, "mm_proj", h, full["w_in"], mode="nn", out_dtype=F32)
    att = [_attn_fwd(proj, g, f"attn_fwd_g{g}") for g in range(3)]
    outs, lses = [a[0] for a in att], [a[1] for a in att]
    o_att = _attn_merge_fwd(outs, lses, "attn_merge")
    oraw, o_hg, states = call(_hg_fwd, "hgrn_fwd", proj, small["hg_lower_bounds"], small["hg_norm_w"])
    ya = call(_mm, "mm_branch_a", o_att, full["w_branch_a"], mode="nn", out_dtype=F32)
    yb = call(_mm, "mm_branch_b", o_hg, full["w_branch_b"], mode="nn", out_dtype=F32)
    merged = _gate_fwd(proj, ya, yb, "gate_fwd")
    x1 = call(_mm, "mm_out", merged, full["w_out"], mode="nn", out_dtype=F32, res=x)

    hc = _rms_fwd(x1, small["ln_cross_w"], "rms_cross")
    mn = _rms_fwd(mem, small["ln_mem_w"], "rms_mem")
    qc = call(_mm, "mm_q", hc, full["wq_cross"], mode="nn", out_dtype=F32)
    kvc = call(_mm, "mm_kv", mn, full["wkv_cross"], mode="nn", out_dtype=F32)
    oc = _cross_fwd(qc, kvc, "cross_fwd")
    x2 = call(_mm, "mm_o", oc, full["wo_cross"], mode="nn", out_dtype=F32, res=x1)

    hf = _rms_fwd(x2, small["ln_ffn_w"], "rms_ffn")
    ab = call(_mm, "mm_w13", hf, full["w13"], mode="nn", out_dtype=F32)
    u = _swiglu_fwd(ab, "swiglu_fwd")
    x3 = call(_mm, "mm_w2", u, full["w2"], mode="nn", out_dtype=F32, res=x2)

    dx3, dg_final, loss = _loss_head(x3, small["ln_final_w"], target, "loss_head")

    gs = {"ln_final_w": dg_final}
    du = call(_mm, "mm_du", dx3, full["w2"], mode="nt", out_dtype=F32)
    call(_mm, "mm_dw2", u, dx3, mode="tn", out_dtype=BF16, grad_of="w2")
    dab = call(_swiglu_bwd, "swiglu_bwd", ab, du)
    call(_mm, "mm_dw13", hf, dab, mode="tn", out_dtype=BF16, grad_of="w13")
    dhf = call(_mm, "mm_dhf", dab, full["w13"], mode="nt", out_dtype=F32)
    dx2, gs["ln_ffn_w"] = _rms_bwd(x2, small["ln_ffn_w"], dhf, dx3, "rms_ffn_bwd")
    doc = call(_mm, "mm_doc", dx2, full["wo_cross"], mode="nt", out_dtype=BF16)
    call(_mm, "mm_dwo", oc, dx2, mode="tn", out_dtype=BF16, grad_of="wo_cross")
    dqc, dkvc = _cross_bwd(qc, kvc, doc, "cross_bwd")
    call(_mm, "mm_dwq", hc, dqc, mode="tn", out_dtype=BF16, grad_of="wq_cross")
    dhc = call(_mm, "mm_dhc", dqc, full["wq_cross"], mode="nt", out_dtype=F32)
    call(_mm, "mm_dwkv", mn, dkvc, mode="tn", out_dtype=BF16, grad_of="wkv_cross")
    dmn = call(_mm, "mm_dmn", dkvc, full["wkv_cross"], mode="nt", out_dtype=F32)
    _, gs["ln_mem_w"] = _rms_bwd(mem, small["ln_mem_w"], dmn, None, "rms_mem_bwd")
    dx1, gs["ln_cross_w"] = _rms_bwd(x1, small["ln_cross_w"], dhc, dx2, "rms_cross_bwd")
    dmerged = call(_mm, "mm_dmerged", dx1, full["w_out"], mode="nt", out_dtype=F32)
    call(_mm, "mm_dwout", merged, dx1, mode="tn", out_dtype=BF16, grad_of="w_out")
    dya, dyb, dga, dgb = call(_gate_bwd, "gate_bwd", proj, ya, yb, dmerged)
    call(_mm, "mm_dwa", o_att, dya, mode="tn", out_dtype=BF16, grad_of="w_branch_a")
    do_att = call(_mm, "mm_doatt", dya, full["w_branch_a"], mode="nt", out_dtype=F32)
    call(_mm, "mm_dwb", o_hg, dyb, mode="tn", out_dtype=BF16, grad_of="w_branch_b")
    do_hg = call(_mm, "mm_dohg", dyb, full["w_branch_b"], mode="nt", out_dtype=F32)
    dqh, dfh, dih, dgh, dlb, gs["hg_norm_w"] = call(
        _hg_bwd, "hgrn_bwd", proj, small["hg_lower_bounds"], small["hg_norm_w"], oraw, states, do_hg)
    gs["hg_lb"] = dlb
    do_gs, dl_gs = call(_attn_merge_bwd, "attn_merge_bwd", outs, lses, do_att)
    dqs, dks, dvs = zip(*[_attn_bwd(proj, g, lses[g], do_gs[g], dl_gs[g], f"attn_bwd_g{g}") for g in range(3)])
    dproj = jnp.concatenate([*dqs, *dks, *dvs, dqh, dfh, dih, dgh, dga, dgb], axis=1)
    call(_mm, "mm_dwin", h, dproj, mode="tn", out_dtype=BF16, grad_of="w_in")
    dh = call(_mm, "mm_dh", dproj, full["w_in"], mode="nt", out_dtype=F32)
    dx, gs["ln_mix_w"] = _rms_bwd(x, small["ln_mix_w"], dh, dx1, "rms_mix_bwd")
    return loss, dx, gs


WEIGHT_ORDER = ("ln_mix_w", "w_in", "hg_norm_w", "hg_lower_bounds", "w_branch_a", "w_branch_b", "w_out",
                "ln_cross_w", "ln_mem_w", "wq_cross", "wkv_cross", "wo_cross", "ln_ffn_w", "w1", "w3", "w2",
                "ln_final_w")
ADAM_ORDER = ("w1", "w3", "w2", "w_in") + MID_WEIGHTS


def kernel(x, mem, ln_mix_w, w_in, hg_norm_w, hg_lower_bounds, w_branch_a, w_branch_b, w_out, ln_cross_w, ln_mem_w, wq_cross, wkv_cross, wo_cross, ln_ffn_w, w1, w3, w2, ln_final_w, loss_target, m_ln_mix_w, m_w_in, m_hg_norm_w, m_hg_lower_bounds, m_w_branch_a, m_w_branch_b, m_w_out, m_ln_cross_w, m_ln_mem_w, m_wq_cross, m_wkv_cross, m_wo_cross, m_ln_ffn_w, m_w1, m_w3, m_w2, m_ln_final_w, v_ln_mix_w, v_w_in, v_hg_norm_w, v_hg_lower_bounds, v_w_branch_a, v_w_branch_b, v_w_out, v_ln_cross_w, v_ln_mem_w, v_wq_cross, v_wkv_cross, v_wo_cross, v_ln_ffn_w, v_w1, v_w3, v_w2, v_ln_final_w):
    args = dict(locals())
    w = {n: args[n] for n in WEIGHT_ORDER}
    m = {n: args["m_" + n] for n in WEIGHT_ORDER}
    v = {n: args["v_" + n] for n in WEIGHT_ORDER}
    shapes = {n: w[n].shape for n in WEIGHT_ORDER}
    mat = lambda a: a.reshape(a.shape[-2:])
    shard_shapes = {n: shapes[n][-2:] for n in BIG_WEIGHTS}

    pos = _mesh_scalars()
    fulls = {f: _cast_into_full({n: mat(w[n]) for n in BIG_WEIGHTS if WEIGHT_PLACE[n][0] == f}, f, pos,
                                FULL_SPECS, WEIGHT_PLACE, f"cast_{f}") for f in FULL_SPECS}
    net = _Net(fulls, pos, shard_shapes)
    net.run_comm(("gather", [WHOLE("w_in")]), "gather_w_in")
    small = {n: w[n].reshape(1, -1) for n in SMALL_ROWS}
    small["hg_norm_w"] = w["hg_norm_w"].reshape(1, HEAD_DIM)
    small["hg_lower_bounds"] = w["hg_lower_bounds"]
    loss, dx, gs = _local_step(net, x.reshape(SEQ, D_MODEL), mem.reshape(MEM_LEN, D_MODEL),
                               loss_target.reshape(SEQ, D_MODEL), small)

    out_g, out_d, out_m, out_v = {}, {}, {}, {}
    for n in ADAM_ORDER:
        out_d[n], out_m[n], out_v[n] = net.call(_adamw, f"adamw_{n}", mat(w[n]), net.grads[n], mat(m[n]), mat(v[n]))
        out_g[n] = net.grads[n]

    pad = lambda a: jnp.pad(a, ((0, 0), (0, D_MODEL - a.shape[1])))
    part = jnp.concatenate(
        [gs[n] for n in SMALL_ROWS]
        + [pad(jnp.concatenate([gs["hg_norm_w"][0], loss], axis=1)), pad(gs["hg_lb"]),
           pad(gs["hg_norm_w"][1])], axis=0)
    sg, sd, sm, sv, loss_tot = _small_update(_gather_rows(part), _pack_small(w), _pack_small(m), _pack_small(v))
    for dst, packed in ((out_g, sg), (out_d, sd), (out_m, sm), (out_v, sv)):
        dst.update(_unpack_small(packed, shapes))

    result = [loss_tot[0, 0], dx.reshape(x.shape)]
    for group in (out_g, out_d, out_m, out_v):
        result += [group[n].reshape(shapes[n]) for n in WEIGHT_ORDER]
    return tuple(result)
```

```python
import math

import jax
import jax.numpy as jnp
from jax import lax
from jax.experimental import pallas as pl
from jax.experimental.pallas import tpu as pltpu

F32 = jnp.float32
BF16 = jnp.bfloat16
MESH = pl.DeviceIdType.MESH

D_MODEL = 2048
SEQ = 2048
HEAD_DIM = 128
MEM_LEN = 256
ATT_GROUPS = ((128, 1), (512, 4), (2048, 16))
ATT_HEADS = 4
ATT_WIDTH = 3 * ATT_HEADS * HEAD_DIM
ATT_OUT = ATT_HEADS * HEAD_DIM
ATT_BLOCK = 128
HG_HEADS = 8
HG_WIDTH = HG_HEADS * HEAD_DIM
HG_CHUNK = 64
IN_WIDTH = 3 * ATT_WIDTH + 4 * HG_WIDTH + 2 * D_MODEL
CROSS_HEADS = 4
CROSS_WIDTH = CROSS_HEADS * HEAD_DIM
D_FF = 5632
RMS_EPS = 1e-6
ADAM_LR = 0.001
ADAM_B1 = 0.9
ADAM_B2 = 0.999
ADAM_EPS = 1e-08
ADAM_WD = 0.01
ADAM_STEP = 10
N_CHIPS = 4
N_DEV = 8

VMEM_LIMIT_BYTES = 56 * 1024 * 1024
LANE = 128
MXU_WIDTH = 256
MM_TILE_CAP = 1536
TRANSPOSE_CHUNK = 512
ANY = pl.BlockSpec(memory_space=pl.ANY)


def _cparams(sem=None):
    return pltpu.CompilerParams(dimension_semantics=sem, vmem_limit_bytes=VMEM_LIMIT_BYTES)


def _div(n, cap, mult):
    best = None
    for d in range(mult, min(n, cap) + 1, mult):
        if n % d == 0:
            best = d
    assert best is not None, (n, cap, mult)
    return best


def _sigmoid(x):
    return 1.0 / (1.0 + jnp.exp(-x))


def _dot(a, b):
    return jnp.dot(a.astype(BF16), b.astype(BF16), preferred_element_type=F32)


def _dot_nt(a, b):
    return lax.dot_general(a.astype(BF16), b.astype(BF16), (((1,), (1,)), ((), ())),
                           preferred_element_type=F32)


def _dot_tn(a, b):
    return jnp.dot(a.astype(F32).T.astype(BF16), b.astype(BF16), preferred_element_type=F32)


def _dot_exact(a, b):
    return jnp.dot(a, b, precision=lax.Precision.HIGHEST, preferred_element_type=F32)


class _Carried:
    def __init__(self, arrays, fresh, n_sems, start, finish, mid=None):
        self.arrays, self.fresh, self.n_sems = arrays, fresh, n_sems
        self.start, self.mid, self.finish = start, mid, finish


def _carried_layout(carried):
    akeys = [(ci, k) for ci, cm in enumerate(carried) for k in cm.arrays]
    fkeys = [(ci, k) for ci, cm in enumerate(carried) for k in cm.fresh]
    arrays = [carried[ci].arrays[k] for ci, k in akeys]
    shapes = [jax.ShapeDtypeStruct(a.shape, a.dtype) for a in arrays] + [carried[ci].fresh[k] for ci, k in fkeys]
    sems = []
    for cm in carried:
        sems += [pltpu.SemaphoreType.DMA((cm.n_sems,)), pltpu.SemaphoreType.DMA((cm.n_sems,))]
    return akeys, fkeys, arrays, shapes, sems


def _carried_results(carried, akeys, fkeys, outs):
    res = [dict() for _ in carried]
    for (ci, k), o in zip(akeys + fkeys, outs):
        res[ci][k] = o
    return res


def _pcall(kern, *, name, grid, in_specs, out_specs, out_shape, args, scratch_shapes=(), semantics=None,
           carried=()):
    carried = list(carried)
    single = not isinstance(out_shape, (list, tuple))
    out_specs = [out_specs] if single else list(out_specs)
    out_shape = [out_shape] if single else list(out_shape)
    n_in, n_out, n_scr = len(in_specs), len(out_shape), len(scratch_shapes)
    if not carried:
        outs = pl.pallas_call(kern, name=name, grid=grid, in_specs=list(in_specs), out_specs=out_specs,
                              out_shape=out_shape, scratch_shapes=list(scratch_shapes),
                              compiler_params=_cparams(semantics))(*args)
        return (outs[0] if single else list(outs)), []
    akeys, fkeys, arrays, shapes, sems = _carried_layout(carried)
    n_a, n_f = len(akeys), len(fkeys)
    total = math.prod(grid)
    mid_step = min(total - 1, (17 * total) // 20)

    def wrapped(*refs):
        ins = refs[:n_in]
        o0 = n_in + n_a
        outs = refs[o0:o0 + n_out]
        a0 = o0 + n_out
        s0 = a0 + n_a + n_f
        per = [dict() for _ in carried]
        for (ci, k), r in zip(akeys + fkeys, refs[a0:s0]):
            per[ci][k] = r
        scratch = refs[s0:s0 + n_scr]
        sem = refs[s0 + n_scr:]
        step = 0
        for d, g in enumerate(grid):
            step = step * g + pl.program_id(d)

        @pl.when(step == 0)
        def _():
            for ci, cm in enumerate(carried):
                cm.start(per[ci], sem[2 * ci], sem[2 * ci + 1])

        kern(*ins, *outs, *scratch)

        @pl.when(step == mid_step)
        def _():
            for ci, cm in enumerate(carried):
                if cm.mid is not None:
                    cm.mid(per[ci], sem[2 * ci], sem[2 * ci + 1])

        @pl.when(step == total - 1)
        def _():
            for ci, cm in enumerate(carried):
                cm.finish(per[ci], sem[2 * ci], sem[2 * ci + 1])

    outs = pl.pallas_call(
        wrapped, name=name, grid=grid,
        in_specs=list(in_specs) + [ANY] * n_a, out_specs=out_specs + [ANY] * (n_a + n_f),
        out_shape=out_shape + shapes,
        input_output_aliases={n_in + i: n_out + i for i in range(n_a)},
        scratch_shapes=list(scratch_shapes) + sems,
        compiler_params=_cparams(("arbitrary",) * len(grid)),
    )(*args, *arrays)
    res = _carried_results(carried, akeys, fkeys, outs[n_out:])
    return (outs[0] if single else list(outs[:n_out])), res


def _run_comm(carried, name):
    carried = list(carried)
    akeys, fkeys, arrays, shapes, sems = _carried_layout(carried)
    n_a, n_f = len(akeys), len(fkeys)

    def body(*refs):
        per = [dict() for _ in carried]
        for (ci, k), r in zip(akeys + fkeys, refs[n_a:2 * n_a + n_f]):
            per[ci][k] = r
        sem = refs[2 * n_a + n_f:]
        for hook in ("start", "mid", "finish"):
            for ci, cm in enumerate(carried):
                fn = getattr(cm, hook)
                if fn is not None:
                    fn(per[ci], sem[2 * ci], sem[2 * ci + 1])

    outs = pl.pallas_call(
        body, name=name, in_specs=[ANY] * n_a, out_specs=[ANY] * (n_a + n_f), out_shape=shapes,
        input_output_aliases={i: i for i in range(n_a)}, scratch_shapes=sems,
    )(*arrays)
    return _carried_results(carried, akeys, fkeys, outs)


def _mm(a, b, *, mode, out_dtype, name, res=None, carried=()):
    if mode == "nn":
        (m, k), (k2, n) = a.shape, b.shape
    elif mode == "nt":
        (m, k), (n, k2) = a.shape, b.shape
    else:
        (k, m), (k2, n) = a.shape, b.shape
    assert k == k2, (name, a.shape, b.shape)
    tm = _div(m, MM_TILE_CAP, LANE)
    tn = _div(n, MM_TILE_CAP, MXU_WIDTH) if n % MXU_WIDTH == 0 else 0
    if tn < 1024:
        tn = _div(n, MM_TILE_CAP, LANE)
    out_shape = jax.ShapeDtypeStruct((m, n), out_dtype)

    if mode == "tn":
        assert res is None

        def kern_tn(a_ref, b_ref, o_ref, at_ref):
            @pl.when(pl.program_id(1) == 0)
            def _():
                step = min(TRANSPOSE_CHUNK, k)
                for c0 in range(0, k, step):
                    at_ref[:, c0:c0 + step] = a_ref[c0:c0 + step, :].astype(F32).T.astype(BF16)

            o_ref[...] = jnp.dot(at_ref[...], b_ref[...].astype(BF16),
                                 preferred_element_type=F32).astype(o_ref.dtype)

        return _pcall(
            kern_tn, name=name, grid=(m // tm, n // tn),
            in_specs=[pl.BlockSpec((k, tm), lambda i, j: (0, i)),
                      pl.BlockSpec((k, tn), lambda i, j: (0, j))],
            out_specs=pl.BlockSpec((tm, tn), lambda i, j: (i, j)),
            out_shape=out_shape, args=(a, b),
            scratch_shapes=[pltpu.VMEM((tm, k), BF16)],
            semantics=("parallel", "arbitrary"), carried=carried)

    tk = k if k <= 2048 else _div(k, 3072, LANE)
    nk = k // tk
    a_spec = pl.BlockSpec((tm, tk), lambda i, j, kk: (i, kk))
    if mode == "nn":
        b_spec = pl.BlockSpec((tk, tn), lambda i, j, kk: (kk, j))
        dot = _dot
    else:
        b_spec = pl.BlockSpec((tn, tk), lambda i, j, kk: (j, kk))
        dot = _dot_nt
    o_spec = pl.BlockSpec((tm, tn), lambda i, j, kk: (i, j))
    in_specs = [a_spec, b_spec]
    args = [a, b]
    if res is not None:
        in_specs.append(o_spec)
        args.append(res)
    has_res = res is not None

    def kern(*refs):
        a_ref, b_ref = refs[0], refs[1]
        r_ref = refs[2] if has_res else None
        o_ref = refs[3] if has_res else refs[2]
        part = dot(a_ref[...], b_ref[...])
        if nk == 1:
            if has_res:
                part = part + r_ref[...]
            o_ref[...] = part.astype(o_ref.dtype)
            return
        acc_ref = refs[-1]
        kk = pl.program_id(2)

        @pl.when(kk == 0)
        def _():
            acc_ref[...] = part

        @pl.when(kk > 0)
        def _():
            acc_ref[...] += part

        @pl.when(kk == nk - 1)
        def _():
            tot = acc_ref[...]
            if has_res:
                tot = tot + r_ref[...]
            o_ref[...] = tot.astype(o_ref.dtype)

    return _pcall(
        kern, name=name, grid=(m // tm, n // tn, nk),
        in_specs=in_specs, out_specs=o_spec, out_shape=out_shape, args=args,
        scratch_shapes=[pltpu.VMEM((tm, tn), F32)] if nk > 1 else [],
        semantics=("parallel", "parallel", "arbitrary"), carried=carried)


ROW_BLOCK = 256


def _rms_fwd(x, g, name):
    t, d = x.shape
    tr = min(ROW_BLOCK, t)

    def kern(x_ref, g_ref, o_ref):
        xf = x_ref[...]
        r = lax.rsqrt(jnp.mean(xf * xf, axis=-1, keepdims=True) + RMS_EPS)
        o_ref[...] = (xf * r * g_ref[...]).astype(o_ref.dtype)

    return pl.pallas_call(
        kern, name=name, grid=(t // tr,),
        in_specs=[pl.BlockSpec((tr, d), lambda i: (i, 0)), pl.BlockSpec((1, d), lambda i: (0, 0))],
        out_specs=pl.BlockSpec((tr, d), lambda i: (i, 0)),
        out_shape=jax.ShapeDtypeStruct((t, d), BF16),
        compiler_params=_cparams(("parallel",)),
    )(x, g)


def _rms_bwd(x, g, dh, res, name):
    t, d = x.shape
    tr = min(ROW_BLOCK, t)
    has_res = res is not None

    def kern(*refs):
        x_ref, g_ref, dh_ref = refs[:3]
        r_ref = refs[3] if has_res else None
        dx_ref, dg_ref = refs[-2], refs[-1]
        xf = x_ref[...]
        r = lax.rsqrt(jnp.mean(xf * xf, axis=-1, keepdims=True) + RMS_EPS)
        xn = xf * r
        dh_ = dh_ref[...]
        dhg = dh_ * g_ref[...]
        dx = r * (dhg - xn * jnp.mean(dhg * xn, axis=-1, keepdims=True))
        if has_res:
            dx = dx + r_ref[...]
        dx_ref[...] = dx
        part = jnp.sum(dh_ * xn, axis=0, keepdims=True)

        @pl.when(pl.program_id(0) == 0)
        def _():
            dg_ref[...] = part

        @pl.when(pl.program_id(0) > 0)
        def _():
            dg_ref[...] += part

    row = pl.BlockSpec((tr, d), lambda i: (i, 0))
    vec = pl.BlockSpec((1, d), lambda i: (0, 0))
    in_specs = [row, vec, row] + ([row] if has_res else [])
    args = [x, g, dh] + ([res] if has_res else [])
    return pl.pallas_call(
        kern, name=name, grid=(t // tr,), in_specs=in_specs, out_specs=[row, vec],
        out_shape=[jax.ShapeDtypeStruct((t, d), F32), jax.ShapeDtypeStruct((1, d), F32)],
        compiler_params=_cparams(("arbitrary",)),
    )(*args)


def _loss_head(x3, g, target, name):
    t, d = x3.shape
    tr = ROW_BLOCK

    def kern(x_ref, g_ref, t_ref, dx_ref, dg_ref, loss_ref):
        xf = x_ref[...]
        r = lax.rsqrt(jnp.mean(xf * xf, axis=-1, keepdims=True) + RMS_EPS)
        xn = xf * r
        gg = g_ref[...]
        err = xn * gg - t_ref[...]
        lpart = 0.5 * jnp.sum(jnp.mean(err * err, axis=-1, keepdims=True), axis=0, keepdims=True)
        dy = err * (1.0 / d)
        dyg = dy * gg
        dx_ref[...] = r * (dyg - xn * jnp.mean(dyg * xn, axis=-1, keepdims=True))
        gpart = jnp.sum(dy * xn, axis=0, keepdims=True)
        lrow = jnp.broadcast_to(lpart, (1, LANE))

        @pl.when(pl.program_id(0) == 0)
        def _():
            dg_ref[...] = gpart
            loss_ref[...] = lrow

        @pl.when(pl.program_id(0) > 0)
        def _():
            dg_ref[...] += gpart
            loss_ref[...] += lrow

    row = pl.BlockSpec((tr, d), lambda i: (i, 0))
    vec = pl.BlockSpec((1, d), lambda i: (0, 0))
    return pl.pallas_call(
        kern, name=name, grid=(t // tr,), in_specs=[row, vec, row],
        out_specs=[row, vec, pl.BlockSpec((1, LANE), lambda i: (0, 0))],
        out_shape=[jax.ShapeDtypeStruct((t, d), F32), jax.ShapeDtypeStruct((1, d), F32),
                   jax.ShapeDtypeStruct((1, LANE), F32)],
        compiler_params=_cparams(("arbitrary",)),
    )(x3, g, target)


ATT_SCALE = HEAD_DIM ** -0.5
Q_BLOCK0, K_BLOCK0, V_BLOCK0 = 0, ATT_WIDTH // HEAD_DIM, 2 * ATT_WIDTH // HEAD_DIM


def _residue_rows(dil, r, n):
    if dil == 1:
        return pl.ds(n * ATT_BLOCK, ATT_BLOCK)
    return pl.ds(n * ATT_BLOCK * dil + r, ATT_BLOCK, stride=dil)


def _band_mask(with_prev):
    width = 2 * ATT_BLOCK if with_prev else ATT_BLOCK
    iq = lax.broadcasted_iota(jnp.int32, (ATT_BLOCK, width), 0)
    ik = lax.broadcasted_iota(jnp.int32, (ATT_BLOCK, width), 1)
    if not with_prev:
        return ik <= iq
    return ((ik < ATT_BLOCK) & (iq <= ik)) | ((ik >= ATT_BLOCK) & ((ik - ATT_BLOCK) <= iq))


def _band_keys(ref, dil, r, n):
    own = ref[_residue_rows(dil, r, n), :]
    if n == 0:
        return own
    return jnp.concatenate([ref[_residue_rows(dil, r, n - 1), :], own], axis=0)


def _attn_col_spec(base, grp):
    return pl.BlockSpec((SEQ, HEAD_DIM), lambda h: (0, base + grp * ATT_HEADS + h))


def _attn_fwd(proj, grp, name, carried=()):
    _, dil = ATT_GROUPS[grp]
    nb = SEQ // dil // ATT_BLOCK

    def kern(q_ref, k_ref, v_ref, o_ref, lse_ref):
        for r in range(dil):
            for n in range(nb):
                rows = _residue_rows(dil, r, n)
                s = _dot_nt(q_ref[rows, :], _band_keys(k_ref, dil, r, n)) * ATT_SCALE
                s = jnp.where(_band_mask(n > 0), s, -jnp.inf)
                m = jnp.max(s, axis=-1, keepdims=True)
                p = jnp.exp(s - m)
                l = jnp.sum(p, axis=-1, keepdims=True)
                o_ref[rows, :] = _dot(p / l, _band_keys(v_ref, dil, r, n))
                lse_ref[rows, :] = jnp.broadcast_to(m + jnp.log(l), (ATT_BLOCK, HEAD_DIM))

    out_spec = pl.BlockSpec((SEQ, HEAD_DIM), lambda h: (0, h))
    return _pcall(
        kern, name=name, grid=(ATT_HEADS,),
        in_specs=[_attn_col_spec(Q_BLOCK0, grp), _attn_col_spec(K_BLOCK0, grp), _attn_col_spec(V_BLOCK0, grp)],
        out_specs=[out_spec, out_spec],
        out_shape=[jax.ShapeDtypeStruct((SEQ, ATT_OUT), F32)] * 2, args=(proj, proj, proj),
        semantics=("parallel",), carried=carried)


def _attn_weights(l0, l1, l2):
    mx = jnp.maximum(jnp.maximum(l0, l1), l2)
    e0, e1, e2 = jnp.exp(l0 - mx), jnp.exp(l1 - mx), jnp.exp(l2 - mx)
    den = e0 + e1 + e2
    return e0 / den, e1 / den, e2 / den


def _attn_merge_fwd(outs, lses, name):
    tr = ROW_BLOCK

    def kern(o0, o1, o2, l0, l1, l2, out_ref):
        a0, a1, a2 = _attn_weights(l0[...], l1[...], l2[...])
        out_ref[...] = (a0 * o0[...] + a1 * o1[...] + a2 * o2[...]).astype(out_ref.dtype)

    spec = pl.BlockSpec((tr, ATT_OUT), lambda i: (i, 0))
    return pl.pallas_call(
        kern, name=name, grid=(SEQ // tr,), in_specs=[spec] * 6, out_specs=spec,
        out_shape=jax.ShapeDtypeStruct((SEQ, ATT_OUT), BF16),
        compiler_params=_cparams(("parallel",)),
    )(*outs, *lses)


def _attn_merge_bwd(outs, lses, do_att, name, carried=()):
    tr = ROW_BLOCK

    def kern(o0, o1, o2, l0, l1, l2, do_ref, d0, d1, d2, t0, t1, t2):
        alphas = _attn_weights(l0[...], l1[...], l2[...])
        do = do_ref[...]
        o_att = alphas[0] * o0[...] + alphas[1] * o1[...] + alphas[2] * o2[...]
        prod = do * o_att
        parts = []
        for h in range(ATT_HEADS):
            sl = slice(h * HEAD_DIM, (h + 1) * HEAD_DIM)
            tot = jnp.sum(prod[:, sl], axis=-1, keepdims=True)
            parts.append(jnp.broadcast_to(tot, (tr, HEAD_DIM)))
        dd = jnp.concatenate(parts, axis=1)
        for a, d_ref, t_ref in zip(alphas, (d0, d1, d2), (t0, t1, t2)):
            d_ref[...] = a * do
            t_ref[...] = -a * dd

    spec = pl.BlockSpec((tr, ATT_OUT), lambda i: (i, 0))
    res, cres = _pcall(
        kern, name=name, grid=(SEQ // tr,), in_specs=[spec] * 7, out_specs=[spec] * 6,
        out_shape=[jax.ShapeDtypeStruct((SEQ, ATT_OUT), F32)] * 6, args=(*outs, *lses, do_att),
        semantics=("parallel",), carried=carried)
    return (res[:3], res[3:]), cres


def _attn_bwd(proj, grp, lse, do_g, dl_g, name, carried=()):
    _, dil = ATT_GROUPS[grp]
    nb = SEQ // dil // ATT_BLOCK

    def kern(q_ref, k_ref, v_ref, do_ref, lse_ref, dl_ref, dq_ref, dk_ref, dv_ref, dq_acc, dk_acc, dv_acc):
        dk_acc[...] = jnp.zeros_like(dk_acc)
        dv_acc[...] = jnp.zeros_like(dv_acc)
        for r in range(dil):
            for n in range(nb):
                rows = _residue_rows(dil, r, n)
                q, do = q_ref[rows, :], do_ref[rows, :]
                kk, vv = _band_keys(k_ref, dil, r, n), _band_keys(v_ref, dil, r, n)
                s = _dot_nt(q, kk) * ATT_SCALE
                p = jnp.where(_band_mask(n > 0), jnp.exp(s - lse_ref[rows, :][:, :1]), 0.0)
                ds = p * (_dot_nt(do, vv) + dl_ref[rows, :][:, :1])
                dq_acc[rows, :] = _dot(ds, kk) * ATT_SCALE
                dk = _dot_tn(ds, q) * ATT_SCALE
                dv = _dot_tn(p, do)
                if n > 0:
                    prev = _residue_rows(dil, r, n - 1)
                    dk_acc[prev, :] += dk[:ATT_BLOCK]
                    dv_acc[prev, :] += dv[:ATT_BLOCK]
                    dk, dv = dk[ATT_BLOCK:], dv[ATT_BLOCK:]
                dk_acc[rows, :] += dk
                dv_acc[rows, :] += dv
        dq_ref[...] = dq_acc[...].astype(dq_ref.dtype)
        dk_ref[...] = dk_acc[...].astype(dk_ref.dtype)
        dv_ref[...] = dv_acc[...].astype(dv_ref.dtype)

    spec = pl.BlockSpec((SEQ, HEAD_DIM), lambda h: (0, h))
    return _pcall(
        kern, name=name, grid=(ATT_HEADS,),
        in_specs=[_attn_col_spec(Q_BLOCK0, grp), _attn_col_spec(K_BLOCK0, grp), _attn_col_spec(V_BLOCK0, grp),
                  spec, spec, spec],
        out_specs=[spec] * 3,
        out_shape=[jax.ShapeDtypeStruct((SEQ, ATT_OUT), BF16)] * 3, args=(proj, proj, proj, do_g, lse, dl_g),
        scratch_shapes=[pltpu.VMEM((SEQ, HEAD_DIM), F32)] * 3,
        semantics=("parallel",), carried=carried)


HG_HEADS_PER_STEP = 4
HG_BLOCK_W = HG_HEADS_PER_STEP * HEAD_DIM
HG_Q_BLK = (3 * ATT_WIDTH) // HG_BLOCK_W
HG_N_CHUNKS = SEQ // HG_CHUNK
HG_MID = HG_CHUNK // 2


def _lower_bound(lb_ref, sl):
    l0, l1 = lb_ref[0:1, sl], lb_ref[1:2, sl]
    mx = jnp.maximum(l0, l1)
    e0, e1 = jnp.exp(l0 - mx), jnp.exp(l1 - mx)
    return e0 / (e0 + e1)


def _tri(lower):
    i = lax.broadcasted_iota(jnp.int32, (HG_CHUNK, HG_CHUNK), 0)
    j = lax.broadcasted_iota(jnp.int32, (HG_CHUNK, HG_CHUNK), 1)
    return (i >= j) if lower else (i <= j)


def _hg_chunk_terms(qh, fh, lb):
    sig = _sigmoid(fh)
    f = lb + (1.0 - lb) * sig
    k = 1.0 - f
    b = _dot_exact(_tri(True).astype(F32), jnp.log(f))
    bl = b[HG_CHUNK - 1:HG_CHUNK, :]
    br = b[HG_MID:HG_MID + 1, :]
    sq = _sigmoid(qh)
    q = qh * sq
    return dict(sig=sig, f=f, k=k, b=b, bl=bl, br=br, sq=sq, q=q,
                e1=jnp.exp(bl - b), e2=jnp.exp(b), e3=jnp.exp(b - br), e4=jnp.exp(br - b))


def _hg_fwd(proj, lbw, normw, name, carried=()):
    def in_blk(off):
        return pl.BlockSpec((HG_CHUNK, HG_BLOCK_W), lambda hp, n: (n, HG_Q_BLK + off + hp))

    def kern(q_ref, f_ref, i_ref, g_ref, lb_ref, nw_ref, oraw_ref, ohg_ref, st_ref, state):
        @pl.when(pl.program_id(1) == 0)
        def _():
            state[...] = jnp.zeros_like(state)

        causal = _tri(True)
        for hd in range(HG_HEADS_PER_STEP):
            sl = slice(hd * HEAD_DIM, (hd + 1) * HEAD_DIM)
            t = _hg_chunk_terms(q_ref[:, sl], f_ref[:, sl], _lower_bound(lb_ref, sl))
            v = i_ref[:, sl]
            st = state[hd]
            st_ref[0, hd] = st
            kd = t["k"] * t["e1"]
            inter = _dot_nt(t["q"] * t["e2"], st)
            a = jnp.where(causal, _dot_nt(t["q"] * t["e3"], t["k"] * t["e4"]), 0.0)
            o = inter + _dot(a, v)
            state[hd] = st * jnp.exp(t["bl"]) + _dot_tn(v, kd)
            oraw_ref[:, sl] = o
            r = lax.rsqrt(jnp.mean(o * o, axis=-1, keepdims=True) + RMS_EPS)
            gh = g_ref[:, sl]
            ohg_ref[:, sl] = (o * r * nw_ref[...] * (gh * _sigmoid(gh))).astype(ohg_ref.dtype)

    out_blk = pl.BlockSpec((HG_CHUNK, HG_BLOCK_W), lambda hp, n: (n, hp))
    return _pcall(
        kern, name=name, grid=(HG_HEADS // HG_HEADS_PER_STEP, HG_N_CHUNKS),
        in_specs=[in_blk(0), in_blk(2), in_blk(4), in_blk(6),
                  pl.BlockSpec((2, HG_BLOCK_W), lambda hp, n: (0, hp)),
                  pl.BlockSpec((1, HEAD_DIM), lambda hp, n: (0, 0))],
        out_specs=[out_blk, out_blk,
                   pl.BlockSpec((1, HG_HEADS_PER_STEP, HEAD_DIM, HEAD_DIM), lambda hp, n: (n, hp, 0, 0))],
        out_shape=[jax.ShapeDtypeStruct((SEQ, HG_WIDTH), F32), jax.ShapeDtypeStruct((SEQ, HG_WIDTH), BF16),
                   jax.ShapeDtypeStruct((HG_N_CHUNKS, HG_HEADS, HEAD_DIM, HEAD_DIM), F32)],
        args=(proj, proj, proj, proj, lbw, normw),
        scratch_shapes=[pltpu.VMEM((HG_HEADS_PER_STEP, HEAD_DIM, HEAD_DIM), F32)],
        semantics=("parallel", "arbitrary"), carried=carried)


def _hg_bwd(proj, lbw, normw, oraw, states, do_hg, name, carried=()):
    last = HG_N_CHUNKS - 1

    def in_blk(off):
        return pl.BlockSpec((HG_CHUNK, HG_BLOCK_W), lambda hp, n: (last - n, HG_Q_BLK + off + hp))

    blk = pl.BlockSpec((HG_CHUNK, HG_BLOCK_W), lambda hp, n: (last - n, hp))

    def kern(q_ref, f_ref, i_ref, g_ref, lb_ref, nw_ref, oraw_ref, st_ref, do_ref,
             dq_ref, df_ref, di_ref, dg_ref, dlb_ref, dnw_ref, dstate):
        first = pl.program_id(1) == 0

        @pl.when(first)
        def _():
            dstate[...] = jnp.zeros_like(dstate)

        causal = _tri(True)
        rows = lax.broadcasted_iota(jnp.int32, (HG_CHUNK, HEAD_DIM), 0)
        nw = nw_ref[...]
        dnw_tot = jnp.zeros((1, HEAD_DIM), F32)
        dlb_parts = []
        for hd in range(HG_HEADS_PER_STEP):
            sl = slice(hd * HEAD_DIM, (hd + 1) * HEAD_DIM)
            qh, fh, v, gh = q_ref[:, sl], f_ref[:, sl], i_ref[:, sl], g_ref[:, sl]
            o, dout = oraw_ref[:, sl], do_ref[:, sl]
            sgg = _sigmoid(gh)
            r = lax.rsqrt(jnp.mean(o * o, axis=-1, keepdims=True) + RMS_EPS)
            xn = o * r
            dg_ref[:, sl] = (dout * xn * nw * (sgg * (1.0 + gh * (1.0 - sgg)))).astype(dg_ref.dtype)
            don = dout * (gh * sgg)
            dnw_tot = dnw_tot + jnp.sum(don * xn, axis=0, keepdims=True)
            tt = don * nw
            do = r * (tt - xn * jnp.mean(tt * xn, axis=-1, keepdims=True))
            lb = _lower_bound(lb_ref, sl)
            t = _hg_chunk_terms(qh, fh, lb)
            k, q = t["k"], t["q"]
            kd, qb, qr, kr = k * t["e1"], q * t["e2"], q * t["e3"], k * t["e4"]
            a = jnp.where(causal, _dot_nt(qr, kr), 0.0)
            st = st_ref[0, hd]
            dstn = dstate[hd]
            dqb = _dot(do, st)
            da = jnp.where(causal, _dot_nt(do, v), 0.0)
            dv = _dot_tn(a, do) + _dot_nt(kd, dstn)
            dqr = _dot(da, kr)
            dkr = _dot_tn(da, qr)
            dkd = _dot(v, dstn)
            decay = jnp.exp(t["bl"])
            ddecay = jnp.sum(dstn * st, axis=0, keepdims=True)
            dstate[hd] = dstn * decay + _dot_tn(do, qb)
            dq = dqb * t["e2"] + dqr * t["e3"]
            dk = dkd * t["e1"] + dkr * t["e4"]
            db = dqb * qb + dqr * qr - dkr * kr - dkd * kd
            dbl = jnp.sum(dkd * kd, axis=0, keepdims=True) + ddecay * decay
            dbr = jnp.sum(dkr * kr - dqr * qr, axis=0, keepdims=True)
            dlf = _dot_exact(_tri(False).astype(F32), db) + dbl + jnp.where(rows <= HG_MID, dbr, 0.0)
            df = dlf / t["f"] - dk
            sig, sq = t["sig"], t["sq"]
            df_ref[:, sl] = (df * (1.0 - lb) * sig * (1.0 - sig)).astype(df_ref.dtype)
            dlb_parts.append(jnp.sum(df * (1.0 - sig), axis=0, keepdims=True))
            dq_ref[:, sl] = (dq * (sq * (1.0 + qh * (1.0 - sq)))).astype(dq_ref.dtype)
            di_ref[:, sl] = dv.astype(di_ref.dtype)
        dlb_row = jnp.concatenate(dlb_parts, axis=1)
        dnw_blk = jnp.broadcast_to(dnw_tot, (8, HEAD_DIM))

        @pl.when(first)
        def _():
            dlb_ref[...] = dlb_row
            dnw_ref[...] = dnw_blk

        @pl.when(jnp.logical_not(first))
        def _():
            dlb_ref[...] += dlb_row
            dnw_ref[...] += dnw_blk

    n_hp = HG_HEADS // HG_HEADS_PER_STEP
    outs, cres = _pcall(
        kern, name=name, grid=(n_hp, HG_N_CHUNKS),
        in_specs=[in_blk(0), in_blk(2), in_blk(4), in_blk(6),
                  pl.BlockSpec((2, HG_BLOCK_W), lambda hp, n: (0, hp)),
                  pl.BlockSpec((1, HEAD_DIM), lambda hp, n: (0, 0)),
                  blk,
                  pl.BlockSpec((1, HG_HEADS_PER_STEP, HEAD_DIM, HEAD_DIM), lambda hp, n: (last - n, hp, 0, 0)),
                  blk],
        out_specs=[blk, blk, blk, blk,
                   pl.BlockSpec((1, HG_BLOCK_W), lambda hp, n: (0, hp)),
                   pl.BlockSpec((8, HEAD_DIM), lambda hp, n: (hp, 0))],
        out_shape=[jax.ShapeDtypeStruct((SEQ, HG_WIDTH), BF16)] * 4
        + [jax.ShapeDtypeStruct((1, HG_WIDTH), F32), jax.ShapeDtypeStruct((8 * n_hp, HEAD_DIM), F32)],
        args=(proj, proj, proj, proj, lbw, normw, oraw, states, do_hg),
        scratch_shapes=[pltpu.VMEM((HG_HEADS_PER_STEP, HEAD_DIM, HEAD_DIM), F32)],
        semantics=("parallel", "arbitrary"), carried=carried)
    dqh, dfh, dih, dgh, dlb, dnw = outs
    return (dqh, dfh, dih, dgh, dlb, (dnw[0:1], dnw[8:9])), cres


GATE_BLOCK_W = 512
GATE_A_BLK = (3 * ATT_WIDTH + 4 * HG_WIDTH) // GATE_BLOCK_W
GATE_B_BLK = GATE_A_BLK + D_MODEL // GATE_BLOCK_W


def _gate_specs():
    tr = ROW_BLOCK
    blk = pl.BlockSpec((tr, GATE_BLOCK_W), lambda i, j: (i, j))
    ga = pl.BlockSpec((tr, GATE_BLOCK_W), lambda i, j: (i, GATE_A_BLK + j))
    gb = pl.BlockSpec((tr, GATE_BLOCK_W), lambda i, j: (i, GATE_B_BLK + j))
    return (SEQ // tr, D_MODEL // GATE_BLOCK_W), blk, ga, gb


def _gate_fwd(proj, ya, yb, name, carried=()):
    grid, blk, ga, gb = _gate_specs()

    def kern(ga_ref, gb_ref, ya_ref, yb_ref, o_ref):
        o_ref[...] = (_sigmoid(ga_ref[...]) * ya_ref[...] + _sigmoid(gb_ref[...]) * yb_ref[...]).astype(o_ref.dtype)

    return _pcall(
        kern, name=name, grid=grid, in_specs=[ga, gb, blk, blk], out_specs=blk,
        out_shape=jax.ShapeDtypeStruct((SEQ, D_MODEL), BF16), args=(proj, proj, ya, yb),
        semantics=("parallel", "parallel"), carried=carried)


def _gate_bwd(proj, ya, yb, dmerged, name, carried=()):
    grid, blk, ga, gb = _gate_specs()

    def kern(ga_ref, gb_ref, ya_ref, yb_ref, dm_ref, dya_ref, dyb_ref, dga_ref, dgb_ref):
        dm = dm_ref[...]
        sa, sb = _sigmoid(ga_ref[...]), _sigmoid(gb_ref[...])
        dya_ref[...] = (dm * sa).astype(dya_ref.dtype)
        dyb_ref[...] = (dm * sb).astype(dyb_ref.dtype)
        dga_ref[...] = (dm * ya_ref[...] * sa * (1.0 - sa)).astype(dga_ref.dtype)
        dgb_ref[...] = (dm * yb_ref[...] * sb * (1.0 - sb)).astype(dgb_ref.dtype)

    return _pcall(
        kern, name=name, grid=grid, in_specs=[ga, gb, blk, blk, blk], out_specs=[blk] * 4,
        out_shape=[jax.ShapeDtypeStruct((SEQ, D_MODEL), BF16)] * 4, args=(proj, proj, ya, yb, dmerged),
        semantics=("parallel", "parallel"), carried=carried)


FF_SHARD = D_FF // N_CHIPS


def _swiglu_fwd(ab, name):
    tr = ROW_BLOCK

    def kern(ab_ref, u_ref):
        a, b = ab_ref[:, :FF_SHARD], ab_ref[:, FF_SHARD:]
        u_ref[...] = (a * _sigmoid(a) * b).astype(u_ref.dtype)

    return pl.pallas_call(
        kern, name=name, grid=(SEQ // tr, N_CHIPS),
        in_specs=[pl.BlockSpec((tr, 2 * FF_SHARD), lambda i, j: (i, j))],
        out_specs=pl.BlockSpec((tr, FF_SHARD), lambda i, j: (i, j)),
        out_shape=jax.ShapeDtypeStruct((SEQ, D_FF), BF16),
        compiler_params=_cparams(("parallel", "parallel")),
    )(ab)


def _swiglu_bwd(ab, du, name, carried=()):
    tr = ROW_BLOCK

    def kern(ab_ref, du_ref, dab_ref):
        a, b = ab_ref[:, :FF_SHARD], ab_ref[:, FF_SHARD:]
        du_ = du_ref[...]
        sg = _sigmoid(a)
        dab_ref[:, :FF_SHARD] = (du_ * b * (sg * (1.0 + a * (1.0 - sg)))).astype(dab_ref.dtype)
        dab_ref[:, FF_SHARD:] = (du_ * (a * sg)).astype(dab_ref.dtype)

    wide = pl.BlockSpec((tr, 2 * FF_SHARD), lambda i, j: (i, j))
    return _pcall(
        kern, name=name, grid=(SEQ // tr, N_CHIPS),
        in_specs=[wide, pl.BlockSpec((tr, FF_SHARD), lambda i, j: (i, j))],
        out_specs=wide, out_shape=jax.ShapeDtypeStruct((SEQ, 2 * D_FF), BF16), args=(ab, du),
        semantics=("parallel", "parallel"), carried=carried)


CROSS_ROWS = 512


def _cross_fwd(qc, kvc, name):
    def kern(q_ref, k_ref, v_ref, o_ref):
        s = _dot_nt(q_ref[...], k_ref[...]) * ATT_SCALE
        m = jnp.max(s, axis=-1, keepdims=True)
        e = jnp.exp(s - m)
        p = e / jnp.sum(e, axis=-1, keepdims=True)
        o_ref[...] = _dot(p, v_ref[...]).astype(o_ref.dtype)

    qblk = pl.BlockSpec((CROSS_ROWS, HEAD_DIM), lambda h, i: (i, h))
    return pl.pallas_call(
        kern, name=name, grid=(CROSS_HEADS, SEQ // CROSS_ROWS),
        in_specs=[qblk, pl.BlockSpec((MEM_LEN, HEAD_DIM), lambda h, i: (0, h)),
                  pl.BlockSpec((MEM_LEN, HEAD_DIM), lambda h, i: (0, CROSS_HEADS + h))],
        out_specs=qblk, out_shape=jax.ShapeDtypeStruct((SEQ, CROSS_WIDTH), BF16),
        compiler_params=_cparams(("parallel", "parallel")),
    )(qc, kvc, kvc)


def _cross_bwd(qc, kvc, doc, name):
    def kern(q_ref, k_ref, v_ref, do_ref, dq_ref, dk_ref, dv_ref):
        q, k, v, do = q_ref[...], k_ref[...], v_ref[...], do_ref[...]
        s = _dot_nt(q, k) * ATT_SCALE
        m = jnp.max(s, axis=-1, keepdims=True)
        e = jnp.exp(s - m)
        p = e / jnp.sum(e, axis=-1, keepdims=True)
        dp = _dot_nt(do, v)
        ds = p * (dp - jnp.sum(dp * p, axis=-1, keepdims=True))
        dq_ref[...] = (_dot(ds, k) * ATT_SCALE).astype(dq_ref.dtype)
        dk = _dot_tn(ds, q) * ATT_SCALE
        dv = _dot_tn(p, do)

        @pl.when(pl.program_id(1) == 0)
        def _():
            dk_ref[...] = dk
            dv_ref[...] = dv

        @pl.when(pl.program_id(1) > 0)
        def _():
            dk_ref[...] += dk
            dv_ref[...] += dv

    qblk = pl.BlockSpec((CROSS_ROWS, HEAD_DIM), lambda h, i: (i, h))
    kblk = pl.BlockSpec((MEM_LEN, HEAD_DIM), lambda h, i: (0, h))
    dq, dk, dv = pl.pallas_call(
        kern, name=name, grid=(CROSS_HEADS, SEQ // CROSS_ROWS),
        in_specs=[qblk, kblk, pl.BlockSpec((MEM_LEN, HEAD_DIM), lambda h, i: (0, CROSS_HEADS + h)), qblk],
        out_specs=[qblk, kblk, kblk],
        out_shape=[jax.ShapeDtypeStruct((SEQ, CROSS_WIDTH), BF16),
                   jax.ShapeDtypeStruct((MEM_LEN, CROSS_WIDTH), F32),
                   jax.ShapeDtypeStruct((MEM_LEN, CROSS_WIDTH), F32)],
        compiler_params=_cparams(("parallel", "arbitrary")),
    )(qc, kvc, kvc, doc)
    return dq, jnp.concatenate([dk, dv], axis=1)


FULL_SPECS = {
    "w_in": ("col", D_MODEL, IN_WIDTH),
    "w_branch_a": ("col", ATT_OUT, D_MODEL),
    "w_branch_b": ("col", HG_WIDTH, D_MODEL),
    "w_out": ("row", D_MODEL, D_MODEL),
    "wq_cross": ("row", D_MODEL, CROSS_WIDTH),
    "wkv_cross": ("row", D_MODEL, 2 * CROSS_WIDTH),
    "wo_cross": ("col", CROSS_WIDTH, D_MODEL),
    "w13": ("col", D_MODEL, 2 * D_FF),
    "w2": ("row", D_FF, D_MODEL),
}
WEIGHT_PLACE = {
    "w_in": ("w_in", 0), "w_branch_a": ("w_branch_a", 0), "w_branch_b": ("w_branch_b", 0),
    "w_out": ("w_out", 0), "wq_cross": ("wq_cross", 0), "wkv_cross": ("wkv_cross", 0),
    "wo_cross": ("wo_cross", 0), "w1": ("w13", 0), "w3": ("w13", FF_SHARD), "w2": ("w2", 0),
}
BIG_WEIGHTS = tuple(WEIGHT_PLACE)
EW_BLOCK_ELEMS = 512 * 1024


def _position():
    return lax.axis_index("x"), lax.axis_index("y"), lax.axis_index("c")


def _other_chips(x, y):
    return [(1 - x, y), (x, 1 - y), (1 - x, 1 - y)]


def _half(ref, kind, h):
    r, c = ref.shape
    if kind == "col":
        return ref.at[pl.ds(h * (r // 2), r // 2), :]
    return ref.at[:, pl.ds(h * (c // 2), c // 2)]


def _shard_of(ref, kind, start, size):
    return ref.at[:, pl.ds(start, size)] if kind == "col" else ref.at[pl.ds(start, size), :]


def _rows_of(ref, r0, nrows):
    return ref if nrows is None else ref.at[pl.ds(r0, nrows), :]


def _half_shape(kind, rows, cols):
    return (rows // 2, cols) if kind == "col" else (rows, cols // 2)


def _slot_shape(spec):
    kind, rows, cols = spec
    hr, hc = _half_shape(kind, rows, cols)
    return (hr, hc // N_CHIPS) if kind == "col" else (hr // N_CHIPS, hc)


def _remote(src, dst, send_sem, recv_sem, device):
    return pltpu.make_async_remote_copy(src_ref=src, dst_ref=dst, send_sem=send_sem, recv_sem=recv_sem,
                                        device_id=device, device_id_type=MESH)


def _gather_comm(fulls, jobs, specs):
    def piece(refs, job, chip, c):
        f, r0, nr = job
        kind, rows, cols = specs[f]
        stride = (cols if kind == "col" else rows) // N_CHIPS
        return _rows_of(_half(_shard_of(refs[f], kind, chip * stride, stride), kind, c), r0, nr)

    def rect(refs, job, h):
        f, r0, nr = job
        assert nr is None or specs[f][0] == "col"
        return _rows_of(_half(refs[f], specs[f][0], h), r0, nr)

    def start(refs, ss, rs):
        x, y, c = _position()
        j = 2 * x + y
        for q, job in enumerate(jobs):
            for p, (px, py) in enumerate(_other_chips(x, y)):
                _remote(piece(refs, job, j, c), piece(refs, job, j, c), ss.at[4 * q + p], rs.at[4 * q + p],
                        (px, py, c)).start()

    def mid(refs, ss, rs):
        x, y, c = _position()
        j = 2 * x + y
        for q, job in enumerate(jobs):
            for p, (px, py) in enumerate(_other_chips(x, y)):
                _remote(piece(refs, job, j, c), piece(refs, job, 2 * px + py, c), ss.at[4 * q + p],
                        rs.at[4 * q + p], (px, py, c)).wait_recv()
            _remote(rect(refs, job, c), rect(refs, job, c), ss.at[4 * q + 3], rs.at[4 * q + 3],
                    (x, y, 1 - c)).start()

    def finish(refs, ss, rs):
        x, y, c = _position()
        j = 2 * x + y
        for q, job in enumerate(jobs):
            _remote(rect(refs, job, 1 - c), rect(refs, job, 1 - c), ss.at[4 * q + 3], rs.at[4 * q + 3],
                    (x, y, 1 - c)).wait_recv()
        for q, job in enumerate(jobs):
            for p, (px, py) in enumerate(_other_chips(x, y)):
                _remote(piece(refs, job, j, c), piece(refs, job, j, c), ss.at[4 * q + p], rs.at[4 * q + p],
                        (px, py, c)).wait_send()
            _remote(rect(refs, job, c), rect(refs, job, c), ss.at[4 * q + 3], rs.at[4 * q + 3],
                    (x, y, 1 - c)).wait_send()

    names = list(dict.fromkeys(job[0] for job in jobs))
    return _Carried({f: fulls[f] for f in names}, {}, 4 * len(jobs), start, finish, mid)


def _pairx_comm(grads, names, specs):
    def copies(refs, ss, rs):
        x, y, c = _position()
        return [_remote(_half(refs[("g", f)], specs[f][0], 1 - c), refs[("r", f)], ss.at[i], rs.at[i], (x, y, 1 - c))
                for i, f in enumerate(names)]

    def start(refs, ss, rs):
        for cp in copies(refs, ss, rs):
            cp.start()

    def finish(refs, ss, rs):
        for cp in copies(refs, ss, rs):
            cp.wait_recv()
        for cp in copies(refs, ss, rs):
            cp.wait_send()

    fresh = {("r", f): jax.ShapeDtypeStruct(_half_shape(*specs[f]), BF16) for f in names}
    return _Carried({("g", f): grads[f] for f in names}, fresh, len(names), start, finish)


def _chipx_comm(pair_sums, slots, jobs, specs):
    def copies(refs, ss, rs):
        x, y, c = _position()
        out = []
        for q, (f, r0, nr) in enumerate(jobs):
            kind = specs[f][0]
            width = _slot_shape(specs[f])[1 if kind == "col" else 0]
            for p, (px, py) in enumerate(_other_chips(x, y)):
                src = _rows_of(_shard_of(refs[("p", f)], kind, (2 * px + py) * width, width), r0, nr)
                dst = _rows_of(refs[("s", f)].at[p], r0, nr)
                out.append(_remote(src, dst, ss.at[3 * q + p], rs.at[3 * q + p], (px, py, c)))
        return out

    def start(refs, ss, rs):
        for cp in copies(refs, ss, rs):
            cp.start()

    def finish(refs, ss, rs):
        for cp in copies(refs, ss, rs):
            cp.wait_recv()
        for cp in copies(refs, ss, rs):
            cp.wait_send()

    names = list(dict.fromkeys(job[0] for job in jobs))
    arrays = {("p", f): pair_sums[f] for f in names}
    arrays.update({("s", f): slots[f] for f in names})
    return _Carried(arrays, {}, 3 * len(jobs), start, finish)


def _share_comm(grads, wnames, specs, place):
    def start(refs, ss, rs):
        x, y, c = _position()
        for i, w in enumerate(wnames):
            kind = specs[place[w][0]][0]
            _remote(_half(refs[w], kind, c), _half(refs[w], kind, c), ss.at[i], rs.at[i], (x, y, 1 - c)).start()

    def finish(refs, ss, rs):
        x, y, c = _position()
        for i, w in enumerate(wnames):
            kind = specs[place[w][0]][0]
            _remote(_half(refs[w], kind, 1 - c), _half(refs[w], kind, 1 - c), ss.at[i], rs.at[i],
                    (x, y, 1 - c)).wait_recv()
        for i, w in enumerate(wnames):
            kind = specs[place[w][0]][0]
            _remote(_half(refs[w], kind, c), _half(refs[w], kind, c), ss.at[i], rs.at[i], (x, y, 1 - c)).wait_send()

    return _Carried({w: grads[w] for w in wnames}, {}, len(wnames), start, finish)


def _gather_rows(v, name="gather_small"):
    shape = v.shape

    def body(v_ref, out_ref, send_sem, recv_sem, loc_sem):
        x, y, c = _position()
        me = 4 * x + 2 * y + c
        flips = [(fx, fy, fc) for fx in (0, 1) for fy in (0, 1) for fc in (0, 1)][1:]

        def peer(fl):
            return tuple(1 - a if f else a for a, f in zip((x, y, c), fl))

        loc = pltpu.make_async_copy(v_ref, out_ref.at[me], loc_sem)
        loc.start()
        sends = []
        for i, fl in enumerate(flips):
            cp = _remote(v_ref, out_ref.at[me], send_sem.at[i], recv_sem.at[i], peer(fl))
            cp.start()
            sends.append(cp)
        for i, fl in enumerate(flips):
            px, py, pc = peer(fl)
            _remote(v_ref, out_ref.at[4 * px + 2 * py + pc], send_sem.at[i], recv_sem.at[i], peer(fl)).wait_recv()
        for cp in sends:
            cp.wait_send()
        loc.wait()

    return pl.pallas_call(
        body, name=name, in_specs=[ANY], out_specs=ANY,
        out_shape=jax.ShapeDtypeStruct((N_DEV,) + shape, F32),
        scratch_shapes=[pltpu.SemaphoreType.DMA((N_DEV - 1,)), pltpu.SemaphoreType.DMA((N_DEV - 1,)),
                        pltpu.SemaphoreType.DMA],
    )(v)


def _ew_block(rows, cols):
    tc = cols if cols <= 4096 else _div(cols, 2048, LANE)
    tr = _div(rows, max(16, EW_BLOCK_ELEMS // tc), 16)
    return tr, tc


def _mesh_scalars():
    x, y, c = _position()
    return jnp.stack([c, 2 * x + y]).astype(jnp.int32)


def _grid_spec(grid, in_specs, out_specs):
    return pltpu.PrefetchScalarGridSpec(num_scalar_prefetch=1, grid=grid, in_specs=in_specs, out_specs=out_specs)


def _cast_into_full(parts, fname, pos, specs, place, name):
    kind, rows, cols = specs[fname]
    ws = [w for w in place if place[w][0] == fname]
    if kind == "col":
        stride = cols // N_CHIPS
        hr = rows // 2
        tr = _div(hr, max(16, EW_BLOCK_ELEMS // stride), 16)
        nrb = hr // tr
        in_specs = [pl.BlockSpec((tr, parts[w].shape[1]), lambda i, pos_ref: (i + pos_ref[0] * nrb, 0)) for w in ws]
        out_spec = pl.BlockSpec((tr, stride), lambda i, pos_ref: (i + pos_ref[0] * nrb, pos_ref[1]))
    else:
        stride = rows // N_CHIPS
        hc = cols // 2
        tr = _div(stride, max(16, EW_BLOCK_ELEMS // hc), 16)
        nrb = stride // tr
        in_specs = [pl.BlockSpec((tr, hc), lambda i, pos_ref: (i, pos_ref[0])) for w in ws]
        out_spec = pl.BlockSpec((tr, hc), lambda i, pos_ref: (i + pos_ref[1] * nrb, pos_ref[0]))

    def kern(pos_ref, *refs):
        o_ref = refs[-1]
        for w, r in zip(ws, refs[:-1]):
            off = place[w][1] if kind == "col" else 0
            o_ref[:, off:off + r.shape[1]] = r[...].astype(o_ref.dtype)

    return pl.pallas_call(
        kern, name=name, grid_spec=_grid_spec((nrb,), in_specs, out_spec),
        out_shape=jax.ShapeDtypeStruct((rows, cols), BF16),
        compiler_params=_cparams(("parallel",)),
    )(pos, *[parts[w] for w in ws])


def _pair_sum(grad, recv, pos, spec, name):
    kind, rows, cols = spec
    hr, hc = _half_shape(kind, rows, cols)
    tr, tc = _ew_block(hr, hc)
    nrb, ncb = hr // tr, hc // tc
    blk = pl.BlockSpec((tr, tc), lambda i, jj, pos_ref: (i, jj))
    if kind == "col":
        mine = pl.BlockSpec((tr, tc), lambda i, jj, pos_ref: (i + pos_ref[0] * nrb, jj))
    else:
        mine = pl.BlockSpec((tr, tc), lambda i, jj, pos_ref: (i, jj + pos_ref[0] * ncb))

    def kern(pos_ref, g_ref, r_ref, o_ref, slots_ref):
        o_ref[...] = (g_ref[...].astype(F32) + r_ref[...].astype(F32)).astype(o_ref.dtype)

    return pl.pallas_call(
        kern, name=name, grid_spec=_grid_spec((nrb, ncb), [mine, blk], [blk, ANY]),
        out_shape=[jax.ShapeDtypeStruct((hr, hc), BF16),
                   jax.ShapeDtypeStruct((N_CHIPS - 1,) + _slot_shape(spec), BF16)],
        compiler_params=_cparams(("parallel", "parallel")),
    )(pos, grad, recv)


def _chip_sum(pair_sum, slots, pos, fname, shard_shapes, specs, place, name):
    kind, rows, cols = specs[fname]
    sr, sc = _slot_shape(specs[fname])
    ws = [w for w in place if place[w][0] == fname]
    n_slots = N_CHIPS - 1
    tr = _div(sr, max(16, EW_BLOCK_ELEMS // sc), 16)
    nrb = sr // tr
    slot = pl.BlockSpec((n_slots, tr, sc), lambda i, pos_ref: (0, i, 0))
    if kind == "col":
        own = pl.BlockSpec((tr, sc), lambda i, pos_ref: (i, pos_ref[1]))
        out_specs = [pl.BlockSpec((tr, shard_shapes[w][1]), lambda i, pos_ref: (i + pos_ref[0] * nrb, 0)) for w in ws]
    else:
        own = pl.BlockSpec((tr, sc), lambda i, pos_ref: (i + pos_ref[1] * nrb, 0))
        out_specs = [pl.BlockSpec((tr, sc), lambda i, pos_ref: (i, pos_ref[0])) for w in ws]

    def kern(pos_ref, own_ref, slot_ref, *out_refs):
        tot = own_ref[...].astype(F32)
        for s in range(n_slots):
            tot = tot + slot_ref[s].astype(F32)
        for w, o_ref in zip(ws, out_refs):
            off = place[w][1] if kind == "col" else 0
            o_ref[...] = tot[:, off:off + o_ref.shape[1]]

    outs = pl.pallas_call(
        kern, name=name, grid_spec=_grid_spec((nrb,), [own, slot], out_specs),
        out_shape=[jax.ShapeDtypeStruct(shard_shapes[w], F32) for w in ws],
        compiler_params=_cparams(("parallel",)),
    )(pos, pair_sum, slots)
    return dict(zip(ws, outs))


def _adam_math(w, g, m, v):
    m2 = ADAM_B1 * m + (1.0 - ADAM_B1) * g
    v2 = ADAM_B2 * v + (1.0 - ADAM_B2) * (g * g)
    m_hat = m2 / (1.0 - ADAM_B1 ** ADAM_STEP)
    v_hat = v2 / (1.0 - ADAM_B2 ** ADAM_STEP)
    delta = -ADAM_LR * (m_hat / (jnp.sqrt(v_hat) + ADAM_EPS) + ADAM_WD * w)
    return delta, m2, v2


def _adamw(w, g, m, v, name, carried=()):
    rows, cols = w.shape
    tr, tc = _ew_block(rows, cols)

    def kern(w_ref, g_ref, m_ref, v_ref, d_ref, m2_ref, v2_ref):
        d_ref[...], m2_ref[...], v2_ref[...] = _adam_math(w_ref[...], g_ref[...], m_ref[...], v_ref[...])

    blk = pl.BlockSpec((tr, tc), lambda i, j: (i, j))
    return _pcall(
        kern, name=name, grid=(rows // tr, cols // tc), in_specs=[blk] * 4, out_specs=[blk] * 3,
        out_shape=[jax.ShapeDtypeStruct((rows, cols), F32)] * 3, args=(w, g, m, v),
        semantics=("parallel", "parallel"), carried=carried)


SMALL_ROWS = ("ln_mix_w", "ln_cross_w", "ln_mem_w", "ln_ffn_w", "ln_final_w")
ROW_HG_NORM, ROW_LB0, ROW_LB1 = 5, 6, 7
LOSS_LANE0 = HEAD_DIM


def _pack_small(vals):
    rows = [vals[n].reshape(1, D_MODEL) for n in SMALL_ROWS]
    pad = lambda a: jnp.pad(a, ((0, 0), (0, D_MODEL - a.shape[1])))
    rows.append(pad(vals["hg_norm_w"].reshape(1, HEAD_DIM)))
    rows.append(pad(vals["hg_lower_bounds"].reshape(2, HG_WIDTH)))
    return jnp.concatenate(rows, axis=0)


def _small_update(gathered, w, m, v, name="small_update"):
    def kern(g_ref, w_ref, m_ref, v_ref, grad_ref, d_ref, m2_ref, v2_ref, loss_ref):
        tot = g_ref[0]
        for s in range(1, N_DEV):
            tot = tot + g_ref[s]
        wv = w_ref[...]
        row = lax.broadcasted_iota(jnp.int32, (8, D_MODEL), 0)
        lane = lax.broadcasted_iota(jnp.int32, (8, D_MODEL), 1)
        l0, l1 = wv[ROW_LB0:ROW_LB0 + 1], wv[ROW_LB1:ROW_LB1 + 1]
        mx = jnp.maximum(l0, l1)
        e0, e1 = jnp.exp(l0 - mx), jnp.exp(l1 - mx)
        p0 = e0 / (e0 + e1)
        dlog = tot[ROW_LB0:ROW_LB0 + 1] * p0 * (1.0 - p0)
        tot = jnp.where(row == ROW_HG_NORM, tot + tot[ROW_LB1:ROW_LB1 + 1], tot)
        grad = jnp.where(row == ROW_LB0, dlog, jnp.where(row == ROW_LB1, -dlog, tot))
        grad = jnp.where((row == ROW_HG_NORM) & (lane >= HEAD_DIM), 0.0, grad)
        grad = jnp.where((row >= ROW_LB0) & (lane >= HG_WIDTH), 0.0, grad)
        grad_ref[...] = grad
        d_ref[...], m2_ref[...], v2_ref[...] = _adam_math(wv, grad, m_ref[...], v_ref[...])
        loss_ref[...] = tot[ROW_HG_NORM:ROW_HG_NORM + 1, LOSS_LANE0:LOSS_LANE0 + LANE]

    full = pl.BlockSpec((8, D_MODEL), lambda: (0, 0))
    return pl.pallas_call(
        kern, name=name,
        in_specs=[pl.BlockSpec((N_DEV, 8, D_MODEL), lambda: (0, 0, 0)), full, full, full],
        out_specs=[full, full, full, full, pl.BlockSpec((1, LANE), lambda: (0, 0))],
        out_shape=[jax.ShapeDtypeStruct((8, D_MODEL), F32)] * 4 + [jax.ShapeDtypeStruct((1, LANE), F32)],
        compiler_params=_cparams(),
    )(gathered, w, m, v)


def _unpack_small(p, shapes):
    out = {n: p[i].reshape(shapes[n]) for i, n in enumerate(SMALL_ROWS)}
    out["hg_norm_w"] = p[ROW_HG_NORM, :HEAD_DIM].reshape(shapes["hg_norm_w"])
    out["hg_lower_bounds"] = p[ROW_LB0:ROW_LB1 + 1, :HG_WIDTH].reshape(shapes["hg_lower_bounds"])
    return out


WHOLE = lambda f: (f, 0, None)
MID_MATRICES = ("w_branch_a", "w_branch_b", "w_out", "wq_cross", "wkv_cross", "wo_cross")
MID_WEIGHTS = MID_MATRICES
CARRY = {
    "mm_proj": [("gather", [WHOLE("w_branch_a"), WHOLE("w_branch_b"), WHOLE("w_out"), ("w13", 0, 96)])],
    "attn_fwd_g2": [("gather", [("w13", 96, 80)])],
    "hgrn_fwd": [("gather", [("w13", 176, 384)])],
    "mm_branch_b": [("gather", [WHOLE("wq_cross")])],
    "gate_fwd": [("gather", [WHOLE("wkv_cross")])],
    "mm_out": [("gather", [WHOLE("wo_cross"), ("w13", 560, 32)])],
    "mm_o": [("gather", [("w13", 592, 432)])],
    "mm_w13": [("gather", [WHOLE("w2")])],
    "swiglu_bwd": [("pairx", ["w2"])],
    "mm_dw13": [("chipx", [("w2", 0, 1056)])],
    "mm_dhf": [("pairx", ["w13"]), ("chipx", [("w2", 1056, 352)])],
    "mm_dmerged": [("chipx", [("w13", 0, 96)])],
    "mm_dwout": [("chipx", [("w13", 96, 96)])],
    "gate_bwd": [("chipx", [("w13", 192, 160)])],
    "hgrn_bwd": [("chipx", [("w13", 352, 528)])],
    "attn_merge_bwd": [("pairx", list(MID_MATRICES))],
    "attn_bwd_g0": [("chipx", [("w13", 880, 80)])],
    "attn_bwd_g1": [("chipx", [("w13", 960, 64)])],
    "attn_bwd_g2": [("chipx", [WHOLE("w_branch_a"), WHOLE("wq_cross"), WHOLE("wo_cross")])],
    "mm_dwin": [("chipx", [WHOLE("w_out"), WHOLE("w_branch_b"), WHOLE("wkv_cross")])],
    "mm_dh": [("chipx", [("w_in", 0, 384)]), ("share", ["w1", "w3", "w2"] + list(MID_WEIGHTS))],
    "adamw_w1": [("chipx", [("w_in", 384, 96)])],
    "adamw_w3": [("chipx", [("w_in", 480, 96)])],
    "adamw_w2": [("chipx", [("w_in", 576, 448)])],
}
AFTER = {
    "swiglu_bwd": [("pair_sum", "w2")],
    "mm_dhf": [("chip_sum", "w2"), ("pair_sum", "w13")],
    "attn_merge_bwd": [("pair_sum", f) for f in MID_MATRICES],
    "attn_bwd_g1": [("chip_sum", "w13")],
    "attn_bwd_g2": [("chip_sum", f) for f in ("w_branch_a", "wq_cross", "wo_cross")],
    "mm_dwin": [("chip_sum", f) for f in ("w_out", "w_branch_b", "wkv_cross")]
    + [("run", ("pairx", ["w_in"]), "rs_pair_exchange_w_in"), ("pair_sum", "w_in")],
    "adamw_w2": [("chip_sum", "w_in"), ("run", ("share", ["w_in"]), "rs_sibling_share_w_in")],
}


class _Net:
    def __init__(self, full, pos=None, shard_shapes=None, comm=True, specs=FULL_SPECS, place=WEIGHT_PLACE):
        self.full, self.pos, self.shard_shapes, self.comm = dict(full), pos, shard_shapes, comm
        self.specs, self.place = specs, place
        self.gw, self.recv, self.psum, self.slots, self.grads = {}, {}, {}, {}, {}

    def _make(self, kind, arg):
        if kind == "gather":
            return _gather_comm(self.full, arg, self.specs)
        if kind == "pairx":
            return _pairx_comm(self.gw, arg, self.specs)
        if kind == "chipx":
            return _chipx_comm(self.psum, self.slots, arg, self.specs)
        assert kind == "share"
        return _share_comm(self.grads, arg, self.specs, self.place)

    def _store(self, kind, res):
        if kind == "gather":
            self.full.update(res)
        elif kind == "pairx":
            for (tag, f), a in res.items():
                (self.gw if tag == "g" else self.recv)[f] = a
        elif kind == "chipx":
            for (tag, f), a in res.items():
                (self.psum if tag == "p" else self.slots)[f] = a
        else:
            self.grads.update(res)

    def run_comm(self, item, name):
        kind, arg = item
        self._store(kind, _run_comm([self._make(kind, arg)], name)[0])

    def call(self, fn, name, *args, grad_of=None, **kw):
        items = CARRY.get(name, []) if self.comm else []
        out, res = fn(*args, name=name, carried=[self._make(k, a) for k, a in items], **kw)
        if grad_of is not None:
            self.gw[grad_of] = out
        for (kind, _), r in zip(items, res):
            self._store(kind, r)
        for step in (AFTER.get(name, []) if self.comm else []):
            if step[0] == "pair_sum":
                f = step[1]
                self.psum[f], self.slots[f] = _pair_sum(self.gw[f], self.recv[f], self.pos, self.specs[f],
                                                        f"rs_pair_sum_{f}")
            elif step[0] == "chip_sum":
                f = step[1]
                self.grads.update(_chip_sum(self.psum[f], self.slots[f], self.pos, f, self.shard_shapes,
                                            self.specs, self.place, f"rs_chip_sum_{f}"))
            else:
                self.run_comm(step[1], step[2])
        return out


def _local_step(net, x, mem, target, small):
    full, call = net.full, net.call
    h = _rms_fwd(x, small["ln_mix_w"], "rms_mix")
    proj = call(_mm, "mm_proj", h, full["w_in"], mode="nn", out_dtype=F32)
    att = [call(_attn_fwd, f"attn_fwd_g{g}", proj, g) for g in range(3)]
    outs, lses = [a[0] for a in att], [a[1] for a in att]
    o_att = _attn_merge_fwd(outs, lses, "attn_merge")
    oraw, o_hg, states = call(_hg_fwd, "hgrn_fwd", proj, small["hg_lower_bounds"], small["hg_norm_w"])
    ya = call(_mm, "mm_branch_a", o_att, full["w_branch_a"], mode="nn", out_dtype=F32)
    yb = call(_mm, "mm_branch_b", o_hg, full["w_branch_b"], mode="nn", out_dtype=F32)
    merged = call(_gate_fwd, "gate_fwd", proj, ya, yb)
    x1 = call(_mm, "mm_out", merged, full["w_out"], mode="nn", out_dtype=F32, res=x)

    hc = _rms_fwd(x1, small["ln_cross_w"], "rms_cross")
    mn = _rms_fwd(mem, small["ln_mem_w"], "rms_mem")
    qc = call(_mm, "mm_q", hc, full["wq_cross"], mode="nn", out_dtype=F32)
    kvc = call(_mm, "mm_kv", mn, full["wkv_cross"], mode="nn", out_dtype=F32)
    oc = _cross_fwd(qc, kvc, "cross_fwd")
    x2 = call(_mm, "mm_o", oc, full["wo_cross"], mode="nn", out_dtype=F32, res=x1)

    hf = _rms_fwd(x2, small["ln_ffn_w"], "rms_ffn")
    ab = call(_mm, "mm_w13", hf, full["w13"], mode="nn", out_dtype=F32)
    u = _swiglu_fwd(ab, "swiglu_fwd")
    x3 = call(_mm, "mm_w2", u, full["w2"], mode="nn", out_dtype=F32, res=x2)

    dx3, dg_final, loss = _loss_head(x3, small["ln_final_w"], target, "loss_head")

    gs = {"ln_final_w": dg_final}
    du = call(_mm, "mm_du", dx3, full["w2"], mode="nt", out_dtype=F32)
    call(_mm, "mm_dw2", u, dx3, mode="tn", out_dtype=BF16, grad_of="w2")
    dab = call(_swiglu_bwd, "swiglu_bwd", ab, du)
    call(_mm, "mm_dw13", hf, dab, mode="tn", out_dtype=BF16, grad_of="w13")
    dhf = call(_mm, "mm_dhf", dab, full["w13"], mode="nt", out_dtype=F32)
    dx2, gs["ln_ffn_w"] = _rms_bwd(x2, small["ln_ffn_w"], dhf, dx3, "rms_ffn_bwd")
    doc = call(_mm, "mm_doc", dx2, full["wo_cross"], mode="nt", out_dtype=BF16)
    call(_mm, "mm_dwo", oc, dx2, mode="tn", out_dtype=BF16, grad_of="wo_cross")
    dqc, dkvc = _cross_bwd(qc, kvc, doc, "cross_bwd")
    call(_mm, "mm_dwq", hc, dqc, mode="tn", out_dtype=BF16, grad_of="wq_cross")
    dhc = call(_mm, "mm_dhc", dqc, full["wq_cross"], mode="nt", out_dtype=F32)
    call(_mm, "mm_dwkv", mn, dkvc, mode="tn", out_dtype=BF16, grad_of="wkv_cross")
    dmn = call(_mm, "mm_dmn", dkvc, full["wkv_cross"], mode="nt", out_dtype=F32)
    _, gs["ln_mem_w"] = _rms_bwd(mem, small["ln_mem_w"], dmn, None, "rms_mem_bwd")
    dx1, gs["ln_cross_w"] = _rms_bwd(x1, small["ln_cross_w"], dhc, dx2, "rms_cross_bwd")
    dmerged = call(_mm, "mm_dmerged", dx1, full["w_out"], mode="nt", out_dtype=F32)
    call(_mm, "mm_dwout", merged, dx1, mode="tn", out_dtype=BF16, grad_of="w_out")
    dya, dyb, dga, dgb = call(_gate_bwd, "gate_bwd", proj, ya, yb, dmerged)
    call(_mm, "mm_dwa", o_att, dya, mode="tn", out_dtype=BF16, grad_of="w_branch_a")
    do_att = call(_mm, "mm_doatt", dya, full["w_branch_a"], mode="nt", out_dtype=F32)
    call(_mm, "mm_dwb", o_hg, dyb, mode="tn", out_dtype=BF16, grad_of="w_branch_b")
    do_hg = call(_mm, "mm_dohg", dyb, full["w_branch_b"], mode="nt", out_dtype=F32)
    dqh, dfh, dih, dgh, dlb, gs["hg_norm_w"] = call(
        _hg_bwd, "hgrn_bwd", proj, small["hg_lower_bounds"], small["hg_norm_w"], oraw, states, do_hg)
    gs["hg_lb"] = dlb
    do_gs, dl_gs = call(_attn_merge_bwd, "attn_merge_bwd", outs, lses, do_att)
    dqs, dks, dvs = zip(*[call(_attn_bwd, f"attn_bwd_g{g}", proj, g, lses[g], do_gs[g], dl_gs[g]) for g in range(3)])
    dproj = jnp.concatenate([*dqs, *dks, *dvs, dqh, dfh, dih, dgh, dga, dgb], axis=1)
    call(_mm, "mm_dwin", h, dproj, mode="tn", out_dtype=BF16, grad_of="w_in")
    dh = call(_mm, "mm_dh", dproj, full["w_in"], mode="nt", out_dtype=F32)
    dx, gs["ln_mix_w"] = _rms_bwd(x, small["ln_mix_w"], dh, dx1, "rms_mix_bwd")
    return loss, dx, gs


WEIGHT_ORDER = ("ln_mix_w", "w_in", "hg_norm_w", "hg_lower_bounds", "w_branch_a", "w_branch_b", "w_out",
                "ln_cross_w", "ln_mem_w", "wq_cross", "wkv_cross", "wo_cross", "ln_ffn_w", "w1", "w3", "w2",
                "ln_final_w")
ADAM_ORDER = ("w1", "w3", "w2", "w_in") + MID_WEIGHTS


def kernel(x, mem, ln_mix_w, w_in, hg_norm_w, hg_lower_bounds, w_branch_a, w_branch_b, w_out, ln_cross_w, ln_mem_w, wq_cross, wkv_cross, wo_cross, ln_ffn_w, w1, w3, w2, ln_final_w, loss_target, m_ln_mix_w, m_w_in, m_hg_norm_w, m_hg_lower_bounds, m_w_branch_a, m_w_branch_b, m_w_out, m_ln_cross_w, m_ln_mem_w, m_wq_cross, m_wkv_cross, m_wo_cross, m_ln_ffn_w, m_w1, m_w3, m_w2, m_ln_final_w, v_ln_mix_w, v_w_in, v_hg_norm_w, v_hg_lower_bounds, v_w_branch_a, v_w_branch_b, v_w_out, v_ln_cross_w, v_ln_mem_w, v_wq_cross, v_wkv_cross, v_wo_cross, v_ln_ffn_w, v_w1, v_w3, v_w2, v_ln_final_w):
    args = dict(locals())
    w = {n: args[n] for n in WEIGHT_ORDER}
    m = {n: args["m_" + n] for n in WEIGHT_ORDER}
    v = {n: args["v_" + n] for n in WEIGHT_ORDER}
    shapes = {n: w[n].shape for n in WEIGHT_ORDER}
    mat = lambda a: a.reshape(a.shape[-2:])
    shard_shapes = {n: shapes[n][-2:] for n in BIG_WEIGHTS}

    pos = _mesh_scalars()
    fulls = {f: _cast_into_full({n: mat(w[n]) for n in BIG_WEIGHTS if WEIGHT_PLACE[n][0] == f}, f, pos,
                                FULL_SPECS, WEIGHT_PLACE, f"cast_{f}") for f in FULL_SPECS}
    net = _Net(fulls, pos, shard_shapes)
    net.run_comm(("gather", [WHOLE("w_in")]), "gather_w_in")
    small = {n: w[n].reshape(1, -1) for n in SMALL_ROWS}
    small["hg_norm_w"] = w["hg_norm_w"].reshape(1, HEAD_DIM)
    small["hg_lower_bounds"] = w["hg_lower_bounds"]
    loss, dx, gs = _local_step(net, x.reshape(SEQ, D_MODEL), mem.reshape(MEM_LEN, D_MODEL),
                               loss_target.reshape(SEQ, D_MODEL), small)

    out_g, out_d, out_m, out_v = {}, {}, {}, {}
    for n in ADAM_ORDER:
        out_d[n], out_m[n], out_v[n] = net.call(_adamw, f"adamw_{n}", mat(w[n]), net.grads[n], mat(m[n]), mat(v[n]))
        out_g[n] = net.grads[n]

    pad = lambda a: jnp.pad(a, ((0, 0), (0, D_MODEL - a.shape[1])))
    part = jnp.concatenate(
        [gs[n] for n in SMALL_ROWS]
        + [pad(jnp.concatenate([gs["hg_norm_w"][0], loss], axis=1)), pad(gs["hg_lb"]),
           pad(gs["hg_norm_w"][1])], axis=0)
    sg, sd, sm, sv, loss_tot = _small_update(_gather_rows(part), _pack_small(w), _pack_small(m), _pack_small(v))
    for dst, packed in ((out_g, sg), (out_d, sd), (out_m, sm), (out_v, sv)):
        dst.update(_unpack_small(packed, shapes))

    result = [loss_tot[0, 0], dx.reshape(x.shape)]
    for group in (out_g, out_d, out_m, out_v):
        result += [group[n].reshape(shapes[n]) for n in WEIGHT_ORDER]
    return tuple(result)
```

```python
import math

import jax
import jax.numpy as jnp
from jax import lax
from jax.experimental import pallas as pl
from jax.experimental.pallas import tpu as pltpu

F32 = jnp.float32
BF16 = jnp.bfloat16
MESH = pl.DeviceIdType.MESH

D_MODEL = 2048
SEQ = 2048
HEAD_DIM = 128
MEM_LEN = 256
ATT_GROUPS = ((128, 1), (512, 4), (2048, 16))
ATT_HEADS = 4
ATT_WIDTH = 3 * ATT_HEADS * HEAD_DIM
ATT_OUT = ATT_HEADS * HEAD_DIM
ATT_BLOCK = 128
HG_HEADS = 8
HG_WIDTH = HG_HEADS * HEAD_DIM
HG_CHUNK = 64
IN_WIDTH = 3 * ATT_WIDTH + 4 * HG_WIDTH + 2 * D_MODEL
CROSS_HEADS = 4
CROSS_WIDTH = CROSS_HEADS * HEAD_DIM
D_FF = 5632
RMS_EPS = 1e-6
ADAM_LR = 0.001
ADAM_B1 = 0.9
ADAM_B2 = 0.999
ADAM_EPS = 1e-08
ADAM_WD = 0.01
ADAM_STEP = 10
N_CHIPS = 4
N_DEV = 8

VMEM_LIMIT_BYTES = 56 * 1024 * 1024
LANE = 128
MXU_WIDTH = 256
MM_TILE_CAP = 1536
TRANSPOSE_CHUNK = 512
ANY = pl.BlockSpec(memory_space=pl.ANY)


def _cparams(sem=None):
    return pltpu.CompilerParams(dimension_semantics=sem, vmem_limit_bytes=VMEM_LIMIT_BYTES)


def _div(n, cap, mult):
    best = None
    for d in range(mult, min(n, cap) + 1, mult):
        if n % d == 0:
            best = d
    assert best is not None, (n, cap, mult)
    return best


def _sigmoid(x):
    return 1.0 / (1.0 + jnp.exp(-x))


def _dot(a, b):
    return jnp.dot(a.astype(BF16), b.astype(BF16), preferred_element_type=F32)


def _dot_nt(a, b):
    return lax.dot_general(a.astype(BF16), b.astype(BF16), (((1,), (1,)), ((), ())),
                           preferred_element_type=F32)


def _dot_tn(a, b):
    return jnp.dot(a.astype(F32).T.astype(BF16), b.astype(BF16), preferred_element_type=F32)


def _dot_exact(a, b):
    return jnp.dot(a, b, precision=lax.Precision.HIGHEST, preferred_element_type=F32)


class _Carried:
    def __init__(self, arrays, fresh, n_sems, start, finish, mid=None):
        self.arrays, self.fresh, self.n_sems = arrays, fresh, n_sems
        self.start, self.mid, self.finish = start, mid, finish


class _Token:
    def __init__(self, array):
        self.array = array


TOKEN_SHAPE = (8, LANE)


def _carried_layout(carried):
    akeys = list(dict.fromkeys(k for cm in carried for k in cm.arrays))
    fkeys = [(ci, k) for ci, cm in enumerate(carried) for k in cm.fresh]
    arrays = [next(cm.arrays[k] for cm in carried if k in cm.arrays) for k in akeys]
    shapes = [jax.ShapeDtypeStruct(a.shape, a.dtype) for a in arrays] + [carried[ci].fresh[k] for ci, k in fkeys]
    sems = []
    for cm in carried:
        sems += [pltpu.SemaphoreType.DMA((cm.n_sems,)), pltpu.SemaphoreType.DMA((cm.n_sems,))]
    return akeys, fkeys, arrays, shapes, sems


def _carried_results(carried, akeys, fkeys, outs):
    shared = dict(zip(akeys, outs[:len(akeys)]))
    res = [{k: shared[k] for k in cm.arrays} for cm in carried]
    for (ci, k), o in zip(fkeys, outs[len(akeys):]):
        res[ci][k] = o
    return res


def _pcall(kern, *, name, grid, in_specs, out_specs, out_shape, args, scratch_shapes=(), semantics=None,
           carried=()):
    tokens = [c.array for c in carried if isinstance(c, _Token)]
    carried = [c for c in carried if not isinstance(c, _Token)]
    single = not isinstance(out_shape, (list, tuple))
    out_specs = [out_specs] if single else list(out_specs)
    out_shape = [out_shape] if single else list(out_shape)
    n_real, n_out, n_scr = len(in_specs), len(out_shape), len(scratch_shapes)
    in_specs = list(in_specs) + [pl.BlockSpec(TOKEN_SHAPE, lambda *_: (0, 0))] * len(tokens)
    args = list(args) + tokens
    n_in = len(in_specs)
    if not carried:
        def plain(*refs):
            kern(*refs[:n_real], *refs[n_in:])

        outs = pl.pallas_call(plain if tokens else kern, name=name, grid=grid, in_specs=in_specs,
                              out_specs=out_specs, out_shape=out_shape, scratch_shapes=list(scratch_shapes),
                              compiler_params=_cparams(semantics))(*args)
        return (outs[0] if single else list(outs)), []
    akeys, fkeys, arrays, shapes, sems = _carried_layout(carried)
    n_a, n_f = len(akeys), len(fkeys)
    total = math.prod(grid)
    mid_step = min(total - 1, (17 * total) // 20)

    def wrapped(*refs):
        ins = refs[:n_real]
        o0 = n_in + n_a
        outs = refs[o0:o0 + n_out]
        a0 = o0 + n_out
        s0 = a0 + n_a + n_f
        per = _carried_results(carried, akeys, fkeys, refs[a0:s0])
        scratch = refs[s0:s0 + n_scr]
        sem = refs[s0 + n_scr:]
        step = 0
        for d, g in enumerate(grid):
            step = step * g + pl.program_id(d)

        @pl.when(step == 0)
        def _():
            for ci, cm in enumerate(carried):
                cm.start(per[ci], sem[2 * ci], sem[2 * ci + 1])

        kern(*ins, *outs, *scratch)

        @pl.when(step == mid_step)
        def _():
            for ci, cm in enumerate(carried):
                if cm.mid is not None:
                    cm.mid(per[ci], sem[2 * ci], sem[2 * ci + 1])

        @pl.when(step == total - 1)
        def _():
            for ci, cm in enumerate(carried):
                cm.finish(per[ci], sem[2 * ci], sem[2 * ci + 1])

    outs = pl.pallas_call(
        wrapped, name=name, grid=grid,
        in_specs=list(in_specs) + [ANY] * n_a, out_specs=out_specs + [ANY] * (n_a + n_f),
        out_shape=out_shape + shapes,
        input_output_aliases={n_in + i: n_out + i for i in range(n_a)},
        scratch_shapes=list(scratch_shapes) + sems,
        compiler_params=_cparams(("arbitrary",) * len(grid)),
    )(*args, *arrays)
    res = _carried_results(carried, akeys, fkeys, outs[n_out:])
    return (outs[0] if single else list(outs[:n_out])), res


def _run_comm(carried, name):
    carried = list(carried)
    akeys, fkeys, arrays, shapes, sems = _carried_layout(carried)
    n_a, n_f = len(akeys), len(fkeys)

    def body(*refs):
        per = _carried_results(carried, akeys, fkeys, refs[n_a:2 * n_a + n_f])
        sem = refs[2 * n_a + n_f:]
        for hook in ("start", "mid", "finish"):
            for ci, cm in enumerate(carried):
                fn = getattr(cm, hook)
                if fn is not None:
                    fn(per[ci], sem[2 * ci], sem[2 * ci + 1])

    outs = pl.pallas_call(
        body, name=name, in_specs=[ANY] * n_a, out_specs=[ANY] * (n_a + n_f), out_shape=shapes,
        input_output_aliases={i: i for i in range(n_a)}, scratch_shapes=sems,
    )(*arrays)
    return _carried_results(carried, akeys, fkeys, outs)


HBM_SPEC = pl.BlockSpec(memory_space=pltpu.HBM)
SEM_SPEC = pl.BlockSpec(memory_space=pltpu.SEMAPHORE)
SPLIT_EFFECT = pltpu.SideEffectType.DATAFLOW_SIDE_EFFECTING


def _in_hbm(a):
    return pltpu.with_memory_space_constraint(a, pltpu.HBM)


def _split_start(items, name):
    items = list(items)
    akeys, fkeys, arrays, shapes, sems = _carried_layout(items)
    assert not fkeys
    n_a, n_s = len(akeys), len(sems)

    def body(*refs):
        per = _carried_results(items, akeys, [], refs[n_a:2 * n_a])
        sem = refs[2 * n_a:2 * n_a + n_s]
        for ci, cm in enumerate(items):
            cm.start(per[ci], sem[2 * ci], sem[2 * ci + 1])
        token = refs[2 * n_a + n_s]
        token[...] = jnp.zeros_like(token)

    outs = pl.pallas_call(
        body, name=name, in_specs=[HBM_SPEC] * n_a,
        out_specs=[HBM_SPEC] * n_a + [SEM_SPEC] * n_s + [pl.BlockSpec(memory_space=pltpu.VMEM)],
        out_shape=[pltpu.HBM(s.shape, s.dtype) for s in shapes] + sems + [jax.ShapeDtypeStruct(TOKEN_SHAPE, F32)],
        input_output_aliases={i: i for i in range(n_a)},
        compiler_params=pltpu.CompilerParams(has_side_effects=SPLIT_EFFECT),
    )(*[_in_hbm(a) for a in arrays])
    res = _carried_results(items, akeys, [], outs[:n_a])
    sem_out = outs[n_a:n_a + n_s]
    return res, [(sem_out[2 * ci], sem_out[2 * ci + 1]) for ci in range(len(items))], outs[-1]


def _split_wait(items, sems, after, name):
    items = list(items)
    akeys, fkeys, arrays, shapes, _ = _carried_layout(items)
    n_a, n_s = len(akeys), 2 * len(items)

    def body(*refs):
        per = _carried_results(items, akeys, [], refs[n_a + n_s + 1:])
        sem = refs[n_a:n_a + n_s]
        for ci, cm in enumerate(items):
            cm.finish(per[ci], sem[2 * ci], sem[2 * ci + 1])

    outs = pl.pallas_call(
        body, name=name, in_specs=[HBM_SPEC] * n_a + [SEM_SPEC] * n_s + [ANY],
        out_specs=[HBM_SPEC] * n_a, out_shape=[pltpu.HBM(s.shape, s.dtype) for s in shapes],
        input_output_aliases={i: i for i in range(n_a)},
        compiler_params=pltpu.CompilerParams(has_side_effects=SPLIT_EFFECT),
    )(*arrays, *[s for pair in sems for s in pair], after)
    return _carried_results(items, akeys, [], outs)


def _mm(a, b, *, mode, out_dtype, name, res=None, carried=()):
    if mode == "nn":
        (m, k), (k2, n) = a.shape, b.shape
    elif mode == "nt":
        (m, k), (n, k2) = a.shape, b.shape
    else:
        (k, m), (k2, n) = a.shape, b.shape
    assert k == k2, (name, a.shape, b.shape)
    tm = _div(m, MM_TILE_CAP, LANE)
    tn = _div(n, MM_TILE_CAP, MXU_WIDTH) if n % MXU_WIDTH == 0 else 0
    if tn < 1024:
        tn = _div(n, MM_TILE_CAP, LANE)
    out_shape = jax.ShapeDtypeStruct((m, n), out_dtype)

    if mode == "tn":
        assert res is None

        def kern_tn(a_ref, b_ref, o_ref, at_ref):
            @pl.when(pl.program_id(1) == 0)
            def _():
                step = min(TRANSPOSE_CHUNK, k)
                for c0 in range(0, k, step):
                    at_ref[:, c0:c0 + step] = a_ref[c0:c0 + step, :].astype(F32).T.astype(BF16)

            o_ref[...] = jnp.dot(at_ref[...], b_ref[...].astype(BF16),
                                 preferred_element_type=F32).astype(o_ref.dtype)

        return _pcall(
            kern_tn, name=name, grid=(m // tm, n // tn),
            in_specs=[pl.BlockSpec((k, tm), lambda i, j: (0, i)),
                      pl.BlockSpec((k, tn), lambda i, j: (0, j))],
            out_specs=pl.BlockSpec((tm, tn), lambda i, j: (i, j)),
            out_shape=out_shape, args=(a, b),
            scratch_shapes=[pltpu.VMEM((tm, k), BF16)],
            semantics=("parallel", "arbitrary"), carried=carried)

    tk = k if k <= 2048 else _div(k, 3072, LANE)
    nk = k // tk
    a_spec = pl.BlockSpec((tm, tk), lambda i, j, kk: (i, kk))
    if mode == "nn":
        b_spec = pl.BlockSpec((tk, tn), lambda i, j, kk: (kk, j))
        dot = _dot
    else:
        b_spec = pl.BlockSpec((tn, tk), lambda i, j, kk: (j, kk))
        dot = _dot_nt
    o_spec = pl.BlockSpec((tm, tn), lambda i, j, kk: (i, j))
    in_specs = [a_spec, b_spec]
    args = [a, b]
    if res is not None:
        in_specs.append(o_spec)
        args.append(res)
    has_res = res is not None

    def kern(*refs):
        a_ref, b_ref = refs[0], refs[1]
        r_ref = refs[2] if has_res else None
        o_ref = refs[3] if has_res else refs[2]
        part = dot(a_ref[...], b_ref[...])
        if nk == 1:
            if has_res:
                part = part + r_ref[...]
            o_ref[...] = part.astype(o_ref.dtype)
            return
        acc_ref = refs[-1]
        kk = pl.program_id(2)

        @pl.when(kk == 0)
        def _():
            acc_ref[...] = part

        @pl.when(kk > 0)
        def _():
            acc_ref[...] += part

        @pl.when(kk == nk - 1)
        def _():
            tot = acc_ref[...]
            if has_res:
                tot = tot + r_ref[...]
            o_ref[...] = tot.astype(o_ref.dtype)

    return _pcall(
        kern, name=name, grid=(m // tm, n // tn, nk),
        in_specs=in_specs, out_specs=o_spec, out_shape=out_shape, args=args,
        scratch_shapes=[pltpu.VMEM((tm, tn), F32)] if nk > 1 else [],
        semantics=("parallel", "parallel", "arbitrary"), carried=carried)


ROW_BLOCK = 256


def _rms_fwd(x, g, name):
    t, d = x.shape
    tr = min(ROW_BLOCK, t)

    def kern(x_ref, g_ref, o_ref):
        xf = x_ref[...]
        r = lax.rsqrt(jnp.mean(xf * xf, axis=-1, keepdims=True) + RMS_EPS)
        o_ref[...] = (xf * r * g_ref[...]).astype(o_ref.dtype)

    return pl.pallas_call(
        kern, name=name, grid=(t // tr,),
        in_specs=[pl.BlockSpec((tr, d), lambda i: (i, 0)), pl.BlockSpec((1, d), lambda i: (0, 0))],
        out_specs=pl.BlockSpec((tr, d), lambda i: (i, 0)),
        out_shape=jax.ShapeDtypeStruct((t, d), BF16),
        compiler_params=_cparams(("parallel",)),
    )(x, g)


def _rms_bwd(x, g, dh, res, name):
    t, d = x.shape
    tr = min(ROW_BLOCK, t)
    has_res = res is not None

    def kern(*refs):
        x_ref, g_ref, dh_ref = refs[:3]
        r_ref = refs[3] if has_res else None
        dx_ref, dg_ref = refs[-2], refs[-1]
        xf = x_ref[...]
        r = lax.rsqrt(jnp.mean(xf * xf, axis=-1, keepdims=True) + RMS_EPS)
        xn = xf * r
        dh_ = dh_ref[...]
        dhg = dh_ * g_ref[...]
        dx = r * (dhg - xn * jnp.mean(dhg * xn, axis=-1, keepdims=True))
        if has_res:
            dx = dx + r_ref[...]
        dx_ref[...] = dx
        part = jnp.sum(dh_ * xn, axis=0, keepdims=True)

        @pl.when(pl.program_id(0) == 0)
        def _():
            dg_ref[...] = part

        @pl.when(pl.program_id(0) > 0)
        def _():
            dg_ref[...] += part

    row = pl.BlockSpec((tr, d), lambda i: (i, 0))
    vec = pl.BlockSpec((1, d), lambda i: (0, 0))
    in_specs = [row, vec, row] + ([row] if has_res else [])
    args = [x, g, dh] + ([res] if has_res else [])
    return pl.pallas_call(
        kern, name=name, grid=(t // tr,), in_specs=in_specs, out_specs=[row, vec],
        out_shape=[jax.ShapeDtypeStruct((t, d), F32), jax.ShapeDtypeStruct((1, d), F32)],
        compiler_params=_cparams(("arbitrary",)),
    )(*args)


def _loss_head(x3, g, target, name):
    t, d = x3.shape
    tr = ROW_BLOCK

    def kern(x_ref, g_ref, t_ref, dx_ref, dg_ref, loss_ref):
        xf = x_ref[...]
        r = lax.rsqrt(jnp.mean(xf * xf, axis=-1, keepdims=True) + RMS_EPS)
        xn = xf * r
        gg = g_ref[...]
        err = xn * gg - t_ref[...]
        lpart = 0.5 * jnp.sum(jnp.mean(err * err, axis=-1, keepdims=True), axis=0, keepdims=True)
        dy = err * (1.0 / d)
        dyg = dy * gg
        dx_ref[...] = r * (dyg - xn * jnp.mean(dyg * xn, axis=-1, keepdims=True))
        gpart = jnp.sum(dy * xn, axis=0, keepdims=True)
        lrow = jnp.broadcast_to(lpart, (1, LANE))

        @pl.when(pl.program_id(0) == 0)
        def _():
            dg_ref[...] = gpart
            loss_ref[...] = lrow

        @pl.when(pl.program_id(0) > 0)
        def _():
            dg_ref[...] += gpart
            loss_ref[...] += lrow

    row = pl.BlockSpec((tr, d), lambda i: (i, 0))
    vec = pl.BlockSpec((1, d), lambda i: (0, 0))
    return pl.pallas_call(
        kern, name=name, grid=(t // tr,), in_specs=[row, vec, row],
        out_specs=[row, vec, pl.BlockSpec((1, LANE), lambda i: (0, 0))],
        out_shape=[jax.ShapeDtypeStruct((t, d), F32), jax.ShapeDtypeStruct((1, d), F32),
                   jax.ShapeDtypeStruct((1, LANE), F32)],
        compiler_params=_cparams(("arbitrary",)),
    )(x3, g, target)


ATT_SCALE = HEAD_DIM ** -0.5
Q_BLOCK0, K_BLOCK0, V_BLOCK0 = 0, ATT_WIDTH // HEAD_DIM, 2 * ATT_WIDTH // HEAD_DIM


def _residue_rows(dil, r, n):
    if dil == 1:
        return pl.ds(n * ATT_BLOCK, ATT_BLOCK)
    return pl.ds(n * ATT_BLOCK * dil + r, ATT_BLOCK, stride=dil)


def _band_mask(with_prev):
    width = 2 * ATT_BLOCK if with_prev else ATT_BLOCK
    iq = lax.broadcasted_iota(jnp.int32, (ATT_BLOCK, width), 0)
    ik = lax.broadcasted_iota(jnp.int32, (ATT_BLOCK, width), 1)
    if not with_prev:
        return ik <= iq
    return ((ik < ATT_BLOCK) & (iq <= ik)) | ((ik >= ATT_BLOCK) & ((ik - ATT_BLOCK) <= iq))


def _band_keys(ref, dil, r, n):
    own = ref[_residue_rows(dil, r, n), :]
    if n == 0:
        return own
    return jnp.concatenate([ref[_residue_rows(dil, r, n - 1), :], own], axis=0)


def _attn_col_spec(base, grp):
    return pl.BlockSpec((SEQ, HEAD_DIM), lambda h: (0, base + grp * ATT_HEADS + h))


def _attn_fwd(proj, grp, name, carried=()):
    _, dil = ATT_GROUPS[grp]
    nb = SEQ // dil // ATT_BLOCK

    def kern(q_ref, k_ref, v_ref, o_ref, lse_ref):
        for r in range(dil):
            for n in range(nb):
                rows = _residue_rows(dil, r, n)
                s = _dot_nt(q_ref[rows, :], _band_keys(k_ref, dil, r, n)) * ATT_SCALE
                s = jnp.where(_band_mask(n > 0), s, -jnp.inf)
                m = jnp.max(s, axis=-1, keepdims=True)
                p = jnp.exp(s - m)
                l = jnp.sum(p, axis=-1, keepdims=True)
                o_ref[rows, :] = _dot(p / l, _band_keys(v_ref, dil, r, n))
                lse_ref[rows, :] = jnp.broadcast_to(m + jnp.log(l), (ATT_BLOCK, HEAD_DIM))

    out_spec = pl.BlockSpec((SEQ, HEAD_DIM), lambda h: (0, h))
    return _pcall(
        kern, name=name, grid=(ATT_HEADS,),
        in_specs=[_attn_col_spec(Q_BLOCK0, grp), _attn_col_spec(K_BLOCK0, grp), _attn_col_spec(V_BLOCK0, grp)],
        out_specs=[out_spec, out_spec],
        out_shape=[jax.ShapeDtypeStruct((SEQ, ATT_OUT), F32)] * 2, args=(proj, proj, proj),
        semantics=("parallel",), carried=carried)


def _attn_weights(l0, l1, l2):
    mx = jnp.maximum(jnp.maximum(l0, l1), l2)
    e0, e1, e2 = jnp.exp(l0 - mx), jnp.exp(l1 - mx), jnp.exp(l2 - mx)
    den = e0 + e1 + e2
    return e0 / den, e1 / den, e2 / den


def _attn_merge_fwd(outs, lses, name):
    tr = ROW_BLOCK

    def kern(o0, o1, o2, l0, l1, l2, out_ref):
        a0, a1, a2 = _attn_weights(l0[...], l1[...], l2[...])
        out_ref[...] = (a0 * o0[...] + a1 * o1[...] + a2 * o2[...]).astype(out_ref.dtype)

    spec = pl.BlockSpec((tr, ATT_OUT), lambda i: (i, 0))
    return pl.pallas_call(
        kern, name=name, grid=(SEQ // tr,), in_specs=[spec] * 6, out_specs=spec,
        out_shape=jax.ShapeDtypeStruct((SEQ, ATT_OUT), BF16),
        compiler_params=_cparams(("parallel",)),
    )(*outs, *lses)


def _attn_merge_bwd(outs, lses, do_att, name, carried=()):
    tr = ROW_BLOCK

    def kern(o0, o1, o2, l0, l1, l2, do_ref, d0, d1, d2, t0, t1, t2):
        alphas = _attn_weights(l0[...], l1[...], l2[...])
        do = do_ref[...]
        o_att = alphas[0] * o0[...] + alphas[1] * o1[...] + alphas[2] * o2[...]
        prod = do * o_att
        parts = []
        for h in range(ATT_HEADS):
            sl = slice(h * HEAD_DIM, (h + 1) * HEAD_DIM)
            tot = jnp.sum(prod[:, sl], axis=-1, keepdims=True)
            parts.append(jnp.broadcast_to(tot, (tr, HEAD_DIM)))
        dd = jnp.concatenate(parts, axis=1)
        for a, d_ref, t_ref in zip(alphas, (d0, d1, d2), (t0, t1, t2)):
            d_ref[...] = a * do
            t_ref[...] = -a * dd

    spec = pl.BlockSpec((tr, ATT_OUT), lambda i: (i, 0))
    res, cres = _pcall(
        kern, name=name, grid=(SEQ // tr,), in_specs=[spec] * 7, out_specs=[spec] * 6,
        out_shape=[jax.ShapeDtypeStruct((SEQ, ATT_OUT), F32)] * 6, args=(*outs, *lses, do_att),
        semantics=("parallel",), carried=carried)
    return (res[:3], res[3:]), cres


def _attn_bwd(proj, grp, lse, do_g, dl_g, name, carried=()):
    _, dil = ATT_GROUPS[grp]
    nb = SEQ // dil // ATT_BLOCK

    def kern(q_ref, k_ref, v_ref, do_ref, lse_ref, dl_ref, dq_ref, dk_ref, dv_ref, dq_acc, dk_acc, dv_acc):
        dk_acc[...] = jnp.zeros_like(dk_acc)
        dv_acc[...] = jnp.zeros_like(dv_acc)
        for r in range(dil):
            for n in range(nb):
                rows = _residue_rows(dil, r, n)
                q, do = q_ref[rows, :], do_ref[rows, :]
                kk, vv = _band_keys(k_ref, dil, r, n), _band_keys(v_ref, dil, r, n)
                s = _dot_nt(q, kk) * ATT_SCALE
                p = jnp.where(_band_mask(n > 0), jnp.exp(s - lse_ref[rows, :][:, :1]), 0.0)
                ds = p * (_dot_nt(do, vv) + dl_ref[rows, :][:, :1])
                dq_acc[rows, :] = _dot(ds, kk) * ATT_SCALE
                dk = _dot_tn(ds, q) * ATT_SCALE
                dv = _dot_tn(p, do)
                if n > 0:
                    prev = _residue_rows(dil, r, n - 1)
                    dk_acc[prev, :] += dk[:ATT_BLOCK]
                    dv_acc[prev, :] += dv[:ATT_BLOCK]
                    dk, dv = dk[ATT_BLOCK:], dv[ATT_BLOCK:]
                dk_acc[rows, :] += dk
                dv_acc[rows, :] += dv
        dq_ref[...] = dq_acc[...].astype(dq_ref.dtype)
        dk_ref[...] = dk_acc[...].astype(dk_ref.dtype)
        dv_ref[...] = dv_acc[...].astype(dv_ref.dtype)

    spec = pl.BlockSpec((SEQ, HEAD_DIM), lambda h: (0, h))
    return _pcall(
        kern, name=name, grid=(ATT_HEADS,),
        in_specs=[_attn_col_spec(Q_BLOCK0, grp), _attn_col_spec(K_BLOCK0, grp), _attn_col_spec(V_BLOCK0, grp),
                  spec, spec, spec],
        out_specs=[spec] * 3,
        out_shape=[jax.ShapeDtypeStruct((SEQ, ATT_OUT), BF16)] * 3, args=(proj, proj, proj, do_g, lse, dl_g),
        scratch_shapes=[pltpu.VMEM((SEQ, HEAD_DIM), F32)] * 3,
        semantics=("parallel",), carried=carried)


HG_HEADS_PER_STEP = 4
HG_BLOCK_W = HG_HEADS_PER_STEP * HEAD_DIM
HG_Q_BLK = (3 * ATT_WIDTH) // HG_BLOCK_W
HG_N_CHUNKS = SEQ // HG_CHUNK
HG_MID = HG_CHUNK // 2


def _lower_bound(lb_ref, sl):
    l0, l1 = lb_ref[0:1, sl], lb_ref[1:2, sl]
    mx = jnp.maximum(l0, l1)
    e0, e1 = jnp.exp(l0 - mx), jnp.exp(l1 - mx)
    return e0 / (e0 + e1)


def _tri(lower):
    i = lax.broadcasted_iota(jnp.int32, (HG_CHUNK, HG_CHUNK), 0)
    j = lax.broadcasted_iota(jnp.int32, (HG_CHUNK, HG_CHUNK), 1)
    return (i >= j) if lower else (i <= j)


def _hg_chunk_terms(qh, fh, lb):
    sig = _sigmoid(fh)
    f = lb + (1.0 - lb) * sig
    k = 1.0 - f
    b = _dot_exact(_tri(True).astype(F32), jnp.log(f))
    bl = b[HG_CHUNK - 1:HG_CHUNK, :]
    br = b[HG_MID:HG_MID + 1, :]
    sq = _sigmoid(qh)
    q = qh * sq
    return dict(sig=sig, f=f, k=k, b=b, bl=bl, br=br, sq=sq, q=q,
                e1=jnp.exp(bl - b), e2=jnp.exp(b), e3=jnp.exp(b - br), e4=jnp.exp(br - b))


def _hg_fwd(proj, lbw, normw, name, carried=()):
    def in_blk(off):
        return pl.BlockSpec((HG_CHUNK, HG_BLOCK_W), lambda hp, n: (n, HG_Q_BLK + off + hp))

    def kern(q_ref, f_ref, i_ref, g_ref, lb_ref, nw_ref, oraw_ref, ohg_ref, st_ref, state):
        @pl.when(pl.program_id(1) == 0)
        def _():
            state[...] = jnp.zeros_like(state)

        causal = _tri(True)
        for hd in range(HG_HEADS_PER_STEP):
            sl = slice(hd * HEAD_DIM, (hd + 1) * HEAD_DIM)
            t = _hg_chunk_terms(q_ref[:, sl], f_ref[:, sl], _lower_bound(lb_ref, sl))
            v = i_ref[:, sl]
            st = state[hd]
            st_ref[0, hd] = st
            kd = t["k"] * t["e1"]
            inter = _dot_nt(t["q"] * t["e2"], st)
            a = jnp.where(causal, _dot_nt(t["q"] * t["e3"], t["k"] * t["e4"]), 0.0)
            o = inter + _dot(a, v)
            state[hd] = st * jnp.exp(t["bl"]) + _dot_tn(v, kd)
            oraw_ref[:, sl] = o
            r = lax.rsqrt(jnp.mean(o * o, axis=-1, keepdims=True) + RMS_EPS)
            gh = g_ref[:, sl]
            ohg_ref[:, sl] = (o * r * nw_ref[...] * (gh * _sigmoid(gh))).astype(ohg_ref.dtype)

    out_blk = pl.BlockSpec((HG_CHUNK, HG_BLOCK_W), lambda hp, n: (n, hp))
    return _pcall(
        kern, name=name, grid=(HG_HEADS // HG_HEADS_PER_STEP, HG_N_CHUNKS),
        in_specs=[in_blk(0), in_blk(2), in_blk(4), in_blk(6),
                  pl.BlockSpec((2, HG_BLOCK_W), lambda hp, n: (0, hp)),
                  pl.BlockSpec((1, HEAD_DIM), lambda hp, n: (0, 0))],
        out_specs=[out_blk, out_blk,
                   pl.BlockSpec((1, HG_HEADS_PER_STEP, HEAD_DIM, HEAD_DIM), lambda hp, n: (n, hp, 0, 0))],
        out_shape=[jax.ShapeDtypeStruct((SEQ, HG_WIDTH), F32), jax.ShapeDtypeStruct((SEQ, HG_WIDTH), BF16),
                   jax.ShapeDtypeStruct((HG_N_CHUNKS, HG_HEADS, HEAD_DIM, HEAD_DIM), F32)],
        args=(proj, proj, proj, proj, lbw, normw),
        scratch_shapes=[pltpu.VMEM((HG_HEADS_PER_STEP, HEAD_DIM, HEAD_DIM), F32)],
        semantics=("parallel", "arbitrary"), carried=carried)


def _hg_bwd(proj, lbw, normw, oraw, states, do_hg, name, carried=()):
    last = HG_N_CHUNKS - 1

    def in_blk(off):
        return pl.BlockSpec((HG_CHUNK, HG_BLOCK_W), lambda hp, n: (last - n, HG_Q_BLK + off + hp))

    blk = pl.BlockSpec((HG_CHUNK, HG_BLOCK_W), lambda hp, n: (last - n, hp))

    def kern(q_ref, f_ref, i_ref, g_ref, lb_ref, nw_ref, oraw_ref, st_ref, do_ref,
             dq_ref, df_ref, di_ref, dg_ref, dlb_ref, dnw_ref, dstate):
        first = pl.program_id(1) == 0

        @pl.when(first)
        def _():
            dstate[...] = jnp.zeros_like(dstate)

        causal = _tri(True)
        rows = lax.broadcasted_iota(jnp.int32, (HG_CHUNK, HEAD_DIM), 0)
        nw = nw_ref[...]
        dnw_tot = jnp.zeros((1, HEAD_DIM), F32)
        dlb_parts = []
        for hd in range(HG_HEADS_PER_STEP):
            sl = slice(hd * HEAD_DIM, (hd + 1) * HEAD_DIM)
            qh, fh, v, gh = q_ref[:, sl], f_ref[:, sl], i_ref[:, sl], g_ref[:, sl]
            o, dout = oraw_ref[:, sl], do_ref[:, sl]
            sgg = _sigmoid(gh)
            r = lax.rsqrt(jnp.mean(o * o, axis=-1, keepdims=True) + RMS_EPS)
            xn = o * r
            dg_ref[:, sl] = (dout * xn * nw * (sgg * (1.0 + gh * (1.0 - sgg)))).astype(dg_ref.dtype)
            don = dout * (gh * sgg)
            dnw_tot = dnw_tot + jnp.sum(don * xn, axis=0, keepdims=True)
            tt = don * nw
            do = r * (tt - xn * jnp.mean(tt * xn, axis=-1, keepdims=True))
            lb = _lower_bound(lb_ref, sl)
            t = _hg_chunk_terms(qh, fh, lb)
            k, q = t["k"], t["q"]
            kd, qb, qr, kr = k * t["e1"], q * t["e2"], q * t["e3"], k * t["e4"]
            a = jnp.where(causal, _dot_nt(qr, kr), 0.0)
            st = st_ref[0, hd]
            dstn = dstate[hd]
            dqb = _dot(do, st)
            da = jnp.where(causal, _dot_nt(do, v), 0.0)
            dv = _dot_tn(a, do) + _dot_nt(kd, dstn)
            dqr = _dot(da, kr)
            dkr = _dot_tn(da, qr)
            dkd = _dot(v, dstn)
            decay = jnp.exp(t["bl"])
            ddecay = jnp.sum(dstn * st, axis=0, keepdims=True)
            dstate[hd] = dstn * decay + _dot_tn(do, qb)
            dq = dqb * t["e2"] + dqr * t["e3"]
            dk = dkd * t["e1"] + dkr * t["e4"]
            db = dqb * qb + dqr * qr - dkr * kr - dkd * kd
            dbl = jnp.sum(dkd * kd, axis=0, keepdims=True) + ddecay * decay
            dbr = jnp.sum(dkr * kr - dqr * qr, axis=0, keepdims=True)
            dlf = _dot_exact(_tri(False).astype(F32), db) + dbl + jnp.where(rows <= HG_MID, dbr, 0.0)
            df = dlf / t["f"] - dk
            sig, sq = t["sig"], t["sq"]
            df_ref[:, sl] = (df * (1.0 - lb) * sig * (1.0 - sig)).astype(df_ref.dtype)
            dlb_parts.append(jnp.sum(df * (1.0 - sig), axis=0, keepdims=True))
            dq_ref[:, sl] = (dq * (sq * (1.0 + qh * (1.0 - sq)))).astype(dq_ref.dtype)
            di_ref[:, sl] = dv.astype(di_ref.dtype)
        dlb_row = jnp.concatenate(dlb_parts, axis=1)
        dnw_blk = jnp.broadcast_to(dnw_tot, (8, HEAD_DIM))

        @pl.when(first)
        def _():
            dlb_ref[...] = dlb_row
            dnw_ref[...] = dnw_blk

        @pl.when(jnp.logical_not(first))
        def _():
            dlb_ref[...] += dlb_row
            dnw_ref[...] += dnw_blk

    n_hp = HG_HEADS // HG_HEADS_PER_STEP
    outs, cres = _pcall(
        kern, name=name, grid=(n_hp, HG_N_CHUNKS),
        in_specs=[in_blk(0), in_blk(2), in_blk(4), in_blk(6),
                  pl.BlockSpec((2, HG_BLOCK_W), lambda hp, n: (0, hp)),
                  pl.BlockSpec((1, HEAD_DIM), lambda hp, n: (0, 0)),
                  blk,
                  pl.BlockSpec((1, HG_HEADS_PER_STEP, HEAD_DIM, HEAD_DIM), lambda hp, n: (last - n, hp, 0, 0)),
                  blk],
        out_specs=[blk, blk, blk, blk,
                   pl.BlockSpec((1, HG_BLOCK_W), lambda hp, n: (0, hp)),
                   pl.BlockSpec((8, HEAD_DIM), lambda hp, n: (hp, 0))],
        out_shape=[jax.ShapeDtypeStruct((SEQ, HG_WIDTH), BF16)] * 4
        + [jax.ShapeDtypeStruct((1, HG_WIDTH), F32), jax.ShapeDtypeStruct((8 * n_hp, HEAD_DIM), F32)],
        args=(proj, proj, proj, proj, lbw, normw, oraw, states, do_hg),
        scratch_shapes=[pltpu.VMEM((HG_HEADS_PER_STEP, HEAD_DIM, HEAD_DIM), F32)],
        semantics=("parallel", "arbitrary"), carried=carried)
    dqh, dfh, dih, dgh, dlb, dnw = outs
    return (dqh, dfh, dih, dgh, dlb, (dnw[0:1], dnw[8:9])), cres


GATE_BLOCK_W = 512
GATE_A_BLK = (3 * ATT_WIDTH + 4 * HG_WIDTH) // GATE_BLOCK_W
GATE_B_BLK = GATE_A_BLK + D_MODEL // GATE_BLOCK_W


def _gate_specs():
    tr = ROW_BLOCK
    blk = pl.BlockSpec((tr, GATE_BLOCK_W), lambda i, j: (i, j))
    ga = pl.BlockSpec((tr, GATE_BLOCK_W), lambda i, j: (i, GATE_A_BLK + j))
    gb = pl.BlockSpec((tr, GATE_BLOCK_W), lambda i, j: (i, GATE_B_BLK + j))
    return (SEQ // tr, D_MODEL // GATE_BLOCK_W), blk, ga, gb


def _gate_fwd(proj, ya, yb, name, carried=()):
    grid, blk, ga, gb = _gate_specs()

    def kern(ga_ref, gb_ref, ya_ref, yb_ref, o_ref):
        o_ref[...] = (_sigmoid(ga_ref[...]) * ya_ref[...] + _sigmoid(gb_ref[...]) * yb_ref[...]).astype(o_ref.dtype)

    return _pcall(
        kern, name=name, grid=grid, in_specs=[ga, gb, blk, blk], out_specs=blk,
        out_shape=jax.ShapeDtypeStruct((SEQ, D_MODEL), BF16), args=(proj, proj, ya, yb),
        semantics=("parallel", "parallel"), carried=carried)


def _gate_bwd(proj, ya, yb, dmerged, name, carried=()):
    grid, blk, ga, gb = _gate_specs()

    def kern(ga_ref, gb_ref, ya_ref, yb_ref, dm_ref, dya_ref, dyb_ref, dga_ref, dgb_ref):
        dm = dm_ref[...]
        sa, sb = _sigmoid(ga_ref[...]), _sigmoid(gb_ref[...])
        dya_ref[...] = (dm * sa).astype(dya_ref.dtype)
        dyb_ref[...] = (dm * sb).astype(dyb_ref.dtype)
        dga_ref[...] = (dm * ya_ref[...] * sa * (1.0 - sa)).astype(dga_ref.dtype)
        dgb_ref[...] = (dm * yb_ref[...] * sb * (1.0 - sb)).astype(dgb_ref.dtype)

    return _pcall(
        kern, name=name, grid=grid, in_specs=[ga, gb, blk, blk, blk], out_specs=[blk] * 4,
        out_shape=[jax.ShapeDtypeStruct((SEQ, D_MODEL), BF16)] * 4, args=(proj, proj, ya, yb, dmerged),
        semantics=("parallel", "parallel"), carried=carried)


FF_SHARD = D_FF // N_CHIPS


def _swiglu_fwd(ab, name):
    tr = ROW_BLOCK

    def kern(ab_ref, u_ref):
        a, b = ab_ref[:, :FF_SHARD], ab_ref[:, FF_SHARD:]
        u_ref[...] = (a * _sigmoid(a) * b).astype(u_ref.dtype)

    return pl.pallas_call(
        kern, name=name, grid=(SEQ // tr, N_CHIPS),
        in_specs=[pl.BlockSpec((tr, 2 * FF_SHARD), lambda i, j: (i, j))],
        out_specs=pl.BlockSpec((tr, FF_SHARD), lambda i, j: (i, j)),
        out_shape=jax.ShapeDtypeStruct((SEQ, D_FF), BF16),
        compiler_params=_cparams(("parallel", "parallel")),
    )(ab)


def _swiglu_bwd(ab, du, name, carried=()):
    tr = ROW_BLOCK

    def kern(ab_ref, du_ref, dab_ref):
        a, b = ab_ref[:, :FF_SHARD], ab_ref[:, FF_SHARD:]
        du_ = du_ref[...]
        sg = _sigmoid(a)
        dab_ref[:, :FF_SHARD] = (du_ * b * (sg * (1.0 + a * (1.0 - sg)))).astype(dab_ref.dtype)
        dab_ref[:, FF_SHARD:] = (du_ * (a * sg)).astype(dab_ref.dtype)

    wide = pl.BlockSpec((tr, 2 * FF_SHARD), lambda i, j: (i, j))
    return _pcall(
        kern, name=name, grid=(SEQ // tr, N_CHIPS),
        in_specs=[wide, pl.BlockSpec((tr, FF_SHARD), lambda i, j: (i, j))],
        out_specs=wide, out_shape=jax.ShapeDtypeStruct((SEQ, 2 * D_FF), BF16), args=(ab, du),
        semantics=("parallel", "parallel"), carried=carried)


CROSS_ROWS = 512


def _cross_fwd(qc, kvc, name):
    def kern(q_ref, k_ref, v_ref, o_ref):
        s = _dot_nt(q_ref[...], k_ref[...]) * ATT_SCALE
        m = jnp.max(s, axis=-1, keepdims=True)
        e = jnp.exp(s - m)
        p = e / jnp.sum(e, axis=-1, keepdims=True)
        o_ref[...] = _dot(p, v_ref[...]).astype(o_ref.dtype)

    qblk = pl.BlockSpec((CROSS_ROWS, HEAD_DIM), lambda h, i: (i, h))
    return pl.pallas_call(
        kern, name=name, grid=(CROSS_HEADS, SEQ // CROSS_ROWS),
        in_specs=[qblk, pl.BlockSpec((MEM_LEN, HEAD_DIM), lambda h, i: (0, h)),
                  pl.BlockSpec((MEM_LEN, HEAD_DIM), lambda h, i: (0, CROSS_HEADS + h))],
        out_specs=qblk, out_shape=jax.ShapeDtypeStruct((SEQ, CROSS_WIDTH), BF16),
        compiler_params=_cparams(("parallel", "parallel")),
    )(qc, kvc, kvc)


def _cross_bwd(qc, kvc, doc, name):
    def kern(q_ref, k_ref, v_ref, do_ref, dq_ref, dk_ref, dv_ref):
        q, k, v, do = q_ref[...], k_ref[...], v_ref[...], do_ref[...]
        s = _dot_nt(q, k) * ATT_SCALE
        m = jnp.max(s, axis=-1, keepdims=True)
        e = jnp.exp(s - m)
        p = e / jnp.sum(e, axis=-1, keepdims=True)
        dp = _dot_nt(do, v)
        ds = p * (dp - jnp.sum(dp * p, axis=-1, keepdims=True))
        dq_ref[...] = (_dot(ds, k) * ATT_SCALE).astype(dq_ref.dtype)
        dk = _dot_tn(ds, q) * ATT_SCALE
        dv = _dot_tn(p, do)

        @pl.when(pl.program_id(1) == 0)
        def _():
            dk_ref[...] = dk
            dv_ref[...] = dv

        @pl.when(pl.program_id(1) > 0)
        def _():
            dk_ref[...] += dk
            dv_ref[...] += dv

    qblk = pl.BlockSpec((CROSS_ROWS, HEAD_DIM), lambda h, i: (i, h))
    kblk = pl.BlockSpec((MEM_LEN, HEAD_DIM), lambda h, i: (0, h))
    dq, dk, dv = pl.pallas_call(
        kern, name=name, grid=(CROSS_HEADS, SEQ // CROSS_ROWS),
        in_specs=[qblk, kblk, pl.BlockSpec((MEM_LEN, HEAD_DIM), lambda h, i: (0, CROSS_HEADS + h)), qblk],
        out_specs=[qblk, kblk, kblk],
        out_shape=[jax.ShapeDtypeStruct((SEQ, CROSS_WIDTH), BF16),
                   jax.ShapeDtypeStruct((MEM_LEN, CROSS_WIDTH), F32),
                   jax.ShapeDtypeStruct((MEM_LEN, CROSS_WIDTH), F32)],
        compiler_params=_cparams(("parallel", "arbitrary")),
    )(qc, kvc, kvc, doc)
    return dq, jnp.concatenate([dk, dv], axis=1)


FULL_SPECS = {
    "w_in": ("col", D_MODEL, IN_WIDTH),
    "w_branch_a": ("col", ATT_OUT, D_MODEL),
    "w_branch_b": ("col", HG_WIDTH, D_MODEL),
    "w_out": ("row", D_MODEL, D_MODEL),
    "wq_cross": ("row", D_MODEL, CROSS_WIDTH),
    "wkv_cross": ("row", D_MODEL, 2 * CROSS_WIDTH),
    "wo_cross": ("col", CROSS_WIDTH, D_MODEL),
    "w13": ("col", D_MODEL, 2 * D_FF),
    "w2": ("row", D_FF, D_MODEL),
}
WEIGHT_PLACE = {
    "w_in": ("w_in", 0), "w_branch_a": ("w_branch_a", 0), "w_branch_b": ("w_branch_b", 0),
    "w_out": ("w_out", 0), "wq_cross": ("wq_cross", 0), "wkv_cross": ("wkv_cross", 0),
    "wo_cross": ("wo_cross", 0), "w1": ("w13", 0), "w3": ("w13", FF_SHARD), "w2": ("w2", 0),
}
BIG_WEIGHTS = tuple(WEIGHT_PLACE)
EW_BLOCK_ELEMS = 512 * 1024


def _position():
    return lax.axis_index("x"), lax.axis_index("y"), lax.axis_index("c")


def _other_chips(x, y):
    return [(1 - x, y), (x, 1 - y), (1 - x, 1 - y)]


def _half(ref, kind, h):
    r, c = ref.shape
    if kind == "col":
        return ref.at[pl.ds(h * (r // 2), r // 2), :]
    return ref.at[:, pl.ds(h * (c // 2), c // 2)]


def _shard_of(ref, kind, start, size):
    return ref.at[:, pl.ds(start, size)] if kind == "col" else ref.at[pl.ds(start, size), :]


def _rows_of(ref, r0, nrows):
    return ref if nrows is None else ref.at[pl.ds(r0, nrows), :]


def _half_shape(kind, rows, cols):
    return (rows // 2, cols) if kind == "col" else (rows, cols // 2)


def _slot_shape(spec):
    kind, rows, cols = spec
    hr, hc = _half_shape(kind, rows, cols)
    return (hr, hc // N_CHIPS) if kind == "col" else (hr // N_CHIPS, hc)


def _remote(src, dst, send_sem, recv_sem, device):
    return pltpu.make_async_remote_copy(src_ref=src, dst_ref=dst, send_sem=send_sem, recv_sem=recv_sem,
                                        device_id=device, device_id_type=MESH)


def _gather_ici_comm(fulls, jobs, specs):
    def piece(refs, job, chip, c):
        f, r0, nr = job
        kind, rows, cols = specs[f]
        stride = (cols if kind == "col" else rows) // N_CHIPS
        return _rows_of(_half(_shard_of(refs[f], kind, chip * stride, stride), kind, c), r0, nr)

    def start(refs, ss, rs):
        x, y, c = _position()
        j = 2 * x + y
        for q, job in enumerate(jobs):
            for p, (px, py) in enumerate(_other_chips(x, y)):
                _remote(piece(refs, job, j, c), piece(refs, job, j, c), ss.at[3 * q + p], rs.at[3 * q + p],
                        (px, py, c)).start()

    def finish(refs, ss, rs):
        x, y, c = _position()
        j = 2 * x + y
        for q, job in enumerate(jobs):
            for p, (px, py) in enumerate(_other_chips(x, y)):
                _remote(piece(refs, job, j, c), piece(refs, job, 2 * px + py, c), ss.at[3 * q + p],
                        rs.at[3 * q + p], (px, py, c)).wait_recv()
        for q, job in enumerate(jobs):
            for p, (px, py) in enumerate(_other_chips(x, y)):
                _remote(piece(refs, job, j, c), piece(refs, job, j, c), ss.at[3 * q + p], rs.at[3 * q + p],
                        (px, py, c)).wait_send()

    names = list(dict.fromkeys(job[0] for job in jobs))
    return _Carried({f: fulls[f] for f in names}, {}, 3 * len(jobs), start, finish)


def _gather_d2d_comm(fulls, jobs, specs):
    def rect(refs, job, h):
        f, r0, nr = job
        assert nr is None or specs[f][0] == "col"
        return _rows_of(_half(refs[f], specs[f][0], h), r0, nr)

    def start(refs, ss, rs):
        x, y, c = _position()
        for q, job in enumerate(jobs):
            _remote(rect(refs, job, c), rect(refs, job, c), ss.at[q], rs.at[q], (x, y, 1 - c)).start()

    def finish(refs, ss, rs):
        x, y, c = _position()
        for q, job in enumerate(jobs):
            _remote(rect(refs, job, 1 - c), rect(refs, job, 1 - c), ss.at[q], rs.at[q], (x, y, 1 - c)).wait_recv()
        for q, job in enumerate(jobs):
            _remote(rect(refs, job, c), rect(refs, job, c), ss.at[q], rs.at[q], (x, y, 1 - c)).wait_send()

    names = list(dict.fromkeys(job[0] for job in jobs))
    return _Carried({f: fulls[f] for f in names}, {}, len(jobs), start, finish)


def _pairx_comm(grads, names, specs):
    def copies(refs, ss, rs):
        x, y, c = _position()
        return [_remote(_half(refs[("g", f)], specs[f][0], 1 - c), refs[("r", f)], ss.at[i], rs.at[i], (x, y, 1 - c))
                for i, f in enumerate(names)]

    def start(refs, ss, rs):
        for cp in copies(refs, ss, rs):
            cp.start()

    def finish(refs, ss, rs):
        for cp in copies(refs, ss, rs):
            cp.wait_recv()
        for cp in copies(refs, ss, rs):
            cp.wait_send()

    fresh = {("r", f): jax.ShapeDtypeStruct(_half_shape(*specs[f]), BF16) for f in names}
    return _Carried({("g", f): grads[f] for f in names}, fresh, len(names), start, finish)


def _chipx_comm(pair_sums, slots, jobs, specs):
    def copies(refs, ss, rs):
        x, y, c = _position()
        out = []
        for q, (f, r0, nr) in enumerate(jobs):
            kind = specs[f][0]
            width = _slot_shape(specs[f])[1 if kind == "col" else 0]
            for p, (px, py) in enumerate(_other_chips(x, y)):
                src = _rows_of(_shard_of(refs[("p", f)], kind, (2 * px + py) * width, width), r0, nr)
                dst = _rows_of(refs[("s", f)].at[p], r0, nr)
                out.append(_remote(src, dst, ss.at[3 * q + p], rs.at[3 * q + p], (px, py, c)))
        return out

    def start(refs, ss, rs):
        for cp in copies(refs, ss, rs):
            cp.start()

    def finish(refs, ss, rs):
        for cp in copies(refs, ss, rs):
            cp.wait_recv()
        for cp in copies(refs, ss, rs):
            cp.wait_send()

    names = list(dict.fromkeys(job[0] for job in jobs))
    arrays = {("p", f): pair_sums[f] for f in names}
    arrays.update({("s", f): slots[f] for f in names})
    return _Carried(arrays, {}, 3 * len(jobs), start, finish)


def _share_comm(grads, wnames, specs, place):
    def start(refs, ss, rs):
        x, y, c = _position()
        for i, w in enumerate(wnames):
            kind = specs[place[w][0]][0]
            _remote(_half(refs[w], kind, c), _half(refs[w], kind, c), ss.at[i], rs.at[i], (x, y, 1 - c)).start()

    def finish(refs, ss, rs):
        x, y, c = _position()
        for i, w in enumerate(wnames):
            kind = specs[place[w][0]][0]
            _remote(_half(refs[w], kind, 1 - c), _half(refs[w], kind, 1 - c), ss.at[i], rs.at[i],
                    (x, y, 1 - c)).wait_recv()
        for i, w in enumerate(wnames):
            kind = specs[place[w][0]][0]
            _remote(_half(refs[w], kind, c), _half(refs[w], kind, c), ss.at[i], rs.at[i], (x, y, 1 - c)).wait_send()

    return _Carried({w: grads[w] for w in wnames}, {}, len(wnames), start, finish)


def _gather_rows(v, name="gather_small"):
    shape = v.shape

    def body(v_ref, out_ref, send_sem, recv_sem, loc_sem):
        x, y, c = _position()
        me = 4 * x + 2 * y + c
        flips = [(fx, fy, fc) for fx in (0, 1) for fy in (0, 1) for fc in (0, 1)][1:]

        def peer(fl):
            return tuple(1 - a if f else a for a, f in zip((x, y, c), fl))

        loc = pltpu.make_async_copy(v_ref, out_ref.at[me], loc_sem)
        loc.start()
        sends = []
        for i, fl in enumerate(flips):
            cp = _remote(v_ref, out_ref.at[me], send_sem.at[i], recv_sem.at[i], peer(fl))
            cp.start()
            sends.append(cp)
        for i, fl in enumerate(flips):
            px, py, pc = peer(fl)
            _remote(v_ref, out_ref.at[4 * px + 2 * py + pc], send_sem.at[i], recv_sem.at[i], peer(fl)).wait_recv()
        for cp in sends:
            cp.wait_send()
        loc.wait()

    return pl.pallas_call(
        body, name=name, in_specs=[ANY], out_specs=ANY,
        out_shape=jax.ShapeDtypeStruct((N_DEV,) + shape, F32),
        scratch_shapes=[pltpu.SemaphoreType.DMA((N_DEV - 1,)), pltpu.SemaphoreType.DMA((N_DEV - 1,)),
                        pltpu.SemaphoreType.DMA],
    )(v)


def _ew_block(rows, cols):
    tc = cols if cols <= 4096 else _div(cols, 2048, LANE)
    tr = _div(rows, max(16, EW_BLOCK_ELEMS // tc), 16)
    return tr, tc


def _mesh_scalars():
    x, y, c = _position()
    return jnp.stack([c, 2 * x + y]).astype(jnp.int32)


def _grid_spec(grid, in_specs, out_specs):
    return pltpu.PrefetchScalarGridSpec(num_scalar_prefetch=1, grid=grid, in_specs=in_specs, out_specs=out_specs)


def _cast_into_full(parts, fname, pos, specs, place, name):
    kind, rows, cols = specs[fname]
    ws = [w for w in place if place[w][0] == fname]
    if kind == "col":
        stride = cols // N_CHIPS
        hr = rows // 2
        tr = _div(hr, max(16, EW_BLOCK_ELEMS // stride), 16)
        nrb = hr // tr
        in_specs = [pl.BlockSpec((tr, parts[w].shape[1]), lambda i, pos_ref: (i + pos_ref[0] * nrb, 0)) for w in ws]
        out_spec = pl.BlockSpec((tr, stride), lambda i, pos_ref: (i + pos_ref[0] * nrb, pos_ref[1]))
    else:
        stride = rows // N_CHIPS
        hc = cols // 2
        tr = _div(stride, max(16, EW_BLOCK_ELEMS // hc), 16)
        nrb = stride // tr
        in_specs = [pl.BlockSpec((tr, hc), lambda i, pos_ref: (i, pos_ref[0])) for w in ws]
        out_spec = pl.BlockSpec((tr, hc), lambda i, pos_ref: (i + pos_ref[1] * nrb, pos_ref[0]))

    def kern(pos_ref, *refs):
        o_ref = refs[-1]
        for w, r in zip(ws, refs[:-1]):
            off = place[w][1] if kind == "col" else 0
            o_ref[:, off:off + r.shape[1]] = r[...].astype(o_ref.dtype)

    return pl.pallas_call(
        kern, name=name, grid_spec=_grid_spec((nrb,), in_specs, out_spec),
        out_shape=jax.ShapeDtypeStruct((rows, cols), BF16),
        compiler_params=_cparams(("parallel",)),
    )(pos, *[parts[w] for w in ws])


def _pair_sum(grad, recv, pos, spec, name):
    kind, rows, cols = spec
    hr, hc = _half_shape(kind, rows, cols)
    tr, tc = _ew_block(hr, hc)
    nrb, ncb = hr // tr, hc // tc
    blk = pl.BlockSpec((tr, tc), lambda i, jj, pos_ref: (i, jj))
    if kind == "col":
        mine = pl.BlockSpec((tr, tc), lambda i, jj, pos_ref: (i + pos_ref[0] * nrb, jj))
    else:
        mine = pl.BlockSpec((tr, tc), lambda i, jj, pos_ref: (i, jj + pos_ref[0] * ncb))

    def kern(pos_ref, g_ref, r_ref, o_ref, slots_ref):
        o_ref[...] = (g_ref[...].astype(F32) + r_ref[...].astype(F32)).astype(o_ref.dtype)

    return pl.pallas_call(
        kern, name=name, grid_spec=_grid_spec((nrb, ncb), [mine, blk], [blk, ANY]),
        out_shape=[jax.ShapeDtypeStruct((hr, hc), BF16),
                   jax.ShapeDtypeStruct((N_CHIPS - 1,) + _slot_shape(spec), BF16)],
        compiler_params=_cparams(("parallel", "parallel")),
    )(pos, grad, recv)


def _chip_sum(pair_sum, slots, pos, fname, shard_shapes, specs, place, name):
    kind, rows, cols = specs[fname]
    sr, sc = _slot_shape(specs[fname])
    ws = [w for w in place if place[w][0] == fname]
    n_slots = N_CHIPS - 1
    tr = _div(sr, max(16, EW_BLOCK_ELEMS // sc), 16)
    nrb = sr // tr
    slot = pl.BlockSpec((n_slots, tr, sc), lambda i, pos_ref: (0, i, 0))
    if kind == "col":
        own = pl.BlockSpec((tr, sc), lambda i, pos_ref: (i, pos_ref[1]))
        out_specs = [pl.BlockSpec((tr, shard_shapes[w][1]), lambda i, pos_ref: (i + pos_ref[0] * nrb, 0)) for w in ws]
    else:
        own = pl.BlockSpec((tr, sc), lambda i, pos_ref: (i + pos_ref[1] * nrb, 0))
        out_specs = [pl.BlockSpec((tr, sc), lambda i, pos_ref: (i, pos_ref[0])) for w in ws]

    def kern(pos_ref, own_ref, slot_ref, *out_refs):
        tot = own_ref[...].astype(F32)
        for s in range(n_slots):
            tot = tot + slot_ref[s].astype(F32)
        for w, o_ref in zip(ws, out_refs):
            off = place[w][1] if kind == "col" else 0
            o_ref[...] = tot[:, off:off + o_ref.shape[1]]

    outs = pl.pallas_call(
        kern, name=name, grid_spec=_grid_spec((nrb,), [own, slot], out_specs),
        out_shape=[jax.ShapeDtypeStruct(shard_shapes[w], F32) for w in ws],
        compiler_params=_cparams(("parallel",)),
    )(pos, pair_sum, slots)
    return dict(zip(ws, outs))


def _adam_math(w, g, m, v):
    m2 = ADAM_B1 * m + (1.0 - ADAM_B1) * g
    v2 = ADAM_B2 * v + (1.0 - ADAM_B2) * (g * g)
    m_hat = m2 / (1.0 - ADAM_B1 ** ADAM_STEP)
    v_hat = v2 / (1.0 - ADAM_B2 ** ADAM_STEP)
    delta = -ADAM_LR * (m_hat / (jnp.sqrt(v_hat) + ADAM_EPS) + ADAM_WD * w)
    return delta, m2, v2


def _adamw(w, g, m, v, name, carried=()):
    rows, cols = w.shape
    tr, tc = _ew_block(rows, cols)

    def kern(w_ref, g_ref, m_ref, v_ref, d_ref, m2_ref, v2_ref, g_out_ref):
        g_ = g_ref[...]
        d_ref[...], m2_ref[...], v2_ref[...] = _adam_math(w_ref[...], g_, m_ref[...], v_ref[...])
        g_out_ref[...] = g_

    blk = pl.BlockSpec((tr, tc), lambda i, j: (i, j))
    return _pcall(
        kern, name=name, grid=(rows // tr, cols // tc), in_specs=[blk] * 4, out_specs=[blk] * 4,
        out_shape=[jax.ShapeDtypeStruct((rows, cols), F32)] * 4, args=(w, g, m, v),
        semantics=("parallel", "parallel"), carried=carried)


SMALL_ROWS = ("ln_mix_w", "ln_cross_w", "ln_mem_w", "ln_ffn_w", "ln_final_w")
ROW_HG_NORM, ROW_LB0, ROW_LB1 = 5, 6, 7
LOSS_LANE0 = HEAD_DIM


def _pack_small(vals):
    rows = [vals[n].reshape(1, D_MODEL) for n in SMALL_ROWS]
    pad = lambda a: jnp.pad(a, ((0, 0), (0, D_MODEL - a.shape[1])))
    rows.append(pad(vals["hg_norm_w"].reshape(1, HEAD_DIM)))
    rows.append(pad(vals["hg_lower_bounds"].reshape(2, HG_WIDTH)))
    return jnp.concatenate(rows, axis=0)


def _small_update(gathered, w, m, v, name="small_update"):
    def kern(g_ref, w_ref, m_ref, v_ref, grad_ref, d_ref, m2_ref, v2_ref, loss_ref):
        tot = g_ref[0]
        for s in range(1, N_DEV):
            tot = tot + g_ref[s]
        wv = w_ref[...]
        row = lax.broadcasted_iota(jnp.int32, (8, D_MODEL), 0)
        lane = lax.broadcasted_iota(jnp.int32, (8, D_MODEL), 1)
        l0, l1 = wv[ROW_LB0:ROW_LB0 + 1], wv[ROW_LB1:ROW_LB1 + 1]
        mx = jnp.maximum(l0, l1)
        e0, e1 = jnp.exp(l0 - mx), jnp.exp(l1 - mx)
        p0 = e0 / (e0 + e1)
        dlog = tot[ROW_LB0:ROW_LB0 + 1] * p0 * (1.0 - p0)
        tot = jnp.where(row == ROW_HG_NORM, tot + tot[ROW_LB1:ROW_LB1 + 1], tot)
        grad = jnp.where(row == ROW_LB0, dlog, jnp.where(row == ROW_LB1, -dlog, tot))
        grad = jnp.where((row == ROW_HG_NORM) & (lane >= HEAD_DIM), 0.0, grad)
        grad = jnp.where((row >= ROW_LB0) & (lane >= HG_WIDTH), 0.0, grad)
        grad_ref[...] = grad
        d_ref[...], m2_ref[...], v2_ref[...] = _adam_math(wv, grad, m_ref[...], v_ref[...])
        loss_ref[...] = tot[ROW_HG_NORM:ROW_HG_NORM + 1, LOSS_LANE0:LOSS_LANE0 + LANE]

    full = pl.BlockSpec((8, D_MODEL), lambda: (0, 0))
    return pl.pallas_call(
        kern, name=name,
        in_specs=[pl.BlockSpec((N_DEV, 8, D_MODEL), lambda: (0, 0, 0)), full, full, full],
        out_specs=[full, full, full, full, pl.BlockSpec((1, LANE), lambda: (0, 0))],
        out_shape=[jax.ShapeDtypeStruct((8, D_MODEL), F32)] * 4 + [jax.ShapeDtypeStruct((1, LANE), F32)],
        compiler_params=_cparams(),
    )(gathered, w, m, v)


def _unpack_small(p, shapes):
    out = {n: p[i].reshape(shapes[n]) for i, n in enumerate(SMALL_ROWS)}
    out["hg_norm_w"] = p[ROW_HG_NORM, :HEAD_DIM].reshape(shapes["hg_norm_w"])
    out["hg_lower_bounds"] = p[ROW_LB0:ROW_LB1 + 1, :HG_WIDTH].reshape(shapes["hg_lower_bounds"])
    return out


WHOLE = lambda f: (f, 0, None)
MID_MATRICES = ("w_branch_a", "w_branch_b", "w_out", "wq_cross", "wkv_cross", "wo_cross")
MID_WEIGHTS = MID_MATRICES
W_IN_PIECES = [("w_in", r0, 256) for r0 in range(0, D_MODEL // 2, 256)]
GATHER_GROUPS = [(f"w_in{i}", [job]) for i, job in enumerate(W_IN_PIECES)] + [
    ("mid", [WHOLE(f) for f in MID_MATRICES]), ("w13", [WHOLE("w13")]), ("w2", [WHOLE("w2")])]
BEFORE = {
    "hgrn_fwd": [("wait", "mid")],
    "gate_fwd": [("wait", "w13")],
    "mm_w13": [("wait", "w2")],
}
CARRY = {
    "hgrn_fwd": [("d2d", [WHOLE(f) for f in MID_MATRICES])],
    "gate_fwd": [("d2d", [WHOLE("w13")])],
    "mm_w13": [("d2d", [WHOLE("w2")])],
    "swiglu_bwd": [("pairx", ["w2"])],
    "mm_dhf": [("pairx", ["w13"])],
    "attn_merge_bwd": [("pairx", list(MID_MATRICES))],
}
AFTER = {
    "swiglu_bwd": [("pair_sum", "w2"), ("start", "rs_w2", [WHOLE("w2")])],
    "mm_dhf": [("pair_sum", "w13"), ("start", "rs_w13", [WHOLE("w13")])],
    "attn_merge_bwd": [("pair_sum", f) for f in MID_MATRICES] + [("start", "rs_mid", [WHOLE(f) for f in MID_MATRICES])],
    "mm_dwin": [("run", ("pairx", ["w_in"]), "rs_pair_exchange_w_in"), ("pair_sum", "w_in"),
                ("start", "rs_w_in", [WHOLE("w_in")])],
}
FINISH = [
    ("wait", "rs_w2"), ("chip_sum", "w2"), ("wait", "rs_w13"), ("chip_sum", "w13"), ("wait", "rs_mid"),
] + [("chip_sum", f) for f in MID_MATRICES] + [
    ("run", ("share", ["w1", "w3", "w2"] + list(MID_WEIGHTS)), "rs_sibling_share"),
    ("adamw", ["w1", "w3", "w2"] + list(MID_WEIGHTS)),
    ("wait", "rs_w_in"), ("chip_sum", "w_in"), ("run", ("share", ["w_in"]), "rs_sibling_share_w_in"),
    ("adamw", ["w_in"]),
]


class _Net:
    def __init__(self, full, pos=None, shard_shapes=None, comm=True, specs=FULL_SPECS, place=WEIGHT_PLACE):
        self.full, self.pos, self.shard_shapes, self.comm = dict(full), pos, shard_shapes, comm
        self.specs, self.place = specs, place
        self.gw, self.recv, self.psum, self.slots, self.grads = {}, {}, {}, {}, {}
        self.pending, self.token, self.last = {}, None, None

    def _make(self, kind, arg):
        if kind == "gather":
            return _gather_ici_comm(self.full, arg, self.specs)
        if kind == "d2d":
            return _gather_d2d_comm(self.full, arg, self.specs)
        if kind == "pairx":
            return _pairx_comm(self.gw, arg, self.specs)
        if kind == "chipx":
            return _chipx_comm(self.psum, self.slots, arg, self.specs)
        assert kind == "share"
        return _share_comm(self.grads, arg, self.specs, self.place)

    def _store(self, kind, res):
        if kind in ("gather", "d2d"):
            self.full.update(res)
        elif kind == "pairx":
            for (tag, f), a in res.items():
                (self.gw if tag == "g" else self.recv)[f] = a
        elif kind == "chipx":
            for (tag, f), a in res.items():
                (self.psum if tag == "p" else self.slots)[f] = a
        else:
            self.grads.update(res)

    def run_comm(self, item, name):
        kind, arg = item
        self._store(kind, _run_comm([self._make(kind, arg)], name)[0])

    def start(self, groups, kind, name):
        res, sems, token = _split_start([self._make(kind, jobs) for _, jobs in groups], name)
        for (group, jobs), r, s in zip(groups, res, sems):
            self._store(kind, r)
            self.pending[group] = (kind, jobs, s)
        self.token = self.last = token

    def wait(self, group):
        kind, jobs, sems = self.pending.pop(group)
        self._store(kind, _split_wait([self._make(kind, jobs)], [sems], self.last, f"wait_{group}")[0])

    def step(self, step):
        if step[0] == "wait":
            self.wait(step[1])
        elif step[0] == "start":
            self.start([(step[1], step[2])], "chipx", f"start_{step[1]}")
        elif step[0] == "pair_sum":
            f = step[1]
            self.psum[f], self.slots[f] = _pair_sum(self.gw[f], self.recv[f], self.pos, self.specs[f],
                                                    f"rs_pair_sum_{f}")
        elif step[0] == "chip_sum":
            f = step[1]
            self.grads.update(_chip_sum(self.psum[f], self.slots[f], self.pos, f, self.shard_shapes,
                                        self.specs, self.place, f"rs_chip_sum_{f}"))
        else:
            assert step[0] == "run"
            self.run_comm(step[1], step[2])

    def call(self, fn, name, *args, grad_of=None, **kw):
        for step in (BEFORE.get(name, []) if self.comm else []):
            self.step(step)
        items = CARRY.get(name, []) if self.comm else []
        carried = [self._make(k, a) for k, a in items]
        if self.token is not None:
            carried.append(_Token(self.token))
            self.token = None
        out, res = fn(*args, name=name, carried=carried, **kw)
        if grad_of is not None:
            self.gw[grad_of] = out
        self.last = jax.tree.leaves(out)[0]
        for (kind, _), r in zip(items, res):
            self._store(kind, r)
        for step in (AFTER.get(name, []) if self.comm else []):
            self.step(step)
        return out


def _local_step(net, x, mem, target, small):
    full, call = net.full, net.call
    h = _rms_fwd(x, small["ln_mix_w"], "rms_mix")
    proj = call(_mm, "mm_proj", h, full["w_in"], mode="nn", out_dtype=F32)
    att = [call(_attn_fwd, f"attn_fwd_g{g}", proj, g) for g in range(3)]
    outs, lses = [a[0] for a in att], [a[1] for a in att]
    o_att = _attn_merge_fwd(outs, lses, "attn_merge")
    oraw, o_hg, states = call(_hg_fwd, "hgrn_fwd", proj, small["hg_lower_bounds"], small["hg_norm_w"])
    ya = call(_mm, "mm_branch_a", o_att, full["w_branch_a"], mode="nn", out_dtype=F32)
    yb = call(_mm, "mm_branch_b", o_hg, full["w_branch_b"], mode="nn", out_dtype=F32)
    merged = call(_gate_fwd, "gate_fwd", proj, ya, yb)
    x1 = call(_mm, "mm_out", merged, full["w_out"], mode="nn", out_dtype=F32, res=x)

    hc = _rms_fwd(x1, small["ln_cross_w"], "rms_cross")
    mn = _rms_fwd(mem, small["ln_mem_w"], "rms_mem")
    qc = call(_mm, "mm_q", hc, full["wq_cross"], mode="nn", out_dtype=F32)
    kvc = call(_mm, "mm_kv", mn, full["wkv_cross"], mode="nn", out_dtype=F32)
    oc = _cross_fwd(qc, kvc, "cross_fwd")
    x2 = call(_mm, "mm_o", oc, full["wo_cross"], mode="nn", out_dtype=F32, res=x1)

    hf = _rms_fwd(x2, small["ln_ffn_w"], "rms_ffn")
    ab = call(_mm, "mm_w13", hf, full["w13"], mode="nn", out_dtype=F32)
    u = _swiglu_fwd(ab, "swiglu_fwd")
    x3 = call(_mm, "mm_w2", u, full["w2"], mode="nn", out_dtype=F32, res=x2)

    dx3, dg_final, loss = _loss_head(x3, small["ln_final_w"], target, "loss_head")

    gs = {"ln_final_w": dg_final}
    du = call(_mm, "mm_du", dx3, full["w2"], mode="nt", out_dtype=F32)
    call(_mm, "mm_dw2", u, dx3, mode="tn", out_dtype=BF16, grad_of="w2")
    dab = call(_swiglu_bwd, "swiglu_bwd", ab, du)
    call(_mm, "mm_dw13", hf, dab, mode="tn", out_dtype=BF16, grad_of="w13")
    dhf = call(_mm, "mm_dhf", dab, full["w13"], mode="nt", out_dtype=F32)
    dx2, gs["ln_ffn_w"] = _rms_bwd(x2, small["ln_ffn_w"], dhf, dx3, "rms_ffn_bwd")
    doc = call(_mm, "mm_doc", dx2, full["wo_cross"], mode="nt", out_dtype=BF16)
    call(_mm, "mm_dwo", oc, dx2, mode="tn", out_dtype=BF16, grad_of="wo_cross")
    dqc, dkvc = _cross_bwd(qc, kvc, doc, "cross_bwd")
    call(_mm, "mm_dwq", hc, dqc, mode="tn", out_dtype=BF16, grad_of="wq_cross")
    dhc = call(_mm, "mm_dhc", dqc, full["wq_cross"], mode="nt", out_dtype=F32)
    call(_mm, "mm_dwkv", mn, dkvc, mode="tn", out_dtype=BF16, grad_of="wkv_cross")
    dmn = call(_mm, "mm_dmn", dkvc, full["wkv_cross"], mode="nt", out_dtype=F32)
    _, gs["ln_mem_w"] = _rms_bwd(mem, small["ln_mem_w"], dmn, None, "rms_mem_bwd")
    dx1, gs["ln_cross_w"] = _rms_bwd(x1, small["ln_cross_w"], dhc, dx2, "rms_cross_bwd")
    dmerged = call(_mm, "mm_dmerged", dx1, full["w_out"], mode="nt", out_dtype=F32)
    call(_mm, "mm_dwout", merged, dx1, mode="tn", out_dtype=BF16, grad_of="w_out")
    dya, dyb, dga, dgb = call(_gate_bwd, "gate_bwd", proj, ya, yb, dmerged)
    call(_mm, "mm_dwa", o_att, dya, mode="tn", out_dtype=BF16, grad_of="w_branch_a")
    do_att = call(_mm, "mm_doatt", dya, full["w_branch_a"], mode="nt", out_dtype=F32)
    call(_mm, "mm_dwb", o_hg, dyb, mode="tn", out_dtype=BF16, grad_of="w_branch_b")
    do_hg = call(_mm, "mm_dohg", dyb, full["w_branch_b"], mode="nt", out_dtype=F32)
    dqh, dfh, dih, dgh, dlb, gs["hg_norm_w"] = call(
        _hg_bwd, "hgrn_bwd", proj, small["hg_lower_bounds"], small["hg_norm_w"], oraw, states, do_hg)
    gs["hg_lb"] = dlb
    do_gs, dl_gs = call(_attn_merge_bwd, "attn_merge_bwd", outs, lses, do_att)
    dqs, dks, dvs = zip(*[call(_attn_bwd, f"attn_bwd_g{g}", proj, g, lses[g], do_gs[g], dl_gs[g]) for g in range(3)])
    dproj = jnp.concatenate([*dqs, *dks, *dvs, dqh, dfh, dih, dgh, dga, dgb], axis=1)
    call(_mm, "mm_dwin", h, dproj, mode="tn", out_dtype=BF16, grad_of="w_in")
    dh = call(_mm, "mm_dh", dproj, full["w_in"], mode="nt", out_dtype=F32)
    dx, gs["ln_mix_w"] = _rms_bwd(x, small["ln_mix_w"], dh, dx1, "rms_mix_bwd")
    return loss, dx, gs


WEIGHT_ORDER = ("ln_mix_w", "w_in", "hg_norm_w", "hg_lower_bounds", "w_branch_a", "w_branch_b", "w_out",
                "ln_cross_w", "ln_mem_w", "wq_cross", "wkv_cross", "wo_cross", "ln_ffn_w", "w1", "w3", "w2",
                "ln_final_w")


def kernel(x, mem, ln_mix_w, w_in, hg_norm_w, hg_lower_bounds, w_branch_a, w_branch_b, w_out, ln_cross_w, ln_mem_w, wq_cross, wkv_cross, wo_cross, ln_ffn_w, w1, w3, w2, ln_final_w, loss_target, m_ln_mix_w, m_w_in, m_hg_norm_w, m_hg_lower_bounds, m_w_branch_a, m_w_branch_b, m_w_out, m_ln_cross_w, m_ln_mem_w, m_wq_cross, m_wkv_cross, m_wo_cross, m_ln_ffn_w, m_w1, m_w3, m_w2, m_ln_final_w, v_ln_mix_w, v_w_in, v_hg_norm_w, v_hg_lower_bounds, v_w_branch_a, v_w_branch_b, v_w_out, v_ln_cross_w, v_ln_mem_w, v_wq_cross, v_wkv_cross, v_wo_cross, v_ln_ffn_w, v_w1, v_w3, v_w2, v_ln_final_w):
    args = dict(locals())
    w = {n: args[n] for n in WEIGHT_ORDER}
    m = {n: args["m_" + n] for n in WEIGHT_ORDER}
    v = {n: args["v_" + n] for n in WEIGHT_ORDER}
    shapes = {n: w[n].shape for n in WEIGHT_ORDER}
    mat = lambda a: a.reshape(a.shape[-2:])
    shard_shapes = {n: shapes[n][-2:] for n in BIG_WEIGHTS}

    pos = _mesh_scalars()
    fulls = {f: _cast_into_full({n: mat(w[n]) for n in BIG_WEIGHTS if WEIGHT_PLACE[n][0] == f}, f, pos,
                                FULL_SPECS, WEIGHT_PLACE, f"cast_{f}") for f in FULL_SPECS}
    net = _Net(fulls, pos, shard_shapes)
    net.start(GATHER_GROUPS, "gather", "gather_start")
    for i, job in enumerate(W_IN_PIECES):
        net.wait(f"w_in{i}")
        net.run_comm(("d2d", [job]), f"gather_hand_over_w_in{i}")
    small = {n: w[n].reshape(1, -1) for n in SMALL_ROWS}
    small["hg_norm_w"] = w["hg_norm_w"].reshape(1, HEAD_DIM)
    small["hg_lower_bounds"] = w["hg_lower_bounds"]
    loss, dx, gs = _local_step(net, x.reshape(SEQ, D_MODEL), mem.reshape(MEM_LEN, D_MODEL),
                               loss_target.reshape(SEQ, D_MODEL), small)

    out_g, out_d, out_m, out_v = {}, {}, {}, {}
    net.last = dx
    for step in FINISH:
        if step[0] != "adamw":
            net.step(step)
            continue
        for n in step[1]:
            out_d[n], out_m[n], out_v[n], out_g[n] = net.call(_adamw, f"adamw_{n}", mat(w[n]), net.grads[n],
                                                              mat(m[n]), mat(v[n]))

    pad = lambda a: jnp.pad(a, ((0, 0), (0, D_MODEL - a.shape[1])))
    part = jnp.concatenate(
        [gs[n] for n in SMALL_ROWS]
        + [pad(jnp.concatenate([gs["hg_norm_w"][0], loss], axis=1)), pad(gs["hg_lb"]),
           pad(gs["hg_norm_w"][1])], axis=0)
    sg, sd, sm, sv, loss_tot = _small_update(_gather_rows(part), _pack_small(w), _pack_small(m), _pack_small(v))
    for dst, packed in ((out_g, sg), (out_d, sd), (out_m, sm), (out_v, sv)):
        dst.update(_unpack_small(packed, shapes))

    result = [loss_tot[0, 0], dx.reshape(x.shape)]
    for group in (out_g, out_d, out_m, out_v):
        result += [group[n].reshape(shapes[n]) for n in WEIGHT_ORDER]
    return tuple(result)
```

```python
import math

import jax
import jax.numpy as jnp
from jax import lax
from jax.experimental import pallas as pl
from jax.experimental.pallas import tpu as pltpu

F32 = jnp.float32
BF16 = jnp.bfloat16
MESH = pl.DeviceIdType.MESH

D_MODEL = 2048
SEQ = 2048
HEAD_DIM = 128
MEM_LEN = 256
ATT_GROUPS = ((128, 1), (512, 4), (2048, 16))
ATT_HEADS = 4
ATT_WIDTH = 3 * ATT_HEADS * HEAD_DIM
ATT_OUT = ATT_HEADS * HEAD_DIM
ATT_BLOCK = 128
HG_HEADS = 8
HG_WIDTH = HG_HEADS * HEAD_DIM
HG_CHUNK = 64
IN_WIDTH = 3 * ATT_WIDTH + 4 * HG_WIDTH + 2 * D_MODEL
CROSS_HEADS = 4
CROSS_WIDTH = CROSS_HEADS * HEAD_DIM
D_FF = 5632
RMS_EPS = 1e-6
ADAM_LR = 0.001
ADAM_B1 = 0.9
ADAM_B2 = 0.999
ADAM_EPS = 1e-08
ADAM_WD = 0.01
ADAM_STEP = 10
N_CHIPS = 4
N_DEV = 8

VMEM_LIMIT_BYTES = 56 * 1024 * 1024
LANE = 128
MXU_WIDTH = 256
MM_TILE_CAP = 1536
TRANSPOSE_CHUNK = 512
ANY = pl.BlockSpec(memory_space=pl.ANY)


def _cparams(sem=None):
    return pltpu.CompilerParams(dimension_semantics=sem, vmem_limit_bytes=VMEM_LIMIT_BYTES)


def _div(n, cap, mult):
    best = None
    for d in range(mult, min(n, cap) + 1, mult):
        if n % d == 0:
            best = d
    assert best is not None, (n, cap, mult)
    return best


def _sigmoid(x):
    return 1.0 / (1.0 + jnp.exp(-x))


def _dot(a, b):
    return jnp.dot(a.astype(BF16), b.astype(BF16), preferred_element_type=F32)


def _dot_nt(a, b):
    return lax.dot_general(a.astype(BF16), b.astype(BF16), (((1,), (1,)), ((), ())),
                           preferred_element_type=F32)


def _dot_tn(a, b):
    return jnp.dot(a.astype(F32).T.astype(BF16), b.astype(BF16), preferred_element_type=F32)


def _dot_exact(a, b):
    return jnp.dot(a, b, precision=lax.Precision.HIGHEST, preferred_element_type=F32)


class _Carried:
    def __init__(self, arrays, fresh, n_sems, start, finish, mid=None):
        self.arrays, self.fresh, self.n_sems = arrays, fresh, n_sems
        self.start, self.mid, self.finish = start, mid, finish


class _Token:
    def __init__(self, array):
        self.array = array


TOKEN_SHAPE = (8, LANE)


def _carried_layout(carried):
    akeys = list(dict.fromkeys(k for cm in carried for k in cm.arrays))
    fkeys = [(ci, k) for ci, cm in enumerate(carried) for k in cm.fresh]
    arrays = [next(cm.arrays[k] for cm in carried if k in cm.arrays) for k in akeys]
    shapes = [jax.ShapeDtypeStruct(a.shape, a.dtype) for a in arrays] + [carried[ci].fresh[k] for ci, k in fkeys]
    sems = []
    for cm in carried:
        sems += [pltpu.SemaphoreType.DMA((cm.n_sems,)), pltpu.SemaphoreType.DMA((cm.n_sems,))]
    return akeys, fkeys, arrays, shapes, sems


def _carried_results(carried, akeys, fkeys, outs):
    shared = dict(zip(akeys, outs[:len(akeys)]))
    res = [{k: shared[k] for k in cm.arrays} for cm in carried]
    for (ci, k), o in zip(fkeys, outs[len(akeys):]):
        res[ci][k] = o
    return res


def _pcall(kern, *, name, grid, in_specs, out_specs, out_shape, args, scratch_shapes=(), semantics=None,
           carried=()):
    tokens = [c.array for c in carried if isinstance(c, _Token)]
    carried = [c for c in carried if not isinstance(c, _Token)]
    single = not isinstance(out_shape, (list, tuple))
    out_specs = [out_specs] if single else list(out_specs)
    out_shape = [out_shape] if single else list(out_shape)
    n_real, n_out, n_scr = len(in_specs), len(out_shape), len(scratch_shapes)
    in_specs = list(in_specs) + [pl.BlockSpec(TOKEN_SHAPE, lambda *_: (0, 0))] * len(tokens)
    args = list(args) + tokens
    n_in = len(in_specs)
    if not carried:
        def plain(*refs):
            kern(*refs[:n_real], *refs[n_in:])

        outs = pl.pallas_call(plain if tokens else kern, name=name, grid=grid, in_specs=in_specs,
                              out_specs=out_specs, out_shape=out_shape, scratch_shapes=list(scratch_shapes),
                              compiler_params=_cparams(semantics))(*args)
        return (outs[0] if single else list(outs)), []
    akeys, fkeys, arrays, shapes, sems = _carried_layout(carried)
    n_a, n_f = len(akeys), len(fkeys)
    total = math.prod(grid)
    mid_step = min(total - 1, (17 * total) // 20)

    def wrapped(*refs):
        ins = refs[:n_real]
        o0 = n_in + n_a
        outs = refs[o0:o0 + n_out]
        a0 = o0 + n_out
        s0 = a0 + n_a + n_f
        per = _carried_results(carried, akeys, fkeys, refs[a0:s0])
        scratch = refs[s0:s0 + n_scr]
        sem = refs[s0 + n_scr:]
        step = 0
        for d, g in enumerate(grid):
            step = step * g + pl.program_id(d)

        @pl.when(step == 0)
        def _():
            for ci, cm in enumerate(carried):
                cm.start(per[ci], sem[2 * ci], sem[2 * ci + 1])

        kern(*ins, *outs, *scratch)

        @pl.when(step == mid_step)
        def _():
            for ci, cm in enumerate(carried):
                if cm.mid is not None:
                    cm.mid(per[ci], sem[2 * ci], sem[2 * ci + 1])

        @pl.when(step == total - 1)
        def _():
            for ci, cm in enumerate(carried):
                cm.finish(per[ci], sem[2 * ci], sem[2 * ci + 1])

    outs = pl.pallas_call(
        wrapped, name=name, grid=grid,
        in_specs=list(in_specs) + [ANY] * n_a, out_specs=out_specs + [ANY] * (n_a + n_f),
        out_shape=out_shape + shapes,
        input_output_aliases={n_in + i: n_out + i for i in range(n_a)},
        scratch_shapes=list(scratch_shapes) + sems,
        compiler_params=_cparams(("arbitrary",) * len(grid)),
    )(*args, *arrays)
    res = _carried_results(carried, akeys, fkeys, outs[n_out:])
    return (outs[0] if single else list(outs[:n_out])), res


def _run_comm(carried, name):
    carried = list(carried)
    akeys, fkeys, arrays, shapes, sems = _carried_layout(carried)
    n_a, n_f = len(akeys), len(fkeys)

    def body(*refs):
        per = _carried_results(carried, akeys, fkeys, refs[n_a:2 * n_a + n_f])
        sem = refs[2 * n_a + n_f:]
        for hook in ("start", "mid", "finish"):
            for ci, cm in enumerate(carried):
                fn = getattr(cm, hook)
                if fn is not None:
                    fn(per[ci], sem[2 * ci], sem[2 * ci + 1])

    outs = pl.pallas_call(
        body, name=name, in_specs=[ANY] * n_a, out_specs=[ANY] * (n_a + n_f), out_shape=shapes,
        input_output_aliases={i: i for i in range(n_a)}, scratch_shapes=sems,
    )(*arrays)
    return _carried_results(carried, akeys, fkeys, outs)


HBM_SPEC = pl.BlockSpec(memory_space=pltpu.HBM)
SEM_SPEC = pl.BlockSpec(memory_space=pltpu.SEMAPHORE)
SPLIT_EFFECT = pltpu.SideEffectType.DATAFLOW_SIDE_EFFECTING


def _in_hbm(a):
    return pltpu.with_memory_space_constraint(a, pltpu.HBM)


def _split_start(items, name):
    items = list(items)
    akeys, fkeys, arrays, shapes, sems = _carried_layout(items)
    assert not fkeys
    n_a, n_s = len(akeys), len(sems)

    def body(*refs):
        per = _carried_results(items, akeys, [], refs[n_a:2 * n_a])
        sem = refs[2 * n_a:2 * n_a + n_s]
        for ci, cm in enumerate(items):
            cm.start(per[ci], sem[2 * ci], sem[2 * ci + 1])
        token = refs[2 * n_a + n_s]
        token[...] = jnp.zeros_like(token)

    outs = pl.pallas_call(
        body, name=name, in_specs=[HBM_SPEC] * n_a,
        out_specs=[HBM_SPEC] * n_a + [SEM_SPEC] * n_s + [pl.BlockSpec(memory_space=pltpu.VMEM)],
        out_shape=[pltpu.HBM(s.shape, s.dtype) for s in shapes] + sems + [jax.ShapeDtypeStruct(TOKEN_SHAPE, F32)],
        input_output_aliases={i: i for i in range(n_a)},
        compiler_params=pltpu.CompilerParams(has_side_effects=SPLIT_EFFECT),
    )(*[_in_hbm(a) for a in arrays])
    res = _carried_results(items, akeys, [], outs[:n_a])
    sem_out = outs[n_a:n_a + n_s]
    return res, [(sem_out[2 * ci], sem_out[2 * ci + 1]) for ci in range(len(items))], outs[-1]


def _split_wait(items, sems, after, name):
    items = list(items)
    akeys, fkeys, arrays, shapes, _ = _carried_layout(items)
    n_a, n_s = len(akeys), 2 * len(items)

    def body(*refs):
        per = _carried_results(items, akeys, [], refs[n_a + n_s + 1:])
        sem = refs[n_a:n_a + n_s]
        for ci, cm in enumerate(items):
            cm.finish(per[ci], sem[2 * ci], sem[2 * ci + 1])

    outs = pl.pallas_call(
        body, name=name, in_specs=[HBM_SPEC] * n_a + [SEM_SPEC] * n_s + [ANY],
        out_specs=[HBM_SPEC] * n_a, out_shape=[pltpu.HBM(s.shape, s.dtype) for s in shapes],
        input_output_aliases={i: i for i in range(n_a)},
        compiler_params=pltpu.CompilerParams(has_side_effects=SPLIT_EFFECT),
    )(*arrays, *[s for pair in sems for s in pair], after)
    return _carried_results(items, akeys, [], outs)


def _mm(a, b, *, mode, out_dtype, name, res=None, carried=()):
    if mode == "nn":
        (m, k), (k2, n) = a.shape, b.shape
    elif mode == "nt":
        (m, k), (n, k2) = a.shape, b.shape
    else:
        (k, m), (k2, n) = a.shape, b.shape
    assert k == k2, (name, a.shape, b.shape)
    tm = _div(m, MM_TILE_CAP, LANE)
    tn = _div(n, MM_TILE_CAP, MXU_WIDTH) if n % MXU_WIDTH == 0 else 0
    if tn < 1024:
        tn = _div(n, MM_TILE_CAP, LANE)
    out_shape = jax.ShapeDtypeStruct((m, n), out_dtype)

    if mode == "tn":
        assert res is None

        def kern_tn(a_ref, b_ref, o_ref, at_ref):
            @pl.when(pl.program_id(1) == 0)
            def _():
                step = min(TRANSPOSE_CHUNK, k)
                for c0 in range(0, k, step):
                    at_ref[:, c0:c0 + step] = a_ref[c0:c0 + step, :].astype(F32).T.astype(BF16)

            o_ref[...] = jnp.dot(at_ref[...], b_ref[...].astype(BF16),
                                 preferred_element_type=F32).astype(o_ref.dtype)

        return _pcall(
            kern_tn, name=name, grid=(m // tm, n // tn),
            in_specs=[pl.BlockSpec((k, tm), lambda i, j: (0, i)),
                      pl.BlockSpec((k, tn), lambda i, j: (0, j))],
            out_specs=pl.BlockSpec((tm, tn), lambda i, j: (i, j)),
            out_shape=out_shape, args=(a, b),
            scratch_shapes=[pltpu.VMEM((tm, k), BF16)],
            semantics=("parallel", "arbitrary"), carried=carried)

    tk = k if k <= 2048 else _div(k, 3072, LANE)
    nk = k // tk
    a_spec = pl.BlockSpec((tm, tk), lambda i, j, kk: (i, kk))
    if mode == "nn":
        b_spec = pl.BlockSpec((tk, tn), lambda i, j, kk: (kk, j))
        dot = _dot
    else:
        b_spec = pl.BlockSpec((tn, tk), lambda i, j, kk: (j, kk))
        dot = _dot_nt
    o_spec = pl.BlockSpec((tm, tn), lambda i, j, kk: (i, j))
    in_specs = [a_spec, b_spec]
    args = [a, b]
    if res is not None:
        in_specs.append(o_spec)
        args.append(res)
    has_res = res is not None

    def kern(*refs):
        a_ref, b_ref = refs[0], refs[1]
        r_ref = refs[2] if has_res else None
        o_ref = refs[3] if has_res else refs[2]
        part = dot(a_ref[...], b_ref[...])
        if nk == 1:
            if has_res:
                part = part + r_ref[...]
            o_ref[...] = part.astype(o_ref.dtype)
            return
        acc_ref = refs[-1]
        kk = pl.program_id(2)

        @pl.when(kk == 0)
        def _():
            acc_ref[...] = part

        @pl.when(kk > 0)
        def _():
            acc_ref[...] += part

        @pl.when(kk == nk - 1)
        def _():
            tot = acc_ref[...]
            if has_res:
                tot = tot + r_ref[...]
            o_ref[...] = tot.astype(o_ref.dtype)

    return _pcall(
        kern, name=name, grid=(m // tm, n // tn, nk),
        in_specs=in_specs, out_specs=o_spec, out_shape=out_shape, args=args,
        scratch_shapes=[pltpu.VMEM((tm, tn), F32)] if nk > 1 else [],
        semantics=("parallel", "parallel", "arbitrary"), carried=carried)


ROW_BLOCK = 256


def _rms_fwd(x, g, name):
    t, d = x.shape
    tr = min(ROW_BLOCK, t)

    def kern(x_ref, g_ref, o_ref):
        xf = x_ref[...]
        r = lax.rsqrt(jnp.mean(xf * xf, axis=-1, keepdims=True) + RMS_EPS)
        o_ref[...] = (xf * r * g_ref[...]).astype(o_ref.dtype)

    return pl.pallas_call(
        kern, name=name, grid=(t // tr,),
        in_specs=[pl.BlockSpec((tr, d), lambda i: (i, 0)), pl.BlockSpec((1, d), lambda i: (0, 0))],
        out_specs=pl.BlockSpec((tr, d), lambda i: (i, 0)),
        out_shape=jax.ShapeDtypeStruct((t, d), BF16),
        compiler_params=_cparams(("parallel",)),
    )(x, g)


def _rms_bwd(x, g, dh, res, name):
    t, d = x.shape
    tr = min(ROW_BLOCK, t)
    has_res = res is not None

    def kern(*refs):
        x_ref, g_ref, dh_ref = refs[:3]
        r_ref = refs[3] if has_res else None
        dx_ref, dg_ref = refs[-2], refs[-1]
        xf = x_ref[...]
        r = lax.rsqrt(jnp.mean(xf * xf, axis=-1, keepdims=True) + RMS_EPS)
        xn = xf * r
        dh_ = dh_ref[...]
        dhg = dh_ * g_ref[...]
        dx = r * (dhg - xn * jnp.mean(dhg * xn, axis=-1, keepdims=True))
        if has_res:
            dx = dx + r_ref[...]
        dx_ref[...] = dx
        part = jnp.sum(dh_ * xn, axis=0, keepdims=True)

        @pl.when(pl.program_id(0) == 0)
        def _():
            dg_ref[...] = part

        @pl.when(pl.program_id(0) > 0)
        def _():
            dg_ref[...] += part

    row = pl.BlockSpec((tr, d), lambda i: (i, 0))
    vec = pl.BlockSpec((1, d), lambda i: (0, 0))
    in_specs = [row, vec, row] + ([row] if has_res else [])
    args = [x, g, dh] + ([res] if has_res else [])
    return pl.pallas_call(
        kern, name=name, grid=(t // tr,), in_specs=in_specs, out_specs=[row, vec],
        out_shape=[jax.ShapeDtypeStruct((t, d), F32), jax.ShapeDtypeStruct((1, d), F32)],
        compiler_params=_cparams(("arbitrary",)),
    )(*args)


def _loss_head(x3, g, target, name):
    t, d = x3.shape
    tr = ROW_BLOCK

    def kern(x_ref, g_ref, t_ref, dx_ref, dg_ref, loss_ref):
        xf = x_ref[...]
        r = lax.rsqrt(jnp.mean(xf * xf, axis=-1, keepdims=True) + RMS_EPS)
        xn = xf * r
        gg = g_ref[...]
        err = xn * gg - t_ref[...]
        lpart = 0.5 * jnp.sum(jnp.mean(err * err, axis=-1, keepdims=True), axis=0, keepdims=True)
        dy = err * (1.0 / d)
        dyg = dy * gg
        dx_ref[...] = r * (dyg - xn * jnp.mean(dyg * xn, axis=-1, keepdims=True))
        gpart = jnp.sum(dy * xn, axis=0, keepdims=True)
        lrow = jnp.broadcast_to(lpart, (1, LANE))

        @pl.when(pl.program_id(0) == 0)
        def _():
            dg_ref[...] = gpart
            loss_ref[...] = lrow

        @pl.when(pl.program_id(0) > 0)
        def _():
            dg_ref[...] += gpart
            loss_ref[...] += lrow

    row = pl.BlockSpec((tr, d), lambda i: (i, 0))
    vec = pl.BlockSpec((1, d), lambda i: (0, 0))
    return pl.pallas_call(
        kern, name=name, grid=(t // tr,), in_specs=[row, vec, row],
        out_specs=[row, vec, pl.BlockSpec((1, LANE), lambda i: (0, 0))],
        out_shape=[jax.ShapeDtypeStruct((t, d), F32), jax.ShapeDtypeStruct((1, d), F32),
                   jax.ShapeDtypeStruct((1, LANE), F32)],
        compiler_params=_cparams(("arbitrary",)),
    )(x3, g, target)


ATT_SCALE = HEAD_DIM ** -0.5
Q_BLOCK0, K_BLOCK0, V_BLOCK0 = 0, ATT_WIDTH // HEAD_DIM, 2 * ATT_WIDTH // HEAD_DIM


def _residue_rows(dil, r, n):
    if dil == 1:
        return pl.ds(n * ATT_BLOCK, ATT_BLOCK)
    return pl.ds(n * ATT_BLOCK * dil + r, ATT_BLOCK, stride=dil)


def _band_mask(with_prev):
    width = 2 * ATT_BLOCK if with_prev else ATT_BLOCK
    iq = lax.broadcasted_iota(jnp.int32, (ATT_BLOCK, width), 0)
    ik = lax.broadcasted_iota(jnp.int32, (ATT_BLOCK, width), 1)
    if not with_prev:
        return ik <= iq
    return ((ik < ATT_BLOCK) & (iq <= ik)) | ((ik >= ATT_BLOCK) & ((ik - ATT_BLOCK) <= iq))


def _band_keys(ref, dil, r, n):
    own = ref[_residue_rows(dil, r, n), :]
    if n == 0:
        return own
    return jnp.concatenate([ref[_residue_rows(dil, r, n - 1), :], own], axis=0)


def _attn_col_spec(base, grp):
    return pl.BlockSpec((SEQ, HEAD_DIM), lambda h: (0, base + grp * ATT_HEADS + h))


def _attn_fwd(proj, grp, name, carried=()):
    _, dil = ATT_GROUPS[grp]
    nb = SEQ // dil // ATT_BLOCK

    def kern(q_ref, k_ref, v_ref, o_ref, lse_ref):
        for r in range(dil):
            for n in range(nb):
                rows = _residue_rows(dil, r, n)
                s = _dot_nt(q_ref[rows, :], _band_keys(k_ref, dil, r, n)) * ATT_SCALE
                s = jnp.where(_band_mask(n > 0), s, -jnp.inf)
                m = jnp.max(s, axis=-1, keepdims=True)
                p = jnp.exp(s - m)
                l = jnp.sum(p, axis=-1, keepdims=True)
                o_ref[rows, :] = _dot(p / l, _band_keys(v_ref, dil, r, n))
                lse_ref[rows, :] = jnp.broadcast_to(m + jnp.log(l), (ATT_BLOCK, HEAD_DIM))

    out_spec = pl.BlockSpec((SEQ, HEAD_DIM), lambda h: (0, h))
    return _pcall(
        kern, name=name, grid=(ATT_HEADS,),
        in_specs=[_attn_col_spec(Q_BLOCK0, grp), _attn_col_spec(K_BLOCK0, grp), _attn_col_spec(V_BLOCK0, grp)],
        out_specs=[out_spec, out_spec],
        out_shape=[jax.ShapeDtypeStruct((SEQ, ATT_OUT), F32)] * 2, args=(proj, proj, proj),
        semantics=("parallel",), carried=carried)


def _attn_weights(l0, l1, l2):
    mx = jnp.maximum(jnp.maximum(l0, l1), l2)
    e0, e1, e2 = jnp.exp(l0 - mx), jnp.exp(l1 - mx), jnp.exp(l2 - mx)
    den = e0 + e1 + e2
    return e0 / den, e1 / den, e2 / den


def _attn_merge_fwd(outs, lses, name):
    tr = ROW_BLOCK

    def kern(o0, o1, o2, l0, l1, l2, out_ref):
        a0, a1, a2 = _attn_weights(l0[...], l1[...], l2[...])
        out_ref[...] = (a0 * o0[...] + a1 * o1[...] + a2 * o2[...]).astype(out_ref.dtype)

    spec = pl.BlockSpec((tr, ATT_OUT), lambda i: (i, 0))
    return pl.pallas_call(
        kern, name=name, grid=(SEQ // tr,), in_specs=[spec] * 6, out_specs=spec,
        out_shape=jax.ShapeDtypeStruct((SEQ, ATT_OUT), BF16),
        compiler_params=_cparams(("parallel",)),
    )(*outs, *lses)


def _attn_merge_bwd(outs, lses, do_att, name, carried=()):
    tr = ROW_BLOCK

    def kern(o0, o1, o2, l0, l1, l2, do_ref, d0, d1, d2, t0, t1, t2):
        alphas = _attn_weights(l0[...], l1[...], l2[...])
        do = do_ref[...]
        o_att = alphas[0] * o0[...] + alphas[1] * o1[...] + alphas[2] * o2[...]
        prod = do * o_att
        parts = []
        for h in range(ATT_HEADS):
            sl = slice(h * HEAD_DIM, (h + 1) * HEAD_DIM)
            tot = jnp.sum(prod[:, sl], axis=-1, keepdims=True)
            parts.append(jnp.broadcast_to(tot, (tr, HEAD_DIM)))
        dd = jnp.concatenate(parts, axis=1)
        for a, d_ref, t_ref in zip(alphas, (d0, d1, d2), (t0, t1, t2)):
            d_ref[...] = a * do
            t_ref[...] = -a * dd

    spec = pl.BlockSpec((tr, ATT_OUT), lambda i: (i, 0))
    res, cres = _pcall(
        kern, name=name, grid=(SEQ // tr,), in_specs=[spec] * 7, out_specs=[spec] * 6,
        out_shape=[jax.ShapeDtypeStruct((SEQ, ATT_OUT), F32)] * 6, args=(*outs, *lses, do_att),
        semantics=("parallel",), carried=carried)
    return (res[:3], res[3:]), cres


def _attn_bwd(proj, grp, lse, do_g, dl_g, name, carried=()):
    _, dil = ATT_GROUPS[grp]
    nb = SEQ // dil // ATT_BLOCK

    def kern(q_ref, k_ref, v_ref, do_ref, lse_ref, dl_ref, dq_ref, dk_ref, dv_ref, dq_acc, dk_acc, dv_acc):
        dk_acc[...] = jnp.zeros_like(dk_acc)
        dv_acc[...] = jnp.zeros_like(dv_acc)
        for r in range(dil):
            for n in range(nb):
                rows = _residue_rows(dil, r, n)
                q, do = q_ref[rows, :], do_ref[rows, :]
                kk, vv = _band_keys(k_ref, dil, r, n), _band_keys(v_ref, dil, r, n)
                s = _dot_nt(q, kk) * ATT_SCALE
                p = jnp.where(_band_mask(n > 0), jnp.exp(s - lse_ref[rows, :][:, :1]), 0.0)
                ds = p * (_dot_nt(do, vv) + dl_ref[rows, :][:, :1])
                dq_acc[rows, :] = _dot(ds, kk) * ATT_SCALE
                dk = _dot_tn(ds, q) * ATT_SCALE
                dv = _dot_tn(p, do)
                if n > 0:
                    prev = _residue_rows(dil, r, n - 1)
                    dk_acc[prev, :] += dk[:ATT_BLOCK]
                    dv_acc[prev, :] += dv[:ATT_BLOCK]
                    dk, dv = dk[ATT_BLOCK:], dv[ATT_BLOCK:]
                dk_acc[rows, :] += dk
                dv_acc[rows, :] += dv
        dq_ref[...] = dq_acc[...].astype(dq_ref.dtype)
        dk_ref[...] = dk_acc[...].astype(dk_ref.dtype)
        dv_ref[...] = dv_acc[...].astype(dv_ref.dtype)

    spec = pl.BlockSpec((SEQ, HEAD_DIM), lambda h: (0, h))
    return _pcall(
        kern, name=name, grid=(ATT_HEADS,),
        in_specs=[_attn_col_spec(Q_BLOCK0, grp), _attn_col_spec(K_BLOCK0, grp), _attn_col_spec(V_BLOCK0, grp),
                  spec, spec, spec],
        out_specs=[spec] * 3,
        out_shape=[jax.ShapeDtypeStruct((SEQ, ATT_OUT), BF16)] * 3, args=(proj, proj, proj, do_g, lse, dl_g),
        scratch_shapes=[pltpu.VMEM((SEQ, HEAD_DIM), F32)] * 3,
        semantics=("parallel",), carried=carried)


HG_HEADS_PER_STEP = 4
HG_BLOCK_W = HG_HEADS_PER_STEP * HEAD_DIM
HG_Q_BLK = (3 * ATT_WIDTH) // HG_BLOCK_W
HG_N_CHUNKS = SEQ // HG_CHUNK
HG_MID = HG_CHUNK // 2


def _lower_bound(lb_ref, sl):
    l0, l1 = lb_ref[0:1, sl], lb_ref[1:2, sl]
    mx = jnp.maximum(l0, l1)
    e0, e1 = jnp.exp(l0 - mx), jnp.exp(l1 - mx)
    return e0 / (e0 + e1)


def _tri(lower):
    i = lax.broadcasted_iota(jnp.int32, (HG_CHUNK, HG_CHUNK), 0)
    j = lax.broadcasted_iota(jnp.int32, (HG_CHUNK, HG_CHUNK), 1)
    return (i >= j) if lower else (i <= j)


def _hg_chunk_terms(qh, fh, lb):
    sig = _sigmoid(fh)
    f = lb + (1.0 - lb) * sig
    k = 1.0 - f
    b = _dot_exact(_tri(True).astype(F32), jnp.log(f))
    bl = b[HG_CHUNK - 1:HG_CHUNK, :]
    br = b[HG_MID:HG_MID + 1, :]
    sq = _sigmoid(qh)
    q = qh * sq
    return dict(sig=sig, f=f, k=k, b=b, bl=bl, br=br, sq=sq, q=q,
                e1=jnp.exp(bl - b), e2=jnp.exp(b), e3=jnp.exp(b - br), e4=jnp.exp(br - b))


def _hg_fwd(proj, lbw, normw, name, carried=()):
    def in_blk(off):
        return pl.BlockSpec((HG_CHUNK, HG_BLOCK_W), lambda hp, n: (n, HG_Q_BLK + off + hp))

    def kern(q_ref, f_ref, i_ref, g_ref, lb_ref, nw_ref, oraw_ref, ohg_ref, st_ref, state):
        @pl.when(pl.program_id(1) == 0)
        def _():
            state[...] = jnp.zeros_like(state)

        causal = _tri(True)
        for hd in range(HG_HEADS_PER_STEP):
            sl = slice(hd * HEAD_DIM, (hd + 1) * HEAD_DIM)
            t = _hg_chunk_terms(q_ref[:, sl], f_ref[:, sl], _lower_bound(lb_ref, sl))
            v = i_ref[:, sl]
            st = state[hd]
            st_ref[0, hd] = st
            kd = t["k"] * t["e1"]
            inter = _dot_nt(t["q"] * t["e2"], st)
            a = jnp.where(causal, _dot_nt(t["q"] * t["e3"], t["k"] * t["e4"]), 0.0)
            o = inter + _dot(a, v)
            state[hd] = st * jnp.exp(t["bl"]) + _dot_tn(v, kd)
            oraw_ref[:, sl] = o
            r = lax.rsqrt(jnp.mean(o * o, axis=-1, keepdims=True) + RMS_EPS)
            gh = g_ref[:, sl]
            ohg_ref[:, sl] = (o * r * nw_ref[...] * (gh * _sigmoid(gh))).astype(ohg_ref.dtype)

    out_blk = pl.BlockSpec((HG_CHUNK, HG_BLOCK_W), lambda hp, n: (n, hp))
    return _pcall(
        kern, name=name, grid=(HG_HEADS // HG_HEADS_PER_STEP, HG_N_CHUNKS),
        in_specs=[in_blk(0), in_blk(2), in_blk(4), in_blk(6),
                  pl.BlockSpec((2, HG_BLOCK_W), lambda hp, n: (0, hp)),
                  pl.BlockSpec((1, HEAD_DIM), lambda hp, n: (0, 0))],
        out_specs=[out_blk, out_blk,
                   pl.BlockSpec((1, HG_HEADS_PER_STEP, HEAD_DIM, HEAD_DIM), lambda hp, n: (n, hp, 0, 0))],
        out_shape=[jax.ShapeDtypeStruct((SEQ, HG_WIDTH), F32), jax.ShapeDtypeStruct((SEQ, HG_WIDTH), BF16),
                   jax.ShapeDtypeStruct((HG_N_CHUNKS, HG_HEADS, HEAD_DIM, HEAD_DIM), F32)],
        args=(proj, proj, proj, proj, lbw, normw),
        scratch_shapes=[pltpu.VMEM((HG_HEADS_PER_STEP, HEAD_DIM, HEAD_DIM), F32)],
        semantics=("parallel", "arbitrary"), carried=carried)


def _hg_bwd(proj, lbw, normw, oraw, states, do_hg, name, carried=()):
    last = HG_N_CHUNKS - 1

    def in_blk(off):
        return pl.BlockSpec((HG_CHUNK, HG_BLOCK_W), lambda hp, n: (last - n, HG_Q_BLK + off + hp))

    blk = pl.BlockSpec((HG_CHUNK, HG_BLOCK_W), lambda hp, n: (last - n, hp))

    def kern(q_ref, f_ref, i_ref, g_ref, lb_ref, nw_ref, oraw_ref, st_ref, do_ref,
             dq_ref, df_ref, di_ref, dg_ref, dlb_ref, dnw_ref, dstate):
        first = pl.program_id(1) == 0

        @pl.when(first)
        def _():
            dstate[...] = jnp.zeros_like(dstate)

        causal = _tri(True)
        rows = lax.broadcasted_iota(jnp.int32, (HG_CHUNK, HEAD_DIM), 0)
        nw = nw_ref[...]
        dnw_tot = jnp.zeros((1, HEAD_DIM), F32)
        dlb_parts = []
        for hd in range(HG_HEADS_PER_STEP):
            sl = slice(hd * HEAD_DIM, (hd + 1) * HEAD_DIM)
            qh, fh, v, gh = q_ref[:, sl], f_ref[:, sl], i_ref[:, sl], g_ref[:, sl]
            o, dout = oraw_ref[:, sl], do_ref[:, sl]
            sgg = _sigmoid(gh)
            r = lax.rsqrt(jnp.mean(o * o, axis=-1, keepdims=True) + RMS_EPS)
            xn = o * r
            dg_ref[:, sl] = (dout * xn * nw * (sgg * (1.0 + gh * (1.0 - sgg)))).astype(dg_ref.dtype)
            don = dout * (gh * sgg)
            dnw_tot = dnw_tot + jnp.sum(don * xn, axis=0, keepdims=True)
            tt = don * nw
            do = r * (tt - xn * jnp.mean(tt * xn, axis=-1, keepdims=True))
            lb = _lower_bound(lb_ref, sl)
            t = _hg_chunk_terms(qh, fh, lb)
            k, q = t["k"], t["q"]
            kd, qb, qr, kr = k * t["e1"], q * t["e2"], q * t["e3"], k * t["e4"]
            a = jnp.where(causal, _dot_nt(qr, kr), 0.0)
            st = st_ref[0, hd]
            dstn = dstate[hd]
            dqb = _dot(do, st)
            da = jnp.where(causal, _dot_nt(do, v), 0.0)
            dv = _dot_tn(a, do) + _dot_nt(kd, dstn)
            dqr = _dot(da, kr)
            dkr = _dot_tn(da, qr)
            dkd = _dot(v, dstn)
            decay = jnp.exp(t["bl"])
            ddecay = jnp.sum(dstn * st, axis=0, keepdims=True)
            dstate[hd] = dstn * decay + _dot_tn(do, qb)
            dq = dqb * t["e2"] + dqr * t["e3"]
            dk = dkd * t["e1"] + dkr * t["e4"]
            db = dqb * qb + dqr * qr - dkr * kr - dkd * kd
            dbl = jnp.sum(dkd * kd, axis=0, keepdims=True) + ddecay * decay
            dbr = jnp.sum(dkr * kr - dqr * qr, axis=0, keepdims=True)
            dlf = _dot_exact(_tri(False).astype(F32), db) + dbl + jnp.where(rows <= HG_MID, dbr, 0.0)
            df = dlf / t["f"] - dk
            sig, sq = t["sig"], t["sq"]
            df_ref[:, sl] = (df * (1.0 - lb) * sig * (1.0 - sig)).astype(df_ref.dtype)
            dlb_parts.append(jnp.sum(df * (1.0 - sig), axis=0, keepdims=True))
            dq_ref[:, sl] = (dq * (sq * (1.0 + qh * (1.0 - sq)))).astype(dq_ref.dtype)
            di_ref[:, sl] = dv.astype(di_ref.dtype)
        dlb_row = jnp.concatenate(dlb_parts, axis=1)
        dnw_blk = jnp.broadcast_to(dnw_tot, (8, HEAD_DIM))

        @pl.when(first)
        def _():
            dlb_ref[...] = dlb_row
            dnw_ref[...] = dnw_blk

        @pl.when(jnp.logical_not(first))
        def _():
            dlb_ref[...] += dlb_row
            dnw_ref[...] += dnw_blk

    n_hp = HG_HEADS // HG_HEADS_PER_STEP
    outs, cres = _pcall(
        kern, name=name, grid=(n_hp, HG_N_CHUNKS),
        in_specs=[in_blk(0), in_blk(2), in_blk(4), in_blk(6),
                  pl.BlockSpec((2, HG_BLOCK_W), lambda hp, n: (0, hp)),
                  pl.BlockSpec((1, HEAD_DIM), lambda hp, n: (0, 0)),
                  blk,
                  pl.BlockSpec((1, HG_HEADS_PER_STEP, HEAD_DIM, HEAD_DIM), lambda hp, n: (last - n, hp, 0, 0)),
                  blk],
        out_specs=[blk, blk, blk, blk,
                   pl.BlockSpec((1, HG_BLOCK_W), lambda hp, n: (0, hp)),
                   pl.BlockSpec((8, HEAD_DIM), lambda hp, n: (hp, 0))],
        out_shape=[jax.ShapeDtypeStruct((SEQ, HG_WIDTH), BF16)] * 4
        + [jax.ShapeDtypeStruct((1, HG_WIDTH), F32), jax.ShapeDtypeStruct((8 * n_hp, HEAD_DIM), F32)],
        args=(proj, proj, proj, proj, lbw, normw, oraw, states, do_hg),
        scratch_shapes=[pltpu.VMEM((HG_HEADS_PER_STEP, HEAD_DIM, HEAD_DIM), F32)],
        semantics=("parallel", "arbitrary"), carried=carried)
    dqh, dfh, dih, dgh, dlb, dnw = outs
    return (dqh, dfh, dih, dgh, dlb, (dnw[0:1], dnw[8:9])), cres


GATE_BLOCK_W = 512
GATE_A_BLK = (3 * ATT_WIDTH + 4 * HG_WIDTH) // GATE_BLOCK_W
GATE_B_BLK = GATE_A_BLK + D_MODEL // GATE_BLOCK_W


def _gate_specs():
    tr = ROW_BLOCK
    blk = pl.BlockSpec((tr, GATE_BLOCK_W), lambda i, j: (i, j))
    ga = pl.BlockSpec((tr, GATE_BLOCK_W), lambda i, j: (i, GATE_A_BLK + j))
    gb = pl.BlockSpec((tr, GATE_BLOCK_W), lambda i, j: (i, GATE_B_BLK + j))
    return (SEQ // tr, D_MODEL // GATE_BLOCK_W), blk, ga, gb


def _gate_fwd(proj, ya, yb, name, carried=()):
    grid, blk, ga, gb = _gate_specs()

    def kern(ga_ref, gb_ref, ya_ref, yb_ref, o_ref):
        o_ref[...] = (_sigmoid(ga_ref[...]) * ya_ref[...] + _sigmoid(gb_ref[...]) * yb_ref[...]).astype(o_ref.dtype)

    return _pcall(
        kern, name=name, grid=grid, in_specs=[ga, gb, blk, blk], out_specs=blk,
        out_shape=jax.ShapeDtypeStruct((SEQ, D_MODEL), BF16), args=(proj, proj, ya, yb),
        semantics=("parallel", "parallel"), carried=carried)


def _gate_bwd(proj, ya, yb, dmerged, name, carried=()):
    grid, blk, ga, gb = _gate_specs()

    def kern(ga_ref, gb_ref, ya_ref, yb_ref, dm_ref, dya_ref, dyb_ref, dga_ref, dgb_ref):
        dm = dm_ref[...]
        sa, sb = _sigmoid(ga_ref[...]), _sigmoid(gb_ref[...])
        dya_ref[...] = (dm * sa).astype(dya_ref.dtype)
        dyb_ref[...] = (dm * sb).astype(dyb_ref.dtype)
        dga_ref[...] = (dm * ya_ref[...] * sa * (1.0 - sa)).astype(dga_ref.dtype)
        dgb_ref[...] = (dm * yb_ref[...] * sb * (1.0 - sb)).astype(dgb_ref.dtype)

    return _pcall(
        kern, name=name, grid=grid, in_specs=[ga, gb, blk, blk, blk], out_specs=[blk] * 4,
        out_shape=[jax.ShapeDtypeStruct((SEQ, D_MODEL), BF16)] * 4, args=(proj, proj, ya, yb, dmerged),
        semantics=("parallel", "parallel"), carried=carried)


FF_SHARD = D_FF // N_CHIPS


def _swiglu_fwd(ab, name):
    tr = ROW_BLOCK

    def kern(ab_ref, u_ref):
        a, b = ab_ref[:, :FF_SHARD], ab_ref[:, FF_SHARD:]
        u_ref[...] = (a * _sigmoid(a) * b).astype(u_ref.dtype)

    return pl.pallas_call(
        kern, name=name, grid=(SEQ // tr, N_CHIPS),
        in_specs=[pl.BlockSpec((tr, 2 * FF_SHARD), lambda i, j: (i, j))],
        out_specs=pl.BlockSpec((tr, FF_SHARD), lambda i, j: (i, j)),
        out_shape=jax.ShapeDtypeStruct((SEQ, D_FF), BF16),
        compiler_params=_cparams(("parallel", "parallel")),
    )(ab)


def _swiglu_bwd(ab, du, name, carried=()):
    tr = ROW_BLOCK

    def kern(ab_ref, du_ref, dab_ref):
        a, b = ab_ref[:, :FF_SHARD], ab_ref[:, FF_SHARD:]
        du_ = du_ref[...]
        sg = _sigmoid(a)
        dab_ref[:, :FF_SHARD] = (du_ * b * (sg * (1.0 + a * (1.0 - sg)))).astype(dab_ref.dtype)
        dab_ref[:, FF_SHARD:] = (du_ * (a * sg)).astype(dab_ref.dtype)

    wide = pl.BlockSpec((tr, 2 * FF_SHARD), lambda i, j: (i, j))
    return _pcall(
        kern, name=name, grid=(SEQ // tr, N_CHIPS),
        in_specs=[wide, pl.BlockSpec((tr, FF_SHARD), lambda i, j: (i, j))],
        out_specs=wide, out_shape=jax.ShapeDtypeStruct((SEQ, 2 * D_FF), BF16), args=(ab, du),
        semantics=("parallel", "parallel"), carried=carried)


CROSS_ROWS = 512


def _cross_fwd(qc, kvc, name):
    def kern(q_ref, k_ref, v_ref, o_ref):
        s = _dot_nt(q_ref[...], k_ref[...]) * ATT_SCALE
        m = jnp.max(s, axis=-1, keepdims=True)
        e = jnp.exp(s - m)
        p = e / jnp.sum(e, axis=-1, keepdims=True)
        o_ref[...] = _dot(p, v_ref[...]).astype(o_ref.dtype)

    qblk = pl.BlockSpec((CROSS_ROWS, HEAD_DIM), lambda h, i: (i, h))
    return pl.pallas_call(
        kern, name=name, grid=(CROSS_HEADS, SEQ // CROSS_ROWS),
        in_specs=[qblk, pl.BlockSpec((MEM_LEN, HEAD_DIM), lambda h, i: (0, h)),
                  pl.BlockSpec((MEM_LEN, HEAD_DIM), lambda h, i: (0, CROSS_HEADS + h))],
        out_specs=qblk, out_shape=jax.ShapeDtypeStruct((SEQ, CROSS_WIDTH), BF16),
        compiler_params=_cparams(("parallel", "parallel")),
    )(qc, kvc, kvc)


def _cross_bwd(qc, kvc, doc, name):
    def kern(q_ref, k_ref, v_ref, do_ref, dq_ref, dk_ref, dv_ref):
        q, k, v, do = q_ref[...], k_ref[...], v_ref[...], do_ref[...]
        s = _dot_nt(q, k) * ATT_SCALE
        m = jnp.max(s, axis=-1, keepdims=True)
        e = jnp.exp(s - m)
        p = e / jnp.sum(e, axis=-1, keepdims=True)
        dp = _dot_nt(do, v)
        ds = p * (dp - jnp.sum(dp * p, axis=-1, keepdims=True))
        dq_ref[...] = (_dot(ds, k) * ATT_SCALE).astype(dq_ref.dtype)
        dk = _dot_tn(ds, q) * ATT_SCALE
        dv = _dot_tn(p, do)

        @pl.when(pl.program_id(1) == 0)
        def _():
            dk_ref[...] = dk
            dv_ref[...] = dv

        @pl.when(pl.program_id(1) > 0)
        def _():
            dk_ref[...] += dk
            dv_ref[...] += dv

    qblk = pl.BlockSpec((CROSS_ROWS, HEAD_DIM), lambda h, i: (i, h))
    kblk = pl.BlockSpec((MEM_LEN, HEAD_DIM), lambda h, i: (0, h))
    dq, dk, dv = pl.pallas_call(
        kern, name=name, grid=(CROSS_HEADS, SEQ // CROSS_ROWS),
        in_specs=[qblk, kblk, pl.BlockSpec((MEM_LEN, HEAD_DIM), lambda h, i: (0, CROSS_HEADS + h)), qblk],
        out_specs=[qblk, kblk, kblk],
        out_shape=[jax.ShapeDtypeStruct((SEQ, CROSS_WIDTH), BF16),
                   jax.ShapeDtypeStruct((MEM_LEN, CROSS_WIDTH), F32),
                   jax.ShapeDtypeStruct((MEM_LEN, CROSS_WIDTH), F32)],
        compiler_params=_cparams(("parallel", "arbitrary")),
    )(qc, kvc, kvc, doc)
    return dq, jnp.concatenate([dk, dv], axis=1)


FULL_SPECS = {
    "w_in": ("col", D_MODEL, IN_WIDTH),
    "w_branch_a": ("col", ATT_OUT, D_MODEL),
    "w_branch_b": ("col", HG_WIDTH, D_MODEL),
    "w_out": ("row", D_MODEL, D_MODEL),
    "wq_cross": ("row", D_MODEL, CROSS_WIDTH),
    "wkv_cross": ("row", D_MODEL, 2 * CROSS_WIDTH),
    "wo_cross": ("col", CROSS_WIDTH, D_MODEL),
    "w13": ("col", D_MODEL, 2 * D_FF),
    "w2": ("row", D_FF, D_MODEL),
}
WEIGHT_PLACE = {
    "w_in": ("w_in", 0), "w_branch_a": ("w_branch_a", 0), "w_branch_b": ("w_branch_b", 0),
    "w_out": ("w_out", 0), "wq_cross": ("wq_cross", 0), "wkv_cross": ("wkv_cross", 0),
    "wo_cross": ("wo_cross", 0), "w1": ("w13", 0), "w3": ("w13", FF_SHARD), "w2": ("w2", 0),
}
BIG_WEIGHTS = tuple(WEIGHT_PLACE)
EW_BLOCK_ELEMS = 512 * 1024


def _position():
    return lax.axis_index("x"), lax.axis_index("y"), lax.axis_index("c")


def _other_chips(x, y):
    return [(1 - x, y), (x, 1 - y), (1 - x, 1 - y)]


def _half(ref, kind, h):
    r, c = ref.shape
    if kind == "col":
        return ref.at[pl.ds(h * (r // 2), r // 2), :]
    return ref.at[:, pl.ds(h * (c // 2), c // 2)]


def _shard_of(ref, kind, start, size):
    return ref.at[:, pl.ds(start, size)] if kind == "col" else ref.at[pl.ds(start, size), :]


def _rows_of(ref, r0, nrows):
    return ref if nrows is None else ref.at[pl.ds(r0, nrows), :]


def _half_shape(kind, rows, cols):
    return (rows // 2, cols) if kind == "col" else (rows, cols // 2)


def _slot_shape(spec):
    kind, rows, cols = spec
    hr, hc = _half_shape(kind, rows, cols)
    return (hr, hc // N_CHIPS) if kind == "col" else (hr // N_CHIPS, hc)


def _remote(src, dst, send_sem, recv_sem, device):
    return pltpu.make_async_remote_copy(src_ref=src, dst_ref=dst, send_sem=send_sem, recv_sem=recv_sem,
                                        device_id=device, device_id_type=MESH)


def _gather_ici_comm(fulls, jobs, specs):
    def piece(refs, job, chip, c):
        f, r0, nr = job
        kind, rows, cols = specs[f]
        stride = (cols if kind == "col" else rows) // N_CHIPS
        return _rows_of(_half(_shard_of(refs[f], kind, chip * stride, stride), kind, c), r0, nr)

    def start(refs, ss, rs):
        x, y, c = _position()
        j = 2 * x + y
        for q, job in enumerate(jobs):
            for p, (px, py) in enumerate(_other_chips(x, y)):
                _remote(piece(refs, job, j, c), piece(refs, job, j, c), ss.at[3 * q + p], rs.at[3 * q + p],
                        (px, py, c)).start()

    def finish(refs, ss, rs):
        x, y, c = _position()
        j = 2 * x + y
        for q, job in enumerate(jobs):
            for p, (px, py) in enumerate(_other_chips(x, y)):
                _remote(piece(refs, job, j, c), piece(refs, job, 2 * px + py, c), ss.at[3 * q + p],
                        rs.at[3 * q + p], (px, py, c)).wait_recv()
        for q, job in enumerate(jobs):
            for p, (px, py) in enumerate(_other_chips(x, y)):
                _remote(piece(refs, job, j, c), piece(refs, job, j, c), ss.at[3 * q + p], rs.at[3 * q + p],
                        (px, py, c)).wait_send()

    names = list(dict.fromkeys(job[0] for job in jobs))
    return _Carried({f: fulls[f] for f in names}, {}, 3 * len(jobs), start, finish)


def _gather_d2d_comm(fulls, jobs, specs):
    def rect(refs, job, h):
        f, r0, nr = job
        assert nr is None or specs[f][0] == "col"
        return _rows_of(_half(refs[f], specs[f][0], h), r0, nr)

    def start(refs, ss, rs):
        x, y, c = _position()
        for q, job in enumerate(jobs):
            _remote(rect(refs, job, c), rect(refs, job, c), ss.at[q], rs.at[q], (x, y, 1 - c)).start()

    def finish(refs, ss, rs):
        x, y, c = _position()
        for q, job in enumerate(jobs):
            _remote(rect(refs, job, 1 - c), rect(refs, job, 1 - c), ss.at[q], rs.at[q], (x, y, 1 - c)).wait_recv()
        for q, job in enumerate(jobs):
            _remote(rect(refs, job, c), rect(refs, job, c), ss.at[q], rs.at[q], (x, y, 1 - c)).wait_send()

    names = list(dict.fromkeys(job[0] for job in jobs))
    return _Carried({f: fulls[f] for f in names}, {}, len(jobs), start, finish)


def _pairx_comm(grads, names, specs):
    def copies(refs, ss, rs):
        x, y, c = _position()
        return [_remote(_half(refs[("g", f)], specs[f][0], 1 - c), refs[("r", f)], ss.at[i], rs.at[i], (x, y, 1 - c))
                for i, f in enumerate(names)]

    def start(refs, ss, rs):
        for cp in copies(refs, ss, rs):
            cp.start()

    def finish(refs, ss, rs):
        for cp in copies(refs, ss, rs):
            cp.wait_recv()
        for cp in copies(refs, ss, rs):
            cp.wait_send()

    fresh = {("r", f): jax.ShapeDtypeStruct(_half_shape(*specs[f]), BF16) for f in names}
    return _Carried({("g", f): grads[f] for f in names}, fresh, len(names), start, finish)


def _chipx_comm(pair_sums, slots, jobs, specs):
    def copies(refs, ss, rs):
        x, y, c = _position()
        out = []
        for q, (f, r0, nr) in enumerate(jobs):
            kind = specs[f][0]
            width = _slot_shape(specs[f])[1 if kind == "col" else 0]
            for p, (px, py) in enumerate(_other_chips(x, y)):
                src = _rows_of(_shard_of(refs[("p", f)], kind, (2 * px + py) * width, width), r0, nr)
                dst = _rows_of(refs[("s", f)].at[p], r0, nr)
                out.append(_remote(src, dst, ss.at[3 * q + p], rs.at[3 * q + p], (px, py, c)))
        return out

    def start(refs, ss, rs):
        for cp in copies(refs, ss, rs):
            cp.start()

    def finish(refs, ss, rs):
        for cp in copies(refs, ss, rs):
            cp.wait_recv()
        for cp in copies(refs, ss, rs):
            cp.wait_send()

    names = list(dict.fromkeys(job[0] for job in jobs))
    arrays = {("p", f): pair_sums[f] for f in names}
    arrays.update({("s", f): slots[f] for f in names})
    return _Carried(arrays, {}, 3 * len(jobs), start, finish)


def _share_comm(grads, wnames, specs, place):
    def start(refs, ss, rs):
        x, y, c = _position()
        for i, w in enumerate(wnames):
            kind = specs[place[w][0]][0]
            _remote(_half(refs[w], kind, c), _half(refs[w], kind, c), ss.at[i], rs.at[i], (x, y, 1 - c)).start()

    def finish(refs, ss, rs):
        x, y, c = _position()
        for i, w in enumerate(wnames):
            kind = specs[place[w][0]][0]
            _remote(_half(refs[w], kind, 1 - c), _half(refs[w], kind, 1 - c), ss.at[i], rs.at[i],
                    (x, y, 1 - c)).wait_recv()
        for i, w in enumerate(wnames):
            kind = specs[place[w][0]][0]
            _remote(_half(refs[w], kind, c), _half(refs[w], kind, c), ss.at[i], rs.at[i], (x, y, 1 - c)).wait_send()

    return _Carried({w: grads[w] for w in wnames}, {}, len(wnames), start, finish)


def _gather_rows(v, name="gather_small"):
    shape = v.shape

    def body(v_ref, out_ref, send_sem, recv_sem, loc_sem):
        x, y, c = _position()
        me = 4 * x + 2 * y + c
        flips = [(fx, fy, fc) for fx in (0, 1) for fy in (0, 1) for fc in (0, 1)][1:]

        def peer(fl):
            return tuple(1 - a if f else a for a, f in zip((x, y, c), fl))

        loc = pltpu.make_async_copy(v_ref, out_ref.at[me], loc_sem)
        loc.start()
        sends = []
        for i, fl in enumerate(flips):
            cp = _remote(v_ref, out_ref.at[me], send_sem.at[i], recv_sem.at[i], peer(fl))
            cp.start()
            sends.append(cp)
        for i, fl in enumerate(flips):
            px, py, pc = peer(fl)
            _remote(v_ref, out_ref.at[4 * px + 2 * py + pc], send_sem.at[i], recv_sem.at[i], peer(fl)).wait_recv()
        for cp in sends:
            cp.wait_send()
        loc.wait()

    return pl.pallas_call(
        body, name=name, in_specs=[ANY], out_specs=ANY,
        out_shape=jax.ShapeDtypeStruct((N_DEV,) + shape, F32),
        scratch_shapes=[pltpu.SemaphoreType.DMA((N_DEV - 1,)), pltpu.SemaphoreType.DMA((N_DEV - 1,)),
                        pltpu.SemaphoreType.DMA],
    )(v)


def _ew_block(rows, cols, elems=EW_BLOCK_ELEMS):
    tc = cols if cols <= 4096 else _div(cols, 2048, LANE)
    tr = _div(rows, max(16, elems // tc), 16)
    return tr, tc


def _mesh_scalars():
    x, y, c = _position()
    return jnp.stack([c, 2 * x + y]).astype(jnp.int32)


def _grid_spec(grid, in_specs, out_specs):
    return pltpu.PrefetchScalarGridSpec(num_scalar_prefetch=1, grid=grid, in_specs=in_specs, out_specs=out_specs)


def _cast_into_full(parts, fname, pos, specs, place, name):
    kind, rows, cols = specs[fname]
    ws = [w for w in place if place[w][0] == fname]
    if kind == "col":
        stride = cols // N_CHIPS
        hr = rows // 2
        tr = _div(hr, max(16, EW_BLOCK_ELEMS // stride), 16)
        nrb = hr // tr
        in_specs = [pl.BlockSpec((tr, parts[w].shape[1]), lambda i, pos_ref: (i + pos_ref[0] * nrb, 0)) for w in ws]
        out_spec = pl.BlockSpec((tr, stride), lambda i, pos_ref: (i + pos_ref[0] * nrb, pos_ref[1]))
    else:
        stride = rows // N_CHIPS
        hc = cols // 2
        tr = _div(stride, max(16, EW_BLOCK_ELEMS // hc), 16)
        nrb = stride // tr
        in_specs = [pl.BlockSpec((tr, hc), lambda i, pos_ref: (i, pos_ref[0])) for w in ws]
        out_spec = pl.BlockSpec((tr, hc), lambda i, pos_ref: (i + pos_ref[1] * nrb, pos_ref[0]))

    def kern(pos_ref, *refs):
        o_ref = refs[-1]
        for w, r in zip(ws, refs[:-1]):
            off = place[w][1] if kind == "col" else 0
            o_ref[:, off:off + r.shape[1]] = r[...].astype(o_ref.dtype)

    return pl.pallas_call(
        kern, name=name, grid_spec=_grid_spec((nrb,), in_specs, out_spec),
        out_shape=jax.ShapeDtypeStruct((rows, cols), BF16),
        compiler_params=_cparams(("parallel",)),
    )(pos, *[parts[w] for w in ws])


def _pair_sum(grad, recv, pos, spec, name):
    kind, rows, cols = spec
    hr, hc = _half_shape(kind, rows, cols)
    tr, tc = _ew_block(hr, hc, 2 * EW_BLOCK_ELEMS)
    nrb, ncb = hr // tr, hc // tc
    blk = pl.BlockSpec((tr, tc), lambda i, jj, pos_ref: (i, jj))
    if kind == "col":
        mine = pl.BlockSpec((tr, tc), lambda i, jj, pos_ref: (i + pos_ref[0] * nrb, jj))
    else:
        mine = pl.BlockSpec((tr, tc), lambda i, jj, pos_ref: (i, jj + pos_ref[0] * ncb))

    def kern(pos_ref, g_ref, r_ref, o_ref, slots_ref):
        o_ref[...] = (g_ref[...].astype(F32) + r_ref[...].astype(F32)).astype(o_ref.dtype)

    return pl.pallas_call(
        kern, name=name, grid_spec=_grid_spec((nrb, ncb), [mine, blk], [blk, ANY]),
        out_shape=[jax.ShapeDtypeStruct((hr, hc), BF16),
                   jax.ShapeDtypeStruct((N_CHIPS - 1,) + _slot_shape(spec), BF16)],
        compiler_params=_cparams(("parallel", "parallel")),
    )(pos, grad, recv)


def _chip_sum(pair_sum, slots, pos, fname, shard_shapes, specs, place, name):
    kind, rows, cols = specs[fname]
    sr, sc = _slot_shape(specs[fname])
    ws = [w for w in place if place[w][0] == fname]
    n_slots = N_CHIPS - 1
    tr = _div(sr, max(16, EW_BLOCK_ELEMS // sc), 16)
    nrb = sr // tr
    slot = pl.BlockSpec((n_slots, tr, sc), lambda i, pos_ref: (0, i, 0))
    if kind == "col":
        own = pl.BlockSpec((tr, sc), lambda i, pos_ref: (i, pos_ref[1]))
        out_specs = [pl.BlockSpec((tr, shard_shapes[w][1]), lambda i, pos_ref: (i + pos_ref[0] * nrb, 0)) for w in ws]
    else:
        own = pl.BlockSpec((tr, sc), lambda i, pos_ref: (i + pos_ref[1] * nrb, 0))
        out_specs = [pl.BlockSpec((tr, sc), lambda i, pos_ref: (i, pos_ref[0])) for w in ws]

    def kern(pos_ref, own_ref, slot_ref, *out_refs):
        tot = own_ref[...].astype(F32)
        for s in range(n_slots):
            tot = tot + slot_ref[s].astype(F32)
        for w, o_ref in zip(ws, out_refs):
            off = place[w][1] if kind == "col" else 0
            o_ref[...] = tot[:, off:off + o_ref.shape[1]]

    outs = pl.pallas_call(
        kern, name=name, grid_spec=_grid_spec((nrb,), [own, slot], out_specs),
        out_shape=[jax.ShapeDtypeStruct(shard_shapes[w], F32) for w in ws],
        compiler_params=_cparams(("parallel",)),
    )(pos, pair_sum, slots)
    return dict(zip(ws, outs))


def _adam_math(w, g, m, v):
    m2 = ADAM_B1 * m + (1.0 - ADAM_B1) * g
    v2 = ADAM_B2 * v + (1.0 - ADAM_B2) * (g * g)
    m_hat = m2 / (1.0 - ADAM_B1 ** ADAM_STEP)
    v_hat = v2 / (1.0 - ADAM_B2 ** ADAM_STEP)
    delta = -ADAM_LR * (m_hat / (jnp.sqrt(v_hat) + ADAM_EPS) + ADAM_WD * w)
    return delta, m2, v2


def _adamw(w, g, m, v, name, carried=()):
    rows, cols = w.shape
    tr, tc = _ew_block(rows, cols)

    def kern(w_ref, g_ref, m_ref, v_ref, d_ref, m2_ref, v2_ref, g_out_ref):
        g_ = g_ref[...]
        d_ref[...], m2_ref[...], v2_ref[...] = _adam_math(w_ref[...], g_, m_ref[...], v_ref[...])
        g_out_ref[...] = g_

    blk = pl.BlockSpec((tr, tc), lambda i, j: (i, j))
    return _pcall(
        kern, name=name, grid=(rows // tr, cols // tc), in_specs=[blk] * 4, out_specs=[blk] * 4,
        out_shape=[jax.ShapeDtypeStruct((rows, cols), F32)] * 4, args=(w, g, m, v),
        semantics=("parallel", "parallel"), carried=carried)


SMALL_ROWS = ("ln_mix_w", "ln_cross_w", "ln_mem_w", "ln_ffn_w", "ln_final_w")
ROW_HG_NORM, ROW_LB0, ROW_LB1 = 5, 6, 7
LOSS_LANE0 = HEAD_DIM


def _pack_small(vals):
    rows = [vals[n].reshape(1, D_MODEL) for n in SMALL_ROWS]
    pad = lambda a: jnp.pad(a, ((0, 0), (0, D_MODEL - a.shape[1])))
    rows.append(pad(vals["hg_norm_w"].reshape(1, HEAD_DIM)))
    rows.append(pad(vals["hg_lower_bounds"].reshape(2, HG_WIDTH)))
    return jnp.concatenate(rows, axis=0)


def _small_update(gathered, w, m, v, name="small_update"):
    def kern(g_ref, w_ref, m_ref, v_ref, grad_ref, d_ref, m2_ref, v2_ref, loss_ref):
        tot = g_ref[0]
        for s in range(1, N_DEV):
            tot = tot + g_ref[s]
        wv = w_ref[...]
        row = lax.broadcasted_iota(jnp.int32, (8, D_MODEL), 0)
        lane = lax.broadcasted_iota(jnp.int32, (8, D_MODEL), 1)
        l0, l1 = wv[ROW_LB0:ROW_LB0 + 1], wv[ROW_LB1:ROW_LB1 + 1]
        mx = jnp.maximum(l0, l1)
        e0, e1 = jnp.exp(l0 - mx), jnp.exp(l1 - mx)
        p0 = e0 / (e0 + e1)
        dlog = tot[ROW_LB0:ROW_LB0 + 1] * p0 * (1.0 - p0)
        tot = jnp.where(row == ROW_HG_NORM, tot + tot[ROW_LB1:ROW_LB1 + 1], tot)
        grad = jnp.where(row == ROW_LB0, dlog, jnp.where(row == ROW_LB1, -dlog, tot))
        grad = jnp.where((row == ROW_HG_NORM) & (lane >= HEAD_DIM), 0.0, grad)
        grad = jnp.where((row >= ROW_LB0) & (lane >= HG_WIDTH), 0.0, grad)
        grad_ref[...] = grad
        d_ref[...], m2_ref[...], v2_ref[...] = _adam_math(wv, grad, m_ref[...], v_ref[...])
        loss_ref[...] = tot[ROW_HG_NORM:ROW_HG_NORM + 1, LOSS_LANE0:LOSS_LANE0 + LANE]

    full = pl.BlockSpec((8, D_MODEL), lambda: (0, 0))
    return pl.pallas_call(
        kern, name=name,
        in_specs=[pl.BlockSpec((N_DEV, 8, D_MODEL), lambda: (0, 0, 0)), full, full, full],
        out_specs=[full, full, full, full, pl.BlockSpec((1, LANE), lambda: (0, 0))],
        out_shape=[jax.ShapeDtypeStruct((8, D_MODEL), F32)] * 4 + [jax.ShapeDtypeStruct((1, LANE), F32)],
        compiler_params=_cparams(),
    )(gathered, w, m, v)


def _unpack_small(p, shapes):
    out = {n: p[i].reshape(shapes[n]) for i, n in enumerate(SMALL_ROWS)}
    out["hg_norm_w"] = p[ROW_HG_NORM, :HEAD_DIM].reshape(shapes["hg_norm_w"])
    out["hg_lower_bounds"] = p[ROW_LB0:ROW_LB1 + 1, :HG_WIDTH].reshape(shapes["hg_lower_bounds"])
    return out


WHOLE = lambda f: (f, 0, None)
MID_MATRICES = ("w_branch_a", "w_branch_b", "w_out", "wq_cross", "wkv_cross", "wo_cross")
MID_WEIGHTS = MID_MATRICES
W_IN_PIECES = [("w_in", r0, 256) for r0 in range(0, D_MODEL // 2, 256)]
GATHER_GROUPS = [(f"w_in{i}", [job]) for i, job in enumerate(W_IN_PIECES)] + [
    ("mid", [WHOLE(f) for f in MID_MATRICES]), ("w13", [WHOLE("w13")]), ("w2", [WHOLE("w2")])]
OTHER_WEIGHTS = ["w1", "w3", "w2"] + list(MID_WEIGHTS)
BEFORE = {
    "hgrn_fwd": [("wait", "mid")],
    "gate_fwd": [("wait", "w13")],
    "mm_w13": [("wait", "w2")],
    "mm_dh": [("wait", "rs_w2"), ("chip_sum", "w2"), ("wait", "rs_w13"), ("chip_sum", "w13"), ("wait", "rs_mid")]
    + [("chip_sum", f) for f in MID_MATRICES],
}
CARRY = {
    "hgrn_fwd": [("d2d", [WHOLE(f) for f in MID_MATRICES])],
    "gate_fwd": [("d2d", [WHOLE("w13")])],
    "mm_w13": [("d2d", [WHOLE("w2")])],
    "swiglu_bwd": [("pairx", ["w2"])],
    "mm_dhf": [("pairx", ["w13"])],
    "attn_merge_bwd": [("pairx", list(MID_MATRICES))],
    "mm_dh": [("share", OTHER_WEIGHTS)],
}
AFTER = {
    "swiglu_bwd": [("pair_sum", "w2"), ("start", "rs_w2", [WHOLE("w2")])],
    "mm_dhf": [("pair_sum", "w13"), ("start", "rs_w13", [WHOLE("w13")])],
    "attn_merge_bwd": [("pair_sum", f) for f in MID_MATRICES] + [("start", "rs_mid", [WHOLE(f) for f in MID_MATRICES])],
    "mm_dwin": [("run", ("pairx", ["w_in"]), "rs_pair_exchange_w_in"), ("pair_sum", "w_in"),
                ("start", "rs_w_in", [WHOLE("w_in")])],
}
FINISH = [
    ("adamw", OTHER_WEIGHTS), ("wait", "rs_w_in"), ("small",), ("chip_sum", "w_in"),
    ("run", ("share", ["w_in"]), "rs_sibling_share_w_in"), ("adamw", ["w_in"]),
]


class _Net:
    def __init__(self, full, pos=None, shard_shapes=None, comm=True, specs=FULL_SPECS, place=WEIGHT_PLACE):
        self.full, self.pos, self.shard_shapes, self.comm = dict(full), pos, shard_shapes, comm
        self.specs, self.place = specs, place
        self.gw, self.recv, self.psum, self.slots, self.grads = {}, {}, {}, {}, {}
        self.pending, self.token, self.last = {}, None, None

    def _make(self, kind, arg):
        if kind == "gather":
            return _gather_ici_comm(self.full, arg, self.specs)
        if kind == "d2d":
            return _gather_d2d_comm(self.full, arg, self.specs)
        if kind == "pairx":
            return _pairx_comm(self.gw, arg, self.specs)
        if kind == "chipx":
            return _chipx_comm(self.psum, self.slots, arg, self.specs)
        assert kind == "share"
        return _share_comm(self.grads, arg, self.specs, self.place)

    def _store(self, kind, res):
        if kind in ("gather", "d2d"):
            self.full.update(res)
        elif kind == "pairx":
            for (tag, f), a in res.items():
                (self.gw if tag == "g" else self.recv)[f] = a
        elif kind == "chipx":
            for (tag, f), a in res.items():
                (self.psum if tag == "p" else self.slots)[f] = a
        else:
            self.grads.update(res)

    def run_comm(self, item, name):
        kind, arg = item
        self._store(kind, _run_comm([self._make(kind, arg)], name)[0])

    def start(self, groups, kind, name):
        res, sems, token = _split_start([self._make(kind, jobs) for _, jobs in groups], name)
        for (group, jobs), r, s in zip(groups, res, sems):
            self._store(kind, r)
            self.pending[group] = (kind, jobs, s)
        self.token = self.last = token

    def wait(self, group):
        kind, jobs, sems = self.pending.pop(group)
        self._store(kind, _split_wait([self._make(kind, jobs)], [sems], self.last, f"wait_{group}")[0])

    def step(self, step):
        if step[0] == "wait":
            self.wait(step[1])
        elif step[0] == "start":
            self.start([(step[1], step[2])], "chipx", f"start_{step[1]}")
        elif step[0] == "pair_sum":
            f = step[1]
            self.psum[f], self.slots[f] = _pair_sum(self.gw[f], self.recv[f], self.pos, self.specs[f],
                                                    f"rs_pair_sum_{f}")
        elif step[0] == "chip_sum":
            f = step[1]
            self.grads.update(_chip_sum(self.psum[f], self.slots[f], self.pos, f, self.shard_shapes,
                                        self.specs, self.place, f"rs_chip_sum_{f}"))
        else:
            assert step[0] == "run"
            self.run_comm(step[1], step[2])

    def call(self, fn, name, *args, grad_of=None, **kw):
        for step in (BEFORE.get(name, []) if self.comm else []):
            self.step(step)
        items = CARRY.get(name, []) if self.comm else []
        carried = [self._make(k, a) for k, a in items]
        if self.token is not None:
            carried.append(_Token(self.token))
            self.token = None
        out, res = fn(*args, name=name, carried=carried, **kw)
        if grad_of is not None:
            self.gw[grad_of] = out
        self.last = jax.tree.leaves(out)[0]
        for (kind, _), r in zip(items, res):
            self._store(kind, r)
        for step in (AFTER.get(name, []) if self.comm else []):
            self.step(step)
        return out


def _local_step(net, x, h, mem, target, small):
    full, call = net.full, net.call
    proj = call(_mm, "mm_proj", h, full["w_in"], mode="nn", out_dtype=F32)
    att = [call(_attn_fwd, f"attn_fwd_g{g}", proj, g) for g in range(3)]
    outs, lses = [a[0] for a in att], [a[1] for a in att]
    o_att = _attn_merge_fwd(outs, lses, "attn_merge")
    oraw, o_hg, states = call(_hg_fwd, "hgrn_fwd", proj, small["hg_lower_bounds"], small["hg_norm_w"])
    ya = call(_mm, "mm_branch_a", o_att, full["w_branch_a"], mode="nn", out_dtype=F32)
    yb = call(_mm, "mm_branch_b", o_hg, full["w_branch_b"], mode="nn", out_dtype=F32)
    merged = call(_gate_fwd, "gate_fwd", proj, ya, yb)
    x1 = call(_mm, "mm_out", merged, full["w_out"], mode="nn", out_dtype=F32, res=x)

    hc = _rms_fwd(x1, small["ln_cross_w"], "rms_cross")
    mn = _rms_fwd(mem, small["ln_mem_w"], "rms_mem")
    qc = call(_mm, "mm_q", hc, full["wq_cross"], mode="nn", out_dtype=F32)
    kvc = call(_mm, "mm_kv", mn, full["wkv_cross"], mode="nn", out_dtype=F32)
    oc = _cross_fwd(qc, kvc, "cross_fwd")
    x2 = call(_mm, "mm_o", oc, full["wo_cross"], mode="nn", out_dtype=F32, res=x1)

    hf = _rms_fwd(x2, small["ln_ffn_w"], "rms_ffn")
    ab = call(_mm, "mm_w13", hf, full["w13"], mode="nn", out_dtype=F32)
    u = _swiglu_fwd(ab, "swiglu_fwd")
    x3 = call(_mm, "mm_w2", u, full["w2"], mode="nn", out_dtype=F32, res=x2)

    dx3, dg_final, loss = _loss_head(x3, small["ln_final_w"], target, "loss_head")

    gs = {"ln_final_w": dg_final}
    du = call(_mm, "mm_du", dx3, full["w2"], mode="nt", out_dtype=F32)
    call(_mm, "mm_dw2", u, dx3, mode="tn", out_dtype=BF16, grad_of="w2")
    dab = call(_swiglu_bwd, "swiglu_bwd", ab, du)
    call(_mm, "mm_dw13", hf, dab, mode="tn", out_dtype=BF16, grad_of="w13")
    dhf = call(_mm, "mm_dhf", dab, full["w13"], mode="nt", out_dtype=F32)
    dx2, gs["ln_ffn_w"] = _rms_bwd(x2, small["ln_ffn_w"], dhf, dx3, "rms_ffn_bwd")
    doc = call(_mm, "mm_doc", dx2, full["wo_cross"], mode="nt", out_dtype=BF16)
    call(_mm, "mm_dwo", oc, dx2, mode="tn", out_dtype=BF16, grad_of="wo_cross")
    dqc, dkvc = _cross_bwd(qc, kvc, doc, "cross_bwd")
    call(_mm, "mm_dwq", hc, dqc, mode="tn", out_dtype=BF16, grad_of="wq_cross")
    dhc = call(_mm, "mm_dhc", dqc, full["wq_cross"], mode="nt", out_dtype=F32)
    call(_mm, "mm_dwkv", mn, dkvc, mode="tn", out_dtype=BF16, grad_of="wkv_cross")
    dmn = call(_mm, "mm_dmn", dkvc, full["wkv_cross"], mode="nt", out_dtype=F32)
    _, gs["ln_mem_w"] = _rms_bwd(mem, small["ln_mem_w"], dmn, None, "rms_mem_bwd")
    dx1, gs["ln_cross_w"] = _rms_bwd(x1, small["ln_cross_w"], dhc, dx2, "rms_cross_bwd")
    dmerged = call(_mm, "mm_dmerged", dx1, full["w_out"], mode="nt", out_dtype=F32)
    call(_mm, "mm_dwout", merged, dx1, mode="tn", out_dtype=BF16, grad_of="w_out")
    dya, dyb, dga, dgb = call(_gate_bwd, "gate_bwd", proj, ya, yb, dmerged)
    call(_mm, "mm_dwa", o_att, dya, mode="tn", out_dtype=BF16, grad_of="w_branch_a")
    do_att = call(_mm, "mm_doatt", dya, full["w_branch_a"], mode="nt", out_dtype=F32)
    call(_mm, "mm_dwb", o_hg, dyb, mode="tn", out_dtype=BF16, grad_of="w_branch_b")
    do_hg = call(_mm, "mm_dohg", dyb, full["w_branch_b"], mode="nt", out_dtype=F32)
    dqh, dfh, dih, dgh, dlb, gs["hg_norm_w"] = call(
        _hg_bwd, "hgrn_bwd", proj, small["hg_lower_bounds"], small["hg_norm_w"], oraw, states, do_hg)
    gs["hg_lb"] = dlb
    do_gs, dl_gs = call(_attn_merge_bwd, "attn_merge_bwd", outs, lses, do_att)
    dqs, dks, dvs = zip(*[call(_attn_bwd, f"attn_bwd_g{g}", proj, g, lses[g], do_gs[g], dl_gs[g]) for g in range(3)])
    dproj = jnp.concatenate([*dqs, *dks, *dvs, dqh, dfh, dih, dgh, dga, dgb], axis=1)
    call(_mm, "mm_dwin", h, dproj, mode="tn", out_dtype=BF16, grad_of="w_in")
    dh = call(_mm, "mm_dh", dproj, full["w_in"], mode="nt", out_dtype=F32)
    dx, gs["ln_mix_w"] = _rms_bwd(x, small["ln_mix_w"], dh, dx1, "rms_mix_bwd")
    return loss, dx, gs


WEIGHT_ORDER = ("ln_mix_w", "w_in", "hg_norm_w", "hg_lower_bounds", "w_branch_a", "w_branch_b", "w_out",
                "ln_cross_w", "ln_mem_w", "wq_cross", "wkv_cross", "wo_cross", "ln_ffn_w", "w1", "w3", "w2",
                "ln_final_w")


def kernel(x, mem, ln_mix_w, w_in, hg_norm_w, hg_lower_bounds, w_branch_a, w_branch_b, w_out, ln_cross_w, ln_mem_w, wq_cross, wkv_cross, wo_cross, ln_ffn_w, w1, w3, w2, ln_final_w, loss_target, m_ln_mix_w, m_w_in, m_hg_norm_w, m_hg_lower_bounds, m_w_branch_a, m_w_branch_b, m_w_out, m_ln_cross_w, m_ln_mem_w, m_wq_cross, m_wkv_cross, m_wo_cross, m_ln_ffn_w, m_w1, m_w3, m_w2, m_ln_final_w, v_ln_mix_w, v_w_in, v_hg_norm_w, v_hg_lower_bounds, v_w_branch_a, v_w_branch_b, v_w_out, v_ln_cross_w, v_ln_mem_w, v_wq_cross, v_wkv_cross, v_wo_cross, v_ln_ffn_w, v_w1, v_w3, v_w2, v_ln_final_w):
    args = dict(locals())
    w = {n: args[n] for n in WEIGHT_ORDER}
    m = {n: args["m_" + n] for n in WEIGHT_ORDER}
    v = {n: args["v_" + n] for n in WEIGHT_ORDER}
    shapes = {n: w[n].shape for n in WEIGHT_ORDER}
    mat = lambda a: a.reshape(a.shape[-2:])
    shard_shapes = {n: shapes[n][-2:] for n in BIG_WEIGHTS}

    pos = _mesh_scalars()

    def cast(f):
        return _cast_into_full({n: mat(w[n]) for n in BIG_WEIGHTS if WEIGHT_PLACE[n][0] == f}, f, pos,
                               FULL_SPECS, WEIGHT_PLACE, f"cast_{f}")

    n_first = len(W_IN_PIECES)
    net = _Net({"w_in": cast("w_in")}, pos, shard_shapes)
    net.start(GATHER_GROUPS[:n_first], "gather", "gather_start_w_in")
    net.full.update({f: cast(f) for f in FULL_SPECS if f != "w_in"})
    net.start(GATHER_GROUPS[n_first:], "gather", "gather_start_rest")
    small = {n: w[n].reshape(1, -1) for n in SMALL_ROWS}
    small["hg_norm_w"] = w["hg_norm_w"].reshape(1, HEAD_DIM)
    small["hg_lower_bounds"] = w["hg_lower_bounds"]
    x2d = x.reshape(SEQ, D_MODEL)
    h = _rms_fwd(x2d, small["ln_mix_w"], "rms_mix")
    for i, job in enumerate(W_IN_PIECES):
        net.wait(f"w_in{i}")
        net.run_comm(("d2d", [job]), f"gather_hand_over_w_in{i}")
    loss, dx, gs = _local_step(net, x2d, h, mem.reshape(MEM_LEN, D_MODEL), loss_target.reshape(SEQ, D_MODEL), small)

    out_g, out_d, out_m, out_v = {}, {}, {}, {}
    net.last = dx
    for step in FINISH:
        if step[0] == "adamw":
            for n in step[1]:
                out_d[n], out_m[n], out_v[n], out_g[n] = net.call(_adamw, f"adamw_{n}", mat(w[n]), net.grads[n],
                                                                  mat(m[n]), mat(v[n]))
        elif step[0] == "small":
            pad = lambda a: jnp.pad(a, ((0, 0), (0, D_MODEL - a.shape[1])))
            part = jnp.concatenate(
                [gs[n] for n in SMALL_ROWS]
                + [pad(jnp.concatenate([gs["hg_norm_w"][0], loss], axis=1)), pad(gs["hg_lb"]),
                   pad(gs["hg_norm_w"][1])], axis=0)
            part, net.psum["w_in"] = lax.optimization_barrier((part, net.psum["w_in"]))
            sg, sd, sm, sv, loss_tot = _small_update(_gather_rows(part), _pack_small(w), _pack_small(m),
                                                     _pack_small(v))
            for dst, packed in ((out_g, sg), (out_d, sd), (out_m, sm), (out_v, sv)):
                dst.update(_unpack_small(packed, shapes))
        else:
            net.step(step)

    result = [loss_tot[0, 0], dx.reshape(x.shape)]
    for group in (out_g, out_d, out_m, out_v):
        result += [group[n].reshape(shapes[n]) for n in WEIGHT_ORDER]
    return tuple(result)
```

```python
import math

import jax
import jax.numpy as jnp
from jax import lax
from jax.experimental import pallas as pl
from jax.experimental.pallas import tpu as pltpu

F32 = jnp.float32
BF16 = jnp.bfloat16
MESH = pl.DeviceIdType.MESH

D_MODEL = 2048
SEQ = 2048
HEAD_DIM = 128
MEM_LEN = 256
ATT_GROUPS = ((128, 1), (512, 4), (2048, 16))
ATT_HEADS = 4
ATT_WIDTH = 3 * ATT_HEADS * HEAD_DIM
ATT_OUT = ATT_HEADS * HEAD_DIM
ATT_BLOCK = 128
HG_HEADS = 8
HG_WIDTH = HG_HEADS * HEAD_DIM
HG_CHUNK = 64
IN_WIDTH = 3 * ATT_WIDTH + 4 * HG_WIDTH + 2 * D_MODEL
CROSS_HEADS = 4
CROSS_WIDTH = CROSS_HEADS * HEAD_DIM
D_FF = 5632
RMS_EPS = 1e-6
ADAM_LR = 0.001
ADAM_B1 = 0.9
ADAM_B2 = 0.999
ADAM_EPS = 1e-08
ADAM_WD = 0.01
ADAM_STEP = 10
N_CHIPS = 4
N_DEV = 8

VMEM_LIMIT_BYTES = 56 * 1024 * 1024
LANE = 128
MXU_WIDTH = 256
MM_TILE_CAP = 1536
TRANSPOSE_CHUNK = 512
ANY = pl.BlockSpec(memory_space=pl.ANY)


def _cparams(sem=None):
    return pltpu.CompilerParams(dimension_semantics=sem, vmem_limit_bytes=VMEM_LIMIT_BYTES)


def _div(n, cap, mult):
    best = None
    for d in range(mult, min(n, cap) + 1, mult):
        if n % d == 0:
            best = d
    assert best is not None, (n, cap, mult)
    return best


def _sigmoid(x):
    return 1.0 / (1.0 + jnp.exp(-x))


def _dot(a, b):
    return jnp.dot(a.astype(BF16), b.astype(BF16), preferred_element_type=F32)


def _dot_nt(a, b):
    return lax.dot_general(a.astype(BF16), b.astype(BF16), (((1,), (1,)), ((), ())),
                           preferred_element_type=F32)


def _dot_tn(a, b):
    return jnp.dot(a.astype(F32).T.astype(BF16), b.astype(BF16), preferred_element_type=F32)


def _dot_exact(a, b):
    return jnp.dot(a, b, precision=lax.Precision.HIGHEST, preferred_element_type=F32)


class _Carried:
    def __init__(self, arrays, fresh, n_sems, start, finish, mid=None):
        self.arrays, self.fresh, self.n_sems = arrays, fresh, n_sems
        self.start, self.mid, self.finish = start, mid, finish


class _Token:
    def __init__(self, array):
        self.array = array


TOKEN_SHAPE = (8, LANE)


def _carried_layout(carried):
    akeys = list(dict.fromkeys(k for cm in carried for k in cm.arrays))
    fkeys = [(ci, k) for ci, cm in enumerate(carried) for k in cm.fresh]
    arrays = [next(cm.arrays[k] for cm in carried if k in cm.arrays) for k in akeys]
    shapes = [jax.ShapeDtypeStruct(a.shape, a.dtype) for a in arrays] + [carried[ci].fresh[k] for ci, k in fkeys]
    sems = []
    for cm in carried:
        sems += [pltpu.SemaphoreType.DMA((cm.n_sems,)), pltpu.SemaphoreType.DMA((cm.n_sems,))]
    return akeys, fkeys, arrays, shapes, sems


def _carried_results(carried, akeys, fkeys, outs):
    shared = dict(zip(akeys, outs[:len(akeys)]))
    res = [{k: shared[k] for k in cm.arrays} for cm in carried]
    for (ci, k), o in zip(fkeys, outs[len(akeys):]):
        res[ci][k] = o
    return res


def _pcall(kern, *, name, grid, in_specs, out_specs, out_shape, args, scratch_shapes=(), semantics=None,
           carried=()):
    tokens = [c.array for c in carried if isinstance(c, _Token)]
    carried = [c for c in carried if not isinstance(c, _Token)]
    single = not isinstance(out_shape, (list, tuple))
    out_specs = [out_specs] if single else list(out_specs)
    out_shape = [out_shape] if single else list(out_shape)
    n_real, n_out, n_scr = len(in_specs), len(out_shape), len(scratch_shapes)
    in_specs = list(in_specs) + [pl.BlockSpec(TOKEN_SHAPE, lambda *_: (0, 0))] * len(tokens)
    args = list(args) + tokens
    n_in = len(in_specs)
    if not carried:
        def plain(*refs):
            kern(*refs[:n_real], *refs[n_in:])

        outs = pl.pallas_call(plain if tokens else kern, name=name, grid=grid, in_specs=in_specs,
                              out_specs=out_specs, out_shape=out_shape, scratch_shapes=list(scratch_shapes),
                              compiler_params=_cparams(semantics))(*args)
        return (outs[0] if single else list(outs)), []
    akeys, fkeys, arrays, shapes, sems = _carried_layout(carried)
    n_a, n_f = len(akeys), len(fkeys)
    total = math.prod(grid)
    mid_step = min(total - 1, (17 * total) // 20)

    def wrapped(*refs):
        ins = refs[:n_real]
        o0 = n_in + n_a
        outs = refs[o0:o0 + n_out]
        a0 = o0 + n_out
        s0 = a0 + n_a + n_f
        per = _carried_results(carried, akeys, fkeys, refs[a0:s0])
        scratch = refs[s0:s0 + n_scr]
        sem = refs[s0 + n_scr:]
        step = 0
        for d, g in enumerate(grid):
            step = step * g + pl.program_id(d)

        @pl.when(step == 0)
        def _():
            for ci, cm in enumerate(carried):
                cm.start(per[ci], sem[2 * ci], sem[2 * ci + 1])

        kern(*ins, *outs, *scratch)

        @pl.when(step == mid_step)
        def _():
            for ci, cm in enumerate(carried):
                if cm.mid is not None:
                    cm.mid(per[ci], sem[2 * ci], sem[2 * ci + 1])

        @pl.when(step == total - 1)
        def _():
            for ci, cm in enumerate(carried):
                cm.finish(per[ci], sem[2 * ci], sem[2 * ci + 1])

    outs = pl.pallas_call(
        wrapped, name=name, grid=grid,
        in_specs=list(in_specs) + [ANY] * n_a, out_specs=out_specs + [ANY] * (n_a + n_f),
        out_shape=out_shape + shapes,
        input_output_aliases={n_in + i: n_out + i for i in range(n_a)},
        scratch_shapes=list(scratch_shapes) + sems,
        compiler_params=_cparams(("arbitrary",) * len(grid)),
    )(*args, *arrays)
    res = _carried_results(carried, akeys, fkeys, outs[n_out:])
    return (outs[0] if single else list(outs[:n_out])), res


def _run_comm(carried, name):
    carried = list(carried)
    akeys, fkeys, arrays, shapes, sems = _carried_layout(carried)
    n_a, n_f = len(akeys), len(fkeys)

    def body(*refs):
        per = _carried_results(carried, akeys, fkeys, refs[n_a:2 * n_a + n_f])
        sem = refs[2 * n_a + n_f:]
        for hook in ("start", "mid", "finish"):
            for ci, cm in enumerate(carried):
                fn = getattr(cm, hook)
                if fn is not None:
                    fn(per[ci], sem[2 * ci], sem[2 * ci + 1])

    outs = pl.pallas_call(
        body, name=name, in_specs=[ANY] * n_a, out_specs=[ANY] * (n_a + n_f), out_shape=shapes,
        input_output_aliases={i: i for i in range(n_a)}, scratch_shapes=sems,
    )(*arrays)
    return _carried_results(carried, akeys, fkeys, outs)


HBM_SPEC = pl.BlockSpec(memory_space=pltpu.HBM)
SEM_SPEC = pl.BlockSpec(memory_space=pltpu.SEMAPHORE)
SPLIT_EFFECT = pltpu.SideEffectType.DATAFLOW_SIDE_EFFECTING


def _in_hbm(a):
    return pltpu.with_memory_space_constraint(a, pltpu.HBM)


def _split_start(items, name, after=None):
    items = list(items)
    akeys, fkeys, arrays, shapes, sems = _carried_layout(items)
    assert not fkeys
    n_a, n_s = len(akeys), len(sems)
    n_in = n_a + (after is not None)

    def body(*refs):
        per = _carried_results(items, akeys, [], refs[n_in:n_in + n_a])
        sem = refs[n_in + n_a:n_in + n_a + n_s]
        for ci, cm in enumerate(items):
            cm.start(per[ci], sem[2 * ci], sem[2 * ci + 1])
        token = refs[n_in + n_a + n_s]
        token[...] = jnp.zeros_like(token)

    outs = pl.pallas_call(
        body, name=name, in_specs=[HBM_SPEC] * n_a + [ANY] * (after is not None),
        out_specs=[HBM_SPEC] * n_a + [SEM_SPEC] * n_s + [pl.BlockSpec(memory_space=pltpu.VMEM)],
        out_shape=[pltpu.HBM(s.shape, s.dtype) for s in shapes] + sems + [jax.ShapeDtypeStruct(TOKEN_SHAPE, F32)],
        input_output_aliases={i: i for i in range(n_a)},
        compiler_params=pltpu.CompilerParams(has_side_effects=SPLIT_EFFECT),
    )(*[_in_hbm(a) for a in arrays], *([after] if after is not None else []))
    res = _carried_results(items, akeys, [], outs[:n_a])
    sem_out = outs[n_a:n_a + n_s]
    return res, [(sem_out[2 * ci], sem_out[2 * ci + 1]) for ci in range(len(items))], outs[-1]


def _split_wait(items, sems, after, name):
    items = list(items)
    after = list(after) if isinstance(after, (list, tuple)) else [after]
    akeys, fkeys, arrays, shapes, _ = _carried_layout(items)
    n_a, n_s = len(akeys), 2 * len(items)

    def body(*refs):
        per = _carried_results(items, akeys, [], refs[n_a + n_s + len(after):])
        sem = refs[n_a:n_a + n_s]
        for ci, cm in enumerate(items):
            cm.finish(per[ci], sem[2 * ci], sem[2 * ci + 1])

    outs = pl.pallas_call(
        body, name=name, in_specs=[HBM_SPEC] * n_a + [SEM_SPEC] * n_s + [ANY] * len(after),
        out_specs=[HBM_SPEC] * n_a, out_shape=[pltpu.HBM(s.shape, s.dtype) for s in shapes],
        input_output_aliases={i: i for i in range(n_a)},
        compiler_params=pltpu.CompilerParams(has_side_effects=SPLIT_EFFECT),
    )(*arrays, *[s for pair in sems for s in pair], *after)
    return _carried_results(items, akeys, [], outs)


def _mm(a, b, *, mode, out_dtype, name, res=None, carried=()):
    if mode == "nn":
        (m, k), (k2, n) = a.shape, b.shape
    elif mode == "nt":
        (m, k), (n, k2) = a.shape, b.shape
    else:
        (k, m), (k2, n) = a.shape, b.shape
    assert k == k2, (name, a.shape, b.shape)
    tm = _div(m, MM_TILE_CAP, LANE)
    tn = _div(n, MM_TILE_CAP, MXU_WIDTH) if n % MXU_WIDTH == 0 else 0
    if tn < 1024:
        tn = _div(n, MM_TILE_CAP, LANE)
    out_shape = jax.ShapeDtypeStruct((m, n), out_dtype)

    if mode == "tn":
        assert res is None

        def kern_tn(a_ref, b_ref, o_ref, at_ref):
            @pl.when(pl.program_id(1) == 0)
            def _():
                step = min(TRANSPOSE_CHUNK, k)
                for c0 in range(0, k, step):
                    at_ref[:, c0:c0 + step] = a_ref[c0:c0 + step, :].astype(F32).T.astype(BF16)

            o_ref[...] = jnp.dot(at_ref[...], b_ref[...].astype(BF16),
                                 preferred_element_type=F32).astype(o_ref.dtype)

        return _pcall(
            kern_tn, name=name, grid=(m // tm, n // tn),
            in_specs=[pl.BlockSpec((k, tm), lambda i, j: (0, i)),
                      pl.BlockSpec((k, tn), lambda i, j: (0, j))],
            out_specs=pl.BlockSpec((tm, tn), lambda i, j: (i, j)),
            out_shape=out_shape, args=(a, b),
            scratch_shapes=[pltpu.VMEM((tm, k), BF16)],
            semantics=("parallel", "arbitrary"), carried=carried)

    tk = k if k <= 2048 else _div(k, 3072, LANE)
    nk = k // tk
    a_spec = pl.BlockSpec((tm, tk), lambda i, j, kk: (i, kk))
    if mode == "nn":
        b_spec = pl.BlockSpec((tk, tn), lambda i, j, kk: (kk, j))
        dot = _dot
    else:
        b_spec = pl.BlockSpec((tn, tk), lambda i, j, kk: (j, kk))
        dot = _dot_nt
    o_spec = pl.BlockSpec((tm, tn), lambda i, j, kk: (i, j))
    in_specs = [a_spec, b_spec]
    args = [a, b]
    if res is not None:
        in_specs.append(o_spec)
        args.append(res)
    has_res = res is not None

    def kern(*refs):
        a_ref, b_ref = refs[0], refs[1]
        r_ref = refs[2] if has_res else None
        o_ref = refs[3] if has_res else refs[2]
        part = dot(a_ref[...], b_ref[...])
        if nk == 1:
            if has_res:
                part = part + r_ref[...]
            o_ref[...] = part.astype(o_ref.dtype)
            return
        acc_ref = refs[-1]
        kk = pl.program_id(2)

        @pl.when(kk == 0)
        def _():
            acc_ref[...] = part

        @pl.when(kk > 0)
        def _():
            acc_ref[...] += part

        @pl.when(kk == nk - 1)
        def _():
            tot = acc_ref[...]
            if has_res:
                tot = tot + r_ref[...]
            o_ref[...] = tot.astype(o_ref.dtype)

    return _pcall(
        kern, name=name, grid=(m // tm, n // tn, nk),
        in_specs=in_specs, out_specs=o_spec, out_shape=out_shape, args=args,
        scratch_shapes=[pltpu.VMEM((tm, tn), F32)] if nk > 1 else [],
        semantics=("parallel", "parallel", "arbitrary"), carried=carried)


ROW_BLOCK = 256


def _rms_fwd(x, g, name):
    t, d = x.shape
    tr = min(ROW_BLOCK, t)

    def kern(x_ref, g_ref, o_ref):
        xf = x_ref[...]
        r = lax.rsqrt(jnp.mean(xf * xf, axis=-1, keepdims=True) + RMS_EPS)
        o_ref[...] = (xf * r * g_ref[...]).astype(o_ref.dtype)

    return pl.pallas_call(
        kern, name=name, grid=(t // tr,),
        in_specs=[pl.BlockSpec((tr, d), lambda i: (i, 0)), pl.BlockSpec((1, d), lambda i: (0, 0))],
        out_specs=pl.BlockSpec((tr, d), lambda i: (i, 0)),
        out_shape=jax.ShapeDtypeStruct((t, d), BF16),
        compiler_params=_cparams(("parallel",)),
    )(x, g)


def _rms_bwd(x, g, dh, res, name):
    t, d = x.shape
    tr = min(ROW_BLOCK, t)
    has_res = res is not None

    def kern(*refs):
        x_ref, g_ref, dh_ref = refs[:3]
        r_ref = refs[3] if has_res else None
        dx_ref, dg_ref = refs[-2], refs[-1]
        xf = x_ref[...]
        r = lax.rsqrt(jnp.mean(xf * xf, axis=-1, keepdims=True) + RMS_EPS)
        xn = xf * r
        dh_ = dh_ref[...]
        dhg = dh_ * g_ref[...]
        dx = r * (dhg - xn * jnp.mean(dhg * xn, axis=-1, keepdims=True))
        if has_res:
            dx = dx + r_ref[...]
        dx_ref[...] = dx
        part = jnp.sum(dh_ * xn, axis=0, keepdims=True)

        @pl.when(pl.program_id(0) == 0)
        def _():
            dg_ref[...] = part

        @pl.when(pl.program_id(0) > 0)
        def _():
            dg_ref[...] += part

    row = pl.BlockSpec((tr, d), lambda i: (i, 0))
    vec = pl.BlockSpec((1, d), lambda i: (0, 0))
    in_specs = [row, vec, row] + ([row] if has_res else [])
    args = [x, g, dh] + ([res] if has_res else [])
    return pl.pallas_call(
        kern, name=name, grid=(t // tr,), in_specs=in_specs, out_specs=[row, vec],
        out_shape=[jax.ShapeDtypeStruct((t, d), F32), jax.ShapeDtypeStruct((1, d), F32)],
        compiler_params=_cparams(("arbitrary",)),
    )(*args)


def _loss_head(x3, g, target, name):
    t, d = x3.shape
    tr = ROW_BLOCK

    def kern(x_ref, g_ref, t_ref, dx_ref, dg_ref, loss_ref):
        xf = x_ref[...]
        r = lax.rsqrt(jnp.mean(xf * xf, axis=-1, keepdims=True) + RMS_EPS)
        xn = xf * r
        gg = g_ref[...]
        err = xn * gg - t_ref[...]
        lpart = 0.5 * jnp.sum(jnp.mean(err * err, axis=-1, keepdims=True), axis=0, keepdims=True)
        dy = err * (1.0 / d)
        dyg = dy * gg
        dx_ref[...] = r * (dyg - xn * jnp.mean(dyg * xn, axis=-1, keepdims=True))
        gpart = jnp.sum(dy * xn, axis=0, keepdims=True)
        lrow = jnp.broadcast_to(lpart, (1, LANE))

        @pl.when(pl.program_id(0) == 0)
        def _():
            dg_ref[...] = gpart
            loss_ref[...] = lrow

        @pl.when(pl.program_id(0) > 0)
        def _():
            dg_ref[...] += gpart
            loss_ref[...] += lrow

    row = pl.BlockSpec((tr, d), lambda i: (i, 0))
    vec = pl.BlockSpec((1, d), lambda i: (0, 0))
    return pl.pallas_call(
        kern, name=name, grid=(t // tr,), in_specs=[row, vec, row],
        out_specs=[row, vec, pl.BlockSpec((1, LANE), lambda i: (0, 0))],
        out_shape=[jax.ShapeDtypeStruct((t, d), F32), jax.ShapeDtypeStruct((1, d), F32),
                   jax.ShapeDtypeStruct((1, LANE), F32)],
        compiler_params=_cparams(("arbitrary",)),
    )(x3, g, target)


ATT_SCALE = HEAD_DIM ** -0.5
Q_BLOCK0, K_BLOCK0, V_BLOCK0 = 0, ATT_WIDTH // HEAD_DIM, 2 * ATT_WIDTH // HEAD_DIM


def _residue_rows(dil, r, n):
    if dil == 1:
        return pl.ds(n * ATT_BLOCK, ATT_BLOCK)
    return pl.ds(n * ATT_BLOCK * dil + r, ATT_BLOCK, stride=dil)


def _band_mask(with_prev):
    width = 2 * ATT_BLOCK if with_prev else ATT_BLOCK
    iq = lax.broadcasted_iota(jnp.int32, (ATT_BLOCK, width), 0)
    ik = lax.broadcasted_iota(jnp.int32, (ATT_BLOCK, width), 1)
    if not with_prev:
        return ik <= iq
    return ((ik < ATT_BLOCK) & (iq <= ik)) | ((ik >= ATT_BLOCK) & ((ik - ATT_BLOCK) <= iq))


def _band_keys(ref, dil, r, n):
    own = ref[_residue_rows(dil, r, n), :]
    if n == 0:
        return own
    return jnp.concatenate([ref[_residue_rows(dil, r, n - 1), :], own], axis=0)


def _attn_col_spec(base, grp):
    return pl.BlockSpec((SEQ, HEAD_DIM), lambda h: (0, base + grp * ATT_HEADS + h))


def _attn_fwd(proj, grp, name, carried=()):
    _, dil = ATT_GROUPS[grp]
    nb = SEQ // dil // ATT_BLOCK

    def kern(q_ref, k_ref, v_ref, o_ref, lse_ref):
        for r in range(dil):
            for n in range(nb):
                rows = _residue_rows(dil, r, n)
                s = _dot_nt(q_ref[rows, :], _band_keys(k_ref, dil, r, n)) * ATT_SCALE
                s = jnp.where(_band_mask(n > 0), s, -jnp.inf)
                m = jnp.max(s, axis=-1, keepdims=True)
                p = jnp.exp(s - m)
                l = jnp.sum(p, axis=-1, keepdims=True)
                o_ref[rows, :] = _dot(p / l, _band_keys(v_ref, dil, r, n))
                lse_ref[rows, :] = jnp.broadcast_to(m + jnp.log(l), (ATT_BLOCK, HEAD_DIM))

    out_spec = pl.BlockSpec((SEQ, HEAD_DIM), lambda h: (0, h))
    return _pcall(
        kern, name=name, grid=(ATT_HEADS,),
        in_specs=[_attn_col_spec(Q_BLOCK0, grp), _attn_col_spec(K_BLOCK0, grp), _attn_col_spec(V_BLOCK0, grp)],
        out_specs=[out_spec, out_spec],
        out_shape=[jax.ShapeDtypeStruct((SEQ, ATT_OUT), F32)] * 2, args=(proj, proj, proj),
        semantics=("parallel",), carried=carried)


def _attn_weights(l0, l1, l2):
    mx = jnp.maximum(jnp.maximum(l0, l1), l2)
    e0, e1, e2 = jnp.exp(l0 - mx), jnp.exp(l1 - mx), jnp.exp(l2 - mx)
    den = e0 + e1 + e2
    return e0 / den, e1 / den, e2 / den


def _attn_merge_fwd(outs, lses, name):
    tr = ROW_BLOCK

    def kern(o0, o1, o2, l0, l1, l2, out_ref):
        a0, a1, a2 = _attn_weights(l0[...], l1[...], l2[...])
        out_ref[...] = (a0 * o0[...] + a1 * o1[...] + a2 * o2[...]).astype(out_ref.dtype)

    spec = pl.BlockSpec((tr, ATT_OUT), lambda i: (i, 0))
    return pl.pallas_call(
        kern, name=name, grid=(SEQ // tr,), in_specs=[spec] * 6, out_specs=spec,
        out_shape=jax.ShapeDtypeStruct((SEQ, ATT_OUT), BF16),
        compiler_params=_cparams(("parallel",)),
    )(*outs, *lses)


def _attn_merge_bwd(outs, lses, do_att, name, carried=()):
    tr = ROW_BLOCK

    def kern(o0, o1, o2, l0, l1, l2, do_ref, d0, d1, d2, t0, t1, t2):
        alphas = _attn_weights(l0[...], l1[...], l2[...])
        do = do_ref[...]
        o_att = alphas[0] * o0[...] + alphas[1] * o1[...] + alphas[2] * o2[...]
        prod = do * o_att
        parts = []
        for h in range(ATT_HEADS):
            sl = slice(h * HEAD_DIM, (h + 1) * HEAD_DIM)
            tot = jnp.sum(prod[:, sl], axis=-1, keepdims=True)
            parts.append(jnp.broadcast_to(tot, (tr, HEAD_DIM)))
        dd = jnp.concatenate(parts, axis=1)
        for a, d_ref, t_ref in zip(alphas, (d0, d1, d2), (t0, t1, t2)):
            d_ref[...] = a * do
            t_ref[...] = -a * dd

    spec = pl.BlockSpec((tr, ATT_OUT), lambda i: (i, 0))
    res, cres = _pcall(
        kern, name=name, grid=(SEQ // tr,), in_specs=[spec] * 7, out_specs=[spec] * 6,
        out_shape=[jax.ShapeDtypeStruct((SEQ, ATT_OUT), F32)] * 6, args=(*outs, *lses, do_att),
        semantics=("parallel",), carried=carried)
    return (res[:3], res[3:]), cres


def _attn_bwd(proj, grp, lse, do_g, dl_g, name, carried=()):
    _, dil = ATT_GROUPS[grp]
    nb = SEQ // dil // ATT_BLOCK

    def kern(q_ref, k_ref, v_ref, do_ref, lse_ref, dl_ref, dq_ref, dk_ref, dv_ref, dq_acc, dk_acc, dv_acc):
        dk_acc[...] = jnp.zeros_like(dk_acc)
        dv_acc[...] = jnp.zeros_like(dv_acc)
        for r in range(dil):
            for n in range(nb):
                rows = _residue_rows(dil, r, n)
                q, do = q_ref[rows, :], do_ref[rows, :]
                kk, vv = _band_keys(k_ref, dil, r, n), _band_keys(v_ref, dil, r, n)
                s = _dot_nt(q, kk) * ATT_SCALE
                p = jnp.where(_band_mask(n > 0), jnp.exp(s - lse_ref[rows, :][:, :1]), 0.0)
                ds = p * (_dot_nt(do, vv) + dl_ref[rows, :][:, :1])
                dq_acc[rows, :] = _dot(ds, kk) * ATT_SCALE
                dk = _dot_tn(ds, q) * ATT_SCALE
                dv = _dot_tn(p, do)
                if n > 0:
                    prev = _residue_rows(dil, r, n - 1)
                    dk_acc[prev, :] += dk[:ATT_BLOCK]
                    dv_acc[prev, :] += dv[:ATT_BLOCK]
                    dk, dv = dk[ATT_BLOCK:], dv[ATT_BLOCK:]
                dk_acc[rows, :] += dk
                dv_acc[rows, :] += dv
        dq_ref[...] = dq_acc[...].astype(dq_ref.dtype)
        dk_ref[...] = dk_acc[...].astype(dk_ref.dtype)
        dv_ref[...] = dv_acc[...].astype(dv_ref.dtype)

    spec = pl.BlockSpec((SEQ, HEAD_DIM), lambda h: (0, h))
    return _pcall(
        kern, name=name, grid=(ATT_HEADS,),
        in_specs=[_attn_col_spec(Q_BLOCK0, grp), _attn_col_spec(K_BLOCK0, grp), _attn_col_spec(V_BLOCK0, grp),
                  spec, spec, spec],
        out_specs=[spec] * 3,
        out_shape=[jax.ShapeDtypeStruct((SEQ, ATT_OUT), BF16)] * 3, args=(proj, proj, proj, do_g, lse, dl_g),
        scratch_shapes=[pltpu.VMEM((SEQ, HEAD_DIM), F32)] * 3,
        semantics=("parallel",), carried=carried)


HG_HEADS_PER_STEP = 4
HG_BLOCK_W = HG_HEADS_PER_STEP * HEAD_DIM
HG_Q_BLK = (3 * ATT_WIDTH) // HG_BLOCK_W
HG_N_CHUNKS = SEQ // HG_CHUNK
HG_MID = HG_CHUNK // 2


def _lower_bound(lb_ref, sl):
    l0, l1 = lb_ref[0:1, sl], lb_ref[1:2, sl]
    mx = jnp.maximum(l0, l1)
    e0, e1 = jnp.exp(l0 - mx), jnp.exp(l1 - mx)
    return e0 / (e0 + e1)


def _tri(lower):
    i = lax.broadcasted_iota(jnp.int32, (HG_CHUNK, HG_CHUNK), 0)
    j = lax.broadcasted_iota(jnp.int32, (HG_CHUNK, HG_CHUNK), 1)
    return (i >= j) if lower else (i <= j)


def _hg_chunk_terms(qh, fh, lb):
    sig = _sigmoid(fh)
    f = lb + (1.0 - lb) * sig
    k = 1.0 - f
    b = _dot_exact(_tri(True).astype(F32), jnp.log(f))
    bl = b[HG_CHUNK - 1:HG_CHUNK, :]
    br = b[HG_MID:HG_MID + 1, :]
    sq = _sigmoid(qh)
    q = qh * sq
    return dict(sig=sig, f=f, k=k, b=b, bl=bl, br=br, sq=sq, q=q,
                e1=jnp.exp(bl - b), e2=jnp.exp(b), e3=jnp.exp(b - br), e4=jnp.exp(br - b))


def _hg_fwd(proj, lbw, normw, name, carried=()):
    def in_blk(off):
        return pl.BlockSpec((HG_CHUNK, HG_BLOCK_W), lambda hp, n: (n, HG_Q_BLK + off + hp))

    def kern(q_ref, f_ref, i_ref, g_ref, lb_ref, nw_ref, oraw_ref, ohg_ref, st_ref, state):
        @pl.when(pl.program_id(1) == 0)
        def _():
            state[...] = jnp.zeros_like(state)

        causal = _tri(True)
        for hd in range(HG_HEADS_PER_STEP):
            sl = slice(hd * HEAD_DIM, (hd + 1) * HEAD_DIM)
            t = _hg_chunk_terms(q_ref[:, sl], f_ref[:, sl], _lower_bound(lb_ref, sl))
            v = i_ref[:, sl]
            st = state[hd]
            st_ref[0, hd] = st
            kd = t["k"] * t["e1"]
            inter = _dot_nt(t["q"] * t["e2"], st)
            a = jnp.where(causal, _dot_nt(t["q"] * t["e3"], t["k"] * t["e4"]), 0.0)
            o = inter + _dot(a, v)
            state[hd] = st * jnp.exp(t["bl"]) + _dot_tn(v, kd)
            oraw_ref[:, sl] = o
            r = lax.rsqrt(jnp.mean(o * o, axis=-1, keepdims=True) + RMS_EPS)
            gh = g_ref[:, sl]
            ohg_ref[:, sl] = (o * r * nw_ref[...] * (gh * _sigmoid(gh))).astype(ohg_ref.dtype)

    out_blk = pl.BlockSpec((HG_CHUNK, HG_BLOCK_W), lambda hp, n: (n, hp))
    return _pcall(
        kern, name=name, grid=(HG_HEADS // HG_HEADS_PER_STEP, HG_N_CHUNKS),
        in_specs=[in_blk(0), in_blk(2), in_blk(4), in_blk(6),
                  pl.BlockSpec((2, HG_BLOCK_W), lambda hp, n: (0, hp)),
                  pl.BlockSpec((1, HEAD_DIM), lambda hp, n: (0, 0))],
        out_specs=[out_blk, out_blk,
                   pl.BlockSpec((1, HG_HEADS_PER_STEP, HEAD_DIM, HEAD_DIM), lambda hp, n: (n, hp, 0, 0))],
        out_shape=[jax.ShapeDtypeStruct((SEQ, HG_WIDTH), F32), jax.ShapeDtypeStruct((SEQ, HG_WIDTH), BF16),
                   jax.ShapeDtypeStruct((HG_N_CHUNKS, HG_HEADS, HEAD_DIM, HEAD_DIM), F32)],
        args=(proj, proj, proj, proj, lbw, normw),
        scratch_shapes=[pltpu.VMEM((HG_HEADS_PER_STEP, HEAD_DIM, HEAD_DIM), F32)],
        semantics=("parallel", "arbitrary"), carried=carried)


def _hg_bwd(proj, lbw, normw, oraw, states, do_hg, name, carried=()):
    last = HG_N_CHUNKS - 1

    def in_blk(off):
        return pl.BlockSpec((HG_CHUNK, HG_BLOCK_W), lambda hp, n: (last - n, HG_Q_BLK + off + hp))

    blk = pl.BlockSpec((HG_CHUNK, HG_BLOCK_W), lambda hp, n: (last - n, hp))

    def kern(q_ref, f_ref, i_ref, g_ref, lb_ref, nw_ref, oraw_ref, st_ref, do_ref,
             dq_ref, df_ref, di_ref, dg_ref, dlb_ref, dnw_ref, dstate):
        first = pl.program_id(1) == 0

        @pl.when(first)
        def _():
            dstate[...] = jnp.zeros_like(dstate)

        causal = _tri(True)
        rows = lax.broadcasted_iota(jnp.int32, (HG_CHUNK, HEAD_DIM), 0)
        nw = nw_ref[...]
        dnw_tot = jnp.zeros((1, HEAD_DIM), F32)
        dlb_parts = []
        for hd in range(HG_HEADS_PER_STEP):
            sl = slice(hd * HEAD_DIM, (hd + 1) * HEAD_DIM)
            qh, fh, v, gh = q_ref[:, sl], f_ref[:, sl], i_ref[:, sl], g_ref[:, sl]
            o, dout = oraw_ref[:, sl], do_ref[:, sl]
            sgg = _sigmoid(gh)
            r = lax.rsqrt(jnp.mean(o * o, axis=-1, keepdims=True) + RMS_EPS)
            xn = o * r
            dg_ref[:, sl] = (dout * xn * nw * (sgg * (1.0 + gh * (1.0 - sgg)))).astype(dg_ref.dtype)
            don = dout * (gh * sgg)
            dnw_tot = dnw_tot + jnp.sum(don * xn, axis=0, keepdims=True)
            tt = don * nw
            do = r * (tt - xn * jnp.mean(tt * xn, axis=-1, keepdims=True))
            lb = _lower_bound(lb_ref, sl)
            t = _hg_chunk_terms(qh, fh, lb)
            k, q = t["k"], t["q"]
            kd, qb, qr, kr = k * t["e1"], q * t["e2"], q * t["e3"], k * t["e4"]
            a = jnp.where(causal, _dot_nt(qr, kr), 0.0)
            st = st_ref[0, hd]
            dstn = dstate[hd]
            dqb = _dot(do, st)
            da = jnp.where(causal, _dot_nt(do, v), 0.0)
            dv = _dot_tn(a, do) + _dot_nt(kd, dstn)
            dqr = _dot(da, kr)
            dkr = _dot_tn(da, qr)
            dkd = _dot(v, dstn)
            decay = jnp.exp(t["bl"])
            ddecay = jnp.sum(dstn * st, axis=0, keepdims=True)
            dstate[hd] = dstn * decay + _dot_tn(do, qb)
            dq = dqb * t["e2"] + dqr * t["e3"]
            dk = dkd * t["e1"] + dkr * t["e4"]
            db = dqb * qb + dqr * qr - dkr * kr - dkd * kd
            dbl = jnp.sum(dkd * kd, axis=0, keepdims=True) + ddecay * decay
            dbr = jnp.sum(dkr * kr - dqr * qr, axis=0, keepdims=True)
            dlf = _dot_exact(_tri(False).astype(F32), db) + dbl + jnp.where(rows <= HG_MID, dbr, 0.0)
            df = dlf / t["f"] - dk
            sig, sq = t["sig"], t["sq"]
            df_ref[:, sl] = (df * (1.0 - lb) * sig * (1.0 - sig)).astype(df_ref.dtype)
            dlb_parts.append(jnp.sum(df * (1.0 - sig), axis=0, keepdims=True))
            dq_ref[:, sl] = (dq * (sq * (1.0 + qh * (1.0 - sq)))).astype(dq_ref.dtype)
            di_ref[:, sl] = dv.astype(di_ref.dtype)
        dlb_row = jnp.concatenate(dlb_parts, axis=1)
        dnw_blk = jnp.broadcast_to(dnw_tot, (8, HEAD_DIM))

        @pl.when(first)
        def _():
            dlb_ref[...] = dlb_row
            dnw_ref[...] = dnw_blk

        @pl.when(jnp.logical_not(first))
        def _():
            dlb_ref[...] += dlb_row
            dnw_ref[...] += dnw_blk

    n_hp = HG_HEADS // HG_HEADS_PER_STEP
    outs, cres = _pcall(
        kern, name=name, grid=(n_hp, HG_N_CHUNKS),
        in_specs=[in_blk(0), in_blk(2), in_blk(4), in_blk(6),
                  pl.BlockSpec((2, HG_BLOCK_W), lambda hp, n: (0, hp)),
                  pl.BlockSpec((1, HEAD_DIM), lambda hp, n: (0, 0)),
                  blk,
                  pl.BlockSpec((1, HG_HEADS_PER_STEP, HEAD_DIM, HEAD_DIM), lambda hp, n: (last - n, hp, 0, 0)),
                  blk],
        out_specs=[blk, blk, blk, blk,
                   pl.BlockSpec((1, HG_BLOCK_W), lambda hp, n: (0, hp)),
                   pl.BlockSpec((8, HEAD_DIM), lambda hp, n: (hp, 0))],
        out_shape=[jax.ShapeDtypeStruct((SEQ, HG_WIDTH), BF16)] * 4
        + [jax.ShapeDtypeStruct((1, HG_WIDTH), F32), jax.ShapeDtypeStruct((8 * n_hp, HEAD_DIM), F32)],
        args=(proj, proj, proj, proj, lbw, normw, oraw, states, do_hg),
        scratch_shapes=[pltpu.VMEM((HG_HEADS_PER_STEP, HEAD_DIM, HEAD_DIM), F32)],
        semantics=("parallel", "arbitrary"), carried=carried)
    dqh, dfh, dih, dgh, dlb, dnw = outs
    return (dqh, dfh, dih, dgh, dlb, (dnw[0:1], dnw[8:9])), cres


GATE_BLOCK_W = 512
GATE_A_BLK = (3 * ATT_WIDTH + 4 * HG_WIDTH) // GATE_BLOCK_W
GATE_B_BLK = GATE_A_BLK + D_MODEL // GATE_BLOCK_W


def _gate_specs():
    tr = ROW_BLOCK
    blk = pl.BlockSpec((tr, GATE_BLOCK_W), lambda i, j: (i, j))
    ga = pl.BlockSpec((tr, GATE_BLOCK_W), lambda i, j: (i, GATE_A_BLK + j))
    gb = pl.BlockSpec((tr, GATE_BLOCK_W), lambda i, j: (i, GATE_B_BLK + j))
    return (SEQ // tr, D_MODEL // GATE_BLOCK_W), blk, ga, gb


def _gate_fwd(proj, ya, yb, name, carried=()):
    grid, blk, ga, gb = _gate_specs()

    def kern(ga_ref, gb_ref, ya_ref, yb_ref, o_ref):
        o_ref[...] = (_sigmoid(ga_ref[...]) * ya_ref[...] + _sigmoid(gb_ref[...]) * yb_ref[...]).astype(o_ref.dtype)

    return _pcall(
        kern, name=name, grid=grid, in_specs=[ga, gb, blk, blk], out_specs=blk,
        out_shape=jax.ShapeDtypeStruct((SEQ, D_MODEL), BF16), args=(proj, proj, ya, yb),
        semantics=("parallel", "parallel"), carried=carried)


def _gate_bwd(proj, ya, yb, dmerged, name, carried=()):
    grid, blk, ga, gb = _gate_specs()

    def kern(ga_ref, gb_ref, ya_ref, yb_ref, dm_ref, dya_ref, dyb_ref, dga_ref, dgb_ref):
        dm = dm_ref[...]
        sa, sb = _sigmoid(ga_ref[...]), _sigmoid(gb_ref[...])
        dya_ref[...] = (dm * sa).astype(dya_ref.dtype)
        dyb_ref[...] = (dm * sb).astype(dyb_ref.dtype)
        dga_ref[...] = (dm * ya_ref[...] * sa * (1.0 - sa)).astype(dga_ref.dtype)
        dgb_ref[...] = (dm * yb_ref[...] * sb * (1.0 - sb)).astype(dgb_ref.dtype)

    return _pcall(
        kern, name=name, grid=grid, in_specs=[ga, gb, blk, blk, blk], out_specs=[blk] * 4,
        out_shape=[jax.ShapeDtypeStruct((SEQ, D_MODEL), BF16)] * 4, args=(proj, proj, ya, yb, dmerged),
        semantics=("parallel", "parallel"), carried=carried)


FF_SHARD = D_FF // N_CHIPS


def _swiglu_fwd(ab, name):
    tr = ROW_BLOCK

    def kern(ab_ref, u_ref):
        a, b = ab_ref[:, :FF_SHARD], ab_ref[:, FF_SHARD:]
        u_ref[...] = (a * _sigmoid(a) * b).astype(u_ref.dtype)

    return pl.pallas_call(
        kern, name=name, grid=(SEQ // tr, N_CHIPS),
        in_specs=[pl.BlockSpec((tr, 2 * FF_SHARD), lambda i, j: (i, j))],
        out_specs=pl.BlockSpec((tr, FF_SHARD), lambda i, j: (i, j)),
        out_shape=jax.ShapeDtypeStruct((SEQ, D_FF), BF16),
        compiler_params=_cparams(("parallel", "parallel")),
    )(ab)


def _swiglu_bwd(ab, du, name, carried=()):
    tr = ROW_BLOCK

    def kern(ab_ref, du_ref, dab_ref):
        a, b = ab_ref[:, :FF_SHARD], ab_ref[:, FF_SHARD:]
        du_ = du_ref[...]
        sg = _sigmoid(a)
        dab_ref[:, :FF_SHARD] = (du_ * b * (sg * (1.0 + a * (1.0 - sg)))).astype(dab_ref.dtype)
        dab_ref[:, FF_SHARD:] = (du_ * (a * sg)).astype(dab_ref.dtype)

    wide = pl.BlockSpec((tr, 2 * FF_SHARD), lambda i, j: (i, j))
    return _pcall(
        kern, name=name, grid=(SEQ // tr, N_CHIPS),
        in_specs=[wide, pl.BlockSpec((tr, FF_SHARD), lambda i, j: (i, j))],
        out_specs=wide, out_shape=jax.ShapeDtypeStruct((SEQ, 2 * D_FF), BF16), args=(ab, du),
        semantics=("parallel", "parallel"), carried=carried)


CROSS_ROWS = 512


def _cross_fwd(qc, kvc, name):
    def kern(q_ref, k_ref, v_ref, o_ref):
        s = _dot_nt(q_ref[...], k_ref[...]) * ATT_SCALE
        m = jnp.max(s, axis=-1, keepdims=True)
        e = jnp.exp(s - m)
        p = e / jnp.sum(e, axis=-1, keepdims=True)
        o_ref[...] = _dot(p, v_ref[...]).astype(o_ref.dtype)

    qblk = pl.BlockSpec((CROSS_ROWS, HEAD_DIM), lambda h, i: (i, h))
    return pl.pallas_call(
        kern, name=name, grid=(CROSS_HEADS, SEQ // CROSS_ROWS),
        in_specs=[qblk, pl.BlockSpec((MEM_LEN, HEAD_DIM), lambda h, i: (0, h)),
                  pl.BlockSpec((MEM_LEN, HEAD_DIM), lambda h, i: (0, CROSS_HEADS + h))],
        out_specs=qblk, out_shape=jax.ShapeDtypeStruct((SEQ, CROSS_WIDTH), BF16),
        compiler_params=_cparams(("parallel", "parallel")),
    )(qc, kvc, kvc)


def _cross_bwd(qc, kvc, doc, name):
    def kern(q_ref, k_ref, v_ref, do_ref, dq_ref, dk_ref, dv_ref):
        q, k, v, do = q_ref[...], k_ref[...], v_ref[...], do_ref[...]
        s = _dot_nt(q, k) * ATT_SCALE
        m = jnp.max(s, axis=-1, keepdims=True)
        e = jnp.exp(s - m)
        p = e / jnp.sum(e, axis=-1, keepdims=True)
        dp = _dot_nt(do, v)
        ds = p * (dp - jnp.sum(dp * p, axis=-1, keepdims=True))
        dq_ref[...] = (_dot(ds, k) * ATT_SCALE).astype(dq_ref.dtype)
        dk = _dot_tn(ds, q) * ATT_SCALE
        dv = _dot_tn(p, do)

        @pl.when(pl.program_id(1) == 0)
        def _():
            dk_ref[...] = dk
            dv_ref[...] = dv

        @pl.when(pl.program_id(1) > 0)
        def _():
            dk_ref[...] += dk
            dv_ref[...] += dv

    qblk = pl.BlockSpec((CROSS_ROWS, HEAD_DIM), lambda h, i: (i, h))
    kblk = pl.BlockSpec((MEM_LEN, HEAD_DIM), lambda h, i: (0, h))
    dq, dk, dv = pl.pallas_call(
        kern, name=name, grid=(CROSS_HEADS, SEQ // CROSS_ROWS),
        in_specs=[qblk, kblk, pl.BlockSpec((MEM_LEN, HEAD_DIM), lambda h, i: (0, CROSS_HEADS + h)), qblk],
        out_specs=[qblk, kblk, kblk],
        out_shape=[jax.ShapeDtypeStruct((SEQ, CROSS_WIDTH), BF16),
                   jax.ShapeDtypeStruct((MEM_LEN, CROSS_WIDTH), F32),
                   jax.ShapeDtypeStruct((MEM_LEN, CROSS_WIDTH), F32)],
        compiler_params=_cparams(("parallel", "arbitrary")),
    )(qc, kvc, kvc, doc)
    return dq, jnp.concatenate([dk, dv], axis=1)


FULL_SPECS = {
    "w_in": ("col", D_MODEL, IN_WIDTH),
    "w_branch_a": ("col", ATT_OUT, D_MODEL),
    "w_branch_b": ("col", HG_WIDTH, D_MODEL),
    "w_out": ("row", D_MODEL, D_MODEL),
    "wq_cross": ("row", D_MODEL, CROSS_WIDTH),
    "wkv_cross": ("row", D_MODEL, 2 * CROSS_WIDTH),
    "wo_cross": ("col", CROSS_WIDTH, D_MODEL),
    "w13": ("col", D_MODEL, 2 * D_FF),
    "w2": ("row", D_FF, D_MODEL),
}
WEIGHT_PLACE = {
    "w_in": ("w_in", 0), "w_branch_a": ("w_branch_a", 0), "w_branch_b": ("w_branch_b", 0),
    "w_out": ("w_out", 0), "wq_cross": ("wq_cross", 0), "wkv_cross": ("wkv_cross", 0),
    "wo_cross": ("wo_cross", 0), "w1": ("w13", 0), "w3": ("w13", FF_SHARD), "w2": ("w2", 0),
}
BIG_WEIGHTS = tuple(WEIGHT_PLACE)
EW_BLOCK_ELEMS = 512 * 1024


def _position():
    return lax.axis_index("x"), lax.axis_index("y"), lax.axis_index("c")


def _other_chips(x, y):
    return [(1 - x, y), (x, 1 - y), (1 - x, 1 - y)]


def _half(ref, kind, h):
    r, c = ref.shape
    if kind == "col":
        return ref.at[pl.ds(h * (r // 2), r // 2), :]
    return ref.at[:, pl.ds(h * (c // 2), c // 2)]


def _shard_of(ref, kind, start, size):
    return ref.at[:, pl.ds(start, size)] if kind == "col" else ref.at[pl.ds(start, size), :]


def _rows_of(ref, r0, nrows):
    return ref if nrows is None else ref.at[pl.ds(r0, nrows), :]


def _half_shape(kind, rows, cols):
    return (rows // 2, cols) if kind == "col" else (rows, cols // 2)


def _slot_shape(spec):
    kind, rows, cols = spec
    hr, hc = _half_shape(kind, rows, cols)
    return (hr, hc // N_CHIPS) if kind == "col" else (hr // N_CHIPS, hc)


def _remote(src, dst, send_sem, recv_sem, device):
    return pltpu.make_async_remote_copy(src_ref=src, dst_ref=dst, send_sem=send_sem, recv_sem=recv_sem,
                                        device_id=device, device_id_type=MESH)


def _gather_ici_comm(fulls, jobs, specs):
    def piece(refs, job, chip, c):
        f, r0, nr = job
        kind, rows, cols = specs[f]
        stride = (cols if kind == "col" else rows) // N_CHIPS
        return _rows_of(_half(_shard_of(refs[f], kind, chip * stride, stride), kind, c), r0, nr)

    def start(refs, ss, rs):
        x, y, c = _position()
        j = 2 * x + y
        for q, job in enumerate(jobs):
            for p, (px, py) in enumerate(_other_chips(x, y)):
                _remote(piece(refs, job, j, c), piece(refs, job, j, c), ss.at[3 * q + p], rs.at[3 * q + p],
                        (px, py, c)).start()

    def finish(refs, ss, rs):
        x, y, c = _position()
        j = 2 * x + y
        for q, job in enumerate(jobs):
            for p, (px, py) in enumerate(_other_chips(x, y)):
                _remote(piece(refs, job, j, c), piece(refs, job, 2 * px + py, c), ss.at[3 * q + p],
                        rs.at[3 * q + p], (px, py, c)).wait_recv()
        for q, job in enumerate(jobs):
            for p, (px, py) in enumerate(_other_chips(x, y)):
                _remote(piece(refs, job, j, c), piece(refs, job, j, c), ss.at[3 * q + p], rs.at[3 * q + p],
                        (px, py, c)).wait_send()

    names = list(dict.fromkeys(job[0] for job in jobs))
    return _Carried({f: fulls[f] for f in names}, {}, 3 * len(jobs), start, finish)


def _gather_d2d_comm(fulls, jobs, specs):
    def rect(refs, job, h):
        f, r0, nr = job
        assert nr is None or specs[f][0] == "col"
        return _rows_of(_half(refs[f], specs[f][0], h), r0, nr)

    def start(refs, ss, rs):
        x, y, c = _position()
        for q, job in enumerate(jobs):
            _remote(rect(refs, job, c), rect(refs, job, c), ss.at[q], rs.at[q], (x, y, 1 - c)).start()

    def finish(refs, ss, rs):
        x, y, c = _position()
        for q, job in enumerate(jobs):
            _remote(rect(refs, job, 1 - c), rect(refs, job, 1 - c), ss.at[q], rs.at[q], (x, y, 1 - c)).wait_recv()
        for q, job in enumerate(jobs):
            _remote(rect(refs, job, c), rect(refs, job, c), ss.at[q], rs.at[q], (x, y, 1 - c)).wait_send()

    names = list(dict.fromkeys(job[0] for job in jobs))
    return _Carried({f: fulls[f] for f in names}, {}, len(jobs), start, finish)


def _pairx_comm(grads, names, specs):
    def copies(refs, ss, rs):
        x, y, c = _position()
        return [_remote(_half(refs[("g", f)], specs[f][0], 1 - c), refs[("r", f)], ss.at[i], rs.at[i], (x, y, 1 - c))
                for i, f in enumerate(names)]

    def start(refs, ss, rs):
        for cp in copies(refs, ss, rs):
            cp.start()

    def finish(refs, ss, rs):
        for cp in copies(refs, ss, rs):
            cp.wait_recv()
        for cp in copies(refs, ss, rs):
            cp.wait_send()

    fresh = {("r", f): jax.ShapeDtypeStruct(_half_shape(*specs[f]), BF16) for f in names}
    return _Carried({("g", f): grads[f] for f in names}, fresh, len(names), start, finish)


def _chipx_comm(pair_sums, slots, jobs, specs):
    def copies(refs, ss, rs):
        x, y, c = _position()
        out = []
        for q, (f, r0, nr) in enumerate(jobs):
            kind = specs[f][0]
            width = _slot_shape(specs[f])[1 if kind == "col" else 0]
            for p, (px, py) in enumerate(_other_chips(x, y)):
                src = _rows_of(_shard_of(refs[("p", f)], kind, (2 * px + py) * width, width), r0, nr)
                dst = _rows_of(refs[("s", f)].at[p], r0, nr)
                out.append(_remote(src, dst, ss.at[3 * q + p], rs.at[3 * q + p], (px, py, c)))
        return out

    def start(refs, ss, rs):
        for cp in copies(refs, ss, rs):
            cp.start()

    def finish(refs, ss, rs):
        for cp in copies(refs, ss, rs):
            cp.wait_recv()
        for cp in copies(refs, ss, rs):
            cp.wait_send()

    names = list(dict.fromkeys(job[0] for job in jobs))
    arrays = {("p", f): pair_sums[f] for f in names}
    arrays.update({("s", f): slots[f] for f in names})
    return _Carried(arrays, {}, 3 * len(jobs), start, finish)


def _share_comm(grads, wnames, specs, place):
    def start(refs, ss, rs):
        x, y, c = _position()
        for i, w in enumerate(wnames):
            kind = specs[place[w][0]][0]
            _remote(_half(refs[w], kind, c), _half(refs[w], kind, c), ss.at[i], rs.at[i], (x, y, 1 - c)).start()

    def finish(refs, ss, rs):
        x, y, c = _position()
        for i, w in enumerate(wnames):
            kind = specs[place[w][0]][0]
            _remote(_half(refs[w], kind, 1 - c), _half(refs[w], kind, 1 - c), ss.at[i], rs.at[i],
                    (x, y, 1 - c)).wait_recv()
        for i, w in enumerate(wnames):
            kind = specs[place[w][0]][0]
            _remote(_half(refs[w], kind, c), _half(refs[w], kind, c), ss.at[i], rs.at[i], (x, y, 1 - c)).wait_send()

    return _Carried({w: grads[w] for w in wnames}, {}, len(wnames), start, finish)


def _gather_rows(v, name="gather_small"):
    shape = v.shape

    def body(v_ref, out_ref, send_sem, recv_sem, loc_sem):
        x, y, c = _position()
        me = 4 * x + 2 * y + c
        flips = [(fx, fy, fc) for fx in (0, 1) for fy in (0, 1) for fc in (0, 1)][1:]

        def peer(fl):
            return tuple(1 - a if f else a for a, f in zip((x, y, c), fl))

        loc = pltpu.make_async_copy(v_ref, out_ref.at[me], loc_sem)
        loc.start()
        sends = []
        for i, fl in enumerate(flips):
            cp = _remote(v_ref, out_ref.at[me], send_sem.at[i], recv_sem.at[i], peer(fl))
            cp.start()
            sends.append(cp)
        for i, fl in enumerate(flips):
            px, py, pc = peer(fl)
            _remote(v_ref, out_ref.at[4 * px + 2 * py + pc], send_sem.at[i], recv_sem.at[i], peer(fl)).wait_recv()
        for cp in sends:
            cp.wait_send()
        loc.wait()

    return pl.pallas_call(
        body, name=name, in_specs=[ANY], out_specs=ANY,
        out_shape=jax.ShapeDtypeStruct((N_DEV,) + shape, F32),
        scratch_shapes=[pltpu.SemaphoreType.DMA((N_DEV - 1,)), pltpu.SemaphoreType.DMA((N_DEV - 1,)),
                        pltpu.SemaphoreType.DMA],
    )(v)


def _ew_block(rows, cols, elems=EW_BLOCK_ELEMS):
    tc = cols if cols <= 4096 else _div(cols, 2048, LANE)
    tr = _div(rows, max(16, elems // tc), 16)
    return tr, tc


def _mesh_scalars():
    x, y, c = _position()
    return jnp.stack([c, 2 * x + y]).astype(jnp.int32)


def _grid_spec(grid, in_specs, out_specs):
    return pltpu.PrefetchScalarGridSpec(num_scalar_prefetch=1, grid=grid, in_specs=in_specs, out_specs=out_specs)


def _cast_into_full(parts, fname, pos, specs, place, name, token=None):
    kind, rows, cols = specs[fname]
    ws = [w for w in place if place[w][0] == fname]
    if kind == "col":
        stride = cols // N_CHIPS
        hr = rows // 2
        tr = _div(hr, max(16, EW_BLOCK_ELEMS // stride), 16)
        nrb = hr // tr
        in_specs = [pl.BlockSpec((tr, parts[w].shape[1]), lambda i, pos_ref: (i + pos_ref[0] * nrb, 0)) for w in ws]
        out_spec = pl.BlockSpec((tr, stride), lambda i, pos_ref: (i + pos_ref[0] * nrb, pos_ref[1]))
    else:
        stride = rows // N_CHIPS
        hc = cols // 2
        tr = _div(stride, max(16, EW_BLOCK_ELEMS // hc), 16)
        nrb = stride // tr
        in_specs = [pl.BlockSpec((tr, hc), lambda i, pos_ref: (i, pos_ref[0])) for w in ws]
        out_spec = pl.BlockSpec((tr, hc), lambda i, pos_ref: (i + pos_ref[1] * nrb, pos_ref[0]))

    def kern(pos_ref, *refs):
        o_ref = refs[-1]
        for w, r in zip(ws, refs[:len(ws)]):
            off = place[w][1] if kind == "col" else 0
            o_ref[:, off:off + r.shape[1]] = r[...].astype(o_ref.dtype)

    tokens = [] if token is None else [token]
    in_specs = in_specs + [pl.BlockSpec(TOKEN_SHAPE, lambda i, pos_ref: (0, 0))] * len(tokens)
    return pl.pallas_call(
        kern, name=name, grid_spec=_grid_spec((nrb,), in_specs, out_spec),
        out_shape=jax.ShapeDtypeStruct((rows, cols), BF16),
        compiler_params=_cparams(("parallel",)),
    )(pos, *[parts[w] for w in ws], *tokens)


def _pair_sum(grad, recv, pos, spec, name):
    kind, rows, cols = spec
    hr, hc = _half_shape(kind, rows, cols)
    tr, tc = _ew_block(hr, hc, 2 * EW_BLOCK_ELEMS)
    nrb, ncb = hr // tr, hc // tc
    blk = pl.BlockSpec((tr, tc), lambda i, jj, pos_ref: (i, jj))
    if kind == "col":
        mine = pl.BlockSpec((tr, tc), lambda i, jj, pos_ref: (i + pos_ref[0] * nrb, jj))
    else:
        mine = pl.BlockSpec((tr, tc), lambda i, jj, pos_ref: (i, jj + pos_ref[0] * ncb))

    def kern(pos_ref, g_ref, r_ref, o_ref, slots_ref):
        o_ref[...] = (g_ref[...].astype(F32) + r_ref[...].astype(F32)).astype(o_ref.dtype)

    return pl.pallas_call(
        kern, name=name, grid_spec=_grid_spec((nrb, ncb), [mine, blk], [blk, ANY]),
        out_shape=[jax.ShapeDtypeStruct((hr, hc), BF16),
                   jax.ShapeDtypeStruct((N_CHIPS - 1,) + _slot_shape(spec), BF16)],
        compiler_params=_cparams(("parallel", "parallel")),
    )(pos, grad, recv)


def _chip_sum(pair_sum, slots, pos, fname, shard_shapes, specs, place, name):
    kind, rows, cols = specs[fname]
    sr, sc = _slot_shape(specs[fname])
    ws = [w for w in place if place[w][0] == fname]
    n_slots = N_CHIPS - 1
    tr = _div(sr, max(16, EW_BLOCK_ELEMS // sc), 16)
    nrb = sr // tr
    slot = pl.BlockSpec((n_slots, tr, sc), lambda i, pos_ref: (0, i, 0))
    if kind == "col":
        own = pl.BlockSpec((tr, sc), lambda i, pos_ref: (i, pos_ref[1]))
        out_specs = [pl.BlockSpec((tr, shard_shapes[w][1]), lambda i, pos_ref: (i + pos_ref[0] * nrb, 0)) for w in ws]
    else:
        own = pl.BlockSpec((tr, sc), lambda i, pos_ref: (i + pos_ref[1] * nrb, 0))
        out_specs = [pl.BlockSpec((tr, sc), lambda i, pos_ref: (i, pos_ref[0])) for w in ws]

    def kern(pos_ref, own_ref, slot_ref, *out_refs):
        tot = own_ref[...].astype(F32)
        for s in range(n_slots):
            tot = tot + slot_ref[s].astype(F32)
        for w, o_ref in zip(ws, out_refs):
            off = place[w][1] if kind == "col" else 0
            o_ref[...] = tot[:, off:off + o_ref.shape[1]]

    outs = pl.pallas_call(
        kern, name=name, grid_spec=_grid_spec((nrb,), [own, slot], out_specs),
        out_shape=[jax.ShapeDtypeStruct(shard_shapes[w], F32) for w in ws],
        compiler_params=_cparams(("parallel",)),
    )(pos, pair_sum, slots)
    return dict(zip(ws, outs))


def _adam_math(w, g, m, v):
    m2 = ADAM_B1 * m + (1.0 - ADAM_B1) * g
    v2 = ADAM_B2 * v + (1.0 - ADAM_B2) * (g * g)
    m_hat = m2 / (1.0 - ADAM_B1 ** ADAM_STEP)
    v_hat = v2 / (1.0 - ADAM_B2 ** ADAM_STEP)
    delta = -ADAM_LR * (m_hat / (jnp.sqrt(v_hat) + ADAM_EPS) + ADAM_WD * w)
    return delta, m2, v2


def _adamw(w, g, m, v, name, carried=()):
    rows, cols = w.shape
    tr, tc = _ew_block(rows, cols)

    def kern(w_ref, g_ref, m_ref, v_ref, d_ref, m2_ref, v2_ref, g_out_ref):
        g_ = g_ref[...]
        d_ref[...], m2_ref[...], v2_ref[...] = _adam_math(w_ref[...], g_, m_ref[...], v_ref[...])
        g_out_ref[...] = g_

    blk = pl.BlockSpec((tr, tc), lambda i, j: (i, j))
    return _pcall(
        kern, name=name, grid=(rows // tr, cols // tc), in_specs=[blk] * 4, out_specs=[blk] * 4,
        out_shape=[jax.ShapeDtypeStruct((rows, cols), F32)] * 4, args=(w, g, m, v),
        semantics=("parallel", "parallel"), carried=carried)


SMALL_ROWS = ("ln_mix_w", "ln_cross_w", "ln_mem_w", "ln_ffn_w", "ln_final_w")
ROW_HG_NORM, ROW_LB0, ROW_LB1 = 5, 6, 7
LOSS_LANE0 = HEAD_DIM


def _pack_small(vals):
    rows = [vals[n].reshape(1, D_MODEL) for n in SMALL_ROWS]
    pad = lambda a: jnp.pad(a, ((0, 0), (0, D_MODEL - a.shape[1])))
    rows.append(pad(vals["hg_norm_w"].reshape(1, HEAD_DIM)))
    rows.append(pad(vals["hg_lower_bounds"].reshape(2, HG_WIDTH)))
    return jnp.concatenate(rows, axis=0)


def _small_update(gathered, w, m, v, name="small_update"):
    def kern(g_ref, w_ref, m_ref, v_ref, grad_ref, d_ref, m2_ref, v2_ref, loss_ref):
        tot = g_ref[0]
        for s in range(1, N_DEV):
            tot = tot + g_ref[s]
        wv = w_ref[...]
        row = lax.broadcasted_iota(jnp.int32, (8, D_MODEL), 0)
        lane = lax.broadcasted_iota(jnp.int32, (8, D_MODEL), 1)
        l0, l1 = wv[ROW_LB0:ROW_LB0 + 1], wv[ROW_LB1:ROW_LB1 + 1]
        mx = jnp.maximum(l0, l1)
        e0, e1 = jnp.exp(l0 - mx), jnp.exp(l1 - mx)
        p0 = e0 / (e0 + e1)
        dlog = tot[ROW_LB0:ROW_LB0 + 1] * p0 * (1.0 - p0)
        tot = jnp.where(row == ROW_HG_NORM, tot + tot[ROW_LB1:ROW_LB1 + 1], tot)
        grad = jnp.where(row == ROW_LB0, dlog, jnp.where(row == ROW_LB1, -dlog, tot))
        grad = jnp.where((row == ROW_HG_NORM) & (lane >= HEAD_DIM), 0.0, grad)
        grad = jnp.where((row >= ROW_LB0) & (lane >= HG_WIDTH), 0.0, grad)
        grad_ref[...] = grad
        d_ref[...], m2_ref[...], v2_ref[...] = _adam_math(wv, grad, m_ref[...], v_ref[...])
        loss_ref[...] = tot[ROW_HG_NORM:ROW_HG_NORM + 1, LOSS_LANE0:LOSS_LANE0 + LANE]

    full = pl.BlockSpec((8, D_MODEL), lambda: (0, 0))
    return pl.pallas_call(
        kern, name=name,
        in_specs=[pl.BlockSpec((N_DEV, 8, D_MODEL), lambda: (0, 0, 0)), full, full, full],
        out_specs=[full, full, full, full, pl.BlockSpec((1, LANE), lambda: (0, 0))],
        out_shape=[jax.ShapeDtypeStruct((8, D_MODEL), F32)] * 4 + [jax.ShapeDtypeStruct((1, LANE), F32)],
        compiler_params=_cparams(),
    )(gathered, w, m, v)


def _unpack_small(p, shapes):
    out = {n: p[i].reshape(shapes[n]) for i, n in enumerate(SMALL_ROWS)}
    out["hg_norm_w"] = p[ROW_HG_NORM, :HEAD_DIM].reshape(shapes["hg_norm_w"])
    out["hg_lower_bounds"] = p[ROW_LB0:ROW_LB1 + 1, :HG_WIDTH].reshape(shapes["hg_lower_bounds"])
    return out


WHOLE = lambda f: (f, 0, None)
MID_MATRICES = ("w_branch_a", "w_branch_b", "w_out", "wq_cross", "wkv_cross", "wo_cross")
MID_WEIGHTS = MID_MATRICES
W_IN_PIECES = [("w_in", r0, 256) for r0 in range(0, D_MODEL // 2, 256)]
GATHER_GROUPS = [(f"w_in{i}", [job]) for i, job in enumerate(W_IN_PIECES)] + [
    ("mid", [WHOLE(f) for f in MID_MATRICES]), ("w13", [WHOLE("w13")]), ("w2", [WHOLE("w2")])]
OTHER_WEIGHTS = ["w1", "w3", "w2"] + list(MID_WEIGHTS)
BEFORE = {
    "hgrn_fwd": [("wait", "mid")],
    "gate_fwd": [("wait", "w13")],
    "mm_w13": [("wait", "w2")],
    "mm_dh": [("wait", "rs_w2"), ("chip_sum", "w2"), ("wait", "rs_w13"), ("chip_sum", "w13"), ("wait", "rs_mid")]
    + [("chip_sum", f) for f in MID_MATRICES],
}
CARRY = {
    "hgrn_fwd": [("d2d", [WHOLE(f) for f in MID_MATRICES])],
    "gate_fwd": [("d2d", [WHOLE("w13")])],
    "mm_w13": [("d2d", [WHOLE("w2")])],
    "swiglu_bwd": [("pairx", ["w2"])],
    "mm_dhf": [("pairx", ["w13"])],
    "attn_merge_bwd": [("pairx", list(MID_MATRICES))],
    "mm_dh": [("share", OTHER_WEIGHTS)],
}
AFTER = {
    "swiglu_bwd": [("pair_sum", "w2"), ("start", "rs_w2", [WHOLE("w2")])],
    "mm_dhf": [("pair_sum", "w13"), ("start", "rs_w13", [WHOLE("w13")])],
    "attn_merge_bwd": [("pair_sum", f) for f in MID_MATRICES] + [("start", "rs_mid", [WHOLE(f) for f in MID_MATRICES])],
    "mm_dwin": [("run", ("pairx", ["w_in"]), "rs_pair_exchange_w_in"), ("pair_sum", "w_in"),
                ("start", "rs_w_in", [WHOLE("w_in")])],
}
FINISH = [
    ("adamw", OTHER_WEIGHTS), ("wait", "rs_w_in"), ("small",), ("chip_sum", "w_in"),
    ("run", ("share", ["w_in"]), "rs_sibling_share_w_in"), ("adamw", ["w_in"]),
]


class _Net:
    def __init__(self, full, pos=None, shard_shapes=None, comm=True, specs=FULL_SPECS, place=WEIGHT_PLACE):
        self.full, self.pos, self.shard_shapes, self.comm = dict(full), pos, shard_shapes, comm
        self.specs, self.place = specs, place
        self.gw, self.recv, self.psum, self.slots, self.grads = {}, {}, {}, {}, {}
        self.pending, self.token, self.last = {}, None, None

    def _make(self, kind, arg):
        if kind == "gather":
            return _gather_ici_comm(self.full, arg, self.specs)
        if kind == "d2d":
            return _gather_d2d_comm(self.full, arg, self.specs)
        if kind == "pairx":
            return _pairx_comm(self.gw, arg, self.specs)
        if kind == "chipx":
            return _chipx_comm(self.psum, self.slots, arg, self.specs)
        assert kind == "share"
        return _share_comm(self.grads, arg, self.specs, self.place)

    def _store(self, kind, res):
        if kind in ("gather", "d2d"):
            self.full.update(res)
        elif kind == "pairx":
            for (tag, f), a in res.items():
                (self.gw if tag == "g" else self.recv)[f] = a
        elif kind == "chipx":
            for (tag, f), a in res.items():
                (self.psum if tag == "p" else self.slots)[f] = a
        else:
            self.grads.update(res)

    def run_comm(self, item, name):
        kind, arg = item
        self._store(kind, _run_comm([self._make(kind, arg)], name)[0])

    def start(self, groups, kind, name):
        res, sems, token = _split_start([self._make(kind, jobs) for _, jobs in groups], name, after=self.last)
        for (group, jobs), r, s in zip(groups, res, sems):
            self._store(kind, r)
            self.pending[group] = (kind, jobs, s)
        self.token = self.last = token

    def wait(self, group, after=()):
        kind, jobs, sems = self.pending.pop(group)
        self._store(kind, _split_wait([self._make(kind, jobs)], [sems], [self.last, *after], f"wait_{group}")[0])

    def step(self, step):
        if step[0] == "wait":
            self.wait(step[1])
        elif step[0] == "start":
            self.start([(step[1], step[2])], "chipx", f"start_{step[1]}")
        elif step[0] == "pair_sum":
            f = step[1]
            self.psum[f], self.slots[f] = _pair_sum(self.gw[f], self.recv[f], self.pos, self.specs[f],
                                                    f"rs_pair_sum_{f}")
        elif step[0] == "chip_sum":
            f = step[1]
            self.grads.update(_chip_sum(self.psum[f], self.slots[f], self.pos, f, self.shard_shapes,
                                        self.specs, self.place, f"rs_chip_sum_{f}"))
        else:
            assert step[0] == "run"
            self.run_comm(step[1], step[2])

    def call(self, fn, name, *args, grad_of=None, **kw):
        for step in (BEFORE.get(name, []) if self.comm else []):
            self.step(step)
        items = CARRY.get(name, []) if self.comm else []
        carried = [self._make(k, a) for k, a in items]
        if self.token is not None:
            carried.append(_Token(self.token))
            self.token = None
        out, res = fn(*args, name=name, carried=carried, **kw)
        if grad_of is not None:
            self.gw[grad_of] = out
        self.last = jax.tree.leaves(out)[0]
        for (kind, _), r in zip(items, res):
            self._store(kind, r)
        for step in (AFTER.get(name, []) if self.comm else []):
            self.step(step)
        return out


def _local_step(net, x, h, mem, target, small):
    full, call = net.full, net.call
    proj = call(_mm, "mm_proj", h, full["w_in"], mode="nn", out_dtype=F32)
    att = [call(_attn_fwd, f"attn_fwd_g{g}", proj, g) for g in range(3)]
    outs, lses = [a[0] for a in att], [a[1] for a in att]
    o_att = _attn_merge_fwd(outs, lses, "attn_merge")
    oraw, o_hg, states = call(_hg_fwd, "hgrn_fwd", proj, small["hg_lower_bounds"], small["hg_norm_w"])
    ya = call(_mm, "mm_branch_a", o_att, full["w_branch_a"], mode="nn", out_dtype=F32)
    yb = call(_mm, "mm_branch_b", o_hg, full["w_branch_b"], mode="nn", out_dtype=F32)
    merged = call(_gate_fwd, "gate_fwd", proj, ya, yb)
    x1 = call(_mm, "mm_out", merged, full["w_out"], mode="nn", out_dtype=F32, res=x)

    hc = _rms_fwd(x1, small["ln_cross_w"], "rms_cross")
    mn = _rms_fwd(mem, small["ln_mem_w"], "rms_mem")
    qc = call(_mm, "mm_q", hc, full["wq_cross"], mode="nn", out_dtype=F32)
    kvc = call(_mm, "mm_kv", mn, full["wkv_cross"], mode="nn", out_dtype=F32)
    oc = _cross_fwd(qc, kvc, "cross_fwd")
    x2 = call(_mm, "mm_o", oc, full["wo_cross"], mode="nn", out_dtype=F32, res=x1)

    hf = _rms_fwd(x2, small["ln_ffn_w"], "rms_ffn")
    ab = call(_mm, "mm_w13", hf, full["w13"], mode="nn", out_dtype=F32)
    u = _swiglu_fwd(ab, "swiglu_fwd")
    x3 = call(_mm, "mm_w2", u, full["w2"], mode="nn", out_dtype=F32, res=x2)

    dx3, dg_final, loss = _loss_head(x3, small["ln_final_w"], target, "loss_head")

    gs = {"ln_final_w": dg_final}
    du = call(_mm, "mm_du", dx3, full["w2"], mode="nt", out_dtype=F32)
    call(_mm, "mm_dw2", u, dx3, mode="tn", out_dtype=BF16, grad_of="w2")
    dab = call(_swiglu_bwd, "swiglu_bwd", ab, du)
    call(_mm, "mm_dw13", hf, dab, mode="tn", out_dtype=BF16, grad_of="w13")
    dhf = call(_mm, "mm_dhf", dab, full["w13"], mode="nt", out_dtype=F32)
    dx2, gs["ln_ffn_w"] = _rms_bwd(x2, small["ln_ffn_w"], dhf, dx3, "rms_ffn_bwd")
    doc = call(_mm, "mm_doc", dx2, full["wo_cross"], mode="nt", out_dtype=BF16)
    call(_mm, "mm_dwo", oc, dx2, mode="tn", out_dtype=BF16, grad_of="wo_cross")
    dqc, dkvc = _cross_bwd(qc, kvc, doc, "cross_bwd")
    call(_mm, "mm_dwq", hc, dqc, mode="tn", out_dtype=BF16, grad_of="wq_cross")
    dhc = call(_mm, "mm_dhc", dqc, full["wq_cross"], mode="nt", out_dtype=F32)
    call(_mm, "mm_dwkv", mn, dkvc, mode="tn", out_dtype=BF16, grad_of="wkv_cross")
    dmn = call(_mm, "mm_dmn", dkvc, full["wkv_cross"], mode="nt", out_dtype=F32)
    _, gs["ln_mem_w"] = _rms_bwd(mem, small["ln_mem_w"], dmn, None, "rms_mem_bwd")
    dx1, gs["ln_cross_w"] = _rms_bwd(x1, small["ln_cross_w"], dhc, dx2, "rms_cross_bwd")
    dmerged = call(_mm, "mm_dmerged", dx1, full["w_out"], mode="nt", out_dtype=F32)
    call(_mm, "mm_dwout", merged, dx1, mode="tn", out_dtype=BF16, grad_of="w_out")
    dya, dyb, dga, dgb = call(_gate_bwd, "gate_bwd", proj, ya, yb, dmerged)
    call(_mm, "mm_dwa", o_att, dya, mode="tn", out_dtype=BF16, grad_of="w_branch_a")
    do_att = call(_mm, "mm_doatt", dya, full["w_branch_a"], mode="nt", out_dtype=F32)
    call(_mm, "mm_dwb", o_hg, dyb, mode="tn", out_dtype=BF16, grad_of="w_branch_b")
    do_hg = call(_mm, "mm_dohg", dyb, full["w_branch_b"], mode="nt", out_dtype=F32)
    dqh, dfh, dih, dgh, dlb, gs["hg_norm_w"] = call(
        _hg_bwd, "hgrn_bwd", proj, small["hg_lower_bounds"], small["hg_norm_w"], oraw, states, do_hg)
    gs["hg_lb"] = dlb
    do_gs, dl_gs = call(_attn_merge_bwd, "attn_merge_bwd", outs, lses, do_att)
    dqs, dks, dvs = zip(*[call(_attn_bwd, f"attn_bwd_g{g}", proj, g, lses[g], do_gs[g], dl_gs[g]) for g in range(3)])
    dproj = jnp.concatenate([*dqs, *dks, *dvs, dqh, dfh, dih, dgh, dga, dgb], axis=1)
    call(_mm, "mm_dwin", h, dproj, mode="tn", out_dtype=BF16, grad_of="w_in")
    dh = call(_mm, "mm_dh", dproj, full["w_in"], mode="nt", out_dtype=F32)
    dx, gs["ln_mix_w"] = _rms_bwd(x, small["ln_mix_w"], dh, dx1, "rms_mix_bwd")
    return loss, dx, gs


WEIGHT_ORDER = ("ln_mix_w", "w_in", "hg_norm_w", "hg_lower_bounds", "w_branch_a", "w_branch_b", "w_out",
                "ln_cross_w", "ln_mem_w", "wq_cross", "wkv_cross", "wo_cross", "ln_ffn_w", "w1", "w3", "w2",
                "ln_final_w")


def kernel(x, mem, ln_mix_w, w_in, hg_norm_w, hg_lower_bounds, w_branch_a, w_branch_b, w_out, ln_cross_w, ln_mem_w, wq_cross, wkv_cross, wo_cross, ln_ffn_w, w1, w3, w2, ln_final_w, loss_target, m_ln_mix_w, m_w_in, m_hg_norm_w, m_hg_lower_bounds, m_w_branch_a, m_w_branch_b, m_w_out, m_ln_cross_w, m_ln_mem_w, m_wq_cross, m_wkv_cross, m_wo_cross, m_ln_ffn_w, m_w1, m_w3, m_w2, m_ln_final_w, v_ln_mix_w, v_w_in, v_hg_norm_w, v_hg_lower_bounds, v_w_branch_a, v_w_branch_b, v_w_out, v_ln_cross_w, v_ln_mem_w, v_wq_cross, v_wkv_cross, v_wo_cross, v_ln_ffn_w, v_w1, v_w3, v_w2, v_ln_final_w):
    args = dict(locals())
    w = {n: args[n] for n in WEIGHT_ORDER}
    m = {n: args["m_" + n] for n in WEIGHT_ORDER}
    v = {n: args["v_" + n] for n in WEIGHT_ORDER}
    shapes = {n: w[n].shape for n in WEIGHT_ORDER}
    mat = lambda a: a.reshape(a.shape[-2:])
    shard_shapes = {n: shapes[n][-2:] for n in BIG_WEIGHTS}

    pos = _mesh_scalars()

    def cast(f, token=None):
        return _cast_into_full({n: mat(w[n]) for n in BIG_WEIGHTS if WEIGHT_PLACE[n][0] == f}, f, pos,
                               FULL_SPECS, WEIGHT_PLACE, f"cast_{f}", token)

    n_first = len(W_IN_PIECES)
    net = _Net({"w_in": cast("w_in")}, pos, shard_shapes)
    net.start(GATHER_GROUPS[:n_first], "gather", "gather_start_w_in")
    net.full.update({f: cast(f, net.token) for f in FULL_SPECS if f != "w_in"})
    net.start(GATHER_GROUPS[n_first:], "gather", "gather_start_rest")
    small = {n: w[n].reshape(1, -1) for n in SMALL_ROWS}
    small["hg_norm_w"] = w["hg_norm_w"].reshape(1, HEAD_DIM)
    small["hg_lower_bounds"] = w["hg_lower_bounds"]
    x2d = x.reshape(SEQ, D_MODEL)
    h = _rms_fwd(x2d, small["ln_mix_w"], "rms_mix")
    for i, job in enumerate(W_IN_PIECES):
        net.wait(f"w_in{i}")
        net.run_comm(("d2d", [job]), f"gather_hand_over_w_in{i}")
    loss, dx, gs = _local_step(net, x2d, h, mem.reshape(MEM_LEN, D_MODEL), loss_target.reshape(SEQ, D_MODEL), small)

    out_g, out_d, out_m, out_v = {}, {}, {}, {}
    net.last = dx
    for step in FINISH:
        if step[0] == "adamw":
            for n in step[1]:
                out_d[n], out_m[n], out_v[n], out_g[n] = net.call(_adamw, f"adamw_{n}", mat(w[n]), net.grads[n],
                                                                  mat(m[n]), mat(v[n]))
        elif step[0] == "wait":
            net.wait(step[1], after=list(out_d.values()))
        elif step[0] == "small":
            pad = lambda a: jnp.pad(a, ((0, 0), (0, D_MODEL - a.shape[1])))
            part = jnp.concatenate(
                [gs[n] for n in SMALL_ROWS]
                + [pad(jnp.concatenate([gs["hg_norm_w"][0], loss], axis=1)), pad(gs["hg_lb"]),
                   pad(gs["hg_norm_w"][1])], axis=0)
            part, net.psum["w_in"] = lax.optimization_barrier((part, net.psum["w_in"]))
            sg, sd, sm, sv, loss_tot = _small_update(_gather_rows(part), _pack_small(w), _pack_small(m),
                                                     _pack_small(v))
            for dst, packed in ((out_g, sg), (out_d, sd), (out_m, sm), (out_v, sv)):
                dst.update(_unpack_small(packed, shapes))
        else:
            net.step(step)

    result = [loss_tot[0, 0], dx.reshape(x.shape)]
    for group in (out_g, out_d, out_m, out_v):
        result += [group[n].reshape(shapes[n]) for n in WEIGHT_ORDER]
    return tuple(result)
```

```python
import math

import jax
import jax.numpy as jnp
from jax import lax
from jax.experimental import pallas as pl
from jax.experimental.pallas import tpu as pltpu

F32 = jnp.float32
BF16 = jnp.bfloat16
MESH = pl.DeviceIdType.MESH

D_MODEL = 2048
SEQ = 2048
HEAD_DIM = 128
MEM_LEN = 256
ATT_GROUPS = ((128, 1), (512, 4), (2048, 16))
ATT_HEADS = 4
ATT_WIDTH = 3 * ATT_HEADS * HEAD_DIM
ATT_OUT = ATT_HEADS * HEAD_DIM
ATT_BLOCK = 128
HG_HEADS = 8
HG_WIDTH = HG_HEADS * HEAD_DIM
HG_CHUNK = 64
IN_WIDTH = 3 * ATT_WIDTH + 4 * HG_WIDTH + 2 * D_MODEL
CROSS_HEADS = 4
CROSS_WIDTH = CROSS_HEADS * HEAD_DIM
D_FF = 5632
RMS_EPS = 1e-6
ADAM_LR = 0.001
ADAM_B1 = 0.9
ADAM_B2 = 0.999
ADAM_EPS = 1e-08
ADAM_WD = 0.01
ADAM_STEP = 10
N_CHIPS = 4
N_DEV = 8

VMEM_LIMIT_BYTES = 56 * 1024 * 1024
LANE = 128
MXU_WIDTH = 256
MM_TILE_CAP = 1536
TRANSPOSE_CHUNK = 512
ANY = pl.BlockSpec(memory_space=pl.ANY)


def _cparams(sem=None):
    return pltpu.CompilerParams(dimension_semantics=sem, vmem_limit_bytes=VMEM_LIMIT_BYTES)


def _div(n, cap, mult):
    best = None
    for d in range(mult, min(n, cap) + 1, mult):
        if n % d == 0:
            best = d
    assert best is not None, (n, cap, mult)
    return best


def _sigmoid(x):
    return 1.0 / (1.0 + jnp.exp(-x))


def _dot(a, b):
    return jnp.dot(a.astype(BF16), b.astype(BF16), preferred_element_type=F32)


def _dot_nt(a, b):
    return lax.dot_general(a.astype(BF16), b.astype(BF16), (((1,), (1,)), ((), ())),
                           preferred_element_type=F32)


def _dot_tn(a, b):
    return jnp.dot(a.astype(F32).T.astype(BF16), b.astype(BF16), preferred_element_type=F32)


def _dot_exact(a, b):
    return jnp.dot(a, b, precision=lax.Precision.HIGHEST, preferred_element_type=F32)


class _Carried:
    def __init__(self, arrays, fresh, n_sems, start, finish, mid=None, reads=None):
        self.arrays, self.fresh, self.n_sems, self.reads = arrays, fresh, n_sems, reads or {}
        self.start, self.mid, self.finish = start, mid, finish


class _Token:
    def __init__(self, array):
        self.array = array


TOKEN_SHAPE = (8, LANE)


def _carried_layout(carried):
    akeys = list(dict.fromkeys(k for cm in carried for k in cm.arrays))
    fkeys = [(ci, k) for ci, cm in enumerate(carried) for k in cm.fresh]
    arrays = [next(cm.arrays[k] for cm in carried if k in cm.arrays) for k in akeys]
    shapes = [jax.ShapeDtypeStruct(a.shape, a.dtype) for a in arrays] + [carried[ci].fresh[k] for ci, k in fkeys]
    sems = []
    for cm in carried:
        sems += [pltpu.SemaphoreType.DMA((cm.n_sems,)), pltpu.SemaphoreType.DMA((cm.n_sems,))]
    return akeys, fkeys, arrays, shapes, sems


def _carried_reads(carried):
    rkeys = list(dict.fromkeys(k for cm in carried for k in cm.reads))
    return rkeys, [next(cm.reads[k] for cm in carried if k in cm.reads) for k in rkeys]


def _carried_results(carried, akeys, fkeys, outs, rkeys=(), read_refs=()):
    shared = dict(zip(akeys, outs[:len(akeys)]))
    shared.update(zip(rkeys, read_refs))
    res = [{k: shared[k] for k in list(cm.arrays) + [r for r in cm.reads if r in shared]} for cm in carried]
    for (ci, k), o in zip(fkeys, outs[len(akeys):]):
        res[ci][k] = o
    return res


def _pcall(kern, *, name, grid, in_specs, out_specs, out_shape, args, scratch_shapes=(), semantics=None,
           carried=()):
    tokens = [c.array for c in carried if isinstance(c, _Token)]
    carried = [c for c in carried if not isinstance(c, _Token)]
    single = not isinstance(out_shape, (list, tuple))
    out_specs = [out_specs] if single else list(out_specs)
    out_shape = [out_shape] if single else list(out_shape)
    n_real, n_out, n_scr = len(in_specs), len(out_shape), len(scratch_shapes)
    in_specs = list(in_specs) + [pl.BlockSpec(TOKEN_SHAPE, lambda *_: (0, 0))] * len(tokens)
    args = list(args) + tokens
    n_in = len(in_specs)
    if not carried:
        def plain(*refs):
            kern(*refs[:n_real], *refs[n_in:])

        outs = pl.pallas_call(plain if tokens else kern, name=name, grid=grid, in_specs=in_specs,
                              out_specs=out_specs, out_shape=out_shape, scratch_shapes=list(scratch_shapes),
                              compiler_params=_cparams(semantics))(*args)
        return (outs[0] if single else list(outs)), []
    akeys, fkeys, arrays, shapes, sems = _carried_layout(carried)
    rkeys, reads = _carried_reads(carried)
    n_a, n_f, n_r = len(akeys), len(fkeys), len(rkeys)
    total = math.prod(grid)
    mid_step = min(total - 1, (17 * total) // 20)

    def wrapped(*refs):
        ins = refs[:n_real]
        r0 = n_in + n_a
        o0 = r0 + n_r
        outs = refs[o0:o0 + n_out]
        a0 = o0 + n_out
        s0 = a0 + n_a + n_f
        per = _carried_results(carried, akeys, fkeys, refs[a0:s0], rkeys, refs[r0:o0])
        scratch = refs[s0:s0 + n_scr]
        sem = refs[s0 + n_scr:]
        step = 0
        for d, g in enumerate(grid):
            step = step * g + pl.program_id(d)

        @pl.when(step == 0)
        def _():
            for ci, cm in enumerate(carried):
                cm.start(per[ci], sem[2 * ci], sem[2 * ci + 1])

        kern(*ins, *outs, *scratch)

        @pl.when(step == mid_step)
        def _():
            for ci, cm in enumerate(carried):
                if cm.mid is not None:
                    cm.mid(per[ci], sem[2 * ci], sem[2 * ci + 1])

        @pl.when(step == total - 1)
        def _():
            for ci, cm in enumerate(carried):
                cm.finish(per[ci], sem[2 * ci], sem[2 * ci + 1])

    outs = pl.pallas_call(
        wrapped, name=name, grid=grid,
        in_specs=list(in_specs) + [ANY] * (n_a + n_r), out_specs=out_specs + [ANY] * (n_a + n_f),
        out_shape=out_shape + shapes,
        input_output_aliases={n_in + i: n_out + i for i in range(n_a)},
        scratch_shapes=list(scratch_shapes) + sems,
        compiler_params=_cparams(("arbitrary",) * len(grid)),
    )(*args, *arrays, *reads)
    res = _carried_results(carried, akeys, fkeys, outs[n_out:])
    return (outs[0] if single else list(outs[:n_out])), res


def _run_comm(carried, name):
    carried = list(carried)
    akeys, fkeys, arrays, shapes, sems = _carried_layout(carried)
    rkeys, reads = _carried_reads(carried)
    n_a, n_f, n_r = len(akeys), len(fkeys), len(rkeys)

    def body(*refs):
        o0 = n_a + n_r
        per = _carried_results(carried, akeys, fkeys, refs[o0:o0 + n_a + n_f], rkeys, refs[n_a:o0])
        sem = refs[o0 + n_a + n_f:]
        for hook in ("start", "mid", "finish"):
            for ci, cm in enumerate(carried):
                fn = getattr(cm, hook)
                if fn is not None:
                    fn(per[ci], sem[2 * ci], sem[2 * ci + 1])

    outs = pl.pallas_call(
        body, name=name, in_specs=[ANY] * (n_a + n_r), out_specs=[ANY] * (n_a + n_f), out_shape=shapes,
        input_output_aliases={i: i for i in range(n_a)}, scratch_shapes=sems,
    )(*arrays, *reads)
    return _carried_results(carried, akeys, fkeys, outs)


HBM_SPEC = pl.BlockSpec(memory_space=pltpu.HBM)
SEM_SPEC = pl.BlockSpec(memory_space=pltpu.SEMAPHORE)
SPLIT_EFFECT = pltpu.SideEffectType.DATAFLOW_SIDE_EFFECTING


def _in_hbm(a):
    return pltpu.with_memory_space_constraint(a, pltpu.HBM)


def _split_start(items, name, after=None):
    items = list(items)
    akeys, fkeys, arrays, shapes, sems = _carried_layout(items)
    assert not fkeys
    n_a, n_s = len(akeys), len(sems)
    n_in = n_a + (after is not None)

    def body(*refs):
        per = _carried_results(items, akeys, [], refs[n_in:n_in + n_a])
        sem = refs[n_in + n_a:n_in + n_a + n_s]
        for ci, cm in enumerate(items):
            cm.start(per[ci], sem[2 * ci], sem[2 * ci + 1])
        token = refs[n_in + n_a + n_s]
        token[...] = jnp.zeros_like(token)

    outs = pl.pallas_call(
        body, name=name, in_specs=[HBM_SPEC] * n_a + [ANY] * (after is not None),
        out_specs=[HBM_SPEC] * n_a + [SEM_SPEC] * n_s + [pl.BlockSpec(memory_space=pltpu.VMEM)],
        out_shape=[pltpu.HBM(s.shape, s.dtype) for s in shapes] + sems + [jax.ShapeDtypeStruct(TOKEN_SHAPE, F32)],
        input_output_aliases={i: i for i in range(n_a)},
        compiler_params=pltpu.CompilerParams(has_side_effects=SPLIT_EFFECT),
    )(*[_in_hbm(a) for a in arrays], *([after] if after is not None else []))
    res = _carried_results(items, akeys, [], outs[:n_a])
    sem_out = outs[n_a:n_a + n_s]
    return res, [(sem_out[2 * ci], sem_out[2 * ci + 1]) for ci in range(len(items))], outs[-1]


def _split_wait(items, sems, after, name):
    items = list(items)
    after = list(after) if isinstance(after, (list, tuple)) else [after]
    akeys, fkeys, arrays, shapes, _ = _carried_layout(items)
    n_a, n_s = len(akeys), 2 * len(items)

    def body(*refs):
        per = _carried_results(items, akeys, [], refs[n_a + n_s + len(after):])
        sem = refs[n_a:n_a + n_s]
        for ci, cm in enumerate(items):
            cm.finish(per[ci], sem[2 * ci], sem[2 * ci + 1])

    outs = pl.pallas_call(
        body, name=name, in_specs=[HBM_SPEC] * n_a + [SEM_SPEC] * n_s + [ANY] * len(after),
        out_specs=[HBM_SPEC] * n_a, out_shape=[pltpu.HBM(s.shape, s.dtype) for s in shapes],
        input_output_aliases={i: i for i in range(n_a)},
        compiler_params=pltpu.CompilerParams(has_side_effects=SPLIT_EFFECT),
    )(*arrays, *[s for pair in sems for s in pair], *after)
    return _carried_results(items, akeys, [], outs)


def _mm(a, b, *, mode, out_dtype, name, res=None, carried=()):
    if mode == "nn":
        (m, k), (k2, n) = a.shape, b.shape
    elif mode == "nt":
        (m, k), (n, k2) = a.shape, b.shape
    else:
        (k, m), (k2, n) = a.shape, b.shape
    assert k == k2, (name, a.shape, b.shape)
    tm = _div(m, MM_TILE_CAP, LANE)
    tn = _div(n, MM_TILE_CAP, MXU_WIDTH) if n % MXU_WIDTH == 0 else 0
    if tn < 1024:
        tn = _div(n, MM_TILE_CAP, LANE)
    out_shape = jax.ShapeDtypeStruct((m, n), out_dtype)

    if mode == "tn":
        assert res is None

        def kern_tn(a_ref, b_ref, o_ref, at_ref):
            @pl.when(pl.program_id(1) == 0)
            def _():
                step = min(TRANSPOSE_CHUNK, k)
                for c0 in range(0, k, step):
                    at_ref[:, c0:c0 + step] = a_ref[c0:c0 + step, :].astype(F32).T.astype(BF16)

            o_ref[...] = jnp.dot(at_ref[...], b_ref[...].astype(BF16),
                                 preferred_element_type=F32).astype(o_ref.dtype)

        return _pcall(
            kern_tn, name=name, grid=(m // tm, n // tn),
            in_specs=[pl.BlockSpec((k, tm), lambda i, j: (0, i)),
                      pl.BlockSpec((k, tn), lambda i, j: (0, j))],
            out_specs=pl.BlockSpec((tm, tn), lambda i, j: (i, j)),
            out_shape=out_shape, args=(a, b),
            scratch_shapes=[pltpu.VMEM((tm, k), BF16)],
            semantics=("parallel", "arbitrary"), carried=carried)

    tk = k if k <= 2048 else _div(k, 3072, LANE)
    nk = k // tk
    a_spec = pl.BlockSpec((tm, tk), lambda i, j, kk: (i, kk))
    if mode == "nn":
        b_spec = pl.BlockSpec((tk, tn), lambda i, j, kk: (kk, j))
        dot = _dot
    else:
        b_spec = pl.BlockSpec((tn, tk), lambda i, j, kk: (j, kk))
        dot = _dot_nt
    o_spec = pl.BlockSpec((tm, tn), lambda i, j, kk: (i, j))
    in_specs = [a_spec, b_spec]
    args = [a, b]
    if res is not None:
        in_specs.append(o_spec)
        args.append(res)
    has_res = res is not None

    def kern(*refs):
        a_ref, b_ref = refs[0], refs[1]
        r_ref = refs[2] if has_res else None
        o_ref = refs[3] if has_res else refs[2]
        part = dot(a_ref[...], b_ref[...])
        if nk == 1:
            if has_res:
                part = part + r_ref[...]
            o_ref[...] = part.astype(o_ref.dtype)
            return
        acc_ref = refs[-1]
        kk = pl.program_id(2)

        @pl.when(kk == 0)
        def _():
            acc_ref[...] = part

        @pl.when(kk > 0)
        def _():
            acc_ref[...] += part

        @pl.when(kk == nk - 1)
        def _():
            tot = acc_ref[...]
            if has_res:
                tot = tot + r_ref[...]
            o_ref[...] = tot.astype(o_ref.dtype)

    return _pcall(
        kern, name=name, grid=(m // tm, n // tn, nk),
        in_specs=in_specs, out_specs=o_spec, out_shape=out_shape, args=args,
        scratch_shapes=[pltpu.VMEM((tm, tn), F32)] if nk > 1 else [],
        semantics=("parallel", "parallel", "arbitrary"), carried=carried)


ROW_BLOCK = 256


def _rms_fwd(x, g, name):
    t, d = x.shape
    tr = min(ROW_BLOCK, t)

    def kern(x_ref, g_ref, o_ref):
        xf = x_ref[...]
        r = lax.rsqrt(jnp.mean(xf * xf, axis=-1, keepdims=True) + RMS_EPS)
        o_ref[...] = (xf * r * g_ref[...]).astype(o_ref.dtype)

    return pl.pallas_call(
        kern, name=name, grid=(t // tr,),
        in_specs=[pl.BlockSpec((tr, d), lambda i: (i, 0)), pl.BlockSpec((1, d), lambda i: (0, 0))],
        out_specs=pl.BlockSpec((tr, d), lambda i: (i, 0)),
        out_shape=jax.ShapeDtypeStruct((t, d), BF16),
        compiler_params=_cparams(("parallel",)),
    )(x, g)


def _rms_bwd(x, g, dh, res, name):
    t, d = x.shape
    tr = min(ROW_BLOCK, t)
    has_res = res is not None

    def kern(*refs):
        x_ref, g_ref, dh_ref = refs[:3]
        r_ref = refs[3] if has_res else None
        dx_ref, dg_ref = refs[-2], refs[-1]
        xf = x_ref[...]
        r = lax.rsqrt(jnp.mean(xf * xf, axis=-1, keepdims=True) + RMS_EPS)
        xn = xf * r
        dh_ = dh_ref[...]
        dhg = dh_ * g_ref[...]
        dx = r * (dhg - xn * jnp.mean(dhg * xn, axis=-1, keepdims=True))
        if has_res:
            dx = dx + r_ref[...]
        dx_ref[...] = dx
        part = jnp.sum(dh_ * xn, axis=0, keepdims=True)

        @pl.when(pl.program_id(0) == 0)
        def _():
            dg_ref[...] = part

        @pl.when(pl.program_id(0) > 0)
        def _():
            dg_ref[...] += part

    row = pl.BlockSpec((tr, d), lambda i: (i, 0))
    vec = pl.BlockSpec((1, d), lambda i: (0, 0))
    in_specs = [row, vec, row] + ([row] if has_res else [])
    args = [x, g, dh] + ([res] if has_res else [])
    return pl.pallas_call(
        kern, name=name, grid=(t // tr,), in_specs=in_specs, out_specs=[row, vec],
        out_shape=[jax.ShapeDtypeStruct((t, d), F32), jax.ShapeDtypeStruct((1, d), F32)],
        compiler_params=_cparams(("arbitrary",)),
    )(*args)


def _loss_head(x3, g, target, name):
    t, d = x3.shape
    tr = ROW_BLOCK

    def kern(x_ref, g_ref, t_ref, dx_ref, dg_ref, loss_ref):
        xf = x_ref[...]
        r = lax.rsqrt(jnp.mean(xf * xf, axis=-1, keepdims=True) + RMS_EPS)
        xn = xf * r
        gg = g_ref[...]
        err = xn * gg - t_ref[...]
        lpart = 0.5 * jnp.sum(jnp.mean(err * err, axis=-1, keepdims=True), axis=0, keepdims=True)
        dy = err * (1.0 / d)
        dyg = dy * gg
        dx_ref[...] = r * (dyg - xn * jnp.mean(dyg * xn, axis=-1, keepdims=True))
        gpart = jnp.sum(dy * xn, axis=0, keepdims=True)
        lrow = jnp.broadcast_to(lpart, (1, LANE))

        @pl.when(pl.program_id(0) == 0)
        def _():
            dg_ref[...] = gpart
            loss_ref[...] = lrow

        @pl.when(pl.program_id(0) > 0)
        def _():
            dg_ref[...] += gpart
            loss_ref[...] += lrow

    row = pl.BlockSpec((tr, d), lambda i: (i, 0))
    vec = pl.BlockSpec((1, d), lambda i: (0, 0))
    return pl.pallas_call(
        kern, name=name, grid=(t // tr,), in_specs=[row, vec, row],
        out_specs=[row, vec, pl.BlockSpec((1, LANE), lambda i: (0, 0))],
        out_shape=[jax.ShapeDtypeStruct((t, d), F32), jax.ShapeDtypeStruct((1, d), F32),
                   jax.ShapeDtypeStruct((1, LANE), F32)],
        compiler_params=_cparams(("arbitrary",)),
    )(x3, g, target)


ATT_SCALE = HEAD_DIM ** -0.5
Q_BLOCK0, K_BLOCK0, V_BLOCK0 = 0, ATT_WIDTH // HEAD_DIM, 2 * ATT_WIDTH // HEAD_DIM


def _residue_rows(dil, r, n):
    if dil == 1:
        return pl.ds(n * ATT_BLOCK, ATT_BLOCK)
    return pl.ds(n * ATT_BLOCK * dil + r, ATT_BLOCK, stride=dil)


def _band_mask(with_prev):
    width = 2 * ATT_BLOCK if with_prev else ATT_BLOCK
    iq = lax.broadcasted_iota(jnp.int32, (ATT_BLOCK, width), 0)
    ik = lax.broadcasted_iota(jnp.int32, (ATT_BLOCK, width), 1)
    if not with_prev:
        return ik <= iq
    return ((ik < ATT_BLOCK) & (iq <= ik)) | ((ik >= ATT_BLOCK) & ((ik - ATT_BLOCK) <= iq))


def _band_keys(ref, dil, r, n):
    own = ref[_residue_rows(dil, r, n), :]
    if n == 0:
        return own
    return jnp.concatenate([ref[_residue_rows(dil, r, n - 1), :], own], axis=0)


def _attn_col_spec(base, grp):
    return pl.BlockSpec((SEQ, HEAD_DIM), lambda h: (0, base + grp * ATT_HEADS + h))


def _attn_fwd(proj, grp, name, carried=()):
    _, dil = ATT_GROUPS[grp]
    nb = SEQ // dil // ATT_BLOCK

    def kern(q_ref, k_ref, v_ref, o_ref, lse_ref):
        for r in range(dil):
            for n in range(nb):
                rows = _residue_rows(dil, r, n)
                s = _dot_nt(q_ref[rows, :], _band_keys(k_ref, dil, r, n)) * ATT_SCALE
                s = jnp.where(_band_mask(n > 0), s, -jnp.inf)
                m = jnp.max(s, axis=-1, keepdims=True)
                p = jnp.exp(s - m)
                l = jnp.sum(p, axis=-1, keepdims=True)
                o_ref[rows, :] = _dot(p / l, _band_keys(v_ref, dil, r, n))
                lse_ref[rows, :] = jnp.broadcast_to(m + jnp.log(l), (ATT_BLOCK, HEAD_DIM))

    out_spec = pl.BlockSpec((SEQ, HEAD_DIM), lambda h: (0, h))
    return _pcall(
        kern, name=name, grid=(ATT_HEADS,),
        in_specs=[_attn_col_spec(Q_BLOCK0, grp), _attn_col_spec(K_BLOCK0, grp), _attn_col_spec(V_BLOCK0, grp)],
        out_specs=[out_spec, out_spec],
        out_shape=[jax.ShapeDtypeStruct((SEQ, ATT_OUT), F32)] * 2, args=(proj, proj, proj),
        semantics=("parallel",), carried=carried)


def _attn_weights(l0, l1, l2):
    mx = jnp.maximum(jnp.maximum(l0, l1), l2)
    e0, e1, e2 = jnp.exp(l0 - mx), jnp.exp(l1 - mx), jnp.exp(l2 - mx)
    den = e0 + e1 + e2
    return e0 / den, e1 / den, e2 / den


def _attn_merge_fwd(outs, lses, name):
    tr = ROW_BLOCK

    def kern(o0, o1, o2, l0, l1, l2, out_ref):
        a0, a1, a2 = _attn_weights(l0[...], l1[...], l2[...])
        out_ref[...] = (a0 * o0[...] + a1 * o1[...] + a2 * o2[...]).astype(out_ref.dtype)

    spec = pl.BlockSpec((tr, ATT_OUT), lambda i: (i, 0))
    return pl.pallas_call(
        kern, name=name, grid=(SEQ // tr,), in_specs=[spec] * 6, out_specs=spec,
        out_shape=jax.ShapeDtypeStruct((SEQ, ATT_OUT), BF16),
        compiler_params=_cparams(("parallel",)),
    )(*outs, *lses)


def _attn_merge_bwd(outs, lses, do_att, name, carried=()):
    tr = ROW_BLOCK

    def kern(o0, o1, o2, l0, l1, l2, do_ref, d0, d1, d2, t0, t1, t2):
        alphas = _attn_weights(l0[...], l1[...], l2[...])
        do = do_ref[...]
        o_att = alphas[0] * o0[...] + alphas[1] * o1[...] + alphas[2] * o2[...]
        prod = do * o_att
        parts = []
        for h in range(ATT_HEADS):
            sl = slice(h * HEAD_DIM, (h + 1) * HEAD_DIM)
            tot = jnp.sum(prod[:, sl], axis=-1, keepdims=True)
            parts.append(jnp.broadcast_to(tot, (tr, HEAD_DIM)))
        dd = jnp.concatenate(parts, axis=1)
        for a, d_ref, t_ref in zip(alphas, (d0, d1, d2), (t0, t1, t2)):
            d_ref[...] = a * do
            t_ref[...] = -a * dd

    spec = pl.BlockSpec((tr, ATT_OUT), lambda i: (i, 0))
    res, cres = _pcall(
        kern, name=name, grid=(SEQ // tr,), in_specs=[spec] * 7, out_specs=[spec] * 6,
        out_shape=[jax.ShapeDtypeStruct((SEQ, ATT_OUT), F32)] * 6, args=(*outs, *lses, do_att),
        semantics=("parallel",), carried=carried)
    return (res[:3], res[3:]), cres


def _attn_bwd(proj, grp, lse, do_g, dl_g, name, carried=()):
    _, dil = ATT_GROUPS[grp]
    nb = SEQ // dil // ATT_BLOCK

    def kern(q_ref, k_ref, v_ref, do_ref, lse_ref, dl_ref, dq_ref, dk_ref, dv_ref, dq_acc, dk_acc, dv_acc):
        dk_acc[...] = jnp.zeros_like(dk_acc)
        dv_acc[...] = jnp.zeros_like(dv_acc)
        for r in range(dil):
            for n in range(nb):
                rows = _residue_rows(dil, r, n)
                q, do = q_ref[rows, :], do_ref[rows, :]
                kk, vv = _band_keys(k_ref, dil, r, n), _band_keys(v_ref, dil, r, n)
                s = _dot_nt(q, kk) * ATT_SCALE
                p = jnp.where(_band_mask(n > 0), jnp.exp(s - lse_ref[rows, :][:, :1]), 0.0)
                ds = p * (_dot_nt(do, vv) + dl_ref[rows, :][:, :1])
                dq_acc[rows, :] = _dot(ds, kk) * ATT_SCALE
                dk = _dot_tn(ds, q) * ATT_SCALE
                dv = _dot_tn(p, do)
                if n > 0:
                    prev = _residue_rows(dil, r, n - 1)
                    dk_acc[prev, :] += dk[:ATT_BLOCK]
                    dv_acc[prev, :] += dv[:ATT_BLOCK]
                    dk, dv = dk[ATT_BLOCK:], dv[ATT_BLOCK:]
                dk_acc[rows, :] += dk
                dv_acc[rows, :] += dv
        dq_ref[...] = dq_acc[...].astype(dq_ref.dtype)
        dk_ref[...] = dk_acc[...].astype(dk_ref.dtype)
        dv_ref[...] = dv_acc[...].astype(dv_ref.dtype)

    spec = pl.BlockSpec((SEQ, HEAD_DIM), lambda h: (0, h))
    return _pcall(
        kern, name=name, grid=(ATT_HEADS,),
        in_specs=[_attn_col_spec(Q_BLOCK0, grp), _attn_col_spec(K_BLOCK0, grp), _attn_col_spec(V_BLOCK0, grp),
                  spec, spec, spec],
        out_specs=[spec] * 3,
        out_shape=[jax.ShapeDtypeStruct((SEQ, ATT_OUT), BF16)] * 3, args=(proj, proj, proj, do_g, lse, dl_g),
        scratch_shapes=[pltpu.VMEM((SEQ, HEAD_DIM), F32)] * 3,
        semantics=("parallel",), carried=carried)


HG_HEADS_PER_STEP = 4
HG_BLOCK_W = HG_HEADS_PER_STEP * HEAD_DIM
HG_Q_BLK = (3 * ATT_WIDTH) // HG_BLOCK_W
HG_N_CHUNKS = SEQ // HG_CHUNK
HG_MID = HG_CHUNK // 2


def _lower_bound(lb_ref, sl):
    l0, l1 = lb_ref[0:1, sl], lb_ref[1:2, sl]
    mx = jnp.maximum(l0, l1)
    e0, e1 = jnp.exp(l0 - mx), jnp.exp(l1 - mx)
    return e0 / (e0 + e1)


def _tri(lower):
    i = lax.broadcasted_iota(jnp.int32, (HG_CHUNK, HG_CHUNK), 0)
    j = lax.broadcasted_iota(jnp.int32, (HG_CHUNK, HG_CHUNK), 1)
    return (i >= j) if lower else (i <= j)


def _hg_chunk_terms(qh, fh, lb):
    sig = _sigmoid(fh)
    f = lb + (1.0 - lb) * sig
    k = 1.0 - f
    b = _dot_exact(_tri(True).astype(F32), jnp.log(f))
    bl = b[HG_CHUNK - 1:HG_CHUNK, :]
    br = b[HG_MID:HG_MID + 1, :]
    sq = _sigmoid(qh)
    q = qh * sq
    return dict(sig=sig, f=f, k=k, b=b, bl=bl, br=br, sq=sq, q=q,
                e1=jnp.exp(bl - b), e2=jnp.exp(b), e3=jnp.exp(b - br), e4=jnp.exp(br - b))


def _hg_fwd(proj, lbw, normw, name, carried=()):
    def in_blk(off):
        return pl.BlockSpec((HG_CHUNK, HG_BLOCK_W), lambda hp, n: (n, HG_Q_BLK + off + hp))

    def kern(q_ref, f_ref, i_ref, g_ref, lb_ref, nw_ref, oraw_ref, ohg_ref, st_ref, state):
        @pl.when(pl.program_id(1) == 0)
        def _():
            state[...] = jnp.zeros_like(state)

        causal = _tri(True)
        for hd in range(HG_HEADS_PER_STEP):
            sl = slice(hd * HEAD_DIM, (hd + 1) * HEAD_DIM)
            t = _hg_chunk_terms(q_ref[:, sl], f_ref[:, sl], _lower_bound(lb_ref, sl))
            v = i_ref[:, sl]
            st = state[hd]
            st_ref[0, hd] = st
            kd = t["k"] * t["e1"]
            inter = _dot_nt(t["q"] * t["e2"], st)
            a = jnp.where(causal, _dot_nt(t["q"] * t["e3"], t["k"] * t["e4"]), 0.0)
            o = inter + _dot(a, v)
            state[hd] = st * jnp.exp(t["bl"]) + _dot_tn(v, kd)
            oraw_ref[:, sl] = o
            r = lax.rsqrt(jnp.mean(o * o, axis=-1, keepdims=True) + RMS_EPS)
            gh = g_ref[:, sl]
            ohg_ref[:, sl] = (o * r * nw_ref[...] * (gh * _sigmoid(gh))).astype(ohg_ref.dtype)

    out_blk = pl.BlockSpec((HG_CHUNK, HG_BLOCK_W), lambda hp, n: (n, hp))
    return _pcall(
        kern, name=name, grid=(HG_HEADS // HG_HEADS_PER_STEP, HG_N_CHUNKS),
        in_specs=[in_blk(0), in_blk(2), in_blk(4), in_blk(6),
                  pl.BlockSpec((2, HG_BLOCK_W), lambda hp, n: (0, hp)),
                  pl.BlockSpec((1, HEAD_DIM), lambda hp, n: (0, 0))],
        out_specs=[out_blk, out_blk,
                   pl.BlockSpec((1, HG_HEADS_PER_STEP, HEAD_DIM, HEAD_DIM), lambda hp, n: (n, hp, 0, 0))],
        out_shape=[jax.ShapeDtypeStruct((SEQ, HG_WIDTH), F32), jax.ShapeDtypeStruct((SEQ, HG_WIDTH), BF16),
                   jax.ShapeDtypeStruct((HG_N_CHUNKS, HG_HEADS, HEAD_DIM, HEAD_DIM), F32)],
        args=(proj, proj, proj, proj, lbw, normw),
        scratch_shapes=[pltpu.VMEM((HG_HEADS_PER_STEP, HEAD_DIM, HEAD_DIM), F32)],
        semantics=("parallel", "arbitrary"), carried=carried)


def _hg_bwd(proj, lbw, normw, oraw, states, do_hg, name, carried=()):
    last = HG_N_CHUNKS - 1

    def in_blk(off):
        return pl.BlockSpec((HG_CHUNK, HG_BLOCK_W), lambda hp, n: (last - n, HG_Q_BLK + off + hp))

    blk = pl.BlockSpec((HG_CHUNK, HG_BLOCK_W), lambda hp, n: (last - n, hp))

    def kern(q_ref, f_ref, i_ref, g_ref, lb_ref, nw_ref, oraw_ref, st_ref, do_ref,
             dq_ref, df_ref, di_ref, dg_ref, dlb_ref, dnw_ref, dstate):
        first = pl.program_id(1) == 0

        @pl.when(first)
        def _():
            dstate[...] = jnp.zeros_like(dstate)

        causal = _tri(True)
        rows = lax.broadcasted_iota(jnp.int32, (HG_CHUNK, HEAD_DIM), 0)
        nw = nw_ref[...]
        dnw_tot = jnp.zeros((1, HEAD_DIM), F32)
        dlb_parts = []
        for hd in range(HG_HEADS_PER_STEP):
            sl = slice(hd * HEAD_DIM, (hd + 1) * HEAD_DIM)
            qh, fh, v, gh = q_ref[:, sl], f_ref[:, sl], i_ref[:, sl], g_ref[:, sl]
            o, dout = oraw_ref[:, sl], do_ref[:, sl]
            sgg = _sigmoid(gh)
            r = lax.rsqrt(jnp.mean(o * o, axis=-1, keepdims=True) + RMS_EPS)
            xn = o * r
            dg_ref[:, sl] = (dout * xn * nw * (sgg * (1.0 + gh * (1.0 - sgg)))).astype(dg_ref.dtype)
            don = dout * (gh * sgg)
            dnw_tot = dnw_tot + jnp.sum(don * xn, axis=0, keepdims=True)
            tt = don * nw
            do = r * (tt - xn * jnp.mean(tt * xn, axis=-1, keepdims=True))
            lb = _lower_bound(lb_ref, sl)
            t = _hg_chunk_terms(qh, fh, lb)
            k, q = t["k"], t["q"]
            kd, qb, qr, kr = k * t["e1"], q * t["e2"], q * t["e3"], k * t["e4"]
            a = jnp.where(causal, _dot_nt(qr, kr), 0.0)
            st = st_ref[0, hd]
            dstn = dstate[hd]
            dqb = _dot(do, st)
            da = jnp.where(causal, _dot_nt(do, v), 0.0)
            dv = _dot_tn(a, do) + _dot_nt(kd, dstn)
            dqr = _dot(da, kr)
            dkr = _dot_tn(da, qr)
            dkd = _dot(v, dstn)
            decay = jnp.exp(t["bl"])
            ddecay = jnp.sum(dstn * st, axis=0, keepdims=True)
            dstate[hd] = dstn * decay + _dot_tn(do, qb)
            dq = dqb * t["e2"] + dqr * t["e3"]
            dk = dkd * t["e1"] + dkr * t["e4"]
            db = dqb * qb + dqr * qr - dkr * kr - dkd * kd
            dbl = jnp.sum(dkd * kd, axis=0, keepdims=True) + ddecay * decay
            dbr = jnp.sum(dkr * kr - dqr * qr, axis=0, keepdims=True)
            dlf = _dot_exact(_tri(False).astype(F32), db) + dbl + jnp.where(rows <= HG_MID, dbr, 0.0)
            df = dlf / t["f"] - dk
            sig, sq = t["sig"], t["sq"]
            df_ref[:, sl] = (df * (1.0 - lb) * sig * (1.0 - sig)).astype(df_ref.dtype)
            dlb_parts.append(jnp.sum(df * (1.0 - sig), axis=0, keepdims=True))
            dq_ref[:, sl] = (dq * (sq * (1.0 + qh * (1.0 - sq)))).astype(dq_ref.dtype)
            di_ref[:, sl] = dv.astype(di_ref.dtype)
        dlb_row = jnp.concatenate(dlb_parts, axis=1)
        dnw_blk = jnp.broadcast_to(dnw_tot, (8, HEAD_DIM))

        @pl.when(first)
        def _():
            dlb_ref[...] = dlb_row
            dnw_ref[...] = dnw_blk

        @pl.when(jnp.logical_not(first))
        def _():
            dlb_ref[...] += dlb_row
            dnw_ref[...] += dnw_blk

    n_hp = HG_HEADS // HG_HEADS_PER_STEP
    outs, cres = _pcall(
        kern, name=name, grid=(n_hp, HG_N_CHUNKS),
        in_specs=[in_blk(0), in_blk(2), in_blk(4), in_blk(6),
                  pl.BlockSpec((2, HG_BLOCK_W), lambda hp, n: (0, hp)),
                  pl.BlockSpec((1, HEAD_DIM), lambda hp, n: (0, 0)),
                  blk,
                  pl.BlockSpec((1, HG_HEADS_PER_STEP, HEAD_DIM, HEAD_DIM), lambda hp, n: (last - n, hp, 0, 0)),
                  blk],
        out_specs=[blk, blk, blk, blk,
                   pl.BlockSpec((1, HG_BLOCK_W), lambda hp, n: (0, hp)),
                   pl.BlockSpec((8, HEAD_DIM), lambda hp, n: (hp, 0))],
        out_shape=[jax.ShapeDtypeStruct((SEQ, HG_WIDTH), BF16)] * 4
        + [jax.ShapeDtypeStruct((1, HG_WIDTH), F32), jax.ShapeDtypeStruct((8 * n_hp, HEAD_DIM), F32)],
        args=(proj, proj, proj, proj, lbw, normw, oraw, states, do_hg),
        scratch_shapes=[pltpu.VMEM((HG_HEADS_PER_STEP, HEAD_DIM, HEAD_DIM), F32)],
        semantics=("parallel", "arbitrary"), carried=carried)
    dqh, dfh, dih, dgh, dlb, dnw = outs
    return (dqh, dfh, dih, dgh, dlb, (dnw[0:1], dnw[8:9])), cres


GATE_BLOCK_W = 512
GATE_A_BLK = (3 * ATT_WIDTH + 4 * HG_WIDTH) // GATE_BLOCK_W
GATE_B_BLK = GATE_A_BLK + D_MODEL // GATE_BLOCK_W


def _gate_specs():
    tr = ROW_BLOCK
    blk = pl.BlockSpec((tr, GATE_BLOCK_W), lambda i, j: (i, j))
    ga = pl.BlockSpec((tr, GATE_BLOCK_W), lambda i, j: (i, GATE_A_BLK + j))
    gb = pl.BlockSpec((tr, GATE_BLOCK_W), lambda i, j: (i, GATE_B_BLK + j))
    return (SEQ // tr, D_MODEL // GATE_BLOCK_W), blk, ga, gb


def _gate_fwd(proj, ya, yb, name, carried=()):
    grid, blk, ga, gb = _gate_specs()

    def kern(ga_ref, gb_ref, ya_ref, yb_ref, o_ref):
        o_ref[...] = (_sigmoid(ga_ref[...]) * ya_ref[...] + _sigmoid(gb_ref[...]) * yb_ref[...]).astype(o_ref.dtype)

    return _pcall(
        kern, name=name, grid=grid, in_specs=[ga, gb, blk, blk], out_specs=blk,
        out_shape=jax.ShapeDtypeStruct((SEQ, D_MODEL), BF16), args=(proj, proj, ya, yb),
        semantics=("parallel", "parallel"), carried=carried)


def _gate_bwd(proj, ya, yb, dmerged, name, carried=()):
    grid, blk, ga, gb = _gate_specs()

    def kern(ga_ref, gb_ref, ya_ref, yb_ref, dm_ref, dya_ref, dyb_ref, dga_ref, dgb_ref):
        dm = dm_ref[...]
        sa, sb = _sigmoid(ga_ref[...]), _sigmoid(gb_ref[...])
        dya_ref[...] = (dm * sa).astype(dya_ref.dtype)
        dyb_ref[...] = (dm * sb).astype(dyb_ref.dtype)
        dga_ref[...] = (dm * ya_ref[...] * sa * (1.0 - sa)).astype(dga_ref.dtype)
        dgb_ref[...] = (dm * yb_ref[...] * sb * (1.0 - sb)).astype(dgb_ref.dtype)

    return _pcall(
        kern, name=name, grid=grid, in_specs=[ga, gb, blk, blk, blk], out_specs=[blk] * 4,
        out_shape=[jax.ShapeDtypeStruct((SEQ, D_MODEL), BF16)] * 4, args=(proj, proj, ya, yb, dmerged),
        semantics=("parallel", "parallel"), carried=carried)


FF_SHARD = D_FF // N_CHIPS


def _swiglu_fwd(ab, name):
    tr = ROW_BLOCK

    def kern(ab_ref, u_ref):
        a, b = ab_ref[:, :FF_SHARD], ab_ref[:, FF_SHARD:]
        u_ref[...] = (a * _sigmoid(a) * b).astype(u_ref.dtype)

    return pl.pallas_call(
        kern, name=name, grid=(SEQ // tr, N_CHIPS),
        in_specs=[pl.BlockSpec((tr, 2 * FF_SHARD), lambda i, j: (i, j))],
        out_specs=pl.BlockSpec((tr, FF_SHARD), lambda i, j: (i, j)),
        out_shape=jax.ShapeDtypeStruct((SEQ, D_FF), BF16),
        compiler_params=_cparams(("parallel", "parallel")),
    )(ab)


def _swiglu_bwd(ab, du, name, carried=()):
    tr = ROW_BLOCK

    def kern(ab_ref, du_ref, dab_ref):
        a, b = ab_ref[:, :FF_SHARD], ab_ref[:, FF_SHARD:]
        du_ = du_ref[...]
        sg = _sigmoid(a)
        dab_ref[:, :FF_SHARD] = (du_ * b * (sg * (1.0 + a * (1.0 - sg)))).astype(dab_ref.dtype)
        dab_ref[:, FF_SHARD:] = (du_ * (a * sg)).astype(dab_ref.dtype)

    wide = pl.BlockSpec((tr, 2 * FF_SHARD), lambda i, j: (i, j))
    return _pcall(
        kern, name=name, grid=(SEQ // tr, N_CHIPS),
        in_specs=[wide, pl.BlockSpec((tr, FF_SHARD), lambda i, j: (i, j))],
        out_specs=wide, out_shape=jax.ShapeDtypeStruct((SEQ, 2 * D_FF), BF16), args=(ab, du),
        semantics=("parallel", "parallel"), carried=carried)


CROSS_ROWS = 512


def _cross_fwd(qc, kvc, name):
    def kern(q_ref, k_ref, v_ref, o_ref):
        s = _dot_nt(q_ref[...], k_ref[...]) * ATT_SCALE
        m = jnp.max(s, axis=-1, keepdims=True)
        e = jnp.exp(s - m)
        p = e / jnp.sum(e, axis=-1, keepdims=True)
        o_ref[...] = _dot(p, v_ref[...]).astype(o_ref.dtype)

    qblk = pl.BlockSpec((CROSS_ROWS, HEAD_DIM), lambda h, i: (i, h))
    return pl.pallas_call(
        kern, name=name, grid=(CROSS_HEADS, SEQ // CROSS_ROWS),
        in_specs=[qblk, pl.BlockSpec((MEM_LEN, HEAD_DIM), lambda h, i: (0, h)),
                  pl.BlockSpec((MEM_LEN, HEAD_DIM), lambda h, i: (0, CROSS_HEADS + h))],
        out_specs=qblk, out_shape=jax.ShapeDtypeStruct((SEQ, CROSS_WIDTH), BF16),
        compiler_params=_cparams(("parallel", "parallel")),
    )(qc, kvc, kvc)


def _cross_bwd(qc, kvc, doc, name):
    def kern(q_ref, k_ref, v_ref, do_ref, dq_ref, dk_ref, dv_ref):
        q, k, v, do = q_ref[...], k_ref[...], v_ref[...], do_ref[...]
        s = _dot_nt(q, k) * ATT_SCALE
        m = jnp.max(s, axis=-1, keepdims=True)
        e = jnp.exp(s - m)
        p = e / jnp.sum(e, axis=-1, keepdims=True)
        dp = _dot_nt(do, v)
        ds = p * (dp - jnp.sum(dp * p, axis=-1, keepdims=True))
        dq_ref[...] = (_dot(ds, k) * ATT_SCALE).astype(dq_ref.dtype)
        dk = _dot_tn(ds, q) * ATT_SCALE
        dv = _dot_tn(p, do)

        @pl.when(pl.program_id(1) == 0)
        def _():
            dk_ref[...] = dk
            dv_ref[...] = dv

        @pl.when(pl.program_id(1) > 0)
        def _():
            dk_ref[...] += dk
            dv_ref[...] += dv

    qblk = pl.BlockSpec((CROSS_ROWS, HEAD_DIM), lambda h, i: (i, h))
    kblk = pl.BlockSpec((MEM_LEN, HEAD_DIM), lambda h, i: (0, h))
    dq, dk, dv = pl.pallas_call(
        kern, name=name, grid=(CROSS_HEADS, SEQ // CROSS_ROWS),
        in_specs=[qblk, kblk, pl.BlockSpec((MEM_LEN, HEAD_DIM), lambda h, i: (0, CROSS_HEADS + h)), qblk],
        out_specs=[qblk, kblk, kblk],
        out_shape=[jax.ShapeDtypeStruct((SEQ, CROSS_WIDTH), BF16),
                   jax.ShapeDtypeStruct((MEM_LEN, CROSS_WIDTH), F32),
                   jax.ShapeDtypeStruct((MEM_LEN, CROSS_WIDTH), F32)],
        compiler_params=_cparams(("parallel", "arbitrary")),
    )(qc, kvc, kvc, doc)
    return dq, jnp.concatenate([dk, dv], axis=1)


FULL_SPECS = {
    "w_in": ("col", D_MODEL, IN_WIDTH),
    "w_branch_a": ("col", ATT_OUT, D_MODEL),
    "w_branch_b": ("col", HG_WIDTH, D_MODEL),
    "w_out": ("row", D_MODEL, D_MODEL),
    "wq_cross": ("row", D_MODEL, CROSS_WIDTH),
    "wkv_cross": ("row", D_MODEL, 2 * CROSS_WIDTH),
    "wo_cross": ("col", CROSS_WIDTH, D_MODEL),
    "w13": ("col", D_MODEL, 2 * D_FF),
    "w2": ("row", D_FF, D_MODEL),
}
WEIGHT_PLACE = {
    "w_in": ("w_in", 0), "w_branch_a": ("w_branch_a", 0), "w_branch_b": ("w_branch_b", 0),
    "w_out": ("w_out", 0), "wq_cross": ("wq_cross", 0), "wkv_cross": ("wkv_cross", 0),
    "wo_cross": ("wo_cross", 0), "w1": ("w13", 0), "w3": ("w13", FF_SHARD), "w2": ("w2", 0),
}
BIG_WEIGHTS = tuple(WEIGHT_PLACE)
EW_BLOCK_ELEMS = 512 * 1024


def _position():
    return lax.axis_index("x"), lax.axis_index("y"), lax.axis_index("c")


def _other_chips(x, y):
    return [(1 - x, y), (x, 1 - y), (1 - x, 1 - y)]


def _half(ref, kind, h):
    r, c = ref.shape
    if kind == "col":
        return ref.at[pl.ds(h * (r // 2), r // 2), :]
    return ref.at[:, pl.ds(h * (c // 2), c // 2)]


def _shard_of(ref, kind, start, size):
    return ref.at[:, pl.ds(start, size)] if kind == "col" else ref.at[pl.ds(start, size), :]


def _rows_of(ref, r0, nrows):
    return ref if nrows is None else ref.at[pl.ds(r0, nrows), :]


def _half_shape(kind, rows, cols):
    return (rows // 2, cols) if kind == "col" else (rows, cols // 2)


def _slot_shape(spec):
    kind, rows, cols = spec
    hr, hc = _half_shape(kind, rows, cols)
    return (hr, hc // N_CHIPS) if kind == "col" else (hr // N_CHIPS, hc)


def _remote(src, dst, send_sem, recv_sem, device):
    return pltpu.make_async_remote_copy(src_ref=src, dst_ref=dst, send_sem=send_sem, recv_sem=recv_sem,
                                        device_id=device, device_id_type=MESH)


def _gather_ici_comm(fulls, jobs, specs):
    def piece(refs, job, chip, c):
        f, r0, nr = job
        kind, rows, cols = specs[f]
        stride = (cols if kind == "col" else rows) // N_CHIPS
        return _rows_of(_half(_shard_of(refs[f], kind, chip * stride, stride), kind, c), r0, nr)

    def start(refs, ss, rs):
        x, y, c = _position()
        j = 2 * x + y
        for q, job in enumerate(jobs):
            for p, (px, py) in enumerate(_other_chips(x, y)):
                _remote(piece(refs, job, j, c), piece(refs, job, j, c), ss.at[3 * q + p], rs.at[3 * q + p],
                        (px, py, c)).start()

    def finish(refs, ss, rs):
        x, y, c = _position()
        j = 2 * x + y
        for q, job in enumerate(jobs):
            for p, (px, py) in enumerate(_other_chips(x, y)):
                _remote(piece(refs, job, j, c), piece(refs, job, 2 * px + py, c), ss.at[3 * q + p],
                        rs.at[3 * q + p], (px, py, c)).wait_recv()
        for q, job in enumerate(jobs):
            for p, (px, py) in enumerate(_other_chips(x, y)):
                _remote(piece(refs, job, j, c), piece(refs, job, j, c), ss.at[3 * q + p], rs.at[3 * q + p],
                        (px, py, c)).wait_send()

    names = list(dict.fromkeys(job[0] for job in jobs))
    return _Carried({f: fulls[f] for f in names}, {}, 3 * len(jobs), start, finish)


def _gather_ring_comm(fulls, f, r0, nr, phase, specs):
    kind, _, cols = specs[f]
    assert kind == "col" and nr % 32 == 0
    stride = cols // N_CHIPS
    half = nr // 2

    def rows(refs, chip, c, lo, n):
        return _rows_of(_half(_shard_of(refs[f], kind, chip * stride, stride), kind, c), r0 + lo, n)

    def copies(refs, ss, rs):
        x, y, c = _position()
        me, nx, ny, dg = 2 * x + y, 2 * (1 - x) + y, 2 * x + (1 - y), 2 * (1 - x) + (1 - y)
        to_x, to_y = (1 - x, y, c), (x, 1 - y, c)
        if phase == "a":
            mine = rows(refs, me, c, 0, nr)
            return [(_remote(mine, mine, ss.at[0], rs.at[0], to_x), rows(refs, nx, c, 0, nr)),
                    (_remote(mine, mine, ss.at[1], rs.at[1], to_y), rows(refs, ny, c, 0, nr))]
        up, low = rows(refs, ny, c, half, half), rows(refs, nx, c, 0, half)
        return [(_remote(up, up, ss.at[0], rs.at[0], to_x), rows(refs, dg, c, half, half)),
                (_remote(low, low, ss.at[1], rs.at[1], to_y), rows(refs, dg, c, 0, half))]

    def start(refs, ss, rs):
        for cp, _ in copies(refs, ss, rs):
            cp.start()

    def finish(refs, ss, rs):
        x, y, c = _position()
        mine = copies(refs, ss, rs)
        for i, (_, landing) in enumerate(mine):
            _remote(landing, landing, ss.at[i], rs.at[i], (x, y, c)).wait_recv()
        for cp, _ in mine:
            cp.wait_send()

    return _Carried({f: fulls[f]}, {}, 2, start, finish)


def _gather_d2d_comm(fulls, jobs, specs):
    def rect(refs, job, h):
        f, r0, nr = job
        assert nr is None or specs[f][0] == "col"
        return _rows_of(_half(refs[f], specs[f][0], h), r0, nr)

    def start(refs, ss, rs):
        x, y, c = _position()
        for q, job in enumerate(jobs):
            _remote(rect(refs, job, c), rect(refs, job, c), ss.at[q], rs.at[q], (x, y, 1 - c)).start()

    def finish(refs, ss, rs):
        x, y, c = _position()
        for q, job in enumerate(jobs):
            _remote(rect(refs, job, 1 - c), rect(refs, job, 1 - c), ss.at[q], rs.at[q], (x, y, 1 - c)).wait_recv()
        for q, job in enumerate(jobs):
            _remote(rect(refs, job, c), rect(refs, job, c), ss.at[q], rs.at[q], (x, y, 1 - c)).wait_send()

    names = list(dict.fromkeys(job[0] for job in jobs))
    return _Carried({f: fulls[f] for f in names}, {}, len(jobs), start, finish)


def _pairx_comm(grads, names, specs):
    def copies(refs, ss, rs):
        x, y, c = _position()
        return [_remote(_half(refs[("g", f)], specs[f][0], 1 - c), refs[("r", f)], ss.at[i], rs.at[i], (x, y, 1 - c))
                for i, f in enumerate(names)]

    def start(refs, ss, rs):
        for cp in copies(refs, ss, rs):
            cp.start()

    def finish(refs, ss, rs):
        for cp in copies(refs, ss, rs):
            cp.wait_recv()
        for cp in copies(refs, ss, rs):
            cp.wait_send()

    fresh = {("r", f): jax.ShapeDtypeStruct(_half_shape(*specs[f]), BF16) for f in names}
    return _Carried({}, fresh, len(names), start, finish, reads={("g", f): grads[f] for f in names})


def _chipx_comm(pair_sums, slots, jobs, specs):
    def copies(refs, ss, rs):
        x, y, c = _position()
        out = []
        for q, (f, r0, nr) in enumerate(jobs):
            kind = specs[f][0]
            width = _slot_shape(specs[f])[1 if kind == "col" else 0]
            for p, (px, py) in enumerate(_other_chips(x, y)):
                src = _rows_of(_shard_of(refs[("p", f)], kind, (2 * px + py) * width, width), r0, nr)
                dst = _rows_of(refs[("s", f)].at[p], r0, nr)
                out.append(_remote(src, dst, ss.at[3 * q + p], rs.at[3 * q + p], (px, py, c)))
        return out

    def start(refs, ss, rs):
        for cp in copies(refs, ss, rs):
            cp.start()

    def finish(refs, ss, rs):
        for cp in copies(refs, ss, rs):
            cp.wait_recv()
        for cp in copies(refs, ss, rs):
            cp.wait_send()

    names = list(dict.fromkeys(job[0] for job in jobs))
    arrays = {("p", f): pair_sums[f] for f in names}
    arrays.update({("s", f): slots[f] for f in names})
    return _Carried(arrays, {}, 3 * len(jobs), start, finish)


def _share_comm(grads, wnames, specs, place):
    def start(refs, ss, rs):
        x, y, c = _position()
        for i, w in enumerate(wnames):
            kind = specs[place[w][0]][0]
            _remote(_half(refs[w], kind, c), _half(refs[w], kind, c), ss.at[i], rs.at[i], (x, y, 1 - c)).start()

    def finish(refs, ss, rs):
        x, y, c = _position()
        for i, w in enumerate(wnames):
            kind = specs[place[w][0]][0]
            _remote(_half(refs[w], kind, 1 - c), _half(refs[w], kind, 1 - c), ss.at[i], rs.at[i],
                    (x, y, 1 - c)).wait_recv()
        for i, w in enumerate(wnames):
            kind = specs[place[w][0]][0]
            _remote(_half(refs[w], kind, c), _half(refs[w], kind, c), ss.at[i], rs.at[i], (x, y, 1 - c)).wait_send()

    return _Carried({w: grads[w] for w in wnames}, {}, len(wnames), start, finish)


def _gather_rows(v, name="gather_small"):
    shape = v.shape

    def body(v_ref, out_ref, send_sem, recv_sem, loc_sem):
        x, y, c = _position()
        me = 4 * x + 2 * y + c
        flips = [(fx, fy, fc) for fx in (0, 1) for fy in (0, 1) for fc in (0, 1)][1:]

        def peer(fl):
            return tuple(1 - a if f else a for a, f in zip((x, y, c), fl))

        loc = pltpu.make_async_copy(v_ref, out_ref.at[me], loc_sem)
        loc.start()
        sends = []
        for i, fl in enumerate(flips):
            cp = _remote(v_ref, out_ref.at[me], send_sem.at[i], recv_sem.at[i], peer(fl))
            cp.start()
            sends.append(cp)
        for i, fl in enumerate(flips):
            px, py, pc = peer(fl)
            _remote(v_ref, out_ref.at[4 * px + 2 * py + pc], send_sem.at[i], recv_sem.at[i], peer(fl)).wait_recv()
        for cp in sends:
            cp.wait_send()
        loc.wait()

    return pl.pallas_call(
        body, name=name, in_specs=[ANY], out_specs=ANY,
        out_shape=jax.ShapeDtypeStruct((N_DEV,) + shape, F32),
        scratch_shapes=[pltpu.SemaphoreType.DMA((N_DEV - 1,)), pltpu.SemaphoreType.DMA((N_DEV - 1,)),
                        pltpu.SemaphoreType.DMA],
    )(v)


def _ew_block(rows, cols, elems=EW_BLOCK_ELEMS):
    tc = cols if cols <= 4096 else _div(cols, 2048, LANE)
    tr = _div(rows, max(16, elems // tc), 16)
    return tr, tc


def _mesh_scalars():
    x, y, c = _position()
    return jnp.stack([c, 2 * x + y]).astype(jnp.int32)


def _grid_spec(grid, in_specs, out_specs):
    return pltpu.PrefetchScalarGridSpec(num_scalar_prefetch=1, grid=grid, in_specs=in_specs, out_specs=out_specs)


def _cast_into_full(parts, fname, pos, specs, place, name, token=None):
    kind, rows, cols = specs[fname]
    ws = [w for w in place if place[w][0] == fname]
    if kind == "col":
        stride = cols // N_CHIPS
        hr = rows // 2
        tr = _div(hr, max(16, EW_BLOCK_ELEMS // stride), 16)
        nrb = hr // tr
        in_specs = [pl.BlockSpec((tr, parts[w].shape[1]), lambda i, pos_ref: (i + pos_ref[0] * nrb, 0)) for w in ws]
        out_spec = pl.BlockSpec((tr, stride), lambda i, pos_ref: (i + pos_ref[0] * nrb, pos_ref[1]))
    else:
        stride = rows // N_CHIPS
        hc = cols // 2
        tr = _div(stride, max(16, EW_BLOCK_ELEMS // hc), 16)
        nrb = stride // tr
        in_specs = [pl.BlockSpec((tr, hc), lambda i, pos_ref: (i, pos_ref[0])) for w in ws]
        out_spec = pl.BlockSpec((tr, hc), lambda i, pos_ref: (i + pos_ref[1] * nrb, pos_ref[0]))

    def kern(pos_ref, *refs):
        o_ref = refs[-1]
        for w, r in zip(ws, refs[:len(ws)]):
            off = place[w][1] if kind == "col" else 0
            o_ref[:, off:off + r.shape[1]] = r[...].astype(o_ref.dtype)

    tokens = [] if token is None else [token]
    in_specs = in_specs + [pl.BlockSpec(TOKEN_SHAPE, lambda i, pos_ref: (0, 0))] * len(tokens)
    return pl.pallas_call(
        kern, name=name, grid_spec=_grid_spec((nrb,), in_specs, out_spec),
        out_shape=jax.ShapeDtypeStruct((rows, cols), BF16),
        compiler_params=_cparams(("parallel",)),
    )(pos, *[parts[w] for w in ws], *tokens)


def _pair_sum(grad, recv, pos, spec, name):
    kind, rows, cols = spec
    hr, hc = _half_shape(kind, rows, cols)
    tr, tc = _ew_block(hr, hc, 2 * EW_BLOCK_ELEMS)
    nrb, ncb = hr // tr, hc // tc
    blk = pl.BlockSpec((tr, tc), lambda i, jj, pos_ref: (i, jj))
    if kind == "col":
        mine = pl.BlockSpec((tr, tc), lambda i, jj, pos_ref: (i + pos_ref[0] * nrb, jj))
    else:
        mine = pl.BlockSpec((tr, tc), lambda i, jj, pos_ref: (i, jj + pos_ref[0] * ncb))

    def kern(pos_ref, g_ref, r_ref, o_ref, slots_ref):
        o_ref[...] = (g_ref[...].astype(F32) + r_ref[...].astype(F32)).astype(o_ref.dtype)

    return pl.pallas_call(
        kern, name=name, grid_spec=_grid_spec((nrb, ncb), [mine, blk], [blk, ANY]),
        out_shape=[jax.ShapeDtypeStruct((hr, hc), BF16),
                   jax.ShapeDtypeStruct((N_CHIPS - 1,) + _slot_shape(spec), BF16)],
        compiler_params=_cparams(("parallel", "parallel")),
    )(pos, grad, recv)


def _chip_sum(pair_sum, slots, pos, fname, shard_shapes, specs, place, name):
    kind, rows, cols = specs[fname]
    sr, sc = _slot_shape(specs[fname])
    ws = [w for w in place if place[w][0] == fname]
    n_slots = N_CHIPS - 1
    tr = _div(sr, max(16, EW_BLOCK_ELEMS // sc), 16)
    nrb = sr // tr
    slot = pl.BlockSpec((n_slots, tr, sc), lambda i, pos_ref: (0, i, 0))
    if kind == "col":
        own = pl.BlockSpec((tr, sc), lambda i, pos_ref: (i, pos_ref[1]))
        out_specs = [pl.BlockSpec((tr, shard_shapes[w][1]), lambda i, pos_ref: (i + pos_ref[0] * nrb, 0)) for w in ws]
    else:
        own = pl.BlockSpec((tr, sc), lambda i, pos_ref: (i + pos_ref[1] * nrb, 0))
        out_specs = [pl.BlockSpec((tr, sc), lambda i, pos_ref: (i, pos_ref[0])) for w in ws]

    def kern(pos_ref, own_ref, slot_ref, *out_refs):
        tot = own_ref[...].astype(F32)
        for s in range(n_slots):
            tot = tot + slot_ref[s].astype(F32)
        for w, o_ref in zip(ws, out_refs):
            off = place[w][1] if kind == "col" else 0
            o_ref[...] = tot[:, off:off + o_ref.shape[1]]

    outs = pl.pallas_call(
        kern, name=name, grid_spec=_grid_spec((nrb,), [own, slot], out_specs),
        out_shape=[jax.ShapeDtypeStruct(shard_shapes[w], F32) for w in ws],
        compiler_params=_cparams(("parallel",)),
    )(pos, pair_sum, slots)
    return dict(zip(ws, outs))


def _adam_math(w, g, m, v):
    m2 = ADAM_B1 * m + (1.0 - ADAM_B1) * g
    v2 = ADAM_B2 * v + (1.0 - ADAM_B2) * (g * g)
    m_hat = m2 / (1.0 - ADAM_B1 ** ADAM_STEP)
    v_hat = v2 / (1.0 - ADAM_B2 ** ADAM_STEP)
    delta = -ADAM_LR * (m_hat / (jnp.sqrt(v_hat) + ADAM_EPS) + ADAM_WD * w)
    return delta, m2, v2


def _adamw(w, g, m, v, name, carried=()):
    rows, cols = w.shape
    tr, tc = _ew_block(rows, cols)

    def kern(w_ref, g_ref, m_ref, v_ref, d_ref, m2_ref, v2_ref, g_out_ref):
        g_ = g_ref[...]
        d_ref[...], m2_ref[...], v2_ref[...] = _adam_math(w_ref[...], g_, m_ref[...], v_ref[...])
        g_out_ref[...] = g_

    blk = pl.BlockSpec((tr, tc), lambda i, j: (i, j))
    return _pcall(
        kern, name=name, grid=(rows // tr, cols // tc), in_specs=[blk] * 4, out_specs=[blk] * 4,
        out_shape=[jax.ShapeDtypeStruct((rows, cols), F32)] * 4, args=(w, g, m, v),
        semantics=("parallel", "parallel"), carried=carried)


SMALL_ROWS = ("ln_mix_w", "ln_cross_w", "ln_mem_w", "ln_ffn_w", "ln_final_w")
ROW_HG_NORM, ROW_LB0, ROW_LB1 = 5, 6, 7
LOSS_LANE0 = HEAD_DIM


def _pack_small(vals):
    rows = [vals[n].reshape(1, D_MODEL) for n in SMALL_ROWS]
    pad = lambda a: jnp.pad(a, ((0, 0), (0, D_MODEL - a.shape[1])))
    rows.append(pad(vals["hg_norm_w"].reshape(1, HEAD_DIM)))
    rows.append(pad(vals["hg_lower_bounds"].reshape(2, HG_WIDTH)))
    return jnp.concatenate(rows, axis=0)


def _small_update(gathered, w, m, v, name="small_update"):
    def kern(g_ref, w_ref, m_ref, v_ref, grad_ref, d_ref, m2_ref, v2_ref, loss_ref):
        tot = g_ref[0]
        for s in range(1, N_DEV):
            tot = tot + g_ref[s]
        wv = w_ref[...]
        row = lax.broadcasted_iota(jnp.int32, (8, D_MODEL), 0)
        lane = lax.broadcasted_iota(jnp.int32, (8, D_MODEL), 1)
        l0, l1 = wv[ROW_LB0:ROW_LB0 + 1], wv[ROW_LB1:ROW_LB1 + 1]
        mx = jnp.maximum(l0, l1)
        e0, e1 = jnp.exp(l0 - mx), jnp.exp(l1 - mx)
        p0 = e0 / (e0 + e1)
        dlog = tot[ROW_LB0:ROW_LB0 + 1] * p0 * (1.0 - p0)
        tot = jnp.where(row == ROW_HG_NORM, tot + tot[ROW_LB1:ROW_LB1 + 1], tot)
        grad = jnp.where(row == ROW_LB0, dlog, jnp.where(row == ROW_LB1, -dlog, tot))
        grad = jnp.where((row == ROW_HG_NORM) & (lane >= HEAD_DIM), 0.0, grad)
        grad = jnp.where((row >= ROW_LB0) & (lane >= HG_WIDTH), 0.0, grad)
        grad_ref[...] = grad
        d_ref[...], m2_ref[...], v2_ref[...] = _adam_math(wv, grad, m_ref[...], v_ref[...])
        loss_ref[...] = tot[ROW_HG_NORM:ROW_HG_NORM + 1, LOSS_LANE0:LOSS_LANE0 + LANE]

    full = pl.BlockSpec((8, D_MODEL), lambda: (0, 0))
    return pl.pallas_call(
        kern, name=name,
        in_specs=[pl.BlockSpec((N_DEV, 8, D_MODEL), lambda: (0, 0, 0)), full, full, full],
        out_specs=[full, full, full, full, pl.BlockSpec((1, LANE), lambda: (0, 0))],
        out_shape=[jax.ShapeDtypeStruct((8, D_MODEL), F32)] * 4 + [jax.ShapeDtypeStruct((1, LANE), F32)],
        compiler_params=_cparams(),
    )(gathered, w, m, v)


def _unpack_small(p, shapes):
    out = {n: p[i].reshape(shapes[n]) for i, n in enumerate(SMALL_ROWS)}
    out["hg_norm_w"] = p[ROW_HG_NORM, :HEAD_DIM].reshape(shapes["hg_norm_w"])
    out["hg_lower_bounds"] = p[ROW_LB0:ROW_LB1 + 1, :HG_WIDTH].reshape(shapes["hg_lower_bounds"])
    return out


WHOLE = lambda f: (f, 0, None)
MID_MATRICES = ("w_branch_a", "w_branch_b", "w_out", "wq_cross", "wkv_cross", "wo_cross")
MID_WEIGHTS = MID_MATRICES
W_IN_PIECES = [("w_in", r0, 512) for r0 in range(0, D_MODEL // 2, 512)]
GATHER_GROUPS = [("mid", [WHOLE(f) for f in MID_MATRICES]), ("w13", [WHOLE("w13")]), ("w2", [WHOLE("w2")])]
OTHER_WEIGHTS = ["w1", "w3", "w2"] + list(MID_WEIGHTS)
BEFORE = {
    "hgrn_fwd": [("wait", "mid")],
    "gate_fwd": [("wait", "w13")],
    "mm_w13": [("wait", "w2")],
    "mm_dh": [("wait", "rs_w2"), ("chip_sum", "w2"), ("wait", "rs_w13"), ("chip_sum", "w13"), ("wait", "rs_mid")]
    + [("chip_sum", f) for f in MID_MATRICES],
}
CARRY = {
    "hgrn_fwd": [("d2d", [WHOLE(f) for f in MID_MATRICES])],
    "gate_fwd": [("d2d", [WHOLE("w13")])],
    "mm_w13": [("d2d", [WHOLE("w2")])],
    "swiglu_bwd": [("pairx", ["w2"])],
    "mm_dhf": [("pairx", ["w13"])],
    "attn_merge_bwd": [("pairx", list(MID_MATRICES))],
    "mm_dh": [("share", OTHER_WEIGHTS)],
}
AFTER = {
    "swiglu_bwd": [("pair_sum", "w2"), ("start", "rs_w2", [WHOLE("w2")])],
    "mm_dhf": [("pair_sum", "w13"), ("start", "rs_w13", [WHOLE("w13")])],
    "attn_merge_bwd": [("pair_sum", f) for f in MID_MATRICES] + [("start", "rs_mid", [WHOLE(f) for f in MID_MATRICES])],
    "mm_dwin": [("run", ("pairx", ["w_in"]), "rs_pair_exchange_w_in"), ("pair_sum", "w_in"),
                ("start", "rs_w_in", [WHOLE("w_in")])],
}
FINISH = [
    ("adamw", OTHER_WEIGHTS), ("wait", "rs_w_in"), ("small",), ("chip_sum", "w_in"),
    ("run", ("share", ["w_in"]), "rs_sibling_share_w_in"), ("adamw", ["w_in"]),
]


class _Net:
    def __init__(self, full, pos=None, shard_shapes=None, comm=True, specs=FULL_SPECS, place=WEIGHT_PLACE):
        self.full, self.pos, self.shard_shapes, self.comm = dict(full), pos, shard_shapes, comm
        self.specs, self.place = specs, place
        self.gw, self.recv, self.psum, self.slots, self.grads = {}, {}, {}, {}, {}
        self.pending, self.token, self.last = {}, None, None

    def _make(self, kind, arg):
        if kind == "gather":
            return _gather_ici_comm(self.full, arg, self.specs)
        if kind == "ring":
            return _gather_ring_comm(self.full, *arg, self.specs)
        if kind == "d2d":
            return _gather_d2d_comm(self.full, arg, self.specs)
        if kind == "pairx":
            return _pairx_comm(self.gw, arg, self.specs)
        if kind == "chipx":
            return _chipx_comm(self.psum, self.slots, arg, self.specs)
        assert kind == "share"
        return _share_comm(self.grads, arg, self.specs, self.place)

    def _store(self, kind, res):
        if kind in ("gather", "ring", "d2d"):
            self.full.update(res)
        elif kind == "pairx":
            for (tag, f), a in res.items():
                (self.gw if tag == "g" else self.recv)[f] = a
        elif kind == "chipx":
            for (tag, f), a in res.items():
                (self.psum if tag == "p" else self.slots)[f] = a
        else:
            self.grads.update(res)

    def run_comm(self, item, name):
        kind, arg = item
        self._store(kind, _run_comm([self._make(kind, arg)], name)[0])

    @staticmethod
    def _others(after, items):
        own = [a for cm in items for a in cm.arrays.values()]
        return [a for a in after if a is not None and all(a is not o for o in own)]

    def start(self, groups, kind, name):
        items = [self._make(kind, jobs) for _, jobs in groups]
        after = self._others([self.last], items)
        res, sems, token = _split_start(items, name, after=after[0] if after else None)
        for (group, jobs), r, s in zip(groups, res, sems):
            self._store(kind, r)
            self.pending[group] = (kind, jobs, s)
        self.token = self.last = token

    def wait(self, group, after=()):
        kind, jobs, sems = self.pending.pop(group)
        item = self._make(kind, jobs)
        res = _split_wait([item], [sems], self._others([self.last, *after], [item]), f"wait_{group}")[0]
        self._store(kind, res)

    def step(self, step):
        if step[0] == "wait":
            self.wait(step[1])
        elif step[0] == "start":
            self.start([(step[1], step[2])], "chipx", f"start_{step[1]}")
        elif step[0] == "pair_sum":
            f = step[1]
            self.psum[f], self.slots[f] = _pair_sum(self.gw[f], self.recv[f], self.pos, self.specs[f],
                                                    f"rs_pair_sum_{f}")
        elif step[0] == "chip_sum":
            f = step[1]
            self.grads.update(_chip_sum(self.psum[f], self.slots[f], self.pos, f, self.shard_shapes,
                                        self.specs, self.place, f"rs_chip_sum_{f}"))
        else:
            assert step[0] == "run"
            self.run_comm(step[1], step[2])

    def call(self, fn, name, *args, grad_of=None, **kw):
        for step in (BEFORE.get(name, []) if self.comm else []):
            self.step(step)
        items = CARRY.get(name, []) if self.comm else []
        carried = [self._make(k, a) for k, a in items]
        if self.token is not None:
            carried.append(_Token(self.token))
            self.token = None
        out, res = fn(*args, name=name, carried=carried, **kw)
        if grad_of is not None:
            self.gw[grad_of] = out
        self.last = jax.tree.leaves(out)[0]
        for (kind, _), r in zip(items, res):
            self._store(kind, r)
        for step in (AFTER.get(name, []) if self.comm else []):
            self.step(step)
        return out


def _local_step(net, x, h, mem, target, small):
    full, call = net.full, net.call
    proj = call(_mm, "mm_proj", h, full["w_in"], mode="nn", out_dtype=F32)
    att = [call(_attn_fwd, f"attn_fwd_g{g}", proj, g) for g in range(3)]
    outs, lses = [a[0] for a in att], [a[1] for a in att]
    o_att = _attn_merge_fwd(outs, lses, "attn_merge")
    oraw, o_hg, states = call(_hg_fwd, "hgrn_fwd", proj, small["hg_lower_bounds"], small["hg_norm_w"])
    ya = call(_mm, "mm_branch_a", o_att, full["w_branch_a"], mode="nn", out_dtype=F32)
    yb = call(_mm, "mm_branch_b", o_hg, full["w_branch_b"], mode="nn", out_dtype=F32)
    merged = call(_gate_fwd, "gate_fwd", proj, ya, yb)
    x1 = call(_mm, "mm_out", merged, full["w_out"], mode="nn", out_dtype=F32, res=x)

    hc = _rms_fwd(x1, small["ln_cross_w"], "rms_cross")
    mn = _rms_fwd(mem, small["ln_mem_w"], "rms_mem")
    qc = call(_mm, "mm_q", hc, full["wq_cross"], mode="nn", out_dtype=F32)
    kvc = call(_mm, "mm_kv", mn, full["wkv_cross"], mode="nn", out_dtype=F32)
    oc = _cross_fwd(qc, kvc, "cross_fwd")
    x2 = call(_mm, "mm_o", oc, full["wo_cross"], mode="nn", out_dtype=F32, res=x1)

    hf = _rms_fwd(x2, small["ln_ffn_w"], "rms_ffn")
    ab = call(_mm, "mm_w13", hf, full["w13"], mode="nn", out_dtype=F32)
    u = _swiglu_fwd(ab, "swiglu_fwd")
    x3 = call(_mm, "mm_w2", u, full["w2"], mode="nn", out_dtype=F32, res=x2)

    dx3, dg_final, loss = _loss_head(x3, small["ln_final_w"], target, "loss_head")

    gs = {"ln_final_w": dg_final}
    du = call(_mm, "mm_du", dx3, full["w2"], mode="nt", out_dtype=F32)
    call(_mm, "mm_dw2", u, dx3, mode="tn", out_dtype=BF16, grad_of="w2")
    dab = call(_swiglu_bwd, "swiglu_bwd", ab, du)
    call(_mm, "mm_dw13", hf, dab, mode="tn", out_dtype=BF16, grad_of="w13")
    dhf = call(_mm, "mm_dhf", dab, full["w13"], mode="nt", out_dtype=F32)
    dx2, gs["ln_ffn_w"] = _rms_bwd(x2, small["ln_ffn_w"], dhf, dx3, "rms_ffn_bwd")
    doc = call(_mm, "mm_doc", dx2, full["wo_cross"], mode="nt", out_dtype=BF16)
    call(_mm, "mm_dwo", oc, dx2, mode="tn", out_dtype=BF16, grad_of="wo_cross")
    dqc, dkvc = _cross_bwd(qc, kvc, doc, "cross_bwd")
    call(_mm, "mm_dwq", hc, dqc, mode="tn", out_dtype=BF16, grad_of="wq_cross")
    dhc = call(_mm, "mm_dhc", dqc, full["wq_cross"], mode="nt", out_dtype=F32)
    call(_mm, "mm_dwkv", mn, dkvc, mode="tn", out_dtype=BF16, grad_of="wkv_cross")
    dmn = call(_mm, "mm_dmn", dkvc, full["wkv_cross"], mode="nt", out_dtype=F32)
    _, gs["ln_mem_w"] = _rms_bwd(mem, small["ln_mem_w"], dmn, None, "rms_mem_bwd")
    dx1, gs["ln_cross_w"] = _rms_bwd(x1, small["ln_cross_w"], dhc, dx2, "rms_cross_bwd")
    dmerged = call(_mm, "mm_dmerged", dx1, full["w_out"], mode="nt", out_dtype=F32)
    call(_mm, "mm_dwout", merged, dx1, mode="tn", out_dtype=BF16, grad_of="w_out")
    dya, dyb, dga, dgb = call(_gate_bwd, "gate_bwd", proj, ya, yb, dmerged)
    call(_mm, "mm_dwa", o_att, dya, mode="tn", out_dtype=BF16, grad_of="w_branch_a")
    do_att = call(_mm, "mm_doatt", dya, full["w_branch_a"], mode="nt", out_dtype=F32)
    call(_mm, "mm_dwb", o_hg, dyb, mode="tn", out_dtype=BF16, grad_of="w_branch_b")
    do_hg = call(_mm, "mm_dohg", dyb, full["w_branch_b"], mode="nt", out_dtype=F32)
    dqh, dfh, dih, dgh, dlb, gs["hg_norm_w"] = call(
        _hg_bwd, "hgrn_bwd", proj, small["hg_lower_bounds"], small["hg_norm_w"], oraw, states, do_hg)
    gs["hg_lb"] = dlb
    do_gs, dl_gs = call(_attn_merge_bwd, "attn_merge_bwd", outs, lses, do_att)
    dqs, dks, dvs = zip(*[call(_attn_bwd, f"attn_bwd_g{g}", proj, g, lses[g], do_gs[g], dl_gs[g]) for g in range(3)])
    dproj = jnp.concatenate([*dqs, *dks, *dvs, dqh, dfh, dih, dgh, dga, dgb], axis=1)
    call(_mm, "mm_dwin", h, dproj, mode="tn", out_dtype=BF16, grad_of="w_in")
    dh = call(_mm, "mm_dh", dproj, full["w_in"], mode="nt", out_dtype=F32)
    dx, gs["ln_mix_w"] = _rms_bwd(x, small["ln_mix_w"], dh, dx1, "rms_mix_bwd")
    return loss, dx, gs


WEIGHT_ORDER = ("ln_mix_w", "w_in", "hg_norm_w", "hg_lower_bounds", "w_branch_a", "w_branch_b", "w_out",
                "ln_cross_w", "ln_mem_w", "wq_cross", "wkv_cross", "wo_cross", "ln_ffn_w", "w1", "w3", "w2",
                "ln_final_w")


def kernel(x, mem, ln_mix_w, w_in, hg_norm_w, hg_lower_bounds, w_branch_a, w_branch_b, w_out, ln_cross_w, ln_mem_w, wq_cross, wkv_cross, wo_cross, ln_ffn_w, w1, w3, w2, ln_final_w, loss_target, m_ln_mix_w, m_w_in, m_hg_norm_w, m_hg_lower_bounds, m_w_branch_a, m_w_branch_b, m_w_out, m_ln_cross_w, m_ln_mem_w, m_wq_cross, m_wkv_cross, m_wo_cross, m_ln_ffn_w, m_w1, m_w3, m_w2, m_ln_final_w, v_ln_mix_w, v_w_in, v_hg_norm_w, v_hg_lower_bounds, v_w_branch_a, v_w_branch_b, v_w_out, v_ln_cross_w, v_ln_mem_w, v_wq_cross, v_wkv_cross, v_wo_cross, v_ln_ffn_w, v_w1, v_w3, v_w2, v_ln_final_w):
    args = dict(locals())
    w = {n: args[n] for n in WEIGHT_ORDER}
    m = {n: args["m_" + n] for n in WEIGHT_ORDER}
    v = {n: args["v_" + n] for n in WEIGHT_ORDER}
    shapes = {n: w[n].shape for n in WEIGHT_ORDER}
    mat = lambda a: a.reshape(a.shape[-2:])
    shard_shapes = {n: shapes[n][-2:] for n in BIG_WEIGHTS}

    pos = _mesh_scalars()

    def cast(f, token=None):
        return _cast_into_full({n: mat(w[n]) for n in BIG_WEIGHTS if WEIGHT_PLACE[n][0] == f}, f, pos,
                               FULL_SPECS, WEIGHT_PLACE, f"cast_{f}", token)

    net = _Net({"w_in": cast("w_in")}, pos, shard_shapes)
    net.start([(f"ring_a{i}", (*job, "a")) for i, job in enumerate(W_IN_PIECES)], "ring", "gather_start_w_in")
    net.full.update({f: cast(f, net.token) for f in FULL_SPECS if f != "w_in"})
    for i, job in enumerate(W_IN_PIECES):
        net.wait(f"ring_a{i}")
        net.start([(f"ring_b{i}", (*job, "b"))], "ring", f"gather_pass_on_w_in{i}")
    net.start(GATHER_GROUPS, "gather", "gather_start_rest")
    small = {n: w[n].reshape(1, -1) for n in SMALL_ROWS}
    small["hg_norm_w"] = w["hg_norm_w"].reshape(1, HEAD_DIM)
    small["hg_lower_bounds"] = w["hg_lower_bounds"]
    x2d = x.reshape(SEQ, D_MODEL)
    h = _rms_fwd(x2d, small["ln_mix_w"], "rms_mix")
    for i, job in enumerate(W_IN_PIECES):
        net.wait(f"ring_b{i}")
        net.run_comm(("d2d", [job]), f"gather_hand_over_w_in{i}")
    loss, dx, gs = _local_step(net, x2d, h, mem.reshape(MEM_LEN, D_MODEL), loss_target.reshape(SEQ, D_MODEL), small)

    out_g, out_d, out_m, out_v = {}, {}, {}, {}
    net.last = dx
    for step in FINISH:
        if step[0] == "adamw":
            for n in step[1]:
                out_d[n], out_m[n], out_v[n], out_g[n] = net.call(_adamw, f"adamw_{n}", mat(w[n]), net.grads[n],
                                                                  mat(m[n]), mat(v[n]))
        elif step[0] == "wait":
            net.wait(step[1], after=list(out_d.values()))
        elif step[0] == "small":
            pad = lambda a: jnp.pad(a, ((0, 0), (0, D_MODEL - a.shape[1])))
            part = jnp.concatenate(
                [gs[n] for n in SMALL_ROWS]
                + [pad(jnp.concatenate([gs["hg_norm_w"][0], loss], axis=1)), pad(gs["hg_lb"]),
                   pad(gs["hg_norm_w"][1])], axis=0)
            part, net.psum["w_in"] = lax.optimization_barrier((part, net.psum["w_in"]))
            sg, sd, sm, sv, loss_tot = _small_update(_gather_rows(part), _pack_small(w), _pack_small(m),
                                                     _pack_small(v))
            for dst, packed in ((out_g, sg), (out_d, sd), (out_m, sm), (out_v, sv)):
                dst.update(_unpack_small(packed, shapes))
        else:
            net.step(step)

    result = [loss_tot[0, 0], dx.reshape(x.shape)]
    for group in (out_g, out_d, out_m, out_v):
        result += [group[n].reshape(shapes[n]) for n in WEIGHT_ORDER]
    return tuple(result)
```

```python
import math

import jax
import jax.numpy as jnp
from jax import lax
from jax.experimental import pallas as pl
from jax.experimental.pallas import tpu as pltpu

F32 = jnp.float32
BF16 = jnp.bfloat16
MESH = pl.DeviceIdType.MESH

D_MODEL = 2048
SEQ = 2048
HEAD_DIM = 128
MEM_LEN = 256
ATT_GROUPS = ((128, 1), (512, 4), (2048, 16))
ATT_HEADS = 4
ATT_WIDTH = 3 * ATT_HEADS * HEAD_DIM
ATT_OUT = ATT_HEADS * HEAD_DIM
ATT_BLOCK = 128
HG_HEADS = 8
HG_WIDTH = HG_HEADS * HEAD_DIM
HG_CHUNK = 64
IN_WIDTH = 3 * ATT_WIDTH + 4 * HG_WIDTH + 2 * D_MODEL
CROSS_HEADS = 4
CROSS_WIDTH = CROSS_HEADS * HEAD_DIM
D_FF = 5632
RMS_EPS = 1e-6
ADAM_LR = 0.001
ADAM_B1 = 0.9
ADAM_B2 = 0.999
ADAM_EPS = 1e-08
ADAM_WD = 0.01
ADAM_STEP = 10
N_CHIPS = 4
N_DEV = 8

VMEM_LIMIT_BYTES = 56 * 1024 * 1024
LANE = 128
MXU_WIDTH = 256
MM_TILE_CAP = 1536
TRANSPOSE_CHUNK = 512
ANY = pl.BlockSpec(memory_space=pl.ANY)


def _cparams(sem=None):
    return pltpu.CompilerParams(dimension_semantics=sem, vmem_limit_bytes=VMEM_LIMIT_BYTES)


def _div(n, cap, mult):
    best = None
    for d in range(mult, min(n, cap) + 1, mult):
        if n % d == 0:
            best = d
    assert best is not None, (n, cap, mult)
    return best


def _sigmoid(x):
    return 1.0 / (1.0 + jnp.exp(-x))


def _dot(a, b):
    return jnp.dot(a.astype(BF16), b.astype(BF16), preferred_element_type=F32)


def _dot_nt(a, b):
    return lax.dot_general(a.astype(BF16), b.astype(BF16), (((1,), (1,)), ((), ())),
                           preferred_element_type=F32)


def _dot_tn(a, b):
    return jnp.dot(a.astype(F32).T.astype(BF16), b.astype(BF16), preferred_element_type=F32)


def _dot_exact(a, b):
    return jnp.dot(a, b, precision=lax.Precision.HIGHEST, preferred_element_type=F32)


class _Carried:
    def __init__(self, arrays, fresh, n_sems, start, finish, mid=None, reads=None):
        self.arrays, self.fresh, self.n_sems, self.reads = arrays, fresh, n_sems, reads or {}
        self.start, self.mid, self.finish = start, mid, finish


class _Token:
    def __init__(self, array):
        self.array = array


TOKEN_SHAPE = (8, LANE)


def _carried_layout(carried):
    akeys = list(dict.fromkeys(k for cm in carried for k in cm.arrays))
    fkeys = [(ci, k) for ci, cm in enumerate(carried) for k in cm.fresh]
    arrays = [next(cm.arrays[k] for cm in carried if k in cm.arrays) for k in akeys]
    shapes = [jax.ShapeDtypeStruct(a.shape, a.dtype) for a in arrays] + [carried[ci].fresh[k] for ci, k in fkeys]
    sems = []
    for cm in carried:
        sems += [pltpu.SemaphoreType.DMA((cm.n_sems,)), pltpu.SemaphoreType.DMA((cm.n_sems,))]
    return akeys, fkeys, arrays, shapes, sems


def _carried_reads(carried):
    rkeys = list(dict.fromkeys(k for cm in carried for k in cm.reads))
    return rkeys, [next(cm.reads[k] for cm in carried if k in cm.reads) for k in rkeys]


def _carried_results(carried, akeys, fkeys, outs, rkeys=(), read_refs=()):
    shared = dict(zip(akeys, outs[:len(akeys)]))
    shared.update(zip(rkeys, read_refs))
    res = [{k: shared[k] for k in list(cm.arrays) + [r for r in cm.reads if r in shared]} for cm in carried]
    for (ci, k), o in zip(fkeys, outs[len(akeys):]):
        res[ci][k] = o
    return res


def _pcall(kern, *, name, grid, in_specs, out_specs, out_shape, args, scratch_shapes=(), semantics=None,
           carried=()):
    tokens = [c.array for c in carried if isinstance(c, _Token)]
    carried = [c for c in carried if not isinstance(c, _Token)]
    single = not isinstance(out_shape, (list, tuple))
    out_specs = [out_specs] if single else list(out_specs)
    out_shape = [out_shape] if single else list(out_shape)
    n_real, n_out, n_scr = len(in_specs), len(out_shape), len(scratch_shapes)
    in_specs = list(in_specs) + [pl.BlockSpec(TOKEN_SHAPE, lambda *_: (0, 0))] * len(tokens)
    args = list(args) + tokens
    n_in = len(in_specs)
    if not carried:
        def plain(*refs):
            kern(*refs[:n_real], *refs[n_in:])

        outs = pl.pallas_call(plain if tokens else kern, name=name, grid=grid, in_specs=in_specs,
                              out_specs=out_specs, out_shape=out_shape, scratch_shapes=list(scratch_shapes),
                              compiler_params=_cparams(semantics))(*args)
        return (outs[0] if single else list(outs)), []
    akeys, fkeys, arrays, shapes, sems = _carried_layout(carried)
    rkeys, reads = _carried_reads(carried)
    n_a, n_f, n_r = len(akeys), len(fkeys), len(rkeys)
    total = math.prod(grid)
    mid_step = min(total - 1, (17 * total) // 20)

    def wrapped(*refs):
        ins = refs[:n_real]
        r0 = n_in + n_a
        o0 = r0 + n_r
        outs = refs[o0:o0 + n_out]
        a0 = o0 + n_out
        s0 = a0 + n_a + n_f
        per = _carried_results(carried, akeys, fkeys, refs[a0:s0], rkeys, refs[r0:o0])
        scratch = refs[s0:s0 + n_scr]
        sem = refs[s0 + n_scr:]
        step = 0
        for d, g in enumerate(grid):
            step = step * g + pl.program_id(d)

        @pl.when(step == 0)
        def _():
            for ci, cm in enumerate(carried):
                cm.start(per[ci], sem[2 * ci], sem[2 * ci + 1])

        kern(*ins, *outs, *scratch)

        @pl.when(step == mid_step)
        def _():
            for ci, cm in enumerate(carried):
                if cm.mid is not None:
                    cm.mid(per[ci], sem[2 * ci], sem[2 * ci + 1])

        @pl.when(step == total - 1)
        def _():
            for ci, cm in enumerate(carried):
                cm.finish(per[ci], sem[2 * ci], sem[2 * ci + 1])

    outs = pl.pallas_call(
        wrapped, name=name, grid=grid,
        in_specs=list(in_specs) + [ANY] * (n_a + n_r), out_specs=out_specs + [ANY] * (n_a + n_f),
        out_shape=out_shape + shapes,
        input_output_aliases={n_in + i: n_out + i for i in range(n_a)},
        scratch_shapes=list(scratch_shapes) + sems,
        compiler_params=_cparams(("arbitrary",) * len(grid)),
    )(*args, *arrays, *reads)
    res = _carried_results(carried, akeys, fkeys, outs[n_out:])
    return (outs[0] if single else list(outs[:n_out])), res


def _run_comm(carried, name):
    carried = list(carried)
    akeys, fkeys, arrays, shapes, sems = _carried_layout(carried)
    rkeys, reads = _carried_reads(carried)
    n_a, n_f, n_r = len(akeys), len(fkeys), len(rkeys)

    def body(*refs):
        o0 = n_a + n_r
        per = _carried_results(carried, akeys, fkeys, refs[o0:o0 + n_a + n_f], rkeys, refs[n_a:o0])
        sem = refs[o0 + n_a + n_f:]
        for hook in ("start", "mid", "finish"):
            for ci, cm in enumerate(carried):
                fn = getattr(cm, hook)
                if fn is not None:
                    fn(per[ci], sem[2 * ci], sem[2 * ci + 1])

    outs = pl.pallas_call(
        body, name=name, in_specs=[ANY] * (n_a + n_r), out_specs=[ANY] * (n_a + n_f), out_shape=shapes,
        input_output_aliases={i: i for i in range(n_a)}, scratch_shapes=sems,
    )(*arrays, *reads)
    return _carried_results(carried, akeys, fkeys, outs)


HBM_SPEC = pl.BlockSpec(memory_space=pltpu.HBM)
SEM_SPEC = pl.BlockSpec(memory_space=pltpu.SEMAPHORE)
SPLIT_EFFECT = pltpu.SideEffectType.DATAFLOW_SIDE_EFFECTING


def _in_hbm(a):
    return pltpu.with_memory_space_constraint(a, pltpu.HBM)


def _split_start(items, name, after=None):
    items = list(items)
    akeys, fkeys, arrays, shapes, sems = _carried_layout(items)
    assert not fkeys
    n_a, n_s = len(akeys), len(sems)
    n_in = n_a + (after is not None)

    def body(*refs):
        per = _carried_results(items, akeys, [], refs[n_in:n_in + n_a])
        sem = refs[n_in + n_a:n_in + n_a + n_s]
        for ci, cm in enumerate(items):
            cm.start(per[ci], sem[2 * ci], sem[2 * ci + 1])
        token = refs[n_in + n_a + n_s]
        token[...] = jnp.zeros_like(token)

    outs = pl.pallas_call(
        body, name=name, in_specs=[HBM_SPEC] * n_a + [ANY] * (after is not None),
        out_specs=[HBM_SPEC] * n_a + [SEM_SPEC] * n_s + [pl.BlockSpec(memory_space=pltpu.VMEM)],
        out_shape=[pltpu.HBM(s.shape, s.dtype) for s in shapes] + sems + [jax.ShapeDtypeStruct(TOKEN_SHAPE, F32)],
        input_output_aliases={i: i for i in range(n_a)},
        compiler_params=pltpu.CompilerParams(has_side_effects=SPLIT_EFFECT),
    )(*[_in_hbm(a) for a in arrays], *([after] if after is not None else []))
    res = _carried_results(items, akeys, [], outs[:n_a])
    sem_out = outs[n_a:n_a + n_s]
    return res, [(sem_out[2 * ci], sem_out[2 * ci + 1]) for ci in range(len(items))], outs[-1]


def _split_wait(items, sems, after, name):
    items = list(items)
    after = list(after) if isinstance(after, (list, tuple)) else [after]
    akeys, fkeys, arrays, shapes, _ = _carried_layout(items)
    n_a, n_s = len(akeys), 2 * len(items)

    def body(*refs):
        per = _carried_results(items, akeys, [], refs[n_a + n_s + len(after):])
        sem = refs[n_a:n_a + n_s]
        for ci, cm in enumerate(items):
            cm.finish(per[ci], sem[2 * ci], sem[2 * ci + 1])

    outs = pl.pallas_call(
        body, name=name, in_specs=[HBM_SPEC] * n_a + [SEM_SPEC] * n_s + [ANY] * len(after),
        out_specs=[HBM_SPEC] * n_a, out_shape=[pltpu.HBM(s.shape, s.dtype) for s in shapes],
        input_output_aliases={i: i for i in range(n_a)},
        compiler_params=pltpu.CompilerParams(has_side_effects=SPLIT_EFFECT),
    )(*arrays, *[s for pair in sems for s in pair], *after)
    return _carried_results(items, akeys, [], outs)


def _mm(a, b, *, mode, out_dtype, name, res=None, carried=()):
    if mode == "nn":
        (m, k), (k2, n) = a.shape, b.shape
    elif mode == "nt":
        (m, k), (n, k2) = a.shape, b.shape
    else:
        (k, m), (k2, n) = a.shape, b.shape
    assert k == k2, (name, a.shape, b.shape)
    tm = _div(m, MM_TILE_CAP, LANE)
    tn = _div(n, MM_TILE_CAP, MXU_WIDTH) if n % MXU_WIDTH == 0 else 0
    if tn < 1024:
        tn = _div(n, MM_TILE_CAP, LANE)
    out_shape = jax.ShapeDtypeStruct((m, n), out_dtype)

    if mode == "tn":
        assert res is None

        def kern_tn(a_ref, b_ref, o_ref, at_ref):
            @pl.when(pl.program_id(1) == 0)
            def _():
                step = min(TRANSPOSE_CHUNK, k)
                for c0 in range(0, k, step):
                    at_ref[:, c0:c0 + step] = a_ref[c0:c0 + step, :].astype(F32).T.astype(BF16)

            o_ref[...] = jnp.dot(at_ref[...], b_ref[...].astype(BF16),
                                 preferred_element_type=F32).astype(o_ref.dtype)

        return _pcall(
            kern_tn, name=name, grid=(m // tm, n // tn),
            in_specs=[pl.BlockSpec((k, tm), lambda i, j: (0, i)),
                      pl.BlockSpec((k, tn), lambda i, j: (0, j))],
            out_specs=pl.BlockSpec((tm, tn), lambda i, j: (i, j)),
            out_shape=out_shape, args=(a, b),
            scratch_shapes=[pltpu.VMEM((tm, k), BF16)],
            semantics=("parallel", "arbitrary"), carried=carried)

    tk = k if k <= 2048 else _div(k, 3072, LANE)
    nk = k // tk
    a_spec = pl.BlockSpec((tm, tk), lambda i, j, kk: (i, kk))
    if mode == "nn":
        b_spec = pl.BlockSpec((tk, tn), lambda i, j, kk: (kk, j))
        dot = _dot
    else:
        b_spec = pl.BlockSpec((tn, tk), lambda i, j, kk: (j, kk))
        dot = _dot_nt
    o_spec = pl.BlockSpec((tm, tn), lambda i, j, kk: (i, j))
    in_specs = [a_spec, b_spec]
    args = [a, b]
    if res is not None:
        in_specs.append(o_spec)
        args.append(res)
    has_res = res is not None

    def kern(*refs):
        a_ref, b_ref = refs[0], refs[1]
        r_ref = refs[2] if has_res else None
        o_ref = refs[3] if has_res else refs[2]
        part = dot(a_ref[...], b_ref[...])
        if nk == 1:
            if has_res:
                part = part + r_ref[...]
            o_ref[...] = part.astype(o_ref.dtype)
            return
        acc_ref = refs[-1]
        kk = pl.program_id(2)

        @pl.when(kk == 0)
        def _():
            acc_ref[...] = part

        @pl.when(kk > 0)
        def _():
            acc_ref[...] += part

        @pl.when(kk == nk - 1)
        def _():
            tot = acc_ref[...]
            if has_res:
                tot = tot + r_ref[...]
            o_ref[...] = tot.astype(o_ref.dtype)

    return _pcall(
        kern, name=name, grid=(m // tm, n // tn, nk),
        in_specs=in_specs, out_specs=o_spec, out_shape=out_shape, args=args,
        scratch_shapes=[pltpu.VMEM((tm, tn), F32)] if nk > 1 else [],
        semantics=("parallel", "parallel", "arbitrary"), carried=carried)


ROW_BLOCK = 256


def _rms_fwd(x, g, name):
    t, d = x.shape
    tr = min(ROW_BLOCK, t)

    def kern(x_ref, g_ref, o_ref):
        xf = x_ref[...]
        r = lax.rsqrt(jnp.mean(xf * xf, axis=-1, keepdims=True) + RMS_EPS)
        o_ref[...] = (xf * r * g_ref[...]).astype(o_ref.dtype)

    return pl.pallas_call(
        kern, name=name, grid=(t // tr,),
        in_specs=[pl.BlockSpec((tr, d), lambda i: (i, 0)), pl.BlockSpec((1, d), lambda i: (0, 0))],
        out_specs=pl.BlockSpec((tr, d), lambda i: (i, 0)),
        out_shape=jax.ShapeDtypeStruct((t, d), BF16),
        compiler_params=_cparams(("parallel",)),
    )(x, g)


def _rms_bwd(x, g, dh, res, name):
    t, d = x.shape
    tr = min(ROW_BLOCK, t)
    has_res = res is not None

    def kern(*refs):
        x_ref, g_ref, dh_ref = refs[:3]
        r_ref = refs[3] if has_res else None
        dx_ref, dg_ref = refs[-2], refs[-1]
        xf = x_ref[...]
        r = lax.rsqrt(jnp.mean(xf * xf, axis=-1, keepdims=True) + RMS_EPS)
        xn = xf * r
        dh_ = dh_ref[...]
        dhg = dh_ * g_ref[...]
        dx = r * (dhg - xn * jnp.mean(dhg * xn, axis=-1, keepdims=True))
        if has_res:
            dx = dx + r_ref[...]
        dx_ref[...] = dx
        part = jnp.sum(dh_ * xn, axis=0, keepdims=True)

        @pl.when(pl.program_id(0) == 0)
        def _():
            dg_ref[...] = part

        @pl.when(pl.program_id(0) > 0)
        def _():
            dg_ref[...] += part

    row = pl.BlockSpec((tr, d), lambda i: (i, 0))
    vec = pl.BlockSpec((1, d), lambda i: (0, 0))
    in_specs = [row, vec, row] + ([row] if has_res else [])
    args = [x, g, dh] + ([res] if has_res else [])
    return pl.pallas_call(
        kern, name=name, grid=(t // tr,), in_specs=in_specs, out_specs=[row, vec],
        out_shape=[jax.ShapeDtypeStruct((t, d), F32), jax.ShapeDtypeStruct((1, d), F32)],
        compiler_params=_cparams(("arbitrary",)),
    )(*args)


def _loss_head(x3, g, target, name):
    t, d = x3.shape
    tr = ROW_BLOCK

    def kern(x_ref, g_ref, t_ref, dx_ref, dg_ref, loss_ref):
        xf = x_ref[...]
        r = lax.rsqrt(jnp.mean(xf * xf, axis=-1, keepdims=True) + RMS_EPS)
        xn = xf * r
        gg = g_ref[...]
        err = xn * gg - t_ref[...]
        lpart = 0.5 * jnp.sum(jnp.mean(err * err, axis=-1, keepdims=True), axis=0, keepdims=True)
        dy = err * (1.0 / d)
        dyg = dy * gg
        dx_ref[...] = r * (dyg - xn * jnp.mean(dyg * xn, axis=-1, keepdims=True))
        gpart = jnp.sum(dy * xn, axis=0, keepdims=True)
        lrow = jnp.broadcast_to(lpart, (1, LANE))

        @pl.when(pl.program_id(0) == 0)
        def _():
            dg_ref[...] = gpart
            loss_ref[...] = lrow

        @pl.when(pl.program_id(0) > 0)
        def _():
            dg_ref[...] += gpart
            loss_ref[...] += lrow

    row = pl.BlockSpec((tr, d), lambda i: (i, 0))
    vec = pl.BlockSpec((1, d), lambda i: (0, 0))
    return pl.pallas_call(
        kern, name=name, grid=(t // tr,), in_specs=[row, vec, row],
        out_specs=[row, vec, pl.BlockSpec((1, LANE), lambda i: (0, 0))],
        out_shape=[jax.ShapeDtypeStruct((t, d), F32), jax.ShapeDtypeStruct((1, d), F32),
                   jax.ShapeDtypeStruct((1, LANE), F32)],
        compiler_params=_cparams(("arbitrary",)),
    )(x3, g, target)


ATT_SCALE = HEAD_DIM ** -0.5
Q_BLOCK0, K_BLOCK0, V_BLOCK0 = 0, ATT_WIDTH // HEAD_DIM, 2 * ATT_WIDTH // HEAD_DIM


def _residue_rows(dil, r, n):
    if dil == 1:
        return pl.ds(n * ATT_BLOCK, ATT_BLOCK)
    return pl.ds(n * ATT_BLOCK * dil + r, ATT_BLOCK, stride=dil)


def _band_mask(with_prev):
    width = 2 * ATT_BLOCK if with_prev else ATT_BLOCK
    iq = lax.broadcasted_iota(jnp.int32, (ATT_BLOCK, width), 0)
    ik = lax.broadcasted_iota(jnp.int32, (ATT_BLOCK, width), 1)
    if not with_prev:
        return ik <= iq
    return ((ik < ATT_BLOCK) & (iq <= ik)) | ((ik >= ATT_BLOCK) & ((ik - ATT_BLOCK) <= iq))


def _band_keys(ref, dil, r, n):
    own = ref[_residue_rows(dil, r, n), :]
    if n == 0:
        return own
    return jnp.concatenate([ref[_residue_rows(dil, r, n - 1), :], own], axis=0)


def _attn_col_spec(base, grp):
    return pl.BlockSpec((SEQ, HEAD_DIM), lambda h: (0, base + grp * ATT_HEADS + h))


def _attn_fwd(proj, grp, name, carried=()):
    _, dil = ATT_GROUPS[grp]
    nb = SEQ // dil // ATT_BLOCK

    def kern(q_ref, k_ref, v_ref, o_ref, lse_ref):
        for r in range(dil):
            for n in range(nb):
                rows = _residue_rows(dil, r, n)
                s = _dot_nt(q_ref[rows, :], _band_keys(k_ref, dil, r, n)) * ATT_SCALE
                s = jnp.where(_band_mask(n > 0), s, -jnp.inf)
                m = jnp.max(s, axis=-1, keepdims=True)
                p = jnp.exp(s - m)
                l = jnp.sum(p, axis=-1, keepdims=True)
                o_ref[rows, :] = _dot(p / l, _band_keys(v_ref, dil, r, n))
                lse_ref[rows, :] = jnp.broadcast_to(m + jnp.log(l), (ATT_BLOCK, HEAD_DIM))

    out_spec = pl.BlockSpec((SEQ, HEAD_DIM), lambda h: (0, h))
    return _pcall(
        kern, name=name, grid=(ATT_HEADS,),
        in_specs=[_attn_col_spec(Q_BLOCK0, grp), _attn_col_spec(K_BLOCK0, grp), _attn_col_spec(V_BLOCK0, grp)],
        out_specs=[out_spec, out_spec],
        out_shape=[jax.ShapeDtypeStruct((SEQ, ATT_OUT), F32)] * 2, args=(proj, proj, proj),
        semantics=("parallel",), carried=carried)


def _attn_weights(l0, l1, l2):
    mx = jnp.maximum(jnp.maximum(l0, l1), l2)
    e0, e1, e2 = jnp.exp(l0 - mx), jnp.exp(l1 - mx), jnp.exp(l2 - mx)
    den = e0 + e1 + e2
    return e0 / den, e1 / den, e2 / den


def _attn_merge_fwd(outs, lses, name):
    tr = ROW_BLOCK

    def kern(o0, o1, o2, l0, l1, l2, out_ref):
        a0, a1, a2 = _attn_weights(l0[...], l1[...], l2[...])
        out_ref[...] = (a0 * o0[...] + a1 * o1[...] + a2 * o2[...]).astype(out_ref.dtype)

    spec = pl.BlockSpec((tr, ATT_OUT), lambda i: (i, 0))
    return pl.pallas_call(
        kern, name=name, grid=(SEQ // tr,), in_specs=[spec] * 6, out_specs=spec,
        out_shape=jax.ShapeDtypeStruct((SEQ, ATT_OUT), BF16),
        compiler_params=_cparams(("parallel",)),
    )(*outs, *lses)


def _attn_merge_bwd(outs, lses, do_att, name, carried=()):
    tr = ROW_BLOCK

    def kern(o0, o1, o2, l0, l1, l2, do_ref, d0, d1, d2, t0, t1, t2):
        alphas = _attn_weights(l0[...], l1[...], l2[...])
        do = do_ref[...]
        o_att = alphas[0] * o0[...] + alphas[1] * o1[...] + alphas[2] * o2[...]
        prod = do * o_att
        parts = []
        for h in range(ATT_HEADS):
            sl = slice(h * HEAD_DIM, (h + 1) * HEAD_DIM)
            tot = jnp.sum(prod[:, sl], axis=-1, keepdims=True)
            parts.append(jnp.broadcast_to(tot, (tr, HEAD_DIM)))
        dd = jnp.concatenate(parts, axis=1)
        for a, d_ref, t_ref in zip(alphas, (d0, d1, d2), (t0, t1, t2)):
            d_ref[...] = a * do
            t_ref[...] = -a * dd

    spec = pl.BlockSpec((tr, ATT_OUT), lambda i: (i, 0))
    res, cres = _pcall(
        kern, name=name, grid=(SEQ // tr,), in_specs=[spec] * 7, out_specs=[spec] * 6,
        out_shape=[jax.ShapeDtypeStruct((SEQ, ATT_OUT), F32)] * 6, args=(*outs, *lses, do_att),
        semantics=("parallel",), carried=carried)
    return (res[:3], res[3:]), cres


def _attn_bwd(proj, grp, lse, do_g, dl_g, name, carried=()):
    _, dil = ATT_GROUPS[grp]
    nb = SEQ // dil // ATT_BLOCK

    def kern(q_ref, k_ref, v_ref, do_ref, lse_ref, dl_ref, dq_ref, dk_ref, dv_ref, dq_acc, dk_acc, dv_acc):
        dk_acc[...] = jnp.zeros_like(dk_acc)
        dv_acc[...] = jnp.zeros_like(dv_acc)
        for r in range(dil):
            for n in range(nb):
                rows = _residue_rows(dil, r, n)
                q, do = q_ref[rows, :], do_ref[rows, :]
                kk, vv = _band_keys(k_ref, dil, r, n), _band_keys(v_ref, dil, r, n)
                s = _dot_nt(q, kk) * ATT_SCALE
                p = jnp.where(_band_mask(n > 0), jnp.exp(s - lse_ref[rows, :][:, :1]), 0.0)
                ds = p * (_dot_nt(do, vv) + dl_ref[rows, :][:, :1])
                dq_acc[rows, :] = _dot(ds, kk) * ATT_SCALE
                dk = _dot_tn(ds, q) * ATT_SCALE
                dv = _dot_tn(p, do)
                if n > 0:
                    prev = _residue_rows(dil, r, n - 1)
                    dk_acc[prev, :] += dk[:ATT_BLOCK]
                    dv_acc[prev, :] += dv[:ATT_BLOCK]
                    dk, dv = dk[ATT_BLOCK:], dv[ATT_BLOCK:]
                dk_acc[rows, :] += dk
                dv_acc[rows, :] += dv
        dq_ref[...] = dq_acc[...].astype(dq_ref.dtype)
        dk_ref[...] = dk_acc[...].astype(dk_ref.dtype)
        dv_ref[...] = dv_acc[...].astype(dv_ref.dtype)

    spec = pl.BlockSpec((SEQ, HEAD_DIM), lambda h: (0, h))
    return _pcall(
        kern, name=name, grid=(ATT_HEADS,),
        in_specs=[_attn_col_spec(Q_BLOCK0, grp), _attn_col_spec(K_BLOCK0, grp), _attn_col_spec(V_BLOCK0, grp),
                  spec, spec, spec],
        out_specs=[spec] * 3,
        out_shape=[jax.ShapeDtypeStruct((SEQ, ATT_OUT), BF16)] * 3, args=(proj, proj, proj, do_g, lse, dl_g),
        scratch_shapes=[pltpu.VMEM((SEQ, HEAD_DIM), F32)] * 3,
        semantics=("parallel",), carried=carried)


HG_HEADS_PER_STEP = 8
HG_BLOCK_W = 4 * HEAD_DIM
HG_BLOCKS = HG_HEADS_PER_STEP * HEAD_DIM // HG_BLOCK_W
HG_STEP_W = HG_HEADS_PER_STEP * HEAD_DIM
HG_Q_BLK = (3 * ATT_WIDTH) // HG_BLOCK_W
HG_N_CHUNKS = SEQ // HG_CHUNK
HG_MID = HG_CHUNK // 2


def _lower_bound(lb_ref, sl):
    l0, l1 = lb_ref[0:1, sl], lb_ref[1:2, sl]
    mx = jnp.maximum(l0, l1)
    e0, e1 = jnp.exp(l0 - mx), jnp.exp(l1 - mx)
    return e0 / (e0 + e1)


def _tri(lower):
    i = lax.broadcasted_iota(jnp.int32, (HG_CHUNK, HG_CHUNK), 0)
    j = lax.broadcasted_iota(jnp.int32, (HG_CHUNK, HG_CHUNK), 1)
    return (i >= j) if lower else (i <= j)


def _head_mean(x):
    parts = []
    for hd in range(x.shape[1] // HEAD_DIM):
        m = jnp.mean(x[:, hd * HEAD_DIM:(hd + 1) * HEAD_DIM], axis=-1, keepdims=True)
        parts.append(jnp.broadcast_to(m, (x.shape[0], HEAD_DIM)))
    return jnp.concatenate(parts, axis=1)


def _hg_chunk_terms(qh, fh, lb):
    sig = _sigmoid(fh)
    f = lb + (1.0 - lb) * sig
    k = 1.0 - f
    b = _dot_exact(_tri(True).astype(F32), jnp.log(f))
    bl = b[HG_CHUNK - 1:HG_CHUNK, :]
    br = b[HG_MID:HG_MID + 1, :]
    sq = _sigmoid(qh)
    q = qh * sq
    return dict(sig=sig, f=f, k=k, b=b, bl=bl, br=br, sq=sq, q=q,
                e1=jnp.exp(bl - b), e2=jnp.exp(b), e3=jnp.exp(b - br), e4=jnp.exp(br - b))


def _hg_fwd(proj, lbw, normw, name, carried=()):
    def in_blks(off):
        return [pl.BlockSpec((HG_CHUNK, HG_BLOCK_W), lambda hp, n, b=b: (n, HG_Q_BLK + off + hp * HG_BLOCKS + b))
                for b in range(HG_BLOCKS)]

    def kern(*refs):
        q_refs, f_refs, i_refs, g_refs = (refs[k * HG_BLOCKS:(k + 1) * HG_BLOCKS] for k in range(4))
        lb_ref, nw_ref, oraw_ref, ohg_ref, st_ref, state = refs[4 * HG_BLOCKS:]

        @pl.when(pl.program_id(1) == 0)
        def _():
            state[...] = jnp.zeros_like(state)

        causal = _tri(True)
        wide = lambda rs: jnp.concatenate([r[...] for r in rs], axis=1)
        t = _hg_chunk_terms(wide(q_refs), wide(f_refs), _lower_bound(lb_ref, slice(None)))
        v, gh = wide(i_refs), wide(g_refs)
        kd, qb, qr, kr = t["k"] * t["e1"], t["q"] * t["e2"], t["q"] * t["e3"], t["k"] * t["e4"]
        decay = jnp.exp(t["bl"])
        outs = []
        for hd in range(HG_HEADS_PER_STEP):
            sl = slice(hd * HEAD_DIM, (hd + 1) * HEAD_DIM)
            st = state[hd]
            st_ref[0, hd] = st
            a = jnp.where(causal, _dot_nt(qr[:, sl], kr[:, sl]), 0.0)
            outs.append(_dot_nt(qb[:, sl], st) + _dot(a, v[:, sl]))
            state[hd] = st * decay[:, sl] + _dot_tn(v[:, sl], kd[:, sl])
        o = jnp.concatenate(outs, axis=1)
        oraw_ref[...] = o
        r = lax.rsqrt(_head_mean(o * o) + RMS_EPS)
        nw = jnp.tile(nw_ref[...], (1, HG_HEADS_PER_STEP))
        ohg_ref[...] = (o * r * nw * (gh * _sigmoid(gh))).astype(ohg_ref.dtype)

    out_blk = pl.BlockSpec((HG_CHUNK, HG_STEP_W), lambda hp, n: (n, hp))
    return _pcall(
        kern, name=name, grid=(HG_HEADS // HG_HEADS_PER_STEP, HG_N_CHUNKS),
        in_specs=[*in_blks(0), *in_blks(2), *in_blks(4), *in_blks(6),
                  pl.BlockSpec((2, HG_STEP_W), lambda hp, n: (0, hp)),
                  pl.BlockSpec((1, HEAD_DIM), lambda hp, n: (0, 0))],
        out_specs=[out_blk, out_blk,
                   pl.BlockSpec((1, HG_HEADS_PER_STEP, HEAD_DIM, HEAD_DIM), lambda hp, n: (n, hp, 0, 0))],
        out_shape=[jax.ShapeDtypeStruct((SEQ, HG_WIDTH), F32), jax.ShapeDtypeStruct((SEQ, HG_WIDTH), BF16),
                   jax.ShapeDtypeStruct((HG_N_CHUNKS, HG_HEADS, HEAD_DIM, HEAD_DIM), F32)],
        args=(*[proj] * (4 * HG_BLOCKS), lbw, normw),
        scratch_shapes=[pltpu.VMEM((HG_HEADS_PER_STEP, HEAD_DIM, HEAD_DIM), F32)],
        semantics=("parallel", "arbitrary"), carried=carried)


def _hg_bwd(proj, lbw, normw, oraw, states, do_hg, name, carried=()):
    last = HG_N_CHUNKS - 1

    def in_blks(off):
        return [pl.BlockSpec((HG_CHUNK, HG_BLOCK_W),
                             lambda hp, n, b=b: (last - n, HG_Q_BLK + off + hp * HG_BLOCKS + b))
                for b in range(HG_BLOCKS)]

    blk = pl.BlockSpec((HG_CHUNK, HG_STEP_W), lambda hp, n: (last - n, hp))

    def kern(*refs):
        q_refs, f_refs, i_refs, g_refs = (refs[k * HG_BLOCKS:(k + 1) * HG_BLOCKS] for k in range(4))
        (lb_ref, nw_ref, oraw_ref, st_ref, do_ref, dq_ref, df_ref, di_ref, dg_ref, dlb_ref, dnw_ref,
         dstate) = refs[4 * HG_BLOCKS:]
        first = pl.program_id(1) == 0

        @pl.when(first)
        def _():
            dstate[...] = jnp.zeros_like(dstate)

        causal = _tri(True)
        wide = lambda rs: jnp.concatenate([r[...] for r in rs], axis=1)
        cat = lambda parts: jnp.concatenate(parts, axis=1)
        qh, fh, v, gh = wide(q_refs), wide(f_refs), wide(i_refs), wide(g_refs)
        o, dout = oraw_ref[...], do_ref[...]
        nw = jnp.tile(nw_ref[...], (1, HG_HEADS_PER_STEP))
        sgg = _sigmoid(gh)
        r = lax.rsqrt(_head_mean(o * o) + RMS_EPS)
        xn = o * r
        dg_ref[...] = (dout * xn * nw * (sgg * (1.0 + gh * (1.0 - sgg)))).astype(dg_ref.dtype)
        don = dout * (gh * sgg)
        dnw_wide = jnp.sum(don * xn, axis=0, keepdims=True)
        dnw_tot = dnw_wide[:, :HEAD_DIM]
        for hd in range(1, HG_HEADS_PER_STEP):
            dnw_tot = dnw_tot + dnw_wide[:, hd * HEAD_DIM:(hd + 1) * HEAD_DIM]
        tt = don * nw
        do = r * (tt - xn * _head_mean(tt * xn))
        lb = _lower_bound(lb_ref, slice(None))
        t = _hg_chunk_terms(qh, fh, lb)
        k, q = t["k"], t["q"]
        kd, qb, qr, kr = k * t["e1"], q * t["e2"], q * t["e3"], k * t["e4"]
        decay = jnp.exp(t["bl"])
        dqb, dqr, dkr, dkd, dv, ddecay = [], [], [], [], [], []
        for hd in range(HG_HEADS_PER_STEP):
            sl = slice(hd * HEAD_DIM, (hd + 1) * HEAD_DIM)
            st = st_ref[0, hd]
            dstn = dstate[hd]
            a = jnp.where(causal, _dot_nt(qr[:, sl], kr[:, sl]), 0.0)
            da = jnp.where(causal, _dot_nt(do[:, sl], v[:, sl]), 0.0)
            dqb.append(_dot(do[:, sl], st))
            dv.append(_dot_tn(a, do[:, sl]) + _dot_nt(kd[:, sl], dstn))
            dqr.append(_dot(da, kr[:, sl]))
            dkr.append(_dot_tn(da, qr[:, sl]))
            dkd.append(_dot(v[:, sl], dstn))
            ddecay.append(jnp.sum(dstn * st, axis=0, keepdims=True))
            dstate[hd] = dstn * decay[:, sl] + _dot_tn(do[:, sl], qb[:, sl])
        dqb, dqr, dkr, dkd, dv, ddecay = cat(dqb), cat(dqr), cat(dkr), cat(dkd), cat(dv), cat(ddecay)
        dq = dqb * t["e2"] + dqr * t["e3"]
        dk = dkd * t["e1"] + dkr * t["e4"]
        db = dqb * qb + dqr * qr - dkr * kr - dkd * kd
        dbl = jnp.sum(dkd * kd, axis=0, keepdims=True) + ddecay * decay
        dbr = jnp.sum(dkr * kr - dqr * qr, axis=0, keepdims=True)
        rows = lax.broadcasted_iota(jnp.int32, db.shape, 0)
        dlf = _dot_exact(_tri(False).astype(F32), db) + dbl + jnp.where(rows <= HG_MID, dbr, 0.0)
        df = dlf / t["f"] - dk
        sig, sq = t["sig"], t["sq"]
        df_ref[...] = (df * (1.0 - lb) * sig * (1.0 - sig)).astype(df_ref.dtype)
        dlb_row = jnp.sum(df * (1.0 - sig), axis=0, keepdims=True)
        dq_ref[...] = (dq * (sq * (1.0 + qh * (1.0 - sq)))).astype(dq_ref.dtype)
        di_ref[...] = dv.astype(di_ref.dtype)
        dnw_blk = jnp.broadcast_to(dnw_tot, (8, HEAD_DIM))

        @pl.when(first)
        def _():
            dlb_ref[...] = dlb_row
            dnw_ref[...] = dnw_blk

        @pl.when(jnp.logical_not(first))
        def _():
            dlb_ref[...] += dlb_row
            dnw_ref[...] += dnw_blk

    n_hp = HG_HEADS // HG_HEADS_PER_STEP
    outs, cres = _pcall(
        kern, name=name, grid=(n_hp, HG_N_CHUNKS),
        in_specs=[*in_blks(0), *in_blks(2), *in_blks(4), *in_blks(6),
                  pl.BlockSpec((2, HG_STEP_W), lambda hp, n: (0, hp)),
                  pl.BlockSpec((1, HEAD_DIM), lambda hp, n: (0, 0)),
                  blk,
                  pl.BlockSpec((1, HG_HEADS_PER_STEP, HEAD_DIM, HEAD_DIM), lambda hp, n: (last - n, hp, 0, 0)),
                  blk],
        out_specs=[blk, blk, blk, blk,
                   pl.BlockSpec((1, HG_STEP_W), lambda hp, n: (0, hp)),
                   pl.BlockSpec((8, HEAD_DIM), lambda hp, n: (hp, 0))],
        out_shape=[jax.ShapeDtypeStruct((SEQ, HG_WIDTH), BF16)] * 4
        + [jax.ShapeDtypeStruct((1, HG_WIDTH), F32), jax.ShapeDtypeStruct((8 * n_hp, HEAD_DIM), F32)],
        args=(*[proj] * (4 * HG_BLOCKS), lbw, normw, oraw, states, do_hg),
        scratch_shapes=[pltpu.VMEM((HG_HEADS_PER_STEP, HEAD_DIM, HEAD_DIM), F32)],
        semantics=("parallel", "arbitrary"), carried=carried)
    dqh, dfh, dih, dgh, dlb, dnw = outs
    return (dqh, dfh, dih, dgh, dlb, [dnw[8 * i:8 * i + 1] for i in range(n_hp)]), cres


GATE_BLOCK_W = 512
GATE_A_BLK = (3 * ATT_WIDTH + 4 * HG_WIDTH) // GATE_BLOCK_W
GATE_B_BLK = GATE_A_BLK + D_MODEL // GATE_BLOCK_W


def _gate_specs():
    tr = ROW_BLOCK
    blk = pl.BlockSpec((tr, GATE_BLOCK_W), lambda i, j: (i, j))
    ga = pl.BlockSpec((tr, GATE_BLOCK_W), lambda i, j: (i, GATE_A_BLK + j))
    gb = pl.BlockSpec((tr, GATE_BLOCK_W), lambda i, j: (i, GATE_B_BLK + j))
    return (SEQ // tr, D_MODEL // GATE_BLOCK_W), blk, ga, gb


def _gate_fwd(proj, ya, yb, name, carried=()):
    grid, blk, ga, gb = _gate_specs()

    def kern(ga_ref, gb_ref, ya_ref, yb_ref, o_ref):
        o_ref[...] = (_sigmoid(ga_ref[...]) * ya_ref[...] + _sigmoid(gb_ref[...]) * yb_ref[...]).astype(o_ref.dtype)

    return _pcall(
        kern, name=name, grid=grid, in_specs=[ga, gb, blk, blk], out_specs=blk,
        out_shape=jax.ShapeDtypeStruct((SEQ, D_MODEL), BF16), args=(proj, proj, ya, yb),
        semantics=("parallel", "parallel"), carried=carried)


def _gate_bwd(proj, ya, yb, dmerged, name, carried=()):
    grid, blk, ga, gb = _gate_specs()

    def kern(ga_ref, gb_ref, ya_ref, yb_ref, dm_ref, dya_ref, dyb_ref, dga_ref, dgb_ref):
        dm = dm_ref[...]
        sa, sb = _sigmoid(ga_ref[...]), _sigmoid(gb_ref[...])
        dya_ref[...] = (dm * sa).astype(dya_ref.dtype)
        dyb_ref[...] = (dm * sb).astype(dyb_ref.dtype)
        dga_ref[...] = (dm * ya_ref[...] * sa * (1.0 - sa)).astype(dga_ref.dtype)
        dgb_ref[...] = (dm * yb_ref[...] * sb * (1.0 - sb)).astype(dgb_ref.dtype)

    return _pcall(
        kern, name=name, grid=grid, in_specs=[ga, gb, blk, blk, blk], out_specs=[blk] * 4,
        out_shape=[jax.ShapeDtypeStruct((SEQ, D_MODEL), BF16)] * 4, args=(proj, proj, ya, yb, dmerged),
        semantics=("parallel", "parallel"), carried=carried)


FF_SHARD = D_FF // N_CHIPS


def _swiglu_fwd(ab, name):
    tr = ROW_BLOCK

    def kern(ab_ref, u_ref):
        a, b = ab_ref[:, :FF_SHARD], ab_ref[:, FF_SHARD:]
        u_ref[...] = (a * _sigmoid(a) * b).astype(u_ref.dtype)

    return pl.pallas_call(
        kern, name=name, grid=(SEQ // tr, N_CHIPS),
        in_specs=[pl.BlockSpec((tr, 2 * FF_SHARD), lambda i, j: (i, j))],
        out_specs=pl.BlockSpec((tr, FF_SHARD), lambda i, j: (i, j)),
        out_shape=jax.ShapeDtypeStruct((SEQ, D_FF), BF16),
        compiler_params=_cparams(("parallel", "parallel")),
    )(ab)


def _swiglu_bwd(ab, du, name, carried=()):
    tr = ROW_BLOCK

    def kern(ab_ref, du_ref, dab_ref):
        a, b = ab_ref[:, :FF_SHARD], ab_ref[:, FF_SHARD:]
        du_ = du_ref[...]
        sg = _sigmoid(a)
        dab_ref[:, :FF_SHARD] = (du_ * b * (sg * (1.0 + a * (1.0 - sg)))).astype(dab_ref.dtype)
        dab_ref[:, FF_SHARD:] = (du_ * (a * sg)).astype(dab_ref.dtype)

    wide = pl.BlockSpec((tr, 2 * FF_SHARD), lambda i, j: (i, j))
    return _pcall(
        kern, name=name, grid=(SEQ // tr, N_CHIPS),
        in_specs=[wide, pl.BlockSpec((tr, FF_SHARD), lambda i, j: (i, j))],
        out_specs=wide, out_shape=jax.ShapeDtypeStruct((SEQ, 2 * D_FF), BF16), args=(ab, du),
        semantics=("parallel", "parallel"), carried=carried)


CROSS_ROWS = 512


def _cross_fwd(qc, kvc, name):
    def kern(q_ref, k_ref, v_ref, o_ref):
        s = _dot_nt(q_ref[...], k_ref[...]) * ATT_SCALE
        m = jnp.max(s, axis=-1, keepdims=True)
        e = jnp.exp(s - m)
        p = e / jnp.sum(e, axis=-1, keepdims=True)
        o_ref[...] = _dot(p, v_ref[...]).astype(o_ref.dtype)

    qblk = pl.BlockSpec((CROSS_ROWS, HEAD_DIM), lambda h, i: (i, h))
    return pl.pallas_call(
        kern, name=name, grid=(CROSS_HEADS, SEQ // CROSS_ROWS),
        in_specs=[qblk, pl.BlockSpec((MEM_LEN, HEAD_DIM), lambda h, i: (0, h)),
                  pl.BlockSpec((MEM_LEN, HEAD_DIM), lambda h, i: (0, CROSS_HEADS + h))],
        out_specs=qblk, out_shape=jax.ShapeDtypeStruct((SEQ, CROSS_WIDTH), BF16),
        compiler_params=_cparams(("parallel", "parallel")),
    )(qc, kvc, kvc)


def _cross_bwd(qc, kvc, doc, name):
    def kern(q_ref, k_ref, v_ref, do_ref, dq_ref, dk_ref, dv_ref):
        q, k, v, do = q_ref[...], k_ref[...], v_ref[...], do_ref[...]
        s = _dot_nt(q, k) * ATT_SCALE
        m = jnp.max(s, axis=-1, keepdims=True)
        e = jnp.exp(s - m)
        p = e / jnp.sum(e, axis=-1, keepdims=True)
        dp = _dot_nt(do, v)
        ds = p * (dp - jnp.sum(dp * p, axis=-1, keepdims=True))
        dq_ref[...] = (_dot(ds, k) * ATT_SCALE).astype(dq_ref.dtype)
        dk = _dot_tn(ds, q) * ATT_SCALE
        dv = _dot_tn(p, do)

        @pl.when(pl.program_id(1) == 0)
        def _():
            dk_ref[...] = dk
            dv_ref[...] = dv

        @pl.when(pl.program_id(1) > 0)
        def _():
            dk_ref[...] += dk
            dv_ref[...] += dv

    qblk = pl.BlockSpec((CROSS_ROWS, HEAD_DIM), lambda h, i: (i, h))
    kblk = pl.BlockSpec((MEM_LEN, HEAD_DIM), lambda h, i: (0, h))
    dq, dk, dv = pl.pallas_call(
        kern, name=name, grid=(CROSS_HEADS, SEQ // CROSS_ROWS),
        in_specs=[qblk, kblk, pl.BlockSpec((MEM_LEN, HEAD_DIM), lambda h, i: (0, CROSS_HEADS + h)), qblk],
        out_specs=[qblk, kblk, kblk],
        out_shape=[jax.ShapeDtypeStruct((SEQ, CROSS_WIDTH), BF16),
                   jax.ShapeDtypeStruct((MEM_LEN, CROSS_WIDTH), F32),
                   jax.ShapeDtypeStruct((MEM_LEN, CROSS_WIDTH), F32)],
        compiler_params=_cparams(("parallel", "arbitrary")),
    )(qc, kvc, kvc, doc)
    return dq, jnp.concatenate([dk, dv], axis=1)


FULL_SPECS = {
    "w_in": ("col", D_MODEL, IN_WIDTH),
    "w_branch_a": ("col", ATT_OUT, D_MODEL),
    "w_branch_b": ("col", HG_WIDTH, D_MODEL),
    "w_out": ("row", D_MODEL, D_MODEL),
    "wq_cross": ("row", D_MODEL, CROSS_WIDTH),
    "wkv_cross": ("row", D_MODEL, 2 * CROSS_WIDTH),
    "wo_cross": ("col", CROSS_WIDTH, D_MODEL),
    "w13": ("col", D_MODEL, 2 * D_FF),
    "w2": ("row", D_FF, D_MODEL),
}
WEIGHT_PLACE = {
    "w_in": ("w_in", 0), "w_branch_a": ("w_branch_a", 0), "w_branch_b": ("w_branch_b", 0),
    "w_out": ("w_out", 0), "wq_cross": ("wq_cross", 0), "wkv_cross": ("wkv_cross", 0),
    "wo_cross": ("wo_cross", 0), "w1": ("w13", 0), "w3": ("w13", FF_SHARD), "w2": ("w2", 0),
}
BIG_WEIGHTS = tuple(WEIGHT_PLACE)
EW_BLOCK_ELEMS = 512 * 1024


def _position():
    return lax.axis_index("x"), lax.axis_index("y"), lax.axis_index("c")


def _other_chips(x, y):
    return [(1 - x, y), (x, 1 - y), (1 - x, 1 - y)]


def _half(ref, kind, h):
    r, c = ref.shape
    if kind == "col":
        return ref.at[pl.ds(h * (r // 2), r // 2), :]
    return ref.at[:, pl.ds(h * (c // 2), c // 2)]


def _shard_of(ref, kind, start, size):
    return ref.at[:, pl.ds(start, size)] if kind == "col" else ref.at[pl.ds(start, size), :]


def _rows_of(ref, r0, nrows):
    return ref if nrows is None else ref.at[pl.ds(r0, nrows), :]


def _half_shape(kind, rows, cols):
    return (rows // 2, cols) if kind == "col" else (rows, cols // 2)


def _slot_shape(spec):
    kind, rows, cols = spec
    hr, hc = _half_shape(kind, rows, cols)
    return (hr, hc // N_CHIPS) if kind == "col" else (hr // N_CHIPS, hc)


def _remote(src, dst, send_sem, recv_sem, device):
    return pltpu.make_async_remote_copy(src_ref=src, dst_ref=dst, send_sem=send_sem, recv_sem=recv_sem,
                                        device_id=device, device_id_type=MESH)


def _gather_ici_comm(fulls, jobs, specs):
    def piece(refs, job, chip, c):
        f, r0, nr = job
        kind, rows, cols = specs[f]
        stride = (cols if kind == "col" else rows) // N_CHIPS
        return _rows_of(_half(_shard_of(refs[f], kind, chip * stride, stride), kind, c), r0, nr)

    def start(refs, ss, rs):
        x, y, c = _position()
        j = 2 * x + y
        for q, job in enumerate(jobs):
            for p, (px, py) in enumerate(_other_chips(x, y)):
                _remote(piece(refs, job, j, c), piece(refs, job, j, c), ss.at[3 * q + p], rs.at[3 * q + p],
                        (px, py, c)).start()

    def finish(refs, ss, rs):
        x, y, c = _position()
        j = 2 * x + y
        for q, job in enumerate(jobs):
            for p, (px, py) in enumerate(_other_chips(x, y)):
                _remote(piece(refs, job, j, c), piece(refs, job, 2 * px + py, c), ss.at[3 * q + p],
                        rs.at[3 * q + p], (px, py, c)).wait_recv()
        for q, job in enumerate(jobs):
            for p, (px, py) in enumerate(_other_chips(x, y)):
                _remote(piece(refs, job, j, c), piece(refs, job, j, c), ss.at[3 * q + p], rs.at[3 * q + p],
                        (px, py, c)).wait_send()

    names = list(dict.fromkeys(job[0] for job in jobs))
    return _Carried({f: fulls[f] for f in names}, {}, 3 * len(jobs), start, finish)


def _gather_ring_comm(fulls, f, r0, nr, phase, specs):
    kind, _, cols = specs[f]
    assert kind == "col" and nr % 32 == 0
    stride = cols // N_CHIPS
    half = nr // 2

    def rows(refs, chip, c, lo, n):
        return _rows_of(_half(_shard_of(refs[f], kind, chip * stride, stride), kind, c), r0 + lo, n)

    def copies(refs, ss, rs):
        x, y, c = _position()
        me, nx, ny, dg = 2 * x + y, 2 * (1 - x) + y, 2 * x + (1 - y), 2 * (1 - x) + (1 - y)
        to_x, to_y = (1 - x, y, c), (x, 1 - y, c)
        if phase == "a":
            mine = rows(refs, me, c, 0, nr)
            return [(_remote(mine, mine, ss.at[0], rs.at[0], to_x), rows(refs, nx, c, 0, nr)),
                    (_remote(mine, mine, ss.at[1], rs.at[1], to_y), rows(refs, ny, c, 0, nr))]
        up, low = rows(refs, ny, c, half, half), rows(refs, nx, c, 0, half)
        return [(_remote(up, up, ss.at[0], rs.at[0], to_x), rows(refs, dg, c, half, half)),
                (_remote(low, low, ss.at[1], rs.at[1], to_y), rows(refs, dg, c, 0, half))]

    def start(refs, ss, rs):
        for cp, _ in copies(refs, ss, rs):
            cp.start()

    def finish(refs, ss, rs):
        x, y, c = _position()
        mine = copies(refs, ss, rs)
        for i, (_, landing) in enumerate(mine):
            _remote(landing, landing, ss.at[i], rs.at[i], (x, y, c)).wait_recv()
        for cp, _ in mine:
            cp.wait_send()

    return _Carried({f: fulls[f]}, {}, 2, start, finish)


def _gather_d2d_comm(fulls, jobs, specs):
    def rect(refs, job, h):
        f, r0, nr = job
        assert nr is None or specs[f][0] == "col"
        return _rows_of(_half(refs[f], specs[f][0], h), r0, nr)

    def start(refs, ss, rs):
        x, y, c = _position()
        for q, job in enumerate(jobs):
            _remote(rect(refs, job, c), rect(refs, job, c), ss.at[q], rs.at[q], (x, y, 1 - c)).start()

    def finish(refs, ss, rs):
        x, y, c = _position()
        for q, job in enumerate(jobs):
            _remote(rect(refs, job, 1 - c), rect(refs, job, 1 - c), ss.at[q], rs.at[q], (x, y, 1 - c)).wait_recv()
        for q, job in enumerate(jobs):
            _remote(rect(refs, job, c), rect(refs, job, c), ss.at[q], rs.at[q], (x, y, 1 - c)).wait_send()

    names = list(dict.fromkeys(job[0] for job in jobs))
    return _Carried({f: fulls[f] for f in names}, {}, len(jobs), start, finish)


def _pairx_comm(grads, names, specs):
    def copies(refs, ss, rs):
        x, y, c = _position()
        return [_remote(_half(refs[("g", f)], specs[f][0], 1 - c), refs[("r", f)], ss.at[i], rs.at[i], (x, y, 1 - c))
                for i, f in enumerate(names)]

    def start(refs, ss, rs):
        for cp in copies(refs, ss, rs):
            cp.start()

    def finish(refs, ss, rs):
        for cp in copies(refs, ss, rs):
            cp.wait_recv()
        for cp in copies(refs, ss, rs):
            cp.wait_send()

    fresh = {("r", f): jax.ShapeDtypeStruct(_half_shape(*specs[f]), BF16) for f in names}
    return _Carried({}, fresh, len(names), start, finish, reads={("g", f): grads[f] for f in names})


def _chipx_comm(pair_sums, slots, jobs, specs):
    def copies(refs, ss, rs):
        x, y, c = _position()
        out = []
        for q, (f, r0, nr) in enumerate(jobs):
            kind = specs[f][0]
            width = _slot_shape(specs[f])[1 if kind == "col" else 0]
            for p, (px, py) in enumerate(_other_chips(x, y)):
                src = _rows_of(_shard_of(refs[("p", f)], kind, (2 * px + py) * width, width), r0, nr)
                dst = _rows_of(refs[("s", f)].at[p], r0, nr)
                out.append(_remote(src, dst, ss.at[3 * q + p], rs.at[3 * q + p], (px, py, c)))
        return out

    def start(refs, ss, rs):
        for cp in copies(refs, ss, rs):
            cp.start()

    def finish(refs, ss, rs):
        for cp in copies(refs, ss, rs):
            cp.wait_recv()
        for cp in copies(refs, ss, rs):
            cp.wait_send()

    names = list(dict.fromkeys(job[0] for job in jobs))
    arrays = {("p", f): pair_sums[f] for f in names}
    arrays.update({("s", f): slots[f] for f in names})
    return _Carried(arrays, {}, 3 * len(jobs), start, finish)


def _share_comm(grads, wnames, specs, place):
    def start(refs, ss, rs):
        x, y, c = _position()
        for i, w in enumerate(wnames):
            kind = specs[place[w][0]][0]
            _remote(_half(refs[w], kind, c), _half(refs[w], kind, c), ss.at[i], rs.at[i], (x, y, 1 - c)).start()

    def finish(refs, ss, rs):
        x, y, c = _position()
        for i, w in enumerate(wnames):
            kind = specs[place[w][0]][0]
            _remote(_half(refs[w], kind, 1 - c), _half(refs[w], kind, 1 - c), ss.at[i], rs.at[i],
                    (x, y, 1 - c)).wait_recv()
        for i, w in enumerate(wnames):
            kind = specs[place[w][0]][0]
            _remote(_half(refs[w], kind, c), _half(refs[w], kind, c), ss.at[i], rs.at[i], (x, y, 1 - c)).wait_send()

    return _Carried({w: grads[w] for w in wnames}, {}, len(wnames), start, finish)


def _gather_rows(v, name="gather_small"):
    shape = v.shape

    def body(v_ref, out_ref, send_sem, recv_sem, loc_sem):
        x, y, c = _position()
        me = 4 * x + 2 * y + c
        flips = [(fx, fy, fc) for fx in (0, 1) for fy in (0, 1) for fc in (0, 1)][1:]

        def peer(fl):
            return tuple(1 - a if f else a for a, f in zip((x, y, c), fl))

        loc = pltpu.make_async_copy(v_ref, out_ref.at[me], loc_sem)
        loc.start()
        sends = []
        for i, fl in enumerate(flips):
            cp = _remote(v_ref, out_ref.at[me], send_sem.at[i], recv_sem.at[i], peer(fl))
            cp.start()
            sends.append(cp)
        for i, fl in enumerate(flips):
            px, py, pc = peer(fl)
            _remote(v_ref, out_ref.at[4 * px + 2 * py + pc], send_sem.at[i], recv_sem.at[i], peer(fl)).wait_recv()
        for cp in sends:
            cp.wait_send()
        loc.wait()

    return pl.pallas_call(
        body, name=name, in_specs=[ANY], out_specs=ANY,
        out_shape=jax.ShapeDtypeStruct((N_DEV,) + shape, F32),
        scratch_shapes=[pltpu.SemaphoreType.DMA((N_DEV - 1,)), pltpu.SemaphoreType.DMA((N_DEV - 1,)),
                        pltpu.SemaphoreType.DMA],
    )(v)


def _ew_block(rows, cols, elems=EW_BLOCK_ELEMS):
    tc = cols if cols <= 4096 else _div(cols, 2048, LANE)
    tr = _div(rows, max(16, elems // tc), 16)
    return tr, tc


def _mesh_scalars():
    x, y, c = _position()
    return jnp.stack([c, 2 * x + y]).astype(jnp.int32)


def _grid_spec(grid, in_specs, out_specs):
    return pltpu.PrefetchScalarGridSpec(num_scalar_prefetch=1, grid=grid, in_specs=in_specs, out_specs=out_specs)


def _cast_into_full(parts, fname, pos, specs, place, name, token=None):
    kind, rows, cols = specs[fname]
    ws = [w for w in place if place[w][0] == fname]
    if kind == "col":
        stride = cols // N_CHIPS
        hr = rows // 2
        tr = _div(hr, max(16, EW_BLOCK_ELEMS // stride), 16)
        nrb = hr // tr
        in_specs = [pl.BlockSpec((tr, parts[w].shape[1]), lambda i, pos_ref: (i + pos_ref[0] * nrb, 0)) for w in ws]
        out_spec = pl.BlockSpec((tr, stride), lambda i, pos_ref: (i + pos_ref[0] * nrb, pos_ref[1]))
    else:
        stride = rows // N_CHIPS
        hc = cols // 2
        tr = _div(stride, max(16, EW_BLOCK_ELEMS // hc), 16)
        nrb = stride // tr
        in_specs = [pl.BlockSpec((tr, hc), lambda i, pos_ref: (i, pos_ref[0])) for w in ws]
        out_spec = pl.BlockSpec((tr, hc), lambda i, pos_ref: (i + pos_ref[1] * nrb, pos_ref[0]))

    def kern(pos_ref, *refs):
        o_ref = refs[-1]
        for w, r in zip(ws, refs[:len(ws)]):
            off = place[w][1] if kind == "col" else 0
            o_ref[:, off:off + r.shape[1]] = r[...].astype(o_ref.dtype)

    tokens = [] if token is None else [token]
    in_specs = in_specs + [pl.BlockSpec(TOKEN_SHAPE, lambda i, pos_ref: (0, 0))] * len(tokens)
    return pl.pallas_call(
        kern, name=name, grid_spec=_grid_spec((nrb,), in_specs, out_spec),
        out_shape=jax.ShapeDtypeStruct((rows, cols), BF16),
        compiler_params=_cparams(("parallel",)),
    )(pos, *[parts[w] for w in ws], *tokens)


def _pair_sum(grad, recv, pos, spec, name):
    kind, rows, cols = spec
    hr, hc = _half_shape(kind, rows, cols)
    tr, tc = _ew_block(hr, hc, 2 * EW_BLOCK_ELEMS)
    nrb, ncb = hr // tr, hc // tc
    blk = pl.BlockSpec((tr, tc), lambda i, jj, pos_ref: (i, jj))
    if kind == "col":
        mine = pl.BlockSpec((tr, tc), lambda i, jj, pos_ref: (i + pos_ref[0] * nrb, jj))
    else:
        mine = pl.BlockSpec((tr, tc), lambda i, jj, pos_ref: (i, jj + pos_ref[0] * ncb))

    def kern(pos_ref, g_ref, r_ref, o_ref, slots_ref):
        o_ref[...] = (g_ref[...].astype(F32) + r_ref[...].astype(F32)).astype(o_ref.dtype)

    return pl.pallas_call(
        kern, name=name, grid_spec=_grid_spec((nrb, ncb), [mine, blk], [blk, ANY]),
        out_shape=[jax.ShapeDtypeStruct((hr, hc), BF16),
                   jax.ShapeDtypeStruct((N_CHIPS - 1,) + _slot_shape(spec), BF16)],
        compiler_params=_cparams(("parallel", "parallel")),
    )(pos, grad, recv)


def _chip_sum(pair_sum, slots, pos, fname, shard_shapes, specs, place, name):
    kind, rows, cols = specs[fname]
    sr, sc = _slot_shape(specs[fname])
    ws = [w for w in place if place[w][0] == fname]
    n_slots = N_CHIPS - 1
    tr = _div(sr, max(16, EW_BLOCK_ELEMS // sc), 16)
    nrb = sr // tr
    slot = pl.BlockSpec((n_slots, tr, sc), lambda i, pos_ref: (0, i, 0))
    if kind == "col":
        own = pl.BlockSpec((tr, sc), lambda i, pos_ref: (i, pos_ref[1]))
        out_specs = [pl.BlockSpec((tr, shard_shapes[w][1]), lambda i, pos_ref: (i + pos_ref[0] * nrb, 0)) for w in ws]
    else:
        own = pl.BlockSpec((tr, sc), lambda i, pos_ref: (i + pos_ref[1] * nrb, 0))
        out_specs = [pl.BlockSpec((tr, sc), lambda i, pos_ref: (i, pos_ref[0])) for w in ws]

    def kern(pos_ref, own_ref, slot_ref, *out_refs):
        tot = own_ref[...].astype(F32)
        for s in range(n_slots):
            tot = tot + slot_ref[s].astype(F32)
        for w, o_ref in zip(ws, out_refs):
            off = place[w][1] if kind == "col" else 0
            o_ref[...] = tot[:, off:off + o_ref.shape[1]]

    outs = pl.pallas_call(
        kern, name=name, grid_spec=_grid_spec((nrb,), [own, slot], out_specs),
        out_shape=[jax.ShapeDtypeStruct(shard_shapes[w], F32) for w in ws],
        compiler_params=_cparams(("parallel",)),
    )(pos, pair_sum, slots)
    return dict(zip(ws, outs))


def _adam_math(w, g, m, v):
    m2 = ADAM_B1 * m + (1.0 - ADAM_B1) * g
    v2 = ADAM_B2 * v + (1.0 - ADAM_B2) * (g * g)
    m_hat = m2 / (1.0 - ADAM_B1 ** ADAM_STEP)
    v_hat = v2 / (1.0 - ADAM_B2 ** ADAM_STEP)
    delta = -ADAM_LR * (m_hat / (jnp.sqrt(v_hat) + ADAM_EPS) + ADAM_WD * w)
    return delta, m2, v2


def _adamw(w, g, m, v, name, carried=()):
    rows, cols = w.shape
    tr, tc = _ew_block(rows, cols)

    def kern(w_ref, g_ref, m_ref, v_ref, d_ref, m2_ref, v2_ref, g_out_ref):
        g_ = g_ref[...]
        d_ref[...], m2_ref[...], v2_ref[...] = _adam_math(w_ref[...], g_, m_ref[...], v_ref[...])
        g_out_ref[...] = g_

    blk = pl.BlockSpec((tr, tc), lambda i, j: (i, j))
    return _pcall(
        kern, name=name, grid=(rows // tr, cols // tc), in_specs=[blk] * 4, out_specs=[blk] * 4,
        out_shape=[jax.ShapeDtypeStruct((rows, cols), F32)] * 4, args=(w, g, m, v),
        semantics=("parallel", "parallel"), carried=carried)


SMALL_ROWS = ("ln_mix_w", "ln_cross_w", "ln_mem_w", "ln_ffn_w", "ln_final_w")
ROW_HG_NORM, ROW_LB0, ROW_LB1 = 5, 6, 7
LOSS_LANE0 = HEAD_DIM


def _pack_small(vals):
    rows = [vals[n].reshape(1, D_MODEL) for n in SMALL_ROWS]
    pad = lambda a: jnp.pad(a, ((0, 0), (0, D_MODEL - a.shape[1])))
    rows.append(pad(vals["hg_norm_w"].reshape(1, HEAD_DIM)))
    rows.append(pad(vals["hg_lower_bounds"].reshape(2, HG_WIDTH)))
    return jnp.concatenate(rows, axis=0)


def _small_update(gathered, w, m, v, name="small_update"):
    def kern(g_ref, w_ref, m_ref, v_ref, grad_ref, d_ref, m2_ref, v2_ref, loss_ref):
        tot = g_ref[0]
        for s in range(1, N_DEV):
            tot = tot + g_ref[s]
        wv = w_ref[...]
        row = lax.broadcasted_iota(jnp.int32, (8, D_MODEL), 0)
        lane = lax.broadcasted_iota(jnp.int32, (8, D_MODEL), 1)
        l0, l1 = wv[ROW_LB0:ROW_LB0 + 1], wv[ROW_LB1:ROW_LB1 + 1]
        mx = jnp.maximum(l0, l1)
        e0, e1 = jnp.exp(l0 - mx), jnp.exp(l1 - mx)
        p0 = e0 / (e0 + e1)
        dlog = tot[ROW_LB0:ROW_LB0 + 1] * p0 * (1.0 - p0)
        tot = jnp.where(row == ROW_HG_NORM, tot + tot[ROW_LB1:ROW_LB1 + 1], tot)
        grad = jnp.where(row == ROW_LB0, dlog, jnp.where(row == ROW_LB1, -dlog, tot))
        grad = jnp.where((row == ROW_HG_NORM) & (lane >= HEAD_DIM), 0.0, grad)
        grad = jnp.where((row >= ROW_LB0) & (lane >= HG_WIDTH), 0.0, grad)
        grad_ref[...] = grad
        d_ref[...], m2_ref[...], v2_ref[...] = _adam_math(wv, grad, m_ref[...], v_ref[...])
        loss_ref[...] = tot[ROW_HG_NORM:ROW_HG_NORM + 1, LOSS_LANE0:LOSS_LANE0 + LANE]

    full = pl.BlockSpec((8, D_MODEL), lambda: (0, 0))
    return pl.pallas_call(
        kern, name=name,
        in_specs=[pl.BlockSpec((N_DEV, 8, D_MODEL), lambda: (0, 0, 0)), full, full, full],
        out_specs=[full, full, full, full, pl.BlockSpec((1, LANE), lambda: (0, 0))],
        out_shape=[jax.ShapeDtypeStruct((8, D_MODEL), F32)] * 4 + [jax.ShapeDtypeStruct((1, LANE), F32)],
        compiler_params=_cparams(),
    )(gathered, w, m, v)


def _unpack_small(p, shapes):
    out = {n: p[i].reshape(shapes[n]) for i, n in enumerate(SMALL_ROWS)}
    out["hg_norm_w"] = p[ROW_HG_NORM, :HEAD_DIM].reshape(shapes["hg_norm_w"])
    out["hg_lower_bounds"] = p[ROW_LB0:ROW_LB1 + 1, :HG_WIDTH].reshape(shapes["hg_lower_bounds"])
    return out


WHOLE = lambda f: (f, 0, None)
MID_MATRICES = ("w_branch_a", "w_branch_b", "w_out", "wq_cross", "wkv_cross", "wo_cross")
MID_WEIGHTS = MID_MATRICES
W_IN_PIECES = [("w_in", r0, 512) for r0 in range(0, D_MODEL // 2, 512)]
GATHER_GROUPS = [("mid", [WHOLE(f) for f in MID_MATRICES]), ("w13", [WHOLE("w13")]), ("w2", [WHOLE("w2")])]
OTHER_WEIGHTS = ["w1", "w3", "w2"] + list(MID_WEIGHTS)
BEFORE = {
    "hgrn_fwd": [("wait", "mid")],
    "mm_out": [("wait", "w13")],
    "mm_w13": [("wait", "w2")],
    "mm_dh": [("wait", "rs_w2"), ("chip_sum", "w2"), ("wait", "rs_w13"), ("chip_sum", "w13"), ("wait", "rs_mid")]
    + [("chip_sum", f) for f in MID_MATRICES],
}
CARRY = {
    "hgrn_fwd": [("d2d", [WHOLE(f) for f in MID_MATRICES])],
    "mm_out": [("d2d", [WHOLE("w13")])],
    "mm_w13": [("d2d", [WHOLE("w2")])],
    "swiglu_bwd": [("pairx", ["w2"])],
    "mm_dhf": [("pairx", ["w13"])],
    "attn_merge_bwd": [("pairx", list(MID_MATRICES))],
    "mm_dh": [("share", OTHER_WEIGHTS)],
}
AFTER = {
    "swiglu_bwd": [("pair_sum", "w2"), ("start", "rs_w2", [WHOLE("w2")])],
    "mm_dhf": [("pair_sum", "w13"), ("start", "rs_w13", [WHOLE("w13")])],
    "attn_merge_bwd": [("pair_sum", f) for f in MID_MATRICES] + [("start", "rs_mid", [WHOLE(f) for f in MID_MATRICES])],
    "mm_dwin": [("run", ("pairx", ["w_in"]), "rs_pair_exchange_w_in"), ("pair_sum", "w_in"),
                ("start", "rs_w_in", [WHOLE("w_in")])],
}
FINISH = [
    ("adamw", OTHER_WEIGHTS), ("wait", "rs_w_in"), ("small",), ("chip_sum", "w_in"),
    ("run", ("share", ["w_in"]), "rs_sibling_share_w_in"), ("adamw", ["w_in"]),
]


class _Net:
    def __init__(self, full, pos=None, shard_shapes=None, comm=True, specs=FULL_SPECS, place=WEIGHT_PLACE):
        self.full, self.pos, self.shard_shapes, self.comm = dict(full), pos, shard_shapes, comm
        self.specs, self.place = specs, place
        self.gw, self.recv, self.psum, self.slots, self.grads = {}, {}, {}, {}, {}
        self.pending, self.token, self.last = {}, None, None

    def _make(self, kind, arg):
        if kind == "gather":
            return _gather_ici_comm(self.full, arg, self.specs)
        if kind == "ring":
            return _gather_ring_comm(self.full, *arg, self.specs)
        if kind == "d2d":
            return _gather_d2d_comm(self.full, arg, self.specs)
        if kind == "pairx":
            return _pairx_comm(self.gw, arg, self.specs)
        if kind == "chipx":
            return _chipx_comm(self.psum, self.slots, arg, self.specs)
        assert kind == "share"
        return _share_comm(self.grads, arg, self.specs, self.place)

    def _store(self, kind, res):
        if kind in ("gather", "ring", "d2d"):
            self.full.update(res)
        elif kind == "pairx":
            for (tag, f), a in res.items():
                (self.gw if tag == "g" else self.recv)[f] = a
        elif kind == "chipx":
            for (tag, f), a in res.items():
                (self.psum if tag == "p" else self.slots)[f] = a
        else:
            self.grads.update(res)

    def run_comm(self, item, name):
        kind, arg = item
        self._store(kind, _run_comm([self._make(kind, arg)], name)[0])

    @staticmethod
    def _others(after, items):
        own = [a for cm in items for a in cm.arrays.values()]
        return [a for a in after if a is not None and all(a is not o for o in own)]

    def start(self, groups, kind, name):
        items = [self._make(kind, jobs) for _, jobs in groups]
        after = self._others([self.last], items)
        res, sems, token = _split_start(items, name, after=after[0] if after else None)
        for (group, jobs), r, s in zip(groups, res, sems):
            self._store(kind, r)
            self.pending[group] = (kind, jobs, s)
        self.token = self.last = token

    def wait(self, group, after=()):
        kind, jobs, sems = self.pending.pop(group)
        item = self._make(kind, jobs)
        res = _split_wait([item], [sems], self._others([self.last, *after], [item]), f"wait_{group}")[0]
        self._store(kind, res)

    def step(self, step):
        if step[0] == "wait":
            self.wait(step[1])
        elif step[0] == "start":
            self.start([(step[1], step[2])], "chipx", f"start_{step[1]}")
        elif step[0] == "pair_sum":
            f = step[1]
            self.psum[f], self.slots[f] = _pair_sum(self.gw[f], self.recv[f], self.pos, self.specs[f],
                                                    f"rs_pair_sum_{f}")
        elif step[0] == "chip_sum":
            f = step[1]
            self.grads.update(_chip_sum(self.psum[f], self.slots[f], self.pos, f, self.shard_shapes,
                                        self.specs, self.place, f"rs_chip_sum_{f}"))
        else:
            assert step[0] == "run"
            self.run_comm(step[1], step[2])

    def call(self, fn, name, *args, grad_of=None, **kw):
        for step in (BEFORE.get(name, []) if self.comm else []):
            self.step(step)
        items = CARRY.get(name, []) if self.comm else []
        carried = [self._make(k, a) for k, a in items]
        if self.token is not None:
            carried.append(_Token(self.token))
            self.token = None
        out, res = fn(*args, name=name, carried=carried, **kw)
        if grad_of is not None:
            self.gw[grad_of] = out
        self.last = jax.tree.leaves(out)[0]
        for (kind, _), r in zip(items, res):
            self._store(kind, r)
        for step in (AFTER.get(name, []) if self.comm else []):
            self.step(step)
        return out


def _local_step(net, x, h, mem, target, small):
    full, call = net.full, net.call
    proj = call(_mm, "mm_proj", h, full["w_in"], mode="nn", out_dtype=F32)
    att = [call(_attn_fwd, f"attn_fwd_g{g}", proj, g) for g in range(3)]
    outs, lses = [a[0] for a in att], [a[1] for a in att]
    o_att = _attn_merge_fwd(outs, lses, "attn_merge")
    oraw, o_hg, states = call(_hg_fwd, "hgrn_fwd", proj, small["hg_lower_bounds"], small["hg_norm_w"])
    ya = call(_mm, "mm_branch_a", o_att, full["w_branch_a"], mode="nn", out_dtype=F32)
    yb = call(_mm, "mm_branch_b", o_hg, full["w_branch_b"], mode="nn", out_dtype=F32)
    merged = call(_gate_fwd, "gate_fwd", proj, ya, yb)
    x1 = call(_mm, "mm_out", merged, full["w_out"], mode="nn", out_dtype=F32, res=x)

    hc = _rms_fwd(x1, small["ln_cross_w"], "rms_cross")
    mn = _rms_fwd(mem, small["ln_mem_w"], "rms_mem")
    qc = call(_mm, "mm_q", hc, full["wq_cross"], mode="nn", out_dtype=F32)
    kvc = call(_mm, "mm_kv", mn, full["wkv_cross"], mode="nn", out_dtype=F32)
    oc = _cross_fwd(qc, kvc, "cross_fwd")
    x2 = call(_mm, "mm_o", oc, full["wo_cross"], mode="nn", out_dtype=F32, res=x1)

    hf = _rms_fwd(x2, small["ln_ffn_w"], "rms_ffn")
    ab = call(_mm, "mm_w13", hf, full["w13"], mode="nn", out_dtype=F32)
    u = _swiglu_fwd(ab, "swiglu_fwd")
    x3 = call(_mm, "mm_w2", u, full["w2"], mode="nn", out_dtype=F32, res=x2)

    dx3, dg_final, loss = _loss_head(x3, small["ln_final_w"], target, "loss_head")

    gs = {"ln_final_w": dg_final}
    du = call(_mm, "mm_du", dx3, full["w2"], mode="nt", out_dtype=F32)
    call(_mm, "mm_dw2", u, dx3, mode="tn", out_dtype=BF16, grad_of="w2")
    dab = call(_swiglu_bwd, "swiglu_bwd", ab, du)
    call(_mm, "mm_dw13", hf, dab, mode="tn", out_dtype=BF16, grad_of="w13")
    dhf = call(_mm, "mm_dhf", dab, full["w13"], mode="nt", out_dtype=F32)
    dx2, gs["ln_ffn_w"] = _rms_bwd(x2, small["ln_ffn_w"], dhf, dx3, "rms_ffn_bwd")
    doc = call(_mm, "mm_doc", dx2, full["wo_cross"], mode="nt", out_dtype=BF16)
    call(_mm, "mm_dwo", oc, dx2, mode="tn", out_dtype=BF16, grad_of="wo_cross")
    dqc, dkvc = _cross_bwd(qc, kvc, doc, "cross_bwd")
    call(_mm, "mm_dwq", hc, dqc, mode="tn", out_dtype=BF16, grad_of="wq_cross")
    dhc = call(_mm, "mm_dhc", dqc, full["wq_cross"], mode="nt", out_dtype=F32)
    call(_mm, "mm_dwkv", mn, dkvc, mode="tn", out_dtype=BF16, grad_of="wkv_cross")
    dmn = call(_mm, "mm_dmn", dkvc, full["wkv_cross"], mode="nt", out_dtype=F32)
    _, gs["ln_mem_w"] = _rms_bwd(mem, small["ln_mem_w"], dmn, None, "rms_mem_bwd")
    dx1, gs["ln_cross_w"] = _rms_bwd(x1, small["ln_cross_w"], dhc, dx2, "rms_cross_bwd")
    dmerged = call(_mm, "mm_dmerged", dx1, full["w_out"], mode="nt", out_dtype=F32)
    call(_mm, "mm_dwout", merged, dx1, mode="tn", out_dtype=BF16, grad_of="w_out")
    dya, dyb, dga, dgb = call(_gate_bwd, "gate_bwd", proj, ya, yb, dmerged)
    call(_mm, "mm_dwa", o_att, dya, mode="tn", out_dtype=BF16, grad_of="w_branch_a")
    do_att = call(_mm, "mm_doatt", dya, full["w_branch_a"], mode="nt", out_dtype=F32)
    call(_mm, "mm_dwb", o_hg, dyb, mode="tn", out_dtype=BF16, grad_of="w_branch_b")
    do_hg = call(_mm, "mm_dohg", dyb, full["w_branch_b"], mode="nt", out_dtype=F32)
    dqh, dfh, dih, dgh, dlb, gs["hg_norm_w"] = call(
        _hg_bwd, "hgrn_bwd", proj, small["hg_lower_bounds"], small["hg_norm_w"], oraw, states, do_hg)
    gs["hg_lb"] = dlb
    do_gs, dl_gs = call(_attn_merge_bwd, "attn_merge_bwd", outs, lses, do_att)
    dqs, dks, dvs = zip(*[call(_attn_bwd, f"attn_bwd_g{g}", proj, g, lses[g], do_gs[g], dl_gs[g]) for g in range(3)])
    dproj = jnp.concatenate([*dqs, *dks, *dvs, dqh, dfh, dih, dgh, dga, dgb], axis=1)
    call(_mm, "mm_dwin", h, dproj, mode="tn", out_dtype=BF16, grad_of="w_in")
    dh = call(_mm, "mm_dh", dproj, full["w_in"], mode="nt", out_dtype=F32)
    dx, gs["ln_mix_w"] = _rms_bwd(x, small["ln_mix_w"], dh, dx1, "rms_mix_bwd")
    return loss, dx, gs


WEIGHT_ORDER = ("ln_mix_w", "w_in", "hg_norm_w", "hg_lower_bounds", "w_branch_a", "w_branch_b", "w_out",
                "ln_cross_w", "ln_mem_w", "wq_cross", "wkv_cross", "wo_cross", "ln_ffn_w", "w1", "w3", "w2",
                "ln_final_w")


def kernel(x, mem, ln_mix_w, w_in, hg_norm_w, hg_lower_bounds, w_branch_a, w_branch_b, w_out, ln_cross_w, ln_mem_w, wq_cross, wkv_cross, wo_cross, ln_ffn_w, w1, w3, w2, ln_final_w, loss_target, m_ln_mix_w, m_w_in, m_hg_norm_w, m_hg_lower_bounds, m_w_branch_a, m_w_branch_b, m_w_out, m_ln_cross_w, m_ln_mem_w, m_wq_cross, m_wkv_cross, m_wo_cross, m_ln_ffn_w, m_w1, m_w3, m_w2, m_ln_final_w, v_ln_mix_w, v_w_in, v_hg_norm_w, v_hg_lower_bounds, v_w_branch_a, v_w_branch_b, v_w_out, v_ln_cross_w, v_ln_mem_w, v_wq_cross, v_wkv_cross, v_wo_cross, v_ln_ffn_w, v_w1, v_w3, v_w2, v_ln_final_w):
    args = dict(locals())
    w = {n: args[n] for n in WEIGHT_ORDER}
    m = {n: args["m_" + n] for n in WEIGHT_ORDER}
    v = {n: args["v_" + n] for n in WEIGHT_ORDER}
    shapes = {n: w[n].shape for n in WEIGHT_ORDER}
    mat = lambda a: a.reshape(a.shape[-2:])
    shard_shapes = {n: shapes[n][-2:] for n in BIG_WEIGHTS}

    pos = _mesh_scalars()

    def cast(f, token=None):
        return _cast_into_full({n: mat(w[n]) for n in BIG_WEIGHTS if WEIGHT_PLACE[n][0] == f}, f, pos,
                               FULL_SPECS, WEIGHT_PLACE, f"cast_{f}", token)

    net = _Net({"w_in": cast("w_in")}, pos, shard_shapes)
    net.start([(f"ring_a{i}", (*job, "a")) for i, job in enumerate(W_IN_PIECES)], "ring", "gather_start_w_in")
    net.full.update({f: cast(f, net.token) for f in FULL_SPECS if f != "w_in"})
    for i, job in enumerate(W_IN_PIECES):
        net.wait(f"ring_a{i}")
        net.start([(f"ring_b{i}", (*job, "b"))], "ring", f"gather_pass_on_w_in{i}")
    net.start(GATHER_GROUPS, "gather", "gather_start_rest")
    small = {n: w[n].reshape(1, -1) for n in SMALL_ROWS}
    small["hg_norm_w"] = w["hg_norm_w"].reshape(1, HEAD_DIM)
    small["hg_lower_bounds"] = w["hg_lower_bounds"]
    x2d = x.reshape(SEQ, D_MODEL)
    h = _rms_fwd(x2d, small["ln_mix_w"], "rms_mix")
    for i, job in enumerate(W_IN_PIECES):
        net.wait(f"ring_b{i}")
        net.run_comm(("d2d", [job]), f"gather_hand_over_w_in{i}")
    loss, dx, gs = _local_step(net, x2d, h, mem.reshape(MEM_LEN, D_MODEL), loss_target.reshape(SEQ, D_MODEL), small)

    out_g, out_d, out_m, out_v = {}, {}, {}, {}
    net.last = dx
    for step in FINISH:
        if step[0] == "adamw":
            for n in step[1]:
                out_d[n], out_m[n], out_v[n], out_g[n] = net.call(_adamw, f"adamw_{n}", mat(w[n]), net.grads[n],
                                                                  mat(m[n]), mat(v[n]))
        elif step[0] == "wait":
            net.wait(step[1], after=list(out_d.values()))
        elif step[0] == "small":
            pad = lambda a: jnp.pad(a, ((0, 0), (0, D_MODEL - a.shape[1])))
            part = jnp.concatenate(
                [gs[n] for n in SMALL_ROWS]
                + [pad(jnp.concatenate([gs["hg_norm_w"][0], loss], axis=1)), pad(gs["hg_lb"]),
                   pad(gs["hg_norm_w"][1]) if len(gs["hg_norm_w"]) > 1 else jnp.zeros((1, D_MODEL), F32)], axis=0)
            part, net.psum["w_in"] = lax.optimization_barrier((part, net.psum["w_in"]))
            sg, sd, sm, sv, loss_tot = _small_update(_gather_rows(part), _pack_small(w), _pack_small(m),
                                                     _pack_small(v))
            for dst, packed in ((out_g, sg), (out_d, sd), (out_m, sm), (out_v, sv)):
                dst.update(_unpack_small(packed, shapes))
        else:
            net.step(step)

    result = [loss_tot[0, 0], dx.reshape(x.shape)]
    for group in (out_g, out_d, out_m, out_v):
        result += [group[n].reshape(shapes[n]) for n in WEIGHT_ORDER]
    return tuple(result)
```

```python
import math

import jax
import jax.numpy as jnp
from jax import lax
from jax.experimental import pallas as pl
from jax.experimental.pallas import tpu as pltpu

F32 = jnp.float32
BF16 = jnp.bfloat16
MESH = pl.DeviceIdType.MESH

D_MODEL = 2048
SEQ = 2048
HEAD_DIM = 128
MEM_LEN = 256
ATT_GROUPS = ((128, 1), (512, 4), (2048, 16))
ATT_HEADS = 4
ATT_WIDTH = 3 * ATT_HEADS * HEAD_DIM
ATT_OUT = ATT_HEADS * HEAD_DIM
ATT_BLOCK = 128
HG_HEADS = 8
HG_WIDTH = HG_HEADS * HEAD_DIM
HG_CHUNK = 64
IN_WIDTH = 3 * ATT_WIDTH + 4 * HG_WIDTH + 2 * D_MODEL
CROSS_HEADS = 4
CROSS_WIDTH = CROSS_HEADS * HEAD_DIM
D_FF = 5632
RMS_EPS = 1e-6
ADAM_LR = 0.001
ADAM_B1 = 0.9
ADAM_B2 = 0.999
ADAM_EPS = 1e-08
ADAM_WD = 0.01
ADAM_STEP = 10
N_CHIPS = 4
N_DEV = 8

VMEM_LIMIT_BYTES = 56 * 1024 * 1024
LANE = 128
MXU_WIDTH = 256
MM_TILE_CAP = 1536
TRANSPOSE_CHUNK = 512
ANY = pl.BlockSpec(memory_space=pl.ANY)


def _cparams(sem=None):
    return pltpu.CompilerParams(dimension_semantics=sem, vmem_limit_bytes=VMEM_LIMIT_BYTES)


def _div(n, cap, mult):
    best = None
    for d in range(mult, min(n, cap) + 1, mult):
        if n % d == 0:
            best = d
    assert best is not None, (n, cap, mult)
    return best


def _sigmoid(x):
    return 1.0 / (1.0 + jnp.exp(-x))


def _dot(a, b):
    return jnp.dot(a.astype(BF16), b.astype(BF16), preferred_element_type=F32)


def _dot_nt(a, b):
    return lax.dot_general(a.astype(BF16), b.astype(BF16), (((1,), (1,)), ((), ())),
                           preferred_element_type=F32)


def _dot_tn(a, b):
    return jnp.dot(a.astype(F32).T.astype(BF16), b.astype(BF16), preferred_element_type=F32)


def _dot_exact(a, b):
    return jnp.dot(a, b, precision=lax.Precision.HIGHEST, preferred_element_type=F32)


class _Carried:
    def __init__(self, arrays, fresh, n_sems, start, finish, mid=None, reads=None):
        self.arrays, self.fresh, self.n_sems, self.reads = arrays, fresh, n_sems, reads or {}
        self.start, self.mid, self.finish = start, mid, finish


class _Token:
    def __init__(self, array):
        self.array = array


TOKEN_SHAPE = (8, LANE)


def _carried_layout(carried):
    akeys = list(dict.fromkeys(k for cm in carried for k in cm.arrays))
    fkeys = [(ci, k) for ci, cm in enumerate(carried) for k in cm.fresh]
    arrays = [next(cm.arrays[k] for cm in carried if k in cm.arrays) for k in akeys]
    shapes = [jax.ShapeDtypeStruct(a.shape, a.dtype) for a in arrays] + [carried[ci].fresh[k] for ci, k in fkeys]
    sems = []
    for cm in carried:
        sems += [pltpu.SemaphoreType.DMA((cm.n_sems,)), pltpu.SemaphoreType.DMA((cm.n_sems,))]
    return akeys, fkeys, arrays, shapes, sems


def _carried_reads(carried):
    rkeys = list(dict.fromkeys(k for cm in carried for k in cm.reads))
    return rkeys, [next(cm.reads[k] for cm in carried if k in cm.reads) for k in rkeys]


def _carried_results(carried, akeys, fkeys, outs, rkeys=(), read_refs=()):
    shared = dict(zip(akeys, outs[:len(akeys)]))
    shared.update(zip(rkeys, read_refs))
    res = [{k: shared[k] for k in list(cm.arrays) + [r for r in cm.reads if r in shared]} for cm in carried]
    for (ci, k), o in zip(fkeys, outs[len(akeys):]):
        res[ci][k] = o
    return res


def _pcall(kern, *, name, grid, in_specs, out_specs, out_shape, args, scratch_shapes=(), semantics=None,
           carried=()):
    tokens = [c.array for c in carried if isinstance(c, _Token)]
    carried = [c for c in carried if not isinstance(c, _Token)]
    single = not isinstance(out_shape, (list, tuple))
    out_specs = [out_specs] if single else list(out_specs)
    out_shape = [out_shape] if single else list(out_shape)
    n_real, n_out, n_scr = len(in_specs), len(out_shape), len(scratch_shapes)
    in_specs = list(in_specs) + [pl.BlockSpec(TOKEN_SHAPE, lambda *_: (0, 0))] * len(tokens)
    args = list(args) + tokens
    n_in = len(in_specs)
    if not carried:
        def plain(*refs):
            kern(*refs[:n_real], *refs[n_in:])

        outs = pl.pallas_call(plain if tokens else kern, name=name, grid=grid, in_specs=in_specs,
                              out_specs=out_specs, out_shape=out_shape, scratch_shapes=list(scratch_shapes),
                              compiler_params=_cparams(semantics))(*args)
        return (outs[0] if single else list(outs)), []
    akeys, fkeys, arrays, shapes, sems = _carried_layout(carried)
    rkeys, reads = _carried_reads(carried)
    n_a, n_f, n_r = len(akeys), len(fkeys), len(rkeys)
    total = math.prod(grid)
    mid_step = min(total - 1, (17 * total) // 20)

    def wrapped(*refs):
        ins = refs[:n_real]
        r0 = n_in + n_a
        o0 = r0 + n_r
        outs = refs[o0:o0 + n_out]
        a0 = o0 + n_out
        s0 = a0 + n_a + n_f
        per = _carried_results(carried, akeys, fkeys, refs[a0:s0], rkeys, refs[r0:o0])
        scratch = refs[s0:s0 + n_scr]
        sem = refs[s0 + n_scr:]
        step = 0
        for d, g in enumerate(grid):
            step = step * g + pl.program_id(d)

        @pl.when(step == 0)
        def _():
            for ci, cm in enumerate(carried):
                cm.start(per[ci], sem[2 * ci], sem[2 * ci + 1])

        kern(*ins, *outs, *scratch)

        @pl.when(step == mid_step)
        def _():
            for ci, cm in enumerate(carried):
                if cm.mid is not None:
                    cm.mid(per[ci], sem[2 * ci], sem[2 * ci + 1])

        @pl.when(step == total - 1)
        def _():
            for ci, cm in enumerate(carried):
                cm.finish(per[ci], sem[2 * ci], sem[2 * ci + 1])

    outs = pl.pallas_call(
        wrapped, name=name, grid=grid,
        in_specs=list(in_specs) + [ANY] * (n_a + n_r), out_specs=out_specs + [ANY] * (n_a + n_f),
        out_shape=out_shape + shapes,
        input_output_aliases={n_in + i: n_out + i for i in range(n_a)},
        scratch_shapes=list(scratch_shapes) + sems,
        compiler_params=_cparams(("arbitrary",) * len(grid)),
    )(*args, *arrays, *reads)
    res = _carried_results(carried, akeys, fkeys, outs[n_out:])
    return (outs[0] if single else list(outs[:n_out])), res


def _run_comm(carried, name):
    carried = list(carried)
    akeys, fkeys, arrays, shapes, sems = _carried_layout(carried)
    rkeys, reads = _carried_reads(carried)
    n_a, n_f, n_r = len(akeys), len(fkeys), len(rkeys)

    def body(*refs):
        o0 = n_a + n_r
        per = _carried_results(carried, akeys, fkeys, refs[o0:o0 + n_a + n_f], rkeys, refs[n_a:o0])
        sem = refs[o0 + n_a + n_f:]
        for hook in ("start", "mid", "finish"):
            for ci, cm in enumerate(carried):
                fn = getattr(cm, hook)
                if fn is not None:
                    fn(per[ci], sem[2 * ci], sem[2 * ci + 1])

    outs = pl.pallas_call(
        body, name=name, in_specs=[ANY] * (n_a + n_r), out_specs=[ANY] * (n_a + n_f), out_shape=shapes,
        input_output_aliases={i: i for i in range(n_a)}, scratch_shapes=sems,
    )(*arrays, *reads)
    return _carried_results(carried, akeys, fkeys, outs)


HBM_SPEC = pl.BlockSpec(memory_space=pltpu.HBM)
SEM_SPEC = pl.BlockSpec(memory_space=pltpu.SEMAPHORE)
SPLIT_EFFECT = pltpu.SideEffectType.DATAFLOW_SIDE_EFFECTING


def _in_hbm(a):
    return pltpu.with_memory_space_constraint(a, pltpu.HBM)


def _split_start(items, name, after=None):
    items = list(items)
    akeys, fkeys, arrays, shapes, sems = _carried_layout(items)
    assert not fkeys
    n_a, n_s = len(akeys), len(sems)
    n_in = n_a + (after is not None)

    def body(*refs):
        per = _carried_results(items, akeys, [], refs[n_in:n_in + n_a])
        sem = refs[n_in + n_a:n_in + n_a + n_s]
        for ci, cm in enumerate(items):
            cm.start(per[ci], sem[2 * ci], sem[2 * ci + 1])
        token = refs[n_in + n_a + n_s]
        token[...] = jnp.zeros_like(token)

    outs = pl.pallas_call(
        body, name=name, in_specs=[HBM_SPEC] * n_a + [ANY] * (after is not None),
        out_specs=[HBM_SPEC] * n_a + [SEM_SPEC] * n_s + [pl.BlockSpec(memory_space=pltpu.VMEM)],
        out_shape=[pltpu.HBM(s.shape, s.dtype) for s in shapes] + sems + [jax.ShapeDtypeStruct(TOKEN_SHAPE, F32)],
        input_output_aliases={i: i for i in range(n_a)},
        compiler_params=pltpu.CompilerParams(has_side_effects=SPLIT_EFFECT),
    )(*[_in_hbm(a) for a in arrays], *([after] if after is not None else []))
    res = _carried_results(items, akeys, [], outs[:n_a])
    sem_out = outs[n_a:n_a + n_s]
    return res, [(sem_out[2 * ci], sem_out[2 * ci + 1]) for ci in range(len(items))], outs[-1]


def _split_wait(items, sems, after, name):
    items = list(items)
    after = list(after) if isinstance(after, (list, tuple)) else [after]
    akeys, fkeys, arrays, shapes, _ = _carried_layout(items)
    n_a, n_s = len(akeys), 2 * len(items)

    def body(*refs):
        per = _carried_results(items, akeys, [], refs[n_a + n_s + len(after):])
        sem = refs[n_a:n_a + n_s]
        for ci, cm in enumerate(items):
            cm.finish(per[ci], sem[2 * ci], sem[2 * ci + 1])

    outs = pl.pallas_call(
        body, name=name, in_specs=[HBM_SPEC] * n_a + [SEM_SPEC] * n_s + [ANY] * len(after),
        out_specs=[HBM_SPEC] * n_a, out_shape=[pltpu.HBM(s.shape, s.dtype) for s in shapes],
        input_output_aliases={i: i for i in range(n_a)},
        compiler_params=pltpu.CompilerParams(has_side_effects=SPLIT_EFFECT),
    )(*arrays, *[s for pair in sems for s in pair], *after)
    return _carried_results(items, akeys, [], outs)


def _mm(a, b, *, mode, out_dtype, name, res=None, carried=()):
    if mode == "nn":
        (m, k), (k2, n) = a.shape, b.shape
    elif mode == "nt":
        (m, k), (n, k2) = a.shape, b.shape
    else:
        (k, m), (k2, n) = a.shape, b.shape
    assert k == k2, (name, a.shape, b.shape)
    tm = _div(m, MM_TILE_CAP, LANE)
    tn = _div(n, MM_TILE_CAP, MXU_WIDTH) if n % MXU_WIDTH == 0 else 0
    if tn < 1024:
        tn = _div(n, MM_TILE_CAP, LANE)
    out_shape = jax.ShapeDtypeStruct((m, n), out_dtype)

    if mode == "tn":
        assert res is None

        def kern_tn(a_ref, b_ref, o_ref, at_ref):
            @pl.when(pl.program_id(1) == 0)
            def _():
                step = min(TRANSPOSE_CHUNK, k)
                for c0 in range(0, k, step):
                    at_ref[:, c0:c0 + step] = a_ref[c0:c0 + step, :].astype(F32).T.astype(BF16)

            o_ref[...] = jnp.dot(at_ref[...], b_ref[...].astype(BF16),
                                 preferred_element_type=F32).astype(o_ref.dtype)

        return _pcall(
            kern_tn, name=name, grid=(m // tm, n // tn),
            in_specs=[pl.BlockSpec((k, tm), lambda i, j: (0, i)),
                      pl.BlockSpec((k, tn), lambda i, j: (0, j))],
            out_specs=pl.BlockSpec((tm, tn), lambda i, j: (i, j)),
            out_shape=out_shape, args=(a, b),
            scratch_shapes=[pltpu.VMEM((tm, k), BF16)],
            semantics=("parallel", "arbitrary"), carried=carried)

    tk = k if k <= 2048 else _div(k, 3072, LANE)
    nk = k // tk
    a_spec = pl.BlockSpec((tm, tk), lambda i, j, kk: (i, kk))
    if mode == "nn":
        b_spec = pl.BlockSpec((tk, tn), lambda i, j, kk: (kk, j))
        dot = _dot
    else:
        b_spec = pl.BlockSpec((tn, tk), lambda i, j, kk: (j, kk))
        dot = _dot_nt
    o_spec = pl.BlockSpec((tm, tn), lambda i, j, kk: (i, j))
    in_specs = [a_spec, b_spec]
    args = [a, b]
    if res is not None:
        in_specs.append(o_spec)
        args.append(res)
    has_res = res is not None

    def kern(*refs):
        a_ref, b_ref = refs[0], refs[1]
        r_ref = refs[2] if has_res else None
        o_ref = refs[3] if has_res else refs[2]
        part = dot(a_ref[...], b_ref[...])
        if nk == 1:
            if has_res:
                part = part + r_ref[...]
            o_ref[...] = part.astype(o_ref.dtype)
            return
        acc_ref = refs[-1]
        kk = pl.program_id(2)

        @pl.when(kk == 0)
        def _():
            acc_ref[...] = part

        @pl.when(kk > 0)
        def _():
            acc_ref[...] += part

        @pl.when(kk == nk - 1)
        def _():
            tot = acc_ref[...]
            if has_res:
                tot = tot + r_ref[...]
            o_ref[...] = tot.astype(o_ref.dtype)

    return _pcall(
        kern, name=name, grid=(m // tm, n // tn, nk),
        in_specs=in_specs, out_specs=o_spec, out_shape=out_shape, args=args,
        scratch_shapes=[pltpu.VMEM((tm, tn), F32)] if nk > 1 else [],
        semantics=("parallel", "parallel", "arbitrary"), carried=carried)


ROW_BLOCK = 256


def _rms_fwd(x, g, name):
    t, d = x.shape
    tr = min(ROW_BLOCK, t)

    def kern(x_ref, g_ref, o_ref):
        xf = x_ref[...]
        r = lax.rsqrt(jnp.mean(xf * xf, axis=-1, keepdims=True) + RMS_EPS)
        o_ref[...] = (xf * r * g_ref[...]).astype(o_ref.dtype)

    return pl.pallas_call(
        kern, name=name, grid=(t // tr,),
        in_specs=[pl.BlockSpec((tr, d), lambda i: (i, 0)), pl.BlockSpec((1, d), lambda i: (0, 0))],
        out_specs=pl.BlockSpec((tr, d), lambda i: (i, 0)),
        out_shape=jax.ShapeDtypeStruct((t, d), BF16),
        compiler_params=_cparams(("parallel",)),
    )(x, g)


def _rms_bwd(x, g, dh, res, name):
    t, d = x.shape
    tr = min(ROW_BLOCK, t)
    has_res = res is not None

    def kern(*refs):
        x_ref, g_ref, dh_ref = refs[:3]
        r_ref = refs[3] if has_res else None
        dx_ref, dg_ref = refs[-2], refs[-1]
        xf = x_ref[...]
        r = lax.rsqrt(jnp.mean(xf * xf, axis=-1, keepdims=True) + RMS_EPS)
        xn = xf * r
        dh_ = dh_ref[...]
        dhg = dh_ * g_ref[...]
        dx = r * (dhg - xn * jnp.mean(dhg * xn, axis=-1, keepdims=True))
        if has_res:
            dx = dx + r_ref[...]
        dx_ref[...] = dx
        part = jnp.sum(dh_ * xn, axis=0, keepdims=True)

        @pl.when(pl.program_id(0) == 0)
        def _():
            dg_ref[...] = part

        @pl.when(pl.program_id(0) > 0)
        def _():
            dg_ref[...] += part

    row = pl.BlockSpec((tr, d), lambda i: (i, 0))
    vec = pl.BlockSpec((1, d), lambda i: (0, 0))
    in_specs = [row, vec, row] + ([row] if has_res else [])
    args = [x, g, dh] + ([res] if has_res else [])
    return pl.pallas_call(
        kern, name=name, grid=(t // tr,), in_specs=in_specs, out_specs=[row, vec],
        out_shape=[jax.ShapeDtypeStruct((t, d), F32), jax.ShapeDtypeStruct((1, d), F32)],
        compiler_params=_cparams(("arbitrary",)),
    )(*args)


def _loss_head(x3, g, target, name):
    t, d = x3.shape
    tr = ROW_BLOCK

    def kern(x_ref, g_ref, t_ref, dx_ref, dg_ref, loss_ref):
        xf = x_ref[...]
        r = lax.rsqrt(jnp.mean(xf * xf, axis=-1, keepdims=True) + RMS_EPS)
        xn = xf * r
        gg = g_ref[...]
        err = xn * gg - t_ref[...]
        lpart = 0.5 * jnp.sum(jnp.mean(err * err, axis=-1, keepdims=True), axis=0, keepdims=True)
        dy = err * (1.0 / d)
        dyg = dy * gg
        dx_ref[...] = r * (dyg - xn * jnp.mean(dyg * xn, axis=-1, keepdims=True))
        gpart = jnp.sum(dy * xn, axis=0, keepdims=True)
        lrow = jnp.broadcast_to(lpart, (1, LANE))

        @pl.when(pl.program_id(0) == 0)
        def _():
            dg_ref[...] = gpart
            loss_ref[...] = lrow

        @pl.when(pl.program_id(0) > 0)
        def _():
            dg_ref[...] += gpart
            loss_ref[...] += lrow

    row = pl.BlockSpec((tr, d), lambda i: (i, 0))
    vec = pl.BlockSpec((1, d), lambda i: (0, 0))
    return pl.pallas_call(
        kern, name=name, grid=(t // tr,), in_specs=[row, vec, row],
        out_specs=[row, vec, pl.BlockSpec((1, LANE), lambda i: (0, 0))],
        out_shape=[jax.ShapeDtypeStruct((t, d), F32), jax.ShapeDtypeStruct((1, d), F32),
                   jax.ShapeDtypeStruct((1, LANE), F32)],
        compiler_params=_cparams(("arbitrary",)),
    )(x3, g, target)


ATT_SCALE = HEAD_DIM ** -0.5
Q_BLOCK0, K_BLOCK0, V_BLOCK0 = 0, ATT_WIDTH // HEAD_DIM, 2 * ATT_WIDTH // HEAD_DIM


def _residue_rows(dil, r, n):
    if dil == 1:
        return pl.ds(n * ATT_BLOCK, ATT_BLOCK)
    return pl.ds(n * ATT_BLOCK * dil + r, ATT_BLOCK, stride=dil)


def _band_mask(with_prev):
    width = 2 * ATT_BLOCK if with_prev else ATT_BLOCK
    iq = lax.broadcasted_iota(jnp.int32, (ATT_BLOCK, width), 0)
    ik = lax.broadcasted_iota(jnp.int32, (ATT_BLOCK, width), 1)
    if not with_prev:
        return ik <= iq
    return ((ik < ATT_BLOCK) & (iq <= ik)) | ((ik >= ATT_BLOCK) & ((ik - ATT_BLOCK) <= iq))


def _band_keys(ref, dil, r, n):
    own = ref[_residue_rows(dil, r, n), :]
    if n == 0:
        return own
    return jnp.concatenate([ref[_residue_rows(dil, r, n - 1), :], own], axis=0)


def _attn_col_spec(base, grp):
    return pl.BlockSpec((SEQ, HEAD_DIM), lambda h: (0, base + grp * ATT_HEADS + h))


def _attn_fwd(proj, grp, name, carried=()):
    _, dil = ATT_GROUPS[grp]
    nb = SEQ // dil // ATT_BLOCK

    def kern(q_ref, k_ref, v_ref, o_ref, lse_ref):
        for r in range(dil):
            for n in range(nb):
                rows = _residue_rows(dil, r, n)
                s = _dot_nt(q_ref[rows, :], _band_keys(k_ref, dil, r, n)) * ATT_SCALE
                s = jnp.where(_band_mask(n > 0), s, -jnp.inf)
                m = jnp.max(s, axis=-1, keepdims=True)
                p = jnp.exp(s - m)
                l = jnp.sum(p, axis=-1, keepdims=True)
                o_ref[rows, :] = _dot(p / l, _band_keys(v_ref, dil, r, n))
                lse_ref[rows, :] = jnp.broadcast_to(m + jnp.log(l), (ATT_BLOCK, HEAD_DIM))

    out_spec = pl.BlockSpec((SEQ, HEAD_DIM), lambda h: (0, h))
    return _pcall(
        kern, name=name, grid=(ATT_HEADS,),
        in_specs=[_attn_col_spec(Q_BLOCK0, grp), _attn_col_spec(K_BLOCK0, grp), _attn_col_spec(V_BLOCK0, grp)],
        out_specs=[out_spec, out_spec],
        out_shape=[jax.ShapeDtypeStruct((SEQ, ATT_OUT), F32)] * 2, args=(proj, proj, proj),
        semantics=("parallel",), carried=carried)


def _attn_weights(l0, l1, l2):
    mx = jnp.maximum(jnp.maximum(l0, l1), l2)
    e0, e1, e2 = jnp.exp(l0 - mx), jnp.exp(l1 - mx), jnp.exp(l2 - mx)
    den = e0 + e1 + e2
    return e0 / den, e1 / den, e2 / den


def _attn_merge_fwd(outs, lses, name):
    tr = ROW_BLOCK

    def kern(o0, o1, o2, l0, l1, l2, out_ref):
        a0, a1, a2 = _attn_weights(l0[...], l1[...], l2[...])
        out_ref[...] = (a0 * o0[...] + a1 * o1[...] + a2 * o2[...]).astype(out_ref.dtype)

    spec = pl.BlockSpec((tr, ATT_OUT), lambda i: (i, 0))
    return pl.pallas_call(
        kern, name=name, grid=(SEQ // tr,), in_specs=[spec] * 6, out_specs=spec,
        out_shape=jax.ShapeDtypeStruct((SEQ, ATT_OUT), BF16),
        compiler_params=_cparams(("parallel",)),
    )(*outs, *lses)


def _attn_merge_bwd(outs, lses, do_att, name, carried=()):
    tr = ROW_BLOCK

    def kern(o0, o1, o2, l0, l1, l2, do_ref, d0, d1, d2, t0, t1, t2):
        alphas = _attn_weights(l0[...], l1[...], l2[...])
        do = do_ref[...]
        o_att = alphas[0] * o0[...] + alphas[1] * o1[...] + alphas[2] * o2[...]
        prod = do * o_att
        parts = []
        for h in range(ATT_HEADS):
            sl = slice(h * HEAD_DIM, (h + 1) * HEAD_DIM)
            tot = jnp.sum(prod[:, sl], axis=-1, keepdims=True)
            parts.append(jnp.broadcast_to(tot, (tr, HEAD_DIM)))
        dd = jnp.concatenate(parts, axis=1)
        for a, d_ref, t_ref in zip(alphas, (d0, d1, d2), (t0, t1, t2)):
            d_ref[...] = a * do
            t_ref[...] = -a * dd

    spec = pl.BlockSpec((tr, ATT_OUT), lambda i: (i, 0))
    res, cres = _pcall(
        kern, name=name, grid=(SEQ // tr,), in_specs=[spec] * 7, out_specs=[spec] * 6,
        out_shape=[jax.ShapeDtypeStruct((SEQ, ATT_OUT), F32)] * 6, args=(*outs, *lses, do_att),
        semantics=("parallel",), carried=carried)
    return (res[:3], res[3:]), cres


def _attn_bwd(proj, grp, lse, do_g, dl_g, name, carried=()):
    _, dil = ATT_GROUPS[grp]
    nb = SEQ // dil // ATT_BLOCK

    def kern(q_ref, k_ref, v_ref, do_ref, lse_ref, dl_ref, dq_ref, dk_ref, dv_ref, dq_acc, dk_acc, dv_acc):
        dk_acc[...] = jnp.zeros_like(dk_acc)
        dv_acc[...] = jnp.zeros_like(dv_acc)
        for r in range(dil):
            for n in range(nb):
                rows = _residue_rows(dil, r, n)
                q, do = q_ref[rows, :], do_ref[rows, :]
                kk, vv = _band_keys(k_ref, dil, r, n), _band_keys(v_ref, dil, r, n)
                s = _dot_nt(q, kk) * ATT_SCALE
                p = jnp.where(_band_mask(n > 0), jnp.exp(s - lse_ref[rows, :][:, :1]), 0.0)
                ds = p * (_dot_nt(do, vv) + dl_ref[rows, :][:, :1])
                dq_acc[rows, :] = _dot(ds, kk) * ATT_SCALE
                dk = _dot_tn(ds, q) * ATT_SCALE
                dv = _dot_tn(p, do)
                if n > 0:
                    prev = _residue_rows(dil, r, n - 1)
                    dk_acc[prev, :] += dk[:ATT_BLOCK]
                    dv_acc[prev, :] += dv[:ATT_BLOCK]
                    dk, dv = dk[ATT_BLOCK:], dv[ATT_BLOCK:]
                dk_acc[rows, :] += dk
                dv_acc[rows, :] += dv
        dq_ref[...] = dq_acc[...].astype(dq_ref.dtype)
        dk_ref[...] = dk_acc[...].astype(dk_ref.dtype)
        dv_ref[...] = dv_acc[...].astype(dv_ref.dtype)

    spec = pl.BlockSpec((SEQ, HEAD_DIM), lambda h: (0, h))
    return _pcall(
        kern, name=name, grid=(ATT_HEADS,),
        in_specs=[_attn_col_spec(Q_BLOCK0, grp), _attn_col_spec(K_BLOCK0, grp), _attn_col_spec(V_BLOCK0, grp),
                  spec, spec, spec],
        out_specs=[spec] * 3,
        out_shape=[jax.ShapeDtypeStruct((SEQ, ATT_OUT), BF16)] * 3, args=(proj, proj, proj, do_g, lse, dl_g),
        scratch_shapes=[pltpu.VMEM((SEQ, HEAD_DIM), F32)] * 3,
        semantics=("parallel",), carried=carried)


HG_HEADS_PER_STEP = 8
HG_BLOCK_W = 4 * HEAD_DIM
HG_BLOCKS = HG_HEADS_PER_STEP * HEAD_DIM // HG_BLOCK_W
HG_STEP_W = HG_HEADS_PER_STEP * HEAD_DIM
HG_Q_BLK = (3 * ATT_WIDTH) // HG_BLOCK_W
HG_N_CHUNKS = SEQ // HG_CHUNK
HG_MID = HG_CHUNK // 2


def _lower_bound(lb_ref, sl):
    l0, l1 = lb_ref[0:1, sl], lb_ref[1:2, sl]
    mx = jnp.maximum(l0, l1)
    e0, e1 = jnp.exp(l0 - mx), jnp.exp(l1 - mx)
    return e0 / (e0 + e1)


def _tri(lower):
    i = lax.broadcasted_iota(jnp.int32, (HG_CHUNK, HG_CHUNK), 0)
    j = lax.broadcasted_iota(jnp.int32, (HG_CHUNK, HG_CHUNK), 1)
    return (i >= j) if lower else (i <= j)


def _head_mean(x):
    parts = []
    for hd in range(x.shape[1] // HEAD_DIM):
        m = jnp.mean(x[:, hd * HEAD_DIM:(hd + 1) * HEAD_DIM], axis=-1, keepdims=True)
        parts.append(jnp.broadcast_to(m, (x.shape[0], HEAD_DIM)))
    return jnp.concatenate(parts, axis=1)


def _hg_chunk_terms(qh, fh, lb):
    sig = _sigmoid(fh)
    f = lb + (1.0 - lb) * sig
    k = 1.0 - f
    b = _dot_exact(_tri(True).astype(F32), jnp.log(f))
    bl = b[HG_CHUNK - 1:HG_CHUNK, :]
    br = b[HG_MID:HG_MID + 1, :]
    sq = _sigmoid(qh)
    q = qh * sq
    return dict(sig=sig, f=f, k=k, b=b, bl=bl, br=br, sq=sq, q=q,
                e1=jnp.exp(bl - b), e2=jnp.exp(b), e3=jnp.exp(b - br), e4=jnp.exp(br - b))


def _hg_fwd(proj, lbw, normw, name, carried=()):
    def in_blks(off):
        return [pl.BlockSpec((HG_CHUNK, HG_BLOCK_W), lambda hp, n, b=b: (n, HG_Q_BLK + off + hp * HG_BLOCKS + b))
                for b in range(HG_BLOCKS)]

    def kern(*refs):
        q_refs, f_refs, i_refs, g_refs = (refs[k * HG_BLOCKS:(k + 1) * HG_BLOCKS] for k in range(4))
        lb_ref, nw_ref, oraw_ref, ohg_ref, st_ref, state = refs[4 * HG_BLOCKS:]

        @pl.when(pl.program_id(1) == 0)
        def _():
            state[...] = jnp.zeros_like(state)

        causal = _tri(True)
        wide = lambda rs: jnp.concatenate([r[...] for r in rs], axis=1)
        t = _hg_chunk_terms(wide(q_refs), wide(f_refs), _lower_bound(lb_ref, slice(None)))
        v, gh = wide(i_refs), wide(g_refs)
        kd, qb, qr, kr = t["k"] * t["e1"], t["q"] * t["e2"], t["q"] * t["e3"], t["k"] * t["e4"]
        decay = jnp.exp(t["bl"])
        outs = []
        for hd in range(HG_HEADS_PER_STEP):
            sl = slice(hd * HEAD_DIM, (hd + 1) * HEAD_DIM)
            st = state[hd]
            st_ref[0, hd] = st
            a = jnp.where(causal, _dot_nt(qr[:, sl], kr[:, sl]), 0.0)
            outs.append(_dot_nt(qb[:, sl], st) + _dot(a, v[:, sl]))
            state[hd] = st * decay[:, sl] + _dot_tn(v[:, sl], kd[:, sl])
        o = jnp.concatenate(outs, axis=1)
        oraw_ref[...] = o
        r = lax.rsqrt(_head_mean(o * o) + RMS_EPS)
        nw = jnp.tile(nw_ref[...], (1, HG_HEADS_PER_STEP))
        ohg_ref[...] = (o * r * nw * (gh * _sigmoid(gh))).astype(ohg_ref.dtype)

    out_blk = pl.BlockSpec((HG_CHUNK, HG_STEP_W), lambda hp, n: (n, hp))
    return _pcall(
        kern, name=name, grid=(HG_HEADS // HG_HEADS_PER_STEP, HG_N_CHUNKS),
        in_specs=[*in_blks(0), *in_blks(2), *in_blks(4), *in_blks(6),
                  pl.BlockSpec((2, HG_STEP_W), lambda hp, n: (0, hp)),
                  pl.BlockSpec((1, HEAD_DIM), lambda hp, n: (0, 0))],
        out_specs=[out_blk, out_blk,
                   pl.BlockSpec((1, HG_HEADS_PER_STEP, HEAD_DIM, HEAD_DIM), lambda hp, n: (n, hp, 0, 0))],
        out_shape=[jax.ShapeDtypeStruct((SEQ, HG_WIDTH), F32), jax.ShapeDtypeStruct((SEQ, HG_WIDTH), BF16),
                   jax.ShapeDtypeStruct((HG_N_CHUNKS, HG_HEADS, HEAD_DIM, HEAD_DIM), F32)],
        args=(*[proj] * (4 * HG_BLOCKS), lbw, normw),
        scratch_shapes=[pltpu.VMEM((HG_HEADS_PER_STEP, HEAD_DIM, HEAD_DIM), F32)],
        semantics=("parallel", "arbitrary"), carried=carried)


def _hg_bwd(proj, lbw, normw, oraw, states, do_hg, name, carried=()):
    last = HG_N_CHUNKS - 1

    def in_blks(off):
        return [pl.BlockSpec((HG_CHUNK, HG_BLOCK_W),
                             lambda hp, n, b=b: (last - n, HG_Q_BLK + off + hp * HG_BLOCKS + b))
                for b in range(HG_BLOCKS)]

    blk = pl.BlockSpec((HG_CHUNK, HG_STEP_W), lambda hp, n: (last - n, hp))

    def kern(*refs):
        q_refs, f_refs, i_refs, g_refs = (refs[k * HG_BLOCKS:(k + 1) * HG_BLOCKS] for k in range(4))
        (lb_ref, nw_ref, oraw_ref, st_ref, do_ref, dq_ref, df_ref, di_ref, dg_ref, dlb_ref, dnw_ref,
         dstate) = refs[4 * HG_BLOCKS:]
        first = pl.program_id(1) == 0

        @pl.when(first)
        def _():
            dstate[...] = jnp.zeros_like(dstate)

        causal = _tri(True)
        wide = lambda rs: jnp.concatenate([r[...] for r in rs], axis=1)
        cat = lambda parts: jnp.concatenate(parts, axis=1)
        qh, fh, v, gh = wide(q_refs), wide(f_refs), wide(i_refs), wide(g_refs)
        o, dout = oraw_ref[...], do_ref[...]
        nw = jnp.tile(nw_ref[...], (1, HG_HEADS_PER_STEP))
        sgg = _sigmoid(gh)
        r = lax.rsqrt(_head_mean(o * o) + RMS_EPS)
        xn = o * r
        dg_ref[...] = (dout * xn * nw * (sgg * (1.0 + gh * (1.0 - sgg)))).astype(dg_ref.dtype)
        don = dout * (gh * sgg)
        dnw_wide = jnp.sum(don * xn, axis=0, keepdims=True)
        dnw_tot = dnw_wide[:, :HEAD_DIM]
        for hd in range(1, HG_HEADS_PER_STEP):
            dnw_tot = dnw_tot + dnw_wide[:, hd * HEAD_DIM:(hd + 1) * HEAD_DIM]
        tt = don * nw
        do = r * (tt - xn * _head_mean(tt * xn))
        lb = _lower_bound(lb_ref, slice(None))
        t = _hg_chunk_terms(qh, fh, lb)
        k, q = t["k"], t["q"]
        kd, qb, qr, kr = k * t["e1"], q * t["e2"], q * t["e3"], k * t["e4"]
        decay = jnp.exp(t["bl"])
        dqb, dqr, dkr, dkd, dv, ddecay = [], [], [], [], [], []
        for hd in range(HG_HEADS_PER_STEP):
            sl = slice(hd * HEAD_DIM, (hd + 1) * HEAD_DIM)
            st = st_ref[0, hd]
            dstn = dstate[hd]
            a = jnp.where(causal, _dot_nt(qr[:, sl], kr[:, sl]), 0.0)
            da = jnp.where(causal, _dot_nt(do[:, sl], v[:, sl]), 0.0)
            dqb.append(_dot(do[:, sl], st))
            dv.append(_dot_tn(a, do[:, sl]) + _dot_nt(kd[:, sl], dstn))
            dqr.append(_dot(da, kr[:, sl]))
            dkr.append(_dot_tn(da, qr[:, sl]))
            dkd.append(_dot(v[:, sl], dstn))
            ddecay.append(jnp.sum(dstn * st, axis=0, keepdims=True))
            dstate[hd] = dstn * decay[:, sl] + _dot_tn(do[:, sl], qb[:, sl])
        dqb, dqr, dkr, dkd, dv, ddecay = cat(dqb), cat(dqr), cat(dkr), cat(dkd), cat(dv), cat(ddecay)
        dq = dqb * t["e2"] + dqr * t["e3"]
        dk = dkd * t["e1"] + dkr * t["e4"]
        db = dqb * qb + dqr * qr - dkr * kr - dkd * kd
        dbl = jnp.sum(dkd * kd, axis=0, keepdims=True) + ddecay * decay
        dbr = jnp.sum(dkr * kr - dqr * qr, axis=0, keepdims=True)
        rows = lax.broadcasted_iota(jnp.int32, db.shape, 0)
        dlf = _dot_exact(_tri(False).astype(F32), db) + dbl + jnp.where(rows <= HG_MID, dbr, 0.0)
        df = dlf / t["f"] - dk
        sig, sq = t["sig"], t["sq"]
        df_ref[...] = (df * (1.0 - lb) * sig * (1.0 - sig)).astype(df_ref.dtype)
        dlb_row = jnp.sum(df * (1.0 - sig), axis=0, keepdims=True)
        dq_ref[...] = (dq * (sq * (1.0 + qh * (1.0 - sq)))).astype(dq_ref.dtype)
        di_ref[...] = dv.astype(di_ref.dtype)
        dnw_blk = jnp.broadcast_to(dnw_tot, (8, HEAD_DIM))

        @pl.when(first)
        def _():
            dlb_ref[...] = dlb_row
            dnw_ref[...] = dnw_blk

        @pl.when(jnp.logical_not(first))
        def _():
            dlb_ref[...] += dlb_row
            dnw_ref[...] += dnw_blk

    n_hp = HG_HEADS // HG_HEADS_PER_STEP
    outs, cres = _pcall(
        kern, name=name, grid=(n_hp, HG_N_CHUNKS),
        in_specs=[*in_blks(0), *in_blks(2), *in_blks(4), *in_blks(6),
                  pl.BlockSpec((2, HG_STEP_W), lambda hp, n: (0, hp)),
                  pl.BlockSpec((1, HEAD_DIM), lambda hp, n: (0, 0)),
                  blk,
                  pl.BlockSpec((1, HG_HEADS_PER_STEP, HEAD_DIM, HEAD_DIM), lambda hp, n: (last - n, hp, 0, 0)),
                  blk],
        out_specs=[blk, blk, blk, blk,
                   pl.BlockSpec((1, HG_STEP_W), lambda hp, n: (0, hp)),
                   pl.BlockSpec((8, HEAD_DIM), lambda hp, n: (hp, 0))],
        out_shape=[jax.ShapeDtypeStruct((SEQ, HG_WIDTH), BF16)] * 4
        + [jax.ShapeDtypeStruct((1, HG_WIDTH), F32), jax.ShapeDtypeStruct((8 * n_hp, HEAD_DIM), F32)],
        args=(*[proj] * (4 * HG_BLOCKS), lbw, normw, oraw, states, do_hg),
        scratch_shapes=[pltpu.VMEM((HG_HEADS_PER_STEP, HEAD_DIM, HEAD_DIM), F32)],
        semantics=("parallel", "arbitrary"), carried=carried)
    dqh, dfh, dih, dgh, dlb, dnw = outs
    return (dqh, dfh, dih, dgh, dlb, [dnw[8 * i:8 * i + 1] for i in range(n_hp)]), cres


GATE_BLOCK_W = 512
GATE_A_BLK = (3 * ATT_WIDTH + 4 * HG_WIDTH) // GATE_BLOCK_W
GATE_B_BLK = GATE_A_BLK + D_MODEL // GATE_BLOCK_W


def _gate_specs():
    tr = ROW_BLOCK
    blk = pl.BlockSpec((tr, GATE_BLOCK_W), lambda i, j: (i, j))
    ga = pl.BlockSpec((tr, GATE_BLOCK_W), lambda i, j: (i, GATE_A_BLK + j))
    gb = pl.BlockSpec((tr, GATE_BLOCK_W), lambda i, j: (i, GATE_B_BLK + j))
    return (SEQ // tr, D_MODEL // GATE_BLOCK_W), blk, ga, gb


def _gate_fwd(proj, ya, yb, name, carried=()):
    grid, blk, ga, gb = _gate_specs()

    def kern(ga_ref, gb_ref, ya_ref, yb_ref, o_ref):
        o_ref[...] = (_sigmoid(ga_ref[...]) * ya_ref[...] + _sigmoid(gb_ref[...]) * yb_ref[...]).astype(o_ref.dtype)

    return _pcall(
        kern, name=name, grid=grid, in_specs=[ga, gb, blk, blk], out_specs=blk,
        out_shape=jax.ShapeDtypeStruct((SEQ, D_MODEL), BF16), args=(proj, proj, ya, yb),
        semantics=("parallel", "parallel"), carried=carried)


def _gate_bwd(proj, ya, yb, dmerged, name, carried=()):
    grid, blk, ga, gb = _gate_specs()

    def kern(ga_ref, gb_ref, ya_ref, yb_ref, dm_ref, dya_ref, dyb_ref, dga_ref, dgb_ref):
        dm = dm_ref[...]
        sa, sb = _sigmoid(ga_ref[...]), _sigmoid(gb_ref[...])
        dya_ref[...] = (dm * sa).astype(dya_ref.dtype)
        dyb_ref[...] = (dm * sb).astype(dyb_ref.dtype)
        dga_ref[...] = (dm * ya_ref[...] * sa * (1.0 - sa)).astype(dga_ref.dtype)
        dgb_ref[...] = (dm * yb_ref[...] * sb * (1.0 - sb)).astype(dgb_ref.dtype)

    return _pcall(
        kern, name=name, grid=grid, in_specs=[ga, gb, blk, blk, blk], out_specs=[blk] * 4,
        out_shape=[jax.ShapeDtypeStruct((SEQ, D_MODEL), BF16)] * 4, args=(proj, proj, ya, yb, dmerged),
        semantics=("parallel", "parallel"), carried=carried)


FF_SHARD = D_FF // N_CHIPS


def _swiglu_fwd(ab, name):
    tr = ROW_BLOCK

    def kern(ab_ref, u_ref):
        a, b = ab_ref[:, :FF_SHARD], ab_ref[:, FF_SHARD:]
        u_ref[...] = (a * _sigmoid(a) * b).astype(u_ref.dtype)

    return pl.pallas_call(
        kern, name=name, grid=(SEQ // tr, N_CHIPS),
        in_specs=[pl.BlockSpec((tr, 2 * FF_SHARD), lambda i, j: (i, j))],
        out_specs=pl.BlockSpec((tr, FF_SHARD), lambda i, j: (i, j)),
        out_shape=jax.ShapeDtypeStruct((SEQ, D_FF), BF16),
        compiler_params=_cparams(("parallel", "parallel")),
    )(ab)


def _swiglu_bwd(ab, du, name, carried=()):
    tr = ROW_BLOCK

    def kern(ab_ref, du_ref, dab_ref):
        a, b = ab_ref[:, :FF_SHARD], ab_ref[:, FF_SHARD:]
        du_ = du_ref[...]
        sg = _sigmoid(a)
        dab_ref[:, :FF_SHARD] = (du_ * b * (sg * (1.0 + a * (1.0 - sg)))).astype(dab_ref.dtype)
        dab_ref[:, FF_SHARD:] = (du_ * (a * sg)).astype(dab_ref.dtype)

    wide = pl.BlockSpec((tr, 2 * FF_SHARD), lambda i, j: (i, j))
    return _pcall(
        kern, name=name, grid=(SEQ // tr, N_CHIPS),
        in_specs=[wide, pl.BlockSpec((tr, FF_SHARD), lambda i, j: (i, j))],
        out_specs=wide, out_shape=jax.ShapeDtypeStruct((SEQ, 2 * D_FF), BF16), args=(ab, du),
        semantics=("parallel", "parallel"), carried=carried)


CROSS_ROWS = 512


def _cross_fwd(qc, kvc, name):
    def kern(q_ref, k_ref, v_ref, o_ref):
        s = _dot_nt(q_ref[...], k_ref[...]) * ATT_SCALE
        m = jnp.max(s, axis=-1, keepdims=True)
        e = jnp.exp(s - m)
        p = e / jnp.sum(e, axis=-1, keepdims=True)
        o_ref[...] = _dot(p, v_ref[...]).astype(o_ref.dtype)

    qblk = pl.BlockSpec((CROSS_ROWS, HEAD_DIM), lambda h, i: (i, h))
    return pl.pallas_call(
        kern, name=name, grid=(CROSS_HEADS, SEQ // CROSS_ROWS),
        in_specs=[qblk, pl.BlockSpec((MEM_LEN, HEAD_DIM), lambda h, i: (0, h)),
                  pl.BlockSpec((MEM_LEN, HEAD_DIM), lambda h, i: (0, CROSS_HEADS + h))],
        out_specs=qblk, out_shape=jax.ShapeDtypeStruct((SEQ, CROSS_WIDTH), BF16),
        compiler_params=_cparams(("parallel", "parallel")),
    )(qc, kvc, kvc)


def _cross_bwd(qc, kvc, doc, name):
    def kern(q_ref, k_ref, v_ref, do_ref, dq_ref, dk_ref, dv_ref):
        q, k, v, do = q_ref[...], k_ref[...], v_ref[...], do_ref[...]
        s = _dot_nt(q, k) * ATT_SCALE
        m = jnp.max(s, axis=-1, keepdims=True)
        e = jnp.exp(s - m)
        p = e / jnp.sum(e, axis=-1, keepdims=True)
        dp = _dot_nt(do, v)
        ds = p * (dp - jnp.sum(dp * p, axis=-1, keepdims=True))
        dq_ref[...] = (_dot(ds, k) * ATT_SCALE).astype(dq_ref.dtype)
        dk = _dot_tn(ds, q) * ATT_SCALE
        dv = _dot_tn(p, do)

        @pl.when(pl.program_id(1) == 0)
        def _():
            dk_ref[...] = dk
            dv_ref[...] = dv

        @pl.when(pl.program_id(1) > 0)
        def _():
            dk_ref[...] += dk
            dv_ref[...] += dv

    qblk = pl.BlockSpec((CROSS_ROWS, HEAD_DIM), lambda h, i: (i, h))
    kblk = pl.BlockSpec((MEM_LEN, HEAD_DIM), lambda h, i: (0, h))
    dq, dk, dv = pl.pallas_call(
        kern, name=name, grid=(CROSS_HEADS, SEQ // CROSS_ROWS),
        in_specs=[qblk, kblk, pl.BlockSpec((MEM_LEN, HEAD_DIM), lambda h, i: (0, CROSS_HEADS + h)), qblk],
        out_specs=[qblk, kblk, kblk],
        out_shape=[jax.ShapeDtypeStruct((SEQ, CROSS_WIDTH), BF16),
                   jax.ShapeDtypeStruct((MEM_LEN, CROSS_WIDTH), F32),
                   jax.ShapeDtypeStruct((MEM_LEN, CROSS_WIDTH), F32)],
        compiler_params=_cparams(("parallel", "arbitrary")),
    )(qc, kvc, kvc, doc)
    return dq, jnp.concatenate([dk, dv], axis=1)


FULL_SPECS = {
    "w_in": ("col", D_MODEL, IN_WIDTH),
    "w_branch_a": ("col", ATT_OUT, D_MODEL),
    "w_branch_b": ("col", HG_WIDTH, D_MODEL),
    "w_out": ("row", D_MODEL, D_MODEL),
    "wq_cross": ("row", D_MODEL, CROSS_WIDTH),
    "wkv_cross": ("row", D_MODEL, 2 * CROSS_WIDTH),
    "wo_cross": ("col", CROSS_WIDTH, D_MODEL),
    "w13": ("col", D_MODEL, 2 * D_FF),
    "w2": ("row", D_FF, D_MODEL),
}
WEIGHT_PLACE = {
    "w_in": ("w_in", 0), "w_branch_a": ("w_branch_a", 0), "w_branch_b": ("w_branch_b", 0),
    "w_out": ("w_out", 0), "wq_cross": ("wq_cross", 0), "wkv_cross": ("wkv_cross", 0),
    "wo_cross": ("wo_cross", 0), "w1": ("w13", 0), "w3": ("w13", FF_SHARD), "w2": ("w2", 0),
}
BIG_WEIGHTS = tuple(WEIGHT_PLACE)
EW_BLOCK_ELEMS = 512 * 1024


def _position():
    return lax.axis_index("x"), lax.axis_index("y"), lax.axis_index("c")


def _other_chips(x, y):
    return [(1 - x, y), (x, 1 - y), (1 - x, 1 - y)]


def _half(ref, kind, h):
    r, c = ref.shape
    if kind == "col":
        return ref.at[pl.ds(h * (r // 2), r // 2), :]
    return ref.at[:, pl.ds(h * (c // 2), c // 2)]


def _shard_of(ref, kind, start, size):
    return ref.at[:, pl.ds(start, size)] if kind == "col" else ref.at[pl.ds(start, size), :]


def _rows_of(ref, r0, nrows):
    return ref if nrows is None else ref.at[pl.ds(r0, nrows), :]


def _half_shape(kind, rows, cols):
    return (rows // 2, cols) if kind == "col" else (rows, cols // 2)


def _slot_shape(spec):
    kind, rows, cols = spec
    hr, hc = _half_shape(kind, rows, cols)
    return (hr, hc // N_CHIPS) if kind == "col" else (hr // N_CHIPS, hc)


def _remote(src, dst, send_sem, recv_sem, device):
    return pltpu.make_async_remote_copy(src_ref=src, dst_ref=dst, send_sem=send_sem, recv_sem=recv_sem,
                                        device_id=device, device_id_type=MESH)


def _gather_ici_comm(fulls, jobs, specs):
    def piece(refs, job, chip, c):
        f, r0, nr = job
        kind, rows, cols = specs[f]
        stride = (cols if kind == "col" else rows) // N_CHIPS
        return _rows_of(_half(_shard_of(refs[f], kind, chip * stride, stride), kind, c), r0, nr)

    def start(refs, ss, rs):
        x, y, c = _position()
        j = 2 * x + y
        for q, job in enumerate(jobs):
            for p, (px, py) in enumerate(_other_chips(x, y)):
                _remote(piece(refs, job, j, c), piece(refs, job, j, c), ss.at[3 * q + p], rs.at[3 * q + p],
                        (px, py, c)).start()

    def finish(refs, ss, rs):
        x, y, c = _position()
        j = 2 * x + y
        for q, job in enumerate(jobs):
            for p, (px, py) in enumerate(_other_chips(x, y)):
                _remote(piece(refs, job, j, c), piece(refs, job, 2 * px + py, c), ss.at[3 * q + p],
                        rs.at[3 * q + p], (px, py, c)).wait_recv()
        for q, job in enumerate(jobs):
            for p, (px, py) in enumerate(_other_chips(x, y)):
                _remote(piece(refs, job, j, c), piece(refs, job, j, c), ss.at[3 * q + p], rs.at[3 * q + p],
                        (px, py, c)).wait_send()

    names = list(dict.fromkeys(job[0] for job in jobs))
    return _Carried({f: fulls[f] for f in names}, {}, 3 * len(jobs), start, finish)


def _gather_ring_comm(fulls, f, r0, nr, phase, specs):
    kind, _, cols = specs[f]
    assert kind == "col" and nr % 32 == 0
    stride = cols // N_CHIPS
    half = nr // 2

    def rows(refs, chip, c, lo, n):
        return _rows_of(_half(_shard_of(refs[f], kind, chip * stride, stride), kind, c), r0 + lo, n)

    def copies(refs, ss, rs):
        x, y, c = _position()
        me, nx, ny, dg = 2 * x + y, 2 * (1 - x) + y, 2 * x + (1 - y), 2 * (1 - x) + (1 - y)
        to_x, to_y = (1 - x, y, c), (x, 1 - y, c)
        if phase == "a":
            mine = rows(refs, me, c, 0, nr)
            return [(_remote(mine, mine, ss.at[0], rs.at[0], to_x), rows(refs, nx, c, 0, nr)),
                    (_remote(mine, mine, ss.at[1], rs.at[1], to_y), rows(refs, ny, c, 0, nr))]
        up, low = rows(refs, ny, c, half, half), rows(refs, nx, c, 0, half)
        return [(_remote(up, up, ss.at[0], rs.at[0], to_x), rows(refs, dg, c, half, half)),
                (_remote(low, low, ss.at[1], rs.at[1], to_y), rows(refs, dg, c, 0, half))]

    def start(refs, ss, rs):
        for cp, _ in copies(refs, ss, rs):
            cp.start()

    def finish(refs, ss, rs):
        x, y, c = _position()
        mine = copies(refs, ss, rs)
        for i, (_, landing) in enumerate(mine):
            _remote(landing, landing, ss.at[i], rs.at[i], (x, y, c)).wait_recv()
        for cp, _ in mine:
            cp.wait_send()

    return _Carried({f: fulls[f]}, {}, 2, start, finish)


def _gather_d2d_comm(fulls, jobs, specs):
    def rect(refs, job, h):
        f, r0, nr = job
        assert nr is None or specs[f][0] == "col"
        return _rows_of(_half(refs[f], specs[f][0], h), r0, nr)

    def start(refs, ss, rs):
        x, y, c = _position()
        for q, job in enumerate(jobs):
            _remote(rect(refs, job, c), rect(refs, job, c), ss.at[q], rs.at[q], (x, y, 1 - c)).start()

    def finish(refs, ss, rs):
        x, y, c = _position()
        for q, job in enumerate(jobs):
            _remote(rect(refs, job, 1 - c), rect(refs, job, 1 - c), ss.at[q], rs.at[q], (x, y, 1 - c)).wait_recv()
        for q, job in enumerate(jobs):
            _remote(rect(refs, job, c), rect(refs, job, c), ss.at[q], rs.at[q], (x, y, 1 - c)).wait_send()

    names = list(dict.fromkeys(job[0] for job in jobs))
    return _Carried({f: fulls[f] for f in names}, {}, len(jobs), start, finish)


def _pairx_comm(grads, names, specs, whole=False):
    def copies(refs, ss, rs):
        x, y, c = _position()
        src = (lambda f: refs[("g", f)]) if whole else (lambda f: _half(refs[("g", f)], specs[f][0], 1 - c))
        return [_remote(src(f), refs[("r", f)], ss.at[i], rs.at[i], (x, y, 1 - c)) for i, f in enumerate(names)]

    def start(refs, ss, rs):
        for cp in copies(refs, ss, rs):
            cp.start()

    def finish(refs, ss, rs):
        for cp in copies(refs, ss, rs):
            cp.wait_recv()
        for cp in copies(refs, ss, rs):
            cp.wait_send()

    fresh = {("r", f): jax.ShapeDtypeStruct(_half_shape(*specs[f]), BF16) for f in names}
    return _Carried({}, fresh, len(names), start, finish, reads={("g", f): grads[f] for f in names})


def _chipx_comm(pair_sums, slots, jobs, specs):
    def copies(refs, ss, rs):
        x, y, c = _position()
        out = []
        for q, (f, r0, nr) in enumerate(jobs):
            kind = specs[f][0]
            width = _slot_shape(specs[f])[1 if kind == "col" else 0]
            for p, (px, py) in enumerate(_other_chips(x, y)):
                src = _rows_of(_shard_of(refs[("p", f)], kind, (2 * px + py) * width, width), r0, nr)
                dst = _rows_of(refs[("s", f)].at[p], r0, nr)
                out.append(_remote(src, dst, ss.at[3 * q + p], rs.at[3 * q + p], (px, py, c)))
        return out

    def start(refs, ss, rs):
        for cp in copies(refs, ss, rs):
            cp.start()

    def finish(refs, ss, rs):
        for cp in copies(refs, ss, rs):
            cp.wait_recv()
        for cp in copies(refs, ss, rs):
            cp.wait_send()

    names = list(dict.fromkeys(job[0] for job in jobs))
    arrays = {("p", f): pair_sums[f] for f in names}
    arrays.update({("s", f): slots[f] for f in names})
    return _Carried(arrays, {}, 3 * len(jobs), start, finish)


def _share_comm(grads, wnames, specs, place):
    def start(refs, ss, rs):
        x, y, c = _position()
        for i, w in enumerate(wnames):
            kind = specs[place[w][0]][0]
            _remote(_half(refs[w], kind, c), _half(refs[w], kind, c), ss.at[i], rs.at[i], (x, y, 1 - c)).start()

    def finish(refs, ss, rs):
        x, y, c = _position()
        for i, w in enumerate(wnames):
            kind = specs[place[w][0]][0]
            _remote(_half(refs[w], kind, 1 - c), _half(refs[w], kind, 1 - c), ss.at[i], rs.at[i],
                    (x, y, 1 - c)).wait_recv()
        for i, w in enumerate(wnames):
            kind = specs[place[w][0]][0]
            _remote(_half(refs[w], kind, c), _half(refs[w], kind, c), ss.at[i], rs.at[i], (x, y, 1 - c)).wait_send()

    return _Carried({w: grads[w] for w in wnames}, {}, len(wnames), start, finish)


def _gather_rows(v, name="gather_small"):
    shape = v.shape

    def body(v_ref, out_ref, send_sem, recv_sem, loc_sem):
        x, y, c = _position()
        me = 4 * x + 2 * y + c
        flips = [(fx, fy, fc) for fx in (0, 1) for fy in (0, 1) for fc in (0, 1)][1:]

        def peer(fl):
            return tuple(1 - a if f else a for a, f in zip((x, y, c), fl))

        loc = pltpu.make_async_copy(v_ref, out_ref.at[me], loc_sem)
        loc.start()
        sends = []
        for i, fl in enumerate(flips):
            cp = _remote(v_ref, out_ref.at[me], send_sem.at[i], recv_sem.at[i], peer(fl))
            cp.start()
            sends.append(cp)
        for i, fl in enumerate(flips):
            px, py, pc = peer(fl)
            _remote(v_ref, out_ref.at[4 * px + 2 * py + pc], send_sem.at[i], recv_sem.at[i], peer(fl)).wait_recv()
        for cp in sends:
            cp.wait_send()
        loc.wait()

    return pl.pallas_call(
        body, name=name, in_specs=[ANY], out_specs=ANY,
        out_shape=jax.ShapeDtypeStruct((N_DEV,) + shape, F32),
        scratch_shapes=[pltpu.SemaphoreType.DMA((N_DEV - 1,)), pltpu.SemaphoreType.DMA((N_DEV - 1,)),
                        pltpu.SemaphoreType.DMA],
    )(v)


def _ew_block(rows, cols, elems=EW_BLOCK_ELEMS):
    tc = cols if cols <= 4096 else _div(cols, 2048, LANE)
    tr = _div(rows, max(16, elems // tc), 16)
    return tr, tc


def _mesh_scalars():
    x, y, c = _position()
    return jnp.stack([c, 2 * x + y]).astype(jnp.int32)


def _grid_spec(grid, in_specs, out_specs):
    return pltpu.PrefetchScalarGridSpec(num_scalar_prefetch=1, grid=grid, in_specs=in_specs, out_specs=out_specs)


def _cast_into_full(parts, fname, pos, specs, place, name, token=None):
    kind, rows, cols = specs[fname]
    ws = [w for w in place if place[w][0] == fname]
    if kind == "col":
        stride = cols // N_CHIPS
        hr = rows // 2
        tr = _div(hr, max(16, EW_BLOCK_ELEMS // stride), 16)
        nrb = hr // tr
        in_specs = [pl.BlockSpec((tr, parts[w].shape[1]), lambda i, pos_ref: (i + pos_ref[0] * nrb, 0)) for w in ws]
        out_spec = pl.BlockSpec((tr, stride), lambda i, pos_ref: (i + pos_ref[0] * nrb, pos_ref[1]))
    else:
        stride = rows // N_CHIPS
        hc = cols // 2
        tr = _div(stride, max(16, EW_BLOCK_ELEMS // hc), 16)
        nrb = stride // tr
        in_specs = [pl.BlockSpec((tr, hc), lambda i, pos_ref: (i, pos_ref[0])) for w in ws]
        out_spec = pl.BlockSpec((tr, hc), lambda i, pos_ref: (i + pos_ref[1] * nrb, pos_ref[0]))

    def kern(pos_ref, *refs):
        o_ref = refs[-1]
        for w, r in zip(ws, refs[:len(ws)]):
            off = place[w][1] if kind == "col" else 0
            o_ref[:, off:off + r.shape[1]] = r[...].astype(o_ref.dtype)

    tokens = [] if token is None else [token]
    in_specs = in_specs + [pl.BlockSpec(TOKEN_SHAPE, lambda i, pos_ref: (0, 0))] * len(tokens)
    return pl.pallas_call(
        kern, name=name, grid_spec=_grid_spec((nrb,), in_specs, out_spec),
        out_shape=jax.ShapeDtypeStruct((rows, cols), BF16),
        compiler_params=_cparams(("parallel",)),
    )(pos, *[parts[w] for w in ws], *tokens)


def _pair_sum(grad, recv, pos, spec, name, whole=False):
    kind, rows, cols = spec
    hr, hc = _half_shape(kind, rows, cols)
    tr, tc = _ew_block(hr, hc, 2 * EW_BLOCK_ELEMS)
    nrb, ncb = hr // tr, hc // tc
    blk = pl.BlockSpec((tr, tc), lambda i, jj, pos_ref: (i, jj))
    if whole:
        mine = blk
    elif kind == "col":
        mine = pl.BlockSpec((tr, tc), lambda i, jj, pos_ref: (i + pos_ref[0] * nrb, jj))
    else:
        mine = pl.BlockSpec((tr, tc), lambda i, jj, pos_ref: (i, jj + pos_ref[0] * ncb))

    def kern(pos_ref, g_ref, r_ref, o_ref, slots_ref):
        o_ref[...] = (g_ref[...].astype(F32) + r_ref[...].astype(F32)).astype(o_ref.dtype)

    return pl.pallas_call(
        kern, name=name, grid_spec=_grid_spec((nrb, ncb), [mine, blk], [blk, ANY]),
        out_shape=[jax.ShapeDtypeStruct((hr, hc), BF16),
                   jax.ShapeDtypeStruct((N_CHIPS - 1,) + _slot_shape(spec), BF16)],
        compiler_params=_cparams(("parallel", "parallel")),
    )(pos, grad, recv)


def _chip_sum(pair_sum, slots, pos, fname, shard_shapes, specs, place, name):
    kind, rows, cols = specs[fname]
    sr, sc = _slot_shape(specs[fname])
    ws = [w for w in place if place[w][0] == fname]
    n_slots = N_CHIPS - 1
    tr = _div(sr, max(16, EW_BLOCK_ELEMS // sc), 16)
    nrb = sr // tr
    slot = pl.BlockSpec((n_slots, tr, sc), lambda i, pos_ref: (0, i, 0))
    if kind == "col":
        own = pl.BlockSpec((tr, sc), lambda i, pos_ref: (i, pos_ref[1]))
        out_specs = [pl.BlockSpec((tr, shard_shapes[w][1]), lambda i, pos_ref: (i + pos_ref[0] * nrb, 0)) for w in ws]
    else:
        own = pl.BlockSpec((tr, sc), lambda i, pos_ref: (i + pos_ref[1] * nrb, 0))
        out_specs = [pl.BlockSpec((tr, sc), lambda i, pos_ref: (i, pos_ref[0])) for w in ws]

    def kern(pos_ref, own_ref, slot_ref, *out_refs):
        tot = own_ref[...].astype(F32)
        for s in range(n_slots):
            tot = tot + slot_ref[s].astype(F32)
        for w, o_ref in zip(ws, out_refs):
            off = place[w][1] if kind == "col" else 0
            o_ref[...] = tot[:, off:off + o_ref.shape[1]]

    outs = pl.pallas_call(
        kern, name=name, grid_spec=_grid_spec((nrb,), [own, slot], out_specs),
        out_shape=[jax.ShapeDtypeStruct(shard_shapes[w], F32) for w in ws],
        compiler_params=_cparams(("parallel",)),
    )(pos, pair_sum, slots)
    return dict(zip(ws, outs))


def _adam_math(w, g, m, v):
    m2 = ADAM_B1 * m + (1.0 - ADAM_B1) * g
    v2 = ADAM_B2 * v + (1.0 - ADAM_B2) * (g * g)
    m_hat = m2 / (1.0 - ADAM_B1 ** ADAM_STEP)
    v_hat = v2 / (1.0 - ADAM_B2 ** ADAM_STEP)
    delta = -ADAM_LR * (m_hat / (jnp.sqrt(v_hat) + ADAM_EPS) + ADAM_WD * w)
    return delta, m2, v2


def _adamw(w, g, m, v, name, carried=()):
    rows, cols = w.shape
    tr, tc = _ew_block(rows, cols)

    def kern(w_ref, g_ref, m_ref, v_ref, d_ref, m2_ref, v2_ref, g_out_ref):
        g_ = g_ref[...]
        d_ref[...], m2_ref[...], v2_ref[...] = _adam_math(w_ref[...], g_, m_ref[...], v_ref[...])
        g_out_ref[...] = g_

    blk = pl.BlockSpec((tr, tc), lambda i, j: (i, j))
    return _pcall(
        kern, name=name, grid=(rows // tr, cols // tc), in_specs=[blk] * 4, out_specs=[blk] * 4,
        out_shape=[jax.ShapeDtypeStruct((rows, cols), F32)] * 4, args=(w, g, m, v),
        semantics=("parallel", "parallel"), carried=carried)


SMALL_ROWS = ("ln_mix_w", "ln_cross_w", "ln_mem_w", "ln_ffn_w", "ln_final_w")
ROW_HG_NORM, ROW_LB0, ROW_LB1 = 5, 6, 7
LOSS_LANE0 = HEAD_DIM


def _pack_small(vals):
    rows = [vals[n].reshape(1, D_MODEL) for n in SMALL_ROWS]
    pad = lambda a: jnp.pad(a, ((0, 0), (0, D_MODEL - a.shape[1])))
    rows.append(pad(vals["hg_norm_w"].reshape(1, HEAD_DIM)))
    rows.append(pad(vals["hg_lower_bounds"].reshape(2, HG_WIDTH)))
    return jnp.concatenate(rows, axis=0)


def _small_update(gathered, w, m, v, name="small_update"):
    def kern(g_ref, w_ref, m_ref, v_ref, grad_ref, d_ref, m2_ref, v2_ref, loss_ref):
        tot = g_ref[0]
        for s in range(1, N_DEV):
            tot = tot + g_ref[s]
        wv = w_ref[...]
        row = lax.broadcasted_iota(jnp.int32, (8, D_MODEL), 0)
        lane = lax.broadcasted_iota(jnp.int32, (8, D_MODEL), 1)
        l0, l1 = wv[ROW_LB0:ROW_LB0 + 1], wv[ROW_LB1:ROW_LB1 + 1]
        mx = jnp.maximum(l0, l1)
        e0, e1 = jnp.exp(l0 - mx), jnp.exp(l1 - mx)
        p0 = e0 / (e0 + e1)
        dlog = tot[ROW_LB0:ROW_LB0 + 1] * p0 * (1.0 - p0)
        tot = jnp.where(row == ROW_HG_NORM, tot + tot[ROW_LB1:ROW_LB1 + 1], tot)
        grad = jnp.where(row == ROW_LB0, dlog, jnp.where(row == ROW_LB1, -dlog, tot))
        grad = jnp.where((row == ROW_HG_NORM) & (lane >= HEAD_DIM), 0.0, grad)
        grad = jnp.where((row >= ROW_LB0) & (lane >= HG_WIDTH), 0.0, grad)
        grad_ref[...] = grad
        d_ref[...], m2_ref[...], v2_ref[...] = _adam_math(wv, grad, m_ref[...], v_ref[...])
        loss_ref[...] = tot[ROW_HG_NORM:ROW_HG_NORM + 1, LOSS_LANE0:LOSS_LANE0 + LANE]

    full = pl.BlockSpec((8, D_MODEL), lambda: (0, 0))
    return pl.pallas_call(
        kern, name=name,
        in_specs=[pl.BlockSpec((N_DEV, 8, D_MODEL), lambda: (0, 0, 0)), full, full, full],
        out_specs=[full, full, full, full, pl.BlockSpec((1, LANE), lambda: (0, 0))],
        out_shape=[jax.ShapeDtypeStruct((8, D_MODEL), F32)] * 4 + [jax.ShapeDtypeStruct((1, LANE), F32)],
        compiler_params=_cparams(),
    )(gathered, w, m, v)


def _unpack_small(p, shapes):
    out = {n: p[i].reshape(shapes[n]) for i, n in enumerate(SMALL_ROWS)}
    out["hg_norm_w"] = p[ROW_HG_NORM, :HEAD_DIM].reshape(shapes["hg_norm_w"])
    out["hg_lower_bounds"] = p[ROW_LB0:ROW_LB1 + 1, :HG_WIDTH].reshape(shapes["hg_lower_bounds"])
    return out


WHOLE = lambda f: (f, 0, None)
MID_MATRICES = ("w_branch_a", "w_branch_b", "w_out", "wq_cross", "wkv_cross", "wo_cross")
MID_WEIGHTS = MID_MATRICES
W_IN_PIECES = [("w_in", r0, 512) for r0 in range(0, D_MODEL // 2, 512)]
W13_PIECES = [("w13", r0, 512) for r0 in range(0, D_MODEL // 2, 512)]
GATHER_GROUPS = [("mid", [WHOLE(f) for f in MID_MATRICES]), ("w13a", W13_PIECES[:1]), ("w13b", W13_PIECES[1:]),
                 ("w2", [WHOLE("w2")])]
OTHER_WEIGHTS = ["w1", "w3", "w2"] + list(MID_WEIGHTS)
BEFORE = {
    "hgrn_fwd": [("wait", "mid")],
    "gate_fwd": [("wait", "w13a")],
    "mm_o": [("wait", "w13b")],
    "mm_w13": [("wait", "w2")],
    "mm_dh": [("wait", "rs_w2"), ("chip_sum", "w2"), ("wait", "rs_w13"), ("chip_sum", "w13"), ("wait", "rs_mid")]
    + [("chip_sum", f) for f in MID_MATRICES],
}
CARRY = {
    "hgrn_fwd": [("d2d", [WHOLE(f) for f in MID_MATRICES])],
    "gate_fwd": [("d2d", W13_PIECES[:1])],
    "mm_o": [("d2d", W13_PIECES[1:])],
    "mm_w13": [("d2d", [WHOLE("w2")])],
    "swiglu_bwd": [("pairx", ["w2"])],
    "mm_dhf": [("pairx", ["w13"])],
    "attn_merge_bwd": [("pairx", list(MID_MATRICES))],
    "mm_dwin_own": [("pairx_whole", ["w_in"])],
    "mm_dh": [("share", OTHER_WEIGHTS)],
}
AFTER = {
    "swiglu_bwd": [("pair_sum", "w2"), ("start", "rs_w2", [WHOLE("w2")])],
    "mm_dhf": [("pair_sum", "w13"), ("start", "rs_w13", [WHOLE("w13")])],
    "attn_merge_bwd": [("pair_sum", f) for f in MID_MATRICES] + [("start", "rs_mid", [WHOLE(f) for f in MID_MATRICES])],
    "mm_dwin_own": [("pair_sum", "w_in"), ("start", "rs_w_in", [WHOLE("w_in")])],
}
FINISH = [
    ("adamw", OTHER_WEIGHTS), ("wait", "rs_w_in"), ("small",), ("chip_sum", "w_in"),
    ("run", ("share", ["w_in"]), "rs_sibling_share_w_in"), ("adamw", ["w_in"]),
]


class _Net:
    def __init__(self, full, pos=None, shard_shapes=None, comm=True, specs=FULL_SPECS, place=WEIGHT_PLACE):
        self.full, self.pos, self.shard_shapes, self.comm = dict(full), pos, shard_shapes, comm
        self.specs, self.place = specs, place
        self.gw, self.recv, self.psum, self.slots, self.grads = {}, {}, {}, {}, {}
        self.gw_sibling = {}
        self.pending, self.token, self.last = {}, None, None

    def _make(self, kind, arg):
        if kind == "gather":
            return _gather_ici_comm(self.full, arg, self.specs)
        if kind == "ring":
            return _gather_ring_comm(self.full, *arg, self.specs)
        if kind == "d2d":
            return _gather_d2d_comm(self.full, arg, self.specs)
        if kind == "pairx":
            return _pairx_comm(self.gw, arg, self.specs)
        if kind == "pairx_whole":
            return _pairx_comm(self.gw_sibling, arg, self.specs, whole=True)
        if kind == "chipx":
            return _chipx_comm(self.psum, self.slots, arg, self.specs)
        assert kind == "share"
        return _share_comm(self.grads, arg, self.specs, self.place)

    def _store(self, kind, res):
        if kind in ("gather", "ring", "d2d"):
            self.full.update(res)
        elif kind in ("pairx", "pairx_whole"):
            for (tag, f), a in res.items():
                (self.gw if tag == "g" else self.recv)[f] = a
        elif kind == "chipx":
            for (tag, f), a in res.items():
                (self.psum if tag == "p" else self.slots)[f] = a
        else:
            self.grads.update(res)

    def run_comm(self, item, name):
        kind, arg = item
        self._store(kind, _run_comm([self._make(kind, arg)], name)[0])

    @staticmethod
    def _others(after, items):
        own = [a for cm in items for a in cm.arrays.values()]
        return [a for a in after if a is not None and all(a is not o for o in own)]

    def start(self, groups, kind, name):
        items = [self._make(kind, jobs) for _, jobs in groups]
        after = self._others([self.last], items)
        res, sems, token = _split_start(items, name, after=after[0] if after else None)
        for (group, jobs), r, s in zip(groups, res, sems):
            self._store(kind, r)
            self.pending[group] = (kind, jobs, s)
        self.token = self.last = token

    def wait(self, group, after=()):
        kind, jobs, sems = self.pending.pop(group)
        item = self._make(kind, jobs)
        res = _split_wait([item], [sems], self._others([self.last, *after], [item]), f"wait_{group}")[0]
        self._store(kind, res)

    def step(self, step):
        if step[0] == "wait":
            self.wait(step[1])
        elif step[0] == "start":
            self.start([(step[1], step[2])], "chipx", f"start_{step[1]}")
        elif step[0] == "pair_sum":
            f = step[1]
            self.psum[f], self.slots[f] = _pair_sum(self.gw[f], self.recv[f], self.pos, self.specs[f],
                                                    f"rs_pair_sum_{f}", whole=f in self.gw_sibling)
        elif step[0] == "chip_sum":
            f = step[1]
            self.grads.update(_chip_sum(self.psum[f], self.slots[f], self.pos, f, self.shard_shapes,
                                        self.specs, self.place, f"rs_chip_sum_{f}"))
        else:
            assert step[0] == "run"
            self.run_comm(step[1], step[2])

    def call(self, fn, name, *args, grad_of=None, sibling_half=False, **kw):
        for step in (BEFORE.get(name, []) if self.comm else []):
            self.step(step)
        items = CARRY.get(name, []) if self.comm else []
        carried = [self._make(k, a) for k, a in items]
        if self.token is not None:
            carried.append(_Token(self.token))
            self.token = None
        out, res = fn(*args, name=name, carried=carried, **kw)
        if grad_of is not None:
            (self.gw_sibling if sibling_half else self.gw)[grad_of] = out
        self.last = jax.tree.leaves(out)[0]
        for (kind, _), r in zip(items, res):
            self._store(kind, r)
        for step in (AFTER.get(name, []) if self.comm else []):
            self.step(step)
        return out


def _local_step(net, x, h, mem, target, small):
    full, call = net.full, net.call
    proj = call(_mm, "mm_proj", h, full["w_in"], mode="nn", out_dtype=F32)
    att = [call(_attn_fwd, f"attn_fwd_g{g}", proj, g) for g in range(3)]
    outs, lses = [a[0] for a in att], [a[1] for a in att]
    o_att = _attn_merge_fwd(outs, lses, "attn_merge")
    oraw, o_hg, states = call(_hg_fwd, "hgrn_fwd", proj, small["hg_lower_bounds"], small["hg_norm_w"])
    ya = call(_mm, "mm_branch_a", o_att, full["w_branch_a"], mode="nn", out_dtype=F32)
    yb = call(_mm, "mm_branch_b", o_hg, full["w_branch_b"], mode="nn", out_dtype=F32)
    merged = call(_gate_fwd, "gate_fwd", proj, ya, yb)
    x1 = call(_mm, "mm_out", merged, full["w_out"], mode="nn", out_dtype=F32, res=x)

    hc = _rms_fwd(x1, small["ln_cross_w"], "rms_cross")
    mn = _rms_fwd(mem, small["ln_mem_w"], "rms_mem")
    qc = call(_mm, "mm_q", hc, full["wq_cross"], mode="nn", out_dtype=F32)
    kvc = call(_mm, "mm_kv", mn, full["wkv_cross"], mode="nn", out_dtype=F32)
    oc = _cross_fwd(qc, kvc, "cross_fwd")
    x2 = call(_mm, "mm_o", oc, full["wo_cross"], mode="nn", out_dtype=F32, res=x1)

    hf = _rms_fwd(x2, small["ln_ffn_w"], "rms_ffn")
    ab = call(_mm, "mm_w13", hf, full["w13"], mode="nn", out_dtype=F32)
    u = _swiglu_fwd(ab, "swiglu_fwd")
    x3 = call(_mm, "mm_w2", u, full["w2"], mode="nn", out_dtype=F32, res=x2)

    dx3, dg_final, loss = _loss_head(x3, small["ln_final_w"], target, "loss_head")

    gs = {"ln_final_w": dg_final}
    du = call(_mm, "mm_du", dx3, full["w2"], mode="nt", out_dtype=F32)
    call(_mm, "mm_dw2", u, dx3, mode="tn", out_dtype=BF16, grad_of="w2")
    dab = call(_swiglu_bwd, "swiglu_bwd", ab, du)
    call(_mm, "mm_dw13", hf, dab, mode="tn", out_dtype=BF16, grad_of="w13")
    dhf = call(_mm, "mm_dhf", dab, full["w13"], mode="nt", out_dtype=F32)
    dx2, gs["ln_ffn_w"] = _rms_bwd(x2, small["ln_ffn_w"], dhf, dx3, "rms_ffn_bwd")
    doc = call(_mm, "mm_doc", dx2, full["wo_cross"], mode="nt", out_dtype=BF16)
    call(_mm, "mm_dwo", oc, dx2, mode="tn", out_dtype=BF16, grad_of="wo_cross")
    dqc, dkvc = _cross_bwd(qc, kvc, doc, "cross_bwd")
    call(_mm, "mm_dwq", hc, dqc, mode="tn", out_dtype=BF16, grad_of="wq_cross")
    dhc = call(_mm, "mm_dhc", dqc, full["wq_cross"], mode="nt", out_dtype=F32)
    call(_mm, "mm_dwkv", mn, dkvc, mode="tn", out_dtype=BF16, grad_of="wkv_cross")
    dmn = call(_mm, "mm_dmn", dkvc, full["wkv_cross"], mode="nt", out_dtype=F32)
    _, gs["ln_mem_w"] = _rms_bwd(mem, small["ln_mem_w"], dmn, None, "rms_mem_bwd")
    dx1, gs["ln_cross_w"] = _rms_bwd(x1, small["ln_cross_w"], dhc, dx2, "rms_cross_bwd")
    dmerged = call(_mm, "mm_dmerged", dx1, full["w_out"], mode="nt", out_dtype=F32)
    call(_mm, "mm_dwout", merged, dx1, mode="tn", out_dtype=BF16, grad_of="w_out")
    dya, dyb, dga, dgb = call(_gate_bwd, "gate_bwd", proj, ya, yb, dmerged)
    call(_mm, "mm_dwa", o_att, dya, mode="tn", out_dtype=BF16, grad_of="w_branch_a")
    do_att = call(_mm, "mm_doatt", dya, full["w_branch_a"], mode="nt", out_dtype=F32)
    call(_mm, "mm_dwb", o_hg, dyb, mode="tn", out_dtype=BF16, grad_of="w_branch_b")
    do_hg = call(_mm, "mm_dohg", dyb, full["w_branch_b"], mode="nt", out_dtype=F32)
    dqh, dfh, dih, dgh, dlb, gs["hg_norm_w"] = call(
        _hg_bwd, "hgrn_bwd", proj, small["hg_lower_bounds"], small["hg_norm_w"], oraw, states, do_hg)
    gs["hg_lb"] = dlb
    do_gs, dl_gs = call(_attn_merge_bwd, "attn_merge_bwd", outs, lses, do_att)
    dqs, dks, dvs = zip(*[call(_attn_bwd, f"attn_bwd_g{g}", proj, g, lses[g], do_gs[g], dl_gs[g]) for g in range(3)])
    dproj = jnp.concatenate([*dqs, *dks, *dvs, dqh, dfh, dih, dgh, dga, dgb], axis=1)
    if net.comm:
        half = D_MODEL // 2
        c = lax.axis_index("c")
        h_sibling = lax.dynamic_slice_in_dim(h, (1 - c) * half, half, axis=1)
        h_own = lax.dynamic_slice_in_dim(h, c * half, half, axis=1)
        call(_mm, "mm_dwin_sibling", h_sibling, dproj, mode="tn", out_dtype=BF16, grad_of="w_in", sibling_half=True)
        call(_mm, "mm_dwin_own", h_own, dproj, mode="tn", out_dtype=BF16, grad_of="w_in")
    else:
        call(_mm, "mm_dwin", h, dproj, mode="tn", out_dtype=BF16, grad_of="w_in")
    dh = call(_mm, "mm_dh", dproj, full["w_in"], mode="nt", out_dtype=F32)
    dx, gs["ln_mix_w"] = _rms_bwd(x, small["ln_mix_w"], dh, dx1, "rms_mix_bwd")
    return loss, dx, gs


WEIGHT_ORDER = ("ln_mix_w", "w_in", "hg_norm_w", "hg_lower_bounds", "w_branch_a", "w_branch_b", "w_out",
                "ln_cross_w", "ln_mem_w", "wq_cross", "wkv_cross", "wo_cross", "ln_ffn_w", "w1", "w3", "w2",
                "ln_final_w")


def kernel(x, mem, ln_mix_w, w_in, hg_norm_w, hg_lower_bounds, w_branch_a, w_branch_b, w_out, ln_cross_w, ln_mem_w, wq_cross, wkv_cross, wo_cross, ln_ffn_w, w1, w3, w2, ln_final_w, loss_target, m_ln_mix_w, m_w_in, m_hg_norm_w, m_hg_lower_bounds, m_w_branch_a, m_w_branch_b, m_w_out, m_ln_cross_w, m_ln_mem_w, m_wq_cross, m_wkv_cross, m_wo_cross, m_ln_ffn_w, m_w1, m_w3, m_w2, m_ln_final_w, v_ln_mix_w, v_w_in, v_hg_norm_w, v_hg_lower_bounds, v_w_branch_a, v_w_branch_b, v_w_out, v_ln_cross_w, v_ln_mem_w, v_wq_cross, v_wkv_cross, v_wo_cross, v_ln_ffn_w, v_w1, v_w3, v_w2, v_ln_final_w):
    args = dict(locals())
    w = {n: args[n] for n in WEIGHT_ORDER}
    m = {n: args["m_" + n] for n in WEIGHT_ORDER}
    v = {n: args["v_" + n] for n in WEIGHT_ORDER}
    shapes = {n: w[n].shape for n in WEIGHT_ORDER}
    mat = lambda a: a.reshape(a.shape[-2:])
    shard_shapes = {n: shapes[n][-2:] for n in BIG_WEIGHTS}

    pos = _mesh_scalars()

    def cast(f, token=None):
        return _cast_into_full({n: mat(w[n]) for n in BIG_WEIGHTS if WEIGHT_PLACE[n][0] == f}, f, pos,
                               FULL_SPECS, WEIGHT_PLACE, f"cast_{f}", token)

    net = _Net({"w_in": cast("w_in")}, pos, shard_shapes)
    net.start([(f"ring_a{i}", (*job, "a")) for i, job in enumerate(W_IN_PIECES)], "ring", "gather_start_w_in")
    rest = {f: cast(f, net.token) for f in FULL_SPECS if f != "w_in"}
    net.full.update(rest)
    small = {n: w[n].reshape(1, -1) for n in SMALL_ROWS}
    small["hg_norm_w"] = w["hg_norm_w"].reshape(1, HEAD_DIM)
    small["hg_lower_bounds"] = w["hg_lower_bounds"]
    x2d = x.reshape(SEQ, D_MODEL)
    h = _rms_fwd(x2d, small["ln_mix_w"], "rms_mix")
    for i, job in enumerate(W_IN_PIECES):
        net.wait(f"ring_a{i}", after=[*rest.values(), h] if i == 0 else ())
        net.start([(f"ring_b{i}", (*job, "b"))], "ring", f"gather_pass_on_w_in{i}")
    net.start(GATHER_GROUPS, "gather", "gather_start_rest")
    for i, job in enumerate(W_IN_PIECES):
        net.wait(f"ring_b{i}")
        net.run_comm(("d2d", [job]), f"gather_hand_over_w_in{i}")
    loss, dx, gs = _local_step(net, x2d, h, mem.reshape(MEM_LEN, D_MODEL), loss_target.reshape(SEQ, D_MODEL), small)

    out_g, out_d, out_m, out_v = {}, {}, {}, {}
    net.last = dx
    for step in FINISH:
        if step[0] == "adamw":
            for n in step[1]:
                out_d[n], out_m[n], out_v[n], out_g[n] = net.call(_adamw, f"adamw_{n}", mat(w[n]), net.grads[n],
                                                                  mat(m[n]), mat(v[n]))
        elif step[0] == "wait":
            net.wait(step[1], after=list(out_d.values()))
        elif step[0] == "small":
            pad = lambda a: jnp.pad(a, ((0, 0), (0, D_MODEL - a.shape[1])))
            part = jnp.concatenate(
                [gs[n] for n in SMALL_ROWS]
                + [pad(jnp.concatenate([gs["hg_norm_w"][0], loss], axis=1)), pad(gs["hg_lb"]),
                   pad(gs["hg_norm_w"][1]) if len(gs["hg_norm_w"]) > 1 else jnp.zeros((1, D_MODEL), F32)], axis=0)
            part, net.psum["w_in"] = lax.optimization_barrier((part, net.psum["w_in"]))
            sg, sd, sm, sv, loss_tot = _small_update(_gather_rows(part), _pack_small(w), _pack_small(m),
                                                     _pack_small(v))
            for dst, packed in ((out_g, sg), (out_d, sd), (out_m, sm), (out_v, sv)):
                dst.update(_unpack_small(packed, shapes))
        else:
            net.step(step)

    result = [loss_tot[0, 0], dx.reshape(x.shape)]
    for group in (out_g, out_d, out_m, out_v):
        result += [group[n].reshape(shapes[n]) for n in WEIGHT_ORDER]
    return tuple(result)
```

```python
import math

import jax
import jax.numpy as jnp
from jax import lax
from jax.experimental import pallas as pl
from jax.experimental.pallas import tpu as pltpu

F32 = jnp.float32
BF16 = jnp.bfloat16
MESH = pl.DeviceIdType.MESH

D_MODEL = 2048
SEQ = 2048
HEAD_DIM = 128
MEM_LEN = 256
ATT_GROUPS = ((128, 1), (512, 4), (2048, 16))
ATT_HEADS = 4
ATT_WIDTH = 3 * ATT_HEADS * HEAD_DIM
ATT_OUT = ATT_HEADS * HEAD_DIM
ATT_BLOCK = 128
HG_HEADS = 8
HG_WIDTH = HG_HEADS * HEAD_DIM
HG_CHUNK = 64
IN_WIDTH = 3 * ATT_WIDTH + 4 * HG_WIDTH + 2 * D_MODEL
CROSS_HEADS = 4
CROSS_WIDTH = CROSS_HEADS * HEAD_DIM
D_FF = 5632
RMS_EPS = 1e-6
ADAM_LR = 0.001
ADAM_B1 = 0.9
ADAM_B2 = 0.999
ADAM_EPS = 1e-08
ADAM_WD = 0.01
ADAM_STEP = 10
N_CHIPS = 4
N_DEV = 8

VMEM_LIMIT_BYTES = 56 * 1024 * 1024
LANE = 128
MXU_WIDTH = 256
MM_TILE_CAP = 1536
TRANSPOSE_CHUNK = 512
ANY = pl.BlockSpec(memory_space=pl.ANY)


def _cparams(sem=None):
    return pltpu.CompilerParams(dimension_semantics=sem, vmem_limit_bytes=VMEM_LIMIT_BYTES)


def _div(n, cap, mult):
    best = None
    for d in range(mult, min(n, cap) + 1, mult):
        if n % d == 0:
            best = d
    assert best is not None, (n, cap, mult)
    return best


def _sigmoid(x):
    return 1.0 / (1.0 + jnp.exp(-x))


def _dot(a, b):
    return jnp.dot(a.astype(BF16), b.astype(BF16), preferred_element_type=F32)


def _dot_nt(a, b):
    return lax.dot_general(a.astype(BF16), b.astype(BF16), (((1,), (1,)), ((), ())),
                           preferred_element_type=F32)


def _dot_tn(a, b):
    return jnp.dot(a.astype(F32).T.astype(BF16), b.astype(BF16), preferred_element_type=F32)


def _dot_exact(a, b):
    return jnp.dot(a, b, precision=lax.Precision.HIGHEST, preferred_element_type=F32)


class _Carried:
    def __init__(self, arrays, fresh, n_sems, start, finish, mid=None, reads=None):
        self.arrays, self.fresh, self.n_sems, self.reads = arrays, fresh, n_sems, reads or {}
        self.start, self.mid, self.finish = start, mid, finish


class _Token:
    def __init__(self, array):
        self.array = array


TOKEN_SHAPE = (8, LANE)


def _carried_layout(carried):
    akeys = list(dict.fromkeys(k for cm in carried for k in cm.arrays))
    fkeys = [(ci, k) for ci, cm in enumerate(carried) for k in cm.fresh]
    arrays = [next(cm.arrays[k] for cm in carried if k in cm.arrays) for k in akeys]
    shapes = [jax.ShapeDtypeStruct(a.shape, a.dtype) for a in arrays] + [carried[ci].fresh[k] for ci, k in fkeys]
    sems = []
    for cm in carried:
        sems += [pltpu.SemaphoreType.DMA((cm.n_sems,)), pltpu.SemaphoreType.DMA((cm.n_sems,))]
    return akeys, fkeys, arrays, shapes, sems


def _carried_reads(carried):
    rkeys = list(dict.fromkeys(k for cm in carried for k in cm.reads))
    return rkeys, [next(cm.reads[k] for cm in carried if k in cm.reads) for k in rkeys]


def _carried_results(carried, akeys, fkeys, outs, rkeys=(), read_refs=()):
    shared = dict(zip(akeys, outs[:len(akeys)]))
    shared.update(zip(rkeys, read_refs))
    res = [{k: shared[k] for k in list(cm.arrays) + [r for r in cm.reads if r in shared]} for cm in carried]
    for (ci, k), o in zip(fkeys, outs[len(akeys):]):
        res[ci][k] = o
    return res


def _pcall(kern, *, name, grid, in_specs, out_specs, out_shape, args, scratch_shapes=(), semantics=None,
           carried=()):
    tokens = [c.array for c in carried if isinstance(c, _Token)]
    carried = [c for c in carried if not isinstance(c, _Token)]
    single = not isinstance(out_shape, (list, tuple))
    out_specs = [out_specs] if single else list(out_specs)
    out_shape = [out_shape] if single else list(out_shape)
    n_real, n_out, n_scr = len(in_specs), len(out_shape), len(scratch_shapes)
    in_specs = list(in_specs) + [pl.BlockSpec(TOKEN_SHAPE, lambda *_: (0, 0))] * len(tokens)
    args = list(args) + tokens
    n_in = len(in_specs)
    if not carried:
        def plain(*refs):
            kern(*refs[:n_real], *refs[n_in:])

        outs = pl.pallas_call(plain if tokens else kern, name=name, grid=grid, in_specs=in_specs,
                              out_specs=out_specs, out_shape=out_shape, scratch_shapes=list(scratch_shapes),
                              compiler_params=_cparams(semantics))(*args)
        return (outs[0] if single else list(outs)), []
    akeys, fkeys, arrays, shapes, sems = _carried_layout(carried)
    rkeys, reads = _carried_reads(carried)
    n_a, n_f, n_r = len(akeys), len(fkeys), len(rkeys)
    total = math.prod(grid)
    mid_step = min(total - 1, (17 * total) // 20)

    def wrapped(*refs):
        ins = refs[:n_real]
        r0 = n_in + n_a
        o0 = r0 + n_r
        outs = refs[o0:o0 + n_out]
        a0 = o0 + n_out
        s0 = a0 + n_a + n_f
        per = _carried_results(carried, akeys, fkeys, refs[a0:s0], rkeys, refs[r0:o0])
        scratch = refs[s0:s0 + n_scr]
        sem = refs[s0 + n_scr:]
        step = 0
        for d, g in enumerate(grid):
            step = step * g + pl.program_id(d)

        @pl.when(step == 0)
        def _():
            for ci, cm in enumerate(carried):
                cm.start(per[ci], sem[2 * ci], sem[2 * ci + 1])

        kern(*ins, *outs, *scratch)

        @pl.when(step == mid_step)
        def _():
            for ci, cm in enumerate(carried):
                if cm.mid is not None:
                    cm.mid(per[ci], sem[2 * ci], sem[2 * ci + 1])

        @pl.when(step == total - 1)
        def _():
            for ci, cm in enumerate(carried):
                cm.finish(per[ci], sem[2 * ci], sem[2 * ci + 1])

    outs = pl.pallas_call(
        wrapped, name=name, grid=grid,
        in_specs=list(in_specs) + [ANY] * (n_a + n_r), out_specs=out_specs + [ANY] * (n_a + n_f),
        out_shape=out_shape + shapes,
        input_output_aliases={n_in + i: n_out + i for i in range(n_a)},
        scratch_shapes=list(scratch_shapes) + sems,
        compiler_params=_cparams(("arbitrary",) * len(grid)),
    )(*args, *arrays, *reads)
    res = _carried_results(carried, akeys, fkeys, outs[n_out:])
    return (outs[0] if single else list(outs[:n_out])), res


def _run_comm(carried, name):
    carried = list(carried)
    akeys, fkeys, arrays, shapes, sems = _carried_layout(carried)
    rkeys, reads = _carried_reads(carried)
    n_a, n_f, n_r = len(akeys), len(fkeys), len(rkeys)

    def body(*refs):
        o0 = n_a + n_r
        per = _carried_results(carried, akeys, fkeys, refs[o0:o0 + n_a + n_f], rkeys, refs[n_a:o0])
        sem = refs[o0 + n_a + n_f:]
        for hook in ("start", "mid", "finish"):
            for ci, cm in enumerate(carried):
                fn = getattr(cm, hook)
                if fn is not None:
                    fn(per[ci], sem[2 * ci], sem[2 * ci + 1])

    outs = pl.pallas_call(
        body, name=name, in_specs=[ANY] * (n_a + n_r), out_specs=[ANY] * (n_a + n_f), out_shape=shapes,
        input_output_aliases={i: i for i in range(n_a)}, scratch_shapes=sems,
    )(*arrays, *reads)
    return _carried_results(carried, akeys, fkeys, outs)


HBM_SPEC = pl.BlockSpec(memory_space=pltpu.HBM)
SEM_SPEC = pl.BlockSpec(memory_space=pltpu.SEMAPHORE)
SPLIT_EFFECT = pltpu.SideEffectType.DATAFLOW_SIDE_EFFECTING


def _in_hbm(a):
    return pltpu.with_memory_space_constraint(a, pltpu.HBM)


def _split_start(items, name, after=None):
    items = list(items)
    akeys, fkeys, arrays, shapes, sems = _carried_layout(items)
    assert not fkeys
    n_a, n_s = len(akeys), len(sems)
    n_in = n_a + (after is not None)

    def body(*refs):
        per = _carried_results(items, akeys, [], refs[n_in:n_in + n_a])
        sem = refs[n_in + n_a:n_in + n_a + n_s]
        for ci, cm in enumerate(items):
            cm.start(per[ci], sem[2 * ci], sem[2 * ci + 1])
        token = refs[n_in + n_a + n_s]
        token[...] = jnp.zeros_like(token)

    outs = pl.pallas_call(
        body, name=name, in_specs=[HBM_SPEC] * n_a + [ANY] * (after is not None),
        out_specs=[HBM_SPEC] * n_a + [SEM_SPEC] * n_s + [pl.BlockSpec(memory_space=pltpu.VMEM)],
        out_shape=[pltpu.HBM(s.shape, s.dtype) for s in shapes] + sems + [jax.ShapeDtypeStruct(TOKEN_SHAPE, F32)],
        input_output_aliases={i: i for i in range(n_a)},
        compiler_params=pltpu.CompilerParams(has_side_effects=SPLIT_EFFECT),
    )(*[_in_hbm(a) for a in arrays], *([after] if after is not None else []))
    res = _carried_results(items, akeys, [], outs[:n_a])
    sem_out = outs[n_a:n_a + n_s]
    return res, [(sem_out[2 * ci], sem_out[2 * ci + 1]) for ci in range(len(items))], outs[-1]


def _split_wait(items, sems, after, name):
    items = list(items)
    after = list(after) if isinstance(after, (list, tuple)) else [after]
    akeys, fkeys, arrays, shapes, _ = _carried_layout(items)
    n_a, n_s = len(akeys), 2 * len(items)

    def body(*refs):
        per = _carried_results(items, akeys, [], refs[n_a + n_s + len(after):])
        sem = refs[n_a:n_a + n_s]
        for ci, cm in enumerate(items):
            cm.finish(per[ci], sem[2 * ci], sem[2 * ci + 1])

    outs = pl.pallas_call(
        body, name=name, in_specs=[HBM_SPEC] * n_a + [SEM_SPEC] * n_s + [ANY] * len(after),
        out_specs=[HBM_SPEC] * n_a, out_shape=[pltpu.HBM(s.shape, s.dtype) for s in shapes],
        input_output_aliases={i: i for i in range(n_a)},
        compiler_params=pltpu.CompilerParams(has_side_effects=SPLIT_EFFECT),
    )(*arrays, *[s for pair in sems for s in pair], *after)
    return _carried_results(items, akeys, [], outs)


def _mm(a, b, *, mode, out_dtype, name, res=None, carried=()):
    if mode == "nn":
        (m, k), (k2, n) = a.shape, b.shape
    elif mode == "nt":
        (m, k), (n, k2) = a.shape, b.shape
    else:
        (k, m), (k2, n) = a.shape, b.shape
    assert k == k2, (name, a.shape, b.shape)
    tm = _div(m, MM_TILE_CAP, LANE)
    tn = _div(n, MM_TILE_CAP, MXU_WIDTH) if n % MXU_WIDTH == 0 else 0
    if tn < 1024:
        tn = _div(n, MM_TILE_CAP, LANE)
    out_shape = jax.ShapeDtypeStruct((m, n), out_dtype)

    if mode == "tn":
        assert res is None

        def kern_tn(a_ref, b_ref, o_ref, at_ref):
            @pl.when(pl.program_id(1) == 0)
            def _():
                step = min(TRANSPOSE_CHUNK, k)
                for c0 in range(0, k, step):
                    at_ref[:, c0:c0 + step] = a_ref[c0:c0 + step, :].astype(F32).T.astype(BF16)

            o_ref[...] = jnp.dot(at_ref[...], b_ref[...].astype(BF16),
                                 preferred_element_type=F32).astype(o_ref.dtype)

        return _pcall(
            kern_tn, name=name, grid=(m // tm, n // tn),
            in_specs=[pl.BlockSpec((k, tm), lambda i, j: (0, i)),
                      pl.BlockSpec((k, tn), lambda i, j: (0, j))],
            out_specs=pl.BlockSpec((tm, tn), lambda i, j: (i, j)),
            out_shape=out_shape, args=(a, b),
            scratch_shapes=[pltpu.VMEM((tm, k), BF16)],
            semantics=("parallel", "arbitrary"), carried=carried)

    tk = k if k <= 2048 else _div(k, 3072, LANE)
    nk = k // tk
    a_spec = pl.BlockSpec((tm, tk), lambda i, j, kk: (i, kk))
    if mode == "nn":
        b_spec = pl.BlockSpec((tk, tn), lambda i, j, kk: (kk, j))
        dot = _dot
    else:
        b_spec = pl.BlockSpec((tn, tk), lambda i, j, kk: (j, kk))
        dot = _dot_nt
    o_spec = pl.BlockSpec((tm, tn), lambda i, j, kk: (i, j))
    in_specs = [a_spec, b_spec]
    args = [a, b]
    if res is not None:
        in_specs.append(o_spec)
        args.append(res)
    has_res = res is not None

    def kern(*refs):
        a_ref, b_ref = refs[0], refs[1]
        r_ref = refs[2] if has_res else None
        o_ref = refs[3] if has_res else refs[2]
        part = dot(a_ref[...], b_ref[...])
        if nk == 1:
            if has_res:
                part = part + r_ref[...]
            o_ref[...] = part.astype(o_ref.dtype)
            return
        acc_ref = refs[-1]
        kk = pl.program_id(2)

        @pl.when(kk == 0)
        def _():
            acc_ref[...] = part

        @pl.when(kk > 0)
        def _():
            acc_ref[...] += part

        @pl.when(kk == nk - 1)
        def _():
            tot = acc_ref[...]
            if has_res:
                tot = tot + r_ref[...]
            o_ref[...] = tot.astype(o_ref.dtype)

    return _pcall(
        kern, name=name, grid=(m // tm, n // tn, nk),
        in_specs=in_specs, out_specs=o_spec, out_shape=out_shape, args=args,
        scratch_shapes=[pltpu.VMEM((tm, tn), F32)] if nk > 1 else [],
        semantics=("parallel", "parallel", "arbitrary"), carried=carried)


ROW_BLOCK = 256


def _rms_fwd(x, g, name):
    t, d = x.shape
    tr = min(ROW_BLOCK, t)

    def kern(x_ref, g_ref, o_ref):
        xf = x_ref[...]
        r = lax.rsqrt(jnp.mean(xf * xf, axis=-1, keepdims=True) + RMS_EPS)
        o_ref[...] = (xf * r * g_ref[...]).astype(o_ref.dtype)

    return pl.pallas_call(
        kern, name=name, grid=(t // tr,),
        in_specs=[pl.BlockSpec((tr, d), lambda i: (i, 0)), pl.BlockSpec((1, d), lambda i: (0, 0))],
        out_specs=pl.BlockSpec((tr, d), lambda i: (i, 0)),
        out_shape=jax.ShapeDtypeStruct((t, d), BF16),
        compiler_params=_cparams(("parallel",)),
    )(x, g)


def _rms_bwd(x, g, dh, res, name):
    t, d = x.shape
    tr = min(ROW_BLOCK, t)
    has_res = res is not None

    def kern(*refs):
        x_ref, g_ref, dh_ref = refs[:3]
        r_ref = refs[3] if has_res else None
        dx_ref, dg_ref = refs[-2], refs[-1]
        xf = x_ref[...]
        r = lax.rsqrt(jnp.mean(xf * xf, axis=-1, keepdims=True) + RMS_EPS)
        xn = xf * r
        dh_ = dh_ref[...]
        dhg = dh_ * g_ref[...]
        dx = r * (dhg - xn * jnp.mean(dhg * xn, axis=-1, keepdims=True))
        if has_res:
            dx = dx + r_ref[...]
        dx_ref[...] = dx
        part = jnp.sum(dh_ * xn, axis=0, keepdims=True)

        @pl.when(pl.program_id(0) == 0)
        def _():
            dg_ref[...] = part

        @pl.when(pl.program_id(0) > 0)
        def _():
            dg_ref[...] += part

    row = pl.BlockSpec((tr, d), lambda i: (i, 0))
    vec = pl.BlockSpec((1, d), lambda i: (0, 0))
    in_specs = [row, vec, row] + ([row] if has_res else [])
    args = [x, g, dh] + ([res] if has_res else [])
    return pl.pallas_call(
        kern, name=name, grid=(t // tr,), in_specs=in_specs, out_specs=[row, vec],
        out_shape=[jax.ShapeDtypeStruct((t, d), F32), jax.ShapeDtypeStruct((1, d), F32)],
        compiler_params=_cparams(("arbitrary",)),
    )(*args)


def _loss_head(x3, g, target, name):
    t, d = x3.shape
    tr = ROW_BLOCK

    def kern(x_ref, g_ref, t_ref, dx_ref, dg_ref, loss_ref):
        xf = x_ref[...]
        r = lax.rsqrt(jnp.mean(xf * xf, axis=-1, keepdims=True) + RMS_EPS)
        xn = xf * r
        gg = g_ref[...]
        err = xn * gg - t_ref[...]
        lpart = 0.5 * jnp.sum(jnp.mean(err * err, axis=-1, keepdims=True), axis=0, keepdims=True)
        dy = err * (1.0 / d)
        dyg = dy * gg
        dx_ref[...] = r * (dyg - xn * jnp.mean(dyg * xn, axis=-1, keepdims=True))
        gpart = jnp.sum(dy * xn, axis=0, keepdims=True)
        lrow = jnp.broadcast_to(lpart, (1, LANE))

        @pl.when(pl.program_id(0) == 0)
        def _():
            dg_ref[...] = gpart
            loss_ref[...] = lrow

        @pl.when(pl.program_id(0) > 0)
        def _():
            dg_ref[...] += gpart
            loss_ref[...] += lrow

    row = pl.BlockSpec((tr, d), lambda i: (i, 0))
    vec = pl.BlockSpec((1, d), lambda i: (0, 0))
    return pl.pallas_call(
        kern, name=name, grid=(t // tr,), in_specs=[row, vec, row],
        out_specs=[row, vec, pl.BlockSpec((1, LANE), lambda i: (0, 0))],
        out_shape=[jax.ShapeDtypeStruct((t, d), F32), jax.ShapeDtypeStruct((1, d), F32),
                   jax.ShapeDtypeStruct((1, LANE), F32)],
        compiler_params=_cparams(("arbitrary",)),
    )(x3, g, target)


ATT_SCALE = HEAD_DIM ** -0.5
Q_BLOCK0, K_BLOCK0, V_BLOCK0 = 0, ATT_WIDTH // HEAD_DIM, 2 * ATT_WIDTH // HEAD_DIM


def _residue_rows(dil, r, n):
    if dil == 1:
        return pl.ds(n * ATT_BLOCK, ATT_BLOCK)
    return pl.ds(n * ATT_BLOCK * dil + r, ATT_BLOCK, stride=dil)


def _band_mask(with_prev):
    width = 2 * ATT_BLOCK if with_prev else ATT_BLOCK
    iq = lax.broadcasted_iota(jnp.int32, (ATT_BLOCK, width), 0)
    ik = lax.broadcasted_iota(jnp.int32, (ATT_BLOCK, width), 1)
    if not with_prev:
        return ik <= iq
    return ((ik < ATT_BLOCK) & (iq <= ik)) | ((ik >= ATT_BLOCK) & ((ik - ATT_BLOCK) <= iq))


def _band_keys(ref, dil, r, n):
    own = ref[_residue_rows(dil, r, n), :]
    if n == 0:
        return own
    return jnp.concatenate([ref[_residue_rows(dil, r, n - 1), :], own], axis=0)


def _attn_col_spec(base, grp):
    return pl.BlockSpec((SEQ, HEAD_DIM), lambda h: (0, base + grp * ATT_HEADS + h))


def _attn_fwd(proj, grp, name, carried=()):
    _, dil = ATT_GROUPS[grp]
    nb = SEQ // dil // ATT_BLOCK

    def kern(q_ref, k_ref, v_ref, o_ref, lse_ref):
        for r in range(dil):
            for n in range(nb):
                rows = _residue_rows(dil, r, n)
                s = _dot_nt(q_ref[rows, :], _band_keys(k_ref, dil, r, n)) * ATT_SCALE
                s = jnp.where(_band_mask(n > 0), s, -jnp.inf)
                m = jnp.max(s, axis=-1, keepdims=True)
                p = jnp.exp(s - m)
                l = jnp.sum(p, axis=-1, keepdims=True)
                o_ref[rows, :] = _dot(p / l, _band_keys(v_ref, dil, r, n))
                lse_ref[rows, :] = jnp.broadcast_to(m + jnp.log(l), (ATT_BLOCK, HEAD_DIM))

    out_spec = pl.BlockSpec((SEQ, HEAD_DIM), lambda h: (0, h))
    return _pcall(
        kern, name=name, grid=(ATT_HEADS,),
        in_specs=[_attn_col_spec(Q_BLOCK0, grp), _attn_col_spec(K_BLOCK0, grp), _attn_col_spec(V_BLOCK0, grp)],
        out_specs=[out_spec, out_spec],
        out_shape=[jax.ShapeDtypeStruct((SEQ, ATT_OUT), F32)] * 2, args=(proj, proj, proj),
        semantics=("parallel",), carried=carried)


def _attn_weights(l0, l1, l2):
    mx = jnp.maximum(jnp.maximum(l0, l1), l2)
    e0, e1, e2 = jnp.exp(l0 - mx), jnp.exp(l1 - mx), jnp.exp(l2 - mx)
    den = e0 + e1 + e2
    return e0 / den, e1 / den, e2 / den


def _attn_merge_fwd(outs, lses, name):
    tr = ROW_BLOCK

    def kern(o0, o1, o2, l0, l1, l2, out_ref):
        a0, a1, a2 = _attn_weights(l0[...], l1[...], l2[...])
        out_ref[...] = (a0 * o0[...] + a1 * o1[...] + a2 * o2[...]).astype(out_ref.dtype)

    spec = pl.BlockSpec((tr, ATT_OUT), lambda i: (i, 0))
    return pl.pallas_call(
        kern, name=name, grid=(SEQ // tr,), in_specs=[spec] * 6, out_specs=spec,
        out_shape=jax.ShapeDtypeStruct((SEQ, ATT_OUT), BF16),
        compiler_params=_cparams(("parallel",)),
    )(*outs, *lses)


def _attn_merge_bwd(outs, lses, do_att, name, carried=()):
    tr = ROW_BLOCK

    def kern(o0, o1, o2, l0, l1, l2, do_ref, d0, d1, d2, t0, t1, t2):
        alphas = _attn_weights(l0[...], l1[...], l2[...])
        do = do_ref[...]
        o_att = alphas[0] * o0[...] + alphas[1] * o1[...] + alphas[2] * o2[...]
        prod = do * o_att
        parts = []
        for h in range(ATT_HEADS):
            sl = slice(h * HEAD_DIM, (h + 1) * HEAD_DIM)
            tot = jnp.sum(prod[:, sl], axis=-1, keepdims=True)
            parts.append(jnp.broadcast_to(tot, (tr, HEAD_DIM)))
        dd = jnp.concatenate(parts, axis=1)
        for a, d_ref, t_ref in zip(alphas, (d0, d1, d2), (t0, t1, t2)):
            d_ref[...] = a * do
            t_ref[...] = -a * dd

    spec = pl.BlockSpec((tr, ATT_OUT), lambda i: (i, 0))
    res, cres = _pcall(
        kern, name=name, grid=(SEQ // tr,), in_specs=[spec] * 7, out_specs=[spec] * 6,
        out_shape=[jax.ShapeDtypeStruct((SEQ, ATT_OUT), F32)] * 6, args=(*outs, *lses, do_att),
        semantics=("parallel",), carried=carried)
    return (res[:3], res[3:]), cres


def _attn_bwd(proj, grp, lse, do_g, dl_g, name, carried=()):
    _, dil = ATT_GROUPS[grp]
    nb = SEQ // dil // ATT_BLOCK

    def kern(q_ref, k_ref, v_ref, do_ref, lse_ref, dl_ref, dq_ref, dk_ref, dv_ref, dq_acc, dk_acc, dv_acc):
        dk_acc[...] = jnp.zeros_like(dk_acc)
        dv_acc[...] = jnp.zeros_like(dv_acc)
        for r in range(dil):
            for n in range(nb):
                rows = _residue_rows(dil, r, n)
                q, do = q_ref[rows, :], do_ref[rows, :]
                kk, vv = _band_keys(k_ref, dil, r, n), _band_keys(v_ref, dil, r, n)
                s = _dot_nt(q, kk) * ATT_SCALE
                p = jnp.where(_band_mask(n > 0), jnp.exp(s - lse_ref[rows, :][:, :1]), 0.0)
                ds = p * (_dot_nt(do, vv) + dl_ref[rows, :][:, :1])
                dq_acc[rows, :] = _dot(ds, kk) * ATT_SCALE
                dk = _dot_tn(ds, q) * ATT_SCALE
                dv = _dot_tn(p, do)
                if n > 0:
                    prev = _residue_rows(dil, r, n - 1)
                    dk_acc[prev, :] += dk[:ATT_BLOCK]
                    dv_acc[prev, :] += dv[:ATT_BLOCK]
                    dk, dv = dk[ATT_BLOCK:], dv[ATT_BLOCK:]
                dk_acc[rows, :] += dk
                dv_acc[rows, :] += dv
        dq_ref[...] = dq_acc[...].astype(dq_ref.dtype)
        dk_ref[...] = dk_acc[...].astype(dk_ref.dtype)
        dv_ref[...] = dv_acc[...].astype(dv_ref.dtype)

    spec = pl.BlockSpec((SEQ, HEAD_DIM), lambda h: (0, h))
    return _pcall(
        kern, name=name, grid=(ATT_HEADS,),
        in_specs=[_attn_col_spec(Q_BLOCK0, grp), _attn_col_spec(K_BLOCK0, grp), _attn_col_spec(V_BLOCK0, grp),
                  spec, spec, spec],
        out_specs=[spec] * 3,
        out_shape=[jax.ShapeDtypeStruct((SEQ, ATT_OUT), BF16)] * 3, args=(proj, proj, proj, do_g, lse, dl_g),
        scratch_shapes=[pltpu.VMEM((SEQ, HEAD_DIM), F32)] * 3,
        semantics=("parallel",), carried=carried)


HG_HEADS_PER_STEP = 8
HG_BLOCK_W = 4 * HEAD_DIM
HG_BLOCKS = HG_HEADS_PER_STEP * HEAD_DIM // HG_BLOCK_W
HG_STEP_W = HG_HEADS_PER_STEP * HEAD_DIM
HG_Q_BLK = (3 * ATT_WIDTH) // HG_BLOCK_W
HG_N_CHUNKS = SEQ // HG_CHUNK
HG_MID = HG_CHUNK // 2


def _lower_bound(lb_ref, sl):
    l0, l1 = lb_ref[0:1, sl], lb_ref[1:2, sl]
    mx = jnp.maximum(l0, l1)
    e0, e1 = jnp.exp(l0 - mx), jnp.exp(l1 - mx)
    return e0 / (e0 + e1)


def _tri(lower):
    i = lax.broadcasted_iota(jnp.int32, (HG_CHUNK, HG_CHUNK), 0)
    j = lax.broadcasted_iota(jnp.int32, (HG_CHUNK, HG_CHUNK), 1)
    return (i >= j) if lower else (i <= j)


def _head_mean(x):
    parts = []
    for hd in range(x.shape[1] // HEAD_DIM):
        m = jnp.mean(x[:, hd * HEAD_DIM:(hd + 1) * HEAD_DIM], axis=-1, keepdims=True)
        parts.append(jnp.broadcast_to(m, (x.shape[0], HEAD_DIM)))
    return jnp.concatenate(parts, axis=1)


def _hg_chunk_terms(qh, fh, lb):
    sig = _sigmoid(fh)
    f = lb + (1.0 - lb) * sig
    k = 1.0 - f
    b = _dot_exact(_tri(True).astype(F32), jnp.log(f))
    bl = b[HG_CHUNK - 1:HG_CHUNK, :]
    br = b[HG_MID:HG_MID + 1, :]
    sq = _sigmoid(qh)
    q = qh * sq
    return dict(sig=sig, f=f, k=k, b=b, bl=bl, br=br, sq=sq, q=q,
                e1=jnp.exp(bl - b), e2=jnp.exp(b), e3=jnp.exp(b - br), e4=jnp.exp(br - b))


def _hg_fwd(proj, lbw, normw, name, carried=()):
    def in_blks(off):
        return [pl.BlockSpec((HG_CHUNK, HG_BLOCK_W), lambda hp, n, b=b: (n, HG_Q_BLK + off + hp * HG_BLOCKS + b))
                for b in range(HG_BLOCKS)]

    def kern(*refs):
        q_refs, f_refs, i_refs, g_refs = (refs[k * HG_BLOCKS:(k + 1) * HG_BLOCKS] for k in range(4))
        lb_ref, nw_ref, oraw_ref, ohg_ref, st_ref, state = refs[4 * HG_BLOCKS:]

        @pl.when(pl.program_id(1) == 0)
        def _():
            state[...] = jnp.zeros_like(state)

        causal = _tri(True)
        wide = lambda rs: jnp.concatenate([r[...] for r in rs], axis=1)
        t = _hg_chunk_terms(wide(q_refs), wide(f_refs), _lower_bound(lb_ref, slice(None)))
        v, gh = wide(i_refs), wide(g_refs)
        kd, qb, qr, kr = t["k"] * t["e1"], t["q"] * t["e2"], t["q"] * t["e3"], t["k"] * t["e4"]
        decay = jnp.exp(t["bl"])
        outs = []
        for hd in range(HG_HEADS_PER_STEP):
            sl = slice(hd * HEAD_DIM, (hd + 1) * HEAD_DIM)
            st = state[hd]
            st_ref[0, hd] = st
            a = jnp.where(causal, _dot_nt(qr[:, sl], kr[:, sl]), 0.0)
            outs.append(_dot_nt(qb[:, sl], st) + _dot(a, v[:, sl]))
            state[hd] = st * decay[:, sl] + _dot_tn(v[:, sl], kd[:, sl])
        o = jnp.concatenate(outs, axis=1)
        oraw_ref[...] = o
        r = lax.rsqrt(_head_mean(o * o) + RMS_EPS)
        nw = jnp.tile(nw_ref[...], (1, HG_HEADS_PER_STEP))
        ohg_ref[...] = (o * r * nw * (gh * _sigmoid(gh))).astype(ohg_ref.dtype)

    out_blk = pl.BlockSpec((HG_CHUNK, HG_STEP_W), lambda hp, n: (n, hp))
    return _pcall(
        kern, name=name, grid=(HG_HEADS // HG_HEADS_PER_STEP, HG_N_CHUNKS),
        in_specs=[*in_blks(0), *in_blks(2), *in_blks(4), *in_blks(6),
                  pl.BlockSpec((2, HG_STEP_W), lambda hp, n: (0, hp)),
                  pl.BlockSpec((1, HEAD_DIM), lambda hp, n: (0, 0))],
        out_specs=[out_blk, out_blk,
                   pl.BlockSpec((1, HG_HEADS_PER_STEP, HEAD_DIM, HEAD_DIM), lambda hp, n: (n, hp, 0, 0))],
        out_shape=[jax.ShapeDtypeStruct((SEQ, HG_WIDTH), F32), jax.ShapeDtypeStruct((SEQ, HG_WIDTH), BF16),
                   jax.ShapeDtypeStruct((HG_N_CHUNKS, HG_HEADS, HEAD_DIM, HEAD_DIM), F32)],
        args=(*[proj] * (4 * HG_BLOCKS), lbw, normw),
        scratch_shapes=[pltpu.VMEM((HG_HEADS_PER_STEP, HEAD_DIM, HEAD_DIM), F32)],
        semantics=("parallel", "arbitrary"), carried=carried)


def _hg_bwd(proj, lbw, normw, oraw, states, do_hg, name, carried=()):
    last = HG_N_CHUNKS - 1

    def in_blks(off):
        return [pl.BlockSpec((HG_CHUNK, HG_BLOCK_W),
                             lambda hp, n, b=b: (last - n, HG_Q_BLK + off + hp * HG_BLOCKS + b))
                for b in range(HG_BLOCKS)]

    blk = pl.BlockSpec((HG_CHUNK, HG_STEP_W), lambda hp, n: (last - n, hp))

    def kern(*refs):
        q_refs, f_refs, i_refs, g_refs = (refs[k * HG_BLOCKS:(k + 1) * HG_BLOCKS] for k in range(4))
        (lb_ref, nw_ref, oraw_ref, st_ref, do_ref, dq_ref, df_ref, di_ref, dg_ref, dlb_ref, dnw_ref,
         dstate) = refs[4 * HG_BLOCKS:]
        first = pl.program_id(1) == 0

        @pl.when(first)
        def _():
            dstate[...] = jnp.zeros_like(dstate)

        causal = _tri(True)
        wide = lambda rs: jnp.concatenate([r[...] for r in rs], axis=1)
        cat = lambda parts: jnp.concatenate(parts, axis=1)
        qh, fh, v, gh = wide(q_refs), wide(f_refs), wide(i_refs), wide(g_refs)
        o, dout = oraw_ref[...], do_ref[...]
        nw = jnp.tile(nw_ref[...], (1, HG_HEADS_PER_STEP))
        sgg = _sigmoid(gh)
        r = lax.rsqrt(_head_mean(o * o) + RMS_EPS)
        xn = o * r
        dg_ref[...] = (dout * xn * nw * (sgg * (1.0 + gh * (1.0 - sgg)))).astype(dg_ref.dtype)
        don = dout * (gh * sgg)
        dnw_wide = jnp.sum(don * xn, axis=0, keepdims=True)
        dnw_tot = dnw_wide[:, :HEAD_DIM]
        for hd in range(1, HG_HEADS_PER_STEP):
            dnw_tot = dnw_tot + dnw_wide[:, hd * HEAD_DIM:(hd + 1) * HEAD_DIM]
        tt = don * nw
        do = r * (tt - xn * _head_mean(tt * xn))
        lb = _lower_bound(lb_ref, slice(None))
        t = _hg_chunk_terms(qh, fh, lb)
        k, q = t["k"], t["q"]
        kd, qb, qr, kr = k * t["e1"], q * t["e2"], q * t["e3"], k * t["e4"]
        decay = jnp.exp(t["bl"])
        dqb, dqr, dkr, dkd, dv, ddecay = [], [], [], [], [], []
        for hd in range(HG_HEADS_PER_STEP):
            sl = slice(hd * HEAD_DIM, (hd + 1) * HEAD_DIM)
            st = st_ref[0, hd]
            dstn = dstate[hd]
            a = jnp.where(causal, _dot_nt(qr[:, sl], kr[:, sl]), 0.0)
            da = jnp.where(causal, _dot_nt(do[:, sl], v[:, sl]), 0.0)
            dqb.append(_dot(do[:, sl], st))
            dv.append(_dot_tn(a, do[:, sl]) + _dot_nt(kd[:, sl], dstn))
            dqr.append(_dot(da, kr[:, sl]))
            dkr.append(_dot_tn(da, qr[:, sl]))
            dkd.append(_dot(v[:, sl], dstn))
            ddecay.append(jnp.sum(dstn * st, axis=0, keepdims=True))
            dstate[hd] = dstn * decay[:, sl] + _dot_tn(do[:, sl], qb[:, sl])
        dqb, dqr, dkr, dkd, dv, ddecay = cat(dqb), cat(dqr), cat(dkr), cat(dkd), cat(dv), cat(ddecay)
        dq = dqb * t["e2"] + dqr * t["e3"]
        dk = dkd * t["e1"] + dkr * t["e4"]
        db = dqb * qb + dqr * qr - dkr * kr - dkd * kd
        dbl = jnp.sum(dkd * kd, axis=0, keepdims=True) + ddecay * decay
        dbr = jnp.sum(dkr * kr - dqr * qr, axis=0, keepdims=True)
        rows = lax.broadcasted_iota(jnp.int32, db.shape, 0)
        dlf = _dot_exact(_tri(False).astype(F32), db) + dbl + jnp.where(rows <= HG_MID, dbr, 0.0)
        df = dlf / t["f"] - dk
        sig, sq = t["sig"], t["sq"]
        df_ref[...] = (df * (1.0 - lb) * sig * (1.0 - sig)).astype(df_ref.dtype)
        dlb_row = jnp.sum(df * (1.0 - sig), axis=0, keepdims=True)
        dq_ref[...] = (dq * (sq * (1.0 + qh * (1.0 - sq)))).astype(dq_ref.dtype)
        di_ref[...] = dv.astype(di_ref.dtype)
        dnw_blk = jnp.broadcast_to(dnw_tot, (8, HEAD_DIM))

        @pl.when(first)
        def _():
            dlb_ref[...] = dlb_row
            dnw_ref[...] = dnw_blk

        @pl.when(jnp.logical_not(first))
        def _():
            dlb_ref[...] += dlb_row
            dnw_ref[...] += dnw_blk

    n_hp = HG_HEADS // HG_HEADS_PER_STEP
    outs, cres = _pcall(
        kern, name=name, grid=(n_hp, HG_N_CHUNKS),
        in_specs=[*in_blks(0), *in_blks(2), *in_blks(4), *in_blks(6),
                  pl.BlockSpec((2, HG_STEP_W), lambda hp, n: (0, hp)),
                  pl.BlockSpec((1, HEAD_DIM), lambda hp, n: (0, 0)),
                  blk,
                  pl.BlockSpec((1, HG_HEADS_PER_STEP, HEAD_DIM, HEAD_DIM), lambda hp, n: (last - n, hp, 0, 0)),
                  blk],
        out_specs=[blk, blk, blk, blk,
                   pl.BlockSpec((1, HG_STEP_W), lambda hp, n: (0, hp)),
                   pl.BlockSpec((8, HEAD_DIM), lambda hp, n: (hp, 0))],
        out_shape=[jax.ShapeDtypeStruct((SEQ, HG_WIDTH), BF16)] * 4
        + [jax.ShapeDtypeStruct((1, HG_WIDTH), F32), jax.ShapeDtypeStruct((8 * n_hp, HEAD_DIM), F32)],
        args=(*[proj] * (4 * HG_BLOCKS), lbw, normw, oraw, states, do_hg),
        scratch_shapes=[pltpu.VMEM((HG_HEADS_PER_STEP, HEAD_DIM, HEAD_DIM), F32)],
        semantics=("parallel", "arbitrary"), carried=carried)
    dqh, dfh, dih, dgh, dlb, dnw = outs
    return (dqh, dfh, dih, dgh, dlb, [dnw[8 * i:8 * i + 1] for i in range(n_hp)]), cres


GATE_BLOCK_W = 512
GATE_A_BLK = (3 * ATT_WIDTH + 4 * HG_WIDTH) // GATE_BLOCK_W
GATE_B_BLK = GATE_A_BLK + D_MODEL // GATE_BLOCK_W


def _gate_specs():
    tr = ROW_BLOCK
    blk = pl.BlockSpec((tr, GATE_BLOCK_W), lambda i, j: (i, j))
    ga = pl.BlockSpec((tr, GATE_BLOCK_W), lambda i, j: (i, GATE_A_BLK + j))
    gb = pl.BlockSpec((tr, GATE_BLOCK_W), lambda i, j: (i, GATE_B_BLK + j))
    return (SEQ // tr, D_MODEL // GATE_BLOCK_W), blk, ga, gb


def _gate_fwd(proj, ya, yb, name, carried=()):
    grid, blk, ga, gb = _gate_specs()

    def kern(ga_ref, gb_ref, ya_ref, yb_ref, o_ref):
        o_ref[...] = (_sigmoid(ga_ref[...]) * ya_ref[...] + _sigmoid(gb_ref[...]) * yb_ref[...]).astype(o_ref.dtype)

    return _pcall(
        kern, name=name, grid=grid, in_specs=[ga, gb, blk, blk], out_specs=blk,
        out_shape=jax.ShapeDtypeStruct((SEQ, D_MODEL), BF16), args=(proj, proj, ya, yb),
        semantics=("parallel", "parallel"), carried=carried)


def _gate_bwd(proj, ya, yb, dmerged, name, carried=()):
    grid, blk, ga, gb = _gate_specs()

    def kern(ga_ref, gb_ref, ya_ref, yb_ref, dm_ref, dya_ref, dyb_ref, dga_ref, dgb_ref):
        dm = dm_ref[...]
        sa, sb = _sigmoid(ga_ref[...]), _sigmoid(gb_ref[...])
        dya_ref[...] = (dm * sa).astype(dya_ref.dtype)
        dyb_ref[...] = (dm * sb).astype(dyb_ref.dtype)
        dga_ref[...] = (dm * ya_ref[...] * sa * (1.0 - sa)).astype(dga_ref.dtype)
        dgb_ref[...] = (dm * yb_ref[...] * sb * (1.0 - sb)).astype(dgb_ref.dtype)

    return _pcall(
        kern, name=name, grid=grid, in_specs=[ga, gb, blk, blk, blk], out_specs=[blk] * 4,
        out_shape=[jax.ShapeDtypeStruct((SEQ, D_MODEL), BF16)] * 4, args=(proj, proj, ya, yb, dmerged),
        semantics=("parallel", "parallel"), carried=carried)


FF_SHARD = D_FF // N_CHIPS


def _swiglu_fwd(ab, name, carried=()):
    tr = ROW_BLOCK

    def kern(ab_ref, u_ref):
        a, b = ab_ref[:, :FF_SHARD], ab_ref[:, FF_SHARD:]
        u_ref[...] = (a * _sigmoid(a) * b).astype(u_ref.dtype)

    return _pcall(
        kern, name=name, grid=(SEQ // tr, N_CHIPS),
        in_specs=[pl.BlockSpec((tr, 2 * FF_SHARD), lambda i, j: (i, j))],
        out_specs=pl.BlockSpec((tr, FF_SHARD), lambda i, j: (i, j)),
        out_shape=jax.ShapeDtypeStruct((SEQ, D_FF), BF16), args=(ab,),
        semantics=("parallel", "parallel"), carried=carried)


def _swiglu_bwd(ab, du, name, carried=()):
    tr = ROW_BLOCK

    def kern(ab_ref, du_ref, dab_ref):
        a, b = ab_ref[:, :FF_SHARD], ab_ref[:, FF_SHARD:]
        du_ = du_ref[...]
        sg = _sigmoid(a)
        dab_ref[:, :FF_SHARD] = (du_ * b * (sg * (1.0 + a * (1.0 - sg)))).astype(dab_ref.dtype)
        dab_ref[:, FF_SHARD:] = (du_ * (a * sg)).astype(dab_ref.dtype)

    wide = pl.BlockSpec((tr, 2 * FF_SHARD), lambda i, j: (i, j))
    return _pcall(
        kern, name=name, grid=(SEQ // tr, N_CHIPS),
        in_specs=[wide, pl.BlockSpec((tr, FF_SHARD), lambda i, j: (i, j))],
        out_specs=wide, out_shape=jax.ShapeDtypeStruct((SEQ, 2 * D_FF), BF16), args=(ab, du),
        semantics=("parallel", "parallel"), carried=carried)


CROSS_ROWS = 512


def _cross_fwd(qc, kvc, name, carried=()):
    def kern(q_ref, k_ref, v_ref, o_ref):
        s = _dot_nt(q_ref[...], k_ref[...]) * ATT_SCALE
        m = jnp.max(s, axis=-1, keepdims=True)
        e = jnp.exp(s - m)
        p = e / jnp.sum(e, axis=-1, keepdims=True)
        o_ref[...] = _dot(p, v_ref[...]).astype(o_ref.dtype)

    qblk = pl.BlockSpec((CROSS_ROWS, HEAD_DIM), lambda h, i: (i, h))
    return _pcall(
        kern, name=name, grid=(CROSS_HEADS, SEQ // CROSS_ROWS),
        in_specs=[qblk, pl.BlockSpec((MEM_LEN, HEAD_DIM), lambda h, i: (0, h)),
                  pl.BlockSpec((MEM_LEN, HEAD_DIM), lambda h, i: (0, CROSS_HEADS + h))],
        out_specs=qblk, out_shape=jax.ShapeDtypeStruct((SEQ, CROSS_WIDTH), BF16), args=(qc, kvc, kvc),
        semantics=("parallel", "parallel"), carried=carried)


def _cross_bwd(qc, kvc, doc, name):
    def kern(q_ref, k_ref, v_ref, do_ref, dq_ref, dk_ref, dv_ref):
        q, k, v, do = q_ref[...], k_ref[...], v_ref[...], do_ref[...]
        s = _dot_nt(q, k) * ATT_SCALE
        m = jnp.max(s, axis=-1, keepdims=True)
        e = jnp.exp(s - m)
        p = e / jnp.sum(e, axis=-1, keepdims=True)
        dp = _dot_nt(do, v)
        ds = p * (dp - jnp.sum(dp * p, axis=-1, keepdims=True))
        dq_ref[...] = (_dot(ds, k) * ATT_SCALE).astype(dq_ref.dtype)
        dk = _dot_tn(ds, q) * ATT_SCALE
        dv = _dot_tn(p, do)

        @pl.when(pl.program_id(1) == 0)
        def _():
            dk_ref[...] = dk
            dv_ref[...] = dv

        @pl.when(pl.program_id(1) > 0)
        def _():
            dk_ref[...] += dk
            dv_ref[...] += dv

    qblk = pl.BlockSpec((CROSS_ROWS, HEAD_DIM), lambda h, i: (i, h))
    kblk = pl.BlockSpec((MEM_LEN, HEAD_DIM), lambda h, i: (0, h))
    dq, dk, dv = pl.pallas_call(
        kern, name=name, grid=(CROSS_HEADS, SEQ // CROSS_ROWS),
        in_specs=[qblk, kblk, pl.BlockSpec((MEM_LEN, HEAD_DIM), lambda h, i: (0, CROSS_HEADS + h)), qblk],
        out_specs=[qblk, kblk, kblk],
        out_shape=[jax.ShapeDtypeStruct((SEQ, CROSS_WIDTH), BF16),
                   jax.ShapeDtypeStruct((MEM_LEN, CROSS_WIDTH), F32),
                   jax.ShapeDtypeStruct((MEM_LEN, CROSS_WIDTH), F32)],
        compiler_params=_cparams(("parallel", "arbitrary")),
    )(qc, kvc, kvc, doc)
    return dq, jnp.concatenate([dk, dv], axis=1)


FULL_SPECS = {
    "w_in": ("col", D_MODEL, IN_WIDTH),
    "w_branch_a": ("col", ATT_OUT, D_MODEL),
    "w_branch_b": ("col", HG_WIDTH, D_MODEL),
    "w_out": ("row", D_MODEL, D_MODEL),
    "wq_cross": ("row", D_MODEL, CROSS_WIDTH),
    "wkv_cross": ("row", D_MODEL, 2 * CROSS_WIDTH),
    "wo_cross": ("col", CROSS_WIDTH, D_MODEL),
    "w13": ("col", D_MODEL, 2 * D_FF),
    "w2": ("row", D_FF, D_MODEL),
}
WEIGHT_PLACE = {
    "w_in": ("w_in", 0), "w_branch_a": ("w_branch_a", 0), "w_branch_b": ("w_branch_b", 0),
    "w_out": ("w_out", 0), "wq_cross": ("wq_cross", 0), "wkv_cross": ("wkv_cross", 0),
    "wo_cross": ("wo_cross", 0), "w1": ("w13", 0), "w3": ("w13", FF_SHARD), "w2": ("w2", 0),
}
BIG_WEIGHTS = tuple(WEIGHT_PLACE)
EW_BLOCK_ELEMS = 512 * 1024


def _position():
    return lax.axis_index("x"), lax.axis_index("y"), lax.axis_index("c")


def _other_chips(x, y):
    return [(1 - x, y), (x, 1 - y), (1 - x, 1 - y)]


def _half(ref, kind, h):
    r, c = ref.shape
    if kind == "col":
        return ref.at[pl.ds(h * (r // 2), r // 2), :]
    return ref.at[:, pl.ds(h * (c // 2), c // 2)]


def _shard_of(ref, kind, start, size):
    return ref.at[:, pl.ds(start, size)] if kind == "col" else ref.at[pl.ds(start, size), :]


def _rows_of(ref, r0, nrows):
    return ref if nrows is None else ref.at[pl.ds(r0, nrows), :]


def _half_shape(kind, rows, cols):
    return (rows // 2, cols) if kind == "col" else (rows, cols // 2)


def _slot_shape(spec):
    kind, rows, cols = spec
    hr, hc = _half_shape(kind, rows, cols)
    return (hr, hc // N_CHIPS) if kind == "col" else (hr // N_CHIPS, hc)


def _remote(src, dst, send_sem, recv_sem, device):
    return pltpu.make_async_remote_copy(src_ref=src, dst_ref=dst, send_sem=send_sem, recv_sem=recv_sem,
                                        device_id=device, device_id_type=MESH)


def _gather_ici_comm(fulls, jobs, specs):
    def piece(refs, job, chip, c):
        f, r0, nr = job
        kind, rows, cols = specs[f]
        stride = (cols if kind == "col" else rows) // N_CHIPS
        return _rows_of(_half(_shard_of(refs[f], kind, chip * stride, stride), kind, c), r0, nr)

    def start(refs, ss, rs):
        x, y, c = _position()
        j = 2 * x + y
        for q, job in enumerate(jobs):
            for p, (px, py) in enumerate(_other_chips(x, y)):
                _remote(piece(refs, job, j, c), piece(refs, job, j, c), ss.at[3 * q + p], rs.at[3 * q + p],
                        (px, py, c)).start()

    def finish(refs, ss, rs):
        x, y, c = _position()
        j = 2 * x + y
        for q, job in enumerate(jobs):
            for p, (px, py) in enumerate(_other_chips(x, y)):
                _remote(piece(refs, job, j, c), piece(refs, job, 2 * px + py, c), ss.at[3 * q + p],
                        rs.at[3 * q + p], (px, py, c)).wait_recv()
        for q, job in enumerate(jobs):
            for p, (px, py) in enumerate(_other_chips(x, y)):
                _remote(piece(refs, job, j, c), piece(refs, job, j, c), ss.at[3 * q + p], rs.at[3 * q + p],
                        (px, py, c)).wait_send()

    names = list(dict.fromkeys(job[0] for job in jobs))
    return _Carried({f: fulls[f] for f in names}, {}, 3 * len(jobs), start, finish)


def _gather_ring_comm(fulls, f, r0, nr, phase, specs):
    kind, _, cols = specs[f]
    assert kind == "col" and nr % 32 == 0
    stride = cols // N_CHIPS
    half = nr // 2

    def rows(refs, chip, c, lo, n):
        return _rows_of(_half(_shard_of(refs[f], kind, chip * stride, stride), kind, c), r0 + lo, n)

    def copies(refs, ss, rs):
        x, y, c = _position()
        me, nx, ny, dg = 2 * x + y, 2 * (1 - x) + y, 2 * x + (1 - y), 2 * (1 - x) + (1 - y)
        to_x, to_y = (1 - x, y, c), (x, 1 - y, c)
        if phase == "a":
            mine = rows(refs, me, c, 0, nr)
            return [(_remote(mine, mine, ss.at[0], rs.at[0], to_x), rows(refs, nx, c, 0, nr)),
                    (_remote(mine, mine, ss.at[1], rs.at[1], to_y), rows(refs, ny, c, 0, nr))]
        up, low = rows(refs, ny, c, half, half), rows(refs, nx, c, 0, half)
        return [(_remote(up, up, ss.at[0], rs.at[0], to_x), rows(refs, dg, c, half, half)),
                (_remote(low, low, ss.at[1], rs.at[1], to_y), rows(refs, dg, c, 0, half))]

    def start(refs, ss, rs):
        for cp, _ in copies(refs, ss, rs):
            cp.start()

    def finish(refs, ss, rs):
        x, y, c = _position()
        mine = copies(refs, ss, rs)
        for i, (_, landing) in enumerate(mine):
            _remote(landing, landing, ss.at[i], rs.at[i], (x, y, c)).wait_recv()
        for cp, _ in mine:
            cp.wait_send()

    return _Carried({f: fulls[f]}, {}, 2, start, finish)


def _gather_d2d_comm(fulls, jobs, specs):
    def rect(refs, job, h):
        f, r0, nr = job
        assert nr is None or specs[f][0] == "col"
        return _rows_of(_half(refs[f], specs[f][0], h), r0, nr)

    def start(refs, ss, rs):
        x, y, c = _position()
        for q, job in enumerate(jobs):
            _remote(rect(refs, job, c), rect(refs, job, c), ss.at[q], rs.at[q], (x, y, 1 - c)).start()

    def finish(refs, ss, rs):
        x, y, c = _position()
        for q, job in enumerate(jobs):
            _remote(rect(refs, job, 1 - c), rect(refs, job, 1 - c), ss.at[q], rs.at[q], (x, y, 1 - c)).wait_recv()
        for q, job in enumerate(jobs):
            _remote(rect(refs, job, c), rect(refs, job, c), ss.at[q], rs.at[q], (x, y, 1 - c)).wait_send()

    names = list(dict.fromkeys(job[0] for job in jobs))
    return _Carried({f: fulls[f] for f in names}, {}, len(jobs), start, finish)


def _pairx_comm(grads, names, specs, whole=False):
    def copies(refs, ss, rs):
        x, y, c = _position()
        src = (lambda f: refs[("g", f)]) if whole else (lambda f: _half(refs[("g", f)], specs[f][0], 1 - c))
        return [_remote(src(f), refs[("r", f)], ss.at[i], rs.at[i], (x, y, 1 - c)) for i, f in enumerate(names)]

    def start(refs, ss, rs):
        for cp in copies(refs, ss, rs):
            cp.start()

    def finish(refs, ss, rs):
        for cp in copies(refs, ss, rs):
            cp.wait_recv()
        for cp in copies(refs, ss, rs):
            cp.wait_send()

    fresh = {("r", f): jax.ShapeDtypeStruct(_half_shape(*specs[f]), BF16) for f in names}
    return _Carried({}, fresh, len(names), start, finish, reads={("g", f): grads[f] for f in names})


def _chipx_comm(pair_sums, slots, jobs, specs):
    def copies(refs, ss, rs):
        x, y, c = _position()
        out = []
        for q, (f, r0, nr) in enumerate(jobs):
            kind = specs[f][0]
            width = _slot_shape(specs[f])[1 if kind == "col" else 0]
            for p, (px, py) in enumerate(_other_chips(x, y)):
                src = _rows_of(_shard_of(refs[("p", f)], kind, (2 * px + py) * width, width), r0, nr)
                dst = _rows_of(refs[("s", f)].at[p], r0, nr)
                out.append(_remote(src, dst, ss.at[3 * q + p], rs.at[3 * q + p], (px, py, c)))
        return out

    def start(refs, ss, rs):
        for cp in copies(refs, ss, rs):
            cp.start()

    def finish(refs, ss, rs):
        for cp in copies(refs, ss, rs):
            cp.wait_recv()
        for cp in copies(refs, ss, rs):
            cp.wait_send()

    names = list(dict.fromkeys(job[0] for job in jobs))
    arrays = {("p", f): pair_sums[f] for f in names}
    arrays.update({("s", f): slots[f] for f in names})
    return _Carried(arrays, {}, 3 * len(jobs), start, finish)


def _share_comm(grads, wnames, specs, place):
    def start(refs, ss, rs):
        x, y, c = _position()
        for i, w in enumerate(wnames):
            kind = specs[place[w][0]][0]
            _remote(_half(refs[w], kind, c), _half(refs[w], kind, c), ss.at[i], rs.at[i], (x, y, 1 - c)).start()

    def finish(refs, ss, rs):
        x, y, c = _position()
        for i, w in enumerate(wnames):
            kind = specs[place[w][0]][0]
            _remote(_half(refs[w], kind, 1 - c), _half(refs[w], kind, 1 - c), ss.at[i], rs.at[i],
                    (x, y, 1 - c)).wait_recv()
        for i, w in enumerate(wnames):
            kind = specs[place[w][0]][0]
            _remote(_half(refs[w], kind, c), _half(refs[w], kind, c), ss.at[i], rs.at[i], (x, y, 1 - c)).wait_send()

    return _Carried({w: grads[w] for w in wnames}, {}, len(wnames), start, finish)


def _gather_rows(v, name="gather_small"):
    shape = v.shape

    def body(v_ref, out_ref, send_sem, recv_sem, loc_sem):
        x, y, c = _position()
        me = 4 * x + 2 * y + c
        flips = [(fx, fy, fc) for fx in (0, 1) for fy in (0, 1) for fc in (0, 1)][1:]

        def peer(fl):
            return tuple(1 - a if f else a for a, f in zip((x, y, c), fl))

        loc = pltpu.make_async_copy(v_ref, out_ref.at[me], loc_sem)
        loc.start()
        sends = []
        for i, fl in enumerate(flips):
            cp = _remote(v_ref, out_ref.at[me], send_sem.at[i], recv_sem.at[i], peer(fl))
            cp.start()
            sends.append(cp)
        for i, fl in enumerate(flips):
            px, py, pc = peer(fl)
            _remote(v_ref, out_ref.at[4 * px + 2 * py + pc], send_sem.at[i], recv_sem.at[i], peer(fl)).wait_recv()
        for cp in sends:
            cp.wait_send()
        loc.wait()

    return pl.pallas_call(
        body, name=name, in_specs=[ANY], out_specs=ANY,
        out_shape=jax.ShapeDtypeStruct((N_DEV,) + shape, F32),
        scratch_shapes=[pltpu.SemaphoreType.DMA((N_DEV - 1,)), pltpu.SemaphoreType.DMA((N_DEV - 1,)),
                        pltpu.SemaphoreType.DMA],
    )(v)


def _ew_block(rows, cols, elems=EW_BLOCK_ELEMS):
    tc = cols if cols <= 4096 else _div(cols, 2048, LANE)
    tr = _div(rows, max(16, elems // tc), 16)
    return tr, tc


def _mesh_scalars():
    x, y, c = _position()
    return jnp.stack([c, 2 * x + y]).astype(jnp.int32)


def _grid_spec(grid, in_specs, out_specs):
    return pltpu.PrefetchScalarGridSpec(num_scalar_prefetch=1, grid=grid, in_specs=in_specs, out_specs=out_specs)


def _cast_into_full(parts, fname, pos, specs, place, name, token=None):
    kind, rows, cols = specs[fname]
    ws = [w for w in place if place[w][0] == fname]
    if kind == "col":
        stride = cols // N_CHIPS
        hr = rows // 2
        tr = _div(hr, max(16, EW_BLOCK_ELEMS // stride), 16)
        nrb = hr // tr
        in_specs = [pl.BlockSpec((tr, parts[w].shape[1]), lambda i, pos_ref: (i + pos_ref[0] * nrb, 0)) for w in ws]
        out_spec = pl.BlockSpec((tr, stride), lambda i, pos_ref: (i + pos_ref[0] * nrb, pos_ref[1]))
    else:
        stride = rows // N_CHIPS
        hc = cols // 2
        tr = _div(stride, max(16, EW_BLOCK_ELEMS // hc), 16)
        nrb = stride // tr
        in_specs = [pl.BlockSpec((tr, hc), lambda i, pos_ref: (i, pos_ref[0])) for w in ws]
        out_spec = pl.BlockSpec((tr, hc), lambda i, pos_ref: (i + pos_ref[1] * nrb, pos_ref[0]))

    def kern(pos_ref, *refs):
        o_ref = refs[-1]
        for w, r in zip(ws, refs[:len(ws)]):
            off = place[w][1] if kind == "col" else 0
            o_ref[:, off:off + r.shape[1]] = r[...].astype(o_ref.dtype)

    tokens = [] if token is None else [token]
    in_specs = in_specs + [pl.BlockSpec(TOKEN_SHAPE, lambda i, pos_ref: (0, 0))] * len(tokens)
    return pl.pallas_call(
        kern, name=name, grid_spec=_grid_spec((nrb,), in_specs, out_spec),
        out_shape=jax.ShapeDtypeStruct((rows, cols), BF16),
        compiler_params=_cparams(("parallel",)),
    )(pos, *[parts[w] for w in ws], *tokens)


def _pair_sum(grad, recv, pos, spec, name, whole=False):
    kind, rows, cols = spec
    hr, hc = _half_shape(kind, rows, cols)
    tr, tc = _ew_block(hr, hc, 2 * EW_BLOCK_ELEMS)
    nrb, ncb = hr // tr, hc // tc
    blk = pl.BlockSpec((tr, tc), lambda i, jj, pos_ref: (i, jj))
    if whole:
        mine = blk
    elif kind == "col":
        mine = pl.BlockSpec((tr, tc), lambda i, jj, pos_ref: (i + pos_ref[0] * nrb, jj))
    else:
        mine = pl.BlockSpec((tr, tc), lambda i, jj, pos_ref: (i, jj + pos_ref[0] * ncb))

    def kern(pos_ref, g_ref, r_ref, o_ref, slots_ref):
        o_ref[...] = (g_ref[...].astype(F32) + r_ref[...].astype(F32)).astype(o_ref.dtype)

    return pl.pallas_call(
        kern, name=name, grid_spec=_grid_spec((nrb, ncb), [mine, blk], [blk, ANY]),
        out_shape=[jax.ShapeDtypeStruct((hr, hc), BF16),
                   jax.ShapeDtypeStruct((N_CHIPS - 1,) + _slot_shape(spec), BF16)],
        compiler_params=_cparams(("parallel", "parallel")),
    )(pos, grad, recv)


def _chip_sum(pair_sum, slots, pos, fname, shard_shapes, specs, place, name):
    kind, rows, cols = specs[fname]
    sr, sc = _slot_shape(specs[fname])
    ws = [w for w in place if place[w][0] == fname]
    n_slots = N_CHIPS - 1
    tr = _div(sr, max(16, EW_BLOCK_ELEMS // sc), 16)
    nrb = sr // tr
    slot = pl.BlockSpec((n_slots, tr, sc), lambda i, pos_ref: (0, i, 0))
    if kind == "col":
        own = pl.BlockSpec((tr, sc), lambda i, pos_ref: (i, pos_ref[1]))
        out_specs = [pl.BlockSpec((tr, shard_shapes[w][1]), lambda i, pos_ref: (i + pos_ref[0] * nrb, 0)) for w in ws]
    else:
        own = pl.BlockSpec((tr, sc), lambda i, pos_ref: (i + pos_ref[1] * nrb, 0))
        out_specs = [pl.BlockSpec((tr, sc), lambda i, pos_ref: (i, pos_ref[0])) for w in ws]

    def kern(pos_ref, own_ref, slot_ref, *out_refs):
        tot = own_ref[...].astype(F32)
        for s in range(n_slots):
            tot = tot + slot_ref[s].astype(F32)
        for w, o_ref in zip(ws, out_refs):
            off = place[w][1] if kind == "col" else 0
            o_ref[...] = tot[:, off:off + o_ref.shape[1]]

    outs = pl.pallas_call(
        kern, name=name, grid_spec=_grid_spec((nrb,), [own, slot], out_specs),
        out_shape=[jax.ShapeDtypeStruct(shard_shapes[w], F32) for w in ws],
        compiler_params=_cparams(("parallel",)),
    )(pos, pair_sum, slots)
    return dict(zip(ws, outs))


def _adam_math(w, g, m, v):
    m2 = ADAM_B1 * m + (1.0 - ADAM_B1) * g
    v2 = ADAM_B2 * v + (1.0 - ADAM_B2) * (g * g)
    m_hat = m2 / (1.0 - ADAM_B1 ** ADAM_STEP)
    v_hat = v2 / (1.0 - ADAM_B2 ** ADAM_STEP)
    delta = -ADAM_LR * (m_hat / (jnp.sqrt(v_hat) + ADAM_EPS) + ADAM_WD * w)
    return delta, m2, v2


def _adamw(w, g, m, v, name, carried=()):
    rows, cols = w.shape
    tr, tc = _ew_block(rows, cols)

    def kern(w_ref, g_ref, m_ref, v_ref, d_ref, m2_ref, v2_ref, g_out_ref):
        g_ = g_ref[...]
        d_ref[...], m2_ref[...], v2_ref[...] = _adam_math(w_ref[...], g_, m_ref[...], v_ref[...])
        g_out_ref[...] = g_

    blk = pl.BlockSpec((tr, tc), lambda i, j: (i, j))
    return _pcall(
        kern, name=name, grid=(rows // tr, cols // tc), in_specs=[blk] * 4, out_specs=[blk] * 4,
        out_shape=[jax.ShapeDtypeStruct((rows, cols), F32)] * 4, args=(w, g, m, v),
        semantics=("parallel", "parallel"), carried=carried)


SMALL_ROWS = ("ln_mix_w", "ln_cross_w", "ln_mem_w", "ln_ffn_w", "ln_final_w")
ROW_HG_NORM, ROW_LB0, ROW_LB1 = 5, 6, 7
LOSS_LANE0 = HEAD_DIM


def _pack_small(vals):
    rows = [vals[n].reshape(1, D_MODEL) for n in SMALL_ROWS]
    pad = lambda a: jnp.pad(a, ((0, 0), (0, D_MODEL - a.shape[1])))
    rows.append(pad(vals["hg_norm_w"].reshape(1, HEAD_DIM)))
    rows.append(pad(vals["hg_lower_bounds"].reshape(2, HG_WIDTH)))
    return jnp.concatenate(rows, axis=0)


def _small_update(gathered, w, m, v, name="small_update"):
    def kern(g_ref, w_ref, m_ref, v_ref, grad_ref, d_ref, m2_ref, v2_ref, loss_ref):
        tot = g_ref[0]
        for s in range(1, N_DEV):
            tot = tot + g_ref[s]
        wv = w_ref[...]
        row = lax.broadcasted_iota(jnp.int32, (8, D_MODEL), 0)
        lane = lax.broadcasted_iota(jnp.int32, (8, D_MODEL), 1)
        l0, l1 = wv[ROW_LB0:ROW_LB0 + 1], wv[ROW_LB1:ROW_LB1 + 1]
        mx = jnp.maximum(l0, l1)
        e0, e1 = jnp.exp(l0 - mx), jnp.exp(l1 - mx)
        p0 = e0 / (e0 + e1)
        dlog = tot[ROW_LB0:ROW_LB0 + 1] * p0 * (1.0 - p0)
        tot = jnp.where(row == ROW_HG_NORM, tot + tot[ROW_LB1:ROW_LB1 + 1], tot)
        grad = jnp.where(row == ROW_LB0, dlog, jnp.where(row == ROW_LB1, -dlog, tot))
        grad = jnp.where((row == ROW_HG_NORM) & (lane >= HEAD_DIM), 0.0, grad)
        grad = jnp.where((row >= ROW_LB0) & (lane >= HG_WIDTH), 0.0, grad)
        grad_ref[...] = grad
        d_ref[...], m2_ref[...], v2_ref[...] = _adam_math(wv, grad, m_ref[...], v_ref[...])
        loss_ref[...] = tot[ROW_HG_NORM:ROW_HG_NORM + 1, LOSS_LANE0:LOSS_LANE0 + LANE]

    full = pl.BlockSpec((8, D_MODEL), lambda: (0, 0))
    return pl.pallas_call(
        kern, name=name,
        in_specs=[pl.BlockSpec((N_DEV, 8, D_MODEL), lambda: (0, 0, 0)), full, full, full],
        out_specs=[full, full, full, full, pl.BlockSpec((1, LANE), lambda: (0, 0))],
        out_shape=[jax.ShapeDtypeStruct((8, D_MODEL), F32)] * 4 + [jax.ShapeDtypeStruct((1, LANE), F32)],
        compiler_params=_cparams(),
    )(gathered, w, m, v)


def _unpack_small(p, shapes):
    out = {n: p[i].reshape(shapes[n]) for i, n in enumerate(SMALL_ROWS)}
    out["hg_norm_w"] = p[ROW_HG_NORM, :HEAD_DIM].reshape(shapes["hg_norm_w"])
    out["hg_lower_bounds"] = p[ROW_LB0:ROW_LB1 + 1, :HG_WIDTH].reshape(shapes["hg_lower_bounds"])
    return out


WHOLE = lambda f: (f, 0, None)
MID_MATRICES = ("w_branch_a", "w_branch_b", "w_out", "wq_cross", "wkv_cross", "wo_cross")
MID_WEIGHTS = MID_MATRICES
W_IN_PIECES = [("w_in", r0, 512) for r0 in range(0, D_MODEL // 2, 512)]
W13_PIECES = [("w13", r0, 512) for r0 in range(0, D_MODEL // 2, 512)]
GATHER_GROUPS = [("mid", [WHOLE(f) for f in MID_MATRICES]), ("w13a", W13_PIECES[:1]), ("w13b", W13_PIECES[1:]),
                 ("w2", [WHOLE("w2")])]
OTHER_WEIGHTS = ["w1", "w3", "w2"] + list(MID_WEIGHTS)
BEFORE = {
    "hgrn_fwd": [("wait", "mid")],
    "gate_fwd": [("wait", "w13a")],
    "mm_o": [("wait", "w13b")],
    "swiglu_fwd": [("wait", "w2")],
    "mm_dh": [("wait", "rs_w2"), ("chip_sum", "w2"), ("wait", "rs_w13"), ("chip_sum", "w13"), ("wait", "rs_mid")]
    + [("chip_sum", f) for f in MID_MATRICES],
}
CARRY = {
    "hgrn_fwd": [("d2d", [WHOLE(f) for f in MID_MATRICES])],
    "gate_fwd": [("d2d", W13_PIECES[:1])],
    "mm_o": [("d2d", W13_PIECES[1:])],
    "swiglu_fwd": [("d2d", [WHOLE("w2")])],
    "swiglu_bwd": [("pairx", ["w2"])],
    "mm_dhf": [("pairx", ["w13"])],
    "attn_merge_bwd": [("pairx", list(MID_MATRICES))],
    "mm_dwin_own": [("pairx_whole", ["w_in"])],
    "mm_dh": [("share", OTHER_WEIGHTS)],
}
AFTER = {
    "swiglu_bwd": [("pair_sum", "w2"), ("start", "rs_w2", [WHOLE("w2")])],
    "mm_dhf": [("pair_sum", "w13"), ("start", "rs_w13", [WHOLE("w13")])],
    "attn_merge_bwd": [("pair_sum", f) for f in MID_MATRICES] + [("start", "rs_mid", [WHOLE(f) for f in MID_MATRICES])],
    "mm_dwin_own": [("pair_sum", "w_in"), ("start", "rs_w_in", [WHOLE("w_in")])],
}
FINISH = [
    ("adamw", OTHER_WEIGHTS), ("wait", "rs_w_in"), ("small",), ("chip_sum", "w_in"),
    ("run", ("share", ["w_in"]), "rs_sibling_share_w_in"), ("adamw", ["w_in"]),
]


class _Net:
    def __init__(self, full, pos=None, shard_shapes=None, comm=True, specs=FULL_SPECS, place=WEIGHT_PLACE):
        self.full, self.pos, self.shard_shapes, self.comm = dict(full), pos, shard_shapes, comm
        self.specs, self.place = specs, place
        self.gw, self.recv, self.psum, self.slots, self.grads = {}, {}, {}, {}, {}
        self.gw_sibling = {}
        self.pending, self.token, self.last = {}, None, None

    def _make(self, kind, arg):
        if kind == "gather":
            return _gather_ici_comm(self.full, arg, self.specs)
        if kind == "ring":
            return _gather_ring_comm(self.full, *arg, self.specs)
        if kind == "d2d":
            return _gather_d2d_comm(self.full, arg, self.specs)
        if kind == "pairx":
            return _pairx_comm(self.gw, arg, self.specs)
        if kind == "pairx_whole":
            return _pairx_comm(self.gw_sibling, arg, self.specs, whole=True)
        if kind == "chipx":
            return _chipx_comm(self.psum, self.slots, arg, self.specs)
        assert kind == "share"
        return _share_comm(self.grads, arg, self.specs, self.place)

    def _store(self, kind, res):
        if kind in ("gather", "ring", "d2d"):
            self.full.update(res)
        elif kind in ("pairx", "pairx_whole"):
            for (tag, f), a in res.items():
                (self.gw if tag == "g" else self.recv)[f] = a
        elif kind == "chipx":
            for (tag, f), a in res.items():
                (self.psum if tag == "p" else self.slots)[f] = a
        else:
            self.grads.update(res)

    def run_comm(self, item, name):
        kind, arg = item
        self._store(kind, _run_comm([self._make(kind, arg)], name)[0])

    @staticmethod
    def _others(after, items):
        own = [a for cm in items for a in cm.arrays.values()]
        return [a for a in after if a is not None and all(a is not o for o in own)]

    def start(self, groups, kind, name):
        items = [self._make(kind, jobs) for _, jobs in groups]
        after = self._others([self.last], items)
        res, sems, token = _split_start(items, name, after=after[0] if after else None)
        for (group, jobs), r, s in zip(groups, res, sems):
            self._store(kind, r)
            self.pending[group] = (kind, jobs, s)
        self.token = self.last = token

    def wait(self, group, after=()):
        kind, jobs, sems = self.pending.pop(group)
        item = self._make(kind, jobs)
        res = _split_wait([item], [sems], self._others([self.last, *after], [item]), f"wait_{group}")[0]
        self._store(kind, res)

    def step(self, step):
        if step[0] == "wait":
            self.wait(step[1])
        elif step[0] == "start":
            self.start([(step[1], step[2])], "chipx", f"start_{step[1]}")
        elif step[0] == "pair_sum":
            f = step[1]
            self.psum[f], self.slots[f] = _pair_sum(self.gw[f], self.recv[f], self.pos, self.specs[f],
                                                    f"rs_pair_sum_{f}", whole=f in self.gw_sibling)
        elif step[0] == "chip_sum":
            f = step[1]
            self.grads.update(_chip_sum(self.psum[f], self.slots[f], self.pos, f, self.shard_shapes,
                                        self.specs, self.place, f"rs_chip_sum_{f}"))
        else:
            assert step[0] == "run"
            self.run_comm(step[1], step[2])

    def call(self, fn, name, *args, grad_of=None, sibling_half=False, **kw):
        for step in (BEFORE.get(name, []) if self.comm else []):
            self.step(step)
        items = CARRY.get(name, []) if self.comm else []
        carried = [self._make(k, a) for k, a in items]
        if self.token is not None:
            carried.append(_Token(self.token))
            self.token = None
        out, res = fn(*args, name=name, carried=carried, **kw)
        if grad_of is not None:
            (self.gw_sibling if sibling_half else self.gw)[grad_of] = out
        self.last = jax.tree.leaves(out)[0]
        for (kind, _), r in zip(items, res):
            self._store(kind, r)
        for step in (AFTER.get(name, []) if self.comm else []):
            self.step(step)
        return out


def _local_step(net, x, h, mem, target, small):
    full, call = net.full, net.call
    proj = call(_mm, "mm_proj", h, full["w_in"], mode="nn", out_dtype=F32)
    att = [call(_attn_fwd, f"attn_fwd_g{g}", proj, g) for g in range(3)]
    outs, lses = [a[0] for a in att], [a[1] for a in att]
    o_att = _attn_merge_fwd(outs, lses, "attn_merge")
    oraw, o_hg, states = call(_hg_fwd, "hgrn_fwd", proj, small["hg_lower_bounds"], small["hg_norm_w"])
    ya = call(_mm, "mm_branch_a", o_att, full["w_branch_a"], mode="nn", out_dtype=F32)
    yb = call(_mm, "mm_branch_b", o_hg, full["w_branch_b"], mode="nn", out_dtype=F32)
    merged = call(_gate_fwd, "gate_fwd", proj, ya, yb)
    x1 = call(_mm, "mm_out", merged, full["w_out"], mode="nn", out_dtype=F32, res=x)

    hc = _rms_fwd(x1, small["ln_cross_w"], "rms_cross")
    mn = _rms_fwd(mem, small["ln_mem_w"], "rms_mem")
    qc = call(_mm, "mm_q", hc, full["wq_cross"], mode="nn", out_dtype=F32)
    kvc = call(_mm, "mm_kv", mn, full["wkv_cross"], mode="nn", out_dtype=F32)
    oc = call(_cross_fwd, "cross_fwd", qc, kvc)
    x2 = call(_mm, "mm_o", oc, full["wo_cross"], mode="nn", out_dtype=F32, res=x1)

    hf = _rms_fwd(x2, small["ln_ffn_w"], "rms_ffn")
    ab = call(_mm, "mm_w13", hf, full["w13"], mode="nn", out_dtype=F32)
    u = call(_swiglu_fwd, "swiglu_fwd", ab)
    x3 = call(_mm, "mm_w2", u, full["w2"], mode="nn", out_dtype=F32, res=x2)

    dx3, dg_final, loss = _loss_head(x3, small["ln_final_w"], target, "loss_head")

    gs = {"ln_final_w": dg_final}
    du = call(_mm, "mm_du", dx3, full["w2"], mode="nt", out_dtype=F32)
    call(_mm, "mm_dw2", u, dx3, mode="tn", out_dtype=BF16, grad_of="w2")
    dab = call(_swiglu_bwd, "swiglu_bwd", ab, du)
    call(_mm, "mm_dw13", hf, dab, mode="tn", out_dtype=BF16, grad_of="w13")
    dhf = call(_mm, "mm_dhf", dab, full["w13"], mode="nt", out_dtype=F32)
    dx2, gs["ln_ffn_w"] = _rms_bwd(x2, small["ln_ffn_w"], dhf, dx3, "rms_ffn_bwd")
    doc = call(_mm, "mm_doc", dx2, full["wo_cross"], mode="nt", out_dtype=BF16)
    call(_mm, "mm_dwo", oc, dx2, mode="tn", out_dtype=BF16, grad_of="wo_cross")
    dqc, dkvc = _cross_bwd(qc, kvc, doc, "cross_bwd")
    call(_mm, "mm_dwq", hc, dqc, mode="tn", out_dtype=BF16, grad_of="wq_cross")
    dhc = call(_mm, "mm_dhc", dqc, full["wq_cross"], mode="nt", out_dtype=F32)
    call(_mm, "mm_dwkv", mn, dkvc, mode="tn", out_dtype=BF16, grad_of="wkv_cross")
    dmn = call(_mm, "mm_dmn", dkvc, full["wkv_cross"], mode="nt", out_dtype=F32)
    _, gs["ln_mem_w"] = _rms_bwd(mem, small["ln_mem_w"], dmn, None, "rms_mem_bwd")
    dx1, gs["ln_cross_w"] = _rms_bwd(x1, small["ln_cross_w"], dhc, dx2, "rms_cross_bwd")
    dmerged = call(_mm, "mm_dmerged", dx1, full["w_out"], mode="nt", out_dtype=F32)
    call(_mm, "mm_dwout", merged, dx1, mode="tn", out_dtype=BF16, grad_of="w_out")
    dya, dyb, dga, dgb = call(_gate_bwd, "gate_bwd", proj, ya, yb, dmerged)
    call(_mm, "mm_dwa", o_att, dya, mode="tn", out_dtype=BF16, grad_of="w_branch_a")
    do_att = call(_mm, "mm_doatt", dya, full["w_branch_a"], mode="nt", out_dtype=F32)
    call(_mm, "mm_dwb", o_hg, dyb, mode="tn", out_dtype=BF16, grad_of="w_branch_b")
    do_hg = call(_mm, "mm_dohg", dyb, full["w_branch_b"], mode="nt", out_dtype=F32)
    dqh, dfh, dih, dgh, dlb, gs["hg_norm_w"] = call(
        _hg_bwd, "hgrn_bwd", proj, small["hg_lower_bounds"], small["hg_norm_w"], oraw, states, do_hg)
    gs["hg_lb"] = dlb
    do_gs, dl_gs = call(_attn_merge_bwd, "attn_merge_bwd", outs, lses, do_att)
    dqs, dks, dvs = zip(*[call(_attn_bwd, f"attn_bwd_g{g}", proj, g, lses[g], do_gs[g], dl_gs[g]) for g in range(3)])
    dproj = jnp.concatenate([*dqs, *dks, *dvs, dqh, dfh, dih, dgh, dga, dgb], axis=1)
    if net.comm:
        half = D_MODEL // 2
        c = lax.axis_index("c")
        h_sibling = lax.dynamic_slice_in_dim(h, (1 - c) * half, half, axis=1)
        h_own = lax.dynamic_slice_in_dim(h, c * half, half, axis=1)
        call(_mm, "mm_dwin_sibling", h_sibling, dproj, mode="tn", out_dtype=BF16, grad_of="w_in", sibling_half=True)
        call(_mm, "mm_dwin_own", h_own, dproj, mode="tn", out_dtype=BF16, grad_of="w_in")
    else:
        call(_mm, "mm_dwin", h, dproj, mode="tn", out_dtype=BF16, grad_of="w_in")
    dh = call(_mm, "mm_dh", dproj, full["w_in"], mode="nt", out_dtype=F32)
    dx, gs["ln_mix_w"] = _rms_bwd(x, small["ln_mix_w"], dh, dx1, "rms_mix_bwd")
    return loss, dx, gs


WEIGHT_ORDER = ("ln_mix_w", "w_in", "hg_norm_w", "hg_lower_bounds", "w_branch_a", "w_branch_b", "w_out",
                "ln_cross_w", "ln_mem_w", "wq_cross", "wkv_cross", "wo_cross", "ln_ffn_w", "w1", "w3", "w2",
                "ln_final_w")


def kernel(x, mem, ln_mix_w, w_in, hg_norm_w, hg_lower_bounds, w_branch_a, w_branch_b, w_out, ln_cross_w, ln_mem_w, wq_cross, wkv_cross, wo_cross, ln_ffn_w, w1, w3, w2, ln_final_w, loss_target, m_ln_mix_w, m_w_in, m_hg_norm_w, m_hg_lower_bounds, m_w_branch_a, m_w_branch_b, m_w_out, m_ln_cross_w, m_ln_mem_w, m_wq_cross, m_wkv_cross, m_wo_cross, m_ln_ffn_w, m_w1, m_w3, m_w2, m_ln_final_w, v_ln_mix_w, v_w_in, v_hg_norm_w, v_hg_lower_bounds, v_w_branch_a, v_w_branch_b, v_w_out, v_ln_cross_w, v_ln_mem_w, v_wq_cross, v_wkv_cross, v_wo_cross, v_ln_ffn_w, v_w1, v_w3, v_w2, v_ln_final_w):
    args = dict(locals())
    w = {n: args[n] for n in WEIGHT_ORDER}
    m = {n: args["m_" + n] for n in WEIGHT_ORDER}
    v = {n: args["v_" + n] for n in WEIGHT_ORDER}
    shapes = {n: w[n].shape for n in WEIGHT_ORDER}
    mat = lambda a: a.reshape(a.shape[-2:])
    shard_shapes = {n: shapes[n][-2:] for n in BIG_WEIGHTS}

    pos = _mesh_scalars()

    def cast(f, token=None):
        return _cast_into_full({n: mat(w[n]) for n in BIG_WEIGHTS if WEIGHT_PLACE[n][0] == f}, f, pos,
                               FULL_SPECS, WEIGHT_PLACE, f"cast_{f}", token)

    net = _Net({"w_in": cast("w_in")}, pos, shard_shapes)
    net.start([(f"ring_a{i}", (*job, "a")) for i, job in enumerate(W_IN_PIECES)], "ring", "gather_start_w_in")
    rest = {f: cast(f, net.token) for f in FULL_SPECS if f != "w_in"}
    net.full.update(rest)
    small = {n: w[n].reshape(1, -1) for n in SMALL_ROWS}
    small["hg_norm_w"] = w["hg_norm_w"].reshape(1, HEAD_DIM)
    small["hg_lower_bounds"] = w["hg_lower_bounds"]
    x2d = x.reshape(SEQ, D_MODEL)
    h = _rms_fwd(x2d, small["ln_mix_w"], "rms_mix")
    for i, job in enumerate(W_IN_PIECES):
        net.wait(f"ring_a{i}", after=[*rest.values(), h] if i == 0 else ())
        net.start([(f"ring_b{i}", (*job, "b"))], "ring", f"gather_pass_on_w_in{i}")
    net.start(GATHER_GROUPS, "gather", "gather_start_rest")
    for i, job in enumerate(W_IN_PIECES):
        net.wait(f"ring_b{i}")
        net.run_comm(("d2d", [job]), f"gather_hand_over_w_in{i}")
    loss, dx, gs = _local_step(net, x2d, h, mem.reshape(MEM_LEN, D_MODEL), loss_target.reshape(SEQ, D_MODEL), small)

    out_g, out_d, out_m, out_v = {}, {}, {}, {}
    net.last = dx
    for step in FINISH:
        if step[0] == "adamw":
            for n in step[1]:
                out_d[n], out_m[n], out_v[n], out_g[n] = net.call(_adamw, f"adamw_{n}", mat(w[n]), net.grads[n],
                                                                  mat(m[n]), mat(v[n]))
        elif step[0] == "wait":
            net.wait(step[1], after=list(out_d.values()))
        elif step[0] == "small":
            pad = lambda a: jnp.pad(a, ((0, 0), (0, D_MODEL - a.shape[1])))
            part = jnp.concatenate(
                [gs[n] for n in SMALL_ROWS]
                + [pad(jnp.concatenate([gs["hg_norm_w"][0], loss], axis=1)), pad(gs["hg_lb"]),
                   pad(gs["hg_norm_w"][1]) if len(gs["hg_norm_w"]) > 1 else jnp.zeros((1, D_MODEL), F32)], axis=0)
            part, net.psum["w_in"] = lax.optimization_barrier((part, net.psum["w_in"]))
            sg, sd, sm, sv, loss_tot = _small_update(_gather_rows(part), _pack_small(w), _pack_small(m),
                                                     _pack_small(v))
            for dst, packed in ((out_g, sg), (out_d, sd), (out_m, sm), (out_v, sv)):
                dst.update(_unpack_small(packed, shapes))
        else:
            net.step(step)

    result = [loss_tot[0, 0], dx.reshape(x.shape)]
    for group in (out_g, out_d, out_m, out_v):
        result += [group[n].reshape(shapes[n]) for n in WEIGHT_ORDER]
    return tuple(result)
```

```python
import math

import jax
import jax.numpy as jnp
from jax import lax
from jax.experimental import pallas as pl
from jax.experimental.pallas import tpu as pltpu

F32 = jnp.float32
BF16 = jnp.bfloat16
MESH = pl.DeviceIdType.MESH

D_MODEL = 2048
SEQ = 2048
HEAD_DIM = 128
MEM_LEN = 256
ATT_GROUPS = ((128, 1), (512, 4), (2048, 16))
ATT_HEADS = 4
ATT_WIDTH = 3 * ATT_HEADS * HEAD_DIM
ATT_OUT = ATT_HEADS * HEAD_DIM
ATT_BLOCK = 128
HG_HEADS = 8
HG_WIDTH = HG_HEADS * HEAD_DIM
HG_CHUNK = 64
IN_WIDTH = 3 * ATT_WIDTH + 4 * HG_WIDTH + 2 * D_MODEL
CROSS_HEADS = 4
CROSS_WIDTH = CROSS_HEADS * HEAD_DIM
D_FF = 5632
RMS_EPS = 1e-6
ADAM_LR = 0.001
ADAM_B1 = 0.9
ADAM_B2 = 0.999
ADAM_EPS = 1e-08
ADAM_WD = 0.01
ADAM_STEP = 10
N_CHIPS = 4
N_DEV = 8

VMEM_LIMIT_BYTES = 56 * 1024 * 1024
LANE = 128
MXU_WIDTH = 256
MM_TILE_CAP = 1536
TRANSPOSE_CHUNK = 512
ANY = pl.BlockSpec(memory_space=pl.ANY)


def _cparams(sem=None):
    return pltpu.CompilerParams(dimension_semantics=sem, vmem_limit_bytes=VMEM_LIMIT_BYTES)


def _div(n, cap, mult):
    best = None
    for d in range(mult, min(n, cap) + 1, mult):
        if n % d == 0:
            best = d
    assert best is not None, (n, cap, mult)
    return best


def _sigmoid(x):
    return 1.0 / (1.0 + jnp.exp(-x))


def _dot(a, b):
    return jnp.dot(a.astype(BF16), b.astype(BF16), preferred_element_type=F32)


def _dot_nt(a, b):
    return lax.dot_general(a.astype(BF16), b.astype(BF16), (((1,), (1,)), ((), ())),
                           preferred_element_type=F32)


def _dot_tn(a, b):
    return jnp.dot(a.astype(F32).T.astype(BF16), b.astype(BF16), preferred_element_type=F32)


def _dot_exact(a, b):
    return jnp.dot(a, b, precision=lax.Precision.HIGHEST, preferred_element_type=F32)


class _Carried:
    def __init__(self, arrays, fresh, n_sems, start, finish, mid=None, reads=None):
        self.arrays, self.fresh, self.n_sems, self.reads = arrays, fresh, n_sems, reads or {}
        self.start, self.mid, self.finish = start, mid, finish


class _Token:
    def __init__(self, array):
        self.array = array


TOKEN_SHAPE = (8, LANE)


def _carried_layout(carried):
    akeys = list(dict.fromkeys(k for cm in carried for k in cm.arrays))
    fkeys = [(ci, k) for ci, cm in enumerate(carried) for k in cm.fresh]
    arrays = [next(cm.arrays[k] for cm in carried if k in cm.arrays) for k in akeys]
    shapes = [jax.ShapeDtypeStruct(a.shape, a.dtype) for a in arrays] + [carried[ci].fresh[k] for ci, k in fkeys]
    sems = []
    for cm in carried:
        sems += [pltpu.SemaphoreType.DMA((cm.n_sems,)), pltpu.SemaphoreType.DMA((cm.n_sems,))]
    return akeys, fkeys, arrays, shapes, sems


def _carried_reads(carried):
    rkeys = list(dict.fromkeys(k for cm in carried for k in cm.reads))
    return rkeys, [next(cm.reads[k] for cm in carried if k in cm.reads) for k in rkeys]


def _carried_results(carried, akeys, fkeys, outs, rkeys=(), read_refs=()):
    shared = dict(zip(akeys, outs[:len(akeys)]))
    shared.update(zip(rkeys, read_refs))
    res = [{k: shared[k] for k in list(cm.arrays) + [r for r in cm.reads if r in shared]} for cm in carried]
    for (ci, k), o in zip(fkeys, outs[len(akeys):]):
        res[ci][k] = o
    return res


def _pcall(kern, *, name, grid, in_specs, out_specs, out_shape, args, scratch_shapes=(), semantics=None,
           carried=()):
    tokens = [c.array for c in carried if isinstance(c, _Token)]
    carried = [c for c in carried if not isinstance(c, _Token)]
    single = not isinstance(out_shape, (list, tuple))
    out_specs = [out_specs] if single else list(out_specs)
    out_shape = [out_shape] if single else list(out_shape)
    n_real, n_out, n_scr = len(in_specs), len(out_shape), len(scratch_shapes)
    in_specs = list(in_specs) + [pl.BlockSpec(TOKEN_SHAPE, lambda *_: (0, 0))] * len(tokens)
    args = list(args) + tokens
    n_in = len(in_specs)
    if not carried:
        def plain(*refs):
            kern(*refs[:n_real], *refs[n_in:])

        outs = pl.pallas_call(plain if tokens else kern, name=name, grid=grid, in_specs=in_specs,
                              out_specs=out_specs, out_shape=out_shape, scratch_shapes=list(scratch_shapes),
                              compiler_params=_cparams(semantics))(*args)
        return (outs[0] if single else list(outs)), []
    akeys, fkeys, arrays, shapes, sems = _carried_layout(carried)
    rkeys, reads = _carried_reads(carried)
    n_a, n_f, n_r = len(akeys), len(fkeys), len(rkeys)
    total = math.prod(grid)
    mid_step = min(total - 1, (17 * total) // 20)

    def wrapped(*refs):
        ins = refs[:n_real]
        r0 = n_in + n_a
        o0 = r0 + n_r
        outs = refs[o0:o0 + n_out]
        a0 = o0 + n_out
        s0 = a0 + n_a + n_f
        per = _carried_results(carried, akeys, fkeys, refs[a0:s0], rkeys, refs[r0:o0])
        scratch = refs[s0:s0 + n_scr]
        sem = refs[s0 + n_scr:]
        step = 0
        for d, g in enumerate(grid):
            step = step * g + pl.program_id(d)

        @pl.when(step == 0)
        def _():
            for ci, cm in enumerate(carried):
                cm.start(per[ci], sem[2 * ci], sem[2 * ci + 1])

        kern(*ins, *outs, *scratch)

        @pl.when(step == mid_step)
        def _():
            for ci, cm in enumerate(carried):
                if cm.mid is not None:
                    cm.mid(per[ci], sem[2 * ci], sem[2 * ci + 1])

        @pl.when(step == total - 1)
        def _():
            for ci, cm in enumerate(carried):
                cm.finish(per[ci], sem[2 * ci], sem[2 * ci + 1])

    outs = pl.pallas_call(
        wrapped, name=name, grid=grid,
        in_specs=list(in_specs) + [ANY] * (n_a + n_r), out_specs=out_specs + [ANY] * (n_a + n_f),
        out_shape=out_shape + shapes,
        input_output_aliases={n_in + i: n_out + i for i in range(n_a)},
        scratch_shapes=list(scratch_shapes) + sems,
        compiler_params=_cparams(("arbitrary",) * len(grid)),
    )(*args, *arrays, *reads)
    res = _carried_results(carried, akeys, fkeys, outs[n_out:])
    return (outs[0] if single else list(outs[:n_out])), res


def _run_comm(carried, name):
    carried = list(carried)
    akeys, fkeys, arrays, shapes, sems = _carried_layout(carried)
    rkeys, reads = _carried_reads(carried)
    n_a, n_f, n_r = len(akeys), len(fkeys), len(rkeys)

    def body(*refs):
        o0 = n_a + n_r
        per = _carried_results(carried, akeys, fkeys, refs[o0:o0 + n_a + n_f], rkeys, refs[n_a:o0])
        sem = refs[o0 + n_a + n_f:]
        for hook in ("start", "mid", "finish"):
            for ci, cm in enumerate(carried):
                fn = getattr(cm, hook)
                if fn is not None:
                    fn(per[ci], sem[2 * ci], sem[2 * ci + 1])

    outs = pl.pallas_call(
        body, name=name, in_specs=[ANY] * (n_a + n_r), out_specs=[ANY] * (n_a + n_f), out_shape=shapes,
        input_output_aliases={i: i for i in range(n_a)}, scratch_shapes=sems,
    )(*arrays, *reads)
    return _carried_results(carried, akeys, fkeys, outs)


HBM_SPEC = pl.BlockSpec(memory_space=pltpu.HBM)
SEM_SPEC = pl.BlockSpec(memory_space=pltpu.SEMAPHORE)
SPLIT_EFFECT = pltpu.SideEffectType.DATAFLOW_SIDE_EFFECTING


def _in_hbm(a):
    return pltpu.with_memory_space_constraint(a, pltpu.HBM)


def _split_start(items, name, after=None):
    items = list(items)
    akeys, fkeys, arrays, shapes, sems = _carried_layout(items)
    assert not fkeys
    n_a, n_s = len(akeys), len(sems)
    n_in = n_a + (after is not None)

    def body(*refs):
        per = _carried_results(items, akeys, [], refs[n_in:n_in + n_a])
        sem = refs[n_in + n_a:n_in + n_a + n_s]
        for ci, cm in enumerate(items):
            cm.start(per[ci], sem[2 * ci], sem[2 * ci + 1])
        token = refs[n_in + n_a + n_s]
        token[...] = jnp.zeros_like(token)

    outs = pl.pallas_call(
        body, name=name, in_specs=[HBM_SPEC] * n_a + [ANY] * (after is not None),
        out_specs=[HBM_SPEC] * n_a + [SEM_SPEC] * n_s + [pl.BlockSpec(memory_space=pltpu.VMEM)],
        out_shape=[pltpu.HBM(s.shape, s.dtype) for s in shapes] + sems + [jax.ShapeDtypeStruct(TOKEN_SHAPE, F32)],
        input_output_aliases={i: i for i in range(n_a)},
        compiler_params=pltpu.CompilerParams(has_side_effects=SPLIT_EFFECT),
    )(*[_in_hbm(a) for a in arrays], *([after] if after is not None else []))
    res = _carried_results(items, akeys, [], outs[:n_a])
    sem_out = outs[n_a:n_a + n_s]
    return res, [(sem_out[2 * ci], sem_out[2 * ci + 1]) for ci in range(len(items))], outs[-1]


def _split_wait(items, sems, after, name):
    items = list(items)
    after = list(after) if isinstance(after, (list, tuple)) else [after]
    akeys, fkeys, arrays, shapes, _ = _carried_layout(items)
    n_a, n_s = len(akeys), 2 * len(items)

    def body(*refs):
        per = _carried_results(items, akeys, [], refs[n_a + n_s + len(after):])
        sem = refs[n_a:n_a + n_s]
        for ci, cm in enumerate(items):
            cm.finish(per[ci], sem[2 * ci], sem[2 * ci + 1])

    outs = pl.pallas_call(
        body, name=name, in_specs=[HBM_SPEC] * n_a + [SEM_SPEC] * n_s + [ANY] * len(after),
        out_specs=[HBM_SPEC] * n_a, out_shape=[pltpu.HBM(s.shape, s.dtype) for s in shapes],
        input_output_aliases={i: i for i in range(n_a)},
        compiler_params=pltpu.CompilerParams(has_side_effects=SPLIT_EFFECT),
    )(*arrays, *[s for pair in sems for s in pair], *after)
    return _carried_results(items, akeys, [], outs)


def _mm(a, b, *, mode, out_dtype, name, res=None, carried=()):
    if mode == "nn":
        (m, k), (k2, n) = a.shape, b.shape
    elif mode == "nt":
        (m, k), (n, k2) = a.shape, b.shape
    else:
        (k, m), (k2, n) = a.shape, b.shape
    assert k == k2, (name, a.shape, b.shape)
    tm = _div(m, MM_TILE_CAP, LANE)
    tn = _div(n, MM_TILE_CAP, MXU_WIDTH) if n % MXU_WIDTH == 0 else 0
    if tn < 1024:
        tn = _div(n, MM_TILE_CAP, LANE)
    out_shape = jax.ShapeDtypeStruct((m, n), out_dtype)

    if mode == "tn":
        assert res is None

        def kern_tn(a_ref, b_ref, o_ref, at_ref):
            @pl.when(pl.program_id(1) == 0)
            def _():
                step = min(TRANSPOSE_CHUNK, k)
                for c0 in range(0, k, step):
                    at_ref[:, c0:c0 + step] = a_ref[c0:c0 + step, :].astype(F32).T.astype(BF16)

            o_ref[...] = jnp.dot(at_ref[...], b_ref[...].astype(BF16),
                                 preferred_element_type=F32).astype(o_ref.dtype)

        return _pcall(
            kern_tn, name=name, grid=(m // tm, n // tn),
            in_specs=[pl.BlockSpec((k, tm), lambda i, j: (0, i)),
                      pl.BlockSpec((k, tn), lambda i, j: (0, j))],
            out_specs=pl.BlockSpec((tm, tn), lambda i, j: (i, j)),
            out_shape=out_shape, args=(a, b),
            scratch_shapes=[pltpu.VMEM((tm, k), BF16)],
            semantics=("parallel", "arbitrary"), carried=carried)

    tk = k if k <= 2048 else _div(k, 3072, LANE)
    nk = k // tk
    a_spec = pl.BlockSpec((tm, tk), lambda i, j, kk: (i, kk))
    if mode == "nn":
        b_spec = pl.BlockSpec((tk, tn), lambda i, j, kk: (kk, j))
        dot = _dot
    else:
        b_spec = pl.BlockSpec((tn, tk), lambda i, j, kk: (j, kk))
        dot = _dot_nt
    o_spec = pl.BlockSpec((tm, tn), lambda i, j, kk: (i, j))
    in_specs = [a_spec, b_spec]
    args = [a, b]
    if res is not None:
        in_specs.append(o_spec)
        args.append(res)
    has_res = res is not None

    def kern(*refs):
        a_ref, b_ref = refs[0], refs[1]
        r_ref = refs[2] if has_res else None
        o_ref = refs[3] if has_res else refs[2]
        part = dot(a_ref[...], b_ref[...])
        if nk == 1:
            if has_res:
                part = part + r_ref[...]
            o_ref[...] = part.astype(o_ref.dtype)
            return
        acc_ref = refs[-1]
        kk = pl.program_id(2)

        @pl.when(kk == 0)
        def _():
            acc_ref[...] = part

        @pl.when(kk > 0)
        def _():
            acc_ref[...] += part

        @pl.when(kk == nk - 1)
        def _():
            tot = acc_ref[...]
            if has_res:
                tot = tot + r_ref[...]
            o_ref[...] = tot.astype(o_ref.dtype)

    return _pcall(
        kern, name=name, grid=(m // tm, n // tn, nk),
        in_specs=in_specs, out_specs=o_spec, out_shape=out_shape, args=args,
        scratch_shapes=[pltpu.VMEM((tm, tn), F32)] if nk > 1 else [],
        semantics=("parallel", "parallel", "arbitrary"), carried=carried)


ROW_BLOCK = 512


def _rms_fwd(x, g, name):
    t, d = x.shape
    tr = min(ROW_BLOCK, t)

    def kern(x_ref, g_ref, o_ref):
        xf = x_ref[...]
        r = lax.rsqrt(jnp.mean(xf * xf, axis=-1, keepdims=True) + RMS_EPS)
        o_ref[...] = (xf * r * g_ref[...]).astype(o_ref.dtype)

    return pl.pallas_call(
        kern, name=name, grid=(t // tr,),
        in_specs=[pl.BlockSpec((tr, d), lambda i: (i, 0)), pl.BlockSpec((1, d), lambda i: (0, 0))],
        out_specs=pl.BlockSpec((tr, d), lambda i: (i, 0)),
        out_shape=jax.ShapeDtypeStruct((t, d), BF16),
        compiler_params=_cparams(("parallel",)),
    )(x, g)


def _rms_bwd(x, g, dh, res, name):
    t, d = x.shape
    tr = min(ROW_BLOCK, t)
    has_res = res is not None

    def kern(*refs):
        x_ref, g_ref, dh_ref = refs[:3]
        r_ref = refs[3] if has_res else None
        dx_ref, dg_ref = refs[-2], refs[-1]
        xf = x_ref[...]
        r = lax.rsqrt(jnp.mean(xf * xf, axis=-1, keepdims=True) + RMS_EPS)
        xn = xf * r
        dh_ = dh_ref[...]
        dhg = dh_ * g_ref[...]
        dx = r * (dhg - xn * jnp.mean(dhg * xn, axis=-1, keepdims=True))
        if has_res:
            dx = dx + r_ref[...]
        dx_ref[...] = dx
        part = jnp.sum(dh_ * xn, axis=0, keepdims=True)

        @pl.when(pl.program_id(0) == 0)
        def _():
            dg_ref[...] = part

        @pl.when(pl.program_id(0) > 0)
        def _():
            dg_ref[...] += part

    row = pl.BlockSpec((tr, d), lambda i: (i, 0))
    vec = pl.BlockSpec((1, d), lambda i: (0, 0))
    in_specs = [row, vec, row] + ([row] if has_res else [])
    args = [x, g, dh] + ([res] if has_res else [])
    return pl.pallas_call(
        kern, name=name, grid=(t // tr,), in_specs=in_specs, out_specs=[row, vec],
        out_shape=[jax.ShapeDtypeStruct((t, d), F32), jax.ShapeDtypeStruct((1, d), F32)],
        compiler_params=_cparams(("arbitrary",)),
    )(*args)


def _loss_head(x3, g, target, name):
    t, d = x3.shape
    tr = ROW_BLOCK

    def kern(x_ref, g_ref, t_ref, dx_ref, dg_ref, loss_ref):
        xf = x_ref[...]
        r = lax.rsqrt(jnp.mean(xf * xf, axis=-1, keepdims=True) + RMS_EPS)
        xn = xf * r
        gg = g_ref[...]
        err = xn * gg - t_ref[...]
        lpart = 0.5 * jnp.sum(jnp.mean(err * err, axis=-1, keepdims=True), axis=0, keepdims=True)
        dy = err * (1.0 / d)
        dyg = dy * gg
        dx_ref[...] = r * (dyg - xn * jnp.mean(dyg * xn, axis=-1, keepdims=True))
        gpart = jnp.sum(dy * xn, axis=0, keepdims=True)
        lrow = jnp.broadcast_to(lpart, (1, LANE))

        @pl.when(pl.program_id(0) == 0)
        def _():
            dg_ref[...] = gpart
            loss_ref[...] = lrow

        @pl.when(pl.program_id(0) > 0)
        def _():
            dg_ref[...] += gpart
            loss_ref[...] += lrow

    row = pl.BlockSpec((tr, d), lambda i: (i, 0))
    vec = pl.BlockSpec((1, d), lambda i: (0, 0))
    return pl.pallas_call(
        kern, name=name, grid=(t // tr,), in_specs=[row, vec, row],
        out_specs=[row, vec, pl.BlockSpec((1, LANE), lambda i: (0, 0))],
        out_shape=[jax.ShapeDtypeStruct((t, d), F32), jax.ShapeDtypeStruct((1, d), F32),
                   jax.ShapeDtypeStruct((1, LANE), F32)],
        compiler_params=_cparams(("arbitrary",)),
    )(x3, g, target)


ATT_SCALE = HEAD_DIM ** -0.5
Q_BLOCK0, K_BLOCK0, V_BLOCK0 = 0, ATT_WIDTH // HEAD_DIM, 2 * ATT_WIDTH // HEAD_DIM


def _residue_rows(dil, r, n):
    if dil == 1:
        return pl.ds(n * ATT_BLOCK, ATT_BLOCK)
    return pl.ds(n * ATT_BLOCK * dil + r, ATT_BLOCK, stride=dil)


def _band_mask(with_prev):
    width = 2 * ATT_BLOCK if with_prev else ATT_BLOCK
    iq = lax.broadcasted_iota(jnp.int32, (ATT_BLOCK, width), 0)
    ik = lax.broadcasted_iota(jnp.int32, (ATT_BLOCK, width), 1)
    if not with_prev:
        return ik <= iq
    return ((ik < ATT_BLOCK) & (iq <= ik)) | ((ik >= ATT_BLOCK) & ((ik - ATT_BLOCK) <= iq))


def _band_keys(ref, dil, r, n):
    own = ref[_residue_rows(dil, r, n), :]
    if n == 0:
        return own
    return jnp.concatenate([ref[_residue_rows(dil, r, n - 1), :], own], axis=0)


def _attn_col_spec(base, grp):
    return pl.BlockSpec((SEQ, HEAD_DIM), lambda h: (0, base + grp * ATT_HEADS + h))


def _attn_fwd(proj, grp, name, carried=()):
    _, dil = ATT_GROUPS[grp]
    nb = SEQ // dil // ATT_BLOCK

    def kern(q_ref, k_ref, v_ref, o_ref, lse_ref):
        for r in range(dil):
            for n in range(nb):
                rows = _residue_rows(dil, r, n)
                s = _dot_nt(q_ref[rows, :], _band_keys(k_ref, dil, r, n)) * ATT_SCALE
                s = jnp.where(_band_mask(n > 0), s, -jnp.inf)
                m = jnp.max(s, axis=-1, keepdims=True)
                p = jnp.exp(s - m)
                l = jnp.sum(p, axis=-1, keepdims=True)
                o_ref[rows, :] = _dot(p / l, _band_keys(v_ref, dil, r, n))
                lse_ref[rows, :] = jnp.broadcast_to(m + jnp.log(l), (ATT_BLOCK, HEAD_DIM))

    out_spec = pl.BlockSpec((SEQ, HEAD_DIM), lambda h: (0, h))
    return _pcall(
        kern, name=name, grid=(ATT_HEADS,),
        in_specs=[_attn_col_spec(Q_BLOCK0, grp), _attn_col_spec(K_BLOCK0, grp), _attn_col_spec(V_BLOCK0, grp)],
        out_specs=[out_spec, out_spec],
        out_shape=[jax.ShapeDtypeStruct((SEQ, ATT_OUT), F32)] * 2, args=(proj, proj, proj),
        semantics=("parallel",), carried=carried)


def _attn_weights(l0, l1, l2):
    mx = jnp.maximum(jnp.maximum(l0, l1), l2)
    e0, e1, e2 = jnp.exp(l0 - mx), jnp.exp(l1 - mx), jnp.exp(l2 - mx)
    den = e0 + e1 + e2
    return e0 / den, e1 / den, e2 / den


def _attn_merge_fwd(outs, lses, name):
    tr = ROW_BLOCK

    def kern(o0, o1, o2, l0, l1, l2, out_ref):
        a0, a1, a2 = _attn_weights(l0[...], l1[...], l2[...])
        out_ref[...] = (a0 * o0[...] + a1 * o1[...] + a2 * o2[...]).astype(out_ref.dtype)

    spec = pl.BlockSpec((tr, ATT_OUT), lambda i: (i, 0))
    return pl.pallas_call(
        kern, name=name, grid=(SEQ // tr,), in_specs=[spec] * 6, out_specs=spec,
        out_shape=jax.ShapeDtypeStruct((SEQ, ATT_OUT), BF16),
        compiler_params=_cparams(("parallel",)),
    )(*outs, *lses)


def _attn_merge_bwd(outs, lses, do_att, name, carried=()):
    tr = ROW_BLOCK

    def kern(o0, o1, o2, l0, l1, l2, do_ref, d0, d1, d2, t0, t1, t2):
        alphas = _attn_weights(l0[...], l1[...], l2[...])
        do = do_ref[...]
        o_att = alphas[0] * o0[...] + alphas[1] * o1[...] + alphas[2] * o2[...]
        prod = do * o_att
        parts = []
        for h in range(ATT_HEADS):
            sl = slice(h * HEAD_DIM, (h + 1) * HEAD_DIM)
            tot = jnp.sum(prod[:, sl], axis=-1, keepdims=True)
            parts.append(jnp.broadcast_to(tot, (tr, HEAD_DIM)))
        dd = jnp.concatenate(parts, axis=1)
        for a, d_ref, t_ref in zip(alphas, (d0, d1, d2), (t0, t1, t2)):
            d_ref[...] = a * do
            t_ref[...] = -a * dd

    spec = pl.BlockSpec((tr, ATT_OUT), lambda i: (i, 0))
    res, cres = _pcall(
        kern, name=name, grid=(SEQ // tr,), in_specs=[spec] * 7, out_specs=[spec] * 6,
        out_shape=[jax.ShapeDtypeStruct((SEQ, ATT_OUT), F32)] * 6, args=(*outs, *lses, do_att),
        semantics=("parallel",), carried=carried)
    return (res[:3], res[3:]), cres


def _attn_bwd(proj, grp, lse, do_g, dl_g, name, carried=()):
    _, dil = ATT_GROUPS[grp]
    nb = SEQ // dil // ATT_BLOCK

    def kern(q_ref, k_ref, v_ref, do_ref, lse_ref, dl_ref, dq_ref, dk_ref, dv_ref, dq_acc, dk_acc, dv_acc):
        dk_acc[...] = jnp.zeros_like(dk_acc)
        dv_acc[...] = jnp.zeros_like(dv_acc)
        for r in range(dil):
            for n in range(nb):
                rows = _residue_rows(dil, r, n)
                q, do = q_ref[rows, :], do_ref[rows, :]
                kk, vv = _band_keys(k_ref, dil, r, n), _band_keys(v_ref, dil, r, n)
                s = _dot_nt(q, kk) * ATT_SCALE
                p = jnp.where(_band_mask(n > 0), jnp.exp(s - lse_ref[rows, :][:, :1]), 0.0)
                ds = p * (_dot_nt(do, vv) + dl_ref[rows, :][:, :1])
                dq_acc[rows, :] = _dot(ds, kk) * ATT_SCALE
                dk = _dot_tn(ds, q) * ATT_SCALE
                dv = _dot_tn(p, do)
                if n > 0:
                    prev = _residue_rows(dil, r, n - 1)
                    dk_acc[prev, :] += dk[:ATT_BLOCK]
                    dv_acc[prev, :] += dv[:ATT_BLOCK]
                    dk, dv = dk[ATT_BLOCK:], dv[ATT_BLOCK:]
                dk_acc[rows, :] += dk
                dv_acc[rows, :] += dv
        dq_ref[...] = dq_acc[...].astype(dq_ref.dtype)
        dk_ref[...] = dk_acc[...].astype(dk_ref.dtype)
        dv_ref[...] = dv_acc[...].astype(dv_ref.dtype)

    spec = pl.BlockSpec((SEQ, HEAD_DIM), lambda h: (0, h))
    return _pcall(
        kern, name=name, grid=(ATT_HEADS,),
        in_specs=[_attn_col_spec(Q_BLOCK0, grp), _attn_col_spec(K_BLOCK0, grp), _attn_col_spec(V_BLOCK0, grp),
                  spec, spec, spec],
        out_specs=[spec] * 3,
        out_shape=[jax.ShapeDtypeStruct((SEQ, ATT_OUT), BF16)] * 3, args=(proj, proj, proj, do_g, lse, dl_g),
        scratch_shapes=[pltpu.VMEM((SEQ, HEAD_DIM), F32)] * 3,
        semantics=("parallel",), carried=carried)


HG_HEADS_PER_STEP = 8
HG_BLOCK_W = 4 * HEAD_DIM
HG_BLOCKS = HG_HEADS_PER_STEP * HEAD_DIM // HG_BLOCK_W
HG_STEP_W = HG_HEADS_PER_STEP * HEAD_DIM
HG_Q_BLK = (3 * ATT_WIDTH) // HG_BLOCK_W
HG_N_CHUNKS = SEQ // HG_CHUNK
HG_MID = HG_CHUNK // 2


def _lower_bound(lb_ref, sl):
    l0, l1 = lb_ref[0:1, sl], lb_ref[1:2, sl]
    mx = jnp.maximum(l0, l1)
    e0, e1 = jnp.exp(l0 - mx), jnp.exp(l1 - mx)
    return e0 / (e0 + e1)


def _tri(lower):
    i = lax.broadcasted_iota(jnp.int32, (HG_CHUNK, HG_CHUNK), 0)
    j = lax.broadcasted_iota(jnp.int32, (HG_CHUNK, HG_CHUNK), 1)
    return (i >= j) if lower else (i <= j)


def _head_mean(x):
    parts = []
    for hd in range(x.shape[1] // HEAD_DIM):
        m = jnp.mean(x[:, hd * HEAD_DIM:(hd + 1) * HEAD_DIM], axis=-1, keepdims=True)
        parts.append(jnp.broadcast_to(m, (x.shape[0], HEAD_DIM)))
    return jnp.concatenate(parts, axis=1)


def _hg_chunk_terms(qh, fh, lb):
    sig = _sigmoid(fh)
    f = lb + (1.0 - lb) * sig
    k = 1.0 - f
    b = _dot_exact(_tri(True).astype(F32), jnp.log(f))
    bl = b[HG_CHUNK - 1:HG_CHUNK, :]
    br = b[HG_MID:HG_MID + 1, :]
    sq = _sigmoid(qh)
    q = qh * sq
    return dict(sig=sig, f=f, k=k, b=b, bl=bl, br=br, sq=sq, q=q,
                e1=jnp.exp(bl - b), e2=jnp.exp(b), e3=jnp.exp(b - br), e4=jnp.exp(br - b))


def _hg_fwd(proj, lbw, normw, name, carried=()):
    def in_blks(off):
        return [pl.BlockSpec((HG_CHUNK, HG_BLOCK_W), lambda hp, n, b=b: (n, HG_Q_BLK + off + hp * HG_BLOCKS + b))
                for b in range(HG_BLOCKS)]

    def kern(*refs):
        q_refs, f_refs, i_refs, g_refs = (refs[k * HG_BLOCKS:(k + 1) * HG_BLOCKS] for k in range(4))
        lb_ref, nw_ref, oraw_ref, ohg_ref, st_ref, state = refs[4 * HG_BLOCKS:]

        @pl.when(pl.program_id(1) == 0)
        def _():
            state[...] = jnp.zeros_like(state)

        causal = _tri(True)
        wide = lambda rs: jnp.concatenate([r[...] for r in rs], axis=1)
        t = _hg_chunk_terms(wide(q_refs), wide(f_refs), _lower_bound(lb_ref, slice(None)))
        v, gh = wide(i_refs), wide(g_refs)
        kd, qb, qr, kr = t["k"] * t["e1"], t["q"] * t["e2"], t["q"] * t["e3"], t["k"] * t["e4"]
        decay = jnp.exp(t["bl"])
        outs = []
        for hd in range(HG_HEADS_PER_STEP):
            sl = slice(hd * HEAD_DIM, (hd + 1) * HEAD_DIM)
            st = state[hd]
            st_ref[0, hd] = st
            a = jnp.where(causal, _dot_nt(qr[:, sl], kr[:, sl]), 0.0)
            outs.append(_dot_nt(qb[:, sl], st) + _dot(a, v[:, sl]))
            state[hd] = st * decay[:, sl] + _dot_tn(v[:, sl], kd[:, sl])
        o = jnp.concatenate(outs, axis=1)
        oraw_ref[...] = o
        r = lax.rsqrt(_head_mean(o * o) + RMS_EPS)
        nw = jnp.tile(nw_ref[...], (1, HG_HEADS_PER_STEP))
        ohg_ref[...] = (o * r * nw * (gh * _sigmoid(gh))).astype(ohg_ref.dtype)

    out_blk = pl.BlockSpec((HG_CHUNK, HG_STEP_W), lambda hp, n: (n, hp))
    return _pcall(
        kern, name=name, grid=(HG_HEADS // HG_HEADS_PER_STEP, HG_N_CHUNKS),
        in_specs=[*in_blks(0), *in_blks(2), *in_blks(4), *in_blks(6),
                  pl.BlockSpec((2, HG_STEP_W), lambda hp, n: (0, hp)),
                  pl.BlockSpec((1, HEAD_DIM), lambda hp, n: (0, 0))],
        out_specs=[out_blk, out_blk,
                   pl.BlockSpec((1, HG_HEADS_PER_STEP, HEAD_DIM, HEAD_DIM), lambda hp, n: (n, hp, 0, 0))],
        out_shape=[jax.ShapeDtypeStruct((SEQ, HG_WIDTH), F32), jax.ShapeDtypeStruct((SEQ, HG_WIDTH), BF16),
                   jax.ShapeDtypeStruct((HG_N_CHUNKS, HG_HEADS, HEAD_DIM, HEAD_DIM), F32)],
        args=(*[proj] * (4 * HG_BLOCKS), lbw, normw),
        scratch_shapes=[pltpu.VMEM((HG_HEADS_PER_STEP, HEAD_DIM, HEAD_DIM), F32)],
        semantics=("parallel", "arbitrary"), carried=carried)


def _hg_bwd(proj, lbw, normw, oraw, states, do_hg, name, carried=()):
    last = HG_N_CHUNKS - 1

    def in_blks(off):
        return [pl.BlockSpec((HG_CHUNK, HG_BLOCK_W),
                             lambda hp, n, b=b: (last - n, HG_Q_BLK + off + hp * HG_BLOCKS + b))
                for b in range(HG_BLOCKS)]

    blk = pl.BlockSpec((HG_CHUNK, HG_STEP_W), lambda hp, n: (last - n, hp))

    def kern(*refs):
        q_refs, f_refs, i_refs, g_refs = (refs[k * HG_BLOCKS:(k + 1) * HG_BLOCKS] for k in range(4))
        (lb_ref, nw_ref, oraw_ref, st_ref, do_ref, dq_ref, df_ref, di_ref, dg_ref, dlb_ref, dnw_ref,
         dstate) = refs[4 * HG_BLOCKS:]
        first = pl.program_id(1) == 0

        @pl.when(first)
        def _():
            dstate[...] = jnp.zeros_like(dstate)

        causal = _tri(True)
        wide = lambda rs: jnp.concatenate([r[...] for r in rs], axis=1)
        cat = lambda parts: jnp.concatenate(parts, axis=1)
        qh, fh, v, gh = wide(q_refs), wide(f_refs), wide(i_refs), wide(g_refs)
        o, dout = oraw_ref[...], do_ref[...]
        nw = jnp.tile(nw_ref[...], (1, HG_HEADS_PER_STEP))
        sgg = _sigmoid(gh)
        r = lax.rsqrt(_head_mean(o * o) + RMS_EPS)
        xn = o * r
        dg_ref[...] = (dout * xn * nw * (sgg * (1.0 + gh * (1.0 - sgg)))).astype(dg_ref.dtype)
        don = dout * (gh * sgg)
        dnw_wide = jnp.sum(don * xn, axis=0, keepdims=True)
        dnw_tot = dnw_wide[:, :HEAD_DIM]
        for hd in range(1, HG_HEADS_PER_STEP):
            dnw_tot = dnw_tot + dnw_wide[:, hd * HEAD_DIM:(hd + 1) * HEAD_DIM]
        tt = don * nw
        do = r * (tt - xn * _head_mean(tt * xn))
        lb = _lower_bound(lb_ref, slice(None))
        t = _hg_chunk_terms(qh, fh, lb)
        k, q = t["k"], t["q"]
        kd, qb, qr, kr = k * t["e1"], q * t["e2"], q * t["e3"], k * t["e4"]
        decay = jnp.exp(t["bl"])
        dqb, dqr, dkr, dkd, dv, ddecay = [], [], [], [], [], []
        for hd in range(HG_HEADS_PER_STEP):
            sl = slice(hd * HEAD_DIM, (hd + 1) * HEAD_DIM)
            st = st_ref[0, hd]
            dstn = dstate[hd]
            a = jnp.where(causal, _dot_nt(qr[:, sl], kr[:, sl]), 0.0)
            da = jnp.where(causal, _dot_nt(do[:, sl], v[:, sl]), 0.0)
            dqb.append(_dot(do[:, sl], st))
            dv.append(_dot_tn(a, do[:, sl]) + _dot_nt(kd[:, sl], dstn))
            dqr.append(_dot(da, kr[:, sl]))
            dkr.append(_dot_tn(da, qr[:, sl]))
            dkd.append(_dot(v[:, sl], dstn))
            ddecay.append(jnp.sum(dstn * st, axis=0, keepdims=True))
            dstate[hd] = dstn * decay[:, sl] + _dot_tn(do[:, sl], qb[:, sl])
        dqb, dqr, dkr, dkd, dv, ddecay = cat(dqb), cat(dqr), cat(dkr), cat(dkd), cat(dv), cat(ddecay)
        dq = dqb * t["e2"] + dqr * t["e3"]
        dk = dkd * t["e1"] + dkr * t["e4"]
        db = dqb * qb + dqr * qr - dkr * kr - dkd * kd
        dbl = jnp.sum(dkd * kd, axis=0, keepdims=True) + ddecay * decay
        dbr = jnp.sum(dkr * kr - dqr * qr, axis=0, keepdims=True)
        rows = lax.broadcasted_iota(jnp.int32, db.shape, 0)
        dlf = _dot_exact(_tri(False).astype(F32), db) + dbl + jnp.where(rows <= HG_MID, dbr, 0.0)
        df = dlf / t["f"] - dk
        sig, sq = t["sig"], t["sq"]
        df_ref[...] = (df * (1.0 - lb) * sig * (1.0 - sig)).astype(df_ref.dtype)
        dlb_row = jnp.sum(df * (1.0 - sig), axis=0, keepdims=True)
        dq_ref[...] = (dq * (sq * (1.0 + qh * (1.0 - sq)))).astype(dq_ref.dtype)
        di_ref[...] = dv.astype(di_ref.dtype)
        dnw_blk = jnp.broadcast_to(dnw_tot, (8, HEAD_DIM))

        @pl.when(first)
        def _():
            dlb_ref[...] = dlb_row
            dnw_ref[...] = dnw_blk

        @pl.when(jnp.logical_not(first))
        def _():
            dlb_ref[...] += dlb_row
            dnw_ref[...] += dnw_blk

    n_hp = HG_HEADS // HG_HEADS_PER_STEP
    outs, cres = _pcall(
        kern, name=name, grid=(n_hp, HG_N_CHUNKS),
        in_specs=[*in_blks(0), *in_blks(2), *in_blks(4), *in_blks(6),
                  pl.BlockSpec((2, HG_STEP_W), lambda hp, n: (0, hp)),
                  pl.BlockSpec((1, HEAD_DIM), lambda hp, n: (0, 0)),
                  blk,
                  pl.BlockSpec((1, HG_HEADS_PER_STEP, HEAD_DIM, HEAD_DIM), lambda hp, n: (last - n, hp, 0, 0)),
                  blk],
        out_specs=[blk, blk, blk, blk,
                   pl.BlockSpec((1, HG_STEP_W), lambda hp, n: (0, hp)),
                   pl.BlockSpec((8, HEAD_DIM), lambda hp, n: (hp, 0))],
        out_shape=[jax.ShapeDtypeStruct((SEQ, HG_WIDTH), BF16)] * 4
        + [jax.ShapeDtypeStruct((1, HG_WIDTH), F32), jax.ShapeDtypeStruct((8 * n_hp, HEAD_DIM), F32)],
        args=(*[proj] * (4 * HG_BLOCKS), lbw, normw, oraw, states, do_hg),
        scratch_shapes=[pltpu.VMEM((HG_HEADS_PER_STEP, HEAD_DIM, HEAD_DIM), F32)],
        semantics=("parallel", "arbitrary"), carried=carried)
    dqh, dfh, dih, dgh, dlb, dnw = outs
    return (dqh, dfh, dih, dgh, dlb, [dnw[8 * i:8 * i + 1] for i in range(n_hp)]), cres


GATE_BLOCK_W = 512
GATE_A_BLK = (3 * ATT_WIDTH + 4 * HG_WIDTH) // GATE_BLOCK_W
GATE_B_BLK = GATE_A_BLK + D_MODEL // GATE_BLOCK_W


def _gate_specs():
    tr = ROW_BLOCK
    blk = pl.BlockSpec((tr, GATE_BLOCK_W), lambda i, j: (i, j))
    ga = pl.BlockSpec((tr, GATE_BLOCK_W), lambda i, j: (i, GATE_A_BLK + j))
    gb = pl.BlockSpec((tr, GATE_BLOCK_W), lambda i, j: (i, GATE_B_BLK + j))
    return (SEQ // tr, D_MODEL // GATE_BLOCK_W), blk, ga, gb


def _gate_fwd(proj, ya, yb, name, carried=()):
    grid, blk, ga, gb = _gate_specs()

    def kern(ga_ref, gb_ref, ya_ref, yb_ref, o_ref):
        o_ref[...] = (_sigmoid(ga_ref[...]) * ya_ref[...] + _sigmoid(gb_ref[...]) * yb_ref[...]).astype(o_ref.dtype)

    return _pcall(
        kern, name=name, grid=grid, in_specs=[ga, gb, blk, blk], out_specs=blk,
        out_shape=jax.ShapeDtypeStruct((SEQ, D_MODEL), BF16), args=(proj, proj, ya, yb),
        semantics=("parallel", "parallel"), carried=carried)


def _gate_bwd(proj, ya, yb, dmerged, name, carried=()):
    grid, blk, ga, gb = _gate_specs()

    def kern(ga_ref, gb_ref, ya_ref, yb_ref, dm_ref, dya_ref, dyb_ref, dga_ref, dgb_ref):
        dm = dm_ref[...]
        sa, sb = _sigmoid(ga_ref[...]), _sigmoid(gb_ref[...])
        dya_ref[...] = (dm * sa).astype(dya_ref.dtype)
        dyb_ref[...] = (dm * sb).astype(dyb_ref.dtype)
        dga_ref[...] = (dm * ya_ref[...] * sa * (1.0 - sa)).astype(dga_ref.dtype)
        dgb_ref[...] = (dm * yb_ref[...] * sb * (1.0 - sb)).astype(dgb_ref.dtype)

    return _pcall(
        kern, name=name, grid=grid, in_specs=[ga, gb, blk, blk, blk], out_specs=[blk] * 4,
        out_shape=[jax.ShapeDtypeStruct((SEQ, D_MODEL), BF16)] * 4, args=(proj, proj, ya, yb, dmerged),
        semantics=("parallel", "parallel"), carried=carried)


FF_SHARD = D_FF // N_CHIPS


def _swiglu_fwd(ab, name, carried=()):
    tr = ROW_BLOCK

    def kern(ab_ref, u_ref):
        a, b = ab_ref[:, :FF_SHARD], ab_ref[:, FF_SHARD:]
        u_ref[...] = (a * _sigmoid(a) * b).astype(u_ref.dtype)

    return _pcall(
        kern, name=name, grid=(SEQ // tr, N_CHIPS),
        in_specs=[pl.BlockSpec((tr, 2 * FF_SHARD), lambda i, j: (i, j))],
        out_specs=pl.BlockSpec((tr, FF_SHARD), lambda i, j: (i, j)),
        out_shape=jax.ShapeDtypeStruct((SEQ, D_FF), BF16), args=(ab,),
        semantics=("parallel", "parallel"), carried=carried)


def _swiglu_bwd(ab, du, name, carried=()):
    tr = ROW_BLOCK

    def kern(ab_ref, du_ref, dab_ref):
        a, b = ab_ref[:, :FF_SHARD], ab_ref[:, FF_SHARD:]
        du_ = du_ref[...]
        sg = _sigmoid(a)
        dab_ref[:, :FF_SHARD] = (du_ * b * (sg * (1.0 + a * (1.0 - sg)))).astype(dab_ref.dtype)
        dab_ref[:, FF_SHARD:] = (du_ * (a * sg)).astype(dab_ref.dtype)

    wide = pl.BlockSpec((tr, 2 * FF_SHARD), lambda i, j: (i, j))
    return _pcall(
        kern, name=name, grid=(SEQ // tr, N_CHIPS),
        in_specs=[wide, pl.BlockSpec((tr, FF_SHARD), lambda i, j: (i, j))],
        out_specs=wide, out_shape=jax.ShapeDtypeStruct((SEQ, 2 * D_FF), BF16), args=(ab, du),
        semantics=("parallel", "parallel"), carried=carried)


CROSS_ROWS = 512


def _cross_fwd(qc, kvc, name, carried=()):
    def kern(q_ref, k_ref, v_ref, o_ref):
        s = _dot_nt(q_ref[...], k_ref[...]) * ATT_SCALE
        m = jnp.max(s, axis=-1, keepdims=True)
        e = jnp.exp(s - m)
        p = e / jnp.sum(e, axis=-1, keepdims=True)
        o_ref[...] = _dot(p, v_ref[...]).astype(o_ref.dtype)

    qblk = pl.BlockSpec((CROSS_ROWS, HEAD_DIM), lambda h, i: (i, h))
    return _pcall(
        kern, name=name, grid=(CROSS_HEADS, SEQ // CROSS_ROWS),
        in_specs=[qblk, pl.BlockSpec((MEM_LEN, HEAD_DIM), lambda h, i: (0, h)),
                  pl.BlockSpec((MEM_LEN, HEAD_DIM), lambda h, i: (0, CROSS_HEADS + h))],
        out_specs=qblk, out_shape=jax.ShapeDtypeStruct((SEQ, CROSS_WIDTH), BF16), args=(qc, kvc, kvc),
        semantics=("parallel", "parallel"), carried=carried)


def _cross_bwd(qc, kvc, doc, name):
    def kern(q_ref, k_ref, v_ref, do_ref, dq_ref, dk_ref, dv_ref):
        q, k, v, do = q_ref[...], k_ref[...], v_ref[...], do_ref[...]
        s = _dot_nt(q, k) * ATT_SCALE
        m = jnp.max(s, axis=-1, keepdims=True)
        e = jnp.exp(s - m)
        p = e / jnp.sum(e, axis=-1, keepdims=True)
        dp = _dot_nt(do, v)
        ds = p * (dp - jnp.sum(dp * p, axis=-1, keepdims=True))
        dq_ref[...] = (_dot(ds, k) * ATT_SCALE).astype(dq_ref.dtype)
        dk = _dot_tn(ds, q) * ATT_SCALE
        dv = _dot_tn(p, do)

        @pl.when(pl.program_id(1) == 0)
        def _():
            dk_ref[...] = dk
            dv_ref[...] = dv

        @pl.when(pl.program_id(1) > 0)
        def _():
            dk_ref[...] += dk
            dv_ref[...] += dv

    qblk = pl.BlockSpec((CROSS_ROWS, HEAD_DIM), lambda h, i: (i, h))
    kblk = pl.BlockSpec((MEM_LEN, HEAD_DIM), lambda h, i: (0, h))
    dq, dk, dv = pl.pallas_call(
        kern, name=name, grid=(CROSS_HEADS, SEQ // CROSS_ROWS),
        in_specs=[qblk, kblk, pl.BlockSpec((MEM_LEN, HEAD_DIM), lambda h, i: (0, CROSS_HEADS + h)), qblk],
        out_specs=[qblk, kblk, kblk],
        out_shape=[jax.ShapeDtypeStruct((SEQ, CROSS_WIDTH), BF16),
                   jax.ShapeDtypeStruct((MEM_LEN, CROSS_WIDTH), F32),
                   jax.ShapeDtypeStruct((MEM_LEN, CROSS_WIDTH), F32)],
        compiler_params=_cparams(("parallel", "arbitrary")),
    )(qc, kvc, kvc, doc)
    return dq, jnp.concatenate([dk, dv], axis=1)


FULL_SPECS = {
    "w_in": ("col", D_MODEL, IN_WIDTH),
    "w_branch_a": ("col", ATT_OUT, D_MODEL),
    "w_branch_b": ("col", HG_WIDTH, D_MODEL),
    "w_out": ("row", D_MODEL, D_MODEL),
    "wq_cross": ("row", D_MODEL, CROSS_WIDTH),
    "wkv_cross": ("row", D_MODEL, 2 * CROSS_WIDTH),
    "wo_cross": ("col", CROSS_WIDTH, D_MODEL),
    "w13": ("col", D_MODEL, 2 * D_FF),
    "w2": ("row", D_FF, D_MODEL),
}
WEIGHT_PLACE = {
    "w_in": ("w_in", 0), "w_branch_a": ("w_branch_a", 0), "w_branch_b": ("w_branch_b", 0),
    "w_out": ("w_out", 0), "wq_cross": ("wq_cross", 0), "wkv_cross": ("wkv_cross", 0),
    "wo_cross": ("wo_cross", 0), "w1": ("w13", 0), "w3": ("w13", FF_SHARD), "w2": ("w2", 0),
}
BIG_WEIGHTS = tuple(WEIGHT_PLACE)
EW_BLOCK_ELEMS = 512 * 1024


def _position():
    return lax.axis_index("x"), lax.axis_index("y"), lax.axis_index("c")


def _other_chips(x, y):
    return [(1 - x, y), (x, 1 - y), (1 - x, 1 - y)]


def _half(ref, kind, h):
    r, c = ref.shape
    if kind == "col":
        return ref.at[pl.ds(h * (r // 2), r // 2), :]
    return ref.at[:, pl.ds(h * (c // 2), c // 2)]


def _shard_of(ref, kind, start, size):
    return ref.at[:, pl.ds(start, size)] if kind == "col" else ref.at[pl.ds(start, size), :]


def _rows_of(ref, r0, nrows):
    return ref if nrows is None else ref.at[pl.ds(r0, nrows), :]


def _half_shape(kind, rows, cols):
    return (rows // 2, cols) if kind == "col" else (rows, cols // 2)


def _slot_shape(spec):
    kind, rows, cols = spec
    hr, hc = _half_shape(kind, rows, cols)
    return (hr, hc // N_CHIPS) if kind == "col" else (hr // N_CHIPS, hc)


def _remote(src, dst, send_sem, recv_sem, device):
    return pltpu.make_async_remote_copy(src_ref=src, dst_ref=dst, send_sem=send_sem, recv_sem=recv_sem,
                                        device_id=device, device_id_type=MESH)


def _gather_ici_comm(fulls, jobs, specs):
    def piece(refs, job, chip, c):
        f, r0, nr = job
        kind, rows, cols = specs[f]
        stride = (cols if kind == "col" else rows) // N_CHIPS
        return _rows_of(_half(_shard_of(refs[f], kind, chip * stride, stride), kind, c), r0, nr)

    def start(refs, ss, rs):
        x, y, c = _position()
        j = 2 * x + y
        for q, job in enumerate(jobs):
            for p, (px, py) in enumerate(_other_chips(x, y)):
                _remote(piece(refs, job, j, c), piece(refs, job, j, c), ss.at[3 * q + p], rs.at[3 * q + p],
                        (px, py, c)).start()

    def finish(refs, ss, rs):
        x, y, c = _position()
        j = 2 * x + y
        for q, job in enumerate(jobs):
            for p, (px, py) in enumerate(_other_chips(x, y)):
                _remote(piece(refs, job, j, c), piece(refs, job, 2 * px + py, c), ss.at[3 * q + p],
                        rs.at[3 * q + p], (px, py, c)).wait_recv()
        for q, job in enumerate(jobs):
            for p, (px, py) in enumerate(_other_chips(x, y)):
                _remote(piece(refs, job, j, c), piece(refs, job, j, c), ss.at[3 * q + p], rs.at[3 * q + p],
                        (px, py, c)).wait_send()

    names = list(dict.fromkeys(job[0] for job in jobs))
    return _Carried({f: fulls[f] for f in names}, {}, 3 * len(jobs), start, finish)


def _gather_ring_comm(fulls, f, r0, nr, phase, specs):
    kind, _, cols = specs[f]
    assert kind == "col" and nr % 32 == 0
    stride = cols // N_CHIPS
    half = nr // 2

    def rows(refs, chip, c, lo, n):
        return _rows_of(_half(_shard_of(refs[f], kind, chip * stride, stride), kind, c), r0 + lo, n)

    def copies(refs, ss, rs):
        x, y, c = _position()
        me, nx, ny, dg = 2 * x + y, 2 * (1 - x) + y, 2 * x + (1 - y), 2 * (1 - x) + (1 - y)
        to_x, to_y = (1 - x, y, c), (x, 1 - y, c)
        if phase == "a":
            mine = rows(refs, me, c, 0, nr)
            return [(_remote(mine, mine, ss.at[0], rs.at[0], to_x), rows(refs, nx, c, 0, nr)),
                    (_remote(mine, mine, ss.at[1], rs.at[1], to_y), rows(refs, ny, c, 0, nr))]
        up, low = rows(refs, ny, c, half, half), rows(refs, nx, c, 0, half)
        return [(_remote(up, up, ss.at[0], rs.at[0], to_x), rows(refs, dg, c, half, half)),
                (_remote(low, low, ss.at[1], rs.at[1], to_y), rows(refs, dg, c, 0, half))]

    def start(refs, ss, rs):
        for cp, _ in copies(refs, ss, rs):
            cp.start()

    def finish(refs, ss, rs):
        x, y, c = _position()
        mine = copies(refs, ss, rs)
        for i, (_, landing) in enumerate(mine):
            _remote(landing, landing, ss.at[i], rs.at[i], (x, y, c)).wait_recv()
        for cp, _ in mine:
            cp.wait_send()

    return _Carried({f: fulls[f]}, {}, 2, start, finish)


def _gather_d2d_comm(fulls, jobs, specs):
    def rect(refs, job, h):
        f, r0, nr = job
        assert nr is None or specs[f][0] == "col"
        return _rows_of(_half(refs[f], specs[f][0], h), r0, nr)

    def start(refs, ss, rs):
        x, y, c = _position()
        for q, job in enumerate(jobs):
            _remote(rect(refs, job, c), rect(refs, job, c), ss.at[q], rs.at[q], (x, y, 1 - c)).start()

    def finish(refs, ss, rs):
        x, y, c = _position()
        for q, job in enumerate(jobs):
            _remote(rect(refs, job, 1 - c), rect(refs, job, 1 - c), ss.at[q], rs.at[q], (x, y, 1 - c)).wait_recv()
        for q, job in enumerate(jobs):
            _remote(rect(refs, job, c), rect(refs, job, c), ss.at[q], rs.at[q], (x, y, 1 - c)).wait_send()

    names = list(dict.fromkeys(job[0] for job in jobs))
    return _Carried({f: fulls[f] for f in names}, {}, len(jobs), start, finish)


def _pairx_comm(grads, names, specs, whole=False):
    def copies(refs, ss, rs):
        x, y, c = _position()
        src = (lambda f: refs[("g", f)]) if whole else (lambda f: _half(refs[("g", f)], specs[f][0], 1 - c))
        return [_remote(src(f), refs[("r", f)], ss.at[i], rs.at[i], (x, y, 1 - c)) for i, f in enumerate(names)]

    def start(refs, ss, rs):
        for cp in copies(refs, ss, rs):
            cp.start()

    def finish(refs, ss, rs):
        for cp in copies(refs, ss, rs):
            cp.wait_recv()
        for cp in copies(refs, ss, rs):
            cp.wait_send()

    fresh = {("r", f): jax.ShapeDtypeStruct(_half_shape(*specs[f]), BF16) for f in names}
    return _Carried({}, fresh, len(names), start, finish, reads={("g", f): grads[f] for f in names})


def _chipx_comm(pair_sums, slots, jobs, specs):
    def copies(refs, ss, rs):
        x, y, c = _position()
        out = []
        for q, (f, r0, nr) in enumerate(jobs):
            kind = specs[f][0]
            width = _slot_shape(specs[f])[1 if kind == "col" else 0]
            for p, (px, py) in enumerate(_other_chips(x, y)):
                src = _rows_of(_shard_of(refs[("p", f)], kind, (2 * px + py) * width, width), r0, nr)
                dst = _rows_of(refs[("s", f)].at[p], r0, nr)
                out.append(_remote(src, dst, ss.at[3 * q + p], rs.at[3 * q + p], (px, py, c)))
        return out

    def start(refs, ss, rs):
        for cp in copies(refs, ss, rs):
            cp.start()

    def finish(refs, ss, rs):
        for cp in copies(refs, ss, rs):
            cp.wait_recv()
        for cp in copies(refs, ss, rs):
            cp.wait_send()

    names = list(dict.fromkeys(job[0] for job in jobs))
    arrays = {("p", f): pair_sums[f] for f in names}
    arrays.update({("s", f): slots[f] for f in names})
    return _Carried(arrays, {}, 3 * len(jobs), start, finish)


def _share_comm(grads, wnames, specs, place):
    def start(refs, ss, rs):
        x, y, c = _position()
        for i, w in enumerate(wnames):
            kind = specs[place[w][0]][0]
            _remote(_half(refs[w], kind, c), _half(refs[w], kind, c), ss.at[i], rs.at[i], (x, y, 1 - c)).start()

    def finish(refs, ss, rs):
        x, y, c = _position()
        for i, w in enumerate(wnames):
            kind = specs[place[w][0]][0]
            _remote(_half(refs[w], kind, 1 - c), _half(refs[w], kind, 1 - c), ss.at[i], rs.at[i],
                    (x, y, 1 - c)).wait_recv()
        for i, w in enumerate(wnames):
            kind = specs[place[w][0]][0]
            _remote(_half(refs[w], kind, c), _half(refs[w], kind, c), ss.at[i], rs.at[i], (x, y, 1 - c)).wait_send()

    return _Carried({w: grads[w] for w in wnames}, {}, len(wnames), start, finish)


def _gather_rows(v, name="gather_small"):
    shape = v.shape

    def body(v_ref, out_ref, send_sem, recv_sem, loc_sem):
        x, y, c = _position()
        me = 4 * x + 2 * y + c
        flips = [(fx, fy, fc) for fx in (0, 1) for fy in (0, 1) for fc in (0, 1)][1:]

        def peer(fl):
            return tuple(1 - a if f else a for a, f in zip((x, y, c), fl))

        loc = pltpu.make_async_copy(v_ref, out_ref.at[me], loc_sem)
        loc.start()
        sends = []
        for i, fl in enumerate(flips):
            cp = _remote(v_ref, out_ref.at[me], send_sem.at[i], recv_sem.at[i], peer(fl))
            cp.start()
            sends.append(cp)
        for i, fl in enumerate(flips):
            px, py, pc = peer(fl)
            _remote(v_ref, out_ref.at[4 * px + 2 * py + pc], send_sem.at[i], recv_sem.at[i], peer(fl)).wait_recv()
        for cp in sends:
            cp.wait_send()
        loc.wait()

    return pl.pallas_call(
        body, name=name, in_specs=[ANY], out_specs=ANY,
        out_shape=jax.ShapeDtypeStruct((N_DEV,) + shape, F32),
        scratch_shapes=[pltpu.SemaphoreType.DMA((N_DEV - 1,)), pltpu.SemaphoreType.DMA((N_DEV - 1,)),
                        pltpu.SemaphoreType.DMA],
    )(v)


def _ew_block(rows, cols, elems=EW_BLOCK_ELEMS):
    tc = cols if cols <= 4096 else _div(cols, 2048, LANE)
    tr = _div(rows, max(16, elems // tc), 16)
    return tr, tc


def _mesh_scalars():
    x, y, c = _position()
    return jnp.stack([c, 2 * x + y]).astype(jnp.int32)


def _grid_spec(grid, in_specs, out_specs):
    return pltpu.PrefetchScalarGridSpec(num_scalar_prefetch=1, grid=grid, in_specs=in_specs, out_specs=out_specs)


def _cast_into_full(parts, fname, pos, specs, place, name, token=None):
    kind, rows, cols = specs[fname]
    ws = [w for w in place if place[w][0] == fname]
    if kind == "col":
        stride = cols // N_CHIPS
        hr = rows // 2
        tr = _div(hr, max(16, EW_BLOCK_ELEMS // stride), 16)
        nrb = hr // tr
        in_specs = [pl.BlockSpec((tr, parts[w].shape[1]), lambda i, pos_ref: (i + pos_ref[0] * nrb, 0)) for w in ws]
        out_spec = pl.BlockSpec((tr, stride), lambda i, pos_ref: (i + pos_ref[0] * nrb, pos_ref[1]))
    else:
        stride = rows // N_CHIPS
        hc = cols // 2
        tr = _div(stride, max(16, EW_BLOCK_ELEMS // hc), 16)
        nrb = stride // tr
        in_specs = [pl.BlockSpec((tr, hc), lambda i, pos_ref: (i, pos_ref[0])) for w in ws]
        out_spec = pl.BlockSpec((tr, hc), lambda i, pos_ref: (i + pos_ref[1] * nrb, pos_ref[0]))

    def kern(pos_ref, *refs):
        o_ref = refs[-1]
        for w, r in zip(ws, refs[:len(ws)]):
            off = place[w][1] if kind == "col" else 0
            o_ref[:, off:off + r.shape[1]] = r[...].astype(o_ref.dtype)

    tokens = [] if token is None else [token]
    in_specs = in_specs + [pl.BlockSpec(TOKEN_SHAPE, lambda i, pos_ref: (0, 0))] * len(tokens)
    return pl.pallas_call(
        kern, name=name, grid_spec=_grid_spec((nrb,), in_specs, out_spec),
        out_shape=jax.ShapeDtypeStruct((rows, cols), BF16),
        compiler_params=_cparams(("parallel",)),
    )(pos, *[parts[w] for w in ws], *tokens)


def _pair_sum(grad, recv, pos, spec, name, whole=False):
    kind, rows, cols = spec
    hr, hc = _half_shape(kind, rows, cols)
    tr, tc = _ew_block(hr, hc, 2 * EW_BLOCK_ELEMS)
    nrb, ncb = hr // tr, hc // tc
    blk = pl.BlockSpec((tr, tc), lambda i, jj, pos_ref: (i, jj))
    if whole:
        mine = blk
    elif kind == "col":
        mine = pl.BlockSpec((tr, tc), lambda i, jj, pos_ref: (i + pos_ref[0] * nrb, jj))
    else:
        mine = pl.BlockSpec((tr, tc), lambda i, jj, pos_ref: (i, jj + pos_ref[0] * ncb))

    def kern(pos_ref, g_ref, r_ref, o_ref, slots_ref):
        o_ref[...] = (g_ref[...].astype(F32) + r_ref[...].astype(F32)).astype(o_ref.dtype)

    return pl.pallas_call(
        kern, name=name, grid_spec=_grid_spec((nrb, ncb), [mine, blk], [blk, ANY]),
        out_shape=[jax.ShapeDtypeStruct((hr, hc), BF16),
                   jax.ShapeDtypeStruct((N_CHIPS - 1,) + _slot_shape(spec), BF16)],
        compiler_params=_cparams(("parallel", "parallel")),
    )(pos, grad, recv)


def _chip_sum(pair_sum, slots, pos, fname, shard_shapes, specs, place, name):
    kind, rows, cols = specs[fname]
    sr, sc = _slot_shape(specs[fname])
    ws = [w for w in place if place[w][0] == fname]
    n_slots = N_CHIPS - 1
    tr = _div(sr, max(16, EW_BLOCK_ELEMS // sc), 16)
    nrb = sr // tr
    slot = pl.BlockSpec((n_slots, tr, sc), lambda i, pos_ref: (0, i, 0))
    if kind == "col":
        own = pl.BlockSpec((tr, sc), lambda i, pos_ref: (i, pos_ref[1]))
        out_specs = [pl.BlockSpec((tr, shard_shapes[w][1]), lambda i, pos_ref: (i + pos_ref[0] * nrb, 0)) for w in ws]
    else:
        own = pl.BlockSpec((tr, sc), lambda i, pos_ref: (i + pos_ref[1] * nrb, 0))
        out_specs = [pl.BlockSpec((tr, sc), lambda i, pos_ref: (i, pos_ref[0])) for w in ws]

    def kern(pos_ref, own_ref, slot_ref, *out_refs):
        tot = own_ref[...].astype(F32)
        for s in range(n_slots):
            tot = tot + slot_ref[s].astype(F32)
        for w, o_ref in zip(ws, out_refs):
            off = place[w][1] if kind == "col" else 0
            o_ref[...] = tot[:, off:off + o_ref.shape[1]]

    outs = pl.pallas_call(
        kern, name=name, grid_spec=_grid_spec((nrb,), [own, slot], out_specs),
        out_shape=[jax.ShapeDtypeStruct(shard_shapes[w], F32) for w in ws],
        compiler_params=_cparams(("parallel",)),
    )(pos, pair_sum, slots)
    return dict(zip(ws, outs))


def _adam_math(w, g, m, v):
    m2 = ADAM_B1 * m + (1.0 - ADAM_B1) * g
    v2 = ADAM_B2 * v + (1.0 - ADAM_B2) * (g * g)
    m_hat = m2 / (1.0 - ADAM_B1 ** ADAM_STEP)
    v_hat = v2 / (1.0 - ADAM_B2 ** ADAM_STEP)
    delta = -ADAM_LR * (m_hat / (jnp.sqrt(v_hat) + ADAM_EPS) + ADAM_WD * w)
    return delta, m2, v2


def _adamw(w, g, m, v, name, carried=()):
    rows, cols = w.shape
    tr, tc = _ew_block(rows, cols)

    def kern(w_ref, g_ref, m_ref, v_ref, d_ref, m2_ref, v2_ref, g_out_ref):
        g_ = g_ref[...]
        d_ref[...], m2_ref[...], v2_ref[...] = _adam_math(w_ref[...], g_, m_ref[...], v_ref[...])
        g_out_ref[...] = g_

    blk = pl.BlockSpec((tr, tc), lambda i, j: (i, j))
    return _pcall(
        kern, name=name, grid=(rows // tr, cols // tc), in_specs=[blk] * 4, out_specs=[blk] * 4,
        out_shape=[jax.ShapeDtypeStruct((rows, cols), F32)] * 4, args=(w, g, m, v),
        semantics=("parallel", "parallel"), carried=carried)


SMALL_ROWS = ("ln_mix_w", "ln_cross_w", "ln_mem_w", "ln_ffn_w", "ln_final_w")
ROW_HG_NORM, ROW_LB0, ROW_LB1 = 5, 6, 7
LOSS_LANE0 = HEAD_DIM


def _pack_small(vals):
    rows = [vals[n].reshape(1, D_MODEL) for n in SMALL_ROWS]
    pad = lambda a: jnp.pad(a, ((0, 0), (0, D_MODEL - a.shape[1])))
    rows.append(pad(vals["hg_norm_w"].reshape(1, HEAD_DIM)))
    rows.append(pad(vals["hg_lower_bounds"].reshape(2, HG_WIDTH)))
    return jnp.concatenate(rows, axis=0)


def _small_update(gathered, w, m, v, name="small_update"):
    def kern(g_ref, w_ref, m_ref, v_ref, grad_ref, d_ref, m2_ref, v2_ref, loss_ref):
        tot = g_ref[0]
        for s in range(1, N_DEV):
            tot = tot + g_ref[s]
        wv = w_ref[...]
        row = lax.broadcasted_iota(jnp.int32, (8, D_MODEL), 0)
        lane = lax.broadcasted_iota(jnp.int32, (8, D_MODEL), 1)
        l0, l1 = wv[ROW_LB0:ROW_LB0 + 1], wv[ROW_LB1:ROW_LB1 + 1]
        mx = jnp.maximum(l0, l1)
        e0, e1 = jnp.exp(l0 - mx), jnp.exp(l1 - mx)
        p0 = e0 / (e0 + e1)
        dlog = tot[ROW_LB0:ROW_LB0 + 1] * p0 * (1.0 - p0)
        tot = jnp.where(row == ROW_HG_NORM, tot + tot[ROW_LB1:ROW_LB1 + 1], tot)
        grad = jnp.where(row == ROW_LB0, dlog, jnp.where(row == ROW_LB1, -dlog, tot))
        grad = jnp.where((row == ROW_HG_NORM) & (lane >= HEAD_DIM), 0.0, grad)
        grad = jnp.where((row >= ROW_LB0) & (lane >= HG_WIDTH), 0.0, grad)
        grad_ref[...] = grad
        d_ref[...], m2_ref[...], v2_ref[...] = _adam_math(wv, grad, m_ref[...], v_ref[...])
        loss_ref[...] = tot[ROW_HG_NORM:ROW_HG_NORM + 1, LOSS_LANE0:LOSS_LANE0 + LANE]

    full = pl.BlockSpec((8, D_MODEL), lambda: (0, 0))
    return pl.pallas_call(
        kern, name=name,
        in_specs=[pl.BlockSpec((N_DEV, 8, D_MODEL), lambda: (0, 0, 0)), full, full, full],
        out_specs=[full, full, full, full, pl.BlockSpec((1, LANE), lambda: (0, 0))],
        out_shape=[jax.ShapeDtypeStruct((8, D_MODEL), F32)] * 4 + [jax.ShapeDtypeStruct((1, LANE), F32)],
        compiler_params=_cparams(),
    )(gathered, w, m, v)


def _unpack_small(p, shapes):
    out = {n: p[i].reshape(shapes[n]) for i, n in enumerate(SMALL_ROWS)}
    out["hg_norm_w"] = p[ROW_HG_NORM, :HEAD_DIM].reshape(shapes["hg_norm_w"])
    out["hg_lower_bounds"] = p[ROW_LB0:ROW_LB1 + 1, :HG_WIDTH].reshape(shapes["hg_lower_bounds"])
    return out


WHOLE = lambda f: (f, 0, None)
MID_MATRICES = ("w_branch_a", "w_branch_b", "w_out", "wq_cross", "wkv_cross", "wo_cross")
MID_WEIGHTS = MID_MATRICES
W_IN_PIECES = [("w_in", r0, 512) for r0 in range(0, D_MODEL // 2, 512)]
W13_PIECES = [("w13", r0, 512) for r0 in range(0, D_MODEL // 2, 512)]
GATHER_GROUPS = [("mid", [WHOLE(f) for f in MID_MATRICES]), ("w13a", W13_PIECES[:1]), ("w13b", W13_PIECES[1:]),
                 ("w2", [WHOLE("w2")])]
OTHER_WEIGHTS = ["w1", "w3", "w2"] + list(MID_WEIGHTS)
BEFORE = {
    "hgrn_fwd": [("wait", "mid")],
    "gate_fwd": [("wait", "w13a")],
    "mm_o": [("wait", "w13b")],
    "swiglu_fwd": [("wait", "w2")],
    "mm_dh": [("wait", "rs_w2"), ("chip_sum", "w2"), ("wait", "rs_w13"), ("chip_sum", "w13"), ("wait", "rs_mid")]
    + [("chip_sum", f) for f in MID_MATRICES],
}
CARRY = {
    "hgrn_fwd": [("d2d", [WHOLE(f) for f in MID_MATRICES])],
    "gate_fwd": [("d2d", W13_PIECES[:1])],
    "mm_o": [("d2d", W13_PIECES[1:])],
    "swiglu_fwd": [("d2d", [WHOLE("w2")])],
    "swiglu_bwd": [("pairx", ["w2"])],
    "mm_dhf": [("pairx", ["w13"])],
    "attn_merge_bwd": [("pairx", list(MID_MATRICES))],
    "mm_dwin_own": [("pairx_whole", ["w_in"])],
    "mm_dh": [("share", OTHER_WEIGHTS)],
}
AFTER = {
    "swiglu_bwd": [("pair_sum", "w2"), ("start", "rs_w2", [WHOLE("w2")])],
    "mm_dhf": [("pair_sum", "w13"), ("start", "rs_w13", [WHOLE("w13")])],
    "attn_merge_bwd": [("pair_sum", f) for f in MID_MATRICES] + [("start", "rs_mid", [WHOLE(f) for f in MID_MATRICES])],
    "mm_dwin_own": [("pair_sum", "w_in"), ("start", "rs_w_in", [WHOLE("w_in")])],
}
FINISH = [
    ("adamw", OTHER_WEIGHTS), ("wait", "rs_w_in"), ("small",), ("chip_sum", "w_in"),
    ("run", ("share", ["w_in"]), "rs_sibling_share_w_in"), ("adamw", ["w_in"]),
]


class _Net:
    def __init__(self, full, pos=None, shard_shapes=None, comm=True, specs=FULL_SPECS, place=WEIGHT_PLACE):
        self.full, self.pos, self.shard_shapes, self.comm = dict(full), pos, shard_shapes, comm
        self.specs, self.place = specs, place
        self.gw, self.recv, self.psum, self.slots, self.grads = {}, {}, {}, {}, {}
        self.gw_sibling = {}
        self.pending, self.token, self.last = {}, None, None

    def _make(self, kind, arg):
        if kind == "gather":
            return _gather_ici_comm(self.full, arg, self.specs)
        if kind == "ring":
            return _gather_ring_comm(self.full, *arg, self.specs)
        if kind == "d2d":
            return _gather_d2d_comm(self.full, arg, self.specs)
        if kind == "pairx":
            return _pairx_comm(self.gw, arg, self.specs)
        if kind == "pairx_whole":
            return _pairx_comm(self.gw_sibling, arg, self.specs, whole=True)
        if kind == "chipx":
            return _chipx_comm(self.psum, self.slots, arg, self.specs)
        assert kind == "share"
        return _share_comm(self.grads, arg, self.specs, self.place)

    def _store(self, kind, res):
        if kind in ("gather", "ring", "d2d"):
            self.full.update(res)
        elif kind in ("pairx", "pairx_whole"):
            for (tag, f), a in res.items():
                (self.gw if tag == "g" else self.recv)[f] = a
        elif kind == "chipx":
            for (tag, f), a in res.items():
                (self.psum if tag == "p" else self.slots)[f] = a
        else:
            self.grads.update(res)

    def run_comm(self, item, name):
        kind, arg = item
        self._store(kind, _run_comm([self._make(kind, arg)], name)[0])

    @staticmethod
    def _others(after, items):
        own = [a for cm in items for a in cm.arrays.values()]
        return [a for a in after if a is not None and all(a is not o for o in own)]

    def start(self, groups, kind, name):
        items = [self._make(kind, jobs) for _, jobs in groups]
        after = self._others([self.last], items)
        res, sems, token = _split_start(items, name, after=after[0] if after else None)
        for (group, jobs), r, s in zip(groups, res, sems):
            self._store(kind, r)
            self.pending[group] = (kind, jobs, s)
        self.token = self.last = token

    def wait(self, group, after=()):
        kind, jobs, sems = self.pending.pop(group)
        item = self._make(kind, jobs)
        res = _split_wait([item], [sems], self._others([self.last, *after], [item]), f"wait_{group}")[0]
        self._store(kind, res)

    def step(self, step):
        if step[0] == "wait":
            self.wait(step[1])
        elif step[0] == "start":
            self.start([(step[1], step[2])], "chipx", f"start_{step[1]}")
        elif step[0] == "pair_sum":
            f = step[1]
            self.psum[f], self.slots[f] = _pair_sum(self.gw[f], self.recv[f], self.pos, self.specs[f],
                                                    f"rs_pair_sum_{f}", whole=f in self.gw_sibling)
        elif step[0] == "chip_sum":
            f = step[1]
            self.grads.update(_chip_sum(self.psum[f], self.slots[f], self.pos, f, self.shard_shapes,
                                        self.specs, self.place, f"rs_chip_sum_{f}"))
        else:
            assert step[0] == "run"
            self.run_comm(step[1], step[2])

    def call(self, fn, name, *args, grad_of=None, sibling_half=False, **kw):
        for step in (BEFORE.get(name, []) if self.comm else []):
            self.step(step)
        items = CARRY.get(name, []) if self.comm else []
        carried = [self._make(k, a) for k, a in items]
        if self.token is not None:
            carried.append(_Token(self.token))
            self.token = None
        out, res = fn(*args, name=name, carried=carried, **kw)
        if grad_of is not None:
            (self.gw_sibling if sibling_half else self.gw)[grad_of] = out
        self.last = jax.tree.leaves(out)[0]
        for (kind, _), r in zip(items, res):
            self._store(kind, r)
        for step in (AFTER.get(name, []) if self.comm else []):
            self.step(step)
        return out


def _local_step(net, x, h, mem, target, small):
    full, call = net.full, net.call
    proj = call(_mm, "mm_proj", h, full["w_in"], mode="nn", out_dtype=F32)
    att = [call(_attn_fwd, f"attn_fwd_g{g}", proj, g) for g in range(3)]
    outs, lses = [a[0] for a in att], [a[1] for a in att]
    o_att = _attn_merge_fwd(outs, lses, "attn_merge")
    oraw, o_hg, states = call(_hg_fwd, "hgrn_fwd", proj, small["hg_lower_bounds"], small["hg_norm_w"])
    ya = call(_mm, "mm_branch_a", o_att, full["w_branch_a"], mode="nn", out_dtype=F32)
    yb = call(_mm, "mm_branch_b", o_hg, full["w_branch_b"], mode="nn", out_dtype=F32)
    merged = call(_gate_fwd, "gate_fwd", proj, ya, yb)
    x1 = call(_mm, "mm_out", merged, full["w_out"], mode="nn", out_dtype=F32, res=x)

    hc = _rms_fwd(x1, small["ln_cross_w"], "rms_cross")
    mn = _rms_fwd(mem, small["ln_mem_w"], "rms_mem")
    qc = call(_mm, "mm_q", hc, full["wq_cross"], mode="nn", out_dtype=F32)
    kvc = call(_mm, "mm_kv", mn, full["wkv_cross"], mode="nn", out_dtype=F32)
    oc = call(_cross_fwd, "cross_fwd", qc, kvc)
    x2 = call(_mm, "mm_o", oc, full["wo_cross"], mode="nn", out_dtype=F32, res=x1)

    hf = _rms_fwd(x2, small["ln_ffn_w"], "rms_ffn")
    ab = call(_mm, "mm_w13", hf, full["w13"], mode="nn", out_dtype=F32)
    u = call(_swiglu_fwd, "swiglu_fwd", ab)
    x3 = call(_mm, "mm_w2", u, full["w2"], mode="nn", out_dtype=F32, res=x2)

    dx3, dg_final, loss = _loss_head(x3, small["ln_final_w"], target, "loss_head")

    gs = {"ln_final_w": dg_final}
    du = call(_mm, "mm_du", dx3, full["w2"], mode="nt", out_dtype=F32)
    call(_mm, "mm_dw2", u, dx3, mode="tn", out_dtype=BF16, grad_of="w2")
    dab = call(_swiglu_bwd, "swiglu_bwd", ab, du)
    call(_mm, "mm_dw13", hf, dab, mode="tn", out_dtype=BF16, grad_of="w13")
    dhf = call(_mm, "mm_dhf", dab, full["w13"], mode="nt", out_dtype=F32)
    dx2, gs["ln_ffn_w"] = _rms_bwd(x2, small["ln_ffn_w"], dhf, dx3, "rms_ffn_bwd")
    doc = call(_mm, "mm_doc", dx2, full["wo_cross"], mode="nt", out_dtype=BF16)
    call(_mm, "mm_dwo", oc, dx2, mode="tn", out_dtype=BF16, grad_of="wo_cross")
    dqc, dkvc = _cross_bwd(qc, kvc, doc, "cross_bwd")
    call(_mm, "mm_dwq", hc, dqc, mode="tn", out_dtype=BF16, grad_of="wq_cross")
    dhc = call(_mm, "mm_dhc", dqc, full["wq_cross"], mode="nt", out_dtype=F32)
    call(_mm, "mm_dwkv", mn, dkvc, mode="tn", out_dtype=BF16, grad_of="wkv_cross")
    dmn = call(_mm, "mm_dmn", dkvc, full["wkv_cross"], mode="nt", out_dtype=F32)
    _, gs["ln_mem_w"] = _rms_bwd(mem, small["ln_mem_w"], dmn, None, "rms_mem_bwd")
    dx1, gs["ln_cross_w"] = _rms_bwd(x1, small["ln_cross_w"], dhc, dx2, "rms_cross_bwd")
    dmerged = call(_mm, "mm_dmerged", dx1, full["w_out"], mode="nt", out_dtype=F32)
    call(_mm, "mm_dwout", merged, dx1, mode="tn", out_dtype=BF16, grad_of="w_out")
    dya, dyb, dga, dgb = call(_gate_bwd, "gate_bwd", proj, ya, yb, dmerged)
    call(_mm, "mm_dwa", o_att, dya, mode="tn", out_dtype=BF16, grad_of="w_branch_a")
    do_att = call(_mm, "mm_doatt", dya, full["w_branch_a"], mode="nt", out_dtype=F32)
    call(_mm, "mm_dwb", o_hg, dyb, mode="tn", out_dtype=BF16, grad_of="w_branch_b")
    do_hg = call(_mm, "mm_dohg", dyb, full["w_branch_b"], mode="nt", out_dtype=F32)
    dqh, dfh, dih, dgh, dlb, gs["hg_norm_w"] = call(
        _hg_bwd, "hgrn_bwd", proj, small["hg_lower_bounds"], small["hg_norm_w"], oraw, states, do_hg)
    gs["hg_lb"] = dlb
    do_gs, dl_gs = call(_attn_merge_bwd, "attn_merge_bwd", outs, lses, do_att)
    dqs, dks, dvs = zip(*[call(_attn_bwd, f"attn_bwd_g{g}", proj, g, lses[g], do_gs[g], dl_gs[g]) for g in range(3)])
    dproj = jnp.concatenate([*dqs, *dks, *dvs, dqh, dfh, dih, dgh, dga, dgb], axis=1)
    if net.comm:
        half = D_MODEL // 2
        c = lax.axis_index("c")
        h_sibling = lax.dynamic_slice_in_dim(h, (1 - c) * half, half, axis=1)
        h_own = lax.dynamic_slice_in_dim(h, c * half, half, axis=1)
        call(_mm, "mm_dwin_sibling", h_sibling, dproj, mode="tn", out_dtype=BF16, grad_of="w_in", sibling_half=True)
        call(_mm, "mm_dwin_own", h_own, dproj, mode="tn", out_dtype=BF16, grad_of="w_in")
    else:
        call(_mm, "mm_dwin", h, dproj, mode="tn", out_dtype=BF16, grad_of="w_in")
    dh = call(_mm, "mm_dh", dproj, full["w_in"], mode="nt", out_dtype=F32)
    dx, gs["ln_mix_w"] = _rms_bwd(x, small["ln_mix_w"], dh, dx1, "rms_mix_bwd")
    return loss, dx, gs


WEIGHT_ORDER = ("ln_mix_w", "w_in", "hg_norm_w", "hg_lower_bounds", "w_branch_a", "w_branch_b", "w_out",
                "ln_cross_w", "ln_mem_w", "wq_cross", "wkv_cross", "wo_cross", "ln_ffn_w", "w1", "w3", "w2",
                "ln_final_w")


def kernel(x, mem, ln_mix_w, w_in, hg_norm_w, hg_lower_bounds, w_branch_a, w_branch_b, w_out, ln_cross_w, ln_mem_w, wq_cross, wkv_cross, wo_cross, ln_ffn_w, w1, w3, w2, ln_final_w, loss_target, m_ln_mix_w, m_w_in, m_hg_norm_w, m_hg_lower_bounds, m_w_branch_a, m_w_branch_b, m_w_out, m_ln_cross_w, m_ln_mem_w, m_wq_cross, m_wkv_cross, m_wo_cross, m_ln_ffn_w, m_w1, m_w3, m_w2, m_ln_final_w, v_ln_mix_w, v_w_in, v_hg_norm_w, v_hg_lower_bounds, v_w_branch_a, v_w_branch_b, v_w_out, v_ln_cross_w, v_ln_mem_w, v_wq_cross, v_wkv_cross, v_wo_cross, v_ln_ffn_w, v_w1, v_w3, v_w2, v_ln_final_w):
    args = dict(locals())
    w = {n: args[n] for n in WEIGHT_ORDER}
    m = {n: args["m_" + n] for n in WEIGHT_ORDER}
    v = {n: args["v_" + n] for n in WEIGHT_ORDER}
    shapes = {n: w[n].shape for n in WEIGHT_ORDER}
    mat = lambda a: a.reshape(a.shape[-2:])
    shard_shapes = {n: shapes[n][-2:] for n in BIG_WEIGHTS}

    pos = _mesh_scalars()

    def cast(f, token=None):
        return _cast_into_full({n: mat(w[n]) for n in BIG_WEIGHTS if WEIGHT_PLACE[n][0] == f}, f, pos,
                               FULL_SPECS, WEIGHT_PLACE, f"cast_{f}", token)

    net = _Net({"w_in": cast("w_in")}, pos, shard_shapes)
    net.start([(f"ring_a{i}", (*job, "a")) for i, job in enumerate(W_IN_PIECES)], "ring", "gather_start_w_in")
    rest = {f: cast(f, net.token) for f in FULL_SPECS if f != "w_in"}
    net.full.update(rest)
    small = {n: w[n].reshape(1, -1) for n in SMALL_ROWS}
    small["hg_norm_w"] = w["hg_norm_w"].reshape(1, HEAD_DIM)
    small["hg_lower_bounds"] = w["hg_lower_bounds"]
    x2d = x.reshape(SEQ, D_MODEL)
    h = _rms_fwd(x2d, small["ln_mix_w"], "rms_mix")
    for i, job in enumerate(W_IN_PIECES):
        net.wait(f"ring_a{i}", after=[*rest.values(), h] if i == 0 else ())
        net.start([(f"ring_b{i}", (*job, "b"))], "ring", f"gather_pass_on_w_in{i}")
    net.start(GATHER_GROUPS, "gather", "gather_start_rest")
    for i, job in enumerate(W_IN_PIECES):
        net.wait(f"ring_b{i}")
        net.run_comm(("d2d", [job]), f"gather_hand_over_w_in{i}")
    loss, dx, gs = _local_step(net, x2d, h, mem.reshape(MEM_LEN, D_MODEL), loss_target.reshape(SEQ, D_MODEL), small)

    out_g, out_d, out_m, out_v = {}, {}, {}, {}
    net.last = dx
    for step in FINISH:
        if step[0] == "adamw":
            for n in step[1]:
                out_d[n], out_m[n], out_v[n], out_g[n] = net.call(_adamw, f"adamw_{n}", mat(w[n]), net.grads[n],
                                                                  mat(m[n]), mat(v[n]))
        elif step[0] == "wait":
            net.wait(step[1], after=list(out_d.values()))
        elif step[0] == "small":
            pad = lambda a: jnp.pad(a, ((0, 0), (0, D_MODEL - a.shape[1])))
            part = jnp.concatenate(
                [gs[n] for n in SMALL_ROWS]
                + [pad(jnp.concatenate([gs["hg_norm_w"][0], loss], axis=1)), pad(gs["hg_lb"]),
                   pad(gs["hg_norm_w"][1]) if len(gs["hg_norm_w"]) > 1 else jnp.zeros((1, D_MODEL), F32)], axis=0)
            part, net.psum["w_in"] = lax.optimization_barrier((part, net.psum["w_in"]))
            sg, sd, sm, sv, loss_tot = _small_update(_gather_rows(part), _pack_small(w), _pack_small(m),
                                                     _pack_small(v))
            for dst, packed in ((out_g, sg), (out_d, sd), (out_m, sm), (out_v, sv)):
                dst.update(_unpack_small(packed, shapes))
        else:
            net.step(step)

    result = [loss_tot[0, 0], dx.reshape(x.shape)]
    for group in (out_g, out_d, out_m, out_v):
        result += [group[n].reshape(shapes[n]) for n in WEIGHT_ORDER]
    return tuple(result)
```

```python
import math

import jax
import jax.numpy as jnp
from jax import lax
from jax.experimental import pallas as pl
from jax.experimental.pallas import tpu as pltpu

F32 = jnp.float32
BF16 = jnp.bfloat16
MESH = pl.DeviceIdType.MESH

D_MODEL = 2048
SEQ = 2048
HEAD_DIM = 128
MEM_LEN = 256
ATT_GROUPS = ((128, 1), (512, 4), (2048, 16))
ATT_HEADS = 4
ATT_WIDTH = 3 * ATT_HEADS * HEAD_DIM
ATT_OUT = ATT_HEADS * HEAD_DIM
ATT_BLOCK = 128
HG_HEADS = 8
HG_WIDTH = HG_HEADS * HEAD_DIM
HG_CHUNK = 64
IN_WIDTH = 3 * ATT_WIDTH + 4 * HG_WIDTH + 2 * D_MODEL
CROSS_HEADS = 4
CROSS_WIDTH = CROSS_HEADS * HEAD_DIM
D_FF = 5632
RMS_EPS = 1e-6
ADAM_LR = 0.001
ADAM_B1 = 0.9
ADAM_B2 = 0.999
ADAM_EPS = 1e-08
ADAM_WD = 0.01
ADAM_STEP = 10
N_CHIPS = 4
N_DEV = 8

VMEM_LIMIT_BYTES = 56 * 1024 * 1024
LANE = 128
MXU_WIDTH = 256
MM_TILE_CAP = 1536
TRANSPOSE_CHUNK = 512
ANY = pl.BlockSpec(memory_space=pl.ANY)


def _cparams(sem=None):
    return pltpu.CompilerParams(dimension_semantics=sem, vmem_limit_bytes=VMEM_LIMIT_BYTES)


def _div(n, cap, mult):
    best = None
    for d in range(mult, min(n, cap) + 1, mult):
        if n % d == 0:
            best = d
    assert best is not None, (n, cap, mult)
    return best


def _sigmoid(x):
    return 1.0 / (1.0 + jnp.exp(-x))


def _dot(a, b):
    return jnp.dot(a.astype(BF16), b.astype(BF16), preferred_element_type=F32)


def _dot_nt(a, b):
    return lax.dot_general(a.astype(BF16), b.astype(BF16), (((1,), (1,)), ((), ())),
                           preferred_element_type=F32)


def _dot_tn(a, b):
    return jnp.dot(a.astype(F32).T.astype(BF16), b.astype(BF16), preferred_element_type=F32)


def _dot_exact(a, b):
    return jnp.dot(a, b, precision=lax.Precision.HIGHEST, preferred_element_type=F32)


class _Carried:
    def __init__(self, arrays, fresh, n_sems, start, finish, mid=None, reads=None):
        self.arrays, self.fresh, self.n_sems, self.reads = arrays, fresh, n_sems, reads or {}
        self.start, self.mid, self.finish = start, mid, finish


class _Token:
    def __init__(self, array):
        self.array = array


TOKEN_SHAPE = (8, LANE)


def _carried_layout(carried):
    akeys = list(dict.fromkeys(k for cm in carried for k in cm.arrays))
    fkeys = [(ci, k) for ci, cm in enumerate(carried) for k in cm.fresh]
    arrays = [next(cm.arrays[k] for cm in carried if k in cm.arrays) for k in akeys]
    shapes = [jax.ShapeDtypeStruct(a.shape, a.dtype) for a in arrays] + [carried[ci].fresh[k] for ci, k in fkeys]
    sems = []
    for cm in carried:
        sems += [pltpu.SemaphoreType.DMA((cm.n_sems,)), pltpu.SemaphoreType.DMA((cm.n_sems,))]
    return akeys, fkeys, arrays, shapes, sems


def _carried_reads(carried):
    rkeys = list(dict.fromkeys(k for cm in carried for k in cm.reads))
    return rkeys, [next(cm.reads[k] for cm in carried if k in cm.reads) for k in rkeys]


def _carried_results(carried, akeys, fkeys, outs, rkeys=(), read_refs=()):
    shared = dict(zip(akeys, outs[:len(akeys)]))
    shared.update(zip(rkeys, read_refs))
    res = [{k: shared[k] for k in list(cm.arrays) + [r for r in cm.reads if r in shared]} for cm in carried]
    for (ci, k), o in zip(fkeys, outs[len(akeys):]):
        res[ci][k] = o
    return res


def _pcall(kern, *, name, grid, in_specs, out_specs, out_shape, args, scratch_shapes=(), semantics=None,
           carried=()):
    tokens = [c.array for c in carried if isinstance(c, _Token)]
    carried = [c for c in carried if not isinstance(c, _Token)]
    single = not isinstance(out_shape, (list, tuple))
    out_specs = [out_specs] if single else list(out_specs)
    out_shape = [out_shape] if single else list(out_shape)
    n_real, n_out, n_scr = len(in_specs), len(out_shape), len(scratch_shapes)
    in_specs = list(in_specs) + [pl.BlockSpec(TOKEN_SHAPE, lambda *_: (0, 0))] * len(tokens)
    args = list(args) + tokens
    n_in = len(in_specs)
    if not carried:
        def plain(*refs):
            kern(*refs[:n_real], *refs[n_in:])

        outs = pl.pallas_call(plain if tokens else kern, name=name, grid=grid, in_specs=in_specs,
                              out_specs=out_specs, out_shape=out_shape, scratch_shapes=list(scratch_shapes),
                              compiler_params=_cparams(semantics))(*args)
        return (outs[0] if single else list(outs)), []
    akeys, fkeys, arrays, shapes, sems = _carried_layout(carried)
    rkeys, reads = _carried_reads(carried)
    n_a, n_f, n_r = len(akeys), len(fkeys), len(rkeys)
    total = math.prod(grid)
    mid_step = min(total - 1, (17 * total) // 20)

    def wrapped(*refs):
        ins = refs[:n_real]
        r0 = n_in + n_a
        o0 = r0 + n_r
        outs = refs[o0:o0 + n_out]
        a0 = o0 + n_out
        s0 = a0 + n_a + n_f
        per = _carried_results(carried, akeys, fkeys, refs[a0:s0], rkeys, refs[r0:o0])
        scratch = refs[s0:s0 + n_scr]
        sem = refs[s0 + n_scr:]
        step = 0
        for d, g in enumerate(grid):
            step = step * g + pl.program_id(d)

        @pl.when(step == 0)
        def _():
            for ci, cm in enumerate(carried):
                cm.start(per[ci], sem[2 * ci], sem[2 * ci + 1])

        kern(*ins, *outs, *scratch)

        @pl.when(step == mid_step)
        def _():
            for ci, cm in enumerate(carried):
                if cm.mid is not None:
                    cm.mid(per[ci], sem[2 * ci], sem[2 * ci + 1])

        @pl.when(step == total - 1)
        def _():
            for ci, cm in enumerate(carried):
                cm.finish(per[ci], sem[2 * ci], sem[2 * ci + 1])

    outs = pl.pallas_call(
        wrapped, name=name, grid=grid,
        in_specs=list(in_specs) + [ANY] * (n_a + n_r), out_specs=out_specs + [ANY] * (n_a + n_f),
        out_shape=out_shape + shapes,
        input_output_aliases={n_in + i: n_out + i for i in range(n_a)},
        scratch_shapes=list(scratch_shapes) + sems,
        compiler_params=_cparams(("arbitrary",) * len(grid)),
    )(*args, *arrays, *reads)
    res = _carried_results(carried, akeys, fkeys, outs[n_out:])
    return (outs[0] if single else list(outs[:n_out])), res


def _run_comm(carried, name):
    carried = list(carried)
    akeys, fkeys, arrays, shapes, sems = _carried_layout(carried)
    rkeys, reads = _carried_reads(carried)
    n_a, n_f, n_r = len(akeys), len(fkeys), len(rkeys)

    def body(*refs):
        o0 = n_a + n_r
        per = _carried_results(carried, akeys, fkeys, refs[o0:o0 + n_a + n_f], rkeys, refs[n_a:o0])
        sem = refs[o0 + n_a + n_f:]
        for hook in ("start", "mid", "finish"):
            for ci, cm in enumerate(carried):
                fn = getattr(cm, hook)
                if fn is not None:
                    fn(per[ci], sem[2 * ci], sem[2 * ci + 1])

    outs = pl.pallas_call(
        body, name=name, in_specs=[ANY] * (n_a + n_r), out_specs=[ANY] * (n_a + n_f), out_shape=shapes,
        input_output_aliases={i: i for i in range(n_a)}, scratch_shapes=sems,
    )(*arrays, *reads)
    return _carried_results(carried, akeys, fkeys, outs)


HBM_SPEC = pl.BlockSpec(memory_space=pltpu.HBM)
SEM_SPEC = pl.BlockSpec(memory_space=pltpu.SEMAPHORE)
SPLIT_EFFECT = pltpu.SideEffectType.DATAFLOW_SIDE_EFFECTING


def _in_hbm(a):
    return pltpu.with_memory_space_constraint(a, pltpu.HBM)


def _split_start(items, name, after=None):
    items = list(items)
    akeys, fkeys, arrays, shapes, sems = _carried_layout(items)
    assert not fkeys
    n_a, n_s = len(akeys), len(sems)
    n_in = n_a + (after is not None)

    def body(*refs):
        per = _carried_results(items, akeys, [], refs[n_in:n_in + n_a])
        sem = refs[n_in + n_a:n_in + n_a + n_s]
        for ci, cm in enumerate(items):
            cm.start(per[ci], sem[2 * ci], sem[2 * ci + 1])
        token = refs[n_in + n_a + n_s]
        token[...] = jnp.zeros_like(token)

    outs = pl.pallas_call(
        body, name=name, in_specs=[HBM_SPEC] * n_a + [ANY] * (after is not None),
        out_specs=[HBM_SPEC] * n_a + [SEM_SPEC] * n_s + [pl.BlockSpec(memory_space=pltpu.VMEM)],
        out_shape=[pltpu.HBM(s.shape, s.dtype) for s in shapes] + sems + [jax.ShapeDtypeStruct(TOKEN_SHAPE, F32)],
        input_output_aliases={i: i for i in range(n_a)},
        compiler_params=pltpu.CompilerParams(has_side_effects=SPLIT_EFFECT),
    )(*[_in_hbm(a) for a in arrays], *([after] if after is not None else []))
    res = _carried_results(items, akeys, [], outs[:n_a])
    sem_out = outs[n_a:n_a + n_s]
    return res, [(sem_out[2 * ci], sem_out[2 * ci + 1]) for ci in range(len(items))], outs[-1]


def _split_wait(items, sems, after, name):
    items = list(items)
    after = list(after) if isinstance(after, (list, tuple)) else [after]
    akeys, fkeys, arrays, shapes, _ = _carried_layout(items)
    n_a, n_s = len(akeys), 2 * len(items)

    def body(*refs):
        per = _carried_results(items, akeys, [], refs[n_a + n_s + len(after):])
        sem = refs[n_a:n_a + n_s]
        for ci, cm in enumerate(items):
            cm.finish(per[ci], sem[2 * ci], sem[2 * ci + 1])

    outs = pl.pallas_call(
        body, name=name, in_specs=[HBM_SPEC] * n_a + [SEM_SPEC] * n_s + [ANY] * len(after),
        out_specs=[HBM_SPEC] * n_a, out_shape=[pltpu.HBM(s.shape, s.dtype) for s in shapes],
        input_output_aliases={i: i for i in range(n_a)},
        compiler_params=pltpu.CompilerParams(has_side_effects=SPLIT_EFFECT),
    )(*arrays, *[s for pair in sems for s in pair], *after)
    return _carried_results(items, akeys, [], outs)


def _mm(a, b, *, mode, out_dtype, name, res=None, carried=(), fused=None):
    if mode == "nn":
        (m, k), (k2, n) = a.shape, b.shape
    elif mode == "nt":
        (m, k), (n, k2) = a.shape, b.shape
    else:
        (k, m), (k2, n) = a.shape, b.shape
    assert k == k2, (name, a.shape, b.shape)
    tm = _div(m, MM_TILE_CAP, LANE)
    tn = _div(n, MM_TILE_CAP, MXU_WIDTH) if n % MXU_WIDTH == 0 else 0
    if tn < 1024:
        tn = _div(n, MM_TILE_CAP, LANE)
    out_shape = jax.ShapeDtypeStruct((m, n), out_dtype)

    if fused is not None:
        assert mode in ("nn", "nt") and res is None and k <= 2048
        tm, tn = fused["tile"]
        dot = _dot if mode == "nn" else _dot_nt
        n_x = len(fused["ins"])

        def kern_fused(*refs):
            part = dot(refs[0][...], refs[1][...])
            outs = fused["post"](part, *[r[...] for r in refs[2:2 + n_x]])
            for o_ref, val in zip(refs[2 + n_x:], outs):
                o_ref[...] = val.astype(o_ref.dtype)

        tile_spec = lambda shape: pl.BlockSpec(shape, lambda j, i: (i, j))
        return _pcall(
            kern_fused, name=name, grid=(n // tn, m // tm),
            in_specs=[pl.BlockSpec((tm, k), lambda j, i: (i, 0)),
                      pl.BlockSpec((k, tn), lambda j, i: (0, j)) if mode == "nn"
                      else pl.BlockSpec((tn, k), lambda j, i: (j, 0))] + [tile_spec(s) for _, s in fused["ins"]],
            out_specs=[tile_spec(s) for _, s in fused["outs"]], out_shape=[o for o, _ in fused["outs"]],
            args=(a, b, *[x for x, _ in fused["ins"]]), semantics=("parallel", "parallel"), carried=carried)

    if mode == "tn":
        assert res is None

        def kern_tn(a_ref, b_ref, o_ref, at_ref):
            @pl.when(pl.program_id(1) == 0)
            def _():
                step = min(TRANSPOSE_CHUNK, k)
                for c0 in range(0, k, step):
                    at_ref[:, c0:c0 + step] = a_ref[c0:c0 + step, :].astype(F32).T.astype(BF16)

            o_ref[...] = jnp.dot(at_ref[...], b_ref[...].astype(BF16),
                                 preferred_element_type=F32).astype(o_ref.dtype)

        return _pcall(
            kern_tn, name=name, grid=(m // tm, n // tn),
            in_specs=[pl.BlockSpec((k, tm), lambda i, j: (0, i)),
                      pl.BlockSpec((k, tn), lambda i, j: (0, j))],
            out_specs=pl.BlockSpec((tm, tn), lambda i, j: (i, j)),
            out_shape=out_shape, args=(a, b),
            scratch_shapes=[pltpu.VMEM((tm, k), BF16)],
            semantics=("parallel", "arbitrary"), carried=carried)

    tk = k if k <= 2048 else _div(k, 3072, LANE)
    nk = k // tk
    a_spec = pl.BlockSpec((tm, tk), lambda i, j, kk: (i, kk))
    if mode == "nn":
        b_spec = pl.BlockSpec((tk, tn), lambda i, j, kk: (kk, j))
        dot = _dot
    else:
        b_spec = pl.BlockSpec((tn, tk), lambda i, j, kk: (j, kk))
        dot = _dot_nt
    o_spec = pl.BlockSpec((tm, tn), lambda i, j, kk: (i, j))
    in_specs = [a_spec, b_spec]
    args = [a, b]
    if res is not None:
        in_specs.append(o_spec)
        args.append(res)
    has_res = res is not None

    def kern(*refs):
        a_ref, b_ref = refs[0], refs[1]
        r_ref = refs[2] if has_res else None
        o_ref = refs[3] if has_res else refs[2]
        part = dot(a_ref[...], b_ref[...])
        if nk == 1:
            if has_res:
                part = part + r_ref[...]
            o_ref[...] = part.astype(o_ref.dtype)
            return
        acc_ref = refs[-1]
        kk = pl.program_id(2)

        @pl.when(kk == 0)
        def _():
            acc_ref[...] = part

        @pl.when(kk > 0)
        def _():
            acc_ref[...] += part

        @pl.when(kk == nk - 1)
        def _():
            tot = acc_ref[...]
            if has_res:
                tot = tot + r_ref[...]
            o_ref[...] = tot.astype(o_ref.dtype)

    return _pcall(
        kern, name=name, grid=(m // tm, n // tn, nk),
        in_specs=in_specs, out_specs=o_spec, out_shape=out_shape, args=args,
        scratch_shapes=[pltpu.VMEM((tm, tn), F32)] if nk > 1 else [],
        semantics=("parallel", "parallel", "arbitrary"), carried=carried)


ROW_BLOCK = 512


def _rms_fwd(x, g, name):
    t, d = x.shape
    tr = min(ROW_BLOCK, t)

    def kern(x_ref, g_ref, o_ref):
        xf = x_ref[...]
        r = lax.rsqrt(jnp.mean(xf * xf, axis=-1, keepdims=True) + RMS_EPS)
        o_ref[...] = (xf * r * g_ref[...]).astype(o_ref.dtype)

    return pl.pallas_call(
        kern, name=name, grid=(t // tr,),
        in_specs=[pl.BlockSpec((tr, d), lambda i: (i, 0)), pl.BlockSpec((1, d), lambda i: (0, 0))],
        out_specs=pl.BlockSpec((tr, d), lambda i: (i, 0)),
        out_shape=jax.ShapeDtypeStruct((t, d), BF16),
        compiler_params=_cparams(("parallel",)),
    )(x, g)


def _rms_bwd(x, g, dh, res, name):
    t, d = x.shape
    tr = min(ROW_BLOCK, t)
    has_res = res is not None

    def kern(*refs):
        x_ref, g_ref, dh_ref = refs[:3]
        r_ref = refs[3] if has_res else None
        dx_ref, dg_ref = refs[-2], refs[-1]
        xf = x_ref[...]
        r = lax.rsqrt(jnp.mean(xf * xf, axis=-1, keepdims=True) + RMS_EPS)
        xn = xf * r
        dh_ = dh_ref[...]
        dhg = dh_ * g_ref[...]
        dx = r * (dhg - xn * jnp.mean(dhg * xn, axis=-1, keepdims=True))
        if has_res:
            dx = dx + r_ref[...]
        dx_ref[...] = dx
        part = jnp.sum(dh_ * xn, axis=0, keepdims=True)

        @pl.when(pl.program_id(0) == 0)
        def _():
            dg_ref[...] = part

        @pl.when(pl.program_id(0) > 0)
        def _():
            dg_ref[...] += part

    row = pl.BlockSpec((tr, d), lambda i: (i, 0))
    vec = pl.BlockSpec((1, d), lambda i: (0, 0))
    in_specs = [row, vec, row] + ([row] if has_res else [])
    args = [x, g, dh] + ([res] if has_res else [])
    return pl.pallas_call(
        kern, name=name, grid=(t // tr,), in_specs=in_specs, out_specs=[row, vec],
        out_shape=[jax.ShapeDtypeStruct((t, d), F32), jax.ShapeDtypeStruct((1, d), F32)],
        compiler_params=_cparams(("arbitrary",)),
    )(*args)


def _loss_head(x3, g, target, name):
    t, d = x3.shape
    tr = ROW_BLOCK

    def kern(x_ref, g_ref, t_ref, dx_ref, dg_ref, loss_ref):
        xf = x_ref[...]
        r = lax.rsqrt(jnp.mean(xf * xf, axis=-1, keepdims=True) + RMS_EPS)
        xn = xf * r
        gg = g_ref[...]
        err = xn * gg - t_ref[...]
        lpart = 0.5 * jnp.sum(jnp.mean(err * err, axis=-1, keepdims=True), axis=0, keepdims=True)
        dy = err * (1.0 / d)
        dyg = dy * gg
        dx_ref[...] = r * (dyg - xn * jnp.mean(dyg * xn, axis=-1, keepdims=True))
        gpart = jnp.sum(dy * xn, axis=0, keepdims=True)
        lrow = jnp.broadcast_to(lpart, (1, LANE))

        @pl.when(pl.program_id(0) == 0)
        def _():
            dg_ref[...] = gpart
            loss_ref[...] = lrow

        @pl.when(pl.program_id(0) > 0)
        def _():
            dg_ref[...] += gpart
            loss_ref[...] += lrow

    row = pl.BlockSpec((tr, d), lambda i: (i, 0))
    vec = pl.BlockSpec((1, d), lambda i: (0, 0))
    return pl.pallas_call(
        kern, name=name, grid=(t // tr,), in_specs=[row, vec, row],
        out_specs=[row, vec, pl.BlockSpec((1, LANE), lambda i: (0, 0))],
        out_shape=[jax.ShapeDtypeStruct((t, d), F32), jax.ShapeDtypeStruct((1, d), F32),
                   jax.ShapeDtypeStruct((1, LANE), F32)],
        compiler_params=_cparams(("arbitrary",)),
    )(x3, g, target)


ATT_SCALE = HEAD_DIM ** -0.5
Q_BLOCK0, K_BLOCK0, V_BLOCK0 = 0, ATT_WIDTH // HEAD_DIM, 2 * ATT_WIDTH // HEAD_DIM


def _residue_rows(dil, r, n):
    if dil == 1:
        return pl.ds(n * ATT_BLOCK, ATT_BLOCK)
    return pl.ds(n * ATT_BLOCK * dil + r, ATT_BLOCK, stride=dil)


def _band_mask(with_prev):
    width = 2 * ATT_BLOCK if with_prev else ATT_BLOCK
    iq = lax.broadcasted_iota(jnp.int32, (ATT_BLOCK, width), 0)
    ik = lax.broadcasted_iota(jnp.int32, (ATT_BLOCK, width), 1)
    if not with_prev:
        return ik <= iq
    return ((ik < ATT_BLOCK) & (iq <= ik)) | ((ik >= ATT_BLOCK) & ((ik - ATT_BLOCK) <= iq))


def _band_keys(ref, dil, r, n):
    own = ref[_residue_rows(dil, r, n), :]
    if n == 0:
        return own
    return jnp.concatenate([ref[_residue_rows(dil, r, n - 1), :], own], axis=0)


def _attn_col_spec(base, grp):
    return pl.BlockSpec((SEQ, HEAD_DIM), lambda h: (0, base + grp * ATT_HEADS + h))


def _attn_fwd(proj, grp, name, carried=()):
    _, dil = ATT_GROUPS[grp]
    nb = SEQ // dil // ATT_BLOCK

    def kern(q_ref, k_ref, v_ref, o_ref, lse_ref):
        for r in range(dil):
            for n in range(nb):
                rows = _residue_rows(dil, r, n)
                s = _dot_nt(q_ref[rows, :], _band_keys(k_ref, dil, r, n)) * ATT_SCALE
                s = jnp.where(_band_mask(n > 0), s, -jnp.inf)
                m = jnp.max(s, axis=-1, keepdims=True)
                p = jnp.exp(s - m)
                l = jnp.sum(p, axis=-1, keepdims=True)
                o_ref[rows, :] = _dot(p / l, _band_keys(v_ref, dil, r, n))
                lse_ref[rows, :] = jnp.broadcast_to(m + jnp.log(l), (ATT_BLOCK, HEAD_DIM))

    out_spec = pl.BlockSpec((SEQ, HEAD_DIM), lambda h: (0, h))
    return _pcall(
        kern, name=name, grid=(ATT_HEADS,),
        in_specs=[_attn_col_spec(Q_BLOCK0, grp), _attn_col_spec(K_BLOCK0, grp), _attn_col_spec(V_BLOCK0, grp)],
        out_specs=[out_spec, out_spec],
        out_shape=[jax.ShapeDtypeStruct((SEQ, ATT_OUT), F32)] * 2, args=(proj, proj, proj),
        semantics=("parallel",), carried=carried)


def _attn_weights(l0, l1, l2):
    mx = jnp.maximum(jnp.maximum(l0, l1), l2)
    e0, e1, e2 = jnp.exp(l0 - mx), jnp.exp(l1 - mx), jnp.exp(l2 - mx)
    den = e0 + e1 + e2
    return e0 / den, e1 / den, e2 / den


def _attn_merge_fwd(outs, lses, name):
    tr = ROW_BLOCK

    def kern(o0, o1, o2, l0, l1, l2, out_ref):
        a0, a1, a2 = _attn_weights(l0[...], l1[...], l2[...])
        out_ref[...] = (a0 * o0[...] + a1 * o1[...] + a2 * o2[...]).astype(out_ref.dtype)

    spec = pl.BlockSpec((tr, ATT_OUT), lambda i: (i, 0))
    return pl.pallas_call(
        kern, name=name, grid=(SEQ // tr,), in_specs=[spec] * 6, out_specs=spec,
        out_shape=jax.ShapeDtypeStruct((SEQ, ATT_OUT), BF16),
        compiler_params=_cparams(("parallel",)),
    )(*outs, *lses)


def _attn_merge_bwd(outs, lses, do_att, name, carried=()):
    tr = ROW_BLOCK

    def kern(o0, o1, o2, l0, l1, l2, do_ref, d0, d1, d2, t0, t1, t2):
        alphas = _attn_weights(l0[...], l1[...], l2[...])
        do = do_ref[...]
        o_att = alphas[0] * o0[...] + alphas[1] * o1[...] + alphas[2] * o2[...]
        prod = do * o_att
        parts = []
        for h in range(ATT_HEADS):
            sl = slice(h * HEAD_DIM, (h + 1) * HEAD_DIM)
            tot = jnp.sum(prod[:, sl], axis=-1, keepdims=True)
            parts.append(jnp.broadcast_to(tot, (tr, HEAD_DIM)))
        dd = jnp.concatenate(parts, axis=1)
        for a, d_ref, t_ref in zip(alphas, (d0, d1, d2), (t0, t1, t2)):
            d_ref[...] = a * do
            t_ref[...] = -a * dd

    spec = pl.BlockSpec((tr, ATT_OUT), lambda i: (i, 0))
    res, cres = _pcall(
        kern, name=name, grid=(SEQ // tr,), in_specs=[spec] * 7, out_specs=[spec] * 6,
        out_shape=[jax.ShapeDtypeStruct((SEQ, ATT_OUT), F32)] * 6, args=(*outs, *lses, do_att),
        semantics=("parallel",), carried=carried)
    return (res[:3], res[3:]), cres


def _attn_bwd(proj, grp, lse, do_g, dl_g, name, carried=()):
    _, dil = ATT_GROUPS[grp]
    nb = SEQ // dil // ATT_BLOCK

    def kern(q_ref, k_ref, v_ref, do_ref, lse_ref, dl_ref, dq_ref, dk_ref, dv_ref, dq_acc, dk_acc, dv_acc):
        dk_acc[...] = jnp.zeros_like(dk_acc)
        dv_acc[...] = jnp.zeros_like(dv_acc)
        for r in range(dil):
            for n in range(nb):
                rows = _residue_rows(dil, r, n)
                q, do = q_ref[rows, :], do_ref[rows, :]
                kk, vv = _band_keys(k_ref, dil, r, n), _band_keys(v_ref, dil, r, n)
                s = _dot_nt(q, kk) * ATT_SCALE
                p = jnp.where(_band_mask(n > 0), jnp.exp(s - lse_ref[rows, :][:, :1]), 0.0)
                ds = p * (_dot_nt(do, vv) + dl_ref[rows, :][:, :1])
                dq_acc[rows, :] = _dot(ds, kk) * ATT_SCALE
                dk = _dot_tn(ds, q) * ATT_SCALE
                dv = _dot_tn(p, do)
                if n > 0:
                    prev = _residue_rows(dil, r, n - 1)
                    dk_acc[prev, :] += dk[:ATT_BLOCK]
                    dv_acc[prev, :] += dv[:ATT_BLOCK]
                    dk, dv = dk[ATT_BLOCK:], dv[ATT_BLOCK:]
                dk_acc[rows, :] += dk
                dv_acc[rows, :] += dv
        dq_ref[...] = dq_acc[...].astype(dq_ref.dtype)
        dk_ref[...] = dk_acc[...].astype(dk_ref.dtype)
        dv_ref[...] = dv_acc[...].astype(dv_ref.dtype)

    spec = pl.BlockSpec((SEQ, HEAD_DIM), lambda h: (0, h))
    return _pcall(
        kern, name=name, grid=(ATT_HEADS,),
        in_specs=[_attn_col_spec(Q_BLOCK0, grp), _attn_col_spec(K_BLOCK0, grp), _attn_col_spec(V_BLOCK0, grp),
                  spec, spec, spec],
        out_specs=[spec] * 3,
        out_shape=[jax.ShapeDtypeStruct((SEQ, ATT_OUT), BF16)] * 3, args=(proj, proj, proj, do_g, lse, dl_g),
        scratch_shapes=[pltpu.VMEM((SEQ, HEAD_DIM), F32)] * 3,
        semantics=("parallel",), carried=carried)


HG_HEADS_PER_STEP = 8
HG_BLOCK_W = 4 * HEAD_DIM
HG_BLOCKS = HG_HEADS_PER_STEP * HEAD_DIM // HG_BLOCK_W
HG_STEP_W = HG_HEADS_PER_STEP * HEAD_DIM
HG_Q_BLK = (3 * ATT_WIDTH) // HG_BLOCK_W
HG_N_CHUNKS = SEQ // HG_CHUNK
HG_MID = HG_CHUNK // 2


def _lower_bound(lb_ref, sl):
    l0, l1 = lb_ref[0:1, sl], lb_ref[1:2, sl]
    mx = jnp.maximum(l0, l1)
    e0, e1 = jnp.exp(l0 - mx), jnp.exp(l1 - mx)
    return e0 / (e0 + e1)


def _tri(lower):
    i = lax.broadcasted_iota(jnp.int32, (HG_CHUNK, HG_CHUNK), 0)
    j = lax.broadcasted_iota(jnp.int32, (HG_CHUNK, HG_CHUNK), 1)
    return (i >= j) if lower else (i <= j)


def _head_mean(x):
    parts = []
    for hd in range(x.shape[1] // HEAD_DIM):
        m = jnp.mean(x[:, hd * HEAD_DIM:(hd + 1) * HEAD_DIM], axis=-1, keepdims=True)
        parts.append(jnp.broadcast_to(m, (x.shape[0], HEAD_DIM)))
    return jnp.concatenate(parts, axis=1)


def _hg_chunk_terms(qh, fh, lb):
    sig = _sigmoid(fh)
    f = lb + (1.0 - lb) * sig
    k = 1.0 - f
    b = _dot_exact(_tri(True).astype(F32), jnp.log(f))
    bl = b[HG_CHUNK - 1:HG_CHUNK, :]
    br = b[HG_MID:HG_MID + 1, :]
    sq = _sigmoid(qh)
    q = qh * sq
    return dict(sig=sig, f=f, k=k, b=b, bl=bl, br=br, sq=sq, q=q,
                e1=jnp.exp(bl - b), e2=jnp.exp(b), e3=jnp.exp(b - br), e4=jnp.exp(br - b))


def _hg_fwd(proj, lbw, normw, name, carried=()):
    def in_blks(off):
        return [pl.BlockSpec((HG_CHUNK, HG_BLOCK_W), lambda hp, n, b=b: (n, HG_Q_BLK + off + hp * HG_BLOCKS + b))
                for b in range(HG_BLOCKS)]

    def kern(*refs):
        q_refs, f_refs, i_refs, g_refs = (refs[k * HG_BLOCKS:(k + 1) * HG_BLOCKS] for k in range(4))
        lb_ref, nw_ref, oraw_ref, ohg_ref, st_ref, state = refs[4 * HG_BLOCKS:]

        @pl.when(pl.program_id(1) == 0)
        def _():
            state[...] = jnp.zeros_like(state)

        causal = _tri(True)
        wide = lambda rs: jnp.concatenate([r[...] for r in rs], axis=1)
        t = _hg_chunk_terms(wide(q_refs), wide(f_refs), _lower_bound(lb_ref, slice(None)))
        v, gh = wide(i_refs), wide(g_refs)
        kd, qb, qr, kr = t["k"] * t["e1"], t["q"] * t["e2"], t["q"] * t["e3"], t["k"] * t["e4"]
        decay = jnp.exp(t["bl"])
        outs = []
        for hd in range(HG_HEADS_PER_STEP):
            sl = slice(hd * HEAD_DIM, (hd + 1) * HEAD_DIM)
            st = state[hd]
            st_ref[0, hd] = st
            a = jnp.where(causal, _dot_nt(qr[:, sl], kr[:, sl]), 0.0)
            outs.append(_dot_nt(qb[:, sl], st) + _dot(a, v[:, sl]))
            state[hd] = st * decay[:, sl] + _dot_tn(v[:, sl], kd[:, sl])
        o = jnp.concatenate(outs, axis=1)
        oraw_ref[...] = o
        r = lax.rsqrt(_head_mean(o * o) + RMS_EPS)
        nw = jnp.tile(nw_ref[...], (1, HG_HEADS_PER_STEP))
        ohg_ref[...] = (o * r * nw * (gh * _sigmoid(gh))).astype(ohg_ref.dtype)

    out_blk = pl.BlockSpec((HG_CHUNK, HG_STEP_W), lambda hp, n: (n, hp))
    return _pcall(
        kern, name=name, grid=(HG_HEADS // HG_HEADS_PER_STEP, HG_N_CHUNKS),
        in_specs=[*in_blks(0), *in_blks(2), *in_blks(4), *in_blks(6),
                  pl.BlockSpec((2, HG_STEP_W), lambda hp, n: (0, hp)),
                  pl.BlockSpec((1, HEAD_DIM), lambda hp, n: (0, 0))],
        out_specs=[out_blk, out_blk,
                   pl.BlockSpec((1, HG_HEADS_PER_STEP, HEAD_DIM, HEAD_DIM), lambda hp, n: (n, hp, 0, 0))],
        out_shape=[jax.ShapeDtypeStruct((SEQ, HG_WIDTH), F32), jax.ShapeDtypeStruct((SEQ, HG_WIDTH), BF16),
                   jax.ShapeDtypeStruct((HG_N_CHUNKS, HG_HEADS, HEAD_DIM, HEAD_DIM), F32)],
        args=(*[proj] * (4 * HG_BLOCKS), lbw, normw),
        scratch_shapes=[pltpu.VMEM((HG_HEADS_PER_STEP, HEAD_DIM, HEAD_DIM), F32)],
        semantics=("parallel", "arbitrary"), carried=carried)


def _hg_bwd(proj, lbw, normw, oraw, states, do_hg, name, carried=()):
    last = HG_N_CHUNKS - 1

    def in_blks(off):
        return [pl.BlockSpec((HG_CHUNK, HG_BLOCK_W),
                             lambda hp, n, b=b: (last - n, HG_Q_BLK + off + hp * HG_BLOCKS + b))
                for b in range(HG_BLOCKS)]

    blk = pl.BlockSpec((HG_CHUNK, HG_STEP_W), lambda hp, n: (last - n, hp))

    def kern(*refs):
        q_refs, f_refs, i_refs, g_refs = (refs[k * HG_BLOCKS:(k + 1) * HG_BLOCKS] for k in range(4))
        (lb_ref, nw_ref, oraw_ref, st_ref, do_ref, dq_ref, df_ref, di_ref, dg_ref, dlb_ref, dnw_ref,
         dstate) = refs[4 * HG_BLOCKS:]
        first = pl.program_id(1) == 0

        @pl.when(first)
        def _():
            dstate[...] = jnp.zeros_like(dstate)

        causal = _tri(True)
        wide = lambda rs: jnp.concatenate([r[...] for r in rs], axis=1)
        cat = lambda parts: jnp.concatenate(parts, axis=1)
        qh, fh, v, gh = wide(q_refs), wide(f_refs), wide(i_refs), wide(g_refs)
        o, dout = oraw_ref[...], do_ref[...]
        nw = jnp.tile(nw_ref[...], (1, HG_HEADS_PER_STEP))
        sgg = _sigmoid(gh)
        r = lax.rsqrt(_head_mean(o * o) + RMS_EPS)
        xn = o * r
        dg_ref[...] = (dout * xn * nw * (sgg * (1.0 + gh * (1.0 - sgg)))).astype(dg_ref.dtype)
        don = dout * (gh * sgg)
        dnw_wide = jnp.sum(don * xn, axis=0, keepdims=True)
        dnw_tot = dnw_wide[:, :HEAD_DIM]
        for hd in range(1, HG_HEADS_PER_STEP):
            dnw_tot = dnw_tot + dnw_wide[:, hd * HEAD_DIM:(hd + 1) * HEAD_DIM]
        tt = don * nw
        do = r * (tt - xn * _head_mean(tt * xn))
        lb = _lower_bound(lb_ref, slice(None))
        t = _hg_chunk_terms(qh, fh, lb)
        k, q = t["k"], t["q"]
        kd, qb, qr, kr = k * t["e1"], q * t["e2"], q * t["e3"], k * t["e4"]
        decay = jnp.exp(t["bl"])
        dqb, dqr, dkr, dkd, dv, ddecay = [], [], [], [], [], []
        for hd in range(HG_HEADS_PER_STEP):
            sl = slice(hd * HEAD_DIM, (hd + 1) * HEAD_DIM)
            st = st_ref[0, hd]
            dstn = dstate[hd]
            a = jnp.where(causal, _dot_nt(qr[:, sl], kr[:, sl]), 0.0)
            da = jnp.where(causal, _dot_nt(do[:, sl], v[:, sl]), 0.0)
            dqb.append(_dot(do[:, sl], st))
            dv.append(_dot_tn(a, do[:, sl]) + _dot_nt(kd[:, sl], dstn))
            dqr.append(_dot(da, kr[:, sl]))
            dkr.append(_dot_tn(da, qr[:, sl]))
            dkd.append(_dot(v[:, sl], dstn))
            ddecay.append(jnp.sum(dstn * st, axis=0, keepdims=True))
            dstate[hd] = dstn * decay[:, sl] + _dot_tn(do[:, sl], qb[:, sl])
        dqb, dqr, dkr, dkd, dv, ddecay = cat(dqb), cat(dqr), cat(dkr), cat(dkd), cat(dv), cat(ddecay)
        dq = dqb * t["e2"] + dqr * t["e3"]
        dk = dkd * t["e1"] + dkr * t["e4"]
        db = dqb * qb + dqr * qr - dkr * kr - dkd * kd
        dbl = jnp.sum(dkd * kd, axis=0, keepdims=True) + ddecay * decay
        dbr = jnp.sum(dkr * kr - dqr * qr, axis=0, keepdims=True)
        rows = lax.broadcasted_iota(jnp.int32, db.shape, 0)
        dlf = _dot_exact(_tri(False).astype(F32), db) + dbl + jnp.where(rows <= HG_MID, dbr, 0.0)
        df = dlf / t["f"] - dk
        sig, sq = t["sig"], t["sq"]
        df_ref[...] = (df * (1.0 - lb) * sig * (1.0 - sig)).astype(df_ref.dtype)
        dlb_row = jnp.sum(df * (1.0 - sig), axis=0, keepdims=True)
        dq_ref[...] = (dq * (sq * (1.0 + qh * (1.0 - sq)))).astype(dq_ref.dtype)
        di_ref[...] = dv.astype(di_ref.dtype)
        dnw_blk = jnp.broadcast_to(dnw_tot, (8, HEAD_DIM))

        @pl.when(first)
        def _():
            dlb_ref[...] = dlb_row
            dnw_ref[...] = dnw_blk

        @pl.when(jnp.logical_not(first))
        def _():
            dlb_ref[...] += dlb_row
            dnw_ref[...] += dnw_blk

    n_hp = HG_HEADS // HG_HEADS_PER_STEP
    outs, cres = _pcall(
        kern, name=name, grid=(n_hp, HG_N_CHUNKS),
        in_specs=[*in_blks(0), *in_blks(2), *in_blks(4), *in_blks(6),
                  pl.BlockSpec((2, HG_STEP_W), lambda hp, n: (0, hp)),
                  pl.BlockSpec((1, HEAD_DIM), lambda hp, n: (0, 0)),
                  blk,
                  pl.BlockSpec((1, HG_HEADS_PER_STEP, HEAD_DIM, HEAD_DIM), lambda hp, n: (last - n, hp, 0, 0)),
                  blk],
        out_specs=[blk, blk, blk, blk,
                   pl.BlockSpec((1, HG_STEP_W), lambda hp, n: (0, hp)),
                   pl.BlockSpec((8, HEAD_DIM), lambda hp, n: (hp, 0))],
        out_shape=[jax.ShapeDtypeStruct((SEQ, HG_WIDTH), BF16)] * 4
        + [jax.ShapeDtypeStruct((1, HG_WIDTH), F32), jax.ShapeDtypeStruct((8 * n_hp, HEAD_DIM), F32)],
        args=(*[proj] * (4 * HG_BLOCKS), lbw, normw, oraw, states, do_hg),
        scratch_shapes=[pltpu.VMEM((HG_HEADS_PER_STEP, HEAD_DIM, HEAD_DIM), F32)],
        semantics=("parallel", "arbitrary"), carried=carried)
    dqh, dfh, dih, dgh, dlb, dnw = outs
    return (dqh, dfh, dih, dgh, dlb, [dnw[8 * i:8 * i + 1] for i in range(n_hp)]), cres


GATE_BLOCK_W = 512
GATE_A_BLK = (3 * ATT_WIDTH + 4 * HG_WIDTH) // GATE_BLOCK_W
GATE_B_BLK = GATE_A_BLK + D_MODEL // GATE_BLOCK_W


def _gate_specs():
    tr = ROW_BLOCK
    blk = pl.BlockSpec((tr, GATE_BLOCK_W), lambda i, j: (i, j))
    ga = pl.BlockSpec((tr, GATE_BLOCK_W), lambda i, j: (i, GATE_A_BLK + j))
    gb = pl.BlockSpec((tr, GATE_BLOCK_W), lambda i, j: (i, GATE_B_BLK + j))
    return (SEQ // tr, D_MODEL // GATE_BLOCK_W), blk, ga, gb


def _gate_fwd(proj, ya, yb, name, carried=()):
    grid, blk, ga, gb = _gate_specs()

    def kern(ga_ref, gb_ref, ya_ref, yb_ref, o_ref):
        o_ref[...] = (_sigmoid(ga_ref[...]) * ya_ref[...] + _sigmoid(gb_ref[...]) * yb_ref[...]).astype(o_ref.dtype)

    return _pcall(
        kern, name=name, grid=grid, in_specs=[ga, gb, blk, blk], out_specs=blk,
        out_shape=jax.ShapeDtypeStruct((SEQ, D_MODEL), BF16), args=(proj, proj, ya, yb),
        semantics=("parallel", "parallel"), carried=carried)


def _gate_bwd(proj, ya, yb, dmerged, name, carried=()):
    grid, blk, ga, gb = _gate_specs()

    def kern(ga_ref, gb_ref, ya_ref, yb_ref, dm_ref, dya_ref, dyb_ref, dga_ref, dgb_ref):
        dm = dm_ref[...]
        sa, sb = _sigmoid(ga_ref[...]), _sigmoid(gb_ref[...])
        dya_ref[...] = (dm * sa).astype(dya_ref.dtype)
        dyb_ref[...] = (dm * sb).astype(dyb_ref.dtype)
        dga_ref[...] = (dm * ya_ref[...] * sa * (1.0 - sa)).astype(dga_ref.dtype)
        dgb_ref[...] = (dm * yb_ref[...] * sb * (1.0 - sb)).astype(dgb_ref.dtype)

    return _pcall(
        kern, name=name, grid=grid, in_specs=[ga, gb, blk, blk, blk], out_specs=[blk] * 4,
        out_shape=[jax.ShapeDtypeStruct((SEQ, D_MODEL), BF16)] * 4, args=(proj, proj, ya, yb, dmerged),
        semantics=("parallel", "parallel"), carried=carried)


FF_SHARD = D_FF // N_CHIPS


FF_TILE_ROWS = 512


def _swiglu_tile(ab):
    a, b = ab[:, :FF_SHARD], ab[:, FF_SHARD:]
    return a * _sigmoid(a) * b


def _swiglu_grad_tile(du, ab):
    a, b = ab[:, :FF_SHARD], ab[:, FF_SHARD:]
    sg = _sigmoid(a)
    return jnp.concatenate([du * b * (sg * (1.0 + a * (1.0 - sg))), du * (a * sg)], axis=1)


def _ff_up(hf, w13, name, carried=()):
    wide, narrow = (FF_TILE_ROWS, 2 * FF_SHARD), (FF_TILE_ROWS, FF_SHARD)
    return _mm(hf, w13, mode="nn", out_dtype=F32, name=name, carried=carried, fused=dict(
        tile=wide, ins=[],
        outs=[(jax.ShapeDtypeStruct((SEQ, 2 * D_FF), F32), wide), (jax.ShapeDtypeStruct((SEQ, D_FF), BF16), narrow)],
        post=lambda p: (p, _swiglu_tile(p))))


def _ff_down_bwd(dx3, w2, ab, name, carried=()):
    wide, narrow = (FF_TILE_ROWS, 2 * FF_SHARD), (FF_TILE_ROWS, FF_SHARD)
    out, res = _mm(dx3, w2, mode="nt", out_dtype=BF16, name=name, carried=carried, fused=dict(
        tile=narrow, ins=[(ab, wide)], outs=[(jax.ShapeDtypeStruct((SEQ, 2 * D_FF), BF16), wide)],
        post=lambda du, ab_: (_swiglu_grad_tile(du, ab_),)))
    return out[0], res


CROSS_ROWS = 512


def _cross_fwd(qc, kvc, name, carried=()):
    def kern(q_ref, k_ref, v_ref, o_ref):
        s = _dot_nt(q_ref[...], k_ref[...]) * ATT_SCALE
        m = jnp.max(s, axis=-1, keepdims=True)
        e = jnp.exp(s - m)
        p = e / jnp.sum(e, axis=-1, keepdims=True)
        o_ref[...] = _dot(p, v_ref[...]).astype(o_ref.dtype)

    qblk = pl.BlockSpec((CROSS_ROWS, HEAD_DIM), lambda h, i: (i, h))
    return _pcall(
        kern, name=name, grid=(CROSS_HEADS, SEQ // CROSS_ROWS),
        in_specs=[qblk, pl.BlockSpec((MEM_LEN, HEAD_DIM), lambda h, i: (0, h)),
                  pl.BlockSpec((MEM_LEN, HEAD_DIM), lambda h, i: (0, CROSS_HEADS + h))],
        out_specs=qblk, out_shape=jax.ShapeDtypeStruct((SEQ, CROSS_WIDTH), BF16), args=(qc, kvc, kvc),
        semantics=("parallel", "parallel"), carried=carried)


def _cross_bwd(qc, kvc, doc, name):
    def kern(q_ref, k_ref, v_ref, do_ref, dq_ref, dk_ref, dv_ref):
        q, k, v, do = q_ref[...], k_ref[...], v_ref[...], do_ref[...]
        s = _dot_nt(q, k) * ATT_SCALE
        m = jnp.max(s, axis=-1, keepdims=True)
        e = jnp.exp(s - m)
        p = e / jnp.sum(e, axis=-1, keepdims=True)
        dp = _dot_nt(do, v)
        ds = p * (dp - jnp.sum(dp * p, axis=-1, keepdims=True))
        dq_ref[...] = (_dot(ds, k) * ATT_SCALE).astype(dq_ref.dtype)
        dk = _dot_tn(ds, q) * ATT_SCALE
        dv = _dot_tn(p, do)

        @pl.when(pl.program_id(1) == 0)
        def _():
            dk_ref[...] = dk
            dv_ref[...] = dv

        @pl.when(pl.program_id(1) > 0)
        def _():
            dk_ref[...] += dk
            dv_ref[...] += dv

    qblk = pl.BlockSpec((CROSS_ROWS, HEAD_DIM), lambda h, i: (i, h))
    kblk = pl.BlockSpec((MEM_LEN, HEAD_DIM), lambda h, i: (0, h))
    dq, dk, dv = pl.pallas_call(
        kern, name=name, grid=(CROSS_HEADS, SEQ // CROSS_ROWS),
        in_specs=[qblk, kblk, pl.BlockSpec((MEM_LEN, HEAD_DIM), lambda h, i: (0, CROSS_HEADS + h)), qblk],
        out_specs=[qblk, kblk, kblk],
        out_shape=[jax.ShapeDtypeStruct((SEQ, CROSS_WIDTH), BF16),
                   jax.ShapeDtypeStruct((MEM_LEN, CROSS_WIDTH), F32),
                   jax.ShapeDtypeStruct((MEM_LEN, CROSS_WIDTH), F32)],
        compiler_params=_cparams(("parallel", "arbitrary")),
    )(qc, kvc, kvc, doc)
    return dq, jnp.concatenate([dk, dv], axis=1)


FULL_SPECS = {
    "w_in": ("col", D_MODEL, IN_WIDTH),
    "w_branch_a": ("col", ATT_OUT, D_MODEL),
    "w_branch_b": ("col", HG_WIDTH, D_MODEL),
    "w_out": ("row", D_MODEL, D_MODEL),
    "wq_cross": ("row", D_MODEL, CROSS_WIDTH),
    "wkv_cross": ("row", D_MODEL, 2 * CROSS_WIDTH),
    "wo_cross": ("col", CROSS_WIDTH, D_MODEL),
    "w13": ("col", D_MODEL, 2 * D_FF),
    "w2": ("row", D_FF, D_MODEL),
}
WEIGHT_PLACE = {
    "w_in": ("w_in", 0), "w_branch_a": ("w_branch_a", 0), "w_branch_b": ("w_branch_b", 0),
    "w_out": ("w_out", 0), "wq_cross": ("wq_cross", 0), "wkv_cross": ("wkv_cross", 0),
    "wo_cross": ("wo_cross", 0), "w1": ("w13", 0), "w3": ("w13", FF_SHARD), "w2": ("w2", 0),
}
BIG_WEIGHTS = tuple(WEIGHT_PLACE)
EW_BLOCK_ELEMS = 512 * 1024


def _position():
    return lax.axis_index("x"), lax.axis_index("y"), lax.axis_index("c")


def _other_chips(x, y):
    return [(1 - x, y), (x, 1 - y), (1 - x, 1 - y)]


def _half(ref, kind, h):
    r, c = ref.shape
    if kind == "col":
        return ref.at[pl.ds(h * (r // 2), r // 2), :]
    return ref.at[:, pl.ds(h * (c // 2), c // 2)]


def _shard_of(ref, kind, start, size):
    return ref.at[:, pl.ds(start, size)] if kind == "col" else ref.at[pl.ds(start, size), :]


def _rows_of(ref, r0, nrows):
    return ref if nrows is None else ref.at[pl.ds(r0, nrows), :]


def _half_shape(kind, rows, cols):
    return (rows // 2, cols) if kind == "col" else (rows, cols // 2)


def _slot_shape(spec):
    kind, rows, cols = spec
    hr, hc = _half_shape(kind, rows, cols)
    return (hr, hc // N_CHIPS) if kind == "col" else (hr // N_CHIPS, hc)


def _remote(src, dst, send_sem, recv_sem, device):
    return pltpu.make_async_remote_copy(src_ref=src, dst_ref=dst, send_sem=send_sem, recv_sem=recv_sem,
                                        device_id=device, device_id_type=MESH)


def _gather_ici_comm(fulls, jobs, specs):
    def piece(refs, job, chip, c):
        f, r0, nr = job
        kind, rows, cols = specs[f]
        stride = (cols if kind == "col" else rows) // N_CHIPS
        return _rows_of(_half(_shard_of(refs[f], kind, chip * stride, stride), kind, c), r0, nr)

    def start(refs, ss, rs):
        x, y, c = _position()
        j = 2 * x + y
        for q, job in enumerate(jobs):
            for p, (px, py) in enumerate(_other_chips(x, y)):
                _remote(piece(refs, job, j, c), piece(refs, job, j, c), ss.at[3 * q + p], rs.at[3 * q + p],
                        (px, py, c)).start()

    def finish(refs, ss, rs):
        x, y, c = _position()
        j = 2 * x + y
        for q, job in enumerate(jobs):
            for p, (px, py) in enumerate(_other_chips(x, y)):
                _remote(piece(refs, job, j, c), piece(refs, job, 2 * px + py, c), ss.at[3 * q + p],
                        rs.at[3 * q + p], (px, py, c)).wait_recv()
        for q, job in enumerate(jobs):
            for p, (px, py) in enumerate(_other_chips(x, y)):
                _remote(piece(refs, job, j, c), piece(refs, job, j, c), ss.at[3 * q + p], rs.at[3 * q + p],
                        (px, py, c)).wait_send()

    names = list(dict.fromkeys(job[0] for job in jobs))
    return _Carried({f: fulls[f] for f in names}, {}, 3 * len(jobs), start, finish)


def _gather_ring_comm(fulls, f, r0, nr, phase, specs):
    kind, _, cols = specs[f]
    assert kind == "col" and nr % 32 == 0
    stride = cols // N_CHIPS
    half = nr // 2

    def rows(refs, chip, c, lo, n):
        return _rows_of(_half(_shard_of(refs[f], kind, chip * stride, stride), kind, c), r0 + lo, n)

    def copies(refs, ss, rs):
        x, y, c = _position()
        me, nx, ny, dg = 2 * x + y, 2 * (1 - x) + y, 2 * x + (1 - y), 2 * (1 - x) + (1 - y)
        to_x, to_y = (1 - x, y, c), (x, 1 - y, c)
        if phase == "a":
            mine = rows(refs, me, c, 0, nr)
            return [(_remote(mine, mine, ss.at[0], rs.at[0], to_x), rows(refs, nx, c, 0, nr)),
                    (_remote(mine, mine, ss.at[1], rs.at[1], to_y), rows(refs, ny, c, 0, nr))]
        up, low = rows(refs, ny, c, half, half), rows(refs, nx, c, 0, half)
        return [(_remote(up, up, ss.at[0], rs.at[0], to_x), rows(refs, dg, c, half, half)),
                (_remote(low, low, ss.at[1], rs.at[1], to_y), rows(refs, dg, c, 0, half))]

    def start(refs, ss, rs):
        for cp, _ in copies(refs, ss, rs):
            cp.start()

    def finish(refs, ss, rs):
        x, y, c = _position()
        mine = copies(refs, ss, rs)
        for i, (_, landing) in enumerate(mine):
            _remote(landing, landing, ss.at[i], rs.at[i], (x, y, c)).wait_recv()
        for cp, _ in mine:
            cp.wait_send()

    return _Carried({f: fulls[f]}, {}, 2, start, finish)


def _gather_d2d_comm(fulls, jobs, specs):
    def rect(refs, job, h):
        f, r0, nr = job
        assert nr is None or specs[f][0] == "col"
        return _rows_of(_half(refs[f], specs[f][0], h), r0, nr)

    def start(refs, ss, rs):
        x, y, c = _position()
        for q, job in enumerate(jobs):
            _remote(rect(refs, job, c), rect(refs, job, c), ss.at[q], rs.at[q], (x, y, 1 - c)).start()

    def finish(refs, ss, rs):
        x, y, c = _position()
        for q, job in enumerate(jobs):
            _remote(rect(refs, job, 1 - c), rect(refs, job, 1 - c), ss.at[q], rs.at[q], (x, y, 1 - c)).wait_recv()
        for q, job in enumerate(jobs):
            _remote(rect(refs, job, c), rect(refs, job, c), ss.at[q], rs.at[q], (x, y, 1 - c)).wait_send()

    names = list(dict.fromkeys(job[0] for job in jobs))
    return _Carried({f: fulls[f] for f in names}, {}, len(jobs), start, finish)


def _pairx_comm(grads, names, specs, whole=False):
    def copies(refs, ss, rs):
        x, y, c = _position()
        src = (lambda f: refs[("g", f)]) if whole else (lambda f: _half(refs[("g", f)], specs[f][0], 1 - c))
        return [_remote(src(f), refs[("r", f)], ss.at[i], rs.at[i], (x, y, 1 - c)) for i, f in enumerate(names)]

    def start(refs, ss, rs):
        for cp in copies(refs, ss, rs):
            cp.start()

    def finish(refs, ss, rs):
        for cp in copies(refs, ss, rs):
            cp.wait_recv()
        for cp in copies(refs, ss, rs):
            cp.wait_send()

    fresh = {("r", f): jax.ShapeDtypeStruct(_half_shape(*specs[f]), BF16) for f in names}
    return _Carried({}, fresh, len(names), start, finish, reads={("g", f): grads[f] for f in names})


def _chipx_comm(pair_sums, slots, jobs, specs):
    def copies(refs, ss, rs):
        x, y, c = _position()
        out = []
        for q, (f, r0, nr) in enumerate(jobs):
            kind = specs[f][0]
            width = _slot_shape(specs[f])[1 if kind == "col" else 0]
            for p, (px, py) in enumerate(_other_chips(x, y)):
                src = _rows_of(_shard_of(refs[("p", f)], kind, (2 * px + py) * width, width), r0, nr)
                dst = _rows_of(refs[("s", f)].at[p], r0, nr)
                out.append(_remote(src, dst, ss.at[3 * q + p], rs.at[3 * q + p], (px, py, c)))
        return out

    def start(refs, ss, rs):
        for cp in copies(refs, ss, rs):
            cp.start()

    def finish(refs, ss, rs):
        for cp in copies(refs, ss, rs):
            cp.wait_recv()
        for cp in copies(refs, ss, rs):
            cp.wait_send()

    names = list(dict.fromkeys(job[0] for job in jobs))
    arrays = {("p", f): pair_sums[f] for f in names}
    arrays.update({("s", f): slots[f] for f in names})
    return _Carried(arrays, {}, 3 * len(jobs), start, finish)


def _share_comm(grads, wnames, specs, place):
    def start(refs, ss, rs):
        x, y, c = _position()
        for i, w in enumerate(wnames):
            kind = specs[place[w][0]][0]
            _remote(_half(refs[w], kind, c), _half(refs[w], kind, c), ss.at[i], rs.at[i], (x, y, 1 - c)).start()

    def finish(refs, ss, rs):
        x, y, c = _position()
        for i, w in enumerate(wnames):
            kind = specs[place[w][0]][0]
            _remote(_half(refs[w], kind, 1 - c), _half(refs[w], kind, 1 - c), ss.at[i], rs.at[i],
                    (x, y, 1 - c)).wait_recv()
        for i, w in enumerate(wnames):
            kind = specs[place[w][0]][0]
            _remote(_half(refs[w], kind, c), _half(refs[w], kind, c), ss.at[i], rs.at[i], (x, y, 1 - c)).wait_send()

    return _Carried({w: grads[w] for w in wnames}, {}, len(wnames), start, finish)


def _gather_rows(v, name="gather_small"):
    shape = v.shape

    def body(v_ref, out_ref, send_sem, recv_sem, loc_sem):
        x, y, c = _position()
        me = 4 * x + 2 * y + c
        flips = [(fx, fy, fc) for fx in (0, 1) for fy in (0, 1) for fc in (0, 1)][1:]

        def peer(fl):
            return tuple(1 - a if f else a for a, f in zip((x, y, c), fl))

        loc = pltpu.make_async_copy(v_ref, out_ref.at[me], loc_sem)
        loc.start()
        sends = []
        for i, fl in enumerate(flips):
            cp = _remote(v_ref, out_ref.at[me], send_sem.at[i], recv_sem.at[i], peer(fl))
            cp.start()
            sends.append(cp)
        for i, fl in enumerate(flips):
            px, py, pc = peer(fl)
            _remote(v_ref, out_ref.at[4 * px + 2 * py + pc], send_sem.at[i], recv_sem.at[i], peer(fl)).wait_recv()
        for cp in sends:
            cp.wait_send()
        loc.wait()

    return pl.pallas_call(
        body, name=name, in_specs=[ANY], out_specs=ANY,
        out_shape=jax.ShapeDtypeStruct((N_DEV,) + shape, F32),
        scratch_shapes=[pltpu.SemaphoreType.DMA((N_DEV - 1,)), pltpu.SemaphoreType.DMA((N_DEV - 1,)),
                        pltpu.SemaphoreType.DMA],
    )(v)


def _ew_block(rows, cols, elems=EW_BLOCK_ELEMS):
    tc = cols if cols <= 4096 else _div(cols, 2048, LANE)
    tr = _div(rows, max(16, elems // tc), 16)
    return tr, tc


def _mesh_scalars():
    x, y, c = _position()
    return jnp.stack([c, 2 * x + y]).astype(jnp.int32)


def _grid_spec(grid, in_specs, out_specs):
    return pltpu.PrefetchScalarGridSpec(num_scalar_prefetch=1, grid=grid, in_specs=in_specs, out_specs=out_specs)


def _cast_into_full(parts, fname, pos, specs, place, name, token=None):
    kind, rows, cols = specs[fname]
    ws = [w for w in place if place[w][0] == fname]
    if kind == "col":
        stride = cols // N_CHIPS
        hr = rows // 2
        tr = _div(hr, max(16, EW_BLOCK_ELEMS // stride), 16)
        nrb = hr // tr
        in_specs = [pl.BlockSpec((tr, parts[w].shape[1]), lambda i, pos_ref: (i + pos_ref[0] * nrb, 0)) for w in ws]
        out_spec = pl.BlockSpec((tr, stride), lambda i, pos_ref: (i + pos_ref[0] * nrb, pos_ref[1]))
    else:
        stride = rows // N_CHIPS
        hc = cols // 2
        tr = _div(stride, max(16, EW_BLOCK_ELEMS // hc), 16)
        nrb = stride // tr
        in_specs = [pl.BlockSpec((tr, hc), lambda i, pos_ref: (i, pos_ref[0])) for w in ws]
        out_spec = pl.BlockSpec((tr, hc), lambda i, pos_ref: (i + pos_ref[1] * nrb, pos_ref[0]))

    def kern(pos_ref, *refs):
        o_ref = refs[-1]
        for w, r in zip(ws, refs[:len(ws)]):
            off = place[w][1] if kind == "col" else 0
            o_ref[:, off:off + r.shape[1]] = r[...].astype(o_ref.dtype)

    tokens = [] if token is None else [token]
    in_specs = in_specs + [pl.BlockSpec(TOKEN_SHAPE, lambda i, pos_ref: (0, 0))] * len(tokens)
    return pl.pallas_call(
        kern, name=name, grid_spec=_grid_spec((nrb,), in_specs, out_spec),
        out_shape=jax.ShapeDtypeStruct((rows, cols), BF16),
        compiler_params=_cparams(("parallel",)),
    )(pos, *[parts[w] for w in ws], *tokens)


def _pair_sum(grad, recv, pos, spec, name, whole=False):
    kind, rows, cols = spec
    hr, hc = _half_shape(kind, rows, cols)
    tr, tc = _ew_block(hr, hc, 2 * EW_BLOCK_ELEMS)
    nrb, ncb = hr // tr, hc // tc
    blk = pl.BlockSpec((tr, tc), lambda i, jj, pos_ref: (i, jj))
    if whole:
        mine = blk
    elif kind == "col":
        mine = pl.BlockSpec((tr, tc), lambda i, jj, pos_ref: (i + pos_ref[0] * nrb, jj))
    else:
        mine = pl.BlockSpec((tr, tc), lambda i, jj, pos_ref: (i, jj + pos_ref[0] * ncb))

    def kern(pos_ref, g_ref, r_ref, o_ref, slots_ref):
        o_ref[...] = (g_ref[...].astype(F32) + r_ref[...].astype(F32)).astype(o_ref.dtype)

    return pl.pallas_call(
        kern, name=name, grid_spec=_grid_spec((nrb, ncb), [mine, blk], [blk, ANY]),
        out_shape=[jax.ShapeDtypeStruct((hr, hc), BF16),
                   jax.ShapeDtypeStruct((N_CHIPS - 1,) + _slot_shape(spec), BF16)],
        compiler_params=_cparams(("parallel", "parallel")),
    )(pos, grad, recv)


def _chip_sum(pair_sum, slots, pos, fname, shard_shapes, specs, place, name):
    kind, rows, cols = specs[fname]
    sr, sc = _slot_shape(specs[fname])
    ws = [w for w in place if place[w][0] == fname]
    n_slots = N_CHIPS - 1
    tr = _div(sr, max(16, EW_BLOCK_ELEMS // sc), 16)
    nrb = sr // tr
    slot = pl.BlockSpec((n_slots, tr, sc), lambda i, pos_ref: (0, i, 0))
    if kind == "col":
        own = pl.BlockSpec((tr, sc), lambda i, pos_ref: (i, pos_ref[1]))
        out_specs = [pl.BlockSpec((tr, shard_shapes[w][1]), lambda i, pos_ref: (i + pos_ref[0] * nrb, 0)) for w in ws]
    else:
        own = pl.BlockSpec((tr, sc), lambda i, pos_ref: (i + pos_ref[1] * nrb, 0))
        out_specs = [pl.BlockSpec((tr, sc), lambda i, pos_ref: (i, pos_ref[0])) for w in ws]

    def kern(pos_ref, own_ref, slot_ref, *out_refs):
        tot = own_ref[...].astype(F32)
        for s in range(n_slots):
            tot = tot + slot_ref[s].astype(F32)
        for w, o_ref in zip(ws, out_refs):
            off = place[w][1] if kind == "col" else 0
            o_ref[...] = tot[:, off:off + o_ref.shape[1]]

    outs = pl.pallas_call(
        kern, name=name, grid_spec=_grid_spec((nrb,), [own, slot], out_specs),
        out_shape=[jax.ShapeDtypeStruct(shard_shapes[w], F32) for w in ws],
        compiler_params=_cparams(("parallel",)),
    )(pos, pair_sum, slots)
    return dict(zip(ws, outs))


def _adam_math(w, g, m, v):
    m2 = ADAM_B1 * m + (1.0 - ADAM_B1) * g
    v2 = ADAM_B2 * v + (1.0 - ADAM_B2) * (g * g)
    m_hat = m2 / (1.0 - ADAM_B1 ** ADAM_STEP)
    v_hat = v2 / (1.0 - ADAM_B2 ** ADAM_STEP)
    delta = -ADAM_LR * (m_hat / (jnp.sqrt(v_hat) + ADAM_EPS) + ADAM_WD * w)
    return delta, m2, v2


def _adamw(w, g, m, v, name, carried=()):
    rows, cols = w.shape
    tr, tc = _ew_block(rows, cols)

    def kern(w_ref, g_ref, m_ref, v_ref, d_ref, m2_ref, v2_ref, g_out_ref):
        g_ = g_ref[...]
        d_ref[...], m2_ref[...], v2_ref[...] = _adam_math(w_ref[...], g_, m_ref[...], v_ref[...])
        g_out_ref[...] = g_

    blk = pl.BlockSpec((tr, tc), lambda i, j: (i, j))
    return _pcall(
        kern, name=name, grid=(rows // tr, cols // tc), in_specs=[blk] * 4, out_specs=[blk] * 4,
        out_shape=[jax.ShapeDtypeStruct((rows, cols), F32)] * 4, args=(w, g, m, v),
        semantics=("parallel", "parallel"), carried=carried)


SMALL_ROWS = ("ln_mix_w", "ln_cross_w", "ln_mem_w", "ln_ffn_w", "ln_final_w")
ROW_HG_NORM, ROW_LB0, ROW_LB1 = 5, 6, 7
LOSS_LANE0 = HEAD_DIM


def _pack_small(vals):
    rows = [vals[n].reshape(1, D_MODEL) for n in SMALL_ROWS]
    pad = lambda a: jnp.pad(a, ((0, 0), (0, D_MODEL - a.shape[1])))
    rows.append(pad(vals["hg_norm_w"].reshape(1, HEAD_DIM)))
    rows.append(pad(vals["hg_lower_bounds"].reshape(2, HG_WIDTH)))
    return jnp.concatenate(rows, axis=0)


def _small_update(gathered, w, m, v, name="small_update"):
    def kern(g_ref, w_ref, m_ref, v_ref, grad_ref, d_ref, m2_ref, v2_ref, loss_ref):
        tot = g_ref[0]
        for s in range(1, N_DEV):
            tot = tot + g_ref[s]
        wv = w_ref[...]
        row = lax.broadcasted_iota(jnp.int32, (8, D_MODEL), 0)
        lane = lax.broadcasted_iota(jnp.int32, (8, D_MODEL), 1)
        l0, l1 = wv[ROW_LB0:ROW_LB0 + 1], wv[ROW_LB1:ROW_LB1 + 1]
        mx = jnp.maximum(l0, l1)
        e0, e1 = jnp.exp(l0 - mx), jnp.exp(l1 - mx)
        p0 = e0 / (e0 + e1)
        dlog = tot[ROW_LB0:ROW_LB0 + 1] * p0 * (1.0 - p0)
        tot = jnp.where(row == ROW_HG_NORM, tot + tot[ROW_LB1:ROW_LB1 + 1], tot)
        grad = jnp.where(row == ROW_LB0, dlog, jnp.where(row == ROW_LB1, -dlog, tot))
        grad = jnp.where((row == ROW_HG_NORM) & (lane >= HEAD_DIM), 0.0, grad)
        grad = jnp.where((row >= ROW_LB0) & (lane >= HG_WIDTH), 0.0, grad)
        grad_ref[...] = grad
        d_ref[...], m2_ref[...], v2_ref[...] = _adam_math(wv, grad, m_ref[...], v_ref[...])
        loss_ref[...] = tot[ROW_HG_NORM:ROW_HG_NORM + 1, LOSS_LANE0:LOSS_LANE0 + LANE]

    full = pl.BlockSpec((8, D_MODEL), lambda: (0, 0))
    return pl.pallas_call(
        kern, name=name,
        in_specs=[pl.BlockSpec((N_DEV, 8, D_MODEL), lambda: (0, 0, 0)), full, full, full],
        out_specs=[full, full, full, full, pl.BlockSpec((1, LANE), lambda: (0, 0))],
        out_shape=[jax.ShapeDtypeStruct((8, D_MODEL), F32)] * 4 + [jax.ShapeDtypeStruct((1, LANE), F32)],
        compiler_params=_cparams(),
    )(gathered, w, m, v)


def _unpack_small(p, shapes):
    out = {n: p[i].reshape(shapes[n]) for i, n in enumerate(SMALL_ROWS)}
    out["hg_norm_w"] = p[ROW_HG_NORM, :HEAD_DIM].reshape(shapes["hg_norm_w"])
    out["hg_lower_bounds"] = p[ROW_LB0:ROW_LB1 + 1, :HG_WIDTH].reshape(shapes["hg_lower_bounds"])
    return out


WHOLE = lambda f: (f, 0, None)
MID_MATRICES = ("w_branch_a", "w_branch_b", "w_out", "wq_cross", "wkv_cross", "wo_cross")
MID_WEIGHTS = MID_MATRICES
W_IN_PIECES = [("w_in", r0, 512) for r0 in range(0, D_MODEL // 2, 512)]
W13_PIECES = [("w13", r0, 512) for r0 in range(0, D_MODEL // 2, 512)]
GATHER_GROUPS = [("mid", [WHOLE(f) for f in MID_MATRICES]), ("w13a", W13_PIECES[:1]), ("w13b", W13_PIECES[1:]),
                 ("w2", [WHOLE("w2")])]
OTHER_WEIGHTS = ["w1", "w3", "w2"] + list(MID_WEIGHTS)
BEFORE = {
    "hgrn_fwd": [("wait", "mid")],
    "gate_fwd": [("wait", "w13a")],
    "mm_o": [("wait", "w13b")],
    "mm_w13": [("wait", "w2")],
    "mm_dh": [("wait", "rs_w2"), ("chip_sum", "w2"), ("wait", "rs_w13"), ("chip_sum", "w13"), ("wait", "rs_mid")]
    + [("chip_sum", f) for f in MID_MATRICES],
}
CARRY = {
    "hgrn_fwd": [("d2d", [WHOLE(f) for f in MID_MATRICES])],
    "gate_fwd": [("d2d", W13_PIECES[:1])],
    "mm_o": [("d2d", W13_PIECES[1:])],
    "mm_w13": [("d2d", [WHOLE("w2")])],
    "mm_du": [("pairx", ["w2"])],
    "mm_dhf": [("pairx", ["w13"])],
    "attn_merge_bwd": [("pairx", list(MID_MATRICES))],
    "mm_dwin_own": [("pairx_whole", ["w_in"])],
    "mm_dh": [("share", OTHER_WEIGHTS)],
}
AFTER = {
    "mm_du": [("pair_sum", "w2"), ("start", "rs_w2", [WHOLE("w2")])],
    "mm_dhf": [("pair_sum", "w13"), ("start", "rs_w13", [WHOLE("w13")])],
    "attn_merge_bwd": [("pair_sum", f) for f in MID_MATRICES] + [("start", "rs_mid", [WHOLE(f) for f in MID_MATRICES])],
    "mm_dwin_own": [("pair_sum", "w_in"), ("start", "rs_w_in", [WHOLE("w_in")])],
}
FINISH = [
    ("adamw", OTHER_WEIGHTS), ("wait", "rs_w_in"), ("small",), ("chip_sum", "w_in"),
    ("run", ("share", ["w_in"]), "rs_sibling_share_w_in"), ("adamw", ["w_in"]),
]


class _Net:
    def __init__(self, full, pos=None, shard_shapes=None, comm=True, specs=FULL_SPECS, place=WEIGHT_PLACE):
        self.full, self.pos, self.shard_shapes, self.comm = dict(full), pos, shard_shapes, comm
        self.specs, self.place = specs, place
        self.gw, self.recv, self.psum, self.slots, self.grads = {}, {}, {}, {}, {}
        self.gw_sibling = {}
        self.pending, self.token, self.last = {}, None, None

    def _make(self, kind, arg):
        if kind == "gather":
            return _gather_ici_comm(self.full, arg, self.specs)
        if kind == "ring":
            return _gather_ring_comm(self.full, *arg, self.specs)
        if kind == "d2d":
            return _gather_d2d_comm(self.full, arg, self.specs)
        if kind == "pairx":
            return _pairx_comm(self.gw, arg, self.specs)
        if kind == "pairx_whole":
            return _pairx_comm(self.gw_sibling, arg, self.specs, whole=True)
        if kind == "chipx":
            return _chipx_comm(self.psum, self.slots, arg, self.specs)
        assert kind == "share"
        return _share_comm(self.grads, arg, self.specs, self.place)

    def _store(self, kind, res):
        if kind in ("gather", "ring", "d2d"):
            self.full.update(res)
        elif kind in ("pairx", "pairx_whole"):
            for (tag, f), a in res.items():
                (self.gw if tag == "g" else self.recv)[f] = a
        elif kind == "chipx":
            for (tag, f), a in res.items():
                (self.psum if tag == "p" else self.slots)[f] = a
        else:
            self.grads.update(res)

    def run_comm(self, item, name):
        kind, arg = item
        self._store(kind, _run_comm([self._make(kind, arg)], name)[0])

    @staticmethod
    def _others(after, items):
        own = [a for cm in items for a in cm.arrays.values()]
        return [a for a in after if a is not None and all(a is not o for o in own)]

    def start(self, groups, kind, name):
        items = [self._make(kind, jobs) for _, jobs in groups]
        after = self._others([self.last], items)
        res, sems, token = _split_start(items, name, after=after[0] if after else None)
        for (group, jobs), r, s in zip(groups, res, sems):
            self._store(kind, r)
            self.pending[group] = (kind, jobs, s)
        self.token = self.last = token

    def wait(self, group, after=()):
        kind, jobs, sems = self.pending.pop(group)
        item = self._make(kind, jobs)
        res = _split_wait([item], [sems], self._others([self.last, *after], [item]), f"wait_{group}")[0]
        self._store(kind, res)

    def step(self, step):
        if step[0] == "wait":
            self.wait(step[1])
        elif step[0] == "start":
            self.start([(step[1], step[2])], "chipx", f"start_{step[1]}")
        elif step[0] == "pair_sum":
            f = step[1]
            self.psum[f], self.slots[f] = _pair_sum(self.gw[f], self.recv[f], self.pos, self.specs[f],
                                                    f"rs_pair_sum_{f}", whole=f in self.gw_sibling)
        elif step[0] == "chip_sum":
            f = step[1]
            self.grads.update(_chip_sum(self.psum[f], self.slots[f], self.pos, f, self.shard_shapes,
                                        self.specs, self.place, f"rs_chip_sum_{f}"))
        else:
            assert step[0] == "run"
            self.run_comm(step[1], step[2])

    def call(self, fn, name, *args, grad_of=None, sibling_half=False, **kw):
        for step in (BEFORE.get(name, []) if self.comm else []):
            self.step(step)
        items = CARRY.get(name, []) if self.comm else []
        carried = [self._make(k, a) for k, a in items]
        if self.token is not None:
            carried.append(_Token(self.token))
            self.token = None
        out, res = fn(*args, name=name, carried=carried, **kw)
        if grad_of is not None:
            (self.gw_sibling if sibling_half else self.gw)[grad_of] = out
        self.last = jax.tree.leaves(out)[0]
        for (kind, _), r in zip(items, res):
            self._store(kind, r)
        for step in (AFTER.get(name, []) if self.comm else []):
            self.step(step)
        return out


def _local_step(net, x, h, mem, target, small):
    full, call = net.full, net.call
    proj = call(_mm, "mm_proj", h, full["w_in"], mode="nn", out_dtype=F32)
    att = [call(_attn_fwd, f"attn_fwd_g{g}", proj, g) for g in range(3)]
    outs, lses = [a[0] for a in att], [a[1] for a in att]
    o_att = _attn_merge_fwd(outs, lses, "attn_merge")
    oraw, o_hg, states = call(_hg_fwd, "hgrn_fwd", proj, small["hg_lower_bounds"], small["hg_norm_w"])
    ya = call(_mm, "mm_branch_a", o_att, full["w_branch_a"], mode="nn", out_dtype=F32)
    yb = call(_mm, "mm_branch_b", o_hg, full["w_branch_b"], mode="nn", out_dtype=F32)
    merged = call(_gate_fwd, "gate_fwd", proj, ya, yb)
    x1 = call(_mm, "mm_out", merged, full["w_out"], mode="nn", out_dtype=F32, res=x)

    hc = _rms_fwd(x1, small["ln_cross_w"], "rms_cross")
    mn = _rms_fwd(mem, small["ln_mem_w"], "rms_mem")
    qc = call(_mm, "mm_q", hc, full["wq_cross"], mode="nn", out_dtype=F32)
    kvc = call(_mm, "mm_kv", mn, full["wkv_cross"], mode="nn", out_dtype=F32)
    oc = call(_cross_fwd, "cross_fwd", qc, kvc)
    x2 = call(_mm, "mm_o", oc, full["wo_cross"], mode="nn", out_dtype=F32, res=x1)

    hf = _rms_fwd(x2, small["ln_ffn_w"], "rms_ffn")
    ab, u = call(_ff_up, "mm_w13", hf, full["w13"])
    x3 = call(_mm, "mm_w2", u, full["w2"], mode="nn", out_dtype=F32, res=x2)

    dx3, dg_final, loss = _loss_head(x3, small["ln_final_w"], target, "loss_head")

    gs = {"ln_final_w": dg_final}
    call(_mm, "mm_dw2", u, dx3, mode="tn", out_dtype=BF16, grad_of="w2")
    dab = call(_ff_down_bwd, "mm_du", dx3, full["w2"], ab)
    call(_mm, "mm_dw13", hf, dab, mode="tn", out_dtype=BF16, grad_of="w13")
    dhf = call(_mm, "mm_dhf", dab, full["w13"], mode="nt", out_dtype=F32)
    dx2, gs["ln_ffn_w"] = _rms_bwd(x2, small["ln_ffn_w"], dhf, dx3, "rms_ffn_bwd")
    doc = call(_mm, "mm_doc", dx2, full["wo_cross"], mode="nt", out_dtype=BF16)
    call(_mm, "mm_dwo", oc, dx2, mode="tn", out_dtype=BF16, grad_of="wo_cross")
    dqc, dkvc = _cross_bwd(qc, kvc, doc, "cross_bwd")
    call(_mm, "mm_dwq", hc, dqc, mode="tn", out_dtype=BF16, grad_of="wq_cross")
    dhc = call(_mm, "mm_dhc", dqc, full["wq_cross"], mode="nt", out_dtype=F32)
    call(_mm, "mm_dwkv", mn, dkvc, mode="tn", out_dtype=BF16, grad_of="wkv_cross")
    dmn = call(_mm, "mm_dmn", dkvc, full["wkv_cross"], mode="nt", out_dtype=F32)
    _, gs["ln_mem_w"] = _rms_bwd(mem, small["ln_mem_w"], dmn, None, "rms_mem_bwd")
    dx1, gs["ln_cross_w"] = _rms_bwd(x1, small["ln_cross_w"], dhc, dx2, "rms_cross_bwd")
    dmerged = call(_mm, "mm_dmerged", dx1, full["w_out"], mode="nt", out_dtype=F32)
    call(_mm, "mm_dwout", merged, dx1, mode="tn", out_dtype=BF16, grad_of="w_out")
    dya, dyb, dga, dgb = call(_gate_bwd, "gate_bwd", proj, ya, yb, dmerged)
    call(_mm, "mm_dwa", o_att, dya, mode="tn", out_dtype=BF16, grad_of="w_branch_a")
    do_att = call(_mm, "mm_doatt", dya, full["w_branch_a"], mode="nt", out_dtype=F32)
    call(_mm, "mm_dwb", o_hg, dyb, mode="tn", out_dtype=BF16, grad_of="w_branch_b")
    do_hg = call(_mm, "mm_dohg", dyb, full["w_branch_b"], mode="nt", out_dtype=F32)
    dqh, dfh, dih, dgh, dlb, gs["hg_norm_w"] = call(
        _hg_bwd, "hgrn_bwd", proj, small["hg_lower_bounds"], small["hg_norm_w"], oraw, states, do_hg)
    gs["hg_lb"] = dlb
    do_gs, dl_gs = call(_attn_merge_bwd, "attn_merge_bwd", outs, lses, do_att)
    dqs, dks, dvs = zip(*[call(_attn_bwd, f"attn_bwd_g{g}", proj, g, lses[g], do_gs[g], dl_gs[g]) for g in range(3)])
    dproj = jnp.concatenate([*dqs, *dks, *dvs, dqh, dfh, dih, dgh, dga, dgb], axis=1)
    if net.comm:
        half = D_MODEL // 2
        c = lax.axis_index("c")
        h_sibling = lax.dynamic_slice_in_dim(h, (1 - c) * half, half, axis=1)
        h_own = lax.dynamic_slice_in_dim(h, c * half, half, axis=1)
        call(_mm, "mm_dwin_sibling", h_sibling, dproj, mode="tn", out_dtype=BF16, grad_of="w_in", sibling_half=True)
        call(_mm, "mm_dwin_own", h_own, dproj, mode="tn", out_dtype=BF16, grad_of="w_in")
    else:
        call(_mm, "mm_dwin", h, dproj, mode="tn", out_dtype=BF16, grad_of="w_in")
    dh = call(_mm, "mm_dh", dproj, full["w_in"], mode="nt", out_dtype=F32)
    dx, gs["ln_mix_w"] = _rms_bwd(x, small["ln_mix_w"], dh, dx1, "rms_mix_bwd")
    return loss, dx, gs


WEIGHT_ORDER = ("ln_mix_w", "w_in", "hg_norm_w", "hg_lower_bounds", "w_branch_a", "w_branch_b", "w_out",
                "ln_cross_w", "ln_mem_w", "wq_cross", "wkv_cross", "wo_cross", "ln_ffn_w", "w1", "w3", "w2",
                "ln_final_w")


def kernel(x, mem, ln_mix_w, w_in, hg_norm_w, hg_lower_bounds, w_branch_a, w_branch_b, w_out, ln_cross_w, ln_mem_w, wq_cross, wkv_cross, wo_cross, ln_ffn_w, w1, w3, w2, ln_final_w, loss_target, m_ln_mix_w, m_w_in, m_hg_norm_w, m_hg_lower_bounds, m_w_branch_a, m_w_branch_b, m_w_out, m_ln_cross_w, m_ln_mem_w, m_wq_cross, m_wkv_cross, m_wo_cross, m_ln_ffn_w, m_w1, m_w3, m_w2, m_ln_final_w, v_ln_mix_w, v_w_in, v_hg_norm_w, v_hg_lower_bounds, v_w_branch_a, v_w_branch_b, v_w_out, v_ln_cross_w, v_ln_mem_w, v_wq_cross, v_wkv_cross, v_wo_cross, v_ln_ffn_w, v_w1, v_w3, v_w2, v_ln_final_w):
    args = dict(locals())
    w = {n: args[n] for n in WEIGHT_ORDER}
    m = {n: args["m_" + n] for n in WEIGHT_ORDER}
    v = {n: args["v_" + n] for n in WEIGHT_ORDER}
    shapes = {n: w[n].shape for n in WEIGHT_ORDER}
    mat = lambda a: a.reshape(a.shape[-2:])
    shard_shapes = {n: shapes[n][-2:] for n in BIG_WEIGHTS}

    pos = _mesh_scalars()

    def cast(f, token=None):
        return _cast_into_full({n: mat(w[n]) for n in BIG_WEIGHTS if WEIGHT_PLACE[n][0] == f}, f, pos,
                               FULL_SPECS, WEIGHT_PLACE, f"cast_{f}", token)

    net = _Net({"w_in": cast("w_in")}, pos, shard_shapes)
    net.start([(f"ring_a{i}", (*job, "a")) for i, job in enumerate(W_IN_PIECES)], "ring", "gather_start_w_in")
    rest = {f: cast(f, net.token) for f in FULL_SPECS if f != "w_in"}
    net.full.update(rest)
    small = {n: w[n].reshape(1, -1) for n in SMALL_ROWS}
    small["hg_norm_w"] = w["hg_norm_w"].reshape(1, HEAD_DIM)
    small["hg_lower_bounds"] = w["hg_lower_bounds"]
    x2d = x.reshape(SEQ, D_MODEL)
    h = _rms_fwd(x2d, small["ln_mix_w"], "rms_mix")
    for i, job in enumerate(W_IN_PIECES):
        net.wait(f"ring_a{i}", after=[*rest.values(), h] if i == 0 else ())
        net.start([(f"ring_b{i}", (*job, "b"))], "ring", f"gather_pass_on_w_in{i}")
    net.start(GATHER_GROUPS, "gather", "gather_start_rest")
    for i, job in enumerate(W_IN_PIECES):
        net.wait(f"ring_b{i}")
        net.run_comm(("d2d", [job]), f"gather_hand_over_w_in{i}")
    loss, dx, gs = _local_step(net, x2d, h, mem.reshape(MEM_LEN, D_MODEL), loss_target.reshape(SEQ, D_MODEL), small)

    out_g, out_d, out_m, out_v = {}, {}, {}, {}
    net.last = dx
    for step in FINISH:
        if step[0] == "adamw":
            for n in step[1]:
                out_d[n], out_m[n], out_v[n], out_g[n] = net.call(_adamw, f"adamw_{n}", mat(w[n]), net.grads[n],
                                                                  mat(m[n]), mat(v[n]))
        elif step[0] == "wait":
            net.wait(step[1], after=list(out_d.values()))
        elif step[0] == "small":
            pad = lambda a: jnp.pad(a, ((0, 0), (0, D_MODEL - a.shape[1])))
            part = jnp.concatenate(
                [gs[n] for n in SMALL_ROWS]
                + [pad(jnp.concatenate([gs["hg_norm_w"][0], loss], axis=1)), pad(gs["hg_lb"]),
                   pad(gs["hg_norm_w"][1]) if len(gs["hg_norm_w"]) > 1 else jnp.zeros((1, D_MODEL), F32)], axis=0)
            part, net.psum["w_in"] = lax.optimization_barrier((part, net.psum["w_in"]))
            sg, sd, sm, sv, loss_tot = _small_update(_gather_rows(part), _pack_small(w), _pack_small(m),
                                                     _pack_small(v))
            for dst, packed in ((out_g, sg), (out_d, sd), (out_m, sm), (out_v, sv)):
                dst.update(_unpack_small(packed, shapes))
        else:
            net.step(step)

    result = [loss_tot[0, 0], dx.reshape(x.shape)]
    for group in (out_g, out_d, out_m, out_v):
        result += [group[n].reshape(shapes[n]) for n in WEIGHT_ORDER]
    return tuple(result)
```

```python
import math

import jax
import jax.numpy as jnp
from jax import lax
from jax.experimental import pallas as pl
from jax.experimental.pallas import tpu as pltpu

F32 = jnp.float32
BF16 = jnp.bfloat16
MESH = pl.DeviceIdType.MESH

D_MODEL = 2048
SEQ = 2048
HEAD_DIM = 128
MEM_LEN = 256
ATT_GROUPS = ((128, 1), (512, 4), (2048, 16))
ATT_HEADS = 4
ATT_WIDTH = 3 * ATT_HEADS * HEAD_DIM
ATT_OUT = ATT_HEADS * HEAD_DIM
ATT_BLOCK = 128
HG_HEADS = 8
HG_WIDTH = HG_HEADS * HEAD_DIM
HG_CHUNK = 64
IN_WIDTH = 3 * ATT_WIDTH + 4 * HG_WIDTH + 2 * D_MODEL
CROSS_HEADS = 4
CROSS_WIDTH = CROSS_HEADS * HEAD_DIM
D_FF = 5632
RMS_EPS = 1e-6
ADAM_LR = 0.001
ADAM_B1 = 0.9
ADAM_B2 = 0.999
ADAM_EPS = 1e-08
ADAM_WD = 0.01
ADAM_STEP = 10
N_CHIPS = 4
N_DEV = 8

VMEM_LIMIT_BYTES = 56 * 1024 * 1024
LANE = 128
MXU_WIDTH = 256
MM_TILE_CAP = 1536
TRANSPOSE_CHUNK = 512
ANY = pl.BlockSpec(memory_space=pl.ANY)


def _cparams(sem=None):
    return pltpu.CompilerParams(dimension_semantics=sem, vmem_limit_bytes=VMEM_LIMIT_BYTES)


def _div(n, cap, mult):
    best = None
    for d in range(mult, min(n, cap) + 1, mult):
        if n % d == 0:
            best = d
    assert best is not None, (n, cap, mult)
    return best


def _sigmoid(x):
    return 1.0 / (1.0 + jnp.exp(-x))


def _dot(a, b):
    return jnp.dot(a.astype(BF16), b.astype(BF16), preferred_element_type=F32)


def _dot_nt(a, b):
    return lax.dot_general(a.astype(BF16), b.astype(BF16), (((1,), (1,)), ((), ())),
                           preferred_element_type=F32)


def _dot_tn(a, b):
    return jnp.dot(a.astype(F32).T.astype(BF16), b.astype(BF16), preferred_element_type=F32)


def _dot_exact(a, b):
    return jnp.dot(a, b, precision=lax.Precision.HIGHEST, preferred_element_type=F32)


class _Carried:
    def __init__(self, arrays, fresh, n_sems, start, finish, mid=None, reads=None):
        self.arrays, self.fresh, self.n_sems, self.reads = arrays, fresh, n_sems, reads or {}
        self.start, self.mid, self.finish = start, mid, finish


class _Token:
    def __init__(self, array):
        self.array = array


TOKEN_SHAPE = (8, LANE)


def _carried_layout(carried):
    akeys = list(dict.fromkeys(k for cm in carried for k in cm.arrays))
    fkeys = [(ci, k) for ci, cm in enumerate(carried) for k in cm.fresh]
    arrays = [next(cm.arrays[k] for cm in carried if k in cm.arrays) for k in akeys]
    shapes = [jax.ShapeDtypeStruct(a.shape, a.dtype) for a in arrays] + [carried[ci].fresh[k] for ci, k in fkeys]
    sems = []
    for cm in carried:
        sems += [pltpu.SemaphoreType.DMA((cm.n_sems,)), pltpu.SemaphoreType.DMA((cm.n_sems,))]
    return akeys, fkeys, arrays, shapes, sems


def _carried_reads(carried):
    rkeys = list(dict.fromkeys(k for cm in carried for k in cm.reads))
    return rkeys, [next(cm.reads[k] for cm in carried if k in cm.reads) for k in rkeys]


def _carried_results(carried, akeys, fkeys, outs, rkeys=(), read_refs=()):
    shared = dict(zip(akeys, outs[:len(akeys)]))
    shared.update(zip(rkeys, read_refs))
    res = [{k: shared[k] for k in list(cm.arrays) + [r for r in cm.reads if r in shared]} for cm in carried]
    for (ci, k), o in zip(fkeys, outs[len(akeys):]):
        res[ci][k] = o
    return res


def _pcall(kern, *, name, grid, in_specs, out_specs, out_shape, args, scratch_shapes=(), semantics=None,
           carried=()):
    tokens = [c.array for c in carried if isinstance(c, _Token)]
    carried = [c for c in carried if not isinstance(c, _Token)]
    single = not isinstance(out_shape, (list, tuple))
    out_specs = [out_specs] if single else list(out_specs)
    out_shape = [out_shape] if single else list(out_shape)
    n_real, n_out, n_scr = len(in_specs), len(out_shape), len(scratch_shapes)
    in_specs = list(in_specs) + [pl.BlockSpec(TOKEN_SHAPE, lambda *_: (0, 0))] * len(tokens)
    args = list(args) + tokens
    n_in = len(in_specs)
    if not carried:
        def plain(*refs):
            kern(*refs[:n_real], *refs[n_in:])

        outs = pl.pallas_call(plain if tokens else kern, name=name, grid=grid, in_specs=in_specs,
                              out_specs=out_specs, out_shape=out_shape, scratch_shapes=list(scratch_shapes),
                              compiler_params=_cparams(semantics))(*args)
        return (outs[0] if single else list(outs)), []
    akeys, fkeys, arrays, shapes, sems = _carried_layout(carried)
    rkeys, reads = _carried_reads(carried)
    n_a, n_f, n_r = len(akeys), len(fkeys), len(rkeys)
    total = math.prod(grid)
    mid_step = min(total - 1, (17 * total) // 20)

    def wrapped(*refs):
        ins = refs[:n_real]
        r0 = n_in + n_a
        o0 = r0 + n_r
        outs = refs[o0:o0 + n_out]
        a0 = o0 + n_out
        s0 = a0 + n_a + n_f
        per = _carried_results(carried, akeys, fkeys, refs[a0:s0], rkeys, refs[r0:o0])
        scratch = refs[s0:s0 + n_scr]
        sem = refs[s0 + n_scr:]
        step = 0
        for d, g in enumerate(grid):
            step = step * g + pl.program_id(d)

        @pl.when(step == 0)
        def _():
            for ci, cm in enumerate(carried):
                cm.start(per[ci], sem[2 * ci], sem[2 * ci + 1])

        kern(*ins, *outs, *scratch)

        @pl.when(step == mid_step)
        def _():
            for ci, cm in enumerate(carried):
                if cm.mid is not None:
                    cm.mid(per[ci], sem[2 * ci], sem[2 * ci + 1])

        @pl.when(step == total - 1)
        def _():
            for ci, cm in enumerate(carried):
                cm.finish(per[ci], sem[2 * ci], sem[2 * ci + 1])

    outs = pl.pallas_call(
        wrapped, name=name, grid=grid,
        in_specs=list(in_specs) + [ANY] * (n_a + n_r), out_specs=out_specs + [ANY] * (n_a + n_f),
        out_shape=out_shape + shapes,
        input_output_aliases={n_in + i: n_out + i for i in range(n_a)},
        scratch_shapes=list(scratch_shapes) + sems,
        compiler_params=_cparams(("arbitrary",) * len(grid)),
    )(*args, *arrays, *reads)
    res = _carried_results(carried, akeys, fkeys, outs[n_out:])
    return (outs[0] if single else list(outs[:n_out])), res


def _run_comm(carried, name):
    carried = list(carried)
    akeys, fkeys, arrays, shapes, sems = _carried_layout(carried)
    rkeys, reads = _carried_reads(carried)
    n_a, n_f, n_r = len(akeys), len(fkeys), len(rkeys)

    def body(*refs):
        o0 = n_a + n_r
        per = _carried_results(carried, akeys, fkeys, refs[o0:o0 + n_a + n_f], rkeys, refs[n_a:o0])
        sem = refs[o0 + n_a + n_f:]
        for hook in ("start", "mid", "finish"):
            for ci, cm in enumerate(carried):
                fn = getattr(cm, hook)
                if fn is not None:
                    fn(per[ci], sem[2 * ci], sem[2 * ci + 1])

    outs = pl.pallas_call(
        body, name=name, in_specs=[ANY] * (n_a + n_r), out_specs=[ANY] * (n_a + n_f), out_shape=shapes,
        input_output_aliases={i: i for i in range(n_a)}, scratch_shapes=sems,
    )(*arrays, *reads)
    return _carried_results(carried, akeys, fkeys, outs)


HBM_SPEC = pl.BlockSpec(memory_space=pltpu.HBM)
SEM_SPEC = pl.BlockSpec(memory_space=pltpu.SEMAPHORE)
SPLIT_EFFECT = pltpu.SideEffectType.DATAFLOW_SIDE_EFFECTING


def _in_hbm(a):
    return pltpu.with_memory_space_constraint(a, pltpu.HBM)


def _split_start(items, name, after=None):
    items = list(items)
    akeys, fkeys, arrays, shapes, sems = _carried_layout(items)
    assert not fkeys
    n_a, n_s = len(akeys), len(sems)
    n_in = n_a + (after is not None)

    def body(*refs):
        per = _carried_results(items, akeys, [], refs[n_in:n_in + n_a])
        sem = refs[n_in + n_a:n_in + n_a + n_s]
        for ci, cm in enumerate(items):
            cm.start(per[ci], sem[2 * ci], sem[2 * ci + 1])
        token = refs[n_in + n_a + n_s]
        token[...] = jnp.zeros_like(token)

    outs = pl.pallas_call(
        body, name=name, in_specs=[HBM_SPEC] * n_a + [ANY] * (after is not None),
        out_specs=[HBM_SPEC] * n_a + [SEM_SPEC] * n_s + [pl.BlockSpec(memory_space=pltpu.VMEM)],
        out_shape=[pltpu.HBM(s.shape, s.dtype) for s in shapes] + sems + [jax.ShapeDtypeStruct(TOKEN_SHAPE, F32)],
        input_output_aliases={i: i for i in range(n_a)},
        compiler_params=pltpu.CompilerParams(has_side_effects=SPLIT_EFFECT),
    )(*[_in_hbm(a) for a in arrays], *([after] if after is not None else []))
    res = _carried_results(items, akeys, [], outs[:n_a])
    sem_out = outs[n_a:n_a + n_s]
    return res, [(sem_out[2 * ci], sem_out[2 * ci + 1]) for ci in range(len(items))], outs[-1]


def _split_wait(items, sems, after, name):
    items = list(items)
    after = list(after) if isinstance(after, (list, tuple)) else [after]
    akeys, fkeys, arrays, shapes, _ = _carried_layout(items)
    n_a, n_s = len(akeys), 2 * len(items)

    def body(*refs):
        per = _carried_results(items, akeys, [], refs[n_a + n_s + len(after):])
        sem = refs[n_a:n_a + n_s]
        for ci, cm in enumerate(items):
            cm.finish(per[ci], sem[2 * ci], sem[2 * ci + 1])

    outs = pl.pallas_call(
        body, name=name, in_specs=[HBM_SPEC] * n_a + [SEM_SPEC] * n_s + [ANY] * len(after),
        out_specs=[HBM_SPEC] * n_a, out_shape=[pltpu.HBM(s.shape, s.dtype) for s in shapes],
        input_output_aliases={i: i for i in range(n_a)},
        compiler_params=pltpu.CompilerParams(has_side_effects=SPLIT_EFFECT),
    )(*arrays, *[s for pair in sems for s in pair], *after)
    return _carried_results(items, akeys, [], outs)


def _mm(a, b, *, mode, out_dtype, name, res=None, carried=(), fused=None):
    if mode == "nn":
        (m, k), (k2, n) = a.shape, b.shape
    elif mode == "nt":
        (m, k), (n, k2) = a.shape, b.shape
    else:
        (k, m), (k2, n) = a.shape, b.shape
    assert k == k2, (name, a.shape, b.shape)
    tm = _div(m, MM_TILE_CAP, LANE)
    tn = _div(n, MM_TILE_CAP, MXU_WIDTH) if n % MXU_WIDTH == 0 else 0
    if tn < 1024:
        tn = _div(n, MM_TILE_CAP, LANE)
    out_shape = jax.ShapeDtypeStruct((m, n), out_dtype)

    if fused is not None:
        assert mode in ("nn", "nt") and res is None and k <= 2048
        tm, tn = fused["tile"]
        dot = _dot if mode == "nn" else _dot_nt
        n_x = len(fused["ins"])

        def kern_fused(*refs):
            part = dot(refs[0][...], refs[1][...])
            outs = fused["post"](part, *[r[...] for r in refs[2:2 + n_x]])
            for o_ref, val in zip(refs[2 + n_x:], outs):
                o_ref[...] = val.astype(o_ref.dtype)

        tile_spec = lambda shape: pl.BlockSpec(shape, lambda j, i: (i, j))
        return _pcall(
            kern_fused, name=name, grid=(n // tn, m // tm),
            in_specs=[pl.BlockSpec((tm, k), lambda j, i: (i, 0)),
                      pl.BlockSpec((k, tn), lambda j, i: (0, j)) if mode == "nn"
                      else pl.BlockSpec((tn, k), lambda j, i: (j, 0))] + [tile_spec(s) for _, s in fused["ins"]],
            out_specs=[tile_spec(s) for _, s in fused["outs"]], out_shape=[o for o, _ in fused["outs"]],
            args=(a, b, *[x for x, _ in fused["ins"]]), semantics=("parallel", "parallel"), carried=carried)

    if mode == "tn":
        assert res is None

        def kern_tn(a_ref, b_ref, o_ref, at_ref):
            @pl.when(pl.program_id(1) == 0)
            def _():
                step = min(TRANSPOSE_CHUNK, k)
                for c0 in range(0, k, step):
                    at_ref[:, c0:c0 + step] = a_ref[c0:c0 + step, :].astype(F32).T.astype(BF16)

            o_ref[...] = jnp.dot(at_ref[...], b_ref[...].astype(BF16),
                                 preferred_element_type=F32).astype(o_ref.dtype)

        return _pcall(
            kern_tn, name=name, grid=(m // tm, n // tn),
            in_specs=[pl.BlockSpec((k, tm), lambda i, j: (0, i)),
                      pl.BlockSpec((k, tn), lambda i, j: (0, j))],
            out_specs=pl.BlockSpec((tm, tn), lambda i, j: (i, j)),
            out_shape=out_shape, args=(a, b),
            scratch_shapes=[pltpu.VMEM((tm, k), BF16)],
            semantics=("parallel", "arbitrary"), carried=carried)

    tk = k if k <= 2048 else _div(k, 3072, LANE)
    nk = k // tk
    a_spec = pl.BlockSpec((tm, tk), lambda i, j, kk: (i, kk))
    if mode == "nn":
        b_spec = pl.BlockSpec((tk, tn), lambda i, j, kk: (kk, j))
        dot = _dot
    else:
        b_spec = pl.BlockSpec((tn, tk), lambda i, j, kk: (j, kk))
        dot = _dot_nt
    o_spec = pl.BlockSpec((tm, tn), lambda i, j, kk: (i, j))
    in_specs = [a_spec, b_spec]
    args = [a, b]
    if res is not None:
        in_specs.append(o_spec)
        args.append(res)
    has_res = res is not None

    def kern(*refs):
        a_ref, b_ref = refs[0], refs[1]
        r_ref = refs[2] if has_res else None
        o_ref = refs[3] if has_res else refs[2]
        part = dot(a_ref[...], b_ref[...])
        if nk == 1:
            if has_res:
                part = part + r_ref[...]
            o_ref[...] = part.astype(o_ref.dtype)
            return
        acc_ref = refs[-1]
        kk = pl.program_id(2)

        @pl.when(kk == 0)
        def _():
            acc_ref[...] = part

        @pl.when(kk > 0)
        def _():
            acc_ref[...] += part

        @pl.when(kk == nk - 1)
        def _():
            tot = acc_ref[...]
            if has_res:
                tot = tot + r_ref[...]
            o_ref[...] = tot.astype(o_ref.dtype)

    return _pcall(
        kern, name=name, grid=(m // tm, n // tn, nk),
        in_specs=in_specs, out_specs=o_spec, out_shape=out_shape, args=args,
        scratch_shapes=[pltpu.VMEM((tm, tn), F32)] if nk > 1 else [],
        semantics=("parallel", "parallel", "arbitrary"), carried=carried)


ROW_BLOCK = 512


def _rms_fwd(x, g, name):
    t, d = x.shape
    tr = min(ROW_BLOCK, t)

    def kern(x_ref, g_ref, o_ref):
        xf = x_ref[...]
        r = lax.rsqrt(jnp.mean(xf * xf, axis=-1, keepdims=True) + RMS_EPS)
        o_ref[...] = (xf * r * g_ref[...]).astype(o_ref.dtype)

    return pl.pallas_call(
        kern, name=name, grid=(t // tr,),
        in_specs=[pl.BlockSpec((tr, d), lambda i: (i, 0)), pl.BlockSpec((1, d), lambda i: (0, 0))],
        out_specs=pl.BlockSpec((tr, d), lambda i: (i, 0)),
        out_shape=jax.ShapeDtypeStruct((t, d), BF16),
        compiler_params=_cparams(("parallel",)),
    )(x, g)


def _rms_bwd(x, g, dh, res, name):
    t, d = x.shape
    tr = min(ROW_BLOCK, t)
    has_res = res is not None

    def kern(*refs):
        x_ref, g_ref, dh_ref = refs[:3]
        r_ref = refs[3] if has_res else None
        dx_ref, dg_ref = refs[-2], refs[-1]
        xf = x_ref[...]
        r = lax.rsqrt(jnp.mean(xf * xf, axis=-1, keepdims=True) + RMS_EPS)
        xn = xf * r
        dh_ = dh_ref[...]
        dhg = dh_ * g_ref[...]
        dx = r * (dhg - xn * jnp.mean(dhg * xn, axis=-1, keepdims=True))
        if has_res:
            dx = dx + r_ref[...]
        dx_ref[...] = dx
        part = jnp.sum(dh_ * xn, axis=0, keepdims=True)

        @pl.when(pl.program_id(0) == 0)
        def _():
            dg_ref[...] = part

        @pl.when(pl.program_id(0) > 0)
        def _():
            dg_ref[...] += part

    row = pl.BlockSpec((tr, d), lambda i: (i, 0))
    vec = pl.BlockSpec((1, d), lambda i: (0, 0))
    in_specs = [row, vec, row] + ([row] if has_res else [])
    args = [x, g, dh] + ([res] if has_res else [])
    return pl.pallas_call(
        kern, name=name, grid=(t // tr,), in_specs=in_specs, out_specs=[row, vec],
        out_shape=[jax.ShapeDtypeStruct((t, d), F32), jax.ShapeDtypeStruct((1, d), F32)],
        compiler_params=_cparams(("arbitrary",)),
    )(*args)


def _loss_head(x3, g, target, name):
    t, d = x3.shape
    tr = ROW_BLOCK

    def kern(x_ref, g_ref, t_ref, dx_ref, dg_ref, loss_ref):
        xf = x_ref[...]
        r = lax.rsqrt(jnp.mean(xf * xf, axis=-1, keepdims=True) + RMS_EPS)
        xn = xf * r
        gg = g_ref[...]
        err = xn * gg - t_ref[...]
        lpart = 0.5 * jnp.sum(jnp.mean(err * err, axis=-1, keepdims=True), axis=0, keepdims=True)
        dy = err * (1.0 / d)
        dyg = dy * gg
        dx_ref[...] = r * (dyg - xn * jnp.mean(dyg * xn, axis=-1, keepdims=True))
        gpart = jnp.sum(dy * xn, axis=0, keepdims=True)
        lrow = jnp.broadcast_to(lpart, (1, LANE))

        @pl.when(pl.program_id(0) == 0)
        def _():
            dg_ref[...] = gpart
            loss_ref[...] = lrow

        @pl.when(pl.program_id(0) > 0)
        def _():
            dg_ref[...] += gpart
            loss_ref[...] += lrow

    row = pl.BlockSpec((tr, d), lambda i: (i, 0))
    vec = pl.BlockSpec((1, d), lambda i: (0, 0))
    return pl.pallas_call(
        kern, name=name, grid=(t // tr,), in_specs=[row, vec, row],
        out_specs=[row, vec, pl.BlockSpec((1, LANE), lambda i: (0, 0))],
        out_shape=[jax.ShapeDtypeStruct((t, d), F32), jax.ShapeDtypeStruct((1, d), F32),
                   jax.ShapeDtypeStruct((1, LANE), F32)],
        compiler_params=_cparams(("arbitrary",)),
    )(x3, g, target)


ATT_SCALE = HEAD_DIM ** -0.5
Q_BLOCK0, K_BLOCK0, V_BLOCK0 = 0, ATT_WIDTH // HEAD_DIM, 2 * ATT_WIDTH // HEAD_DIM


def _residue_rows(dil, r, n):
    if dil == 1:
        return pl.ds(n * ATT_BLOCK, ATT_BLOCK)
    return pl.ds(n * ATT_BLOCK * dil + r, ATT_BLOCK, stride=dil)


def _band_mask(with_prev):
    width = 2 * ATT_BLOCK if with_prev else ATT_BLOCK
    iq = lax.broadcasted_iota(jnp.int32, (ATT_BLOCK, width), 0)
    ik = lax.broadcasted_iota(jnp.int32, (ATT_BLOCK, width), 1)
    if not with_prev:
        return ik <= iq
    return ((ik < ATT_BLOCK) & (iq <= ik)) | ((ik >= ATT_BLOCK) & ((ik - ATT_BLOCK) <= iq))


def _band_keys(ref, dil, r, n):
    own = ref[_residue_rows(dil, r, n), :]
    if n == 0:
        return own
    return jnp.concatenate([ref[_residue_rows(dil, r, n - 1), :], own], axis=0)


def _attn_col_spec(base, grp):
    return pl.BlockSpec((SEQ, HEAD_DIM), lambda h: (0, base + grp * ATT_HEADS + h))


def _attn_fwd(proj, grp, name, carried=()):
    _, dil = ATT_GROUPS[grp]
    nb = SEQ // dil // ATT_BLOCK

    def kern(q_ref, k_ref, v_ref, o_ref, lse_ref):
        for r in range(dil):
            for n in range(nb):
                rows = _residue_rows(dil, r, n)
                s = _dot_nt(q_ref[rows, :], _band_keys(k_ref, dil, r, n)) * ATT_SCALE
                s = jnp.where(_band_mask(n > 0), s, -jnp.inf)
                m = jnp.max(s, axis=-1, keepdims=True)
                p = jnp.exp(s - m)
                l = jnp.sum(p, axis=-1, keepdims=True)
                o_ref[rows, :] = _dot(p / l, _band_keys(v_ref, dil, r, n))
                lse_ref[rows, :] = jnp.broadcast_to(m + jnp.log(l), (ATT_BLOCK, HEAD_DIM))

    out_spec = pl.BlockSpec((SEQ, HEAD_DIM), lambda h: (0, h))
    return _pcall(
        kern, name=name, grid=(ATT_HEADS,),
        in_specs=[_attn_col_spec(Q_BLOCK0, grp), _attn_col_spec(K_BLOCK0, grp), _attn_col_spec(V_BLOCK0, grp)],
        out_specs=[out_spec, out_spec],
        out_shape=[jax.ShapeDtypeStruct((SEQ, ATT_OUT), F32)] * 2, args=(proj, proj, proj),
        semantics=("parallel",), carried=carried)


def _attn_weights(l0, l1, l2):
    mx = jnp.maximum(jnp.maximum(l0, l1), l2)
    e0, e1, e2 = jnp.exp(l0 - mx), jnp.exp(l1 - mx), jnp.exp(l2 - mx)
    den = e0 + e1 + e2
    return e0 / den, e1 / den, e2 / den


def _attn_merge_fwd(outs, lses, name):
    tr = ROW_BLOCK

    def kern(o0, o1, o2, l0, l1, l2, out_ref):
        a0, a1, a2 = _attn_weights(l0[...], l1[...], l2[...])
        out_ref[...] = (a0 * o0[...] + a1 * o1[...] + a2 * o2[...]).astype(out_ref.dtype)

    spec = pl.BlockSpec((tr, ATT_OUT), lambda i: (i, 0))
    return pl.pallas_call(
        kern, name=name, grid=(SEQ // tr,), in_specs=[spec] * 6, out_specs=spec,
        out_shape=jax.ShapeDtypeStruct((SEQ, ATT_OUT), BF16),
        compiler_params=_cparams(("parallel",)),
    )(*outs, *lses)


def _attn_merge_bwd(outs, lses, do_att, name, carried=()):
    tr = ROW_BLOCK

    def kern(o0, o1, o2, l0, l1, l2, do_ref, d0, d1, d2, t0, t1, t2):
        alphas = _attn_weights(l0[...], l1[...], l2[...])
        do = do_ref[...]
        o_att = alphas[0] * o0[...] + alphas[1] * o1[...] + alphas[2] * o2[...]
        prod = do * o_att
        parts = []
        for h in range(ATT_HEADS):
            sl = slice(h * HEAD_DIM, (h + 1) * HEAD_DIM)
            tot = jnp.sum(prod[:, sl], axis=-1, keepdims=True)
            parts.append(jnp.broadcast_to(tot, (tr, HEAD_DIM)))
        dd = jnp.concatenate(parts, axis=1)
        for a, d_ref, t_ref in zip(alphas, (d0, d1, d2), (t0, t1, t2)):
            d_ref[...] = a * do
            t_ref[...] = -a * dd

    spec = pl.BlockSpec((tr, ATT_OUT), lambda i: (i, 0))
    res, cres = _pcall(
        kern, name=name, grid=(SEQ // tr,), in_specs=[spec] * 7, out_specs=[spec] * 6,
        out_shape=[jax.ShapeDtypeStruct((SEQ, ATT_OUT), F32)] * 6, args=(*outs, *lses, do_att),
        semantics=("parallel",), carried=carried)
    return (res[:3], res[3:]), cres


def _attn_bwd(proj, grp, lse, do_g, dl_g, name, carried=()):
    _, dil = ATT_GROUPS[grp]
    nb = SEQ // dil // ATT_BLOCK

    def kern(q_ref, k_ref, v_ref, do_ref, lse_ref, dl_ref, dq_ref, dk_ref, dv_ref, dq_acc, dk_acc, dv_acc):
        dk_acc[...] = jnp.zeros_like(dk_acc)
        dv_acc[...] = jnp.zeros_like(dv_acc)
        for r in range(dil):
            for n in range(nb):
                rows = _residue_rows(dil, r, n)
                q, do = q_ref[rows, :], do_ref[rows, :]
                kk, vv = _band_keys(k_ref, dil, r, n), _band_keys(v_ref, dil, r, n)
                s = _dot_nt(q, kk) * ATT_SCALE
                p = jnp.where(_band_mask(n > 0), jnp.exp(s - lse_ref[rows, :][:, :1]), 0.0)
                ds = p * (_dot_nt(do, vv) + dl_ref[rows, :][:, :1])
                dq_acc[rows, :] = _dot(ds, kk) * ATT_SCALE
                dk = _dot_tn(ds, q) * ATT_SCALE
                dv = _dot_tn(p, do)
                if n > 0:
                    prev = _residue_rows(dil, r, n - 1)
                    dk_acc[prev, :] += dk[:ATT_BLOCK]
                    dv_acc[prev, :] += dv[:ATT_BLOCK]
                    dk, dv = dk[ATT_BLOCK:], dv[ATT_BLOCK:]
                dk_acc[rows, :] += dk
                dv_acc[rows, :] += dv
        dq_ref[...] = dq_acc[...].astype(dq_ref.dtype)
        dk_ref[...] = dk_acc[...].astype(dk_ref.dtype)
        dv_ref[...] = dv_acc[...].astype(dv_ref.dtype)

    spec = pl.BlockSpec((SEQ, HEAD_DIM), lambda h: (0, h))
    return _pcall(
        kern, name=name, grid=(ATT_HEADS,),
        in_specs=[_attn_col_spec(Q_BLOCK0, grp), _attn_col_spec(K_BLOCK0, grp), _attn_col_spec(V_BLOCK0, grp),
                  spec, spec, spec],
        out_specs=[spec] * 3,
        out_shape=[jax.ShapeDtypeStruct((SEQ, ATT_OUT), BF16)] * 3, args=(proj, proj, proj, do_g, lse, dl_g),
        scratch_shapes=[pltpu.VMEM((SEQ, HEAD_DIM), F32)] * 3,
        semantics=("parallel",), carried=carried)


HG_HEADS_PER_STEP = 8
HG_BLOCK_W = 4 * HEAD_DIM
HG_BLOCKS = HG_HEADS_PER_STEP * HEAD_DIM // HG_BLOCK_W
HG_STEP_W = HG_HEADS_PER_STEP * HEAD_DIM
HG_Q_BLK = (3 * ATT_WIDTH) // HG_BLOCK_W
HG_N_CHUNKS = SEQ // HG_CHUNK
HG_MID = HG_CHUNK // 2


def _lower_bound(lb_ref, sl):
    l0, l1 = lb_ref[0:1, sl], lb_ref[1:2, sl]
    mx = jnp.maximum(l0, l1)
    e0, e1 = jnp.exp(l0 - mx), jnp.exp(l1 - mx)
    return e0 / (e0 + e1)


def _tri(lower):
    i = lax.broadcasted_iota(jnp.int32, (HG_CHUNK, HG_CHUNK), 0)
    j = lax.broadcasted_iota(jnp.int32, (HG_CHUNK, HG_CHUNK), 1)
    return (i >= j) if lower else (i <= j)


def _head_mean(x):
    parts = []
    for hd in range(x.shape[1] // HEAD_DIM):
        m = jnp.mean(x[:, hd * HEAD_DIM:(hd + 1) * HEAD_DIM], axis=-1, keepdims=True)
        parts.append(jnp.broadcast_to(m, (x.shape[0], HEAD_DIM)))
    return jnp.concatenate(parts, axis=1)


def _hg_chunk_terms(qh, fh, lb):
    sig = _sigmoid(fh)
    f = lb + (1.0 - lb) * sig
    k = 1.0 - f
    b = _dot_exact(_tri(True).astype(F32), jnp.log(f))
    bl = b[HG_CHUNK - 1:HG_CHUNK, :]
    br = b[HG_MID:HG_MID + 1, :]
    sq = _sigmoid(qh)
    q = qh * sq
    return dict(sig=sig, f=f, k=k, b=b, bl=bl, br=br, sq=sq, q=q,
                e1=jnp.exp(bl - b), e2=jnp.exp(b), e3=jnp.exp(b - br), e4=jnp.exp(br - b))


def _hg_fwd(proj, lbw, normw, name, carried=()):
    def in_blks(off):
        return [pl.BlockSpec((HG_CHUNK, HG_BLOCK_W), lambda hp, n, b=b: (n, HG_Q_BLK + off + hp * HG_BLOCKS + b))
                for b in range(HG_BLOCKS)]

    def kern(*refs):
        q_refs, f_refs, i_refs, g_refs = (refs[k * HG_BLOCKS:(k + 1) * HG_BLOCKS] for k in range(4))
        lb_ref, nw_ref, oraw_ref, ohg_ref, st_ref, state = refs[4 * HG_BLOCKS:]

        @pl.when(pl.program_id(1) == 0)
        def _():
            state[...] = jnp.zeros_like(state)

        causal = _tri(True)
        wide = lambda rs: jnp.concatenate([r[...] for r in rs], axis=1)
        t = _hg_chunk_terms(wide(q_refs), wide(f_refs), _lower_bound(lb_ref, slice(None)))
        v, gh = wide(i_refs), wide(g_refs)
        kd, qb, qr, kr = t["k"] * t["e1"], t["q"] * t["e2"], t["q"] * t["e3"], t["k"] * t["e4"]
        decay = jnp.exp(t["bl"])
        outs = []
        for hd in range(HG_HEADS_PER_STEP):
            sl = slice(hd * HEAD_DIM, (hd + 1) * HEAD_DIM)
            st = state[hd]
            st_ref[0, hd] = st
            a = jnp.where(causal, _dot_nt(qr[:, sl], kr[:, sl]), 0.0)
            outs.append(_dot_nt(qb[:, sl], st) + _dot(a, v[:, sl]))
            state[hd] = st * decay[:, sl] + _dot_tn(v[:, sl], kd[:, sl])
        o = jnp.concatenate(outs, axis=1)
        oraw_ref[...] = o
        r = lax.rsqrt(_head_mean(o * o) + RMS_EPS)
        nw = jnp.tile(nw_ref[...], (1, HG_HEADS_PER_STEP))
        ohg_ref[...] = (o * r * nw * (gh * _sigmoid(gh))).astype(ohg_ref.dtype)

    out_blk = pl.BlockSpec((HG_CHUNK, HG_STEP_W), lambda hp, n: (n, hp))
    return _pcall(
        kern, name=name, grid=(HG_HEADS // HG_HEADS_PER_STEP, HG_N_CHUNKS),
        in_specs=[*in_blks(0), *in_blks(2), *in_blks(4), *in_blks(6),
                  pl.BlockSpec((2, HG_STEP_W), lambda hp, n: (0, hp)),
                  pl.BlockSpec((1, HEAD_DIM), lambda hp, n: (0, 0))],
        out_specs=[out_blk, out_blk,
                   pl.BlockSpec((1, HG_HEADS_PER_STEP, HEAD_DIM, HEAD_DIM), lambda hp, n: (n, hp, 0, 0))],
        out_shape=[jax.ShapeDtypeStruct((SEQ, HG_WIDTH), F32), jax.ShapeDtypeStruct((SEQ, HG_WIDTH), BF16),
                   jax.ShapeDtypeStruct((HG_N_CHUNKS, HG_HEADS, HEAD_DIM, HEAD_DIM), F32)],
        args=(*[proj] * (4 * HG_BLOCKS), lbw, normw),
        scratch_shapes=[pltpu.VMEM((HG_HEADS_PER_STEP, HEAD_DIM, HEAD_DIM), F32)],
        semantics=("parallel", "arbitrary"), carried=carried)


def _hg_bwd(proj, lbw, normw, oraw, states, do_hg, name, carried=()):
    last = HG_N_CHUNKS - 1

    def in_blks(off):
        return [pl.BlockSpec((HG_CHUNK, HG_BLOCK_W),
                             lambda hp, n, b=b: (last - n, HG_Q_BLK + off + hp * HG_BLOCKS + b))
                for b in range(HG_BLOCKS)]

    blk = pl.BlockSpec((HG_CHUNK, HG_STEP_W), lambda hp, n: (last - n, hp))

    def kern(*refs):
        q_refs, f_refs, i_refs, g_refs = (refs[k * HG_BLOCKS:(k + 1) * HG_BLOCKS] for k in range(4))
        (lb_ref, nw_ref, oraw_ref, st_ref, do_ref, dq_ref, df_ref, di_ref, dg_ref, dlb_ref, dnw_ref,
         dstate) = refs[4 * HG_BLOCKS:]
        first = pl.program_id(1) == 0

        @pl.when(first)
        def _():
            dstate[...] = jnp.zeros_like(dstate)

        causal = _tri(True)
        wide = lambda rs: jnp.concatenate([r[...] for r in rs], axis=1)
        cat = lambda parts: jnp.concatenate(parts, axis=1)
        qh, fh, v, gh = wide(q_refs), wide(f_refs), wide(i_refs), wide(g_refs)
        o, dout = oraw_ref[...], do_ref[...]
        nw = jnp.tile(nw_ref[...], (1, HG_HEADS_PER_STEP))
        sgg = _sigmoid(gh)
        r = lax.rsqrt(_head_mean(o * o) + RMS_EPS)
        xn = o * r
        dg_ref[...] = (dout * xn * nw * (sgg * (1.0 + gh * (1.0 - sgg)))).astype(dg_ref.dtype)
        don = dout * (gh * sgg)
        dnw_wide = jnp.sum(don * xn, axis=0, keepdims=True)
        dnw_tot = dnw_wide[:, :HEAD_DIM]
        for hd in range(1, HG_HEADS_PER_STEP):
            dnw_tot = dnw_tot + dnw_wide[:, hd * HEAD_DIM:(hd + 1) * HEAD_DIM]
        tt = don * nw
        do = r * (tt - xn * _head_mean(tt * xn))
        lb = _lower_bound(lb_ref, slice(None))
        t = _hg_chunk_terms(qh, fh, lb)
        k, q = t["k"], t["q"]
        kd, qb, qr, kr = k * t["e1"], q * t["e2"], q * t["e3"], k * t["e4"]
        decay = jnp.exp(t["bl"])
        dqb, dqr, dkr, dkd, dv, ddecay = [], [], [], [], [], []
        for hd in range(HG_HEADS_PER_STEP):
            sl = slice(hd * HEAD_DIM, (hd + 1) * HEAD_DIM)
            st = st_ref[0, hd]
            dstn = dstate[hd]
            a = jnp.where(causal, _dot_nt(qr[:, sl], kr[:, sl]), 0.0)
            da = jnp.where(causal, _dot_nt(do[:, sl], v[:, sl]), 0.0)
            dqb.append(_dot(do[:, sl], st))
            dv.append(_dot_tn(a, do[:, sl]) + _dot_nt(kd[:, sl], dstn))
            dqr.append(_dot(da, kr[:, sl]))
            dkr.append(_dot_tn(da, qr[:, sl]))
            dkd.append(_dot(v[:, sl], dstn))
            ddecay.append(jnp.sum(dstn * st, axis=0, keepdims=True))
            dstate[hd] = dstn * decay[:, sl] + _dot_tn(do[:, sl], qb[:, sl])
        dqb, dqr, dkr, dkd, dv, ddecay = cat(dqb), cat(dqr), cat(dkr), cat(dkd), cat(dv), cat(ddecay)
        dq = dqb * t["e2"] + dqr * t["e3"]
        dk = dkd * t["e1"] + dkr * t["e4"]
        db = dqb * qb + dqr * qr - dkr * kr - dkd * kd
        dbl = jnp.sum(dkd * kd, axis=0, keepdims=True) + ddecay * decay
        dbr = jnp.sum(dkr * kr - dqr * qr, axis=0, keepdims=True)
        rows = lax.broadcasted_iota(jnp.int32, db.shape, 0)
        dlf = _dot_exact(_tri(False).astype(F32), db) + dbl + jnp.where(rows <= HG_MID, dbr, 0.0)
        df = dlf / t["f"] - dk
        sig, sq = t["sig"], t["sq"]
        df_ref[...] = (df * (1.0 - lb) * sig * (1.0 - sig)).astype(df_ref.dtype)
        dlb_row = jnp.sum(df * (1.0 - sig), axis=0, keepdims=True)
        dq_ref[...] = (dq * (sq * (1.0 + qh * (1.0 - sq)))).astype(dq_ref.dtype)
        di_ref[...] = dv.astype(di_ref.dtype)
        dnw_blk = jnp.broadcast_to(dnw_tot, (8, HEAD_DIM))

        @pl.when(first)
        def _():
            dlb_ref[...] = dlb_row
            dnw_ref[...] = dnw_blk

        @pl.when(jnp.logical_not(first))
        def _():
            dlb_ref[...] += dlb_row
            dnw_ref[...] += dnw_blk

    n_hp = HG_HEADS // HG_HEADS_PER_STEP
    outs, cres = _pcall(
        kern, name=name, grid=(n_hp, HG_N_CHUNKS),
        in_specs=[*in_blks(0), *in_blks(2), *in_blks(4), *in_blks(6),
                  pl.BlockSpec((2, HG_STEP_W), lambda hp, n: (0, hp)),
                  pl.BlockSpec((1, HEAD_DIM), lambda hp, n: (0, 0)),
                  blk,
                  pl.BlockSpec((1, HG_HEADS_PER_STEP, HEAD_DIM, HEAD_DIM), lambda hp, n: (last - n, hp, 0, 0)),
                  blk],
        out_specs=[blk, blk, blk, blk,
                   pl.BlockSpec((1, HG_STEP_W), lambda hp, n: (0, hp)),
                   pl.BlockSpec((8, HEAD_DIM), lambda hp, n: (hp, 0))],
        out_shape=[jax.ShapeDtypeStruct((SEQ, HG_WIDTH), BF16)] * 4
        + [jax.ShapeDtypeStruct((1, HG_WIDTH), F32), jax.ShapeDtypeStruct((8 * n_hp, HEAD_DIM), F32)],
        args=(*[proj] * (4 * HG_BLOCKS), lbw, normw, oraw, states, do_hg),
        scratch_shapes=[pltpu.VMEM((HG_HEADS_PER_STEP, HEAD_DIM, HEAD_DIM), F32)],
        semantics=("parallel", "arbitrary"), carried=carried)
    dqh, dfh, dih, dgh, dlb, dnw = outs
    return (dqh, dfh, dih, dgh, dlb, [dnw[8 * i:8 * i + 1] for i in range(n_hp)]), cres


GATE_BLOCK_W = 512
GATE_A_BLK = (3 * ATT_WIDTH + 4 * HG_WIDTH) // GATE_BLOCK_W
GATE_B_BLK = GATE_A_BLK + D_MODEL // GATE_BLOCK_W


def _gate_specs():
    tr = ROW_BLOCK
    blk = pl.BlockSpec((tr, GATE_BLOCK_W), lambda i, j: (i, j))
    ga = pl.BlockSpec((tr, GATE_BLOCK_W), lambda i, j: (i, GATE_A_BLK + j))
    gb = pl.BlockSpec((tr, GATE_BLOCK_W), lambda i, j: (i, GATE_B_BLK + j))
    return (SEQ // tr, D_MODEL // GATE_BLOCK_W), blk, ga, gb


def _gate_fwd(proj, ya, yb, name, carried=()):
    grid, blk, ga, gb = _gate_specs()

    def kern(ga_ref, gb_ref, ya_ref, yb_ref, o_ref):
        o_ref[...] = (_sigmoid(ga_ref[...]) * ya_ref[...] + _sigmoid(gb_ref[...]) * yb_ref[...]).astype(o_ref.dtype)

    return _pcall(
        kern, name=name, grid=grid, in_specs=[ga, gb, blk, blk], out_specs=blk,
        out_shape=jax.ShapeDtypeStruct((SEQ, D_MODEL), BF16), args=(proj, proj, ya, yb),
        semantics=("parallel", "parallel"), carried=carried)


def _gate_bwd(proj, ya, yb, dmerged, name, carried=()):
    grid, blk, ga, gb = _gate_specs()

    def kern(ga_ref, gb_ref, ya_ref, yb_ref, dm_ref, dya_ref, dyb_ref, dga_ref, dgb_ref):
        dm = dm_ref[...]
        sa, sb = _sigmoid(ga_ref[...]), _sigmoid(gb_ref[...])
        dya_ref[...] = (dm * sa).astype(dya_ref.dtype)
        dyb_ref[...] = (dm * sb).astype(dyb_ref.dtype)
        dga_ref[...] = (dm * ya_ref[...] * sa * (1.0 - sa)).astype(dga_ref.dtype)
        dgb_ref[...] = (dm * yb_ref[...] * sb * (1.0 - sb)).astype(dgb_ref.dtype)

    return _pcall(
        kern, name=name, grid=grid, in_specs=[ga, gb, blk, blk, blk], out_specs=[blk] * 4,
        out_shape=[jax.ShapeDtypeStruct((SEQ, D_MODEL), BF16)] * 4, args=(proj, proj, ya, yb, dmerged),
        semantics=("parallel", "parallel"), carried=carried)


FF_SHARD = D_FF // N_CHIPS


FF_TILE_ROWS = 512


def _swiglu_tile(ab):
    a, b = ab[:, :FF_SHARD], ab[:, FF_SHARD:]
    return a * _sigmoid(a) * b


def _swiglu_grad_tile(du, ab):
    a, b = ab[:, :FF_SHARD], ab[:, FF_SHARD:]
    sg = _sigmoid(a)
    return jnp.concatenate([du * b * (sg * (1.0 + a * (1.0 - sg))), du * (a * sg)], axis=1)


def _ff_up(hf, w13, name, carried=()):
    wide, narrow = (FF_TILE_ROWS, 2 * FF_SHARD), (FF_TILE_ROWS, FF_SHARD)
    return _mm(hf, w13, mode="nn", out_dtype=F32, name=name, carried=carried, fused=dict(
        tile=wide, ins=[],
        outs=[(jax.ShapeDtypeStruct((SEQ, 2 * D_FF), F32), wide), (jax.ShapeDtypeStruct((SEQ, D_FF), BF16), narrow)],
        post=lambda p: (p, _swiglu_tile(p))))


def _ff_down_bwd(dx3, w2, ab, name, carried=()):
    wide, narrow = (FF_TILE_ROWS, 2 * FF_SHARD), (FF_TILE_ROWS, FF_SHARD)
    out, res = _mm(dx3, w2, mode="nt", out_dtype=BF16, name=name, carried=carried, fused=dict(
        tile=narrow, ins=[(ab, wide)], outs=[(jax.ShapeDtypeStruct((SEQ, 2 * D_FF), BF16), wide)],
        post=lambda du, ab_: (_swiglu_grad_tile(du, ab_),)))
    return out[0], res


CROSS_ROWS = 512


def _cross_fwd(qc, kvc, name, carried=()):
    def kern(q_ref, k_ref, v_ref, o_ref):
        s = _dot_nt(q_ref[...], k_ref[...]) * ATT_SCALE
        m = jnp.max(s, axis=-1, keepdims=True)
        e = jnp.exp(s - m)
        p = e / jnp.sum(e, axis=-1, keepdims=True)
        o_ref[...] = _dot(p, v_ref[...]).astype(o_ref.dtype)

    qblk = pl.BlockSpec((CROSS_ROWS, HEAD_DIM), lambda h, i: (i, h))
    return _pcall(
        kern, name=name, grid=(CROSS_HEADS, SEQ // CROSS_ROWS),
        in_specs=[qblk, pl.BlockSpec((MEM_LEN, HEAD_DIM), lambda h, i: (0, h)),
                  pl.BlockSpec((MEM_LEN, HEAD_DIM), lambda h, i: (0, CROSS_HEADS + h))],
        out_specs=qblk, out_shape=jax.ShapeDtypeStruct((SEQ, CROSS_WIDTH), BF16), args=(qc, kvc, kvc),
        semantics=("parallel", "parallel"), carried=carried)


def _cross_bwd(qc, kvc, doc, name):
    def kern(q_ref, k_ref, v_ref, do_ref, dq_ref, dk_ref, dv_ref):
        q, k, v, do = q_ref[...], k_ref[...], v_ref[...], do_ref[...]
        s = _dot_nt(q, k) * ATT_SCALE
        m = jnp.max(s, axis=-1, keepdims=True)
        e = jnp.exp(s - m)
        p = e / jnp.sum(e, axis=-1, keepdims=True)
        dp = _dot_nt(do, v)
        ds = p * (dp - jnp.sum(dp * p, axis=-1, keepdims=True))
        dq_ref[...] = (_dot(ds, k) * ATT_SCALE).astype(dq_ref.dtype)
        dk = _dot_tn(ds, q) * ATT_SCALE
        dv = _dot_tn(p, do)

        @pl.when(pl.program_id(1) == 0)
        def _():
            dk_ref[...] = dk
            dv_ref[...] = dv

        @pl.when(pl.program_id(1) > 0)
        def _():
            dk_ref[...] += dk
            dv_ref[...] += dv

    qblk = pl.BlockSpec((CROSS_ROWS, HEAD_DIM), lambda h, i: (i, h))
    kblk = pl.BlockSpec((MEM_LEN, HEAD_DIM), lambda h, i: (0, h))
    dq, dk, dv = pl.pallas_call(
        kern, name=name, grid=(CROSS_HEADS, SEQ // CROSS_ROWS),
        in_specs=[qblk, kblk, pl.BlockSpec((MEM_LEN, HEAD_DIM), lambda h, i: (0, CROSS_HEADS + h)), qblk],
        out_specs=[qblk, kblk, kblk],
        out_shape=[jax.ShapeDtypeStruct((SEQ, CROSS_WIDTH), BF16),
                   jax.ShapeDtypeStruct((MEM_LEN, CROSS_WIDTH), F32),
                   jax.ShapeDtypeStruct((MEM_LEN, CROSS_WIDTH), F32)],
        compiler_params=_cparams(("parallel", "arbitrary")),
    )(qc, kvc, kvc, doc)
    return dq, jnp.concatenate([dk, dv], axis=1)


FULL_SPECS = {
    "w_in": ("col", D_MODEL, IN_WIDTH),
    "w_branch_a": ("col", ATT_OUT, D_MODEL),
    "w_branch_b": ("col", HG_WIDTH, D_MODEL),
    "w_out": ("row", D_MODEL, D_MODEL),
    "wq_cross": ("row", D_MODEL, CROSS_WIDTH),
    "wkv_cross": ("row", D_MODEL, 2 * CROSS_WIDTH),
    "wo_cross": ("col", CROSS_WIDTH, D_MODEL),
    "w13": ("col", D_MODEL, 2 * D_FF),
    "w2": ("row", D_FF, D_MODEL),
}
WEIGHT_PLACE = {
    "w_in": ("w_in", 0), "w_branch_a": ("w_branch_a", 0), "w_branch_b": ("w_branch_b", 0),
    "w_out": ("w_out", 0), "wq_cross": ("wq_cross", 0), "wkv_cross": ("wkv_cross", 0),
    "wo_cross": ("wo_cross", 0), "w1": ("w13", 0), "w3": ("w13", FF_SHARD), "w2": ("w2", 0),
}
BIG_WEIGHTS = tuple(WEIGHT_PLACE)
EW_BLOCK_ELEMS = 512 * 1024


def _position():
    return lax.axis_index("x"), lax.axis_index("y"), lax.axis_index("c")


def _other_chips(x, y):
    return [(1 - x, y), (x, 1 - y), (1 - x, 1 - y)]


def _half(ref, kind, h):
    r, c = ref.shape
    if kind == "col":
        return ref.at[pl.ds(h * (r // 2), r // 2), :]
    return ref.at[:, pl.ds(h * (c // 2), c // 2)]


def _shard_of(ref, kind, start, size):
    return ref.at[:, pl.ds(start, size)] if kind == "col" else ref.at[pl.ds(start, size), :]


def _rows_of(ref, r0, nrows):
    return ref if nrows is None else ref.at[pl.ds(r0, nrows), :]


def _half_shape(kind, rows, cols):
    return (rows // 2, cols) if kind == "col" else (rows, cols // 2)


def _slot_shape(spec):
    kind, rows, cols = spec
    hr, hc = _half_shape(kind, rows, cols)
    return (hr, hc // N_CHIPS) if kind == "col" else (hr // N_CHIPS, hc)


def _remote(src, dst, send_sem, recv_sem, device):
    return pltpu.make_async_remote_copy(src_ref=src, dst_ref=dst, send_sem=send_sem, recv_sem=recv_sem,
                                        device_id=device, device_id_type=MESH)


def _gather_ici_comm(fulls, jobs, specs):
    def piece(refs, job, chip, c):
        f, r0, nr = job
        kind, rows, cols = specs[f]
        stride = (cols if kind == "col" else rows) // N_CHIPS
        return _rows_of(_half(_shard_of(refs[f], kind, chip * stride, stride), kind, c), r0, nr)

    def start(refs, ss, rs):
        x, y, c = _position()
        j = 2 * x + y
        for q, job in enumerate(jobs):
            for p, (px, py) in enumerate(_other_chips(x, y)):
                _remote(piece(refs, job, j, c), piece(refs, job, j, c), ss.at[3 * q + p], rs.at[3 * q + p],
                        (px, py, c)).start()

    def finish(refs, ss, rs):
        x, y, c = _position()
        j = 2 * x + y
        for q, job in enumerate(jobs):
            for p, (px, py) in enumerate(_other_chips(x, y)):
                _remote(piece(refs, job, j, c), piece(refs, job, 2 * px + py, c), ss.at[3 * q + p],
                        rs.at[3 * q + p], (px, py, c)).wait_recv()
        for q, job in enumerate(jobs):
            for p, (px, py) in enumerate(_other_chips(x, y)):
                _remote(piece(refs, job, j, c), piece(refs, job, j, c), ss.at[3 * q + p], rs.at[3 * q + p],
                        (px, py, c)).wait_send()

    names = list(dict.fromkeys(job[0] for job in jobs))
    return _Carried({f: fulls[f] for f in names}, {}, 3 * len(jobs), start, finish)


def _gather_ring_comm(fulls, f, r0, nr, phase, specs):
    kind, _, cols = specs[f]
    assert kind == "col" and nr % 32 == 0
    stride = cols // N_CHIPS
    half = nr // 2

    def rows(refs, chip, c, lo, n):
        return _rows_of(_half(_shard_of(refs[f], kind, chip * stride, stride), kind, c), r0 + lo, n)

    def copies(refs, ss, rs):
        x, y, c = _position()
        me, nx, ny, dg = 2 * x + y, 2 * (1 - x) + y, 2 * x + (1 - y), 2 * (1 - x) + (1 - y)
        to_x, to_y = (1 - x, y, c), (x, 1 - y, c)
        if phase == "a":
            mine = rows(refs, me, c, 0, nr)
            return [(_remote(mine, mine, ss.at[0], rs.at[0], to_x), rows(refs, nx, c, 0, nr)),
                    (_remote(mine, mine, ss.at[1], rs.at[1], to_y), rows(refs, ny, c, 0, nr))]
        up, low = rows(refs, ny, c, half, half), rows(refs, nx, c, 0, half)
        return [(_remote(up, up, ss.at[0], rs.at[0], to_x), rows(refs, dg, c, half, half)),
                (_remote(low, low, ss.at[1], rs.at[1], to_y), rows(refs, dg, c, 0, half))]

    def start(refs, ss, rs):
        for cp, _ in copies(refs, ss, rs):
            cp.start()

    def finish(refs, ss, rs):
        x, y, c = _position()
        mine = copies(refs, ss, rs)
        for i, (_, landing) in enumerate(mine):
            _remote(landing, landing, ss.at[i], rs.at[i], (x, y, c)).wait_recv()
        for cp, _ in mine:
            cp.wait_send()

    return _Carried({f: fulls[f]}, {}, 2, start, finish)


def _gather_d2d_comm(fulls, jobs, specs):
    def rect(refs, job, h):
        f, r0, nr = job
        assert nr is None or specs[f][0] == "col"
        return _rows_of(_half(refs[f], specs[f][0], h), r0, nr)

    def start(refs, ss, rs):
        x, y, c = _position()
        for q, job in enumerate(jobs):
            _remote(rect(refs, job, c), rect(refs, job, c), ss.at[q], rs.at[q], (x, y, 1 - c)).start()

    def finish(refs, ss, rs):
        x, y, c = _position()
        for q, job in enumerate(jobs):
            _remote(rect(refs, job, 1 - c), rect(refs, job, 1 - c), ss.at[q], rs.at[q], (x, y, 1 - c)).wait_recv()
        for q, job in enumerate(jobs):
            _remote(rect(refs, job, c), rect(refs, job, c), ss.at[q], rs.at[q], (x, y, 1 - c)).wait_send()

    names = list(dict.fromkeys(job[0] for job in jobs))
    return _Carried({f: fulls[f] for f in names}, {}, len(jobs), start, finish)


def _pairx_comm(grads, names, specs, whole=False):
    def copies(refs, ss, rs):
        x, y, c = _position()
        src = (lambda f: refs[("g", f)]) if whole else (lambda f: _half(refs[("g", f)], specs[f][0], 1 - c))
        return [_remote(src(f), refs[("r", f)], ss.at[i], rs.at[i], (x, y, 1 - c)) for i, f in enumerate(names)]

    def start(refs, ss, rs):
        for cp in copies(refs, ss, rs):
            cp.start()

    def finish(refs, ss, rs):
        for cp in copies(refs, ss, rs):
            cp.wait_recv()
        for cp in copies(refs, ss, rs):
            cp.wait_send()

    fresh = {("r", f): jax.ShapeDtypeStruct(_half_shape(*specs[f]), BF16) for f in names}
    return _Carried({}, fresh, len(names), start, finish, reads={("g", f): grads[f] for f in names})


def _chipx_comm(pair_sums, slots, jobs, specs):
    def copies(refs, ss, rs):
        x, y, c = _position()
        out = []
        for q, (f, r0, nr) in enumerate(jobs):
            kind = specs[f][0]
            width = _slot_shape(specs[f])[1 if kind == "col" else 0]
            for p, (px, py) in enumerate(_other_chips(x, y)):
                src = _rows_of(_shard_of(refs[("p", f)], kind, (2 * px + py) * width, width), r0, nr)
                dst = _rows_of(refs[("s", f)].at[p], r0, nr)
                out.append(_remote(src, dst, ss.at[3 * q + p], rs.at[3 * q + p], (px, py, c)))
        return out

    def start(refs, ss, rs):
        for cp in copies(refs, ss, rs):
            cp.start()

    def finish(refs, ss, rs):
        for cp in copies(refs, ss, rs):
            cp.wait_recv()
        for cp in copies(refs, ss, rs):
            cp.wait_send()

    names = list(dict.fromkeys(job[0] for job in jobs))
    arrays = {("p", f): pair_sums[f] for f in names}
    arrays.update({("s", f): slots[f] for f in names})
    return _Carried(arrays, {}, 3 * len(jobs), start, finish)


def _share_comm(grads, wnames, specs, place):
    def start(refs, ss, rs):
        x, y, c = _position()
        for i, w in enumerate(wnames):
            kind = specs[place[w][0]][0]
            _remote(_half(refs[w], kind, c), _half(refs[w], kind, c), ss.at[i], rs.at[i], (x, y, 1 - c)).start()

    def finish(refs, ss, rs):
        x, y, c = _position()
        for i, w in enumerate(wnames):
            kind = specs[place[w][0]][0]
            _remote(_half(refs[w], kind, 1 - c), _half(refs[w], kind, 1 - c), ss.at[i], rs.at[i],
                    (x, y, 1 - c)).wait_recv()
        for i, w in enumerate(wnames):
            kind = specs[place[w][0]][0]
            _remote(_half(refs[w], kind, c), _half(refs[w], kind, c), ss.at[i], rs.at[i], (x, y, 1 - c)).wait_send()

    return _Carried({w: grads[w] for w in wnames}, {}, len(wnames), start, finish)


def _gather_rows(v, name="gather_small"):
    shape = v.shape

    def body(v_ref, out_ref, send_sem, recv_sem, loc_sem):
        x, y, c = _position()
        me = 4 * x + 2 * y + c
        flips = [(fx, fy, fc) for fx in (0, 1) for fy in (0, 1) for fc in (0, 1)][1:]

        def peer(fl):
            return tuple(1 - a if f else a for a, f in zip((x, y, c), fl))

        loc = pltpu.make_async_copy(v_ref, out_ref.at[me], loc_sem)
        loc.start()
        sends = []
        for i, fl in enumerate(flips):
            cp = _remote(v_ref, out_ref.at[me], send_sem.at[i], recv_sem.at[i], peer(fl))
            cp.start()
            sends.append(cp)
        for i, fl in enumerate(flips):
            px, py, pc = peer(fl)
            _remote(v_ref, out_ref.at[4 * px + 2 * py + pc], send_sem.at[i], recv_sem.at[i], peer(fl)).wait_recv()
        for cp in sends:
            cp.wait_send()
        loc.wait()

    return pl.pallas_call(
        body, name=name, in_specs=[ANY], out_specs=ANY,
        out_shape=jax.ShapeDtypeStruct((N_DEV,) + shape, F32),
        scratch_shapes=[pltpu.SemaphoreType.DMA((N_DEV - 1,)), pltpu.SemaphoreType.DMA((N_DEV - 1,)),
                        pltpu.SemaphoreType.DMA],
    )(v)


def _ew_block(rows, cols, elems=EW_BLOCK_ELEMS):
    tc = cols if cols <= 4096 else _div(cols, 2048, LANE)
    tr = _div(rows, max(16, elems // tc), 16)
    return tr, tc


def _mesh_scalars():
    x, y, c = _position()
    return jnp.stack([c, 2 * x + y]).astype(jnp.int32)


def _grid_spec(grid, in_specs, out_specs):
    return pltpu.PrefetchScalarGridSpec(num_scalar_prefetch=1, grid=grid, in_specs=in_specs, out_specs=out_specs)


def _cast_into_full(parts, fname, pos, specs, place, name, token=None):
    kind, rows, cols = specs[fname]
    ws = [w for w in place if place[w][0] == fname]
    if kind == "col":
        stride = cols // N_CHIPS
        hr = rows // 2
        tr = _div(hr, max(16, EW_BLOCK_ELEMS // stride), 16)
        nrb = hr // tr
        in_specs = [pl.BlockSpec((tr, parts[w].shape[1]), lambda i, pos_ref: (i + pos_ref[0] * nrb, 0)) for w in ws]
        out_spec = pl.BlockSpec((tr, stride), lambda i, pos_ref: (i + pos_ref[0] * nrb, pos_ref[1]))
    else:
        stride = rows // N_CHIPS
        hc = cols // 2
        tr = _div(stride, max(16, EW_BLOCK_ELEMS // hc), 16)
        nrb = stride // tr
        in_specs = [pl.BlockSpec((tr, hc), lambda i, pos_ref: (i, pos_ref[0])) for w in ws]
        out_spec = pl.BlockSpec((tr, hc), lambda i, pos_ref: (i + pos_ref[1] * nrb, pos_ref[0]))

    def kern(pos_ref, *refs):
        o_ref = refs[-1]
        for w, r in zip(ws, refs[:len(ws)]):
            off = place[w][1] if kind == "col" else 0
            o_ref[:, off:off + r.shape[1]] = r[...].astype(o_ref.dtype)

    tokens = [] if token is None else [token]
    in_specs = in_specs + [pl.BlockSpec(TOKEN_SHAPE, lambda i, pos_ref: (0, 0))] * len(tokens)
    return pl.pallas_call(
        kern, name=name, grid_spec=_grid_spec((nrb,), in_specs, out_spec),
        out_shape=jax.ShapeDtypeStruct((rows, cols), BF16),
        compiler_params=_cparams(("parallel",)),
    )(pos, *[parts[w] for w in ws], *tokens)


def _pair_sum(grad, recv, pos, spec, name, whole=False):
    kind, rows, cols = spec
    hr, hc = _half_shape(kind, rows, cols)
    tr, tc = _ew_block(hr, hc, 2 * EW_BLOCK_ELEMS)
    nrb, ncb = hr // tr, hc // tc
    blk = pl.BlockSpec((tr, tc), lambda i, jj, pos_ref: (i, jj))
    if whole:
        mine = blk
    elif kind == "col":
        mine = pl.BlockSpec((tr, tc), lambda i, jj, pos_ref: (i + pos_ref[0] * nrb, jj))
    else:
        mine = pl.BlockSpec((tr, tc), lambda i, jj, pos_ref: (i, jj + pos_ref[0] * ncb))

    def kern(pos_ref, g_ref, r_ref, o_ref, slots_ref):
        o_ref[...] = (g_ref[...].astype(F32) + r_ref[...].astype(F32)).astype(o_ref.dtype)

    return pl.pallas_call(
        kern, name=name, grid_spec=_grid_spec((nrb, ncb), [mine, blk], [blk, ANY]),
        out_shape=[jax.ShapeDtypeStruct((hr, hc), BF16),
                   jax.ShapeDtypeStruct((N_CHIPS - 1,) + _slot_shape(spec), BF16)],
        compiler_params=_cparams(("parallel", "parallel")),
    )(pos, grad, recv)


def _chip_sum(pair_sum, slots, pos, fname, shard_shapes, specs, place, name):
    kind, rows, cols = specs[fname]
    sr, sc = _slot_shape(specs[fname])
    ws = [w for w in place if place[w][0] == fname]
    n_slots = N_CHIPS - 1
    tr = _div(sr, max(16, EW_BLOCK_ELEMS // sc), 16)
    nrb = sr // tr
    slot = pl.BlockSpec((n_slots, tr, sc), lambda i, pos_ref: (0, i, 0))
    if kind == "col":
        own = pl.BlockSpec((tr, sc), lambda i, pos_ref: (i, pos_ref[1]))
        out_specs = [pl.BlockSpec((tr, shard_shapes[w][1]), lambda i, pos_ref: (i + pos_ref[0] * nrb, 0)) for w in ws]
    else:
        own = pl.BlockSpec((tr, sc), lambda i, pos_ref: (i + pos_ref[1] * nrb, 0))
        out_specs = [pl.BlockSpec((tr, sc), lambda i, pos_ref: (i, pos_ref[0])) for w in ws]

    def kern(pos_ref, own_ref, slot_ref, *out_refs):
        tot = own_ref[...].astype(F32)
        for s in range(n_slots):
            tot = tot + slot_ref[s].astype(F32)
        for w, o_ref in zip(ws, out_refs):
            off = place[w][1] if kind == "col" else 0
            o_ref[...] = tot[:, off:off + o_ref.shape[1]]

    outs = pl.pallas_call(
        kern, name=name, grid_spec=_grid_spec((nrb,), [own, slot], out_specs),
        out_shape=[jax.ShapeDtypeStruct(shard_shapes[w], F32) for w in ws],
        compiler_params=_cparams(("parallel",)),
    )(pos, pair_sum, slots)
    return dict(zip(ws, outs))


def _adam_math(w, g, m, v):
    m2 = ADAM_B1 * m + (1.0 - ADAM_B1) * g
    v2 = ADAM_B2 * v + (1.0 - ADAM_B2) * (g * g)
    m_hat = m2 / (1.0 - ADAM_B1 ** ADAM_STEP)
    v_hat = v2 / (1.0 - ADAM_B2 ** ADAM_STEP)
    delta = -ADAM_LR * (m_hat / (jnp.sqrt(v_hat) + ADAM_EPS) + ADAM_WD * w)
    return delta, m2, v2


def _adamw(w, g, m, v, name, carried=()):
    rows, cols = w.shape
    tr, tc = _ew_block(rows, cols)

    def kern(w_ref, g_ref, m_ref, v_ref, d_ref, m2_ref, v2_ref, g_out_ref):
        g_ = g_ref[...]
        d_ref[...], m2_ref[...], v2_ref[...] = _adam_math(w_ref[...], g_, m_ref[...], v_ref[...])
        g_out_ref[...] = g_

    blk = pl.BlockSpec((tr, tc), lambda i, j: (i, j))
    return _pcall(
        kern, name=name, grid=(rows // tr, cols // tc), in_specs=[blk] * 4, out_specs=[blk] * 4,
        out_shape=[jax.ShapeDtypeStruct((rows, cols), F32)] * 4, args=(w, g, m, v),
        semantics=("parallel", "parallel"), carried=carried)


SMALL_ROWS = ("ln_mix_w", "ln_cross_w", "ln_mem_w", "ln_ffn_w", "ln_final_w")
ROW_HG_NORM, ROW_LB0, ROW_LB1 = 5, 6, 7
LOSS_LANE0 = HEAD_DIM


def _pack_small(vals):
    rows = [vals[n].reshape(1, D_MODEL) for n in SMALL_ROWS]
    pad = lambda a: jnp.pad(a, ((0, 0), (0, D_MODEL - a.shape[1])))
    rows.append(pad(vals["hg_norm_w"].reshape(1, HEAD_DIM)))
    rows.append(pad(vals["hg_lower_bounds"].reshape(2, HG_WIDTH)))
    return jnp.concatenate(rows, axis=0)


def _small_update(gathered, w, m, v, name="small_update"):
    def kern(g_ref, w_ref, m_ref, v_ref, grad_ref, d_ref, m2_ref, v2_ref, loss_ref):
        tot = g_ref[0]
        for s in range(1, N_DEV):
            tot = tot + g_ref[s]
        wv = w_ref[...]
        row = lax.broadcasted_iota(jnp.int32, (8, D_MODEL), 0)
        lane = lax.broadcasted_iota(jnp.int32, (8, D_MODEL), 1)
        l0, l1 = wv[ROW_LB0:ROW_LB0 + 1], wv[ROW_LB1:ROW_LB1 + 1]
        mx = jnp.maximum(l0, l1)
        e0, e1 = jnp.exp(l0 - mx), jnp.exp(l1 - mx)
        p0 = e0 / (e0 + e1)
        dlog = tot[ROW_LB0:ROW_LB0 + 1] * p0 * (1.0 - p0)
        tot = jnp.where(row == ROW_HG_NORM, tot + tot[ROW_LB1:ROW_LB1 + 1], tot)
        grad = jnp.where(row == ROW_LB0, dlog, jnp.where(row == ROW_LB1, -dlog, tot))
        grad = jnp.where((row == ROW_HG_NORM) & (lane >= HEAD_DIM), 0.0, grad)
        grad = jnp.where((row >= ROW_LB0) & (lane >= HG_WIDTH), 0.0, grad)
        grad_ref[...] = grad
        d_ref[...], m2_ref[...], v2_ref[...] = _adam_math(wv, grad, m_ref[...], v_ref[...])
        loss_ref[...] = tot[ROW_HG_NORM:ROW_HG_NORM + 1, LOSS_LANE0:LOSS_LANE0 + LANE]

    full = pl.BlockSpec((8, D_MODEL), lambda: (0, 0))
    return pl.pallas_call(
        kern, name=name,
        in_specs=[pl.BlockSpec((N_DEV, 8, D_MODEL), lambda: (0, 0, 0)), full, full, full],
        out_specs=[full, full, full, full, pl.BlockSpec((1, LANE), lambda: (0, 0))],
        out_shape=[jax.ShapeDtypeStruct((8, D_MODEL), F32)] * 4 + [jax.ShapeDtypeStruct((1, LANE), F32)],
        compiler_params=_cparams(),
    )(gathered, w, m, v)


def _unpack_small(p, shapes):
    out = {n: p[i].reshape(shapes[n]) for i, n in enumerate(SMALL_ROWS)}
    out["hg_norm_w"] = p[ROW_HG_NORM, :HEAD_DIM].reshape(shapes["hg_norm_w"])
    out["hg_lower_bounds"] = p[ROW_LB0:ROW_LB1 + 1, :HG_WIDTH].reshape(shapes["hg_lower_bounds"])
    return out


WHOLE = lambda f: (f, 0, None)
MID_MATRICES = ("w_branch_a", "w_branch_b", "w_out", "wq_cross", "wkv_cross", "wo_cross")
MID_WEIGHTS = MID_MATRICES
W_IN_PIECES = [("w_in", r0, 512) for r0 in range(0, D_MODEL // 2, 512)]
W13_PIECES = [("w13", r0, 512) for r0 in range(0, D_MODEL // 2, 512)]
GATHER_GROUPS = [("mid", [WHOLE(f) for f in MID_MATRICES]), ("w13a", W13_PIECES[:1]), ("w13b", W13_PIECES[1:]),
                 ("w2", [WHOLE("w2")])]
OTHER_WEIGHTS = ["w1", "w3", "w2"] + list(MID_WEIGHTS)
BEFORE = {
    "hgrn_fwd": [("wait", "mid")],
    "gate_fwd": [("wait", "w13a")],
    "mm_o": [("wait", "w13b")],
    "mm_w2": [("wait", "w2"), ("run", ("d2d", [WHOLE("w2")]), "gather_hand_over_w2")],
    "mm_dh": [("wait", "rs_w2"), ("chip_sum", "w2"), ("wait", "rs_w13"), ("chip_sum", "w13"), ("wait", "rs_mid")]
    + [("chip_sum", f) for f in MID_MATRICES],
}
CARRY = {
    "hgrn_fwd": [("d2d", [WHOLE(f) for f in MID_MATRICES])],
    "gate_fwd": [("d2d", W13_PIECES[:1])],
    "mm_o": [("d2d", W13_PIECES[1:])],
    "mm_du": [("pairx", ["w2"])],
    "mm_dhf": [("pairx", ["w13"])],
    "attn_merge_bwd": [("pairx", list(MID_MATRICES))],
    "mm_dwin_own": [("pairx_whole", ["w_in"])],
    "mm_dh": [("share", OTHER_WEIGHTS)],
}
AFTER = {
    "mm_du": [("pair_sum", "w2"), ("start", "rs_w2", [WHOLE("w2")])],
    "mm_dhf": [("pair_sum", "w13"), ("start", "rs_w13", [WHOLE("w13")])],
    "attn_merge_bwd": [("pair_sum", f) for f in MID_MATRICES] + [("start", "rs_mid", [WHOLE(f) for f in MID_MATRICES])],
    "mm_dwin_own": [("pair_sum", "w_in"), ("start", "rs_w_in", [WHOLE("w_in")])],
}
FINISH = [
    ("adamw", OTHER_WEIGHTS), ("wait", "rs_w_in"), ("small",), ("chip_sum", "w_in"),
    ("run", ("share", ["w_in"]), "rs_sibling_share_w_in"), ("adamw", ["w_in"]),
]


class _Net:
    def __init__(self, full, pos=None, shard_shapes=None, comm=True, specs=FULL_SPECS, place=WEIGHT_PLACE):
        self.full, self.pos, self.shard_shapes, self.comm = dict(full), pos, shard_shapes, comm
        self.specs, self.place = specs, place
        self.gw, self.recv, self.psum, self.slots, self.grads = {}, {}, {}, {}, {}
        self.gw_sibling = {}
        self.pending, self.token, self.last = {}, None, None

    def _make(self, kind, arg):
        if kind == "gather":
            return _gather_ici_comm(self.full, arg, self.specs)
        if kind == "ring":
            return _gather_ring_comm(self.full, *arg, self.specs)
        if kind == "d2d":
            return _gather_d2d_comm(self.full, arg, self.specs)
        if kind == "pairx":
            return _pairx_comm(self.gw, arg, self.specs)
        if kind == "pairx_whole":
            return _pairx_comm(self.gw_sibling, arg, self.specs, whole=True)
        if kind == "chipx":
            return _chipx_comm(self.psum, self.slots, arg, self.specs)
        assert kind == "share"
        return _share_comm(self.grads, arg, self.specs, self.place)

    def _store(self, kind, res):
        if kind in ("gather", "ring", "d2d"):
            self.full.update(res)
        elif kind in ("pairx", "pairx_whole"):
            for (tag, f), a in res.items():
                (self.gw if tag == "g" else self.recv)[f] = a
        elif kind == "chipx":
            for (tag, f), a in res.items():
                (self.psum if tag == "p" else self.slots)[f] = a
        else:
            self.grads.update(res)

    def run_comm(self, item, name):
        kind, arg = item
        self._store(kind, _run_comm([self._make(kind, arg)], name)[0])

    @staticmethod
    def _others(after, items):
        own = [a for cm in items for a in cm.arrays.values()]
        return [a for a in after if a is not None and all(a is not o for o in own)]

    def start(self, groups, kind, name):
        items = [self._make(kind, jobs) for _, jobs in groups]
        after = self._others([self.last], items)
        res, sems, token = _split_start(items, name, after=after[0] if after else None)
        for (group, jobs), r, s in zip(groups, res, sems):
            self._store(kind, r)
            self.pending[group] = (kind, jobs, s)
        self.token = self.last = token

    def wait(self, group, after=()):
        kind, jobs, sems = self.pending.pop(group)
        item = self._make(kind, jobs)
        res = _split_wait([item], [sems], self._others([self.last, *after], [item]), f"wait_{group}")[0]
        self._store(kind, res)

    def step(self, step):
        if step[0] == "wait":
            self.wait(step[1])
        elif step[0] == "start":
            self.start([(step[1], step[2])], "chipx", f"start_{step[1]}")
        elif step[0] == "pair_sum":
            f = step[1]
            self.psum[f], self.slots[f] = _pair_sum(self.gw[f], self.recv[f], self.pos, self.specs[f],
                                                    f"rs_pair_sum_{f}", whole=f in self.gw_sibling)
        elif step[0] == "chip_sum":
            f = step[1]
            self.grads.update(_chip_sum(self.psum[f], self.slots[f], self.pos, f, self.shard_shapes,
                                        self.specs, self.place, f"rs_chip_sum_{f}"))
        else:
            assert step[0] == "run"
            self.run_comm(step[1], step[2])

    def call(self, fn, name, *args, grad_of=None, sibling_half=False, **kw):
        for step in (BEFORE.get(name, []) if self.comm else []):
            self.step(step)
        args = [a() if callable(a) else a for a in args]
        items = CARRY.get(name, []) if self.comm else []
        carried = [self._make(k, a) for k, a in items]
        if self.token is not None:
            carried.append(_Token(self.token))
            self.token = None
        out, res = fn(*args, name=name, carried=carried, **kw)
        if grad_of is not None:
            (self.gw_sibling if sibling_half else self.gw)[grad_of] = out
        self.last = jax.tree.leaves(out)[0]
        for (kind, _), r in zip(items, res):
            self._store(kind, r)
        for step in (AFTER.get(name, []) if self.comm else []):
            self.step(step)
        return out


def _local_step(net, x, h, mem, target, small):
    full, call = net.full, net.call
    proj = call(_mm, "mm_proj", h, full["w_in"], mode="nn", out_dtype=F32)
    att = [call(_attn_fwd, f"attn_fwd_g{g}", proj, g) for g in range(3)]
    outs, lses = [a[0] for a in att], [a[1] for a in att]
    o_att = _attn_merge_fwd(outs, lses, "attn_merge")
    oraw, o_hg, states = call(_hg_fwd, "hgrn_fwd", proj, small["hg_lower_bounds"], small["hg_norm_w"])
    ya = call(_mm, "mm_branch_a", o_att, full["w_branch_a"], mode="nn", out_dtype=F32)
    yb = call(_mm, "mm_branch_b", o_hg, full["w_branch_b"], mode="nn", out_dtype=F32)
    merged = call(_gate_fwd, "gate_fwd", proj, ya, yb)
    x1 = call(_mm, "mm_out", merged, full["w_out"], mode="nn", out_dtype=F32, res=x)

    hc = _rms_fwd(x1, small["ln_cross_w"], "rms_cross")
    mn = _rms_fwd(mem, small["ln_mem_w"], "rms_mem")
    qc = call(_mm, "mm_q", hc, full["wq_cross"], mode="nn", out_dtype=F32)
    kvc = call(_mm, "mm_kv", mn, full["wkv_cross"], mode="nn", out_dtype=F32)
    oc = call(_cross_fwd, "cross_fwd", qc, kvc)
    x2 = call(_mm, "mm_o", oc, full["wo_cross"], mode="nn", out_dtype=F32, res=x1)

    hf = _rms_fwd(x2, small["ln_ffn_w"], "rms_ffn")
    ab, u = call(_ff_up, "mm_w13", hf, full["w13"])
    x3 = call(_mm, "mm_w2", u, lambda: full["w2"], mode="nn", out_dtype=F32, res=x2)

    dx3, dg_final, loss = _loss_head(x3, small["ln_final_w"], target, "loss_head")

    gs = {"ln_final_w": dg_final}
    call(_mm, "mm_dw2", u, dx3, mode="tn", out_dtype=BF16, grad_of="w2")
    dab = call(_ff_down_bwd, "mm_du", dx3, full["w2"], ab)
    call(_mm, "mm_dw13", hf, dab, mode="tn", out_dtype=BF16, grad_of="w13")
    dhf = call(_mm, "mm_dhf", dab, full["w13"], mode="nt", out_dtype=F32)
    dx2, gs["ln_ffn_w"] = _rms_bwd(x2, small["ln_ffn_w"], dhf, dx3, "rms_ffn_bwd")
    doc = call(_mm, "mm_doc", dx2, full["wo_cross"], mode="nt", out_dtype=BF16)
    call(_mm, "mm_dwo", oc, dx2, mode="tn", out_dtype=BF16, grad_of="wo_cross")
    dqc, dkvc = _cross_bwd(qc, kvc, doc, "cross_bwd")
    call(_mm, "mm_dwq", hc, dqc, mode="tn", out_dtype=BF16, grad_of="wq_cross")
    dhc = call(_mm, "mm_dhc", dqc, full["wq_cross"], mode="nt", out_dtype=F32)
    call(_mm, "mm_dwkv", mn, dkvc, mode="tn", out_dtype=BF16, grad_of="wkv_cross")
    dmn = call(_mm, "mm_dmn", dkvc, full["wkv_cross"], mode="nt", out_dtype=F32)
    _, gs["ln_mem_w"] = _rms_bwd(mem, small["ln_mem_w"], dmn, None, "rms_mem_bwd")
    dx1, gs["ln_cross_w"] = _rms_bwd(x1, small["ln_cross_w"], dhc, dx2, "rms_cross_bwd")
    dmerged = call(_mm, "mm_dmerged", dx1, full["w_out"], mode="nt", out_dtype=F32)
    call(_mm, "mm_dwout", merged, dx1, mode="tn", out_dtype=BF16, grad_of="w_out")
    dya, dyb, dga, dgb = call(_gate_bwd, "gate_bwd", proj, ya, yb, dmerged)
    call(_mm, "mm_dwa", o_att, dya, mode="tn", out_dtype=BF16, grad_of="w_branch_a")
    do_att = call(_mm, "mm_doatt", dya, full["w_branch_a"], mode="nt", out_dtype=F32)
    call(_mm, "mm_dwb", o_hg, dyb, mode="tn", out_dtype=BF16, grad_of="w_branch_b")
    do_hg = call(_mm, "mm_dohg", dyb, full["w_branch_b"], mode="nt", out_dtype=F32)
    dqh, dfh, dih, dgh, dlb, gs["hg_norm_w"] = call(
        _hg_bwd, "hgrn_bwd", proj, small["hg_lower_bounds"], small["hg_norm_w"], oraw, states, do_hg)
    gs["hg_lb"] = dlb
    do_gs, dl_gs = call(_attn_merge_bwd, "attn_merge_bwd", outs, lses, do_att)
    dqs, dks, dvs = zip(*[call(_attn_bwd, f"attn_bwd_g{g}", proj, g, lses[g], do_gs[g], dl_gs[g]) for g in range(3)])
    dproj = jnp.concatenate([*dqs, *dks, *dvs, dqh, dfh, dih, dgh, dga, dgb], axis=1)
    if net.comm:
        half = D_MODEL // 2
        c = lax.axis_index("c")
        h_sibling = lax.dynamic_slice_in_dim(h, (1 - c) * half, half, axis=1)
        h_own = lax.dynamic_slice_in_dim(h, c * half, half, axis=1)
        call(_mm, "mm_dwin_sibling", h_sibling, dproj, mode="tn", out_dtype=BF16, grad_of="w_in", sibling_half=True)
        call(_mm, "mm_dwin_own", h_own, dproj, mode="tn", out_dtype=BF16, grad_of="w_in")
    else:
        call(_mm, "mm_dwin", h, dproj, mode="tn", out_dtype=BF16, grad_of="w_in")
    dh = call(_mm, "mm_dh", dproj, full["w_in"], mode="nt", out_dtype=F32)
    dx, gs["ln_mix_w"] = _rms_bwd(x, small["ln_mix_w"], dh, dx1, "rms_mix_bwd")
    return loss, dx, gs


WEIGHT_ORDER = ("ln_mix_w", "w_in", "hg_norm_w", "hg_lower_bounds", "w_branch_a", "w_branch_b", "w_out",
                "ln_cross_w", "ln_mem_w", "wq_cross", "wkv_cross", "wo_cross", "ln_ffn_w", "w1", "w3", "w2",
                "ln_final_w")


def kernel(x, mem, ln_mix_w, w_in, hg_norm_w, hg_lower_bounds, w_branch_a, w_branch_b, w_out, ln_cross_w, ln_mem_w, wq_cross, wkv_cross, wo_cross, ln_ffn_w, w1, w3, w2, ln_final_w, loss_target, m_ln_mix_w, m_w_in, m_hg_norm_w, m_hg_lower_bounds, m_w_branch_a, m_w_branch_b, m_w_out, m_ln_cross_w, m_ln_mem_w, m_wq_cross, m_wkv_cross, m_wo_cross, m_ln_ffn_w, m_w1, m_w3, m_w2, m_ln_final_w, v_ln_mix_w, v_w_in, v_hg_norm_w, v_hg_lower_bounds, v_w_branch_a, v_w_branch_b, v_w_out, v_ln_cross_w, v_ln_mem_w, v_wq_cross, v_wkv_cross, v_wo_cross, v_ln_ffn_w, v_w1, v_w3, v_w2, v_ln_final_w):
    args = dict(locals())
    w = {n: args[n] for n in WEIGHT_ORDER}
    m = {n: args["m_" + n] for n in WEIGHT_ORDER}
    v = {n: args["v_" + n] for n in WEIGHT_ORDER}
    shapes = {n: w[n].shape for n in WEIGHT_ORDER}
    mat = lambda a: a.reshape(a.shape[-2:])
    shard_shapes = {n: shapes[n][-2:] for n in BIG_WEIGHTS}

    pos = _mesh_scalars()

    def cast(f, token=None):
        return _cast_into_full({n: mat(w[n]) for n in BIG_WEIGHTS if WEIGHT_PLACE[n][0] == f}, f, pos,
                               FULL_SPECS, WEIGHT_PLACE, f"cast_{f}", token)

    net = _Net({"w_in": cast("w_in")}, pos, shard_shapes)
    net.start([(f"ring_a{i}", (*job, "a")) for i, job in enumerate(W_IN_PIECES)], "ring", "gather_start_w_in")
    rest = {f: cast(f, net.token) for f in FULL_SPECS if f != "w_in"}
    net.full.update(rest)
    small = {n: w[n].reshape(1, -1) for n in SMALL_ROWS}
    small["hg_norm_w"] = w["hg_norm_w"].reshape(1, HEAD_DIM)
    small["hg_lower_bounds"] = w["hg_lower_bounds"]
    x2d = x.reshape(SEQ, D_MODEL)
    h = _rms_fwd(x2d, small["ln_mix_w"], "rms_mix")
    for i, job in enumerate(W_IN_PIECES):
        net.wait(f"ring_a{i}", after=[*rest.values(), h] if i == 0 else ())
        net.start([(f"ring_b{i}", (*job, "b"))], "ring", f"gather_pass_on_w_in{i}")
    net.start(GATHER_GROUPS, "gather", "gather_start_rest")
    for i, job in enumerate(W_IN_PIECES):
        net.wait(f"ring_b{i}")
        net.run_comm(("d2d", [job]), f"gather_hand_over_w_in{i}")
    loss, dx, gs = _local_step(net, x2d, h, mem.reshape(MEM_LEN, D_MODEL), loss_target.reshape(SEQ, D_MODEL), small)

    out_g, out_d, out_m, out_v = {}, {}, {}, {}
    net.last = dx
    for step in FINISH:
        if step[0] == "adamw":
            for n in step[1]:
                out_d[n], out_m[n], out_v[n], out_g[n] = net.call(_adamw, f"adamw_{n}", mat(w[n]), net.grads[n],
                                                                  mat(m[n]), mat(v[n]))
        elif step[0] == "wait":
            net.wait(step[1], after=list(out_d.values()))
        elif step[0] == "small":
            pad = lambda a: jnp.pad(a, ((0, 0), (0, D_MODEL - a.shape[1])))
            part = jnp.concatenate(
                [gs[n] for n in SMALL_ROWS]
                + [pad(jnp.concatenate([gs["hg_norm_w"][0], loss], axis=1)), pad(gs["hg_lb"]),
                   pad(gs["hg_norm_w"][1]) if len(gs["hg_norm_w"]) > 1 else jnp.zeros((1, D_MODEL), F32)], axis=0)
            part, net.psum["w_in"] = lax.optimization_barrier((part, net.psum["w_in"]))
            sg, sd, sm, sv, loss_tot = _small_update(_gather_rows(part), _pack_small(w), _pack_small(m),
                                                     _pack_small(v))
            for dst, packed in ((out_g, sg), (out_d, sd), (out_m, sm), (out_v, sv)):
                dst.update(_unpack_small(packed, shapes))
        else:
            net.step(step)

    result = [loss_tot[0, 0], dx.reshape(x.shape)]
    for group in (out_g, out_d, out_m, out_v):
        result += [group[n].reshape(shapes[n]) for n in WEIGHT_ORDER]
    return tuple(result)
```

```python
import math

import jax
import jax.numpy as jnp
from jax import lax
from jax.experimental import pallas as pl
from jax.experimental.pallas import tpu as pltpu

F32 = jnp.float32
BF16 = jnp.bfloat16
MESH = pl.DeviceIdType.MESH

D_MODEL = 2048
SEQ = 2048
HEAD_DIM = 128
MEM_LEN = 256
ATT_GROUPS = ((128, 1), (512, 4), (2048, 16))
ATT_HEADS = 4
ATT_WIDTH = 3 * ATT_HEADS * HEAD_DIM
ATT_OUT = ATT_HEADS * HEAD_DIM
ATT_BLOCK = 128
HG_HEADS = 8
HG_WIDTH = HG_HEADS * HEAD_DIM
HG_CHUNK = 64
IN_WIDTH = 3 * ATT_WIDTH + 4 * HG_WIDTH + 2 * D_MODEL
CROSS_HEADS = 4
CROSS_WIDTH = CROSS_HEADS * HEAD_DIM
D_FF = 5632
RMS_EPS = 1e-6
ADAM_LR = 0.001
ADAM_B1 = 0.9
ADAM_B2 = 0.999
ADAM_EPS = 1e-08
ADAM_WD = 0.01
ADAM_STEP = 10
N_CHIPS = 4
N_DEV = 8

VMEM_LIMIT_BYTES = 56 * 1024 * 1024
LANE = 128
MXU_WIDTH = 256
MM_TILE_CAP = 1536
TRANSPOSE_CHUNK = 512
ANY = pl.BlockSpec(memory_space=pl.ANY)


def _cparams(sem=None):
    return pltpu.CompilerParams(dimension_semantics=sem, vmem_limit_bytes=VMEM_LIMIT_BYTES)


def _div(n, cap, mult):
    best = None
    for d in range(mult, min(n, cap) + 1, mult):
        if n % d == 0:
            best = d
    assert best is not None, (n, cap, mult)
    return best


def _sigmoid(x):
    return 1.0 / (1.0 + jnp.exp(-x))


def _dot(a, b):
    return jnp.dot(a.astype(BF16), b.astype(BF16), preferred_element_type=F32)


def _dot_nt(a, b):
    return lax.dot_general(a.astype(BF16), b.astype(BF16), (((1,), (1,)), ((), ())),
                           preferred_element_type=F32)


def _dot_tn(a, b):
    return jnp.dot(a.astype(F32).T.astype(BF16), b.astype(BF16), preferred_element_type=F32)


def _dot_exact(a, b):
    return jnp.dot(a, b, precision=lax.Precision.HIGHEST, preferred_element_type=F32)


class _Carried:
    def __init__(self, arrays, fresh, n_sems, start, finish, mid=None, reads=None):
        self.arrays, self.fresh, self.n_sems, self.reads = arrays, fresh, n_sems, reads or {}
        self.start, self.mid, self.finish = start, mid, finish


class _Token:
    def __init__(self, array):
        self.array = array


TOKEN_SHAPE = (8, LANE)


def _carried_layout(carried):
    akeys = list(dict.fromkeys(k for cm in carried for k in cm.arrays))
    fkeys = [(ci, k) for ci, cm in enumerate(carried) for k in cm.fresh]
    arrays = [next(cm.arrays[k] for cm in carried if k in cm.arrays) for k in akeys]
    shapes = [jax.ShapeDtypeStruct(a.shape, a.dtype) for a in arrays] + [carried[ci].fresh[k] for ci, k in fkeys]
    sems = []
    for cm in carried:
        sems += [pltpu.SemaphoreType.DMA((cm.n_sems,)), pltpu.SemaphoreType.DMA((cm.n_sems,))]
    return akeys, fkeys, arrays, shapes, sems


def _carried_reads(carried):
    rkeys = list(dict.fromkeys(k for cm in carried for k in cm.reads))
    return rkeys, [next(cm.reads[k] for cm in carried if k in cm.reads) for k in rkeys]


def _carried_results(carried, akeys, fkeys, outs, rkeys=(), read_refs=()):
    shared = dict(zip(akeys, outs[:len(akeys)]))
    shared.update(zip(rkeys, read_refs))
    res = [{k: shared[k] for k in list(cm.arrays) + [r for r in cm.reads if r in shared]} for cm in carried]
    for (ci, k), o in zip(fkeys, outs[len(akeys):]):
        res[ci][k] = o
    return res


def _pcall(kern, *, name, grid, in_specs, out_specs, out_shape, args, scratch_shapes=(), semantics=None,
           carried=()):
    tokens = [c.array for c in carried if isinstance(c, _Token)]
    carried = [c for c in carried if not isinstance(c, _Token)]
    single = not isinstance(out_shape, (list, tuple))
    out_specs = [out_specs] if single else list(out_specs)
    out_shape = [out_shape] if single else list(out_shape)
    n_real, n_out, n_scr = len(in_specs), len(out_shape), len(scratch_shapes)
    in_specs = list(in_specs) + [pl.BlockSpec(TOKEN_SHAPE, lambda *_: (0, 0))] * len(tokens)
    args = list(args) + tokens
    n_in = len(in_specs)
    if not carried:
        def plain(*refs):
            kern(*refs[:n_real], *refs[n_in:])

        outs = pl.pallas_call(plain if tokens else kern, name=name, grid=grid, in_specs=in_specs,
                              out_specs=out_specs, out_shape=out_shape, scratch_shapes=list(scratch_shapes),
                              compiler_params=_cparams(semantics))(*args)
        return (outs[0] if single else list(outs)), []
    akeys, fkeys, arrays, shapes, sems = _carried_layout(carried)
    rkeys, reads = _carried_reads(carried)
    n_a, n_f, n_r = len(akeys), len(fkeys), len(rkeys)
    total = math.prod(grid)
    mid_step = min(total - 1, (17 * total) // 20)

    def wrapped(*refs):
        ins = refs[:n_real]
        r0 = n_in + n_a
        o0 = r0 + n_r
        outs = refs[o0:o0 + n_out]
        a0 = o0 + n_out
        s0 = a0 + n_a + n_f
        per = _carried_results(carried, akeys, fkeys, refs[a0:s0], rkeys, refs[r0:o0])
        scratch = refs[s0:s0 + n_scr]
        sem = refs[s0 + n_scr:]
        step = 0
        for d, g in enumerate(grid):
            step = step * g + pl.program_id(d)

        @pl.when(step == 0)
        def _():
            for ci, cm in enumerate(carried):
                cm.start(per[ci], sem[2 * ci], sem[2 * ci + 1])

        kern(*ins, *outs, *scratch)

        @pl.when(step == mid_step)
        def _():
            for ci, cm in enumerate(carried):
                if cm.mid is not None:
                    cm.mid(per[ci], sem[2 * ci], sem[2 * ci + 1])

        @pl.when(step == total - 1)
        def _():
            for ci, cm in enumerate(carried):
                cm.finish(per[ci], sem[2 * ci], sem[2 * ci + 1])

    outs = pl.pallas_call(
        wrapped, name=name, grid=grid,
        in_specs=list(in_specs) + [ANY] * (n_a + n_r), out_specs=out_specs + [ANY] * (n_a + n_f),
        out_shape=out_shape + shapes,
        input_output_aliases={n_in + i: n_out + i for i in range(n_a)},
        scratch_shapes=list(scratch_shapes) + sems,
        compiler_params=_cparams(("arbitrary",) * len(grid)),
    )(*args, *arrays, *reads)
    res = _carried_results(carried, akeys, fkeys, outs[n_out:])
    return (outs[0] if single else list(outs[:n_out])), res


def _run_comm(carried, name):
    carried = list(carried)
    akeys, fkeys, arrays, shapes, sems = _carried_layout(carried)
    rkeys, reads = _carried_reads(carried)
    n_a, n_f, n_r = len(akeys), len(fkeys), len(rkeys)

    def body(*refs):
        o0 = n_a + n_r
        per = _carried_results(carried, akeys, fkeys, refs[o0:o0 + n_a + n_f], rkeys, refs[n_a:o0])
        sem = refs[o0 + n_a + n_f:]
        for hook in ("start", "mid", "finish"):
            for ci, cm in enumerate(carried):
                fn = getattr(cm, hook)
                if fn is not None:
                    fn(per[ci], sem[2 * ci], sem[2 * ci + 1])

    outs = pl.pallas_call(
        body, name=name, in_specs=[ANY] * (n_a + n_r), out_specs=[ANY] * (n_a + n_f), out_shape=shapes,
        input_output_aliases={i: i for i in range(n_a)}, scratch_shapes=sems,
    )(*arrays, *reads)
    return _carried_results(carried, akeys, fkeys, outs)


HBM_SPEC = pl.BlockSpec(memory_space=pltpu.HBM)
SEM_SPEC = pl.BlockSpec(memory_space=pltpu.SEMAPHORE)
SPLIT_EFFECT = pltpu.SideEffectType.DATAFLOW_SIDE_EFFECTING


def _in_hbm(a):
    return pltpu.with_memory_space_constraint(a, pltpu.HBM)


def _split_start(items, name, after=None):
    items = list(items)
    akeys, fkeys, arrays, shapes, sems = _carried_layout(items)
    assert not fkeys
    n_a, n_s = len(akeys), len(sems)
    n_in = n_a + (after is not None)

    def body(*refs):
        per = _carried_results(items, akeys, [], refs[n_in:n_in + n_a])
        sem = refs[n_in + n_a:n_in + n_a + n_s]
        for ci, cm in enumerate(items):
            cm.start(per[ci], sem[2 * ci], sem[2 * ci + 1])
        token = refs[n_in + n_a + n_s]
        token[...] = jnp.zeros_like(token)

    outs = pl.pallas_call(
        body, name=name, in_specs=[HBM_SPEC] * n_a + [ANY] * (after is not None),
        out_specs=[HBM_SPEC] * n_a + [SEM_SPEC] * n_s + [pl.BlockSpec(memory_space=pltpu.VMEM)],
        out_shape=[pltpu.HBM(s.shape, s.dtype) for s in shapes] + sems + [jax.ShapeDtypeStruct(TOKEN_SHAPE, F32)],
        input_output_aliases={i: i for i in range(n_a)},
        compiler_params=pltpu.CompilerParams(has_side_effects=SPLIT_EFFECT),
    )(*[_in_hbm(a) for a in arrays], *([after] if after is not None else []))
    res = _carried_results(items, akeys, [], outs[:n_a])
    sem_out = outs[n_a:n_a + n_s]
    return res, [(sem_out[2 * ci], sem_out[2 * ci + 1]) for ci in range(len(items))], outs[-1]


def _split_wait(items, sems, after, name):
    items = list(items)
    after = list(after) if isinstance(after, (list, tuple)) else [after]
    akeys, fkeys, arrays, shapes, _ = _carried_layout(items)
    n_a, n_s = len(akeys), 2 * len(items)

    def body(*refs):
        per = _carried_results(items, akeys, [], refs[n_a + n_s + len(after):])
        sem = refs[n_a:n_a + n_s]
        for ci, cm in enumerate(items):
            cm.finish(per[ci], sem[2 * ci], sem[2 * ci + 1])

    outs = pl.pallas_call(
        body, name=name, in_specs=[HBM_SPEC] * n_a + [SEM_SPEC] * n_s + [ANY] * len(after),
        out_specs=[HBM_SPEC] * n_a, out_shape=[pltpu.HBM(s.shape, s.dtype) for s in shapes],
        input_output_aliases={i: i for i in range(n_a)},
        compiler_params=pltpu.CompilerParams(has_side_effects=SPLIT_EFFECT),
    )(*arrays, *[s for pair in sems for s in pair], *after)
    return _carried_results(items, akeys, [], outs)


def _mm(a, b, *, mode, out_dtype, name, res=None, carried=(), fused=None):
    if mode == "nn":
        (m, k), (k2, n) = a.shape, b.shape
    elif mode == "nt":
        (m, k), (n, k2) = a.shape, b.shape
    else:
        (k, m), (k2, n) = a.shape, b.shape
    assert k == k2, (name, a.shape, b.shape)
    tm = _div(m, MM_TILE_CAP, LANE)
    tn = _div(n, MM_TILE_CAP, MXU_WIDTH) if n % MXU_WIDTH == 0 else 0
    if tn < 1024:
        tn = _div(n, MM_TILE_CAP, LANE)
    out_shape = jax.ShapeDtypeStruct((m, n), out_dtype)

    if fused is not None:
        assert mode in ("nn", "nt") and res is None and k <= 2048
        tm, tn = fused["tile"]
        dot = _dot if mode == "nn" else _dot_nt
        n_x = len(fused["ins"])

        def kern_fused(*refs):
            part = dot(refs[0][...], refs[1][...])
            outs = fused["post"](part, *[r[...] for r in refs[2:2 + n_x]])
            for o_ref, val in zip(refs[2 + n_x:], outs):
                o_ref[...] = val.astype(o_ref.dtype)

        def tile_spec(shape, index=lambda i, j: (i, j)):
            return pl.BlockSpec(shape, lambda j, i: index(i, j))

        return _pcall(
            kern_fused, name=name, grid=(n // tn, m // tm),
            in_specs=[pl.BlockSpec((tm, k), lambda j, i: (i, 0)),
                      pl.BlockSpec((k, tn), lambda j, i: (0, j)) if mode == "nn"
                      else pl.BlockSpec((tn, k), lambda j, i: (j, 0))] + [tile_spec(*e[1:]) for e in fused["ins"]],
            out_specs=[tile_spec(s) for _, s in fused["outs"]], out_shape=[o for o, _ in fused["outs"]],
            args=(a, b, *[e[0] for e in fused["ins"]]), semantics=("parallel", "parallel"), carried=carried)

    if mode == "tn":
        assert res is None

        def kern_tn(a_ref, b_ref, o_ref, at_ref):
            @pl.when(pl.program_id(1) == 0)
            def _():
                step = min(TRANSPOSE_CHUNK, k)
                for c0 in range(0, k, step):
                    at_ref[:, c0:c0 + step] = a_ref[c0:c0 + step, :].astype(F32).T.astype(BF16)

            o_ref[...] = jnp.dot(at_ref[...], b_ref[...].astype(BF16),
                                 preferred_element_type=F32).astype(o_ref.dtype)

        return _pcall(
            kern_tn, name=name, grid=(m // tm, n // tn),
            in_specs=[pl.BlockSpec((k, tm), lambda i, j: (0, i)),
                      pl.BlockSpec((k, tn), lambda i, j: (0, j))],
            out_specs=pl.BlockSpec((tm, tn), lambda i, j: (i, j)),
            out_shape=out_shape, args=(a, b),
            scratch_shapes=[pltpu.VMEM((tm, k), BF16)],
            semantics=("parallel", "arbitrary"), carried=carried)

    tk = k if k <= 2048 else _div(k, 3072, LANE)
    nk = k // tk
    a_spec = pl.BlockSpec((tm, tk), lambda i, j, kk: (i, kk))
    if mode == "nn":
        b_spec = pl.BlockSpec((tk, tn), lambda i, j, kk: (kk, j))
        dot = _dot
    else:
        b_spec = pl.BlockSpec((tn, tk), lambda i, j, kk: (j, kk))
        dot = _dot_nt
    o_spec = pl.BlockSpec((tm, tn), lambda i, j, kk: (i, j))
    in_specs = [a_spec, b_spec]
    args = [a, b]
    if res is not None:
        in_specs.append(o_spec)
        args.append(res)
    has_res = res is not None

    def kern(*refs):
        a_ref, b_ref = refs[0], refs[1]
        r_ref = refs[2] if has_res else None
        o_ref = refs[3] if has_res else refs[2]
        part = dot(a_ref[...], b_ref[...])
        if nk == 1:
            if has_res:
                part = part + r_ref[...]
            o_ref[...] = part.astype(o_ref.dtype)
            return
        acc_ref = refs[-1]
        kk = pl.program_id(2)

        @pl.when(kk == 0)
        def _():
            acc_ref[...] = part

        @pl.when(kk > 0)
        def _():
            acc_ref[...] += part

        @pl.when(kk == nk - 1)
        def _():
            tot = acc_ref[...]
            if has_res:
                tot = tot + r_ref[...]
            o_ref[...] = tot.astype(o_ref.dtype)

    return _pcall(
        kern, name=name, grid=(m // tm, n // tn, nk),
        in_specs=in_specs, out_specs=o_spec, out_shape=out_shape, args=args,
        scratch_shapes=[pltpu.VMEM((tm, tn), F32)] if nk > 1 else [],
        semantics=("parallel", "parallel", "arbitrary"), carried=carried)


ROW_BLOCK = 512


def _rms_fwd(x, g, name):
    t, d = x.shape
    tr = min(ROW_BLOCK, t)

    def kern(x_ref, g_ref, o_ref):
        xf = x_ref[...]
        r = lax.rsqrt(jnp.mean(xf * xf, axis=-1, keepdims=True) + RMS_EPS)
        o_ref[...] = (xf * r * g_ref[...]).astype(o_ref.dtype)

    return pl.pallas_call(
        kern, name=name, grid=(t // tr,),
        in_specs=[pl.BlockSpec((tr, d), lambda i: (i, 0)), pl.BlockSpec((1, d), lambda i: (0, 0))],
        out_specs=pl.BlockSpec((tr, d), lambda i: (i, 0)),
        out_shape=jax.ShapeDtypeStruct((t, d), BF16),
        compiler_params=_cparams(("parallel",)),
    )(x, g)


def _rms_bwd(x, g, dh, res, name):
    t, d = x.shape
    tr = min(ROW_BLOCK, t)
    has_res = res is not None

    def kern(*refs):
        x_ref, g_ref, dh_ref = refs[:3]
        r_ref = refs[3] if has_res else None
        dx_ref, dg_ref = refs[-2], refs[-1]
        xf = x_ref[...]
        r = lax.rsqrt(jnp.mean(xf * xf, axis=-1, keepdims=True) + RMS_EPS)
        xn = xf * r
        dh_ = dh_ref[...]
        dhg = dh_ * g_ref[...]
        dx = r * (dhg - xn * jnp.mean(dhg * xn, axis=-1, keepdims=True))
        if has_res:
            dx = dx + r_ref[...]
        dx_ref[...] = dx
        part = jnp.sum(dh_ * xn, axis=0, keepdims=True)

        @pl.when(pl.program_id(0) == 0)
        def _():
            dg_ref[...] = part

        @pl.when(pl.program_id(0) > 0)
        def _():
            dg_ref[...] += part

    row = pl.BlockSpec((tr, d), lambda i: (i, 0))
    vec = pl.BlockSpec((1, d), lambda i: (0, 0))
    in_specs = [row, vec, row] + ([row] if has_res else [])
    args = [x, g, dh] + ([res] if has_res else [])
    return pl.pallas_call(
        kern, name=name, grid=(t // tr,), in_specs=in_specs, out_specs=[row, vec],
        out_shape=[jax.ShapeDtypeStruct((t, d), F32), jax.ShapeDtypeStruct((1, d), F32)],
        compiler_params=_cparams(("arbitrary",)),
    )(*args)


def _loss_head(x3, g, target, name):
    t, d = x3.shape
    tr = ROW_BLOCK

    def kern(x_ref, g_ref, t_ref, dx_ref, dg_ref, loss_ref):
        xf = x_ref[...]
        r = lax.rsqrt(jnp.mean(xf * xf, axis=-1, keepdims=True) + RMS_EPS)
        xn = xf * r
        gg = g_ref[...]
        err = xn * gg - t_ref[...]
        lpart = 0.5 * jnp.sum(jnp.mean(err * err, axis=-1, keepdims=True), axis=0, keepdims=True)
        dy = err * (1.0 / d)
        dyg = dy * gg
        dx_ref[...] = r * (dyg - xn * jnp.mean(dyg * xn, axis=-1, keepdims=True))
        gpart = jnp.sum(dy * xn, axis=0, keepdims=True)
        lrow = jnp.broadcast_to(lpart, (1, LANE))

        @pl.when(pl.program_id(0) == 0)
        def _():
            dg_ref[...] = gpart
            loss_ref[...] = lrow

        @pl.when(pl.program_id(0) > 0)
        def _():
            dg_ref[...] += gpart
            loss_ref[...] += lrow

    row = pl.BlockSpec((tr, d), lambda i: (i, 0))
    vec = pl.BlockSpec((1, d), lambda i: (0, 0))
    return pl.pallas_call(
        kern, name=name, grid=(t // tr,), in_specs=[row, vec, row],
        out_specs=[row, vec, pl.BlockSpec((1, LANE), lambda i: (0, 0))],
        out_shape=[jax.ShapeDtypeStruct((t, d), F32), jax.ShapeDtypeStruct((1, d), F32),
                   jax.ShapeDtypeStruct((1, LANE), F32)],
        compiler_params=_cparams(("arbitrary",)),
    )(x3, g, target)


ATT_SCALE = HEAD_DIM ** -0.5
Q_BLOCK0, K_BLOCK0, V_BLOCK0 = 0, ATT_WIDTH // HEAD_DIM, 2 * ATT_WIDTH // HEAD_DIM


def _residue_rows(dil, r, n):
    if dil == 1:
        return pl.ds(n * ATT_BLOCK, ATT_BLOCK)
    return pl.ds(n * ATT_BLOCK * dil + r, ATT_BLOCK, stride=dil)


def _band_mask(with_prev):
    width = 2 * ATT_BLOCK if with_prev else ATT_BLOCK
    iq = lax.broadcasted_iota(jnp.int32, (ATT_BLOCK, width), 0)
    ik = lax.broadcasted_iota(jnp.int32, (ATT_BLOCK, width), 1)
    if not with_prev:
        return ik <= iq
    return ((ik < ATT_BLOCK) & (iq <= ik)) | ((ik >= ATT_BLOCK) & ((ik - ATT_BLOCK) <= iq))


def _band_keys(ref, dil, r, n):
    own = ref[_residue_rows(dil, r, n), :]
    if n == 0:
        return own
    return jnp.concatenate([ref[_residue_rows(dil, r, n - 1), :], own], axis=0)


def _attn_col_spec(base, grp):
    return pl.BlockSpec((SEQ, HEAD_DIM), lambda h: (0, base + grp * ATT_HEADS + h))


def _attn_fwd(proj, grp, name, carried=()):
    _, dil = ATT_GROUPS[grp]
    nb = SEQ // dil // ATT_BLOCK

    def kern(q_ref, k_ref, v_ref, o_ref, lse_ref):
        for r in range(dil):
            for n in range(nb):
                rows = _residue_rows(dil, r, n)
                s = _dot_nt(q_ref[rows, :], _band_keys(k_ref, dil, r, n)) * ATT_SCALE
                s = jnp.where(_band_mask(n > 0), s, -jnp.inf)
                m = jnp.max(s, axis=-1, keepdims=True)
                p = jnp.exp(s - m)
                l = jnp.sum(p, axis=-1, keepdims=True)
                o_ref[rows, :] = _dot(p / l, _band_keys(v_ref, dil, r, n))
                lse_ref[rows, :] = jnp.broadcast_to(m + jnp.log(l), (ATT_BLOCK, HEAD_DIM))

    out_spec = pl.BlockSpec((SEQ, HEAD_DIM), lambda h: (0, h))
    return _pcall(
        kern, name=name, grid=(ATT_HEADS,),
        in_specs=[_attn_col_spec(Q_BLOCK0, grp), _attn_col_spec(K_BLOCK0, grp), _attn_col_spec(V_BLOCK0, grp)],
        out_specs=[out_spec, out_spec],
        out_shape=[jax.ShapeDtypeStruct((SEQ, ATT_OUT), F32)] * 2, args=(proj, proj, proj),
        semantics=("parallel",), carried=carried)


def _attn_weights(l0, l1, l2):
    mx = jnp.maximum(jnp.maximum(l0, l1), l2)
    e0, e1, e2 = jnp.exp(l0 - mx), jnp.exp(l1 - mx), jnp.exp(l2 - mx)
    den = e0 + e1 + e2
    return e0 / den, e1 / den, e2 / den


def _attn_merge_fwd(outs, lses, name):
    tr = ROW_BLOCK

    def kern(o0, o1, o2, l0, l1, l2, out_ref):
        a0, a1, a2 = _attn_weights(l0[...], l1[...], l2[...])
        out_ref[...] = (a0 * o0[...] + a1 * o1[...] + a2 * o2[...]).astype(out_ref.dtype)

    spec = pl.BlockSpec((tr, ATT_OUT), lambda i: (i, 0))
    return pl.pallas_call(
        kern, name=name, grid=(SEQ // tr,), in_specs=[spec] * 6, out_specs=spec,
        out_shape=jax.ShapeDtypeStruct((SEQ, ATT_OUT), BF16),
        compiler_params=_cparams(("parallel",)),
    )(*outs, *lses)


def _attn_merge_bwd(outs, lses, do_att, name, carried=()):
    tr = ROW_BLOCK

    def kern(o0, o1, o2, l0, l1, l2, do_ref, d0, d1, d2, t0, t1, t2):
        alphas = _attn_weights(l0[...], l1[...], l2[...])
        do = do_ref[...]
        o_att = alphas[0] * o0[...] + alphas[1] * o1[...] + alphas[2] * o2[...]
        prod = do * o_att
        parts = []
        for h in range(ATT_HEADS):
            sl = slice(h * HEAD_DIM, (h + 1) * HEAD_DIM)
            tot = jnp.sum(prod[:, sl], axis=-1, keepdims=True)
            parts.append(jnp.broadcast_to(tot, (tr, HEAD_DIM)))
        dd = jnp.concatenate(parts, axis=1)
        for a, d_ref, t_ref in zip(alphas, (d0, d1, d2), (t0, t1, t2)):
            d_ref[...] = a * do
            t_ref[...] = -a * dd

    spec = pl.BlockSpec((tr, ATT_OUT), lambda i: (i, 0))
    res, cres = _pcall(
        kern, name=name, grid=(SEQ // tr,), in_specs=[spec] * 7, out_specs=[spec] * 6,
        out_shape=[jax.ShapeDtypeStruct((SEQ, ATT_OUT), F32)] * 6, args=(*outs, *lses, do_att),
        semantics=("parallel",), carried=carried)
    return (res[:3], res[3:]), cres


def _attn_bwd(proj, grp, lse, do_g, dl_g, name, carried=()):
    _, dil = ATT_GROUPS[grp]
    nb = SEQ // dil // ATT_BLOCK

    def kern(q_ref, k_ref, v_ref, do_ref, lse_ref, dl_ref, dq_ref, dk_ref, dv_ref, dq_acc, dk_acc, dv_acc):
        dk_acc[...] = jnp.zeros_like(dk_acc)
        dv_acc[...] = jnp.zeros_like(dv_acc)
        for r in range(dil):
            for n in range(nb):
                rows = _residue_rows(dil, r, n)
                q, do = q_ref[rows, :], do_ref[rows, :]
                kk, vv = _band_keys(k_ref, dil, r, n), _band_keys(v_ref, dil, r, n)
                s = _dot_nt(q, kk) * ATT_SCALE
                p = jnp.where(_band_mask(n > 0), jnp.exp(s - lse_ref[rows, :][:, :1]), 0.0)
                ds = p * (_dot_nt(do, vv) + dl_ref[rows, :][:, :1])
                dq_acc[rows, :] = _dot(ds, kk) * ATT_SCALE
                dk = _dot_tn(ds, q) * ATT_SCALE
                dv = _dot_tn(p, do)
                if n > 0:
                    prev = _residue_rows(dil, r, n - 1)
                    dk_acc[prev, :] += dk[:ATT_BLOCK]
                    dv_acc[prev, :] += dv[:ATT_BLOCK]
                    dk, dv = dk[ATT_BLOCK:], dv[ATT_BLOCK:]
                dk_acc[rows, :] += dk
                dv_acc[rows, :] += dv
        dq_ref[...] = dq_acc[...].astype(dq_ref.dtype)
        dk_ref[...] = dk_acc[...].astype(dk_ref.dtype)
        dv_ref[...] = dv_acc[...].astype(dv_ref.dtype)

    spec = pl.BlockSpec((SEQ, HEAD_DIM), lambda h: (0, h))
    return _pcall(
        kern, name=name, grid=(ATT_HEADS,),
        in_specs=[_attn_col_spec(Q_BLOCK0, grp), _attn_col_spec(K_BLOCK0, grp), _attn_col_spec(V_BLOCK0, grp),
                  spec, spec, spec],
        out_specs=[spec] * 3,
        out_shape=[jax.ShapeDtypeStruct((SEQ, ATT_OUT), BF16)] * 3, args=(proj, proj, proj, do_g, lse, dl_g),
        scratch_shapes=[pltpu.VMEM((SEQ, HEAD_DIM), F32)] * 3,
        semantics=("parallel",), carried=carried)


HG_HEADS_PER_STEP = 8
HG_BLOCK_W = 4 * HEAD_DIM
HG_BLOCKS = HG_HEADS_PER_STEP * HEAD_DIM // HG_BLOCK_W
HG_STEP_W = HG_HEADS_PER_STEP * HEAD_DIM
HG_Q_BLK = (3 * ATT_WIDTH) // HG_BLOCK_W
HG_N_CHUNKS = SEQ // HG_CHUNK
HG_MID = HG_CHUNK // 2


def _lower_bound(lb_ref, sl):
    l0, l1 = lb_ref[0:1, sl], lb_ref[1:2, sl]
    mx = jnp.maximum(l0, l1)
    e0, e1 = jnp.exp(l0 - mx), jnp.exp(l1 - mx)
    return e0 / (e0 + e1)


def _tri(lower):
    i = lax.broadcasted_iota(jnp.int32, (HG_CHUNK, HG_CHUNK), 0)
    j = lax.broadcasted_iota(jnp.int32, (HG_CHUNK, HG_CHUNK), 1)
    return (i >= j) if lower else (i <= j)


def _head_mean(x):
    parts = []
    for hd in range(x.shape[1] // HEAD_DIM):
        m = jnp.mean(x[:, hd * HEAD_DIM:(hd + 1) * HEAD_DIM], axis=-1, keepdims=True)
        parts.append(jnp.broadcast_to(m, (x.shape[0], HEAD_DIM)))
    return jnp.concatenate(parts, axis=1)


def _hg_chunk_terms(qh, fh, lb):
    sig = _sigmoid(fh)
    f = lb + (1.0 - lb) * sig
    k = 1.0 - f
    b = _dot_exact(_tri(True).astype(F32), jnp.log(f))
    bl = b[HG_CHUNK - 1:HG_CHUNK, :]
    br = b[HG_MID:HG_MID + 1, :]
    sq = _sigmoid(qh)
    q = qh * sq
    return dict(sig=sig, f=f, k=k, b=b, bl=bl, br=br, sq=sq, q=q,
                e1=jnp.exp(bl - b), e2=jnp.exp(b), e3=jnp.exp(b - br), e4=jnp.exp(br - b))


def _hg_fwd(proj, lbw, normw, name, carried=()):
    def in_blks(off):
        return [pl.BlockSpec((HG_CHUNK, HG_BLOCK_W), lambda hp, n, b=b: (n, HG_Q_BLK + off + hp * HG_BLOCKS + b))
                for b in range(HG_BLOCKS)]

    def kern(*refs):
        q_refs, f_refs, i_refs, g_refs = (refs[k * HG_BLOCKS:(k + 1) * HG_BLOCKS] for k in range(4))
        lb_ref, nw_ref, oraw_ref, ohg_ref, st_ref, state = refs[4 * HG_BLOCKS:]

        @pl.when(pl.program_id(1) == 0)
        def _():
            state[...] = jnp.zeros_like(state)

        causal = _tri(True)
        wide = lambda rs: jnp.concatenate([r[...] for r in rs], axis=1)
        t = _hg_chunk_terms(wide(q_refs), wide(f_refs), _lower_bound(lb_ref, slice(None)))
        v, gh = wide(i_refs), wide(g_refs)
        kd, qb, qr, kr = t["k"] * t["e1"], t["q"] * t["e2"], t["q"] * t["e3"], t["k"] * t["e4"]
        decay = jnp.exp(t["bl"])
        outs = []
        for hd in range(HG_HEADS_PER_STEP):
            sl = slice(hd * HEAD_DIM, (hd + 1) * HEAD_DIM)
            st = state[hd]
            st_ref[0, hd] = st
            a = jnp.where(causal, _dot_nt(qr[:, sl], kr[:, sl]), 0.0)
            outs.append(_dot_nt(qb[:, sl], st) + _dot(a, v[:, sl]))
            state[hd] = st * decay[:, sl] + _dot_tn(v[:, sl], kd[:, sl])
        o = jnp.concatenate(outs, axis=1)
        oraw_ref[...] = o
        r = lax.rsqrt(_head_mean(o * o) + RMS_EPS)
        nw = jnp.tile(nw_ref[...], (1, HG_HEADS_PER_STEP))
        ohg_ref[...] = (o * r * nw * (gh * _sigmoid(gh))).astype(ohg_ref.dtype)

    out_blk = pl.BlockSpec((HG_CHUNK, HG_STEP_W), lambda hp, n: (n, hp))
    return _pcall(
        kern, name=name, grid=(HG_HEADS // HG_HEADS_PER_STEP, HG_N_CHUNKS),
        in_specs=[*in_blks(0), *in_blks(2), *in_blks(4), *in_blks(6),
                  pl.BlockSpec((2, HG_STEP_W), lambda hp, n: (0, hp)),
                  pl.BlockSpec((1, HEAD_DIM), lambda hp, n: (0, 0))],
        out_specs=[out_blk, out_blk,
                   pl.BlockSpec((1, HG_HEADS_PER_STEP, HEAD_DIM, HEAD_DIM), lambda hp, n: (n, hp, 0, 0))],
        out_shape=[jax.ShapeDtypeStruct((SEQ, HG_WIDTH), F32), jax.ShapeDtypeStruct((SEQ, HG_WIDTH), BF16),
                   jax.ShapeDtypeStruct((HG_N_CHUNKS, HG_HEADS, HEAD_DIM, HEAD_DIM), F32)],
        args=(*[proj] * (4 * HG_BLOCKS), lbw, normw),
        scratch_shapes=[pltpu.VMEM((HG_HEADS_PER_STEP, HEAD_DIM, HEAD_DIM), F32)],
        semantics=("parallel", "arbitrary"), carried=carried)


def _hg_bwd(proj, lbw, normw, oraw, states, do_hg, name, carried=()):
    last = HG_N_CHUNKS - 1

    def in_blks(off):
        return [pl.BlockSpec((HG_CHUNK, HG_BLOCK_W),
                             lambda hp, n, b=b: (last - n, HG_Q_BLK + off + hp * HG_BLOCKS + b))
                for b in range(HG_BLOCKS)]

    blk = pl.BlockSpec((HG_CHUNK, HG_STEP_W), lambda hp, n: (last - n, hp))

    def kern(*refs):
        q_refs, f_refs, i_refs, g_refs = (refs[k * HG_BLOCKS:(k + 1) * HG_BLOCKS] for k in range(4))
        (lb_ref, nw_ref, oraw_ref, st_ref, do_ref, dq_ref, df_ref, di_ref, dg_ref, dlb_ref, dnw_ref,
         dstate) = refs[4 * HG_BLOCKS:]
        first = pl.program_id(1) == 0

        @pl.when(first)
        def _():
            dstate[...] = jnp.zeros_like(dstate)

        causal = _tri(True)
        wide = lambda rs: jnp.concatenate([r[...] for r in rs], axis=1)
        cat = lambda parts: jnp.concatenate(parts, axis=1)
        qh, fh, v, gh = wide(q_refs), wide(f_refs), wide(i_refs), wide(g_refs)
        o, dout = oraw_ref[...], do_ref[...]
        nw = jnp.tile(nw_ref[...], (1, HG_HEADS_PER_STEP))
        sgg = _sigmoid(gh)
        r = lax.rsqrt(_head_mean(o * o) + RMS_EPS)
        xn = o * r
        dg_ref[...] = (dout * xn * nw * (sgg * (1.0 + gh * (1.0 - sgg)))).astype(dg_ref.dtype)
        don = dout * (gh * sgg)
        dnw_wide = jnp.sum(don * xn, axis=0, keepdims=True)
        dnw_tot = dnw_wide[:, :HEAD_DIM]
        for hd in range(1, HG_HEADS_PER_STEP):
            dnw_tot = dnw_tot + dnw_wide[:, hd * HEAD_DIM:(hd + 1) * HEAD_DIM]
        tt = don * nw
        do = r * (tt - xn * _head_mean(tt * xn))
        lb = _lower_bound(lb_ref, slice(None))
        t = _hg_chunk_terms(qh, fh, lb)
        k, q = t["k"], t["q"]
        kd, qb, qr, kr = k * t["e1"], q * t["e2"], q * t["e3"], k * t["e4"]
        decay = jnp.exp(t["bl"])
        dqb, dqr, dkr, dkd, dv, ddecay = [], [], [], [], [], []
        for hd in range(HG_HEADS_PER_STEP):
            sl = slice(hd * HEAD_DIM, (hd + 1) * HEAD_DIM)
            st = st_ref[0, hd]
            dstn = dstate[hd]
            a = jnp.where(causal, _dot_nt(qr[:, sl], kr[:, sl]), 0.0)
            da = jnp.where(causal, _dot_nt(do[:, sl], v[:, sl]), 0.0)
            dqb.append(_dot(do[:, sl], st))
            dv.append(_dot_tn(a, do[:, sl]) + _dot_nt(kd[:, sl], dstn))
            dqr.append(_dot(da, kr[:, sl]))
            dkr.append(_dot_tn(da, qr[:, sl]))
            dkd.append(_dot(v[:, sl], dstn))
            ddecay.append(jnp.sum(dstn * st, axis=0, keepdims=True))
            dstate[hd] = dstn * decay[:, sl] + _dot_tn(do[:, sl], qb[:, sl])
        dqb, dqr, dkr, dkd, dv, ddecay = cat(dqb), cat(dqr), cat(dkr), cat(dkd), cat(dv), cat(ddecay)
        dq = dqb * t["e2"] + dqr * t["e3"]
        dk = dkd * t["e1"] + dkr * t["e4"]
        db = dqb * qb + dqr * qr - dkr * kr - dkd * kd
        dbl = jnp.sum(dkd * kd, axis=0, keepdims=True) + ddecay * decay
        dbr = jnp.sum(dkr * kr - dqr * qr, axis=0, keepdims=True)
        rows = lax.broadcasted_iota(jnp.int32, db.shape, 0)
        dlf = _dot_exact(_tri(False).astype(F32), db) + dbl + jnp.where(rows <= HG_MID, dbr, 0.0)
        df = dlf / t["f"] - dk
        sig, sq = t["sig"], t["sq"]
        df_ref[...] = (df * (1.0 - lb) * sig * (1.0 - sig)).astype(df_ref.dtype)
        dlb_row = jnp.sum(df * (1.0 - sig), axis=0, keepdims=True)
        dq_ref[...] = (dq * (sq * (1.0 + qh * (1.0 - sq)))).astype(dq_ref.dtype)
        di_ref[...] = dv.astype(di_ref.dtype)
        dnw_blk = jnp.broadcast_to(dnw_tot, (8, HEAD_DIM))

        @pl.when(first)
        def _():
            dlb_ref[...] = dlb_row
            dnw_ref[...] = dnw_blk

        @pl.when(jnp.logical_not(first))
        def _():
            dlb_ref[...] += dlb_row
            dnw_ref[...] += dnw_blk

    n_hp = HG_HEADS // HG_HEADS_PER_STEP
    outs, cres = _pcall(
        kern, name=name, grid=(n_hp, HG_N_CHUNKS),
        in_specs=[*in_blks(0), *in_blks(2), *in_blks(4), *in_blks(6),
                  pl.BlockSpec((2, HG_STEP_W), lambda hp, n: (0, hp)),
                  pl.BlockSpec((1, HEAD_DIM), lambda hp, n: (0, 0)),
                  blk,
                  pl.BlockSpec((1, HG_HEADS_PER_STEP, HEAD_DIM, HEAD_DIM), lambda hp, n: (last - n, hp, 0, 0)),
                  blk],
        out_specs=[blk, blk, blk, blk,
                   pl.BlockSpec((1, HG_STEP_W), lambda hp, n: (0, hp)),
                   pl.BlockSpec((8, HEAD_DIM), lambda hp, n: (hp, 0))],
        out_shape=[jax.ShapeDtypeStruct((SEQ, HG_WIDTH), BF16)] * 4
        + [jax.ShapeDtypeStruct((1, HG_WIDTH), F32), jax.ShapeDtypeStruct((8 * n_hp, HEAD_DIM), F32)],
        args=(*[proj] * (4 * HG_BLOCKS), lbw, normw, oraw, states, do_hg),
        scratch_shapes=[pltpu.VMEM((HG_HEADS_PER_STEP, HEAD_DIM, HEAD_DIM), F32)],
        semantics=("parallel", "arbitrary"), carried=carried)
    dqh, dfh, dih, dgh, dlb, dnw = outs
    return (dqh, dfh, dih, dgh, dlb, [dnw[8 * i:8 * i + 1] for i in range(n_hp)]), cres


GATE_BLOCK_W = 512
GATE_A_BLK = (3 * ATT_WIDTH + 4 * HG_WIDTH) // GATE_BLOCK_W
GATE_B_BLK = GATE_A_BLK + D_MODEL // GATE_BLOCK_W


GATE_TILE = (1024, GATE_BLOCK_W)


def _gate_ins(proj, ya, yb=None):
    ins = [(proj, GATE_TILE, lambda i, j: (i, GATE_A_BLK + j)), (proj, GATE_TILE, lambda i, j: (i, GATE_B_BLK + j)),
           (ya, GATE_TILE)]
    return ins + ([(yb, GATE_TILE)] if yb is not None else [])


def _branch_b_gate(o_hg, w_b, proj, ya, name, carried=()):
    def post(yb, ga, gb, ya_):
        return yb, _sigmoid(ga) * ya_ + _sigmoid(gb) * yb

    return _mm(o_hg, w_b, mode="nn", out_dtype=F32, name=name, carried=carried, fused=dict(
        tile=GATE_TILE, ins=_gate_ins(proj, ya), post=post,
        outs=[(jax.ShapeDtypeStruct((SEQ, D_MODEL), F32), GATE_TILE), (jax.ShapeDtypeStruct((SEQ, D_MODEL), BF16), GATE_TILE)]))


def _dmerged_gate_bwd(dx1, w_out, proj, ya, yb, name, carried=()):
    def post(dm, ga, gb, ya_, yb_):
        sa, sb = _sigmoid(ga), _sigmoid(gb)
        return dm * sa, dm * sb, dm * ya_ * sa * (1.0 - sa), dm * yb_ * sb * (1.0 - sb)

    return _mm(dx1, w_out, mode="nt", out_dtype=BF16, name=name, carried=carried, fused=dict(
        tile=GATE_TILE, ins=_gate_ins(proj, ya, yb), post=post,
        outs=[(jax.ShapeDtypeStruct((SEQ, D_MODEL), BF16), GATE_TILE)] * 4))


NORM_TILE = (512, D_MODEL)


def _residual_norm(a, w, x, g, name, carried=()):
    def post(part, x_, g_):
        xn = part + x_
        r = lax.rsqrt(jnp.mean(xn * xn, axis=-1, keepdims=True) + RMS_EPS)
        return xn, xn * r * g_

    return _mm(a, w, mode="nn", out_dtype=F32, name=name, carried=carried, fused=dict(
        tile=NORM_TILE, ins=[(x, NORM_TILE), (g, (1, D_MODEL), lambda i, j: (0, j))], post=post,
        outs=[(jax.ShapeDtypeStruct((SEQ, D_MODEL), F32), NORM_TILE), (jax.ShapeDtypeStruct((SEQ, D_MODEL), BF16), NORM_TILE)]))


FF_SHARD = D_FF // N_CHIPS


FF_TILE_ROWS = 512


def _swiglu_tile(ab):
    a, b = ab[:, :FF_SHARD], ab[:, FF_SHARD:]
    return a * _sigmoid(a) * b


def _swiglu_grad_tile(du, ab):
    a, b = ab[:, :FF_SHARD], ab[:, FF_SHARD:]
    sg = _sigmoid(a)
    return jnp.concatenate([du * b * (sg * (1.0 + a * (1.0 - sg))), du * (a * sg)], axis=1)


def _ff_up(hf, w13, name, carried=()):
    wide, narrow = (FF_TILE_ROWS, 2 * FF_SHARD), (FF_TILE_ROWS, FF_SHARD)
    return _mm(hf, w13, mode="nn", out_dtype=F32, name=name, carried=carried, fused=dict(
        tile=wide, ins=[],
        outs=[(jax.ShapeDtypeStruct((SEQ, 2 * D_FF), F32), wide), (jax.ShapeDtypeStruct((SEQ, D_FF), BF16), narrow)],
        post=lambda p: (p, _swiglu_tile(p))))


def _ff_down_bwd(dx3, w2, ab, name, carried=()):
    wide, narrow = (FF_TILE_ROWS, 2 * FF_SHARD), (FF_TILE_ROWS, FF_SHARD)
    out, res = _mm(dx3, w2, mode="nt", out_dtype=BF16, name=name, carried=carried, fused=dict(
        tile=narrow, ins=[(ab, wide)], outs=[(jax.ShapeDtypeStruct((SEQ, 2 * D_FF), BF16), wide)],
        post=lambda du, ab_: (_swiglu_grad_tile(du, ab_),)))
    return out[0], res


CROSS_ROWS = 512


def _cross_fwd(qc, kvc, name, carried=()):
    def kern(q_ref, k_ref, v_ref, o_ref):
        s = _dot_nt(q_ref[...], k_ref[...]) * ATT_SCALE
        m = jnp.max(s, axis=-1, keepdims=True)
        e = jnp.exp(s - m)
        p = e / jnp.sum(e, axis=-1, keepdims=True)
        o_ref[...] = _dot(p, v_ref[...]).astype(o_ref.dtype)

    qblk = pl.BlockSpec((CROSS_ROWS, HEAD_DIM), lambda h, i: (i, h))
    return _pcall(
        kern, name=name, grid=(CROSS_HEADS, SEQ // CROSS_ROWS),
        in_specs=[qblk, pl.BlockSpec((MEM_LEN, HEAD_DIM), lambda h, i: (0, h)),
                  pl.BlockSpec((MEM_LEN, HEAD_DIM), lambda h, i: (0, CROSS_HEADS + h))],
        out_specs=qblk, out_shape=jax.ShapeDtypeStruct((SEQ, CROSS_WIDTH), BF16), args=(qc, kvc, kvc),
        semantics=("parallel", "parallel"), carried=carried)


def _cross_bwd(qc, kvc, doc, name):
    def kern(q_ref, k_ref, v_ref, do_ref, dq_ref, dk_ref, dv_ref):
        q, k, v, do = q_ref[...], k_ref[...], v_ref[...], do_ref[...]
        s = _dot_nt(q, k) * ATT_SCALE
        m = jnp.max(s, axis=-1, keepdims=True)
        e = jnp.exp(s - m)
        p = e / jnp.sum(e, axis=-1, keepdims=True)
        dp = _dot_nt(do, v)
        ds = p * (dp - jnp.sum(dp * p, axis=-1, keepdims=True))
        dq_ref[...] = (_dot(ds, k) * ATT_SCALE).astype(dq_ref.dtype)
        dk = _dot_tn(ds, q) * ATT_SCALE
        dv = _dot_tn(p, do)

        @pl.when(pl.program_id(1) == 0)
        def _():
            dk_ref[...] = dk
            dv_ref[...] = dv

        @pl.when(pl.program_id(1) > 0)
        def _():
            dk_ref[...] += dk
            dv_ref[...] += dv

    qblk = pl.BlockSpec((CROSS_ROWS, HEAD_DIM), lambda h, i: (i, h))
    kblk = pl.BlockSpec((MEM_LEN, HEAD_DIM), lambda h, i: (0, h))
    dq, dk, dv = pl.pallas_call(
        kern, name=name, grid=(CROSS_HEADS, SEQ // CROSS_ROWS),
        in_specs=[qblk, kblk, pl.BlockSpec((MEM_LEN, HEAD_DIM), lambda h, i: (0, CROSS_HEADS + h)), qblk],
        out_specs=[qblk, kblk, kblk],
        out_shape=[jax.ShapeDtypeStruct((SEQ, CROSS_WIDTH), BF16),
                   jax.ShapeDtypeStruct((MEM_LEN, CROSS_WIDTH), F32),
                   jax.ShapeDtypeStruct((MEM_LEN, CROSS_WIDTH), F32)],
        compiler_params=_cparams(("parallel", "arbitrary")),
    )(qc, kvc, kvc, doc)
    return dq, jnp.concatenate([dk, dv], axis=1)


FULL_SPECS = {
    "w_in": ("col", D_MODEL, IN_WIDTH),
    "w_branch_a": ("col", ATT_OUT, D_MODEL),
    "w_branch_b": ("col", HG_WIDTH, D_MODEL),
    "w_out": ("row", D_MODEL, D_MODEL),
    "wq_cross": ("row", D_MODEL, CROSS_WIDTH),
    "wkv_cross": ("row", D_MODEL, 2 * CROSS_WIDTH),
    "wo_cross": ("col", CROSS_WIDTH, D_MODEL),
    "w13": ("col", D_MODEL, 2 * D_FF),
    "w2": ("row", D_FF, D_MODEL),
}
WEIGHT_PLACE = {
    "w_in": ("w_in", 0), "w_branch_a": ("w_branch_a", 0), "w_branch_b": ("w_branch_b", 0),
    "w_out": ("w_out", 0), "wq_cross": ("wq_cross", 0), "wkv_cross": ("wkv_cross", 0),
    "wo_cross": ("wo_cross", 0), "w1": ("w13", 0), "w3": ("w13", FF_SHARD), "w2": ("w2", 0),
}
BIG_WEIGHTS = tuple(WEIGHT_PLACE)
EW_BLOCK_ELEMS = 512 * 1024


def _position():
    return lax.axis_index("x"), lax.axis_index("y"), lax.axis_index("c")


def _other_chips(x, y):
    return [(1 - x, y), (x, 1 - y), (1 - x, 1 - y)]


def _half(ref, kind, h):
    r, c = ref.shape
    if kind == "col":
        return ref.at[pl.ds(h * (r // 2), r // 2), :]
    return ref.at[:, pl.ds(h * (c // 2), c // 2)]


def _shard_of(ref, kind, start, size):
    return ref.at[:, pl.ds(start, size)] if kind == "col" else ref.at[pl.ds(start, size), :]


def _rows_of(ref, r0, nrows):
    return ref if nrows is None else ref.at[pl.ds(r0, nrows), :]


def _half_shape(kind, rows, cols):
    return (rows // 2, cols) if kind == "col" else (rows, cols // 2)


def _slot_shape(spec):
    kind, rows, cols = spec
    hr, hc = _half_shape(kind, rows, cols)
    return (hr, hc // N_CHIPS) if kind == "col" else (hr // N_CHIPS, hc)


def _remote(src, dst, send_sem, recv_sem, device):
    return pltpu.make_async_remote_copy(src_ref=src, dst_ref=dst, send_sem=send_sem, recv_sem=recv_sem,
                                        device_id=device, device_id_type=MESH)


def _gather_ici_comm(fulls, jobs, specs):
    def piece(refs, job, chip, c):
        f, r0, nr = job
        kind, rows, cols = specs[f]
        stride = (cols if kind == "col" else rows) // N_CHIPS
        return _rows_of(_half(_shard_of(refs[f], kind, chip * stride, stride), kind, c), r0, nr)

    def start(refs, ss, rs):
        x, y, c = _position()
        j = 2 * x + y
        for q, job in enumerate(jobs):
            for p, (px, py) in enumerate(_other_chips(x, y)):
                _remote(piece(refs, job, j, c), piece(refs, job, j, c), ss.at[3 * q + p], rs.at[3 * q + p],
                        (px, py, c)).start()

    def finish(refs, ss, rs):
        x, y, c = _position()
        j = 2 * x + y
        for q, job in enumerate(jobs):
            for p, (px, py) in enumerate(_other_chips(x, y)):
                _remote(piece(refs, job, j, c), piece(refs, job, 2 * px + py, c), ss.at[3 * q + p],
                        rs.at[3 * q + p], (px, py, c)).wait_recv()
        for q, job in enumerate(jobs):
            for p, (px, py) in enumerate(_other_chips(x, y)):
                _remote(piece(refs, job, j, c), piece(refs, job, j, c), ss.at[3 * q + p], rs.at[3 * q + p],
                        (px, py, c)).wait_send()

    names = list(dict.fromkeys(job[0] for job in jobs))
    return _Carried({f: fulls[f] for f in names}, {}, 3 * len(jobs), start, finish)


def _gather_ring_comm(fulls, f, r0, nr, phase, specs):
    kind, _, cols = specs[f]
    assert kind == "col" and nr % 32 == 0
    stride = cols // N_CHIPS
    half = nr // 2

    def rows(refs, chip, c, lo, n):
        return _rows_of(_half(_shard_of(refs[f], kind, chip * stride, stride), kind, c), r0 + lo, n)

    def copies(refs, ss, rs):
        x, y, c = _position()
        me, nx, ny, dg = 2 * x + y, 2 * (1 - x) + y, 2 * x + (1 - y), 2 * (1 - x) + (1 - y)
        to_x, to_y = (1 - x, y, c), (x, 1 - y, c)
        if phase == "a":
            mine = rows(refs, me, c, 0, nr)
            return [(_remote(mine, mine, ss.at[0], rs.at[0], to_x), rows(refs, nx, c, 0, nr)),
                    (_remote(mine, mine, ss.at[1], rs.at[1], to_y), rows(refs, ny, c, 0, nr))]
        up, low = rows(refs, ny, c, half, half), rows(refs, nx, c, 0, half)
        return [(_remote(up, up, ss.at[0], rs.at[0], to_x), rows(refs, dg, c, half, half)),
                (_remote(low, low, ss.at[1], rs.at[1], to_y), rows(refs, dg, c, 0, half))]

    def start(refs, ss, rs):
        for cp, _ in copies(refs, ss, rs):
            cp.start()

    def finish(refs, ss, rs):
        x, y, c = _position()
        mine = copies(refs, ss, rs)
        for i, (_, landing) in enumerate(mine):
            _remote(landing, landing, ss.at[i], rs.at[i], (x, y, c)).wait_recv()
        for cp, _ in mine:
            cp.wait_send()

    return _Carried({f: fulls[f]}, {}, 2, start, finish)


def _gather_d2d_comm(fulls, jobs, specs):
    def rect(refs, job, h):
        f, r0, nr = job
        assert nr is None or specs[f][0] == "col"
        return _rows_of(_half(refs[f], specs[f][0], h), r0, nr)

    def start(refs, ss, rs):
        x, y, c = _position()
        for q, job in enumerate(jobs):
            _remote(rect(refs, job, c), rect(refs, job, c), ss.at[q], rs.at[q], (x, y, 1 - c)).start()

    def finish(refs, ss, rs):
        x, y, c = _position()
        for q, job in enumerate(jobs):
            _remote(rect(refs, job, 1 - c), rect(refs, job, 1 - c), ss.at[q], rs.at[q], (x, y, 1 - c)).wait_recv()
        for q, job in enumerate(jobs):
            _remote(rect(refs, job, c), rect(refs, job, c), ss.at[q], rs.at[q], (x, y, 1 - c)).wait_send()

    names = list(dict.fromkeys(job[0] for job in jobs))
    return _Carried({f: fulls[f] for f in names}, {}, len(jobs), start, finish)


def _pairx_comm(grads, names, specs, whole=False):
    def copies(refs, ss, rs):
        x, y, c = _position()
        src = (lambda f: refs[("g", f)]) if whole else (lambda f: _half(refs[("g", f)], specs[f][0], 1 - c))
        return [_remote(src(f), refs[("r", f)], ss.at[i], rs.at[i], (x, y, 1 - c)) for i, f in enumerate(names)]

    def start(refs, ss, rs):
        for cp in copies(refs, ss, rs):
            cp.start()

    def finish(refs, ss, rs):
        for cp in copies(refs, ss, rs):
            cp.wait_recv()
        for cp in copies(refs, ss, rs):
            cp.wait_send()

    fresh = {("r", f): jax.ShapeDtypeStruct(_half_shape(*specs[f]), BF16) for f in names}
    return _Carried({}, fresh, len(names), start, finish, reads={("g", f): grads[f] for f in names})


def _chipx_comm(pair_sums, slots, jobs, specs):
    def copies(refs, ss, rs):
        x, y, c = _position()
        out = []
        for q, (f, r0, nr) in enumerate(jobs):
            kind = specs[f][0]
            width = _slot_shape(specs[f])[1 if kind == "col" else 0]
            for p, (px, py) in enumerate(_other_chips(x, y)):
                src = _rows_of(_shard_of(refs[("p", f)], kind, (2 * px + py) * width, width), r0, nr)
                dst = _rows_of(refs[("s", f)].at[p], r0, nr)
                out.append(_remote(src, dst, ss.at[3 * q + p], rs.at[3 * q + p], (px, py, c)))
        return out

    def start(refs, ss, rs):
        for cp in copies(refs, ss, rs):
            cp.start()

    def finish(refs, ss, rs):
        for cp in copies(refs, ss, rs):
            cp.wait_recv()
        for cp in copies(refs, ss, rs):
            cp.wait_send()

    names = list(dict.fromkeys(job[0] for job in jobs))
    arrays = {("p", f): pair_sums[f] for f in names}
    arrays.update({("s", f): slots[f] for f in names})
    return _Carried(arrays, {}, 3 * len(jobs), start, finish)


def _share_comm(grads, wnames, specs, place):
    def start(refs, ss, rs):
        x, y, c = _position()
        for i, w in enumerate(wnames):
            kind = specs[place[w][0]][0]
            _remote(_half(refs[w], kind, c), _half(refs[w], kind, c), ss.at[i], rs.at[i], (x, y, 1 - c)).start()

    def finish(refs, ss, rs):
        x, y, c = _position()
        for i, w in enumerate(wnames):
            kind = specs[place[w][0]][0]
            _remote(_half(refs[w], kind, 1 - c), _half(refs[w], kind, 1 - c), ss.at[i], rs.at[i],
                    (x, y, 1 - c)).wait_recv()
        for i, w in enumerate(wnames):
            kind = specs[place[w][0]][0]
            _remote(_half(refs[w], kind, c), _half(refs[w], kind, c), ss.at[i], rs.at[i], (x, y, 1 - c)).wait_send()

    return _Carried({w: grads[w] for w in wnames}, {}, len(wnames), start, finish)


def _gather_rows(v, name="gather_small"):
    shape = v.shape

    def body(v_ref, out_ref, send_sem, recv_sem, loc_sem):
        x, y, c = _position()
        me = 4 * x + 2 * y + c
        flips = [(fx, fy, fc) for fx in (0, 1) for fy in (0, 1) for fc in (0, 1)][1:]

        def peer(fl):
            return tuple(1 - a if f else a for a, f in zip((x, y, c), fl))

        loc = pltpu.make_async_copy(v_ref, out_ref.at[me], loc_sem)
        loc.start()
        sends = []
        for i, fl in enumerate(flips):
            cp = _remote(v_ref, out_ref.at[me], send_sem.at[i], recv_sem.at[i], peer(fl))
            cp.start()
            sends.append(cp)
        for i, fl in enumerate(flips):
            px, py, pc = peer(fl)
            _remote(v_ref, out_ref.at[4 * px + 2 * py + pc], send_sem.at[i], recv_sem.at[i], peer(fl)).wait_recv()
        for cp in sends:
            cp.wait_send()
        loc.wait()

    return pl.pallas_call(
        body, name=name, in_specs=[ANY], out_specs=ANY,
        out_shape=jax.ShapeDtypeStruct((N_DEV,) + shape, F32),
        scratch_shapes=[pltpu.SemaphoreType.DMA((N_DEV - 1,)), pltpu.SemaphoreType.DMA((N_DEV - 1,)),
                        pltpu.SemaphoreType.DMA],
    )(v)


def _ew_block(rows, cols, elems=EW_BLOCK_ELEMS):
    tc = cols if cols <= 4096 else _div(cols, 2048, LANE)
    tr = _div(rows, max(16, elems // tc), 16)
    return tr, tc


def _mesh_scalars():
    x, y, c = _position()
    return jnp.stack([c, 2 * x + y]).astype(jnp.int32)


def _grid_spec(grid, in_specs, out_specs):
    return pltpu.PrefetchScalarGridSpec(num_scalar_prefetch=1, grid=grid, in_specs=in_specs, out_specs=out_specs)


def _cast_into_full(parts, fname, pos, specs, place, name, token=None):
    kind, rows, cols = specs[fname]
    ws = [w for w in place if place[w][0] == fname]
    if kind == "col":
        stride = cols // N_CHIPS
        hr = rows // 2
        tr = _div(hr, max(16, EW_BLOCK_ELEMS // stride), 16)
        nrb = hr // tr
        in_specs = [pl.BlockSpec((tr, parts[w].shape[1]), lambda i, pos_ref: (i + pos_ref[0] * nrb, 0)) for w in ws]
        out_spec = pl.BlockSpec((tr, stride), lambda i, pos_ref: (i + pos_ref[0] * nrb, pos_ref[1]))
    else:
        stride = rows // N_CHIPS
        hc = cols // 2
        tr = _div(stride, max(16, EW_BLOCK_ELEMS // hc), 16)
        nrb = stride // tr
        in_specs = [pl.BlockSpec((tr, hc), lambda i, pos_ref: (i, pos_ref[0])) for w in ws]
        out_spec = pl.BlockSpec((tr, hc), lambda i, pos_ref: (i + pos_ref[1] * nrb, pos_ref[0]))

    def kern(pos_ref, *refs):
        o_ref = refs[-1]
        for w, r in zip(ws, refs[:len(ws)]):
            off = place[w][1] if kind == "col" else 0
            o_ref[:, off:off + r.shape[1]] = r[...].astype(o_ref.dtype)

    tokens = [] if token is None else [token]
    in_specs = in_specs + [pl.BlockSpec(TOKEN_SHAPE, lambda i, pos_ref: (0, 0))] * len(tokens)
    return pl.pallas_call(
        kern, name=name, grid_spec=_grid_spec((nrb,), in_specs, out_spec),
        out_shape=jax.ShapeDtypeStruct((rows, cols), BF16),
        compiler_params=_cparams(("parallel",)),
    )(pos, *[parts[w] for w in ws], *tokens)


def _pair_sum(grad, recv, pos, spec, name, whole=False):
    kind, rows, cols = spec
    hr, hc = _half_shape(kind, rows, cols)
    tr, tc = _ew_block(hr, hc, 2 * EW_BLOCK_ELEMS)
    nrb, ncb = hr // tr, hc // tc
    blk = pl.BlockSpec((tr, tc), lambda i, jj, pos_ref: (i, jj))
    if whole:
        mine = blk
    elif kind == "col":
        mine = pl.BlockSpec((tr, tc), lambda i, jj, pos_ref: (i + pos_ref[0] * nrb, jj))
    else:
        mine = pl.BlockSpec((tr, tc), lambda i, jj, pos_ref: (i, jj + pos_ref[0] * ncb))

    def kern(pos_ref, g_ref, r_ref, o_ref, slots_ref):
        o_ref[...] = (g_ref[...].astype(F32) + r_ref[...].astype(F32)).astype(o_ref.dtype)

    return pl.pallas_call(
        kern, name=name, grid_spec=_grid_spec((nrb, ncb), [mine, blk], [blk, ANY]),
        out_shape=[jax.ShapeDtypeStruct((hr, hc), BF16),
                   jax.ShapeDtypeStruct((N_CHIPS - 1,) + _slot_shape(spec), BF16)],
        compiler_params=_cparams(("parallel", "parallel")),
    )(pos, grad, recv)


def _chip_sum(pair_sum, slots, pos, fname, shard_shapes, specs, place, name):
    kind, rows, cols = specs[fname]
    sr, sc = _slot_shape(specs[fname])
    ws = [w for w in place if place[w][0] == fname]
    n_slots = N_CHIPS - 1
    tr = _div(sr, max(16, EW_BLOCK_ELEMS // sc), 16)
    nrb = sr // tr
    slot = pl.BlockSpec((n_slots, tr, sc), lambda i, pos_ref: (0, i, 0))
    if kind == "col":
        own = pl.BlockSpec((tr, sc), lambda i, pos_ref: (i, pos_ref[1]))
        out_specs = [pl.BlockSpec((tr, shard_shapes[w][1]), lambda i, pos_ref: (i + pos_ref[0] * nrb, 0)) for w in ws]
    else:
        own = pl.BlockSpec((tr, sc), lambda i, pos_ref: (i + pos_ref[1] * nrb, 0))
        out_specs = [pl.BlockSpec((tr, sc), lambda i, pos_ref: (i, pos_ref[0])) for w in ws]

    def kern(pos_ref, own_ref, slot_ref, *out_refs):
        tot = own_ref[...].astype(F32)
        for s in range(n_slots):
            tot = tot + slot_ref[s].astype(F32)
        for w, o_ref in zip(ws, out_refs):
            off = place[w][1] if kind == "col" else 0
            o_ref[...] = tot[:, off:off + o_ref.shape[1]]

    outs = pl.pallas_call(
        kern, name=name, grid_spec=_grid_spec((nrb,), [own, slot], out_specs),
        out_shape=[jax.ShapeDtypeStruct(shard_shapes[w], F32) for w in ws],
        compiler_params=_cparams(("parallel",)),
    )(pos, pair_sum, slots)
    return dict(zip(ws, outs))


def _adam_math(w, g, m, v):
    m2 = ADAM_B1 * m + (1.0 - ADAM_B1) * g
    v2 = ADAM_B2 * v + (1.0 - ADAM_B2) * (g * g)
    m_hat = m2 / (1.0 - ADAM_B1 ** ADAM_STEP)
    v_hat = v2 / (1.0 - ADAM_B2 ** ADAM_STEP)
    delta = -ADAM_LR * (m_hat / (jnp.sqrt(v_hat) + ADAM_EPS) + ADAM_WD * w)
    return delta, m2, v2


def _adamw(w, g, m, v, name, carried=()):
    rows, cols = w.shape
    tr, tc = _ew_block(rows, cols)

    def kern(w_ref, g_ref, m_ref, v_ref, d_ref, m2_ref, v2_ref, g_out_ref):
        g_ = g_ref[...]
        d_ref[...], m2_ref[...], v2_ref[...] = _adam_math(w_ref[...], g_, m_ref[...], v_ref[...])
        g_out_ref[...] = g_

    blk = pl.BlockSpec((tr, tc), lambda i, j: (i, j))
    return _pcall(
        kern, name=name, grid=(rows // tr, cols // tc), in_specs=[blk] * 4, out_specs=[blk] * 4,
        out_shape=[jax.ShapeDtypeStruct((rows, cols), F32)] * 4, args=(w, g, m, v),
        semantics=("parallel", "parallel"), carried=carried)


SMALL_ROWS = ("ln_mix_w", "ln_cross_w", "ln_mem_w", "ln_ffn_w", "ln_final_w")
ROW_HG_NORM, ROW_LB0, ROW_LB1 = 5, 6, 7
LOSS_LANE0 = HEAD_DIM


def _pack_small(vals):
    rows = [vals[n].reshape(1, D_MODEL) for n in SMALL_ROWS]
    pad = lambda a: jnp.pad(a, ((0, 0), (0, D_MODEL - a.shape[1])))
    rows.append(pad(vals["hg_norm_w"].reshape(1, HEAD_DIM)))
    rows.append(pad(vals["hg_lower_bounds"].reshape(2, HG_WIDTH)))
    return jnp.concatenate(rows, axis=0)


def _small_update(gathered, w, m, v, name="small_update"):
    def kern(g_ref, w_ref, m_ref, v_ref, grad_ref, d_ref, m2_ref, v2_ref, loss_ref):
        tot = g_ref[0]
        for s in range(1, N_DEV):
            tot = tot + g_ref[s]
        wv = w_ref[...]
        row = lax.broadcasted_iota(jnp.int32, (8, D_MODEL), 0)
        lane = lax.broadcasted_iota(jnp.int32, (8, D_MODEL), 1)
        l0, l1 = wv[ROW_LB0:ROW_LB0 + 1], wv[ROW_LB1:ROW_LB1 + 1]
        mx = jnp.maximum(l0, l1)
        e0, e1 = jnp.exp(l0 - mx), jnp.exp(l1 - mx)
        p0 = e0 / (e0 + e1)
        dlog = tot[ROW_LB0:ROW_LB0 + 1] * p0 * (1.0 - p0)
        tot = jnp.where(row == ROW_HG_NORM, tot + tot[ROW_LB1:ROW_LB1 + 1], tot)
        grad = jnp.where(row == ROW_LB0, dlog, jnp.where(row == ROW_LB1, -dlog, tot))
        grad = jnp.where((row == ROW_HG_NORM) & (lane >= HEAD_DIM), 0.0, grad)
        grad = jnp.where((row >= ROW_LB0) & (lane >= HG_WIDTH), 0.0, grad)
        grad_ref[...] = grad
        d_ref[...], m2_ref[...], v2_ref[...] = _adam_math(wv, grad, m_ref[...], v_ref[...])
        loss_ref[...] = tot[ROW_HG_NORM:ROW_HG_NORM + 1, LOSS_LANE0:LOSS_LANE0 + LANE]

    full = pl.BlockSpec((8, D_MODEL), lambda: (0, 0))
    return pl.pallas_call(
        kern, name=name,
        in_specs=[pl.BlockSpec((N_DEV, 8, D_MODEL), lambda: (0, 0, 0)), full, full, full],
        out_specs=[full, full, full, full, pl.BlockSpec((1, LANE), lambda: (0, 0))],
        out_shape=[jax.ShapeDtypeStruct((8, D_MODEL), F32)] * 4 + [jax.ShapeDtypeStruct((1, LANE), F32)],
        compiler_params=_cparams(),
    )(gathered, w, m, v)


def _unpack_small(p, shapes):
    out = {n: p[i].reshape(shapes[n]) for i, n in enumerate(SMALL_ROWS)}
    out["hg_norm_w"] = p[ROW_HG_NORM, :HEAD_DIM].reshape(shapes["hg_norm_w"])
    out["hg_lower_bounds"] = p[ROW_LB0:ROW_LB1 + 1, :HG_WIDTH].reshape(shapes["hg_lower_bounds"])
    return out


WHOLE = lambda f: (f, 0, None)
MID_MATRICES = ("w_branch_a", "w_branch_b", "w_out", "wq_cross", "wkv_cross", "wo_cross")
MID_WEIGHTS = MID_MATRICES
W_IN_PIECES = [("w_in", r0, 512) for r0 in range(0, D_MODEL // 2, 512)]
W13_PIECES = [("w13", r0, 512) for r0 in range(0, D_MODEL // 2, 512)]
GATHER_GROUPS = [("mid", [WHOLE(f) for f in MID_MATRICES]), ("w13a", W13_PIECES[:1]), ("w13b", W13_PIECES[1:]),
                 ("w2", [WHOLE("w2")])]
OTHER_WEIGHTS = ["w1", "w3", "w2"] + list(MID_WEIGHTS)
BEFORE = {
    "hgrn_fwd": [("wait", "mid")],
    "mm_out": [("wait", "w13a")],
    "mm_o": [("wait", "w13b")],
    "mm_w2": [("wait", "w2"), ("run", ("d2d", [WHOLE("w2")]), "gather_hand_over_w2")],
    "mm_dh": [("wait", "rs_w2"), ("chip_sum", "w2"), ("wait", "rs_w13"), ("chip_sum", "w13"), ("wait", "rs_mid")]
    + [("chip_sum", f) for f in MID_MATRICES],
}
CARRY = {
    "hgrn_fwd": [("d2d", [WHOLE(f) for f in MID_MATRICES])],
    "mm_out": [("d2d", W13_PIECES[:1])],
    "mm_o": [("d2d", W13_PIECES[1:])],
    "mm_du": [("pairx", ["w2"])],
    "mm_dhf": [("pairx", ["w13"])],
    "attn_merge_bwd": [("pairx", list(MID_MATRICES))],
    "mm_dwin_own": [("pairx_whole", ["w_in"])],
    "mm_dh": [("share", OTHER_WEIGHTS)],
}
AFTER = {
    "mm_du": [("pair_sum", "w2"), ("start", "rs_w2", [WHOLE("w2")])],
    "mm_dhf": [("pair_sum", "w13"), ("start", "rs_w13", [WHOLE("w13")])],
    "attn_merge_bwd": [("pair_sum", f) for f in MID_MATRICES] + [("start", "rs_mid", [WHOLE(f) for f in MID_MATRICES])],
    "mm_dwin_own": [("pair_sum", "w_in"), ("start", "rs_w_in", [WHOLE("w_in")])],
}
FINISH = [
    ("adamw", OTHER_WEIGHTS), ("wait", "rs_w_in"), ("small",), ("chip_sum", "w_in"),
    ("run", ("share", ["w_in"]), "rs_sibling_share_w_in"), ("adamw", ["w_in"]),
]


class _Net:
    def __init__(self, full, pos=None, shard_shapes=None, comm=True, specs=FULL_SPECS, place=WEIGHT_PLACE):
        self.full, self.pos, self.shard_shapes, self.comm = dict(full), pos, shard_shapes, comm
        self.specs, self.place = specs, place
        self.gw, self.recv, self.psum, self.slots, self.grads = {}, {}, {}, {}, {}
        self.gw_sibling = {}
        self.pending, self.token, self.last = {}, None, None

    def _make(self, kind, arg):
        if kind == "gather":
            return _gather_ici_comm(self.full, arg, self.specs)
        if kind == "ring":
            return _gather_ring_comm(self.full, *arg, self.specs)
        if kind == "d2d":
            return _gather_d2d_comm(self.full, arg, self.specs)
        if kind == "pairx":
            return _pairx_comm(self.gw, arg, self.specs)
        if kind == "pairx_whole":
            return _pairx_comm(self.gw_sibling, arg, self.specs, whole=True)
        if kind == "chipx":
            return _chipx_comm(self.psum, self.slots, arg, self.specs)
        assert kind == "share"
        return _share_comm(self.grads, arg, self.specs, self.place)

    def _store(self, kind, res):
        if kind in ("gather", "ring", "d2d"):
            self.full.update(res)
        elif kind in ("pairx", "pairx_whole"):
            for (tag, f), a in res.items():
                (self.gw if tag == "g" else self.recv)[f] = a
        elif kind == "chipx":
            for (tag, f), a in res.items():
                (self.psum if tag == "p" else self.slots)[f] = a
        else:
            self.grads.update(res)

    def run_comm(self, item, name):
        kind, arg = item
        self._store(kind, _run_comm([self._make(kind, arg)], name)[0])

    @staticmethod
    def _others(after, items):
        own = [a for cm in items for a in cm.arrays.values()]
        return [a for a in after if a is not None and all(a is not o for o in own)]

    def start(self, groups, kind, name):
        items = [self._make(kind, jobs) for _, jobs in groups]
        after = self._others([self.last], items)
        res, sems, token = _split_start(items, name, after=after[0] if after else None)
        for (group, jobs), r, s in zip(groups, res, sems):
            self._store(kind, r)
            self.pending[group] = (kind, jobs, s)
        self.token = self.last = token

    def wait(self, group, after=()):
        kind, jobs, sems = self.pending.pop(group)
        item = self._make(kind, jobs)
        res = _split_wait([item], [sems], self._others([self.last, *after], [item]), f"wait_{group}")[0]
        self._store(kind, res)

    def step(self, step):
        if step[0] == "wait":
            self.wait(step[1])
        elif step[0] == "start":
            self.start([(step[1], step[2])], "chipx", f"start_{step[1]}")
        elif step[0] == "pair_sum":
            f = step[1]
            self.psum[f], self.slots[f] = _pair_sum(self.gw[f], self.recv[f], self.pos, self.specs[f],
                                                    f"rs_pair_sum_{f}", whole=f in self.gw_sibling)
        elif step[0] == "chip_sum":
            f = step[1]
            self.grads.update(_chip_sum(self.psum[f], self.slots[f], self.pos, f, self.shard_shapes,
                                        self.specs, self.place, f"rs_chip_sum_{f}"))
        else:
            assert step[0] == "run"
            self.run_comm(step[1], step[2])

    def call(self, fn, name, *args, grad_of=None, sibling_half=False, **kw):
        for step in (BEFORE.get(name, []) if self.comm else []):
            self.step(step)
        args = [a() if callable(a) else a for a in args]
        items = CARRY.get(name, []) if self.comm else []
        carried = [self._make(k, a) for k, a in items]
        if self.token is not None:
            carried.append(_Token(self.token))
            self.token = None
        out, res = fn(*args, name=name, carried=carried, **kw)
        if grad_of is not None:
            (self.gw_sibling if sibling_half else self.gw)[grad_of] = out
        self.last = jax.tree.leaves(out)[0]
        for (kind, _), r in zip(items, res):
            self._store(kind, r)
        for step in (AFTER.get(name, []) if self.comm else []):
            self.step(step)
        return out


def _local_step(net, x, h, mem, target, small):
    full, call = net.full, net.call
    proj = call(_mm, "mm_proj", h, full["w_in"], mode="nn", out_dtype=F32)
    att = [call(_attn_fwd, f"attn_fwd_g{g}", proj, g) for g in range(3)]
    outs, lses = [a[0] for a in att], [a[1] for a in att]
    o_att = _attn_merge_fwd(outs, lses, "attn_merge")
    oraw, o_hg, states = call(_hg_fwd, "hgrn_fwd", proj, small["hg_lower_bounds"], small["hg_norm_w"])
    ya = call(_mm, "mm_branch_a", o_att, full["w_branch_a"], mode="nn", out_dtype=F32)
    yb, merged = call(_branch_b_gate, "mm_branch_b", o_hg, full["w_branch_b"], proj, ya)
    x1, hc = call(_residual_norm, "mm_out", merged, full["w_out"], x, small["ln_cross_w"])

    mn = _rms_fwd(mem, small["ln_mem_w"], "rms_mem")
    qc = call(_mm, "mm_q", hc, full["wq_cross"], mode="nn", out_dtype=F32)
    kvc = call(_mm, "mm_kv", mn, full["wkv_cross"], mode="nn", out_dtype=F32)
    oc = call(_cross_fwd, "cross_fwd", qc, kvc)
    x2, hf = call(_residual_norm, "mm_o", oc, full["wo_cross"], x1, small["ln_ffn_w"])

    ab, u = call(_ff_up, "mm_w13", hf, full["w13"])
    x3 = call(_mm, "mm_w2", u, lambda: full["w2"], mode="nn", out_dtype=F32, res=x2)

    dx3, dg_final, loss = _loss_head(x3, small["ln_final_w"], target, "loss_head")

    gs = {"ln_final_w": dg_final}
    call(_mm, "mm_dw2", u, dx3, mode="tn", out_dtype=BF16, grad_of="w2")
    dab = call(_ff_down_bwd, "mm_du", dx3, full["w2"], ab)
    call(_mm, "mm_dw13", hf, dab, mode="tn", out_dtype=BF16, grad_of="w13")
    dhf = call(_mm, "mm_dhf", dab, full["w13"], mode="nt", out_dtype=F32)
    dx2, gs["ln_ffn_w"] = _rms_bwd(x2, small["ln_ffn_w"], dhf, dx3, "rms_ffn_bwd")
    doc = call(_mm, "mm_doc", dx2, full["wo_cross"], mode="nt", out_dtype=BF16)
    call(_mm, "mm_dwo", oc, dx2, mode="tn", out_dtype=BF16, grad_of="wo_cross")
    dqc, dkvc = _cross_bwd(qc, kvc, doc, "cross_bwd")
    call(_mm, "mm_dwq", hc, dqc, mode="tn", out_dtype=BF16, grad_of="wq_cross")
    dhc = call(_mm, "mm_dhc", dqc, full["wq_cross"], mode="nt", out_dtype=F32)
    call(_mm, "mm_dwkv", mn, dkvc, mode="tn", out_dtype=BF16, grad_of="wkv_cross")
    dmn = call(_mm, "mm_dmn", dkvc, full["wkv_cross"], mode="nt", out_dtype=F32)
    _, gs["ln_mem_w"] = _rms_bwd(mem, small["ln_mem_w"], dmn, None, "rms_mem_bwd")
    dx1, gs["ln_cross_w"] = _rms_bwd(x1, small["ln_cross_w"], dhc, dx2, "rms_cross_bwd")
    dya, dyb, dga, dgb = call(_dmerged_gate_bwd, "mm_dmerged", dx1, full["w_out"], proj, ya, yb)
    call(_mm, "mm_dwout", merged, dx1, mode="tn", out_dtype=BF16, grad_of="w_out")
    call(_mm, "mm_dwa", o_att, dya, mode="tn", out_dtype=BF16, grad_of="w_branch_a")
    do_att = call(_mm, "mm_doatt", dya, full["w_branch_a"], mode="nt", out_dtype=F32)
    call(_mm, "mm_dwb", o_hg, dyb, mode="tn", out_dtype=BF16, grad_of="w_branch_b")
    do_hg = call(_mm, "mm_dohg", dyb, full["w_branch_b"], mode="nt", out_dtype=F32)
    dqh, dfh, dih, dgh, dlb, gs["hg_norm_w"] = call(
        _hg_bwd, "hgrn_bwd", proj, small["hg_lower_bounds"], small["hg_norm_w"], oraw, states, do_hg)
    gs["hg_lb"] = dlb
    do_gs, dl_gs = call(_attn_merge_bwd, "attn_merge_bwd", outs, lses, do_att)
    dqs, dks, dvs = zip(*[call(_attn_bwd, f"attn_bwd_g{g}", proj, g, lses[g], do_gs[g], dl_gs[g]) for g in range(3)])
    dproj = jnp.concatenate([*dqs, *dks, *dvs, dqh, dfh, dih, dgh, dga, dgb], axis=1)
    if net.comm:
        half = D_MODEL // 2
        c = lax.axis_index("c")
        h_sibling = lax.dynamic_slice_in_dim(h, (1 - c) * half, half, axis=1)
        h_own = lax.dynamic_slice_in_dim(h, c * half, half, axis=1)
        call(_mm, "mm_dwin_sibling", h_sibling, dproj, mode="tn", out_dtype=BF16, grad_of="w_in", sibling_half=True)
        call(_mm, "mm_dwin_own", h_own, dproj, mode="tn", out_dtype=BF16, grad_of="w_in")
    else:
        call(_mm, "mm_dwin", h, dproj, mode="tn", out_dtype=BF16, grad_of="w_in")
    dh = call(_mm, "mm_dh", dproj, full["w_in"], mode="nt", out_dtype=F32)
    dx, gs["ln_mix_w"] = _rms_bwd(x, small["ln_mix_w"], dh, dx1, "rms_mix_bwd")
    return loss, dx, gs


WEIGHT_ORDER = ("ln_mix_w", "w_in", "hg_norm_w", "hg_lower_bounds", "w_branch_a", "w_branch_b", "w_out",
                "ln_cross_w", "ln_mem_w", "wq_cross", "wkv_cross", "wo_cross", "ln_ffn_w", "w1", "w3", "w2",
                "ln_final_w")


def kernel(x, mem, ln_mix_w, w_in, hg_norm_w, hg_lower_bounds, w_branch_a, w_branch_b, w_out, ln_cross_w, ln_mem_w, wq_cross, wkv_cross, wo_cross, ln_ffn_w, w1, w3, w2, ln_final_w, loss_target, m_ln_mix_w, m_w_in, m_hg_norm_w, m_hg_lower_bounds, m_w_branch_a, m_w_branch_b, m_w_out, m_ln_cross_w, m_ln_mem_w, m_wq_cross, m_wkv_cross, m_wo_cross, m_ln_ffn_w, m_w1, m_w3, m_w2, m_ln_final_w, v_ln_mix_w, v_w_in, v_hg_norm_w, v_hg_lower_bounds, v_w_branch_a, v_w_branch_b, v_w_out, v_ln_cross_w, v_ln_mem_w, v_wq_cross, v_wkv_cross, v_wo_cross, v_ln_ffn_w, v_w1, v_w3, v_w2, v_ln_final_w):
    args = dict(locals())
    w = {n: args[n] for n in WEIGHT_ORDER}
    m = {n: args["m_" + n] for n in WEIGHT_ORDER}
    v = {n: args["v_" + n] for n in WEIGHT_ORDER}
    shapes = {n: w[n].shape for n in WEIGHT_ORDER}
    mat = lambda a: a.reshape(a.shape[-2:])
    shard_shapes = {n: shapes[n][-2:] for n in BIG_WEIGHTS}

    pos = _mesh_scalars()

    def cast(f, token=None):
        return _cast_into_full({n: mat(w[n]) for n in BIG_WEIGHTS if WEIGHT_PLACE[n][0] == f}, f, pos,
                               FULL_SPECS, WEIGHT_PLACE, f"cast_{f}", token)

    net = _Net({"w_in": cast("w_in")}, pos, shard_shapes)
    net.start([(f"ring_a{i}", (*job, "a")) for i, job in enumerate(W_IN_PIECES)], "ring", "gather_start_w_in")
    rest = {f: cast(f, net.token) for f in FULL_SPECS if f != "w_in"}
    net.full.update(rest)
    small = {n: w[n].reshape(1, -1) for n in SMALL_ROWS}
    small["hg_norm_w"] = w["hg_norm_w"].reshape(1, HEAD_DIM)
    small["hg_lower_bounds"] = w["hg_lower_bounds"]
    x2d = x.reshape(SEQ, D_MODEL)
    h = _rms_fwd(x2d, small["ln_mix_w"], "rms_mix")
    for i, job in enumerate(W_IN_PIECES):
        net.wait(f"ring_a{i}", after=[*rest.values(), h] if i == 0 else ())
        net.start([(f"ring_b{i}", (*job, "b"))], "ring", f"gather_pass_on_w_in{i}")
    net.start(GATHER_GROUPS, "gather", "gather_start_rest")
    for i, job in enumerate(W_IN_PIECES):
        net.wait(f"ring_b{i}")
        net.run_comm(("d2d", [job]), f"gather_hand_over_w_in{i}")
    loss, dx, gs = _local_step(net, x2d, h, mem.reshape(MEM_LEN, D_MODEL), loss_target.reshape(SEQ, D_MODEL), small)

    out_g, out_d, out_m, out_v = {}, {}, {}, {}
    net.last = dx
    for step in FINISH:
        if step[0] == "adamw":
            for n in step[1]:
                out_d[n], out_m[n], out_v[n], out_g[n] = net.call(_adamw, f"adamw_{n}", mat(w[n]), net.grads[n],
                                                                  mat(m[n]), mat(v[n]))
        elif step[0] == "wait":
            net.wait(step[1], after=list(out_d.values()))
        elif step[0] == "small":
            pad = lambda a: jnp.pad(a, ((0, 0), (0, D_MODEL - a.shape[1])))
            part = jnp.concatenate(
                [gs[n] for n in SMALL_ROWS]
                + [pad(jnp.concatenate([gs["hg_norm_w"][0], loss], axis=1)), pad(gs["hg_lb"]),
                   pad(gs["hg_norm_w"][1]) if len(gs["hg_norm_w"]) > 1 else jnp.zeros((1, D_MODEL), F32)], axis=0)
            part, net.psum["w_in"] = lax.optimization_barrier((part, net.psum["w_in"]))
            sg, sd, sm, sv, loss_tot = _small_update(_gather_rows(part), _pack_small(w), _pack_small(m),
                                                     _pack_small(v))
            for dst, packed in ((out_g, sg), (out_d, sd), (out_m, sm), (out_v, sv)):
                dst.update(_unpack_small(packed, shapes))
        else:
            net.step(step)

    result = [loss_tot[0, 0], dx.reshape(x.shape)]
    for group in (out_g, out_d, out_m, out_v):
        result += [group[n].reshape(shapes[n]) for n in WEIGHT_ORDER]
    return tuple(result)
```

```python
import math

import jax
import jax.numpy as jnp
from jax import lax
from jax.experimental import pallas as pl
from jax.experimental.pallas import tpu as pltpu

F32 = jnp.float32
BF16 = jnp.bfloat16
MESH = pl.DeviceIdType.MESH

D_MODEL = 2048
SEQ = 2048
HEAD_DIM = 128
MEM_LEN = 256
ATT_GROUPS = ((128, 1), (512, 4), (2048, 16))
ATT_HEADS = 4
ATT_WIDTH = 3 * ATT_HEADS * HEAD_DIM
ATT_OUT = ATT_HEADS * HEAD_DIM
ATT_BLOCK = 128
HG_HEADS = 8
HG_WIDTH = HG_HEADS * HEAD_DIM
HG_CHUNK = 64
IN_WIDTH = 3 * ATT_WIDTH + 4 * HG_WIDTH + 2 * D_MODEL
CROSS_HEADS = 4
CROSS_WIDTH = CROSS_HEADS * HEAD_DIM
D_FF = 5632
RMS_EPS = 1e-6
ADAM_LR = 0.001
ADAM_B1 = 0.9
ADAM_B2 = 0.999
ADAM_EPS = 1e-08
ADAM_WD = 0.01
ADAM_STEP = 10
N_CHIPS = 4
N_DEV = 8

VMEM_LIMIT_BYTES = 56 * 1024 * 1024
LANE = 128
MXU_WIDTH = 256
MM_TILE_CAP = 1536
TRANSPOSE_CHUNK = 512
ANY = pl.BlockSpec(memory_space=pl.ANY)


def _cparams(sem=None):
    return pltpu.CompilerParams(dimension_semantics=sem, vmem_limit_bytes=VMEM_LIMIT_BYTES)


def _div(n, cap, mult):
    best = None
    for d in range(mult, min(n, cap) + 1, mult):
        if n % d == 0:
            best = d
    assert best is not None, (n, cap, mult)
    return best


def _sigmoid(x):
    return 1.0 / (1.0 + jnp.exp(-x))


def _dot(a, b):
    return jnp.dot(a.astype(BF16), b.astype(BF16), preferred_element_type=F32)


def _dot_nt(a, b):
    return lax.dot_general(a.astype(BF16), b.astype(BF16), (((1,), (1,)), ((), ())),
                           preferred_element_type=F32)


def _dot_tn(a, b):
    return jnp.dot(a.astype(F32).T.astype(BF16), b.astype(BF16), preferred_element_type=F32)


def _dot_exact(a, b):
    return jnp.dot(a, b, precision=lax.Precision.HIGHEST, preferred_element_type=F32)


class _Carried:
    def __init__(self, arrays, fresh, n_sems, start, finish, mid=None, reads=None):
        self.arrays, self.fresh, self.n_sems, self.reads = arrays, fresh, n_sems, reads or {}
        self.start, self.mid, self.finish = start, mid, finish


class _Token:
    def __init__(self, array):
        self.array = array


TOKEN_SHAPE = (8, LANE)


def _carried_layout(carried):
    akeys = list(dict.fromkeys(k for cm in carried for k in cm.arrays))
    fkeys = [(ci, k) for ci, cm in enumerate(carried) for k in cm.fresh]
    arrays = [next(cm.arrays[k] for cm in carried if k in cm.arrays) for k in akeys]
    shapes = [jax.ShapeDtypeStruct(a.shape, a.dtype) for a in arrays] + [carried[ci].fresh[k] for ci, k in fkeys]
    sems = []
    for cm in carried:
        sems += [pltpu.SemaphoreType.DMA((cm.n_sems,)), pltpu.SemaphoreType.DMA((cm.n_sems,))]
    return akeys, fkeys, arrays, shapes, sems


def _carried_reads(carried):
    rkeys = list(dict.fromkeys(k for cm in carried for k in cm.reads))
    return rkeys, [next(cm.reads[k] for cm in carried if k in cm.reads) for k in rkeys]


def _carried_results(carried, akeys, fkeys, outs, rkeys=(), read_refs=()):
    shared = dict(zip(akeys, outs[:len(akeys)]))
    shared.update(zip(rkeys, read_refs))
    res = [{k: shared[k] for k in list(cm.arrays) + [r for r in cm.reads if r in shared]} for cm in carried]
    for (ci, k), o in zip(fkeys, outs[len(akeys):]):
        res[ci][k] = o
    return res


def _pcall(kern, *, name, grid, in_specs, out_specs, out_shape, args, scratch_shapes=(), semantics=None,
           carried=()):
    tokens = [c.array for c in carried if isinstance(c, _Token)]
    carried = [c for c in carried if not isinstance(c, _Token)]
    single = not isinstance(out_shape, (list, tuple))
    out_specs = [out_specs] if single else list(out_specs)
    out_shape = [out_shape] if single else list(out_shape)
    n_real, n_out, n_scr = len(in_specs), len(out_shape), len(scratch_shapes)
    in_specs = list(in_specs) + [pl.BlockSpec(TOKEN_SHAPE, lambda *_: (0, 0))] * len(tokens)
    args = list(args) + tokens
    n_in = len(in_specs)
    if not carried:
        def plain(*refs):
            kern(*refs[:n_real], *refs[n_in:])

        outs = pl.pallas_call(plain if tokens else kern, name=name, grid=grid, in_specs=in_specs,
                              out_specs=out_specs, out_shape=out_shape, scratch_shapes=list(scratch_shapes),
                              compiler_params=_cparams(semantics))(*args)
        return (outs[0] if single else list(outs)), []
    akeys, fkeys, arrays, shapes, sems = _carried_layout(carried)
    rkeys, reads = _carried_reads(carried)
    n_a, n_f, n_r = len(akeys), len(fkeys), len(rkeys)
    total = math.prod(grid)
    mid_step = min(total - 1, (17 * total) // 20)

    def wrapped(*refs):
        ins = refs[:n_real]
        r0 = n_in + n_a
        o0 = r0 + n_r
        outs = refs[o0:o0 + n_out]
        a0 = o0 + n_out
        s0 = a0 + n_a + n_f
        per = _carried_results(carried, akeys, fkeys, refs[a0:s0], rkeys, refs[r0:o0])
        scratch = refs[s0:s0 + n_scr]
        sem = refs[s0 + n_scr:]
        step = 0
        for d, g in enumerate(grid):
            step = step * g + pl.program_id(d)

        @pl.when(step == 0)
        def _():
            for ci, cm in enumerate(carried):
                cm.start(per[ci], sem[2 * ci], sem[2 * ci + 1])

        kern(*ins, *outs, *scratch)

        @pl.when(step == mid_step)
        def _():
            for ci, cm in enumerate(carried):
                if cm.mid is not None:
                    cm.mid(per[ci], sem[2 * ci], sem[2 * ci + 1])

        @pl.when(step == total - 1)
        def _():
            for ci, cm in enumerate(carried):
                cm.finish(per[ci], sem[2 * ci], sem[2 * ci + 1])

    outs = pl.pallas_call(
        wrapped, name=name, grid=grid,
        in_specs=list(in_specs) + [ANY] * (n_a + n_r), out_specs=out_specs + [ANY] * (n_a + n_f),
        out_shape=out_shape + shapes,
        input_output_aliases={n_in + i: n_out + i for i in range(n_a)},
        scratch_shapes=list(scratch_shapes) + sems,
        compiler_params=_cparams(("arbitrary",) * len(grid)),
    )(*args, *arrays, *reads)
    res = _carried_results(carried, akeys, fkeys, outs[n_out:])
    return (outs[0] if single else list(outs[:n_out])), res


def _run_comm(carried, name):
    carried = list(carried)
    akeys, fkeys, arrays, shapes, sems = _carried_layout(carried)
    rkeys, reads = _carried_reads(carried)
    n_a, n_f, n_r = len(akeys), len(fkeys), len(rkeys)

    def body(*refs):
        o0 = n_a + n_r
        per = _carried_results(carried, akeys, fkeys, refs[o0:o0 + n_a + n_f], rkeys, refs[n_a:o0])
        sem = refs[o0 + n_a + n_f:]
        for hook in ("start", "mid", "finish"):
            for ci, cm in enumerate(carried):
                fn = getattr(cm, hook)
                if fn is not None:
                    fn(per[ci], sem[2 * ci], sem[2 * ci + 1])

    outs = pl.pallas_call(
        body, name=name, in_specs=[ANY] * (n_a + n_r), out_specs=[ANY] * (n_a + n_f), out_shape=shapes,
        input_output_aliases={i: i for i in range(n_a)}, scratch_shapes=sems,
    )(*arrays, *reads)
    return _carried_results(carried, akeys, fkeys, outs)


HBM_SPEC = pl.BlockSpec(memory_space=pltpu.HBM)
SEM_SPEC = pl.BlockSpec(memory_space=pltpu.SEMAPHORE)
SPLIT_EFFECT = pltpu.SideEffectType.DATAFLOW_SIDE_EFFECTING


def _in_hbm(a):
    return pltpu.with_memory_space_constraint(a, pltpu.HBM)


def _split_start(items, name, after=None):
    items = list(items)
    akeys, fkeys, arrays, shapes, sems = _carried_layout(items)
    assert not fkeys
    n_a, n_s = len(akeys), len(sems)
    n_in = n_a + (after is not None)

    def body(*refs):
        per = _carried_results(items, akeys, [], refs[n_in:n_in + n_a])
        sem = refs[n_in + n_a:n_in + n_a + n_s]
        for ci, cm in enumerate(items):
            cm.start(per[ci], sem[2 * ci], sem[2 * ci + 1])
        token = refs[n_in + n_a + n_s]
        token[...] = jnp.zeros_like(token)

    outs = pl.pallas_call(
        body, name=name, in_specs=[HBM_SPEC] * n_a + [ANY] * (after is not None),
        out_specs=[HBM_SPEC] * n_a + [SEM_SPEC] * n_s + [pl.BlockSpec(memory_space=pltpu.VMEM)],
        out_shape=[pltpu.HBM(s.shape, s.dtype) for s in shapes] + sems + [jax.ShapeDtypeStruct(TOKEN_SHAPE, F32)],
        input_output_aliases={i: i for i in range(n_a)},
        compiler_params=pltpu.CompilerParams(has_side_effects=SPLIT_EFFECT),
    )(*[_in_hbm(a) for a in arrays], *([after] if after is not None else []))
    res = _carried_results(items, akeys, [], outs[:n_a])
    sem_out = outs[n_a:n_a + n_s]
    return res, [(sem_out[2 * ci], sem_out[2 * ci + 1]) for ci in range(len(items))], outs[-1]


def _split_wait(items, sems, after, name):
    items = list(items)
    after = list(after) if isinstance(after, (list, tuple)) else [after]
    akeys, fkeys, arrays, shapes, _ = _carried_layout(items)
    n_a, n_s = len(akeys), 2 * len(items)

    def body(*refs):
        per = _carried_results(items, akeys, [], refs[n_a + n_s + len(after):])
        sem = refs[n_a:n_a + n_s]
        for ci, cm in enumerate(items):
            cm.finish(per[ci], sem[2 * ci], sem[2 * ci + 1])

    outs = pl.pallas_call(
        body, name=name, in_specs=[HBM_SPEC] * n_a + [SEM_SPEC] * n_s + [ANY] * len(after),
        out_specs=[HBM_SPEC] * n_a, out_shape=[pltpu.HBM(s.shape, s.dtype) for s in shapes],
        input_output_aliases={i: i for i in range(n_a)},
        compiler_params=pltpu.CompilerParams(has_side_effects=SPLIT_EFFECT),
    )(*arrays, *[s for pair in sems for s in pair], *after)
    return _carried_results(items, akeys, [], outs)


def _mm(a, b, *, mode, out_dtype, name, res=None, carried=(), fused=None):
    if mode == "nn":
        (m, k), (k2, n) = a.shape, b.shape
    elif mode == "nt":
        (m, k), (n, k2) = a.shape, b.shape
    else:
        (k, m), (k2, n) = a.shape, b.shape
    assert k == k2, (name, a.shape, b.shape)
    tm = _div(m, MM_TILE_CAP, LANE)
    tn = _div(n, MM_TILE_CAP, MXU_WIDTH) if n % MXU_WIDTH == 0 else 0
    if tn < 1024:
        tn = _div(n, MM_TILE_CAP, LANE)
    out_shape = jax.ShapeDtypeStruct((m, n), out_dtype)

    if fused is not None:
        assert mode in ("nn", "nt") and res is None and k <= 2048
        tm, tn = fused["tile"]
        dot = _dot if mode == "nn" else _dot_nt
        n_x = len(fused["ins"])

        summed = [len(e) > 2 for e in fused["outs"]]

        def kern_fused(*refs):
            part = dot(refs[0][...], refs[1][...])
            outs = fused["post"](part, *[r[...] for r in refs[2:2 + n_x]])
            first_row_tile = pl.program_id(1) == 0
            for o_ref, val, acc in zip(refs[2 + n_x:], outs, summed):
                if not acc:
                    o_ref[...] = val.astype(o_ref.dtype)
                    continue

                @pl.when(first_row_tile)
                def _():
                    o_ref[...] = val

                @pl.when(jnp.logical_not(first_row_tile))
                def _():
                    o_ref[...] += val

        def tile_spec(shape, index=lambda i, j: (i, j)):
            return pl.BlockSpec(shape, lambda j, i: index(i, j))

        return _pcall(
            kern_fused, name=name, grid=(n // tn, m // tm),
            in_specs=[pl.BlockSpec((tm, k), lambda j, i: (i, 0)),
                      pl.BlockSpec((k, tn), lambda j, i: (0, j)) if mode == "nn"
                      else pl.BlockSpec((tn, k), lambda j, i: (j, 0))] + [tile_spec(*e[1:]) for e in fused["ins"]],
            out_specs=[tile_spec(*e[1:]) for e in fused["outs"]], out_shape=[e[0] for e in fused["outs"]],
            args=(a, b, *[e[0] for e in fused["ins"]]), semantics=("parallel", "arbitrary"), carried=carried)

    if mode == "tn":
        assert res is None

        def kern_tn(a_ref, b_ref, o_ref, at_ref):
            @pl.when(pl.program_id(1) == 0)
            def _():
                step = min(TRANSPOSE_CHUNK, k)
                for c0 in range(0, k, step):
                    at_ref[:, c0:c0 + step] = a_ref[c0:c0 + step, :].astype(F32).T.astype(BF16)

            o_ref[...] = jnp.dot(at_ref[...], b_ref[...].astype(BF16),
                                 preferred_element_type=F32).astype(o_ref.dtype)

        return _pcall(
            kern_tn, name=name, grid=(m // tm, n // tn),
            in_specs=[pl.BlockSpec((k, tm), lambda i, j: (0, i)),
                      pl.BlockSpec((k, tn), lambda i, j: (0, j))],
            out_specs=pl.BlockSpec((tm, tn), lambda i, j: (i, j)),
            out_shape=out_shape, args=(a, b),
            scratch_shapes=[pltpu.VMEM((tm, k), BF16)],
            semantics=("parallel", "arbitrary"), carried=carried)

    tk = k if k <= 2048 else _div(k, 3072, LANE)
    nk = k // tk
    a_spec = pl.BlockSpec((tm, tk), lambda i, j, kk: (i, kk))
    if mode == "nn":
        b_spec = pl.BlockSpec((tk, tn), lambda i, j, kk: (kk, j))
        dot = _dot
    else:
        b_spec = pl.BlockSpec((tn, tk), lambda i, j, kk: (j, kk))
        dot = _dot_nt
    o_spec = pl.BlockSpec((tm, tn), lambda i, j, kk: (i, j))
    in_specs = [a_spec, b_spec]
    args = [a, b]
    if res is not None:
        in_specs.append(o_spec)
        args.append(res)
    has_res = res is not None

    def kern(*refs):
        a_ref, b_ref = refs[0], refs[1]
        r_ref = refs[2] if has_res else None
        o_ref = refs[3] if has_res else refs[2]
        part = dot(a_ref[...], b_ref[...])
        if nk == 1:
            if has_res:
                part = part + r_ref[...]
            o_ref[...] = part.astype(o_ref.dtype)
            return
        acc_ref = refs[-1]
        kk = pl.program_id(2)

        @pl.when(kk == 0)
        def _():
            acc_ref[...] = part

        @pl.when(kk > 0)
        def _():
            acc_ref[...] += part

        @pl.when(kk == nk - 1)
        def _():
            tot = acc_ref[...]
            if has_res:
                tot = tot + r_ref[...]
            o_ref[...] = tot.astype(o_ref.dtype)

    return _pcall(
        kern, name=name, grid=(m // tm, n // tn, nk),
        in_specs=in_specs, out_specs=o_spec, out_shape=out_shape, args=args,
        scratch_shapes=[pltpu.VMEM((tm, tn), F32)] if nk > 1 else [],
        semantics=("parallel", "parallel", "arbitrary"), carried=carried)


ROW_BLOCK = 512


def _rms_fwd(x, g, name):
    t, d = x.shape
    tr = min(ROW_BLOCK, t)

    def kern(x_ref, g_ref, o_ref):
        xf = x_ref[...]
        r = lax.rsqrt(jnp.mean(xf * xf, axis=-1, keepdims=True) + RMS_EPS)
        o_ref[...] = (xf * r * g_ref[...]).astype(o_ref.dtype)

    return pl.pallas_call(
        kern, name=name, grid=(t // tr,),
        in_specs=[pl.BlockSpec((tr, d), lambda i: (i, 0)), pl.BlockSpec((1, d), lambda i: (0, 0))],
        out_specs=pl.BlockSpec((tr, d), lambda i: (i, 0)),
        out_shape=jax.ShapeDtypeStruct((t, d), BF16),
        compiler_params=_cparams(("parallel",)),
    )(x, g)


def _rms_bwd(x, g, dh, res, name):
    t, d = x.shape
    tr = min(ROW_BLOCK, t)
    has_res = res is not None

    def kern(*refs):
        x_ref, g_ref, dh_ref = refs[:3]
        r_ref = refs[3] if has_res else None
        dx_ref, dg_ref = refs[-2], refs[-1]
        xf = x_ref[...]
        r = lax.rsqrt(jnp.mean(xf * xf, axis=-1, keepdims=True) + RMS_EPS)
        xn = xf * r
        dh_ = dh_ref[...]
        dhg = dh_ * g_ref[...]
        dx = r * (dhg - xn * jnp.mean(dhg * xn, axis=-1, keepdims=True))
        if has_res:
            dx = dx + r_ref[...]
        dx_ref[...] = dx
        part = jnp.sum(dh_ * xn, axis=0, keepdims=True)

        @pl.when(pl.program_id(0) == 0)
        def _():
            dg_ref[...] = part

        @pl.when(pl.program_id(0) > 0)
        def _():
            dg_ref[...] += part

    row = pl.BlockSpec((tr, d), lambda i: (i, 0))
    vec = pl.BlockSpec((1, d), lambda i: (0, 0))
    in_specs = [row, vec, row] + ([row] if has_res else [])
    args = [x, g, dh] + ([res] if has_res else [])
    return pl.pallas_call(
        kern, name=name, grid=(t // tr,), in_specs=in_specs, out_specs=[row, vec],
        out_shape=[jax.ShapeDtypeStruct((t, d), F32), jax.ShapeDtypeStruct((1, d), F32)],
        compiler_params=_cparams(("arbitrary",)),
    )(*args)


def _loss_head(x3, g, target, name):
    t, d = x3.shape
    tr = ROW_BLOCK

    def kern(x_ref, g_ref, t_ref, dx_ref, dg_ref, loss_ref):
        xf = x_ref[...]
        r = lax.rsqrt(jnp.mean(xf * xf, axis=-1, keepdims=True) + RMS_EPS)
        xn = xf * r
        gg = g_ref[...]
        err = xn * gg - t_ref[...]
        lpart = 0.5 * jnp.sum(jnp.mean(err * err, axis=-1, keepdims=True), axis=0, keepdims=True)
        dy = err * (1.0 / d)
        dyg = dy * gg
        dx_ref[...] = r * (dyg - xn * jnp.mean(dyg * xn, axis=-1, keepdims=True))
        gpart = jnp.sum(dy * xn, axis=0, keepdims=True)
        lrow = jnp.broadcast_to(lpart, (1, LANE))

        @pl.when(pl.program_id(0) == 0)
        def _():
            dg_ref[...] = gpart
            loss_ref[...] = lrow

        @pl.when(pl.program_id(0) > 0)
        def _():
            dg_ref[...] += gpart
            loss_ref[...] += lrow

    row = pl.BlockSpec((tr, d), lambda i: (i, 0))
    vec = pl.BlockSpec((1, d), lambda i: (0, 0))
    return pl.pallas_call(
        kern, name=name, grid=(t // tr,), in_specs=[row, vec, row],
        out_specs=[row, vec, pl.BlockSpec((1, LANE), lambda i: (0, 0))],
        out_shape=[jax.ShapeDtypeStruct((t, d), F32), jax.ShapeDtypeStruct((1, d), F32),
                   jax.ShapeDtypeStruct((1, LANE), F32)],
        compiler_params=_cparams(("arbitrary",)),
    )(x3, g, target)


ATT_SCALE = HEAD_DIM ** -0.5
Q_BLOCK0, K_BLOCK0, V_BLOCK0 = 0, ATT_WIDTH // HEAD_DIM, 2 * ATT_WIDTH // HEAD_DIM


def _residue_rows(dil, r, n):
    if dil == 1:
        return pl.ds(n * ATT_BLOCK, ATT_BLOCK)
    return pl.ds(n * ATT_BLOCK * dil + r, ATT_BLOCK, stride=dil)


def _band_mask(with_prev):
    width = 2 * ATT_BLOCK if with_prev else ATT_BLOCK
    iq = lax.broadcasted_iota(jnp.int32, (ATT_BLOCK, width), 0)
    ik = lax.broadcasted_iota(jnp.int32, (ATT_BLOCK, width), 1)
    if not with_prev:
        return ik <= iq
    return ((ik < ATT_BLOCK) & (iq <= ik)) | ((ik >= ATT_BLOCK) & ((ik - ATT_BLOCK) <= iq))


def _band_keys(ref, dil, r, n):
    own = ref[_residue_rows(dil, r, n), :]
    if n == 0:
        return own
    return jnp.concatenate([ref[_residue_rows(dil, r, n - 1), :], own], axis=0)


def _attn_col_spec(base, grp):
    return pl.BlockSpec((SEQ, HEAD_DIM), lambda h: (0, base + grp * ATT_HEADS + h))


def _attn_fwd(proj, grp, name, carried=()):
    _, dil = ATT_GROUPS[grp]
    nb = SEQ // dil // ATT_BLOCK

    def kern(q_ref, k_ref, v_ref, o_ref, lse_ref):
        for r in range(dil):
            for n in range(nb):
                rows = _residue_rows(dil, r, n)
                s = _dot_nt(q_ref[rows, :], _band_keys(k_ref, dil, r, n)) * ATT_SCALE
                s = jnp.where(_band_mask(n > 0), s, -jnp.inf)
                m = jnp.max(s, axis=-1, keepdims=True)
                p = jnp.exp(s - m)
                l = jnp.sum(p, axis=-1, keepdims=True)
                o_ref[rows, :] = _dot(p / l, _band_keys(v_ref, dil, r, n))
                lse_ref[rows, :] = jnp.broadcast_to(m + jnp.log(l), (ATT_BLOCK, HEAD_DIM))

    out_spec = pl.BlockSpec((SEQ, HEAD_DIM), lambda h: (0, h))
    return _pcall(
        kern, name=name, grid=(ATT_HEADS,),
        in_specs=[_attn_col_spec(Q_BLOCK0, grp), _attn_col_spec(K_BLOCK0, grp), _attn_col_spec(V_BLOCK0, grp)],
        out_specs=[out_spec, out_spec],
        out_shape=[jax.ShapeDtypeStruct((SEQ, ATT_OUT), F32)] * 2, args=(proj, proj, proj),
        semantics=("parallel",), carried=carried)


def _attn_weights(l0, l1, l2):
    mx = jnp.maximum(jnp.maximum(l0, l1), l2)
    e0, e1, e2 = jnp.exp(l0 - mx), jnp.exp(l1 - mx), jnp.exp(l2 - mx)
    den = e0 + e1 + e2
    return e0 / den, e1 / den, e2 / den


def _attn_merge_fwd(outs, lses, name):
    tr = ROW_BLOCK

    def kern(o0, o1, o2, l0, l1, l2, out_ref):
        a0, a1, a2 = _attn_weights(l0[...], l1[...], l2[...])
        out_ref[...] = (a0 * o0[...] + a1 * o1[...] + a2 * o2[...]).astype(out_ref.dtype)

    spec = pl.BlockSpec((tr, ATT_OUT), lambda i: (i, 0))
    return pl.pallas_call(
        kern, name=name, grid=(SEQ // tr,), in_specs=[spec] * 6, out_specs=spec,
        out_shape=jax.ShapeDtypeStruct((SEQ, ATT_OUT), BF16),
        compiler_params=_cparams(("parallel",)),
    )(*outs, *lses)


def _attn_merge_bwd(outs, lses, do_att, name, carried=()):
    tr = ROW_BLOCK

    def kern(o0, o1, o2, l0, l1, l2, do_ref, d0, d1, d2, t0, t1, t2):
        alphas = _attn_weights(l0[...], l1[...], l2[...])
        do = do_ref[...]
        o_att = alphas[0] * o0[...] + alphas[1] * o1[...] + alphas[2] * o2[...]
        prod = do * o_att
        parts = []
        for h in range(ATT_HEADS):
            sl = slice(h * HEAD_DIM, (h + 1) * HEAD_DIM)
            tot = jnp.sum(prod[:, sl], axis=-1, keepdims=True)
            parts.append(jnp.broadcast_to(tot, (tr, HEAD_DIM)))
        dd = jnp.concatenate(parts, axis=1)
        for a, d_ref, t_ref in zip(alphas, (d0, d1, d2), (t0, t1, t2)):
            d_ref[...] = a * do
            t_ref[...] = -a * dd

    spec = pl.BlockSpec((tr, ATT_OUT), lambda i: (i, 0))
    res, cres = _pcall(
        kern, name=name, grid=(SEQ // tr,), in_specs=[spec] * 7, out_specs=[spec] * 6,
        out_shape=[jax.ShapeDtypeStruct((SEQ, ATT_OUT), F32)] * 6, args=(*outs, *lses, do_att),
        semantics=("parallel",), carried=carried)
    return (res[:3], res[3:]), cres


def _attn_bwd(proj, grp, lse, do_g, dl_g, name, carried=()):
    _, dil = ATT_GROUPS[grp]
    nb = SEQ // dil // ATT_BLOCK

    def kern(q_ref, k_ref, v_ref, do_ref, lse_ref, dl_ref, dq_ref, dk_ref, dv_ref, dq_acc, dk_acc, dv_acc):
        dk_acc[...] = jnp.zeros_like(dk_acc)
        dv_acc[...] = jnp.zeros_like(dv_acc)
        for r in range(dil):
            for n in range(nb):
                rows = _residue_rows(dil, r, n)
                q, do = q_ref[rows, :], do_ref[rows, :]
                kk, vv = _band_keys(k_ref, dil, r, n), _band_keys(v_ref, dil, r, n)
                s = _dot_nt(q, kk) * ATT_SCALE
                p = jnp.where(_band_mask(n > 0), jnp.exp(s - lse_ref[rows, :][:, :1]), 0.0)
                ds = p * (_dot_nt(do, vv) + dl_ref[rows, :][:, :1])
                dq_acc[rows, :] = _dot(ds, kk) * ATT_SCALE
                dk = _dot_tn(ds, q) * ATT_SCALE
                dv = _dot_tn(p, do)
                if n > 0:
                    prev = _residue_rows(dil, r, n - 1)
                    dk_acc[prev, :] += dk[:ATT_BLOCK]
                    dv_acc[prev, :] += dv[:ATT_BLOCK]
                    dk, dv = dk[ATT_BLOCK:], dv[ATT_BLOCK:]
                dk_acc[rows, :] += dk
                dv_acc[rows, :] += dv
        dq_ref[...] = dq_acc[...].astype(dq_ref.dtype)
        dk_ref[...] = dk_acc[...].astype(dk_ref.dtype)
        dv_ref[...] = dv_acc[...].astype(dv_ref.dtype)

    spec = pl.BlockSpec((SEQ, HEAD_DIM), lambda h: (0, h))
    return _pcall(
        kern, name=name, grid=(ATT_HEADS,),
        in_specs=[_attn_col_spec(Q_BLOCK0, grp), _attn_col_spec(K_BLOCK0, grp), _attn_col_spec(V_BLOCK0, grp),
                  spec, spec, spec],
        out_specs=[spec] * 3,
        out_shape=[jax.ShapeDtypeStruct((SEQ, ATT_OUT), BF16)] * 3, args=(proj, proj, proj, do_g, lse, dl_g),
        scratch_shapes=[pltpu.VMEM((SEQ, HEAD_DIM), F32)] * 3,
        semantics=("parallel",), carried=carried)


HG_HEADS_PER_STEP = 8
HG_BLOCK_W = 4 * HEAD_DIM
HG_BLOCKS = HG_HEADS_PER_STEP * HEAD_DIM // HG_BLOCK_W
HG_STEP_W = HG_HEADS_PER_STEP * HEAD_DIM
HG_Q_BLK = (3 * ATT_WIDTH) // HG_BLOCK_W
HG_N_CHUNKS = SEQ // HG_CHUNK
HG_MID = HG_CHUNK // 2


def _lower_bound(lb_ref, sl):
    l0, l1 = lb_ref[0:1, sl], lb_ref[1:2, sl]
    mx = jnp.maximum(l0, l1)
    e0, e1 = jnp.exp(l0 - mx), jnp.exp(l1 - mx)
    return e0 / (e0 + e1)


def _tri(lower):
    i = lax.broadcasted_iota(jnp.int32, (HG_CHUNK, HG_CHUNK), 0)
    j = lax.broadcasted_iota(jnp.int32, (HG_CHUNK, HG_CHUNK), 1)
    return (i >= j) if lower else (i <= j)


def _head_mean(x):
    parts = []
    for hd in range(x.shape[1] // HEAD_DIM):
        m = jnp.mean(x[:, hd * HEAD_DIM:(hd + 1) * HEAD_DIM], axis=-1, keepdims=True)
        parts.append(jnp.broadcast_to(m, (x.shape[0], HEAD_DIM)))
    return jnp.concatenate(parts, axis=1)


def _hg_chunk_terms(qh, fh, lb):
    sig = _sigmoid(fh)
    f = lb + (1.0 - lb) * sig
    k = 1.0 - f
    b = _dot_exact(_tri(True).astype(F32), jnp.log(f))
    bl = b[HG_CHUNK - 1:HG_CHUNK, :]
    br = b[HG_MID:HG_MID + 1, :]
    sq = _sigmoid(qh)
    q = qh * sq
    return dict(sig=sig, f=f, k=k, b=b, bl=bl, br=br, sq=sq, q=q,
                e1=jnp.exp(bl - b), e2=jnp.exp(b), e3=jnp.exp(b - br), e4=jnp.exp(br - b))


def _hg_fwd(proj, lbw, normw, name, carried=()):
    def in_blks(off):
        return [pl.BlockSpec((HG_CHUNK, HG_BLOCK_W), lambda hp, n, b=b: (n, HG_Q_BLK + off + hp * HG_BLOCKS + b))
                for b in range(HG_BLOCKS)]

    def kern(*refs):
        q_refs, f_refs, i_refs, g_refs = (refs[k * HG_BLOCKS:(k + 1) * HG_BLOCKS] for k in range(4))
        lb_ref, nw_ref, oraw_ref, ohg_ref, st_ref, state = refs[4 * HG_BLOCKS:]

        @pl.when(pl.program_id(1) == 0)
        def _():
            state[...] = jnp.zeros_like(state)

        causal = _tri(True)
        wide = lambda rs: jnp.concatenate([r[...] for r in rs], axis=1)
        t = _hg_chunk_terms(wide(q_refs), wide(f_refs), _lower_bound(lb_ref, slice(None)))
        v, gh = wide(i_refs), wide(g_refs)
        kd, qb, qr, kr = t["k"] * t["e1"], t["q"] * t["e2"], t["q"] * t["e3"], t["k"] * t["e4"]
        decay = jnp.exp(t["bl"])
        outs = []
        for hd in range(HG_HEADS_PER_STEP):
            sl = slice(hd * HEAD_DIM, (hd + 1) * HEAD_DIM)
            st = state[hd]
            st_ref[0, hd] = st
            a = jnp.where(causal, _dot_nt(qr[:, sl], kr[:, sl]), 0.0)
            outs.append(_dot_nt(qb[:, sl], st) + _dot(a, v[:, sl]))
            state[hd] = st * decay[:, sl] + _dot_tn(v[:, sl], kd[:, sl])
        o = jnp.concatenate(outs, axis=1)
        oraw_ref[...] = o
        r = lax.rsqrt(_head_mean(o * o) + RMS_EPS)
        nw = jnp.tile(nw_ref[...], (1, HG_HEADS_PER_STEP))
        ohg_ref[...] = (o * r * nw * (gh * _sigmoid(gh))).astype(ohg_ref.dtype)

    out_blk = pl.BlockSpec((HG_CHUNK, HG_STEP_W), lambda hp, n: (n, hp))
    return _pcall(
        kern, name=name, grid=(HG_HEADS // HG_HEADS_PER_STEP, HG_N_CHUNKS),
        in_specs=[*in_blks(0), *in_blks(2), *in_blks(4), *in_blks(6),
                  pl.BlockSpec((2, HG_STEP_W), lambda hp, n: (0, hp)),
                  pl.BlockSpec((1, HEAD_DIM), lambda hp, n: (0, 0))],
        out_specs=[out_blk, out_blk,
                   pl.BlockSpec((1, HG_HEADS_PER_STEP, HEAD_DIM, HEAD_DIM), lambda hp, n: (n, hp, 0, 0))],
        out_shape=[jax.ShapeDtypeStruct((SEQ, HG_WIDTH), F32), jax.ShapeDtypeStruct((SEQ, HG_WIDTH), BF16),
                   jax.ShapeDtypeStruct((HG_N_CHUNKS, HG_HEADS, HEAD_DIM, HEAD_DIM), F32)],
        args=(*[proj] * (4 * HG_BLOCKS), lbw, normw),
        scratch_shapes=[pltpu.VMEM((HG_HEADS_PER_STEP, HEAD_DIM, HEAD_DIM), F32)],
        semantics=("parallel", "arbitrary"), carried=carried)


def _hg_bwd(proj, lbw, normw, oraw, states, do_hg, name, carried=()):
    last = HG_N_CHUNKS - 1

    def in_blks(off):
        return [pl.BlockSpec((HG_CHUNK, HG_BLOCK_W),
                             lambda hp, n, b=b: (last - n, HG_Q_BLK + off + hp * HG_BLOCKS + b))
                for b in range(HG_BLOCKS)]

    blk = pl.BlockSpec((HG_CHUNK, HG_STEP_W), lambda hp, n: (last - n, hp))

    def kern(*refs):
        q_refs, f_refs, i_refs, g_refs = (refs[k * HG_BLOCKS:(k + 1) * HG_BLOCKS] for k in range(4))
        (lb_ref, nw_ref, oraw_ref, st_ref, do_ref, dq_ref, df_ref, di_ref, dg_ref, dlb_ref, dnw_ref,
         dstate) = refs[4 * HG_BLOCKS:]
        first = pl.program_id(1) == 0

        @pl.when(first)
        def _():
            dstate[...] = jnp.zeros_like(dstate)

        causal = _tri(True)
        wide = lambda rs: jnp.concatenate([r[...] for r in rs], axis=1)
        cat = lambda parts: jnp.concatenate(parts, axis=1)
        qh, fh, v, gh = wide(q_refs), wide(f_refs), wide(i_refs), wide(g_refs)
        o, dout = oraw_ref[...], do_ref[...]
        nw = jnp.tile(nw_ref[...], (1, HG_HEADS_PER_STEP))
        sgg = _sigmoid(gh)
        r = lax.rsqrt(_head_mean(o * o) + RMS_EPS)
        xn = o * r
        dg_ref[...] = (dout * xn * nw * (sgg * (1.0 + gh * (1.0 - sgg)))).astype(dg_ref.dtype)
        don = dout * (gh * sgg)
        dnw_wide = jnp.sum(don * xn, axis=0, keepdims=True)
        dnw_tot = dnw_wide[:, :HEAD_DIM]
        for hd in range(1, HG_HEADS_PER_STEP):
            dnw_tot = dnw_tot + dnw_wide[:, hd * HEAD_DIM:(hd + 1) * HEAD_DIM]
        tt = don * nw
        do = r * (tt - xn * _head_mean(tt * xn))
        lb = _lower_bound(lb_ref, slice(None))
        t = _hg_chunk_terms(qh, fh, lb)
        k, q = t["k"], t["q"]
        kd, qb, qr, kr = k * t["e1"], q * t["e2"], q * t["e3"], k * t["e4"]
        decay = jnp.exp(t["bl"])
        dqb, dqr, dkr, dkd, dv, ddecay = [], [], [], [], [], []
        for hd in range(HG_HEADS_PER_STEP):
            sl = slice(hd * HEAD_DIM, (hd + 1) * HEAD_DIM)
            st = st_ref[0, hd]
            dstn = dstate[hd]
            a = jnp.where(causal, _dot_nt(qr[:, sl], kr[:, sl]), 0.0)
            da = jnp.where(causal, _dot_nt(do[:, sl], v[:, sl]), 0.0)
            dqb.append(_dot(do[:, sl], st))
            dv.append(_dot_tn(a, do[:, sl]) + _dot_nt(kd[:, sl], dstn))
            dqr.append(_dot(da, kr[:, sl]))
            dkr.append(_dot_tn(da, qr[:, sl]))
            dkd.append(_dot(v[:, sl], dstn))
            ddecay.append(jnp.sum(dstn * st, axis=0, keepdims=True))
            dstate[hd] = dstn * decay[:, sl] + _dot_tn(do[:, sl], qb[:, sl])
        dqb, dqr, dkr, dkd, dv, ddecay = cat(dqb), cat(dqr), cat(dkr), cat(dkd), cat(dv), cat(ddecay)
        dq = dqb * t["e2"] + dqr * t["e3"]
        dk = dkd * t["e1"] + dkr * t["e4"]
        db = dqb * qb + dqr * qr - dkr * kr - dkd * kd
        dbl = jnp.sum(dkd * kd, axis=0, keepdims=True) + ddecay * decay
        dbr = jnp.sum(dkr * kr - dqr * qr, axis=0, keepdims=True)
        rows = lax.broadcasted_iota(jnp.int32, db.shape, 0)
        dlf = _dot_exact(_tri(False).astype(F32), db) + dbl + jnp.where(rows <= HG_MID, dbr, 0.0)
        df = dlf / t["f"] - dk
        sig, sq = t["sig"], t["sq"]
        df_ref[...] = (df * (1.0 - lb) * sig * (1.0 - sig)).astype(df_ref.dtype)
        dlb_row = jnp.sum(df * (1.0 - sig), axis=0, keepdims=True)
        dq_ref[...] = (dq * (sq * (1.0 + qh * (1.0 - sq)))).astype(dq_ref.dtype)
        di_ref[...] = dv.astype(di_ref.dtype)
        dnw_blk = jnp.broadcast_to(dnw_tot, (8, HEAD_DIM))

        @pl.when(first)
        def _():
            dlb_ref[...] = dlb_row
            dnw_ref[...] = dnw_blk

        @pl.when(jnp.logical_not(first))
        def _():
            dlb_ref[...] += dlb_row
            dnw_ref[...] += dnw_blk

    n_hp = HG_HEADS // HG_HEADS_PER_STEP
    outs, cres = _pcall(
        kern, name=name, grid=(n_hp, HG_N_CHUNKS),
        in_specs=[*in_blks(0), *in_blks(2), *in_blks(4), *in_blks(6),
                  pl.BlockSpec((2, HG_STEP_W), lambda hp, n: (0, hp)),
                  pl.BlockSpec((1, HEAD_DIM), lambda hp, n: (0, 0)),
                  blk,
                  pl.BlockSpec((1, HG_HEADS_PER_STEP, HEAD_DIM, HEAD_DIM), lambda hp, n: (last - n, hp, 0, 0)),
                  blk],
        out_specs=[blk, blk, blk, blk,
                   pl.BlockSpec((1, HG_STEP_W), lambda hp, n: (0, hp)),
                   pl.BlockSpec((8, HEAD_DIM), lambda hp, n: (hp, 0))],
        out_shape=[jax.ShapeDtypeStruct((SEQ, HG_WIDTH), BF16)] * 4
        + [jax.ShapeDtypeStruct((1, HG_WIDTH), F32), jax.ShapeDtypeStruct((8 * n_hp, HEAD_DIM), F32)],
        args=(*[proj] * (4 * HG_BLOCKS), lbw, normw, oraw, states, do_hg),
        scratch_shapes=[pltpu.VMEM((HG_HEADS_PER_STEP, HEAD_DIM, HEAD_DIM), F32)],
        semantics=("parallel", "arbitrary"), carried=carried)
    dqh, dfh, dih, dgh, dlb, dnw = outs
    return (dqh, dfh, dih, dgh, dlb, [dnw[8 * i:8 * i + 1] for i in range(n_hp)]), cres


GATE_BLOCK_W = 512
GATE_A_BLK = (3 * ATT_WIDTH + 4 * HG_WIDTH) // GATE_BLOCK_W
GATE_B_BLK = GATE_A_BLK + D_MODEL // GATE_BLOCK_W


GATE_TILE = (1024, GATE_BLOCK_W)


def _gate_ins(proj, ya, yb=None):
    ins = [(proj, GATE_TILE, lambda i, j: (i, GATE_A_BLK + j)), (proj, GATE_TILE, lambda i, j: (i, GATE_B_BLK + j)),
           (ya, GATE_TILE)]
    return ins + ([(yb, GATE_TILE)] if yb is not None else [])


def _branch_b_gate(o_hg, w_b, proj, ya, name, carried=()):
    def post(yb, ga, gb, ya_):
        return yb, _sigmoid(ga) * ya_ + _sigmoid(gb) * yb

    return _mm(o_hg, w_b, mode="nn", out_dtype=F32, name=name, carried=carried, fused=dict(
        tile=GATE_TILE, ins=_gate_ins(proj, ya), post=post,
        outs=[(jax.ShapeDtypeStruct((SEQ, D_MODEL), F32), GATE_TILE), (jax.ShapeDtypeStruct((SEQ, D_MODEL), BF16), GATE_TILE)]))


def _dmerged_gate_bwd(dx1, w_out, proj, ya, yb, name, carried=()):
    def post(dm, ga, gb, ya_, yb_):
        sa, sb = _sigmoid(ga), _sigmoid(gb)
        return dm * sa, dm * sb, dm * ya_ * sa * (1.0 - sa), dm * yb_ * sb * (1.0 - sb)

    return _mm(dx1, w_out, mode="nt", out_dtype=BF16, name=name, carried=carried, fused=dict(
        tile=GATE_TILE, ins=_gate_ins(proj, ya, yb), post=post,
        outs=[(jax.ShapeDtypeStruct((SEQ, D_MODEL), BF16), GATE_TILE)] * 4))


NORM_TILE = (512, D_MODEL)


def _residual_norm(a, w, x, g, name, carried=()):
    def post(part, x_, g_):
        xn = part + x_
        r = lax.rsqrt(jnp.mean(xn * xn, axis=-1, keepdims=True) + RMS_EPS)
        return xn, xn * r * g_

    return _mm(a, w, mode="nn", out_dtype=F32, name=name, carried=carried, fused=dict(
        tile=NORM_TILE, ins=[(x, NORM_TILE), (g, (1, D_MODEL), lambda i, j: (0, j))], post=post,
        outs=[(jax.ShapeDtypeStruct((SEQ, D_MODEL), F32), NORM_TILE), (jax.ShapeDtypeStruct((SEQ, D_MODEL), BF16), NORM_TILE)]))


def _norm_bwd_residual(dq, w, x, g, res, name, carried=()):
    def post(dh, x_, g_, res_):
        r = lax.rsqrt(jnp.mean(x_ * x_, axis=-1, keepdims=True) + RMS_EPS)
        xn = x_ * r
        dhg = dh * g_
        dx = r * (dhg - xn * jnp.mean(dhg * xn, axis=-1, keepdims=True))
        return dx + res_, jnp.sum(dh * xn, axis=0, keepdims=True)

    vec = ((1, D_MODEL), lambda i, j: (0, j))
    return _mm(dq, w, mode="nt", out_dtype=F32, name=name, carried=carried, fused=dict(
        tile=NORM_TILE, ins=[(x, NORM_TILE), (g, *vec), (res, NORM_TILE)], post=post,
        outs=[(jax.ShapeDtypeStruct((SEQ, D_MODEL), F32), NORM_TILE), (jax.ShapeDtypeStruct((1, D_MODEL), F32), *vec)]))


FF_SHARD = D_FF // N_CHIPS


FF_TILE_ROWS = 512


def _swiglu_tile(ab):
    a, b = ab[:, :FF_SHARD], ab[:, FF_SHARD:]
    return a * _sigmoid(a) * b


def _swiglu_grad_tile(du, ab):
    a, b = ab[:, :FF_SHARD], ab[:, FF_SHARD:]
    sg = _sigmoid(a)
    return jnp.concatenate([du * b * (sg * (1.0 + a * (1.0 - sg))), du * (a * sg)], axis=1)


def _ff_up(hf, w13, name, carried=()):
    wide, narrow = (FF_TILE_ROWS, 2 * FF_SHARD), (FF_TILE_ROWS, FF_SHARD)
    return _mm(hf, w13, mode="nn", out_dtype=F32, name=name, carried=carried, fused=dict(
        tile=wide, ins=[],
        outs=[(jax.ShapeDtypeStruct((SEQ, 2 * D_FF), F32), wide), (jax.ShapeDtypeStruct((SEQ, D_FF), BF16), narrow)],
        post=lambda p: (p, _swiglu_tile(p))))


def _ff_down_bwd(dx3, w2, ab, name, carried=()):
    wide, narrow = (FF_TILE_ROWS, 2 * FF_SHARD), (FF_TILE_ROWS, FF_SHARD)
    out, res = _mm(dx3, w2, mode="nt", out_dtype=BF16, name=name, carried=carried, fused=dict(
        tile=narrow, ins=[(ab, wide)], outs=[(jax.ShapeDtypeStruct((SEQ, 2 * D_FF), BF16), wide)],
        post=lambda du, ab_: (_swiglu_grad_tile(du, ab_),)))
    return out[0], res


CROSS_ROWS = 512


def _cross_fwd(qc, kvc, name, carried=()):
    def kern(q_ref, k_ref, v_ref, o_ref):
        s = _dot_nt(q_ref[...], k_ref[...]) * ATT_SCALE
        m = jnp.max(s, axis=-1, keepdims=True)
        e = jnp.exp(s - m)
        p = e / jnp.sum(e, axis=-1, keepdims=True)
        o_ref[...] = _dot(p, v_ref[...]).astype(o_ref.dtype)

    qblk = pl.BlockSpec((CROSS_ROWS, HEAD_DIM), lambda h, i: (i, h))
    return _pcall(
        kern, name=name, grid=(CROSS_HEADS, SEQ // CROSS_ROWS),
        in_specs=[qblk, pl.BlockSpec((MEM_LEN, HEAD_DIM), lambda h, i: (0, h)),
                  pl.BlockSpec((MEM_LEN, HEAD_DIM), lambda h, i: (0, CROSS_HEADS + h))],
        out_specs=qblk, out_shape=jax.ShapeDtypeStruct((SEQ, CROSS_WIDTH), BF16), args=(qc, kvc, kvc),
        semantics=("parallel", "parallel"), carried=carried)


def _cross_bwd(qc, kvc, doc, name):
    def kern(q_ref, k_ref, v_ref, do_ref, dq_ref, dk_ref, dv_ref):
        q, k, v, do = q_ref[...], k_ref[...], v_ref[...], do_ref[...]
        s = _dot_nt(q, k) * ATT_SCALE
        m = jnp.max(s, axis=-1, keepdims=True)
        e = jnp.exp(s - m)
        p = e / jnp.sum(e, axis=-1, keepdims=True)
        dp = _dot_nt(do, v)
        ds = p * (dp - jnp.sum(dp * p, axis=-1, keepdims=True))
        dq_ref[...] = (_dot(ds, k) * ATT_SCALE).astype(dq_ref.dtype)
        dk = _dot_tn(ds, q) * ATT_SCALE
        dv = _dot_tn(p, do)

        @pl.when(pl.program_id(1) == 0)
        def _():
            dk_ref[...] = dk
            dv_ref[...] = dv

        @pl.when(pl.program_id(1) > 0)
        def _():
            dk_ref[...] += dk
            dv_ref[...] += dv

    qblk = pl.BlockSpec((CROSS_ROWS, HEAD_DIM), lambda h, i: (i, h))
    kblk = pl.BlockSpec((MEM_LEN, HEAD_DIM), lambda h, i: (0, h))
    dq, dk, dv = pl.pallas_call(
        kern, name=name, grid=(CROSS_HEADS, SEQ // CROSS_ROWS),
        in_specs=[qblk, kblk, pl.BlockSpec((MEM_LEN, HEAD_DIM), lambda h, i: (0, CROSS_HEADS + h)), qblk],
        out_specs=[qblk, kblk, kblk],
        out_shape=[jax.ShapeDtypeStruct((SEQ, CROSS_WIDTH), BF16),
                   jax.ShapeDtypeStruct((MEM_LEN, CROSS_WIDTH), F32),
                   jax.ShapeDtypeStruct((MEM_LEN, CROSS_WIDTH), F32)],
        compiler_params=_cparams(("parallel", "arbitrary")),
    )(qc, kvc, kvc, doc)
    return dq, jnp.concatenate([dk, dv], axis=1)


FULL_SPECS = {
    "w_in": ("col", D_MODEL, IN_WIDTH),
    "w_branch_a": ("col", ATT_OUT, D_MODEL),
    "w_branch_b": ("col", HG_WIDTH, D_MODEL),
    "w_out": ("row", D_MODEL, D_MODEL),
    "wq_cross": ("row", D_MODEL, CROSS_WIDTH),
    "wkv_cross": ("row", D_MODEL, 2 * CROSS_WIDTH),
    "wo_cross": ("col", CROSS_WIDTH, D_MODEL),
    "w13": ("col", D_MODEL, 2 * D_FF),
    "w2": ("row", D_FF, D_MODEL),
}
WEIGHT_PLACE = {
    "w_in": ("w_in", 0), "w_branch_a": ("w_branch_a", 0), "w_branch_b": ("w_branch_b", 0),
    "w_out": ("w_out", 0), "wq_cross": ("wq_cross", 0), "wkv_cross": ("wkv_cross", 0),
    "wo_cross": ("wo_cross", 0), "w1": ("w13", 0), "w3": ("w13", FF_SHARD), "w2": ("w2", 0),
}
BIG_WEIGHTS = tuple(WEIGHT_PLACE)
EW_BLOCK_ELEMS = 512 * 1024


def _position():
    return lax.axis_index("x"), lax.axis_index("y"), lax.axis_index("c")


def _other_chips(x, y):
    return [(1 - x, y), (x, 1 - y), (1 - x, 1 - y)]


def _half(ref, kind, h):
    r, c = ref.shape
    if kind == "col":
        return ref.at[pl.ds(h * (r // 2), r // 2), :]
    return ref.at[:, pl.ds(h * (c // 2), c // 2)]


def _shard_of(ref, kind, start, size):
    return ref.at[:, pl.ds(start, size)] if kind == "col" else ref.at[pl.ds(start, size), :]


def _rows_of(ref, r0, nrows):
    return ref if nrows is None else ref.at[pl.ds(r0, nrows), :]


def _half_shape(kind, rows, cols):
    return (rows // 2, cols) if kind == "col" else (rows, cols // 2)


def _slot_shape(spec):
    kind, rows, cols = spec
    hr, hc = _half_shape(kind, rows, cols)
    return (hr, hc // N_CHIPS) if kind == "col" else (hr // N_CHIPS, hc)


def _remote(src, dst, send_sem, recv_sem, device):
    return pltpu.make_async_remote_copy(src_ref=src, dst_ref=dst, send_sem=send_sem, recv_sem=recv_sem,
                                        device_id=device, device_id_type=MESH)


def _gather_ici_comm(fulls, jobs, specs):
    def piece(refs, job, chip, c):
        f, r0, nr = job
        kind, rows, cols = specs[f]
        stride = (cols if kind == "col" else rows) // N_CHIPS
        return _rows_of(_half(_shard_of(refs[f], kind, chip * stride, stride), kind, c), r0, nr)

    def start(refs, ss, rs):
        x, y, c = _position()
        j = 2 * x + y
        for q, job in enumerate(jobs):
            for p, (px, py) in enumerate(_other_chips(x, y)):
                _remote(piece(refs, job, j, c), piece(refs, job, j, c), ss.at[3 * q + p], rs.at[3 * q + p],
                        (px, py, c)).start()

    def finish(refs, ss, rs):
        x, y, c = _position()
        j = 2 * x + y
        for q, job in enumerate(jobs):
            for p, (px, py) in enumerate(_other_chips(x, y)):
                _remote(piece(refs, job, j, c), piece(refs, job, 2 * px + py, c), ss.at[3 * q + p],
                        rs.at[3 * q + p], (px, py, c)).wait_recv()
        for q, job in enumerate(jobs):
            for p, (px, py) in enumerate(_other_chips(x, y)):
                _remote(piece(refs, job, j, c), piece(refs, job, j, c), ss.at[3 * q + p], rs.at[3 * q + p],
                        (px, py, c)).wait_send()

    names = list(dict.fromkeys(job[0] for job in jobs))
    return _Carried({f: fulls[f] for f in names}, {}, 3 * len(jobs), start, finish)


def _gather_ring_comm(fulls, f, r0, nr, phase, specs):
    kind, _, cols = specs[f]
    assert kind == "col" and nr % 32 == 0
    stride = cols // N_CHIPS
    half = nr // 2

    def rows(refs, chip, c, lo, n):
        return _rows_of(_half(_shard_of(refs[f], kind, chip * stride, stride), kind, c), r0 + lo, n)

    def copies(refs, ss, rs):
        x, y, c = _position()
        me, nx, ny, dg = 2 * x + y, 2 * (1 - x) + y, 2 * x + (1 - y), 2 * (1 - x) + (1 - y)
        to_x, to_y = (1 - x, y, c), (x, 1 - y, c)
        if phase == "a":
            mine = rows(refs, me, c, 0, nr)
            return [(_remote(mine, mine, ss.at[0], rs.at[0], to_x), rows(refs, nx, c, 0, nr)),
                    (_remote(mine, mine, ss.at[1], rs.at[1], to_y), rows(refs, ny, c, 0, nr))]
        up, low = rows(refs, ny, c, half, half), rows(refs, nx, c, 0, half)
        return [(_remote(up, up, ss.at[0], rs.at[0], to_x), rows(refs, dg, c, half, half)),
                (_remote(low, low, ss.at[1], rs.at[1], to_y), rows(refs, dg, c, 0, half))]

    def start(refs, ss, rs):
        for cp, _ in copies(refs, ss, rs):
            cp.start()

    def finish(refs, ss, rs):
        x, y, c = _position()
        mine = copies(refs, ss, rs)
        for i, (_, landing) in enumerate(mine):
            _remote(landing, landing, ss.at[i], rs.at[i], (x, y, c)).wait_recv()
        for cp, _ in mine:
            cp.wait_send()

    return _Carried({f: fulls[f]}, {}, 2, start, finish)


def _gather_d2d_comm(fulls, jobs, specs):
    def rect(refs, job, h):
        f, r0, nr = job
        assert nr is None or specs[f][0] == "col"
        return _rows_of(_half(refs[f], specs[f][0], h), r0, nr)

    def start(refs, ss, rs):
        x, y, c = _position()
        for q, job in enumerate(jobs):
            _remote(rect(refs, job, c), rect(refs, job, c), ss.at[q], rs.at[q], (x, y, 1 - c)).start()

    def finish(refs, ss, rs):
        x, y, c = _position()
        for q, job in enumerate(jobs):
            _remote(rect(refs, job, 1 - c), rect(refs, job, 1 - c), ss.at[q], rs.at[q], (x, y, 1 - c)).wait_recv()
        for q, job in enumerate(jobs):
            _remote(rect(refs, job, c), rect(refs, job, c), ss.at[q], rs.at[q], (x, y, 1 - c)).wait_send()

    names = list(dict.fromkeys(job[0] for job in jobs))
    return _Carried({f: fulls[f] for f in names}, {}, len(jobs), start, finish)


def _pairx_comm(grads, names, specs, whole=False):
    def copies(refs, ss, rs):
        x, y, c = _position()
        src = (lambda f: refs[("g", f)]) if whole else (lambda f: _half(refs[("g", f)], specs[f][0], 1 - c))
        return [_remote(src(f), refs[("r", f)], ss.at[i], rs.at[i], (x, y, 1 - c)) for i, f in enumerate(names)]

    def start(refs, ss, rs):
        for cp in copies(refs, ss, rs):
            cp.start()

    def finish(refs, ss, rs):
        for cp in copies(refs, ss, rs):
            cp.wait_recv()
        for cp in copies(refs, ss, rs):
            cp.wait_send()

    fresh = {("r", f): jax.ShapeDtypeStruct(_half_shape(*specs[f]), BF16) for f in names}
    return _Carried({}, fresh, len(names), start, finish, reads={("g", f): grads[f] for f in names})


def _chipx_comm(pair_sums, slots, jobs, specs):
    def copies(refs, ss, rs):
        x, y, c = _position()
        out = []
        for q, (f, r0, nr) in enumerate(jobs):
            kind = specs[f][0]
            width = _slot_shape(specs[f])[1 if kind == "col" else 0]
            for p, (px, py) in enumerate(_other_chips(x, y)):
                src = _rows_of(_shard_of(refs[("p", f)], kind, (2 * px + py) * width, width), r0, nr)
                dst = _rows_of(refs[("s", f)].at[p], r0, nr)
                out.append(_remote(src, dst, ss.at[3 * q + p], rs.at[3 * q + p], (px, py, c)))
        return out

    def start(refs, ss, rs):
        for cp in copies(refs, ss, rs):
            cp.start()

    def finish(refs, ss, rs):
        for cp in copies(refs, ss, rs):
            cp.wait_recv()
        for cp in copies(refs, ss, rs):
            cp.wait_send()

    names = list(dict.fromkeys(job[0] for job in jobs))
    arrays = {("p", f): pair_sums[f] for f in names}
    arrays.update({("s", f): slots[f] for f in names})
    return _Carried(arrays, {}, 3 * len(jobs), start, finish)


def _share_comm(grads, wnames, specs, place):
    def start(refs, ss, rs):
        x, y, c = _position()
        for i, w in enumerate(wnames):
            kind = specs[place[w][0]][0]
            _remote(_half(refs[w], kind, c), _half(refs[w], kind, c), ss.at[i], rs.at[i], (x, y, 1 - c)).start()

    def finish(refs, ss, rs):
        x, y, c = _position()
        for i, w in enumerate(wnames):
            kind = specs[place[w][0]][0]
            _remote(_half(refs[w], kind, 1 - c), _half(refs[w], kind, 1 - c), ss.at[i], rs.at[i],
                    (x, y, 1 - c)).wait_recv()
        for i, w in enumerate(wnames):
            kind = specs[place[w][0]][0]
            _remote(_half(refs[w], kind, c), _half(refs[w], kind, c), ss.at[i], rs.at[i], (x, y, 1 - c)).wait_send()

    return _Carried({w: grads[w] for w in wnames}, {}, len(wnames), start, finish)


def _gather_rows(v, name="gather_small"):
    shape = v.shape

    def body(v_ref, out_ref, send_sem, recv_sem, loc_sem):
        x, y, c = _position()
        me = 4 * x + 2 * y + c
        flips = [(fx, fy, fc) for fx in (0, 1) for fy in (0, 1) for fc in (0, 1)][1:]

        def peer(fl):
            return tuple(1 - a if f else a for a, f in zip((x, y, c), fl))

        loc = pltpu.make_async_copy(v_ref, out_ref.at[me], loc_sem)
        loc.start()
        sends = []
        for i, fl in enumerate(flips):
            cp = _remote(v_ref, out_ref.at[me], send_sem.at[i], recv_sem.at[i], peer(fl))
            cp.start()
            sends.append(cp)
        for i, fl in enumerate(flips):
            px, py, pc = peer(fl)
            _remote(v_ref, out_ref.at[4 * px + 2 * py + pc], send_sem.at[i], recv_sem.at[i], peer(fl)).wait_recv()
        for cp in sends:
            cp.wait_send()
        loc.wait()

    return pl.pallas_call(
        body, name=name, in_specs=[ANY], out_specs=ANY,
        out_shape=jax.ShapeDtypeStruct((N_DEV,) + shape, F32),
        scratch_shapes=[pltpu.SemaphoreType.DMA((N_DEV - 1,)), pltpu.SemaphoreType.DMA((N_DEV - 1,)),
                        pltpu.SemaphoreType.DMA],
    )(v)


def _ew_block(rows, cols, elems=EW_BLOCK_ELEMS):
    tc = cols if cols <= 4096 else _div(cols, 2048, LANE)
    tr = _div(rows, max(16, elems // tc), 16)
    return tr, tc


def _mesh_scalars():
    x, y, c = _position()
    return jnp.stack([c, 2 * x + y]).astype(jnp.int32)


def _grid_spec(grid, in_specs, out_specs):
    return pltpu.PrefetchScalarGridSpec(num_scalar_prefetch=1, grid=grid, in_specs=in_specs, out_specs=out_specs)


def _cast_into_full(parts, fname, pos, specs, place, name, token=None):
    kind, rows, cols = specs[fname]
    ws = [w for w in place if place[w][0] == fname]
    if kind == "col":
        stride = cols // N_CHIPS
        hr = rows // 2
        tr = _div(hr, max(16, EW_BLOCK_ELEMS // stride), 16)
        nrb = hr // tr
        in_specs = [pl.BlockSpec((tr, parts[w].shape[1]), lambda i, pos_ref: (i + pos_ref[0] * nrb, 0)) for w in ws]
        out_spec = pl.BlockSpec((tr, stride), lambda i, pos_ref: (i + pos_ref[0] * nrb, pos_ref[1]))
    else:
        stride = rows // N_CHIPS
        hc = cols // 2
        tr = _div(stride, max(16, EW_BLOCK_ELEMS // hc), 16)
        nrb = stride // tr
        in_specs = [pl.BlockSpec((tr, hc), lambda i, pos_ref: (i, pos_ref[0])) for w in ws]
        out_spec = pl.BlockSpec((tr, hc), lambda i, pos_ref: (i + pos_ref[1] * nrb, pos_ref[0]))

    def kern(pos_ref, *refs):
        o_ref = refs[-1]
        for w, r in zip(ws, refs[:len(ws)]):
            off = place[w][1] if kind == "col" else 0
            o_ref[:, off:off + r.shape[1]] = r[...].astype(o_ref.dtype)

    tokens = [] if token is None else [token]
    in_specs = in_specs + [pl.BlockSpec(TOKEN_SHAPE, lambda i, pos_ref: (0, 0))] * len(tokens)
    return pl.pallas_call(
        kern, name=name, grid_spec=_grid_spec((nrb,), in_specs, out_spec),
        out_shape=jax.ShapeDtypeStruct((rows, cols), BF16),
        compiler_params=_cparams(("parallel",)),
    )(pos, *[parts[w] for w in ws], *tokens)


def _pair_sum(grad, recv, pos, spec, name, whole=False):
    kind, rows, cols = spec
    hr, hc = _half_shape(kind, rows, cols)
    tr, tc = _ew_block(hr, hc, 2 * EW_BLOCK_ELEMS)
    nrb, ncb = hr // tr, hc // tc
    blk = pl.BlockSpec((tr, tc), lambda i, jj, pos_ref: (i, jj))
    if whole:
        mine = blk
    elif kind == "col":
        mine = pl.BlockSpec((tr, tc), lambda i, jj, pos_ref: (i + pos_ref[0] * nrb, jj))
    else:
        mine = pl.BlockSpec((tr, tc), lambda i, jj, pos_ref: (i, jj + pos_ref[0] * ncb))

    def kern(pos_ref, g_ref, r_ref, o_ref, slots_ref):
        o_ref[...] = (g_ref[...].astype(F32) + r_ref[...].astype(F32)).astype(o_ref.dtype)

    return pl.pallas_call(
        kern, name=name, grid_spec=_grid_spec((nrb, ncb), [mine, blk], [blk, ANY]),
        out_shape=[jax.ShapeDtypeStruct((hr, hc), BF16),
                   jax.ShapeDtypeStruct((N_CHIPS - 1,) + _slot_shape(spec), BF16)],
        compiler_params=_cparams(("parallel", "parallel")),
    )(pos, grad, recv)


def _chip_sum(pair_sum, slots, pos, fname, shard_shapes, specs, place, name):
    kind, rows, cols = specs[fname]
    sr, sc = _slot_shape(specs[fname])
    ws = [w for w in place if place[w][0] == fname]
    n_slots = N_CHIPS - 1
    tr = _div(sr, max(16, EW_BLOCK_ELEMS // sc), 16)
    nrb = sr // tr
    slot = pl.BlockSpec((n_slots, tr, sc), lambda i, pos_ref: (0, i, 0))
    if kind == "col":
        own = pl.BlockSpec((tr, sc), lambda i, pos_ref: (i, pos_ref[1]))
        out_specs = [pl.BlockSpec((tr, shard_shapes[w][1]), lambda i, pos_ref: (i + pos_ref[0] * nrb, 0)) for w in ws]
    else:
        own = pl.BlockSpec((tr, sc), lambda i, pos_ref: (i + pos_ref[1] * nrb, 0))
        out_specs = [pl.BlockSpec((tr, sc), lambda i, pos_ref: (i, pos_ref[0])) for w in ws]

    def kern(pos_ref, own_ref, slot_ref, *out_refs):
        tot = own_ref[...].astype(F32)
        for s in range(n_slots):
            tot = tot + slot_ref[s].astype(F32)
        for w, o_ref in zip(ws, out_refs):
            off = place[w][1] if kind == "col" else 0
            o_ref[...] = tot[:, off:off + o_ref.shape[1]]

    outs = pl.pallas_call(
        kern, name=name, grid_spec=_grid_spec((nrb,), [own, slot], out_specs),
        out_shape=[jax.ShapeDtypeStruct(shard_shapes[w], F32) for w in ws],
        compiler_params=_cparams(("parallel",)),
    )(pos, pair_sum, slots)
    return dict(zip(ws, outs))


def _adam_math(w, g, m, v):
    m2 = ADAM_B1 * m + (1.0 - ADAM_B1) * g
    v2 = ADAM_B2 * v + (1.0 - ADAM_B2) * (g * g)
    m_hat = m2 / (1.0 - ADAM_B1 ** ADAM_STEP)
    v_hat = v2 / (1.0 - ADAM_B2 ** ADAM_STEP)
    delta = -ADAM_LR * (m_hat / (jnp.sqrt(v_hat) + ADAM_EPS) + ADAM_WD * w)
    return delta, m2, v2


def _adamw(w, g, m, v, name, carried=()):
    rows, cols = w.shape
    tr, tc = _ew_block(rows, cols)

    def kern(w_ref, g_ref, m_ref, v_ref, d_ref, m2_ref, v2_ref, g_out_ref):
        g_ = g_ref[...]
        d_ref[...], m2_ref[...], v2_ref[...] = _adam_math(w_ref[...], g_, m_ref[...], v_ref[...])
        g_out_ref[...] = g_

    blk = pl.BlockSpec((tr, tc), lambda i, j: (i, j))
    return _pcall(
        kern, name=name, grid=(rows // tr, cols // tc), in_specs=[blk] * 4, out_specs=[blk] * 4,
        out_shape=[jax.ShapeDtypeStruct((rows, cols), F32)] * 4, args=(w, g, m, v),
        semantics=("parallel", "parallel"), carried=carried)


SMALL_ROWS = ("ln_mix_w", "ln_cross_w", "ln_mem_w", "ln_ffn_w", "ln_final_w")
ROW_HG_NORM, ROW_LB0, ROW_LB1 = 5, 6, 7
LOSS_LANE0 = HEAD_DIM


def _pack_small(vals):
    rows = [vals[n].reshape(1, D_MODEL) for n in SMALL_ROWS]
    pad = lambda a: jnp.pad(a, ((0, 0), (0, D_MODEL - a.shape[1])))
    rows.append(pad(vals["hg_norm_w"].reshape(1, HEAD_DIM)))
    rows.append(pad(vals["hg_lower_bounds"].reshape(2, HG_WIDTH)))
    return jnp.concatenate(rows, axis=0)


def _small_update(gathered, w, m, v, name="small_update"):
    def kern(g_ref, w_ref, m_ref, v_ref, grad_ref, d_ref, m2_ref, v2_ref, loss_ref):
        tot = g_ref[0]
        for s in range(1, N_DEV):
            tot = tot + g_ref[s]
        wv = w_ref[...]
        row = lax.broadcasted_iota(jnp.int32, (8, D_MODEL), 0)
        lane = lax.broadcasted_iota(jnp.int32, (8, D_MODEL), 1)
        l0, l1 = wv[ROW_LB0:ROW_LB0 + 1], wv[ROW_LB1:ROW_LB1 + 1]
        mx = jnp.maximum(l0, l1)
        e0, e1 = jnp.exp(l0 - mx), jnp.exp(l1 - mx)
        p0 = e0 / (e0 + e1)
        dlog = tot[ROW_LB0:ROW_LB0 + 1] * p0 * (1.0 - p0)
        tot = jnp.where(row == ROW_HG_NORM, tot + tot[ROW_LB1:ROW_LB1 + 1], tot)
        grad = jnp.where(row == ROW_LB0, dlog, jnp.where(row == ROW_LB1, -dlog, tot))
        grad = jnp.where((row == ROW_HG_NORM) & (lane >= HEAD_DIM), 0.0, grad)
        grad = jnp.where((row >= ROW_LB0) & (lane >= HG_WIDTH), 0.0, grad)
        grad_ref[...] = grad
        d_ref[...], m2_ref[...], v2_ref[...] = _adam_math(wv, grad, m_ref[...], v_ref[...])
        loss_ref[...] = tot[ROW_HG_NORM:ROW_HG_NORM + 1, LOSS_LANE0:LOSS_LANE0 + LANE]

    full = pl.BlockSpec((8, D_MODEL), lambda: (0, 0))
    return pl.pallas_call(
        kern, name=name,
        in_specs=[pl.BlockSpec((N_DEV, 8, D_MODEL), lambda: (0, 0, 0)), full, full, full],
        out_specs=[full, full, full, full, pl.BlockSpec((1, LANE), lambda: (0, 0))],
        out_shape=[jax.ShapeDtypeStruct((8, D_MODEL), F32)] * 4 + [jax.ShapeDtypeStruct((1, LANE), F32)],
        compiler_params=_cparams(),
    )(gathered, w, m, v)


def _unpack_small(p, shapes):
    out = {n: p[i].reshape(shapes[n]) for i, n in enumerate(SMALL_ROWS)}
    out["hg_norm_w"] = p[ROW_HG_NORM, :HEAD_DIM].reshape(shapes["hg_norm_w"])
    out["hg_lower_bounds"] = p[ROW_LB0:ROW_LB1 + 1, :HG_WIDTH].reshape(shapes["hg_lower_bounds"])
    return out


WHOLE = lambda f: (f, 0, None)
MID_MATRICES = ("w_branch_a", "w_branch_b", "w_out", "wq_cross", "wkv_cross", "wo_cross")
MID_WEIGHTS = MID_MATRICES
W_IN_PIECES = [("w_in", r0, 512) for r0 in range(0, D_MODEL // 2, 512)]
W13_PIECES = [("w13", r0, 512) for r0 in range(0, D_MODEL // 2, 512)]
GATHER_GROUPS = [("mid", [WHOLE(f) for f in MID_MATRICES]), ("w13a", W13_PIECES[:1]), ("w13b", W13_PIECES[1:]),
                 ("w2", [WHOLE("w2")])]
OTHER_WEIGHTS = ["w1", "w3", "w2"] + list(MID_WEIGHTS)
BEFORE = {
    "hgrn_fwd": [("wait", "mid")],
    "mm_out": [("wait", "w13a")],
    "mm_o": [("wait", "w13b")],
    "mm_w2": [("wait", "w2"), ("run", ("d2d", [WHOLE("w2")]), "gather_hand_over_w2")],
    "mm_dh": [("wait", "rs_w2"), ("chip_sum", "w2"), ("wait", "rs_w13"), ("chip_sum", "w13"), ("wait", "rs_mid")]
    + [("chip_sum", f) for f in MID_MATRICES],
}
CARRY = {
    "hgrn_fwd": [("d2d", [WHOLE(f) for f in MID_MATRICES])],
    "mm_out": [("d2d", W13_PIECES[:1])],
    "mm_o": [("d2d", W13_PIECES[1:])],
    "mm_du": [("pairx", ["w2"])],
    "mm_dhf": [("pairx", ["w13"])],
    "attn_bwd_g0": [("pairx", list(MID_MATRICES))],
    "mm_dwin_own": [("pairx_whole", ["w_in"])],
    "mm_dh": [("share", OTHER_WEIGHTS)],
}
AFTER = {
    "mm_du": [("pair_sum", "w2"), ("start", "rs_w2", [WHOLE("w2")])],
    "mm_dhf": [("pair_sum", "w13"), ("start", "rs_w13", [WHOLE("w13")])],
    "attn_bwd_g0": [("pair_sum", f) for f in MID_MATRICES] + [("start", "rs_mid", [WHOLE(f) for f in MID_MATRICES])],
    "mm_dwin_own": [("pair_sum", "w_in"), ("start", "rs_w_in", [WHOLE("w_in")])],
}
FINISH = [
    ("adamw", OTHER_WEIGHTS), ("wait", "rs_w_in"), ("small",), ("chip_sum", "w_in"),
    ("run", ("share", ["w_in"]), "rs_sibling_share_w_in"), ("adamw", ["w_in"]),
]


class _Net:
    def __init__(self, full, pos=None, shard_shapes=None, comm=True, specs=FULL_SPECS, place=WEIGHT_PLACE):
        self.full, self.pos, self.shard_shapes, self.comm = dict(full), pos, shard_shapes, comm
        self.specs, self.place = specs, place
        self.gw, self.recv, self.psum, self.slots, self.grads = {}, {}, {}, {}, {}
        self.gw_sibling = {}
        self.pending, self.token, self.last = {}, None, None

    def _make(self, kind, arg):
        if kind == "gather":
            return _gather_ici_comm(self.full, arg, self.specs)
        if kind == "ring":
            return _gather_ring_comm(self.full, *arg, self.specs)
        if kind == "d2d":
            return _gather_d2d_comm(self.full, arg, self.specs)
        if kind == "pairx":
            return _pairx_comm(self.gw, arg, self.specs)
        if kind == "pairx_whole":
            return _pairx_comm(self.gw_sibling, arg, self.specs, whole=True)
        if kind == "chipx":
            return _chipx_comm(self.psum, self.slots, arg, self.specs)
        assert kind == "share"
        return _share_comm(self.grads, arg, self.specs, self.place)

    def _store(self, kind, res):
        if kind in ("gather", "ring", "d2d"):
            self.full.update(res)
        elif kind in ("pairx", "pairx_whole"):
            for (tag, f), a in res.items():
                (self.gw if tag == "g" else self.recv)[f] = a
        elif kind == "chipx":
            for (tag, f), a in res.items():
                (self.psum if tag == "p" else self.slots)[f] = a
        else:
            self.grads.update(res)

    def run_comm(self, item, name):
        kind, arg = item
        self._store(kind, _run_comm([self._make(kind, arg)], name)[0])

    @staticmethod
    def _others(after, items):
        own = [a for cm in items for a in cm.arrays.values()]
        return [a for a in after if a is not None and all(a is not o for o in own)]

    def start(self, groups, kind, name):
        items = [self._make(kind, jobs) for _, jobs in groups]
        after = self._others([self.last], items)
        res, sems, token = _split_start(items, name, after=after[0] if after else None)
        for (group, jobs), r, s in zip(groups, res, sems):
            self._store(kind, r)
            self.pending[group] = (kind, jobs, s)
        self.token = self.last = token

    def wait(self, group, after=()):
        kind, jobs, sems = self.pending.pop(group)
        item = self._make(kind, jobs)
        res = _split_wait([item], [sems], self._others([self.last, *after], [item]), f"wait_{group}")[0]
        self._store(kind, res)

    def step(self, step):
        if step[0] == "wait":
            self.wait(step[1])
        elif step[0] == "start":
            self.start([(step[1], step[2])], "chipx", f"start_{step[1]}")
        elif step[0] == "pair_sum":
            f = step[1]
            self.psum[f], self.slots[f] = _pair_sum(self.gw[f], self.recv[f], self.pos, self.specs[f],
                                                    f"rs_pair_sum_{f}", whole=f in self.gw_sibling)
        elif step[0] == "chip_sum":
            f = step[1]
            self.grads.update(_chip_sum(self.psum[f], self.slots[f], self.pos, f, self.shard_shapes,
                                        self.specs, self.place, f"rs_chip_sum_{f}"))
        else:
            assert step[0] == "run"
            self.run_comm(step[1], step[2])

    def call(self, fn, name, *args, grad_of=None, sibling_half=False, **kw):
        for step in (BEFORE.get(name, []) if self.comm else []):
            self.step(step)
        args = [a() if callable(a) else a for a in args]
        items = CARRY.get(name, []) if self.comm else []
        carried = [self._make(k, a) for k, a in items]
        if self.token is not None:
            carried.append(_Token(self.token))
            self.token = None
        out, res = fn(*args, name=name, carried=carried, **kw)
        if grad_of is not None:
            (self.gw_sibling if sibling_half else self.gw)[grad_of] = out
        self.last = jax.tree.leaves(out)[0]
        for (kind, _), r in zip(items, res):
            self._store(kind, r)
        for step in (AFTER.get(name, []) if self.comm else []):
            self.step(step)
        return out


def _local_step(net, x, h, mem, target, small):
    full, call = net.full, net.call
    proj = call(_mm, "mm_proj", h, full["w_in"], mode="nn", out_dtype=F32)
    att = [call(_attn_fwd, f"attn_fwd_g{g}", proj, g) for g in range(3)]
    outs, lses = [a[0] for a in att], [a[1] for a in att]
    o_att = _attn_merge_fwd(outs, lses, "attn_merge")
    oraw, o_hg, states = call(_hg_fwd, "hgrn_fwd", proj, small["hg_lower_bounds"], small["hg_norm_w"])
    ya = call(_mm, "mm_branch_a", o_att, full["w_branch_a"], mode="nn", out_dtype=F32)
    yb, merged = call(_branch_b_gate, "mm_branch_b", o_hg, full["w_branch_b"], proj, ya)
    x1, hc = call(_residual_norm, "mm_out", merged, full["w_out"], x, small["ln_cross_w"])

    mn = _rms_fwd(mem, small["ln_mem_w"], "rms_mem")
    qc = call(_mm, "mm_q", hc, full["wq_cross"], mode="nn", out_dtype=F32)
    kvc = call(_mm, "mm_kv", mn, full["wkv_cross"], mode="nn", out_dtype=F32)
    oc = call(_cross_fwd, "cross_fwd", qc, kvc)
    x2, hf = call(_residual_norm, "mm_o", oc, full["wo_cross"], x1, small["ln_ffn_w"])

    ab, u = call(_ff_up, "mm_w13", hf, full["w13"])
    x3 = call(_mm, "mm_w2", u, lambda: full["w2"], mode="nn", out_dtype=F32, res=x2)

    dx3, dg_final, loss = _loss_head(x3, small["ln_final_w"], target, "loss_head")

    gs = {"ln_final_w": dg_final}
    call(_mm, "mm_dw2", u, dx3, mode="tn", out_dtype=BF16, grad_of="w2")
    dab = call(_ff_down_bwd, "mm_du", dx3, full["w2"], ab)
    call(_mm, "mm_dw13", hf, dab, mode="tn", out_dtype=BF16, grad_of="w13")
    dhf = call(_mm, "mm_dhf", dab, full["w13"], mode="nt", out_dtype=F32)
    dx2, gs["ln_ffn_w"] = _rms_bwd(x2, small["ln_ffn_w"], dhf, dx3, "rms_ffn_bwd")
    doc = call(_mm, "mm_doc", dx2, full["wo_cross"], mode="nt", out_dtype=BF16)
    call(_mm, "mm_dwo", oc, dx2, mode="tn", out_dtype=BF16, grad_of="wo_cross")
    dqc, dkvc = _cross_bwd(qc, kvc, doc, "cross_bwd")
    call(_mm, "mm_dwq", hc, dqc, mode="tn", out_dtype=BF16, grad_of="wq_cross")
    dx1, gs["ln_cross_w"] = call(_norm_bwd_residual, "mm_dhc", dqc, full["wq_cross"], x1, small["ln_cross_w"], dx2)
    call(_mm, "mm_dwkv", mn, dkvc, mode="tn", out_dtype=BF16, grad_of="wkv_cross")
    dmn = call(_mm, "mm_dmn", dkvc, full["wkv_cross"], mode="nt", out_dtype=F32)
    _, gs["ln_mem_w"] = _rms_bwd(mem, small["ln_mem_w"], dmn, None, "rms_mem_bwd")
    dya, dyb, dga, dgb = call(_dmerged_gate_bwd, "mm_dmerged", dx1, full["w_out"], proj, ya, yb)
    call(_mm, "mm_dwout", merged, dx1, mode="tn", out_dtype=BF16, grad_of="w_out")
    call(_mm, "mm_dwa", o_att, dya, mode="tn", out_dtype=BF16, grad_of="w_branch_a")
    do_att = call(_mm, "mm_doatt", dya, full["w_branch_a"], mode="nt", out_dtype=F32)
    call(_mm, "mm_dwb", o_hg, dyb, mode="tn", out_dtype=BF16, grad_of="w_branch_b")
    do_hg = call(_mm, "mm_dohg", dyb, full["w_branch_b"], mode="nt", out_dtype=F32)
    dqh, dfh, dih, dgh, dlb, gs["hg_norm_w"] = call(
        _hg_bwd, "hgrn_bwd", proj, small["hg_lower_bounds"], small["hg_norm_w"], oraw, states, do_hg)
    gs["hg_lb"] = dlb
    do_gs, dl_gs = call(_attn_merge_bwd, "attn_merge_bwd", outs, lses, do_att)
    dqs, dks, dvs = zip(*[call(_attn_bwd, f"attn_bwd_g{g}", proj, g, lses[g], do_gs[g], dl_gs[g]) for g in range(3)])
    dproj = jnp.concatenate([*dqs, *dks, *dvs, dqh, dfh, dih, dgh, dga, dgb], axis=1)
    if net.comm:
        half = D_MODEL // 2
        c = lax.axis_index("c")
        h_sibling = lax.dynamic_slice_in_dim(h, (1 - c) * half, half, axis=1)
        h_own = lax.dynamic_slice_in_dim(h, c * half, half, axis=1)
        call(_mm, "mm_dwin_sibling", h_sibling, dproj, mode="tn", out_dtype=BF16, grad_of="w_in", sibling_half=True)
        call(_mm, "mm_dwin_own", h_own, dproj, mode="tn", out_dtype=BF16, grad_of="w_in")
    else:
        call(_mm, "mm_dwin", h, dproj, mode="tn", out_dtype=BF16, grad_of="w_in")
    dh = call(_mm, "mm_dh", dproj, full["w_in"], mode="nt", out_dtype=F32)
    dx, gs["ln_mix_w"] = _rms_bwd(x, small["ln_mix_w"], dh, dx1, "rms_mix_bwd")
    return loss, dx, gs


WEIGHT_ORDER = ("ln_mix_w", "w_in", "hg_norm_w", "hg_lower_bounds", "w_branch_a", "w_branch_b", "w_out",
                "ln_cross_w", "ln_mem_w", "wq_cross", "wkv_cross", "wo_cross", "ln_ffn_w", "w1", "w3", "w2",
                "ln_final_w")


def kernel(x, mem, ln_mix_w, w_in, hg_norm_w, hg_lower_bounds, w_branch_a, w_branch_b, w_out, ln_cross_w, ln_mem_w, wq_cross, wkv_cross, wo_cross, ln_ffn_w, w1, w3, w2, ln_final_w, loss_target, m_ln_mix_w, m_w_in, m_hg_norm_w, m_hg_lower_bounds, m_w_branch_a, m_w_branch_b, m_w_out, m_ln_cross_w, m_ln_mem_w, m_wq_cross, m_wkv_cross, m_wo_cross, m_ln_ffn_w, m_w1, m_w3, m_w2, m_ln_final_w, v_ln_mix_w, v_w_in, v_hg_norm_w, v_hg_lower_bounds, v_w_branch_a, v_w_branch_b, v_w_out, v_ln_cross_w, v_ln_mem_w, v_wq_cross, v_wkv_cross, v_wo_cross, v_ln_ffn_w, v_w1, v_w3, v_w2, v_ln_final_w):
    args = dict(locals())
    w = {n: args[n] for n in WEIGHT_ORDER}
    m = {n: args["m_" + n] for n in WEIGHT_ORDER}
    v = {n: args["v_" + n] for n in WEIGHT_ORDER}
    shapes = {n: w[n].shape for n in WEIGHT_ORDER}
    mat = lambda a: a.reshape(a.shape[-2:])
    shard_shapes = {n: shapes[n][-2:] for n in BIG_WEIGHTS}

    pos = _mesh_scalars()

    def cast(f, token=None):
        return _cast_into_full({n: mat(w[n]) for n in BIG_WEIGHTS if WEIGHT_PLACE[n][0] == f}, f, pos,
                               FULL_SPECS, WEIGHT_PLACE, f"cast_{f}", token)

    net = _Net({"w_in": cast("w_in")}, pos, shard_shapes)
    net.start([(f"ring_a{i}", (*job, "a")) for i, job in enumerate(W_IN_PIECES)], "ring", "gather_start_w_in")
    rest = {f: cast(f, net.token) for f in FULL_SPECS if f != "w_in"}
    net.full.update(rest)
    small = {n: w[n].reshape(1, -1) for n in SMALL_ROWS}
    small["hg_norm_w"] = w["hg_norm_w"].reshape(1, HEAD_DIM)
    small["hg_lower_bounds"] = w["hg_lower_bounds"]
    x2d = x.reshape(SEQ, D_MODEL)
    h = _rms_fwd(x2d, small["ln_mix_w"], "rms_mix")
    for i, job in enumerate(W_IN_PIECES):
        net.wait(f"ring_a{i}", after=[*rest.values(), h] if i == 0 else ())
        net.start([(f"ring_b{i}", (*job, "b"))], "ring", f"gather_pass_on_w_in{i}")
    net.start(GATHER_GROUPS, "gather", "gather_start_rest")
    for i, job in enumerate(W_IN_PIECES):
        net.wait(f"ring_b{i}")
        net.run_comm(("d2d", [job]), f"gather_hand_over_w_in{i}")
    loss, dx, gs = _local_step(net, x2d, h, mem.reshape(MEM_LEN, D_MODEL), loss_target.reshape(SEQ, D_MODEL), small)

    out_g, out_d, out_m, out_v = {}, {}, {}, {}
    net.last = dx
    for step in FINISH:
        if step[0] == "adamw":
            for n in step[1]:
                out_d[n], out_m[n], out_v[n], out_g[n] = net.call(_adamw, f"adamw_{n}", mat(w[n]), net.grads[n],
                                                                  mat(m[n]), mat(v[n]))
        elif step[0] == "wait":
            net.wait(step[1], after=list(out_d.values()))
        elif step[0] == "small":
            pad = lambda a: jnp.pad(a, ((0, 0), (0, D_MODEL - a.shape[1])))
            part = jnp.concatenate(
                [gs[n] for n in SMALL_ROWS]
                + [pad(jnp.concatenate([gs["hg_norm_w"][0], loss], axis=1)), pad(gs["hg_lb"]),
                   pad(gs["hg_norm_w"][1]) if len(gs["hg_norm_w"]) > 1 else jnp.zeros((1, D_MODEL), F32)], axis=0)
            part, net.psum["w_in"] = lax.optimization_barrier((part, net.psum["w_in"]))
            sg, sd, sm, sv, loss_tot = _small_update(_gather_rows(part), _pack_small(w), _pack_small(m),
                                                     _pack_small(v))
            for dst, packed in ((out_g, sg), (out_d, sd), (out_m, sm), (out_v, sv)):
                dst.update(_unpack_small(packed, shapes))
        else:
            net.step(step)

    result = [loss_tot[0, 0], dx.reshape(x.shape)]
    for group in (out_g, out_d, out_m, out_v):
        result += [group[n].reshape(shapes[n]) for n in WEIGHT_ORDER]
    return tuple(result)
```

```python
import math

import jax
import jax.numpy as jnp
from jax import lax
from jax.experimental import pallas as pl
from jax.experimental.pallas import tpu as pltpu

F32 = jnp.float32
BF16 = jnp.bfloat16
MESH = pl.DeviceIdType.MESH

D_MODEL = 2048
SEQ = 2048
HEAD_DIM = 128
MEM_LEN = 256
ATT_GROUPS = ((128, 1), (512, 4), (2048, 16))
ATT_HEADS = 4
ATT_WIDTH = 3 * ATT_HEADS * HEAD_DIM
ATT_OUT = ATT_HEADS * HEAD_DIM
ATT_BLOCK = 128
HG_HEADS = 8
HG_WIDTH = HG_HEADS * HEAD_DIM
HG_CHUNK = 64
IN_WIDTH = 3 * ATT_WIDTH + 4 * HG_WIDTH + 2 * D_MODEL
CROSS_HEADS = 4
CROSS_WIDTH = CROSS_HEADS * HEAD_DIM
D_FF = 5632
RMS_EPS = 1e-6
ADAM_LR = 0.001
ADAM_B1 = 0.9
ADAM_B2 = 0.999
ADAM_EPS = 1e-08
ADAM_WD = 0.01
ADAM_STEP = 10
N_CHIPS = 4
N_DEV = 8

VMEM_LIMIT_BYTES = 56 * 1024 * 1024
LANE = 128
MXU_WIDTH = 256
MM_TILE_CAP = 1536
TRANSPOSE_CHUNK = 512
ANY = pl.BlockSpec(memory_space=pl.ANY)


def _cparams(sem=None):
    return pltpu.CompilerParams(dimension_semantics=sem, vmem_limit_bytes=VMEM_LIMIT_BYTES)


def _div(n, cap, mult):
    best = None
    for d in range(mult, min(n, cap) + 1, mult):
        if n % d == 0:
            best = d
    assert best is not None, (n, cap, mult)
    return best


def _sigmoid(x):
    return 1.0 / (1.0 + jnp.exp(-x))


def _dot(a, b):
    return jnp.dot(a.astype(BF16), b.astype(BF16), preferred_element_type=F32)


def _dot_nt(a, b):
    return lax.dot_general(a.astype(BF16), b.astype(BF16), (((1,), (1,)), ((), ())),
                           preferred_element_type=F32)


def _dot_tn(a, b):
    return jnp.dot(a.astype(F32).T.astype(BF16), b.astype(BF16), preferred_element_type=F32)


def _dot_exact(a, b):
    return jnp.dot(a, b, precision=lax.Precision.HIGHEST, preferred_element_type=F32)


class _Carried:
    def __init__(self, arrays, fresh, n_sems, start, finish, mid=None, reads=None):
        self.arrays, self.fresh, self.n_sems, self.reads = arrays, fresh, n_sems, reads or {}
        self.start, self.mid, self.finish = start, mid, finish


class _Token:
    def __init__(self, array):
        self.array = array


TOKEN_SHAPE = (8, LANE)


def _carried_layout(carried):
    akeys = list(dict.fromkeys(k for cm in carried for k in cm.arrays))
    fkeys = [(ci, k) for ci, cm in enumerate(carried) for k in cm.fresh]
    arrays = [next(cm.arrays[k] for cm in carried if k in cm.arrays) for k in akeys]
    shapes = [jax.ShapeDtypeStruct(a.shape, a.dtype) for a in arrays] + [carried[ci].fresh[k] for ci, k in fkeys]
    sems = []
    for cm in carried:
        sems += [pltpu.SemaphoreType.DMA((cm.n_sems,)), pltpu.SemaphoreType.DMA((cm.n_sems,))]
    return akeys, fkeys, arrays, shapes, sems


def _carried_reads(carried):
    rkeys = list(dict.fromkeys(k for cm in carried for k in cm.reads))
    return rkeys, [next(cm.reads[k] for cm in carried if k in cm.reads) for k in rkeys]


def _carried_results(carried, akeys, fkeys, outs, rkeys=(), read_refs=()):
    shared = dict(zip(akeys, outs[:len(akeys)]))
    shared.update(zip(rkeys, read_refs))
    res = [{k: shared[k] for k in list(cm.arrays) + [r for r in cm.reads if r in shared]} for cm in carried]
    for (ci, k), o in zip(fkeys, outs[len(akeys):]):
        res[ci][k] = o
    return res


def _pcall(kern, *, name, grid, in_specs, out_specs, out_shape, args, scratch_shapes=(), semantics=None,
           carried=()):
    tokens = [c.array for c in carried if isinstance(c, _Token)]
    carried = [c for c in carried if not isinstance(c, _Token)]
    single = not isinstance(out_shape, (list, tuple))
    out_specs = [out_specs] if single else list(out_specs)
    out_shape = [out_shape] if single else list(out_shape)
    n_real, n_out, n_scr = len(in_specs), len(out_shape), len(scratch_shapes)
    in_specs = list(in_specs) + [pl.BlockSpec(TOKEN_SHAPE, lambda *_: (0, 0))] * len(tokens)
    args = list(args) + tokens
    n_in = len(in_specs)
    if not carried:
        def plain(*refs):
            kern(*refs[:n_real], *refs[n_in:])

        outs = pl.pallas_call(plain if tokens else kern, name=name, grid=grid, in_specs=in_specs,
                              out_specs=out_specs, out_shape=out_shape, scratch_shapes=list(scratch_shapes),
                              compiler_params=_cparams(semantics))(*args)
        return (outs[0] if single else list(outs)), []
    akeys, fkeys, arrays, shapes, sems = _carried_layout(carried)
    rkeys, reads = _carried_reads(carried)
    n_a, n_f, n_r = len(akeys), len(fkeys), len(rkeys)
    total = math.prod(grid)
    mid_step = min(total - 1, (17 * total) // 20)

    def wrapped(*refs):
        ins = refs[:n_real]
        r0 = n_in + n_a
        o0 = r0 + n_r
        outs = refs[o0:o0 + n_out]
        a0 = o0 + n_out
        s0 = a0 + n_a + n_f
        per = _carried_results(carried, akeys, fkeys, refs[a0:s0], rkeys, refs[r0:o0])
        scratch = refs[s0:s0 + n_scr]
        sem = refs[s0 + n_scr:]
        step = 0
        for d, g in enumerate(grid):
            step = step * g + pl.program_id(d)

        @pl.when(step == 0)
        def _():
            for ci, cm in enumerate(carried):
                cm.start(per[ci], sem[2 * ci], sem[2 * ci + 1])

        kern(*ins, *outs, *scratch)

        @pl.when(step == mid_step)
        def _():
            for ci, cm in enumerate(carried):
                if cm.mid is not None:
                    cm.mid(per[ci], sem[2 * ci], sem[2 * ci + 1])

        @pl.when(step == total - 1)
        def _():
            for ci, cm in enumerate(carried):
                cm.finish(per[ci], sem[2 * ci], sem[2 * ci + 1])

    outs = pl.pallas_call(
        wrapped, name=name, grid=grid,
        in_specs=list(in_specs) + [ANY] * (n_a + n_r), out_specs=out_specs + [ANY] * (n_a + n_f),
        out_shape=out_shape + shapes,
        input_output_aliases={n_in + i: n_out + i for i in range(n_a)},
        scratch_shapes=list(scratch_shapes) + sems,
        compiler_params=_cparams(("arbitrary",) * len(grid)),
    )(*args, *arrays, *reads)
    res = _carried_results(carried, akeys, fkeys, outs[n_out:])
    return (outs[0] if single else list(outs[:n_out])), res


def _run_comm(carried, name):
    carried = list(carried)
    akeys, fkeys, arrays, shapes, sems = _carried_layout(carried)
    rkeys, reads = _carried_reads(carried)
    n_a, n_f, n_r = len(akeys), len(fkeys), len(rkeys)

    def body(*refs):
        o0 = n_a + n_r
        per = _carried_results(carried, akeys, fkeys, refs[o0:o0 + n_a + n_f], rkeys, refs[n_a:o0])
        sem = refs[o0 + n_a + n_f:]
        for hook in ("start", "mid", "finish"):
            for ci, cm in enumerate(carried):
                fn = getattr(cm, hook)
                if fn is not None:
                    fn(per[ci], sem[2 * ci], sem[2 * ci + 1])

    outs = pl.pallas_call(
        body, name=name, in_specs=[ANY] * (n_a + n_r), out_specs=[ANY] * (n_a + n_f), out_shape=shapes,
        input_output_aliases={i: i for i in range(n_a)}, scratch_shapes=sems,
    )(*arrays, *reads)
    return _carried_results(carried, akeys, fkeys, outs)


HBM_SPEC = pl.BlockSpec(memory_space=pltpu.HBM)
SEM_SPEC = pl.BlockSpec(memory_space=pltpu.SEMAPHORE)
SPLIT_EFFECT = pltpu.SideEffectType.DATAFLOW_SIDE_EFFECTING


def _in_hbm(a):
    return pltpu.with_memory_space_constraint(a, pltpu.HBM)


def _split_start(items, name, after=None):
    items = list(items)
    akeys, fkeys, arrays, shapes, sems = _carried_layout(items)
    assert not fkeys
    n_a, n_s = len(akeys), len(sems)
    n_in = n_a + (after is not None)

    def body(*refs):
        per = _carried_results(items, akeys, [], refs[n_in:n_in + n_a])
        sem = refs[n_in + n_a:n_in + n_a + n_s]
        for ci, cm in enumerate(items):
            cm.start(per[ci], sem[2 * ci], sem[2 * ci + 1])
        token = refs[n_in + n_a + n_s]
        token[...] = jnp.zeros_like(token)

    outs = pl.pallas_call(
        body, name=name, in_specs=[HBM_SPEC] * n_a + [ANY] * (after is not None),
        out_specs=[HBM_SPEC] * n_a + [SEM_SPEC] * n_s + [pl.BlockSpec(memory_space=pltpu.VMEM)],
        out_shape=[pltpu.HBM(s.shape, s.dtype) for s in shapes] + sems + [jax.ShapeDtypeStruct(TOKEN_SHAPE, F32)],
        input_output_aliases={i: i for i in range(n_a)},
        compiler_params=pltpu.CompilerParams(has_side_effects=SPLIT_EFFECT),
    )(*[_in_hbm(a) for a in arrays], *([after] if after is not None else []))
    res = _carried_results(items, akeys, [], outs[:n_a])
    sem_out = outs[n_a:n_a + n_s]
    return res, [(sem_out[2 * ci], sem_out[2 * ci + 1]) for ci in range(len(items))], outs[-1]


def _split_wait(items, sems, after, name):
    items = list(items)
    after = list(after) if isinstance(after, (list, tuple)) else [after]
    akeys, fkeys, arrays, shapes, _ = _carried_layout(items)
    n_a, n_s = len(akeys), 2 * len(items)

    def body(*refs):
        per = _carried_results(items, akeys, [], refs[n_a + n_s + len(after):])
        sem = refs[n_a:n_a + n_s]
        for ci, cm in enumerate(items):
            cm.finish(per[ci], sem[2 * ci], sem[2 * ci + 1])

    outs = pl.pallas_call(
        body, name=name, in_specs=[HBM_SPEC] * n_a + [SEM_SPEC] * n_s + [ANY] * len(after),
        out_specs=[HBM_SPEC] * n_a, out_shape=[pltpu.HBM(s.shape, s.dtype) for s in shapes],
        input_output_aliases={i: i for i in range(n_a)},
        compiler_params=pltpu.CompilerParams(has_side_effects=SPLIT_EFFECT),
    )(*arrays, *[s for pair in sems for s in pair], *after)
    return _carried_results(items, akeys, [], outs)


def _mm(a, b, *, mode, out_dtype, name, res=None, carried=(), fused=None):
    if mode == "nn":
        (m, k), (k2, n) = a.shape, b.shape
    elif mode == "nt":
        (m, k), (n, k2) = a.shape, b.shape
    else:
        (k, m), (k2, n) = a.shape, b.shape
    assert k == k2, (name, a.shape, b.shape)
    tm = _div(m, MM_TILE_CAP, LANE)
    tn = _div(n, MM_TILE_CAP, MXU_WIDTH) if n % MXU_WIDTH == 0 else 0
    if tn < 1024:
        tn = _div(n, MM_TILE_CAP, LANE)
    out_shape = jax.ShapeDtypeStruct((m, n), out_dtype)

    if fused is not None:
        assert mode in ("nn", "nt") and res is None and k <= 2048
        tm, tn = fused["tile"]
        dot = _dot if mode == "nn" else _dot_nt
        n_x = len(fused["ins"])

        summed = [len(e) > 2 for e in fused["outs"]]

        def kern_fused(*refs):
            part = dot(refs[0][...], refs[1][...])
            outs = fused["post"](part, *[r[...] for r in refs[2:2 + n_x]])
            first_row_tile = pl.program_id(1) == 0
            for o_ref, val, acc in zip(refs[2 + n_x:], outs, summed):
                if not acc:
                    o_ref[...] = val.astype(o_ref.dtype)
                    continue

                @pl.when(first_row_tile)
                def _():
                    o_ref[...] = val

                @pl.when(jnp.logical_not(first_row_tile))
                def _():
                    o_ref[...] += val

        def tile_spec(shape, index=lambda i, j: (i, j)):
            return pl.BlockSpec(shape, lambda j, i: index(i, j))

        return _pcall(
            kern_fused, name=name, grid=(n // tn, m // tm),
            in_specs=[pl.BlockSpec((tm, k), lambda j, i: (i, 0)),
                      pl.BlockSpec((k, tn), lambda j, i: (0, j)) if mode == "nn"
                      else pl.BlockSpec((tn, k), lambda j, i: (j, 0))] + [tile_spec(*e[1:]) for e in fused["ins"]],
            out_specs=[tile_spec(*e[1:]) for e in fused["outs"]], out_shape=[e[0] for e in fused["outs"]],
            args=(a, b, *[e[0] for e in fused["ins"]]), semantics=("parallel", "arbitrary"), carried=carried)

    if mode == "tn":
        assert res is None

        def kern_tn(a_ref, b_ref, o_ref, at_ref):
            @pl.when(pl.program_id(1) == 0)
            def _():
                step = min(TRANSPOSE_CHUNK, k)
                for c0 in range(0, k, step):
                    at_ref[:, c0:c0 + step] = a_ref[c0:c0 + step, :].astype(F32).T.astype(BF16)

            o_ref[...] = jnp.dot(at_ref[...], b_ref[...].astype(BF16),
                                 preferred_element_type=F32).astype(o_ref.dtype)

        return _pcall(
            kern_tn, name=name, grid=(m // tm, n // tn),
            in_specs=[pl.BlockSpec((k, tm), lambda i, j: (0, i)),
                      pl.BlockSpec((k, tn), lambda i, j: (0, j))],
            out_specs=pl.BlockSpec((tm, tn), lambda i, j: (i, j)),
            out_shape=out_shape, args=(a, b),
            scratch_shapes=[pltpu.VMEM((tm, k), BF16)],
            semantics=("parallel", "arbitrary"), carried=carried)

    tk = k if k <= 2048 else _div(k, 3072, LANE)
    nk = k // tk
    a_spec = pl.BlockSpec((tm, tk), lambda i, j, kk: (i, kk))
    if mode == "nn":
        b_spec = pl.BlockSpec((tk, tn), lambda i, j, kk: (kk, j))
        dot = _dot
    else:
        b_spec = pl.BlockSpec((tn, tk), lambda i, j, kk: (j, kk))
        dot = _dot_nt
    o_spec = pl.BlockSpec((tm, tn), lambda i, j, kk: (i, j))
    in_specs = [a_spec, b_spec]
    args = [a, b]
    if res is not None:
        in_specs.append(o_spec)
        args.append(res)
    has_res = res is not None

    def kern(*refs):
        a_ref, b_ref = refs[0], refs[1]
        r_ref = refs[2] if has_res else None
        o_ref = refs[3] if has_res else refs[2]
        part = dot(a_ref[...], b_ref[...])
        if nk == 1:
            if has_res:
                part = part + r_ref[...]
            o_ref[...] = part.astype(o_ref.dtype)
            return
        acc_ref = refs[-1]
        kk = pl.program_id(2)

        @pl.when(kk == 0)
        def _():
            acc_ref[...] = part

        @pl.when(kk > 0)
        def _():
            acc_ref[...] += part

        @pl.when(kk == nk - 1)
        def _():
            tot = acc_ref[...]
            if has_res:
                tot = tot + r_ref[...]
            o_ref[...] = tot.astype(o_ref.dtype)

    return _pcall(
        kern, name=name, grid=(m // tm, n // tn, nk),
        in_specs=in_specs, out_specs=o_spec, out_shape=out_shape, args=args,
        scratch_shapes=[pltpu.VMEM((tm, tn), F32)] if nk > 1 else [],
        semantics=("parallel", "parallel", "arbitrary"), carried=carried)


ROW_BLOCK = 512


def _rms_fwd(x, g, name):
    t, d = x.shape
    tr = min(ROW_BLOCK, t)

    def kern(x_ref, g_ref, o_ref):
        xf = x_ref[...]
        r = lax.rsqrt(jnp.mean(xf * xf, axis=-1, keepdims=True) + RMS_EPS)
        o_ref[...] = (xf * r * g_ref[...]).astype(o_ref.dtype)

    return pl.pallas_call(
        kern, name=name, grid=(t // tr,),
        in_specs=[pl.BlockSpec((tr, d), lambda i: (i, 0)), pl.BlockSpec((1, d), lambda i: (0, 0))],
        out_specs=pl.BlockSpec((tr, d), lambda i: (i, 0)),
        out_shape=jax.ShapeDtypeStruct((t, d), BF16),
        compiler_params=_cparams(("parallel",)),
    )(x, g)


def _rms_bwd(x, g, dh, res, name, bf16_copy=False):
    t, d = x.shape
    tr = min(ROW_BLOCK, t)
    has_res = res is not None

    def kern(*refs):
        x_ref, g_ref, dh_ref = refs[:3]
        r_ref = refs[3] if has_res else None
        dx_ref, dg_ref = refs[3 + has_res], refs[4 + has_res]
        xf = x_ref[...]
        r = lax.rsqrt(jnp.mean(xf * xf, axis=-1, keepdims=True) + RMS_EPS)
        xn = xf * r
        dh_ = dh_ref[...]
        dhg = dh_ * g_ref[...]
        dx = r * (dhg - xn * jnp.mean(dhg * xn, axis=-1, keepdims=True))
        if has_res:
            dx = dx + r_ref[...]
        dx_ref[...] = dx
        if bf16_copy:
            refs[-1][...] = dx.astype(BF16)
        part = jnp.sum(dh_ * xn, axis=0, keepdims=True)

        @pl.when(pl.program_id(0) == 0)
        def _():
            dg_ref[...] = part

        @pl.when(pl.program_id(0) > 0)
        def _():
            dg_ref[...] += part

    row = pl.BlockSpec((tr, d), lambda i: (i, 0))
    vec = pl.BlockSpec((1, d), lambda i: (0, 0))
    in_specs = [row, vec, row] + ([row] if has_res else [])
    args = [x, g, dh] + ([res] if has_res else [])
    return pl.pallas_call(
        kern, name=name, grid=(t // tr,), in_specs=in_specs, out_specs=[row, vec] + [row] * bf16_copy,
        out_shape=[jax.ShapeDtypeStruct((t, d), F32), jax.ShapeDtypeStruct((1, d), F32)]
        + [jax.ShapeDtypeStruct((t, d), BF16)] * bf16_copy,
        compiler_params=_cparams(("arbitrary",)),
    )(*args)


def _loss_head(x3, g, target, name):
    t, d = x3.shape
    tr = ROW_BLOCK

    def kern(x_ref, g_ref, t_ref, dx_ref, dg_ref, loss_ref, dxb_ref):
        xf = x_ref[...]
        r = lax.rsqrt(jnp.mean(xf * xf, axis=-1, keepdims=True) + RMS_EPS)
        xn = xf * r
        gg = g_ref[...]
        err = xn * gg - t_ref[...]
        lpart = 0.5 * jnp.sum(jnp.mean(err * err, axis=-1, keepdims=True), axis=0, keepdims=True)
        dy = err * (1.0 / d)
        dyg = dy * gg
        dx = r * (dyg - xn * jnp.mean(dyg * xn, axis=-1, keepdims=True))
        dx_ref[...] = dx
        dxb_ref[...] = dx.astype(BF16)
        gpart = jnp.sum(dy * xn, axis=0, keepdims=True)
        lrow = jnp.broadcast_to(lpart, (1, LANE))

        @pl.when(pl.program_id(0) == 0)
        def _():
            dg_ref[...] = gpart
            loss_ref[...] = lrow

        @pl.when(pl.program_id(0) > 0)
        def _():
            dg_ref[...] += gpart
            loss_ref[...] += lrow

    row = pl.BlockSpec((tr, d), lambda i: (i, 0))
    vec = pl.BlockSpec((1, d), lambda i: (0, 0))
    return pl.pallas_call(
        kern, name=name, grid=(t // tr,), in_specs=[row, vec, row],
        out_specs=[row, vec, pl.BlockSpec((1, LANE), lambda i: (0, 0)), row],
        out_shape=[jax.ShapeDtypeStruct((t, d), F32), jax.ShapeDtypeStruct((1, d), F32),
                   jax.ShapeDtypeStruct((1, LANE), F32), jax.ShapeDtypeStruct((t, d), BF16)],
        compiler_params=_cparams(("arbitrary",)),
    )(x3, g, target)


ATT_SCALE = HEAD_DIM ** -0.5
Q_BLOCK0, K_BLOCK0, V_BLOCK0 = 0, ATT_WIDTH // HEAD_DIM, 2 * ATT_WIDTH // HEAD_DIM


def _residue_rows(dil, r, n):
    if dil == 1:
        return pl.ds(n * ATT_BLOCK, ATT_BLOCK)
    return pl.ds(n * ATT_BLOCK * dil + r, ATT_BLOCK, stride=dil)


def _band_mask(with_prev):
    width = 2 * ATT_BLOCK if with_prev else ATT_BLOCK
    iq = lax.broadcasted_iota(jnp.int32, (ATT_BLOCK, width), 0)
    ik = lax.broadcasted_iota(jnp.int32, (ATT_BLOCK, width), 1)
    if not with_prev:
        return ik <= iq
    return ((ik < ATT_BLOCK) & (iq <= ik)) | ((ik >= ATT_BLOCK) & ((ik - ATT_BLOCK) <= iq))


def _band_keys(ref, dil, r, n):
    own = ref[_residue_rows(dil, r, n), :]
    if n == 0:
        return own
    return jnp.concatenate([ref[_residue_rows(dil, r, n - 1), :], own], axis=0)


def _attn_col_spec(base, grp):
    return pl.BlockSpec((SEQ, HEAD_DIM), lambda h: (0, base + grp * ATT_HEADS + h))


def _attn_fwd(proj, grp, name, carried=()):
    _, dil = ATT_GROUPS[grp]
    nb = SEQ // dil // ATT_BLOCK

    def kern(q_ref, k_ref, v_ref, o_ref, lse_ref):
        for r in range(dil):
            for n in range(nb):
                rows = _residue_rows(dil, r, n)
                s = _dot_nt(q_ref[rows, :], _band_keys(k_ref, dil, r, n)) * ATT_SCALE
                s = jnp.where(_band_mask(n > 0), s, -jnp.inf)
                m = jnp.max(s, axis=-1, keepdims=True)
                p = jnp.exp(s - m)
                l = jnp.sum(p, axis=-1, keepdims=True)
                o_ref[rows, :] = _dot(p / l, _band_keys(v_ref, dil, r, n))
                lse_ref[rows, :] = jnp.broadcast_to(m + jnp.log(l), (ATT_BLOCK, HEAD_DIM))

    out_spec = pl.BlockSpec((SEQ, HEAD_DIM), lambda h: (0, h))
    return _pcall(
        kern, name=name, grid=(ATT_HEADS,),
        in_specs=[_attn_col_spec(Q_BLOCK0, grp), _attn_col_spec(K_BLOCK0, grp), _attn_col_spec(V_BLOCK0, grp)],
        out_specs=[out_spec, out_spec],
        out_shape=[jax.ShapeDtypeStruct((SEQ, ATT_OUT), F32)] * 2, args=(proj, proj, proj),
        semantics=("parallel",), carried=carried)


def _attn_weights(l0, l1, l2):
    mx = jnp.maximum(jnp.maximum(l0, l1), l2)
    e0, e1, e2 = jnp.exp(l0 - mx), jnp.exp(l1 - mx), jnp.exp(l2 - mx)
    den = e0 + e1 + e2
    return e0 / den, e1 / den, e2 / den


def _attn_merge_fwd(outs, lses, name):
    tr = ROW_BLOCK

    def kern(o0, o1, o2, l0, l1, l2, out_ref):
        a0, a1, a2 = _attn_weights(l0[...], l1[...], l2[...])
        out_ref[...] = (a0 * o0[...] + a1 * o1[...] + a2 * o2[...]).astype(out_ref.dtype)

    spec = pl.BlockSpec((tr, ATT_OUT), lambda i: (i, 0))
    return pl.pallas_call(
        kern, name=name, grid=(SEQ // tr,), in_specs=[spec] * 6, out_specs=spec,
        out_shape=jax.ShapeDtypeStruct((SEQ, ATT_OUT), BF16),
        compiler_params=_cparams(("parallel",)),
    )(*outs, *lses)


def _attn_merge_bwd(outs, lses, do_att, name, carried=()):
    tr = ROW_BLOCK

    def kern(o0, o1, o2, l0, l1, l2, do_ref, d0, d1, d2, t0, t1, t2):
        alphas = _attn_weights(l0[...], l1[...], l2[...])
        do = do_ref[...]
        o_att = alphas[0] * o0[...] + alphas[1] * o1[...] + alphas[2] * o2[...]
        prod = do * o_att
        parts = []
        for h in range(ATT_HEADS):
            sl = slice(h * HEAD_DIM, (h + 1) * HEAD_DIM)
            tot = jnp.sum(prod[:, sl], axis=-1, keepdims=True)
            parts.append(jnp.broadcast_to(tot, (tr, HEAD_DIM)))
        dd = jnp.concatenate(parts, axis=1)
        for a, d_ref, t_ref in zip(alphas, (d0, d1, d2), (t0, t1, t2)):
            d_ref[...] = a * do
            t_ref[...] = -a * dd

    spec = pl.BlockSpec((tr, ATT_OUT), lambda i: (i, 0))
    res, cres = _pcall(
        kern, name=name, grid=(SEQ // tr,), in_specs=[spec] * 7, out_specs=[spec] * 6,
        out_shape=[jax.ShapeDtypeStruct((SEQ, ATT_OUT), F32)] * 6, args=(*outs, *lses, do_att),
        semantics=("parallel",), carried=carried)
    return (res[:3], res[3:]), cres


def _attn_bwd(proj, grp, lse, do_g, dl_g, name, carried=()):
    _, dil = ATT_GROUPS[grp]
    nb = SEQ // dil // ATT_BLOCK

    def kern(q_ref, k_ref, v_ref, do_ref, lse_ref, dl_ref, dq_ref, dk_ref, dv_ref, dq_acc, dk_acc, dv_acc):
        dk_acc[...] = jnp.zeros_like(dk_acc)
        dv_acc[...] = jnp.zeros_like(dv_acc)
        for r in range(dil):
            for n in range(nb):
                rows = _residue_rows(dil, r, n)
                q, do = q_ref[rows, :], do_ref[rows, :]
                kk, vv = _band_keys(k_ref, dil, r, n), _band_keys(v_ref, dil, r, n)
                s = _dot_nt(q, kk) * ATT_SCALE
                p = jnp.where(_band_mask(n > 0), jnp.exp(s - lse_ref[rows, :][:, :1]), 0.0)
                ds = p * (_dot_nt(do, vv) + dl_ref[rows, :][:, :1])
                dq_acc[rows, :] = _dot(ds, kk) * ATT_SCALE
                dk = _dot_tn(ds, q) * ATT_SCALE
                dv = _dot_tn(p, do)
                if n > 0:
                    prev = _residue_rows(dil, r, n - 1)
                    dk_acc[prev, :] += dk[:ATT_BLOCK]
                    dv_acc[prev, :] += dv[:ATT_BLOCK]
                    dk, dv = dk[ATT_BLOCK:], dv[ATT_BLOCK:]
                dk_acc[rows, :] += dk
                dv_acc[rows, :] += dv
        dq_ref[...] = dq_acc[...].astype(dq_ref.dtype)
        dk_ref[...] = dk_acc[...].astype(dk_ref.dtype)
        dv_ref[...] = dv_acc[...].astype(dv_ref.dtype)

    spec = pl.BlockSpec((SEQ, HEAD_DIM), lambda h: (0, h))
    return _pcall(
        kern, name=name, grid=(ATT_HEADS,),
        in_specs=[_attn_col_spec(Q_BLOCK0, grp), _attn_col_spec(K_BLOCK0, grp), _attn_col_spec(V_BLOCK0, grp),
                  spec, spec, spec],
        out_specs=[spec] * 3,
        out_shape=[jax.ShapeDtypeStruct((SEQ, ATT_OUT), BF16)] * 3, args=(proj, proj, proj, do_g, lse, dl_g),
        scratch_shapes=[pltpu.VMEM((SEQ, HEAD_DIM), F32)] * 3,
        semantics=("parallel",), carried=carried)


HG_HEADS_PER_STEP = 8
HG_BLOCK_W = 4 * HEAD_DIM
HG_BLOCKS = HG_HEADS_PER_STEP * HEAD_DIM // HG_BLOCK_W
HG_STEP_W = HG_HEADS_PER_STEP * HEAD_DIM
HG_Q_BLK = (3 * ATT_WIDTH) // HG_BLOCK_W
HG_N_CHUNKS = SEQ // HG_CHUNK
HG_MID = HG_CHUNK // 2


def _lower_bound(lb_ref, sl):
    l0, l1 = lb_ref[0:1, sl], lb_ref[1:2, sl]
    mx = jnp.maximum(l0, l1)
    e0, e1 = jnp.exp(l0 - mx), jnp.exp(l1 - mx)
    return e0 / (e0 + e1)


def _tri(lower):
    i = lax.broadcasted_iota(jnp.int32, (HG_CHUNK, HG_CHUNK), 0)
    j = lax.broadcasted_iota(jnp.int32, (HG_CHUNK, HG_CHUNK), 1)
    return (i >= j) if lower else (i <= j)


def _head_mean(x):
    parts = []
    for hd in range(x.shape[1] // HEAD_DIM):
        m = jnp.mean(x[:, hd * HEAD_DIM:(hd + 1) * HEAD_DIM], axis=-1, keepdims=True)
        parts.append(jnp.broadcast_to(m, (x.shape[0], HEAD_DIM)))
    return jnp.concatenate(parts, axis=1)


def _hg_chunk_terms(qh, fh, lb):
    sig = _sigmoid(fh)
    f = lb + (1.0 - lb) * sig
    k = 1.0 - f
    b = _dot_exact(_tri(True).astype(F32), jnp.log(f))
    bl = b[HG_CHUNK - 1:HG_CHUNK, :]
    br = b[HG_MID:HG_MID + 1, :]
    sq = _sigmoid(qh)
    q = qh * sq
    return dict(sig=sig, f=f, k=k, b=b, bl=bl, br=br, sq=sq, q=q,
                e1=jnp.exp(bl - b), e2=jnp.exp(b), e3=jnp.exp(b - br), e4=jnp.exp(br - b))


def _hg_fwd(proj, lbw, normw, name, carried=()):
    def in_blks(off):
        return [pl.BlockSpec((HG_CHUNK, HG_BLOCK_W), lambda hp, n, b=b: (n, HG_Q_BLK + off + hp * HG_BLOCKS + b))
                for b in range(HG_BLOCKS)]

    def kern(*refs):
        q_refs, f_refs, i_refs, g_refs = (refs[k * HG_BLOCKS:(k + 1) * HG_BLOCKS] for k in range(4))
        lb_ref, nw_ref, oraw_ref, ohg_ref, st_ref, state = refs[4 * HG_BLOCKS:]

        @pl.when(pl.program_id(1) == 0)
        def _():
            state[...] = jnp.zeros_like(state)

        causal = _tri(True)
        wide = lambda rs: jnp.concatenate([r[...] for r in rs], axis=1)
        t = _hg_chunk_terms(wide(q_refs), wide(f_refs), _lower_bound(lb_ref, slice(None)))
        v, gh = wide(i_refs), wide(g_refs)
        kd, qb, qr, kr = t["k"] * t["e1"], t["q"] * t["e2"], t["q"] * t["e3"], t["k"] * t["e4"]
        decay = jnp.exp(t["bl"])
        outs = []
        for hd in range(HG_HEADS_PER_STEP):
            sl = slice(hd * HEAD_DIM, (hd + 1) * HEAD_DIM)
            st = state[hd]
            st_ref[0, hd] = st
            a = jnp.where(causal, _dot_nt(qr[:, sl], kr[:, sl]), 0.0)
            outs.append(_dot_nt(qb[:, sl], st) + _dot(a, v[:, sl]))
            state[hd] = st * decay[:, sl] + _dot_tn(v[:, sl], kd[:, sl])
        o = jnp.concatenate(outs, axis=1)
        oraw_ref[...] = o
        r = lax.rsqrt(_head_mean(o * o) + RMS_EPS)
        nw = jnp.tile(nw_ref[...], (1, HG_HEADS_PER_STEP))
        ohg_ref[...] = (o * r * nw * (gh * _sigmoid(gh))).astype(ohg_ref.dtype)

    out_blk = pl.BlockSpec((HG_CHUNK, HG_STEP_W), lambda hp, n: (n, hp))
    return _pcall(
        kern, name=name, grid=(HG_HEADS // HG_HEADS_PER_STEP, HG_N_CHUNKS),
        in_specs=[*in_blks(0), *in_blks(2), *in_blks(4), *in_blks(6),
                  pl.BlockSpec((2, HG_STEP_W), lambda hp, n: (0, hp)),
                  pl.BlockSpec((1, HEAD_DIM), lambda hp, n: (0, 0))],
        out_specs=[out_blk, out_blk,
                   pl.BlockSpec((1, HG_HEADS_PER_STEP, HEAD_DIM, HEAD_DIM), lambda hp, n: (n, hp, 0, 0))],
        out_shape=[jax.ShapeDtypeStruct((SEQ, HG_WIDTH), F32), jax.ShapeDtypeStruct((SEQ, HG_WIDTH), BF16),
                   jax.ShapeDtypeStruct((HG_N_CHUNKS, HG_HEADS, HEAD_DIM, HEAD_DIM), F32)],
        args=(*[proj] * (4 * HG_BLOCKS), lbw, normw),
        scratch_shapes=[pltpu.VMEM((HG_HEADS_PER_STEP, HEAD_DIM, HEAD_DIM), F32)],
        semantics=("parallel", "arbitrary"), carried=carried)


def _hg_bwd(proj, lbw, normw, oraw, states, do_hg, name, carried=()):
    last = HG_N_CHUNKS - 1

    def in_blks(off):
        return [pl.BlockSpec((HG_CHUNK, HG_BLOCK_W),
                             lambda hp, n, b=b: (last - n, HG_Q_BLK + off + hp * HG_BLOCKS + b))
                for b in range(HG_BLOCKS)]

    blk = pl.BlockSpec((HG_CHUNK, HG_STEP_W), lambda hp, n: (last - n, hp))

    def kern(*refs):
        q_refs, f_refs, i_refs, g_refs = (refs[k * HG_BLOCKS:(k + 1) * HG_BLOCKS] for k in range(4))
        (lb_ref, nw_ref, oraw_ref, st_ref, do_ref, dq_ref, df_ref, di_ref, dg_ref, dlb_ref, dnw_ref,
         dstate) = refs[4 * HG_BLOCKS:]
        first = pl.program_id(1) == 0

        @pl.when(first)
        def _():
            dstate[...] = jnp.zeros_like(dstate)

        causal = _tri(True)
        wide = lambda rs: jnp.concatenate([r[...] for r in rs], axis=1)
        cat = lambda parts: jnp.concatenate(parts, axis=1)
        qh, fh, v, gh = wide(q_refs), wide(f_refs), wide(i_refs), wide(g_refs)
        o, dout = oraw_ref[...], do_ref[...]
        nw = jnp.tile(nw_ref[...], (1, HG_HEADS_PER_STEP))
        sgg = _sigmoid(gh)
        r = lax.rsqrt(_head_mean(o * o) + RMS_EPS)
        xn = o * r
        dg_ref[...] = (dout * xn * nw * (sgg * (1.0 + gh * (1.0 - sgg)))).astype(dg_ref.dtype)
        don = dout * (gh * sgg)
        dnw_wide = jnp.sum(don * xn, axis=0, keepdims=True)
        dnw_tot = dnw_wide[:, :HEAD_DIM]
        for hd in range(1, HG_HEADS_PER_STEP):
            dnw_tot = dnw_tot + dnw_wide[:, hd * HEAD_DIM:(hd + 1) * HEAD_DIM]
        tt = don * nw
        do = r * (tt - xn * _head_mean(tt * xn))
        lb = _lower_bound(lb_ref, slice(None))
        t = _hg_chunk_terms(qh, fh, lb)
        k, q = t["k"], t["q"]
        kd, qb, qr, kr = k * t["e1"], q * t["e2"], q * t["e3"], k * t["e4"]
        decay = jnp.exp(t["bl"])
        dqb, dqr, dkr, dkd, dv, ddecay = [], [], [], [], [], []
        for hd in range(HG_HEADS_PER_STEP):
            sl = slice(hd * HEAD_DIM, (hd + 1) * HEAD_DIM)
            st = st_ref[0, hd]
            dstn = dstate[hd]
            a = jnp.where(causal, _dot_nt(qr[:, sl], kr[:, sl]), 0.0)
            da = jnp.where(causal, _dot_nt(do[:, sl], v[:, sl]), 0.0)
            dqb.append(_dot(do[:, sl], st))
            dv.append(_dot_tn(a, do[:, sl]) + _dot_nt(kd[:, sl], dstn))
            dqr.append(_dot(da, kr[:, sl]))
            dkr.append(_dot_tn(da, qr[:, sl]))
            dkd.append(_dot(v[:, sl], dstn))
            ddecay.append(jnp.sum(dstn * st, axis=0, keepdims=True))
            dstate[hd] = dstn * decay[:, sl] + _dot_tn(do[:, sl], qb[:, sl])
        dqb, dqr, dkr, dkd, dv, ddecay = cat(dqb), cat(dqr), cat(dkr), cat(dkd), cat(dv), cat(ddecay)
        dq = dqb * t["e2"] + dqr * t["e3"]
        dk = dkd * t["e1"] + dkr * t["e4"]
        db = dqb * qb + dqr * qr - dkr * kr - dkd * kd
        dbl = jnp.sum(dkd * kd, axis=0, keepdims=True) + ddecay * decay
        dbr = jnp.sum(dkr * kr - dqr * qr, axis=0, keepdims=True)
        rows = lax.broadcasted_iota(jnp.int32, db.shape, 0)
        dlf = _dot_exact(_tri(False).astype(F32), db) + dbl + jnp.where(rows <= HG_MID, dbr, 0.0)
        df = dlf / t["f"] - dk
        sig, sq = t["sig"], t["sq"]
        df_ref[...] = (df * (1.0 - lb) * sig * (1.0 - sig)).astype(df_ref.dtype)
        dlb_row = jnp.sum(df * (1.0 - sig), axis=0, keepdims=True)
        dq_ref[...] = (dq * (sq * (1.0 + qh * (1.0 - sq)))).astype(dq_ref.dtype)
        di_ref[...] = dv.astype(di_ref.dtype)
        dnw_blk = jnp.broadcast_to(dnw_tot, (8, HEAD_DIM))

        @pl.when(first)
        def _():
            dlb_ref[...] = dlb_row
            dnw_ref[...] = dnw_blk

        @pl.when(jnp.logical_not(first))
        def _():
            dlb_ref[...] += dlb_row
            dnw_ref[...] += dnw_blk

    n_hp = HG_HEADS // HG_HEADS_PER_STEP
    outs, cres = _pcall(
        kern, name=name, grid=(n_hp, HG_N_CHUNKS),
        in_specs=[*in_blks(0), *in_blks(2), *in_blks(4), *in_blks(6),
                  pl.BlockSpec((2, HG_STEP_W), lambda hp, n: (0, hp)),
                  pl.BlockSpec((1, HEAD_DIM), lambda hp, n: (0, 0)),
                  blk,
                  pl.BlockSpec((1, HG_HEADS_PER_STEP, HEAD_DIM, HEAD_DIM), lambda hp, n: (last - n, hp, 0, 0)),
                  blk],
        out_specs=[blk, blk, blk, blk,
                   pl.BlockSpec((1, HG_STEP_W), lambda hp, n: (0, hp)),
                   pl.BlockSpec((8, HEAD_DIM), lambda hp, n: (hp, 0))],
        out_shape=[jax.ShapeDtypeStruct((SEQ, HG_WIDTH), BF16)] * 4
        + [jax.ShapeDtypeStruct((1, HG_WIDTH), F32), jax.ShapeDtypeStruct((8 * n_hp, HEAD_DIM), F32)],
        args=(*[proj] * (4 * HG_BLOCKS), lbw, normw, oraw, states, do_hg),
        scratch_shapes=[pltpu.VMEM((HG_HEADS_PER_STEP, HEAD_DIM, HEAD_DIM), F32)],
        semantics=("parallel", "arbitrary"), carried=carried)
    dqh, dfh, dih, dgh, dlb, dnw = outs
    return (dqh, dfh, dih, dgh, dlb, [dnw[8 * i:8 * i + 1] for i in range(n_hp)]), cres


GATE_BLOCK_W = 512
GATE_A_BLK = (3 * ATT_WIDTH + 4 * HG_WIDTH) // GATE_BLOCK_W
GATE_B_BLK = GATE_A_BLK + D_MODEL // GATE_BLOCK_W


GATE_TILE = (1024, GATE_BLOCK_W)


def _gate_ins(proj, ya, yb=None):
    ins = [(proj, GATE_TILE, lambda i, j: (i, GATE_A_BLK + j)), (proj, GATE_TILE, lambda i, j: (i, GATE_B_BLK + j)),
           (ya, GATE_TILE)]
    return ins + ([(yb, GATE_TILE)] if yb is not None else [])


def _branch_b_gate(o_hg, w_b, proj, ya, name, carried=()):
    def post(yb, ga, gb, ya_):
        return yb, _sigmoid(ga) * ya_ + _sigmoid(gb) * yb

    return _mm(o_hg, w_b, mode="nn", out_dtype=F32, name=name, carried=carried, fused=dict(
        tile=GATE_TILE, ins=_gate_ins(proj, ya), post=post,
        outs=[(jax.ShapeDtypeStruct((SEQ, D_MODEL), F32), GATE_TILE), (jax.ShapeDtypeStruct((SEQ, D_MODEL), BF16), GATE_TILE)]))


def _dmerged_gate_bwd(dx1, w_out, proj, ya, yb, name, carried=()):
    def post(dm, ga, gb, ya_, yb_):
        sa, sb = _sigmoid(ga), _sigmoid(gb)
        return dm * sa, dm * sb, dm * ya_ * sa * (1.0 - sa), dm * yb_ * sb * (1.0 - sb)

    return _mm(dx1, w_out, mode="nt", out_dtype=BF16, name=name, carried=carried, fused=dict(
        tile=GATE_TILE, ins=_gate_ins(proj, ya, yb), post=post,
        outs=[(jax.ShapeDtypeStruct((SEQ, D_MODEL), BF16), GATE_TILE)] * 4))


NORM_TILE = (512, D_MODEL)


def _residual_norm(a, w, x, g, name, carried=()):
    def post(part, x_, g_):
        xn = part + x_
        r = lax.rsqrt(jnp.mean(xn * xn, axis=-1, keepdims=True) + RMS_EPS)
        return xn, xn * r * g_

    return _mm(a, w, mode="nn", out_dtype=F32, name=name, carried=carried, fused=dict(
        tile=NORM_TILE, ins=[(x, NORM_TILE), (g, (1, D_MODEL), lambda i, j: (0, j))], post=post,
        outs=[(jax.ShapeDtypeStruct((SEQ, D_MODEL), F32), NORM_TILE), (jax.ShapeDtypeStruct((SEQ, D_MODEL), BF16), NORM_TILE)]))


def _norm_bwd_residual(dq, w, x, g, res, name, carried=()):
    def post(dh, x_, g_, res_):
        r = lax.rsqrt(jnp.mean(x_ * x_, axis=-1, keepdims=True) + RMS_EPS)
        xn = x_ * r
        dhg = dh * g_
        dx = r * (dhg - xn * jnp.mean(dhg * xn, axis=-1, keepdims=True)) + res_
        return dx, jnp.sum(dh * xn, axis=0, keepdims=True), dx

    vec = ((1, D_MODEL), lambda i, j: (0, j))
    return _mm(dq, w, mode="nt", out_dtype=F32, name=name, carried=carried, fused=dict(
        tile=NORM_TILE, ins=[(x, NORM_TILE), (g, *vec), (res, NORM_TILE)], post=post,
        outs=[(jax.ShapeDtypeStruct((SEQ, D_MODEL), F32), NORM_TILE), (jax.ShapeDtypeStruct((1, D_MODEL), F32), *vec),
              (jax.ShapeDtypeStruct((SEQ, D_MODEL), BF16), NORM_TILE)]))


FF_SHARD = D_FF // N_CHIPS


FF_TILE_ROWS = 512


def _swiglu_tile(ab):
    a, b = ab[:, :FF_SHARD], ab[:, FF_SHARD:]
    return a * _sigmoid(a) * b


def _swiglu_grad_tile(du, ab):
    a, b = ab[:, :FF_SHARD], ab[:, FF_SHARD:]
    sg = _sigmoid(a)
    return jnp.concatenate([du * b * (sg * (1.0 + a * (1.0 - sg))), du * (a * sg)], axis=1)


def _ff_up(hf, w13, name, carried=()):
    wide, narrow = (FF_TILE_ROWS, 2 * FF_SHARD), (FF_TILE_ROWS, FF_SHARD)
    return _mm(hf, w13, mode="nn", out_dtype=F32, name=name, carried=carried, fused=dict(
        tile=wide, ins=[],
        outs=[(jax.ShapeDtypeStruct((SEQ, 2 * D_FF), F32), wide), (jax.ShapeDtypeStruct((SEQ, D_FF), BF16), narrow)],
        post=lambda p: (p, _swiglu_tile(p))))


def _ff_down_bwd(dx3, w2, ab, name, carried=()):
    wide, narrow = (FF_TILE_ROWS, 2 * FF_SHARD), (FF_TILE_ROWS, FF_SHARD)
    out, res = _mm(dx3, w2, mode="nt", out_dtype=BF16, name=name, carried=carried, fused=dict(
        tile=narrow, ins=[(ab, wide)], outs=[(jax.ShapeDtypeStruct((SEQ, 2 * D_FF), BF16), wide)],
        post=lambda du, ab_: (_swiglu_grad_tile(du, ab_),)))
    return out[0], res


CROSS_ROWS = 512


def _cross_fwd(qc, kvc, name, carried=()):
    def kern(q_ref, k_ref, v_ref, o_ref):
        s = _dot_nt(q_ref[...], k_ref[...]) * ATT_SCALE
        m = jnp.max(s, axis=-1, keepdims=True)
        e = jnp.exp(s - m)
        p = e / jnp.sum(e, axis=-1, keepdims=True)
        o_ref[...] = _dot(p, v_ref[...]).astype(o_ref.dtype)

    qblk = pl.BlockSpec((CROSS_ROWS, HEAD_DIM), lambda h, i: (i, h))
    return _pcall(
        kern, name=name, grid=(CROSS_HEADS, SEQ // CROSS_ROWS),
        in_specs=[qblk, pl.BlockSpec((MEM_LEN, HEAD_DIM), lambda h, i: (0, h)),
                  pl.BlockSpec((MEM_LEN, HEAD_DIM), lambda h, i: (0, CROSS_HEADS + h))],
        out_specs=qblk, out_shape=jax.ShapeDtypeStruct((SEQ, CROSS_WIDTH), BF16), args=(qc, kvc, kvc),
        semantics=("parallel", "parallel"), carried=carried)


def _cross_bwd(qc, kvc, doc, name):
    def kern(q_ref, k_ref, v_ref, do_ref, dq_ref, dk_ref, dv_ref):
        q, k, v, do = q_ref[...], k_ref[...], v_ref[...], do_ref[...]
        s = _dot_nt(q, k) * ATT_SCALE
        m = jnp.max(s, axis=-1, keepdims=True)
        e = jnp.exp(s - m)
        p = e / jnp.sum(e, axis=-1, keepdims=True)
        dp = _dot_nt(do, v)
        ds = p * (dp - jnp.sum(dp * p, axis=-1, keepdims=True))
        dq_ref[...] = (_dot(ds, k) * ATT_SCALE).astype(dq_ref.dtype)
        dk = _dot_tn(ds, q) * ATT_SCALE
        dv = _dot_tn(p, do)

        @pl.when(pl.program_id(1) == 0)
        def _():
            dk_ref[...] = dk
            dv_ref[...] = dv

        @pl.when(pl.program_id(1) > 0)
        def _():
            dk_ref[...] += dk
            dv_ref[...] += dv

    qblk = pl.BlockSpec((CROSS_ROWS, HEAD_DIM), lambda h, i: (i, h))
    kblk = pl.BlockSpec((MEM_LEN, HEAD_DIM), lambda h, i: (0, h))
    dq, dk, dv = pl.pallas_call(
        kern, name=name, grid=(CROSS_HEADS, SEQ // CROSS_ROWS),
        in_specs=[qblk, kblk, pl.BlockSpec((MEM_LEN, HEAD_DIM), lambda h, i: (0, CROSS_HEADS + h)), qblk],
        out_specs=[qblk, kblk, kblk],
        out_shape=[jax.ShapeDtypeStruct((SEQ, CROSS_WIDTH), BF16),
                   jax.ShapeDtypeStruct((MEM_LEN, CROSS_WIDTH), F32),
                   jax.ShapeDtypeStruct((MEM_LEN, CROSS_WIDTH), F32)],
        compiler_params=_cparams(("parallel", "arbitrary")),
    )(qc, kvc, kvc, doc)
    return dq, jnp.concatenate([dk, dv], axis=1)


FULL_SPECS = {
    "w_in": ("col", D_MODEL, IN_WIDTH),
    "w_branch_a": ("col", ATT_OUT, D_MODEL),
    "w_branch_b": ("col", HG_WIDTH, D_MODEL),
    "w_out": ("row", D_MODEL, D_MODEL),
    "wq_cross": ("row", D_MODEL, CROSS_WIDTH),
    "wkv_cross": ("row", D_MODEL, 2 * CROSS_WIDTH),
    "wo_cross": ("col", CROSS_WIDTH, D_MODEL),
    "w13": ("col", D_MODEL, 2 * D_FF),
    "w2": ("row", D_FF, D_MODEL),
}
WEIGHT_PLACE = {
    "w_in": ("w_in", 0), "w_branch_a": ("w_branch_a", 0), "w_branch_b": ("w_branch_b", 0),
    "w_out": ("w_out", 0), "wq_cross": ("wq_cross", 0), "wkv_cross": ("wkv_cross", 0),
    "wo_cross": ("wo_cross", 0), "w1": ("w13", 0), "w3": ("w13", FF_SHARD), "w2": ("w2", 0),
}
BIG_WEIGHTS = tuple(WEIGHT_PLACE)
EW_BLOCK_ELEMS = 512 * 1024


def _position():
    return lax.axis_index("x"), lax.axis_index("y"), lax.axis_index("c")


def _other_chips(x, y):
    return [(1 - x, y), (x, 1 - y), (1 - x, 1 - y)]


def _half(ref, kind, h):
    r, c = ref.shape
    if kind == "col":
        return ref.at[pl.ds(h * (r // 2), r // 2), :]
    return ref.at[:, pl.ds(h * (c // 2), c // 2)]


def _shard_of(ref, kind, start, size):
    return ref.at[:, pl.ds(start, size)] if kind == "col" else ref.at[pl.ds(start, size), :]


def _rows_of(ref, r0, nrows):
    return ref if nrows is None else ref.at[pl.ds(r0, nrows), :]


def _half_shape(kind, rows, cols):
    return (rows // 2, cols) if kind == "col" else (rows, cols // 2)


def _slot_shape(spec):
    kind, rows, cols = spec
    hr, hc = _half_shape(kind, rows, cols)
    return (hr, hc // N_CHIPS) if kind == "col" else (hr // N_CHIPS, hc)


def _remote(src, dst, send_sem, recv_sem, device):
    return pltpu.make_async_remote_copy(src_ref=src, dst_ref=dst, send_sem=send_sem, recv_sem=recv_sem,
                                        device_id=device, device_id_type=MESH)


def _gather_ici_comm(fulls, jobs, specs):
    def piece(refs, job, chip, c):
        f, r0, nr = job
        kind, rows, cols = specs[f]
        stride = (cols if kind == "col" else rows) // N_CHIPS
        return _rows_of(_half(_shard_of(refs[f], kind, chip * stride, stride), kind, c), r0, nr)

    def start(refs, ss, rs):
        x, y, c = _position()
        j = 2 * x + y
        for q, job in enumerate(jobs):
            for p, (px, py) in enumerate(_other_chips(x, y)):
                _remote(piece(refs, job, j, c), piece(refs, job, j, c), ss.at[3 * q + p], rs.at[3 * q + p],
                        (px, py, c)).start()

    def finish(refs, ss, rs):
        x, y, c = _position()
        j = 2 * x + y
        for q, job in enumerate(jobs):
            for p, (px, py) in enumerate(_other_chips(x, y)):
                _remote(piece(refs, job, j, c), piece(refs, job, 2 * px + py, c), ss.at[3 * q + p],
                        rs.at[3 * q + p], (px, py, c)).wait_recv()
        for q, job in enumerate(jobs):
            for p, (px, py) in enumerate(_other_chips(x, y)):
                _remote(piece(refs, job, j, c), piece(refs, job, j, c), ss.at[3 * q + p], rs.at[3 * q + p],
                        (px, py, c)).wait_send()

    names = list(dict.fromkeys(job[0] for job in jobs))
    return _Carried({f: fulls[f] for f in names}, {}, 3 * len(jobs), start, finish)


def _gather_ring_comm(fulls, f, r0, nr, phase, specs):
    kind, _, cols = specs[f]
    assert kind == "col" and nr % 32 == 0
    stride = cols // N_CHIPS
    half = nr // 2

    def rows(refs, chip, c, lo, n):
        return _rows_of(_half(_shard_of(refs[f], kind, chip * stride, stride), kind, c), r0 + lo, n)

    def copies(refs, ss, rs):
        x, y, c = _position()
        me, nx, ny, dg = 2 * x + y, 2 * (1 - x) + y, 2 * x + (1 - y), 2 * (1 - x) + (1 - y)
        to_x, to_y = (1 - x, y, c), (x, 1 - y, c)
        if phase == "a":
            mine = rows(refs, me, c, 0, nr)
            return [(_remote(mine, mine, ss.at[0], rs.at[0], to_x), rows(refs, nx, c, 0, nr)),
                    (_remote(mine, mine, ss.at[1], rs.at[1], to_y), rows(refs, ny, c, 0, nr))]
        up, low = rows(refs, ny, c, half, half), rows(refs, nx, c, 0, half)
        return [(_remote(up, up, ss.at[0], rs.at[0], to_x), rows(refs, dg, c, half, half)),
                (_remote(low, low, ss.at[1], rs.at[1], to_y), rows(refs, dg, c, 0, half))]

    def start(refs, ss, rs):
        for cp, _ in copies(refs, ss, rs):
            cp.start()

    def finish(refs, ss, rs):
        x, y, c = _position()
        mine = copies(refs, ss, rs)
        for i, (_, landing) in enumerate(mine):
            _remote(landing, landing, ss.at[i], rs.at[i], (x, y, c)).wait_recv()
        for cp, _ in mine:
            cp.wait_send()

    return _Carried({f: fulls[f]}, {}, 2, start, finish)


def _gather_d2d_comm(fulls, jobs, specs):
    def rect(refs, job, h):
        f, r0, nr = job
        assert nr is None or specs[f][0] == "col"
        return _rows_of(_half(refs[f], specs[f][0], h), r0, nr)

    def start(refs, ss, rs):
        x, y, c = _position()
        for q, job in enumerate(jobs):
            _remote(rect(refs, job, c), rect(refs, job, c), ss.at[q], rs.at[q], (x, y, 1 - c)).start()

    def finish(refs, ss, rs):
        x, y, c = _position()
        for q, job in enumerate(jobs):
            _remote(rect(refs, job, 1 - c), rect(refs, job, 1 - c), ss.at[q], rs.at[q], (x, y, 1 - c)).wait_recv()
        for q, job in enumerate(jobs):
            _remote(rect(refs, job, c), rect(refs, job, c), ss.at[q], rs.at[q], (x, y, 1 - c)).wait_send()

    names = list(dict.fromkeys(job[0] for job in jobs))
    return _Carried({f: fulls[f] for f in names}, {}, len(jobs), start, finish)


def _pairx_comm(grads, names, specs, whole=False):
    def copies(refs, ss, rs):
        x, y, c = _position()
        src = (lambda f: refs[("g", f)]) if whole else (lambda f: _half(refs[("g", f)], specs[f][0], 1 - c))
        return [_remote(src(f), refs[("r", f)], ss.at[i], rs.at[i], (x, y, 1 - c)) for i, f in enumerate(names)]

    def start(refs, ss, rs):
        for cp in copies(refs, ss, rs):
            cp.start()

    def finish(refs, ss, rs):
        for cp in copies(refs, ss, rs):
            cp.wait_recv()
        for cp in copies(refs, ss, rs):
            cp.wait_send()

    fresh = {("r", f): jax.ShapeDtypeStruct(_half_shape(*specs[f]), BF16) for f in names}
    return _Carried({}, fresh, len(names), start, finish, reads={("g", f): grads[f] for f in names})


def _chipx_comm(pair_sums, slots, jobs, specs):
    def copies(refs, ss, rs):
        x, y, c = _position()
        out = []
        for q, (f, r0, nr) in enumerate(jobs):
            kind = specs[f][0]
            width = _slot_shape(specs[f])[1 if kind == "col" else 0]
            for p, (px, py) in enumerate(_other_chips(x, y)):
                src = _rows_of(_shard_of(refs[("p", f)], kind, (2 * px + py) * width, width), r0, nr)
                dst = _rows_of(refs[("s", f)].at[p], r0, nr)
                out.append(_remote(src, dst, ss.at[3 * q + p], rs.at[3 * q + p], (px, py, c)))
        return out

    def start(refs, ss, rs):
        for cp in copies(refs, ss, rs):
            cp.start()

    def finish(refs, ss, rs):
        for cp in copies(refs, ss, rs):
            cp.wait_recv()
        for cp in copies(refs, ss, rs):
            cp.wait_send()

    names = list(dict.fromkeys(job[0] for job in jobs))
    arrays = {("p", f): pair_sums[f] for f in names}
    arrays.update({("s", f): slots[f] for f in names})
    return _Carried(arrays, {}, 3 * len(jobs), start, finish)


def _share_comm(grads, wnames, specs, place):
    def start(refs, ss, rs):
        x, y, c = _position()
        for i, w in enumerate(wnames):
            kind = specs[place[w][0]][0]
            _remote(_half(refs[w], kind, c), _half(refs[w], kind, c), ss.at[i], rs.at[i], (x, y, 1 - c)).start()

    def finish(refs, ss, rs):
        x, y, c = _position()
        for i, w in enumerate(wnames):
            kind = specs[place[w][0]][0]
            _remote(_half(refs[w], kind, 1 - c), _half(refs[w], kind, 1 - c), ss.at[i], rs.at[i],
                    (x, y, 1 - c)).wait_recv()
        for i, w in enumerate(wnames):
            kind = specs[place[w][0]][0]
            _remote(_half(refs[w], kind, c), _half(refs[w], kind, c), ss.at[i], rs.at[i], (x, y, 1 - c)).wait_send()

    return _Carried({w: grads[w] for w in wnames}, {}, len(wnames), start, finish)


def _gather_rows(v, name="gather_small"):
    shape = v.shape

    def body(v_ref, out_ref, send_sem, recv_sem, loc_sem):
        x, y, c = _position()
        me = 4 * x + 2 * y + c
        flips = [(fx, fy, fc) for fx in (0, 1) for fy in (0, 1) for fc in (0, 1)][1:]

        def peer(fl):
            return tuple(1 - a if f else a for a, f in zip((x, y, c), fl))

        loc = pltpu.make_async_copy(v_ref, out_ref.at[me], loc_sem)
        loc.start()
        sends = []
        for i, fl in enumerate(flips):
            cp = _remote(v_ref, out_ref.at[me], send_sem.at[i], recv_sem.at[i], peer(fl))
            cp.start()
            sends.append(cp)
        for i, fl in enumerate(flips):
            px, py, pc = peer(fl)
            _remote(v_ref, out_ref.at[4 * px + 2 * py + pc], send_sem.at[i], recv_sem.at[i], peer(fl)).wait_recv()
        for cp in sends:
            cp.wait_send()
        loc.wait()

    return pl.pallas_call(
        body, name=name, in_specs=[ANY], out_specs=ANY,
        out_shape=jax.ShapeDtypeStruct((N_DEV,) + shape, F32),
        scratch_shapes=[pltpu.SemaphoreType.DMA((N_DEV - 1,)), pltpu.SemaphoreType.DMA((N_DEV - 1,)),
                        pltpu.SemaphoreType.DMA],
    )(v)


def _ew_block(rows, cols, elems=EW_BLOCK_ELEMS):
    tc = cols if cols <= 4096 else _div(cols, 2048, LANE)
    tr = _div(rows, max(16, elems // tc), 16)
    return tr, tc


def _mesh_scalars():
    x, y, c = _position()
    return jnp.stack([c, 2 * x + y]).astype(jnp.int32)


def _grid_spec(grid, in_specs, out_specs):
    return pltpu.PrefetchScalarGridSpec(num_scalar_prefetch=1, grid=grid, in_specs=in_specs, out_specs=out_specs)


def _cast_into_full(parts, fname, pos, specs, place, name, token=None):
    kind, rows, cols = specs[fname]
    ws = [w for w in place if place[w][0] == fname]
    if kind == "col":
        stride = cols // N_CHIPS
        hr = rows // 2
        tr = _div(hr, max(16, EW_BLOCK_ELEMS // stride), 16)
        nrb = hr // tr
        in_specs = [pl.BlockSpec((tr, parts[w].shape[1]), lambda i, pos_ref: (i + pos_ref[0] * nrb, 0)) for w in ws]
        out_spec = pl.BlockSpec((tr, stride), lambda i, pos_ref: (i + pos_ref[0] * nrb, pos_ref[1]))
    else:
        stride = rows // N_CHIPS
        hc = cols // 2
        tr = _div(stride, max(16, EW_BLOCK_ELEMS // hc), 16)
        nrb = stride // tr
        in_specs = [pl.BlockSpec((tr, hc), lambda i, pos_ref: (i, pos_ref[0])) for w in ws]
        out_spec = pl.BlockSpec((tr, hc), lambda i, pos_ref: (i + pos_ref[1] * nrb, pos_ref[0]))

    def kern(pos_ref, *refs):
        o_ref = refs[-1]
        for w, r in zip(ws, refs[:len(ws)]):
            off = place[w][1] if kind == "col" else 0
            o_ref[:, off:off + r.shape[1]] = r[...].astype(o_ref.dtype)

    tokens = [] if token is None else [token]
    in_specs = in_specs + [pl.BlockSpec(TOKEN_SHAPE, lambda i, pos_ref: (0, 0))] * len(tokens)
    return pl.pallas_call(
        kern, name=name, grid_spec=_grid_spec((nrb,), in_specs, out_spec),
        out_shape=jax.ShapeDtypeStruct((rows, cols), BF16),
        compiler_params=_cparams(("parallel",)),
    )(pos, *[parts[w] for w in ws], *tokens)


def _pair_sum(grad, recv, pos, spec, name, whole=False):
    kind, rows, cols = spec
    hr, hc = _half_shape(kind, rows, cols)
    tr, tc = _ew_block(hr, hc, 2 * EW_BLOCK_ELEMS)
    nrb, ncb = hr // tr, hc // tc
    blk = pl.BlockSpec((tr, tc), lambda i, jj, pos_ref: (i, jj))
    if whole:
        mine = blk
    elif kind == "col":
        mine = pl.BlockSpec((tr, tc), lambda i, jj, pos_ref: (i + pos_ref[0] * nrb, jj))
    else:
        mine = pl.BlockSpec((tr, tc), lambda i, jj, pos_ref: (i, jj + pos_ref[0] * ncb))

    def kern(pos_ref, g_ref, r_ref, o_ref, slots_ref):
        o_ref[...] = (g_ref[...].astype(F32) + r_ref[...].astype(F32)).astype(o_ref.dtype)

    return pl.pallas_call(
        kern, name=name, grid_spec=_grid_spec((nrb, ncb), [mine, blk], [blk, ANY]),
        out_shape=[jax.ShapeDtypeStruct((hr, hc), BF16),
                   jax.ShapeDtypeStruct((N_CHIPS - 1,) + _slot_shape(spec), BF16)],
        compiler_params=_cparams(("parallel", "parallel")),
    )(pos, grad, recv)


def _chip_sum(pair_sum, slots, pos, fname, shard_shapes, specs, place, name):
    kind, rows, cols = specs[fname]
    sr, sc = _slot_shape(specs[fname])
    ws = [w for w in place if place[w][0] == fname]
    n_slots = N_CHIPS - 1
    tr = _div(sr, max(16, EW_BLOCK_ELEMS // sc), 16)
    nrb = sr // tr
    slot = pl.BlockSpec((n_slots, tr, sc), lambda i, pos_ref: (0, i, 0))
    if kind == "col":
        own = pl.BlockSpec((tr, sc), lambda i, pos_ref: (i, pos_ref[1]))
        out_specs = [pl.BlockSpec((tr, shard_shapes[w][1]), lambda i, pos_ref: (i + pos_ref[0] * nrb, 0)) for w in ws]
    else:
        own = pl.BlockSpec((tr, sc), lambda i, pos_ref: (i + pos_ref[1] * nrb, 0))
        out_specs = [pl.BlockSpec((tr, sc), lambda i, pos_ref: (i, pos_ref[0])) for w in ws]

    def kern(pos_ref, own_ref, slot_ref, *out_refs):
        tot = own_ref[...].astype(F32)
        for s in range(n_slots):
            tot = tot + slot_ref[s].astype(F32)
        for w, o_ref in zip(ws, out_refs):
            off = place[w][1] if kind == "col" else 0
            o_ref[...] = tot[:, off:off + o_ref.shape[1]]

    outs = pl.pallas_call(
        kern, name=name, grid_spec=_grid_spec((nrb,), [own, slot], out_specs),
        out_shape=[jax.ShapeDtypeStruct(shard_shapes[w], F32) for w in ws],
        compiler_params=_cparams(("parallel",)),
    )(pos, pair_sum, slots)
    return dict(zip(ws, outs))


def _adam_math(w, g, m, v):
    m2 = ADAM_B1 * m + (1.0 - ADAM_B1) * g
    v2 = ADAM_B2 * v + (1.0 - ADAM_B2) * (g * g)
    m_hat = m2 / (1.0 - ADAM_B1 ** ADAM_STEP)
    v_hat = v2 / (1.0 - ADAM_B2 ** ADAM_STEP)
    delta = -ADAM_LR * (m_hat / (jnp.sqrt(v_hat) + ADAM_EPS) + ADAM_WD * w)
    return delta, m2, v2


def _adamw(w, g, m, v, name, carried=()):
    rows, cols = w.shape
    tr, tc = _ew_block(rows, cols)

    def kern(w_ref, g_ref, m_ref, v_ref, d_ref, m2_ref, v2_ref, g_out_ref):
        g_ = g_ref[...]
        d_ref[...], m2_ref[...], v2_ref[...] = _adam_math(w_ref[...], g_, m_ref[...], v_ref[...])
        g_out_ref[...] = g_

    blk = pl.BlockSpec((tr, tc), lambda i, j: (i, j))
    return _pcall(
        kern, name=name, grid=(rows // tr, cols // tc), in_specs=[blk] * 4, out_specs=[blk] * 4,
        out_shape=[jax.ShapeDtypeStruct((rows, cols), F32)] * 4, args=(w, g, m, v),
        semantics=("parallel", "parallel"), carried=carried)


SMALL_ROWS = ("ln_mix_w", "ln_cross_w", "ln_mem_w", "ln_ffn_w", "ln_final_w")
ROW_HG_NORM, ROW_LB0, ROW_LB1 = 5, 6, 7
LOSS_LANE0 = HEAD_DIM


def _pack_small(vals):
    rows = [vals[n].reshape(1, D_MODEL) for n in SMALL_ROWS]
    pad = lambda a: jnp.pad(a, ((0, 0), (0, D_MODEL - a.shape[1])))
    rows.append(pad(vals["hg_norm_w"].reshape(1, HEAD_DIM)))
    rows.append(pad(vals["hg_lower_bounds"].reshape(2, HG_WIDTH)))
    return jnp.concatenate(rows, axis=0)


def _small_update(gathered, w, m, v, name="small_update"):
    def kern(g_ref, w_ref, m_ref, v_ref, grad_ref, d_ref, m2_ref, v2_ref, loss_ref):
        tot = g_ref[0]
        for s in range(1, N_DEV):
            tot = tot + g_ref[s]
        wv = w_ref[...]
        row = lax.broadcasted_iota(jnp.int32, (8, D_MODEL), 0)
        lane = lax.broadcasted_iota(jnp.int32, (8, D_MODEL), 1)
        l0, l1 = wv[ROW_LB0:ROW_LB0 + 1], wv[ROW_LB1:ROW_LB1 + 1]
        mx = jnp.maximum(l0, l1)
        e0, e1 = jnp.exp(l0 - mx), jnp.exp(l1 - mx)
        p0 = e0 / (e0 + e1)
        dlog = tot[ROW_LB0:ROW_LB0 + 1] * p0 * (1.0 - p0)
        tot = jnp.where(row == ROW_HG_NORM, tot + tot[ROW_LB1:ROW_LB1 + 1], tot)
        grad = jnp.where(row == ROW_LB0, dlog, jnp.where(row == ROW_LB1, -dlog, tot))
        grad = jnp.where((row == ROW_HG_NORM) & (lane >= HEAD_DIM), 0.0, grad)
        grad = jnp.where((row >= ROW_LB0) & (lane >= HG_WIDTH), 0.0, grad)
        grad_ref[...] = grad
        d_ref[...], m2_ref[...], v2_ref[...] = _adam_math(wv, grad, m_ref[...], v_ref[...])
        loss_ref[...] = tot[ROW_HG_NORM:ROW_HG_NORM + 1, LOSS_LANE0:LOSS_LANE0 + LANE]

    full = pl.BlockSpec((8, D_MODEL), lambda: (0, 0))
    return pl.pallas_call(
        kern, name=name,
        in_specs=[pl.BlockSpec((N_DEV, 8, D_MODEL), lambda: (0, 0, 0)), full, full, full],
        out_specs=[full, full, full, full, pl.BlockSpec((1, LANE), lambda: (0, 0))],
        out_shape=[jax.ShapeDtypeStruct((8, D_MODEL), F32)] * 4 + [jax.ShapeDtypeStruct((1, LANE), F32)],
        compiler_params=_cparams(),
    )(gathered, w, m, v)


def _unpack_small(p, shapes):
    out = {n: p[i].reshape(shapes[n]) for i, n in enumerate(SMALL_ROWS)}
    out["hg_norm_w"] = p[ROW_HG_NORM, :HEAD_DIM].reshape(shapes["hg_norm_w"])
    out["hg_lower_bounds"] = p[ROW_LB0:ROW_LB1 + 1, :HG_WIDTH].reshape(shapes["hg_lower_bounds"])
    return out


WHOLE = lambda f: (f, 0, None)
MID_MATRICES = ("w_branch_a", "w_branch_b", "w_out", "wq_cross", "wkv_cross", "wo_cross")
MID_WEIGHTS = MID_MATRICES
W_IN_PIECES = [("w_in", r0, 512) for r0 in range(0, D_MODEL // 2, 512)]
W13_PIECES = [("w13", r0, 512) for r0 in range(0, D_MODEL // 2, 512)]
GATHER_GROUPS = [("mid", [WHOLE(f) for f in MID_MATRICES]), ("w13a", W13_PIECES[:1]), ("w13b", W13_PIECES[1:]),
                 ("w2", [WHOLE("w2")])]
OTHER_WEIGHTS = ["w1", "w3", "w2"] + list(MID_WEIGHTS)
BEFORE = {
    "hgrn_fwd": [("wait", "mid")],
    "mm_out": [("wait", "w13a")],
    "mm_o": [("wait", "w13b")],
    "mm_w2": [("wait", "w2"), ("run", ("d2d", [WHOLE("w2")]), "gather_hand_over_w2")],
    "mm_dh": [("wait", "rs_w2"), ("chip_sum", "w2"), ("wait", "rs_w13"), ("chip_sum", "w13"), ("wait", "rs_mid")]
    + [("chip_sum", f) for f in MID_MATRICES],
}
CARRY = {
    "hgrn_fwd": [("d2d", [WHOLE(f) for f in MID_MATRICES])],
    "mm_out": [("d2d", W13_PIECES[:1])],
    "mm_o": [("d2d", W13_PIECES[1:])],
    "mm_du": [("pairx", ["w2"])],
    "mm_dhf": [("pairx", ["w13"])],
    "attn_bwd_g0": [("pairx", list(MID_MATRICES))],
    "mm_dwin_own": [("pairx_whole", ["w_in"])],
    "mm_dh": [("share", OTHER_WEIGHTS)],
}
AFTER = {
    "mm_du": [("pair_sum", "w2"), ("start", "rs_w2", [WHOLE("w2")])],
    "mm_dhf": [("pair_sum", "w13"), ("start", "rs_w13", [WHOLE("w13")])],
    "attn_bwd_g0": [("pair_sum", f) for f in MID_MATRICES] + [("start", "rs_mid", [WHOLE(f) for f in MID_MATRICES])],
    "mm_dwin_own": [("pair_sum", "w_in"), ("start", "rs_w_in", [WHOLE("w_in")])],
}
FINISH = [
    ("adamw", OTHER_WEIGHTS), ("wait", "rs_w_in"), ("small",), ("chip_sum", "w_in"),
    ("run", ("share", ["w_in"]), "rs_sibling_share_w_in"), ("adamw", ["w_in"]),
]


class _Net:
    def __init__(self, full, pos=None, shard_shapes=None, comm=True, specs=FULL_SPECS, place=WEIGHT_PLACE):
        self.full, self.pos, self.shard_shapes, self.comm = dict(full), pos, shard_shapes, comm
        self.specs, self.place = specs, place
        self.gw, self.recv, self.psum, self.slots, self.grads = {}, {}, {}, {}, {}
        self.gw_sibling = {}
        self.pending, self.token, self.last = {}, None, None

    def _make(self, kind, arg):
        if kind == "gather":
            return _gather_ici_comm(self.full, arg, self.specs)
        if kind == "ring":
            return _gather_ring_comm(self.full, *arg, self.specs)
        if kind == "d2d":
            return _gather_d2d_comm(self.full, arg, self.specs)
        if kind == "pairx":
            return _pairx_comm(self.gw, arg, self.specs)
        if kind == "pairx_whole":
            return _pairx_comm(self.gw_sibling, arg, self.specs, whole=True)
        if kind == "chipx":
            return _chipx_comm(self.psum, self.slots, arg, self.specs)
        assert kind == "share"
        return _share_comm(self.grads, arg, self.specs, self.place)

    def _store(self, kind, res):
        if kind in ("gather", "ring", "d2d"):
            self.full.update(res)
        elif kind in ("pairx", "pairx_whole"):
            for (tag, f), a in res.items():
                (self.gw if tag == "g" else self.recv)[f] = a
        elif kind == "chipx":
            for (tag, f), a in res.items():
                (self.psum if tag == "p" else self.slots)[f] = a
        else:
            self.grads.update(res)

    def run_comm(self, item, name):
        kind, arg = item
        self._store(kind, _run_comm([self._make(kind, arg)], name)[0])

    @staticmethod
    def _others(after, items):
        own = [a for cm in items for a in cm.arrays.values()]
        return [a for a in after if a is not None and all(a is not o for o in own)]

    def start(self, groups, kind, name):
        items = [self._make(kind, jobs) for _, jobs in groups]
        after = self._others([self.last], items)
        res, sems, token = _split_start(items, name, after=after[0] if after else None)
        for (group, jobs), r, s in zip(groups, res, sems):
            self._store(kind, r)
            self.pending[group] = (kind, jobs, s)
        self.token = self.last = token

    def wait(self, group, after=()):
        kind, jobs, sems = self.pending.pop(group)
        item = self._make(kind, jobs)
        res = _split_wait([item], [sems], self._others([self.last, *after], [item]), f"wait_{group}")[0]
        self._store(kind, res)

    def step(self, step):
        if step[0] == "wait":
            self.wait(step[1])
        elif step[0] == "start":
            self.start([(step[1], step[2])], "chipx", f"start_{step[1]}")
        elif step[0] == "pair_sum":
            f = step[1]
            self.psum[f], self.slots[f] = _pair_sum(self.gw[f], self.recv[f], self.pos, self.specs[f],
                                                    f"rs_pair_sum_{f}", whole=f in self.gw_sibling)
        elif step[0] == "chip_sum":
            f = step[1]
            self.grads.update(_chip_sum(self.psum[f], self.slots[f], self.pos, f, self.shard_shapes,
                                        self.specs, self.place, f"rs_chip_sum_{f}"))
        else:
            assert step[0] == "run"
            self.run_comm(step[1], step[2])

    def call(self, fn, name, *args, grad_of=None, sibling_half=False, **kw):
        for step in (BEFORE.get(name, []) if self.comm else []):
            self.step(step)
        args = [a() if callable(a) else a for a in args]
        items = CARRY.get(name, []) if self.comm else []
        carried = [self._make(k, a) for k, a in items]
        if self.token is not None:
            carried.append(_Token(self.token))
            self.token = None
        out, res = fn(*args, name=name, carried=carried, **kw)
        if grad_of is not None:
            (self.gw_sibling if sibling_half else self.gw)[grad_of] = out
        self.last = jax.tree.leaves(out)[0]
        for (kind, _), r in zip(items, res):
            self._store(kind, r)
        for step in (AFTER.get(name, []) if self.comm else []):
            self.step(step)
        return out


def _local_step(net, x, h, mem, target, small):
    full, call = net.full, net.call
    proj = call(_mm, "mm_proj", h, full["w_in"], mode="nn", out_dtype=F32)
    att = [call(_attn_fwd, f"attn_fwd_g{g}", proj, g) for g in range(3)]
    outs, lses = [a[0] for a in att], [a[1] for a in att]
    o_att = _attn_merge_fwd(outs, lses, "attn_merge")
    oraw, o_hg, states = call(_hg_fwd, "hgrn_fwd", proj, small["hg_lower_bounds"], small["hg_norm_w"])
    ya = call(_mm, "mm_branch_a", o_att, full["w_branch_a"], mode="nn", out_dtype=F32)
    yb, merged = call(_branch_b_gate, "mm_branch_b", o_hg, full["w_branch_b"], proj, ya)
    x1, hc = call(_residual_norm, "mm_out", merged, full["w_out"], x, small["ln_cross_w"])

    mn = _rms_fwd(mem, small["ln_mem_w"], "rms_mem")
    qc = call(_mm, "mm_q", hc, full["wq_cross"], mode="nn", out_dtype=F32)
    kvc = call(_mm, "mm_kv", mn, full["wkv_cross"], mode="nn", out_dtype=F32)
    oc = call(_cross_fwd, "cross_fwd", qc, kvc)
    x2, hf = call(_residual_norm, "mm_o", oc, full["wo_cross"], x1, small["ln_ffn_w"])

    ab, u = call(_ff_up, "mm_w13", hf, full["w13"])
    x3 = call(_mm, "mm_w2", u, lambda: full["w2"], mode="nn", out_dtype=F32, res=x2)

    dx3, dg_final, loss, dx3_bf16 = _loss_head(x3, small["ln_final_w"], target, "loss_head")

    gs = {"ln_final_w": dg_final}
    call(_mm, "mm_dw2", u, dx3_bf16, mode="tn", out_dtype=BF16, grad_of="w2")
    dab = call(_ff_down_bwd, "mm_du", dx3_bf16, full["w2"], ab)
    call(_mm, "mm_dw13", hf, dab, mode="tn", out_dtype=BF16, grad_of="w13")
    dhf = call(_mm, "mm_dhf", dab, full["w13"], mode="nt", out_dtype=F32)
    dx2, gs["ln_ffn_w"], dx2_bf16 = _rms_bwd(x2, small["ln_ffn_w"], dhf, dx3, "rms_ffn_bwd", bf16_copy=True)
    doc = call(_mm, "mm_doc", dx2_bf16, full["wo_cross"], mode="nt", out_dtype=BF16)
    call(_mm, "mm_dwo", oc, dx2_bf16, mode="tn", out_dtype=BF16, grad_of="wo_cross")
    dqc, dkvc = _cross_bwd(qc, kvc, doc, "cross_bwd")
    call(_mm, "mm_dwq", hc, dqc, mode="tn", out_dtype=BF16, grad_of="wq_cross")
    dx1, gs["ln_cross_w"], dx1_bf16 = call(_norm_bwd_residual, "mm_dhc", dqc, full["wq_cross"], x1,
                                           small["ln_cross_w"], dx2)
    call(_mm, "mm_dwkv", mn, dkvc, mode="tn", out_dtype=BF16, grad_of="wkv_cross")
    dmn = call(_mm, "mm_dmn", dkvc, full["wkv_cross"], mode="nt", out_dtype=F32)
    _, gs["ln_mem_w"] = _rms_bwd(mem, small["ln_mem_w"], dmn, None, "rms_mem_bwd")
    dya, dyb, dga, dgb = call(_dmerged_gate_bwd, "mm_dmerged", dx1_bf16, full["w_out"], proj, ya, yb)
    call(_mm, "mm_dwout", merged, dx1_bf16, mode="tn", out_dtype=BF16, grad_of="w_out")
    call(_mm, "mm_dwa", o_att, dya, mode="tn", out_dtype=BF16, grad_of="w_branch_a")
    do_att = call(_mm, "mm_doatt", dya, full["w_branch_a"], mode="nt", out_dtype=F32)
    call(_mm, "mm_dwb", o_hg, dyb, mode="tn", out_dtype=BF16, grad_of="w_branch_b")
    do_hg = call(_mm, "mm_dohg", dyb, full["w_branch_b"], mode="nt", out_dtype=F32)
    dqh, dfh, dih, dgh, dlb, gs["hg_norm_w"] = call(
        _hg_bwd, "hgrn_bwd", proj, small["hg_lower_bounds"], small["hg_norm_w"], oraw, states, do_hg)
    gs["hg_lb"] = dlb
    do_gs, dl_gs = call(_attn_merge_bwd, "attn_merge_bwd", outs, lses, do_att)
    dqs, dks, dvs = zip(*[call(_attn_bwd, f"attn_bwd_g{g}", proj, g, lses[g], do_gs[g], dl_gs[g]) for g in range(3)])
    dproj = jnp.concatenate([*dqs, *dks, *dvs, dqh, dfh, dih, dgh, dga, dgb], axis=1)
    if net.comm:
        half = D_MODEL // 2
        c = lax.axis_index("c")
        h_sibling = lax.dynamic_slice_in_dim(h, (1 - c) * half, half, axis=1)
        h_own = lax.dynamic_slice_in_dim(h, c * half, half, axis=1)
        call(_mm, "mm_dwin_sibling", h_sibling, dproj, mode="tn", out_dtype=BF16, grad_of="w_in", sibling_half=True)
        call(_mm, "mm_dwin_own", h_own, dproj, mode="tn", out_dtype=BF16, grad_of="w_in")
    else:
        call(_mm, "mm_dwin", h, dproj, mode="tn", out_dtype=BF16, grad_of="w_in")
    dh = call(_mm, "mm_dh", dproj, full["w_in"], mode="nt", out_dtype=F32)
    dx, gs["ln_mix_w"] = _rms_bwd(x, small["ln_mix_w"], dh, dx1, "rms_mix_bwd")
    return loss, dx, gs


WEIGHT_ORDER = ("ln_mix_w", "w_in", "hg_norm_w", "hg_lower_bounds", "w_branch_a", "w_branch_b", "w_out",
                "ln_cross_w", "ln_mem_w", "wq_cross", "wkv_cross", "wo_cross", "ln_ffn_w", "w1", "w3", "w2",
                "ln_final_w")


def kernel(x, mem, ln_mix_w, w_in, hg_norm_w, hg_lower_bounds, w_branch_a, w_branch_b, w_out, ln_cross_w, ln_mem_w, wq_cross, wkv_cross, wo_cross, ln_ffn_w, w1, w3, w2, ln_final_w, loss_target, m_ln_mix_w, m_w_in, m_hg_norm_w, m_hg_lower_bounds, m_w_branch_a, m_w_branch_b, m_w_out, m_ln_cross_w, m_ln_mem_w, m_wq_cross, m_wkv_cross, m_wo_cross, m_ln_ffn_w, m_w1, m_w3, m_w2, m_ln_final_w, v_ln_mix_w, v_w_in, v_hg_norm_w, v_hg_lower_bounds, v_w_branch_a, v_w_branch_b, v_w_out, v_ln_cross_w, v_ln_mem_w, v_wq_cross, v_wkv_cross, v_wo_cross, v_ln_ffn_w, v_w1, v_w3, v_w2, v_ln_final_w):
    args = dict(locals())
    w = {n: args[n] for n in WEIGHT_ORDER}
    m = {n: args["m_" + n] for n in WEIGHT_ORDER}
    v = {n: args["v_" + n] for n in WEIGHT_ORDER}
    shapes = {n: w[n].shape for n in WEIGHT_ORDER}
    mat = lambda a: a.reshape(a.shape[-2:])
    shard_shapes = {n: shapes[n][-2:] for n in BIG_WEIGHTS}

    pos = _mesh_scalars()

    def cast(f, token=None):
        return _cast_into_full({n: mat(w[n]) for n in BIG_WEIGHTS if WEIGHT_PLACE[n][0] == f}, f, pos,
                               FULL_SPECS, WEIGHT_PLACE, f"cast_{f}", token)

    net = _Net({"w_in": cast("w_in")}, pos, shard_shapes)
    net.start([(f"ring_a{i}", (*job, "a")) for i, job in enumerate(W_IN_PIECES)], "ring", "gather_start_w_in")
    rest = {f: cast(f, net.token) for f in FULL_SPECS if f != "w_in"}
    net.full.update(rest)
    small = {n: w[n].reshape(1, -1) for n in SMALL_ROWS}
    small["hg_norm_w"] = w["hg_norm_w"].reshape(1, HEAD_DIM)
    small["hg_lower_bounds"] = w["hg_lower_bounds"]
    x2d = x.reshape(SEQ, D_MODEL)
    h = _rms_fwd(x2d, small["ln_mix_w"], "rms_mix")
    for i, job in enumerate(W_IN_PIECES):
        net.wait(f"ring_a{i}", after=[*rest.values(), h] if i == 0 else ())
        net.start([(f"ring_b{i}", (*job, "b"))], "ring", f"gather_pass_on_w_in{i}")
    net.start(GATHER_GROUPS, "gather", "gather_start_rest")
    for i, job in enumerate(W_IN_PIECES):
        net.wait(f"ring_b{i}")
        net.run_comm(("d2d", [job]), f"gather_hand_over_w_in{i}")
    loss, dx, gs = _local_step(net, x2d, h, mem.reshape(MEM_LEN, D_MODEL), loss_target.reshape(SEQ, D_MODEL), small)

    out_g, out_d, out_m, out_v = {}, {}, {}, {}
    net.last = dx
    for step in FINISH:
        if step[0] == "adamw":
            for n in step[1]:
                out_d[n], out_m[n], out_v[n], out_g[n] = net.call(_adamw, f"adamw_{n}", mat(w[n]), net.grads[n],
                                                                  mat(m[n]), mat(v[n]))
        elif step[0] == "wait":
            net.wait(step[1], after=list(out_d.values()))
        elif step[0] == "small":
            pad = lambda a: jnp.pad(a, ((0, 0), (0, D_MODEL - a.shape[1])))
            part = jnp.concatenate(
                [gs[n] for n in SMALL_ROWS]
                + [pad(jnp.concatenate([gs["hg_norm_w"][0], loss], axis=1)), pad(gs["hg_lb"]),
                   pad(gs["hg_norm_w"][1]) if len(gs["hg_norm_w"]) > 1 else jnp.zeros((1, D_MODEL), F32)], axis=0)
            part, net.psum["w_in"] = lax.optimization_barrier((part, net.psum["w_in"]))
            sg, sd, sm, sv, loss_tot = _small_update(_gather_rows(part), _pack_small(w), _pack_small(m),
                                                     _pack_small(v))
            for dst, packed in ((out_g, sg), (out_d, sd), (out_m, sm), (out_v, sv)):
                dst.update(_unpack_small(packed, shapes))
        else:
            net.step(step)

    result = [loss_tot[0, 0], dx.reshape(x.shape)]
    for group in (out_g, out_d, out_m, out_v):
        result += [group[n].reshape(shapes[n]) for n in WEIGHT_ORDER]
    return tuple(result)
```

```python
import math

import jax
import jax.numpy as jnp
from jax import lax
from jax.experimental import pallas as pl
from jax.experimental.pallas import tpu as pltpu

F32 = jnp.float32
BF16 = jnp.bfloat16
MESH = pl.DeviceIdType.MESH

D_MODEL = 2048
SEQ = 2048
HEAD_DIM = 128
MEM_LEN = 256
ATT_GROUPS = ((128, 1), (512, 4), (2048, 16))
ATT_HEADS = 4
ATT_WIDTH = 3 * ATT_HEADS * HEAD_DIM
ATT_OUT = ATT_HEADS * HEAD_DIM
ATT_BLOCK = 128
HG_HEADS = 8
HG_WIDTH = HG_HEADS * HEAD_DIM
HG_CHUNK = 64
IN_WIDTH = 3 * ATT_WIDTH + 4 * HG_WIDTH + 2 * D_MODEL
CROSS_HEADS = 4
CROSS_WIDTH = CROSS_HEADS * HEAD_DIM
D_FF = 5632
RMS_EPS = 1e-6
ADAM_LR = 0.001
ADAM_B1 = 0.9
ADAM_B2 = 0.999
ADAM_EPS = 1e-08
ADAM_WD = 0.01
ADAM_STEP = 10
N_CHIPS = 4
N_DEV = 8

VMEM_LIMIT_BYTES = 56 * 1024 * 1024
LANE = 128
MXU_WIDTH = 256
MM_TILE_CAP = 1536
TRANSPOSE_CHUNK = 512
ANY = pl.BlockSpec(memory_space=pl.ANY)


def _cparams(sem=None):
    return pltpu.CompilerParams(dimension_semantics=sem, vmem_limit_bytes=VMEM_LIMIT_BYTES)


def _div(n, cap, mult):
    best = None
    for d in range(mult, min(n, cap) + 1, mult):
        if n % d == 0:
            best = d
    assert best is not None, (n, cap, mult)
    return best


def _sigmoid(x):
    return 1.0 / (1.0 + jnp.exp(-x))


def _dot(a, b):
    return jnp.dot(a.astype(BF16), b.astype(BF16), preferred_element_type=F32)


def _dot_nt(a, b):
    return lax.dot_general(a.astype(BF16), b.astype(BF16), (((1,), (1,)), ((), ())),
                           preferred_element_type=F32)


def _dot_tn(a, b):
    return jnp.dot(a.astype(F32).T.astype(BF16), b.astype(BF16), preferred_element_type=F32)


def _dot_exact(a, b):
    return jnp.dot(a, b, precision=lax.Precision.HIGHEST, preferred_element_type=F32)


class _Carried:
    def __init__(self, arrays, fresh, n_sems, start, finish, mid=None, reads=None):
        self.arrays, self.fresh, self.n_sems, self.reads = arrays, fresh, n_sems, reads or {}
        self.start, self.mid, self.finish = start, mid, finish


class _Token:
    def __init__(self, array):
        self.array = array


TOKEN_SHAPE = (8, LANE)


def _carried_layout(carried):
    akeys = list(dict.fromkeys(k for cm in carried for k in cm.arrays))
    fkeys = [(ci, k) for ci, cm in enumerate(carried) for k in cm.fresh]
    arrays = [next(cm.arrays[k] for cm in carried if k in cm.arrays) for k in akeys]
    shapes = [jax.ShapeDtypeStruct(a.shape, a.dtype) for a in arrays] + [carried[ci].fresh[k] for ci, k in fkeys]
    sems = []
    for cm in carried:
        sems += [pltpu.SemaphoreType.DMA((cm.n_sems,)), pltpu.SemaphoreType.DMA((cm.n_sems,))]
    return akeys, fkeys, arrays, shapes, sems


def _carried_reads(carried):
    rkeys = list(dict.fromkeys(k for cm in carried for k in cm.reads))
    return rkeys, [next(cm.reads[k] for cm in carried if k in cm.reads) for k in rkeys]


def _carried_results(carried, akeys, fkeys, outs, rkeys=(), read_refs=()):
    shared = dict(zip(akeys, outs[:len(akeys)]))
    shared.update(zip(rkeys, read_refs))
    res = [{k: shared[k] for k in list(cm.arrays) + [r for r in cm.reads if r in shared]} for cm in carried]
    for (ci, k), o in zip(fkeys, outs[len(akeys):]):
        res[ci][k] = o
    return res


def _pcall(kern, *, name, grid, in_specs, out_specs, out_shape, args, scratch_shapes=(), semantics=None,
           carried=()):
    tokens = [c.array for c in carried if isinstance(c, _Token)]
    carried = [c for c in carried if not isinstance(c, _Token)]
    single = not isinstance(out_shape, (list, tuple))
    out_specs = [out_specs] if single else list(out_specs)
    out_shape = [out_shape] if single else list(out_shape)
    n_real, n_out, n_scr = len(in_specs), len(out_shape), len(scratch_shapes)
    in_specs = list(in_specs) + [pl.BlockSpec(TOKEN_SHAPE, lambda *_: (0, 0))] * len(tokens)
    args = list(args) + tokens
    n_in = len(in_specs)
    if not carried:
        def plain(*refs):
            kern(*refs[:n_real], *refs[n_in:])

        outs = pl.pallas_call(plain if tokens else kern, name=name, grid=grid, in_specs=in_specs,
                              out_specs=out_specs, out_shape=out_shape, scratch_shapes=list(scratch_shapes),
                              compiler_params=_cparams(semantics))(*args)
        return (outs[0] if single else list(outs)), []
    akeys, fkeys, arrays, shapes, sems = _carried_layout(carried)
    rkeys, reads = _carried_reads(carried)
    n_a, n_f, n_r = len(akeys), len(fkeys), len(rkeys)
    total = math.prod(grid)
    mid_step = min(total - 1, (17 * total) // 20)

    def wrapped(*refs):
        ins = refs[:n_real]
        r0 = n_in + n_a
        o0 = r0 + n_r
        outs = refs[o0:o0 + n_out]
        a0 = o0 + n_out
        s0 = a0 + n_a + n_f
        per = _carried_results(carried, akeys, fkeys, refs[a0:s0], rkeys, refs[r0:o0])
        scratch = refs[s0:s0 + n_scr]
        sem = refs[s0 + n_scr:]
        step = 0
        for d, g in enumerate(grid):
            step = step * g + pl.program_id(d)

        @pl.when(step == 0)
        def _():
            for ci, cm in enumerate(carried):
                cm.start(per[ci], sem[2 * ci], sem[2 * ci + 1])

        kern(*ins, *outs, *scratch)

        @pl.when(step == mid_step)
        def _():
            for ci, cm in enumerate(carried):
                if cm.mid is not None:
                    cm.mid(per[ci], sem[2 * ci], sem[2 * ci + 1])

        @pl.when(step == total - 1)
        def _():
            for ci, cm in enumerate(carried):
                cm.finish(per[ci], sem[2 * ci], sem[2 * ci + 1])

    outs = pl.pallas_call(
        wrapped, name=name, grid=grid,
        in_specs=list(in_specs) + [ANY] * (n_a + n_r), out_specs=out_specs + [ANY] * (n_a + n_f),
        out_shape=out_shape + shapes,
        input_output_aliases={n_in + i: n_out + i for i in range(n_a)},
        scratch_shapes=list(scratch_shapes) + sems,
        compiler_params=_cparams(("arbitrary",) * len(grid)),
    )(*args, *arrays, *reads)
    res = _carried_results(carried, akeys, fkeys, outs[n_out:])
    return (outs[0] if single else list(outs[:n_out])), res


def _run_comm(carried, name):
    carried = list(carried)
    akeys, fkeys, arrays, shapes, sems = _carried_layout(carried)
    rkeys, reads = _carried_reads(carried)
    n_a, n_f, n_r = len(akeys), len(fkeys), len(rkeys)

    def body(*refs):
        o0 = n_a + n_r
        per = _carried_results(carried, akeys, fkeys, refs[o0:o0 + n_a + n_f], rkeys, refs[n_a:o0])
        sem = refs[o0 + n_a + n_f:]
        for hook in ("start", "mid", "finish"):
            for ci, cm in enumerate(carried):
                fn = getattr(cm, hook)
                if fn is not None:
                    fn(per[ci], sem[2 * ci], sem[2 * ci + 1])

    outs = pl.pallas_call(
        body, name=name, in_specs=[ANY] * (n_a + n_r), out_specs=[ANY] * (n_a + n_f), out_shape=shapes,
        input_output_aliases={i: i for i in range(n_a)}, scratch_shapes=sems,
    )(*arrays, *reads)
    return _carried_results(carried, akeys, fkeys, outs)


HBM_SPEC = pl.BlockSpec(memory_space=pltpu.HBM)
SEM_SPEC = pl.BlockSpec(memory_space=pltpu.SEMAPHORE)
SPLIT_EFFECT = pltpu.SideEffectType.DATAFLOW_SIDE_EFFECTING


def _in_hbm(a):
    return pltpu.with_memory_space_constraint(a, pltpu.HBM)


def _split_start(items, name, after=None):
    items = list(items)
    akeys, fkeys, arrays, shapes, sems = _carried_layout(items)
    assert not fkeys
    n_a, n_s = len(akeys), len(sems)
    n_in = n_a + (after is not None)

    def body(*refs):
        per = _carried_results(items, akeys, [], refs[n_in:n_in + n_a])
        sem = refs[n_in + n_a:n_in + n_a + n_s]
        for ci, cm in enumerate(items):
            cm.start(per[ci], sem[2 * ci], sem[2 * ci + 1])
        token = refs[n_in + n_a + n_s]
        token[...] = jnp.zeros_like(token)

    outs = pl.pallas_call(
        body, name=name, in_specs=[HBM_SPEC] * n_a + [ANY] * (after is not None),
        out_specs=[HBM_SPEC] * n_a + [SEM_SPEC] * n_s + [pl.BlockSpec(memory_space=pltpu.VMEM)],
        out_shape=[pltpu.HBM(s.shape, s.dtype) for s in shapes] + sems + [jax.ShapeDtypeStruct(TOKEN_SHAPE, F32)],
        input_output_aliases={i: i for i in range(n_a)},
        compiler_params=pltpu.CompilerParams(has_side_effects=SPLIT_EFFECT),
    )(*[_in_hbm(a) for a in arrays], *([after] if after is not None else []))
    res = _carried_results(items, akeys, [], outs[:n_a])
    sem_out = outs[n_a:n_a + n_s]
    return res, [(sem_out[2 * ci], sem_out[2 * ci + 1]) for ci in range(len(items))], outs[-1]


def _split_wait(items, sems, after, name):
    items = list(items)
    after = list(after) if isinstance(after, (list, tuple)) else [after]
    akeys, fkeys, arrays, shapes, _ = _carried_layout(items)
    n_a, n_s = len(akeys), 2 * len(items)

    def body(*refs):
        per = _carried_results(items, akeys, [], refs[n_a + n_s + len(after):])
        sem = refs[n_a:n_a + n_s]
        for ci, cm in enumerate(items):
            cm.finish(per[ci], sem[2 * ci], sem[2 * ci + 1])

    outs = pl.pallas_call(
        body, name=name, in_specs=[HBM_SPEC] * n_a + [SEM_SPEC] * n_s + [ANY] * len(after),
        out_specs=[HBM_SPEC] * n_a, out_shape=[pltpu.HBM(s.shape, s.dtype) for s in shapes],
        input_output_aliases={i: i for i in range(n_a)},
        compiler_params=pltpu.CompilerParams(has_side_effects=SPLIT_EFFECT),
    )(*arrays, *[s for pair in sems for s in pair], *after)
    return _carried_results(items, akeys, [], outs)


def _mm(a, b, *, mode, out_dtype, name, res=None, carried=(), fused=None):
    if mode == "nn":
        (m, k), (k2, n) = a.shape, b.shape
    elif mode == "nt":
        (m, k), (n, k2) = a.shape, b.shape
    else:
        (k, m), (k2, n) = a.shape, b.shape
    assert k == k2, (name, a.shape, b.shape)
    tm = _div(m, MM_TILE_CAP, LANE)
    tn = _div(n, MM_TILE_CAP, MXU_WIDTH) if n % MXU_WIDTH == 0 else 0
    if tn < 1024:
        tn = _div(n, MM_TILE_CAP, LANE)
    out_shape = jax.ShapeDtypeStruct((m, n), out_dtype)

    if fused is not None:
        assert mode in ("nn", "nt") and res is None and k <= 2048
        tm, tn = fused["tile"]
        dot = _dot if mode == "nn" else _dot_nt
        n_x = len(fused["ins"])

        summed = [len(e) > 2 for e in fused["outs"]]

        def kern_fused(*refs):
            part = dot(refs[0][...], refs[1][...])
            outs = fused["post"](part, *[r[...] for r in refs[2:2 + n_x]])
            first_row_tile = pl.program_id(1) == 0
            for o_ref, val, acc in zip(refs[2 + n_x:], outs, summed):
                if not acc:
                    o_ref[...] = val.astype(o_ref.dtype)
                    continue

                @pl.when(first_row_tile)
                def _():
                    o_ref[...] = val

                @pl.when(jnp.logical_not(first_row_tile))
                def _():
                    o_ref[...] += val

        def tile_spec(shape, index=lambda i, j: (i, j)):
            return pl.BlockSpec(shape, lambda j, i: index(i, j))

        return _pcall(
            kern_fused, name=name, grid=(n // tn, m // tm),
            in_specs=[pl.BlockSpec((tm, k), lambda j, i: (i, 0)),
                      pl.BlockSpec((k, tn), lambda j, i: (0, j)) if mode == "nn"
                      else pl.BlockSpec((tn, k), lambda j, i: (j, 0))] + [tile_spec(*e[1:]) for e in fused["ins"]],
            out_specs=[tile_spec(*e[1:]) for e in fused["outs"]], out_shape=[e[0] for e in fused["outs"]],
            args=(a, b, *[e[0] for e in fused["ins"]]), semantics=("parallel", "arbitrary"), carried=carried)

    if mode == "tn":
        assert res is None

        def kern_tn(a_ref, b_ref, o_ref, at_ref):
            @pl.when(pl.program_id(1) == 0)
            def _():
                step = min(TRANSPOSE_CHUNK, k)
                for c0 in range(0, k, step):
                    at_ref[:, c0:c0 + step] = a_ref[c0:c0 + step, :].astype(F32).T.astype(BF16)

            o_ref[...] = jnp.dot(at_ref[...], b_ref[...].astype(BF16),
                                 preferred_element_type=F32).astype(o_ref.dtype)

        return _pcall(
            kern_tn, name=name, grid=(m // tm, n // tn),
            in_specs=[pl.BlockSpec((k, tm), lambda i, j: (0, i)),
                      pl.BlockSpec((k, tn), lambda i, j: (0, j))],
            out_specs=pl.BlockSpec((tm, tn), lambda i, j: (i, j)),
            out_shape=out_shape, args=(a, b),
            scratch_shapes=[pltpu.VMEM((tm, k), BF16)],
            semantics=("parallel", "arbitrary"), carried=carried)

    tk = k if k <= 2048 else _div(k, 3072, LANE)
    nk = k // tk
    a_spec = pl.BlockSpec((tm, tk), lambda i, j, kk: (i, kk))
    if mode == "nn":
        b_spec = pl.BlockSpec((tk, tn), lambda i, j, kk: (kk, j))
        dot = _dot
    else:
        b_spec = pl.BlockSpec((tn, tk), lambda i, j, kk: (j, kk))
        dot = _dot_nt
    o_spec = pl.BlockSpec((tm, tn), lambda i, j, kk: (i, j))
    in_specs = [a_spec, b_spec]
    args = [a, b]
    if res is not None:
        in_specs.append(o_spec)
        args.append(res)
    has_res = res is not None

    def kern(*refs):
        a_ref, b_ref = refs[0], refs[1]
        r_ref = refs[2] if has_res else None
        o_ref = refs[3] if has_res else refs[2]
        part = dot(a_ref[...], b_ref[...])
        if nk == 1:
            if has_res:
                part = part + r_ref[...]
            o_ref[...] = part.astype(o_ref.dtype)
            return
        acc_ref = refs[-1]
        kk = pl.program_id(2)

        @pl.when(kk == 0)
        def _():
            acc_ref[...] = part

        @pl.when(kk > 0)
        def _():
            acc_ref[...] += part

        @pl.when(kk == nk - 1)
        def _():
            tot = acc_ref[...]
            if has_res:
                tot = tot + r_ref[...]
            o_ref[...] = tot.astype(o_ref.dtype)

    return _pcall(
        kern, name=name, grid=(m // tm, n // tn, nk),
        in_specs=in_specs, out_specs=o_spec, out_shape=out_shape, args=args,
        scratch_shapes=[pltpu.VMEM((tm, tn), F32)] if nk > 1 else [],
        semantics=("parallel", "parallel", "arbitrary"), carried=carried)


ROW_BLOCK = 512


def _rms_fwd(x, g, name):
    t, d = x.shape
    tr = min(ROW_BLOCK, t)

    def kern(x_ref, g_ref, o_ref):
        xf = x_ref[...]
        r = lax.rsqrt(jnp.mean(xf * xf, axis=-1, keepdims=True) + RMS_EPS)
        o_ref[...] = (xf * r * g_ref[...]).astype(o_ref.dtype)

    return pl.pallas_call(
        kern, name=name, grid=(t // tr,),
        in_specs=[pl.BlockSpec((tr, d), lambda i: (i, 0)), pl.BlockSpec((1, d), lambda i: (0, 0))],
        out_specs=pl.BlockSpec((tr, d), lambda i: (i, 0)),
        out_shape=jax.ShapeDtypeStruct((t, d), BF16),
        compiler_params=_cparams(("parallel",)),
    )(x, g)


def _rms_bwd(x, g, dh, res, name, bf16_copy=False):
    t, d = x.shape
    tr = min(ROW_BLOCK, t)
    has_res = res is not None

    def kern(*refs):
        x_ref, g_ref, dh_ref = refs[:3]
        r_ref = refs[3] if has_res else None
        dx_ref, dg_ref = refs[3 + has_res], refs[4 + has_res]
        xf = x_ref[...]
        r = lax.rsqrt(jnp.mean(xf * xf, axis=-1, keepdims=True) + RMS_EPS)
        xn = xf * r
        dh_ = dh_ref[...]
        dhg = dh_ * g_ref[...]
        dx = r * (dhg - xn * jnp.mean(dhg * xn, axis=-1, keepdims=True))
        if has_res:
            dx = dx + r_ref[...]
        dx_ref[...] = dx
        if bf16_copy:
            refs[-1][...] = dx.astype(BF16)
        part = jnp.sum(dh_ * xn, axis=0, keepdims=True)

        @pl.when(pl.program_id(0) == 0)
        def _():
            dg_ref[...] = part

        @pl.when(pl.program_id(0) > 0)
        def _():
            dg_ref[...] += part

    row = pl.BlockSpec((tr, d), lambda i: (i, 0))
    vec = pl.BlockSpec((1, d), lambda i: (0, 0))
    in_specs = [row, vec, row] + ([row] if has_res else [])
    args = [x, g, dh] + ([res] if has_res else [])
    return pl.pallas_call(
        kern, name=name, grid=(t // tr,), in_specs=in_specs, out_specs=[row, vec] + [row] * bf16_copy,
        out_shape=[jax.ShapeDtypeStruct((t, d), F32), jax.ShapeDtypeStruct((1, d), F32)]
        + [jax.ShapeDtypeStruct((t, d), BF16)] * bf16_copy,
        compiler_params=_cparams(("arbitrary",)),
    )(*args)


def _loss_head(x3, g, target, name):
    t, d = x3.shape
    tr = ROW_BLOCK

    def kern(x_ref, g_ref, t_ref, dx_ref, dg_ref, loss_ref, dxb_ref):
        xf = x_ref[...]
        r = lax.rsqrt(jnp.mean(xf * xf, axis=-1, keepdims=True) + RMS_EPS)
        xn = xf * r
        gg = g_ref[...]
        err = xn * gg - t_ref[...]
        lpart = 0.5 * jnp.sum(jnp.mean(err * err, axis=-1, keepdims=True), axis=0, keepdims=True)
        dy = err * (1.0 / d)
        dyg = dy * gg
        dx = r * (dyg - xn * jnp.mean(dyg * xn, axis=-1, keepdims=True))
        dx_ref[...] = dx
        dxb_ref[...] = dx.astype(BF16)
        gpart = jnp.sum(dy * xn, axis=0, keepdims=True)
        lrow = jnp.broadcast_to(lpart, (1, LANE))

        @pl.when(pl.program_id(0) == 0)
        def _():
            dg_ref[...] = gpart
            loss_ref[...] = lrow

        @pl.when(pl.program_id(0) > 0)
        def _():
            dg_ref[...] += gpart
            loss_ref[...] += lrow

    row = pl.BlockSpec((tr, d), lambda i: (i, 0))
    vec = pl.BlockSpec((1, d), lambda i: (0, 0))
    return pl.pallas_call(
        kern, name=name, grid=(t // tr,), in_specs=[row, vec, row],
        out_specs=[row, vec, pl.BlockSpec((1, LANE), lambda i: (0, 0)), row],
        out_shape=[jax.ShapeDtypeStruct((t, d), F32), jax.ShapeDtypeStruct((1, d), F32),
                   jax.ShapeDtypeStruct((1, LANE), F32), jax.ShapeDtypeStruct((t, d), BF16)],
        compiler_params=_cparams(("arbitrary",)),
    )(x3, g, target)


ATT_SCALE = HEAD_DIM ** -0.5
Q_BLOCK0, K_BLOCK0, V_BLOCK0 = 0, ATT_WIDTH // HEAD_DIM, 2 * ATT_WIDTH // HEAD_DIM


def _residue_rows(dil, r, n):
    if dil == 1:
        return pl.ds(n * ATT_BLOCK, ATT_BLOCK)
    return pl.ds(n * ATT_BLOCK * dil + r, ATT_BLOCK, stride=dil)


def _band_mask(with_prev):
    width = 2 * ATT_BLOCK if with_prev else ATT_BLOCK
    iq = lax.broadcasted_iota(jnp.int32, (ATT_BLOCK, width), 0)
    ik = lax.broadcasted_iota(jnp.int32, (ATT_BLOCK, width), 1)
    if not with_prev:
        return ik <= iq
    return ((ik < ATT_BLOCK) & (iq <= ik)) | ((ik >= ATT_BLOCK) & ((ik - ATT_BLOCK) <= iq))


def _band_keys(ref, dil, r, n):
    own = ref[_residue_rows(dil, r, n), :]
    if n == 0:
        return own
    return jnp.concatenate([ref[_residue_rows(dil, r, n - 1), :], own], axis=0)


def _attn_col_spec(base, grp):
    return pl.BlockSpec((SEQ, HEAD_DIM), lambda h: (0, base + grp * ATT_HEADS + h))


def _attn_fwd(proj, grp, name, carried=()):
    _, dil = ATT_GROUPS[grp]
    nb = SEQ // dil // ATT_BLOCK

    def kern(q_ref, k_ref, v_ref, o_ref, lse_ref):
        for r in range(dil):
            for n in range(nb):
                rows = _residue_rows(dil, r, n)
                s = _dot_nt(q_ref[rows, :], _band_keys(k_ref, dil, r, n)) * ATT_SCALE
                s = jnp.where(_band_mask(n > 0), s, -jnp.inf)
                m = jnp.max(s, axis=-1, keepdims=True)
                p = jnp.exp(s - m)
                l = jnp.sum(p, axis=-1, keepdims=True)
                o_ref[rows, :] = _dot(p / l, _band_keys(v_ref, dil, r, n))
                lse_ref[rows, :] = jnp.broadcast_to(m + jnp.log(l), (ATT_BLOCK, HEAD_DIM))

    out_spec = pl.BlockSpec((SEQ, HEAD_DIM), lambda h: (0, h))
    return _pcall(
        kern, name=name, grid=(ATT_HEADS,),
        in_specs=[_attn_col_spec(Q_BLOCK0, grp), _attn_col_spec(K_BLOCK0, grp), _attn_col_spec(V_BLOCK0, grp)],
        out_specs=[out_spec, out_spec],
        out_shape=[jax.ShapeDtypeStruct((SEQ, ATT_OUT), F32)] * 2, args=(proj, proj, proj),
        semantics=("parallel",), carried=carried)


def _attn_weights(l0, l1, l2):
    mx = jnp.maximum(jnp.maximum(l0, l1), l2)
    e0, e1, e2 = jnp.exp(l0 - mx), jnp.exp(l1 - mx), jnp.exp(l2 - mx)
    den = e0 + e1 + e2
    return e0 / den, e1 / den, e2 / den


def _attn_merge_fwd(outs, lses, name):
    tr = ROW_BLOCK

    def kern(o0, o1, o2, l0, l1, l2, out_ref):
        a0, a1, a2 = _attn_weights(l0[...], l1[...], l2[...])
        out_ref[...] = (a0 * o0[...] + a1 * o1[...] + a2 * o2[...]).astype(out_ref.dtype)

    spec = pl.BlockSpec((tr, ATT_OUT), lambda i: (i, 0))
    return pl.pallas_call(
        kern, name=name, grid=(SEQ // tr,), in_specs=[spec] * 6, out_specs=spec,
        out_shape=jax.ShapeDtypeStruct((SEQ, ATT_OUT), BF16),
        compiler_params=_cparams(("parallel",)),
    )(*outs, *lses)


MERGE_TILE = (512, ATT_OUT)


def _attn_merge_bwd(dya, w_a, outs, lses, name, carried=()):
    def post(do, o0, o1, o2, l0, l1, l2):
        alphas = _attn_weights(l0, l1, l2)
        o_att = alphas[0] * o0 + alphas[1] * o1 + alphas[2] * o2
        prod = do * o_att
        parts = []
        for h in range(ATT_HEADS):
            tot = jnp.sum(prod[:, h * HEAD_DIM:(h + 1) * HEAD_DIM], axis=-1, keepdims=True)
            parts.append(jnp.broadcast_to(tot, (do.shape[0], HEAD_DIM)))
        dd = jnp.concatenate(parts, axis=1)
        return tuple(a * do for a in alphas) + tuple(-a * dd for a in alphas)

    res, cres = _mm(dya, w_a, mode="nt", out_dtype=F32, name=name, carried=carried, fused=dict(
        tile=MERGE_TILE, ins=[(t, MERGE_TILE) for t in (*outs, *lses)], post=post,
        outs=[(jax.ShapeDtypeStruct((SEQ, ATT_OUT), F32), MERGE_TILE)] * 6))
    return (res[:3], res[3:]), cres


def _attn_bwd(proj, grp, lse, do_g, dl_g, name, carried=()):
    _, dil = ATT_GROUPS[grp]
    nb = SEQ // dil // ATT_BLOCK

    def kern(q_ref, k_ref, v_ref, do_ref, lse_ref, dl_ref, dq_ref, dk_ref, dv_ref, dq_acc, dk_acc, dv_acc):
        dk_acc[...] = jnp.zeros_like(dk_acc)
        dv_acc[...] = jnp.zeros_like(dv_acc)
        for r in range(dil):
            for n in range(nb):
                rows = _residue_rows(dil, r, n)
                q, do = q_ref[rows, :], do_ref[rows, :]
                kk, vv = _band_keys(k_ref, dil, r, n), _band_keys(v_ref, dil, r, n)
                s = _dot_nt(q, kk) * ATT_SCALE
                p = jnp.where(_band_mask(n > 0), jnp.exp(s - lse_ref[rows, :][:, :1]), 0.0)
                ds = p * (_dot_nt(do, vv) + dl_ref[rows, :][:, :1])
                dq_acc[rows, :] = _dot(ds, kk) * ATT_SCALE
                dk = _dot_tn(ds, q) * ATT_SCALE
                dv = _dot_tn(p, do)
                if n > 0:
                    prev = _residue_rows(dil, r, n - 1)
                    dk_acc[prev, :] += dk[:ATT_BLOCK]
                    dv_acc[prev, :] += dv[:ATT_BLOCK]
                    dk, dv = dk[ATT_BLOCK:], dv[ATT_BLOCK:]
                dk_acc[rows, :] += dk
                dv_acc[rows, :] += dv
        dq_ref[...] = dq_acc[...].astype(dq_ref.dtype)
        dk_ref[...] = dk_acc[...].astype(dk_ref.dtype)
        dv_ref[...] = dv_acc[...].astype(dv_ref.dtype)

    spec = pl.BlockSpec((SEQ, HEAD_DIM), lambda h: (0, h))
    return _pcall(
        kern, name=name, grid=(ATT_HEADS,),
        in_specs=[_attn_col_spec(Q_BLOCK0, grp), _attn_col_spec(K_BLOCK0, grp), _attn_col_spec(V_BLOCK0, grp),
                  spec, spec, spec],
        out_specs=[spec] * 3,
        out_shape=[jax.ShapeDtypeStruct((SEQ, ATT_OUT), BF16)] * 3, args=(proj, proj, proj, do_g, lse, dl_g),
        scratch_shapes=[pltpu.VMEM((SEQ, HEAD_DIM), F32)] * 3,
        semantics=("parallel",), carried=carried)


HG_HEADS_PER_STEP = 8
HG_BLOCK_W = 4 * HEAD_DIM
HG_BLOCKS = HG_HEADS_PER_STEP * HEAD_DIM // HG_BLOCK_W
HG_STEP_W = HG_HEADS_PER_STEP * HEAD_DIM
HG_Q_BLK = (3 * ATT_WIDTH) // HG_BLOCK_W
HG_N_CHUNKS = SEQ // HG_CHUNK
HG_MID = HG_CHUNK // 2


def _lower_bound(lb_ref, sl):
    l0, l1 = lb_ref[0:1, sl], lb_ref[1:2, sl]
    mx = jnp.maximum(l0, l1)
    e0, e1 = jnp.exp(l0 - mx), jnp.exp(l1 - mx)
    return e0 / (e0 + e1)


def _tri(lower):
    i = lax.broadcasted_iota(jnp.int32, (HG_CHUNK, HG_CHUNK), 0)
    j = lax.broadcasted_iota(jnp.int32, (HG_CHUNK, HG_CHUNK), 1)
    return (i >= j) if lower else (i <= j)


def _head_mean(x):
    parts = []
    for hd in range(x.shape[1] // HEAD_DIM):
        m = jnp.mean(x[:, hd * HEAD_DIM:(hd + 1) * HEAD_DIM], axis=-1, keepdims=True)
        parts.append(jnp.broadcast_to(m, (x.shape[0], HEAD_DIM)))
    return jnp.concatenate(parts, axis=1)


def _hg_chunk_terms(qh, fh, lb):
    sig = _sigmoid(fh)
    f = lb + (1.0 - lb) * sig
    k = 1.0 - f
    b = _dot_exact(_tri(True).astype(F32), jnp.log(f))
    bl = b[HG_CHUNK - 1:HG_CHUNK, :]
    br = b[HG_MID:HG_MID + 1, :]
    sq = _sigmoid(qh)
    q = qh * sq
    return dict(sig=sig, f=f, k=k, b=b, bl=bl, br=br, sq=sq, q=q,
                e1=jnp.exp(bl - b), e2=jnp.exp(b), e3=jnp.exp(b - br), e4=jnp.exp(br - b))


def _hg_fwd(proj, lbw, normw, name, carried=()):
    def in_blks(off):
        return [pl.BlockSpec((HG_CHUNK, HG_BLOCK_W), lambda hp, n, b=b: (n, HG_Q_BLK + off + hp * HG_BLOCKS + b))
                for b in range(HG_BLOCKS)]

    def kern(*refs):
        q_refs, f_refs, i_refs, g_refs = (refs[k * HG_BLOCKS:(k + 1) * HG_BLOCKS] for k in range(4))
        lb_ref, nw_ref, oraw_ref, ohg_ref, st_ref, state = refs[4 * HG_BLOCKS:]

        @pl.when(pl.program_id(1) == 0)
        def _():
            state[...] = jnp.zeros_like(state)

        causal = _tri(True)
        wide = lambda rs: jnp.concatenate([r[...] for r in rs], axis=1)
        t = _hg_chunk_terms(wide(q_refs), wide(f_refs), _lower_bound(lb_ref, slice(None)))
        v, gh = wide(i_refs), wide(g_refs)
        kd, qb, qr, kr = t["k"] * t["e1"], t["q"] * t["e2"], t["q"] * t["e3"], t["k"] * t["e4"]
        decay = jnp.exp(t["bl"])
        outs = []
        for hd in range(HG_HEADS_PER_STEP):
            sl = slice(hd * HEAD_DIM, (hd + 1) * HEAD_DIM)
            st = state[hd]
            st_ref[0, hd] = st
            a = jnp.where(causal, _dot_nt(qr[:, sl], kr[:, sl]), 0.0)
            outs.append(_dot_nt(qb[:, sl], st) + _dot(a, v[:, sl]))
            state[hd] = st * decay[:, sl] + _dot_tn(v[:, sl], kd[:, sl])
        o = jnp.concatenate(outs, axis=1)
        oraw_ref[...] = o
        r = lax.rsqrt(_head_mean(o * o) + RMS_EPS)
        nw = jnp.tile(nw_ref[...], (1, HG_HEADS_PER_STEP))
        ohg_ref[...] = (o * r * nw * (gh * _sigmoid(gh))).astype(ohg_ref.dtype)

    out_blk = pl.BlockSpec((HG_CHUNK, HG_STEP_W), lambda hp, n: (n, hp))
    return _pcall(
        kern, name=name, grid=(HG_HEADS // HG_HEADS_PER_STEP, HG_N_CHUNKS),
        in_specs=[*in_blks(0), *in_blks(2), *in_blks(4), *in_blks(6),
                  pl.BlockSpec((2, HG_STEP_W), lambda hp, n: (0, hp)),
                  pl.BlockSpec((1, HEAD_DIM), lambda hp, n: (0, 0))],
        out_specs=[out_blk, out_blk,
                   pl.BlockSpec((1, HG_HEADS_PER_STEP, HEAD_DIM, HEAD_DIM), lambda hp, n: (n, hp, 0, 0))],
        out_shape=[jax.ShapeDtypeStruct((SEQ, HG_WIDTH), F32), jax.ShapeDtypeStruct((SEQ, HG_WIDTH), BF16),
                   jax.ShapeDtypeStruct((HG_N_CHUNKS, HG_HEADS, HEAD_DIM, HEAD_DIM), F32)],
        args=(*[proj] * (4 * HG_BLOCKS), lbw, normw),
        scratch_shapes=[pltpu.VMEM((HG_HEADS_PER_STEP, HEAD_DIM, HEAD_DIM), F32)],
        semantics=("parallel", "arbitrary"), carried=carried)


def _hg_bwd(proj, lbw, normw, oraw, states, do_hg, name, carried=()):
    last = HG_N_CHUNKS - 1

    def in_blks(off):
        return [pl.BlockSpec((HG_CHUNK, HG_BLOCK_W),
                             lambda hp, n, b=b: (last - n, HG_Q_BLK + off + hp * HG_BLOCKS + b))
                for b in range(HG_BLOCKS)]

    blk = pl.BlockSpec((HG_CHUNK, HG_STEP_W), lambda hp, n: (last - n, hp))

    def kern(*refs):
        q_refs, f_refs, i_refs, g_refs = (refs[k * HG_BLOCKS:(k + 1) * HG_BLOCKS] for k in range(4))
        (lb_ref, nw_ref, oraw_ref, st_ref, do_ref, dq_ref, df_ref, di_ref, dg_ref, dlb_ref, dnw_ref,
         dstate) = refs[4 * HG_BLOCKS:]
        first = pl.program_id(1) == 0

        @pl.when(first)
        def _():
            dstate[...] = jnp.zeros_like(dstate)

        causal = _tri(True)
        wide = lambda rs: jnp.concatenate([r[...] for r in rs], axis=1)
        cat = lambda parts: jnp.concatenate(parts, axis=1)
        qh, fh, v, gh = wide(q_refs), wide(f_refs), wide(i_refs), wide(g_refs)
        o, dout = oraw_ref[...], do_ref[...]
        nw = jnp.tile(nw_ref[...], (1, HG_HEADS_PER_STEP))
        sgg = _sigmoid(gh)
        r = lax.rsqrt(_head_mean(o * o) + RMS_EPS)
        xn = o * r
        dg_ref[...] = (dout * xn * nw * (sgg * (1.0 + gh * (1.0 - sgg)))).astype(dg_ref.dtype)
        don = dout * (gh * sgg)
        dnw_wide = jnp.sum(don * xn, axis=0, keepdims=True)
        dnw_tot = dnw_wide[:, :HEAD_DIM]
        for hd in range(1, HG_HEADS_PER_STEP):
            dnw_tot = dnw_tot + dnw_wide[:, hd * HEAD_DIM:(hd + 1) * HEAD_DIM]
        tt = don * nw
        do = r * (tt - xn * _head_mean(tt * xn))
        lb = _lower_bound(lb_ref, slice(None))
        t = _hg_chunk_terms(qh, fh, lb)
        k, q = t["k"], t["q"]
        kd, qb, qr, kr = k * t["e1"], q * t["e2"], q * t["e3"], k * t["e4"]
        decay = jnp.exp(t["bl"])
        dqb, dqr, dkr, dkd, dv, ddecay = [], [], [], [], [], []
        for hd in range(HG_HEADS_PER_STEP):
            sl = slice(hd * HEAD_DIM, (hd + 1) * HEAD_DIM)
            st = st_ref[0, hd]
            dstn = dstate[hd]
            a = jnp.where(causal, _dot_nt(qr[:, sl], kr[:, sl]), 0.0)
            da = jnp.where(causal, _dot_nt(do[:, sl], v[:, sl]), 0.0)
            dqb.append(_dot(do[:, sl], st))
            dv.append(_dot_tn(a, do[:, sl]) + _dot_nt(kd[:, sl], dstn))
            dqr.append(_dot(da, kr[:, sl]))
            dkr.append(_dot_tn(da, qr[:, sl]))
            dkd.append(_dot(v[:, sl], dstn))
            ddecay.append(jnp.sum(dstn * st, axis=0, keepdims=True))
            dstate[hd] = dstn * decay[:, sl] + _dot_tn(do[:, sl], qb[:, sl])
        dqb, dqr, dkr, dkd, dv, ddecay = cat(dqb), cat(dqr), cat(dkr), cat(dkd), cat(dv), cat(ddecay)
        dq = dqb * t["e2"] + dqr * t["e3"]
        dk = dkd * t["e1"] + dkr * t["e4"]
        db = dqb * qb + dqr * qr - dkr * kr - dkd * kd
        dbl = jnp.sum(dkd * kd, axis=0, keepdims=True) + ddecay * decay
        dbr = jnp.sum(dkr * kr - dqr * qr, axis=0, keepdims=True)
        rows = lax.broadcasted_iota(jnp.int32, db.shape, 0)
        dlf = _dot_exact(_tri(False).astype(F32), db) + dbl + jnp.where(rows <= HG_MID, dbr, 0.0)
        df = dlf / t["f"] - dk
        sig, sq = t["sig"], t["sq"]
        df_ref[...] = (df * (1.0 - lb) * sig * (1.0 - sig)).astype(df_ref.dtype)
        dlb_row = jnp.sum(df * (1.0 - sig), axis=0, keepdims=True)
        dq_ref[...] = (dq * (sq * (1.0 + qh * (1.0 - sq)))).astype(dq_ref.dtype)
        di_ref[...] = dv.astype(di_ref.dtype)
        dnw_blk = jnp.broadcast_to(dnw_tot, (8, HEAD_DIM))

        @pl.when(first)
        def _():
            dlb_ref[...] = dlb_row
            dnw_ref[...] = dnw_blk

        @pl.when(jnp.logical_not(first))
        def _():
            dlb_ref[...] += dlb_row
            dnw_ref[...] += dnw_blk

    n_hp = HG_HEADS // HG_HEADS_PER_STEP
    outs, cres = _pcall(
        kern, name=name, grid=(n_hp, HG_N_CHUNKS),
        in_specs=[*in_blks(0), *in_blks(2), *in_blks(4), *in_blks(6),
                  pl.BlockSpec((2, HG_STEP_W), lambda hp, n: (0, hp)),
                  pl.BlockSpec((1, HEAD_DIM), lambda hp, n: (0, 0)),
                  blk,
                  pl.BlockSpec((1, HG_HEADS_PER_STEP, HEAD_DIM, HEAD_DIM), lambda hp, n: (last - n, hp, 0, 0)),
                  blk],
        out_specs=[blk, blk, blk, blk,
                   pl.BlockSpec((1, HG_STEP_W), lambda hp, n: (0, hp)),
                   pl.BlockSpec((8, HEAD_DIM), lambda hp, n: (hp, 0))],
        out_shape=[jax.ShapeDtypeStruct((SEQ, HG_WIDTH), BF16)] * 4
        + [jax.ShapeDtypeStruct((1, HG_WIDTH), F32), jax.ShapeDtypeStruct((8 * n_hp, HEAD_DIM), F32)],
        args=(*[proj] * (4 * HG_BLOCKS), lbw, normw, oraw, states, do_hg),
        scratch_shapes=[pltpu.VMEM((HG_HEADS_PER_STEP, HEAD_DIM, HEAD_DIM), F32)],
        semantics=("parallel", "arbitrary"), carried=carried)
    dqh, dfh, dih, dgh, dlb, dnw = outs
    return (dqh, dfh, dih, dgh, dlb, [dnw[8 * i:8 * i + 1] for i in range(n_hp)]), cres


GATE_BLOCK_W = 512
GATE_A_BLK = (3 * ATT_WIDTH + 4 * HG_WIDTH) // GATE_BLOCK_W
GATE_B_BLK = GATE_A_BLK + D_MODEL // GATE_BLOCK_W


GATE_TILE = (1024, GATE_BLOCK_W)


def _gate_ins(proj, ya, yb=None):
    ins = [(proj, GATE_TILE, lambda i, j: (i, GATE_A_BLK + j)), (proj, GATE_TILE, lambda i, j: (i, GATE_B_BLK + j)),
           (ya, GATE_TILE)]
    return ins + ([(yb, GATE_TILE)] if yb is not None else [])


def _branch_b_gate(o_hg, w_b, proj, ya, name, carried=()):
    def post(yb, ga, gb, ya_):
        return yb, _sigmoid(ga) * ya_ + _sigmoid(gb) * yb

    return _mm(o_hg, w_b, mode="nn", out_dtype=F32, name=name, carried=carried, fused=dict(
        tile=GATE_TILE, ins=_gate_ins(proj, ya), post=post,
        outs=[(jax.ShapeDtypeStruct((SEQ, D_MODEL), F32), GATE_TILE), (jax.ShapeDtypeStruct((SEQ, D_MODEL), BF16), GATE_TILE)]))


def _dmerged_gate_bwd(dx1, w_out, proj, ya, yb, name, carried=()):
    def post(dm, ga, gb, ya_, yb_):
        sa, sb = _sigmoid(ga), _sigmoid(gb)
        return dm * sa, dm * sb, dm * ya_ * sa * (1.0 - sa), dm * yb_ * sb * (1.0 - sb)

    return _mm(dx1, w_out, mode="nt", out_dtype=BF16, name=name, carried=carried, fused=dict(
        tile=GATE_TILE, ins=_gate_ins(proj, ya, yb), post=post,
        outs=[(jax.ShapeDtypeStruct((SEQ, D_MODEL), BF16), GATE_TILE)] * 4))


NORM_TILE = (512, D_MODEL)


def _residual_norm(a, w, x, g, name, carried=()):
    def post(part, x_, g_):
        xn = part + x_
        r = lax.rsqrt(jnp.mean(xn * xn, axis=-1, keepdims=True) + RMS_EPS)
        return xn, xn * r * g_

    return _mm(a, w, mode="nn", out_dtype=F32, name=name, carried=carried, fused=dict(
        tile=NORM_TILE, ins=[(x, NORM_TILE), (g, (1, D_MODEL), lambda i, j: (0, j))], post=post,
        outs=[(jax.ShapeDtypeStruct((SEQ, D_MODEL), F32), NORM_TILE), (jax.ShapeDtypeStruct((SEQ, D_MODEL), BF16), NORM_TILE)]))


def _norm_bwd_residual(dq, w, x, g, res, name, carried=()):
    def post(dh, x_, g_, res_):
        r = lax.rsqrt(jnp.mean(x_ * x_, axis=-1, keepdims=True) + RMS_EPS)
        xn = x_ * r
        dhg = dh * g_
        dx = r * (dhg - xn * jnp.mean(dhg * xn, axis=-1, keepdims=True)) + res_
        return dx, jnp.sum(dh * xn, axis=0, keepdims=True), dx

    vec = ((1, D_MODEL), lambda i, j: (0, j))
    return _mm(dq, w, mode="nt", out_dtype=F32, name=name, carried=carried, fused=dict(
        tile=NORM_TILE, ins=[(x, NORM_TILE), (g, *vec), (res, NORM_TILE)], post=post,
        outs=[(jax.ShapeDtypeStruct((SEQ, D_MODEL), F32), NORM_TILE), (jax.ShapeDtypeStruct((1, D_MODEL), F32), *vec),
              (jax.ShapeDtypeStruct((SEQ, D_MODEL), BF16), NORM_TILE)]))


FF_SHARD = D_FF // N_CHIPS


FF_TILE_ROWS = 512


def _swiglu_tile(ab):
    a, b = ab[:, :FF_SHARD], ab[:, FF_SHARD:]
    return a * _sigmoid(a) * b


def _swiglu_grad_tile(du, ab):
    a, b = ab[:, :FF_SHARD], ab[:, FF_SHARD:]
    sg = _sigmoid(a)
    return jnp.concatenate([du * b * (sg * (1.0 + a * (1.0 - sg))), du * (a * sg)], axis=1)


def _ff_up(hf, w13, name, carried=()):
    wide, narrow = (FF_TILE_ROWS, 2 * FF_SHARD), (FF_TILE_ROWS, FF_SHARD)
    return _mm(hf, w13, mode="nn", out_dtype=F32, name=name, carried=carried, fused=dict(
        tile=wide, ins=[],
        outs=[(jax.ShapeDtypeStruct((SEQ, 2 * D_FF), F32), wide), (jax.ShapeDtypeStruct((SEQ, D_FF), BF16), narrow)],
        post=lambda p: (p, _swiglu_tile(p))))


def _ff_down_bwd(dx3, w2, ab, name, carried=()):
    wide, narrow = (FF_TILE_ROWS, 2 * FF_SHARD), (FF_TILE_ROWS, FF_SHARD)
    out, res = _mm(dx3, w2, mode="nt", out_dtype=BF16, name=name, carried=carried, fused=dict(
        tile=narrow, ins=[(ab, wide)], outs=[(jax.ShapeDtypeStruct((SEQ, 2 * D_FF), BF16), wide)],
        post=lambda du, ab_: (_swiglu_grad_tile(du, ab_),)))
    return out[0], res


CROSS_ROWS = 512


def _cross_fwd(qc, kvc, name, carried=()):
    def kern(q_ref, k_ref, v_ref, o_ref):
        s = _dot_nt(q_ref[...], k_ref[...]) * ATT_SCALE
        m = jnp.max(s, axis=-1, keepdims=True)
        e = jnp.exp(s - m)
        p = e / jnp.sum(e, axis=-1, keepdims=True)
        o_ref[...] = _dot(p, v_ref[...]).astype(o_ref.dtype)

    qblk = pl.BlockSpec((CROSS_ROWS, HEAD_DIM), lambda h, i: (i, h))
    return _pcall(
        kern, name=name, grid=(CROSS_HEADS, SEQ // CROSS_ROWS),
        in_specs=[qblk, pl.BlockSpec((MEM_LEN, HEAD_DIM), lambda h, i: (0, h)),
                  pl.BlockSpec((MEM_LEN, HEAD_DIM), lambda h, i: (0, CROSS_HEADS + h))],
        out_specs=qblk, out_shape=jax.ShapeDtypeStruct((SEQ, CROSS_WIDTH), BF16), args=(qc, kvc, kvc),
        semantics=("parallel", "parallel"), carried=carried)


def _cross_bwd(qc, kvc, doc, name):
    def kern(q_ref, k_ref, v_ref, do_ref, dq_ref, dk_ref, dv_ref):
        q, k, v, do = q_ref[...], k_ref[...], v_ref[...], do_ref[...]
        s = _dot_nt(q, k) * ATT_SCALE
        m = jnp.max(s, axis=-1, keepdims=True)
        e = jnp.exp(s - m)
        p = e / jnp.sum(e, axis=-1, keepdims=True)
        dp = _dot_nt(do, v)
        ds = p * (dp - jnp.sum(dp * p, axis=-1, keepdims=True))
        dq_ref[...] = (_dot(ds, k) * ATT_SCALE).astype(dq_ref.dtype)
        dk = _dot_tn(ds, q) * ATT_SCALE
        dv = _dot_tn(p, do)

        @pl.when(pl.program_id(1) == 0)
        def _():
            dk_ref[...] = dk
            dv_ref[...] = dv

        @pl.when(pl.program_id(1) > 0)
        def _():
            dk_ref[...] += dk
            dv_ref[...] += dv

    qblk = pl.BlockSpec((CROSS_ROWS, HEAD_DIM), lambda h, i: (i, h))
    kblk = pl.BlockSpec((MEM_LEN, HEAD_DIM), lambda h, i: (0, h))
    dq, dk, dv = pl.pallas_call(
        kern, name=name, grid=(CROSS_HEADS, SEQ // CROSS_ROWS),
        in_specs=[qblk, kblk, pl.BlockSpec((MEM_LEN, HEAD_DIM), lambda h, i: (0, CROSS_HEADS + h)), qblk],
        out_specs=[qblk, kblk, kblk],
        out_shape=[jax.ShapeDtypeStruct((SEQ, CROSS_WIDTH), BF16),
                   jax.ShapeDtypeStruct((MEM_LEN, CROSS_WIDTH), F32),
                   jax.ShapeDtypeStruct((MEM_LEN, CROSS_WIDTH), F32)],
        compiler_params=_cparams(("parallel", "arbitrary")),
    )(qc, kvc, kvc, doc)
    return dq, jnp.concatenate([dk, dv], axis=1)


FULL_SPECS = {
    "w_in": ("col", D_MODEL, IN_WIDTH),
    "w_branch_a": ("col", ATT_OUT, D_MODEL),
    "w_branch_b": ("col", HG_WIDTH, D_MODEL),
    "w_out": ("row", D_MODEL, D_MODEL),
    "wq_cross": ("row", D_MODEL, CROSS_WIDTH),
    "wkv_cross": ("row", D_MODEL, 2 * CROSS_WIDTH),
    "wo_cross": ("col", CROSS_WIDTH, D_MODEL),
    "w13": ("col", D_MODEL, 2 * D_FF),
    "w2": ("row", D_FF, D_MODEL),
}
WEIGHT_PLACE = {
    "w_in": ("w_in", 0), "w_branch_a": ("w_branch_a", 0), "w_branch_b": ("w_branch_b", 0),
    "w_out": ("w_out", 0), "wq_cross": ("wq_cross", 0), "wkv_cross": ("wkv_cross", 0),
    "wo_cross": ("wo_cross", 0), "w1": ("w13", 0), "w3": ("w13", FF_SHARD), "w2": ("w2", 0),
}
BIG_WEIGHTS = tuple(WEIGHT_PLACE)
EW_BLOCK_ELEMS = 512 * 1024


def _position():
    return lax.axis_index("x"), lax.axis_index("y"), lax.axis_index("c")


def _other_chips(x, y):
    return [(1 - x, y), (x, 1 - y), (1 - x, 1 - y)]


def _half(ref, kind, h):
    r, c = ref.shape
    if kind == "col":
        return ref.at[pl.ds(h * (r // 2), r // 2), :]
    return ref.at[:, pl.ds(h * (c // 2), c // 2)]


def _shard_of(ref, kind, start, size):
    return ref.at[:, pl.ds(start, size)] if kind == "col" else ref.at[pl.ds(start, size), :]


def _rows_of(ref, r0, nrows):
    return ref if nrows is None else ref.at[pl.ds(r0, nrows), :]


def _half_shape(kind, rows, cols):
    return (rows // 2, cols) if kind == "col" else (rows, cols // 2)


def _slot_shape(spec):
    kind, rows, cols = spec
    hr, hc = _half_shape(kind, rows, cols)
    return (hr, hc // N_CHIPS) if kind == "col" else (hr // N_CHIPS, hc)


def _remote(src, dst, send_sem, recv_sem, device):
    return pltpu.make_async_remote_copy(src_ref=src, dst_ref=dst, send_sem=send_sem, recv_sem=recv_sem,
                                        device_id=device, device_id_type=MESH)


def _gather_ici_comm(fulls, jobs, specs):
    def piece(refs, job, chip, c):
        f, r0, nr = job
        kind, rows, cols = specs[f]
        stride = (cols if kind == "col" else rows) // N_CHIPS
        return _rows_of(_half(_shard_of(refs[f], kind, chip * stride, stride), kind, c), r0, nr)

    def start(refs, ss, rs):
        x, y, c = _position()
        j = 2 * x + y
        for q, job in enumerate(jobs):
            for p, (px, py) in enumerate(_other_chips(x, y)):
                _remote(piece(refs, job, j, c), piece(refs, job, j, c), ss.at[3 * q + p], rs.at[3 * q + p],
                        (px, py, c)).start()

    def finish(refs, ss, rs):
        x, y, c = _position()
        j = 2 * x + y
        for q, job in enumerate(jobs):
            for p, (px, py) in enumerate(_other_chips(x, y)):
                _remote(piece(refs, job, j, c), piece(refs, job, 2 * px + py, c), ss.at[3 * q + p],
                        rs.at[3 * q + p], (px, py, c)).wait_recv()
        for q, job in enumerate(jobs):
            for p, (px, py) in enumerate(_other_chips(x, y)):
                _remote(piece(refs, job, j, c), piece(refs, job, j, c), ss.at[3 * q + p], rs.at[3 * q + p],
                        (px, py, c)).wait_send()

    names = list(dict.fromkeys(job[0] for job in jobs))
    return _Carried({f: fulls[f] for f in names}, {}, 3 * len(jobs), start, finish)


def _gather_ring_comm(fulls, f, r0, nr, phase, specs):
    kind, _, cols = specs[f]
    assert kind == "col" and nr % 32 == 0
    stride = cols // N_CHIPS
    half = nr // 2

    def rows(refs, chip, c, lo, n):
        return _rows_of(_half(_shard_of(refs[f], kind, chip * stride, stride), kind, c), r0 + lo, n)

    def copies(refs, ss, rs):
        x, y, c = _position()
        me, nx, ny, dg = 2 * x + y, 2 * (1 - x) + y, 2 * x + (1 - y), 2 * (1 - x) + (1 - y)
        to_x, to_y = (1 - x, y, c), (x, 1 - y, c)
        if phase == "a":
            mine = rows(refs, me, c, 0, nr)
            return [(_remote(mine, mine, ss.at[0], rs.at[0], to_x), rows(refs, nx, c, 0, nr)),
                    (_remote(mine, mine, ss.at[1], rs.at[1], to_y), rows(refs, ny, c, 0, nr))]
        up, low = rows(refs, ny, c, half, half), rows(refs, nx, c, 0, half)
        return [(_remote(up, up, ss.at[0], rs.at[0], to_x), rows(refs, dg, c, half, half)),
                (_remote(low, low, ss.at[1], rs.at[1], to_y), rows(refs, dg, c, 0, half))]

    def start(refs, ss, rs):
        for cp, _ in copies(refs, ss, rs):
            cp.start()

    def finish(refs, ss, rs):
        x, y, c = _position()
        mine = copies(refs, ss, rs)
        for i, (_, landing) in enumerate(mine):
            _remote(landing, landing, ss.at[i], rs.at[i], (x, y, c)).wait_recv()
        for cp, _ in mine:
            cp.wait_send()

    return _Carried({f: fulls[f]}, {}, 2, start, finish)


def _gather_d2d_comm(fulls, jobs, specs):
    def rect(refs, job, h):
        f, r0, nr = job
        assert nr is None or specs[f][0] == "col"
        return _rows_of(_half(refs[f], specs[f][0], h), r0, nr)

    def start(refs, ss, rs):
        x, y, c = _position()
        for q, job in enumerate(jobs):
            _remote(rect(refs, job, c), rect(refs, job, c), ss.at[q], rs.at[q], (x, y, 1 - c)).start()

    def finish(refs, ss, rs):
        x, y, c = _position()
        for q, job in enumerate(jobs):
            _remote(rect(refs, job, 1 - c), rect(refs, job, 1 - c), ss.at[q], rs.at[q], (x, y, 1 - c)).wait_recv()
        for q, job in enumerate(jobs):
            _remote(rect(refs, job, c), rect(refs, job, c), ss.at[q], rs.at[q], (x, y, 1 - c)).wait_send()

    names = list(dict.fromkeys(job[0] for job in jobs))
    return _Carried({f: fulls[f] for f in names}, {}, len(jobs), start, finish)


def _pairx_comm(grads, names, specs, whole=False):
    def copies(refs, ss, rs):
        x, y, c = _position()
        src = (lambda f: refs[("g", f)]) if whole else (lambda f: _half(refs[("g", f)], specs[f][0], 1 - c))
        return [_remote(src(f), refs[("r", f)], ss.at[i], rs.at[i], (x, y, 1 - c)) for i, f in enumerate(names)]

    def start(refs, ss, rs):
        for cp in copies(refs, ss, rs):
            cp.start()

    def finish(refs, ss, rs):
        for cp in copies(refs, ss, rs):
            cp.wait_recv()
        for cp in copies(refs, ss, rs):
            cp.wait_send()

    fresh = {("r", f): jax.ShapeDtypeStruct(_half_shape(*specs[f]), BF16) for f in names}
    return _Carried({}, fresh, len(names), start, finish, reads={("g", f): grads[f] for f in names})


def _chipx_comm(pair_sums, slots, jobs, specs):
    def copies(refs, ss, rs):
        x, y, c = _position()
        out = []
        for q, (f, r0, nr) in enumerate(jobs):
            kind = specs[f][0]
            width = _slot_shape(specs[f])[1 if kind == "col" else 0]
            for p, (px, py) in enumerate(_other_chips(x, y)):
                src = _rows_of(_shard_of(refs[("p", f)], kind, (2 * px + py) * width, width), r0, nr)
                dst = _rows_of(refs[("s", f)].at[p], r0, nr)
                out.append(_remote(src, dst, ss.at[3 * q + p], rs.at[3 * q + p], (px, py, c)))
        return out

    def start(refs, ss, rs):
        for cp in copies(refs, ss, rs):
            cp.start()

    def finish(refs, ss, rs):
        for cp in copies(refs, ss, rs):
            cp.wait_recv()
        for cp in copies(refs, ss, rs):
            cp.wait_send()

    names = list(dict.fromkeys(job[0] for job in jobs))
    arrays = {("p", f): pair_sums[f] for f in names}
    arrays.update({("s", f): slots[f] for f in names})
    return _Carried(arrays, {}, 3 * len(jobs), start, finish)


def _share_comm(grads, wnames, specs, place):
    def start(refs, ss, rs):
        x, y, c = _position()
        for i, w in enumerate(wnames):
            kind = specs[place[w][0]][0]
            _remote(_half(refs[w], kind, c), _half(refs[w], kind, c), ss.at[i], rs.at[i], (x, y, 1 - c)).start()

    def finish(refs, ss, rs):
        x, y, c = _position()
        for i, w in enumerate(wnames):
            kind = specs[place[w][0]][0]
            _remote(_half(refs[w], kind, 1 - c), _half(refs[w], kind, 1 - c), ss.at[i], rs.at[i],
                    (x, y, 1 - c)).wait_recv()
        for i, w in enumerate(wnames):
            kind = specs[place[w][0]][0]
            _remote(_half(refs[w], kind, c), _half(refs[w], kind, c), ss.at[i], rs.at[i], (x, y, 1 - c)).wait_send()

    return _Carried({w: grads[w] for w in wnames}, {}, len(wnames), start, finish)


def _gather_rows(v, name="gather_small"):
    shape = v.shape

    def body(v_ref, out_ref, send_sem, recv_sem, loc_sem):
        x, y, c = _position()
        me = 4 * x + 2 * y + c
        flips = [(fx, fy, fc) for fx in (0, 1) for fy in (0, 1) for fc in (0, 1)][1:]

        def peer(fl):
            return tuple(1 - a if f else a for a, f in zip((x, y, c), fl))

        loc = pltpu.make_async_copy(v_ref, out_ref.at[me], loc_sem)
        loc.start()
        sends = []
        for i, fl in enumerate(flips):
            cp = _remote(v_ref, out_ref.at[me], send_sem.at[i], recv_sem.at[i], peer(fl))
            cp.start()
            sends.append(cp)
        for i, fl in enumerate(flips):
            px, py, pc = peer(fl)
            _remote(v_ref, out_ref.at[4 * px + 2 * py + pc], send_sem.at[i], recv_sem.at[i], peer(fl)).wait_recv()
        for cp in sends:
            cp.wait_send()
        loc.wait()

    return pl.pallas_call(
        body, name=name, in_specs=[ANY], out_specs=ANY,
        out_shape=jax.ShapeDtypeStruct((N_DEV,) + shape, F32),
        scratch_shapes=[pltpu.SemaphoreType.DMA((N_DEV - 1,)), pltpu.SemaphoreType.DMA((N_DEV - 1,)),
                        pltpu.SemaphoreType.DMA],
    )(v)


def _ew_block(rows, cols, elems=EW_BLOCK_ELEMS):
    tc = cols if cols <= 4096 else _div(cols, 2048, LANE)
    tr = _div(rows, max(16, elems // tc), 16)
    return tr, tc


def _mesh_scalars():
    x, y, c = _position()
    return jnp.stack([c, 2 * x + y]).astype(jnp.int32)


def _grid_spec(grid, in_specs, out_specs):
    return pltpu.PrefetchScalarGridSpec(num_scalar_prefetch=1, grid=grid, in_specs=in_specs, out_specs=out_specs)


def _cast_into_full(parts, fname, pos, specs, place, name, token=None):
    kind, rows, cols = specs[fname]
    ws = [w for w in place if place[w][0] == fname]
    if kind == "col":
        stride = cols // N_CHIPS
        hr = rows // 2
        tr = _div(hr, max(16, EW_BLOCK_ELEMS // stride), 16)
        nrb = hr // tr
        in_specs = [pl.BlockSpec((tr, parts[w].shape[1]), lambda i, pos_ref: (i + pos_ref[0] * nrb, 0)) for w in ws]
        out_spec = pl.BlockSpec((tr, stride), lambda i, pos_ref: (i + pos_ref[0] * nrb, pos_ref[1]))
    else:
        stride = rows // N_CHIPS
        hc = cols // 2
        tr = _div(stride, max(16, EW_BLOCK_ELEMS // hc), 16)
        nrb = stride // tr
        in_specs = [pl.BlockSpec((tr, hc), lambda i, pos_ref: (i, pos_ref[0])) for w in ws]
        out_spec = pl.BlockSpec((tr, hc), lambda i, pos_ref: (i + pos_ref[1] * nrb, pos_ref[0]))

    def kern(pos_ref, *refs):
        o_ref = refs[-1]
        for w, r in zip(ws, refs[:len(ws)]):
            off = place[w][1] if kind == "col" else 0
            o_ref[:, off:off + r.shape[1]] = r[...].astype(o_ref.dtype)

    tokens = [] if token is None else [token]
    in_specs = in_specs + [pl.BlockSpec(TOKEN_SHAPE, lambda i, pos_ref: (0, 0))] * len(tokens)
    return pl.pallas_call(
        kern, name=name, grid_spec=_grid_spec((nrb,), in_specs, out_spec),
        out_shape=jax.ShapeDtypeStruct((rows, cols), BF16),
        compiler_params=_cparams(("parallel",)),
    )(pos, *[parts[w] for w in ws], *tokens)


def _pair_sum(grad, recv, pos, spec, name, whole=False):
    kind, rows, cols = spec
    hr, hc = _half_shape(kind, rows, cols)
    tr, tc = _ew_block(hr, hc, 2 * EW_BLOCK_ELEMS)
    nrb, ncb = hr // tr, hc // tc
    blk = pl.BlockSpec((tr, tc), lambda i, jj, pos_ref: (i, jj))
    if whole:
        mine = blk
    elif kind == "col":
        mine = pl.BlockSpec((tr, tc), lambda i, jj, pos_ref: (i + pos_ref[0] * nrb, jj))
    else:
        mine = pl.BlockSpec((tr, tc), lambda i, jj, pos_ref: (i, jj + pos_ref[0] * ncb))

    def kern(pos_ref, g_ref, r_ref, o_ref, slots_ref):
        o_ref[...] = (g_ref[...].astype(F32) + r_ref[...].astype(F32)).astype(o_ref.dtype)

    return pl.pallas_call(
        kern, name=name, grid_spec=_grid_spec((nrb, ncb), [mine, blk], [blk, ANY]),
        out_shape=[jax.ShapeDtypeStruct((hr, hc), BF16),
                   jax.ShapeDtypeStruct((N_CHIPS - 1,) + _slot_shape(spec), BF16)],
        compiler_params=_cparams(("parallel", "parallel")),
    )(pos, grad, recv)


def _chip_sum(pair_sum, slots, pos, fname, shard_shapes, specs, place, name):
    kind, rows, cols = specs[fname]
    sr, sc = _slot_shape(specs[fname])
    ws = [w for w in place if place[w][0] == fname]
    n_slots = N_CHIPS - 1
    tr = _div(sr, max(16, EW_BLOCK_ELEMS // sc), 16)
    nrb = sr // tr
    slot = pl.BlockSpec((n_slots, tr, sc), lambda i, pos_ref: (0, i, 0))
    if kind == "col":
        own = pl.BlockSpec((tr, sc), lambda i, pos_ref: (i, pos_ref[1]))
        out_specs = [pl.BlockSpec((tr, shard_shapes[w][1]), lambda i, pos_ref: (i + pos_ref[0] * nrb, 0)) for w in ws]
    else:
        own = pl.BlockSpec((tr, sc), lambda i, pos_ref: (i + pos_ref[1] * nrb, 0))
        out_specs = [pl.BlockSpec((tr, sc), lambda i, pos_ref: (i, pos_ref[0])) for w in ws]

    def kern(pos_ref, own_ref, slot_ref, *out_refs):
        tot = own_ref[...].astype(F32)
        for s in range(n_slots):
            tot = tot + slot_ref[s].astype(F32)
        for w, o_ref in zip(ws, out_refs):
            off = place[w][1] if kind == "col" else 0
            o_ref[...] = tot[:, off:off + o_ref.shape[1]]

    outs = pl.pallas_call(
        kern, name=name, grid_spec=_grid_spec((nrb,), [own, slot], out_specs),
        out_shape=[jax.ShapeDtypeStruct(shard_shapes[w], F32) for w in ws],
        compiler_params=_cparams(("parallel",)),
    )(pos, pair_sum, slots)
    return dict(zip(ws, outs))


def _adam_math(w, g, m, v):
    m2 = ADAM_B1 * m + (1.0 - ADAM_B1) * g
    v2 = ADAM_B2 * v + (1.0 - ADAM_B2) * (g * g)
    m_hat = m2 / (1.0 - ADAM_B1 ** ADAM_STEP)
    v_hat = v2 / (1.0 - ADAM_B2 ** ADAM_STEP)
    delta = -ADAM_LR * (m_hat / (jnp.sqrt(v_hat) + ADAM_EPS) + ADAM_WD * w)
    return delta, m2, v2


def _adamw(w, g, m, v, name, carried=()):
    rows, cols = w.shape
    tr, tc = _ew_block(rows, cols)

    def kern(w_ref, g_ref, m_ref, v_ref, d_ref, m2_ref, v2_ref, g_out_ref):
        g_ = g_ref[...]
        d_ref[...], m2_ref[...], v2_ref[...] = _adam_math(w_ref[...], g_, m_ref[...], v_ref[...])
        g_out_ref[...] = g_

    blk = pl.BlockSpec((tr, tc), lambda i, j: (i, j))
    return _pcall(
        kern, name=name, grid=(rows // tr, cols // tc), in_specs=[blk] * 4, out_specs=[blk] * 4,
        out_shape=[jax.ShapeDtypeStruct((rows, cols), F32)] * 4, args=(w, g, m, v),
        semantics=("parallel", "parallel"), carried=carried)


SMALL_ROWS = ("ln_mix_w", "ln_cross_w", "ln_mem_w", "ln_ffn_w", "ln_final_w")
ROW_HG_NORM, ROW_LB0, ROW_LB1 = 5, 6, 7
LOSS_LANE0 = HEAD_DIM


def _pack_small(vals):
    rows = [vals[n].reshape(1, D_MODEL) for n in SMALL_ROWS]
    pad = lambda a: jnp.pad(a, ((0, 0), (0, D_MODEL - a.shape[1])))
    rows.append(pad(vals["hg_norm_w"].reshape(1, HEAD_DIM)))
    rows.append(pad(vals["hg_lower_bounds"].reshape(2, HG_WIDTH)))
    return jnp.concatenate(rows, axis=0)


def _small_update(gathered, w, m, v, name="small_update"):
    def kern(g_ref, w_ref, m_ref, v_ref, grad_ref, d_ref, m2_ref, v2_ref, loss_ref):
        tot = g_ref[0]
        for s in range(1, N_DEV):
            tot = tot + g_ref[s]
        wv = w_ref[...]
        row = lax.broadcasted_iota(jnp.int32, (8, D_MODEL), 0)
        lane = lax.broadcasted_iota(jnp.int32, (8, D_MODEL), 1)
        l0, l1 = wv[ROW_LB0:ROW_LB0 + 1], wv[ROW_LB1:ROW_LB1 + 1]
        mx = jnp.maximum(l0, l1)
        e0, e1 = jnp.exp(l0 - mx), jnp.exp(l1 - mx)
        p0 = e0 / (e0 + e1)
        dlog = tot[ROW_LB0:ROW_LB0 + 1] * p0 * (1.0 - p0)
        tot = jnp.where(row == ROW_HG_NORM, tot + tot[ROW_LB1:ROW_LB1 + 1], tot)
        grad = jnp.where(row == ROW_LB0, dlog, jnp.where(row == ROW_LB1, -dlog, tot))
        grad = jnp.where((row == ROW_HG_NORM) & (lane >= HEAD_DIM), 0.0, grad)
        grad = jnp.where((row >= ROW_LB0) & (lane >= HG_WIDTH), 0.0, grad)
        grad_ref[...] = grad
        d_ref[...], m2_ref[...], v2_ref[...] = _adam_math(wv, grad, m_ref[...], v_ref[...])
        loss_ref[...] = tot[ROW_HG_NORM:ROW_HG_NORM + 1, LOSS_LANE0:LOSS_LANE0 + LANE]

    full = pl.BlockSpec((8, D_MODEL), lambda: (0, 0))
    return pl.pallas_call(
        kern, name=name,
        in_specs=[pl.BlockSpec((N_DEV, 8, D_MODEL), lambda: (0, 0, 0)), full, full, full],
        out_specs=[full, full, full, full, pl.BlockSpec((1, LANE), lambda: (0, 0))],
        out_shape=[jax.ShapeDtypeStruct((8, D_MODEL), F32)] * 4 + [jax.ShapeDtypeStruct((1, LANE), F32)],
        compiler_params=_cparams(),
    )(gathered, w, m, v)


def _unpack_small(p, shapes):
    out = {n: p[i].reshape(shapes[n]) for i, n in enumerate(SMALL_ROWS)}
    out["hg_norm_w"] = p[ROW_HG_NORM, :HEAD_DIM].reshape(shapes["hg_norm_w"])
    out["hg_lower_bounds"] = p[ROW_LB0:ROW_LB1 + 1, :HG_WIDTH].reshape(shapes["hg_lower_bounds"])
    return out


WHOLE = lambda f: (f, 0, None)
MID_MATRICES = ("w_branch_a", "w_branch_b", "w_out", "wq_cross", "wkv_cross", "wo_cross")
MID_WEIGHTS = MID_MATRICES
W_IN_PIECES = [("w_in", 0, 768), ("w_in", 768, 256)]
W13_PIECES = [("w13", r0, 512) for r0 in range(0, D_MODEL // 2, 512)]
GATHER_GROUPS = [("mid", [WHOLE(f) for f in MID_MATRICES]), ("w13a", W13_PIECES[:1]), ("w13b", W13_PIECES[1:]),
                 ("w2", [WHOLE("w2")])]
OTHER_WEIGHTS = ["w1", "w3", "w2"] + list(MID_WEIGHTS)
BEFORE = {
    "hgrn_fwd": [("wait", "mid")],
    "mm_out": [("wait", "w13a")],
    "mm_o": [("wait", "w13b")],
    "mm_w2": [("wait", "w2"), ("run", ("d2d", [WHOLE("w2")]), "gather_hand_over_w2")],
    "mm_dh": [("wait", "rs_w2"), ("chip_sum", "w2"), ("wait", "rs_w13"), ("chip_sum", "w13"), ("wait", "rs_mid")]
    + [("chip_sum", f) for f in MID_MATRICES],
}
CARRY = {
    "hgrn_fwd": [("d2d", [WHOLE(f) for f in MID_MATRICES])],
    "mm_out": [("d2d", W13_PIECES[:1])],
    "mm_o": [("d2d", W13_PIECES[1:])],
    "mm_du": [("pairx", ["w2"])],
    "mm_dhf": [("pairx", ["w13"])],
    "attn_bwd_g0": [("pairx", list(MID_MATRICES))],
    "mm_dwin_own": [("pairx_whole", ["w_in"])],
    "mm_dh": [("share", OTHER_WEIGHTS)],
}
AFTER = {
    "mm_du": [("pair_sum", "w2"), ("start", "rs_w2", [WHOLE("w2")])],
    "mm_dhf": [("pair_sum", "w13"), ("start", "rs_w13", [WHOLE("w13")])],
    "attn_bwd_g0": [("pair_sum", f) for f in MID_MATRICES] + [("start", "rs_mid", [WHOLE(f) for f in MID_MATRICES])],
    "mm_dwin_own": [("pair_sum", "w_in"), ("start", "rs_w_in", [WHOLE("w_in")])],
}
FINISH = [
    ("adamw", OTHER_WEIGHTS), ("wait", "rs_w_in"), ("small",), ("chip_sum", "w_in"),
    ("run", ("share", ["w_in"]), "rs_sibling_share_w_in"), ("adamw", ["w_in"]),
]


class _Net:
    def __init__(self, full, pos=None, shard_shapes=None, comm=True, specs=FULL_SPECS, place=WEIGHT_PLACE):
        self.full, self.pos, self.shard_shapes, self.comm = dict(full), pos, shard_shapes, comm
        self.specs, self.place = specs, place
        self.gw, self.recv, self.psum, self.slots, self.grads = {}, {}, {}, {}, {}
        self.gw_sibling = {}
        self.pending, self.token, self.last = {}, None, None

    def _make(self, kind, arg):
        if kind == "gather":
            return _gather_ici_comm(self.full, arg, self.specs)
        if kind == "ring":
            return _gather_ring_comm(self.full, *arg, self.specs)
        if kind == "d2d":
            return _gather_d2d_comm(self.full, arg, self.specs)
        if kind == "pairx":
            return _pairx_comm(self.gw, arg, self.specs)
        if kind == "pairx_whole":
            return _pairx_comm(self.gw_sibling, arg, self.specs, whole=True)
        if kind == "chipx":
            return _chipx_comm(self.psum, self.slots, arg, self.specs)
        assert kind == "share"
        return _share_comm(self.grads, arg, self.specs, self.place)

    def _store(self, kind, res):
        if kind in ("gather", "ring", "d2d"):
            self.full.update(res)
        elif kind in ("pairx", "pairx_whole"):
            for (tag, f), a in res.items():
                (self.gw if tag == "g" else self.recv)[f] = a
        elif kind == "chipx":
            for (tag, f), a in res.items():
                (self.psum if tag == "p" else self.slots)[f] = a
        else:
            self.grads.update(res)

    def run_comm(self, item, name):
        kind, arg = item
        self._store(kind, _run_comm([self._make(kind, arg)], name)[0])

    @staticmethod
    def _others(after, items):
        own = [a for cm in items for a in cm.arrays.values()]
        return [a for a in after if a is not None and all(a is not o for o in own)]

    def start(self, groups, kind, name):
        items = [self._make(kind, jobs) for _, jobs in groups]
        after = self._others([self.last], items)
        res, sems, token = _split_start(items, name, after=after[0] if after else None)
        for (group, jobs), r, s in zip(groups, res, sems):
            self._store(kind, r)
            self.pending[group] = (kind, jobs, s)
        self.token = self.last = token

    def wait(self, group, after=()):
        kind, jobs, sems = self.pending.pop(group)
        item = self._make(kind, jobs)
        res = _split_wait([item], [sems], self._others([self.last, *after], [item]), f"wait_{group}")[0]
        self._store(kind, res)

    def step(self, step):
        if step[0] == "wait":
            self.wait(step[1])
        elif step[0] == "start":
            self.start([(step[1], step[2])], "chipx", f"start_{step[1]}")
        elif step[0] == "pair_sum":
            f = step[1]
            self.psum[f], self.slots[f] = _pair_sum(self.gw[f], self.recv[f], self.pos, self.specs[f],
                                                    f"rs_pair_sum_{f}", whole=f in self.gw_sibling)
        elif step[0] == "chip_sum":
            f = step[1]
            self.grads.update(_chip_sum(self.psum[f], self.slots[f], self.pos, f, self.shard_shapes,
                                        self.specs, self.place, f"rs_chip_sum_{f}"))
        else:
            assert step[0] == "run"
            self.run_comm(step[1], step[2])

    def call(self, fn, name, *args, grad_of=None, sibling_half=False, **kw):
        for step in (BEFORE.get(name, []) if self.comm else []):
            self.step(step)
        args = [a() if callable(a) else a for a in args]
        items = CARRY.get(name, []) if self.comm else []
        carried = [self._make(k, a) for k, a in items]
        if self.token is not None:
            carried.append(_Token(self.token))
            self.token = None
        out, res = fn(*args, name=name, carried=carried, **kw)
        if grad_of is not None:
            (self.gw_sibling if sibling_half else self.gw)[grad_of] = out
        self.last = jax.tree.leaves(out)[0]
        for (kind, _), r in zip(items, res):
            self._store(kind, r)
        for step in (AFTER.get(name, []) if self.comm else []):
            self.step(step)
        return out


def _local_step(net, x, h, mem, target, small):
    full, call = net.full, net.call
    proj = call(_mm, "mm_proj", h, full["w_in"], mode="nn", out_dtype=F32)
    att = [call(_attn_fwd, f"attn_fwd_g{g}", proj, g) for g in range(3)]
    outs, lses = [a[0] for a in att], [a[1] for a in att]
    o_att = _attn_merge_fwd(outs, lses, "attn_merge")
    oraw, o_hg, states = call(_hg_fwd, "hgrn_fwd", proj, small["hg_lower_bounds"], small["hg_norm_w"])
    ya = call(_mm, "mm_branch_a", o_att, full["w_branch_a"], mode="nn", out_dtype=F32)
    yb, merged = call(_branch_b_gate, "mm_branch_b", o_hg, full["w_branch_b"], proj, ya)
    x1, hc = call(_residual_norm, "mm_out", merged, full["w_out"], x, small["ln_cross_w"])

    mn = _rms_fwd(mem, small["ln_mem_w"], "rms_mem")
    qc = call(_mm, "mm_q", hc, full["wq_cross"], mode="nn", out_dtype=F32)
    kvc = call(_mm, "mm_kv", mn, full["wkv_cross"], mode="nn", out_dtype=F32)
    oc = call(_cross_fwd, "cross_fwd", qc, kvc)
    x2, hf = call(_residual_norm, "mm_o", oc, full["wo_cross"], x1, small["ln_ffn_w"])

    ab, u = call(_ff_up, "mm_w13", hf, full["w13"])
    x3 = call(_mm, "mm_w2", u, lambda: full["w2"], mode="nn", out_dtype=F32, res=x2)

    dx3, dg_final, loss, dx3_bf16 = _loss_head(x3, small["ln_final_w"], target, "loss_head")

    gs = {"ln_final_w": dg_final}
    call(_mm, "mm_dw2", u, dx3_bf16, mode="tn", out_dtype=BF16, grad_of="w2")
    dab = call(_ff_down_bwd, "mm_du", dx3_bf16, full["w2"], ab)
    call(_mm, "mm_dw13", hf, dab, mode="tn", out_dtype=BF16, grad_of="w13")
    dhf = call(_mm, "mm_dhf", dab, full["w13"], mode="nt", out_dtype=F32)
    dx2, gs["ln_ffn_w"], dx2_bf16 = _rms_bwd(x2, small["ln_ffn_w"], dhf, dx3, "rms_ffn_bwd", bf16_copy=True)
    doc = call(_mm, "mm_doc", dx2_bf16, full["wo_cross"], mode="nt", out_dtype=BF16)
    call(_mm, "mm_dwo", oc, dx2_bf16, mode="tn", out_dtype=BF16, grad_of="wo_cross")
    dqc, dkvc = _cross_bwd(qc, kvc, doc, "cross_bwd")
    call(_mm, "mm_dwq", hc, dqc, mode="tn", out_dtype=BF16, grad_of="wq_cross")
    dx1, gs["ln_cross_w"], dx1_bf16 = call(_norm_bwd_residual, "mm_dhc", dqc, full["wq_cross"], x1,
                                           small["ln_cross_w"], dx2)
    call(_mm, "mm_dwkv", mn, dkvc, mode="tn", out_dtype=BF16, grad_of="wkv_cross")
    dmn = call(_mm, "mm_dmn", dkvc, full["wkv_cross"], mode="nt", out_dtype=F32)
    _, gs["ln_mem_w"] = _rms_bwd(mem, small["ln_mem_w"], dmn, None, "rms_mem_bwd")
    dya, dyb, dga, dgb = call(_dmerged_gate_bwd, "mm_dmerged", dx1_bf16, full["w_out"], proj, ya, yb)
    call(_mm, "mm_dwout", merged, dx1_bf16, mode="tn", out_dtype=BF16, grad_of="w_out")
    call(_mm, "mm_dwa", o_att, dya, mode="tn", out_dtype=BF16, grad_of="w_branch_a")
    do_gs, dl_gs = call(_attn_merge_bwd, "mm_doatt", dya, full["w_branch_a"], outs, lses)
    call(_mm, "mm_dwb", o_hg, dyb, mode="tn", out_dtype=BF16, grad_of="w_branch_b")
    do_hg = call(_mm, "mm_dohg", dyb, full["w_branch_b"], mode="nt", out_dtype=F32)
    dqh, dfh, dih, dgh, dlb, gs["hg_norm_w"] = call(
        _hg_bwd, "hgrn_bwd", proj, small["hg_lower_bounds"], small["hg_norm_w"], oraw, states, do_hg)
    gs["hg_lb"] = dlb
    dqs, dks, dvs = zip(*[call(_attn_bwd, f"attn_bwd_g{g}", proj, g, lses[g], do_gs[g], dl_gs[g]) for g in range(3)])
    dproj = jnp.concatenate([*dqs, *dks, *dvs, dqh, dfh, dih, dgh, dga, dgb], axis=1)
    if net.comm:
        half = D_MODEL // 2
        c = lax.axis_index("c")
        h_sibling = lax.dynamic_slice_in_dim(h, (1 - c) * half, half, axis=1)
        h_own = lax.dynamic_slice_in_dim(h, c * half, half, axis=1)
        call(_mm, "mm_dwin_sibling", h_sibling, dproj, mode="tn", out_dtype=BF16, grad_of="w_in", sibling_half=True)
        call(_mm, "mm_dwin_own", h_own, dproj, mode="tn", out_dtype=BF16, grad_of="w_in")
    else:
        call(_mm, "mm_dwin", h, dproj, mode="tn", out_dtype=BF16, grad_of="w_in")
    dh = call(_mm, "mm_dh", dproj, full["w_in"], mode="nt", out_dtype=F32)
    dx, gs["ln_mix_w"] = _rms_bwd(x, small["ln_mix_w"], dh, dx1, "rms_mix_bwd")
    return loss, dx, gs


WEIGHT_ORDER = ("ln_mix_w", "w_in", "hg_norm_w", "hg_lower_bounds", "w_branch_a", "w_branch_b", "w_out",
                "ln_cross_w", "ln_mem_w", "wq_cross", "wkv_cross", "wo_cross", "ln_ffn_w", "w1", "w3", "w2",
                "ln_final_w")


def kernel(x, mem, ln_mix_w, w_in, hg_norm_w, hg_lower_bounds, w_branch_a, w_branch_b, w_out, ln_cross_w, ln_mem_w, wq_cross, wkv_cross, wo_cross, ln_ffn_w, w1, w3, w2, ln_final_w, loss_target, m_ln_mix_w, m_w_in, m_hg_norm_w, m_hg_lower_bounds, m_w_branch_a, m_w_branch_b, m_w_out, m_ln_cross_w, m_ln_mem_w, m_wq_cross, m_wkv_cross, m_wo_cross, m_ln_ffn_w, m_w1, m_w3, m_w2, m_ln_final_w, v_ln_mix_w, v_w_in, v_hg_norm_w, v_hg_lower_bounds, v_w_branch_a, v_w_branch_b, v_w_out, v_ln_cross_w, v_ln_mem_w, v_wq_cross, v_wkv_cross, v_wo_cross, v_ln_ffn_w, v_w1, v_w3, v_w2, v_ln_final_w):
    args = dict(locals())
    w = {n: args[n] for n in WEIGHT_ORDER}
    m = {n: args["m_" + n] for n in WEIGHT_ORDER}
    v = {n: args["v_" + n] for n in WEIGHT_ORDER}
    shapes = {n: w[n].shape for n in WEIGHT_ORDER}
    mat = lambda a: a.reshape(a.shape[-2:])
    shard_shapes = {n: shapes[n][-2:] for n in BIG_WEIGHTS}

    pos = _mesh_scalars()

    def cast(f, token=None):
        return _cast_into_full({n: mat(w[n]) for n in BIG_WEIGHTS if WEIGHT_PLACE[n][0] == f}, f, pos,
                               FULL_SPECS, WEIGHT_PLACE, f"cast_{f}", token)

    net = _Net({"w_in": cast("w_in")}, pos, shard_shapes)
    net.start([(f"ring_a{i}", (*job, "a")) for i, job in enumerate(W_IN_PIECES)], "ring", "gather_start_w_in")
    rest = {f: cast(f, net.token) for f in FULL_SPECS if f != "w_in"}
    net.full.update(rest)
    small = {n: w[n].reshape(1, -1) for n in SMALL_ROWS}
    small["hg_norm_w"] = w["hg_norm_w"].reshape(1, HEAD_DIM)
    small["hg_lower_bounds"] = w["hg_lower_bounds"]
    x2d = x.reshape(SEQ, D_MODEL)
    h = _rms_fwd(x2d, small["ln_mix_w"], "rms_mix")
    for i, job in enumerate(W_IN_PIECES):
        net.wait(f"ring_a{i}", after=[*rest.values(), h] if i == 0 else ())
        net.start([(f"ring_b{i}", (*job, "b"))], "ring", f"gather_pass_on_w_in{i}")
    net.start(GATHER_GROUPS, "gather", "gather_start_rest")
    for i, job in enumerate(W_IN_PIECES):
        net.wait(f"ring_b{i}")
        net.run_comm(("d2d", [job]), f"gather_hand_over_w_in{i}")
    loss, dx, gs = _local_step(net, x2d, h, mem.reshape(MEM_LEN, D_MODEL), loss_target.reshape(SEQ, D_MODEL), small)

    out_g, out_d, out_m, out_v = {}, {}, {}, {}
    net.last = dx
    for step in FINISH:
        if step[0] == "adamw":
            for n in step[1]:
                out_d[n], out_m[n], out_v[n], out_g[n] = net.call(_adamw, f"adamw_{n}", mat(w[n]), net.grads[n],
                                                                  mat(m[n]), mat(v[n]))
        elif step[0] == "wait":
            net.wait(step[1], after=list(out_d.values()))
        elif step[0] == "small":
            pad = lambda a: jnp.pad(a, ((0, 0), (0, D_MODEL - a.shape[1])))
            part = jnp.concatenate(
                [gs[n] for n in SMALL_ROWS]
                + [pad(jnp.concatenate([gs["hg_norm_w"][0], loss], axis=1)), pad(gs["hg_lb"]),
                   pad(gs["hg_norm_w"][1]) if len(gs["hg_norm_w"]) > 1 else jnp.zeros((1, D_MODEL), F32)], axis=0)
            part, net.psum["w_in"] = lax.optimization_barrier((part, net.psum["w_in"]))
            sg, sd, sm, sv, loss_tot = _small_update(_gather_rows(part), _pack_small(w), _pack_small(m),
                                                     _pack_small(v))
            for dst, packed in ((out_g, sg), (out_d, sd), (out_m, sm), (out_v, sv)):
                dst.update(_unpack_small(packed, shapes))
        else:
            net.step(step)

    result = [loss_tot[0, 0], dx.reshape(x.shape)]
    for group in (out_g, out_d, out_m, out_v):
        result += [group[n].reshape(shapes[n]) for n in WEIGHT_ORDER]
    return tuple(result)
```

```python
import math

import jax
import jax.numpy as jnp
from jax import lax
from jax.experimental import pallas as pl
from jax.experimental.pallas import tpu as pltpu

F32 = jnp.float32
BF16 = jnp.bfloat16
MESH = pl.DeviceIdType.MESH

D_MODEL = 2048
SEQ = 2048
HEAD_DIM = 128
MEM_LEN = 256
ATT_GROUPS = ((128, 1), (512, 4), (2048, 16))
ATT_HEADS = 4
ATT_WIDTH = 3 * ATT_HEADS * HEAD_DIM
ATT_OUT = ATT_HEADS * HEAD_DIM
ATT_BLOCK = 128
HG_HEADS = 8
HG_WIDTH = HG_HEADS * HEAD_DIM
HG_CHUNK = 64
IN_WIDTH = 3 * ATT_WIDTH + 4 * HG_WIDTH + 2 * D_MODEL
CROSS_HEADS = 4
CROSS_WIDTH = CROSS_HEADS * HEAD_DIM
D_FF = 5632
RMS_EPS = 1e-6
ADAM_LR = 0.001
ADAM_B1 = 0.9
ADAM_B2 = 0.999
ADAM_EPS = 1e-08
ADAM_WD = 0.01
ADAM_STEP = 10
N_CHIPS = 4
N_DEV = 8

VMEM_LIMIT_BYTES = 56 * 1024 * 1024
LANE = 128
MXU_WIDTH = 256
MM_TILE_CAP = 1536
TRANSPOSE_CHUNK = 512
ANY = pl.BlockSpec(memory_space=pl.ANY)


def _cparams(sem=None):
    return pltpu.CompilerParams(dimension_semantics=sem, vmem_limit_bytes=VMEM_LIMIT_BYTES)


def _div(n, cap, mult):
    best = None
    for d in range(mult, min(n, cap) + 1, mult):
        if n % d == 0:
            best = d
    assert best is not None, (n, cap, mult)
    return best


def _sigmoid(x):
    return 1.0 / (1.0 + jnp.exp(-x))


def _dot(a, b):
    return jnp.dot(a.astype(BF16), b.astype(BF16), preferred_element_type=F32)


def _dot_nt(a, b):
    return lax.dot_general(a.astype(BF16), b.astype(BF16), (((1,), (1,)), ((), ())),
                           preferred_element_type=F32)


def _dot_tn(a, b):
    return jnp.dot(a.astype(F32).T.astype(BF16), b.astype(BF16), preferred_element_type=F32)


def _dot_exact(a, b):
    return jnp.dot(a, b, precision=lax.Precision.HIGHEST, preferred_element_type=F32)


class _Carried:
    def __init__(self, arrays, fresh, n_sems, start, finish, mid=None, reads=None):
        self.arrays, self.fresh, self.n_sems, self.reads = arrays, fresh, n_sems, reads or {}
        self.start, self.mid, self.finish = start, mid, finish


class _Token:
    def __init__(self, array):
        self.array = array


TOKEN_SHAPE = (8, LANE)


def _carried_layout(carried):
    akeys = list(dict.fromkeys(k for cm in carried for k in cm.arrays))
    fkeys = [(ci, k) for ci, cm in enumerate(carried) for k in cm.fresh]
    arrays = [next(cm.arrays[k] for cm in carried if k in cm.arrays) for k in akeys]
    shapes = [jax.ShapeDtypeStruct(a.shape, a.dtype) for a in arrays] + [carried[ci].fresh[k] for ci, k in fkeys]
    sems = []
    for cm in carried:
        sems += [pltpu.SemaphoreType.DMA((cm.n_sems,)), pltpu.SemaphoreType.DMA((cm.n_sems,))]
    return akeys, fkeys, arrays, shapes, sems


def _carried_reads(carried):
    rkeys = list(dict.fromkeys(k for cm in carried for k in cm.reads))
    return rkeys, [next(cm.reads[k] for cm in carried if k in cm.reads) for k in rkeys]


def _carried_results(carried, akeys, fkeys, outs, rkeys=(), read_refs=()):
    shared = dict(zip(akeys, outs[:len(akeys)]))
    shared.update(zip(rkeys, read_refs))
    res = [{k: shared[k] for k in list(cm.arrays) + [r for r in cm.reads if r in shared]} for cm in carried]
    for (ci, k), o in zip(fkeys, outs[len(akeys):]):
        res[ci][k] = o
    return res


def _pcall(kern, *, name, grid, in_specs, out_specs, out_shape, args, scratch_shapes=(), semantics=None,
           carried=()):
    tokens = [c.array for c in carried if isinstance(c, _Token)]
    carried = [c for c in carried if not isinstance(c, _Token)]
    single = not isinstance(out_shape, (list, tuple))
    out_specs = [out_specs] if single else list(out_specs)
    out_shape = [out_shape] if single else list(out_shape)
    n_real, n_out, n_scr = len(in_specs), len(out_shape), len(scratch_shapes)
    in_specs = list(in_specs) + [pl.BlockSpec(TOKEN_SHAPE, lambda *_: (0, 0))] * len(tokens)
    args = list(args) + tokens
    n_in = len(in_specs)
    if not carried:
        def plain(*refs):
            kern(*refs[:n_real], *refs[n_in:])

        outs = pl.pallas_call(plain if tokens else kern, name=name, grid=grid, in_specs=in_specs,
                              out_specs=out_specs, out_shape=out_shape, scratch_shapes=list(scratch_shapes),
                              compiler_params=_cparams(semantics))(*args)
        return (outs[0] if single else list(outs)), []
    akeys, fkeys, arrays, shapes, sems = _carried_layout(carried)
    rkeys, reads = _carried_reads(carried)
    n_a, n_f, n_r = len(akeys), len(fkeys), len(rkeys)
    total = math.prod(grid)
    mid_step = min(total - 1, (17 * total) // 20)

    def wrapped(*refs):
        ins = refs[:n_real]
        r0 = n_in + n_a
        o0 = r0 + n_r
        outs = refs[o0:o0 + n_out]
        a0 = o0 + n_out
        s0 = a0 + n_a + n_f
        per = _carried_results(carried, akeys, fkeys, refs[a0:s0], rkeys, refs[r0:o0])
        scratch = refs[s0:s0 + n_scr]
        sem = refs[s0 + n_scr:]
        step = 0
        for d, g in enumerate(grid):
            step = step * g + pl.program_id(d)

        @pl.when(step == 0)
        def _():
            for ci, cm in enumerate(carried):
                cm.start(per[ci], sem[2 * ci], sem[2 * ci + 1])

        kern(*ins, *outs, *scratch)

        @pl.when(step == mid_step)
        def _():
            for ci, cm in enumerate(carried):
                if cm.mid is not None:
                    cm.mid(per[ci], sem[2 * ci], sem[2 * ci + 1])

        @pl.when(step == total - 1)
        def _():
            for ci, cm in enumerate(carried):
                cm.finish(per[ci], sem[2 * ci], sem[2 * ci + 1])

    outs = pl.pallas_call(
        wrapped, name=name, grid=grid,
        in_specs=list(in_specs) + [ANY] * (n_a + n_r), out_specs=out_specs + [ANY] * (n_a + n_f),
        out_shape=out_shape + shapes,
        input_output_aliases={n_in + i: n_out + i for i in range(n_a)},
        scratch_shapes=list(scratch_shapes) + sems,
        compiler_params=_cparams(("arbitrary",) * len(grid)),
    )(*args, *arrays, *reads)
    res = _carried_results(carried, akeys, fkeys, outs[n_out:])
    return (outs[0] if single else list(outs[:n_out])), res


def _run_comm(carried, name):
    carried = list(carried)
    akeys, fkeys, arrays, shapes, sems = _carried_layout(carried)
    rkeys, reads = _carried_reads(carried)
    n_a, n_f, n_r = len(akeys), len(fkeys), len(rkeys)

    def body(*refs):
        o0 = n_a + n_r
        per = _carried_results(carried, akeys, fkeys, refs[o0:o0 + n_a + n_f], rkeys, refs[n_a:o0])
        sem = refs[o0 + n_a + n_f:]
        for hook in ("start", "mid", "finish"):
            for ci, cm in enumerate(carried):
                fn = getattr(cm, hook)
                if fn is not None:
                    fn(per[ci], sem[2 * ci], sem[2 * ci + 1])

    outs = pl.pallas_call(
        body, name=name, in_specs=[ANY] * (n_a + n_r), out_specs=[ANY] * (n_a + n_f), out_shape=shapes,
        input_output_aliases={i: i for i in range(n_a)}, scratch_shapes=sems,
    )(*arrays, *reads)
    return _carried_results(carried, akeys, fkeys, outs)


HBM_SPEC = pl.BlockSpec(memory_space=pltpu.HBM)
SEM_SPEC = pl.BlockSpec(memory_space=pltpu.SEMAPHORE)
SPLIT_EFFECT = pltpu.SideEffectType.DATAFLOW_SIDE_EFFECTING


def _in_hbm(a):
    return pltpu.with_memory_space_constraint(a, pltpu.HBM)


def _split_start(items, name, after=None):
    items = list(items)
    akeys, fkeys, arrays, shapes, sems = _carried_layout(items)
    assert not fkeys
    n_a, n_s = len(akeys), len(sems)
    n_in = n_a + (after is not None)

    def body(*refs):
        per = _carried_results(items, akeys, [], refs[n_in:n_in + n_a])
        sem = refs[n_in + n_a:n_in + n_a + n_s]
        for ci, cm in enumerate(items):
            cm.start(per[ci], sem[2 * ci], sem[2 * ci + 1])
        token = refs[n_in + n_a + n_s]
        token[...] = jnp.zeros_like(token)

    outs = pl.pallas_call(
        body, name=name, in_specs=[HBM_SPEC] * n_a + [ANY] * (after is not None),
        out_specs=[HBM_SPEC] * n_a + [SEM_SPEC] * n_s + [pl.BlockSpec(memory_space=pltpu.VMEM)],
        out_shape=[pltpu.HBM(s.shape, s.dtype) for s in shapes] + sems + [jax.ShapeDtypeStruct(TOKEN_SHAPE, F32)],
        input_output_aliases={i: i for i in range(n_a)},
        compiler_params=pltpu.CompilerParams(has_side_effects=SPLIT_EFFECT),
    )(*[_in_hbm(a) for a in arrays], *([after] if after is not None else []))
    res = _carried_results(items, akeys, [], outs[:n_a])
    sem_out = outs[n_a:n_a + n_s]
    return res, [(sem_out[2 * ci], sem_out[2 * ci + 1]) for ci in range(len(items))], outs[-1]


def _split_wait(items, sems, after, name):
    items = list(items)
    after = list(after) if isinstance(after, (list, tuple)) else [after]
    akeys, fkeys, arrays, shapes, _ = _carried_layout(items)
    n_a, n_s = len(akeys), 2 * len(items)

    def body(*refs):
        per = _carried_results(items, akeys, [], refs[n_a + n_s + len(after):])
        sem = refs[n_a:n_a + n_s]
        for ci, cm in enumerate(items):
            cm.finish(per[ci], sem[2 * ci], sem[2 * ci + 1])

    outs = pl.pallas_call(
        body, name=name, in_specs=[HBM_SPEC] * n_a + [SEM_SPEC] * n_s + [ANY] * len(after),
        out_specs=[HBM_SPEC] * n_a, out_shape=[pltpu.HBM(s.shape, s.dtype) for s in shapes],
        input_output_aliases={i: i for i in range(n_a)},
        compiler_params=pltpu.CompilerParams(has_side_effects=SPLIT_EFFECT),
    )(*arrays, *[s for pair in sems for s in pair], *after)
    return _carried_results(items, akeys, [], outs)


def _mm(a, b, *, mode, out_dtype, name, res=None, carried=(), fused=None):
    if mode == "nn":
        (m, k), (k2, n) = a.shape, b.shape
    elif mode == "nt":
        (m, k), (n, k2) = a.shape, b.shape
    else:
        (k, m), (k2, n) = a.shape, b.shape
    assert k == k2, (name, a.shape, b.shape)
    tm = _div(m, MM_TILE_CAP, LANE)
    tn = _div(n, MM_TILE_CAP, MXU_WIDTH) if n % MXU_WIDTH == 0 else 0
    if tn < 1024:
        tn = _div(n, MM_TILE_CAP, LANE)
    out_shape = jax.ShapeDtypeStruct((m, n), out_dtype)

    if fused is not None:
        assert mode in ("nn", "nt") and res is None and k <= 2048
        tm, tn = fused["tile"]
        dot = _dot if mode == "nn" else _dot_nt
        n_x = len(fused["ins"])

        summed = [len(e) > 2 for e in fused["outs"]]

        def kern_fused(*refs):
            part = dot(refs[0][...], refs[1][...])
            outs = fused["post"](part, *[r[...] for r in refs[2:2 + n_x]])
            first_row_tile = pl.program_id(1) == 0
            for o_ref, val, acc in zip(refs[2 + n_x:], outs, summed):
                if not acc:
                    o_ref[...] = val.astype(o_ref.dtype)
                    continue

                @pl.when(first_row_tile)
                def _():
                    o_ref[...] = val

                @pl.when(jnp.logical_not(first_row_tile))
                def _():
                    o_ref[...] += val

        def tile_spec(shape, index=lambda i, j: (i, j)):
            return pl.BlockSpec(shape, lambda j, i: index(i, j))

        return _pcall(
            kern_fused, name=name, grid=(n // tn, m // tm),
            in_specs=[pl.BlockSpec((tm, k), lambda j, i: (i, 0)),
                      pl.BlockSpec((k, tn), lambda j, i: (0, j)) if mode == "nn"
                      else pl.BlockSpec((tn, k), lambda j, i: (j, 0))] + [tile_spec(*e[1:]) for e in fused["ins"]],
            out_specs=[tile_spec(*e[1:]) for e in fused["outs"]], out_shape=[e[0] for e in fused["outs"]],
            args=(a, b, *[e[0] for e in fused["ins"]]), semantics=("parallel", "arbitrary"), carried=carried)

    if mode == "tn":
        assert res is None

        def kern_tn(a_ref, b_ref, o_ref, at_ref):
            @pl.when(pl.program_id(1) == 0)
            def _():
                step = min(TRANSPOSE_CHUNK, k)
                for c0 in range(0, k, step):
                    at_ref[:, c0:c0 + step] = a_ref[c0:c0 + step, :].astype(F32).T.astype(BF16)

            o_ref[...] = jnp.dot(at_ref[...], b_ref[...].astype(BF16),
                                 preferred_element_type=F32).astype(o_ref.dtype)

        return _pcall(
            kern_tn, name=name, grid=(m // tm, n // tn),
            in_specs=[pl.BlockSpec((k, tm), lambda i, j: (0, i)),
                      pl.BlockSpec((k, tn), lambda i, j: (0, j))],
            out_specs=pl.BlockSpec((tm, tn), lambda i, j: (i, j)),
            out_shape=out_shape, args=(a, b),
            scratch_shapes=[pltpu.VMEM((tm, k), BF16)],
            semantics=("parallel", "arbitrary"), carried=carried)

    tk = k if k <= 2048 else _div(k, 3072, LANE)
    nk = k // tk
    a_spec = pl.BlockSpec((tm, tk), lambda i, j, kk: (i, kk))
    if mode == "nn":
        b_spec = pl.BlockSpec((tk, tn), lambda i, j, kk: (kk, j))
        dot = _dot
    else:
        b_spec = pl.BlockSpec((tn, tk), lambda i, j, kk: (j, kk))
        dot = _dot_nt
    o_spec = pl.BlockSpec((tm, tn), lambda i, j, kk: (i, j))
    in_specs = [a_spec, b_spec]
    args = [a, b]
    if res is not None:
        in_specs.append(o_spec)
        args.append(res)
    has_res = res is not None

    def kern(*refs):
        a_ref, b_ref = refs[0], refs[1]
        r_ref = refs[2] if has_res else None
        o_ref = refs[3] if has_res else refs[2]
        part = dot(a_ref[...], b_ref[...])
        if nk == 1:
            if has_res:
                part = part + r_ref[...]
            o_ref[...] = part.astype(o_ref.dtype)
            return
        acc_ref = refs[-1]
        kk = pl.program_id(2)

        @pl.when(kk == 0)
        def _():
            acc_ref[...] = part

        @pl.when(kk > 0)
        def _():
            acc_ref[...] += part

        @pl.when(kk == nk - 1)
        def _():
            tot = acc_ref[...]
            if has_res:
                tot = tot + r_ref[...]
            o_ref[...] = tot.astype(o_ref.dtype)

    return _pcall(
        kern, name=name, grid=(m // tm, n // tn, nk),
        in_specs=in_specs, out_specs=o_spec, out_shape=out_shape, args=args,
        scratch_shapes=[pltpu.VMEM((tm, tn), F32)] if nk > 1 else [],
        semantics=("parallel", "parallel", "arbitrary"), carried=carried)


ROW_BLOCK = 512


def _rms_fwd(x, g, name):
    t, d = x.shape
    tr = min(ROW_BLOCK, t)

    def kern(x_ref, g_ref, o_ref):
        xf = x_ref[...]
        r = lax.rsqrt(jnp.mean(xf * xf, axis=-1, keepdims=True) + RMS_EPS)
        o_ref[...] = (xf * r * g_ref[...]).astype(o_ref.dtype)

    return pl.pallas_call(
        kern, name=name, grid=(t // tr,),
        in_specs=[pl.BlockSpec((tr, d), lambda i: (i, 0)), pl.BlockSpec((1, d), lambda i: (0, 0))],
        out_specs=pl.BlockSpec((tr, d), lambda i: (i, 0)),
        out_shape=jax.ShapeDtypeStruct((t, d), BF16),
        compiler_params=_cparams(("parallel",)),
    )(x, g)


def _rms_bwd(x, g, dh, res, name, bf16_copy=False):
    t, d = x.shape
    tr = min(ROW_BLOCK, t)
    has_res = res is not None

    def kern(*refs):
        x_ref, g_ref, dh_ref = refs[:3]
        r_ref = refs[3] if has_res else None
        dx_ref, dg_ref = refs[3 + has_res], refs[4 + has_res]
        xf = x_ref[...]
        r = lax.rsqrt(jnp.mean(xf * xf, axis=-1, keepdims=True) + RMS_EPS)
        xn = xf * r
        dh_ = dh_ref[...]
        dhg = dh_ * g_ref[...]
        dx = r * (dhg - xn * jnp.mean(dhg * xn, axis=-1, keepdims=True))
        if has_res:
            dx = dx + r_ref[...]
        dx_ref[...] = dx
        if bf16_copy:
            refs[-1][...] = dx.astype(BF16)
        part = jnp.sum(dh_ * xn, axis=0, keepdims=True)

        @pl.when(pl.program_id(0) == 0)
        def _():
            dg_ref[...] = part

        @pl.when(pl.program_id(0) > 0)
        def _():
            dg_ref[...] += part

    row = pl.BlockSpec((tr, d), lambda i: (i, 0))
    vec = pl.BlockSpec((1, d), lambda i: (0, 0))
    in_specs = [row, vec, row] + ([row] if has_res else [])
    args = [x, g, dh] + ([res] if has_res else [])
    return pl.pallas_call(
        kern, name=name, grid=(t // tr,), in_specs=in_specs, out_specs=[row, vec] + [row] * bf16_copy,
        out_shape=[jax.ShapeDtypeStruct((t, d), F32), jax.ShapeDtypeStruct((1, d), F32)]
        + [jax.ShapeDtypeStruct((t, d), BF16)] * bf16_copy,
        compiler_params=_cparams(("arbitrary",)),
    )(*args)


def _loss_head(x3, g, target, name):
    t, d = x3.shape
    tr = ROW_BLOCK

    def kern(x_ref, g_ref, t_ref, dx_ref, dg_ref, loss_ref, dxb_ref):
        xf = x_ref[...]
        r = lax.rsqrt(jnp.mean(xf * xf, axis=-1, keepdims=True) + RMS_EPS)
        xn = xf * r
        gg = g_ref[...]
        err = xn * gg - t_ref[...]
        lpart = 0.5 * jnp.sum(jnp.mean(err * err, axis=-1, keepdims=True), axis=0, keepdims=True)
        dy = err * (1.0 / d)
        dyg = dy * gg
        dx = r * (dyg - xn * jnp.mean(dyg * xn, axis=-1, keepdims=True))
        dx_ref[...] = dx
        dxb_ref[...] = dx.astype(BF16)
        gpart = jnp.sum(dy * xn, axis=0, keepdims=True)
        lrow = jnp.broadcast_to(lpart, (1, LANE))

        @pl.when(pl.program_id(0) == 0)
        def _():
            dg_ref[...] = gpart
            loss_ref[...] = lrow

        @pl.when(pl.program_id(0) > 0)
        def _():
            dg_ref[...] += gpart
            loss_ref[...] += lrow

    row = pl.BlockSpec((tr, d), lambda i: (i, 0))
    vec = pl.BlockSpec((1, d), lambda i: (0, 0))
    return pl.pallas_call(
        kern, name=name, grid=(t // tr,), in_specs=[row, vec, row],
        out_specs=[row, vec, pl.BlockSpec((1, LANE), lambda i: (0, 0)), row],
        out_shape=[jax.ShapeDtypeStruct((t, d), F32), jax.ShapeDtypeStruct((1, d), F32),
                   jax.ShapeDtypeStruct((1, LANE), F32), jax.ShapeDtypeStruct((t, d), BF16)],
        compiler_params=_cparams(("arbitrary",)),
    )(x3, g, target)


ATT_SCALE = HEAD_DIM ** -0.5
Q_BLOCK0, K_BLOCK0, V_BLOCK0 = 0, ATT_WIDTH // HEAD_DIM, 2 * ATT_WIDTH // HEAD_DIM


def _residue_rows(dil, r, n):
    if dil == 1:
        return pl.ds(n * ATT_BLOCK, ATT_BLOCK)
    return pl.ds(n * ATT_BLOCK * dil + r, ATT_BLOCK, stride=dil)


def _band_mask(with_prev):
    width = 2 * ATT_BLOCK if with_prev else ATT_BLOCK
    iq = lax.broadcasted_iota(jnp.int32, (ATT_BLOCK, width), 0)
    ik = lax.broadcasted_iota(jnp.int32, (ATT_BLOCK, width), 1)
    if not with_prev:
        return ik <= iq
    return ((ik < ATT_BLOCK) & (iq <= ik)) | ((ik >= ATT_BLOCK) & ((ik - ATT_BLOCK) <= iq))


def _band_keys(ref, dil, r, n):
    own = ref[_residue_rows(dil, r, n), :]
    if n == 0:
        return own
    return jnp.concatenate([ref[_residue_rows(dil, r, n - 1), :], own], axis=0)


def _attn_col_spec(base, grp):
    return pl.BlockSpec((SEQ, HEAD_DIM), lambda h: (0, base + grp * ATT_HEADS + h))


def _attn_fwd(proj, grp, name, carried=()):
    _, dil = ATT_GROUPS[grp]
    nb = SEQ // dil // ATT_BLOCK

    def kern(q_ref, k_ref, v_ref, o_ref, lse_ref):
        for r in range(dil):
            for n in range(nb):
                rows = _residue_rows(dil, r, n)
                s = _dot_nt(q_ref[rows, :], _band_keys(k_ref, dil, r, n)) * ATT_SCALE
                s = jnp.where(_band_mask(n > 0), s, -jnp.inf)
                m = jnp.max(s, axis=-1, keepdims=True)
                p = jnp.exp(s - m)
                l = jnp.sum(p, axis=-1, keepdims=True)
                o_ref[rows, :] = _dot(p / l, _band_keys(v_ref, dil, r, n))
                lse_ref[rows, :] = jnp.broadcast_to(m + jnp.log(l), (ATT_BLOCK, HEAD_DIM))

    out_spec = pl.BlockSpec((SEQ, HEAD_DIM), lambda h: (0, h))
    return _pcall(
        kern, name=name, grid=(ATT_HEADS,),
        in_specs=[_attn_col_spec(Q_BLOCK0, grp), _attn_col_spec(K_BLOCK0, grp), _attn_col_spec(V_BLOCK0, grp)],
        out_specs=[out_spec, out_spec],
        out_shape=[jax.ShapeDtypeStruct((SEQ, ATT_OUT), F32)] * 2, args=(proj, proj, proj),
        semantics=("parallel",), carried=carried)


def _attn_weights(l0, l1, l2):
    mx = jnp.maximum(jnp.maximum(l0, l1), l2)
    e0, e1, e2 = jnp.exp(l0 - mx), jnp.exp(l1 - mx), jnp.exp(l2 - mx)
    den = e0 + e1 + e2
    return e0 / den, e1 / den, e2 / den


def _attn_merge_fwd(outs, lses, name):
    tr = ROW_BLOCK

    def kern(o0, o1, o2, l0, l1, l2, out_ref):
        a0, a1, a2 = _attn_weights(l0[...], l1[...], l2[...])
        out_ref[...] = (a0 * o0[...] + a1 * o1[...] + a2 * o2[...]).astype(out_ref.dtype)

    spec = pl.BlockSpec((tr, ATT_OUT), lambda i: (i, 0))
    return pl.pallas_call(
        kern, name=name, grid=(SEQ // tr,), in_specs=[spec] * 6, out_specs=spec,
        out_shape=jax.ShapeDtypeStruct((SEQ, ATT_OUT), BF16),
        compiler_params=_cparams(("parallel",)),
    )(*outs, *lses)


def _attn_merge_bwd(outs, lses, do_att, name, carried=()):
    tr = ROW_BLOCK

    def kern(o0, o1, o2, l0, l1, l2, do_ref, d0, d1, d2, t0, t1, t2):
        alphas = _attn_weights(l0[...], l1[...], l2[...])
        do = do_ref[...]
        o_att = alphas[0] * o0[...] + alphas[1] * o1[...] + alphas[2] * o2[...]
        prod = do * o_att
        parts = []
        for h in range(ATT_HEADS):
            sl = slice(h * HEAD_DIM, (h + 1) * HEAD_DIM)
            tot = jnp.sum(prod[:, sl], axis=-1, keepdims=True)
            parts.append(jnp.broadcast_to(tot, (tr, HEAD_DIM)))
        dd = jnp.concatenate(parts, axis=1)
        for a, d_ref, t_ref in zip(alphas, (d0, d1, d2), (t0, t1, t2)):
            d_ref[...] = a * do
            t_ref[...] = -a * dd

    spec = pl.BlockSpec((tr, ATT_OUT), lambda i: (i, 0))
    res, cres = _pcall(
        kern, name=name, grid=(SEQ // tr,), in_specs=[spec] * 7, out_specs=[spec] * 6,
        out_shape=[jax.ShapeDtypeStruct((SEQ, ATT_OUT), F32)] * 6, args=(*outs, *lses, do_att),
        semantics=("parallel",), carried=carried)
    return (res[:3], res[3:]), cres


def _attn_bwd(proj, grp, lse, do_g, dl_g, name, carried=()):
    _, dil = ATT_GROUPS[grp]
    nb = SEQ // dil // ATT_BLOCK

    def kern(q_ref, k_ref, v_ref, do_ref, lse_ref, dl_ref, dq_ref, dk_ref, dv_ref, dq_acc, dk_acc, dv_acc):
        dk_acc[...] = jnp.zeros_like(dk_acc)
        dv_acc[...] = jnp.zeros_like(dv_acc)
        for r in range(dil):
            for n in range(nb):
                rows = _residue_rows(dil, r, n)
                q, do = q_ref[rows, :], do_ref[rows, :]
                kk, vv = _band_keys(k_ref, dil, r, n), _band_keys(v_ref, dil, r, n)
                s = _dot_nt(q, kk) * ATT_SCALE
                p = jnp.where(_band_mask(n > 0), jnp.exp(s - lse_ref[rows, :][:, :1]), 0.0)
                ds = p * (_dot_nt(do, vv) + dl_ref[rows, :][:, :1])
                dq_acc[rows, :] = _dot(ds, kk) * ATT_SCALE
                dk = _dot_tn(ds, q) * ATT_SCALE
                dv = _dot_tn(p, do)
                if n > 0:
                    prev = _residue_rows(dil, r, n - 1)
                    dk_acc[prev, :] += dk[:ATT_BLOCK]
                    dv_acc[prev, :] += dv[:ATT_BLOCK]
                    dk, dv = dk[ATT_BLOCK:], dv[ATT_BLOCK:]
                dk_acc[rows, :] += dk
                dv_acc[rows, :] += dv
        dq_ref[...] = dq_acc[...].astype(dq_ref.dtype)
        dk_ref[...] = dk_acc[...].astype(dk_ref.dtype)
        dv_ref[...] = dv_acc[...].astype(dv_ref.dtype)

    spec = pl.BlockSpec((SEQ, HEAD_DIM), lambda h: (0, h))
    return _pcall(
        kern, name=name, grid=(ATT_HEADS,),
        in_specs=[_attn_col_spec(Q_BLOCK0, grp), _attn_col_spec(K_BLOCK0, grp), _attn_col_spec(V_BLOCK0, grp),
                  spec, spec, spec],
        out_specs=[spec] * 3,
        out_shape=[jax.ShapeDtypeStruct((SEQ, ATT_OUT), BF16)] * 3, args=(proj, proj, proj, do_g, lse, dl_g),
        scratch_shapes=[pltpu.VMEM((SEQ, HEAD_DIM), F32)] * 3,
        semantics=("parallel",), carried=carried)


HG_HEADS_PER_STEP = 8
HG_BLOCK_W = 4 * HEAD_DIM
HG_BLOCKS = HG_HEADS_PER_STEP * HEAD_DIM // HG_BLOCK_W
HG_STEP_W = HG_HEADS_PER_STEP * HEAD_DIM
HG_Q_BLK = (3 * ATT_WIDTH) // HG_BLOCK_W
HG_N_CHUNKS = SEQ // HG_CHUNK
HG_MID = HG_CHUNK // 2


def _lower_bound(lb_ref, sl):
    l0, l1 = lb_ref[0:1, sl], lb_ref[1:2, sl]
    mx = jnp.maximum(l0, l1)
    e0, e1 = jnp.exp(l0 - mx), jnp.exp(l1 - mx)
    return e0 / (e0 + e1)


def _tri(lower):
    i = lax.broadcasted_iota(jnp.int32, (HG_CHUNK, HG_CHUNK), 0)
    j = lax.broadcasted_iota(jnp.int32, (HG_CHUNK, HG_CHUNK), 1)
    return (i >= j) if lower else (i <= j)


def _head_mean(x):
    parts = []
    for hd in range(x.shape[1] // HEAD_DIM):
        m = jnp.mean(x[:, hd * HEAD_DIM:(hd + 1) * HEAD_DIM], axis=-1, keepdims=True)
        parts.append(jnp.broadcast_to(m, (x.shape[0], HEAD_DIM)))
    return jnp.concatenate(parts, axis=1)


def _hg_chunk_terms(qh, fh, lb):
    sig = _sigmoid(fh)
    f = lb + (1.0 - lb) * sig
    k = 1.0 - f
    b = _dot_exact(_tri(True).astype(F32), jnp.log(f))
    bl = b[HG_CHUNK - 1:HG_CHUNK, :]
    br = b[HG_MID:HG_MID + 1, :]
    sq = _sigmoid(qh)
    q = qh * sq
    return dict(sig=sig, f=f, k=k, b=b, bl=bl, br=br, sq=sq, q=q,
                e1=jnp.exp(bl - b), e2=jnp.exp(b), e3=jnp.exp(b - br), e4=jnp.exp(br - b))


def _hg_fwd(proj, lbw, normw, name, carried=()):
    def in_blks(off):
        return [pl.BlockSpec((HG_CHUNK, HG_BLOCK_W), lambda hp, n, b=b: (n, HG_Q_BLK + off + hp * HG_BLOCKS + b))
                for b in range(HG_BLOCKS)]

    def kern(*refs):
        q_refs, f_refs, i_refs, g_refs = (refs[k * HG_BLOCKS:(k + 1) * HG_BLOCKS] for k in range(4))
        lb_ref, nw_ref, oraw_ref, ohg_ref, st_ref, state = refs[4 * HG_BLOCKS:]

        @pl.when(pl.program_id(1) == 0)
        def _():
            state[...] = jnp.zeros_like(state)

        causal = _tri(True)
        wide = lambda rs: jnp.concatenate([r[...] for r in rs], axis=1)
        t = _hg_chunk_terms(wide(q_refs), wide(f_refs), _lower_bound(lb_ref, slice(None)))
        v, gh = wide(i_refs), wide(g_refs)
        kd, qb, qr, kr = t["k"] * t["e1"], t["q"] * t["e2"], t["q"] * t["e3"], t["k"] * t["e4"]
        decay = jnp.exp(t["bl"])
        outs = []
        for hd in range(HG_HEADS_PER_STEP):
            sl = slice(hd * HEAD_DIM, (hd + 1) * HEAD_DIM)
            st = state[hd]
            st_ref[0, hd] = st
            a = jnp.where(causal, _dot_nt(qr[:, sl], kr[:, sl]), 0.0)
            outs.append(_dot_nt(qb[:, sl], st) + _dot(a, v[:, sl]))
            state[hd] = st * decay[:, sl] + _dot_tn(v[:, sl], kd[:, sl])
        o = jnp.concatenate(outs, axis=1)
        oraw_ref[...] = o
        r = lax.rsqrt(_head_mean(o * o) + RMS_EPS)
        nw = jnp.tile(nw_ref[...], (1, HG_HEADS_PER_STEP))
        ohg_ref[...] = (o * r * nw * (gh * _sigmoid(gh))).astype(ohg_ref.dtype)

    out_blk = pl.BlockSpec((HG_CHUNK, HG_STEP_W), lambda hp, n: (n, hp))
    return _pcall(
        kern, name=name, grid=(HG_HEADS // HG_HEADS_PER_STEP, HG_N_CHUNKS),
        in_specs=[*in_blks(0), *in_blks(2), *in_blks(4), *in_blks(6),
                  pl.BlockSpec((2, HG_STEP_W), lambda hp, n: (0, hp)),
                  pl.BlockSpec((1, HEAD_DIM), lambda hp, n: (0, 0))],
        out_specs=[out_blk, out_blk,
                   pl.BlockSpec((1, HG_HEADS_PER_STEP, HEAD_DIM, HEAD_DIM), lambda hp, n: (n, hp, 0, 0))],
        out_shape=[jax.ShapeDtypeStruct((SEQ, HG_WIDTH), F32), jax.ShapeDtypeStruct((SEQ, HG_WIDTH), BF16),
                   jax.ShapeDtypeStruct((HG_N_CHUNKS, HG_HEADS, HEAD_DIM, HEAD_DIM), F32)],
        args=(*[proj] * (4 * HG_BLOCKS), lbw, normw),
        scratch_shapes=[pltpu.VMEM((HG_HEADS_PER_STEP, HEAD_DIM, HEAD_DIM), F32)],
        semantics=("parallel", "arbitrary"), carried=carried)


def _hg_bwd(proj, lbw, normw, oraw, states, do_hg, name, carried=()):
    last = HG_N_CHUNKS - 1

    def in_blks(off):
        return [pl.BlockSpec((HG_CHUNK, HG_BLOCK_W),
                             lambda hp, n, b=b: (last - n, HG_Q_BLK + off + hp * HG_BLOCKS + b))
                for b in range(HG_BLOCKS)]

    blk = pl.BlockSpec((HG_CHUNK, HG_STEP_W), lambda hp, n: (last - n, hp))

    def kern(*refs):
        q_refs, f_refs, i_refs, g_refs = (refs[k * HG_BLOCKS:(k + 1) * HG_BLOCKS] for k in range(4))
        (lb_ref, nw_ref, oraw_ref, st_ref, do_ref, dq_ref, df_ref, di_ref, dg_ref, dlb_ref, dnw_ref,
         dstate) = refs[4 * HG_BLOCKS:]
        first = pl.program_id(1) == 0

        @pl.when(first)
        def _():
            dstate[...] = jnp.zeros_like(dstate)

        causal = _tri(True)
        wide = lambda rs: jnp.concatenate([r[...] for r in rs], axis=1)
        cat = lambda parts: jnp.concatenate(parts, axis=1)
        qh, fh, v, gh = wide(q_refs), wide(f_refs), wide(i_refs), wide(g_refs)
        o, dout = oraw_ref[...], do_ref[...]
        nw = jnp.tile(nw_ref[...], (1, HG_HEADS_PER_STEP))
        sgg = _sigmoid(gh)
        r = lax.rsqrt(_head_mean(o * o) + RMS_EPS)
        xn = o * r
        dg_ref[...] = (dout * xn * nw * (sgg * (1.0 + gh * (1.0 - sgg)))).astype(dg_ref.dtype)
        don = dout * (gh * sgg)
        dnw_wide = jnp.sum(don * xn, axis=0, keepdims=True)
        dnw_tot = dnw_wide[:, :HEAD_DIM]
        for hd in range(1, HG_HEADS_PER_STEP):
            dnw_tot = dnw_tot + dnw_wide[:, hd * HEAD_DIM:(hd + 1) * HEAD_DIM]
        tt = don * nw
        do = r * (tt - xn * _head_mean(tt * xn))
        lb = _lower_bound(lb_ref, slice(None))
        t = _hg_chunk_terms(qh, fh, lb)
        k, q = t["k"], t["q"]
        kd, qb, qr, kr = k * t["e1"], q * t["e2"], q * t["e3"], k * t["e4"]
        decay = jnp.exp(t["bl"])
        dqb, dqr, dkr, dkd, dv, ddecay = [], [], [], [], [], []
        for hd in range(HG_HEADS_PER_STEP):
            sl = slice(hd * HEAD_DIM, (hd + 1) * HEAD_DIM)
            st = st_ref[0, hd]
            dstn = dstate[hd]
            a = jnp.where(causal, _dot_nt(qr[:, sl], kr[:, sl]), 0.0)
            da = jnp.where(causal, _dot_nt(do[:, sl], v[:, sl]), 0.0)
            dqb.append(_dot(do[:, sl], st))
            dv.append(_dot_tn(a, do[:, sl]) + _dot_nt(kd[:, sl], dstn))
            dqr.append(_dot(da, kr[:, sl]))
            dkr.append(_dot_tn(da, qr[:, sl]))
            dkd.append(_dot(v[:, sl], dstn))
            ddecay.append(jnp.sum(dstn * st, axis=0, keepdims=True))
            dstate[hd] = dstn * decay[:, sl] + _dot_tn(do[:, sl], qb[:, sl])
        dqb, dqr, dkr, dkd, dv, ddecay = cat(dqb), cat(dqr), cat(dkr), cat(dkd), cat(dv), cat(ddecay)
        dq = dqb * t["e2"] + dqr * t["e3"]
        dk = dkd * t["e1"] + dkr * t["e4"]
        db = dqb * qb + dqr * qr - dkr * kr - dkd * kd
        dbl = jnp.sum(dkd * kd, axis=0, keepdims=True) + ddecay * decay
        dbr = jnp.sum(dkr * kr - dqr * qr, axis=0, keepdims=True)
        rows = lax.broadcasted_iota(jnp.int32, db.shape, 0)
        dlf = _dot_exact(_tri(False).astype(F32), db) + dbl + jnp.where(rows <= HG_MID, dbr, 0.0)
        df = dlf / t["f"] - dk
        sig, sq = t["sig"], t["sq"]
        df_ref[...] = (df * (1.0 - lb) * sig * (1.0 - sig)).astype(df_ref.dtype)
        dlb_row = jnp.sum(df * (1.0 - sig), axis=0, keepdims=True)
        dq_ref[...] = (dq * (sq * (1.0 + qh * (1.0 - sq)))).astype(dq_ref.dtype)
        di_ref[...] = dv.astype(di_ref.dtype)
        dnw_blk = jnp.broadcast_to(dnw_tot, (8, HEAD_DIM))

        @pl.when(first)
        def _():
            dlb_ref[...] = dlb_row
            dnw_ref[...] = dnw_blk

        @pl.when(jnp.logical_not(first))
        def _():
            dlb_ref[...] += dlb_row
            dnw_ref[...] += dnw_blk

    n_hp = HG_HEADS // HG_HEADS_PER_STEP
    outs, cres = _pcall(
        kern, name=name, grid=(n_hp, HG_N_CHUNKS),
        in_specs=[*in_blks(0), *in_blks(2), *in_blks(4), *in_blks(6),
                  pl.BlockSpec((2, HG_STEP_W), lambda hp, n: (0, hp)),
                  pl.BlockSpec((1, HEAD_DIM), lambda hp, n: (0, 0)),
                  blk,
                  pl.BlockSpec((1, HG_HEADS_PER_STEP, HEAD_DIM, HEAD_DIM), lambda hp, n: (last - n, hp, 0, 0)),
                  blk],
        out_specs=[blk, blk, blk, blk,
                   pl.BlockSpec((1, HG_STEP_W), lambda hp, n: (0, hp)),
                   pl.BlockSpec((8, HEAD_DIM), lambda hp, n: (hp, 0))],
        out_shape=[jax.ShapeDtypeStruct((SEQ, HG_WIDTH), BF16)] * 4
        + [jax.ShapeDtypeStruct((1, HG_WIDTH), F32), jax.ShapeDtypeStruct((8 * n_hp, HEAD_DIM), F32)],
        args=(*[proj] * (4 * HG_BLOCKS), lbw, normw, oraw, states, do_hg),
        scratch_shapes=[pltpu.VMEM((HG_HEADS_PER_STEP, HEAD_DIM, HEAD_DIM), F32)],
        semantics=("parallel", "arbitrary"), carried=carried)
    dqh, dfh, dih, dgh, dlb, dnw = outs
    return (dqh, dfh, dih, dgh, dlb, [dnw[8 * i:8 * i + 1] for i in range(n_hp)]), cres


GATE_BLOCK_W = 512
GATE_A_BLK = (3 * ATT_WIDTH + 4 * HG_WIDTH) // GATE_BLOCK_W
GATE_B_BLK = GATE_A_BLK + D_MODEL // GATE_BLOCK_W


GATE_TILE = (1024, GATE_BLOCK_W)


def _gate_ins(proj, ya, yb=None):
    ins = [(proj, GATE_TILE, lambda i, j: (i, GATE_A_BLK + j)), (proj, GATE_TILE, lambda i, j: (i, GATE_B_BLK + j)),
           (ya, GATE_TILE)]
    return ins + ([(yb, GATE_TILE)] if yb is not None else [])


def _branch_b_gate(o_hg, w_b, proj, ya, name, carried=()):
    def post(yb, ga, gb, ya_):
        return yb, _sigmoid(ga) * ya_ + _sigmoid(gb) * yb

    return _mm(o_hg, w_b, mode="nn", out_dtype=F32, name=name, carried=carried, fused=dict(
        tile=GATE_TILE, ins=_gate_ins(proj, ya), post=post,
        outs=[(jax.ShapeDtypeStruct((SEQ, D_MODEL), F32), GATE_TILE), (jax.ShapeDtypeStruct((SEQ, D_MODEL), BF16), GATE_TILE)]))


def _dmerged_gate_bwd(dx1, w_out, proj, ya, yb, name, carried=()):
    def post(dm, ga, gb, ya_, yb_):
        sa, sb = _sigmoid(ga), _sigmoid(gb)
        return dm * sa, dm * sb, dm * ya_ * sa * (1.0 - sa), dm * yb_ * sb * (1.0 - sb)

    return _mm(dx1, w_out, mode="nt", out_dtype=BF16, name=name, carried=carried, fused=dict(
        tile=GATE_TILE, ins=_gate_ins(proj, ya, yb), post=post,
        outs=[(jax.ShapeDtypeStruct((SEQ, D_MODEL), BF16), GATE_TILE)] * 4))


NORM_TILE = (512, D_MODEL)


def _residual_norm(a, w, x, g, name, carried=()):
    def post(part, x_, g_):
        xn = part + x_
        r = lax.rsqrt(jnp.mean(xn * xn, axis=-1, keepdims=True) + RMS_EPS)
        return xn, xn * r * g_

    return _mm(a, w, mode="nn", out_dtype=F32, name=name, carried=carried, fused=dict(
        tile=NORM_TILE, ins=[(x, NORM_TILE), (g, (1, D_MODEL), lambda i, j: (0, j))], post=post,
        outs=[(jax.ShapeDtypeStruct((SEQ, D_MODEL), F32), NORM_TILE), (jax.ShapeDtypeStruct((SEQ, D_MODEL), BF16), NORM_TILE)]))


def _norm_bwd_residual(dq, w, x, g, res, name, carried=()):
    def post(dh, x_, g_, res_):
        r = lax.rsqrt(jnp.mean(x_ * x_, axis=-1, keepdims=True) + RMS_EPS)
        xn = x_ * r
        dhg = dh * g_
        dx = r * (dhg - xn * jnp.mean(dhg * xn, axis=-1, keepdims=True)) + res_
        return dx, jnp.sum(dh * xn, axis=0, keepdims=True), dx

    vec = ((1, D_MODEL), lambda i, j: (0, j))
    return _mm(dq, w, mode="nt", out_dtype=F32, name=name, carried=carried, fused=dict(
        tile=NORM_TILE, ins=[(x, NORM_TILE), (g, *vec), (res, NORM_TILE)], post=post,
        outs=[(jax.ShapeDtypeStruct((SEQ, D_MODEL), F32), NORM_TILE), (jax.ShapeDtypeStruct((1, D_MODEL), F32), *vec),
              (jax.ShapeDtypeStruct((SEQ, D_MODEL), BF16), NORM_TILE)]))


FF_SHARD = D_FF // N_CHIPS


FF_TILE_ROWS = 512


def _swiglu_tile(ab):
    a, b = ab[:, :FF_SHARD], ab[:, FF_SHARD:]
    return a * _sigmoid(a) * b


def _swiglu_grad_tile(du, ab):
    a, b = ab[:, :FF_SHARD], ab[:, FF_SHARD:]
    sg = _sigmoid(a)
    return jnp.concatenate([du * b * (sg * (1.0 + a * (1.0 - sg))), du * (a * sg)], axis=1)


def _ff_up(hf, w13, name, carried=()):
    wide, narrow = (FF_TILE_ROWS, 2 * FF_SHARD), (FF_TILE_ROWS, FF_SHARD)
    return _mm(hf, w13, mode="nn", out_dtype=F32, name=name, carried=carried, fused=dict(
        tile=wide, ins=[],
        outs=[(jax.ShapeDtypeStruct((SEQ, 2 * D_FF), F32), wide), (jax.ShapeDtypeStruct((SEQ, D_FF), BF16), narrow)],
        post=lambda p: (p, _swiglu_tile(p))))


def _ff_down_bwd(dx3, w2, ab, name, carried=()):
    wide, narrow = (FF_TILE_ROWS, 2 * FF_SHARD), (FF_TILE_ROWS, FF_SHARD)
    out, res = _mm(dx3, w2, mode="nt", out_dtype=BF16, name=name, carried=carried, fused=dict(
        tile=narrow, ins=[(ab, wide)], outs=[(jax.ShapeDtypeStruct((SEQ, 2 * D_FF), BF16), wide)],
        post=lambda du, ab_: (_swiglu_grad_tile(du, ab_),)))
    return out[0], res


CROSS_ROWS = 512


def _cross_fwd(qc, kvc, name, carried=()):
    def kern(q_ref, k_ref, v_ref, o_ref):
        s = _dot_nt(q_ref[...], k_ref[...]) * ATT_SCALE
        m = jnp.max(s, axis=-1, keepdims=True)
        e = jnp.exp(s - m)
        p = e / jnp.sum(e, axis=-1, keepdims=True)
        o_ref[...] = _dot(p, v_ref[...]).astype(o_ref.dtype)

    qblk = pl.BlockSpec((CROSS_ROWS, HEAD_DIM), lambda h, i: (i, h))
    return _pcall(
        kern, name=name, grid=(CROSS_HEADS, SEQ // CROSS_ROWS),
        in_specs=[qblk, pl.BlockSpec((MEM_LEN, HEAD_DIM), lambda h, i: (0, h)),
                  pl.BlockSpec((MEM_LEN, HEAD_DIM), lambda h, i: (0, CROSS_HEADS + h))],
        out_specs=qblk, out_shape=jax.ShapeDtypeStruct((SEQ, CROSS_WIDTH), BF16), args=(qc, kvc, kvc),
        semantics=("parallel", "parallel"), carried=carried)


def _cross_bwd(qc, kvc, doc, name):
    def kern(q_ref, k_ref, v_ref, do_ref, dq_ref, dk_ref, dv_ref):
        q, k, v, do = q_ref[...], k_ref[...], v_ref[...], do_ref[...]
        s = _dot_nt(q, k) * ATT_SCALE
        m = jnp.max(s, axis=-1, keepdims=True)
        e = jnp.exp(s - m)
        p = e / jnp.sum(e, axis=-1, keepdims=True)
        dp = _dot_nt(do, v)
        ds = p * (dp - jnp.sum(dp * p, axis=-1, keepdims=True))
        dq_ref[...] = (_dot(ds, k) * ATT_SCALE).astype(dq_ref.dtype)
        dk = _dot_tn(ds, q) * ATT_SCALE
        dv = _dot_tn(p, do)

        @pl.when(pl.program_id(1) == 0)
        def _():
            dk_ref[...] = dk
            dv_ref[...] = dv

        @pl.when(pl.program_id(1) > 0)
        def _():
            dk_ref[...] += dk
            dv_ref[...] += dv

    qblk = pl.BlockSpec((CROSS_ROWS, HEAD_DIM), lambda h, i: (i, h))
    kblk = pl.BlockSpec((MEM_LEN, HEAD_DIM), lambda h, i: (0, h))
    dq, dk, dv = pl.pallas_call(
        kern, name=name, grid=(CROSS_HEADS, SEQ // CROSS_ROWS),
        in_specs=[qblk, kblk, pl.BlockSpec((MEM_LEN, HEAD_DIM), lambda h, i: (0, CROSS_HEADS + h)), qblk],
        out_specs=[qblk, kblk, kblk],
        out_shape=[jax.ShapeDtypeStruct((SEQ, CROSS_WIDTH), BF16),
                   jax.ShapeDtypeStruct((MEM_LEN, CROSS_WIDTH), F32),
                   jax.ShapeDtypeStruct((MEM_LEN, CROSS_WIDTH), F32)],
        compiler_params=_cparams(("parallel", "arbitrary")),
    )(qc, kvc, kvc, doc)
    return dq, jnp.concatenate([dk, dv], axis=1)


FULL_SPECS = {
    "w_in": ("col", D_MODEL, IN_WIDTH),
    "w_branch_a": ("col", ATT_OUT, D_MODEL),
    "w_branch_b": ("col", HG_WIDTH, D_MODEL),
    "w_out": ("row", D_MODEL, D_MODEL),
    "wq_cross": ("row", D_MODEL, CROSS_WIDTH),
    "wkv_cross": ("row", D_MODEL, 2 * CROSS_WIDTH),
    "wo_cross": ("col", CROSS_WIDTH, D_MODEL),
    "w13": ("col", D_MODEL, 2 * D_FF),
    "w2": ("row", D_FF, D_MODEL),
}
WEIGHT_PLACE = {
    "w_in": ("w_in", 0), "w_branch_a": ("w_branch_a", 0), "w_branch_b": ("w_branch_b", 0),
    "w_out": ("w_out", 0), "wq_cross": ("wq_cross", 0), "wkv_cross": ("wkv_cross", 0),
    "wo_cross": ("wo_cross", 0), "w1": ("w13", 0), "w3": ("w13", FF_SHARD), "w2": ("w2", 0),
}
BIG_WEIGHTS = tuple(WEIGHT_PLACE)
EW_BLOCK_ELEMS = 512 * 1024


def _position():
    return lax.axis_index("x"), lax.axis_index("y"), lax.axis_index("c")


def _other_chips(x, y):
    return [(1 - x, y), (x, 1 - y), (1 - x, 1 - y)]


def _half(ref, kind, h):
    r, c = ref.shape
    if kind == "col":
        return ref.at[pl.ds(h * (r // 2), r // 2), :]
    return ref.at[:, pl.ds(h * (c // 2), c // 2)]


def _shard_of(ref, kind, start, size):
    return ref.at[:, pl.ds(start, size)] if kind == "col" else ref.at[pl.ds(start, size), :]


def _rows_of(ref, r0, nrows):
    return ref if nrows is None else ref.at[pl.ds(r0, nrows), :]


def _half_shape(kind, rows, cols):
    return (rows // 2, cols) if kind == "col" else (rows, cols // 2)


def _slot_shape(spec):
    kind, rows, cols = spec
    hr, hc = _half_shape(kind, rows, cols)
    return (hr, hc // N_CHIPS) if kind == "col" else (hr // N_CHIPS, hc)


def _remote(src, dst, send_sem, recv_sem, device):
    return pltpu.make_async_remote_copy(src_ref=src, dst_ref=dst, send_sem=send_sem, recv_sem=recv_sem,
                                        device_id=device, device_id_type=MESH)


def _gather_ici_comm(fulls, jobs, specs):
    def piece(refs, job, chip, c):
        f, r0, nr = job
        kind, rows, cols = specs[f]
        stride = (cols if kind == "col" else rows) // N_CHIPS
        return _rows_of(_half(_shard_of(refs[f], kind, chip * stride, stride), kind, c), r0, nr)

    def start(refs, ss, rs):
        x, y, c = _position()
        j = 2 * x + y
        for q, job in enumerate(jobs):
            for p, (px, py) in enumerate(_other_chips(x, y)):
                _remote(piece(refs, job, j, c), piece(refs, job, j, c), ss.at[3 * q + p], rs.at[3 * q + p],
                        (px, py, c)).start()

    def finish(refs, ss, rs):
        x, y, c = _position()
        j = 2 * x + y
        for q, job in enumerate(jobs):
            for p, (px, py) in enumerate(_other_chips(x, y)):
                _remote(piece(refs, job, j, c), piece(refs, job, 2 * px + py, c), ss.at[3 * q + p],
                        rs.at[3 * q + p], (px, py, c)).wait_recv()
        for q, job in enumerate(jobs):
            for p, (px, py) in enumerate(_other_chips(x, y)):
                _remote(piece(refs, job, j, c), piece(refs, job, j, c), ss.at[3 * q + p], rs.at[3 * q + p],
                        (px, py, c)).wait_send()

    names = list(dict.fromkeys(job[0] for job in jobs))
    return _Carried({f: fulls[f] for f in names}, {}, 3 * len(jobs), start, finish)


def _gather_ring_comm(fulls, f, r0, nr, phase, specs):
    kind, _, cols = specs[f]
    assert kind == "col" and nr % 32 == 0
    stride = cols // N_CHIPS
    half = nr // 2

    def rows(refs, chip, c, lo, n):
        return _rows_of(_half(_shard_of(refs[f], kind, chip * stride, stride), kind, c), r0 + lo, n)

    def copies(refs, ss, rs):
        x, y, c = _position()
        me, nx, ny, dg = 2 * x + y, 2 * (1 - x) + y, 2 * x + (1 - y), 2 * (1 - x) + (1 - y)
        to_x, to_y = (1 - x, y, c), (x, 1 - y, c)
        if phase == "a":
            mine = rows(refs, me, c, 0, nr)
            return [(_remote(mine, mine, ss.at[0], rs.at[0], to_x), rows(refs, nx, c, 0, nr)),
                    (_remote(mine, mine, ss.at[1], rs.at[1], to_y), rows(refs, ny, c, 0, nr))]
        up, low = rows(refs, ny, c, half, half), rows(refs, nx, c, 0, half)
        return [(_remote(up, up, ss.at[0], rs.at[0], to_x), rows(refs, dg, c, half, half)),
                (_remote(low, low, ss.at[1], rs.at[1], to_y), rows(refs, dg, c, 0, half))]

    def start(refs, ss, rs):
        for cp, _ in copies(refs, ss, rs):
            cp.start()

    def finish(refs, ss, rs):
        x, y, c = _position()
        mine = copies(refs, ss, rs)
        for i, (_, landing) in enumerate(mine):
            _remote(landing, landing, ss.at[i], rs.at[i], (x, y, c)).wait_recv()
        for cp, _ in mine:
            cp.wait_send()

    return _Carried({f: fulls[f]}, {}, 2, start, finish)


def _gather_d2d_comm(fulls, jobs, specs):
    def rect(refs, job, h):
        f, r0, nr = job
        assert nr is None or specs[f][0] == "col"
        return _rows_of(_half(refs[f], specs[f][0], h), r0, nr)

    def start(refs, ss, rs):
        x, y, c = _position()
        for q, job in enumerate(jobs):
            _remote(rect(refs, job, c), rect(refs, job, c), ss.at[q], rs.at[q], (x, y, 1 - c)).start()

    def finish(refs, ss, rs):
        x, y, c = _position()
        for q, job in enumerate(jobs):
            _remote(rect(refs, job, 1 - c), rect(refs, job, 1 - c), ss.at[q], rs.at[q], (x, y, 1 - c)).wait_recv()
        for q, job in enumerate(jobs):
            _remote(rect(refs, job, c), rect(refs, job, c), ss.at[q], rs.at[q], (x, y, 1 - c)).wait_send()

    names = list(dict.fromkeys(job[0] for job in jobs))
    return _Carried({f: fulls[f] for f in names}, {}, len(jobs), start, finish)


def _pairx_comm(grads, names, specs, whole=False):
    def copies(refs, ss, rs):
        x, y, c = _position()
        src = (lambda f: refs[("g", f)]) if whole else (lambda f: _half(refs[("g", f)], specs[f][0], 1 - c))
        return [_remote(src(f), refs[("r", f)], ss.at[i], rs.at[i], (x, y, 1 - c)) for i, f in enumerate(names)]

    def start(refs, ss, rs):
        for cp in copies(refs, ss, rs):
            cp.start()

    def finish(refs, ss, rs):
        for cp in copies(refs, ss, rs):
            cp.wait_recv()
        for cp in copies(refs, ss, rs):
            cp.wait_send()

    fresh = {("r", f): jax.ShapeDtypeStruct(_half_shape(*specs[f]), BF16) for f in names}
    return _Carried({}, fresh, len(names), start, finish, reads={("g", f): grads[f] for f in names})


def _chipx_comm(pair_sums, slots, jobs, specs):
    def copies(refs, ss, rs):
        x, y, c = _position()
        out = []
        for q, (f, r0, nr) in enumerate(jobs):
            kind = specs[f][0]
            width = _slot_shape(specs[f])[1 if kind == "col" else 0]
            for p, (px, py) in enumerate(_other_chips(x, y)):
                src = _rows_of(_shard_of(refs[("p", f)], kind, (2 * px + py) * width, width), r0, nr)
                dst = _rows_of(refs[("s", f)].at[p], r0, nr)
                out.append(_remote(src, dst, ss.at[3 * q + p], rs.at[3 * q + p], (px, py, c)))
        return out

    def start(refs, ss, rs):
        for cp in copies(refs, ss, rs):
            cp.start()

    def finish(refs, ss, rs):
        for cp in copies(refs, ss, rs):
            cp.wait_recv()
        for cp in copies(refs, ss, rs):
            cp.wait_send()

    names = list(dict.fromkeys(job[0] for job in jobs))
    arrays = {("p", f): pair_sums[f] for f in names}
    arrays.update({("s", f): slots[f] for f in names})
    return _Carried(arrays, {}, 3 * len(jobs), start, finish)


def _share_comm(grads, wnames, specs, place):
    def start(refs, ss, rs):
        x, y, c = _position()
        for i, w in enumerate(wnames):
            kind = specs[place[w][0]][0]
            _remote(_half(refs[w], kind, c), _half(refs[w], kind, c), ss.at[i], rs.at[i], (x, y, 1 - c)).start()

    def finish(refs, ss, rs):
        x, y, c = _position()
        for i, w in enumerate(wnames):
            kind = specs[place[w][0]][0]
            _remote(_half(refs[w], kind, 1 - c), _half(refs[w], kind, 1 - c), ss.at[i], rs.at[i],
                    (x, y, 1 - c)).wait_recv()
        for i, w in enumerate(wnames):
            kind = specs[place[w][0]][0]
            _remote(_half(refs[w], kind, c), _half(refs[w], kind, c), ss.at[i], rs.at[i], (x, y, 1 - c)).wait_send()

    return _Carried({w: grads[w] for w in wnames}, {}, len(wnames), start, finish)


def _gather_rows(v, name="gather_small"):
    shape = v.shape

    def body(v_ref, out_ref, send_sem, recv_sem, loc_sem):
        x, y, c = _position()
        me = 4 * x + 2 * y + c
        flips = [(fx, fy, fc) for fx in (0, 1) for fy in (0, 1) for fc in (0, 1)][1:]

        def peer(fl):
            return tuple(1 - a if f else a for a, f in zip((x, y, c), fl))

        loc = pltpu.make_async_copy(v_ref, out_ref.at[me], loc_sem)
        loc.start()
        sends = []
        for i, fl in enumerate(flips):
            cp = _remote(v_ref, out_ref.at[me], send_sem.at[i], recv_sem.at[i], peer(fl))
            cp.start()
            sends.append(cp)
        for i, fl in enumerate(flips):
            px, py, pc = peer(fl)
            _remote(v_ref, out_ref.at[4 * px + 2 * py + pc], send_sem.at[i], recv_sem.at[i], peer(fl)).wait_recv()
        for cp in sends:
            cp.wait_send()
        loc.wait()

    return pl.pallas_call(
        body, name=name, in_specs=[ANY], out_specs=ANY,
        out_shape=jax.ShapeDtypeStruct((N_DEV,) + shape, F32),
        scratch_shapes=[pltpu.SemaphoreType.DMA((N_DEV - 1,)), pltpu.SemaphoreType.DMA((N_DEV - 1,)),
                        pltpu.SemaphoreType.DMA],
    )(v)


def _ew_block(rows, cols, elems=EW_BLOCK_ELEMS):
    tc = cols if cols <= 4096 else _div(cols, 2048, LANE)
    tr = _div(rows, max(16, elems // tc), 16)
    return tr, tc


def _mesh_scalars():
    x, y, c = _position()
    return jnp.stack([c, 2 * x + y]).astype(jnp.int32)


def _grid_spec(grid, in_specs, out_specs):
    return pltpu.PrefetchScalarGridSpec(num_scalar_prefetch=1, grid=grid, in_specs=in_specs, out_specs=out_specs)


def _cast_into_full(parts, fname, pos, specs, place, name, token=None):
    kind, rows, cols = specs[fname]
    ws = [w for w in place if place[w][0] == fname]
    if kind == "col":
        stride = cols // N_CHIPS
        hr = rows // 2
        tr = _div(hr, max(16, EW_BLOCK_ELEMS // stride), 16)
        nrb = hr // tr
        in_specs = [pl.BlockSpec((tr, parts[w].shape[1]), lambda i, pos_ref: (i + pos_ref[0] * nrb, 0)) for w in ws]
        out_spec = pl.BlockSpec((tr, stride), lambda i, pos_ref: (i + pos_ref[0] * nrb, pos_ref[1]))
    else:
        stride = rows // N_CHIPS
        hc = cols // 2
        tr = _div(stride, max(16, EW_BLOCK_ELEMS // hc), 16)
        nrb = stride // tr
        in_specs = [pl.BlockSpec((tr, hc), lambda i, pos_ref: (i, pos_ref[0])) for w in ws]
        out_spec = pl.BlockSpec((tr, hc), lambda i, pos_ref: (i + pos_ref[1] * nrb, pos_ref[0]))

    def kern(pos_ref, *refs):
        o_ref = refs[-1]
        for w, r in zip(ws, refs[:len(ws)]):
            off = place[w][1] if kind == "col" else 0
            o_ref[:, off:off + r.shape[1]] = r[...].astype(o_ref.dtype)

    tokens = [] if token is None else [token]
    in_specs = in_specs + [pl.BlockSpec(TOKEN_SHAPE, lambda i, pos_ref: (0, 0))] * len(tokens)
    return pl.pallas_call(
        kern, name=name, grid_spec=_grid_spec((nrb,), in_specs, out_spec),
        out_shape=jax.ShapeDtypeStruct((rows, cols), BF16),
        compiler_params=_cparams(("parallel",)),
    )(pos, *[parts[w] for w in ws], *tokens)


def _pair_sum(grad, recv, pos, spec, name, whole=False):
    kind, rows, cols = spec
    hr, hc = _half_shape(kind, rows, cols)
    tr, tc = _ew_block(hr, hc, 2 * EW_BLOCK_ELEMS)
    nrb, ncb = hr // tr, hc // tc
    blk = pl.BlockSpec((tr, tc), lambda i, jj, pos_ref: (i, jj))
    if whole:
        mine = blk
    elif kind == "col":
        mine = pl.BlockSpec((tr, tc), lambda i, jj, pos_ref: (i + pos_ref[0] * nrb, jj))
    else:
        mine = pl.BlockSpec((tr, tc), lambda i, jj, pos_ref: (i, jj + pos_ref[0] * ncb))

    def kern(pos_ref, g_ref, r_ref, o_ref, slots_ref):
        o_ref[...] = (g_ref[...].astype(F32) + r_ref[...].astype(F32)).astype(o_ref.dtype)

    return pl.pallas_call(
        kern, name=name, grid_spec=_grid_spec((nrb, ncb), [mine, blk], [blk, ANY]),
        out_shape=[jax.ShapeDtypeStruct((hr, hc), BF16),
                   jax.ShapeDtypeStruct((N_CHIPS - 1,) + _slot_shape(spec), BF16)],
        compiler_params=_cparams(("parallel", "parallel")),
    )(pos, grad, recv)


def _chip_sum(pair_sum, slots, pos, fname, shard_shapes, specs, place, name):
    kind, rows, cols = specs[fname]
    sr, sc = _slot_shape(specs[fname])
    ws = [w for w in place if place[w][0] == fname]
    n_slots = N_CHIPS - 1
    tr = _div(sr, max(16, EW_BLOCK_ELEMS // sc), 16)
    nrb = sr // tr
    slot = pl.BlockSpec((n_slots, tr, sc), lambda i, pos_ref: (0, i, 0))
    if kind == "col":
        own = pl.BlockSpec((tr, sc), lambda i, pos_ref: (i, pos_ref[1]))
        out_specs = [pl.BlockSpec((tr, shard_shapes[w][1]), lambda i, pos_ref: (i + pos_ref[0] * nrb, 0)) for w in ws]
    else:
        own = pl.BlockSpec((tr, sc), lambda i, pos_ref: (i + pos_ref[1] * nrb, 0))
        out_specs = [pl.BlockSpec((tr, sc), lambda i, pos_ref: (i, pos_ref[0])) for w in ws]

    def kern(pos_ref, own_ref, slot_ref, *out_refs):
        tot = own_ref[...].astype(F32)
        for s in range(n_slots):
            tot = tot + slot_ref[s].astype(F32)
        for w, o_ref in zip(ws, out_refs):
            off = place[w][1] if kind == "col" else 0
            o_ref[...] = tot[:, off:off + o_ref.shape[1]]

    outs = pl.pallas_call(
        kern, name=name, grid_spec=_grid_spec((nrb,), [own, slot], out_specs),
        out_shape=[jax.ShapeDtypeStruct(shard_shapes[w], F32) for w in ws],
        compiler_params=_cparams(("parallel",)),
    )(pos, pair_sum, slots)
    return dict(zip(ws, outs))


def _adam_math(w, g, m, v):
    m2 = ADAM_B1 * m + (1.0 - ADAM_B1) * g
    v2 = ADAM_B2 * v + (1.0 - ADAM_B2) * (g * g)
    m_hat = m2 / (1.0 - ADAM_B1 ** ADAM_STEP)
    v_hat = v2 / (1.0 - ADAM_B2 ** ADAM_STEP)
    delta = -ADAM_LR * (m_hat / (jnp.sqrt(v_hat) + ADAM_EPS) + ADAM_WD * w)
    return delta, m2, v2


def _adamw(w, g, m, v, name, carried=()):
    rows, cols = w.shape
    tr, tc = _ew_block(rows, cols)

    def kern(w_ref, g_ref, m_ref, v_ref, d_ref, m2_ref, v2_ref, g_out_ref):
        g_ = g_ref[...]
        d_ref[...], m2_ref[...], v2_ref[...] = _adam_math(w_ref[...], g_, m_ref[...], v_ref[...])
        g_out_ref[...] = g_

    blk = pl.BlockSpec((tr, tc), lambda i, j: (i, j))
    return _pcall(
        kern, name=name, grid=(rows // tr, cols // tc), in_specs=[blk] * 4, out_specs=[blk] * 4,
        out_shape=[jax.ShapeDtypeStruct((rows, cols), F32)] * 4, args=(w, g, m, v),
        semantics=("parallel", "parallel"), carried=carried)


SMALL_ROWS = ("ln_mix_w", "ln_cross_w", "ln_mem_w", "ln_ffn_w", "ln_final_w")
ROW_HG_NORM, ROW_LB0, ROW_LB1 = 5, 6, 7
LOSS_LANE0 = HEAD_DIM


def _pack_small(vals):
    rows = [vals[n].reshape(1, D_MODEL) for n in SMALL_ROWS]
    pad = lambda a: jnp.pad(a, ((0, 0), (0, D_MODEL - a.shape[1])))
    rows.append(pad(vals["hg_norm_w"].reshape(1, HEAD_DIM)))
    rows.append(pad(vals["hg_lower_bounds"].reshape(2, HG_WIDTH)))
    return jnp.concatenate(rows, axis=0)


def _small_update(gathered, w, m, v, name="small_update"):
    def kern(g_ref, w_ref, m_ref, v_ref, grad_ref, d_ref, m2_ref, v2_ref, loss_ref):
        tot = g_ref[0]
        for s in range(1, N_DEV):
            tot = tot + g_ref[s]
        wv = w_ref[...]
        row = lax.broadcasted_iota(jnp.int32, (8, D_MODEL), 0)
        lane = lax.broadcasted_iota(jnp.int32, (8, D_MODEL), 1)
        l0, l1 = wv[ROW_LB0:ROW_LB0 + 1], wv[ROW_LB1:ROW_LB1 + 1]
        mx = jnp.maximum(l0, l1)
        e0, e1 = jnp.exp(l0 - mx), jnp.exp(l1 - mx)
        p0 = e0 / (e0 + e1)
        dlog = tot[ROW_LB0:ROW_LB0 + 1] * p0 * (1.0 - p0)
        tot = jnp.where(row == ROW_HG_NORM, tot + tot[ROW_LB1:ROW_LB1 + 1], tot)
        grad = jnp.where(row == ROW_LB0, dlog, jnp.where(row == ROW_LB1, -dlog, tot))
        grad = jnp.where((row == ROW_HG_NORM) & (lane >= HEAD_DIM), 0.0, grad)
        grad = jnp.where((row >= ROW_LB0) & (lane >= HG_WIDTH), 0.0, grad)
        grad_ref[...] = grad
        d_ref[...], m2_ref[...], v2_ref[...] = _adam_math(wv, grad, m_ref[...], v_ref[...])
        loss_ref[...] = tot[ROW_HG_NORM:ROW_HG_NORM + 1, LOSS_LANE0:LOSS_LANE0 + LANE]

    full = pl.BlockSpec((8, D_MODEL), lambda: (0, 0))
    return pl.pallas_call(
        kern, name=name,
        in_specs=[pl.BlockSpec((N_DEV, 8, D_MODEL), lambda: (0, 0, 0)), full, full, full],
        out_specs=[full, full, full, full, pl.BlockSpec((1, LANE), lambda: (0, 0))],
        out_shape=[jax.ShapeDtypeStruct((8, D_MODEL), F32)] * 4 + [jax.ShapeDtypeStruct((1, LANE), F32)],
        compiler_params=_cparams(),
    )(gathered, w, m, v)


def _unpack_small(p, shapes):
    out = {n: p[i].reshape(shapes[n]) for i, n in enumerate(SMALL_ROWS)}
    out["hg_norm_w"] = p[ROW_HG_NORM, :HEAD_DIM].reshape(shapes["hg_norm_w"])
    out["hg_lower_bounds"] = p[ROW_LB0:ROW_LB1 + 1, :HG_WIDTH].reshape(shapes["hg_lower_bounds"])
    return out


def _concat_cols(pieces, name):
    rows = pieces[0].shape[0]
    widths = [p.shape[1] for p in pieces]
    tr = ROW_BLOCK // 2

    def kern(*refs):
        o_ref, off = refs[-1], 0
        for r, w in zip(refs[:-1], widths):
            o_ref[:, off:off + w] = r[...]
            off += w

    return pl.pallas_call(
        kern, name=name, grid=(rows // tr,),
        in_specs=[pl.BlockSpec((tr, w), lambda i: (i, 0)) for w in widths],
        out_specs=pl.BlockSpec((tr, sum(widths)), lambda i: (i, 0)),
        out_shape=jax.ShapeDtypeStruct((rows, sum(widths)), pieces[0].dtype),
        compiler_params=_cparams(("parallel",)),
    )(*pieces)


WHOLE = lambda f: (f, 0, None)
MID_MATRICES = ("w_branch_a", "w_branch_b", "w_out", "wq_cross", "wkv_cross", "wo_cross")
MID_WEIGHTS = MID_MATRICES
W_IN_PIECES = [("w_in", r0, 512) for r0 in range(0, D_MODEL // 2, 512)]
W13_PIECES = [("w13", r0, 512) for r0 in range(0, D_MODEL // 2, 512)]
GATHER_GROUPS = [("mid", [WHOLE(f) for f in MID_MATRICES]), ("w13a", W13_PIECES[:1]), ("w13b", W13_PIECES[1:]),
                 ("w2", [WHOLE("w2")])]
OTHER_WEIGHTS = ["w1", "w3", "w2"] + list(MID_WEIGHTS)
BEFORE = {
    "hgrn_fwd": [("wait", "mid")],
    "mm_out": [("wait", "w13a")],
    "mm_o": [("wait", "w13b")],
    "mm_w2": [("wait", "w2"), ("run", ("d2d", [WHOLE("w2")]), "gather_hand_over_w2")],
    "mm_dh": [("wait", "rs_w2"), ("chip_sum", "w2"), ("wait", "rs_w13"), ("chip_sum", "w13"), ("wait", "rs_mid")]
    + [("chip_sum", f) for f in MID_MATRICES],
}
CARRY = {
    "hgrn_fwd": [("d2d", [WHOLE(f) for f in MID_MATRICES])],
    "mm_out": [("d2d", W13_PIECES[:1])],
    "mm_o": [("d2d", W13_PIECES[1:])],
    "mm_du": [("pairx", ["w2"])],
    "mm_dhf": [("pairx", ["w13"])],
    "attn_bwd_g0": [("pairx", list(MID_MATRICES))],
    "mm_dwin_own": [("pairx_whole", ["w_in"])],
    "mm_dh": [("share", OTHER_WEIGHTS)],
}
AFTER = {
    "mm_du": [("pair_sum", "w2"), ("start", "rs_w2", [WHOLE("w2")])],
    "mm_dhf": [("pair_sum", "w13"), ("start", "rs_w13", [WHOLE("w13")])],
    "attn_bwd_g0": [("pair_sum", f) for f in MID_MATRICES] + [("start", "rs_mid", [WHOLE(f) for f in MID_MATRICES])],
    "mm_dwin_own": [("pair_sum", "w_in"), ("start", "rs_w_in", [WHOLE("w_in")])],
}
FINISH = [
    ("adamw", OTHER_WEIGHTS), ("wait", "rs_w_in"), ("small",), ("chip_sum", "w_in"),
    ("run", ("share", ["w_in"]), "rs_sibling_share_w_in"), ("adamw", ["w_in"]),
]


class _Net:
    def __init__(self, full, pos=None, shard_shapes=None, comm=True, specs=FULL_SPECS, place=WEIGHT_PLACE):
        self.full, self.pos, self.shard_shapes, self.comm = dict(full), pos, shard_shapes, comm
        self.specs, self.place = specs, place
        self.gw, self.recv, self.psum, self.slots, self.grads = {}, {}, {}, {}, {}
        self.gw_sibling = {}
        self.pending, self.token, self.last = {}, None, None

    def _make(self, kind, arg):
        if kind == "gather":
            return _gather_ici_comm(self.full, arg, self.specs)
        if kind == "ring":
            return _gather_ring_comm(self.full, *arg, self.specs)
        if kind == "d2d":
            return _gather_d2d_comm(self.full, arg, self.specs)
        if kind == "pairx":
            return _pairx_comm(self.gw, arg, self.specs)
        if kind == "pairx_whole":
            return _pairx_comm(self.gw_sibling, arg, self.specs, whole=True)
        if kind == "chipx":
            return _chipx_comm(self.psum, self.slots, arg, self.specs)
        assert kind == "share"
        return _share_comm(self.grads, arg, self.specs, self.place)

    def _store(self, kind, res):
        if kind in ("gather", "ring", "d2d"):
            self.full.update(res)
        elif kind in ("pairx", "pairx_whole"):
            for (tag, f), a in res.items():
                (self.gw if tag == "g" else self.recv)[f] = a
        elif kind == "chipx":
            for (tag, f), a in res.items():
                (self.psum if tag == "p" else self.slots)[f] = a
        else:
            self.grads.update(res)

    def run_comm(self, item, name):
        kind, arg = item
        self._store(kind, _run_comm([self._make(kind, arg)], name)[0])

    @staticmethod
    def _others(after, items):
        own = [a for cm in items for a in cm.arrays.values()]
        return [a for a in after if a is not None and all(a is not o for o in own)]

    def start(self, groups, kind, name):
        items = [self._make(kind, jobs) for _, jobs in groups]
        after = self._others([self.last], items)
        res, sems, token = _split_start(items, name, after=after[0] if after else None)
        for (group, jobs), r, s in zip(groups, res, sems):
            self._store(kind, r)
            self.pending[group] = (kind, jobs, s)
        self.token = self.last = token

    def wait(self, group, after=()):
        kind, jobs, sems = self.pending.pop(group)
        item = self._make(kind, jobs)
        res = _split_wait([item], [sems], self._others([self.last, *after], [item]), f"wait_{group}")[0]
        self._store(kind, res)

    def step(self, step):
        if step[0] == "wait":
            self.wait(step[1])
        elif step[0] == "start":
            self.start([(step[1], step[2])], "chipx", f"start_{step[1]}")
        elif step[0] == "pair_sum":
            f = step[1]
            self.psum[f], self.slots[f] = _pair_sum(self.gw[f], self.recv[f], self.pos, self.specs[f],
                                                    f"rs_pair_sum_{f}", whole=f in self.gw_sibling)
        elif step[0] == "chip_sum":
            f = step[1]
            self.grads.update(_chip_sum(self.psum[f], self.slots[f], self.pos, f, self.shard_shapes,
                                        self.specs, self.place, f"rs_chip_sum_{f}"))
        else:
            assert step[0] == "run"
            self.run_comm(step[1], step[2])

    def call(self, fn, name, *args, grad_of=None, sibling_half=False, **kw):
        for step in (BEFORE.get(name, []) if self.comm else []):
            self.step(step)
        args = [a() if callable(a) else a for a in args]
        items = CARRY.get(name, []) if self.comm else []
        carried = [self._make(k, a) for k, a in items]
        if self.token is not None:
            carried.append(_Token(self.token))
            self.token = None
        out, res = fn(*args, name=name, carried=carried, **kw)
        if grad_of is not None:
            (self.gw_sibling if sibling_half else self.gw)[grad_of] = out
        self.last = jax.tree.leaves(out)[0]
        for (kind, _), r in zip(items, res):
            self._store(kind, r)
        for step in (AFTER.get(name, []) if self.comm else []):
            self.step(step)
        return out


def _local_step(net, x, h, mem, target, small):
    full, call = net.full, net.call
    proj = call(_mm, "mm_proj", h, full["w_in"], mode="nn", out_dtype=F32)
    att = [call(_attn_fwd, f"attn_fwd_g{g}", proj, g) for g in range(3)]
    outs, lses = [a[0] for a in att], [a[1] for a in att]
    o_att = _attn_merge_fwd(outs, lses, "attn_merge")
    oraw, o_hg, states = call(_hg_fwd, "hgrn_fwd", proj, small["hg_lower_bounds"], small["hg_norm_w"])
    ya = call(_mm, "mm_branch_a", o_att, full["w_branch_a"], mode="nn", out_dtype=F32)
    yb, merged = call(_branch_b_gate, "mm_branch_b", o_hg, full["w_branch_b"], proj, ya)
    x1, hc = call(_residual_norm, "mm_out", merged, full["w_out"], x, small["ln_cross_w"])

    mn = _rms_fwd(mem, small["ln_mem_w"], "rms_mem")
    qc = call(_mm, "mm_q", hc, full["wq_cross"], mode="nn", out_dtype=F32)
    kvc = call(_mm, "mm_kv", mn, full["wkv_cross"], mode="nn", out_dtype=F32)
    oc = call(_cross_fwd, "cross_fwd", qc, kvc)
    x2, hf = call(_residual_norm, "mm_o", oc, full["wo_cross"], x1, small["ln_ffn_w"])

    ab, u = call(_ff_up, "mm_w13", hf, full["w13"])
    x3 = call(_mm, "mm_w2", u, lambda: full["w2"], mode="nn", out_dtype=F32, res=x2)

    dx3, dg_final, loss, dx3_bf16 = _loss_head(x3, small["ln_final_w"], target, "loss_head")

    gs = {"ln_final_w": dg_final}
    call(_mm, "mm_dw2", u, dx3_bf16, mode="tn", out_dtype=BF16, grad_of="w2")
    dab = call(_ff_down_bwd, "mm_du", dx3_bf16, full["w2"], ab)
    call(_mm, "mm_dw13", hf, dab, mode="tn", out_dtype=BF16, grad_of="w13")
    dhf = call(_mm, "mm_dhf", dab, full["w13"], mode="nt", out_dtype=F32)
    dx2, gs["ln_ffn_w"], dx2_bf16 = _rms_bwd(x2, small["ln_ffn_w"], dhf, dx3, "rms_ffn_bwd", bf16_copy=True)
    doc = call(_mm, "mm_doc", dx2_bf16, full["wo_cross"], mode="nt", out_dtype=BF16)
    call(_mm, "mm_dwo", oc, dx2_bf16, mode="tn", out_dtype=BF16, grad_of="wo_cross")
    dqc, dkvc = _cross_bwd(qc, kvc, doc, "cross_bwd")
    call(_mm, "mm_dwq", hc, dqc, mode="tn", out_dtype=BF16, grad_of="wq_cross")
    dx1, gs["ln_cross_w"], dx1_bf16 = call(_norm_bwd_residual, "mm_dhc", dqc, full["wq_cross"], x1,
                                           small["ln_cross_w"], dx2)
    call(_mm, "mm_dwkv", mn, dkvc, mode="tn", out_dtype=BF16, grad_of="wkv_cross")
    dmn = call(_mm, "mm_dmn", dkvc, full["wkv_cross"], mode="nt", out_dtype=F32)
    _, gs["ln_mem_w"] = _rms_bwd(mem, small["ln_mem_w"], dmn, None, "rms_mem_bwd")
    dya, dyb, dga, dgb = call(_dmerged_gate_bwd, "mm_dmerged", dx1_bf16, full["w_out"], proj, ya, yb)
    call(_mm, "mm_dwout", merged, dx1_bf16, mode="tn", out_dtype=BF16, grad_of="w_out")
    call(_mm, "mm_dwa", o_att, dya, mode="tn", out_dtype=BF16, grad_of="w_branch_a")
    do_att = call(_mm, "mm_doatt", dya, full["w_branch_a"], mode="nt", out_dtype=F32)
    call(_mm, "mm_dwb", o_hg, dyb, mode="tn", out_dtype=BF16, grad_of="w_branch_b")
    do_hg = call(_mm, "mm_dohg", dyb, full["w_branch_b"], mode="nt", out_dtype=F32)
    dqh, dfh, dih, dgh, dlb, gs["hg_norm_w"] = call(
        _hg_bwd, "hgrn_bwd", proj, small["hg_lower_bounds"], small["hg_norm_w"], oraw, states, do_hg)
    gs["hg_lb"] = dlb
    do_gs, dl_gs = call(_attn_merge_bwd, "attn_merge_bwd", outs, lses, do_att)
    dqs, dks, dvs = zip(*[call(_attn_bwd, f"attn_bwd_g{g}", proj, g, lses[g], do_gs[g], dl_gs[g]) for g in range(3)])
    dproj = _concat_cols([*dqs, *dks, *dvs, dqh, dfh, dih, dgh, dga, dgb], "dproj_concat")
    if net.comm:
        half = D_MODEL // 2
        c = lax.axis_index("c")
        h_sibling = lax.dynamic_slice_in_dim(h, (1 - c) * half, half, axis=1)
        h_own = lax.dynamic_slice_in_dim(h, c * half, half, axis=1)
        call(_mm, "mm_dwin_sibling", h_sibling, dproj, mode="tn", out_dtype=BF16, grad_of="w_in", sibling_half=True)
        call(_mm, "mm_dwin_own", h_own, dproj, mode="tn", out_dtype=BF16, grad_of="w_in")
    else:
        call(_mm, "mm_dwin", h, dproj, mode="tn", out_dtype=BF16, grad_of="w_in")
    dh = call(_mm, "mm_dh", dproj, full["w_in"], mode="nt", out_dtype=F32)
    dx, gs["ln_mix_w"] = _rms_bwd(x, small["ln_mix_w"], dh, dx1, "rms_mix_bwd")
    return loss, dx, gs


WEIGHT_ORDER = ("ln_mix_w", "w_in", "hg_norm_w", "hg_lower_bounds", "w_branch_a", "w_branch_b", "w_out",
                "ln_cross_w", "ln_mem_w", "wq_cross", "wkv_cross", "wo_cross", "ln_ffn_w", "w1", "w3", "w2",
                "ln_final_w")


def kernel(x, mem, ln_mix_w, w_in, hg_norm_w, hg_lower_bounds, w_branch_a, w_branch_b, w_out, ln_cross_w, ln_mem_w, wq_cross, wkv_cross, wo_cross, ln_ffn_w, w1, w3, w2, ln_final_w, loss_target, m_ln_mix_w, m_w_in, m_hg_norm_w, m_hg_lower_bounds, m_w_branch_a, m_w_branch_b, m_w_out, m_ln_cross_w, m_ln_mem_w, m_wq_cross, m_wkv_cross, m_wo_cross, m_ln_ffn_w, m_w1, m_w3, m_w2, m_ln_final_w, v_ln_mix_w, v_w_in, v_hg_norm_w, v_hg_lower_bounds, v_w_branch_a, v_w_branch_b, v_w_out, v_ln_cross_w, v_ln_mem_w, v_wq_cross, v_wkv_cross, v_wo_cross, v_ln_ffn_w, v_w1, v_w3, v_w2, v_ln_final_w):
    args = dict(locals())
    w = {n: args[n] for n in WEIGHT_ORDER}
    m = {n: args["m_" + n] for n in WEIGHT_ORDER}
    v = {n: args["v_" + n] for n in WEIGHT_ORDER}
    shapes = {n: w[n].shape for n in WEIGHT_ORDER}
    mat = lambda a: a.reshape(a.shape[-2:])
    shard_shapes = {n: shapes[n][-2:] for n in BIG_WEIGHTS}

    pos = _mesh_scalars()

    def cast(f, token=None):
        return _cast_into_full({n: mat(w[n]) for n in BIG_WEIGHTS if WEIGHT_PLACE[n][0] == f}, f, pos,
                               FULL_SPECS, WEIGHT_PLACE, f"cast_{f}", token)

    net = _Net({"w_in": cast("w_in")}, pos, shard_shapes)
    net.start([(f"ring_a{i}", (*job, "a")) for i, job in enumerate(W_IN_PIECES)], "ring", "gather_start_w_in")
    rest = {f: cast(f, net.token) for f in FULL_SPECS if f != "w_in"}
    net.full.update(rest)
    small = {n: w[n].reshape(1, -1) for n in SMALL_ROWS}
    small["hg_norm_w"] = w["hg_norm_w"].reshape(1, HEAD_DIM)
    small["hg_lower_bounds"] = w["hg_lower_bounds"]
    x2d = x.reshape(SEQ, D_MODEL)
    h = _rms_fwd(x2d, small["ln_mix_w"], "rms_mix")
    for i, job in enumerate(W_IN_PIECES):
        net.wait(f"ring_a{i}", after=[*rest.values(), h] if i == 0 else ())
        net.start([(f"ring_b{i}", (*job, "b"))], "ring", f"gather_pass_on_w_in{i}")
    net.start(GATHER_GROUPS, "gather", "gather_start_rest")
    for i, job in enumerate(W_IN_PIECES):
        net.wait(f"ring_b{i}")
        net.run_comm(("d2d", [job]), f"gather_hand_over_w_in{i}")
    loss, dx, gs = _local_step(net, x2d, h, mem.reshape(MEM_LEN, D_MODEL), loss_target.reshape(SEQ, D_MODEL), small)

    out_g, out_d, out_m, out_v = {}, {}, {}, {}
    net.last = dx
    for step in FINISH:
        if step[0] == "adamw":
            for n in step[1]:
                out_d[n], out_m[n], out_v[n], out_g[n] = net.call(_adamw, f"adamw_{n}", mat(w[n]), net.grads[n],
                                                                  mat(m[n]), mat(v[n]))
        elif step[0] == "wait":
            net.wait(step[1], after=list(out_d.values()))
        elif step[0] == "small":
            pad = lambda a: jnp.pad(a, ((0, 0), (0, D_MODEL - a.shape[1])))
            part = jnp.concatenate(
                [gs[n] for n in SMALL_ROWS]
                + [pad(jnp.concatenate([gs["hg_norm_w"][0], loss], axis=1)), pad(gs["hg_lb"]),
                   pad(gs["hg_norm_w"][1]) if len(gs["hg_norm_w"]) > 1 else jnp.zeros((1, D_MODEL), F32)], axis=0)
            part, net.psum["w_in"] = lax.optimization_barrier((part, net.psum["w_in"]))
            sg, sd, sm, sv, loss_tot = _small_update(_gather_rows(part), _pack_small(w), _pack_small(m),
                                                     _pack_small(v))
            for dst, packed in ((out_g, sg), (out_d, sd), (out_m, sm), (out_v, sv)):
                dst.update(_unpack_small(packed, shapes))
        else:
            net.step(step)

    result = [loss_tot[0, 0], dx.reshape(x.shape)]
    for group in (out_g, out_d, out_m, out_v):
        result += [group[n].reshape(shapes[n]) for n in WEIGHT_ORDER]
    return tuple(result)
```

```python
import math

import jax
import jax.numpy as jnp
from jax import lax
from jax.experimental import pallas as pl
from jax.experimental.pallas import tpu as pltpu

F32 = jnp.float32
BF16 = jnp.bfloat16
MESH = pl.DeviceIdType.MESH

D_MODEL = 2048
SEQ = 2048
HEAD_DIM = 128
MEM_LEN = 256
ATT_GROUPS = ((128, 1), (512, 4), (2048, 16))
ATT_HEADS = 4
ATT_WIDTH = 3 * ATT_HEADS * HEAD_DIM
ATT_OUT = ATT_HEADS * HEAD_DIM
ATT_BLOCK = 128
HG_HEADS = 8
HG_WIDTH = HG_HEADS * HEAD_DIM
HG_CHUNK = 64
IN_WIDTH = 3 * ATT_WIDTH + 4 * HG_WIDTH + 2 * D_MODEL
CROSS_HEADS = 4
CROSS_WIDTH = CROSS_HEADS * HEAD_DIM
D_FF = 5632
RMS_EPS = 1e-6
ADAM_LR = 0.001
ADAM_B1 = 0.9
ADAM_B2 = 0.999
ADAM_EPS = 1e-08
ADAM_WD = 0.01
ADAM_STEP = 10
N_CHIPS = 4
N_DEV = 8

VMEM_LIMIT_BYTES = 56 * 1024 * 1024
LANE = 128
MXU_WIDTH = 256
MM_TILE_CAP = 1536
TRANSPOSE_CHUNK = 512
ANY = pl.BlockSpec(memory_space=pl.ANY)


def _cparams(sem=None):
    return pltpu.CompilerParams(dimension_semantics=sem, vmem_limit_bytes=VMEM_LIMIT_BYTES)


def _div(n, cap, mult):
    best = None
    for d in range(mult, min(n, cap) + 1, mult):
        if n % d == 0:
            best = d
    assert best is not None, (n, cap, mult)
    return best


def _sigmoid(x):
    return 1.0 / (1.0 + jnp.exp(-x))


def _dot(a, b):
    return jnp.dot(a.astype(BF16), b.astype(BF16), preferred_element_type=F32)


def _dot_nt(a, b):
    return lax.dot_general(a.astype(BF16), b.astype(BF16), (((1,), (1,)), ((), ())),
                           preferred_element_type=F32)


def _dot_tn(a, b):
    return jnp.dot(a.astype(F32).T.astype(BF16), b.astype(BF16), preferred_element_type=F32)


def _dot_exact(a, b):
    return jnp.dot(a, b, precision=lax.Precision.HIGHEST, preferred_element_type=F32)


class _Carried:
    def __init__(self, arrays, fresh, n_sems, start, finish, mid=None, reads=None):
        self.arrays, self.fresh, self.n_sems, self.reads = arrays, fresh, n_sems, reads or {}
        self.start, self.mid, self.finish = start, mid, finish


class _Token:
    def __init__(self, array):
        self.array = array


TOKEN_SHAPE = (8, LANE)


def _carried_layout(carried):
    akeys = list(dict.fromkeys(k for cm in carried for k in cm.arrays))
    fkeys = [(ci, k) for ci, cm in enumerate(carried) for k in cm.fresh]
    arrays = [next(cm.arrays[k] for cm in carried if k in cm.arrays) for k in akeys]
    shapes = [jax.ShapeDtypeStruct(a.shape, a.dtype) for a in arrays] + [carried[ci].fresh[k] for ci, k in fkeys]
    sems = []
    for cm in carried:
        sems += [pltpu.SemaphoreType.DMA((cm.n_sems,)), pltpu.SemaphoreType.DMA((cm.n_sems,))]
    return akeys, fkeys, arrays, shapes, sems


def _carried_reads(carried):
    rkeys = list(dict.fromkeys(k for cm in carried for k in cm.reads))
    return rkeys, [next(cm.reads[k] for cm in carried if k in cm.reads) for k in rkeys]


def _carried_results(carried, akeys, fkeys, outs, rkeys=(), read_refs=()):
    shared = dict(zip(akeys, outs[:len(akeys)]))
    shared.update(zip(rkeys, read_refs))
    res = [{k: shared[k] for k in list(cm.arrays) + [r for r in cm.reads if r in shared]} for cm in carried]
    for (ci, k), o in zip(fkeys, outs[len(akeys):]):
        res[ci][k] = o
    return res


def _pcall(kern, *, name, grid, in_specs, out_specs, out_shape, args, scratch_shapes=(), semantics=None,
           carried=()):
    tokens = [c.array for c in carried if isinstance(c, _Token)]
    carried = [c for c in carried if not isinstance(c, _Token)]
    single = not isinstance(out_shape, (list, tuple))
    out_specs = [out_specs] if single else list(out_specs)
    out_shape = [out_shape] if single else list(out_shape)
    n_real, n_out, n_scr = len(in_specs), len(out_shape), len(scratch_shapes)
    in_specs = list(in_specs) + [pl.BlockSpec(TOKEN_SHAPE, lambda *_: (0, 0))] * len(tokens)
    args = list(args) + tokens
    n_in = len(in_specs)
    if not carried:
        def plain(*refs):
            kern(*refs[:n_real], *refs[n_in:])

        outs = pl.pallas_call(plain if tokens else kern, name=name, grid=grid, in_specs=in_specs,
                              out_specs=out_specs, out_shape=out_shape, scratch_shapes=list(scratch_shapes),
                              compiler_params=_cparams(semantics))(*args)
        return (outs[0] if single else list(outs)), []
    akeys, fkeys, arrays, shapes, sems = _carried_layout(carried)
    rkeys, reads = _carried_reads(carried)
    n_a, n_f, n_r = len(akeys), len(fkeys), len(rkeys)
    total = math.prod(grid)
    mid_step = min(total - 1, (17 * total) // 20)

    def wrapped(*refs):
        ins = refs[:n_real]
        r0 = n_in + n_a
        o0 = r0 + n_r
        outs = refs[o0:o0 + n_out]
        a0 = o0 + n_out
        s0 = a0 + n_a + n_f
        per = _carried_results(carried, akeys, fkeys, refs[a0:s0], rkeys, refs[r0:o0])
        scratch = refs[s0:s0 + n_scr]
        sem = refs[s0 + n_scr:]
        step = 0
        for d, g in enumerate(grid):
            step = step * g + pl.program_id(d)

        @pl.when(step == 0)
        def _():
            for ci, cm in enumerate(carried):
                cm.start(per[ci], sem[2 * ci], sem[2 * ci + 1])

        kern(*ins, *outs, *scratch)

        @pl.when(step == mid_step)
        def _():
            for ci, cm in enumerate(carried):
                if cm.mid is not None:
                    cm.mid(per[ci], sem[2 * ci], sem[2 * ci + 1])

        @pl.when(step == total - 1)
        def _():
            for ci, cm in enumerate(carried):
                cm.finish(per[ci], sem[2 * ci], sem[2 * ci + 1])

    outs = pl.pallas_call(
        wrapped, name=name, grid=grid,
        in_specs=list(in_specs) + [ANY] * (n_a + n_r), out_specs=out_specs + [ANY] * (n_a + n_f),
        out_shape=out_shape + shapes,
        input_output_aliases={n_in + i: n_out + i for i in range(n_a)},
        scratch_shapes=list(scratch_shapes) + sems,
        compiler_params=_cparams(("arbitrary",) * len(grid)),
    )(*args, *arrays, *reads)
    res = _carried_results(carried, akeys, fkeys, outs[n_out:])
    return (outs[0] if single else list(outs[:n_out])), res


def _run_comm(carried, name):
    carried = list(carried)
    akeys, fkeys, arrays, shapes, sems = _carried_layout(carried)
    rkeys, reads = _carried_reads(carried)
    n_a, n_f, n_r = len(akeys), len(fkeys), len(rkeys)

    def body(*refs):
        o0 = n_a + n_r
        per = _carried_results(carried, akeys, fkeys, refs[o0:o0 + n_a + n_f], rkeys, refs[n_a:o0])
        sem = refs[o0 + n_a + n_f:]
        for hook in ("start", "mid", "finish"):
            for ci, cm in enumerate(carried):
                fn = getattr(cm, hook)
                if fn is not None:
                    fn(per[ci], sem[2 * ci], sem[2 * ci + 1])

    outs = pl.pallas_call(
        body, name=name, in_specs=[ANY] * (n_a + n_r), out_specs=[ANY] * (n_a + n_f), out_shape=shapes,
        input_output_aliases={i: i for i in range(n_a)}, scratch_shapes=sems,
    )(*arrays, *reads)
    return _carried_results(carried, akeys, fkeys, outs)


HBM_SPEC = pl.BlockSpec(memory_space=pltpu.HBM)
SEM_SPEC = pl.BlockSpec(memory_space=pltpu.SEMAPHORE)
SPLIT_EFFECT = pltpu.SideEffectType.DATAFLOW_SIDE_EFFECTING


def _in_hbm(a):
    return pltpu.with_memory_space_constraint(a, pltpu.HBM)


def _split_start(items, name, after=None):
    items = list(items)
    akeys, fkeys, arrays, shapes, sems = _carried_layout(items)
    assert not fkeys
    n_a, n_s = len(akeys), len(sems)
    n_in = n_a + (after is not None)

    def body(*refs):
        per = _carried_results(items, akeys, [], refs[n_in:n_in + n_a])
        sem = refs[n_in + n_a:n_in + n_a + n_s]
        for ci, cm in enumerate(items):
            cm.start(per[ci], sem[2 * ci], sem[2 * ci + 1])
        token = refs[n_in + n_a + n_s]
        token[...] = jnp.zeros_like(token)

    outs = pl.pallas_call(
        body, name=name, in_specs=[HBM_SPEC] * n_a + [ANY] * (after is not None),
        out_specs=[HBM_SPEC] * n_a + [SEM_SPEC] * n_s + [pl.BlockSpec(memory_space=pltpu.VMEM)],
        out_shape=[pltpu.HBM(s.shape, s.dtype) for s in shapes] + sems + [jax.ShapeDtypeStruct(TOKEN_SHAPE, F32)],
        input_output_aliases={i: i for i in range(n_a)},
        compiler_params=pltpu.CompilerParams(has_side_effects=SPLIT_EFFECT),
    )(*[_in_hbm(a) for a in arrays], *([after] if after is not None else []))
    res = _carried_results(items, akeys, [], outs[:n_a])
    sem_out = outs[n_a:n_a + n_s]
    return res, [(sem_out[2 * ci], sem_out[2 * ci + 1]) for ci in range(len(items))], outs[-1]


def _split_wait(items, sems, after, name):
    items = list(items)
    after = list(after) if isinstance(after, (list, tuple)) else [after]
    akeys, fkeys, arrays, shapes, _ = _carried_layout(items)
    n_a, n_s = len(akeys), 2 * len(items)

    def body(*refs):
        per = _carried_results(items, akeys, [], refs[n_a + n_s + len(after):])
        sem = refs[n_a:n_a + n_s]
        for ci, cm in enumerate(items):
            cm.finish(per[ci], sem[2 * ci], sem[2 * ci + 1])

    outs = pl.pallas_call(
        body, name=name, in_specs=[HBM_SPEC] * n_a + [SEM_SPEC] * n_s + [ANY] * len(after),
        out_specs=[HBM_SPEC] * n_a, out_shape=[pltpu.HBM(s.shape, s.dtype) for s in shapes],
        input_output_aliases={i: i for i in range(n_a)},
        compiler_params=pltpu.CompilerParams(has_side_effects=SPLIT_EFFECT),
    )(*arrays, *[s for pair in sems for s in pair], *after)
    return _carried_results(items, akeys, [], outs)


def _mm(a, b, *, mode, out_dtype, name, res=None, carried=(), fused=None):
    if mode == "nn":
        (m, k), (k2, n) = a.shape, b.shape
    elif mode == "nt":
        (m, k), (n, k2) = a.shape, b.shape
    else:
        (k, m), (k2, n) = a.shape, b.shape
    assert k == k2, (name, a.shape, b.shape)
    tm = _div(m, MM_TILE_CAP, LANE)
    tn = _div(n, MM_TILE_CAP, MXU_WIDTH) if n % MXU_WIDTH == 0 else 0
    if tn < 1024:
        tn = _div(n, MM_TILE_CAP, LANE)
    out_shape = jax.ShapeDtypeStruct((m, n), out_dtype)

    if fused is not None:
        assert mode in ("nn", "nt") and res is None and k <= 2048
        tm, tn = fused["tile"]
        dot = _dot if mode == "nn" else _dot_nt
        n_x = len(fused["ins"])

        summed = [len(e) > 2 for e in fused["outs"]]

        def kern_fused(*refs):
            part = dot(refs[0][...], refs[1][...])
            outs = fused["post"](part, *[r[...] for r in refs[2:2 + n_x]])
            first_row_tile = pl.program_id(1) == 0
            for o_ref, val, acc in zip(refs[2 + n_x:], outs, summed):
                if not acc:
                    o_ref[...] = val.astype(o_ref.dtype)
                    continue

                @pl.when(first_row_tile)
                def _():
                    o_ref[...] = val

                @pl.when(jnp.logical_not(first_row_tile))
                def _():
                    o_ref[...] += val

        def tile_spec(shape, index=lambda i, j: (i, j)):
            return pl.BlockSpec(shape, lambda j, i: index(i, j))

        return _pcall(
            kern_fused, name=name, grid=(n // tn, m // tm),
            in_specs=[pl.BlockSpec((tm, k), lambda j, i: (i, 0)),
                      pl.BlockSpec((k, tn), lambda j, i: (0, j)) if mode == "nn"
                      else pl.BlockSpec((tn, k), lambda j, i: (j, 0))] + [tile_spec(*e[1:]) for e in fused["ins"]],
            out_specs=[tile_spec(*e[1:]) for e in fused["outs"]], out_shape=[e[0] for e in fused["outs"]],
            args=(a, b, *[e[0] for e in fused["ins"]]), semantics=("parallel", "arbitrary"), carried=carried)

    if mode == "tn":
        assert res is None

        def kern_tn(a_ref, b_ref, o_ref, at_ref):
            @pl.when(pl.program_id(1) == 0)
            def _():
                step = min(TRANSPOSE_CHUNK, k)
                for c0 in range(0, k, step):
                    at_ref[:, c0:c0 + step] = a_ref[c0:c0 + step, :].astype(F32).T.astype(BF16)

            o_ref[...] = jnp.dot(at_ref[...], b_ref[...].astype(BF16),
                                 preferred_element_type=F32).astype(o_ref.dtype)

        return _pcall(
            kern_tn, name=name, grid=(m // tm, n // tn),
            in_specs=[pl.BlockSpec((k, tm), lambda i, j: (0, i)),
                      pl.BlockSpec((k, tn), lambda i, j: (0, j))],
            out_specs=pl.BlockSpec((tm, tn), lambda i, j: (i, j)),
            out_shape=out_shape, args=(a, b),
            scratch_shapes=[pltpu.VMEM((tm, k), BF16)],
            semantics=("parallel", "arbitrary"), carried=carried)

    tk = k if k <= 2048 else _div(k, 3072, LANE)
    nk = k // tk
    a_spec = pl.BlockSpec((tm, tk), lambda i, j, kk: (i, kk))
    if mode == "nn":
        b_spec = pl.BlockSpec((tk, tn), lambda i, j, kk: (kk, j))
        dot = _dot
    else:
        b_spec = pl.BlockSpec((tn, tk), lambda i, j, kk: (j, kk))
        dot = _dot_nt
    o_spec = pl.BlockSpec((tm, tn), lambda i, j, kk: (i, j))
    in_specs = [a_spec, b_spec]
    args = [a, b]
    if res is not None:
        in_specs.append(o_spec)
        args.append(res)
    has_res = res is not None

    def kern(*refs):
        a_ref, b_ref = refs[0], refs[1]
        r_ref = refs[2] if has_res else None
        o_ref = refs[3] if has_res else refs[2]
        part = dot(a_ref[...], b_ref[...])
        if nk == 1:
            if has_res:
                part = part + r_ref[...]
            o_ref[...] = part.astype(o_ref.dtype)
            return
        acc_ref = refs[-1]
        kk = pl.program_id(2)

        @pl.when(kk == 0)
        def _():
            acc_ref[...] = part

        @pl.when(kk > 0)
        def _():
            acc_ref[...] += part

        @pl.when(kk == nk - 1)
        def _():
            tot = acc_ref[...]
            if has_res:
                tot = tot + r_ref[...]
            o_ref[...] = tot.astype(o_ref.dtype)

    return _pcall(
        kern, name=name, grid=(m // tm, n // tn, nk),
        in_specs=in_specs, out_specs=o_spec, out_shape=out_shape, args=args,
        scratch_shapes=[pltpu.VMEM((tm, tn), F32)] if nk > 1 else [],
        semantics=("parallel", "parallel", "arbitrary"), carried=carried)


ROW_BLOCK = 512


def _rms_fwd(x, g, name):
    t, d = x.shape
    tr = min(ROW_BLOCK, t)

    def kern(x_ref, g_ref, o_ref):
        xf = x_ref[...]
        r = lax.rsqrt(jnp.mean(xf * xf, axis=-1, keepdims=True) + RMS_EPS)
        o_ref[...] = (xf * r * g_ref[...]).astype(o_ref.dtype)

    return pl.pallas_call(
        kern, name=name, grid=(t // tr,),
        in_specs=[pl.BlockSpec((tr, d), lambda i: (i, 0)), pl.BlockSpec((1, d), lambda i: (0, 0))],
        out_specs=pl.BlockSpec((tr, d), lambda i: (i, 0)),
        out_shape=jax.ShapeDtypeStruct((t, d), BF16),
        compiler_params=_cparams(("parallel",)),
    )(x, g)


def _rms_fwd_halves(x, g, pos, name):
    t, d = x.shape
    tr = min(ROW_BLOCK, t)
    half = d // 2

    def kern(pos_ref, x_ref, g_ref, o_ref, sib_ref, own_ref):
        xf = x_ref[...]
        r = lax.rsqrt(jnp.mean(xf * xf, axis=-1, keepdims=True) + RMS_EPS)
        h = (xf * r * g_ref[...]).astype(o_ref.dtype)
        o_ref[...] = h
        is_south = pos_ref[0] == 0
        sib_ref[...] = jnp.where(is_south, h[:, half:], h[:, :half])
        own_ref[...] = jnp.where(is_south, h[:, :half], h[:, half:])

    row = pl.BlockSpec((tr, d), lambda i, pos_ref: (i, 0))
    part = pl.BlockSpec((tr, half), lambda i, pos_ref: (i, 0))
    return pl.pallas_call(
        kern, name=name,
        grid_spec=_grid_spec((t // tr,), [row, pl.BlockSpec((1, d), lambda i, pos_ref: (0, 0))], [row, part, part]),
        out_shape=[jax.ShapeDtypeStruct((t, d), BF16)] + [jax.ShapeDtypeStruct((t, half), BF16)] * 2,
        compiler_params=_cparams(("parallel",)),
    )(pos, x, g)


def _rms_bwd(x, g, dh, res, name, bf16_copy=False):
    t, d = x.shape
    tr = min(ROW_BLOCK, t)
    has_res = res is not None

    def kern(*refs):
        x_ref, g_ref, dh_ref = refs[:3]
        r_ref = refs[3] if has_res else None
        dx_ref, dg_ref = refs[3 + has_res], refs[4 + has_res]
        xf = x_ref[...]
        r = lax.rsqrt(jnp.mean(xf * xf, axis=-1, keepdims=True) + RMS_EPS)
        xn = xf * r
        dh_ = dh_ref[...]
        dhg = dh_ * g_ref[...]
        dx = r * (dhg - xn * jnp.mean(dhg * xn, axis=-1, keepdims=True))
        if has_res:
            dx = dx + r_ref[...]
        dx_ref[...] = dx
        if bf16_copy:
            refs[-1][...] = dx.astype(BF16)
        part = jnp.sum(dh_ * xn, axis=0, keepdims=True)

        @pl.when(pl.program_id(0) == 0)
        def _():
            dg_ref[...] = part

        @pl.when(pl.program_id(0) > 0)
        def _():
            dg_ref[...] += part

    row = pl.BlockSpec((tr, d), lambda i: (i, 0))
    vec = pl.BlockSpec((1, d), lambda i: (0, 0))
    in_specs = [row, vec, row] + ([row] if has_res else [])
    args = [x, g, dh] + ([res] if has_res else [])
    return pl.pallas_call(
        kern, name=name, grid=(t // tr,), in_specs=in_specs, out_specs=[row, vec] + [row] * bf16_copy,
        out_shape=[jax.ShapeDtypeStruct((t, d), F32), jax.ShapeDtypeStruct((1, d), F32)]
        + [jax.ShapeDtypeStruct((t, d), BF16)] * bf16_copy,
        compiler_params=_cparams(("arbitrary",)),
    )(*args)


def _loss_head(x3, g, target, name):
    t, d = x3.shape
    tr = ROW_BLOCK

    def kern(x_ref, g_ref, t_ref, dx_ref, dg_ref, loss_ref, dxb_ref):
        xf = x_ref[...]
        r = lax.rsqrt(jnp.mean(xf * xf, axis=-1, keepdims=True) + RMS_EPS)
        xn = xf * r
        gg = g_ref[...]
        err = xn * gg - t_ref[...]
        lpart = 0.5 * jnp.sum(jnp.mean(err * err, axis=-1, keepdims=True), axis=0, keepdims=True)
        dy = err * (1.0 / d)
        dyg = dy * gg
        dx = r * (dyg - xn * jnp.mean(dyg * xn, axis=-1, keepdims=True))
        dx_ref[...] = dx
        dxb_ref[...] = dx.astype(BF16)
        gpart = jnp.sum(dy * xn, axis=0, keepdims=True)
        lrow = jnp.broadcast_to(lpart, (1, LANE))

        @pl.when(pl.program_id(0) == 0)
        def _():
            dg_ref[...] = gpart
            loss_ref[...] = lrow

        @pl.when(pl.program_id(0) > 0)
        def _():
            dg_ref[...] += gpart
            loss_ref[...] += lrow

    row = pl.BlockSpec((tr, d), lambda i: (i, 0))
    vec = pl.BlockSpec((1, d), lambda i: (0, 0))
    return pl.pallas_call(
        kern, name=name, grid=(t // tr,), in_specs=[row, vec, row],
        out_specs=[row, vec, pl.BlockSpec((1, LANE), lambda i: (0, 0)), row],
        out_shape=[jax.ShapeDtypeStruct((t, d), F32), jax.ShapeDtypeStruct((1, d), F32),
                   jax.ShapeDtypeStruct((1, LANE), F32), jax.ShapeDtypeStruct((t, d), BF16)],
        compiler_params=_cparams(("arbitrary",)),
    )(x3, g, target)


ATT_SCALE = HEAD_DIM ** -0.5
Q_BLOCK0, K_BLOCK0, V_BLOCK0 = 0, ATT_WIDTH // HEAD_DIM, 2 * ATT_WIDTH // HEAD_DIM


def _residue_rows(dil, r, n):
    if dil == 1:
        return pl.ds(n * ATT_BLOCK, ATT_BLOCK)
    return pl.ds(n * ATT_BLOCK * dil + r, ATT_BLOCK, stride=dil)


def _band_mask(with_prev):
    width = 2 * ATT_BLOCK if with_prev else ATT_BLOCK
    iq = lax.broadcasted_iota(jnp.int32, (ATT_BLOCK, width), 0)
    ik = lax.broadcasted_iota(jnp.int32, (ATT_BLOCK, width), 1)
    if not with_prev:
        return ik <= iq
    return ((ik < ATT_BLOCK) & (iq <= ik)) | ((ik >= ATT_BLOCK) & ((ik - ATT_BLOCK) <= iq))


def _band_keys(ref, dil, r, n):
    own = ref[_residue_rows(dil, r, n), :]
    if n == 0:
        return own
    return jnp.concatenate([ref[_residue_rows(dil, r, n - 1), :], own], axis=0)


def _attn_col_spec(base, grp):
    return pl.BlockSpec((SEQ, HEAD_DIM), lambda h: (0, base + grp * ATT_HEADS + h))


def _attn_fwd(proj, grp, name, carried=()):
    _, dil = ATT_GROUPS[grp]
    nb = SEQ // dil // ATT_BLOCK

    def kern(q_ref, k_ref, v_ref, o_ref, lse_ref):
        for r in range(dil):
            for n in range(nb):
                rows = _residue_rows(dil, r, n)
                s = _dot_nt(q_ref[rows, :], _band_keys(k_ref, dil, r, n)) * ATT_SCALE
                s = jnp.where(_band_mask(n > 0), s, -jnp.inf)
                m = jnp.max(s, axis=-1, keepdims=True)
                p = jnp.exp(s - m)
                l = jnp.sum(p, axis=-1, keepdims=True)
                o_ref[rows, :] = _dot(p / l, _band_keys(v_ref, dil, r, n))
                lse_ref[rows, :] = jnp.broadcast_to(m + jnp.log(l), (ATT_BLOCK, HEAD_DIM))

    out_spec = pl.BlockSpec((SEQ, HEAD_DIM), lambda h: (0, h))
    return _pcall(
        kern, name=name, grid=(ATT_HEADS,),
        in_specs=[_attn_col_spec(Q_BLOCK0, grp), _attn_col_spec(K_BLOCK0, grp), _attn_col_spec(V_BLOCK0, grp)],
        out_specs=[out_spec, out_spec],
        out_shape=[jax.ShapeDtypeStruct((SEQ, ATT_OUT), F32)] * 2, args=(proj, proj, proj),
        semantics=("parallel",), carried=carried)


def _attn_weights(l0, l1, l2):
    mx = jnp.maximum(jnp.maximum(l0, l1), l2)
    e0, e1, e2 = jnp.exp(l0 - mx), jnp.exp(l1 - mx), jnp.exp(l2 - mx)
    den = e0 + e1 + e2
    return e0 / den, e1 / den, e2 / den


def _attn_merge_fwd(outs, lses, name):
    tr = ROW_BLOCK

    def kern(o0, o1, o2, l0, l1, l2, out_ref):
        a0, a1, a2 = _attn_weights(l0[...], l1[...], l2[...])
        out_ref[...] = (a0 * o0[...] + a1 * o1[...] + a2 * o2[...]).astype(out_ref.dtype)

    spec = pl.BlockSpec((tr, ATT_OUT), lambda i: (i, 0))
    return pl.pallas_call(
        kern, name=name, grid=(SEQ // tr,), in_specs=[spec] * 6, out_specs=spec,
        out_shape=jax.ShapeDtypeStruct((SEQ, ATT_OUT), BF16),
        compiler_params=_cparams(("parallel",)),
    )(*outs, *lses)


def _attn_merge_bwd(outs, lses, do_att, name, carried=()):
    tr = ROW_BLOCK

    def kern(o0, o1, o2, l0, l1, l2, do_ref, d0, d1, d2, t0, t1, t2):
        alphas = _attn_weights(l0[...], l1[...], l2[...])
        do = do_ref[...]
        o_att = alphas[0] * o0[...] + alphas[1] * o1[...] + alphas[2] * o2[...]
        prod = do * o_att
        parts = []
        for h in range(ATT_HEADS):
            sl = slice(h * HEAD_DIM, (h + 1) * HEAD_DIM)
            tot = jnp.sum(prod[:, sl], axis=-1, keepdims=True)
            parts.append(jnp.broadcast_to(tot, (tr, HEAD_DIM)))
        dd = jnp.concatenate(parts, axis=1)
        for a, d_ref, t_ref in zip(alphas, (d0, d1, d2), (t0, t1, t2)):
            d_ref[...] = a * do
            t_ref[...] = -a * dd

    spec = pl.BlockSpec((tr, ATT_OUT), lambda i: (i, 0))
    res, cres = _pcall(
        kern, name=name, grid=(SEQ // tr,), in_specs=[spec] * 7, out_specs=[spec] * 6,
        out_shape=[jax.ShapeDtypeStruct((SEQ, ATT_OUT), F32)] * 6, args=(*outs, *lses, do_att),
        semantics=("parallel",), carried=carried)
    return (res[:3], res[3:]), cres


def _attn_bwd(proj, grp, lse, do_g, dl_g, name, carried=()):
    _, dil = ATT_GROUPS[grp]
    nb = SEQ // dil // ATT_BLOCK

    def kern(q_ref, k_ref, v_ref, do_ref, lse_ref, dl_ref, dq_ref, dk_ref, dv_ref, dq_acc, dk_acc, dv_acc):
        dk_acc[...] = jnp.zeros_like(dk_acc)
        dv_acc[...] = jnp.zeros_like(dv_acc)
        for r in range(dil):
            for n in range(nb):
                rows = _residue_rows(dil, r, n)
                q, do = q_ref[rows, :], do_ref[rows, :]
                kk, vv = _band_keys(k_ref, dil, r, n), _band_keys(v_ref, dil, r, n)
                s = _dot_nt(q, kk) * ATT_SCALE
                p = jnp.where(_band_mask(n > 0), jnp.exp(s - lse_ref[rows, :][:, :1]), 0.0)
                ds = p * (_dot_nt(do, vv) + dl_ref[rows, :][:, :1])
                dq_acc[rows, :] = _dot(ds, kk) * ATT_SCALE
                dk = _dot_tn(ds, q) * ATT_SCALE
                dv = _dot_tn(p, do)
                if n > 0:
                    prev = _residue_rows(dil, r, n - 1)
                    dk_acc[prev, :] += dk[:ATT_BLOCK]
                    dv_acc[prev, :] += dv[:ATT_BLOCK]
                    dk, dv = dk[ATT_BLOCK:], dv[ATT_BLOCK:]
                dk_acc[rows, :] += dk
                dv_acc[rows, :] += dv
        dq_ref[...] = dq_acc[...].astype(dq_ref.dtype)
        dk_ref[...] = dk_acc[...].astype(dk_ref.dtype)
        dv_ref[...] = dv_acc[...].astype(dv_ref.dtype)

    spec = pl.BlockSpec((SEQ, HEAD_DIM), lambda h: (0, h))
    return _pcall(
        kern, name=name, grid=(ATT_HEADS,),
        in_specs=[_attn_col_spec(Q_BLOCK0, grp), _attn_col_spec(K_BLOCK0, grp), _attn_col_spec(V_BLOCK0, grp),
                  spec, spec, spec],
        out_specs=[spec] * 3,
        out_shape=[jax.ShapeDtypeStruct((SEQ, ATT_OUT), BF16)] * 3, args=(proj, proj, proj, do_g, lse, dl_g),
        scratch_shapes=[pltpu.VMEM((SEQ, HEAD_DIM), F32)] * 3,
        semantics=("parallel",), carried=carried)


HG_HEADS_PER_STEP = 8
HG_BLOCK_W = 4 * HEAD_DIM
HG_BLOCKS = HG_HEADS_PER_STEP * HEAD_DIM // HG_BLOCK_W
HG_STEP_W = HG_HEADS_PER_STEP * HEAD_DIM
HG_Q_BLK = (3 * ATT_WIDTH) // HG_BLOCK_W
HG_N_CHUNKS = SEQ // HG_CHUNK
HG_MID = HG_CHUNK // 2


def _lower_bound(lb_ref, sl):
    l0, l1 = lb_ref[0:1, sl], lb_ref[1:2, sl]
    mx = jnp.maximum(l0, l1)
    e0, e1 = jnp.exp(l0 - mx), jnp.exp(l1 - mx)
    return e0 / (e0 + e1)


def _tri(lower):
    i = lax.broadcasted_iota(jnp.int32, (HG_CHUNK, HG_CHUNK), 0)
    j = lax.broadcasted_iota(jnp.int32, (HG_CHUNK, HG_CHUNK), 1)
    return (i >= j) if lower else (i <= j)


def _head_mean(x):
    parts = []
    for hd in range(x.shape[1] // HEAD_DIM):
        m = jnp.mean(x[:, hd * HEAD_DIM:(hd + 1) * HEAD_DIM], axis=-1, keepdims=True)
        parts.append(jnp.broadcast_to(m, (x.shape[0], HEAD_DIM)))
    return jnp.concatenate(parts, axis=1)


def _hg_chunk_terms(qh, fh, lb):
    sig = _sigmoid(fh)
    f = lb + (1.0 - lb) * sig
    k = 1.0 - f
    b = _dot_exact(_tri(True).astype(F32), jnp.log(f))
    bl = b[HG_CHUNK - 1:HG_CHUNK, :]
    br = b[HG_MID:HG_MID + 1, :]
    sq = _sigmoid(qh)
    q = qh * sq
    return dict(sig=sig, f=f, k=k, b=b, bl=bl, br=br, sq=sq, q=q,
                e1=jnp.exp(bl - b), e2=jnp.exp(b), e3=jnp.exp(b - br), e4=jnp.exp(br - b))


def _hg_fwd(proj, lbw, normw, name, carried=()):
    def in_blks(off):
        return [pl.BlockSpec((HG_CHUNK, HG_BLOCK_W), lambda hp, n, b=b: (n, HG_Q_BLK + off + hp * HG_BLOCKS + b))
                for b in range(HG_BLOCKS)]

    def kern(*refs):
        q_refs, f_refs, i_refs, g_refs = (refs[k * HG_BLOCKS:(k + 1) * HG_BLOCKS] for k in range(4))
        lb_ref, nw_ref, oraw_ref, ohg_ref, st_ref, state = refs[4 * HG_BLOCKS:]

        @pl.when(pl.program_id(1) == 0)
        def _():
            state[...] = jnp.zeros_like(state)

        causal = _tri(True)
        wide = lambda rs: jnp.concatenate([r[...] for r in rs], axis=1)
        t = _hg_chunk_terms(wide(q_refs), wide(f_refs), _lower_bound(lb_ref, slice(None)))
        v, gh = wide(i_refs), wide(g_refs)
        kd, qb, qr, kr = t["k"] * t["e1"], t["q"] * t["e2"], t["q"] * t["e3"], t["k"] * t["e4"]
        decay = jnp.exp(t["bl"])
        outs = []
        for hd in range(HG_HEADS_PER_STEP):
            sl = slice(hd * HEAD_DIM, (hd + 1) * HEAD_DIM)
            st = state[hd]
            st_ref[0, hd] = st
            a = jnp.where(causal, _dot_nt(qr[:, sl], kr[:, sl]), 0.0)
            outs.append(_dot_nt(qb[:, sl], st) + _dot(a, v[:, sl]))
            state[hd] = st * decay[:, sl] + _dot_tn(v[:, sl], kd[:, sl])
        o = jnp.concatenate(outs, axis=1)
        oraw_ref[...] = o
        r = lax.rsqrt(_head_mean(o * o) + RMS_EPS)
        nw = jnp.tile(nw_ref[...], (1, HG_HEADS_PER_STEP))
        ohg_ref[...] = (o * r * nw * (gh * _sigmoid(gh))).astype(ohg_ref.dtype)

    out_blk = pl.BlockSpec((HG_CHUNK, HG_STEP_W), lambda hp, n: (n, hp))
    return _pcall(
        kern, name=name, grid=(HG_HEADS // HG_HEADS_PER_STEP, HG_N_CHUNKS),
        in_specs=[*in_blks(0), *in_blks(2), *in_blks(4), *in_blks(6),
                  pl.BlockSpec((2, HG_STEP_W), lambda hp, n: (0, hp)),
                  pl.BlockSpec((1, HEAD_DIM), lambda hp, n: (0, 0))],
        out_specs=[out_blk, out_blk,
                   pl.BlockSpec((1, HG_HEADS_PER_STEP, HEAD_DIM, HEAD_DIM), lambda hp, n: (n, hp, 0, 0))],
        out_shape=[jax.ShapeDtypeStruct((SEQ, HG_WIDTH), F32), jax.ShapeDtypeStruct((SEQ, HG_WIDTH), BF16),
                   jax.ShapeDtypeStruct((HG_N_CHUNKS, HG_HEADS, HEAD_DIM, HEAD_DIM), F32)],
        args=(*[proj] * (4 * HG_BLOCKS), lbw, normw),
        scratch_shapes=[pltpu.VMEM((HG_HEADS_PER_STEP, HEAD_DIM, HEAD_DIM), F32)],
        semantics=("parallel", "arbitrary"), carried=carried)


def _hg_bwd(proj, lbw, normw, oraw, states, do_hg, name, carried=()):
    last = HG_N_CHUNKS - 1

    def in_blks(off):
        return [pl.BlockSpec((HG_CHUNK, HG_BLOCK_W),
                             lambda hp, n, b=b: (last - n, HG_Q_BLK + off + hp * HG_BLOCKS + b))
                for b in range(HG_BLOCKS)]

    blk = pl.BlockSpec((HG_CHUNK, HG_STEP_W), lambda hp, n: (last - n, hp))

    def kern(*refs):
        q_refs, f_refs, i_refs, g_refs = (refs[k * HG_BLOCKS:(k + 1) * HG_BLOCKS] for k in range(4))
        (lb_ref, nw_ref, oraw_ref, st_ref, do_ref, dq_ref, df_ref, di_ref, dg_ref, dlb_ref, dnw_ref,
         dstate) = refs[4 * HG_BLOCKS:]
        first = pl.program_id(1) == 0

        @pl.when(first)
        def _():
            dstate[...] = jnp.zeros_like(dstate)

        causal = _tri(True)
        wide = lambda rs: jnp.concatenate([r[...] for r in rs], axis=1)
        cat = lambda parts: jnp.concatenate(parts, axis=1)
        qh, fh, v, gh = wide(q_refs), wide(f_refs), wide(i_refs), wide(g_refs)
        o, dout = oraw_ref[...], do_ref[...]
        nw = jnp.tile(nw_ref[...], (1, HG_HEADS_PER_STEP))
        sgg = _sigmoid(gh)
        r = lax.rsqrt(_head_mean(o * o) + RMS_EPS)
        xn = o * r
        dg_ref[...] = (dout * xn * nw * (sgg * (1.0 + gh * (1.0 - sgg)))).astype(dg_ref.dtype)
        don = dout * (gh * sgg)
        dnw_wide = jnp.sum(don * xn, axis=0, keepdims=True)
        dnw_tot = dnw_wide[:, :HEAD_DIM]
        for hd in range(1, HG_HEADS_PER_STEP):
            dnw_tot = dnw_tot + dnw_wide[:, hd * HEAD_DIM:(hd + 1) * HEAD_DIM]
        tt = don * nw
        do = r * (tt - xn * _head_mean(tt * xn))
        lb = _lower_bound(lb_ref, slice(None))
        t = _hg_chunk_terms(qh, fh, lb)
        k, q = t["k"], t["q"]
        kd, qb, qr, kr = k * t["e1"], q * t["e2"], q * t["e3"], k * t["e4"]
        decay = jnp.exp(t["bl"])
        dqb, dqr, dkr, dkd, dv, ddecay = [], [], [], [], [], []
        for hd in range(HG_HEADS_PER_STEP):
            sl = slice(hd * HEAD_DIM, (hd + 1) * HEAD_DIM)
            st = st_ref[0, hd]
            dstn = dstate[hd]
            a = jnp.where(causal, _dot_nt(qr[:, sl], kr[:, sl]), 0.0)
            da = jnp.where(causal, _dot_nt(do[:, sl], v[:, sl]), 0.0)
            dqb.append(_dot(do[:, sl], st))
            dv.append(_dot_tn(a, do[:, sl]) + _dot_nt(kd[:, sl], dstn))
            dqr.append(_dot(da, kr[:, sl]))
            dkr.append(_dot_tn(da, qr[:, sl]))
            dkd.append(_dot(v[:, sl], dstn))
            ddecay.append(jnp.sum(dstn * st, axis=0, keepdims=True))
            dstate[hd] = dstn * decay[:, sl] + _dot_tn(do[:, sl], qb[:, sl])
        dqb, dqr, dkr, dkd, dv, ddecay = cat(dqb), cat(dqr), cat(dkr), cat(dkd), cat(dv), cat(ddecay)
        dq = dqb * t["e2"] + dqr * t["e3"]
        dk = dkd * t["e1"] + dkr * t["e4"]
        db = dqb * qb + dqr * qr - dkr * kr - dkd * kd
        dbl = jnp.sum(dkd * kd, axis=0, keepdims=True) + ddecay * decay
        dbr = jnp.sum(dkr * kr - dqr * qr, axis=0, keepdims=True)
        rows = lax.broadcasted_iota(jnp.int32, db.shape, 0)
        dlf = _dot_exact(_tri(False).astype(F32), db) + dbl + jnp.where(rows <= HG_MID, dbr, 0.0)
        df = dlf / t["f"] - dk
        sig, sq = t["sig"], t["sq"]
        df_ref[...] = (df * (1.0 - lb) * sig * (1.0 - sig)).astype(df_ref.dtype)
        dlb_row = jnp.sum(df * (1.0 - sig), axis=0, keepdims=True)
        dq_ref[...] = (dq * (sq * (1.0 + qh * (1.0 - sq)))).astype(dq_ref.dtype)
        di_ref[...] = dv.astype(di_ref.dtype)
        dnw_blk = jnp.broadcast_to(dnw_tot, (8, HEAD_DIM))

        @pl.when(first)
        def _():
            dlb_ref[...] = dlb_row
            dnw_ref[...] = dnw_blk

        @pl.when(jnp.logical_not(first))
        def _():
            dlb_ref[...] += dlb_row
            dnw_ref[...] += dnw_blk

    n_hp = HG_HEADS // HG_HEADS_PER_STEP
    outs, cres = _pcall(
        kern, name=name, grid=(n_hp, HG_N_CHUNKS),
        in_specs=[*in_blks(0), *in_blks(2), *in_blks(4), *in_blks(6),
                  pl.BlockSpec((2, HG_STEP_W), lambda hp, n: (0, hp)),
                  pl.BlockSpec((1, HEAD_DIM), lambda hp, n: (0, 0)),
                  blk,
                  pl.BlockSpec((1, HG_HEADS_PER_STEP, HEAD_DIM, HEAD_DIM), lambda hp, n: (last - n, hp, 0, 0)),
                  blk],
        out_specs=[blk, blk, blk, blk,
                   pl.BlockSpec((1, HG_STEP_W), lambda hp, n: (0, hp)),
                   pl.BlockSpec((8, HEAD_DIM), lambda hp, n: (hp, 0))],
        out_shape=[jax.ShapeDtypeStruct((SEQ, HG_WIDTH), BF16)] * 4
        + [jax.ShapeDtypeStruct((1, HG_WIDTH), F32), jax.ShapeDtypeStruct((8 * n_hp, HEAD_DIM), F32)],
        args=(*[proj] * (4 * HG_BLOCKS), lbw, normw, oraw, states, do_hg),
        scratch_shapes=[pltpu.VMEM((HG_HEADS_PER_STEP, HEAD_DIM, HEAD_DIM), F32)],
        semantics=("parallel", "arbitrary"), carried=carried)
    dqh, dfh, dih, dgh, dlb, dnw = outs
    return (dqh, dfh, dih, dgh, dlb, [dnw[8 * i:8 * i + 1] for i in range(n_hp)]), cres


GATE_BLOCK_W = 512
GATE_A_BLK = (3 * ATT_WIDTH + 4 * HG_WIDTH) // GATE_BLOCK_W
GATE_B_BLK = GATE_A_BLK + D_MODEL // GATE_BLOCK_W


GATE_TILE = (1024, GATE_BLOCK_W)


def _gate_ins(proj, ya, yb=None):
    ins = [(proj, GATE_TILE, lambda i, j: (i, GATE_A_BLK + j)), (proj, GATE_TILE, lambda i, j: (i, GATE_B_BLK + j)),
           (ya, GATE_TILE)]
    return ins + ([(yb, GATE_TILE)] if yb is not None else [])


def _branch_b_gate(o_hg, w_b, proj, ya, name, carried=()):
    def post(yb, ga, gb, ya_):
        return yb, _sigmoid(ga) * ya_ + _sigmoid(gb) * yb

    return _mm(o_hg, w_b, mode="nn", out_dtype=F32, name=name, carried=carried, fused=dict(
        tile=GATE_TILE, ins=_gate_ins(proj, ya), post=post,
        outs=[(jax.ShapeDtypeStruct((SEQ, D_MODEL), F32), GATE_TILE), (jax.ShapeDtypeStruct((SEQ, D_MODEL), BF16), GATE_TILE)]))


def _dmerged_gate_bwd(dx1, w_out, proj, ya, yb, name, carried=()):
    def post(dm, ga, gb, ya_, yb_):
        sa, sb = _sigmoid(ga), _sigmoid(gb)
        return dm * sa, dm * sb, dm * ya_ * sa * (1.0 - sa), dm * yb_ * sb * (1.0 - sb)

    return _mm(dx1, w_out, mode="nt", out_dtype=BF16, name=name, carried=carried, fused=dict(
        tile=GATE_TILE, ins=_gate_ins(proj, ya, yb), post=post,
        outs=[(jax.ShapeDtypeStruct((SEQ, D_MODEL), BF16), GATE_TILE)] * 4))


NORM_TILE = (512, D_MODEL)


def _residual_norm(a, w, x, g, name, carried=()):
    def post(part, x_, g_):
        xn = part + x_
        r = lax.rsqrt(jnp.mean(xn * xn, axis=-1, keepdims=True) + RMS_EPS)
        return xn, xn * r * g_

    return _mm(a, w, mode="nn", out_dtype=F32, name=name, carried=carried, fused=dict(
        tile=NORM_TILE, ins=[(x, NORM_TILE), (g, (1, D_MODEL), lambda i, j: (0, j))], post=post,
        outs=[(jax.ShapeDtypeStruct((SEQ, D_MODEL), F32), NORM_TILE), (jax.ShapeDtypeStruct((SEQ, D_MODEL), BF16), NORM_TILE)]))


def _norm_bwd_residual(dq, w, x, g, res, name, carried=()):
    def post(dh, x_, g_, res_):
        r = lax.rsqrt(jnp.mean(x_ * x_, axis=-1, keepdims=True) + RMS_EPS)
        xn = x_ * r
        dhg = dh * g_
        dx = r * (dhg - xn * jnp.mean(dhg * xn, axis=-1, keepdims=True)) + res_
        return dx, jnp.sum(dh * xn, axis=0, keepdims=True), dx

    vec = ((1, D_MODEL), lambda i, j: (0, j))
    return _mm(dq, w, mode="nt", out_dtype=F32, name=name, carried=carried, fused=dict(
        tile=NORM_TILE, ins=[(x, NORM_TILE), (g, *vec), (res, NORM_TILE)], post=post,
        outs=[(jax.ShapeDtypeStruct((SEQ, D_MODEL), F32), NORM_TILE), (jax.ShapeDtypeStruct((1, D_MODEL), F32), *vec),
              (jax.ShapeDtypeStruct((SEQ, D_MODEL), BF16), NORM_TILE)]))


FF_SHARD = D_FF // N_CHIPS


FF_TILE_ROWS = 512


def _swiglu_tile(ab):
    a, b = ab[:, :FF_SHARD], ab[:, FF_SHARD:]
    return a * _sigmoid(a) * b


def _swiglu_grad_tile(du, ab):
    a, b = ab[:, :FF_SHARD], ab[:, FF_SHARD:]
    sg = _sigmoid(a)
    return jnp.concatenate([du * b * (sg * (1.0 + a * (1.0 - sg))), du * (a * sg)], axis=1)


def _ff_up(hf, w13, name, carried=()):
    wide, narrow = (FF_TILE_ROWS, 2 * FF_SHARD), (FF_TILE_ROWS, FF_SHARD)
    return _mm(hf, w13, mode="nn", out_dtype=F32, name=name, carried=carried, fused=dict(
        tile=wide, ins=[],
        outs=[(jax.ShapeDtypeStruct((SEQ, 2 * D_FF), F32), wide), (jax.ShapeDtypeStruct((SEQ, D_FF), BF16), narrow)],
        post=lambda p: (p, _swiglu_tile(p))))


def _ff_down_bwd(dx3, w2, ab, name, carried=()):
    wide, narrow = (FF_TILE_ROWS, 2 * FF_SHARD), (FF_TILE_ROWS, FF_SHARD)
    out, res = _mm(dx3, w2, mode="nt", out_dtype=BF16, name=name, carried=carried, fused=dict(
        tile=narrow, ins=[(ab, wide)], outs=[(jax.ShapeDtypeStruct((SEQ, 2 * D_FF), BF16), wide)],
        post=lambda du, ab_: (_swiglu_grad_tile(du, ab_),)))
    return out[0], res


CROSS_ROWS = 512


def _cross_fwd(qc, kvc, name, carried=()):
    def kern(q_ref, k_ref, v_ref, o_ref):
        s = _dot_nt(q_ref[...], k_ref[...]) * ATT_SCALE
        m = jnp.max(s, axis=-1, keepdims=True)
        e = jnp.exp(s - m)
        p = e / jnp.sum(e, axis=-1, keepdims=True)
        o_ref[...] = _dot(p, v_ref[...]).astype(o_ref.dtype)

    qblk = pl.BlockSpec((CROSS_ROWS, HEAD_DIM), lambda h, i: (i, h))
    return _pcall(
        kern, name=name, grid=(CROSS_HEADS, SEQ // CROSS_ROWS),
        in_specs=[qblk, pl.BlockSpec((MEM_LEN, HEAD_DIM), lambda h, i: (0, h)),
                  pl.BlockSpec((MEM_LEN, HEAD_DIM), lambda h, i: (0, CROSS_HEADS + h))],
        out_specs=qblk, out_shape=jax.ShapeDtypeStruct((SEQ, CROSS_WIDTH), BF16), args=(qc, kvc, kvc),
        semantics=("parallel", "parallel"), carried=carried)


def _cross_bwd(qc, kvc, doc, name):
    def kern(q_ref, k_ref, v_ref, do_ref, dq_ref, dk_ref, dv_ref):
        q, k, v, do = q_ref[...], k_ref[...], v_ref[...], do_ref[...]
        s = _dot_nt(q, k) * ATT_SCALE
        m = jnp.max(s, axis=-1, keepdims=True)
        e = jnp.exp(s - m)
        p = e / jnp.sum(e, axis=-1, keepdims=True)
        dp = _dot_nt(do, v)
        ds = p * (dp - jnp.sum(dp * p, axis=-1, keepdims=True))
        dq_ref[...] = (_dot(ds, k) * ATT_SCALE).astype(dq_ref.dtype)
        dk = _dot_tn(ds, q) * ATT_SCALE
        dv = _dot_tn(p, do)

        @pl.when(pl.program_id(1) == 0)
        def _():
            dk_ref[...] = dk
            dv_ref[...] = dv

        @pl.when(pl.program_id(1) > 0)
        def _():
            dk_ref[...] += dk
            dv_ref[...] += dv

    qblk = pl.BlockSpec((CROSS_ROWS, HEAD_DIM), lambda h, i: (i, h))
    kblk = pl.BlockSpec((MEM_LEN, HEAD_DIM), lambda h, i: (0, h))
    dq, dk, dv = pl.pallas_call(
        kern, name=name, grid=(CROSS_HEADS, SEQ // CROSS_ROWS),
        in_specs=[qblk, kblk, pl.BlockSpec((MEM_LEN, HEAD_DIM), lambda h, i: (0, CROSS_HEADS + h)), qblk],
        out_specs=[qblk, kblk, kblk],
        out_shape=[jax.ShapeDtypeStruct((SEQ, CROSS_WIDTH), BF16),
                   jax.ShapeDtypeStruct((MEM_LEN, CROSS_WIDTH), F32),
                   jax.ShapeDtypeStruct((MEM_LEN, CROSS_WIDTH), F32)],
        compiler_params=_cparams(("parallel", "arbitrary")),
    )(qc, kvc, kvc, doc)
    return dq, jnp.concatenate([dk, dv], axis=1)


FULL_SPECS = {
    "w_in": ("col", D_MODEL, IN_WIDTH),
    "w_branch_a": ("col", ATT_OUT, D_MODEL),
    "w_branch_b": ("col", HG_WIDTH, D_MODEL),
    "w_out": ("row", D_MODEL, D_MODEL),
    "wq_cross": ("row", D_MODEL, CROSS_WIDTH),
    "wkv_cross": ("row", D_MODEL, 2 * CROSS_WIDTH),
    "wo_cross": ("col", CROSS_WIDTH, D_MODEL),
    "w13": ("col", D_MODEL, 2 * D_FF),
    "w2": ("row", D_FF, D_MODEL),
}
WEIGHT_PLACE = {
    "w_in": ("w_in", 0), "w_branch_a": ("w_branch_a", 0), "w_branch_b": ("w_branch_b", 0),
    "w_out": ("w_out", 0), "wq_cross": ("wq_cross", 0), "wkv_cross": ("wkv_cross", 0),
    "wo_cross": ("wo_cross", 0), "w1": ("w13", 0), "w3": ("w13", FF_SHARD), "w2": ("w2", 0),
}
BIG_WEIGHTS = tuple(WEIGHT_PLACE)
EW_BLOCK_ELEMS = 512 * 1024


def _position():
    return lax.axis_index("x"), lax.axis_index("y"), lax.axis_index("c")


def _other_chips(x, y):
    return [(1 - x, y), (x, 1 - y), (1 - x, 1 - y)]


def _half(ref, kind, h):
    r, c = ref.shape
    if kind == "col":
        return ref.at[pl.ds(h * (r // 2), r // 2), :]
    return ref.at[:, pl.ds(h * (c // 2), c // 2)]


def _shard_of(ref, kind, start, size):
    return ref.at[:, pl.ds(start, size)] if kind == "col" else ref.at[pl.ds(start, size), :]


def _rows_of(ref, r0, nrows):
    return ref if nrows is None else ref.at[pl.ds(r0, nrows), :]


def _half_shape(kind, rows, cols):
    return (rows // 2, cols) if kind == "col" else (rows, cols // 2)


def _slot_shape(spec):
    kind, rows, cols = spec
    hr, hc = _half_shape(kind, rows, cols)
    return (hr, hc // N_CHIPS) if kind == "col" else (hr // N_CHIPS, hc)


def _remote(src, dst, send_sem, recv_sem, device):
    return pltpu.make_async_remote_copy(src_ref=src, dst_ref=dst, send_sem=send_sem, recv_sem=recv_sem,
                                        device_id=device, device_id_type=MESH)


def _gather_ici_comm(fulls, jobs, specs):
    def piece(refs, job, chip, c):
        f, r0, nr = job
        kind, rows, cols = specs[f]
        stride = (cols if kind == "col" else rows) // N_CHIPS
        return _rows_of(_half(_shard_of(refs[f], kind, chip * stride, stride), kind, c), r0, nr)

    def start(refs, ss, rs):
        x, y, c = _position()
        j = 2 * x + y
        for q, job in enumerate(jobs):
            for p, (px, py) in enumerate(_other_chips(x, y)):
                _remote(piece(refs, job, j, c), piece(refs, job, j, c), ss.at[3 * q + p], rs.at[3 * q + p],
                        (px, py, c)).start()

    def finish(refs, ss, rs):
        x, y, c = _position()
        j = 2 * x + y
        for q, job in enumerate(jobs):
            for p, (px, py) in enumerate(_other_chips(x, y)):
                _remote(piece(refs, job, j, c), piece(refs, job, 2 * px + py, c), ss.at[3 * q + p],
                        rs.at[3 * q + p], (px, py, c)).wait_recv()
        for q, job in enumerate(jobs):
            for p, (px, py) in enumerate(_other_chips(x, y)):
                _remote(piece(refs, job, j, c), piece(refs, job, j, c), ss.at[3 * q + p], rs.at[3 * q + p],
                        (px, py, c)).wait_send()

    names = list(dict.fromkeys(job[0] for job in jobs))
    return _Carried({f: fulls[f] for f in names}, {}, 3 * len(jobs), start, finish)


def _gather_ring_comm(fulls, f, r0, nr, phase, specs):
    kind, _, cols = specs[f]
    assert kind == "col" and nr % 32 == 0
    stride = cols // N_CHIPS
    half = nr // 2

    def rows(refs, chip, c, lo, n):
        return _rows_of(_half(_shard_of(refs[f], kind, chip * stride, stride), kind, c), r0 + lo, n)

    def copies(refs, ss, rs):
        x, y, c = _position()
        me, nx, ny, dg = 2 * x + y, 2 * (1 - x) + y, 2 * x + (1 - y), 2 * (1 - x) + (1 - y)
        to_x, to_y = (1 - x, y, c), (x, 1 - y, c)
        if phase == "a":
            mine = rows(refs, me, c, 0, nr)
            return [(_remote(mine, mine, ss.at[0], rs.at[0], to_x), rows(refs, nx, c, 0, nr)),
                    (_remote(mine, mine, ss.at[1], rs.at[1], to_y), rows(refs, ny, c, 0, nr))]
        up, low = rows(refs, ny, c, half, half), rows(refs, nx, c, 0, half)
        return [(_remote(up, up, ss.at[0], rs.at[0], to_x), rows(refs, dg, c, half, half)),
                (_remote(low, low, ss.at[1], rs.at[1], to_y), rows(refs, dg, c, 0, half))]

    def start(refs, ss, rs):
        for cp, _ in copies(refs, ss, rs):
            cp.start()

    def finish(refs, ss, rs):
        x, y, c = _position()
        mine = copies(refs, ss, rs)
        for i, (_, landing) in enumerate(mine):
            _remote(landing, landing, ss.at[i], rs.at[i], (x, y, c)).wait_recv()
        for cp, _ in mine:
            cp.wait_send()

    return _Carried({f: fulls[f]}, {}, 2, start, finish)


def _gather_d2d_comm(fulls, jobs, specs):
    def rect(refs, job, h):
        f, r0, nr = job
        assert nr is None or specs[f][0] == "col"
        return _rows_of(_half(refs[f], specs[f][0], h), r0, nr)

    def start(refs, ss, rs):
        x, y, c = _position()
        for q, job in enumerate(jobs):
            _remote(rect(refs, job, c), rect(refs, job, c), ss.at[q], rs.at[q], (x, y, 1 - c)).start()

    def finish(refs, ss, rs):
        x, y, c = _position()
        for q, job in enumerate(jobs):
            _remote(rect(refs, job, 1 - c), rect(refs, job, 1 - c), ss.at[q], rs.at[q], (x, y, 1 - c)).wait_recv()
        for q, job in enumerate(jobs):
            _remote(rect(refs, job, c), rect(refs, job, c), ss.at[q], rs.at[q], (x, y, 1 - c)).wait_send()

    names = list(dict.fromkeys(job[0] for job in jobs))
    return _Carried({f: fulls[f] for f in names}, {}, len(jobs), start, finish)


def _pairx_comm(grads, names, specs, whole=False):
    def copies(refs, ss, rs):
        x, y, c = _position()
        src = (lambda f: refs[("g", f)]) if whole else (lambda f: _half(refs[("g", f)], specs[f][0], 1 - c))
        return [_remote(src(f), refs[("r", f)], ss.at[i], rs.at[i], (x, y, 1 - c)) for i, f in enumerate(names)]

    def start(refs, ss, rs):
        for cp in copies(refs, ss, rs):
            cp.start()

    def finish(refs, ss, rs):
        for cp in copies(refs, ss, rs):
            cp.wait_recv()
        for cp in copies(refs, ss, rs):
            cp.wait_send()

    fresh = {("r", f): jax.ShapeDtypeStruct(_half_shape(*specs[f]), BF16) for f in names}
    return _Carried({}, fresh, len(names), start, finish, reads={("g", f): grads[f] for f in names})


def _chipx_comm(pair_sums, slots, jobs, specs):
    def copies(refs, ss, rs):
        x, y, c = _position()
        out = []
        for q, (f, r0, nr) in enumerate(jobs):
            kind = specs[f][0]
            width = _slot_shape(specs[f])[1 if kind == "col" else 0]
            for p, (px, py) in enumerate(_other_chips(x, y)):
                src = _rows_of(_shard_of(refs[("p", f)], kind, (2 * px + py) * width, width), r0, nr)
                dst = _rows_of(refs[("s", f)].at[p], r0, nr)
                out.append(_remote(src, dst, ss.at[3 * q + p], rs.at[3 * q + p], (px, py, c)))
        return out

    def start(refs, ss, rs):
        for cp in copies(refs, ss, rs):
            cp.start()

    def finish(refs, ss, rs):
        for cp in copies(refs, ss, rs):
            cp.wait_recv()
        for cp in copies(refs, ss, rs):
            cp.wait_send()

    names = list(dict.fromkeys(job[0] for job in jobs))
    arrays = {("p", f): pair_sums[f] for f in names}
    arrays.update({("s", f): slots[f] for f in names})
    return _Carried(arrays, {}, 3 * len(jobs), start, finish)


def _share_comm(grads, wnames, specs, place):
    def start(refs, ss, rs):
        x, y, c = _position()
        for i, w in enumerate(wnames):
            kind = specs[place[w][0]][0]
            _remote(_half(refs[w], kind, c), _half(refs[w], kind, c), ss.at[i], rs.at[i], (x, y, 1 - c)).start()

    def finish(refs, ss, rs):
        x, y, c = _position()
        for i, w in enumerate(wnames):
            kind = specs[place[w][0]][0]
            _remote(_half(refs[w], kind, 1 - c), _half(refs[w], kind, 1 - c), ss.at[i], rs.at[i],
                    (x, y, 1 - c)).wait_recv()
        for i, w in enumerate(wnames):
            kind = specs[place[w][0]][0]
            _remote(_half(refs[w], kind, c), _half(refs[w], kind, c), ss.at[i], rs.at[i], (x, y, 1 - c)).wait_send()

    return _Carried({w: grads[w] for w in wnames}, {}, len(wnames), start, finish)


def _gather_rows(v, name="gather_small"):
    shape = v.shape

    def body(v_ref, out_ref, send_sem, recv_sem, loc_sem):
        x, y, c = _position()
        me = 4 * x + 2 * y + c
        flips = [(fx, fy, fc) for fx in (0, 1) for fy in (0, 1) for fc in (0, 1)][1:]

        def peer(fl):
            return tuple(1 - a if f else a for a, f in zip((x, y, c), fl))

        loc = pltpu.make_async_copy(v_ref, out_ref.at[me], loc_sem)
        loc.start()
        sends = []
        for i, fl in enumerate(flips):
            cp = _remote(v_ref, out_ref.at[me], send_sem.at[i], recv_sem.at[i], peer(fl))
            cp.start()
            sends.append(cp)
        for i, fl in enumerate(flips):
            px, py, pc = peer(fl)
            _remote(v_ref, out_ref.at[4 * px + 2 * py + pc], send_sem.at[i], recv_sem.at[i], peer(fl)).wait_recv()
        for cp in sends:
            cp.wait_send()
        loc.wait()

    return pl.pallas_call(
        body, name=name, in_specs=[ANY], out_specs=ANY,
        out_shape=jax.ShapeDtypeStruct((N_DEV,) + shape, F32),
        scratch_shapes=[pltpu.SemaphoreType.DMA((N_DEV - 1,)), pltpu.SemaphoreType.DMA((N_DEV - 1,)),
                        pltpu.SemaphoreType.DMA],
    )(v)


def _ew_block(rows, cols, elems=EW_BLOCK_ELEMS):
    tc = cols if cols <= 4096 else _div(cols, 2048, LANE)
    tr = _div(rows, max(16, elems // tc), 16)
    return tr, tc


def _mesh_scalars():
    x, y, c = _position()
    return jnp.stack([c, 2 * x + y]).astype(jnp.int32)


def _grid_spec(grid, in_specs, out_specs):
    return pltpu.PrefetchScalarGridSpec(num_scalar_prefetch=1, grid=grid, in_specs=in_specs, out_specs=out_specs)


def _cast_into_full(parts, fname, pos, specs, place, name, token=None):
    kind, rows, cols = specs[fname]
    ws = [w for w in place if place[w][0] == fname]
    if kind == "col":
        stride = cols // N_CHIPS
        hr = rows // 2
        tr = _div(hr, max(16, EW_BLOCK_ELEMS // stride), 16)
        nrb = hr // tr
        in_specs = [pl.BlockSpec((tr, parts[w].shape[1]), lambda i, pos_ref: (i + pos_ref[0] * nrb, 0)) for w in ws]
        out_spec = pl.BlockSpec((tr, stride), lambda i, pos_ref: (i + pos_ref[0] * nrb, pos_ref[1]))
    else:
        stride = rows // N_CHIPS
        hc = cols // 2
        tr = _div(stride, max(16, EW_BLOCK_ELEMS // hc), 16)
        nrb = stride // tr
        in_specs = [pl.BlockSpec((tr, hc), lambda i, pos_ref: (i, pos_ref[0])) for w in ws]
        out_spec = pl.BlockSpec((tr, hc), lambda i, pos_ref: (i + pos_ref[1] * nrb, pos_ref[0]))

    def kern(pos_ref, *refs):
        o_ref = refs[-1]
        for w, r in zip(ws, refs[:len(ws)]):
            off = place[w][1] if kind == "col" else 0
            o_ref[:, off:off + r.shape[1]] = r[...].astype(o_ref.dtype)

    tokens = [] if token is None else [token]
    in_specs = in_specs + [pl.BlockSpec(TOKEN_SHAPE, lambda i, pos_ref: (0, 0))] * len(tokens)
    return pl.pallas_call(
        kern, name=name, grid_spec=_grid_spec((nrb,), in_specs, out_spec),
        out_shape=jax.ShapeDtypeStruct((rows, cols), BF16),
        compiler_params=_cparams(("parallel",)),
    )(pos, *[parts[w] for w in ws], *tokens)


def _pair_sum(grad, recv, pos, spec, name, whole=False):
    kind, rows, cols = spec
    hr, hc = _half_shape(kind, rows, cols)
    tr, tc = _ew_block(hr, hc, 2 * EW_BLOCK_ELEMS)
    nrb, ncb = hr // tr, hc // tc
    blk = pl.BlockSpec((tr, tc), lambda i, jj, pos_ref: (i, jj))
    if whole:
        mine = blk
    elif kind == "col":
        mine = pl.BlockSpec((tr, tc), lambda i, jj, pos_ref: (i + pos_ref[0] * nrb, jj))
    else:
        mine = pl.BlockSpec((tr, tc), lambda i, jj, pos_ref: (i, jj + pos_ref[0] * ncb))

    def kern(pos_ref, g_ref, r_ref, o_ref, slots_ref):
        o_ref[...] = (g_ref[...].astype(F32) + r_ref[...].astype(F32)).astype(o_ref.dtype)

    return pl.pallas_call(
        kern, name=name, grid_spec=_grid_spec((nrb, ncb), [mine, blk], [blk, ANY]),
        out_shape=[jax.ShapeDtypeStruct((hr, hc), BF16),
                   jax.ShapeDtypeStruct((N_CHIPS - 1,) + _slot_shape(spec), BF16)],
        compiler_params=_cparams(("parallel", "parallel")),
    )(pos, grad, recv)


def _chip_sum(pair_sum, slots, pos, fname, shard_shapes, specs, place, name):
    kind, rows, cols = specs[fname]
    sr, sc = _slot_shape(specs[fname])
    ws = [w for w in place if place[w][0] == fname]
    n_slots = N_CHIPS - 1
    tr = _div(sr, max(16, EW_BLOCK_ELEMS // sc), 16)
    nrb = sr // tr
    slot = pl.BlockSpec((n_slots, tr, sc), lambda i, pos_ref: (0, i, 0))
    if kind == "col":
        own = pl.BlockSpec((tr, sc), lambda i, pos_ref: (i, pos_ref[1]))
        out_specs = [pl.BlockSpec((tr, shard_shapes[w][1]), lambda i, pos_ref: (i + pos_ref[0] * nrb, 0)) for w in ws]
    else:
        own = pl.BlockSpec((tr, sc), lambda i, pos_ref: (i + pos_ref[1] * nrb, 0))
        out_specs = [pl.BlockSpec((tr, sc), lambda i, pos_ref: (i, pos_ref[0])) for w in ws]

    def kern(pos_ref, own_ref, slot_ref, *out_refs):
        tot = own_ref[...].astype(F32)
        for s in range(n_slots):
            tot = tot + slot_ref[s].astype(F32)
        for w, o_ref in zip(ws, out_refs):
            off = place[w][1] if kind == "col" else 0
            o_ref[...] = tot[:, off:off + o_ref.shape[1]]

    outs = pl.pallas_call(
        kern, name=name, grid_spec=_grid_spec((nrb,), [own, slot], out_specs),
        out_shape=[jax.ShapeDtypeStruct(shard_shapes[w], F32) for w in ws],
        compiler_params=_cparams(("parallel",)),
    )(pos, pair_sum, slots)
    return dict(zip(ws, outs))


def _adam_math(w, g, m, v):
    m2 = ADAM_B1 * m + (1.0 - ADAM_B1) * g
    v2 = ADAM_B2 * v + (1.0 - ADAM_B2) * (g * g)
    m_hat = m2 / (1.0 - ADAM_B1 ** ADAM_STEP)
    v_hat = v2 / (1.0 - ADAM_B2 ** ADAM_STEP)
    delta = -ADAM_LR * (m_hat / (jnp.sqrt(v_hat) + ADAM_EPS) + ADAM_WD * w)
    return delta, m2, v2


def _adamw(w, g, m, v, name, carried=()):
    rows, cols = w.shape
    tr, tc = _ew_block(rows, cols)

    def kern(w_ref, g_ref, m_ref, v_ref, d_ref, m2_ref, v2_ref, g_out_ref):
        g_ = g_ref[...]
        d_ref[...], m2_ref[...], v2_ref[...] = _adam_math(w_ref[...], g_, m_ref[...], v_ref[...])
        g_out_ref[...] = g_

    blk = pl.BlockSpec((tr, tc), lambda i, j: (i, j))
    return _pcall(
        kern, name=name, grid=(rows // tr, cols // tc), in_specs=[blk] * 4, out_specs=[blk] * 4,
        out_shape=[jax.ShapeDtypeStruct((rows, cols), F32)] * 4, args=(w, g, m, v),
        semantics=("parallel", "parallel"), carried=carried)


SMALL_ROWS = ("ln_mix_w", "ln_cross_w", "ln_mem_w", "ln_ffn_w", "ln_final_w")
ROW_HG_NORM, ROW_LB0, ROW_LB1 = 5, 6, 7
LOSS_LANE0 = HEAD_DIM


def _pack_small(vals):
    rows = [vals[n].reshape(1, D_MODEL) for n in SMALL_ROWS]
    pad = lambda a: jnp.pad(a, ((0, 0), (0, D_MODEL - a.shape[1])))
    rows.append(pad(vals["hg_norm_w"].reshape(1, HEAD_DIM)))
    rows.append(pad(vals["hg_lower_bounds"].reshape(2, HG_WIDTH)))
    return jnp.concatenate(rows, axis=0)


def _small_update(gathered, w, m, v, name="small_update"):
    def kern(g_ref, w_ref, m_ref, v_ref, grad_ref, d_ref, m2_ref, v2_ref, loss_ref):
        tot = g_ref[0]
        for s in range(1, N_DEV):
            tot = tot + g_ref[s]
        wv = w_ref[...]
        row = lax.broadcasted_iota(jnp.int32, (8, D_MODEL), 0)
        lane = lax.broadcasted_iota(jnp.int32, (8, D_MODEL), 1)
        l0, l1 = wv[ROW_LB0:ROW_LB0 + 1], wv[ROW_LB1:ROW_LB1 + 1]
        mx = jnp.maximum(l0, l1)
        e0, e1 = jnp.exp(l0 - mx), jnp.exp(l1 - mx)
        p0 = e0 / (e0 + e1)
        dlog = tot[ROW_LB0:ROW_LB0 + 1] * p0 * (1.0 - p0)
        tot = jnp.where(row == ROW_HG_NORM, tot + tot[ROW_LB1:ROW_LB1 + 1], tot)
        grad = jnp.where(row == ROW_LB0, dlog, jnp.where(row == ROW_LB1, -dlog, tot))
        grad = jnp.where((row == ROW_HG_NORM) & (lane >= HEAD_DIM), 0.0, grad)
        grad = jnp.where((row >= ROW_LB0) & (lane >= HG_WIDTH), 0.0, grad)
        grad_ref[...] = grad
        d_ref[...], m2_ref[...], v2_ref[...] = _adam_math(wv, grad, m_ref[...], v_ref[...])
        loss_ref[...] = tot[ROW_HG_NORM:ROW_HG_NORM + 1, LOSS_LANE0:LOSS_LANE0 + LANE]

    full = pl.BlockSpec((8, D_MODEL), lambda: (0, 0))
    return pl.pallas_call(
        kern, name=name,
        in_specs=[pl.BlockSpec((N_DEV, 8, D_MODEL), lambda: (0, 0, 0)), full, full, full],
        out_specs=[full, full, full, full, pl.BlockSpec((1, LANE), lambda: (0, 0))],
        out_shape=[jax.ShapeDtypeStruct((8, D_MODEL), F32)] * 4 + [jax.ShapeDtypeStruct((1, LANE), F32)],
        compiler_params=_cparams(),
    )(gathered, w, m, v)


def _unpack_small(p, shapes):
    out = {n: p[i].reshape(shapes[n]) for i, n in enumerate(SMALL_ROWS)}
    out["hg_norm_w"] = p[ROW_HG_NORM, :HEAD_DIM].reshape(shapes["hg_norm_w"])
    out["hg_lower_bounds"] = p[ROW_LB0:ROW_LB1 + 1, :HG_WIDTH].reshape(shapes["hg_lower_bounds"])
    return out


def _concat_cols(pieces, name):
    rows = pieces[0].shape[0]
    widths = [p.shape[1] for p in pieces]
    tr = ROW_BLOCK // 2

    def kern(*refs):
        o_ref, off = refs[-1], 0
        for r, w in zip(refs[:-1], widths):
            o_ref[:, off:off + w] = r[...]
            off += w

    return pl.pallas_call(
        kern, name=name, grid=(rows // tr,),
        in_specs=[pl.BlockSpec((tr, w), lambda i: (i, 0)) for w in widths],
        out_specs=pl.BlockSpec((tr, sum(widths)), lambda i: (i, 0)),
        out_shape=jax.ShapeDtypeStruct((rows, sum(widths)), pieces[0].dtype),
        compiler_params=_cparams(("parallel",)),
    )(*pieces)


WHOLE = lambda f: (f, 0, None)
MID_MATRICES = ("w_branch_a", "w_branch_b", "w_out", "wq_cross", "wkv_cross", "wo_cross")
MID_WEIGHTS = MID_MATRICES
W_IN_PIECES = [("w_in", r0, 512) for r0 in range(0, D_MODEL // 2, 512)]
W13_PIECES = [("w13", r0, 512) for r0 in range(0, D_MODEL // 2, 512)]
GATHER_GROUPS = [("mid", [WHOLE(f) for f in MID_MATRICES]), ("w13a", W13_PIECES[:1]), ("w13b", W13_PIECES[1:]),
                 ("w2", [WHOLE("w2")])]
OTHER_WEIGHTS = ["w1", "w3", "w2"] + list(MID_WEIGHTS)
BEFORE = {
    "hgrn_fwd": [("wait", "mid")],
    "mm_out": [("wait", "w13a")],
    "mm_o": [("wait", "w13b")],
    "mm_w2": [("wait", "w2"), ("run", ("d2d", [WHOLE("w2")]), "gather_hand_over_w2")],
    "mm_dh": [("wait", "rs_w2"), ("chip_sum", "w2"), ("wait", "rs_w13"), ("chip_sum", "w13"), ("wait", "rs_mid")]
    + [("chip_sum", f) for f in MID_MATRICES],
}
CARRY = {
    "hgrn_fwd": [("d2d", [WHOLE(f) for f in MID_MATRICES])],
    "mm_out": [("d2d", W13_PIECES[:1])],
    "mm_o": [("d2d", W13_PIECES[1:])],
    "mm_du": [("pairx", ["w2"])],
    "mm_dhf": [("pairx", ["w13"])],
    "attn_bwd_g0": [("pairx", list(MID_MATRICES))],
    "mm_dwin_own": [("pairx_whole", ["w_in"])],
    "mm_dh": [("share", OTHER_WEIGHTS)],
}
AFTER = {
    "mm_du": [("pair_sum", "w2"), ("start", "rs_w2", [WHOLE("w2")])],
    "mm_dhf": [("pair_sum", "w13"), ("start", "rs_w13", [WHOLE("w13")])],
    "attn_bwd_g0": [("pair_sum", f) for f in MID_MATRICES] + [("start", "rs_mid", [WHOLE(f) for f in MID_MATRICES])],
    "mm_dwin_own": [("pair_sum", "w_in"), ("start", "rs_w_in", [WHOLE("w_in")])],
}
FINISH = [
    ("adamw", OTHER_WEIGHTS), ("wait", "rs_w_in"), ("small",), ("chip_sum", "w_in"),
    ("run", ("share", ["w_in"]), "rs_sibling_share_w_in"), ("adamw", ["w_in"]),
]


class _Net:
    def __init__(self, full, pos=None, shard_shapes=None, comm=True, specs=FULL_SPECS, place=WEIGHT_PLACE):
        self.full, self.pos, self.shard_shapes, self.comm = dict(full), pos, shard_shapes, comm
        self.specs, self.place = specs, place
        self.gw, self.recv, self.psum, self.slots, self.grads = {}, {}, {}, {}, {}
        self.gw_sibling = {}
        self.pending, self.token, self.last = {}, None, None

    def _make(self, kind, arg):
        if kind == "gather":
            return _gather_ici_comm(self.full, arg, self.specs)
        if kind == "ring":
            return _gather_ring_comm(self.full, *arg, self.specs)
        if kind == "d2d":
            return _gather_d2d_comm(self.full, arg, self.specs)
        if kind == "pairx":
            return _pairx_comm(self.gw, arg, self.specs)
        if kind == "pairx_whole":
            return _pairx_comm(self.gw_sibling, arg, self.specs, whole=True)
        if kind == "chipx":
            return _chipx_comm(self.psum, self.slots, arg, self.specs)
        assert kind == "share"
        return _share_comm(self.grads, arg, self.specs, self.place)

    def _store(self, kind, res):
        if kind in ("gather", "ring", "d2d"):
            self.full.update(res)
        elif kind in ("pairx", "pairx_whole"):
            for (tag, f), a in res.items():
                (self.gw if tag == "g" else self.recv)[f] = a
        elif kind == "chipx":
            for (tag, f), a in res.items():
                (self.psum if tag == "p" else self.slots)[f] = a
        else:
            self.grads.update(res)

    def run_comm(self, item, name):
        kind, arg = item
        self._store(kind, _run_comm([self._make(kind, arg)], name)[0])

    @staticmethod
    def _others(after, items):
        own = [a for cm in items for a in cm.arrays.values()]
        return [a for a in after if a is not None and all(a is not o for o in own)]

    def start(self, groups, kind, name):
        items = [self._make(kind, jobs) for _, jobs in groups]
        after = self._others([self.last], items)
        res, sems, token = _split_start(items, name, after=after[0] if after else None)
        for (group, jobs), r, s in zip(groups, res, sems):
            self._store(kind, r)
            self.pending[group] = (kind, jobs, s)
        self.token = self.last = token

    def wait(self, group, after=()):
        kind, jobs, sems = self.pending.pop(group)
        item = self._make(kind, jobs)
        res = _split_wait([item], [sems], self._others([self.last, *after], [item]), f"wait_{group}")[0]
        self._store(kind, res)

    def step(self, step):
        if step[0] == "wait":
            self.wait(step[1])
        elif step[0] == "start":
            self.start([(step[1], step[2])], "chipx", f"start_{step[1]}")
        elif step[0] == "pair_sum":
            f = step[1]
            self.psum[f], self.slots[f] = _pair_sum(self.gw[f], self.recv[f], self.pos, self.specs[f],
                                                    f"rs_pair_sum_{f}", whole=f in self.gw_sibling)
        elif step[0] == "chip_sum":
            f = step[1]
            self.grads.update(_chip_sum(self.psum[f], self.slots[f], self.pos, f, self.shard_shapes,
                                        self.specs, self.place, f"rs_chip_sum_{f}"))
        else:
            assert step[0] == "run"
            self.run_comm(step[1], step[2])

    def call(self, fn, name, *args, grad_of=None, sibling_half=False, **kw):
        for step in (BEFORE.get(name, []) if self.comm else []):
            self.step(step)
        args = [a() if callable(a) else a for a in args]
        items = CARRY.get(name, []) if self.comm else []
        carried = [self._make(k, a) for k, a in items]
        if self.token is not None:
            carried.append(_Token(self.token))
            self.token = None
        out, res = fn(*args, name=name, carried=carried, **kw)
        if grad_of is not None:
            (self.gw_sibling if sibling_half else self.gw)[grad_of] = out
        self.last = jax.tree.leaves(out)[0]
        for (kind, _), r in zip(items, res):
            self._store(kind, r)
        for step in (AFTER.get(name, []) if self.comm else []):
            self.step(step)
        return out


def _local_step(net, x, h, mem, target, small, h_halves=None):
    full, call = net.full, net.call
    proj = call(_mm, "mm_proj", h, full["w_in"], mode="nn", out_dtype=F32)
    att = [call(_attn_fwd, f"attn_fwd_g{g}", proj, g) for g in range(3)]
    outs, lses = [a[0] for a in att], [a[1] for a in att]
    o_att = _attn_merge_fwd(outs, lses, "attn_merge")
    oraw, o_hg, states = call(_hg_fwd, "hgrn_fwd", proj, small["hg_lower_bounds"], small["hg_norm_w"])
    ya = call(_mm, "mm_branch_a", o_att, full["w_branch_a"], mode="nn", out_dtype=F32)
    yb, merged = call(_branch_b_gate, "mm_branch_b", o_hg, full["w_branch_b"], proj, ya)
    x1, hc = call(_residual_norm, "mm_out", merged, full["w_out"], x, small["ln_cross_w"])

    mn = _rms_fwd(mem, small["ln_mem_w"], "rms_mem")
    qc = call(_mm, "mm_q", hc, full["wq_cross"], mode="nn", out_dtype=F32)
    kvc = call(_mm, "mm_kv", mn, full["wkv_cross"], mode="nn", out_dtype=F32)
    oc = call(_cross_fwd, "cross_fwd", qc, kvc)
    x2, hf = call(_residual_norm, "mm_o", oc, full["wo_cross"], x1, small["ln_ffn_w"])

    ab, u = call(_ff_up, "mm_w13", hf, full["w13"])
    x3 = call(_mm, "mm_w2", u, lambda: full["w2"], mode="nn", out_dtype=F32, res=x2)

    dx3, dg_final, loss, dx3_bf16 = _loss_head(x3, small["ln_final_w"], target, "loss_head")

    gs = {"ln_final_w": dg_final}
    call(_mm, "mm_dw2", u, dx3_bf16, mode="tn", out_dtype=BF16, grad_of="w2")
    dab = call(_ff_down_bwd, "mm_du", dx3_bf16, full["w2"], ab)
    call(_mm, "mm_dw13", hf, dab, mode="tn", out_dtype=BF16, grad_of="w13")
    dhf = call(_mm, "mm_dhf", dab, full["w13"], mode="nt", out_dtype=F32)
    dx2, gs["ln_ffn_w"], dx2_bf16 = _rms_bwd(x2, small["ln_ffn_w"], dhf, dx3, "rms_ffn_bwd", bf16_copy=True)
    doc = call(_mm, "mm_doc", dx2_bf16, full["wo_cross"], mode="nt", out_dtype=BF16)
    call(_mm, "mm_dwo", oc, dx2_bf16, mode="tn", out_dtype=BF16, grad_of="wo_cross")
    dqc, dkvc = _cross_bwd(qc, kvc, doc, "cross_bwd")
    call(_mm, "mm_dwq", hc, dqc, mode="tn", out_dtype=BF16, grad_of="wq_cross")
    dx1, gs["ln_cross_w"], dx1_bf16 = call(_norm_bwd_residual, "mm_dhc", dqc, full["wq_cross"], x1,
                                           small["ln_cross_w"], dx2)
    call(_mm, "mm_dwkv", mn, dkvc, mode="tn", out_dtype=BF16, grad_of="wkv_cross")
    dmn = call(_mm, "mm_dmn", dkvc, full["wkv_cross"], mode="nt", out_dtype=F32)
    _, gs["ln_mem_w"] = _rms_bwd(mem, small["ln_mem_w"], dmn, None, "rms_mem_bwd")
    dya, dyb, dga, dgb = call(_dmerged_gate_bwd, "mm_dmerged", dx1_bf16, full["w_out"], proj, ya, yb)
    call(_mm, "mm_dwout", merged, dx1_bf16, mode="tn", out_dtype=BF16, grad_of="w_out")
    call(_mm, "mm_dwa", o_att, dya, mode="tn", out_dtype=BF16, grad_of="w_branch_a")
    do_att = call(_mm, "mm_doatt", dya, full["w_branch_a"], mode="nt", out_dtype=F32)
    call(_mm, "mm_dwb", o_hg, dyb, mode="tn", out_dtype=BF16, grad_of="w_branch_b")
    do_hg = call(_mm, "mm_dohg", dyb, full["w_branch_b"], mode="nt", out_dtype=F32)
    dqh, dfh, dih, dgh, dlb, gs["hg_norm_w"] = call(
        _hg_bwd, "hgrn_bwd", proj, small["hg_lower_bounds"], small["hg_norm_w"], oraw, states, do_hg)
    gs["hg_lb"] = dlb
    do_gs, dl_gs = call(_attn_merge_bwd, "attn_merge_bwd", outs, lses, do_att)
    dqs, dks, dvs = zip(*[call(_attn_bwd, f"attn_bwd_g{g}", proj, g, lses[g], do_gs[g], dl_gs[g]) for g in range(3)])
    dproj = _concat_cols([*dqs, *dks, *dvs, dqh, dfh, dih, dgh, dga, dgb], "dproj_concat")
    if net.comm:
        h_sibling, h_own = h_halves
        call(_mm, "mm_dwin_sibling", h_sibling, dproj, mode="tn", out_dtype=BF16, grad_of="w_in", sibling_half=True)
        call(_mm, "mm_dwin_own", h_own, dproj, mode="tn", out_dtype=BF16, grad_of="w_in")
    else:
        call(_mm, "mm_dwin", h, dproj, mode="tn", out_dtype=BF16, grad_of="w_in")
    dh = call(_mm, "mm_dh", dproj, full["w_in"], mode="nt", out_dtype=F32)
    dx, gs["ln_mix_w"] = _rms_bwd(x, small["ln_mix_w"], dh, dx1, "rms_mix_bwd")
    return loss, dx, gs


WEIGHT_ORDER = ("ln_mix_w", "w_in", "hg_norm_w", "hg_lower_bounds", "w_branch_a", "w_branch_b", "w_out",
                "ln_cross_w", "ln_mem_w", "wq_cross", "wkv_cross", "wo_cross", "ln_ffn_w", "w1", "w3", "w2",
                "ln_final_w")


def kernel(x, mem, ln_mix_w, w_in, hg_norm_w, hg_lower_bounds, w_branch_a, w_branch_b, w_out, ln_cross_w, ln_mem_w, wq_cross, wkv_cross, wo_cross, ln_ffn_w, w1, w3, w2, ln_final_w, loss_target, m_ln_mix_w, m_w_in, m_hg_norm_w, m_hg_lower_bounds, m_w_branch_a, m_w_branch_b, m_w_out, m_ln_cross_w, m_ln_mem_w, m_wq_cross, m_wkv_cross, m_wo_cross, m_ln_ffn_w, m_w1, m_w3, m_w2, m_ln_final_w, v_ln_mix_w, v_w_in, v_hg_norm_w, v_hg_lower_bounds, v_w_branch_a, v_w_branch_b, v_w_out, v_ln_cross_w, v_ln_mem_w, v_wq_cross, v_wkv_cross, v_wo_cross, v_ln_ffn_w, v_w1, v_w3, v_w2, v_ln_final_w):
    args = dict(locals())
    w = {n: args[n] for n in WEIGHT_ORDER}
    m = {n: args["m_" + n] for n in WEIGHT_ORDER}
    v = {n: args["v_" + n] for n in WEIGHT_ORDER}
    shapes = {n: w[n].shape for n in WEIGHT_ORDER}
    mat = lambda a: a.reshape(a.shape[-2:])
    shard_shapes = {n: shapes[n][-2:] for n in BIG_WEIGHTS}

    pos = _mesh_scalars()

    def cast(f, token=None):
        return _cast_into_full({n: mat(w[n]) for n in BIG_WEIGHTS if WEIGHT_PLACE[n][0] == f}, f, pos,
                               FULL_SPECS, WEIGHT_PLACE, f"cast_{f}", token)

    net = _Net({"w_in": cast("w_in")}, pos, shard_shapes)
    net.start([(f"ring_a{i}", (*job, "a")) for i, job in enumerate(W_IN_PIECES)], "ring", "gather_start_w_in")
    rest = {f: cast(f, net.token) for f in FULL_SPECS if f != "w_in"}
    net.full.update(rest)
    small = {n: w[n].reshape(1, -1) for n in SMALL_ROWS}
    small["hg_norm_w"] = w["hg_norm_w"].reshape(1, HEAD_DIM)
    small["hg_lower_bounds"] = w["hg_lower_bounds"]
    x2d = x.reshape(SEQ, D_MODEL)
    h, *h_halves = _rms_fwd_halves(x2d, small["ln_mix_w"], pos, "rms_mix")
    for i, job in enumerate(W_IN_PIECES):
        net.wait(f"ring_a{i}", after=[*rest.values(), h] if i == 0 else ())
        net.start([(f"ring_b{i}", (*job, "b"))], "ring", f"gather_pass_on_w_in{i}")
    net.start(GATHER_GROUPS, "gather", "gather_start_rest")
    for i, job in enumerate(W_IN_PIECES):
        net.wait(f"ring_b{i}")
        net.run_comm(("d2d", [job]), f"gather_hand_over_w_in{i}")
    loss, dx, gs = _local_step(net, x2d, h, mem.reshape(MEM_LEN, D_MODEL), loss_target.reshape(SEQ, D_MODEL), small,
                               h_halves)

    out_g, out_d, out_m, out_v = {}, {}, {}, {}
    net.last = dx
    for step in FINISH:
        if step[0] == "adamw":
            for n in step[1]:
                out_d[n], out_m[n], out_v[n], out_g[n] = net.call(_adamw, f"adamw_{n}", mat(w[n]), net.grads[n],
                                                                  mat(m[n]), mat(v[n]))
        elif step[0] == "wait":
            net.wait(step[1], after=list(out_d.values()))
        elif step[0] == "small":
            pad = lambda a: jnp.pad(a, ((0, 0), (0, D_MODEL - a.shape[1])))
            part = jnp.concatenate(
                [gs[n] for n in SMALL_ROWS]
                + [pad(jnp.concatenate([gs["hg_norm_w"][0], loss], axis=1)), pad(gs["hg_lb"]),
                   pad(gs["hg_norm_w"][1]) if len(gs["hg_norm_w"]) > 1 else jnp.zeros((1, D_MODEL), F32)], axis=0)
            part, net.psum["w_in"] = lax.optimization_barrier((part, net.psum["w_in"]))
            sg, sd, sm, sv, loss_tot = _small_update(_gather_rows(part), _pack_small(w), _pack_small(m),
                                                     _pack_small(v))
            for dst, packed in ((out_g, sg), (out_d, sd), (out_m, sm), (out_v, sv)):
                dst.update(_unpack_small(packed, shapes))
        else:
            net.step(step)

    result = [loss_tot[0, 0], dx.reshape(x.shape)]
    for group in (out_g, out_d, out_m, out_v):
        result += [group[n].reshape(shapes[n]) for n in WEIGHT_ORDER]
    return tuple(result)
```

```python
import math

import jax
import jax.numpy as jnp
from jax import lax
from jax.experimental import pallas as pl
from jax.experimental.pallas import tpu as pltpu

F32 = jnp.float32
BF16 = jnp.bfloat16
MESH = pl.DeviceIdType.MESH

D_MODEL = 2048
SEQ = 2048
HEAD_DIM = 128
MEM_LEN = 256
ATT_GROUPS = ((128, 1), (512, 4), (2048, 16))
ATT_HEADS = 4
ATT_WIDTH = 3 * ATT_HEADS * HEAD_DIM
ATT_OUT = ATT_HEADS * HEAD_DIM
ATT_BLOCK = 128
HG_HEADS = 8
HG_WIDTH = HG_HEADS * HEAD_DIM
HG_CHUNK = 64
IN_WIDTH = 3 * ATT_WIDTH + 4 * HG_WIDTH + 2 * D_MODEL
CROSS_HEADS = 4
CROSS_WIDTH = CROSS_HEADS * HEAD_DIM
D_FF = 5632
RMS_EPS = 1e-6
ADAM_LR = 0.001
ADAM_B1 = 0.9
ADAM_B2 = 0.999
ADAM_EPS = 1e-08
ADAM_WD = 0.01
ADAM_STEP = 10
N_CHIPS = 4
N_DEV = 8

VMEM_LIMIT_BYTES = 56 * 1024 * 1024
LANE = 128
MXU_WIDTH = 256
MM_TILE_CAP = 1536
TRANSPOSE_CHUNK = 512
ANY = pl.BlockSpec(memory_space=pl.ANY)


def _cparams(sem=None):
    return pltpu.CompilerParams(dimension_semantics=sem, vmem_limit_bytes=VMEM_LIMIT_BYTES)


def _div(n, cap, mult):
    best = None
    for d in range(mult, min(n, cap) + 1, mult):
        if n % d == 0:
            best = d
    assert best is not None, (n, cap, mult)
    return best


def _sigmoid(x):
    return 1.0 / (1.0 + jnp.exp(-x))


def _dot(a, b):
    return jnp.dot(a.astype(BF16), b.astype(BF16), preferred_element_type=F32)


def _dot_nt(a, b):
    return lax.dot_general(a.astype(BF16), b.astype(BF16), (((1,), (1,)), ((), ())),
                           preferred_element_type=F32)


def _dot_tn(a, b):
    return jnp.dot(a.astype(F32).T.astype(BF16), b.astype(BF16), preferred_element_type=F32)


def _dot_exact(a, b):
    return jnp.dot(a, b, precision=lax.Precision.HIGHEST, preferred_element_type=F32)


class _Carried:
    def __init__(self, arrays, fresh, n_sems, start, finish, mid=None, reads=None):
        self.arrays, self.fresh, self.n_sems, self.reads = arrays, fresh, n_sems, reads or {}
        self.start, self.mid, self.finish = start, mid, finish


class _Token:
    def __init__(self, array):
        self.array = array


TOKEN_SHAPE = (8, LANE)


def _carried_layout(carried):
    akeys = list(dict.fromkeys(k for cm in carried for k in cm.arrays))
    fkeys = [(ci, k) for ci, cm in enumerate(carried) for k in cm.fresh]
    arrays = [next(cm.arrays[k] for cm in carried if k in cm.arrays) for k in akeys]
    shapes = [jax.ShapeDtypeStruct(a.shape, a.dtype) for a in arrays] + [carried[ci].fresh[k] for ci, k in fkeys]
    sems = []
    for cm in carried:
        sems += [pltpu.SemaphoreType.DMA((cm.n_sems,)), pltpu.SemaphoreType.DMA((cm.n_sems,))]
    return akeys, fkeys, arrays, shapes, sems


def _carried_reads(carried):
    rkeys = list(dict.fromkeys(k for cm in carried for k in cm.reads))
    return rkeys, [next(cm.reads[k] for cm in carried if k in cm.reads) for k in rkeys]


def _carried_results(carried, akeys, fkeys, outs, rkeys=(), read_refs=()):
    shared = dict(zip(akeys, outs[:len(akeys)]))
    shared.update(zip(rkeys, read_refs))
    res = [{k: shared[k] for k in list(cm.arrays) + [r for r in cm.reads if r in shared]} for cm in carried]
    for (ci, k), o in zip(fkeys, outs[len(akeys):]):
        res[ci][k] = o
    return res


def _pcall(kern, *, name, grid, in_specs, out_specs, out_shape, args, scratch_shapes=(), semantics=None,
           carried=()):
    tokens = [c.array for c in carried if isinstance(c, _Token)]
    carried = [c for c in carried if not isinstance(c, _Token)]
    single = not isinstance(out_shape, (list, tuple))
    out_specs = [out_specs] if single else list(out_specs)
    out_shape = [out_shape] if single else list(out_shape)
    n_real, n_out, n_scr = len(in_specs), len(out_shape), len(scratch_shapes)
    in_specs = list(in_specs) + [pl.BlockSpec(TOKEN_SHAPE, lambda *_: (0, 0))] * len(tokens)
    args = list(args) + tokens
    n_in = len(in_specs)
    if not carried:
        def plain(*refs):
            kern(*refs[:n_real], *refs[n_in:])

        outs = pl.pallas_call(plain if tokens else kern, name=name, grid=grid, in_specs=in_specs,
                              out_specs=out_specs, out_shape=out_shape, scratch_shapes=list(scratch_shapes),
                              compiler_params=_cparams(semantics))(*args)
        return (outs[0] if single else list(outs)), []
    akeys, fkeys, arrays, shapes, sems = _carried_layout(carried)
    rkeys, reads = _carried_reads(carried)
    n_a, n_f, n_r = len(akeys), len(fkeys), len(rkeys)
    total = math.prod(grid)
    mid_step = min(total - 1, (17 * total) // 20)

    def wrapped(*refs):
        ins = refs[:n_real]
        r0 = n_in + n_a
        o0 = r0 + n_r
        outs = refs[o0:o0 + n_out]
        a0 = o0 + n_out
        s0 = a0 + n_a + n_f
        per = _carried_results(carried, akeys, fkeys, refs[a0:s0], rkeys, refs[r0:o0])
        scratch = refs[s0:s0 + n_scr]
        sem = refs[s0 + n_scr:]
        step = 0
        for d, g in enumerate(grid):
            step = step * g + pl.program_id(d)

        @pl.when(step == 0)
        def _():
            for ci, cm in enumerate(carried):
                cm.start(per[ci], sem[2 * ci], sem[2 * ci + 1])

        kern(*ins, *outs, *scratch)

        @pl.when(step == mid_step)
        def _():
            for ci, cm in enumerate(carried):
                if cm.mid is not None:
                    cm.mid(per[ci], sem[2 * ci], sem[2 * ci + 1])

        @pl.when(step == total - 1)
        def _():
            for ci, cm in enumerate(carried):
                cm.finish(per[ci], sem[2 * ci], sem[2 * ci + 1])

    outs = pl.pallas_call(
        wrapped, name=name, grid=grid,
        in_specs=list(in_specs) + [ANY] * (n_a + n_r), out_specs=out_specs + [ANY] * (n_a + n_f),
        out_shape=out_shape + shapes,
        input_output_aliases={n_in + i: n_out + i for i in range(n_a)},
        scratch_shapes=list(scratch_shapes) + sems,
        compiler_params=_cparams(("arbitrary",) * len(grid)),
    )(*args, *arrays, *reads)
    res = _carried_results(carried, akeys, fkeys, outs[n_out:])
    return (outs[0] if single else list(outs[:n_out])), res


def _run_comm(carried, name):
    carried = list(carried)
    akeys, fkeys, arrays, shapes, sems = _carried_layout(carried)
    rkeys, reads = _carried_reads(carried)
    n_a, n_f, n_r = len(akeys), len(fkeys), len(rkeys)

    def body(*refs):
        o0 = n_a + n_r
        per = _carried_results(carried, akeys, fkeys, refs[o0:o0 + n_a + n_f], rkeys, refs[n_a:o0])
        sem = refs[o0 + n_a + n_f:]
        for hook in ("start", "mid", "finish"):
            for ci, cm in enumerate(carried):
                fn = getattr(cm, hook)
                if fn is not None:
                    fn(per[ci], sem[2 * ci], sem[2 * ci + 1])

    outs = pl.pallas_call(
        body, name=name, in_specs=[ANY] * (n_a + n_r), out_specs=[ANY] * (n_a + n_f), out_shape=shapes,
        input_output_aliases={i: i for i in range(n_a)}, scratch_shapes=sems,
    )(*arrays, *reads)
    return _carried_results(carried, akeys, fkeys, outs)


HBM_SPEC = pl.BlockSpec(memory_space=pltpu.HBM)
SEM_SPEC = pl.BlockSpec(memory_space=pltpu.SEMAPHORE)
SPLIT_EFFECT = pltpu.SideEffectType.DATAFLOW_SIDE_EFFECTING


def _in_hbm(a):
    return pltpu.with_memory_space_constraint(a, pltpu.HBM)


def _split_start(items, name, after=None):
    items = list(items)
    akeys, fkeys, arrays, shapes, sems = _carried_layout(items)
    assert not fkeys
    n_a, n_s = len(akeys), len(sems)
    n_in = n_a + (after is not None)

    def body(*refs):
        per = _carried_results(items, akeys, [], refs[n_in:n_in + n_a])
        sem = refs[n_in + n_a:n_in + n_a + n_s]
        for ci, cm in enumerate(items):
            cm.start(per[ci], sem[2 * ci], sem[2 * ci + 1])
        token = refs[n_in + n_a + n_s]
        token[...] = jnp.zeros_like(token)

    outs = pl.pallas_call(
        body, name=name, in_specs=[HBM_SPEC] * n_a + [ANY] * (after is not None),
        out_specs=[HBM_SPEC] * n_a + [SEM_SPEC] * n_s + [pl.BlockSpec(memory_space=pltpu.VMEM)],
        out_shape=[pltpu.HBM(s.shape, s.dtype) for s in shapes] + sems + [jax.ShapeDtypeStruct(TOKEN_SHAPE, F32)],
        input_output_aliases={i: i for i in range(n_a)},
        compiler_params=pltpu.CompilerParams(has_side_effects=SPLIT_EFFECT),
    )(*[_in_hbm(a) for a in arrays], *([after] if after is not None else []))
    res = _carried_results(items, akeys, [], outs[:n_a])
    sem_out = outs[n_a:n_a + n_s]
    return res, [(sem_out[2 * ci], sem_out[2 * ci + 1]) for ci in range(len(items))], outs[-1]


def _split_wait(items, sems, after, name):
    items = list(items)
    after = list(after) if isinstance(after, (list, tuple)) else [after]
    akeys, fkeys, arrays, shapes, _ = _carried_layout(items)
    n_a, n_s = len(akeys), 2 * len(items)

    def body(*refs):
        per = _carried_results(items, akeys, [], refs[n_a + n_s + len(after):])
        sem = refs[n_a:n_a + n_s]
        for ci, cm in enumerate(items):
            cm.finish(per[ci], sem[2 * ci], sem[2 * ci + 1])

    outs = pl.pallas_call(
        body, name=name, in_specs=[HBM_SPEC] * n_a + [SEM_SPEC] * n_s + [ANY] * len(after),
        out_specs=[HBM_SPEC] * n_a, out_shape=[pltpu.HBM(s.shape, s.dtype) for s in shapes],
        input_output_aliases={i: i for i in range(n_a)},
        compiler_params=pltpu.CompilerParams(has_side_effects=SPLIT_EFFECT),
    )(*arrays, *[s for pair in sems for s in pair], *after)
    return _carried_results(items, akeys, [], outs)


def _mm(a, b, *, mode, out_dtype, name, res=None, carried=(), fused=None):
    if mode == "nn":
        (m, k), (k2, n) = a.shape, b.shape
    elif mode == "nt":
        (m, k), (n, k2) = a.shape, b.shape
    else:
        (k, m), (k2, n) = a.shape, b.shape
    assert k == k2, (name, a.shape, b.shape)
    tm = _div(m, MM_TILE_CAP, LANE)
    tn = _div(n, MM_TILE_CAP, MXU_WIDTH) if n % MXU_WIDTH == 0 else 0
    if tn < 1024:
        tn = _div(n, MM_TILE_CAP, LANE)
    out_shape = jax.ShapeDtypeStruct((m, n), out_dtype)

    if fused is not None:
        assert mode in ("nn", "nt") and res is None and k <= 2048
        tm, tn = fused["tile"]
        dot = _dot if mode == "nn" else _dot_nt
        n_x = len(fused["ins"])

        summed = [len(e) > 2 for e in fused["outs"]]

        def kern_fused(*refs):
            part = dot(refs[0][...], refs[1][...])
            outs = fused["post"](part, *[r[...] for r in refs[2:2 + n_x]])
            first_row_tile = pl.program_id(1) == 0
            for o_ref, val, acc in zip(refs[2 + n_x:], outs, summed):
                if not acc:
                    o_ref[...] = val.astype(o_ref.dtype)
                    continue

                @pl.when(first_row_tile)
                def _():
                    o_ref[...] = val

                @pl.when(jnp.logical_not(first_row_tile))
                def _():
                    o_ref[...] += val

        def tile_spec(shape, index=lambda i, j: (i, j)):
            return pl.BlockSpec(shape, lambda j, i: index(i, j))

        return _pcall(
            kern_fused, name=name, grid=(n // tn, m // tm),
            in_specs=[pl.BlockSpec((tm, k), lambda j, i: (i, 0)),
                      pl.BlockSpec((k, tn), lambda j, i: (0, j)) if mode == "nn"
                      else pl.BlockSpec((tn, k), lambda j, i: (j, 0))] + [tile_spec(*e[1:]) for e in fused["ins"]],
            out_specs=[tile_spec(*e[1:]) for e in fused["outs"]], out_shape=[e[0] for e in fused["outs"]],
            args=(a, b, *[e[0] for e in fused["ins"]]), semantics=("parallel", "arbitrary"), carried=carried)

    if mode == "tn":
        has_res_tn = res is not None

        def kern_tn(*refs):
            a_ref, b_ref, o_ref, at_ref = refs[0], refs[1], refs[-2], refs[-1]

            @pl.when(pl.program_id(1) == 0)
            def _():
                step = min(TRANSPOSE_CHUNK, k)
                for c0 in range(0, k, step):
                    at_ref[:, c0:c0 + step] = a_ref[c0:c0 + step, :].astype(F32).T.astype(BF16)

            part = jnp.dot(at_ref[...], b_ref[...].astype(BF16), preferred_element_type=F32)
            if has_res_tn:
                part = part + refs[2][...].astype(F32)
            o_ref[...] = part.astype(o_ref.dtype)

        o_spec = pl.BlockSpec((tm, tn), lambda i, j: (i, j))
        return _pcall(
            kern_tn, name=name, grid=(m // tm, n // tn),
            in_specs=[pl.BlockSpec((k, tm), lambda i, j: (0, i)),
                      pl.BlockSpec((k, tn), lambda i, j: (0, j))] + [o_spec] * has_res_tn,
            out_specs=o_spec, out_shape=out_shape, args=(a, b) + ((res,) if has_res_tn else ()),
            scratch_shapes=[pltpu.VMEM((tm, k), BF16)],
            semantics=("parallel", "arbitrary"), carried=carried)

    tk = k if k <= 2048 else _div(k, 3072, LANE)
    nk = k // tk
    a_spec = pl.BlockSpec((tm, tk), lambda i, j, kk: (i, kk))
    if mode == "nn":
        b_spec = pl.BlockSpec((tk, tn), lambda i, j, kk: (kk, j))
        dot = _dot
    else:
        b_spec = pl.BlockSpec((tn, tk), lambda i, j, kk: (j, kk))
        dot = _dot_nt
    o_spec = pl.BlockSpec((tm, tn), lambda i, j, kk: (i, j))
    in_specs = [a_spec, b_spec]
    args = [a, b]
    if res is not None:
        in_specs.append(o_spec)
        args.append(res)
    has_res = res is not None

    def kern(*refs):
        a_ref, b_ref = refs[0], refs[1]
        r_ref = refs[2] if has_res else None
        o_ref = refs[3] if has_res else refs[2]
        part = dot(a_ref[...], b_ref[...])
        if nk == 1:
            if has_res:
                part = part + r_ref[...]
            o_ref[...] = part.astype(o_ref.dtype)
            return
        acc_ref = refs[-1]
        kk = pl.program_id(2)

        @pl.when(kk == 0)
        def _():
            acc_ref[...] = part

        @pl.when(kk > 0)
        def _():
            acc_ref[...] += part

        @pl.when(kk == nk - 1)
        def _():
            tot = acc_ref[...]
            if has_res:
                tot = tot + r_ref[...]
            o_ref[...] = tot.astype(o_ref.dtype)

    return _pcall(
        kern, name=name, grid=(m // tm, n // tn, nk),
        in_specs=in_specs, out_specs=o_spec, out_shape=out_shape, args=args,
        scratch_shapes=[pltpu.VMEM((tm, tn), F32)] if nk > 1 else [],
        semantics=("parallel", "parallel", "arbitrary"), carried=carried)


ROW_BLOCK = 512


def _rms_fwd(x, g, name):
    t, d = x.shape
    tr = min(ROW_BLOCK, t)

    def kern(x_ref, g_ref, o_ref):
        xf = x_ref[...]
        r = lax.rsqrt(jnp.mean(xf * xf, axis=-1, keepdims=True) + RMS_EPS)
        o_ref[...] = (xf * r * g_ref[...]).astype(o_ref.dtype)

    return pl.pallas_call(
        kern, name=name, grid=(t // tr,),
        in_specs=[pl.BlockSpec((tr, d), lambda i: (i, 0)), pl.BlockSpec((1, d), lambda i: (0, 0))],
        out_specs=pl.BlockSpec((tr, d), lambda i: (i, 0)),
        out_shape=jax.ShapeDtypeStruct((t, d), BF16),
        compiler_params=_cparams(("parallel",)),
    )(x, g)


def _rms_fwd_halves(x, g, pos, name):
    t, d = x.shape
    tr = min(ROW_BLOCK, t)
    half = d // 2

    def kern(pos_ref, x_ref, g_ref, o_ref, sib_ref, own_ref):
        xf = x_ref[...]
        r = lax.rsqrt(jnp.mean(xf * xf, axis=-1, keepdims=True) + RMS_EPS)
        h = (xf * r * g_ref[...]).astype(o_ref.dtype)
        o_ref[...] = h
        is_south = pos_ref[0] == 0
        sib_ref[...] = jnp.where(is_south, h[:, half:], h[:, :half])
        own_ref[...] = jnp.where(is_south, h[:, :half], h[:, half:])

    row = pl.BlockSpec((tr, d), lambda i, pos_ref: (i, 0))
    part = pl.BlockSpec((tr, half), lambda i, pos_ref: (i, 0))
    return pl.pallas_call(
        kern, name=name,
        grid_spec=_grid_spec((t // tr,), [row, pl.BlockSpec((1, d), lambda i, pos_ref: (0, 0))], [row, part, part]),
        out_shape=[jax.ShapeDtypeStruct((t, d), BF16)] + [jax.ShapeDtypeStruct((t, half), BF16)] * 2,
        compiler_params=_cparams(("parallel",)),
    )(pos, x, g)


def _rms_bwd(x, g, dh, res, name, bf16_copy=False):
    t, d = x.shape
    tr = min(ROW_BLOCK, t)
    has_res = res is not None

    def kern(*refs):
        x_ref, g_ref, dh_ref = refs[:3]
        r_ref = refs[3] if has_res else None
        dx_ref, dg_ref = refs[3 + has_res], refs[4 + has_res]
        xf = x_ref[...]
        r = lax.rsqrt(jnp.mean(xf * xf, axis=-1, keepdims=True) + RMS_EPS)
        xn = xf * r
        dh_ = dh_ref[...]
        dhg = dh_ * g_ref[...]
        dx = r * (dhg - xn * jnp.mean(dhg * xn, axis=-1, keepdims=True))
        if has_res:
            dx = dx + r_ref[...]
        dx_ref[...] = dx
        if bf16_copy:
            refs[-1][...] = dx.astype(BF16)
        part = jnp.sum(dh_ * xn, axis=0, keepdims=True)

        @pl.when(pl.program_id(0) == 0)
        def _():
            dg_ref[...] = part

        @pl.when(pl.program_id(0) > 0)
        def _():
            dg_ref[...] += part

    row = pl.BlockSpec((tr, d), lambda i: (i, 0))
    vec = pl.BlockSpec((1, d), lambda i: (0, 0))
    in_specs = [row, vec, row] + ([row] if has_res else [])
    args = [x, g, dh] + ([res] if has_res else [])
    return pl.pallas_call(
        kern, name=name, grid=(t // tr,), in_specs=in_specs, out_specs=[row, vec] + [row] * bf16_copy,
        out_shape=[jax.ShapeDtypeStruct((t, d), F32), jax.ShapeDtypeStruct((1, d), F32)]
        + [jax.ShapeDtypeStruct((t, d), BF16)] * bf16_copy,
        compiler_params=_cparams(("arbitrary",)),
    )(*args)


def _loss_head(x3, g, target, name, core_row=None):
    t, d = x3.shape
    tr = ROW_BLOCK
    with_halves = core_row is not None

    def kern(x_ref, g_ref, t_ref, *rest):
        dx_ref, dg_ref, loss_ref, dxb_ref = rest[with_halves:with_halves + 4]
        xf = x_ref[...]
        r = lax.rsqrt(jnp.mean(xf * xf, axis=-1, keepdims=True) + RMS_EPS)
        xn = xf * r
        gg = g_ref[...]
        err = xn * gg - t_ref[...]
        lpart = 0.5 * jnp.sum(jnp.mean(err * err, axis=-1, keepdims=True), axis=0, keepdims=True)
        dy = err * (1.0 / d)
        dyg = dy * gg
        dx = r * (dyg - xn * jnp.mean(dyg * xn, axis=-1, keepdims=True))
        dx_ref[...] = dx
        dxb = dx.astype(BF16)
        dxb_ref[...] = dxb
        if with_halves:
            rest[-2][...], rest[-1][...] = _column_halves(dxb, rest[0][...])
        gpart = jnp.sum(dy * xn, axis=0, keepdims=True)
        lrow = jnp.broadcast_to(lpart, (1, LANE))

        @pl.when(pl.program_id(0) == 0)
        def _():
            dg_ref[...] = gpart
            loss_ref[...] = lrow

        @pl.when(pl.program_id(0) > 0)
        def _():
            dg_ref[...] += gpart
            loss_ref[...] += lrow

    row = pl.BlockSpec((tr, d), lambda i: (i, 0))
    vec = pl.BlockSpec((1, d), lambda i: (0, 0))
    part = pl.BlockSpec((tr, d // 2), lambda i: (i, 0))
    return pl.pallas_call(
        kern, name=name, grid=(t // tr,), in_specs=[row, vec, row] + [vec] * with_halves,
        out_specs=[row, vec, pl.BlockSpec((1, LANE), lambda i: (0, 0)), row] + [part] * (2 * with_halves),
        out_shape=[jax.ShapeDtypeStruct((t, d), F32), jax.ShapeDtypeStruct((1, d), F32),
                   jax.ShapeDtypeStruct((1, LANE), F32), jax.ShapeDtypeStruct((t, d), BF16)]
        + [jax.ShapeDtypeStruct((t, d // 2), BF16)] * (2 * with_halves),
        compiler_params=_cparams(("arbitrary",)),
    )(x3, g, target, *([core_row] if with_halves else []))


ATT_SCALE = HEAD_DIM ** -0.5
Q_BLOCK0, K_BLOCK0, V_BLOCK0 = 0, ATT_WIDTH // HEAD_DIM, 2 * ATT_WIDTH // HEAD_DIM


def _residue_rows(dil, r, n):
    if dil == 1:
        return pl.ds(n * ATT_BLOCK, ATT_BLOCK)
    return pl.ds(n * ATT_BLOCK * dil + r, ATT_BLOCK, stride=dil)


def _band_mask(with_prev):
    width = 2 * ATT_BLOCK if with_prev else ATT_BLOCK
    iq = lax.broadcasted_iota(jnp.int32, (ATT_BLOCK, width), 0)
    ik = lax.broadcasted_iota(jnp.int32, (ATT_BLOCK, width), 1)
    if not with_prev:
        return ik <= iq
    return ((ik < ATT_BLOCK) & (iq <= ik)) | ((ik >= ATT_BLOCK) & ((ik - ATT_BLOCK) <= iq))


def _band_keys(ref, dil, r, n):
    own = ref[_residue_rows(dil, r, n), :]
    if n == 0:
        return own
    return jnp.concatenate([ref[_residue_rows(dil, r, n - 1), :], own], axis=0)


def _attn_col_spec(base, grp):
    return pl.BlockSpec((SEQ, HEAD_DIM), lambda h: (0, base + grp * ATT_HEADS + h))


def _attn_fwd(proj, grp, name, carried=()):
    _, dil = ATT_GROUPS[grp]
    nb = SEQ // dil // ATT_BLOCK

    def kern(q_ref, k_ref, v_ref, o_ref, lse_ref):
        for r in range(dil):
            for n in range(nb):
                rows = _residue_rows(dil, r, n)
                s = _dot_nt(q_ref[rows, :], _band_keys(k_ref, dil, r, n)) * ATT_SCALE
                s = jnp.where(_band_mask(n > 0), s, -jnp.inf)
                m = jnp.max(s, axis=-1, keepdims=True)
                p = jnp.exp(s - m)
                l = jnp.sum(p, axis=-1, keepdims=True)
                o_ref[rows, :] = _dot(p / l, _band_keys(v_ref, dil, r, n))
                lse_ref[rows, :] = jnp.broadcast_to(m + jnp.log(l), (ATT_BLOCK, HEAD_DIM))

    out_spec = pl.BlockSpec((SEQ, HEAD_DIM), lambda h: (0, h))
    return _pcall(
        kern, name=name, grid=(ATT_HEADS,),
        in_specs=[_attn_col_spec(Q_BLOCK0, grp), _attn_col_spec(K_BLOCK0, grp), _attn_col_spec(V_BLOCK0, grp)],
        out_specs=[out_spec, out_spec],
        out_shape=[jax.ShapeDtypeStruct((SEQ, ATT_OUT), F32)] * 2, args=(proj, proj, proj),
        semantics=("parallel",), carried=carried)


def _attn_weights(l0, l1, l2):
    mx = jnp.maximum(jnp.maximum(l0, l1), l2)
    e0, e1, e2 = jnp.exp(l0 - mx), jnp.exp(l1 - mx), jnp.exp(l2 - mx)
    den = e0 + e1 + e2
    return e0 / den, e1 / den, e2 / den


def _attn_merge_fwd(outs, lses, name):
    tr = ROW_BLOCK

    def kern(o0, o1, o2, l0, l1, l2, out_ref):
        a0, a1, a2 = _attn_weights(l0[...], l1[...], l2[...])
        out_ref[...] = (a0 * o0[...] + a1 * o1[...] + a2 * o2[...]).astype(out_ref.dtype)

    spec = pl.BlockSpec((tr, ATT_OUT), lambda i: (i, 0))
    return pl.pallas_call(
        kern, name=name, grid=(SEQ // tr,), in_specs=[spec] * 6, out_specs=spec,
        out_shape=jax.ShapeDtypeStruct((SEQ, ATT_OUT), BF16),
        compiler_params=_cparams(("parallel",)),
    )(*outs, *lses)


def _attn_merge_bwd(outs, lses, do_att, name, carried=()):
    tr = ROW_BLOCK

    def kern(o0, o1, o2, l0, l1, l2, do_ref, d0, d1, d2, t0, t1, t2):
        alphas = _attn_weights(l0[...], l1[...], l2[...])
        do = do_ref[...]
        o_att = alphas[0] * o0[...] + alphas[1] * o1[...] + alphas[2] * o2[...]
        prod = do * o_att
        parts = []
        for h in range(ATT_HEADS):
            sl = slice(h * HEAD_DIM, (h + 1) * HEAD_DIM)
            tot = jnp.sum(prod[:, sl], axis=-1, keepdims=True)
            parts.append(jnp.broadcast_to(tot, (tr, HEAD_DIM)))
        dd = jnp.concatenate(parts, axis=1)
        for a, d_ref, t_ref in zip(alphas, (d0, d1, d2), (t0, t1, t2)):
            d_ref[...] = a * do
            t_ref[...] = -a * dd

    spec = pl.BlockSpec((tr, ATT_OUT), lambda i: (i, 0))
    res, cres = _pcall(
        kern, name=name, grid=(SEQ // tr,), in_specs=[spec] * 7, out_specs=[spec] * 6,
        out_shape=[jax.ShapeDtypeStruct((SEQ, ATT_OUT), F32)] * 6, args=(*outs, *lses, do_att),
        semantics=("parallel",), carried=carried)
    return (res[:3], res[3:]), cres


def _attn_bwd(proj, grp, lse, do_g, dl_g, name, carried=()):
    _, dil = ATT_GROUPS[grp]
    nb = SEQ // dil // ATT_BLOCK

    def kern(q_ref, k_ref, v_ref, do_ref, lse_ref, dl_ref, dq_ref, dk_ref, dv_ref, dq_acc, dk_acc, dv_acc):
        dk_acc[...] = jnp.zeros_like(dk_acc)
        dv_acc[...] = jnp.zeros_like(dv_acc)
        for r in range(dil):
            for n in range(nb):
                rows = _residue_rows(dil, r, n)
                q, do = q_ref[rows, :], do_ref[rows, :]
                kk, vv = _band_keys(k_ref, dil, r, n), _band_keys(v_ref, dil, r, n)
                s = _dot_nt(q, kk) * ATT_SCALE
                p = jnp.where(_band_mask(n > 0), jnp.exp(s - lse_ref[rows, :][:, :1]), 0.0)
                ds = p * (_dot_nt(do, vv) + dl_ref[rows, :][:, :1])
                dq_acc[rows, :] = _dot(ds, kk) * ATT_SCALE
                dk = _dot_tn(ds, q) * ATT_SCALE
                dv = _dot_tn(p, do)
                if n > 0:
                    prev = _residue_rows(dil, r, n - 1)
                    dk_acc[prev, :] += dk[:ATT_BLOCK]
                    dv_acc[prev, :] += dv[:ATT_BLOCK]
                    dk, dv = dk[ATT_BLOCK:], dv[ATT_BLOCK:]
                dk_acc[rows, :] += dk
                dv_acc[rows, :] += dv
        dq_ref[...] = dq_acc[...].astype(dq_ref.dtype)
        dk_ref[...] = dk_acc[...].astype(dk_ref.dtype)
        dv_ref[...] = dv_acc[...].astype(dv_ref.dtype)

    spec = pl.BlockSpec((SEQ, HEAD_DIM), lambda h: (0, h))
    return _pcall(
        kern, name=name, grid=(ATT_HEADS,),
        in_specs=[_attn_col_spec(Q_BLOCK0, grp), _attn_col_spec(K_BLOCK0, grp), _attn_col_spec(V_BLOCK0, grp),
                  spec, spec, spec],
        out_specs=[spec] * 3,
        out_shape=[jax.ShapeDtypeStruct((SEQ, ATT_OUT), BF16)] * 3, args=(proj, proj, proj, do_g, lse, dl_g),
        scratch_shapes=[pltpu.VMEM((SEQ, HEAD_DIM), F32)] * 3,
        semantics=("parallel",), carried=carried)


HG_HEADS_PER_STEP = 8
HG_BLOCK_W = 4 * HEAD_DIM
HG_BLOCKS = HG_HEADS_PER_STEP * HEAD_DIM // HG_BLOCK_W
HG_STEP_W = HG_HEADS_PER_STEP * HEAD_DIM
HG_Q_BLK = (3 * ATT_WIDTH) // HG_BLOCK_W
HG_N_CHUNKS = SEQ // HG_CHUNK
HG_MID = HG_CHUNK // 2


def _lower_bound(lb_ref, sl):
    l0, l1 = lb_ref[0:1, sl], lb_ref[1:2, sl]
    mx = jnp.maximum(l0, l1)
    e0, e1 = jnp.exp(l0 - mx), jnp.exp(l1 - mx)
    return e0 / (e0 + e1)


def _tri(lower):
    i = lax.broadcasted_iota(jnp.int32, (HG_CHUNK, HG_CHUNK), 0)
    j = lax.broadcasted_iota(jnp.int32, (HG_CHUNK, HG_CHUNK), 1)
    return (i >= j) if lower else (i <= j)


def _head_mean(x):
    parts = []
    for hd in range(x.shape[1] // HEAD_DIM):
        m = jnp.mean(x[:, hd * HEAD_DIM:(hd + 1) * HEAD_DIM], axis=-1, keepdims=True)
        parts.append(jnp.broadcast_to(m, (x.shape[0], HEAD_DIM)))
    return jnp.concatenate(parts, axis=1)


def _hg_chunk_terms(qh, fh, lb):
    sig = _sigmoid(fh)
    f = lb + (1.0 - lb) * sig
    k = 1.0 - f
    b = _dot_exact(_tri(True).astype(F32), jnp.log(f))
    bl = b[HG_CHUNK - 1:HG_CHUNK, :]
    br = b[HG_MID:HG_MID + 1, :]
    sq = _sigmoid(qh)
    q = qh * sq
    return dict(sig=sig, f=f, k=k, b=b, bl=bl, br=br, sq=sq, q=q,
                e1=jnp.exp(bl - b), e2=jnp.exp(b), e3=jnp.exp(b - br), e4=jnp.exp(br - b))


def _hg_fwd(proj, lbw, normw, name, carried=()):
    def in_blks(off):
        return [pl.BlockSpec((HG_CHUNK, HG_BLOCK_W), lambda hp, n, b=b: (n, HG_Q_BLK + off + hp * HG_BLOCKS + b))
                for b in range(HG_BLOCKS)]

    def kern(*refs):
        q_refs, f_refs, i_refs, g_refs = (refs[k * HG_BLOCKS:(k + 1) * HG_BLOCKS] for k in range(4))
        lb_ref, nw_ref, oraw_ref, ohg_ref, st_ref, state = refs[4 * HG_BLOCKS:]

        @pl.when(pl.program_id(1) == 0)
        def _():
            state[...] = jnp.zeros_like(state)

        causal = _tri(True)
        wide = lambda rs: jnp.concatenate([r[...] for r in rs], axis=1)
        t = _hg_chunk_terms(wide(q_refs), wide(f_refs), _lower_bound(lb_ref, slice(None)))
        v, gh = wide(i_refs), wide(g_refs)
        kd, qb, qr, kr = t["k"] * t["e1"], t["q"] * t["e2"], t["q"] * t["e3"], t["k"] * t["e4"]
        decay = jnp.exp(t["bl"])
        outs = []
        for hd in range(HG_HEADS_PER_STEP):
            sl = slice(hd * HEAD_DIM, (hd + 1) * HEAD_DIM)
            st = state[hd]
            st_ref[0, hd] = st
            a = jnp.where(causal, _dot_nt(qr[:, sl], kr[:, sl]), 0.0)
            outs.append(_dot_nt(qb[:, sl], st) + _dot(a, v[:, sl]))
            state[hd] = st * decay[:, sl] + _dot_tn(v[:, sl], kd[:, sl])
        o = jnp.concatenate(outs, axis=1)
        oraw_ref[...] = o
        r = lax.rsqrt(_head_mean(o * o) + RMS_EPS)
        nw = jnp.tile(nw_ref[...], (1, HG_HEADS_PER_STEP))
        ohg_ref[...] = (o * r * nw * (gh * _sigmoid(gh))).astype(ohg_ref.dtype)

    out_blk = pl.BlockSpec((HG_CHUNK, HG_STEP_W), lambda hp, n: (n, hp))
    return _pcall(
        kern, name=name, grid=(HG_HEADS // HG_HEADS_PER_STEP, HG_N_CHUNKS),
        in_specs=[*in_blks(0), *in_blks(2), *in_blks(4), *in_blks(6),
                  pl.BlockSpec((2, HG_STEP_W), lambda hp, n: (0, hp)),
                  pl.BlockSpec((1, HEAD_DIM), lambda hp, n: (0, 0))],
        out_specs=[out_blk, out_blk,
                   pl.BlockSpec((1, HG_HEADS_PER_STEP, HEAD_DIM, HEAD_DIM), lambda hp, n: (n, hp, 0, 0))],
        out_shape=[jax.ShapeDtypeStruct((SEQ, HG_WIDTH), F32), jax.ShapeDtypeStruct((SEQ, HG_WIDTH), BF16),
                   jax.ShapeDtypeStruct((HG_N_CHUNKS, HG_HEADS, HEAD_DIM, HEAD_DIM), F32)],
        args=(*[proj] * (4 * HG_BLOCKS), lbw, normw),
        scratch_shapes=[pltpu.VMEM((HG_HEADS_PER_STEP, HEAD_DIM, HEAD_DIM), F32)],
        semantics=("parallel", "arbitrary"), carried=carried)


def _hg_bwd(proj, lbw, normw, oraw, states, do_hg, name, carried=()):
    last = HG_N_CHUNKS - 1

    def in_blks(off):
        return [pl.BlockSpec((HG_CHUNK, HG_BLOCK_W),
                             lambda hp, n, b=b: (last - n, HG_Q_BLK + off + hp * HG_BLOCKS + b))
                for b in range(HG_BLOCKS)]

    blk = pl.BlockSpec((HG_CHUNK, HG_STEP_W), lambda hp, n: (last - n, hp))

    def kern(*refs):
        q_refs, f_refs, i_refs, g_refs = (refs[k * HG_BLOCKS:(k + 1) * HG_BLOCKS] for k in range(4))
        (lb_ref, nw_ref, oraw_ref, st_ref, do_ref, dq_ref, df_ref, di_ref, dg_ref, dlb_ref, dnw_ref,
         dstate) = refs[4 * HG_BLOCKS:]
        first = pl.program_id(1) == 0

        @pl.when(first)
        def _():
            dstate[...] = jnp.zeros_like(dstate)

        causal = _tri(True)
        wide = lambda rs: jnp.concatenate([r[...] for r in rs], axis=1)
        cat = lambda parts: jnp.concatenate(parts, axis=1)
        qh, fh, v, gh = wide(q_refs), wide(f_refs), wide(i_refs), wide(g_refs)
        o, dout = oraw_ref[...], do_ref[...]
        nw = jnp.tile(nw_ref[...], (1, HG_HEADS_PER_STEP))
        sgg = _sigmoid(gh)
        r = lax.rsqrt(_head_mean(o * o) + RMS_EPS)
        xn = o * r
        dg_ref[...] = (dout * xn * nw * (sgg * (1.0 + gh * (1.0 - sgg)))).astype(dg_ref.dtype)
        don = dout * (gh * sgg)
        dnw_wide = jnp.sum(don * xn, axis=0, keepdims=True)
        dnw_tot = dnw_wide[:, :HEAD_DIM]
        for hd in range(1, HG_HEADS_PER_STEP):
            dnw_tot = dnw_tot + dnw_wide[:, hd * HEAD_DIM:(hd + 1) * HEAD_DIM]
        tt = don * nw
        do = r * (tt - xn * _head_mean(tt * xn))
        lb = _lower_bound(lb_ref, slice(None))
        t = _hg_chunk_terms(qh, fh, lb)
        k, q = t["k"], t["q"]
        kd, qb, qr, kr = k * t["e1"], q * t["e2"], q * t["e3"], k * t["e4"]
        decay = jnp.exp(t["bl"])
        dqb, dqr, dkr, dkd, dv, ddecay = [], [], [], [], [], []
        for hd in range(HG_HEADS_PER_STEP):
            sl = slice(hd * HEAD_DIM, (hd + 1) * HEAD_DIM)
            st = st_ref[0, hd]
            dstn = dstate[hd]
            a = jnp.where(causal, _dot_nt(qr[:, sl], kr[:, sl]), 0.0)
            da = jnp.where(causal, _dot_nt(do[:, sl], v[:, sl]), 0.0)
            dqb.append(_dot(do[:, sl], st))
            dv.append(_dot_tn(a, do[:, sl]) + _dot_nt(kd[:, sl], dstn))
            dqr.append(_dot(da, kr[:, sl]))
            dkr.append(_dot_tn(da, qr[:, sl]))
            dkd.append(_dot(v[:, sl], dstn))
            ddecay.append(jnp.sum(dstn * st, axis=0, keepdims=True))
            dstate[hd] = dstn * decay[:, sl] + _dot_tn(do[:, sl], qb[:, sl])
        dqb, dqr, dkr, dkd, dv, ddecay = cat(dqb), cat(dqr), cat(dkr), cat(dkd), cat(dv), cat(ddecay)
        dq = dqb * t["e2"] + dqr * t["e3"]
        dk = dkd * t["e1"] + dkr * t["e4"]
        db = dqb * qb + dqr * qr - dkr * kr - dkd * kd
        dbl = jnp.sum(dkd * kd, axis=0, keepdims=True) + ddecay * decay
        dbr = jnp.sum(dkr * kr - dqr * qr, axis=0, keepdims=True)
        rows = lax.broadcasted_iota(jnp.int32, db.shape, 0)
        dlf = _dot_exact(_tri(False).astype(F32), db) + dbl + jnp.where(rows <= HG_MID, dbr, 0.0)
        df = dlf / t["f"] - dk
        sig, sq = t["sig"], t["sq"]
        df_ref[...] = (df * (1.0 - lb) * sig * (1.0 - sig)).astype(df_ref.dtype)
        dlb_row = jnp.sum(df * (1.0 - sig), axis=0, keepdims=True)
        dq_ref[...] = (dq * (sq * (1.0 + qh * (1.0 - sq)))).astype(dq_ref.dtype)
        di_ref[...] = dv.astype(di_ref.dtype)
        dnw_blk = jnp.broadcast_to(dnw_tot, (8, HEAD_DIM))

        @pl.when(first)
        def _():
            dlb_ref[...] = dlb_row
            dnw_ref[...] = dnw_blk

        @pl.when(jnp.logical_not(first))
        def _():
            dlb_ref[...] += dlb_row
            dnw_ref[...] += dnw_blk

    n_hp = HG_HEADS // HG_HEADS_PER_STEP
    outs, cres = _pcall(
        kern, name=name, grid=(n_hp, HG_N_CHUNKS),
        in_specs=[*in_blks(0), *in_blks(2), *in_blks(4), *in_blks(6),
                  pl.BlockSpec((2, HG_STEP_W), lambda hp, n: (0, hp)),
                  pl.BlockSpec((1, HEAD_DIM), lambda hp, n: (0, 0)),
                  blk,
                  pl.BlockSpec((1, HG_HEADS_PER_STEP, HEAD_DIM, HEAD_DIM), lambda hp, n: (last - n, hp, 0, 0)),
                  blk],
        out_specs=[blk, blk, blk, blk,
                   pl.BlockSpec((1, HG_STEP_W), lambda hp, n: (0, hp)),
                   pl.BlockSpec((8, HEAD_DIM), lambda hp, n: (hp, 0))],
        out_shape=[jax.ShapeDtypeStruct((SEQ, HG_WIDTH), BF16)] * 4
        + [jax.ShapeDtypeStruct((1, HG_WIDTH), F32), jax.ShapeDtypeStruct((8 * n_hp, HEAD_DIM), F32)],
        args=(*[proj] * (4 * HG_BLOCKS), lbw, normw, oraw, states, do_hg),
        scratch_shapes=[pltpu.VMEM((HG_HEADS_PER_STEP, HEAD_DIM, HEAD_DIM), F32)],
        semantics=("parallel", "arbitrary"), carried=carried)
    dqh, dfh, dih, dgh, dlb, dnw = outs
    return (dqh, dfh, dih, dgh, dlb, [dnw[8 * i:8 * i + 1] for i in range(n_hp)]), cres


GATE_BLOCK_W = 512
GATE_A_BLK = (3 * ATT_WIDTH + 4 * HG_WIDTH) // GATE_BLOCK_W
GATE_B_BLK = GATE_A_BLK + D_MODEL // GATE_BLOCK_W


GATE_TILE = (1024, GATE_BLOCK_W)


def _gate_ins(proj, ya, yb=None):
    ins = [(proj, GATE_TILE, lambda i, j: (i, GATE_A_BLK + j)), (proj, GATE_TILE, lambda i, j: (i, GATE_B_BLK + j)),
           (ya, GATE_TILE)]
    return ins + ([(yb, GATE_TILE)] if yb is not None else [])


def _branch_b_gate(o_hg, w_b, proj, ya, name, carried=()):
    def post(yb, ga, gb, ya_):
        return yb, _sigmoid(ga) * ya_ + _sigmoid(gb) * yb

    return _mm(o_hg, w_b, mode="nn", out_dtype=F32, name=name, carried=carried, fused=dict(
        tile=GATE_TILE, ins=_gate_ins(proj, ya), post=post,
        outs=[(jax.ShapeDtypeStruct((SEQ, D_MODEL), F32), GATE_TILE), (jax.ShapeDtypeStruct((SEQ, D_MODEL), BF16), GATE_TILE)]))


def _dmerged_gate_bwd(dx1, w_out, proj, ya, yb, name, carried=()):
    def post(dm, ga, gb, ya_, yb_):
        sa, sb = _sigmoid(ga), _sigmoid(gb)
        return dm * sa, dm * sb, dm * ya_ * sa * (1.0 - sa), dm * yb_ * sb * (1.0 - sb)

    return _mm(dx1, w_out, mode="nt", out_dtype=BF16, name=name, carried=carried, fused=dict(
        tile=GATE_TILE, ins=_gate_ins(proj, ya, yb), post=post,
        outs=[(jax.ShapeDtypeStruct((SEQ, D_MODEL), BF16), GATE_TILE)] * 4))


NORM_TILE = (512, D_MODEL)


def _column_halves(v, core_row):
    half = v.shape[1] // 2
    south = core_row[:, :half] < 0.5
    lo, hi = v[:, :half], v[:, half:]
    return jnp.where(south, hi, lo), jnp.where(south, lo, hi)


def _residual_norm(a, w, x, g, name, carried=(), core_row=None):
    vec = ((1, D_MODEL), lambda i, j: (0, j))
    half_tile = (NORM_TILE[0], D_MODEL // 2)

    def post(part, x_, g_, *core):
        xn = part + x_
        r = lax.rsqrt(jnp.mean(xn * xn, axis=-1, keepdims=True) + RMS_EPS)
        hn = (xn * r * g_).astype(BF16)
        return (xn, hn) + (_column_halves(hn, core[0]) if core else ())

    wide = [(jax.ShapeDtypeStruct((SEQ, D_MODEL), F32), NORM_TILE), (jax.ShapeDtypeStruct((SEQ, D_MODEL), BF16), NORM_TILE)]
    halves = [(jax.ShapeDtypeStruct((SEQ, D_MODEL // 2), BF16), half_tile)] * 2
    return _mm(a, w, mode="nn", out_dtype=F32, name=name, carried=carried, fused=dict(
        tile=NORM_TILE, post=post, ins=[(x, NORM_TILE), (g, *vec)] + ([(core_row, *vec)] if core_row is not None else []),
        outs=wide + (halves if core_row is not None else [])))


def _norm_bwd_residual(dq, w, x, g, res, name, carried=()):
    def post(dh, x_, g_, res_):
        r = lax.rsqrt(jnp.mean(x_ * x_, axis=-1, keepdims=True) + RMS_EPS)
        xn = x_ * r
        dhg = dh * g_
        dx = r * (dhg - xn * jnp.mean(dhg * xn, axis=-1, keepdims=True)) + res_
        return dx, jnp.sum(dh * xn, axis=0, keepdims=True), dx

    vec = ((1, D_MODEL), lambda i, j: (0, j))
    return _mm(dq, w, mode="nt", out_dtype=F32, name=name, carried=carried, fused=dict(
        tile=NORM_TILE, ins=[(x, NORM_TILE), (g, *vec), (res, NORM_TILE)], post=post,
        outs=[(jax.ShapeDtypeStruct((SEQ, D_MODEL), F32), NORM_TILE), (jax.ShapeDtypeStruct((1, D_MODEL), F32), *vec),
              (jax.ShapeDtypeStruct((SEQ, D_MODEL), BF16), NORM_TILE)]))


FF_SHARD = D_FF // N_CHIPS


FF_TILE_ROWS = 512


def _swiglu_tile(ab):
    a, b = ab[:, :FF_SHARD], ab[:, FF_SHARD:]
    return a * _sigmoid(a) * b


def _swiglu_grad_tile(du, ab):
    a, b = ab[:, :FF_SHARD], ab[:, FF_SHARD:]
    sg = _sigmoid(a)
    return jnp.concatenate([du * b * (sg * (1.0 + a * (1.0 - sg))), du * (a * sg)], axis=1)


def _ff_up(hf, w13, name, carried=()):
    wide, narrow = (FF_TILE_ROWS, 2 * FF_SHARD), (FF_TILE_ROWS, FF_SHARD)
    return _mm(hf, w13, mode="nn", out_dtype=F32, name=name, carried=carried, fused=dict(
        tile=wide, ins=[],
        outs=[(jax.ShapeDtypeStruct((SEQ, 2 * D_FF), F32), wide), (jax.ShapeDtypeStruct((SEQ, D_FF), BF16), narrow)],
        post=lambda p: (p, _swiglu_tile(p))))


def _ff_down_bwd(dx3, w2, ab, name, carried=()):
    wide, narrow = (FF_TILE_ROWS, 2 * FF_SHARD), (FF_TILE_ROWS, FF_SHARD)
    out, res = _mm(dx3, w2, mode="nt", out_dtype=BF16, name=name, carried=carried, fused=dict(
        tile=narrow, ins=[(ab, wide)], outs=[(jax.ShapeDtypeStruct((SEQ, 2 * D_FF), BF16), wide)],
        post=lambda du, ab_: (_swiglu_grad_tile(du, ab_),)))
    return out[0], res


CROSS_ROWS = 512


def _cross_fwd(qc, kvc, name, carried=()):
    def kern(q_ref, k_ref, v_ref, o_ref):
        s = _dot_nt(q_ref[...], k_ref[...]) * ATT_SCALE
        m = jnp.max(s, axis=-1, keepdims=True)
        e = jnp.exp(s - m)
        p = e / jnp.sum(e, axis=-1, keepdims=True)
        o_ref[...] = _dot(p, v_ref[...]).astype(o_ref.dtype)

    qblk = pl.BlockSpec((CROSS_ROWS, HEAD_DIM), lambda h, i: (i, h))
    return _pcall(
        kern, name=name, grid=(CROSS_HEADS, SEQ // CROSS_ROWS),
        in_specs=[qblk, pl.BlockSpec((MEM_LEN, HEAD_DIM), lambda h, i: (0, h)),
                  pl.BlockSpec((MEM_LEN, HEAD_DIM), lambda h, i: (0, CROSS_HEADS + h))],
        out_specs=qblk, out_shape=jax.ShapeDtypeStruct((SEQ, CROSS_WIDTH), BF16), args=(qc, kvc, kvc),
        semantics=("parallel", "parallel"), carried=carried)


def _cross_bwd(qc, kvc, doc, name):
    def kern(q_ref, k_ref, v_ref, do_ref, dq_ref, dk_ref, dv_ref):
        q, k, v, do = q_ref[...], k_ref[...], v_ref[...], do_ref[...]
        s = _dot_nt(q, k) * ATT_SCALE
        m = jnp.max(s, axis=-1, keepdims=True)
        e = jnp.exp(s - m)
        p = e / jnp.sum(e, axis=-1, keepdims=True)
        dp = _dot_nt(do, v)
        ds = p * (dp - jnp.sum(dp * p, axis=-1, keepdims=True))
        dq_ref[...] = (_dot(ds, k) * ATT_SCALE).astype(dq_ref.dtype)
        dk = _dot_tn(ds, q) * ATT_SCALE
        dv = _dot_tn(p, do)

        @pl.when(pl.program_id(1) == 0)
        def _():
            dk_ref[...] = dk
            dv_ref[...] = dv

        @pl.when(pl.program_id(1) > 0)
        def _():
            dk_ref[...] += dk
            dv_ref[...] += dv

    qblk = pl.BlockSpec((CROSS_ROWS, HEAD_DIM), lambda h, i: (i, h))
    kblk = pl.BlockSpec((MEM_LEN, HEAD_DIM), lambda h, i: (0, h))
    dq, dk, dv = pl.pallas_call(
        kern, name=name, grid=(CROSS_HEADS, SEQ // CROSS_ROWS),
        in_specs=[qblk, kblk, pl.BlockSpec((MEM_LEN, HEAD_DIM), lambda h, i: (0, CROSS_HEADS + h)), qblk],
        out_specs=[qblk, kblk, kblk],
        out_shape=[jax.ShapeDtypeStruct((SEQ, CROSS_WIDTH), BF16),
                   jax.ShapeDtypeStruct((MEM_LEN, CROSS_WIDTH), F32),
                   jax.ShapeDtypeStruct((MEM_LEN, CROSS_WIDTH), F32)],
        compiler_params=_cparams(("parallel", "arbitrary")),
    )(qc, kvc, kvc, doc)
    return dq, jnp.concatenate([dk, dv], axis=1)


FULL_SPECS = {
    "w_in": ("col", D_MODEL, IN_WIDTH),
    "w_branch_a": ("col", ATT_OUT, D_MODEL),
    "w_branch_b": ("col", HG_WIDTH, D_MODEL),
    "w_out": ("row", D_MODEL, D_MODEL),
    "wq_cross": ("row", D_MODEL, CROSS_WIDTH),
    "wkv_cross": ("row", D_MODEL, 2 * CROSS_WIDTH),
    "wo_cross": ("col", CROSS_WIDTH, D_MODEL),
    "w13": ("col", D_MODEL, 2 * D_FF),
    "w2": ("row", D_FF, D_MODEL),
}
WEIGHT_PLACE = {
    "w_in": ("w_in", 0), "w_branch_a": ("w_branch_a", 0), "w_branch_b": ("w_branch_b", 0),
    "w_out": ("w_out", 0), "wq_cross": ("wq_cross", 0), "wkv_cross": ("wkv_cross", 0),
    "wo_cross": ("wo_cross", 0), "w1": ("w13", 0), "w3": ("w13", FF_SHARD), "w2": ("w2", 0),
}
BIG_WEIGHTS = tuple(WEIGHT_PLACE)
EW_BLOCK_ELEMS = 512 * 1024


def _position():
    return lax.axis_index("x"), lax.axis_index("y"), lax.axis_index("c")


def _other_chips(x, y):
    return [(1 - x, y), (x, 1 - y), (1 - x, 1 - y)]


def _half(ref, kind, h):
    r, c = ref.shape
    if kind == "col":
        return ref.at[pl.ds(h * (r // 2), r // 2), :]
    return ref.at[:, pl.ds(h * (c // 2), c // 2)]


def _shard_of(ref, kind, start, size):
    return ref.at[:, pl.ds(start, size)] if kind == "col" else ref.at[pl.ds(start, size), :]


def _rows_of(ref, r0, nrows):
    return ref if nrows is None else ref.at[pl.ds(r0, nrows), :]


def _half_shape(kind, rows, cols):
    return (rows // 2, cols) if kind == "col" else (rows, cols // 2)


def _slot_shape(spec):
    kind, rows, cols = spec
    hr, hc = _half_shape(kind, rows, cols)
    return (hr, hc // N_CHIPS) if kind == "col" else (hr // N_CHIPS, hc)


def _remote(src, dst, send_sem, recv_sem, device):
    return pltpu.make_async_remote_copy(src_ref=src, dst_ref=dst, send_sem=send_sem, recv_sem=recv_sem,
                                        device_id=device, device_id_type=MESH)


def _gather_ici_comm(fulls, jobs, specs):
    def piece(refs, job, chip, c):
        f, r0, nr = job
        kind, rows, cols = specs[f]
        stride = (cols if kind == "col" else rows) // N_CHIPS
        return _rows_of(_half(_shard_of(refs[f], kind, chip * stride, stride), kind, c), r0, nr)

    def start(refs, ss, rs):
        x, y, c = _position()
        j = 2 * x + y
        for q, job in enumerate(jobs):
            for p, (px, py) in enumerate(_other_chips(x, y)):
                _remote(piece(refs, job, j, c), piece(refs, job, j, c), ss.at[3 * q + p], rs.at[3 * q + p],
                        (px, py, c)).start()

    def finish(refs, ss, rs):
        x, y, c = _position()
        j = 2 * x + y
        for q, job in enumerate(jobs):
            for p, (px, py) in enumerate(_other_chips(x, y)):
                _remote(piece(refs, job, j, c), piece(refs, job, 2 * px + py, c), ss.at[3 * q + p],
                        rs.at[3 * q + p], (px, py, c)).wait_recv()
        for q, job in enumerate(jobs):
            for p, (px, py) in enumerate(_other_chips(x, y)):
                _remote(piece(refs, job, j, c), piece(refs, job, j, c), ss.at[3 * q + p], rs.at[3 * q + p],
                        (px, py, c)).wait_send()

    names = list(dict.fromkeys(job[0] for job in jobs))
    return _Carried({f: fulls[f] for f in names}, {}, 3 * len(jobs), start, finish)


def _gather_ring_comm(fulls, f, r0, nr, phase, specs):
    kind, _, cols = specs[f]
    assert kind == "col" and nr % 32 == 0
    stride = cols // N_CHIPS
    half = nr // 2

    def rows(refs, chip, c, lo, n):
        return _rows_of(_half(_shard_of(refs[f], kind, chip * stride, stride), kind, c), r0 + lo, n)

    def copies(refs, ss, rs):
        x, y, c = _position()
        me, nx, ny, dg = 2 * x + y, 2 * (1 - x) + y, 2 * x + (1 - y), 2 * (1 - x) + (1 - y)
        to_x, to_y = (1 - x, y, c), (x, 1 - y, c)
        if phase == "a":
            mine = rows(refs, me, c, 0, nr)
            return [(_remote(mine, mine, ss.at[0], rs.at[0], to_x), rows(refs, nx, c, 0, nr)),
                    (_remote(mine, mine, ss.at[1], rs.at[1], to_y), rows(refs, ny, c, 0, nr))]
        up, low = rows(refs, ny, c, half, half), rows(refs, nx, c, 0, half)
        return [(_remote(up, up, ss.at[0], rs.at[0], to_x), rows(refs, dg, c, half, half)),
                (_remote(low, low, ss.at[1], rs.at[1], to_y), rows(refs, dg, c, 0, half))]

    def start(refs, ss, rs):
        for cp, _ in copies(refs, ss, rs):
            cp.start()

    def finish(refs, ss, rs):
        x, y, c = _position()
        mine = copies(refs, ss, rs)
        for i, (_, landing) in enumerate(mine):
            _remote(landing, landing, ss.at[i], rs.at[i], (x, y, c)).wait_recv()
        for cp, _ in mine:
            cp.wait_send()

    return _Carried({f: fulls[f]}, {}, 2, start, finish)


def _gather_d2d_comm(fulls, jobs, specs):
    def rect(refs, job, h):
        f, r0, nr = job
        assert nr is None or specs[f][0] == "col"
        return _rows_of(_half(refs[f], specs[f][0], h), r0, nr)

    def start(refs, ss, rs):
        x, y, c = _position()
        for q, job in enumerate(jobs):
            _remote(rect(refs, job, c), rect(refs, job, c), ss.at[q], rs.at[q], (x, y, 1 - c)).start()

    def finish(refs, ss, rs):
        x, y, c = _position()
        for q, job in enumerate(jobs):
            _remote(rect(refs, job, 1 - c), rect(refs, job, 1 - c), ss.at[q], rs.at[q], (x, y, 1 - c)).wait_recv()
        for q, job in enumerate(jobs):
            _remote(rect(refs, job, c), rect(refs, job, c), ss.at[q], rs.at[q], (x, y, 1 - c)).wait_send()

    names = list(dict.fromkeys(job[0] for job in jobs))
    return _Carried({f: fulls[f] for f in names}, {}, len(jobs), start, finish)


def _pairx_comm(grads, names, specs, whole=False):
    def copies(refs, ss, rs):
        x, y, c = _position()
        src = (lambda f: refs[("g", f)]) if whole else (lambda f: _half(refs[("g", f)], specs[f][0], 1 - c))
        return [_remote(src(f), refs[("r", f)], ss.at[i], rs.at[i], (x, y, 1 - c)) for i, f in enumerate(names)]

    def start(refs, ss, rs):
        for cp in copies(refs, ss, rs):
            cp.start()

    def finish(refs, ss, rs):
        for cp in copies(refs, ss, rs):
            cp.wait_recv()
        for cp in copies(refs, ss, rs):
            cp.wait_send()

    fresh = {("r", f): jax.ShapeDtypeStruct(_half_shape(*specs[f]), BF16) for f in names}
    return _Carried({}, fresh, len(names), start, finish, reads={("g", f): grads[f] for f in names})


def _chipx_comm(pair_sums, slots, jobs, specs):
    def copies(refs, ss, rs):
        x, y, c = _position()
        out = []
        for q, (f, r0, nr) in enumerate(jobs):
            kind = specs[f][0]
            width = _slot_shape(specs[f])[1 if kind == "col" else 0]
            for p, (px, py) in enumerate(_other_chips(x, y)):
                src = _rows_of(_shard_of(refs[("p", f)], kind, (2 * px + py) * width, width), r0, nr)
                dst = _rows_of(refs[("s", f)].at[p], r0, nr)
                out.append(_remote(src, dst, ss.at[3 * q + p], rs.at[3 * q + p], (px, py, c)))
        return out

    def start(refs, ss, rs):
        for cp in copies(refs, ss, rs):
            cp.start()

    def finish(refs, ss, rs):
        for cp in copies(refs, ss, rs):
            cp.wait_recv()
        for cp in copies(refs, ss, rs):
            cp.wait_send()

    names = list(dict.fromkeys(job[0] for job in jobs))
    arrays = {("p", f): pair_sums[f] for f in names}
    arrays.update({("s", f): slots[f] for f in names})
    return _Carried(arrays, {}, 3 * len(jobs), start, finish)


def _share_comm(grads, wnames, specs, place):
    def start(refs, ss, rs):
        x, y, c = _position()
        for i, w in enumerate(wnames):
            kind = specs[place[w][0]][0]
            _remote(_half(refs[w], kind, c), _half(refs[w], kind, c), ss.at[i], rs.at[i], (x, y, 1 - c)).start()

    def finish(refs, ss, rs):
        x, y, c = _position()
        for i, w in enumerate(wnames):
            kind = specs[place[w][0]][0]
            _remote(_half(refs[w], kind, 1 - c), _half(refs[w], kind, 1 - c), ss.at[i], rs.at[i],
                    (x, y, 1 - c)).wait_recv()
        for i, w in enumerate(wnames):
            kind = specs[place[w][0]][0]
            _remote(_half(refs[w], kind, c), _half(refs[w], kind, c), ss.at[i], rs.at[i], (x, y, 1 - c)).wait_send()

    return _Carried({w: grads[w] for w in wnames}, {}, len(wnames), start, finish)


def _gather_rows(v, name="gather_small"):
    shape = v.shape

    def body(v_ref, out_ref, send_sem, recv_sem, loc_sem):
        x, y, c = _position()
        me = 4 * x + 2 * y + c
        flips = [(fx, fy, fc) for fx in (0, 1) for fy in (0, 1) for fc in (0, 1)][1:]

        def peer(fl):
            return tuple(1 - a if f else a for a, f in zip((x, y, c), fl))

        loc = pltpu.make_async_copy(v_ref, out_ref.at[me], loc_sem)
        loc.start()
        sends = []
        for i, fl in enumerate(flips):
            cp = _remote(v_ref, out_ref.at[me], send_sem.at[i], recv_sem.at[i], peer(fl))
            cp.start()
            sends.append(cp)
        for i, fl in enumerate(flips):
            px, py, pc = peer(fl)
            _remote(v_ref, out_ref.at[4 * px + 2 * py + pc], send_sem.at[i], recv_sem.at[i], peer(fl)).wait_recv()
        for cp in sends:
            cp.wait_send()
        loc.wait()

    return pl.pallas_call(
        body, name=name, in_specs=[ANY], out_specs=ANY,
        out_shape=jax.ShapeDtypeStruct((N_DEV,) + shape, F32),
        scratch_shapes=[pltpu.SemaphoreType.DMA((N_DEV - 1,)), pltpu.SemaphoreType.DMA((N_DEV - 1,)),
                        pltpu.SemaphoreType.DMA],
    )(v)


def _ew_block(rows, cols, elems=EW_BLOCK_ELEMS):
    tc = cols if cols <= 4096 else _div(cols, 2048, LANE)
    tr = _div(rows, max(16, elems // tc), 16)
    return tr, tc


def _mesh_scalars():
    x, y, c = _position()
    return jnp.stack([c, 2 * x + y]).astype(jnp.int32)


def _grid_spec(grid, in_specs, out_specs):
    return pltpu.PrefetchScalarGridSpec(num_scalar_prefetch=1, grid=grid, in_specs=in_specs, out_specs=out_specs)


def _cast_into_full(parts, fname, pos, specs, place, name, token=None):
    kind, rows, cols = specs[fname]
    ws = [w for w in place if place[w][0] == fname]
    if kind == "col":
        stride = cols // N_CHIPS
        hr = rows // 2
        tr = _div(hr, max(16, EW_BLOCK_ELEMS // stride), 16)
        nrb = hr // tr
        in_specs = [pl.BlockSpec((tr, parts[w].shape[1]), lambda i, pos_ref: (i + pos_ref[0] * nrb, 0)) for w in ws]
        out_spec = pl.BlockSpec((tr, stride), lambda i, pos_ref: (i + pos_ref[0] * nrb, pos_ref[1]))
    else:
        stride = rows // N_CHIPS
        hc = cols // 2
        tr = _div(stride, max(16, EW_BLOCK_ELEMS // hc), 16)
        nrb = stride // tr
        in_specs = [pl.BlockSpec((tr, hc), lambda i, pos_ref: (i, pos_ref[0])) for w in ws]
        out_spec = pl.BlockSpec((tr, hc), lambda i, pos_ref: (i + pos_ref[1] * nrb, pos_ref[0]))

    def kern(pos_ref, *refs):
        o_ref = refs[-1]
        for w, r in zip(ws, refs[:len(ws)]):
            off = place[w][1] if kind == "col" else 0
            o_ref[:, off:off + r.shape[1]] = r[...].astype(o_ref.dtype)

    tokens = [] if token is None else [token]
    in_specs = in_specs + [pl.BlockSpec(TOKEN_SHAPE, lambda i, pos_ref: (0, 0))] * len(tokens)
    return pl.pallas_call(
        kern, name=name, grid_spec=_grid_spec((nrb,), in_specs, out_spec),
        out_shape=jax.ShapeDtypeStruct((rows, cols), BF16),
        compiler_params=_cparams(("parallel",)),
    )(pos, *[parts[w] for w in ws], *tokens)


def _pair_sum(grad, recv, pos, spec, name, whole=False):
    kind, rows, cols = spec
    hr, hc = _half_shape(kind, rows, cols)
    tr, tc = _ew_block(hr, hc, 2 * EW_BLOCK_ELEMS)
    nrb, ncb = hr // tr, hc // tc
    blk = pl.BlockSpec((tr, tc), lambda i, jj, pos_ref: (i, jj))
    if whole:
        mine = blk
    elif kind == "col":
        mine = pl.BlockSpec((tr, tc), lambda i, jj, pos_ref: (i + pos_ref[0] * nrb, jj))
    else:
        mine = pl.BlockSpec((tr, tc), lambda i, jj, pos_ref: (i, jj + pos_ref[0] * ncb))

    def kern(pos_ref, g_ref, r_ref, o_ref, slots_ref):
        o_ref[...] = (g_ref[...].astype(F32) + r_ref[...].astype(F32)).astype(o_ref.dtype)

    return pl.pallas_call(
        kern, name=name, grid_spec=_grid_spec((nrb, ncb), [mine, blk], [blk, ANY]),
        out_shape=[jax.ShapeDtypeStruct((hr, hc), BF16),
                   jax.ShapeDtypeStruct((N_CHIPS - 1,) + _slot_shape(spec), BF16)],
        compiler_params=_cparams(("parallel", "parallel")),
    )(pos, grad, recv)


def _chip_sum(pair_sum, slots, pos, fname, shard_shapes, specs, place, name):
    kind, rows, cols = specs[fname]
    sr, sc = _slot_shape(specs[fname])
    ws = [w for w in place if place[w][0] == fname]
    n_slots = N_CHIPS - 1
    tr = _div(sr, max(16, EW_BLOCK_ELEMS // sc), 16)
    nrb = sr // tr
    slot = pl.BlockSpec((n_slots, tr, sc), lambda i, pos_ref: (0, i, 0))
    if kind == "col":
        own = pl.BlockSpec((tr, sc), lambda i, pos_ref: (i, pos_ref[1]))
        out_specs = [pl.BlockSpec((tr, shard_shapes[w][1]), lambda i, pos_ref: (i + pos_ref[0] * nrb, 0)) for w in ws]
    else:
        own = pl.BlockSpec((tr, sc), lambda i, pos_ref: (i + pos_ref[1] * nrb, 0))
        out_specs = [pl.BlockSpec((tr, sc), lambda i, pos_ref: (i, pos_ref[0])) for w in ws]

    def kern(pos_ref, own_ref, slot_ref, *out_refs):
        tot = own_ref[...].astype(F32)
        for s in range(n_slots):
            tot = tot + slot_ref[s].astype(F32)
        for w, o_ref in zip(ws, out_refs):
            off = place[w][1] if kind == "col" else 0
            o_ref[...] = tot[:, off:off + o_ref.shape[1]]

    outs = pl.pallas_call(
        kern, name=name, grid_spec=_grid_spec((nrb,), [own, slot], out_specs),
        out_shape=[jax.ShapeDtypeStruct(shard_shapes[w], F32) for w in ws],
        compiler_params=_cparams(("parallel",)),
    )(pos, pair_sum, slots)
    return dict(zip(ws, outs))


def _adam_math(w, g, m, v):
    m2 = ADAM_B1 * m + (1.0 - ADAM_B1) * g
    v2 = ADAM_B2 * v + (1.0 - ADAM_B2) * (g * g)
    m_hat = m2 / (1.0 - ADAM_B1 ** ADAM_STEP)
    v_hat = v2 / (1.0 - ADAM_B2 ** ADAM_STEP)
    delta = -ADAM_LR * (m_hat / (jnp.sqrt(v_hat) + ADAM_EPS) + ADAM_WD * w)
    return delta, m2, v2


def _adamw(w, g, m, v, name, carried=()):
    rows, cols = w.shape
    tr, tc = _ew_block(rows, cols)

    def kern(w_ref, g_ref, m_ref, v_ref, d_ref, m2_ref, v2_ref, g_out_ref):
        g_ = g_ref[...]
        d_ref[...], m2_ref[...], v2_ref[...] = _adam_math(w_ref[...], g_, m_ref[...], v_ref[...])
        g_out_ref[...] = g_

    blk = pl.BlockSpec((tr, tc), lambda i, j: (i, j))
    return _pcall(
        kern, name=name, grid=(rows // tr, cols // tc), in_specs=[blk] * 4, out_specs=[blk] * 4,
        out_shape=[jax.ShapeDtypeStruct((rows, cols), F32)] * 4, args=(w, g, m, v),
        semantics=("parallel", "parallel"), carried=carried)


SMALL_ROWS = ("ln_mix_w", "ln_cross_w", "ln_mem_w", "ln_ffn_w", "ln_final_w")
ROW_HG_NORM, ROW_LB0, ROW_LB1 = 5, 6, 7
LOSS_LANE0 = HEAD_DIM


def _pack_small(vals):
    rows = [vals[n].reshape(1, D_MODEL) for n in SMALL_ROWS]
    pad = lambda a: jnp.pad(a, ((0, 0), (0, D_MODEL - a.shape[1])))
    rows.append(pad(vals["hg_norm_w"].reshape(1, HEAD_DIM)))
    rows.append(pad(vals["hg_lower_bounds"].reshape(2, HG_WIDTH)))
    return jnp.concatenate(rows, axis=0)


def _small_update(gathered, w, m, v, name="small_update"):
    def kern(g_ref, w_ref, m_ref, v_ref, grad_ref, d_ref, m2_ref, v2_ref, loss_ref):
        tot = g_ref[0]
        for s in range(1, N_DEV):
            tot = tot + g_ref[s]
        wv = w_ref[...]
        row = lax.broadcasted_iota(jnp.int32, (8, D_MODEL), 0)
        lane = lax.broadcasted_iota(jnp.int32, (8, D_MODEL), 1)
        l0, l1 = wv[ROW_LB0:ROW_LB0 + 1], wv[ROW_LB1:ROW_LB1 + 1]
        mx = jnp.maximum(l0, l1)
        e0, e1 = jnp.exp(l0 - mx), jnp.exp(l1 - mx)
        p0 = e0 / (e0 + e1)
        dlog = tot[ROW_LB0:ROW_LB0 + 1] * p0 * (1.0 - p0)
        tot = jnp.where(row == ROW_HG_NORM, tot + tot[ROW_LB1:ROW_LB1 + 1], tot)
        grad = jnp.where(row == ROW_LB0, dlog, jnp.where(row == ROW_LB1, -dlog, tot))
        grad = jnp.where((row == ROW_HG_NORM) & (lane >= HEAD_DIM), 0.0, grad)
        grad = jnp.where((row >= ROW_LB0) & (lane >= HG_WIDTH), 0.0, grad)
        grad_ref[...] = grad
        d_ref[...], m2_ref[...], v2_ref[...] = _adam_math(wv, grad, m_ref[...], v_ref[...])
        loss_ref[...] = tot[ROW_HG_NORM:ROW_HG_NORM + 1, LOSS_LANE0:LOSS_LANE0 + LANE]

    full = pl.BlockSpec((8, D_MODEL), lambda: (0, 0))
    return pl.pallas_call(
        kern, name=name,
        in_specs=[pl.BlockSpec((N_DEV, 8, D_MODEL), lambda: (0, 0, 0)), full, full, full],
        out_specs=[full, full, full, full, pl.BlockSpec((1, LANE), lambda: (0, 0))],
        out_shape=[jax.ShapeDtypeStruct((8, D_MODEL), F32)] * 4 + [jax.ShapeDtypeStruct((1, LANE), F32)],
        compiler_params=_cparams(),
    )(gathered, w, m, v)


def _unpack_small(p, shapes):
    out = {n: p[i].reshape(shapes[n]) for i, n in enumerate(SMALL_ROWS)}
    out["hg_norm_w"] = p[ROW_HG_NORM, :HEAD_DIM].reshape(shapes["hg_norm_w"])
    out["hg_lower_bounds"] = p[ROW_LB0:ROW_LB1 + 1, :HG_WIDTH].reshape(shapes["hg_lower_bounds"])
    return out


def _concat_cols(pieces, name):
    rows = pieces[0].shape[0]
    widths = [p.shape[1] for p in pieces]
    tr = ROW_BLOCK // 2

    def kern(*refs):
        o_ref, off = refs[-1], 0
        for r, w in zip(refs[:-1], widths):
            o_ref[:, off:off + w] = r[...]
            off += w

    return pl.pallas_call(
        kern, name=name, grid=(rows // tr,),
        in_specs=[pl.BlockSpec((tr, w), lambda i: (i, 0)) for w in widths],
        out_specs=pl.BlockSpec((tr, sum(widths)), lambda i: (i, 0)),
        out_shape=jax.ShapeDtypeStruct((rows, sum(widths)), pieces[0].dtype),
        compiler_params=_cparams(("parallel",)),
    )(*pieces)


WHOLE = lambda f: (f, 0, None)
MID_MATRICES = ("w_branch_a", "w_branch_b", "w_out", "wq_cross", "wkv_cross", "wo_cross")
MID_WEIGHTS = MID_MATRICES
W_IN_PIECES = [("w_in", r0, 512) for r0 in range(0, D_MODEL // 2, 512)]
W13_PIECES = [("w13", r0, 512) for r0 in range(0, D_MODEL // 2, 512)]
GATHER_GROUPS = [("mid", [WHOLE(f) for f in MID_MATRICES]), ("w13a", W13_PIECES[:1]), ("w13b", W13_PIECES[1:]),
                 ("w2", [WHOLE("w2")])]
OTHER_WEIGHTS = ["w1", "w3", "w2"] + list(MID_WEIGHTS)
BEFORE = {
    "hgrn_fwd": [("wait", "mid")],
    "mm_out": [("wait", "w13a")],
    "mm_o": [("wait", "w13b")],
    "mm_w2": [("wait", "w2"), ("run", ("d2d", [WHOLE("w2")]), "gather_hand_over_w2")],
    "mm_dh": [("wait", "rs_w2"), ("chip_sum", "w2"), ("wait", "rs_w13"), ("chip_sum", "w13"), ("wait", "rs_mid")]
    + [("chip_sum", f) for f in MID_MATRICES],
}
CARRY = {
    "hgrn_fwd": [("d2d", [WHOLE(f) for f in MID_MATRICES])],
    "mm_out": [("d2d", W13_PIECES[:1])],
    "mm_o": [("d2d", W13_PIECES[1:])],
    "mm_du": [("pairx_whole", ["w2"])],
    "mm_dhf": [("pairx_whole", ["w13"])],
    "attn_bwd_g0": [("pairx", list(MID_MATRICES))],
    "mm_dwin_own": [("pairx_whole", ["w_in"])],
    "mm_dh": [("share", OTHER_WEIGHTS)],
}
AFTER = {
    "mm_dw2_own": [("start", "rs_w2", [WHOLE("w2")])],
    "mm_dw13_own": [("start", "rs_w13", [WHOLE("w13")])],
    "attn_bwd_g0": [("pair_sum", f) for f in MID_MATRICES] + [("start", "rs_mid", [WHOLE(f) for f in MID_MATRICES])],
    "mm_dwin_own": [("pair_sum", "w_in"), ("start", "rs_w_in", [WHOLE("w_in")])],
}
FINISH = [
    ("adamw", OTHER_WEIGHTS), ("wait", "rs_w_in"), ("small",), ("chip_sum", "w_in"),
    ("run", ("share", ["w_in"]), "rs_sibling_share_w_in"), ("adamw", ["w_in"]),
]


class _Late:
    def __init__(self, read):
        self.read = read


class _Net:
    def __init__(self, full, pos=None, shard_shapes=None, comm=True, specs=FULL_SPECS, place=WEIGHT_PLACE):
        self.full, self.pos, self.shard_shapes, self.comm = dict(full), pos, shard_shapes, comm
        self.specs, self.place = specs, place
        self.gw, self.recv, self.psum, self.slots, self.grads = {}, {}, {}, {}, {}
        self.gw_sibling = {}
        self.pending, self.token, self.last = {}, None, None

    def _make(self, kind, arg):
        if kind == "gather":
            return _gather_ici_comm(self.full, arg, self.specs)
        if kind == "ring":
            return _gather_ring_comm(self.full, *arg, self.specs)
        if kind == "d2d":
            return _gather_d2d_comm(self.full, arg, self.specs)
        if kind == "pairx":
            return _pairx_comm(self.gw, arg, self.specs)
        if kind == "pairx_whole":
            return _pairx_comm(self.gw_sibling, arg, self.specs, whole=True)
        if kind == "chipx":
            return _chipx_comm(self.psum, self.slots, arg, self.specs)
        assert kind == "share"
        return _share_comm(self.grads, arg, self.specs, self.place)

    def _store(self, kind, res):
        if kind in ("gather", "ring", "d2d"):
            self.full.update(res)
        elif kind in ("pairx", "pairx_whole"):
            for (tag, f), a in res.items():
                (self.gw if tag == "g" else self.recv)[f] = a
        elif kind == "chipx":
            for (tag, f), a in res.items():
                (self.psum if tag == "p" else self.slots)[f] = a
        else:
            self.grads.update(res)

    def run_comm(self, item, name):
        kind, arg = item
        self._store(kind, _run_comm([self._make(kind, arg)], name)[0])

    @staticmethod
    def _others(after, items):
        own = [a for cm in items for a in cm.arrays.values()]
        return [a for a in after if a is not None and all(a is not o for o in own)]

    def start(self, groups, kind, name):
        items = [self._make(kind, jobs) for _, jobs in groups]
        after = self._others([self.last], items)
        res, sems, token = _split_start(items, name, after=after[0] if after else None)
        for (group, jobs), r, s in zip(groups, res, sems):
            self._store(kind, r)
            self.pending[group] = (kind, jobs, s)
        self.token = self.last = token

    def wait(self, group, after=()):
        kind, jobs, sems = self.pending.pop(group)
        item = self._make(kind, jobs)
        res = _split_wait([item], [sems], self._others([self.last, *after], [item]), f"wait_{group}")[0]
        self._store(kind, res)

    def step(self, step):
        if step[0] == "wait":
            self.wait(step[1])
        elif step[0] == "start":
            self.start([(step[1], step[2])], "chipx", f"start_{step[1]}")
        elif step[0] == "pair_sum":
            f = step[1]
            self.psum[f], self.slots[f] = _pair_sum(self.gw[f], self.recv[f], self.pos, self.specs[f],
                                                    f"rs_pair_sum_{f}", whole=f in self.gw_sibling)
        elif step[0] == "chip_sum":
            f = step[1]
            self.grads.update(_chip_sum(self.psum[f], self.slots[f], self.pos, f, self.shard_shapes,
                                        self.specs, self.place, f"rs_chip_sum_{f}"))
        else:
            assert step[0] == "run"
            self.run_comm(step[1], step[2])

    def call(self, fn, name, *args, grad_of=None, sibling_half=False, pair_sum_of=None, **kw):
        for step in (BEFORE.get(name, []) if self.comm else []):
            self.step(step)
        late = lambda a: a.read() if isinstance(a, _Late) else a
        args = [late(a) for a in args]
        kw = {key: late(val) for key, val in kw.items()}
        items = CARRY.get(name, []) if self.comm else []
        carried = [self._make(k, a) for k, a in items]
        if self.token is not None:
            carried.append(_Token(self.token))
            self.token = None
        out, res = fn(*args, name=name, carried=carried, **kw)
        if grad_of is not None:
            (self.gw_sibling if sibling_half else self.gw)[grad_of] = out
        if pair_sum_of is not None:
            self.psum[pair_sum_of] = out
            self.slots[pair_sum_of] = lax.empty((N_CHIPS - 1,) + _slot_shape(self.specs[pair_sum_of]), BF16)
        self.last = jax.tree.leaves(out)[0]
        for (kind, _), r in zip(items, res):
            self._store(kind, r)
        for step in (AFTER.get(name, []) if self.comm else []):
            self.step(step)
        return out


def _local_step(net, x, h, mem, target, small, h_halves=None, core_row=None):
    full, call = net.full, net.call
    proj = call(_mm, "mm_proj", h, full["w_in"], mode="nn", out_dtype=F32)
    att = [call(_attn_fwd, f"attn_fwd_g{g}", proj, g) for g in range(3)]
    outs, lses = [a[0] for a in att], [a[1] for a in att]
    o_att = _attn_merge_fwd(outs, lses, "attn_merge")
    oraw, o_hg, states = call(_hg_fwd, "hgrn_fwd", proj, small["hg_lower_bounds"], small["hg_norm_w"])
    ya = call(_mm, "mm_branch_a", o_att, full["w_branch_a"], mode="nn", out_dtype=F32)
    yb, merged = call(_branch_b_gate, "mm_branch_b", o_hg, full["w_branch_b"], proj, ya)
    x1, hc = call(_residual_norm, "mm_out", merged, full["w_out"], x, small["ln_cross_w"])

    mn = _rms_fwd(mem, small["ln_mem_w"], "rms_mem")
    qc = call(_mm, "mm_q", hc, full["wq_cross"], mode="nn", out_dtype=F32)
    kvc = call(_mm, "mm_kv", mn, full["wkv_cross"], mode="nn", out_dtype=F32)
    oc = call(_cross_fwd, "cross_fwd", qc, kvc)
    x2, hf, *hf_halves = call(_residual_norm, "mm_o", oc, full["wo_cross"], x1, small["ln_ffn_w"], core_row=core_row)

    ab, u = call(_ff_up, "mm_w13", hf, full["w13"])
    x3 = call(_mm, "mm_w2", u, _Late(lambda: full["w2"]), mode="nn", out_dtype=F32, res=x2)

    dx3, dg_final, loss, dx3_bf16, *dx3_halves = _loss_head(x3, small["ln_final_w"], target, "loss_head", core_row)

    gs = {"ln_final_w": dg_final}
    if net.comm:
        call(_mm, "mm_dw2_sibling", u, dx3_halves[0], mode="tn", out_dtype=BF16, grad_of="w2", sibling_half=True)
        dab = call(_ff_down_bwd, "mm_du", dx3_bf16, full["w2"], ab)
        call(_mm, "mm_dw2_own", u, dx3_halves[1], mode="tn", out_dtype=BF16, res=_Late(lambda: net.recv["w2"]),
             pair_sum_of="w2")
        call(_mm, "mm_dw13_sibling", hf_halves[0], dab, mode="tn", out_dtype=BF16, grad_of="w13", sibling_half=True)
        dhf = call(_mm, "mm_dhf", dab, full["w13"], mode="nt", out_dtype=F32)
        call(_mm, "mm_dw13_own", hf_halves[1], dab, mode="tn", out_dtype=BF16, res=_Late(lambda: net.recv["w13"]),
             pair_sum_of="w13")
    else:
        call(_mm, "mm_dw2", u, dx3_bf16, mode="tn", out_dtype=BF16, grad_of="w2")
        dab = call(_ff_down_bwd, "mm_du", dx3_bf16, full["w2"], ab)
        call(_mm, "mm_dw13", hf, dab, mode="tn", out_dtype=BF16, grad_of="w13")
        dhf = call(_mm, "mm_dhf", dab, full["w13"], mode="nt", out_dtype=F32)
    dx2, gs["ln_ffn_w"], dx2_bf16 = _rms_bwd(x2, small["ln_ffn_w"], dhf, dx3, "rms_ffn_bwd", bf16_copy=True)
    doc = call(_mm, "mm_doc", dx2_bf16, full["wo_cross"], mode="nt", out_dtype=BF16)
    call(_mm, "mm_dwo", oc, dx2_bf16, mode="tn", out_dtype=BF16, grad_of="wo_cross")
    dqc, dkvc = _cross_bwd(qc, kvc, doc, "cross_bwd")
    call(_mm, "mm_dwq", hc, dqc, mode="tn", out_dtype=BF16, grad_of="wq_cross")
    dx1, gs["ln_cross_w"], dx1_bf16 = call(_norm_bwd_residual, "mm_dhc", dqc, full["wq_cross"], x1,
                                           small["ln_cross_w"], dx2)
    call(_mm, "mm_dwkv", mn, dkvc, mode="tn", out_dtype=BF16, grad_of="wkv_cross")
    dmn = call(_mm, "mm_dmn", dkvc, full["wkv_cross"], mode="nt", out_dtype=F32)
    _, gs["ln_mem_w"] = _rms_bwd(mem, small["ln_mem_w"], dmn, None, "rms_mem_bwd")
    dya, dyb, dga, dgb = call(_dmerged_gate_bwd, "mm_dmerged", dx1_bf16, full["w_out"], proj, ya, yb)
    call(_mm, "mm_dwout", merged, dx1_bf16, mode="tn", out_dtype=BF16, grad_of="w_out")
    call(_mm, "mm_dwa", o_att, dya, mode="tn", out_dtype=BF16, grad_of="w_branch_a")
    do_att = call(_mm, "mm_doatt", dya, full["w_branch_a"], mode="nt", out_dtype=F32)
    call(_mm, "mm_dwb", o_hg, dyb, mode="tn", out_dtype=BF16, grad_of="w_branch_b")
    do_hg = call(_mm, "mm_dohg", dyb, full["w_branch_b"], mode="nt", out_dtype=F32)
    dqh, dfh, dih, dgh, dlb, gs["hg_norm_w"] = call(
        _hg_bwd, "hgrn_bwd", proj, small["hg_lower_bounds"], small["hg_norm_w"], oraw, states, do_hg)
    gs["hg_lb"] = dlb
    do_gs, dl_gs = call(_attn_merge_bwd, "attn_merge_bwd", outs, lses, do_att)
    dqs, dks, dvs = zip(*[call(_attn_bwd, f"attn_bwd_g{g}", proj, g, lses[g], do_gs[g], dl_gs[g]) for g in range(3)])
    dproj = _concat_cols([*dqs, *dks, *dvs, dqh, dfh, dih, dgh, dga, dgb], "dproj_concat")
    if net.comm:
        h_sibling, h_own = h_halves
        call(_mm, "mm_dwin_sibling", h_sibling, dproj, mode="tn", out_dtype=BF16, grad_of="w_in", sibling_half=True)
        call(_mm, "mm_dwin_own", h_own, dproj, mode="tn", out_dtype=BF16, grad_of="w_in")
    else:
        call(_mm, "mm_dwin", h, dproj, mode="tn", out_dtype=BF16, grad_of="w_in")
    dh = call(_mm, "mm_dh", dproj, full["w_in"], mode="nt", out_dtype=F32)
    dx, gs["ln_mix_w"] = _rms_bwd(x, small["ln_mix_w"], dh, dx1, "rms_mix_bwd")
    return loss, dx, gs


WEIGHT_ORDER = ("ln_mix_w", "w_in", "hg_norm_w", "hg_lower_bounds", "w_branch_a", "w_branch_b", "w_out",
                "ln_cross_w", "ln_mem_w", "wq_cross", "wkv_cross", "wo_cross", "ln_ffn_w", "w1", "w3", "w2",
                "ln_final_w")


def kernel(x, mem, ln_mix_w, w_in, hg_norm_w, hg_lower_bounds, w_branch_a, w_branch_b, w_out, ln_cross_w, ln_mem_w, wq_cross, wkv_cross, wo_cross, ln_ffn_w, w1, w3, w2, ln_final_w, loss_target, m_ln_mix_w, m_w_in, m_hg_norm_w, m_hg_lower_bounds, m_w_branch_a, m_w_branch_b, m_w_out, m_ln_cross_w, m_ln_mem_w, m_wq_cross, m_wkv_cross, m_wo_cross, m_ln_ffn_w, m_w1, m_w3, m_w2, m_ln_final_w, v_ln_mix_w, v_w_in, v_hg_norm_w, v_hg_lower_bounds, v_w_branch_a, v_w_branch_b, v_w_out, v_ln_cross_w, v_ln_mem_w, v_wq_cross, v_wkv_cross, v_wo_cross, v_ln_ffn_w, v_w1, v_w3, v_w2, v_ln_final_w):
    args = dict(locals())
    w = {n: args[n] for n in WEIGHT_ORDER}
    m = {n: args["m_" + n] for n in WEIGHT_ORDER}
    v = {n: args["v_" + n] for n in WEIGHT_ORDER}
    shapes = {n: w[n].shape for n in WEIGHT_ORDER}
    mat = lambda a: a.reshape(a.shape[-2:])
    shard_shapes = {n: shapes[n][-2:] for n in BIG_WEIGHTS}

    pos = _mesh_scalars()

    def cast(f, token=None):
        return _cast_into_full({n: mat(w[n]) for n in BIG_WEIGHTS if WEIGHT_PLACE[n][0] == f}, f, pos,
                               FULL_SPECS, WEIGHT_PLACE, f"cast_{f}", token)

    net = _Net({"w_in": cast("w_in")}, pos, shard_shapes)
    net.start([(f"ring_a{i}", (*job, "a")) for i, job in enumerate(W_IN_PIECES)], "ring", "gather_start_w_in")
    rest = {f: cast(f, net.token) for f in FULL_SPECS if f != "w_in"}
    net.full.update(rest)
    small = {n: w[n].reshape(1, -1) for n in SMALL_ROWS}
    small["hg_norm_w"] = w["hg_norm_w"].reshape(1, HEAD_DIM)
    small["hg_lower_bounds"] = w["hg_lower_bounds"]
    x2d = x.reshape(SEQ, D_MODEL)
    h, *h_halves = _rms_fwd_halves(x2d, small["ln_mix_w"], pos, "rms_mix")
    for i, job in enumerate(W_IN_PIECES):
        net.wait(f"ring_a{i}", after=[*rest.values(), h] if i == 0 else ())
        net.start([(f"ring_b{i}", (*job, "b"))], "ring", f"gather_pass_on_w_in{i}")
    net.start(GATHER_GROUPS, "gather", "gather_start_rest")
    for i, job in enumerate(W_IN_PIECES):
        net.wait(f"ring_b{i}")
        net.run_comm(("d2d", [job]), f"gather_hand_over_w_in{i}")
    core_row = jnp.full((1, D_MODEL), lax.axis_index("c").astype(F32))
    loss, dx, gs = _local_step(net, x2d, h, mem.reshape(MEM_LEN, D_MODEL), loss_target.reshape(SEQ, D_MODEL), small,
                               h_halves, core_row)

    out_g, out_d, out_m, out_v = {}, {}, {}, {}
    net.last = dx
    for step in FINISH:
        if step[0] == "adamw":
            for n in step[1]:
                out_d[n], out_m[n], out_v[n], out_g[n] = net.call(_adamw, f"adamw_{n}", mat(w[n]), net.grads[n],
                                                                  mat(m[n]), mat(v[n]))
        elif step[0] == "wait":
            net.wait(step[1], after=list(out_d.values()))
        elif step[0] == "small":
            pad = lambda a: jnp.pad(a, ((0, 0), (0, D_MODEL - a.shape[1])))
            part = jnp.concatenate(
                [gs[n] for n in SMALL_ROWS]
                + [pad(jnp.concatenate([gs["hg_norm_w"][0], loss], axis=1)), pad(gs["hg_lb"]),
                   pad(gs["hg_norm_w"][1]) if len(gs["hg_norm_w"]) > 1 else jnp.zeros((1, D_MODEL), F32)], axis=0)
            part, net.psum["w_in"] = lax.optimization_barrier((part, net.psum["w_in"]))
            sg, sd, sm, sv, loss_tot = _small_update(_gather_rows(part), _pack_small(w), _pack_small(m),
                                                     _pack_small(v))
            for dst, packed in ((out_g, sg), (out_d, sd), (out_m, sm), (out_v, sv)):
                dst.update(_unpack_small(packed, shapes))
        else:
            net.step(step)

    result = [loss_tot[0, 0], dx.reshape(x.shape)]
    for group in (out_g, out_d, out_m, out_v):
        result += [group[n].reshape(shapes[n]) for n in WEIGHT_ORDER]
    return tuple(result)
```

```python
import math

import jax
import jax.numpy as jnp
from jax import lax
from jax.experimental import pallas as pl
from jax.experimental.pallas import tpu as pltpu

F32 = jnp.float32
BF16 = jnp.bfloat16
MESH = pl.DeviceIdType.MESH

D_MODEL = 2048
SEQ = 2048
HEAD_DIM = 128
MEM_LEN = 256
ATT_GROUPS = ((128, 1), (512, 4), (2048, 16))
ATT_HEADS = 4
ATT_WIDTH = 3 * ATT_HEADS * HEAD_DIM
ATT_OUT = ATT_HEADS * HEAD_DIM
ATT_BLOCK = 128
HG_HEADS = 8
HG_WIDTH = HG_HEADS * HEAD_DIM
HG_CHUNK = 64
IN_WIDTH = 3 * ATT_WIDTH + 4 * HG_WIDTH + 2 * D_MODEL
CROSS_HEADS = 4
CROSS_WIDTH = CROSS_HEADS * HEAD_DIM
D_FF = 5632
RMS_EPS = 1e-6
ADAM_LR = 0.001
ADAM_B1 = 0.9
ADAM_B2 = 0.999
ADAM_EPS = 1e-08
ADAM_WD = 0.01
ADAM_STEP = 10
N_CHIPS = 4
N_DEV = 8

VMEM_LIMIT_BYTES = 56 * 1024 * 1024
LANE = 128
MXU_WIDTH = 256
MM_TILE_CAP = 1536
TRANSPOSE_CHUNK = 512
ANY = pl.BlockSpec(memory_space=pl.ANY)


def _cparams(sem=None):
    return pltpu.CompilerParams(dimension_semantics=sem, vmem_limit_bytes=VMEM_LIMIT_BYTES)


def _div(n, cap, mult):
    best = None
    for d in range(mult, min(n, cap) + 1, mult):
        if n % d == 0:
            best = d
    assert best is not None, (n, cap, mult)
    return best


def _sigmoid(x):
    return 1.0 / (1.0 + jnp.exp(-x))


def _dot(a, b):
    return jnp.dot(a.astype(BF16), b.astype(BF16), preferred_element_type=F32)


def _dot_nt(a, b):
    return lax.dot_general(a.astype(BF16), b.astype(BF16), (((1,), (1,)), ((), ())),
                           preferred_element_type=F32)


def _dot_tn(a, b):
    return jnp.dot(a.astype(F32).T.astype(BF16), b.astype(BF16), preferred_element_type=F32)


def _dot_exact(a, b):
    return jnp.dot(a, b, precision=lax.Precision.HIGHEST, preferred_element_type=F32)


class _Carried:
    def __init__(self, arrays, fresh, n_sems, start, finish, mid=None, reads=None):
        self.arrays, self.fresh, self.n_sems, self.reads = arrays, fresh, n_sems, reads or {}
        self.start, self.mid, self.finish = start, mid, finish


class _Token:
    def __init__(self, array):
        self.array = array


TOKEN_SHAPE = (8, LANE)


def _carried_layout(carried):
    akeys = list(dict.fromkeys(k for cm in carried for k in cm.arrays))
    fkeys = [(ci, k) for ci, cm in enumerate(carried) for k in cm.fresh]
    arrays = [next(cm.arrays[k] for cm in carried if k in cm.arrays) for k in akeys]
    shapes = [jax.ShapeDtypeStruct(a.shape, a.dtype) for a in arrays] + [carried[ci].fresh[k] for ci, k in fkeys]
    sems = []
    for cm in carried:
        sems += [pltpu.SemaphoreType.DMA((cm.n_sems,)), pltpu.SemaphoreType.DMA((cm.n_sems,))]
    return akeys, fkeys, arrays, shapes, sems


def _carried_reads(carried):
    rkeys = list(dict.fromkeys(k for cm in carried for k in cm.reads))
    return rkeys, [next(cm.reads[k] for cm in carried if k in cm.reads) for k in rkeys]


def _carried_results(carried, akeys, fkeys, outs, rkeys=(), read_refs=()):
    shared = dict(zip(akeys, outs[:len(akeys)]))
    shared.update(zip(rkeys, read_refs))
    res = [{k: shared[k] for k in list(cm.arrays) + [r for r in cm.reads if r in shared]} for cm in carried]
    for (ci, k), o in zip(fkeys, outs[len(akeys):]):
        res[ci][k] = o
    return res


def _pcall(kern, *, name, grid, in_specs, out_specs, out_shape, args, scratch_shapes=(), semantics=None,
           carried=()):
    tokens = [c.array for c in carried if isinstance(c, _Token)]
    carried = [c for c in carried if not isinstance(c, _Token)]
    single = not isinstance(out_shape, (list, tuple))
    out_specs = [out_specs] if single else list(out_specs)
    out_shape = [out_shape] if single else list(out_shape)
    n_real, n_out, n_scr = len(in_specs), len(out_shape), len(scratch_shapes)
    in_specs = list(in_specs) + [pl.BlockSpec(TOKEN_SHAPE, lambda *_: (0, 0))] * len(tokens)
    args = list(args) + tokens
    n_in = len(in_specs)
    if not carried:
        def plain(*refs):
            kern(*refs[:n_real], *refs[n_in:])

        outs = pl.pallas_call(plain if tokens else kern, name=name, grid=grid, in_specs=in_specs,
                              out_specs=out_specs, out_shape=out_shape, scratch_shapes=list(scratch_shapes),
                              compiler_params=_cparams(semantics))(*args)
        return (outs[0] if single else list(outs)), []
    akeys, fkeys, arrays, shapes, sems = _carried_layout(carried)
    rkeys, reads = _carried_reads(carried)
    n_a, n_f, n_r = len(akeys), len(fkeys), len(rkeys)
    total = math.prod(grid)
    mid_step = min(total - 1, (17 * total) // 20)

    def wrapped(*refs):
        ins = refs[:n_real]
        r0 = n_in + n_a
        o0 = r0 + n_r
        outs = refs[o0:o0 + n_out]
        a0 = o0 + n_out
        s0 = a0 + n_a + n_f
        per = _carried_results(carried, akeys, fkeys, refs[a0:s0], rkeys, refs[r0:o0])
        scratch = refs[s0:s0 + n_scr]
        sem = refs[s0 + n_scr:]
        step = 0
        for d, g in enumerate(grid):
            step = step * g + pl.program_id(d)

        @pl.when(step == 0)
        def _():
            for ci, cm in enumerate(carried):
                cm.start(per[ci], sem[2 * ci], sem[2 * ci + 1])

        kern(*ins, *outs, *scratch)

        @pl.when(step == mid_step)
        def _():
            for ci, cm in enumerate(carried):
                if cm.mid is not None:
                    cm.mid(per[ci], sem[2 * ci], sem[2 * ci + 1])

        @pl.when(step == total - 1)
        def _():
            for ci, cm in enumerate(carried):
                cm.finish(per[ci], sem[2 * ci], sem[2 * ci + 1])

    outs = pl.pallas_call(
        wrapped, name=name, grid=grid,
        in_specs=list(in_specs) + [ANY] * (n_a + n_r), out_specs=out_specs + [ANY] * (n_a + n_f),
        out_shape=out_shape + shapes,
        input_output_aliases={n_in + i: n_out + i for i in range(n_a)},
        scratch_shapes=list(scratch_shapes) + sems,
        compiler_params=_cparams(("arbitrary",) * len(grid)),
    )(*args, *arrays, *reads)
    res = _carried_results(carried, akeys, fkeys, outs[n_out:])
    return (outs[0] if single else list(outs[:n_out])), res


def _run_comm(carried, name):
    carried = list(carried)
    akeys, fkeys, arrays, shapes, sems = _carried_layout(carried)
    rkeys, reads = _carried_reads(carried)
    n_a, n_f, n_r = len(akeys), len(fkeys), len(rkeys)

    def body(*refs):
        o0 = n_a + n_r
        per = _carried_results(carried, akeys, fkeys, refs[o0:o0 + n_a + n_f], rkeys, refs[n_a:o0])
        sem = refs[o0 + n_a + n_f:]
        for hook in ("start", "mid", "finish"):
            for ci, cm in enumerate(carried):
                fn = getattr(cm, hook)
                if fn is not None:
                    fn(per[ci], sem[2 * ci], sem[2 * ci + 1])

    outs = pl.pallas_call(
        body, name=name, in_specs=[ANY] * (n_a + n_r), out_specs=[ANY] * (n_a + n_f), out_shape=shapes,
        input_output_aliases={i: i for i in range(n_a)}, scratch_shapes=sems,
    )(*arrays, *reads)
    return _carried_results(carried, akeys, fkeys, outs)


HBM_SPEC = pl.BlockSpec(memory_space=pltpu.HBM)
SEM_SPEC = pl.BlockSpec(memory_space=pltpu.SEMAPHORE)
SPLIT_EFFECT = pltpu.SideEffectType.DATAFLOW_SIDE_EFFECTING


def _in_hbm(a):
    return pltpu.with_memory_space_constraint(a, pltpu.HBM)


def _split_start(items, name, after=None):
    items = list(items)
    akeys, fkeys, arrays, shapes, sems = _carried_layout(items)
    assert not fkeys
    rkeys, reads = _carried_reads(items)
    n_a, n_s = len(akeys), len(sems)
    n_after = n_a + (after is not None)
    n_in = n_after + len(rkeys)

    def body(*refs):
        per = _carried_results(items, akeys, [], refs[n_in:n_in + n_a], rkeys, refs[n_after:n_in])
        sem = refs[n_in + n_a:n_in + n_a + n_s]
        for ci, cm in enumerate(items):
            cm.start(per[ci], sem[2 * ci], sem[2 * ci + 1])
        token = refs[n_in + n_a + n_s]
        token[...] = jnp.zeros_like(token)

    outs = pl.pallas_call(
        body, name=name, in_specs=[HBM_SPEC] * n_a + [ANY] * (n_in - n_a),
        out_specs=[HBM_SPEC] * n_a + [SEM_SPEC] * n_s + [pl.BlockSpec(memory_space=pltpu.VMEM)],
        out_shape=[pltpu.HBM(s.shape, s.dtype) for s in shapes] + sems + [jax.ShapeDtypeStruct(TOKEN_SHAPE, F32)],
        input_output_aliases={i: i for i in range(n_a)},
        compiler_params=pltpu.CompilerParams(has_side_effects=SPLIT_EFFECT),
    )(*[_in_hbm(a) for a in arrays], *([after] if after is not None else []), *reads)
    res = _carried_results(items, akeys, [], outs[:n_a])
    sem_out = outs[n_a:n_a + n_s]
    return res, [(sem_out[2 * ci], sem_out[2 * ci + 1]) for ci in range(len(items))], outs[-1]


def _split_wait(items, sems, after, name):
    items = list(items)
    after = list(after) if isinstance(after, (list, tuple)) else [after]
    akeys, fkeys, arrays, shapes, _ = _carried_layout(items)
    rkeys, reads = _carried_reads(items)
    n_a, n_s = len(akeys), 2 * len(items)
    n_after = n_a + n_s + len(after)
    n_in = n_after + len(rkeys)

    def body(*refs):
        per = _carried_results(items, akeys, [], refs[n_in:], rkeys, refs[n_after:n_in])
        sem = refs[n_a:n_a + n_s]
        for ci, cm in enumerate(items):
            cm.finish(per[ci], sem[2 * ci], sem[2 * ci + 1])

    outs = pl.pallas_call(
        body, name=name, in_specs=[HBM_SPEC] * n_a + [SEM_SPEC] * n_s + [ANY] * (n_in - n_a - n_s),
        out_specs=[HBM_SPEC] * n_a, out_shape=[pltpu.HBM(s.shape, s.dtype) for s in shapes],
        input_output_aliases={i: i for i in range(n_a)},
        compiler_params=pltpu.CompilerParams(has_side_effects=SPLIT_EFFECT),
    )(*arrays, *[s for pair in sems for s in pair], *after, *reads)
    return _carried_results(items, akeys, [], outs)


def _mm(a, b, *, mode, out_dtype, name, res=None, carried=(), fused=None):
    if mode == "nn":
        (m, k), (k2, n) = a.shape, b.shape
    elif mode == "nt":
        (m, k), (n, k2) = a.shape, b.shape
    else:
        (k, m), (k2, n) = a.shape, b.shape
    assert k == k2, (name, a.shape, b.shape)
    tm = _div(m, MM_TILE_CAP, LANE)
    tn = _div(n, MM_TILE_CAP, MXU_WIDTH) if n % MXU_WIDTH == 0 else 0
    if tn < 1024:
        tn = _div(n, MM_TILE_CAP, LANE)
    out_shape = jax.ShapeDtypeStruct((m, n), out_dtype)

    if fused is not None:
        assert mode in ("nn", "nt") and res is None and k <= 2048
        tm, tn = fused["tile"]
        dot = _dot if mode == "nn" else _dot_nt
        n_x = len(fused["ins"])

        summed = [len(e) > 2 for e in fused["outs"]]

        def kern_fused(*refs):
            part = dot(refs[0][...], refs[1][...])
            outs = fused["post"](part, *[r[...] for r in refs[2:2 + n_x]])
            first_row_tile = pl.program_id(1) == 0
            for o_ref, val, acc in zip(refs[2 + n_x:], outs, summed):
                if not acc:
                    o_ref[...] = val.astype(o_ref.dtype)
                    continue

                @pl.when(first_row_tile)
                def _():
                    o_ref[...] = val

                @pl.when(jnp.logical_not(first_row_tile))
                def _():
                    o_ref[...] += val

        def tile_spec(shape, index=lambda i, j: (i, j)):
            return pl.BlockSpec(shape, lambda j, i: index(i, j))

        return _pcall(
            kern_fused, name=name, grid=(n // tn, m // tm),
            in_specs=[pl.BlockSpec((tm, k), lambda j, i: (i, 0)),
                      pl.BlockSpec((k, tn), lambda j, i: (0, j)) if mode == "nn"
                      else pl.BlockSpec((tn, k), lambda j, i: (j, 0))] + [tile_spec(*e[1:]) for e in fused["ins"]],
            out_specs=[tile_spec(*e[1:]) for e in fused["outs"]], out_shape=[e[0] for e in fused["outs"]],
            args=(a, b, *[e[0] for e in fused["ins"]]), semantics=("parallel", "arbitrary"), carried=carried)

    if mode == "tn":
        has_res_tn = res is not None

        def kern_tn(*refs):
            a_ref, b_ref, o_ref, at_ref = refs[0], refs[1], refs[-2], refs[-1]

            @pl.when(pl.program_id(1) == 0)
            def _():
                step = min(TRANSPOSE_CHUNK, k)
                for c0 in range(0, k, step):
                    at_ref[:, c0:c0 + step] = a_ref[c0:c0 + step, :].astype(F32).T.astype(BF16)

            part = jnp.dot(at_ref[...], b_ref[...].astype(BF16), preferred_element_type=F32)
            if has_res_tn:
                part = part + refs[2][...].astype(F32)
            o_ref[...] = part.astype(o_ref.dtype)

        o_spec = pl.BlockSpec((tm, tn), lambda i, j: (i, j))
        return _pcall(
            kern_tn, name=name, grid=(m // tm, n // tn),
            in_specs=[pl.BlockSpec((k, tm), lambda i, j: (0, i)),
                      pl.BlockSpec((k, tn), lambda i, j: (0, j))] + [o_spec] * has_res_tn,
            out_specs=o_spec, out_shape=out_shape, args=(a, b) + ((res,) if has_res_tn else ()),
            scratch_shapes=[pltpu.VMEM((tm, k), BF16)],
            semantics=("parallel", "arbitrary"), carried=carried)

    tk = k if k <= 2048 else _div(k, 3072, LANE)
    nk = k // tk
    a_spec = pl.BlockSpec((tm, tk), lambda i, j, kk: (i, kk))
    if mode == "nn":
        b_spec = pl.BlockSpec((tk, tn), lambda i, j, kk: (kk, j))
        dot = _dot
    else:
        b_spec = pl.BlockSpec((tn, tk), lambda i, j, kk: (j, kk))
        dot = _dot_nt
    o_spec = pl.BlockSpec((tm, tn), lambda i, j, kk: (i, j))
    in_specs = [a_spec, b_spec]
    args = [a, b]
    if res is not None:
        in_specs.append(o_spec)
        args.append(res)
    has_res = res is not None

    def kern(*refs):
        a_ref, b_ref = refs[0], refs[1]
        r_ref = refs[2] if has_res else None
        o_ref = refs[3] if has_res else refs[2]
        part = dot(a_ref[...], b_ref[...])
        if nk == 1:
            if has_res:
                part = part + r_ref[...]
            o_ref[...] = part.astype(o_ref.dtype)
            return
        acc_ref = refs[-1]
        kk = pl.program_id(2)

        @pl.when(kk == 0)
        def _():
            acc_ref[...] = part

        @pl.when(kk > 0)
        def _():
            acc_ref[...] += part

        @pl.when(kk == nk - 1)
        def _():
            tot = acc_ref[...]
            if has_res:
                tot = tot + r_ref[...]
            o_ref[...] = tot.astype(o_ref.dtype)

    return _pcall(
        kern, name=name, grid=(m // tm, n // tn, nk),
        in_specs=in_specs, out_specs=o_spec, out_shape=out_shape, args=args,
        scratch_shapes=[pltpu.VMEM((tm, tn), F32)] if nk > 1 else [],
        semantics=("parallel", "parallel", "arbitrary"), carried=carried)


ROW_BLOCK = 512


def _rms_fwd(x, g, name):
    t, d = x.shape
    tr = min(ROW_BLOCK, t)

    def kern(x_ref, g_ref, o_ref):
        xf = x_ref[...]
        r = lax.rsqrt(jnp.mean(xf * xf, axis=-1, keepdims=True) + RMS_EPS)
        o_ref[...] = (xf * r * g_ref[...]).astype(o_ref.dtype)

    return pl.pallas_call(
        kern, name=name, grid=(t // tr,),
        in_specs=[pl.BlockSpec((tr, d), lambda i: (i, 0)), pl.BlockSpec((1, d), lambda i: (0, 0))],
        out_specs=pl.BlockSpec((tr, d), lambda i: (i, 0)),
        out_shape=jax.ShapeDtypeStruct((t, d), BF16),
        compiler_params=_cparams(("parallel",)),
    )(x, g)


def _rms_fwd_halves(x, g, pos, name):
    t, d = x.shape
    tr = min(ROW_BLOCK, t)
    half = d // 2

    def kern(pos_ref, x_ref, g_ref, o_ref, sib_ref, own_ref):
        xf = x_ref[...]
        r = lax.rsqrt(jnp.mean(xf * xf, axis=-1, keepdims=True) + RMS_EPS)
        h = (xf * r * g_ref[...]).astype(o_ref.dtype)
        o_ref[...] = h
        is_south = pos_ref[0] == 0
        sib_ref[...] = jnp.where(is_south, h[:, half:], h[:, :half])
        own_ref[...] = jnp.where(is_south, h[:, :half], h[:, half:])

    row = pl.BlockSpec((tr, d), lambda i, pos_ref: (i, 0))
    part = pl.BlockSpec((tr, half), lambda i, pos_ref: (i, 0))
    return pl.pallas_call(
        kern, name=name,
        grid_spec=_grid_spec((t // tr,), [row, pl.BlockSpec((1, d), lambda i, pos_ref: (0, 0))], [row, part, part]),
        out_shape=[jax.ShapeDtypeStruct((t, d), BF16)] + [jax.ShapeDtypeStruct((t, half), BF16)] * 2,
        compiler_params=_cparams(("parallel",)),
    )(pos, x, g)


def _rms_bwd(x, g, dh, res, name, bf16_copy=False):
    t, d = x.shape
    tr = min(ROW_BLOCK, t)
    has_res = res is not None

    def kern(*refs):
        x_ref, g_ref, dh_ref = refs[:3]
        r_ref = refs[3] if has_res else None
        dx_ref, dg_ref = refs[3 + has_res], refs[4 + has_res]
        xf = x_ref[...]
        r = lax.rsqrt(jnp.mean(xf * xf, axis=-1, keepdims=True) + RMS_EPS)
        xn = xf * r
        dh_ = dh_ref[...]
        dhg = dh_ * g_ref[...]
        dx = r * (dhg - xn * jnp.mean(dhg * xn, axis=-1, keepdims=True))
        if has_res:
            dx = dx + r_ref[...]
        dx_ref[...] = dx
        if bf16_copy:
            refs[-1][...] = dx.astype(BF16)
        part = jnp.sum(dh_ * xn, axis=0, keepdims=True)

        @pl.when(pl.program_id(0) == 0)
        def _():
            dg_ref[...] = part

        @pl.when(pl.program_id(0) > 0)
        def _():
            dg_ref[...] += part

    row = pl.BlockSpec((tr, d), lambda i: (i, 0))
    vec = pl.BlockSpec((1, d), lambda i: (0, 0))
    in_specs = [row, vec, row] + ([row] if has_res else [])
    args = [x, g, dh] + ([res] if has_res else [])
    return pl.pallas_call(
        kern, name=name, grid=(t // tr,), in_specs=in_specs, out_specs=[row, vec] + [row] * bf16_copy,
        out_shape=[jax.ShapeDtypeStruct((t, d), F32), jax.ShapeDtypeStruct((1, d), F32)]
        + [jax.ShapeDtypeStruct((t, d), BF16)] * bf16_copy,
        compiler_params=_cparams(("arbitrary",)),
    )(*args)


def _loss_head(x3, g, target, name, core_row=None):
    t, d = x3.shape
    tr = ROW_BLOCK
    with_halves = core_row is not None

    def kern(x_ref, g_ref, t_ref, *rest):
        dx_ref, dg_ref, loss_ref, dxb_ref = rest[with_halves:with_halves + 4]
        xf = x_ref[...]
        r = lax.rsqrt(jnp.mean(xf * xf, axis=-1, keepdims=True) + RMS_EPS)
        xn = xf * r
        gg = g_ref[...]
        err = xn * gg - t_ref[...]
        lpart = 0.5 * jnp.sum(jnp.mean(err * err, axis=-1, keepdims=True), axis=0, keepdims=True)
        dy = err * (1.0 / d)
        dyg = dy * gg
        dx = r * (dyg - xn * jnp.mean(dyg * xn, axis=-1, keepdims=True))
        dx_ref[...] = dx
        dxb = dx.astype(BF16)
        dxb_ref[...] = dxb
        if with_halves:
            rest[-2][...], rest[-1][...] = _column_halves(dxb, rest[0][...])
        gpart = jnp.sum(dy * xn, axis=0, keepdims=True)
        lrow = jnp.broadcast_to(lpart, (1, LANE))

        @pl.when(pl.program_id(0) == 0)
        def _():
            dg_ref[...] = gpart
            loss_ref[...] = lrow

        @pl.when(pl.program_id(0) > 0)
        def _():
            dg_ref[...] += gpart
            loss_ref[...] += lrow

    row = pl.BlockSpec((tr, d), lambda i: (i, 0))
    vec = pl.BlockSpec((1, d), lambda i: (0, 0))
    part = pl.BlockSpec((tr, d // 2), lambda i: (i, 0))
    return pl.pallas_call(
        kern, name=name, grid=(t // tr,), in_specs=[row, vec, row] + [vec] * with_halves,
        out_specs=[row, vec, pl.BlockSpec((1, LANE), lambda i: (0, 0)), row] + [part] * (2 * with_halves),
        out_shape=[jax.ShapeDtypeStruct((t, d), F32), jax.ShapeDtypeStruct((1, d), F32),
                   jax.ShapeDtypeStruct((1, LANE), F32), jax.ShapeDtypeStruct((t, d), BF16)]
        + [jax.ShapeDtypeStruct((t, d // 2), BF16)] * (2 * with_halves),
        compiler_params=_cparams(("arbitrary",)),
    )(x3, g, target, *([core_row] if with_halves else []))


ATT_SCALE = HEAD_DIM ** -0.5
Q_BLOCK0, K_BLOCK0, V_BLOCK0 = 0, ATT_WIDTH // HEAD_DIM, 2 * ATT_WIDTH // HEAD_DIM


def _residue_rows(dil, r, n):
    if dil == 1:
        return pl.ds(n * ATT_BLOCK, ATT_BLOCK)
    return pl.ds(n * ATT_BLOCK * dil + r, ATT_BLOCK, stride=dil)


def _band_mask(with_prev):
    width = 2 * ATT_BLOCK if with_prev else ATT_BLOCK
    iq = lax.broadcasted_iota(jnp.int32, (ATT_BLOCK, width), 0)
    ik = lax.broadcasted_iota(jnp.int32, (ATT_BLOCK, width), 1)
    if not with_prev:
        return ik <= iq
    return ((ik < ATT_BLOCK) & (iq <= ik)) | ((ik >= ATT_BLOCK) & ((ik - ATT_BLOCK) <= iq))


def _band_keys(ref, dil, r, n):
    own = ref[_residue_rows(dil, r, n), :]
    if n == 0:
        return own
    return jnp.concatenate([ref[_residue_rows(dil, r, n - 1), :], own], axis=0)


def _attn_col_spec(base, grp):
    return pl.BlockSpec((SEQ, HEAD_DIM), lambda h: (0, base + grp * ATT_HEADS + h))


def _attn_fwd(proj, grp, name, carried=()):
    _, dil = ATT_GROUPS[grp]
    nb = SEQ // dil // ATT_BLOCK

    def kern(q_ref, k_ref, v_ref, o_ref, lse_ref):
        for r in range(dil):
            for n in range(nb):
                rows = _residue_rows(dil, r, n)
                s = _dot_nt(q_ref[rows, :], _band_keys(k_ref, dil, r, n)) * ATT_SCALE
                s = jnp.where(_band_mask(n > 0), s, -jnp.inf)
                m = jnp.max(s, axis=-1, keepdims=True)
                p = jnp.exp(s - m)
                l = jnp.sum(p, axis=-1, keepdims=True)
                o_ref[rows, :] = _dot(p / l, _band_keys(v_ref, dil, r, n))
                lse_ref[rows, :] = jnp.broadcast_to(m + jnp.log(l), (ATT_BLOCK, HEAD_DIM))

    out_spec = pl.BlockSpec((SEQ, HEAD_DIM), lambda h: (0, h))
    return _pcall(
        kern, name=name, grid=(ATT_HEADS,),
        in_specs=[_attn_col_spec(Q_BLOCK0, grp), _attn_col_spec(K_BLOCK0, grp), _attn_col_spec(V_BLOCK0, grp)],
        out_specs=[out_spec, out_spec],
        out_shape=[jax.ShapeDtypeStruct((SEQ, ATT_OUT), F32)] * 2, args=(proj, proj, proj),
        semantics=("parallel",), carried=carried)


def _attn_weights(l0, l1, l2):
    mx = jnp.maximum(jnp.maximum(l0, l1), l2)
    e0, e1, e2 = jnp.exp(l0 - mx), jnp.exp(l1 - mx), jnp.exp(l2 - mx)
    den = e0 + e1 + e2
    return e0 / den, e1 / den, e2 / den


def _attn_merge_fwd(outs, lses, name):
    tr = ROW_BLOCK

    def kern(o0, o1, o2, l0, l1, l2, out_ref):
        a0, a1, a2 = _attn_weights(l0[...], l1[...], l2[...])
        out_ref[...] = (a0 * o0[...] + a1 * o1[...] + a2 * o2[...]).astype(out_ref.dtype)

    spec = pl.BlockSpec((tr, ATT_OUT), lambda i: (i, 0))
    return pl.pallas_call(
        kern, name=name, grid=(SEQ // tr,), in_specs=[spec] * 6, out_specs=spec,
        out_shape=jax.ShapeDtypeStruct((SEQ, ATT_OUT), BF16),
        compiler_params=_cparams(("parallel",)),
    )(*outs, *lses)


def _attn_merge_bwd(outs, lses, do_att, name, carried=()):
    tr = ROW_BLOCK

    def kern(o0, o1, o2, l0, l1, l2, do_ref, d0, d1, d2, t0, t1, t2):
        alphas = _attn_weights(l0[...], l1[...], l2[...])
        do = do_ref[...]
        o_att = alphas[0] * o0[...] + alphas[1] * o1[...] + alphas[2] * o2[...]
        prod = do * o_att
        parts = []
        for h in range(ATT_HEADS):
            sl = slice(h * HEAD_DIM, (h + 1) * HEAD_DIM)
            tot = jnp.sum(prod[:, sl], axis=-1, keepdims=True)
            parts.append(jnp.broadcast_to(tot, (tr, HEAD_DIM)))
        dd = jnp.concatenate(parts, axis=1)
        for a, d_ref, t_ref in zip(alphas, (d0, d1, d2), (t0, t1, t2)):
            d_ref[...] = a * do
            t_ref[...] = -a * dd

    spec = pl.BlockSpec((tr, ATT_OUT), lambda i: (i, 0))
    res, cres = _pcall(
        kern, name=name, grid=(SEQ // tr,), in_specs=[spec] * 7, out_specs=[spec] * 6,
        out_shape=[jax.ShapeDtypeStruct((SEQ, ATT_OUT), F32)] * 6, args=(*outs, *lses, do_att),
        semantics=("parallel",), carried=carried)
    return (res[:3], res[3:]), cres


def _attn_bwd(proj, grp, lse, do_g, dl_g, name, carried=()):
    _, dil = ATT_GROUPS[grp]
    nb = SEQ // dil // ATT_BLOCK

    def kern(q_ref, k_ref, v_ref, do_ref, lse_ref, dl_ref, dq_ref, dk_ref, dv_ref, dq_acc, dk_acc, dv_acc):
        dk_acc[...] = jnp.zeros_like(dk_acc)
        dv_acc[...] = jnp.zeros_like(dv_acc)
        for r in range(dil):
            for n in range(nb):
                rows = _residue_rows(dil, r, n)
                q, do = q_ref[rows, :], do_ref[rows, :]
                kk, vv = _band_keys(k_ref, dil, r, n), _band_keys(v_ref, dil, r, n)
                s = _dot_nt(q, kk) * ATT_SCALE
                p = jnp.where(_band_mask(n > 0), jnp.exp(s - lse_ref[rows, :][:, :1]), 0.0)
                ds = p * (_dot_nt(do, vv) + dl_ref[rows, :][:, :1])
                dq_acc[rows, :] = _dot(ds, kk) * ATT_SCALE
                dk = _dot_tn(ds, q) * ATT_SCALE
                dv = _dot_tn(p, do)
                if n > 0:
                    prev = _residue_rows(dil, r, n - 1)
                    dk_acc[prev, :] += dk[:ATT_BLOCK]
                    dv_acc[prev, :] += dv[:ATT_BLOCK]
                    dk, dv = dk[ATT_BLOCK:], dv[ATT_BLOCK:]
                dk_acc[rows, :] += dk
                dv_acc[rows, :] += dv
        dq_ref[...] = dq_acc[...].astype(dq_ref.dtype)
        dk_ref[...] = dk_acc[...].astype(dk_ref.dtype)
        dv_ref[...] = dv_acc[...].astype(dv_ref.dtype)

    spec = pl.BlockSpec((SEQ, HEAD_DIM), lambda h: (0, h))
    return _pcall(
        kern, name=name, grid=(ATT_HEADS,),
        in_specs=[_attn_col_spec(Q_BLOCK0, grp), _attn_col_spec(K_BLOCK0, grp), _attn_col_spec(V_BLOCK0, grp),
                  spec, spec, spec],
        out_specs=[spec] * 3,
        out_shape=[jax.ShapeDtypeStruct((SEQ, ATT_OUT), BF16)] * 3, args=(proj, proj, proj, do_g, lse, dl_g),
        scratch_shapes=[pltpu.VMEM((SEQ, HEAD_DIM), F32)] * 3,
        semantics=("parallel",), carried=carried)


HG_HEADS_PER_STEP = 8
HG_BLOCK_W = 4 * HEAD_DIM
HG_BLOCKS = HG_HEADS_PER_STEP * HEAD_DIM // HG_BLOCK_W
HG_STEP_W = HG_HEADS_PER_STEP * HEAD_DIM
HG_Q_BLK = (3 * ATT_WIDTH) // HG_BLOCK_W
HG_N_CHUNKS = SEQ // HG_CHUNK
HG_MID = HG_CHUNK // 2


def _lower_bound(lb_ref, sl):
    l0, l1 = lb_ref[0:1, sl], lb_ref[1:2, sl]
    mx = jnp.maximum(l0, l1)
    e0, e1 = jnp.exp(l0 - mx), jnp.exp(l1 - mx)
    return e0 / (e0 + e1)


def _tri(lower):
    i = lax.broadcasted_iota(jnp.int32, (HG_CHUNK, HG_CHUNK), 0)
    j = lax.broadcasted_iota(jnp.int32, (HG_CHUNK, HG_CHUNK), 1)
    return (i >= j) if lower else (i <= j)


def _head_mean(x):
    parts = []
    for hd in range(x.shape[1] // HEAD_DIM):
        m = jnp.mean(x[:, hd * HEAD_DIM:(hd + 1) * HEAD_DIM], axis=-1, keepdims=True)
        parts.append(jnp.broadcast_to(m, (x.shape[0], HEAD_DIM)))
    return jnp.concatenate(parts, axis=1)


def _hg_chunk_terms(qh, fh, lb):
    sig = _sigmoid(fh)
    f = lb + (1.0 - lb) * sig
    k = 1.0 - f
    b = _dot_exact(_tri(True).astype(F32), jnp.log(f))
    bl = b[HG_CHUNK - 1:HG_CHUNK, :]
    br = b[HG_MID:HG_MID + 1, :]
    sq = _sigmoid(qh)
    q = qh * sq
    return dict(sig=sig, f=f, k=k, b=b, bl=bl, br=br, sq=sq, q=q,
                e1=jnp.exp(bl - b), e2=jnp.exp(b), e3=jnp.exp(b - br), e4=jnp.exp(br - b))


def _hg_fwd(proj, lbw, normw, name, carried=()):
    def in_blks(off):
        return [pl.BlockSpec((HG_CHUNK, HG_BLOCK_W), lambda hp, n, b=b: (n, HG_Q_BLK + off + hp * HG_BLOCKS + b))
                for b in range(HG_BLOCKS)]

    def kern(*refs):
        q_refs, f_refs, i_refs, g_refs = (refs[k * HG_BLOCKS:(k + 1) * HG_BLOCKS] for k in range(4))
        lb_ref, nw_ref, oraw_ref, ohg_ref, st_ref, state = refs[4 * HG_BLOCKS:]

        @pl.when(pl.program_id(1) == 0)
        def _():
            state[...] = jnp.zeros_like(state)

        causal = _tri(True)
        wide = lambda rs: jnp.concatenate([r[...] for r in rs], axis=1)
        t = _hg_chunk_terms(wide(q_refs), wide(f_refs), _lower_bound(lb_ref, slice(None)))
        v, gh = wide(i_refs), wide(g_refs)
        kd, qb, qr, kr = t["k"] * t["e1"], t["q"] * t["e2"], t["q"] * t["e3"], t["k"] * t["e4"]
        decay = jnp.exp(t["bl"])
        outs = []
        for hd in range(HG_HEADS_PER_STEP):
            sl = slice(hd * HEAD_DIM, (hd + 1) * HEAD_DIM)
            st = state[hd]
            st_ref[0, hd] = st
            a = jnp.where(causal, _dot_nt(qr[:, sl], kr[:, sl]), 0.0)
            outs.append(_dot_nt(qb[:, sl], st) + _dot(a, v[:, sl]))
            state[hd] = st * decay[:, sl] + _dot_tn(v[:, sl], kd[:, sl])
        o = jnp.concatenate(outs, axis=1)
        oraw_ref[...] = o
        r = lax.rsqrt(_head_mean(o * o) + RMS_EPS)
        nw = jnp.tile(nw_ref[...], (1, HG_HEADS_PER_STEP))
        ohg_ref[...] = (o * r * nw * (gh * _sigmoid(gh))).astype(ohg_ref.dtype)

    out_blk = pl.BlockSpec((HG_CHUNK, HG_STEP_W), lambda hp, n: (n, hp))
    return _pcall(
        kern, name=name, grid=(HG_HEADS // HG_HEADS_PER_STEP, HG_N_CHUNKS),
        in_specs=[*in_blks(0), *in_blks(2), *in_blks(4), *in_blks(6),
                  pl.BlockSpec((2, HG_STEP_W), lambda hp, n: (0, hp)),
                  pl.BlockSpec((1, HEAD_DIM), lambda hp, n: (0, 0))],
        out_specs=[out_blk, out_blk,
                   pl.BlockSpec((1, HG_HEADS_PER_STEP, HEAD_DIM, HEAD_DIM), lambda hp, n: (n, hp, 0, 0))],
        out_shape=[jax.ShapeDtypeStruct((SEQ, HG_WIDTH), F32), jax.ShapeDtypeStruct((SEQ, HG_WIDTH), BF16),
                   jax.ShapeDtypeStruct((HG_N_CHUNKS, HG_HEADS, HEAD_DIM, HEAD_DIM), F32)],
        args=(*[proj] * (4 * HG_BLOCKS), lbw, normw),
        scratch_shapes=[pltpu.VMEM((HG_HEADS_PER_STEP, HEAD_DIM, HEAD_DIM), F32)],
        semantics=("parallel", "arbitrary"), carried=carried)


def _hg_bwd(proj, lbw, normw, oraw, states, do_hg, name, carried=()):
    last = HG_N_CHUNKS - 1

    def in_blks(off):
        return [pl.BlockSpec((HG_CHUNK, HG_BLOCK_W),
                             lambda hp, n, b=b: (last - n, HG_Q_BLK + off + hp * HG_BLOCKS + b))
                for b in range(HG_BLOCKS)]

    blk = pl.BlockSpec((HG_CHUNK, HG_STEP_W), lambda hp, n: (last - n, hp))

    def kern(*refs):
        q_refs, f_refs, i_refs, g_refs = (refs[k * HG_BLOCKS:(k + 1) * HG_BLOCKS] for k in range(4))
        (lb_ref, nw_ref, oraw_ref, st_ref, do_ref, dq_ref, df_ref, di_ref, dg_ref, dlb_ref, dnw_ref,
         dstate) = refs[4 * HG_BLOCKS:]
        first = pl.program_id(1) == 0

        @pl.when(first)
        def _():
            dstate[...] = jnp.zeros_like(dstate)

        causal = _tri(True)
        wide = lambda rs: jnp.concatenate([r[...] for r in rs], axis=1)
        cat = lambda parts: jnp.concatenate(parts, axis=1)
        qh, fh, v, gh = wide(q_refs), wide(f_refs), wide(i_refs), wide(g_refs)
        o, dout = oraw_ref[...], do_ref[...]
        nw = jnp.tile(nw_ref[...], (1, HG_HEADS_PER_STEP))
        sgg = _sigmoid(gh)
        r = lax.rsqrt(_head_mean(o * o) + RMS_EPS)
        xn = o * r
        dg_ref[...] = (dout * xn * nw * (sgg * (1.0 + gh * (1.0 - sgg)))).astype(dg_ref.dtype)
        don = dout * (gh * sgg)
        dnw_wide = jnp.sum(don * xn, axis=0, keepdims=True)
        dnw_tot = dnw_wide[:, :HEAD_DIM]
        for hd in range(1, HG_HEADS_PER_STEP):
            dnw_tot = dnw_tot + dnw_wide[:, hd * HEAD_DIM:(hd + 1) * HEAD_DIM]
        tt = don * nw
        do = r * (tt - xn * _head_mean(tt * xn))
        lb = _lower_bound(lb_ref, slice(None))
        t = _hg_chunk_terms(qh, fh, lb)
        k, q = t["k"], t["q"]
        kd, qb, qr, kr = k * t["e1"], q * t["e2"], q * t["e3"], k * t["e4"]
        decay = jnp.exp(t["bl"])
        dqb, dqr, dkr, dkd, dv, ddecay = [], [], [], [], [], []
        for hd in range(HG_HEADS_PER_STEP):
            sl = slice(hd * HEAD_DIM, (hd + 1) * HEAD_DIM)
            st = st_ref[0, hd]
            dstn = dstate[hd]
            a = jnp.where(causal, _dot_nt(qr[:, sl], kr[:, sl]), 0.0)
            da = jnp.where(causal, _dot_nt(do[:, sl], v[:, sl]), 0.0)
            dqb.append(_dot(do[:, sl], st))
            dv.append(_dot_tn(a, do[:, sl]) + _dot_nt(kd[:, sl], dstn))
            dqr.append(_dot(da, kr[:, sl]))
            dkr.append(_dot_tn(da, qr[:, sl]))
            dkd.append(_dot(v[:, sl], dstn))
            ddecay.append(jnp.sum(dstn * st, axis=0, keepdims=True))
            dstate[hd] = dstn * decay[:, sl] + _dot_tn(do[:, sl], qb[:, sl])
        dqb, dqr, dkr, dkd, dv, ddecay = cat(dqb), cat(dqr), cat(dkr), cat(dkd), cat(dv), cat(ddecay)
        dq = dqb * t["e2"] + dqr * t["e3"]
        dk = dkd * t["e1"] + dkr * t["e4"]
        db = dqb * qb + dqr * qr - dkr * kr - dkd * kd
        dbl = jnp.sum(dkd * kd, axis=0, keepdims=True) + ddecay * decay
        dbr = jnp.sum(dkr * kr - dqr * qr, axis=0, keepdims=True)
        rows = lax.broadcasted_iota(jnp.int32, db.shape, 0)
        dlf = _dot_exact(_tri(False).astype(F32), db) + dbl + jnp.where(rows <= HG_MID, dbr, 0.0)
        df = dlf / t["f"] - dk
        sig, sq = t["sig"], t["sq"]
        df_ref[...] = (df * (1.0 - lb) * sig * (1.0 - sig)).astype(df_ref.dtype)
        dlb_row = jnp.sum(df * (1.0 - sig), axis=0, keepdims=True)
        dq_ref[...] = (dq * (sq * (1.0 + qh * (1.0 - sq)))).astype(dq_ref.dtype)
        di_ref[...] = dv.astype(di_ref.dtype)
        dnw_blk = jnp.broadcast_to(dnw_tot, (8, HEAD_DIM))

        @pl.when(first)
        def _():
            dlb_ref[...] = dlb_row
            dnw_ref[...] = dnw_blk

        @pl.when(jnp.logical_not(first))
        def _():
            dlb_ref[...] += dlb_row
            dnw_ref[...] += dnw_blk

    n_hp = HG_HEADS // HG_HEADS_PER_STEP
    outs, cres = _pcall(
        kern, name=name, grid=(n_hp, HG_N_CHUNKS),
        in_specs=[*in_blks(0), *in_blks(2), *in_blks(4), *in_blks(6),
                  pl.BlockSpec((2, HG_STEP_W), lambda hp, n: (0, hp)),
                  pl.BlockSpec((1, HEAD_DIM), lambda hp, n: (0, 0)),
                  blk,
                  pl.BlockSpec((1, HG_HEADS_PER_STEP, HEAD_DIM, HEAD_DIM), lambda hp, n: (last - n, hp, 0, 0)),
                  blk],
        out_specs=[blk, blk, blk, blk,
                   pl.BlockSpec((1, HG_STEP_W), lambda hp, n: (0, hp)),
                   pl.BlockSpec((8, HEAD_DIM), lambda hp, n: (hp, 0))],
        out_shape=[jax.ShapeDtypeStruct((SEQ, HG_WIDTH), BF16)] * 4
        + [jax.ShapeDtypeStruct((1, HG_WIDTH), F32), jax.ShapeDtypeStruct((8 * n_hp, HEAD_DIM), F32)],
        args=(*[proj] * (4 * HG_BLOCKS), lbw, normw, oraw, states, do_hg),
        scratch_shapes=[pltpu.VMEM((HG_HEADS_PER_STEP, HEAD_DIM, HEAD_DIM), F32)],
        semantics=("parallel", "arbitrary"), carried=carried)
    dqh, dfh, dih, dgh, dlb, dnw = outs
    return (dqh, dfh, dih, dgh, dlb, [dnw[8 * i:8 * i + 1] for i in range(n_hp)]), cres


GATE_BLOCK_W = 512
GATE_A_BLK = (3 * ATT_WIDTH + 4 * HG_WIDTH) // GATE_BLOCK_W
GATE_B_BLK = GATE_A_BLK + D_MODEL // GATE_BLOCK_W


GATE_TILE = (1024, GATE_BLOCK_W)


def _gate_ins(proj, ya, yb=None):
    ins = [(proj, GATE_TILE, lambda i, j: (i, GATE_A_BLK + j)), (proj, GATE_TILE, lambda i, j: (i, GATE_B_BLK + j)),
           (ya, GATE_TILE)]
    return ins + ([(yb, GATE_TILE)] if yb is not None else [])


def _branch_b_gate(o_hg, w_b, proj, ya, name, carried=()):
    def post(yb, ga, gb, ya_):
        return yb, _sigmoid(ga) * ya_ + _sigmoid(gb) * yb

    return _mm(o_hg, w_b, mode="nn", out_dtype=F32, name=name, carried=carried, fused=dict(
        tile=GATE_TILE, ins=_gate_ins(proj, ya), post=post,
        outs=[(jax.ShapeDtypeStruct((SEQ, D_MODEL), F32), GATE_TILE), (jax.ShapeDtypeStruct((SEQ, D_MODEL), BF16), GATE_TILE)]))


def _dmerged_gate_bwd(dx1, w_out, proj, ya, yb, name, carried=()):
    def post(dm, ga, gb, ya_, yb_):
        sa, sb = _sigmoid(ga), _sigmoid(gb)
        return dm * sa, dm * sb, dm * ya_ * sa * (1.0 - sa), dm * yb_ * sb * (1.0 - sb)

    return _mm(dx1, w_out, mode="nt", out_dtype=BF16, name=name, carried=carried, fused=dict(
        tile=GATE_TILE, ins=_gate_ins(proj, ya, yb), post=post,
        outs=[(jax.ShapeDtypeStruct((SEQ, D_MODEL), BF16), GATE_TILE)] * 4))


NORM_TILE = (512, D_MODEL)


def _column_halves(v, core_row):
    half = v.shape[1] // 2
    south = core_row[:, :half] < 0.5
    lo, hi = v[:, :half], v[:, half:]
    return jnp.where(south, hi, lo), jnp.where(south, lo, hi)


def _residual_norm(a, w, x, g, name, carried=(), core_row=None):
    vec = ((1, D_MODEL), lambda i, j: (0, j))
    half_tile = (NORM_TILE[0], D_MODEL // 2)

    def post(part, x_, g_, *core):
        xn = part + x_
        r = lax.rsqrt(jnp.mean(xn * xn, axis=-1, keepdims=True) + RMS_EPS)
        hn = (xn * r * g_).astype(BF16)
        return (xn, hn) + (_column_halves(hn, core[0]) if core else ())

    wide = [(jax.ShapeDtypeStruct((SEQ, D_MODEL), F32), NORM_TILE), (jax.ShapeDtypeStruct((SEQ, D_MODEL), BF16), NORM_TILE)]
    halves = [(jax.ShapeDtypeStruct((SEQ, D_MODEL // 2), BF16), half_tile)] * 2
    return _mm(a, w, mode="nn", out_dtype=F32, name=name, carried=carried, fused=dict(
        tile=NORM_TILE, post=post, ins=[(x, NORM_TILE), (g, *vec)] + ([(core_row, *vec)] if core_row is not None else []),
        outs=wide + (halves if core_row is not None else [])))


def _norm_bwd_residual(dq, w, x, g, res, name, carried=()):
    def post(dh, x_, g_, res_):
        r = lax.rsqrt(jnp.mean(x_ * x_, axis=-1, keepdims=True) + RMS_EPS)
        xn = x_ * r
        dhg = dh * g_
        dx = r * (dhg - xn * jnp.mean(dhg * xn, axis=-1, keepdims=True)) + res_
        return dx, jnp.sum(dh * xn, axis=0, keepdims=True), dx

    vec = ((1, D_MODEL), lambda i, j: (0, j))
    return _mm(dq, w, mode="nt", out_dtype=F32, name=name, carried=carried, fused=dict(
        tile=NORM_TILE, ins=[(x, NORM_TILE), (g, *vec), (res, NORM_TILE)], post=post,
        outs=[(jax.ShapeDtypeStruct((SEQ, D_MODEL), F32), NORM_TILE), (jax.ShapeDtypeStruct((1, D_MODEL), F32), *vec),
              (jax.ShapeDtypeStruct((SEQ, D_MODEL), BF16), NORM_TILE)]))


FF_SHARD = D_FF // N_CHIPS


FF_TILE_ROWS = 512


def _swiglu_tile(ab):
    a, b = ab[:, :FF_SHARD], ab[:, FF_SHARD:]
    return a * _sigmoid(a) * b


def _swiglu_grad_tile(du, ab):
    a, b = ab[:, :FF_SHARD], ab[:, FF_SHARD:]
    sg = _sigmoid(a)
    return jnp.concatenate([du * b * (sg * (1.0 + a * (1.0 - sg))), du * (a * sg)], axis=1)


def _ff_up(hf, w13, name, carried=()):
    wide, narrow = (FF_TILE_ROWS, 2 * FF_SHARD), (FF_TILE_ROWS, FF_SHARD)
    return _mm(hf, w13, mode="nn", out_dtype=F32, name=name, carried=carried, fused=dict(
        tile=wide, ins=[],
        outs=[(jax.ShapeDtypeStruct((SEQ, 2 * D_FF), F32), wide), (jax.ShapeDtypeStruct((SEQ, D_FF), BF16), narrow)],
        post=lambda p: (p, _swiglu_tile(p))))


def _ff_down_bwd(dx3, w2, ab, name, carried=()):
    wide, narrow = (FF_TILE_ROWS, 2 * FF_SHARD), (FF_TILE_ROWS, FF_SHARD)
    out, res = _mm(dx3, w2, mode="nt", out_dtype=BF16, name=name, carried=carried, fused=dict(
        tile=narrow, ins=[(ab, wide)], outs=[(jax.ShapeDtypeStruct((SEQ, 2 * D_FF), BF16), wide)],
        post=lambda du, ab_: (_swiglu_grad_tile(du, ab_),)))
    return out[0], res


CROSS_ROWS = 512


def _cross_fwd(qc, kvc, name, carried=()):
    def kern(q_ref, k_ref, v_ref, o_ref):
        s = _dot_nt(q_ref[...], k_ref[...]) * ATT_SCALE
        m = jnp.max(s, axis=-1, keepdims=True)
        e = jnp.exp(s - m)
        p = e / jnp.sum(e, axis=-1, keepdims=True)
        o_ref[...] = _dot(p, v_ref[...]).astype(o_ref.dtype)

    qblk = pl.BlockSpec((CROSS_ROWS, HEAD_DIM), lambda h, i: (i, h))
    return _pcall(
        kern, name=name, grid=(CROSS_HEADS, SEQ // CROSS_ROWS),
        in_specs=[qblk, pl.BlockSpec((MEM_LEN, HEAD_DIM), lambda h, i: (0, h)),
                  pl.BlockSpec((MEM_LEN, HEAD_DIM), lambda h, i: (0, CROSS_HEADS + h))],
        out_specs=qblk, out_shape=jax.ShapeDtypeStruct((SEQ, CROSS_WIDTH), BF16), args=(qc, kvc, kvc),
        semantics=("parallel", "parallel"), carried=carried)


def _cross_bwd(qc, kvc, doc, name):
    def kern(q_ref, k_ref, v_ref, do_ref, dq_ref, dk_ref, dv_ref):
        q, k, v, do = q_ref[...], k_ref[...], v_ref[...], do_ref[...]
        s = _dot_nt(q, k) * ATT_SCALE
        m = jnp.max(s, axis=-1, keepdims=True)
        e = jnp.exp(s - m)
        p = e / jnp.sum(e, axis=-1, keepdims=True)
        dp = _dot_nt(do, v)
        ds = p * (dp - jnp.sum(dp * p, axis=-1, keepdims=True))
        dq_ref[...] = (_dot(ds, k) * ATT_SCALE).astype(dq_ref.dtype)
        dk = _dot_tn(ds, q) * ATT_SCALE
        dv = _dot_tn(p, do)

        @pl.when(pl.program_id(1) == 0)
        def _():
            dk_ref[...] = dk
            dv_ref[...] = dv

        @pl.when(pl.program_id(1) > 0)
        def _():
            dk_ref[...] += dk
            dv_ref[...] += dv

    qblk = pl.BlockSpec((CROSS_ROWS, HEAD_DIM), lambda h, i: (i, h))
    kblk = pl.BlockSpec((MEM_LEN, HEAD_DIM), lambda h, i: (0, h))
    dq, dk, dv = pl.pallas_call(
        kern, name=name, grid=(CROSS_HEADS, SEQ // CROSS_ROWS),
        in_specs=[qblk, kblk, pl.BlockSpec((MEM_LEN, HEAD_DIM), lambda h, i: (0, CROSS_HEADS + h)), qblk],
        out_specs=[qblk, kblk, kblk],
        out_shape=[jax.ShapeDtypeStruct((SEQ, CROSS_WIDTH), BF16),
                   jax.ShapeDtypeStruct((MEM_LEN, CROSS_WIDTH), F32),
                   jax.ShapeDtypeStruct((MEM_LEN, CROSS_WIDTH), F32)],
        compiler_params=_cparams(("parallel", "arbitrary")),
    )(qc, kvc, kvc, doc)
    return dq, jnp.concatenate([dk, dv], axis=1)


FULL_SPECS = {
    "w_in": ("col", D_MODEL, IN_WIDTH),
    "w_branch_a": ("col", ATT_OUT, D_MODEL),
    "w_branch_b": ("col", HG_WIDTH, D_MODEL),
    "w_out": ("row", D_MODEL, D_MODEL),
    "wq_cross": ("row", D_MODEL, CROSS_WIDTH),
    "wkv_cross": ("row", D_MODEL, 2 * CROSS_WIDTH),
    "wo_cross": ("col", CROSS_WIDTH, D_MODEL),
    "w13": ("col", D_MODEL, 2 * D_FF),
    "w2": ("row", D_FF, D_MODEL),
}
WEIGHT_PLACE = {
    "w_in": ("w_in", 0), "w_branch_a": ("w_branch_a", 0), "w_branch_b": ("w_branch_b", 0),
    "w_out": ("w_out", 0), "wq_cross": ("wq_cross", 0), "wkv_cross": ("wkv_cross", 0),
    "wo_cross": ("wo_cross", 0), "w1": ("w13", 0), "w3": ("w13", FF_SHARD), "w2": ("w2", 0),
}
BIG_WEIGHTS = tuple(WEIGHT_PLACE)
EW_BLOCK_ELEMS = 512 * 1024


def _position():
    return lax.axis_index("x"), lax.axis_index("y"), lax.axis_index("c")


def _other_chips(x, y):
    return [(1 - x, y), (x, 1 - y), (1 - x, 1 - y)]


def _half(ref, kind, h):
    r, c = ref.shape
    if kind == "col":
        return ref.at[pl.ds(h * (r // 2), r // 2), :]
    return ref.at[:, pl.ds(h * (c // 2), c // 2)]


def _shard_of(ref, kind, start, size):
    return ref.at[:, pl.ds(start, size)] if kind == "col" else ref.at[pl.ds(start, size), :]


def _rows_of(ref, r0, nrows):
    return ref if nrows is None else ref.at[pl.ds(r0, nrows), :]


def _half_shape(kind, rows, cols):
    return (rows // 2, cols) if kind == "col" else (rows, cols // 2)


def _slot_shape(spec):
    kind, rows, cols = spec
    hr, hc = _half_shape(kind, rows, cols)
    return (hr, hc // N_CHIPS) if kind == "col" else (hr // N_CHIPS, hc)


def _remote(src, dst, send_sem, recv_sem, device):
    return pltpu.make_async_remote_copy(src_ref=src, dst_ref=dst, send_sem=send_sem, recv_sem=recv_sem,
                                        device_id=device, device_id_type=MESH)


def _gather_ici_comm(fulls, jobs, specs):
    def piece(refs, job, chip, c):
        f, r0, nr = job
        kind, rows, cols = specs[f]
        stride = (cols if kind == "col" else rows) // N_CHIPS
        return _rows_of(_half(_shard_of(refs[f], kind, chip * stride, stride), kind, c), r0, nr)

    def start(refs, ss, rs):
        x, y, c = _position()
        j = 2 * x + y
        for q, job in enumerate(jobs):
            for p, (px, py) in enumerate(_other_chips(x, y)):
                _remote(piece(refs, job, j, c), piece(refs, job, j, c), ss.at[3 * q + p], rs.at[3 * q + p],
                        (px, py, c)).start()

    def finish(refs, ss, rs):
        x, y, c = _position()
        j = 2 * x + y
        for q, job in enumerate(jobs):
            for p, (px, py) in enumerate(_other_chips(x, y)):
                _remote(piece(refs, job, j, c), piece(refs, job, 2 * px + py, c), ss.at[3 * q + p],
                        rs.at[3 * q + p], (px, py, c)).wait_recv()
        for q, job in enumerate(jobs):
            for p, (px, py) in enumerate(_other_chips(x, y)):
                _remote(piece(refs, job, j, c), piece(refs, job, j, c), ss.at[3 * q + p], rs.at[3 * q + p],
                        (px, py, c)).wait_send()

    names = list(dict.fromkeys(job[0] for job in jobs))
    return _Carried({f: fulls[f] for f in names}, {}, 3 * len(jobs), start, finish)


def _gather_ring_comm(fulls, f, r0, nr, phase, specs):
    kind, _, cols = specs[f]
    assert kind == "col" and nr % 32 == 0
    stride = cols // N_CHIPS
    half = nr // 2

    def rows(refs, chip, c, lo, n):
        return _rows_of(_half(_shard_of(refs[f], kind, chip * stride, stride), kind, c), r0 + lo, n)

    def copies(refs, ss, rs):
        x, y, c = _position()
        me, nx, ny, dg = 2 * x + y, 2 * (1 - x) + y, 2 * x + (1 - y), 2 * (1 - x) + (1 - y)
        to_x, to_y = (1 - x, y, c), (x, 1 - y, c)
        if phase == "a":
            mine = rows(refs, me, c, 0, nr)
            return [(_remote(mine, mine, ss.at[0], rs.at[0], to_x), rows(refs, nx, c, 0, nr)),
                    (_remote(mine, mine, ss.at[1], rs.at[1], to_y), rows(refs, ny, c, 0, nr))]
        up, low = rows(refs, ny, c, half, half), rows(refs, nx, c, 0, half)
        return [(_remote(up, up, ss.at[0], rs.at[0], to_x), rows(refs, dg, c, half, half)),
                (_remote(low, low, ss.at[1], rs.at[1], to_y), rows(refs, dg, c, 0, half))]

    def start(refs, ss, rs):
        for cp, _ in copies(refs, ss, rs):
            cp.start()

    def finish(refs, ss, rs):
        x, y, c = _position()
        mine = copies(refs, ss, rs)
        for i, (_, landing) in enumerate(mine):
            _remote(landing, landing, ss.at[i], rs.at[i], (x, y, c)).wait_recv()
        for cp, _ in mine:
            cp.wait_send()

    return _Carried({f: fulls[f]}, {}, 2, start, finish)


def _gather_d2d_comm(fulls, jobs, specs):
    def rect(refs, job, h):
        f, r0, nr = job
        assert nr is None or specs[f][0] == "col"
        return _rows_of(_half(refs[f], specs[f][0], h), r0, nr)

    def start(refs, ss, rs):
        x, y, c = _position()
        for q, job in enumerate(jobs):
            _remote(rect(refs, job, c), rect(refs, job, c), ss.at[q], rs.at[q], (x, y, 1 - c)).start()

    def finish(refs, ss, rs):
        x, y, c = _position()
        for q, job in enumerate(jobs):
            _remote(rect(refs, job, 1 - c), rect(refs, job, 1 - c), ss.at[q], rs.at[q], (x, y, 1 - c)).wait_recv()
        for q, job in enumerate(jobs):
            _remote(rect(refs, job, c), rect(refs, job, c), ss.at[q], rs.at[q], (x, y, 1 - c)).wait_send()

    names = list(dict.fromkeys(job[0] for job in jobs))
    return _Carried({f: fulls[f] for f in names}, {}, len(jobs), start, finish)


def _pairx_comm(grads, names, specs, whole=False, recv=None):
    def copies(refs, ss, rs):
        x, y, c = _position()
        src = (lambda f: refs[("g", f)]) if whole else (lambda f: _half(refs[("g", f)], specs[f][0], 1 - c))
        return [_remote(src(f), refs[("r", f)], ss.at[i], rs.at[i], (x, y, 1 - c)) for i, f in enumerate(names)]

    def start(refs, ss, rs):
        for cp in copies(refs, ss, rs):
            cp.start()

    def finish(refs, ss, rs):
        for cp in copies(refs, ss, rs):
            cp.wait_recv()
        for cp in copies(refs, ss, rs):
            cp.wait_send()

    reads = {("g", f): grads[f] for f in names}
    if recv is not None:
        return _Carried({("r", f): recv[f] for f in names}, {}, len(names), start, finish, reads=reads)
    fresh = {("r", f): jax.ShapeDtypeStruct(_half_shape(*specs[f]), BF16) for f in names}
    return _Carried({}, fresh, len(names), start, finish, reads=reads)


def _chipx_comm(pair_sums, slots, jobs, specs):
    def copies(refs, ss, rs):
        x, y, c = _position()
        out = []
        for q, (f, r0, nr) in enumerate(jobs):
            kind = specs[f][0]
            width = _slot_shape(specs[f])[1 if kind == "col" else 0]
            for p, (px, py) in enumerate(_other_chips(x, y)):
                src = _rows_of(_shard_of(refs[("p", f)], kind, (2 * px + py) * width, width), r0, nr)
                dst = _rows_of(refs[("s", f)].at[p], r0, nr)
                out.append(_remote(src, dst, ss.at[3 * q + p], rs.at[3 * q + p], (px, py, c)))
        return out

    def start(refs, ss, rs):
        for cp in copies(refs, ss, rs):
            cp.start()

    def finish(refs, ss, rs):
        for cp in copies(refs, ss, rs):
            cp.wait_recv()
        for cp in copies(refs, ss, rs):
            cp.wait_send()

    names = list(dict.fromkeys(job[0] for job in jobs))
    arrays = {("p", f): pair_sums[f] for f in names}
    arrays.update({("s", f): slots[f] for f in names})
    return _Carried(arrays, {}, 3 * len(jobs), start, finish)


def _share_comm(grads, wnames, specs, place):
    def start(refs, ss, rs):
        x, y, c = _position()
        for i, w in enumerate(wnames):
            kind = specs[place[w][0]][0]
            _remote(_half(refs[w], kind, c), _half(refs[w], kind, c), ss.at[i], rs.at[i], (x, y, 1 - c)).start()

    def finish(refs, ss, rs):
        x, y, c = _position()
        for i, w in enumerate(wnames):
            kind = specs[place[w][0]][0]
            _remote(_half(refs[w], kind, 1 - c), _half(refs[w], kind, 1 - c), ss.at[i], rs.at[i],
                    (x, y, 1 - c)).wait_recv()
        for i, w in enumerate(wnames):
            kind = specs[place[w][0]][0]
            _remote(_half(refs[w], kind, c), _half(refs[w], kind, c), ss.at[i], rs.at[i], (x, y, 1 - c)).wait_send()

    return _Carried({w: grads[w] for w in wnames}, {}, len(wnames), start, finish)


def _gather_rows(v, name="gather_small"):
    shape = v.shape

    def body(v_ref, out_ref, send_sem, recv_sem, loc_sem):
        x, y, c = _position()
        me = 4 * x + 2 * y + c
        flips = [(fx, fy, fc) for fx in (0, 1) for fy in (0, 1) for fc in (0, 1)][1:]

        def peer(fl):
            return tuple(1 - a if f else a for a, f in zip((x, y, c), fl))

        loc = pltpu.make_async_copy(v_ref, out_ref.at[me], loc_sem)
        loc.start()
        sends = []
        for i, fl in enumerate(flips):
            cp = _remote(v_ref, out_ref.at[me], send_sem.at[i], recv_sem.at[i], peer(fl))
            cp.start()
            sends.append(cp)
        for i, fl in enumerate(flips):
            px, py, pc = peer(fl)
            _remote(v_ref, out_ref.at[4 * px + 2 * py + pc], send_sem.at[i], recv_sem.at[i], peer(fl)).wait_recv()
        for cp in sends:
            cp.wait_send()
        loc.wait()

    return pl.pallas_call(
        body, name=name, in_specs=[ANY], out_specs=ANY,
        out_shape=jax.ShapeDtypeStruct((N_DEV,) + shape, F32),
        scratch_shapes=[pltpu.SemaphoreType.DMA((N_DEV - 1,)), pltpu.SemaphoreType.DMA((N_DEV - 1,)),
                        pltpu.SemaphoreType.DMA],
    )(v)


def _ew_block(rows, cols, elems=EW_BLOCK_ELEMS):
    tc = cols if cols <= 4096 else _div(cols, 2048, LANE)
    tr = _div(rows, max(16, elems // tc), 16)
    return tr, tc


def _mesh_scalars():
    x, y, c = _position()
    return jnp.stack([c, 2 * x + y]).astype(jnp.int32)


def _grid_spec(grid, in_specs, out_specs):
    return pltpu.PrefetchScalarGridSpec(num_scalar_prefetch=1, grid=grid, in_specs=in_specs, out_specs=out_specs)


def _cast_into_full(parts, fname, pos, specs, place, name, token=None):
    kind, rows, cols = specs[fname]
    ws = [w for w in place if place[w][0] == fname]
    if kind == "col":
        stride = cols // N_CHIPS
        hr = rows // 2
        tr = _div(hr, max(16, EW_BLOCK_ELEMS // stride), 16)
        nrb = hr // tr
        in_specs = [pl.BlockSpec((tr, parts[w].shape[1]), lambda i, pos_ref: (i + pos_ref[0] * nrb, 0)) for w in ws]
        out_spec = pl.BlockSpec((tr, stride), lambda i, pos_ref: (i + pos_ref[0] * nrb, pos_ref[1]))
    else:
        stride = rows // N_CHIPS
        hc = cols // 2
        tr = _div(stride, max(16, EW_BLOCK_ELEMS // hc), 16)
        nrb = stride // tr
        in_specs = [pl.BlockSpec((tr, hc), lambda i, pos_ref: (i, pos_ref[0])) for w in ws]
        out_spec = pl.BlockSpec((tr, hc), lambda i, pos_ref: (i + pos_ref[1] * nrb, pos_ref[0]))

    def kern(pos_ref, *refs):
        o_ref = refs[-1]
        for w, r in zip(ws, refs[:len(ws)]):
            off = place[w][1] if kind == "col" else 0
            o_ref[:, off:off + r.shape[1]] = r[...].astype(o_ref.dtype)

    tokens = [] if token is None else [token]
    in_specs = in_specs + [pl.BlockSpec(TOKEN_SHAPE, lambda i, pos_ref: (0, 0))] * len(tokens)
    return pl.pallas_call(
        kern, name=name, grid_spec=_grid_spec((nrb,), in_specs, out_spec),
        out_shape=jax.ShapeDtypeStruct((rows, cols), BF16),
        compiler_params=_cparams(("parallel",)),
    )(pos, *[parts[w] for w in ws], *tokens)


def _pair_sum(grad, recv, pos, spec, name, whole=False):
    kind, rows, cols = spec
    hr, hc = _half_shape(kind, rows, cols)
    tr, tc = _ew_block(hr, hc, 2 * EW_BLOCK_ELEMS)
    nrb, ncb = hr // tr, hc // tc
    blk = pl.BlockSpec((tr, tc), lambda i, jj, pos_ref: (i, jj))
    if whole:
        mine = blk
    elif kind == "col":
        mine = pl.BlockSpec((tr, tc), lambda i, jj, pos_ref: (i + pos_ref[0] * nrb, jj))
    else:
        mine = pl.BlockSpec((tr, tc), lambda i, jj, pos_ref: (i, jj + pos_ref[0] * ncb))

    def kern(pos_ref, g_ref, r_ref, o_ref, slots_ref):
        o_ref[...] = (g_ref[...].astype(F32) + r_ref[...].astype(F32)).astype(o_ref.dtype)

    return pl.pallas_call(
        kern, name=name, grid_spec=_grid_spec((nrb, ncb), [mine, blk], [blk, ANY]),
        out_shape=[jax.ShapeDtypeStruct((hr, hc), BF16),
                   jax.ShapeDtypeStruct((N_CHIPS - 1,) + _slot_shape(spec), BF16)],
        compiler_params=_cparams(("parallel", "parallel")),
    )(pos, grad, recv)


def _chip_sum(pair_sum, slots, pos, fname, shard_shapes, specs, place, name):
    kind, rows, cols = specs[fname]
    sr, sc = _slot_shape(specs[fname])
    ws = [w for w in place if place[w][0] == fname]
    n_slots = N_CHIPS - 1
    tr = _div(sr, max(16, EW_BLOCK_ELEMS // sc), 16)
    nrb = sr // tr
    slot = pl.BlockSpec((n_slots, tr, sc), lambda i, pos_ref: (0, i, 0))
    if kind == "col":
        own = pl.BlockSpec((tr, sc), lambda i, pos_ref: (i, pos_ref[1]))
        out_specs = [pl.BlockSpec((tr, shard_shapes[w][1]), lambda i, pos_ref: (i + pos_ref[0] * nrb, 0)) for w in ws]
    else:
        own = pl.BlockSpec((tr, sc), lambda i, pos_ref: (i + pos_ref[1] * nrb, 0))
        out_specs = [pl.BlockSpec((tr, sc), lambda i, pos_ref: (i, pos_ref[0])) for w in ws]

    def kern(pos_ref, own_ref, slot_ref, *out_refs):
        tot = own_ref[...].astype(F32)
        for s in range(n_slots):
            tot = tot + slot_ref[s].astype(F32)
        for w, o_ref in zip(ws, out_refs):
            off = place[w][1] if kind == "col" else 0
            o_ref[...] = tot[:, off:off + o_ref.shape[1]]

    outs = pl.pallas_call(
        kern, name=name, grid_spec=_grid_spec((nrb,), [own, slot], out_specs),
        out_shape=[jax.ShapeDtypeStruct(shard_shapes[w], F32) for w in ws],
        compiler_params=_cparams(("parallel",)),
    )(pos, pair_sum, slots)
    return dict(zip(ws, outs))


def _adam_math(w, g, m, v):
    m2 = ADAM_B1 * m + (1.0 - ADAM_B1) * g
    v2 = ADAM_B2 * v + (1.0 - ADAM_B2) * (g * g)
    m_hat = m2 / (1.0 - ADAM_B1 ** ADAM_STEP)
    v_hat = v2 / (1.0 - ADAM_B2 ** ADAM_STEP)
    delta = -ADAM_LR * (m_hat / (jnp.sqrt(v_hat) + ADAM_EPS) + ADAM_WD * w)
    return delta, m2, v2


def _adamw(w, g, m, v, name, carried=()):
    rows, cols = w.shape
    tr, tc = _ew_block(rows, cols)

    def kern(w_ref, g_ref, m_ref, v_ref, d_ref, m2_ref, v2_ref, g_out_ref):
        g_ = g_ref[...]
        d_ref[...], m2_ref[...], v2_ref[...] = _adam_math(w_ref[...], g_, m_ref[...], v_ref[...])
        g_out_ref[...] = g_

    blk = pl.BlockSpec((tr, tc), lambda i, j: (i, j))
    return _pcall(
        kern, name=name, grid=(rows // tr, cols // tc), in_specs=[blk] * 4, out_specs=[blk] * 4,
        out_shape=[jax.ShapeDtypeStruct((rows, cols), F32)] * 4, args=(w, g, m, v),
        semantics=("parallel", "parallel"), carried=carried)


SMALL_ROWS = ("ln_mix_w", "ln_cross_w", "ln_mem_w", "ln_ffn_w", "ln_final_w")
ROW_HG_NORM, ROW_LB0, ROW_LB1 = 5, 6, 7
LOSS_LANE0 = HEAD_DIM


def _pack_small(vals):
    rows = [vals[n].reshape(1, D_MODEL) for n in SMALL_ROWS]
    pad = lambda a: jnp.pad(a, ((0, 0), (0, D_MODEL - a.shape[1])))
    rows.append(pad(vals["hg_norm_w"].reshape(1, HEAD_DIM)))
    rows.append(pad(vals["hg_lower_bounds"].reshape(2, HG_WIDTH)))
    return jnp.concatenate(rows, axis=0)


def _small_update(gathered, w, m, v, name="small_update"):
    def kern(g_ref, w_ref, m_ref, v_ref, grad_ref, d_ref, m2_ref, v2_ref, loss_ref):
        tot = g_ref[0]
        for s in range(1, N_DEV):
            tot = tot + g_ref[s]
        wv = w_ref[...]
        row = lax.broadcasted_iota(jnp.int32, (8, D_MODEL), 0)
        lane = lax.broadcasted_iota(jnp.int32, (8, D_MODEL), 1)
        l0, l1 = wv[ROW_LB0:ROW_LB0 + 1], wv[ROW_LB1:ROW_LB1 + 1]
        mx = jnp.maximum(l0, l1)
        e0, e1 = jnp.exp(l0 - mx), jnp.exp(l1 - mx)
        p0 = e0 / (e0 + e1)
        dlog = tot[ROW_LB0:ROW_LB0 + 1] * p0 * (1.0 - p0)
        tot = jnp.where(row == ROW_HG_NORM, tot + tot[ROW_LB1:ROW_LB1 + 1], tot)
        grad = jnp.where(row == ROW_LB0, dlog, jnp.where(row == ROW_LB1, -dlog, tot))
        grad = jnp.where((row == ROW_HG_NORM) & (lane >= HEAD_DIM), 0.0, grad)
        grad = jnp.where((row >= ROW_LB0) & (lane >= HG_WIDTH), 0.0, grad)
        grad_ref[...] = grad
        d_ref[...], m2_ref[...], v2_ref[...] = _adam_math(wv, grad, m_ref[...], v_ref[...])
        loss_ref[...] = tot[ROW_HG_NORM:ROW_HG_NORM + 1, LOSS_LANE0:LOSS_LANE0 + LANE]

    full = pl.BlockSpec((8, D_MODEL), lambda: (0, 0))
    return pl.pallas_call(
        kern, name=name,
        in_specs=[pl.BlockSpec((N_DEV, 8, D_MODEL), lambda: (0, 0, 0)), full, full, full],
        out_specs=[full, full, full, full, pl.BlockSpec((1, LANE), lambda: (0, 0))],
        out_shape=[jax.ShapeDtypeStruct((8, D_MODEL), F32)] * 4 + [jax.ShapeDtypeStruct((1, LANE), F32)],
        compiler_params=_cparams(),
    )(gathered, w, m, v)


def _unpack_small(p, shapes):
    out = {n: p[i].reshape(shapes[n]) for i, n in enumerate(SMALL_ROWS)}
    out["hg_norm_w"] = p[ROW_HG_NORM, :HEAD_DIM].reshape(shapes["hg_norm_w"])
    out["hg_lower_bounds"] = p[ROW_LB0:ROW_LB1 + 1, :HG_WIDTH].reshape(shapes["hg_lower_bounds"])
    return out


def _concat_cols(pieces, name):
    rows = pieces[0].shape[0]
    widths = [p.shape[1] for p in pieces]
    tr = ROW_BLOCK // 2

    def kern(*refs):
        o_ref, off = refs[-1], 0
        for r, w in zip(refs[:-1], widths):
            o_ref[:, off:off + w] = r[...]
            off += w

    return pl.pallas_call(
        kern, name=name, grid=(rows // tr,),
        in_specs=[pl.BlockSpec((tr, w), lambda i: (i, 0)) for w in widths],
        out_specs=pl.BlockSpec((tr, sum(widths)), lambda i: (i, 0)),
        out_shape=jax.ShapeDtypeStruct((rows, sum(widths)), pieces[0].dtype),
        compiler_params=_cparams(("parallel",)),
    )(*pieces)


WHOLE = lambda f: (f, 0, None)
MID_MATRICES = ("w_branch_a", "w_branch_b", "w_out", "wq_cross", "wkv_cross", "wo_cross")
MID_WEIGHTS = MID_MATRICES
W_IN_PIECES = [("w_in", r0, 512) for r0 in range(0, D_MODEL // 2, 512)]
W13_PIECES = [("w13", r0, 512) for r0 in range(0, D_MODEL // 2, 512)]
GATHER_GROUPS = [("mid", [WHOLE(f) for f in MID_MATRICES]), ("w13a", W13_PIECES[:1]), ("w13b", W13_PIECES[1:]),
                 ("w2", [WHOLE("w2")])]
OTHER_WEIGHTS = ["w1", "w3", "w2"] + list(MID_WEIGHTS)
BEFORE = {
    "hgrn_fwd": [("wait", "mid")],
    "mm_out": [("wait", "w13a")],
    "mm_o": [("wait", "w13b")],
    "mm_w2": [("wait", "w2"), ("run", ("d2d", [WHOLE("w2")]), "gather_hand_over_w2")],
    "mm_dwin_own": [("wait", "rs_w2"), ("chip_sum", "w2"), ("wait", "rs_w13"), ("chip_sum", "w13"),
                    ("wait", "pair_w_in")],
    "mm_dh": [("wait", "rs_mid")] + [("chip_sum", f) for f in MID_MATRICES],
}
CARRY = {
    "hgrn_fwd": [("d2d", [WHOLE(f) for f in MID_MATRICES])],
    "mm_out": [("d2d", W13_PIECES[:1])],
    "mm_o": [("d2d", W13_PIECES[1:])],
    "mm_du": [("pairx_whole", ["w2"])],
    "mm_dhf": [("pairx_whole", ["w13"])],
    "attn_bwd_g0": [("pairx", list(MID_MATRICES))],
    "mm_dh": [("share", OTHER_WEIGHTS)],
}
AFTER = {
    "mm_dw2_own": [("start", "rs_w2", [WHOLE("w2")])],
    "mm_dw13_own": [("start", "rs_w13", [WHOLE("w13")])],
    "attn_bwd_g0": [("pair_sum", f) for f in MID_MATRICES] + [("start", "rs_mid", [WHOLE(f) for f in MID_MATRICES])],
    "mm_dwin_sibling": [("start_pairx", "pair_w_in", ["w_in"])],
    "mm_dwin_own": [("start", "rs_w_in", [WHOLE("w_in")])],
}
FINISH = [
    ("adamw", OTHER_WEIGHTS), ("wait", "rs_w_in"), ("small",), ("chip_sum", "w_in"),
    ("run", ("share", ["w_in"]), "rs_sibling_share_w_in"), ("adamw", ["w_in"]),
]


class _Late:
    def __init__(self, read):
        self.read = read


class _Net:
    def __init__(self, full, pos=None, shard_shapes=None, comm=True, specs=FULL_SPECS, place=WEIGHT_PLACE):
        self.full, self.pos, self.shard_shapes, self.comm = dict(full), pos, shard_shapes, comm
        self.specs, self.place = specs, place
        self.gw, self.recv, self.psum, self.slots, self.grads = {}, {}, {}, {}, {}
        self.gw_sibling = {}
        self.pending, self.token, self.last, self.sums = {}, None, None, []

    def _make(self, kind, arg):
        if kind == "gather":
            return _gather_ici_comm(self.full, arg, self.specs)
        if kind == "ring":
            return _gather_ring_comm(self.full, *arg, self.specs)
        if kind == "d2d":
            return _gather_d2d_comm(self.full, arg, self.specs)
        if kind == "pairx":
            return _pairx_comm(self.gw, arg, self.specs)
        if kind == "pairx_whole":
            return _pairx_comm(self.gw_sibling, arg, self.specs, whole=True)
        if kind == "pairx_split":
            return _pairx_comm(self.gw_sibling, arg, self.specs, whole=True, recv=self.recv)
        if kind == "chipx":
            return _chipx_comm(self.psum, self.slots, arg, self.specs)
        assert kind == "share"
        return _share_comm(self.grads, arg, self.specs, self.place)

    def _store(self, kind, res):
        if kind in ("gather", "ring", "d2d"):
            self.full.update(res)
        elif kind in ("pairx", "pairx_whole", "pairx_split"):
            for (tag, f), a in res.items():
                (self.gw if tag == "g" else self.recv)[f] = a
        elif kind == "chipx":
            for (tag, f), a in res.items():
                (self.psum if tag == "p" else self.slots)[f] = a
        else:
            self.grads.update(res)

    def run_comm(self, item, name):
        kind, arg = item
        self._store(kind, _run_comm([self._make(kind, arg)], name)[0])

    @staticmethod
    def _others(after, items):
        own = [a for cm in items for a in cm.arrays.values()]
        return [a for a in after if a is not None and all(a is not o for o in own)]

    def start(self, groups, kind, name):
        items = [self._make(kind, jobs) for _, jobs in groups]
        after = self._others([self.last], items)
        res, sems, token = _split_start(items, name, after=after[0] if after else None)
        for (group, jobs), r, s in zip(groups, res, sems):
            self._store(kind, r)
            self.pending[group] = (kind, jobs, s)
        self.token = self.last = token

    def wait(self, group, after=()):
        kind, jobs, sems = self.pending.pop(group)
        item = self._make(kind, jobs)
        res = _split_wait([item], [sems], self._others([self.last, *after], [item]), f"wait_{group}")[0]
        self._store(kind, res)

    def step(self, step):
        if step[0] == "wait":
            self.wait(step[1], after=self.sums)
            self.sums = []
        elif step[0] == "start":
            self.start([(step[1], step[2])], "chipx", f"start_{step[1]}")
        elif step[0] == "start_pairx":
            for f in step[2]:
                self.recv[f] = lax.empty(_half_shape(*self.specs[f]), BF16)
            self.start([(step[1], step[2])], "pairx_split", f"start_{step[1]}")
        elif step[0] == "pair_sum":
            f = step[1]
            self.psum[f], self.slots[f] = _pair_sum(self.gw[f], self.recv[f], self.pos, self.specs[f],
                                                    f"rs_pair_sum_{f}", whole=f in self.gw_sibling)
        elif step[0] == "chip_sum":
            f = step[1]
            sums = _chip_sum(self.psum[f], self.slots[f], self.pos, f, self.shard_shapes,
                             self.specs, self.place, f"rs_chip_sum_{f}")
            self.grads.update(sums)
            self.sums += list(sums.values())
        else:
            assert step[0] == "run"
            self.run_comm(step[1], step[2])

    def call(self, fn, name, *args, grad_of=None, sibling_half=False, pair_sum_of=None, **kw):
        for step in (BEFORE.get(name, []) if self.comm else []):
            self.step(step)
        self.sums = []
        late = lambda a: a.read() if isinstance(a, _Late) else a
        args = [late(a) for a in args]
        kw = {key: late(val) for key, val in kw.items()}
        items = CARRY.get(name, []) if self.comm else []
        carried = [self._make(k, a) for k, a in items]
        if self.token is not None:
            carried.append(_Token(self.token))
            self.token = None
        out, res = fn(*args, name=name, carried=carried, **kw)
        if grad_of is not None:
            (self.gw_sibling if sibling_half else self.gw)[grad_of] = out
        if pair_sum_of is not None:
            self.psum[pair_sum_of] = out
            self.slots[pair_sum_of] = lax.empty((N_CHIPS - 1,) + _slot_shape(self.specs[pair_sum_of]), BF16)
        self.last = jax.tree.leaves(out)[0]
        for (kind, _), r in zip(items, res):
            self._store(kind, r)
        for step in (AFTER.get(name, []) if self.comm else []):
            self.step(step)
        return out


def _local_step(net, x, h, mem, target, small, h_halves=None, core_row=None):
    full, call = net.full, net.call
    proj = call(_mm, "mm_proj", h, full["w_in"], mode="nn", out_dtype=F32)
    att = [call(_attn_fwd, f"attn_fwd_g{g}", proj, g) for g in range(3)]
    outs, lses = [a[0] for a in att], [a[1] for a in att]
    o_att = _attn_merge_fwd(outs, lses, "attn_merge")
    oraw, o_hg, states = call(_hg_fwd, "hgrn_fwd", proj, small["hg_lower_bounds"], small["hg_norm_w"])
    ya = call(_mm, "mm_branch_a", o_att, full["w_branch_a"], mode="nn", out_dtype=F32)
    yb, merged = call(_branch_b_gate, "mm_branch_b", o_hg, full["w_branch_b"], proj, ya)
    x1, hc = call(_residual_norm, "mm_out", merged, full["w_out"], x, small["ln_cross_w"])

    mn = _rms_fwd(mem, small["ln_mem_w"], "rms_mem")
    qc = call(_mm, "mm_q", hc, full["wq_cross"], mode="nn", out_dtype=F32)
    kvc = call(_mm, "mm_kv", mn, full["wkv_cross"], mode="nn", out_dtype=F32)
    oc = call(_cross_fwd, "cross_fwd", qc, kvc)
    x2, hf, *hf_halves = call(_residual_norm, "mm_o", oc, full["wo_cross"], x1, small["ln_ffn_w"], core_row=core_row)

    ab, u = call(_ff_up, "mm_w13", hf, full["w13"])
    x3 = call(_mm, "mm_w2", u, _Late(lambda: full["w2"]), mode="nn", out_dtype=F32, res=x2)

    dx3, dg_final, loss, dx3_bf16, *dx3_halves = _loss_head(x3, small["ln_final_w"], target, "loss_head", core_row)

    gs = {"ln_final_w": dg_final}
    if net.comm:
        call(_mm, "mm_dw2_sibling", u, dx3_halves[0], mode="tn", out_dtype=BF16, grad_of="w2", sibling_half=True)
        dab = call(_ff_down_bwd, "mm_du", dx3_bf16, full["w2"], ab)
        call(_mm, "mm_dw2_own", u, dx3_halves[1], mode="tn", out_dtype=BF16, res=_Late(lambda: net.recv["w2"]),
             pair_sum_of="w2")
        call(_mm, "mm_dw13_sibling", hf_halves[0], dab, mode="tn", out_dtype=BF16, grad_of="w13", sibling_half=True)
        dhf = call(_mm, "mm_dhf", dab, full["w13"], mode="nt", out_dtype=F32)
        call(_mm, "mm_dw13_own", hf_halves[1], dab, mode="tn", out_dtype=BF16, res=_Late(lambda: net.recv["w13"]),
             pair_sum_of="w13")
    else:
        call(_mm, "mm_dw2", u, dx3_bf16, mode="tn", out_dtype=BF16, grad_of="w2")
        dab = call(_ff_down_bwd, "mm_du", dx3_bf16, full["w2"], ab)
        call(_mm, "mm_dw13", hf, dab, mode="tn", out_dtype=BF16, grad_of="w13")
        dhf = call(_mm, "mm_dhf", dab, full["w13"], mode="nt", out_dtype=F32)
    dx2, gs["ln_ffn_w"], dx2_bf16 = _rms_bwd(x2, small["ln_ffn_w"], dhf, dx3, "rms_ffn_bwd", bf16_copy=True)
    doc = call(_mm, "mm_doc", dx2_bf16, full["wo_cross"], mode="nt", out_dtype=BF16)
    call(_mm, "mm_dwo", oc, dx2_bf16, mode="tn", out_dtype=BF16, grad_of="wo_cross")
    dqc, dkvc = _cross_bwd(qc, kvc, doc, "cross_bwd")
    call(_mm, "mm_dwq", hc, dqc, mode="tn", out_dtype=BF16, grad_of="wq_cross")
    dx1, gs["ln_cross_w"], dx1_bf16 = call(_norm_bwd_residual, "mm_dhc", dqc, full["wq_cross"], x1,
                                           small["ln_cross_w"], dx2)
    call(_mm, "mm_dwkv", mn, dkvc, mode="tn", out_dtype=BF16, grad_of="wkv_cross")
    dmn = call(_mm, "mm_dmn", dkvc, full["wkv_cross"], mode="nt", out_dtype=F32)
    _, gs["ln_mem_w"] = _rms_bwd(mem, small["ln_mem_w"], dmn, None, "rms_mem_bwd")
    dya, dyb, dga, dgb = call(_dmerged_gate_bwd, "mm_dmerged", dx1_bf16, full["w_out"], proj, ya, yb)
    call(_mm, "mm_dwout", merged, dx1_bf16, mode="tn", out_dtype=BF16, grad_of="w_out")
    call(_mm, "mm_dwa", o_att, dya, mode="tn", out_dtype=BF16, grad_of="w_branch_a")
    do_att = call(_mm, "mm_doatt", dya, full["w_branch_a"], mode="nt", out_dtype=F32)
    call(_mm, "mm_dwb", o_hg, dyb, mode="tn", out_dtype=BF16, grad_of="w_branch_b")
    do_hg = call(_mm, "mm_dohg", dyb, full["w_branch_b"], mode="nt", out_dtype=F32)
    dqh, dfh, dih, dgh, dlb, gs["hg_norm_w"] = call(
        _hg_bwd, "hgrn_bwd", proj, small["hg_lower_bounds"], small["hg_norm_w"], oraw, states, do_hg)
    gs["hg_lb"] = dlb
    do_gs, dl_gs = call(_attn_merge_bwd, "attn_merge_bwd", outs, lses, do_att)
    dqs, dks, dvs = zip(*[call(_attn_bwd, f"attn_bwd_g{g}", proj, g, lses[g], do_gs[g], dl_gs[g]) for g in range(3)])
    dproj = _concat_cols([*dqs, *dks, *dvs, dqh, dfh, dih, dgh, dga, dgb], "dproj_concat")
    if net.comm:
        h_sibling, h_own = h_halves
        call(_mm, "mm_dwin_sibling", h_sibling, dproj, mode="tn", out_dtype=BF16, grad_of="w_in", sibling_half=True)
        call(_mm, "mm_dwin_own", h_own, dproj, mode="tn", out_dtype=BF16, res=_Late(lambda: net.recv["w_in"]),
             pair_sum_of="w_in")
    else:
        call(_mm, "mm_dwin", h, dproj, mode="tn", out_dtype=BF16, grad_of="w_in")
    dh = call(_mm, "mm_dh", dproj, full["w_in"], mode="nt", out_dtype=F32)
    dx, gs["ln_mix_w"] = _rms_bwd(x, small["ln_mix_w"], dh, dx1, "rms_mix_bwd")
    return loss, dx, gs


WEIGHT_ORDER = ("ln_mix_w", "w_in", "hg_norm_w", "hg_lower_bounds", "w_branch_a", "w_branch_b", "w_out",
                "ln_cross_w", "ln_mem_w", "wq_cross", "wkv_cross", "wo_cross", "ln_ffn_w", "w1", "w3", "w2",
                "ln_final_w")


def kernel(x, mem, ln_mix_w, w_in, hg_norm_w, hg_lower_bounds, w_branch_a, w_branch_b, w_out, ln_cross_w, ln_mem_w, wq_cross, wkv_cross, wo_cross, ln_ffn_w, w1, w3, w2, ln_final_w, loss_target, m_ln_mix_w, m_w_in, m_hg_norm_w, m_hg_lower_bounds, m_w_branch_a, m_w_branch_b, m_w_out, m_ln_cross_w, m_ln_mem_w, m_wq_cross, m_wkv_cross, m_wo_cross, m_ln_ffn_w, m_w1, m_w3, m_w2, m_ln_final_w, v_ln_mix_w, v_w_in, v_hg_norm_w, v_hg_lower_bounds, v_w_branch_a, v_w_branch_b, v_w_out, v_ln_cross_w, v_ln_mem_w, v_wq_cross, v_wkv_cross, v_wo_cross, v_ln_ffn_w, v_w1, v_w3, v_w2, v_ln_final_w):
    args = dict(locals())
    w = {n: args[n] for n in WEIGHT_ORDER}
    m = {n: args["m_" + n] for n in WEIGHT_ORDER}
    v = {n: args["v_" + n] for n in WEIGHT_ORDER}
    shapes = {n: w[n].shape for n in WEIGHT_ORDER}
    mat = lambda a: a.reshape(a.shape[-2:])
    shard_shapes = {n: shapes[n][-2:] for n in BIG_WEIGHTS}

    pos = _mesh_scalars()

    def cast(f, token=None):
        return _cast_into_full({n: mat(w[n]) for n in BIG_WEIGHTS if WEIGHT_PLACE[n][0] == f}, f, pos,
                               FULL_SPECS, WEIGHT_PLACE, f"cast_{f}", token)

    net = _Net({"w_in": cast("w_in")}, pos, shard_shapes)
    net.start([(f"ring_a{i}", (*job, "a")) for i, job in enumerate(W_IN_PIECES)], "ring", "gather_start_w_in")
    rest = {f: cast(f, net.token) for f in FULL_SPECS if f != "w_in"}
    net.full.update(rest)
    small = {n: w[n].reshape(1, -1) for n in SMALL_ROWS}
    small["hg_norm_w"] = w["hg_norm_w"].reshape(1, HEAD_DIM)
    small["hg_lower_bounds"] = w["hg_lower_bounds"]
    x2d = x.reshape(SEQ, D_MODEL)
    h, *h_halves = _rms_fwd_halves(x2d, small["ln_mix_w"], pos, "rms_mix")
    for i, job in enumerate(W_IN_PIECES):
        net.wait(f"ring_a{i}", after=[*rest.values(), h] if i == 0 else ())
        net.start([(f"ring_b{i}", (*job, "b"))], "ring", f"gather_pass_on_w_in{i}")
    net.start(GATHER_GROUPS, "gather", "gather_start_rest")
    for i, job in enumerate(W_IN_PIECES):
        net.wait(f"ring_b{i}")
        net.run_comm(("d2d", [job]), f"gather_hand_over_w_in{i}")
    core_row = jnp.full((1, D_MODEL), lax.axis_index("c").astype(F32))
    loss, dx, gs = _local_step(net, x2d, h, mem.reshape(MEM_LEN, D_MODEL), loss_target.reshape(SEQ, D_MODEL), small,
                               h_halves, core_row)

    out_g, out_d, out_m, out_v = {}, {}, {}, {}
    net.last = dx
    for step in FINISH:
        if step[0] == "adamw":
            for n in step[1]:
                out_d[n], out_m[n], out_v[n], out_g[n] = net.call(_adamw, f"adamw_{n}", mat(w[n]), net.grads[n],
                                                                  mat(m[n]), mat(v[n]))
        elif step[0] == "wait":
            net.wait(step[1], after=list(out_d.values()))
        elif step[0] == "small":
            pad = lambda a: jnp.pad(a, ((0, 0), (0, D_MODEL - a.shape[1])))
            part = jnp.concatenate(
                [gs[n] for n in SMALL_ROWS]
                + [pad(jnp.concatenate([gs["hg_norm_w"][0], loss], axis=1)), pad(gs["hg_lb"]),
                   pad(gs["hg_norm_w"][1]) if len(gs["hg_norm_w"]) > 1 else jnp.zeros((1, D_MODEL), F32)], axis=0)
            part, net.psum["w_in"] = lax.optimization_barrier((part, net.psum["w_in"]))
            sg, sd, sm, sv, loss_tot = _small_update(_gather_rows(part), _pack_small(w), _pack_small(m),
                                                     _pack_small(v))
            for dst, packed in ((out_g, sg), (out_d, sd), (out_m, sm), (out_v, sv)):
                dst.update(_unpack_small(packed, shapes))
        else:
            net.step(step)

    result = [loss_tot[0, 0], dx.reshape(x.shape)]
    for group in (out_g, out_d, out_m, out_v):
        result += [group[n].reshape(shapes[n]) for n in WEIGHT_ORDER]
    return tuple(result)
```

```python
import math

import jax
import jax.numpy as jnp
from jax import lax
from jax.experimental import pallas as pl
from jax.experimental.pallas import tpu as pltpu

F32 = jnp.float32
BF16 = jnp.bfloat16
MESH = pl.DeviceIdType.MESH

D_MODEL = 2048
SEQ = 2048
HEAD_DIM = 128
MEM_LEN = 256
ATT_GROUPS = ((128, 1), (512, 4), (2048, 16))
ATT_HEADS = 4
ATT_WIDTH = 3 * ATT_HEADS * HEAD_DIM
ATT_OUT = ATT_HEADS * HEAD_DIM
ATT_BLOCK = 128
HG_HEADS = 8
HG_WIDTH = HG_HEADS * HEAD_DIM
HG_CHUNK = 64
IN_WIDTH = 3 * ATT_WIDTH + 4 * HG_WIDTH + 2 * D_MODEL
CROSS_HEADS = 4
CROSS_WIDTH = CROSS_HEADS * HEAD_DIM
D_FF = 5632
RMS_EPS = 1e-6
ADAM_LR = 0.001
ADAM_B1 = 0.9
ADAM_B2 = 0.999
ADAM_EPS = 1e-08
ADAM_WD = 0.01
ADAM_STEP = 10
N_CHIPS = 4
N_DEV = 8

VMEM_LIMIT_BYTES = 56 * 1024 * 1024
LANE = 128
MXU_WIDTH = 256
MM_TILE_CAP = 1536
TRANSPOSE_CHUNK = 512
ANY = pl.BlockSpec(memory_space=pl.ANY)


def _cparams(sem=None):
    return pltpu.CompilerParams(dimension_semantics=sem, vmem_limit_bytes=VMEM_LIMIT_BYTES)


def _div(n, cap, mult):
    best = None
    for d in range(mult, min(n, cap) + 1, mult):
        if n % d == 0:
            best = d
    assert best is not None, (n, cap, mult)
    return best


def _sigmoid(x):
    return 1.0 / (1.0 + jnp.exp(-x))


def _dot(a, b):
    return jnp.dot(a.astype(BF16), b.astype(BF16), preferred_element_type=F32)


def _dot_nt(a, b):
    return lax.dot_general(a.astype(BF16), b.astype(BF16), (((1,), (1,)), ((), ())),
                           preferred_element_type=F32)


def _dot_tn(a, b):
    return jnp.dot(a.astype(F32).T.astype(BF16), b.astype(BF16), preferred_element_type=F32)


def _dot_exact(a, b):
    return jnp.dot(a, b, precision=lax.Precision.HIGHEST, preferred_element_type=F32)


class _Carried:
    def __init__(self, arrays, fresh, n_sems, start, finish, mid=None, reads=None):
        self.arrays, self.fresh, self.n_sems, self.reads = arrays, fresh, n_sems, reads or {}
        self.start, self.mid, self.finish = start, mid, finish


class _Token:
    def __init__(self, array):
        self.array = array


TOKEN_SHAPE = (8, LANE)


def _carried_layout(carried):
    akeys = list(dict.fromkeys(k for cm in carried for k in cm.arrays))
    fkeys = [(ci, k) for ci, cm in enumerate(carried) for k in cm.fresh]
    arrays = [next(cm.arrays[k] for cm in carried if k in cm.arrays) for k in akeys]
    shapes = [jax.ShapeDtypeStruct(a.shape, a.dtype) for a in arrays] + [carried[ci].fresh[k] for ci, k in fkeys]
    sems = []
    for cm in carried:
        sems += [pltpu.SemaphoreType.DMA((cm.n_sems,)), pltpu.SemaphoreType.DMA((cm.n_sems,))]
    return akeys, fkeys, arrays, shapes, sems


def _carried_reads(carried):
    rkeys = list(dict.fromkeys(k for cm in carried for k in cm.reads))
    return rkeys, [next(cm.reads[k] for cm in carried if k in cm.reads) for k in rkeys]


def _carried_results(carried, akeys, fkeys, outs, rkeys=(), read_refs=()):
    shared = dict(zip(akeys, outs[:len(akeys)]))
    shared.update(zip(rkeys, read_refs))
    res = [{k: shared[k] for k in list(cm.arrays) + [r for r in cm.reads if r in shared]} for cm in carried]
    for (ci, k), o in zip(fkeys, outs[len(akeys):]):
        res[ci][k] = o
    return res


def _pcall(kern, *, name, grid, in_specs, out_specs, out_shape, args, scratch_shapes=(), semantics=None,
           carried=()):
    tokens = [c.array for c in carried if isinstance(c, _Token)]
    carried = [c for c in carried if not isinstance(c, _Token)]
    single = not isinstance(out_shape, (list, tuple))
    out_specs = [out_specs] if single else list(out_specs)
    out_shape = [out_shape] if single else list(out_shape)
    n_real, n_out, n_scr = len(in_specs), len(out_shape), len(scratch_shapes)
    in_specs = list(in_specs) + [pl.BlockSpec(TOKEN_SHAPE, lambda *_: (0, 0))] * len(tokens)
    args = list(args) + tokens
    n_in = len(in_specs)
    if not carried:
        def plain(*refs):
            kern(*refs[:n_real], *refs[n_in:])

        outs = pl.pallas_call(plain if tokens else kern, name=name, grid=grid, in_specs=in_specs,
                              out_specs=out_specs, out_shape=out_shape, scratch_shapes=list(scratch_shapes),
                              compiler_params=_cparams(semantics))(*args)
        return (outs[0] if single else list(outs)), []
    akeys, fkeys, arrays, shapes, sems = _carried_layout(carried)
    rkeys, reads = _carried_reads(carried)
    n_a, n_f, n_r = len(akeys), len(fkeys), len(rkeys)
    total = math.prod(grid)
    mid_step = min(total - 1, (17 * total) // 20)

    def wrapped(*refs):
        ins = refs[:n_real]
        r0 = n_in + n_a
        o0 = r0 + n_r
        outs = refs[o0:o0 + n_out]
        a0 = o0 + n_out
        s0 = a0 + n_a + n_f
        per = _carried_results(carried, akeys, fkeys, refs[a0:s0], rkeys, refs[r0:o0])
        scratch = refs[s0:s0 + n_scr]
        sem = refs[s0 + n_scr:]
        step = 0
        for d, g in enumerate(grid):
            step = step * g + pl.program_id(d)

        @pl.when(step == 0)
        def _():
            for ci, cm in enumerate(carried):
                cm.start(per[ci], sem[2 * ci], sem[2 * ci + 1])

        kern(*ins, *outs, *scratch)

        @pl.when(step == mid_step)
        def _():
            for ci, cm in enumerate(carried):
                if cm.mid is not None:
                    cm.mid(per[ci], sem[2 * ci], sem[2 * ci + 1])

        @pl.when(step == total - 1)
        def _():
            for ci, cm in enumerate(carried):
                cm.finish(per[ci], sem[2 * ci], sem[2 * ci + 1])

    outs = pl.pallas_call(
        wrapped, name=name, grid=grid,
        in_specs=list(in_specs) + [ANY] * (n_a + n_r), out_specs=out_specs + [ANY] * (n_a + n_f),
        out_shape=out_shape + shapes,
        input_output_aliases={n_in + i: n_out + i for i in range(n_a)},
        scratch_shapes=list(scratch_shapes) + sems,
        compiler_params=_cparams(("arbitrary",) * len(grid)),
    )(*args, *arrays, *reads)
    res = _carried_results(carried, akeys, fkeys, outs[n_out:])
    return (outs[0] if single else list(outs[:n_out])), res


def _run_comm(carried, name):
    carried = list(carried)
    akeys, fkeys, arrays, shapes, sems = _carried_layout(carried)
    rkeys, reads = _carried_reads(carried)
    n_a, n_f, n_r = len(akeys), len(fkeys), len(rkeys)

    def body(*refs):
        o0 = n_a + n_r
        per = _carried_results(carried, akeys, fkeys, refs[o0:o0 + n_a + n_f], rkeys, refs[n_a:o0])
        sem = refs[o0 + n_a + n_f:]
        for hook in ("start", "mid", "finish"):
            for ci, cm in enumerate(carried):
                fn = getattr(cm, hook)
                if fn is not None:
                    fn(per[ci], sem[2 * ci], sem[2 * ci + 1])

    outs = pl.pallas_call(
        body, name=name, in_specs=[ANY] * (n_a + n_r), out_specs=[ANY] * (n_a + n_f), out_shape=shapes,
        input_output_aliases={i: i for i in range(n_a)}, scratch_shapes=sems,
    )(*arrays, *reads)
    return _carried_results(carried, akeys, fkeys, outs)


HBM_SPEC = pl.BlockSpec(memory_space=pltpu.HBM)
SEM_SPEC = pl.BlockSpec(memory_space=pltpu.SEMAPHORE)
SPLIT_EFFECT = pltpu.SideEffectType.DATAFLOW_SIDE_EFFECTING


def _in_hbm(a):
    return pltpu.with_memory_space_constraint(a, pltpu.HBM)


def _split_start(items, name, after=None):
    items = list(items)
    akeys, fkeys, arrays, shapes, sems = _carried_layout(items)
    assert not fkeys
    rkeys, reads = _carried_reads(items)
    n_a, n_s = len(akeys), len(sems)
    n_after = n_a + (after is not None)
    n_in = n_after + len(rkeys)

    def body(*refs):
        per = _carried_results(items, akeys, [], refs[n_in:n_in + n_a], rkeys, refs[n_after:n_in])
        sem = refs[n_in + n_a:n_in + n_a + n_s]
        for ci, cm in enumerate(items):
            cm.start(per[ci], sem[2 * ci], sem[2 * ci + 1])
        token = refs[n_in + n_a + n_s]
        token[...] = jnp.zeros_like(token)

    outs = pl.pallas_call(
        body, name=name, in_specs=[HBM_SPEC] * n_a + [ANY] * (n_in - n_a),
        out_specs=[HBM_SPEC] * n_a + [SEM_SPEC] * n_s + [pl.BlockSpec(memory_space=pltpu.VMEM)],
        out_shape=[pltpu.HBM(s.shape, s.dtype) for s in shapes] + sems + [jax.ShapeDtypeStruct(TOKEN_SHAPE, F32)],
        input_output_aliases={i: i for i in range(n_a)},
        compiler_params=pltpu.CompilerParams(has_side_effects=SPLIT_EFFECT),
    )(*[_in_hbm(a) for a in arrays], *([after] if after is not None else []), *reads)
    res = _carried_results(items, akeys, [], outs[:n_a])
    sem_out = outs[n_a:n_a + n_s]
    return res, [(sem_out[2 * ci], sem_out[2 * ci + 1]) for ci in range(len(items))], outs[-1]


def _split_wait(items, sems, after, name):
    items = list(items)
    after = list(after) if isinstance(after, (list, tuple)) else [after]
    akeys, fkeys, arrays, shapes, _ = _carried_layout(items)
    rkeys, reads = _carried_reads(items)
    n_a, n_s = len(akeys), 2 * len(items)
    n_after = n_a + n_s + len(after)
    n_in = n_after + len(rkeys)

    def body(*refs):
        per = _carried_results(items, akeys, [], refs[n_in:], rkeys, refs[n_after:n_in])
        sem = refs[n_a:n_a + n_s]
        for ci, cm in enumerate(items):
            cm.finish(per[ci], sem[2 * ci], sem[2 * ci + 1])

    outs = pl.pallas_call(
        body, name=name, in_specs=[HBM_SPEC] * n_a + [SEM_SPEC] * n_s + [ANY] * (n_in - n_a - n_s),
        out_specs=[HBM_SPEC] * n_a, out_shape=[pltpu.HBM(s.shape, s.dtype) for s in shapes],
        input_output_aliases={i: i for i in range(n_a)},
        compiler_params=pltpu.CompilerParams(has_side_effects=SPLIT_EFFECT),
    )(*arrays, *[s for pair in sems for s in pair], *after, *reads)
    return _carried_results(items, akeys, [], outs)


def _mm(a, b, *, mode, out_dtype, name, res=None, carried=(), fused=None):
    if mode == "nn":
        (m, k), (k2, n) = a.shape, b.shape
    elif mode == "nt":
        (m, k), (n, k2) = a.shape, b.shape
    else:
        (k, m), (k2, n) = a.shape, b.shape
    assert k == k2, (name, a.shape, b.shape)
    tm = _div(m, MM_TILE_CAP, LANE)
    tn = _div(n, MM_TILE_CAP, MXU_WIDTH) if n % MXU_WIDTH == 0 else 0
    if tn < 1024:
        tn = _div(n, MM_TILE_CAP, LANE)
    out_shape = jax.ShapeDtypeStruct((m, n), out_dtype)

    if fused is not None:
        assert mode in ("nn", "nt") and res is None and k <= 2048
        tm, tn = fused["tile"]
        dot = _dot if mode == "nn" else _dot_nt
        n_x = len(fused["ins"])

        summed = [len(e) > 2 for e in fused["outs"]]

        def kern_fused(*refs):
            part = dot(refs[0][...], refs[1][...])
            outs = fused["post"](part, *[r[...] for r in refs[2:2 + n_x]])
            first_row_tile = pl.program_id(1) == 0
            for o_ref, val, acc in zip(refs[2 + n_x:], outs, summed):
                if not acc:
                    o_ref[...] = val.astype(o_ref.dtype)
                    continue

                @pl.when(first_row_tile)
                def _():
                    o_ref[...] = val

                @pl.when(jnp.logical_not(first_row_tile))
                def _():
                    o_ref[...] += val

        def tile_spec(shape, index=lambda i, j: (i, j)):
            return pl.BlockSpec(shape, lambda j, i: index(i, j))

        return _pcall(
            kern_fused, name=name, grid=(n // tn, m // tm),
            in_specs=[pl.BlockSpec((tm, k), lambda j, i: (i, 0)),
                      pl.BlockSpec((k, tn), lambda j, i: (0, j)) if mode == "nn"
                      else pl.BlockSpec((tn, k), lambda j, i: (j, 0))] + [tile_spec(*e[1:]) for e in fused["ins"]],
            out_specs=[tile_spec(*e[1:]) for e in fused["outs"]], out_shape=[e[0] for e in fused["outs"]],
            args=(a, b, *[e[0] for e in fused["ins"]]), semantics=("parallel", "arbitrary"), carried=carried)

    if mode == "tn":
        has_res_tn = res is not None

        def kern_tn(*refs):
            a_ref, b_ref, o_ref, at_ref = refs[0], refs[1], refs[-2], refs[-1]

            @pl.when(pl.program_id(1) == 0)
            def _():
                step = min(TRANSPOSE_CHUNK, k)
                for c0 in range(0, k, step):
                    at_ref[:, c0:c0 + step] = a_ref[c0:c0 + step, :].astype(F32).T.astype(BF16)

            part = jnp.dot(at_ref[...], b_ref[...].astype(BF16), preferred_element_type=F32)
            if has_res_tn:
                part = part + refs[2][...].astype(F32)
            o_ref[...] = part.astype(o_ref.dtype)

        o_spec = pl.BlockSpec((tm, tn), lambda i, j: (i, j))
        return _pcall(
            kern_tn, name=name, grid=(m // tm, n // tn),
            in_specs=[pl.BlockSpec((k, tm), lambda i, j: (0, i)),
                      pl.BlockSpec((k, tn), lambda i, j: (0, j))] + [o_spec] * has_res_tn,
            out_specs=o_spec, out_shape=out_shape, args=(a, b) + ((res,) if has_res_tn else ()),
            scratch_shapes=[pltpu.VMEM((tm, k), BF16)],
            semantics=("parallel", "arbitrary"), carried=carried)

    tk = k if k <= 2048 else _div(k, 3072, LANE)
    nk = k // tk
    a_spec = pl.BlockSpec((tm, tk), lambda i, j, kk: (i, kk))
    if mode == "nn":
        b_spec = pl.BlockSpec((tk, tn), lambda i, j, kk: (kk, j))
        dot = _dot
    else:
        b_spec = pl.BlockSpec((tn, tk), lambda i, j, kk: (j, kk))
        dot = _dot_nt
    o_spec = pl.BlockSpec((tm, tn), lambda i, j, kk: (i, j))
    in_specs = [a_spec, b_spec]
    args = [a, b]
    if res is not None:
        in_specs.append(o_spec)
        args.append(res)
    has_res = res is not None

    def kern(*refs):
        a_ref, b_ref = refs[0], refs[1]
        r_ref = refs[2] if has_res else None
        o_ref = refs[3] if has_res else refs[2]
        part = dot(a_ref[...], b_ref[...])
        if nk == 1:
            if has_res:
                part = part + r_ref[...]
            o_ref[...] = part.astype(o_ref.dtype)
            return
        acc_ref = refs[-1]
        kk = pl.program_id(2)

        @pl.when(kk == 0)
        def _():
            acc_ref[...] = part

        @pl.when(kk > 0)
        def _():
            acc_ref[...] += part

        @pl.when(kk == nk - 1)
        def _():
            tot = acc_ref[...]
            if has_res:
                tot = tot + r_ref[...]
            o_ref[...] = tot.astype(o_ref.dtype)

    return _pcall(
        kern, name=name, grid=(m // tm, n // tn, nk),
        in_specs=in_specs, out_specs=o_spec, out_shape=out_shape, args=args,
        scratch_shapes=[pltpu.VMEM((tm, tn), F32)] if nk > 1 else [],
        semantics=("parallel", "parallel", "arbitrary"), carried=carried)


ROW_BLOCK = 512


def _rms_fwd(x, g, name):
    t, d = x.shape
    tr = min(ROW_BLOCK, t)

    def kern(x_ref, g_ref, o_ref):
        xf = x_ref[...]
        r = lax.rsqrt(jnp.mean(xf * xf, axis=-1, keepdims=True) + RMS_EPS)
        o_ref[...] = (xf * r * g_ref[...]).astype(o_ref.dtype)

    return pl.pallas_call(
        kern, name=name, grid=(t // tr,),
        in_specs=[pl.BlockSpec((tr, d), lambda i: (i, 0)), pl.BlockSpec((1, d), lambda i: (0, 0))],
        out_specs=pl.BlockSpec((tr, d), lambda i: (i, 0)),
        out_shape=jax.ShapeDtypeStruct((t, d), BF16),
        compiler_params=_cparams(("parallel",)),
    )(x, g)


def _rms_fwd_halves(x, g, pos, name):
    t, d = x.shape
    tr = min(ROW_BLOCK, t)
    half = d // 2

    def kern(pos_ref, x_ref, g_ref, o_ref, sib_ref, own_ref):
        xf = x_ref[...]
        r = lax.rsqrt(jnp.mean(xf * xf, axis=-1, keepdims=True) + RMS_EPS)
        h = (xf * r * g_ref[...]).astype(o_ref.dtype)
        o_ref[...] = h
        is_south = pos_ref[0] == 0
        sib_ref[...] = jnp.where(is_south, h[:, half:], h[:, :half])
        own_ref[...] = jnp.where(is_south, h[:, :half], h[:, half:])

    row = pl.BlockSpec((tr, d), lambda i, pos_ref: (i, 0))
    part = pl.BlockSpec((tr, half), lambda i, pos_ref: (i, 0))
    return pl.pallas_call(
        kern, name=name,
        grid_spec=_grid_spec((t // tr,), [row, pl.BlockSpec((1, d), lambda i, pos_ref: (0, 0))], [row, part, part]),
        out_shape=[jax.ShapeDtypeStruct((t, d), BF16)] + [jax.ShapeDtypeStruct((t, half), BF16)] * 2,
        compiler_params=_cparams(("parallel",)),
    )(pos, x, g)


def _rms_bwd(x, g, dh, res, name, bf16_copy=False):
    t, d = x.shape
    tr = min(ROW_BLOCK, t)
    has_res = res is not None

    def kern(*refs):
        x_ref, g_ref, dh_ref = refs[:3]
        r_ref = refs[3] if has_res else None
        dx_ref, dg_ref = refs[3 + has_res], refs[4 + has_res]
        xf = x_ref[...]
        r = lax.rsqrt(jnp.mean(xf * xf, axis=-1, keepdims=True) + RMS_EPS)
        xn = xf * r
        dh_ = dh_ref[...]
        dhg = dh_ * g_ref[...]
        dx = r * (dhg - xn * jnp.mean(dhg * xn, axis=-1, keepdims=True))
        if has_res:
            dx = dx + r_ref[...]
        dx_ref[...] = dx
        if bf16_copy:
            refs[-1][...] = dx.astype(BF16)
        part = jnp.sum(dh_ * xn, axis=0, keepdims=True)

        @pl.when(pl.program_id(0) == 0)
        def _():
            dg_ref[...] = part

        @pl.when(pl.program_id(0) > 0)
        def _():
            dg_ref[...] += part

    row = pl.BlockSpec((tr, d), lambda i: (i, 0))
    vec = pl.BlockSpec((1, d), lambda i: (0, 0))
    in_specs = [row, vec, row] + ([row] if has_res else [])
    args = [x, g, dh] + ([res] if has_res else [])
    return pl.pallas_call(
        kern, name=name, grid=(t // tr,), in_specs=in_specs, out_specs=[row, vec] + [row] * bf16_copy,
        out_shape=[jax.ShapeDtypeStruct((t, d), F32), jax.ShapeDtypeStruct((1, d), F32)]
        + [jax.ShapeDtypeStruct((t, d), BF16)] * bf16_copy,
        compiler_params=_cparams(("arbitrary",)),
    )(*args)


def _loss_head(x3, g, target, name, core_row=None):
    t, d = x3.shape
    tr = ROW_BLOCK
    with_halves = core_row is not None

    def kern(x_ref, g_ref, t_ref, *rest):
        dx_ref, dg_ref, loss_ref, dxb_ref = rest[with_halves:with_halves + 4]
        xf = x_ref[...]
        r = lax.rsqrt(jnp.mean(xf * xf, axis=-1, keepdims=True) + RMS_EPS)
        xn = xf * r
        gg = g_ref[...]
        err = xn * gg - t_ref[...]
        lpart = 0.5 * jnp.sum(jnp.mean(err * err, axis=-1, keepdims=True), axis=0, keepdims=True)
        dy = err * (1.0 / d)
        dyg = dy * gg
        dx = r * (dyg - xn * jnp.mean(dyg * xn, axis=-1, keepdims=True))
        dx_ref[...] = dx
        dxb = dx.astype(BF16)
        dxb_ref[...] = dxb
        if with_halves:
            rest[-2][...], rest[-1][...] = _column_halves(dxb, rest[0][...])
        gpart = jnp.sum(dy * xn, axis=0, keepdims=True)
        lrow = jnp.broadcast_to(lpart, (1, LANE))

        @pl.when(pl.program_id(0) == 0)
        def _():
            dg_ref[...] = gpart
            loss_ref[...] = lrow

        @pl.when(pl.program_id(0) > 0)
        def _():
            dg_ref[...] += gpart
            loss_ref[...] += lrow

    row = pl.BlockSpec((tr, d), lambda i: (i, 0))
    vec = pl.BlockSpec((1, d), lambda i: (0, 0))
    part = pl.BlockSpec((tr, d // 2), lambda i: (i, 0))
    return pl.pallas_call(
        kern, name=name, grid=(t // tr,), in_specs=[row, vec, row] + [vec] * with_halves,
        out_specs=[row, vec, pl.BlockSpec((1, LANE), lambda i: (0, 0)), row] + [part] * (2 * with_halves),
        out_shape=[jax.ShapeDtypeStruct((t, d), F32), jax.ShapeDtypeStruct((1, d), F32),
                   jax.ShapeDtypeStruct((1, LANE), F32), jax.ShapeDtypeStruct((t, d), BF16)]
        + [jax.ShapeDtypeStruct((t, d // 2), BF16)] * (2 * with_halves),
        compiler_params=_cparams(("arbitrary",)),
    )(x3, g, target, *([core_row] if with_halves else []))


ATT_SCALE = HEAD_DIM ** -0.5
Q_BLOCK0, K_BLOCK0, V_BLOCK0 = 0, ATT_WIDTH // HEAD_DIM, 2 * ATT_WIDTH // HEAD_DIM


def _residue_rows(dil, r, n):
    if dil == 1:
        return pl.ds(n * ATT_BLOCK, ATT_BLOCK)
    return pl.ds(n * ATT_BLOCK * dil + r, ATT_BLOCK, stride=dil)


def _band_mask(with_prev):
    width = 2 * ATT_BLOCK if with_prev else ATT_BLOCK
    iq = lax.broadcasted_iota(jnp.int32, (ATT_BLOCK, width), 0)
    ik = lax.broadcasted_iota(jnp.int32, (ATT_BLOCK, width), 1)
    if not with_prev:
        return ik <= iq
    return ((ik < ATT_BLOCK) & (iq <= ik)) | ((ik >= ATT_BLOCK) & ((ik - ATT_BLOCK) <= iq))


def _band_keys(ref, dil, r, n):
    own = ref[_residue_rows(dil, r, n), :]
    if n == 0:
        return own
    return jnp.concatenate([ref[_residue_rows(dil, r, n - 1), :], own], axis=0)


def _attn_col_spec(base, grp):
    return pl.BlockSpec((SEQ, HEAD_DIM), lambda h: (0, base + grp * ATT_HEADS + h))


def _attn_fwd(proj, grp, name, carried=()):
    _, dil = ATT_GROUPS[grp]
    nb = SEQ // dil // ATT_BLOCK

    def kern(q_ref, k_ref, v_ref, o_ref, lse_ref):
        for r in range(dil):
            for n in range(nb):
                rows = _residue_rows(dil, r, n)
                s = _dot_nt(q_ref[rows, :], _band_keys(k_ref, dil, r, n)) * ATT_SCALE
                s = jnp.where(_band_mask(n > 0), s, -jnp.inf)
                m = jnp.max(s, axis=-1, keepdims=True)
                p = jnp.exp(s - m)
                l = jnp.sum(p, axis=-1, keepdims=True)
                o_ref[rows, :] = _dot(p / l, _band_keys(v_ref, dil, r, n))
                lse_ref[rows, :] = jnp.broadcast_to(m + jnp.log(l), (ATT_BLOCK, HEAD_DIM))

    out_spec = pl.BlockSpec((SEQ, HEAD_DIM), lambda h: (0, h))
    return _pcall(
        kern, name=name, grid=(ATT_HEADS,),
        in_specs=[_attn_col_spec(Q_BLOCK0, grp), _attn_col_spec(K_BLOCK0, grp), _attn_col_spec(V_BLOCK0, grp)],
        out_specs=[out_spec, out_spec],
        out_shape=[jax.ShapeDtypeStruct((SEQ, ATT_OUT), F32)] * 2, args=(proj, proj, proj),
        semantics=("parallel",), carried=carried)


def _attn_weights(l0, l1, l2):
    mx = jnp.maximum(jnp.maximum(l0, l1), l2)
    e0, e1, e2 = jnp.exp(l0 - mx), jnp.exp(l1 - mx), jnp.exp(l2 - mx)
    den = e0 + e1 + e2
    return e0 / den, e1 / den, e2 / den


def _attn_merge_fwd(outs, lses, name):
    tr = ROW_BLOCK

    def kern(o0, o1, o2, l0, l1, l2, out_ref):
        a0, a1, a2 = _attn_weights(l0[...], l1[...], l2[...])
        out_ref[...] = (a0 * o0[...] + a1 * o1[...] + a2 * o2[...]).astype(out_ref.dtype)

    spec = pl.BlockSpec((tr, ATT_OUT), lambda i: (i, 0))
    return pl.pallas_call(
        kern, name=name, grid=(SEQ // tr,), in_specs=[spec] * 6, out_specs=spec,
        out_shape=jax.ShapeDtypeStruct((SEQ, ATT_OUT), BF16),
        compiler_params=_cparams(("parallel",)),
    )(*outs, *lses)


def _attn_merge_bwd(outs, lses, do_att, name, carried=()):
    tr = ROW_BLOCK

    def kern(o0, o1, o2, l0, l1, l2, do_ref, d0, d1, d2, t0, t1, t2):
        alphas = _attn_weights(l0[...], l1[...], l2[...])
        do = do_ref[...]
        o_att = alphas[0] * o0[...] + alphas[1] * o1[...] + alphas[2] * o2[...]
        prod = do * o_att
        parts = []
        for h in range(ATT_HEADS):
            sl = slice(h * HEAD_DIM, (h + 1) * HEAD_DIM)
            tot = jnp.sum(prod[:, sl], axis=-1, keepdims=True)
            parts.append(jnp.broadcast_to(tot, (tr, HEAD_DIM)))
        dd = jnp.concatenate(parts, axis=1)
        for a, d_ref, t_ref in zip(alphas, (d0, d1, d2), (t0, t1, t2)):
            d_ref[...] = a * do
            t_ref[...] = -a * dd

    spec = pl.BlockSpec((tr, ATT_OUT), lambda i: (i, 0))
    res, cres = _pcall(
        kern, name=name, grid=(SEQ // tr,), in_specs=[spec] * 7, out_specs=[spec] * 6,
        out_shape=[jax.ShapeDtypeStruct((SEQ, ATT_OUT), F32)] * 6, args=(*outs, *lses, do_att),
        semantics=("parallel",), carried=carried)
    return (res[:3], res[3:]), cres


def _attn_bwd(proj, grp, lse, do_g, dl_g, name, carried=()):
    _, dil = ATT_GROUPS[grp]
    nb = SEQ // dil // ATT_BLOCK

    def kern(q_ref, k_ref, v_ref, do_ref, lse_ref, dl_ref, dq_ref, dk_ref, dv_ref, dq_acc, dk_acc, dv_acc):
        dk_acc[...] = jnp.zeros_like(dk_acc)
        dv_acc[...] = jnp.zeros_like(dv_acc)
        for r in range(dil):
            for n in range(nb):
                rows = _residue_rows(dil, r, n)
                q, do = q_ref[rows, :], do_ref[rows, :]
                kk, vv = _band_keys(k_ref, dil, r, n), _band_keys(v_ref, dil, r, n)
                s = _dot_nt(q, kk) * ATT_SCALE
                p = jnp.where(_band_mask(n > 0), jnp.exp(s - lse_ref[rows, :][:, :1]), 0.0)
                ds = p * (_dot_nt(do, vv) + dl_ref[rows, :][:, :1])
                dq_acc[rows, :] = _dot(ds, kk) * ATT_SCALE
                dk = _dot_tn(ds, q) * ATT_SCALE
                dv = _dot_tn(p, do)
                if n > 0:
                    prev = _residue_rows(dil, r, n - 1)
                    dk_acc[prev, :] += dk[:ATT_BLOCK]
                    dv_acc[prev, :] += dv[:ATT_BLOCK]
                    dk, dv = dk[ATT_BLOCK:], dv[ATT_BLOCK:]
                dk_acc[rows, :] += dk
                dv_acc[rows, :] += dv
        dq_ref[...] = dq_acc[...].astype(dq_ref.dtype)
        dk_ref[...] = dk_acc[...].astype(dk_ref.dtype)
        dv_ref[...] = dv_acc[...].astype(dv_ref.dtype)

    spec = pl.BlockSpec((SEQ, HEAD_DIM), lambda h: (0, h))
    return _pcall(
        kern, name=name, grid=(ATT_HEADS,),
        in_specs=[_attn_col_spec(Q_BLOCK0, grp), _attn_col_spec(K_BLOCK0, grp), _attn_col_spec(V_BLOCK0, grp),
                  spec, spec, spec],
        out_specs=[spec] * 3,
        out_shape=[jax.ShapeDtypeStruct((SEQ, ATT_OUT), BF16)] * 3, args=(proj, proj, proj, do_g, lse, dl_g),
        scratch_shapes=[pltpu.VMEM((SEQ, HEAD_DIM), F32)] * 3,
        semantics=("parallel",), carried=carried)


HG_HEADS_PER_STEP = 8
HG_BLOCK_W = 4 * HEAD_DIM
HG_BLOCKS = HG_HEADS_PER_STEP * HEAD_DIM // HG_BLOCK_W
HG_STEP_W = HG_HEADS_PER_STEP * HEAD_DIM
HG_Q_BLK = (3 * ATT_WIDTH) // HG_BLOCK_W
HG_N_CHUNKS = SEQ // HG_CHUNK
HG_MID = HG_CHUNK // 2


def _lower_bound(lb_ref, sl):
    l0, l1 = lb_ref[0:1, sl], lb_ref[1:2, sl]
    mx = jnp.maximum(l0, l1)
    e0, e1 = jnp.exp(l0 - mx), jnp.exp(l1 - mx)
    return e0 / (e0 + e1)


def _tri(lower):
    i = lax.broadcasted_iota(jnp.int32, (HG_CHUNK, HG_CHUNK), 0)
    j = lax.broadcasted_iota(jnp.int32, (HG_CHUNK, HG_CHUNK), 1)
    return (i >= j) if lower else (i <= j)


def _head_mean(x):
    parts = []
    for hd in range(x.shape[1] // HEAD_DIM):
        m = jnp.mean(x[:, hd * HEAD_DIM:(hd + 1) * HEAD_DIM], axis=-1, keepdims=True)
        parts.append(jnp.broadcast_to(m, (x.shape[0], HEAD_DIM)))
    return jnp.concatenate(parts, axis=1)


def _hg_chunk_terms(qh, fh, lb):
    sig = _sigmoid(fh)
    f = lb + (1.0 - lb) * sig
    k = 1.0 - f
    b = _dot_exact(_tri(True).astype(F32), jnp.log(f))
    bl = b[HG_CHUNK - 1:HG_CHUNK, :]
    br = b[HG_MID:HG_MID + 1, :]
    sq = _sigmoid(qh)
    q = qh * sq
    return dict(sig=sig, f=f, k=k, b=b, bl=bl, br=br, sq=sq, q=q,
                e1=jnp.exp(bl - b), e2=jnp.exp(b), e3=jnp.exp(b - br), e4=jnp.exp(br - b))


def _hg_fwd(proj, lbw, normw, name, carried=()):
    def in_blks(off):
        return [pl.BlockSpec((HG_CHUNK, HG_BLOCK_W), lambda hp, n, b=b: (n, HG_Q_BLK + off + hp * HG_BLOCKS + b))
                for b in range(HG_BLOCKS)]

    def kern(*refs):
        q_refs, f_refs, i_refs, g_refs = (refs[k * HG_BLOCKS:(k + 1) * HG_BLOCKS] for k in range(4))
        lb_ref, nw_ref, oraw_ref, ohg_ref, st_ref, state = refs[4 * HG_BLOCKS:]

        @pl.when(pl.program_id(1) == 0)
        def _():
            state[...] = jnp.zeros_like(state)

        causal = _tri(True)
        wide = lambda rs: jnp.concatenate([r[...] for r in rs], axis=1)
        t = _hg_chunk_terms(wide(q_refs), wide(f_refs), _lower_bound(lb_ref, slice(None)))
        v, gh = wide(i_refs), wide(g_refs)
        kd, qb, qr, kr = t["k"] * t["e1"], t["q"] * t["e2"], t["q"] * t["e3"], t["k"] * t["e4"]
        decay = jnp.exp(t["bl"])
        outs = []
        for hd in range(HG_HEADS_PER_STEP):
            sl = slice(hd * HEAD_DIM, (hd + 1) * HEAD_DIM)
            st = state[hd]
            st_ref[0, hd] = st
            a = jnp.where(causal, _dot_nt(qr[:, sl], kr[:, sl]), 0.0)
            outs.append(_dot_nt(qb[:, sl], st) + _dot(a, v[:, sl]))
            state[hd] = st * decay[:, sl] + _dot_tn(v[:, sl], kd[:, sl])
        o = jnp.concatenate(outs, axis=1)
        oraw_ref[...] = o
        r = lax.rsqrt(_head_mean(o * o) + RMS_EPS)
        nw = jnp.tile(nw_ref[...], (1, HG_HEADS_PER_STEP))
        ohg_ref[...] = (o * r * nw * (gh * _sigmoid(gh))).astype(ohg_ref.dtype)

    out_blk = pl.BlockSpec((HG_CHUNK, HG_STEP_W), lambda hp, n: (n, hp))
    return _pcall(
        kern, name=name, grid=(HG_HEADS // HG_HEADS_PER_STEP, HG_N_CHUNKS),
        in_specs=[*in_blks(0), *in_blks(2), *in_blks(4), *in_blks(6),
                  pl.BlockSpec((2, HG_STEP_W), lambda hp, n: (0, hp)),
                  pl.BlockSpec((1, HEAD_DIM), lambda hp, n: (0, 0))],
        out_specs=[out_blk, out_blk,
                   pl.BlockSpec((1, HG_HEADS_PER_STEP, HEAD_DIM, HEAD_DIM), lambda hp, n: (n, hp, 0, 0))],
        out_shape=[jax.ShapeDtypeStruct((SEQ, HG_WIDTH), F32), jax.ShapeDtypeStruct((SEQ, HG_WIDTH), BF16),
                   jax.ShapeDtypeStruct((HG_N_CHUNKS, HG_HEADS, HEAD_DIM, HEAD_DIM), F32)],
        args=(*[proj] * (4 * HG_BLOCKS), lbw, normw),
        scratch_shapes=[pltpu.VMEM((HG_HEADS_PER_STEP, HEAD_DIM, HEAD_DIM), F32)],
        semantics=("parallel", "arbitrary"), carried=carried)


def _hg_bwd(proj, lbw, normw, oraw, states, do_hg, name, carried=()):
    last = HG_N_CHUNKS - 1

    def in_blks(off):
        return [pl.BlockSpec((HG_CHUNK, HG_BLOCK_W),
                             lambda hp, n, b=b: (last - n, HG_Q_BLK + off + hp * HG_BLOCKS + b))
                for b in range(HG_BLOCKS)]

    blk = pl.BlockSpec((HG_CHUNK, HG_STEP_W), lambda hp, n: (last - n, hp))

    def kern(*refs):
        q_refs, f_refs, i_refs, g_refs = (refs[k * HG_BLOCKS:(k + 1) * HG_BLOCKS] for k in range(4))
        (lb_ref, nw_ref, oraw_ref, st_ref, do_ref, dq_ref, df_ref, di_ref, dg_ref, dlb_ref, dnw_ref,
         dstate) = refs[4 * HG_BLOCKS:]
        first = pl.program_id(1) == 0

        @pl.when(first)
        def _():
            dstate[...] = jnp.zeros_like(dstate)

        causal = _tri(True)
        wide = lambda rs: jnp.concatenate([r[...] for r in rs], axis=1)
        cat = lambda parts: jnp.concatenate(parts, axis=1)
        qh, fh, v, gh = wide(q_refs), wide(f_refs), wide(i_refs), wide(g_refs)
        o, dout = oraw_ref[...], do_ref[...]
        nw = jnp.tile(nw_ref[...], (1, HG_HEADS_PER_STEP))
        sgg = _sigmoid(gh)
        r = lax.rsqrt(_head_mean(o * o) + RMS_EPS)
        xn = o * r
        dg_ref[...] = (dout * xn * nw * (sgg * (1.0 + gh * (1.0 - sgg)))).astype(dg_ref.dtype)
        don = dout * (gh * sgg)
        dnw_wide = jnp.sum(don * xn, axis=0, keepdims=True)
        dnw_tot = dnw_wide[:, :HEAD_DIM]
        for hd in range(1, HG_HEADS_PER_STEP):
            dnw_tot = dnw_tot + dnw_wide[:, hd * HEAD_DIM:(hd + 1) * HEAD_DIM]
        tt = don * nw
        do = r * (tt - xn * _head_mean(tt * xn))
        lb = _lower_bound(lb_ref, slice(None))
        t = _hg_chunk_terms(qh, fh, lb)
        k, q = t["k"], t["q"]
        kd, qb, qr, kr = k * t["e1"], q * t["e2"], q * t["e3"], k * t["e4"]
        decay = jnp.exp(t["bl"])
        dqb, dqr, dkr, dkd, dv, ddecay = [], [], [], [], [], []
        for hd in range(HG_HEADS_PER_STEP):
            sl = slice(hd * HEAD_DIM, (hd + 1) * HEAD_DIM)
            st = st_ref[0, hd]
            dstn = dstate[hd]
            a = jnp.where(causal, _dot_nt(qr[:, sl], kr[:, sl]), 0.0)
            da = jnp.where(causal, _dot_nt(do[:, sl], v[:, sl]), 0.0)
            dqb.append(_dot(do[:, sl], st))
            dv.append(_dot_tn(a, do[:, sl]) + _dot_nt(kd[:, sl], dstn))
            dqr.append(_dot(da, kr[:, sl]))
            dkr.append(_dot_tn(da, qr[:, sl]))
            dkd.append(_dot(v[:, sl], dstn))
            ddecay.append(jnp.sum(dstn * st, axis=0, keepdims=True))
            dstate[hd] = dstn * decay[:, sl] + _dot_tn(do[:, sl], qb[:, sl])
        dqb, dqr, dkr, dkd, dv, ddecay = cat(dqb), cat(dqr), cat(dkr), cat(dkd), cat(dv), cat(ddecay)
        dq = dqb * t["e2"] + dqr * t["e3"]
        dk = dkd * t["e1"] + dkr * t["e4"]
        db = dqb * qb + dqr * qr - dkr * kr - dkd * kd
        dbl = jnp.sum(dkd * kd, axis=0, keepdims=True) + ddecay * decay
        dbr = jnp.sum(dkr * kr - dqr * qr, axis=0, keepdims=True)
        rows = lax.broadcasted_iota(jnp.int32, db.shape, 0)
        dlf = _dot_exact(_tri(False).astype(F32), db) + dbl + jnp.where(rows <= HG_MID, dbr, 0.0)
        df = dlf / t["f"] - dk
        sig, sq = t["sig"], t["sq"]
        df_ref[...] = (df * (1.0 - lb) * sig * (1.0 - sig)).astype(df_ref.dtype)
        dlb_row = jnp.sum(df * (1.0 - sig), axis=0, keepdims=True)
        dq_ref[...] = (dq * (sq * (1.0 + qh * (1.0 - sq)))).astype(dq_ref.dtype)
        di_ref[...] = dv.astype(di_ref.dtype)
        dnw_blk = jnp.broadcast_to(dnw_tot, (8, HEAD_DIM))

        @pl.when(first)
        def _():
            dlb_ref[...] = dlb_row
            dnw_ref[...] = dnw_blk

        @pl.when(jnp.logical_not(first))
        def _():
            dlb_ref[...] += dlb_row
            dnw_ref[...] += dnw_blk

    n_hp = HG_HEADS // HG_HEADS_PER_STEP
    outs, cres = _pcall(
        kern, name=name, grid=(n_hp, HG_N_CHUNKS),
        in_specs=[*in_blks(0), *in_blks(2), *in_blks(4), *in_blks(6),
                  pl.BlockSpec((2, HG_STEP_W), lambda hp, n: (0, hp)),
                  pl.BlockSpec((1, HEAD_DIM), lambda hp, n: (0, 0)),
                  blk,
                  pl.BlockSpec((1, HG_HEADS_PER_STEP, HEAD_DIM, HEAD_DIM), lambda hp, n: (last - n, hp, 0, 0)),
                  blk],
        out_specs=[blk, blk, blk, blk,
                   pl.BlockSpec((1, HG_STEP_W), lambda hp, n: (0, hp)),
                   pl.BlockSpec((8, HEAD_DIM), lambda hp, n: (hp, 0))],
        out_shape=[jax.ShapeDtypeStruct((SEQ, HG_WIDTH), BF16)] * 4
        + [jax.ShapeDtypeStruct((1, HG_WIDTH), F32), jax.ShapeDtypeStruct((8 * n_hp, HEAD_DIM), F32)],
        args=(*[proj] * (4 * HG_BLOCKS), lbw, normw, oraw, states, do_hg),
        scratch_shapes=[pltpu.VMEM((HG_HEADS_PER_STEP, HEAD_DIM, HEAD_DIM), F32)],
        semantics=("parallel", "arbitrary"), carried=carried)
    dqh, dfh, dih, dgh, dlb, dnw = outs
    return (dqh, dfh, dih, dgh, dlb, [dnw[8 * i:8 * i + 1] for i in range(n_hp)]), cres


GATE_BLOCK_W = 512
GATE_A_BLK = (3 * ATT_WIDTH + 4 * HG_WIDTH) // GATE_BLOCK_W
GATE_B_BLK = GATE_A_BLK + D_MODEL // GATE_BLOCK_W


GATE_TILE = (1024, GATE_BLOCK_W)


def _gate_ins(proj, ya, yb=None):
    ins = [(proj, GATE_TILE, lambda i, j: (i, GATE_A_BLK + j)), (proj, GATE_TILE, lambda i, j: (i, GATE_B_BLK + j)),
           (ya, GATE_TILE)]
    return ins + ([(yb, GATE_TILE)] if yb is not None else [])


def _branch_b_gate(o_hg, w_b, proj, ya, name, carried=()):
    def post(yb, ga, gb, ya_):
        return yb, _sigmoid(ga) * ya_ + _sigmoid(gb) * yb

    return _mm(o_hg, w_b, mode="nn", out_dtype=F32, name=name, carried=carried, fused=dict(
        tile=GATE_TILE, ins=_gate_ins(proj, ya), post=post,
        outs=[(jax.ShapeDtypeStruct((SEQ, D_MODEL), F32), GATE_TILE), (jax.ShapeDtypeStruct((SEQ, D_MODEL), BF16), GATE_TILE)]))


def _dmerged_gate_bwd(dx1, w_out, proj, ya, yb, name, carried=()):
    def post(dm, ga, gb, ya_, yb_):
        sa, sb = _sigmoid(ga), _sigmoid(gb)
        return dm * sa, dm * sb, dm * ya_ * sa * (1.0 - sa), dm * yb_ * sb * (1.0 - sb)

    return _mm(dx1, w_out, mode="nt", out_dtype=BF16, name=name, carried=carried, fused=dict(
        tile=GATE_TILE, ins=_gate_ins(proj, ya, yb), post=post,
        outs=[(jax.ShapeDtypeStruct((SEQ, D_MODEL), BF16), GATE_TILE)] * 4))


NORM_TILE = (512, D_MODEL)


def _column_halves(v, core_row):
    half = v.shape[1] // 2
    south = core_row[:, :half] < 0.5
    lo, hi = v[:, :half], v[:, half:]
    return jnp.where(south, hi, lo), jnp.where(south, lo, hi)


def _residual_norm(a, w, x, g, name, carried=(), core_row=None):
    vec = ((1, D_MODEL), lambda i, j: (0, j))
    half_tile = (NORM_TILE[0], D_MODEL // 2)

    def post(part, x_, g_, *core):
        xn = part + x_
        r = lax.rsqrt(jnp.mean(xn * xn, axis=-1, keepdims=True) + RMS_EPS)
        hn = (xn * r * g_).astype(BF16)
        return (xn, hn) + (_column_halves(hn, core[0]) if core else ())

    wide = [(jax.ShapeDtypeStruct((SEQ, D_MODEL), F32), NORM_TILE), (jax.ShapeDtypeStruct((SEQ, D_MODEL), BF16), NORM_TILE)]
    halves = [(jax.ShapeDtypeStruct((SEQ, D_MODEL // 2), BF16), half_tile)] * 2
    return _mm(a, w, mode="nn", out_dtype=F32, name=name, carried=carried, fused=dict(
        tile=NORM_TILE, post=post, ins=[(x, NORM_TILE), (g, *vec)] + ([(core_row, *vec)] if core_row is not None else []),
        outs=wide + (halves if core_row is not None else [])))


def _norm_bwd_residual(dq, w, x, g, res, name, carried=()):
    def post(dh, x_, g_, res_):
        r = lax.rsqrt(jnp.mean(x_ * x_, axis=-1, keepdims=True) + RMS_EPS)
        xn = x_ * r
        dhg = dh * g_
        dx = r * (dhg - xn * jnp.mean(dhg * xn, axis=-1, keepdims=True)) + res_
        return dx, jnp.sum(dh * xn, axis=0, keepdims=True), dx

    vec = ((1, D_MODEL), lambda i, j: (0, j))
    return _mm(dq, w, mode="nt", out_dtype=F32, name=name, carried=carried, fused=dict(
        tile=NORM_TILE, ins=[(x, NORM_TILE), (g, *vec), (res, NORM_TILE)], post=post,
        outs=[(jax.ShapeDtypeStruct((SEQ, D_MODEL), F32), NORM_TILE), (jax.ShapeDtypeStruct((1, D_MODEL), F32), *vec),
              (jax.ShapeDtypeStruct((SEQ, D_MODEL), BF16), NORM_TILE)]))


FF_SHARD = D_FF // N_CHIPS


FF_TILE_ROWS = 512


def _swiglu_tile(ab):
    a, b = ab[:, :FF_SHARD], ab[:, FF_SHARD:]
    return a * _sigmoid(a) * b


def _swiglu_grad_tile(du, ab):
    a, b = ab[:, :FF_SHARD], ab[:, FF_SHARD:]
    sg = _sigmoid(a)
    return jnp.concatenate([du * b * (sg * (1.0 + a * (1.0 - sg))), du * (a * sg)], axis=1)


def _ff_up(hf, w13, name, carried=()):
    wide, narrow = (FF_TILE_ROWS, 2 * FF_SHARD), (FF_TILE_ROWS, FF_SHARD)
    return _mm(hf, w13, mode="nn", out_dtype=F32, name=name, carried=carried, fused=dict(
        tile=wide, ins=[],
        outs=[(jax.ShapeDtypeStruct((SEQ, 2 * D_FF), F32), wide), (jax.ShapeDtypeStruct((SEQ, D_FF), BF16), narrow)],
        post=lambda p: (p, _swiglu_tile(p))))


def _ff_down_bwd(dx3, w2, ab, name, carried=()):
    wide, narrow = (FF_TILE_ROWS, 2 * FF_SHARD), (FF_TILE_ROWS, FF_SHARD)
    out, res = _mm(dx3, w2, mode="nt", out_dtype=BF16, name=name, carried=carried, fused=dict(
        tile=narrow, ins=[(ab, wide)], outs=[(jax.ShapeDtypeStruct((SEQ, 2 * D_FF), BF16), wide)],
        post=lambda du, ab_: (_swiglu_grad_tile(du, ab_),)))
    return out[0], res


CROSS_ROWS = 512


def _cross_fwd(qc, kvc, name, carried=()):
    def kern(q_ref, k_ref, v_ref, o_ref):
        s = _dot_nt(q_ref[...], k_ref[...]) * ATT_SCALE
        m = jnp.max(s, axis=-1, keepdims=True)
        e = jnp.exp(s - m)
        p = e / jnp.sum(e, axis=-1, keepdims=True)
        o_ref[...] = _dot(p, v_ref[...]).astype(o_ref.dtype)

    qblk = pl.BlockSpec((CROSS_ROWS, HEAD_DIM), lambda h, i: (i, h))
    return _pcall(
        kern, name=name, grid=(CROSS_HEADS, SEQ // CROSS_ROWS),
        in_specs=[qblk, pl.BlockSpec((MEM_LEN, HEAD_DIM), lambda h, i: (0, h)),
                  pl.BlockSpec((MEM_LEN, HEAD_DIM), lambda h, i: (0, CROSS_HEADS + h))],
        out_specs=qblk, out_shape=jax.ShapeDtypeStruct((SEQ, CROSS_WIDTH), BF16), args=(qc, kvc, kvc),
        semantics=("parallel", "parallel"), carried=carried)


def _cross_bwd(qc, kvc, doc, name):
    def kern(q_ref, k_ref, v_ref, do_ref, dq_ref, dk_ref, dv_ref):
        q, k, v, do = q_ref[...], k_ref[...], v_ref[...], do_ref[...]
        s = _dot_nt(q, k) * ATT_SCALE
        m = jnp.max(s, axis=-1, keepdims=True)
        e = jnp.exp(s - m)
        p = e / jnp.sum(e, axis=-1, keepdims=True)
        dp = _dot_nt(do, v)
        ds = p * (dp - jnp.sum(dp * p, axis=-1, keepdims=True))
        dq_ref[...] = (_dot(ds, k) * ATT_SCALE).astype(dq_ref.dtype)
        dk = _dot_tn(ds, q) * ATT_SCALE
        dv = _dot_tn(p, do)

        @pl.when(pl.program_id(1) == 0)
        def _():
            dk_ref[...] = dk
            dv_ref[...] = dv

        @pl.when(pl.program_id(1) > 0)
        def _():
            dk_ref[...] += dk
            dv_ref[...] += dv

    qblk = pl.BlockSpec((CROSS_ROWS, HEAD_DIM), lambda h, i: (i, h))
    kblk = pl.BlockSpec((MEM_LEN, HEAD_DIM), lambda h, i: (0, h))
    dq, dk, dv = pl.pallas_call(
        kern, name=name, grid=(CROSS_HEADS, SEQ // CROSS_ROWS),
        in_specs=[qblk, kblk, pl.BlockSpec((MEM_LEN, HEAD_DIM), lambda h, i: (0, CROSS_HEADS + h)), qblk],
        out_specs=[qblk, kblk, kblk],
        out_shape=[jax.ShapeDtypeStruct((SEQ, CROSS_WIDTH), BF16),
                   jax.ShapeDtypeStruct((MEM_LEN, CROSS_WIDTH), F32),
                   jax.ShapeDtypeStruct((MEM_LEN, CROSS_WIDTH), F32)],
        compiler_params=_cparams(("parallel", "arbitrary")),
    )(qc, kvc, kvc, doc)
    return dq, jnp.concatenate([dk, dv], axis=1)


FULL_SPECS = {
    "w_in": ("col", D_MODEL, IN_WIDTH),
    "w_branch_a": ("col", ATT_OUT, D_MODEL),
    "w_branch_b": ("col", HG_WIDTH, D_MODEL),
    "w_out": ("row", D_MODEL, D_MODEL),
    "wq_cross": ("row", D_MODEL, CROSS_WIDTH),
    "wkv_cross": ("row", D_MODEL, 2 * CROSS_WIDTH),
    "wo_cross": ("col", CROSS_WIDTH, D_MODEL),
    "w13": ("col", D_MODEL, 2 * D_FF),
    "w2": ("row", D_FF, D_MODEL),
}
WEIGHT_PLACE = {
    "w_in": ("w_in", 0), "w_branch_a": ("w_branch_a", 0), "w_branch_b": ("w_branch_b", 0),
    "w_out": ("w_out", 0), "wq_cross": ("wq_cross", 0), "wkv_cross": ("wkv_cross", 0),
    "wo_cross": ("wo_cross", 0), "w1": ("w13", 0), "w3": ("w13", FF_SHARD), "w2": ("w2", 0),
}
BIG_WEIGHTS = tuple(WEIGHT_PLACE)
EW_BLOCK_ELEMS = 512 * 1024


def _position():
    return lax.axis_index("x"), lax.axis_index("y"), lax.axis_index("c")


def _other_chips(x, y):
    return [(1 - x, y), (x, 1 - y), (1 - x, 1 - y)]


def _half(ref, kind, h):
    r, c = ref.shape
    if kind == "col":
        return ref.at[pl.ds(h * (r // 2), r // 2), :]
    return ref.at[:, pl.ds(h * (c // 2), c // 2)]


def _shard_of(ref, kind, start, size):
    return ref.at[:, pl.ds(start, size)] if kind == "col" else ref.at[pl.ds(start, size), :]


def _rows_of(ref, r0, nrows):
    return ref if nrows is None else ref.at[pl.ds(r0, nrows), :]


def _half_shape(kind, rows, cols):
    return (rows // 2, cols) if kind == "col" else (rows, cols // 2)


def _slot_shape(spec):
    kind, rows, cols = spec
    hr, hc = _half_shape(kind, rows, cols)
    return (hr, hc // N_CHIPS) if kind == "col" else (hr // N_CHIPS, hc)


def _remote(src, dst, send_sem, recv_sem, device):
    return pltpu.make_async_remote_copy(src_ref=src, dst_ref=dst, send_sem=send_sem, recv_sem=recv_sem,
                                        device_id=device, device_id_type=MESH)


def _gather_ici_comm(fulls, jobs, specs):
    def piece(refs, job, chip, c):
        f, r0, nr = job
        kind, rows, cols = specs[f]
        stride = (cols if kind == "col" else rows) // N_CHIPS
        return _rows_of(_half(_shard_of(refs[f], kind, chip * stride, stride), kind, c), r0, nr)

    def start(refs, ss, rs):
        x, y, c = _position()
        j = 2 * x + y
        for q, job in enumerate(jobs):
            for p, (px, py) in enumerate(_other_chips(x, y)):
                _remote(piece(refs, job, j, c), piece(refs, job, j, c), ss.at[3 * q + p], rs.at[3 * q + p],
                        (px, py, c)).start()

    def finish(refs, ss, rs):
        x, y, c = _position()
        j = 2 * x + y
        for q, job in enumerate(jobs):
            for p, (px, py) in enumerate(_other_chips(x, y)):
                _remote(piece(refs, job, j, c), piece(refs, job, 2 * px + py, c), ss.at[3 * q + p],
                        rs.at[3 * q + p], (px, py, c)).wait_recv()
        for q, job in enumerate(jobs):
            for p, (px, py) in enumerate(_other_chips(x, y)):
                _remote(piece(refs, job, j, c), piece(refs, job, j, c), ss.at[3 * q + p], rs.at[3 * q + p],
                        (px, py, c)).wait_send()

    names = list(dict.fromkeys(job[0] for job in jobs))
    return _Carried({f: fulls[f] for f in names}, {}, 3 * len(jobs), start, finish)


def _gather_ring_comm(fulls, f, r0, nr, phase, specs):
    kind, _, cols = specs[f]
    assert kind == "col" and nr % 32 == 0
    stride = cols // N_CHIPS
    half = nr // 2

    def rows(refs, chip, c, lo, n):
        return _rows_of(_half(_shard_of(refs[f], kind, chip * stride, stride), kind, c), r0 + lo, n)

    def copies(refs, ss, rs):
        x, y, c = _position()
        me, nx, ny, dg = 2 * x + y, 2 * (1 - x) + y, 2 * x + (1 - y), 2 * (1 - x) + (1 - y)
        to_x, to_y = (1 - x, y, c), (x, 1 - y, c)
        if phase == "a":
            mine = rows(refs, me, c, 0, nr)
            return [(_remote(mine, mine, ss.at[0], rs.at[0], to_x), rows(refs, nx, c, 0, nr)),
                    (_remote(mine, mine, ss.at[1], rs.at[1], to_y), rows(refs, ny, c, 0, nr))]
        up, low = rows(refs, ny, c, half, half), rows(refs, nx, c, 0, half)
        return [(_remote(up, up, ss.at[0], rs.at[0], to_x), rows(refs, dg, c, half, half)),
                (_remote(low, low, ss.at[1], rs.at[1], to_y), rows(refs, dg, c, 0, half))]

    def start(refs, ss, rs):
        for cp, _ in copies(refs, ss, rs):
            cp.start()

    def finish(refs, ss, rs):
        x, y, c = _position()
        mine = copies(refs, ss, rs)
        for i, (_, landing) in enumerate(mine):
            _remote(landing, landing, ss.at[i], rs.at[i], (x, y, c)).wait_recv()
        for cp, _ in mine:
            cp.wait_send()

    return _Carried({f: fulls[f]}, {}, 2, start, finish)


def _gather_d2d_comm(fulls, jobs, specs):
    def rect(refs, job, h):
        f, r0, nr = job
        assert nr is None or specs[f][0] == "col"
        return _rows_of(_half(refs[f], specs[f][0], h), r0, nr)

    def start(refs, ss, rs):
        x, y, c = _position()
        for q, job in enumerate(jobs):
            _remote(rect(refs, job, c), rect(refs, job, c), ss.at[q], rs.at[q], (x, y, 1 - c)).start()

    def finish(refs, ss, rs):
        x, y, c = _position()
        for q, job in enumerate(jobs):
            _remote(rect(refs, job, 1 - c), rect(refs, job, 1 - c), ss.at[q], rs.at[q], (x, y, 1 - c)).wait_recv()
        for q, job in enumerate(jobs):
            _remote(rect(refs, job, c), rect(refs, job, c), ss.at[q], rs.at[q], (x, y, 1 - c)).wait_send()

    names = list(dict.fromkeys(job[0] for job in jobs))
    return _Carried({f: fulls[f] for f in names}, {}, len(jobs), start, finish)


def _pairx_comm(grads, names, specs, whole=False, recv=None):
    def copies(refs, ss, rs):
        x, y, c = _position()
        src = (lambda f: refs[("g", f)]) if whole else (lambda f: _half(refs[("g", f)], specs[f][0], 1 - c))
        return [_remote(src(f), refs[("r", f)], ss.at[i], rs.at[i], (x, y, 1 - c)) for i, f in enumerate(names)]

    def start(refs, ss, rs):
        for cp in copies(refs, ss, rs):
            cp.start()

    def finish(refs, ss, rs):
        for cp in copies(refs, ss, rs):
            cp.wait_recv()
        for cp in copies(refs, ss, rs):
            cp.wait_send()

    reads = {("g", f): grads[f] for f in names}
    if recv is not None:
        return _Carried({("r", f): recv[f] for f in names}, {}, len(names), start, finish, reads=reads)
    fresh = {("r", f): jax.ShapeDtypeStruct(_half_shape(*specs[f]), BF16) for f in names}
    return _Carried({}, fresh, len(names), start, finish, reads=reads)


def _chipx_comm(pair_sums, slots, jobs, specs):
    def copies(refs, ss, rs):
        x, y, c = _position()
        out = []
        for q, (f, r0, nr) in enumerate(jobs):
            kind = specs[f][0]
            width = _slot_shape(specs[f])[1 if kind == "col" else 0]
            for p, (px, py) in enumerate(_other_chips(x, y)):
                src = _rows_of(_shard_of(refs[("p", f)], kind, (2 * px + py) * width, width), r0, nr)
                dst = _rows_of(refs[("s", f)].at[p], r0, nr)
                out.append(_remote(src, dst, ss.at[3 * q + p], rs.at[3 * q + p], (px, py, c)))
        return out

    def start(refs, ss, rs):
        for cp in copies(refs, ss, rs):
            cp.start()

    def finish(refs, ss, rs):
        for cp in copies(refs, ss, rs):
            cp.wait_recv()
        for cp in copies(refs, ss, rs):
            cp.wait_send()

    names = list(dict.fromkeys(job[0] for job in jobs))
    arrays = {("p", f): pair_sums[f] for f in names}
    arrays.update({("s", f): slots[f] for f in names})
    return _Carried(arrays, {}, 3 * len(jobs), start, finish)


def _share_comm(grads, wnames, specs, place):
    def start(refs, ss, rs):
        x, y, c = _position()
        for i, w in enumerate(wnames):
            kind = specs[place[w][0]][0]
            _remote(_half(refs[w], kind, c), _half(refs[w], kind, c), ss.at[i], rs.at[i], (x, y, 1 - c)).start()

    def finish(refs, ss, rs):
        x, y, c = _position()
        for i, w in enumerate(wnames):
            kind = specs[place[w][0]][0]
            _remote(_half(refs[w], kind, 1 - c), _half(refs[w], kind, 1 - c), ss.at[i], rs.at[i],
                    (x, y, 1 - c)).wait_recv()
        for i, w in enumerate(wnames):
            kind = specs[place[w][0]][0]
            _remote(_half(refs[w], kind, c), _half(refs[w], kind, c), ss.at[i], rs.at[i], (x, y, 1 - c)).wait_send()

    return _Carried({w: grads[w] for w in wnames}, {}, len(wnames), start, finish)


SMALL_IN, SMALL_ROWS_OF_ALL = ("small", "mine"), ("small", "all")


def _small_gather_comm(v):
    flips = [(fx, fy, fc) for fx in (0, 1) for fy in (0, 1) for fc in (0, 1)][1:]

    def copies(refs, ss, rs):
        x, y, c = _position()
        v_ref, out_ref = refs[SMALL_IN], refs[SMALL_ROWS_OF_ALL]
        me = 4 * x + 2 * y + c
        pairs = []
        for i, fl in enumerate(flips):
            px, py, pc = (1 - a if f else a for a, f in zip((x, y, c), fl))
            pairs.append((_remote(v_ref, out_ref.at[me], ss.at[i], rs.at[i], (px, py, pc)),
                          _remote(v_ref, out_ref.at[4 * px + 2 * py + pc], ss.at[i], rs.at[i], (px, py, pc))))
        return pairs, pltpu.make_async_copy(v_ref, out_ref.at[me], ss.at[N_DEV - 1])

    def start(refs, ss, rs):
        pairs, local = copies(refs, ss, rs)
        local.start()
        for mine, _ in pairs:
            mine.start()

    def finish(refs, ss, rs):
        pairs, local = copies(refs, ss, rs)
        for _, theirs in pairs:
            theirs.wait_recv()
        for mine, _ in pairs:
            mine.wait_send()
        local.wait()

    fresh = {SMALL_ROWS_OF_ALL: jax.ShapeDtypeStruct((N_DEV,) + v.shape, F32)}
    return _Carried({}, fresh, N_DEV, start, finish, reads={SMALL_IN: v})


def _ew_block(rows, cols, elems=EW_BLOCK_ELEMS):
    tc = cols if cols <= 4096 else _div(cols, 2048, LANE)
    tr = _div(rows, max(16, elems // tc), 16)
    return tr, tc


def _mesh_scalars():
    x, y, c = _position()
    return jnp.stack([c, 2 * x + y]).astype(jnp.int32)


def _grid_spec(grid, in_specs, out_specs):
    return pltpu.PrefetchScalarGridSpec(num_scalar_prefetch=1, grid=grid, in_specs=in_specs, out_specs=out_specs)


def _cast_into_full(parts, fname, pos, specs, place, name, token=None):
    kind, rows, cols = specs[fname]
    ws = [w for w in place if place[w][0] == fname]
    if kind == "col":
        stride = cols // N_CHIPS
        hr = rows // 2
        tr = _div(hr, max(16, EW_BLOCK_ELEMS // stride), 16)
        nrb = hr // tr
        in_specs = [pl.BlockSpec((tr, parts[w].shape[1]), lambda i, pos_ref: (i + pos_ref[0] * nrb, 0)) for w in ws]
        out_spec = pl.BlockSpec((tr, stride), lambda i, pos_ref: (i + pos_ref[0] * nrb, pos_ref[1]))
    else:
        stride = rows // N_CHIPS
        hc = cols // 2
        tr = _div(stride, max(16, EW_BLOCK_ELEMS // hc), 16)
        nrb = stride // tr
        in_specs = [pl.BlockSpec((tr, hc), lambda i, pos_ref: (i, pos_ref[0])) for w in ws]
        out_spec = pl.BlockSpec((tr, hc), lambda i, pos_ref: (i + pos_ref[1] * nrb, pos_ref[0]))

    def kern(pos_ref, *refs):
        o_ref = refs[-1]
        for w, r in zip(ws, refs[:len(ws)]):
            off = place[w][1] if kind == "col" else 0
            o_ref[:, off:off + r.shape[1]] = r[...].astype(o_ref.dtype)

    tokens = [] if token is None else [token]
    in_specs = in_specs + [pl.BlockSpec(TOKEN_SHAPE, lambda i, pos_ref: (0, 0))] * len(tokens)
    return pl.pallas_call(
        kern, name=name, grid_spec=_grid_spec((nrb,), in_specs, out_spec),
        out_shape=jax.ShapeDtypeStruct((rows, cols), BF16),
        compiler_params=_cparams(("parallel",)),
    )(pos, *[parts[w] for w in ws], *tokens)


def _pair_sum(grad, recv, pos, spec, name, whole=False):
    kind, rows, cols = spec
    hr, hc = _half_shape(kind, rows, cols)
    tr, tc = _ew_block(hr, hc, 2 * EW_BLOCK_ELEMS)
    nrb, ncb = hr // tr, hc // tc
    blk = pl.BlockSpec((tr, tc), lambda i, jj, pos_ref: (i, jj))
    if whole:
        mine = blk
    elif kind == "col":
        mine = pl.BlockSpec((tr, tc), lambda i, jj, pos_ref: (i + pos_ref[0] * nrb, jj))
    else:
        mine = pl.BlockSpec((tr, tc), lambda i, jj, pos_ref: (i, jj + pos_ref[0] * ncb))

    def kern(pos_ref, g_ref, r_ref, o_ref, slots_ref):
        o_ref[...] = (g_ref[...].astype(F32) + r_ref[...].astype(F32)).astype(o_ref.dtype)

    return pl.pallas_call(
        kern, name=name, grid_spec=_grid_spec((nrb, ncb), [mine, blk], [blk, ANY]),
        out_shape=[jax.ShapeDtypeStruct((hr, hc), BF16),
                   jax.ShapeDtypeStruct((N_CHIPS - 1,) + _slot_shape(spec), BF16)],
        compiler_params=_cparams(("parallel", "parallel")),
    )(pos, grad, recv)


def _chip_sum(pair_sum, slots, pos, fname, shard_shapes, specs, place, name):
    kind, rows, cols = specs[fname]
    sr, sc = _slot_shape(specs[fname])
    ws = [w for w in place if place[w][0] == fname]
    n_slots = N_CHIPS - 1
    tr = _div(sr, max(16, EW_BLOCK_ELEMS // sc), 16)
    nrb = sr // tr
    slot = pl.BlockSpec((n_slots, tr, sc), lambda i, pos_ref: (0, i, 0))
    if kind == "col":
        own = pl.BlockSpec((tr, sc), lambda i, pos_ref: (i, pos_ref[1]))
        out_specs = [pl.BlockSpec((tr, shard_shapes[w][1]), lambda i, pos_ref: (i + pos_ref[0] * nrb, 0)) for w in ws]
    else:
        own = pl.BlockSpec((tr, sc), lambda i, pos_ref: (i + pos_ref[1] * nrb, 0))
        out_specs = [pl.BlockSpec((tr, sc), lambda i, pos_ref: (i, pos_ref[0])) for w in ws]

    def kern(pos_ref, own_ref, slot_ref, *out_refs):
        tot = own_ref[...].astype(F32)
        for s in range(n_slots):
            tot = tot + slot_ref[s].astype(F32)
        for w, o_ref in zip(ws, out_refs):
            off = place[w][1] if kind == "col" else 0
            o_ref[...] = tot[:, off:off + o_ref.shape[1]]

    outs = pl.pallas_call(
        kern, name=name, grid_spec=_grid_spec((nrb,), [own, slot], out_specs),
        out_shape=[jax.ShapeDtypeStruct(shard_shapes[w], F32) for w in ws],
        compiler_params=_cparams(("parallel",)),
    )(pos, pair_sum, slots)
    return dict(zip(ws, outs))


def _adam_math(w, g, m, v):
    m2 = ADAM_B1 * m + (1.0 - ADAM_B1) * g
    v2 = ADAM_B2 * v + (1.0 - ADAM_B2) * (g * g)
    m_hat = m2 / (1.0 - ADAM_B1 ** ADAM_STEP)
    v_hat = v2 / (1.0 - ADAM_B2 ** ADAM_STEP)
    delta = -ADAM_LR * (m_hat / (jnp.sqrt(v_hat) + ADAM_EPS) + ADAM_WD * w)
    return delta, m2, v2


def _adamw(w, g, m, v, name, carried=()):
    rows, cols = w.shape
    tr, tc = _ew_block(rows, cols)

    def kern(w_ref, g_ref, m_ref, v_ref, d_ref, m2_ref, v2_ref, g_out_ref):
        g_ = g_ref[...]
        d_ref[...], m2_ref[...], v2_ref[...] = _adam_math(w_ref[...], g_, m_ref[...], v_ref[...])
        g_out_ref[...] = g_

    blk = pl.BlockSpec((tr, tc), lambda i, j: (i, j))
    return _pcall(
        kern, name=name, grid=(rows // tr, cols // tc), in_specs=[blk] * 4, out_specs=[blk] * 4,
        out_shape=[jax.ShapeDtypeStruct((rows, cols), F32)] * 4, args=(w, g, m, v),
        semantics=("parallel", "parallel"), carried=carried)


SMALL_ROWS = ("ln_mix_w", "ln_cross_w", "ln_mem_w", "ln_ffn_w", "ln_final_w")
ROW_HG_NORM, ROW_LB0, ROW_LB1 = 5, 6, 7
LOSS_LANE0 = HEAD_DIM


def _pack_small(vals):
    rows = [vals[n].reshape(1, D_MODEL) for n in SMALL_ROWS]
    pad = lambda a: jnp.pad(a, ((0, 0), (0, D_MODEL - a.shape[1])))
    rows.append(pad(vals["hg_norm_w"].reshape(1, HEAD_DIM)))
    rows.append(pad(vals["hg_lower_bounds"].reshape(2, HG_WIDTH)))
    return jnp.concatenate(rows, axis=0)


def _small_update(gathered, w, m, v, name="small_update"):
    def kern(g_ref, w_ref, m_ref, v_ref, grad_ref, d_ref, m2_ref, v2_ref, loss_ref):
        tot = g_ref[0]
        for s in range(1, N_DEV):
            tot = tot + g_ref[s]
        wv = w_ref[...]
        row = lax.broadcasted_iota(jnp.int32, (8, D_MODEL), 0)
        lane = lax.broadcasted_iota(jnp.int32, (8, D_MODEL), 1)
        l0, l1 = wv[ROW_LB0:ROW_LB0 + 1], wv[ROW_LB1:ROW_LB1 + 1]
        mx = jnp.maximum(l0, l1)
        e0, e1 = jnp.exp(l0 - mx), jnp.exp(l1 - mx)
        p0 = e0 / (e0 + e1)
        dlog = tot[ROW_LB0:ROW_LB0 + 1] * p0 * (1.0 - p0)
        tot = jnp.where(row == ROW_HG_NORM, tot + tot[ROW_LB1:ROW_LB1 + 1], tot)
        grad = jnp.where(row == ROW_LB0, dlog, jnp.where(row == ROW_LB1, -dlog, tot))
        grad = jnp.where((row == ROW_HG_NORM) & (lane >= HEAD_DIM), 0.0, grad)
        grad = jnp.where((row >= ROW_LB0) & (lane >= HG_WIDTH), 0.0, grad)
        grad_ref[...] = grad
        d_ref[...], m2_ref[...], v2_ref[...] = _adam_math(wv, grad, m_ref[...], v_ref[...])
        loss_ref[...] = tot[ROW_HG_NORM:ROW_HG_NORM + 1, LOSS_LANE0:LOSS_LANE0 + LANE]

    full = pl.BlockSpec((8, D_MODEL), lambda: (0, 0))
    return pl.pallas_call(
        kern, name=name,
        in_specs=[pl.BlockSpec((N_DEV, 8, D_MODEL), lambda: (0, 0, 0)), full, full, full],
        out_specs=[full, full, full, full, pl.BlockSpec((1, LANE), lambda: (0, 0))],
        out_shape=[jax.ShapeDtypeStruct((8, D_MODEL), F32)] * 4 + [jax.ShapeDtypeStruct((1, LANE), F32)],
        compiler_params=_cparams(),
    )(gathered, w, m, v)


def _unpack_small(p, shapes):
    out = {n: p[i].reshape(shapes[n]) for i, n in enumerate(SMALL_ROWS)}
    out["hg_norm_w"] = p[ROW_HG_NORM, :HEAD_DIM].reshape(shapes["hg_norm_w"])
    out["hg_lower_bounds"] = p[ROW_LB0:ROW_LB1 + 1, :HG_WIDTH].reshape(shapes["hg_lower_bounds"])
    return out


def _concat_cols(pieces, name):
    rows = pieces[0].shape[0]
    widths = [p.shape[1] for p in pieces]
    tr = ROW_BLOCK // 2

    def kern(*refs):
        o_ref, off = refs[-1], 0
        for r, w in zip(refs[:-1], widths):
            o_ref[:, off:off + w] = r[...]
            off += w

    return pl.pallas_call(
        kern, name=name, grid=(rows // tr,),
        in_specs=[pl.BlockSpec((tr, w), lambda i: (i, 0)) for w in widths],
        out_specs=pl.BlockSpec((tr, sum(widths)), lambda i: (i, 0)),
        out_shape=jax.ShapeDtypeStruct((rows, sum(widths)), pieces[0].dtype),
        compiler_params=_cparams(("parallel",)),
    )(*pieces)


WHOLE = lambda f: (f, 0, None)
MID_MATRICES = ("w_branch_a", "w_branch_b", "w_out", "wq_cross", "wkv_cross", "wo_cross")
MID_WEIGHTS = MID_MATRICES
W_IN_PIECES = [("w_in", r0, 512) for r0 in range(0, D_MODEL // 2, 512)]
W13_PIECES = [("w13", r0, 512) for r0 in range(0, D_MODEL // 2, 512)]
GATHER_GROUPS = [("mid", [WHOLE(f) for f in MID_MATRICES]), ("w13a", W13_PIECES[:1]), ("w13b", W13_PIECES[1:]),
                 ("w2", [WHOLE("w2")])]
OTHER_WEIGHTS = ["w1", "w3", "w2"] + list(MID_WEIGHTS)
BEFORE = {
    "hgrn_fwd": [("wait", "mid")],
    "mm_out": [("wait", "w13a")],
    "mm_o": [("wait", "w13b")],
    "mm_w2": [("wait", "w2"), ("run", ("d2d", [WHOLE("w2")]), "gather_hand_over_w2")],
    "mm_dwin_own": [("wait", "rs_w2"), ("chip_sum", "w2"), ("wait", "rs_w13"), ("chip_sum", "w13"),
                    ("wait", "pair_w_in")],
    "mm_dh": [("wait", "rs_mid")] + [("chip_sum", f) for f in MID_MATRICES],
}
CARRY = {
    "hgrn_fwd": [("d2d", [WHOLE(f) for f in MID_MATRICES])],
    "mm_out": [("d2d", W13_PIECES[:1])],
    "mm_o": [("d2d", W13_PIECES[1:])],
    "mm_du": [("pairx_whole", ["w2"])],
    "mm_dhf": [("pairx_whole", ["w13"])],
    "attn_bwd_g0": [("pairx", list(MID_MATRICES))],
    "mm_dh": [("share", OTHER_WEIGHTS)],
}
AFTER = {
    "mm_dw2_own": [("start", "rs_w2", [WHOLE("w2")])],
    "mm_dw13_own": [("start", "rs_w13", [WHOLE("w13")])],
    "attn_bwd_g0": [("pair_sum", f) for f in MID_MATRICES] + [("start", "rs_mid", [WHOLE(f) for f in MID_MATRICES])],
    "mm_dwin_sibling": [("start_pairx", "pair_w_in", ["w_in"])],
    "mm_dwin_own": [("start", "rs_w_in", [WHOLE("w_in")])],
}
FINISH = [
    ("adamw", OTHER_WEIGHTS), ("wait", "rs_w_in"), ("chip_sum", "w_in"), ("share_and_small", ["w_in"]),
    ("adamw", ["w_in"]),
]


class _Late:
    def __init__(self, read):
        self.read = read


class _Net:
    def __init__(self, full, pos=None, shard_shapes=None, comm=True, specs=FULL_SPECS, place=WEIGHT_PLACE):
        self.full, self.pos, self.shard_shapes, self.comm = dict(full), pos, shard_shapes, comm
        self.specs, self.place = specs, place
        self.gw, self.recv, self.psum, self.slots, self.grads = {}, {}, {}, {}, {}
        self.gw_sibling = {}
        self.pending, self.token, self.last, self.sums = {}, None, None, []

    def _make(self, kind, arg):
        if kind == "gather":
            return _gather_ici_comm(self.full, arg, self.specs)
        if kind == "ring":
            return _gather_ring_comm(self.full, *arg, self.specs)
        if kind == "d2d":
            return _gather_d2d_comm(self.full, arg, self.specs)
        if kind == "pairx":
            return _pairx_comm(self.gw, arg, self.specs)
        if kind == "pairx_whole":
            return _pairx_comm(self.gw_sibling, arg, self.specs, whole=True)
        if kind == "pairx_split":
            return _pairx_comm(self.gw_sibling, arg, self.specs, whole=True, recv=self.recv)
        if kind == "chipx":
            return _chipx_comm(self.psum, self.slots, arg, self.specs)
        assert kind == "share"
        return _share_comm(self.grads, arg, self.specs, self.place)

    def _store(self, kind, res):
        if kind in ("gather", "ring", "d2d"):
            self.full.update(res)
        elif kind in ("pairx", "pairx_whole", "pairx_split"):
            for (tag, f), a in res.items():
                (self.gw if tag == "g" else self.recv)[f] = a
        elif kind == "chipx":
            for (tag, f), a in res.items():
                (self.psum if tag == "p" else self.slots)[f] = a
        else:
            self.grads.update(res)

    def run_comm(self, item, name, extra=()):
        kind, arg = item
        res = _run_comm([self._make(kind, arg), *extra], name)
        self._store(kind, res[0])
        return res[1:]

    @staticmethod
    def _others(after, items):
        own = [a for cm in items for a in cm.arrays.values()]
        return [a for a in after if a is not None and all(a is not o for o in own)]

    def start(self, groups, kind, name):
        items = [self._make(kind, jobs) for _, jobs in groups]
        after = self._others([self.last], items)
        res, sems, token = _split_start(items, name, after=after[0] if after else None)
        for (group, jobs), r, s in zip(groups, res, sems):
            self._store(kind, r)
            self.pending[group] = (kind, jobs, s)
        self.token = self.last = token

    def wait(self, group, after=()):
        kind, jobs, sems = self.pending.pop(group)
        item = self._make(kind, jobs)
        res = _split_wait([item], [sems], self._others([self.last, *after], [item]), f"wait_{group}")[0]
        self._store(kind, res)

    def step(self, step):
        if step[0] == "wait":
            self.wait(step[1], after=self.sums)
            self.sums = []
        elif step[0] == "start":
            self.start([(step[1], step[2])], "chipx", f"start_{step[1]}")
        elif step[0] == "start_pairx":
            for f in step[2]:
                self.recv[f] = lax.empty(_half_shape(*self.specs[f]), BF16)
            self.start([(step[1], step[2])], "pairx_split", f"start_{step[1]}")
        elif step[0] == "pair_sum":
            f = step[1]
            self.psum[f], self.slots[f] = _pair_sum(self.gw[f], self.recv[f], self.pos, self.specs[f],
                                                    f"rs_pair_sum_{f}", whole=f in self.gw_sibling)
        elif step[0] == "chip_sum":
            f = step[1]
            sums = _chip_sum(self.psum[f], self.slots[f], self.pos, f, self.shard_shapes,
                             self.specs, self.place, f"rs_chip_sum_{f}")
            self.grads.update(sums)
            self.sums += list(sums.values())
        else:
            assert step[0] == "run"
            self.run_comm(step[1], step[2])

    def call(self, fn, name, *args, grad_of=None, sibling_half=False, pair_sum_of=None, **kw):
        for step in (BEFORE.get(name, []) if self.comm else []):
            self.step(step)
        self.sums = []
        late = lambda a: a.read() if isinstance(a, _Late) else a
        args = [late(a) for a in args]
        kw = {key: late(val) for key, val in kw.items()}
        items = CARRY.get(name, []) if self.comm else []
        carried = [self._make(k, a) for k, a in items]
        if self.token is not None:
            carried.append(_Token(self.token))
            self.token = None
        out, res = fn(*args, name=name, carried=carried, **kw)
        if grad_of is not None:
            (self.gw_sibling if sibling_half else self.gw)[grad_of] = out
        if pair_sum_of is not None:
            self.psum[pair_sum_of] = out
            self.slots[pair_sum_of] = lax.empty((N_CHIPS - 1,) + _slot_shape(self.specs[pair_sum_of]), BF16)
        self.last = jax.tree.leaves(out)[0]
        for (kind, _), r in zip(items, res):
            self._store(kind, r)
        for step in (AFTER.get(name, []) if self.comm else []):
            self.step(step)
        return out


def _local_step(net, x, h, mem, target, small, h_halves=None, core_row=None):
    full, call = net.full, net.call
    proj = call(_mm, "mm_proj", h, full["w_in"], mode="nn", out_dtype=F32)
    att = [call(_attn_fwd, f"attn_fwd_g{g}", proj, g) for g in range(3)]
    outs, lses = [a[0] for a in att], [a[1] for a in att]
    o_att = _attn_merge_fwd(outs, lses, "attn_merge")
    oraw, o_hg, states = call(_hg_fwd, "hgrn_fwd", proj, small["hg_lower_bounds"], small["hg_norm_w"])
    ya = call(_mm, "mm_branch_a", o_att, full["w_branch_a"], mode="nn", out_dtype=F32)
    yb, merged = call(_branch_b_gate, "mm_branch_b", o_hg, full["w_branch_b"], proj, ya)
    x1, hc = call(_residual_norm, "mm_out", merged, full["w_out"], x, small["ln_cross_w"])

    mn = _rms_fwd(mem, small["ln_mem_w"], "rms_mem")
    qc = call(_mm, "mm_q", hc, full["wq_cross"], mode="nn", out_dtype=F32)
    kvc = call(_mm, "mm_kv", mn, full["wkv_cross"], mode="nn", out_dtype=F32)
    oc = call(_cross_fwd, "cross_fwd", qc, kvc)
    x2, hf, *hf_halves = call(_residual_norm, "mm_o", oc, full["wo_cross"], x1, small["ln_ffn_w"], core_row=core_row)

    ab, u = call(_ff_up, "mm_w13", hf, full["w13"])
    x3 = call(_mm, "mm_w2", u, _Late(lambda: full["w2"]), mode="nn", out_dtype=F32, res=x2)

    dx3, dg_final, loss, dx3_bf16, *dx3_halves = _loss_head(x3, small["ln_final_w"], target, "loss_head", core_row)

    gs = {"ln_final_w": dg_final}
    if net.comm:
        call(_mm, "mm_dw2_sibling", u, dx3_halves[0], mode="tn", out_dtype=BF16, grad_of="w2", sibling_half=True)
        dab = call(_ff_down_bwd, "mm_du", dx3_bf16, full["w2"], ab)
        call(_mm, "mm_dw2_own", u, dx3_halves[1], mode="tn", out_dtype=BF16, res=_Late(lambda: net.recv["w2"]),
             pair_sum_of="w2")
        call(_mm, "mm_dw13_sibling", hf_halves[0], dab, mode="tn", out_dtype=BF16, grad_of="w13", sibling_half=True)
        dhf = call(_mm, "mm_dhf", dab, full["w13"], mode="nt", out_dtype=F32)
        call(_mm, "mm_dw13_own", hf_halves[1], dab, mode="tn", out_dtype=BF16, res=_Late(lambda: net.recv["w13"]),
             pair_sum_of="w13")
    else:
        call(_mm, "mm_dw2", u, dx3_bf16, mode="tn", out_dtype=BF16, grad_of="w2")
        dab = call(_ff_down_bwd, "mm_du", dx3_bf16, full["w2"], ab)
        call(_mm, "mm_dw13", hf, dab, mode="tn", out_dtype=BF16, grad_of="w13")
        dhf = call(_mm, "mm_dhf", dab, full["w13"], mode="nt", out_dtype=F32)
    dx2, gs["ln_ffn_w"], dx2_bf16 = _rms_bwd(x2, small["ln_ffn_w"], dhf, dx3, "rms_ffn_bwd", bf16_copy=True)
    doc = call(_mm, "mm_doc", dx2_bf16, full["wo_cross"], mode="nt", out_dtype=BF16)
    call(_mm, "mm_dwo", oc, dx2_bf16, mode="tn", out_dtype=BF16, grad_of="wo_cross")
    dqc, dkvc = _cross_bwd(qc, kvc, doc, "cross_bwd")
    call(_mm, "mm_dwq", hc, dqc, mode="tn", out_dtype=BF16, grad_of="wq_cross")
    dx1, gs["ln_cross_w"], dx1_bf16 = call(_norm_bwd_residual, "mm_dhc", dqc, full["wq_cross"], x1,
                                           small["ln_cross_w"], dx2)
    call(_mm, "mm_dwkv", mn, dkvc, mode="tn", out_dtype=BF16, grad_of="wkv_cross")
    dmn = call(_mm, "mm_dmn", dkvc, full["wkv_cross"], mode="nt", out_dtype=F32)
    _, gs["ln_mem_w"] = _rms_bwd(mem, small["ln_mem_w"], dmn, None, "rms_mem_bwd")
    dya, dyb, dga, dgb = call(_dmerged_gate_bwd, "mm_dmerged", dx1_bf16, full["w_out"], proj, ya, yb)
    call(_mm, "mm_dwout", merged, dx1_bf16, mode="tn", out_dtype=BF16, grad_of="w_out")
    call(_mm, "mm_dwa", o_att, dya, mode="tn", out_dtype=BF16, grad_of="w_branch_a")
    do_att = call(_mm, "mm_doatt", dya, full["w_branch_a"], mode="nt", out_dtype=F32)
    call(_mm, "mm_dwb", o_hg, dyb, mode="tn", out_dtype=BF16, grad_of="w_branch_b")
    do_hg = call(_mm, "mm_dohg", dyb, full["w_branch_b"], mode="nt", out_dtype=F32)
    dqh, dfh, dih, dgh, dlb, gs["hg_norm_w"] = call(
        _hg_bwd, "hgrn_bwd", proj, small["hg_lower_bounds"], small["hg_norm_w"], oraw, states, do_hg)
    gs["hg_lb"] = dlb
    do_gs, dl_gs = call(_attn_merge_bwd, "attn_merge_bwd", outs, lses, do_att)
    dqs, dks, dvs = zip(*[call(_attn_bwd, f"attn_bwd_g{g}", proj, g, lses[g], do_gs[g], dl_gs[g]) for g in range(3)])
    dproj = _concat_cols([*dqs, *dks, *dvs, dqh, dfh, dih, dgh, dga, dgb], "dproj_concat")
    if net.comm:
        h_sibling, h_own = h_halves
        call(_mm, "mm_dwin_sibling", h_sibling, dproj, mode="tn", out_dtype=BF16, grad_of="w_in", sibling_half=True)
        call(_mm, "mm_dwin_own", h_own, dproj, mode="tn", out_dtype=BF16, res=_Late(lambda: net.recv["w_in"]),
             pair_sum_of="w_in")
    else:
        call(_mm, "mm_dwin", h, dproj, mode="tn", out_dtype=BF16, grad_of="w_in")
    dh = call(_mm, "mm_dh", dproj, full["w_in"], mode="nt", out_dtype=F32)
    dx, gs["ln_mix_w"] = _rms_bwd(x, small["ln_mix_w"], dh, dx1, "rms_mix_bwd")
    return loss, dx, gs


WEIGHT_ORDER = ("ln_mix_w", "w_in", "hg_norm_w", "hg_lower_bounds", "w_branch_a", "w_branch_b", "w_out",
                "ln_cross_w", "ln_mem_w", "wq_cross", "wkv_cross", "wo_cross", "ln_ffn_w", "w1", "w3", "w2",
                "ln_final_w")


def kernel(x, mem, ln_mix_w, w_in, hg_norm_w, hg_lower_bounds, w_branch_a, w_branch_b, w_out, ln_cross_w, ln_mem_w, wq_cross, wkv_cross, wo_cross, ln_ffn_w, w1, w3, w2, ln_final_w, loss_target, m_ln_mix_w, m_w_in, m_hg_norm_w, m_hg_lower_bounds, m_w_branch_a, m_w_branch_b, m_w_out, m_ln_cross_w, m_ln_mem_w, m_wq_cross, m_wkv_cross, m_wo_cross, m_ln_ffn_w, m_w1, m_w3, m_w2, m_ln_final_w, v_ln_mix_w, v_w_in, v_hg_norm_w, v_hg_lower_bounds, v_w_branch_a, v_w_branch_b, v_w_out, v_ln_cross_w, v_ln_mem_w, v_wq_cross, v_wkv_cross, v_wo_cross, v_ln_ffn_w, v_w1, v_w3, v_w2, v_ln_final_w):
    args = dict(locals())
    w = {n: args[n] for n in WEIGHT_ORDER}
    m = {n: args["m_" + n] for n in WEIGHT_ORDER}
    v = {n: args["v_" + n] for n in WEIGHT_ORDER}
    shapes = {n: w[n].shape for n in WEIGHT_ORDER}
    mat = lambda a: a.reshape(a.shape[-2:])
    shard_shapes = {n: shapes[n][-2:] for n in BIG_WEIGHTS}

    pos = _mesh_scalars()

    def cast(f, token=None):
        return _cast_into_full({n: mat(w[n]) for n in BIG_WEIGHTS if WEIGHT_PLACE[n][0] == f}, f, pos,
                               FULL_SPECS, WEIGHT_PLACE, f"cast_{f}", token)

    net = _Net({"w_in": cast("w_in")}, pos, shard_shapes)
    net.start([(f"ring_a{i}", (*job, "a")) for i, job in enumerate(W_IN_PIECES)], "ring", "gather_start_w_in")
    rest = {f: cast(f, net.token) for f in FULL_SPECS if f != "w_in"}
    net.full.update(rest)
    small = {n: w[n].reshape(1, -1) for n in SMALL_ROWS}
    small["hg_norm_w"] = w["hg_norm_w"].reshape(1, HEAD_DIM)
    small["hg_lower_bounds"] = w["hg_lower_bounds"]
    x2d = x.reshape(SEQ, D_MODEL)
    h, *h_halves = _rms_fwd_halves(x2d, small["ln_mix_w"], pos, "rms_mix")
    for i, job in enumerate(W_IN_PIECES):
        net.wait(f"ring_a{i}", after=[*rest.values(), h] if i == 0 else ())
        net.start([(f"ring_b{i}", (*job, "b"))], "ring", f"gather_pass_on_w_in{i}")
    net.start(GATHER_GROUPS, "gather", "gather_start_rest")
    for i, job in enumerate(W_IN_PIECES):
        net.wait(f"ring_b{i}")
        net.run_comm(("d2d", [job]), f"gather_hand_over_w_in{i}")
    core_row = jnp.full((1, D_MODEL), lax.axis_index("c").astype(F32))
    loss, dx, gs = _local_step(net, x2d, h, mem.reshape(MEM_LEN, D_MODEL), loss_target.reshape(SEQ, D_MODEL), small,
                               h_halves, core_row)

    out_g, out_d, out_m, out_v = {}, {}, {}, {}
    net.last = dx
    for step in FINISH:
        if step[0] == "adamw":
            for n in step[1]:
                out_d[n], out_m[n], out_v[n], out_g[n] = net.call(_adamw, f"adamw_{n}", mat(w[n]), net.grads[n],
                                                                  mat(m[n]), mat(v[n]))
        elif step[0] == "wait":
            net.wait(step[1], after=list(out_d.values()))
        elif step[0] == "share_and_small":
            pad = lambda a: jnp.pad(a, ((0, 0), (0, D_MODEL - a.shape[1])))
            part = jnp.concatenate(
                [gs[n] for n in SMALL_ROWS]
                + [pad(jnp.concatenate([gs["hg_norm_w"][0], loss], axis=1)), pad(gs["hg_lb"]),
                   pad(gs["hg_norm_w"][1]) if len(gs["hg_norm_w"]) > 1 else jnp.zeros((1, D_MODEL), F32)], axis=0)
            rows = net.run_comm(("share", step[1]), "rs_sibling_share_and_gather_small",
                                extra=[_small_gather_comm(part)])[0][SMALL_ROWS_OF_ALL]
            sg, sd, sm, sv, loss_tot = _small_update(rows, _pack_small(w), _pack_small(m), _pack_small(v))
            for dst, packed in ((out_g, sg), (out_d, sd), (out_m, sm), (out_v, sv)):
                dst.update(_unpack_small(packed, shapes))
        else:
            net.step(step)

    result = [loss_tot[0, 0], dx.reshape(x.shape)]
    for group in (out_g, out_d, out_m, out_v):
        result += [group[n].reshape(shapes[n]) for n in WEIGHT_ORDER]
    return tuple(result)
```

```python
import math

import jax
import jax.numpy as jnp
from jax import lax
from jax.experimental import pallas as pl
from jax.experimental.pallas import tpu as pltpu

F32 = jnp.float32
BF16 = jnp.bfloat16
MESH = pl.DeviceIdType.MESH

D_MODEL = 2048
SEQ = 2048
HEAD_DIM = 128
MEM_LEN = 256
ATT_GROUPS = ((128, 1), (512, 4), (2048, 16))
ATT_HEADS = 4
ATT_WIDTH = 3 * ATT_HEADS * HEAD_DIM
ATT_OUT = ATT_HEADS * HEAD_DIM
ATT_BLOCK = 128
HG_HEADS = 8
HG_WIDTH = HG_HEADS * HEAD_DIM
HG_CHUNK = 64
IN_WIDTH = 3 * ATT_WIDTH + 4 * HG_WIDTH + 2 * D_MODEL
CROSS_HEADS = 4
CROSS_WIDTH = CROSS_HEADS * HEAD_DIM
D_FF = 5632
RMS_EPS = 1e-6
ADAM_LR = 0.001
ADAM_B1 = 0.9
ADAM_B2 = 0.999
ADAM_EPS = 1e-08
ADAM_WD = 0.01
ADAM_STEP = 10
N_CHIPS = 4
N_DEV = 8

VMEM_LIMIT_BYTES = 56 * 1024 * 1024
LANE = 128
MXU_WIDTH = 256
MM_TILE_CAP = 1536
TRANSPOSE_CHUNK = 512
ANY = pl.BlockSpec(memory_space=pl.ANY)


def _cparams(sem=None):
    return pltpu.CompilerParams(dimension_semantics=sem, vmem_limit_bytes=VMEM_LIMIT_BYTES)


def _div(n, cap, mult):
    best = None
    for d in range(mult, min(n, cap) + 1, mult):
        if n % d == 0:
            best = d
    assert best is not None, (n, cap, mult)
    return best


def _sigmoid(x):
    return 1.0 / (1.0 + jnp.exp(-x))


def _dot(a, b):
    return jnp.dot(a.astype(BF16), b.astype(BF16), preferred_element_type=F32)


def _dot_nt(a, b):
    return lax.dot_general(a.astype(BF16), b.astype(BF16), (((1,), (1,)), ((), ())),
                           preferred_element_type=F32)


def _dot_tn(a, b):
    return jnp.dot(a.astype(F32).T.astype(BF16), b.astype(BF16), preferred_element_type=F32)


def _dot_exact(a, b):
    return jnp.dot(a, b, precision=lax.Precision.HIGHEST, preferred_element_type=F32)


class _Carried:
    def __init__(self, arrays, fresh, n_sems, start, finish, mid=None, reads=None):
        self.arrays, self.fresh, self.n_sems, self.reads = arrays, fresh, n_sems, reads or {}
        self.start, self.mid, self.finish = start, mid, finish


class _Token:
    def __init__(self, array):
        self.array = array


TOKEN_SHAPE = (8, LANE)


def _carried_layout(carried):
    akeys = list(dict.fromkeys(k for cm in carried for k in cm.arrays))
    fkeys = [(ci, k) for ci, cm in enumerate(carried) for k in cm.fresh]
    arrays = [next(cm.arrays[k] for cm in carried if k in cm.arrays) for k in akeys]
    shapes = [jax.ShapeDtypeStruct(a.shape, a.dtype) for a in arrays] + [carried[ci].fresh[k] for ci, k in fkeys]
    sems = []
    for cm in carried:
        sems += [pltpu.SemaphoreType.DMA((cm.n_sems,)), pltpu.SemaphoreType.DMA((cm.n_sems,))]
    return akeys, fkeys, arrays, shapes, sems


def _carried_reads(carried):
    rkeys = list(dict.fromkeys(k for cm in carried for k in cm.reads))
    return rkeys, [next(cm.reads[k] for cm in carried if k in cm.reads) for k in rkeys]


def _carried_results(carried, akeys, fkeys, outs, rkeys=(), read_refs=()):
    shared = dict(zip(akeys, outs[:len(akeys)]))
    shared.update(zip(rkeys, read_refs))
    res = [{k: shared[k] for k in list(cm.arrays) + [r for r in cm.reads if r in shared]} for cm in carried]
    for (ci, k), o in zip(fkeys, outs[len(akeys):]):
        res[ci][k] = o
    return res


def _pcall(kern, *, name, grid, in_specs, out_specs, out_shape, args, scratch_shapes=(), semantics=None,
           carried=()):
    tokens = [c.array for c in carried if isinstance(c, _Token)]
    carried = [c for c in carried if not isinstance(c, _Token)]
    single = not isinstance(out_shape, (list, tuple))
    out_specs = [out_specs] if single else list(out_specs)
    out_shape = [out_shape] if single else list(out_shape)
    n_real, n_out, n_scr = len(in_specs), len(out_shape), len(scratch_shapes)
    in_specs = list(in_specs) + [pl.BlockSpec(TOKEN_SHAPE, lambda *_: (0, 0))] * len(tokens)
    args = list(args) + tokens
    n_in = len(in_specs)
    if not carried:
        def plain(*refs):
            kern(*refs[:n_real], *refs[n_in:])

        outs = pl.pallas_call(plain if tokens else kern, name=name, grid=grid, in_specs=in_specs,
                              out_specs=out_specs, out_shape=out_shape, scratch_shapes=list(scratch_shapes),
                              compiler_params=_cparams(semantics))(*args)
        return (outs[0] if single else list(outs)), []
    akeys, fkeys, arrays, shapes, sems = _carried_layout(carried)
    rkeys, reads = _carried_reads(carried)
    n_a, n_f, n_r = len(akeys), len(fkeys), len(rkeys)
    total = math.prod(grid)
    mid_step = min(total - 1, (17 * total) // 20)

    def wrapped(*refs):
        ins = refs[:n_real]
        r0 = n_in + n_a
        o0 = r0 + n_r
        outs = refs[o0:o0 + n_out]
        a0 = o0 + n_out
        s0 = a0 + n_a + n_f
        per = _carried_results(carried, akeys, fkeys, refs[a0:s0], rkeys, refs[r0:o0])
        scratch = refs[s0:s0 + n_scr]
        sem = refs[s0 + n_scr:]
        step = 0
        for d, g in enumerate(grid):
            step = step * g + pl.program_id(d)

        @pl.when(step == 0)
        def _():
            for ci, cm in enumerate(carried):
                cm.start(per[ci], sem[2 * ci], sem[2 * ci + 1])

        kern(*ins, *outs, *scratch)

        @pl.when(step == mid_step)
        def _():
            for ci, cm in enumerate(carried):
                if cm.mid is not None:
                    cm.mid(per[ci], sem[2 * ci], sem[2 * ci + 1])

        @pl.when(step == total - 1)
        def _():
            for ci, cm in enumerate(carried):
                cm.finish(per[ci], sem[2 * ci], sem[2 * ci + 1])

    outs = pl.pallas_call(
        wrapped, name=name, grid=grid,
        in_specs=list(in_specs) + [ANY] * (n_a + n_r), out_specs=out_specs + [ANY] * (n_a + n_f),
        out_shape=out_shape + shapes,
        input_output_aliases={n_in + i: n_out + i for i in range(n_a)},
        scratch_shapes=list(scratch_shapes) + sems,
        compiler_params=_cparams(("arbitrary",) * len(grid)),
    )(*args, *arrays, *reads)
    res = _carried_results(carried, akeys, fkeys, outs[n_out:])
    return (outs[0] if single else list(outs[:n_out])), res


def _run_comm(carried, name):
    carried = list(carried)
    akeys, fkeys, arrays, shapes, sems = _carried_layout(carried)
    rkeys, reads = _carried_reads(carried)
    n_a, n_f, n_r = len(akeys), len(fkeys), len(rkeys)

    def body(*refs):
        o0 = n_a + n_r
        per = _carried_results(carried, akeys, fkeys, refs[o0:o0 + n_a + n_f], rkeys, refs[n_a:o0])
        sem = refs[o0 + n_a + n_f:]
        for hook in ("start", "mid", "finish"):
            for ci, cm in enumerate(carried):
                fn = getattr(cm, hook)
                if fn is not None:
                    fn(per[ci], sem[2 * ci], sem[2 * ci + 1])

    outs = pl.pallas_call(
        body, name=name, in_specs=[ANY] * (n_a + n_r), out_specs=[ANY] * (n_a + n_f), out_shape=shapes,
        input_output_aliases={i: i for i in range(n_a)}, scratch_shapes=sems,
    )(*arrays, *reads)
    return _carried_results(carried, akeys, fkeys, outs)


HBM_SPEC = pl.BlockSpec(memory_space=pltpu.HBM)
SEM_SPEC = pl.BlockSpec(memory_space=pltpu.SEMAPHORE)
SPLIT_EFFECT = pltpu.SideEffectType.DATAFLOW_SIDE_EFFECTING


def _in_hbm(a):
    return pltpu.with_memory_space_constraint(a, pltpu.HBM)


def _split_start(items, name, after=None):
    items = list(items)
    akeys, fkeys, arrays, shapes, sems = _carried_layout(items)
    assert not fkeys
    rkeys, reads = _carried_reads(items)
    n_a, n_s = len(akeys), len(sems)
    n_after = n_a + (after is not None)
    n_in = n_after + len(rkeys)

    def body(*refs):
        per = _carried_results(items, akeys, [], refs[n_in:n_in + n_a], rkeys, refs[n_after:n_in])
        sem = refs[n_in + n_a:n_in + n_a + n_s]
        for ci, cm in enumerate(items):
            cm.start(per[ci], sem[2 * ci], sem[2 * ci + 1])
        token = refs[n_in + n_a + n_s]
        token[...] = jnp.zeros_like(token)

    outs = pl.pallas_call(
        body, name=name, in_specs=[HBM_SPEC] * n_a + [ANY] * (n_in - n_a),
        out_specs=[HBM_SPEC] * n_a + [SEM_SPEC] * n_s + [pl.BlockSpec(memory_space=pltpu.VMEM)],
        out_shape=[pltpu.HBM(s.shape, s.dtype) for s in shapes] + sems + [jax.ShapeDtypeStruct(TOKEN_SHAPE, F32)],
        input_output_aliases={i: i for i in range(n_a)},
        compiler_params=pltpu.CompilerParams(has_side_effects=SPLIT_EFFECT),
    )(*[_in_hbm(a) for a in arrays], *([after] if after is not None else []), *reads)
    res = _carried_results(items, akeys, [], outs[:n_a])
    sem_out = outs[n_a:n_a + n_s]
    return res, [(sem_out[2 * ci], sem_out[2 * ci + 1]) for ci in range(len(items))], outs[-1]


def _split_wait(items, sems, after, name):
    items = list(items)
    after = list(after) if isinstance(after, (list, tuple)) else [after]
    akeys, fkeys, arrays, shapes, _ = _carried_layout(items)
    rkeys, reads = _carried_reads(items)
    n_a, n_s = len(akeys), 2 * len(items)
    n_after = n_a + n_s + len(after)
    n_in = n_after + len(rkeys)

    def body(*refs):
        per = _carried_results(items, akeys, [], refs[n_in:], rkeys, refs[n_after:n_in])
        sem = refs[n_a:n_a + n_s]
        for ci, cm in enumerate(items):
            cm.finish(per[ci], sem[2 * ci], sem[2 * ci + 1])

    outs = pl.pallas_call(
        body, name=name, in_specs=[HBM_SPEC] * n_a + [SEM_SPEC] * n_s + [ANY] * (n_in - n_a - n_s),
        out_specs=[HBM_SPEC] * n_a, out_shape=[pltpu.HBM(s.shape, s.dtype) for s in shapes],
        input_output_aliases={i: i for i in range(n_a)},
        compiler_params=pltpu.CompilerParams(has_side_effects=SPLIT_EFFECT),
    )(*arrays, *[s for pair in sems for s in pair], *after, *reads)
    return _carried_results(items, akeys, [], outs)


def _mm(a, b, *, mode, out_dtype, name, res=None, carried=(), fused=None):
    if mode == "nn":
        (m, k), (k2, n) = a.shape, b.shape
    elif mode == "nt":
        (m, k), (n, k2) = a.shape, b.shape
    else:
        (k, m), (k2, n) = a.shape, b.shape
    assert k == k2, (name, a.shape, b.shape)
    tm = _div(m, MM_TILE_CAP, LANE)
    tn = _div(n, MM_TILE_CAP, MXU_WIDTH) if n % MXU_WIDTH == 0 else 0
    if tn < 1024:
        tn = _div(n, MM_TILE_CAP, LANE)
    out_shape = jax.ShapeDtypeStruct((m, n), out_dtype)

    if fused is not None:
        assert mode in ("nn", "nt") and res is None and k <= 2048
        tm, tn = fused["tile"]
        dot = _dot if mode == "nn" else _dot_nt
        n_x = len(fused["ins"])

        summed = [len(e) > 2 for e in fused["outs"]]

        def kern_fused(*refs):
            part = dot(refs[0][...], refs[1][...])
            outs = fused["post"](part, *[r[...] for r in refs[2:2 + n_x]])
            first_row_tile = pl.program_id(1) == 0
            for o_ref, val, acc in zip(refs[2 + n_x:], outs, summed):
                if not acc:
                    o_ref[...] = val.astype(o_ref.dtype)
                    continue

                @pl.when(first_row_tile)
                def _():
                    o_ref[...] = val

                @pl.when(jnp.logical_not(first_row_tile))
                def _():
                    o_ref[...] += val

        def tile_spec(shape, index=lambda i, j: (i, j)):
            return pl.BlockSpec(shape, lambda j, i: index(i, j))

        return _pcall(
            kern_fused, name=name, grid=(n // tn, m // tm),
            in_specs=[pl.BlockSpec((tm, k), lambda j, i: (i, 0)),
                      pl.BlockSpec((k, tn), lambda j, i: (0, j)) if mode == "nn"
                      else pl.BlockSpec((tn, k), lambda j, i: (j, 0))] + [tile_spec(*e[1:]) for e in fused["ins"]],
            out_specs=[tile_spec(*e[1:]) for e in fused["outs"]], out_shape=[e[0] for e in fused["outs"]],
            args=(a, b, *[e[0] for e in fused["ins"]]), semantics=("parallel", "arbitrary"), carried=carried)

    if mode == "tn":
        has_res_tn = res is not None

        def kern_tn(*refs):
            a_ref, b_ref, o_ref, at_ref = refs[0], refs[1], refs[-2], refs[-1]

            @pl.when(pl.program_id(1) == 0)
            def _():
                step = min(TRANSPOSE_CHUNK, k)
                for c0 in range(0, k, step):
                    at_ref[:, c0:c0 + step] = a_ref[c0:c0 + step, :].astype(F32).T.astype(BF16)

            part = jnp.dot(at_ref[...], b_ref[...].astype(BF16), preferred_element_type=F32)
            if has_res_tn:
                part = part + refs[2][...].astype(F32)
            o_ref[...] = part.astype(o_ref.dtype)

        o_spec = pl.BlockSpec((tm, tn), lambda i, j: (i, j))
        return _pcall(
            kern_tn, name=name, grid=(m // tm, n // tn),
            in_specs=[pl.BlockSpec((k, tm), lambda i, j: (0, i)),
                      pl.BlockSpec((k, tn), lambda i, j: (0, j))] + [o_spec] * has_res_tn,
            out_specs=o_spec, out_shape=out_shape, args=(a, b) + ((res,) if has_res_tn else ()),
            scratch_shapes=[pltpu.VMEM((tm, k), BF16)],
            semantics=("parallel", "arbitrary"), carried=carried)

    tk = k if k <= 2048 else _div(k, 3072, LANE)
    nk = k // tk
    a_spec = pl.BlockSpec((tm, tk), lambda i, j, kk: (i, kk))
    if mode == "nn":
        b_spec = pl.BlockSpec((tk, tn), lambda i, j, kk: (kk, j))
        dot = _dot
    else:
        b_spec = pl.BlockSpec((tn, tk), lambda i, j, kk: (j, kk))
        dot = _dot_nt
    o_spec = pl.BlockSpec((tm, tn), lambda i, j, kk: (i, j))
    in_specs = [a_spec, b_spec]
    args = [a, b]
    if res is not None:
        in_specs.append(o_spec)
        args.append(res)
    has_res = res is not None

    def kern(*refs):
        a_ref, b_ref = refs[0], refs[1]
        r_ref = refs[2] if has_res else None
        o_ref = refs[3] if has_res else refs[2]
        part = dot(a_ref[...], b_ref[...])
        if nk == 1:
            if has_res:
                part = part + r_ref[...]
            o_ref[...] = part.astype(o_ref.dtype)
            return
        acc_ref = refs[-1]
        kk = pl.program_id(2)

        @pl.when(kk == 0)
        def _():
            acc_ref[...] = part

        @pl.when(kk > 0)
        def _():
            acc_ref[...] += part

        @pl.when(kk == nk - 1)
        def _():
            tot = acc_ref[...]
            if has_res:
                tot = tot + r_ref[...]
            o_ref[...] = tot.astype(o_ref.dtype)

    return _pcall(
        kern, name=name, grid=(m // tm, n // tn, nk),
        in_specs=in_specs, out_specs=o_spec, out_shape=out_shape, args=args,
        scratch_shapes=[pltpu.VMEM((tm, tn), F32)] if nk > 1 else [],
        semantics=("parallel", "parallel", "arbitrary"), carried=carried)


ROW_BLOCK = 512


def _rms_fwd(x, g, name):
    t, d = x.shape
    tr = min(ROW_BLOCK, t)

    def kern(x_ref, g_ref, o_ref):
        xf = x_ref[...]
        r = lax.rsqrt(jnp.mean(xf * xf, axis=-1, keepdims=True) + RMS_EPS)
        o_ref[...] = (xf * r * g_ref[...]).astype(o_ref.dtype)

    return pl.pallas_call(
        kern, name=name, grid=(t // tr,),
        in_specs=[pl.BlockSpec((tr, d), lambda i: (i, 0)), pl.BlockSpec((1, d), lambda i: (0, 0))],
        out_specs=pl.BlockSpec((tr, d), lambda i: (i, 0)),
        out_shape=jax.ShapeDtypeStruct((t, d), BF16),
        compiler_params=_cparams(("parallel",)),
    )(x, g)


def _rms_fwd_halves(x, g, pos, name):
    t, d = x.shape
    tr = min(ROW_BLOCK, t)
    half = d // 2

    def kern(pos_ref, x_ref, g_ref, o_ref, sib_ref, own_ref):
        xf = x_ref[...]
        r = lax.rsqrt(jnp.mean(xf * xf, axis=-1, keepdims=True) + RMS_EPS)
        h = (xf * r * g_ref[...]).astype(o_ref.dtype)
        o_ref[...] = h
        is_south = pos_ref[0] == 0
        sib_ref[...] = jnp.where(is_south, h[:, half:], h[:, :half])
        own_ref[...] = jnp.where(is_south, h[:, :half], h[:, half:])

    row = pl.BlockSpec((tr, d), lambda i, pos_ref: (i, 0))
    part = pl.BlockSpec((tr, half), lambda i, pos_ref: (i, 0))
    return pl.pallas_call(
        kern, name=name,
        grid_spec=_grid_spec((t // tr,), [row, pl.BlockSpec((1, d), lambda i, pos_ref: (0, 0))], [row, part, part]),
        out_shape=[jax.ShapeDtypeStruct((t, d), BF16)] + [jax.ShapeDtypeStruct((t, half), BF16)] * 2,
        compiler_params=_cparams(("parallel",)),
    )(pos, x, g)


def _rms_bwd(x, g, dh, res, name, bf16_copy=False):
    t, d = x.shape
    tr = min(ROW_BLOCK, t)
    has_res = res is not None

    def kern(*refs):
        x_ref, g_ref, dh_ref = refs[:3]
        r_ref = refs[3] if has_res else None
        dx_ref, dg_ref = refs[3 + has_res], refs[4 + has_res]
        xf = x_ref[...]
        r = lax.rsqrt(jnp.mean(xf * xf, axis=-1, keepdims=True) + RMS_EPS)
        xn = xf * r
        dh_ = dh_ref[...]
        dhg = dh_ * g_ref[...]
        dx = r * (dhg - xn * jnp.mean(dhg * xn, axis=-1, keepdims=True))
        if has_res:
            dx = dx + r_ref[...]
        dx_ref[...] = dx
        if bf16_copy:
            refs[-1][...] = dx.astype(BF16)
        part = jnp.sum(dh_ * xn, axis=0, keepdims=True)

        @pl.when(pl.program_id(0) == 0)
        def _():
            dg_ref[...] = part

        @pl.when(pl.program_id(0) > 0)
        def _():
            dg_ref[...] += part

    row = pl.BlockSpec((tr, d), lambda i: (i, 0))
    vec = pl.BlockSpec((1, d), lambda i: (0, 0))
    in_specs = [row, vec, row] + ([row] if has_res else [])
    args = [x, g, dh] + ([res] if has_res else [])
    return pl.pallas_call(
        kern, name=name, grid=(t // tr,), in_specs=in_specs, out_specs=[row, vec] + [row] * bf16_copy,
        out_shape=[jax.ShapeDtypeStruct((t, d), F32), jax.ShapeDtypeStruct((1, d), F32)]
        + [jax.ShapeDtypeStruct((t, d), BF16)] * bf16_copy,
        compiler_params=_cparams(("arbitrary",)),
    )(*args)


def _loss_head(x3, g, target, name, core_row=None):
    t, d = x3.shape
    tr = ROW_BLOCK
    with_halves = core_row is not None

    def kern(x_ref, g_ref, t_ref, *rest):
        dx_ref, dg_ref, loss_ref, dxb_ref = rest[with_halves:with_halves + 4]
        xf = x_ref[...]
        r = lax.rsqrt(jnp.mean(xf * xf, axis=-1, keepdims=True) + RMS_EPS)
        xn = xf * r
        gg = g_ref[...]
        err = xn * gg - t_ref[...]
        lpart = 0.5 * jnp.sum(jnp.mean(err * err, axis=-1, keepdims=True), axis=0, keepdims=True)
        dy = err * (1.0 / d)
        dyg = dy * gg
        dx = r * (dyg - xn * jnp.mean(dyg * xn, axis=-1, keepdims=True))
        dx_ref[...] = dx
        dxb = dx.astype(BF16)
        dxb_ref[...] = dxb
        if with_halves:
            rest[-2][...], rest[-1][...] = _column_halves(dxb, rest[0][...])
        gpart = jnp.sum(dy * xn, axis=0, keepdims=True)
        lrow = jnp.broadcast_to(lpart, (1, LANE))

        @pl.when(pl.program_id(0) == 0)
        def _():
            dg_ref[...] = gpart
            loss_ref[...] = lrow

        @pl.when(pl.program_id(0) > 0)
        def _():
            dg_ref[...] += gpart
            loss_ref[...] += lrow

    row = pl.BlockSpec((tr, d), lambda i: (i, 0))
    vec = pl.BlockSpec((1, d), lambda i: (0, 0))
    part = pl.BlockSpec((tr, d // 2), lambda i: (i, 0))
    return pl.pallas_call(
        kern, name=name, grid=(t // tr,), in_specs=[row, vec, row] + [vec] * with_halves,
        out_specs=[row, vec, pl.BlockSpec((1, LANE), lambda i: (0, 0)), row] + [part] * (2 * with_halves),
        out_shape=[jax.ShapeDtypeStruct((t, d), F32), jax.ShapeDtypeStruct((1, d), F32),
                   jax.ShapeDtypeStruct((1, LANE), F32), jax.ShapeDtypeStruct((t, d), BF16)]
        + [jax.ShapeDtypeStruct((t, d // 2), BF16)] * (2 * with_halves),
        compiler_params=_cparams(("arbitrary",)),
    )(x3, g, target, *([core_row] if with_halves else []))


ATT_SCALE = HEAD_DIM ** -0.5
Q_BLOCK0, K_BLOCK0, V_BLOCK0 = 0, ATT_WIDTH // HEAD_DIM, 2 * ATT_WIDTH // HEAD_DIM


def _residue_rows(dil, r, n):
    if dil == 1:
        return pl.ds(n * ATT_BLOCK, ATT_BLOCK)
    return pl.ds(n * ATT_BLOCK * dil + r, ATT_BLOCK, stride=dil)


def _band_mask(with_prev):
    width = 2 * ATT_BLOCK if with_prev else ATT_BLOCK
    iq = lax.broadcasted_iota(jnp.int32, (ATT_BLOCK, width), 0)
    ik = lax.broadcasted_iota(jnp.int32, (ATT_BLOCK, width), 1)
    if not with_prev:
        return ik <= iq
    return ((ik < ATT_BLOCK) & (iq <= ik)) | ((ik >= ATT_BLOCK) & ((ik - ATT_BLOCK) <= iq))


def _band_keys(ref, dil, r, n):
    own = ref[_residue_rows(dil, r, n), :]
    if n == 0:
        return own
    return jnp.concatenate([ref[_residue_rows(dil, r, n - 1), :], own], axis=0)


def _attn_col_spec(base, grp):
    return pl.BlockSpec((SEQ, HEAD_DIM), lambda h: (0, base + grp * ATT_HEADS + h))


def _attn_fwd(proj, grp, name, carried=()):
    _, dil = ATT_GROUPS[grp]
    nb = SEQ // dil // ATT_BLOCK

    def kern(q_ref, k_ref, v_ref, o_ref, lse_ref):
        for r in range(dil):
            for n in range(nb):
                rows = _residue_rows(dil, r, n)
                s = _dot_nt(q_ref[rows, :], _band_keys(k_ref, dil, r, n)) * ATT_SCALE
                s = jnp.where(_band_mask(n > 0), s, -jnp.inf)
                m = jnp.max(s, axis=-1, keepdims=True)
                p = jnp.exp(s - m)
                l = jnp.sum(p, axis=-1, keepdims=True)
                o_ref[rows, :] = _dot(p / l, _band_keys(v_ref, dil, r, n))
                lse_ref[rows, :] = jnp.broadcast_to(m + jnp.log(l), (ATT_BLOCK, HEAD_DIM))

    out_spec = pl.BlockSpec((SEQ, HEAD_DIM), lambda h: (0, h))
    return _pcall(
        kern, name=name, grid=(ATT_HEADS,),
        in_specs=[_attn_col_spec(Q_BLOCK0, grp), _attn_col_spec(K_BLOCK0, grp), _attn_col_spec(V_BLOCK0, grp)],
        out_specs=[out_spec, out_spec],
        out_shape=[jax.ShapeDtypeStruct((SEQ, ATT_OUT), F32)] * 2, args=(proj, proj, proj),
        semantics=("parallel",), carried=carried)


def _attn_weights(l0, l1, l2):
    mx = jnp.maximum(jnp.maximum(l0, l1), l2)
    e0, e1, e2 = jnp.exp(l0 - mx), jnp.exp(l1 - mx), jnp.exp(l2 - mx)
    den = e0 + e1 + e2
    return e0 / den, e1 / den, e2 / den


def _attn_merge_fwd(outs, lses, name):
    tr = ROW_BLOCK

    def kern(o0, o1, o2, l0, l1, l2, out_ref):
        a0, a1, a2 = _attn_weights(l0[...], l1[...], l2[...])
        out_ref[...] = (a0 * o0[...] + a1 * o1[...] + a2 * o2[...]).astype(out_ref.dtype)

    spec = pl.BlockSpec((tr, ATT_OUT), lambda i: (i, 0))
    return pl.pallas_call(
        kern, name=name, grid=(SEQ // tr,), in_specs=[spec] * 6, out_specs=spec,
        out_shape=jax.ShapeDtypeStruct((SEQ, ATT_OUT), BF16),
        compiler_params=_cparams(("parallel",)),
    )(*outs, *lses)


def _attn_merge_bwd(outs, lses, do_att, name, carried=()):
    tr = ROW_BLOCK

    def kern(o0, o1, o2, l0, l1, l2, do_ref, d0, d1, d2, t0, t1, t2):
        alphas = _attn_weights(l0[...], l1[...], l2[...])
        do = do_ref[...]
        o_att = alphas[0] * o0[...] + alphas[1] * o1[...] + alphas[2] * o2[...]
        prod = do * o_att
        parts = []
        for h in range(ATT_HEADS):
            sl = slice(h * HEAD_DIM, (h + 1) * HEAD_DIM)
            tot = jnp.sum(prod[:, sl], axis=-1, keepdims=True)
            parts.append(jnp.broadcast_to(tot, (tr, HEAD_DIM)))
        dd = jnp.concatenate(parts, axis=1)
        for a, d_ref, t_ref in zip(alphas, (d0, d1, d2), (t0, t1, t2)):
            d_ref[...] = a * do
            t_ref[...] = -a * dd

    spec = pl.BlockSpec((tr, ATT_OUT), lambda i: (i, 0))
    res, cres = _pcall(
        kern, name=name, grid=(SEQ // tr,), in_specs=[spec] * 7, out_specs=[spec] * 6,
        out_shape=[jax.ShapeDtypeStruct((SEQ, ATT_OUT), F32)] * 6, args=(*outs, *lses, do_att),
        semantics=("parallel",), carried=carried)
    return (res[:3], res[3:]), cres


def _attn_bwd(proj, grp, lse, do_g, dl_g, name, carried=()):
    _, dil = ATT_GROUPS[grp]
    nb = SEQ // dil // ATT_BLOCK

    def kern(q_ref, k_ref, v_ref, do_ref, lse_ref, dl_ref, dq_ref, dk_ref, dv_ref, dq_acc, dk_acc, dv_acc):
        dk_acc[...] = jnp.zeros_like(dk_acc)
        dv_acc[...] = jnp.zeros_like(dv_acc)
        for r in range(dil):
            for n in range(nb):
                rows = _residue_rows(dil, r, n)
                q, do = q_ref[rows, :], do_ref[rows, :]
                kk, vv = _band_keys(k_ref, dil, r, n), _band_keys(v_ref, dil, r, n)
                s = _dot_nt(q, kk) * ATT_SCALE
                p = jnp.where(_band_mask(n > 0), jnp.exp(s - lse_ref[rows, :][:, :1]), 0.0)
                ds = p * (_dot_nt(do, vv) + dl_ref[rows, :][:, :1])
                dq_acc[rows, :] = _dot(ds, kk) * ATT_SCALE
                dk = _dot_tn(ds, q) * ATT_SCALE
                dv = _dot_tn(p, do)
                if n > 0:
                    prev = _residue_rows(dil, r, n - 1)
                    dk_acc[prev, :] += dk[:ATT_BLOCK]
                    dv_acc[prev, :] += dv[:ATT_BLOCK]
                    dk, dv = dk[ATT_BLOCK:], dv[ATT_BLOCK:]
                dk_acc[rows, :] += dk
                dv_acc[rows, :] += dv
        dq_ref[...] = dq_acc[...].astype(dq_ref.dtype)
        dk_ref[...] = dk_acc[...].astype(dk_ref.dtype)
        dv_ref[...] = dv_acc[...].astype(dv_ref.dtype)

    spec = pl.BlockSpec((SEQ, HEAD_DIM), lambda h: (0, h))
    return _pcall(
        kern, name=name, grid=(ATT_HEADS,),
        in_specs=[_attn_col_spec(Q_BLOCK0, grp), _attn_col_spec(K_BLOCK0, grp), _attn_col_spec(V_BLOCK0, grp),
                  spec, spec, spec],
        out_specs=[spec] * 3,
        out_shape=[jax.ShapeDtypeStruct((SEQ, ATT_OUT), BF16)] * 3, args=(proj, proj, proj, do_g, lse, dl_g),
        scratch_shapes=[pltpu.VMEM((SEQ, HEAD_DIM), F32)] * 3,
        semantics=("parallel",), carried=carried)


HG_HEADS_PER_STEP = 8
HG_BLOCK_W = 4 * HEAD_DIM
HG_BLOCKS = HG_HEADS_PER_STEP * HEAD_DIM // HG_BLOCK_W
HG_STEP_W = HG_HEADS_PER_STEP * HEAD_DIM
HG_Q_BLK = (3 * ATT_WIDTH) // HG_BLOCK_W
HG_N_CHUNKS = SEQ // HG_CHUNK
HG_MID = HG_CHUNK // 2


def _lower_bound(lb_ref, sl):
    l0, l1 = lb_ref[0:1, sl], lb_ref[1:2, sl]
    mx = jnp.maximum(l0, l1)
    e0, e1 = jnp.exp(l0 - mx), jnp.exp(l1 - mx)
    return e0 / (e0 + e1)


def _tri(lower):
    i = lax.broadcasted_iota(jnp.int32, (HG_CHUNK, HG_CHUNK), 0)
    j = lax.broadcasted_iota(jnp.int32, (HG_CHUNK, HG_CHUNK), 1)
    return (i >= j) if lower else (i <= j)


def _head_mean(x):
    parts = []
    for hd in range(x.shape[1] // HEAD_DIM):
        m = jnp.mean(x[:, hd * HEAD_DIM:(hd + 1) * HEAD_DIM], axis=-1, keepdims=True)
        parts.append(jnp.broadcast_to(m, (x.shape[0], HEAD_DIM)))
    return jnp.concatenate(parts, axis=1)


def _hg_chunk_terms(qh, fh, lb):
    sig = _sigmoid(fh)
    f = lb + (1.0 - lb) * sig
    k = 1.0 - f
    b = _dot_exact(_tri(True).astype(F32), jnp.log(f))
    bl = b[HG_CHUNK - 1:HG_CHUNK, :]
    br = b[HG_MID:HG_MID + 1, :]
    sq = _sigmoid(qh)
    q = qh * sq
    return dict(sig=sig, f=f, k=k, b=b, bl=bl, br=br, sq=sq, q=q,
                e1=jnp.exp(bl - b), e2=jnp.exp(b), e3=jnp.exp(b - br), e4=jnp.exp(br - b))


def _hg_fwd(proj, lbw, normw, name, carried=()):
    def in_blks(off):
        return [pl.BlockSpec((HG_CHUNK, HG_BLOCK_W), lambda hp, n, b=b: (n, HG_Q_BLK + off + hp * HG_BLOCKS + b))
                for b in range(HG_BLOCKS)]

    def kern(*refs):
        q_refs, f_refs, i_refs, g_refs = (refs[k * HG_BLOCKS:(k + 1) * HG_BLOCKS] for k in range(4))
        lb_ref, nw_ref, oraw_ref, ohg_ref, st_ref, state = refs[4 * HG_BLOCKS:]

        @pl.when(pl.program_id(1) == 0)
        def _():
            state[...] = jnp.zeros_like(state)

        causal = _tri(True)
        wide = lambda rs: jnp.concatenate([r[...] for r in rs], axis=1)
        t = _hg_chunk_terms(wide(q_refs), wide(f_refs), _lower_bound(lb_ref, slice(None)))
        v, gh = wide(i_refs), wide(g_refs)
        kd, qb, qr, kr = t["k"] * t["e1"], t["q"] * t["e2"], t["q"] * t["e3"], t["k"] * t["e4"]
        decay = jnp.exp(t["bl"])
        outs = []
        for hd in range(HG_HEADS_PER_STEP):
            sl = slice(hd * HEAD_DIM, (hd + 1) * HEAD_DIM)
            st = state[hd]
            st_ref[0, hd] = st
            a = jnp.where(causal, _dot_nt(qr[:, sl], kr[:, sl]), 0.0)
            outs.append(_dot_nt(qb[:, sl], st) + _dot(a, v[:, sl]))
            state[hd] = st * decay[:, sl] + _dot_tn(v[:, sl], kd[:, sl])
        o = jnp.concatenate(outs, axis=1)
        oraw_ref[...] = o
        r = lax.rsqrt(_head_mean(o * o) + RMS_EPS)
        nw = jnp.tile(nw_ref[...], (1, HG_HEADS_PER_STEP))
        ohg_ref[...] = (o * r * nw * (gh * _sigmoid(gh))).astype(ohg_ref.dtype)

    out_blk = pl.BlockSpec((HG_CHUNK, HG_STEP_W), lambda hp, n: (n, hp))
    return _pcall(
        kern, name=name, grid=(HG_HEADS // HG_HEADS_PER_STEP, HG_N_CHUNKS),
        in_specs=[*in_blks(0), *in_blks(2), *in_blks(4), *in_blks(6),
                  pl.BlockSpec((2, HG_STEP_W), lambda hp, n: (0, hp)),
                  pl.BlockSpec((1, HEAD_DIM), lambda hp, n: (0, 0))],
        out_specs=[out_blk, out_blk,
                   pl.BlockSpec((1, HG_HEADS_PER_STEP, HEAD_DIM, HEAD_DIM), lambda hp, n: (n, hp, 0, 0))],
        out_shape=[jax.ShapeDtypeStruct((SEQ, HG_WIDTH), F32), jax.ShapeDtypeStruct((SEQ, HG_WIDTH), BF16),
                   jax.ShapeDtypeStruct((HG_N_CHUNKS, HG_HEADS, HEAD_DIM, HEAD_DIM), F32)],
        args=(*[proj] * (4 * HG_BLOCKS), lbw, normw),
        scratch_shapes=[pltpu.VMEM((HG_HEADS_PER_STEP, HEAD_DIM, HEAD_DIM), F32)],
        semantics=("parallel", "arbitrary"), carried=carried)


def _hg_bwd(proj, lbw, normw, oraw, states, do_hg, name, carried=()):
    last = HG_N_CHUNKS - 1

    def in_blks(off):
        return [pl.BlockSpec((HG_CHUNK, HG_BLOCK_W),
                             lambda hp, n, b=b: (last - n, HG_Q_BLK + off + hp * HG_BLOCKS + b))
                for b in range(HG_BLOCKS)]

    blk = pl.BlockSpec((HG_CHUNK, HG_STEP_W), lambda hp, n: (last - n, hp))

    def kern(*refs):
        q_refs, f_refs, i_refs, g_refs = (refs[k * HG_BLOCKS:(k + 1) * HG_BLOCKS] for k in range(4))
        (lb_ref, nw_ref, oraw_ref, st_ref, do_ref, dq_ref, df_ref, di_ref, dg_ref, dlb_ref, dnw_ref,
         dstate) = refs[4 * HG_BLOCKS:]
        first = pl.program_id(1) == 0

        @pl.when(first)
        def _():
            dstate[...] = jnp.zeros_like(dstate)

        causal = _tri(True)
        wide = lambda rs: jnp.concatenate([r[...] for r in rs], axis=1)
        cat = lambda parts: jnp.concatenate(parts, axis=1)
        qh, fh, v, gh = wide(q_refs), wide(f_refs), wide(i_refs), wide(g_refs)
        o, dout = oraw_ref[...], do_ref[...]
        nw = jnp.tile(nw_ref[...], (1, HG_HEADS_PER_STEP))
        sgg = _sigmoid(gh)
        r = lax.rsqrt(_head_mean(o * o) + RMS_EPS)
        xn = o * r
        dg_ref[...] = (dout * xn * nw * (sgg * (1.0 + gh * (1.0 - sgg)))).astype(dg_ref.dtype)
        don = dout * (gh * sgg)
        dnw_wide = jnp.sum(don * xn, axis=0, keepdims=True)
        dnw_tot = dnw_wide[:, :HEAD_DIM]
        for hd in range(1, HG_HEADS_PER_STEP):
            dnw_tot = dnw_tot + dnw_wide[:, hd * HEAD_DIM:(hd + 1) * HEAD_DIM]
        tt = don * nw
        do = r * (tt - xn * _head_mean(tt * xn))
        lb = _lower_bound(lb_ref, slice(None))
        t = _hg_chunk_terms(qh, fh, lb)
        k, q = t["k"], t["q"]
        kd, qb, qr, kr = k * t["e1"], q * t["e2"], q * t["e3"], k * t["e4"]
        decay = jnp.exp(t["bl"])
        dqb, dqr, dkr, dkd, dv, ddecay = [], [], [], [], [], []
        for hd in range(HG_HEADS_PER_STEP):
            sl = slice(hd * HEAD_DIM, (hd + 1) * HEAD_DIM)
            st = st_ref[0, hd]
            dstn = dstate[hd]
            a = jnp.where(causal, _dot_nt(qr[:, sl], kr[:, sl]), 0.0)
            da = jnp.where(causal, _dot_nt(do[:, sl], v[:, sl]), 0.0)
            dqb.append(_dot(do[:, sl], st))
            dv.append(_dot_tn(a, do[:, sl]) + _dot_nt(kd[:, sl], dstn))
            dqr.append(_dot(da, kr[:, sl]))
            dkr.append(_dot_tn(da, qr[:, sl]))
            dkd.append(_dot(v[:, sl], dstn))
            ddecay.append(jnp.sum(dstn * st, axis=0, keepdims=True))
            dstate[hd] = dstn * decay[:, sl] + _dot_tn(do[:, sl], qb[:, sl])
        dqb, dqr, dkr, dkd, dv, ddecay = cat(dqb), cat(dqr), cat(dkr), cat(dkd), cat(dv), cat(ddecay)
        dq = dqb * t["e2"] + dqr * t["e3"]
        dk = dkd * t["e1"] + dkr * t["e4"]
        db = dqb * qb + dqr * qr - dkr * kr - dkd * kd
        dbl = jnp.sum(dkd * kd, axis=0, keepdims=True) + ddecay * decay
        dbr = jnp.sum(dkr * kr - dqr * qr, axis=0, keepdims=True)
        rows = lax.broadcasted_iota(jnp.int32, db.shape, 0)
        dlf = _dot_exact(_tri(False).astype(F32), db) + dbl + jnp.where(rows <= HG_MID, dbr, 0.0)
        df = dlf / t["f"] - dk
        sig, sq = t["sig"], t["sq"]
        df_ref[...] = (df * (1.0 - lb) * sig * (1.0 - sig)).astype(df_ref.dtype)
        dlb_row = jnp.sum(df * (1.0 - sig), axis=0, keepdims=True)
        dq_ref[...] = (dq * (sq * (1.0 + qh * (1.0 - sq)))).astype(dq_ref.dtype)
        di_ref[...] = dv.astype(di_ref.dtype)
        dnw_blk = jnp.broadcast_to(dnw_tot, (8, HEAD_DIM))

        @pl.when(first)
        def _():
            dlb_ref[...] = dlb_row
            dnw_ref[...] = dnw_blk

        @pl.when(jnp.logical_not(first))
        def _():
            dlb_ref[...] += dlb_row
            dnw_ref[...] += dnw_blk

    n_hp = HG_HEADS // HG_HEADS_PER_STEP
    outs, cres = _pcall(
        kern, name=name, grid=(n_hp, HG_N_CHUNKS),
        in_specs=[*in_blks(0), *in_blks(2), *in_blks(4), *in_blks(6),
                  pl.BlockSpec((2, HG_STEP_W), lambda hp, n: (0, hp)),
                  pl.BlockSpec((1, HEAD_DIM), lambda hp, n: (0, 0)),
                  blk,
                  pl.BlockSpec((1, HG_HEADS_PER_STEP, HEAD_DIM, HEAD_DIM), lambda hp, n: (last - n, hp, 0, 0)),
                  blk],
        out_specs=[blk, blk, blk, blk,
                   pl.BlockSpec((1, HG_STEP_W), lambda hp, n: (0, hp)),
                   pl.BlockSpec((8, HEAD_DIM), lambda hp, n: (hp, 0))],
        out_shape=[jax.ShapeDtypeStruct((SEQ, HG_WIDTH), BF16)] * 4
        + [jax.ShapeDtypeStruct((1, HG_WIDTH), F32), jax.ShapeDtypeStruct((8 * n_hp, HEAD_DIM), F32)],
        args=(*[proj] * (4 * HG_BLOCKS), lbw, normw, oraw, states, do_hg),
        scratch_shapes=[pltpu.VMEM((HG_HEADS_PER_STEP, HEAD_DIM, HEAD_DIM), F32)],
        semantics=("parallel", "arbitrary"), carried=carried)
    dqh, dfh, dih, dgh, dlb, dnw = outs
    return (dqh, dfh, dih, dgh, dlb, [dnw[8 * i:8 * i + 1] for i in range(n_hp)]), cres


GATE_BLOCK_W = 512
GATE_A_BLK = (3 * ATT_WIDTH + 4 * HG_WIDTH) // GATE_BLOCK_W
GATE_B_BLK = GATE_A_BLK + D_MODEL // GATE_BLOCK_W


GATE_TILE = (1024, GATE_BLOCK_W)


def _gate_ins(proj, ya, yb=None):
    ins = [(proj, GATE_TILE, lambda i, j: (i, GATE_A_BLK + j)), (proj, GATE_TILE, lambda i, j: (i, GATE_B_BLK + j)),
           (ya, GATE_TILE)]
    return ins + ([(yb, GATE_TILE)] if yb is not None else [])


def _branch_b_gate(o_hg, w_b, proj, ya, name, carried=()):
    def post(yb, ga, gb, ya_):
        return yb, _sigmoid(ga) * ya_ + _sigmoid(gb) * yb

    return _mm(o_hg, w_b, mode="nn", out_dtype=F32, name=name, carried=carried, fused=dict(
        tile=GATE_TILE, ins=_gate_ins(proj, ya), post=post,
        outs=[(jax.ShapeDtypeStruct((SEQ, D_MODEL), F32), GATE_TILE), (jax.ShapeDtypeStruct((SEQ, D_MODEL), BF16), GATE_TILE)]))


def _dmerged_gate_bwd(dx1, w_out, proj, ya, yb, name, carried=()):
    def post(dm, ga, gb, ya_, yb_):
        sa, sb = _sigmoid(ga), _sigmoid(gb)
        return dm * sa, dm * sb, dm * ya_ * sa * (1.0 - sa), dm * yb_ * sb * (1.0 - sb)

    return _mm(dx1, w_out, mode="nt", out_dtype=BF16, name=name, carried=carried, fused=dict(
        tile=GATE_TILE, ins=_gate_ins(proj, ya, yb), post=post,
        outs=[(jax.ShapeDtypeStruct((SEQ, D_MODEL), BF16), GATE_TILE)] * 4))


NORM_TILE = (512, D_MODEL)


def _column_halves(v, core_row):
    half = v.shape[1] // 2
    south = core_row[:, :half] < 0.5
    lo, hi = v[:, :half], v[:, half:]
    return jnp.where(south, hi, lo), jnp.where(south, lo, hi)


def _residual_norm(a, w, x, g, name, carried=(), core_row=None):
    vec = ((1, D_MODEL), lambda i, j: (0, j))
    half_tile = (NORM_TILE[0], D_MODEL // 2)

    def post(part, x_, g_, *core):
        xn = part + x_
        r = lax.rsqrt(jnp.mean(xn * xn, axis=-1, keepdims=True) + RMS_EPS)
        hn = (xn * r * g_).astype(BF16)
        return (xn, hn) + (_column_halves(hn, core[0]) if core else ())

    wide = [(jax.ShapeDtypeStruct((SEQ, D_MODEL), F32), NORM_TILE), (jax.ShapeDtypeStruct((SEQ, D_MODEL), BF16), NORM_TILE)]
    halves = [(jax.ShapeDtypeStruct((SEQ, D_MODEL // 2), BF16), half_tile)] * 2
    return _mm(a, w, mode="nn", out_dtype=F32, name=name, carried=carried, fused=dict(
        tile=NORM_TILE, post=post, ins=[(x, NORM_TILE), (g, *vec)] + ([(core_row, *vec)] if core_row is not None else []),
        outs=wide + (halves if core_row is not None else [])))


def _norm_bwd_residual(dq, w, x, g, res, name, carried=()):
    def post(dh, x_, g_, res_):
        r = lax.rsqrt(jnp.mean(x_ * x_, axis=-1, keepdims=True) + RMS_EPS)
        xn = x_ * r
        dhg = dh * g_
        dx = r * (dhg - xn * jnp.mean(dhg * xn, axis=-1, keepdims=True)) + res_
        return dx, jnp.sum(dh * xn, axis=0, keepdims=True), dx

    vec = ((1, D_MODEL), lambda i, j: (0, j))
    return _mm(dq, w, mode="nt", out_dtype=F32, name=name, carried=carried, fused=dict(
        tile=NORM_TILE, ins=[(x, NORM_TILE), (g, *vec), (res, NORM_TILE)], post=post,
        outs=[(jax.ShapeDtypeStruct((SEQ, D_MODEL), F32), NORM_TILE), (jax.ShapeDtypeStruct((1, D_MODEL), F32), *vec),
              (jax.ShapeDtypeStruct((SEQ, D_MODEL), BF16), NORM_TILE)]))


FF_SHARD = D_FF // N_CHIPS


FF_TILE_ROWS = 512


def _swiglu_tile(ab):
    a, b = ab[:, :FF_SHARD], ab[:, FF_SHARD:]
    return a * _sigmoid(a) * b


def _swiglu_grad_tile(du, ab):
    a, b = ab[:, :FF_SHARD], ab[:, FF_SHARD:]
    sg = _sigmoid(a)
    return jnp.concatenate([du * b * (sg * (1.0 + a * (1.0 - sg))), du * (a * sg)], axis=1)


def _ff_up(hf, w13, name, carried=()):
    wide, narrow = (FF_TILE_ROWS, 2 * FF_SHARD), (FF_TILE_ROWS, FF_SHARD)
    return _mm(hf, w13, mode="nn", out_dtype=F32, name=name, carried=carried, fused=dict(
        tile=wide, ins=[],
        outs=[(jax.ShapeDtypeStruct((SEQ, 2 * D_FF), F32), wide), (jax.ShapeDtypeStruct((SEQ, D_FF), BF16), narrow)],
        post=lambda p: (p, _swiglu_tile(p))))


def _ff_down_bwd(dx3, w2, ab, name, carried=()):
    wide, narrow = (FF_TILE_ROWS, 2 * FF_SHARD), (FF_TILE_ROWS, FF_SHARD)
    out, res = _mm(dx3, w2, mode="nt", out_dtype=BF16, name=name, carried=carried, fused=dict(
        tile=narrow, ins=[(ab, wide)], outs=[(jax.ShapeDtypeStruct((SEQ, 2 * D_FF), BF16), wide)],
        post=lambda du, ab_: (_swiglu_grad_tile(du, ab_),)))
    return out[0], res


CROSS_ROWS = 512


def _cross_fwd(qc, kvc, name, carried=()):
    def kern(q_ref, k_ref, v_ref, o_ref):
        s = _dot_nt(q_ref[...], k_ref[...]) * ATT_SCALE
        m = jnp.max(s, axis=-1, keepdims=True)
        e = jnp.exp(s - m)
        p = e / jnp.sum(e, axis=-1, keepdims=True)
        o_ref[...] = _dot(p, v_ref[...]).astype(o_ref.dtype)

    qblk = pl.BlockSpec((CROSS_ROWS, HEAD_DIM), lambda h, i: (i, h))
    return _pcall(
        kern, name=name, grid=(CROSS_HEADS, SEQ // CROSS_ROWS),
        in_specs=[qblk, pl.BlockSpec((MEM_LEN, HEAD_DIM), lambda h, i: (0, h)),
                  pl.BlockSpec((MEM_LEN, HEAD_DIM), lambda h, i: (0, CROSS_HEADS + h))],
        out_specs=qblk, out_shape=jax.ShapeDtypeStruct((SEQ, CROSS_WIDTH), BF16), args=(qc, kvc, kvc),
        semantics=("parallel", "parallel"), carried=carried)


def _cross_bwd(qc, kvc, doc, name):
    def kern(q_ref, k_ref, v_ref, do_ref, dq_ref, dk_ref, dv_ref):
        q, k, v, do = q_ref[...], k_ref[...], v_ref[...], do_ref[...]
        s = _dot_nt(q, k) * ATT_SCALE
        m = jnp.max(s, axis=-1, keepdims=True)
        e = jnp.exp(s - m)
        p = e / jnp.sum(e, axis=-1, keepdims=True)
        dp = _dot_nt(do, v)
        ds = p * (dp - jnp.sum(dp * p, axis=-1, keepdims=True))
        dq_ref[...] = (_dot(ds, k) * ATT_SCALE).astype(dq_ref.dtype)
        dk = _dot_tn(ds, q) * ATT_SCALE
        dv = _dot_tn(p, do)

        @pl.when(pl.program_id(1) == 0)
        def _():
            dk_ref[...] = dk
            dv_ref[...] = dv

        @pl.when(pl.program_id(1) > 0)
        def _():
            dk_ref[...] += dk
            dv_ref[...] += dv

    qblk = pl.BlockSpec((CROSS_ROWS, HEAD_DIM), lambda h, i: (i, h))
    kblk = pl.BlockSpec((MEM_LEN, HEAD_DIM), lambda h, i: (0, h))
    dq, dk, dv = pl.pallas_call(
        kern, name=name, grid=(CROSS_HEADS, SEQ // CROSS_ROWS),
        in_specs=[qblk, kblk, pl.BlockSpec((MEM_LEN, HEAD_DIM), lambda h, i: (0, CROSS_HEADS + h)), qblk],
        out_specs=[qblk, kblk, kblk],
        out_shape=[jax.ShapeDtypeStruct((SEQ, CROSS_WIDTH), BF16),
                   jax.ShapeDtypeStruct((MEM_LEN, CROSS_WIDTH), F32),
                   jax.ShapeDtypeStruct((MEM_LEN, CROSS_WIDTH), F32)],
        compiler_params=_cparams(("parallel", "arbitrary")),
    )(qc, kvc, kvc, doc)
    return dq, jnp.concatenate([dk, dv], axis=1)


FULL_SPECS = {
    "w_in": ("col", D_MODEL, IN_WIDTH),
    "w_branch_a": ("col", ATT_OUT, D_MODEL),
    "w_branch_b": ("col", HG_WIDTH, D_MODEL),
    "w_out": ("row", D_MODEL, D_MODEL),
    "wq_cross": ("row", D_MODEL, CROSS_WIDTH),
    "wkv_cross": ("row", D_MODEL, 2 * CROSS_WIDTH),
    "wo_cross": ("col", CROSS_WIDTH, D_MODEL),
    "w13": ("col", D_MODEL, 2 * D_FF),
    "w2": ("row", D_FF, D_MODEL),
}
WEIGHT_PLACE = {
    "w_in": ("w_in", 0), "w_branch_a": ("w_branch_a", 0), "w_branch_b": ("w_branch_b", 0),
    "w_out": ("w_out", 0), "wq_cross": ("wq_cross", 0), "wkv_cross": ("wkv_cross", 0),
    "wo_cross": ("wo_cross", 0), "w1": ("w13", 0), "w3": ("w13", FF_SHARD), "w2": ("w2", 0),
}
BIG_WEIGHTS = tuple(WEIGHT_PLACE)
EW_BLOCK_ELEMS = 512 * 1024


def _position():
    return lax.axis_index("x"), lax.axis_index("y"), lax.axis_index("c")


def _other_chips(x, y):
    return [(1 - x, y), (x, 1 - y), (1 - x, 1 - y)]


def _half(ref, kind, h):
    r, c = ref.shape
    if kind == "col":
        return ref.at[pl.ds(h * (r // 2), r // 2), :]
    return ref.at[:, pl.ds(h * (c // 2), c // 2)]


def _shard_of(ref, kind, start, size):
    return ref.at[:, pl.ds(start, size)] if kind == "col" else ref.at[pl.ds(start, size), :]


def _rows_of(ref, r0, nrows):
    return ref if nrows is None else ref.at[pl.ds(r0, nrows), :]


def _half_shape(kind, rows, cols):
    return (rows // 2, cols) if kind == "col" else (rows, cols // 2)


def _slot_shape(spec):
    kind, rows, cols = spec
    hr, hc = _half_shape(kind, rows, cols)
    return (hr, hc // N_CHIPS) if kind == "col" else (hr // N_CHIPS, hc)


def _remote(src, dst, send_sem, recv_sem, device):
    return pltpu.make_async_remote_copy(src_ref=src, dst_ref=dst, send_sem=send_sem, recv_sem=recv_sem,
                                        device_id=device, device_id_type=MESH)


def _gather_ici_comm(fulls, jobs, specs):
    def piece(refs, job, chip, c):
        f, r0, nr = job
        kind, rows, cols = specs[f]
        stride = (cols if kind == "col" else rows) // N_CHIPS
        return _rows_of(_half(_shard_of(refs[f], kind, chip * stride, stride), kind, c), r0, nr)

    def start(refs, ss, rs):
        x, y, c = _position()
        j = 2 * x + y
        for q, job in enumerate(jobs):
            for p, (px, py) in enumerate(_other_chips(x, y)):
                _remote(piece(refs, job, j, c), piece(refs, job, j, c), ss.at[3 * q + p], rs.at[3 * q + p],
                        (px, py, c)).start()

    def finish(refs, ss, rs):
        x, y, c = _position()
        j = 2 * x + y
        for q, job in enumerate(jobs):
            for p, (px, py) in enumerate(_other_chips(x, y)):
                _remote(piece(refs, job, j, c), piece(refs, job, 2 * px + py, c), ss.at[3 * q + p],
                        rs.at[3 * q + p], (px, py, c)).wait_recv()
        for q, job in enumerate(jobs):
            for p, (px, py) in enumerate(_other_chips(x, y)):
                _remote(piece(refs, job, j, c), piece(refs, job, j, c), ss.at[3 * q + p], rs.at[3 * q + p],
                        (px, py, c)).wait_send()

    names = list(dict.fromkeys(job[0] for job in jobs))
    return _Carried({f: fulls[f] for f in names}, {}, 3 * len(jobs), start, finish)


def _gather_ring_comm(fulls, f, r0, nr, phase, specs):
    kind, _, cols = specs[f]
    assert kind == "col" and nr % 32 == 0
    stride = cols // N_CHIPS
    half = nr // 2

    def rows(refs, chip, c, lo, n):
        return _rows_of(_half(_shard_of(refs[f], kind, chip * stride, stride), kind, c), r0 + lo, n)

    def copies(refs, ss, rs):
        x, y, c = _position()
        me, nx, ny, dg = 2 * x + y, 2 * (1 - x) + y, 2 * x + (1 - y), 2 * (1 - x) + (1 - y)
        to_x, to_y = (1 - x, y, c), (x, 1 - y, c)
        if phase == "a":
            mine = rows(refs, me, c, 0, nr)
            return [(_remote(mine, mine, ss.at[0], rs.at[0], to_x), rows(refs, nx, c, 0, nr)),
                    (_remote(mine, mine, ss.at[1], rs.at[1], to_y), rows(refs, ny, c, 0, nr))]
        up, low = rows(refs, ny, c, half, half), rows(refs, nx, c, 0, half)
        return [(_remote(up, up, ss.at[0], rs.at[0], to_x), rows(refs, dg, c, half, half)),
                (_remote(low, low, ss.at[1], rs.at[1], to_y), rows(refs, dg, c, 0, half))]

    def start(refs, ss, rs):
        for cp, _ in copies(refs, ss, rs):
            cp.start()

    def finish(refs, ss, rs):
        x, y, c = _position()
        mine = copies(refs, ss, rs)
        for i, (_, landing) in enumerate(mine):
            _remote(landing, landing, ss.at[i], rs.at[i], (x, y, c)).wait_recv()
        for cp, _ in mine:
            cp.wait_send()

    return _Carried({f: fulls[f]}, {}, 2, start, finish)


def _gather_d2d_comm(fulls, jobs, specs):
    def rect(refs, job, h):
        f, r0, nr = job
        assert nr is None or specs[f][0] == "col"
        return _rows_of(_half(refs[f], specs[f][0], h), r0, nr)

    def start(refs, ss, rs):
        x, y, c = _position()
        for q, job in enumerate(jobs):
            _remote(rect(refs, job, c), rect(refs, job, c), ss.at[q], rs.at[q], (x, y, 1 - c)).start()

    def finish(refs, ss, rs):
        x, y, c = _position()
        for q, job in enumerate(jobs):
            _remote(rect(refs, job, 1 - c), rect(refs, job, 1 - c), ss.at[q], rs.at[q], (x, y, 1 - c)).wait_recv()
        for q, job in enumerate(jobs):
            _remote(rect(refs, job, c), rect(refs, job, c), ss.at[q], rs.at[q], (x, y, 1 - c)).wait_send()

    names = list(dict.fromkeys(job[0] for job in jobs))
    return _Carried({f: fulls[f] for f in names}, {}, len(jobs), start, finish)


def _pairx_comm(grads, names, specs, whole=False, recv=None):
    def copies(refs, ss, rs):
        x, y, c = _position()
        src = (lambda f: refs[("g", f)]) if whole else (lambda f: _half(refs[("g", f)], specs[f][0], 1 - c))
        return [_remote(src(f), refs[("r", f)], ss.at[i], rs.at[i], (x, y, 1 - c)) for i, f in enumerate(names)]

    def start(refs, ss, rs):
        for cp in copies(refs, ss, rs):
            cp.start()

    def finish(refs, ss, rs):
        for cp in copies(refs, ss, rs):
            cp.wait_recv()
        for cp in copies(refs, ss, rs):
            cp.wait_send()

    reads = {("g", f): grads[f] for f in names}
    if recv is not None:
        return _Carried({("r", f): recv[f] for f in names}, {}, len(names), start, finish, reads=reads)
    fresh = {("r", f): jax.ShapeDtypeStruct(_half_shape(*specs[f]), BF16) for f in names}
    return _Carried({}, fresh, len(names), start, finish, reads=reads)


def _chipx_comm(pair_sums, slots, jobs, specs):
    def copies(refs, ss, rs):
        x, y, c = _position()
        out = []
        for q, (f, r0, nr) in enumerate(jobs):
            kind = specs[f][0]
            width = _slot_shape(specs[f])[1 if kind == "col" else 0]
            for p, (px, py) in enumerate(_other_chips(x, y)):
                src = _rows_of(_shard_of(refs[("p", f)], kind, (2 * px + py) * width, width), r0, nr)
                dst = _rows_of(refs[("s", f)].at[p], r0, nr)
                out.append(_remote(src, dst, ss.at[3 * q + p], rs.at[3 * q + p], (px, py, c)))
        return out

    def start(refs, ss, rs):
        for cp in copies(refs, ss, rs):
            cp.start()

    def finish(refs, ss, rs):
        for cp in copies(refs, ss, rs):
            cp.wait_recv()
        for cp in copies(refs, ss, rs):
            cp.wait_send()

    names = list(dict.fromkeys(job[0] for job in jobs))
    arrays = {("p", f): pair_sums[f] for f in names}
    arrays.update({("s", f): slots[f] for f in names})
    return _Carried(arrays, {}, 3 * len(jobs), start, finish)


def _share_comm(grads, wnames, specs, place):
    def start(refs, ss, rs):
        x, y, c = _position()
        for i, w in enumerate(wnames):
            kind = specs[place[w][0]][0]
            _remote(_half(refs[w], kind, c), _half(refs[w], kind, c), ss.at[i], rs.at[i], (x, y, 1 - c)).start()

    def finish(refs, ss, rs):
        x, y, c = _position()
        for i, w in enumerate(wnames):
            kind = specs[place[w][0]][0]
            _remote(_half(refs[w], kind, 1 - c), _half(refs[w], kind, 1 - c), ss.at[i], rs.at[i],
                    (x, y, 1 - c)).wait_recv()
        for i, w in enumerate(wnames):
            kind = specs[place[w][0]][0]
            _remote(_half(refs[w], kind, c), _half(refs[w], kind, c), ss.at[i], rs.at[i], (x, y, 1 - c)).wait_send()

    return _Carried({w: grads[w] for w in wnames}, {}, len(wnames), start, finish)


SMALL_IN, SMALL_ROWS_OF_ALL = ("small", "mine"), ("small", "all")


def _small_gather_comm(v):
    flips = [(fx, fy, fc) for fx in (0, 1) for fy in (0, 1) for fc in (0, 1)][1:]

    def copies(refs, ss, rs):
        x, y, c = _position()
        v_ref, out_ref = refs[SMALL_IN], refs[SMALL_ROWS_OF_ALL]
        me = 4 * x + 2 * y + c
        pairs = []
        for i, fl in enumerate(flips):
            px, py, pc = (1 - a if f else a for a, f in zip((x, y, c), fl))
            pairs.append((_remote(v_ref, out_ref.at[me], ss.at[i], rs.at[i], (px, py, pc)),
                          _remote(v_ref, out_ref.at[4 * px + 2 * py + pc], ss.at[i], rs.at[i], (px, py, pc))))
        return pairs, pltpu.make_async_copy(v_ref, out_ref.at[me], ss.at[N_DEV - 1])

    def start(refs, ss, rs):
        pairs, local = copies(refs, ss, rs)
        local.start()
        for mine, _ in pairs:
            mine.start()

    def finish(refs, ss, rs):
        pairs, local = copies(refs, ss, rs)
        for _, theirs in pairs:
            theirs.wait_recv()
        for mine, _ in pairs:
            mine.wait_send()
        local.wait()

    fresh = {SMALL_ROWS_OF_ALL: jax.ShapeDtypeStruct((N_DEV,) + v.shape, F32)}
    return _Carried({}, fresh, N_DEV, start, finish, reads={SMALL_IN: v})


def _ew_block(rows, cols, elems=EW_BLOCK_ELEMS):
    tc = cols if cols <= 4096 else _div(cols, 2048, LANE)
    tr = _div(rows, max(16, elems // tc), 16)
    return tr, tc


def _mesh_scalars():
    x, y, c = _position()
    return jnp.stack([c, 2 * x + y]).astype(jnp.int32)


def _grid_spec(grid, in_specs, out_specs):
    return pltpu.PrefetchScalarGridSpec(num_scalar_prefetch=1, grid=grid, in_specs=in_specs, out_specs=out_specs)


def _cast_into_full(parts, fname, pos, specs, place, name, token=None):
    kind, rows, cols = specs[fname]
    ws = [w for w in place if place[w][0] == fname]
    if kind == "col":
        stride = cols // N_CHIPS
        hr = rows // 2
        tr = _div(hr, max(16, EW_BLOCK_ELEMS // stride), 16)
        nrb = hr // tr
        in_specs = [pl.BlockSpec((tr, parts[w].shape[1]), lambda i, pos_ref: (i + pos_ref[0] * nrb, 0)) for w in ws]
        out_spec = pl.BlockSpec((tr, stride), lambda i, pos_ref: (i + pos_ref[0] * nrb, pos_ref[1]))
    else:
        stride = rows // N_CHIPS
        hc = cols // 2
        tr = _div(stride, max(16, EW_BLOCK_ELEMS // hc), 16)
        nrb = stride // tr
        in_specs = [pl.BlockSpec((tr, hc), lambda i, pos_ref: (i, pos_ref[0])) for w in ws]
        out_spec = pl.BlockSpec((tr, hc), lambda i, pos_ref: (i + pos_ref[1] * nrb, pos_ref[0]))

    def kern(pos_ref, *refs):
        o_ref = refs[-1]
        for w, r in zip(ws, refs[:len(ws)]):
            off = place[w][1] if kind == "col" else 0
            o_ref[:, off:off + r.shape[1]] = r[...].astype(o_ref.dtype)

    tokens = [] if token is None else [token]
    in_specs = in_specs + [pl.BlockSpec(TOKEN_SHAPE, lambda i, pos_ref: (0, 0))] * len(tokens)
    return pl.pallas_call(
        kern, name=name, grid_spec=_grid_spec((nrb,), in_specs, out_spec),
        out_shape=jax.ShapeDtypeStruct((rows, cols), BF16),
        compiler_params=_cparams(("parallel",)),
    )(pos, *[parts[w] for w in ws], *tokens)


def _pair_sum(grad, recv, pos, spec, name, whole=False):
    kind, rows, cols = spec
    hr, hc = _half_shape(kind, rows, cols)
    tr, tc = _ew_block(hr, hc, 2 * EW_BLOCK_ELEMS)
    nrb, ncb = hr // tr, hc // tc
    blk = pl.BlockSpec((tr, tc), lambda i, jj, pos_ref: (i, jj))
    if whole:
        mine = blk
    elif kind == "col":
        mine = pl.BlockSpec((tr, tc), lambda i, jj, pos_ref: (i + pos_ref[0] * nrb, jj))
    else:
        mine = pl.BlockSpec((tr, tc), lambda i, jj, pos_ref: (i, jj + pos_ref[0] * ncb))

    def kern(pos_ref, g_ref, r_ref, o_ref, slots_ref):
        o_ref[...] = (g_ref[...].astype(F32) + r_ref[...].astype(F32)).astype(o_ref.dtype)

    return pl.pallas_call(
        kern, name=name, grid_spec=_grid_spec((nrb, ncb), [mine, blk], [blk, ANY]),
        out_shape=[jax.ShapeDtypeStruct((hr, hc), BF16),
                   jax.ShapeDtypeStruct((N_CHIPS - 1,) + _slot_shape(spec), BF16)],
        compiler_params=_cparams(("parallel", "parallel")),
    )(pos, grad, recv)


def _chip_sum(pair_sum, slots, pos, fname, shard_shapes, specs, place, name):
    kind, rows, cols = specs[fname]
    sr, sc = _slot_shape(specs[fname])
    ws = [w for w in place if place[w][0] == fname]
    n_slots = N_CHIPS - 1
    tr = _div(sr, max(16, EW_BLOCK_ELEMS // sc), 16)
    nrb = sr // tr
    slot = pl.BlockSpec((n_slots, tr, sc), lambda i, pos_ref: (0, i, 0))
    if kind == "col":
        own = pl.BlockSpec((tr, sc), lambda i, pos_ref: (i, pos_ref[1]))
        out_specs = [pl.BlockSpec((tr, shard_shapes[w][1]), lambda i, pos_ref: (i + pos_ref[0] * nrb, 0)) for w in ws]
    else:
        own = pl.BlockSpec((tr, sc), lambda i, pos_ref: (i + pos_ref[1] * nrb, 0))
        out_specs = [pl.BlockSpec((tr, sc), lambda i, pos_ref: (i, pos_ref[0])) for w in ws]

    def kern(pos_ref, own_ref, slot_ref, *out_refs):
        tot = own_ref[...].astype(F32)
        for s in range(n_slots):
            tot = tot + slot_ref[s].astype(F32)
        for w, o_ref in zip(ws, out_refs):
            off = place[w][1] if kind == "col" else 0
            o_ref[...] = tot[:, off:off + o_ref.shape[1]]

    outs = pl.pallas_call(
        kern, name=name, grid_spec=_grid_spec((nrb,), [own, slot], out_specs),
        out_shape=[jax.ShapeDtypeStruct(shard_shapes[w], F32) for w in ws],
        compiler_params=_cparams(("parallel",)),
    )(pos, pair_sum, slots)
    return dict(zip(ws, outs))


def _adam_math(w, g, m, v):
    m2 = ADAM_B1 * m + (1.0 - ADAM_B1) * g
    v2 = ADAM_B2 * v + (1.0 - ADAM_B2) * (g * g)
    m_hat = m2 / (1.0 - ADAM_B1 ** ADAM_STEP)
    v_hat = v2 / (1.0 - ADAM_B2 ** ADAM_STEP)
    delta = -ADAM_LR * (m_hat / (jnp.sqrt(v_hat) + ADAM_EPS) + ADAM_WD * w)
    return delta, m2, v2


def _adamw(w, g, m, v, name, carried=()):
    rows, cols = w.shape
    tr, tc = _ew_block(rows, cols)

    def kern(w_ref, g_ref, m_ref, v_ref, d_ref, m2_ref, v2_ref, g_out_ref):
        g_ = g_ref[...]
        d_ref[...], m2_ref[...], v2_ref[...] = _adam_math(w_ref[...], g_, m_ref[...], v_ref[...])
        g_out_ref[...] = g_

    blk = pl.BlockSpec((tr, tc), lambda i, j: (i, j))
    return _pcall(
        kern, name=name, grid=(rows // tr, cols // tc), in_specs=[blk] * 4, out_specs=[blk] * 4,
        out_shape=[jax.ShapeDtypeStruct((rows, cols), F32)] * 4, args=(w, g, m, v),
        semantics=("parallel", "parallel"), carried=carried)


SMALL_ROWS = ("ln_mix_w", "ln_cross_w", "ln_mem_w", "ln_ffn_w", "ln_final_w")
ROW_HG_NORM, ROW_LB0, ROW_LB1 = 5, 6, 7
LOSS_LANE0 = HEAD_DIM


def _pack_small(vals):
    rows = [vals[n].reshape(1, D_MODEL) for n in SMALL_ROWS]
    pad = lambda a: jnp.pad(a, ((0, 0), (0, D_MODEL - a.shape[1])))
    rows.append(pad(vals["hg_norm_w"].reshape(1, HEAD_DIM)))
    rows.append(pad(vals["hg_lower_bounds"].reshape(2, HG_WIDTH)))
    return jnp.concatenate(rows, axis=0)


def _small_update(gathered, w, m, v, name="small_update"):
    def kern(g_ref, w_ref, m_ref, v_ref, grad_ref, d_ref, m2_ref, v2_ref, loss_ref):
        tot = g_ref[0]
        for s in range(1, N_DEV):
            tot = tot + g_ref[s]
        wv = w_ref[...]
        row = lax.broadcasted_iota(jnp.int32, (8, D_MODEL), 0)
        lane = lax.broadcasted_iota(jnp.int32, (8, D_MODEL), 1)
        l0, l1 = wv[ROW_LB0:ROW_LB0 + 1], wv[ROW_LB1:ROW_LB1 + 1]
        mx = jnp.maximum(l0, l1)
        e0, e1 = jnp.exp(l0 - mx), jnp.exp(l1 - mx)
        p0 = e0 / (e0 + e1)
        dlog = tot[ROW_LB0:ROW_LB0 + 1] * p0 * (1.0 - p0)
        tot = jnp.where(row == ROW_HG_NORM, tot + tot[ROW_LB1:ROW_LB1 + 1], tot)
        grad = jnp.where(row == ROW_LB0, dlog, jnp.where(row == ROW_LB1, -dlog, tot))
        grad = jnp.where((row == ROW_HG_NORM) & (lane >= HEAD_DIM), 0.0, grad)
        grad = jnp.where((row >= ROW_LB0) & (lane >= HG_WIDTH), 0.0, grad)
        grad_ref[...] = grad
        d_ref[...], m2_ref[...], v2_ref[...] = _adam_math(wv, grad, m_ref[...], v_ref[...])
        loss_ref[...] = tot[ROW_HG_NORM:ROW_HG_NORM + 1, LOSS_LANE0:LOSS_LANE0 + LANE]

    full = pl.BlockSpec((8, D_MODEL), lambda: (0, 0))
    return pl.pallas_call(
        kern, name=name,
        in_specs=[pl.BlockSpec((N_DEV, 8, D_MODEL), lambda: (0, 0, 0)), full, full, full],
        out_specs=[full, full, full, full, pl.BlockSpec((1, LANE), lambda: (0, 0))],
        out_shape=[jax.ShapeDtypeStruct((8, D_MODEL), F32)] * 4 + [jax.ShapeDtypeStruct((1, LANE), F32)],
        compiler_params=_cparams(),
    )(gathered, w, m, v)


def _unpack_small(p, shapes):
    out = {n: p[i].reshape(shapes[n]) for i, n in enumerate(SMALL_ROWS)}
    out["hg_norm_w"] = p[ROW_HG_NORM, :HEAD_DIM].reshape(shapes["hg_norm_w"])
    out["hg_lower_bounds"] = p[ROW_LB0:ROW_LB1 + 1, :HG_WIDTH].reshape(shapes["hg_lower_bounds"])
    return out


def _concat_cols(pieces, name):
    rows = pieces[0].shape[0]
    widths = [p.shape[1] for p in pieces]
    tr = ROW_BLOCK // 2

    def kern(*refs):
        o_ref, off = refs[-1], 0
        for r, w in zip(refs[:-1], widths):
            o_ref[:, off:off + w] = r[...]
            off += w

    return pl.pallas_call(
        kern, name=name, grid=(rows // tr,),
        in_specs=[pl.BlockSpec((tr, w), lambda i: (i, 0)) for w in widths],
        out_specs=pl.BlockSpec((tr, sum(widths)), lambda i: (i, 0)),
        out_shape=jax.ShapeDtypeStruct((rows, sum(widths)), pieces[0].dtype),
        compiler_params=_cparams(("parallel",)),
    )(*pieces)


WHOLE = lambda f: (f, 0, None)
MID_MATRICES = ("w_branch_a", "w_branch_b", "w_out", "wq_cross", "wkv_cross", "wo_cross")
MID_WEIGHTS = MID_MATRICES
W_IN_PIECES = [("w_in", r0, 512) for r0 in range(0, D_MODEL // 2, 512)]
W13_PIECES = [("w13", r0, 512) for r0 in range(0, D_MODEL // 2, 512)]
GATHER_GROUPS = [("mid", [WHOLE(f) for f in MID_MATRICES]), ("w13a", W13_PIECES[:1]), ("w13b", W13_PIECES[1:]),
                 ("w2", [WHOLE("w2")])]
OTHER_WEIGHTS = ["w1", "w3", "w2"] + list(MID_WEIGHTS)
BEFORE = {
    "hgrn_fwd": [("wait", "mid")],
    "mm_out": [("wait", "w13a")],
    "mm_o": [("wait", "w13b")],
    "mm_w2": [("wait", "w2"), ("run", ("d2d", [WHOLE("w2")]), "gather_hand_over_w2")],
    "mm_dwin_own": [("wait", "rs_w2"), ("chip_sum", "w2"), ("wait", "rs_w13"), ("chip_sum", "w13"),
                    ("wait", "pair_w_in")],
    "mm_dh": [("wait", "rs_mid")] + [("chip_sum", f) for f in MID_MATRICES],
}
CARRY = {
    "hgrn_fwd": [("d2d", [WHOLE(f) for f in MID_MATRICES])],
    "mm_out": [("d2d", W13_PIECES[:1])],
    "mm_o": [("d2d", W13_PIECES[1:])],
    "mm_du": [("pairx_whole", ["w2"])],
    "mm_dhf": [("pairx_whole", ["w13"])],
    "attn_bwd_g0": [("pairx", list(MID_MATRICES))],
    "mm_dh": [("share", OTHER_WEIGHTS)],
}
AFTER = {
    "mm_dw2_own": [("start", "rs_w2", [WHOLE("w2")])],
    "mm_dw13_own": [("start", "rs_w13", [WHOLE("w13")])],
    "attn_bwd_g0": [("pair_sum", f) for f in MID_MATRICES] + [("start", "rs_mid", [WHOLE(f) for f in MID_MATRICES])],
    "mm_dwin_sibling": [("start_pairx", "pair_w_in", ["w_in"])],
    "mm_dwin_own": [("start", "rs_w_in", [WHOLE("w_in")])],
}
FINISH = [
    ("adamw", OTHER_WEIGHTS), ("wait", "rs_w_in"), ("chip_sum", "w_in"), ("share_and_small", ["w_in"]),
    ("adamw", ["w_in"]),
]


class _Late:
    def __init__(self, read):
        self.read = read


class _Net:
    def __init__(self, full, pos=None, shard_shapes=None, comm=True, specs=FULL_SPECS, place=WEIGHT_PLACE):
        self.full, self.pos, self.shard_shapes, self.comm = dict(full), pos, shard_shapes, comm
        self.specs, self.place = specs, place
        self.gw, self.recv, self.psum, self.slots, self.grads = {}, {}, {}, {}, {}
        self.gw_sibling = {}
        self.pending, self.token, self.last = {}, None, None
        self.ahead = []

    def _make(self, kind, arg):
        if kind == "gather":
            return _gather_ici_comm(self.full, arg, self.specs)
        if kind == "ring":
            return _gather_ring_comm(self.full, *arg, self.specs)
        if kind == "d2d":
            return _gather_d2d_comm(self.full, arg, self.specs)
        if kind == "pairx":
            return _pairx_comm(self.gw, arg, self.specs)
        if kind == "pairx_whole":
            return _pairx_comm(self.gw_sibling, arg, self.specs, whole=True)
        if kind == "pairx_split":
            return _pairx_comm(self.gw_sibling, arg, self.specs, whole=True, recv=self.recv)
        if kind == "chipx":
            return _chipx_comm(self.psum, self.slots, arg, self.specs)
        assert kind == "share"
        return _share_comm(self.grads, arg, self.specs, self.place)

    def _store(self, kind, res):
        if kind in ("gather", "ring", "d2d"):
            self.full.update(res)
        elif kind in ("pairx", "pairx_whole", "pairx_split"):
            for (tag, f), a in res.items():
                (self.gw if tag == "g" else self.recv)[f] = a
        elif kind == "chipx":
            for (tag, f), a in res.items():
                (self.psum if tag == "p" else self.slots)[f] = a
        else:
            self.grads.update(res)

    def run_comm(self, item, name, extra=()):
        kind, arg = item
        res = _run_comm([self._make(kind, arg), *extra], name)
        self._store(kind, res[0])
        return res[1:]

    @staticmethod
    def _others(after, items):
        own = [a for cm in items for a in cm.arrays.values()]
        return [a for a in after if a is not None and all(a is not o for o in own)]

    def start(self, groups, kind, name):
        items = [self._make(kind, jobs) for _, jobs in groups]
        after = self._others([self.last], items)
        res, sems, token = _split_start(items, name, after=after[0] if after else None)
        for (group, jobs), r, s in zip(groups, res, sems):
            self._store(kind, r)
            self.pending[group] = (kind, jobs, s)
        self.token = self.last = token

    def wait(self, group, after=()):
        kind, jobs, sems = self.pending.pop(group)
        item = self._make(kind, jobs)
        res = _split_wait([item], [sems], self._others([self.last, *after], [item]), f"wait_{group}")[0]
        self._store(kind, res)

    def step(self, step):
        if step[0] == "wait":
            self.wait(step[1], after=self.ahead)
            self.ahead = []
        elif step[0] == "start":
            self.start([(step[1], step[2])], "chipx", f"start_{step[1]}")
        elif step[0] == "start_pairx":
            for f in step[2]:
                self.recv[f] = lax.empty(_half_shape(*self.specs[f]), BF16)
            self.start([(step[1], step[2])], "pairx_split", f"start_{step[1]}")
        elif step[0] == "pair_sum":
            f = step[1]
            self.psum[f], self.slots[f] = _pair_sum(self.gw[f], self.recv[f], self.pos, self.specs[f],
                                                    f"rs_pair_sum_{f}", whole=f in self.gw_sibling)
        elif step[0] == "chip_sum":
            f = step[1]
            sums = _chip_sum(self.psum[f], self.slots[f], self.pos, f, self.shard_shapes,
                             self.specs, self.place, f"rs_chip_sum_{f}")
            self.grads.update(sums)
            self.ahead += list(sums.values())
        else:
            assert step[0] == "run"
            self.run_comm(step[1], step[2])

    def call(self, fn, name, *args, grad_of=None, sibling_half=False, pair_sum_of=None, **kw):
        for step in (BEFORE.get(name, []) if self.comm else []):
            self.step(step)
        self.ahead = []
        late = lambda a: a.read() if isinstance(a, _Late) else a
        args = [late(a) for a in args]
        kw = {key: late(val) for key, val in kw.items()}
        items = CARRY.get(name, []) if self.comm else []
        carried = [self._make(k, a) for k, a in items]
        if self.token is not None:
            carried.append(_Token(self.token))
            self.token = None
        out, res = fn(*args, name=name, carried=carried, **kw)
        if grad_of is not None:
            (self.gw_sibling if sibling_half else self.gw)[grad_of] = out
        if pair_sum_of is not None:
            self.psum[pair_sum_of] = out
            self.slots[pair_sum_of] = lax.empty((N_CHIPS - 1,) + _slot_shape(self.specs[pair_sum_of]), BF16)
        self.last = jax.tree.leaves(out)[0]
        for (kind, _), r in zip(items, res):
            self._store(kind, r)
        for step in (AFTER.get(name, []) if self.comm else []):
            self.step(step)
        return out


def _local_step(net, x, h, mem, target, small, h_halves=None, core_row=None):
    full, call = net.full, net.call
    proj = call(_mm, "mm_proj", h, full["w_in"], mode="nn", out_dtype=F32)
    att = [call(_attn_fwd, f"attn_fwd_g{g}", proj, g) for g in range(3)]
    outs, lses = [a[0] for a in att], [a[1] for a in att]
    o_att = _attn_merge_fwd(outs, lses, "attn_merge")
    oraw, o_hg, states = call(_hg_fwd, "hgrn_fwd", proj, small["hg_lower_bounds"], small["hg_norm_w"])
    ya = call(_mm, "mm_branch_a", o_att, full["w_branch_a"], mode="nn", out_dtype=F32)
    yb, merged = call(_branch_b_gate, "mm_branch_b", o_hg, full["w_branch_b"], proj, ya)
    x1, hc = call(_residual_norm, "mm_out", merged, full["w_out"], x, small["ln_cross_w"])

    mn = _rms_fwd(mem, small["ln_mem_w"], "rms_mem")
    qc = call(_mm, "mm_q", hc, full["wq_cross"], mode="nn", out_dtype=F32)
    kvc = call(_mm, "mm_kv", mn, full["wkv_cross"], mode="nn", out_dtype=F32)
    oc = call(_cross_fwd, "cross_fwd", qc, kvc)
    x2, hf, *hf_halves = call(_residual_norm, "mm_o", oc, full["wo_cross"], x1, small["ln_ffn_w"], core_row=core_row)

    ab, u = call(_ff_up, "mm_w13", hf, full["w13"])
    x3 = call(_mm, "mm_w2", u, _Late(lambda: full["w2"]), mode="nn", out_dtype=F32, res=x2)

    dx3, dg_final, loss, dx3_bf16, *dx3_halves = _loss_head(x3, small["ln_final_w"], target, "loss_head", core_row)

    gs = {"ln_final_w": dg_final}
    if net.comm:
        call(_mm, "mm_dw2_sibling", u, dx3_halves[0], mode="tn", out_dtype=BF16, grad_of="w2", sibling_half=True)
        dab = call(_ff_down_bwd, "mm_du", dx3_bf16, full["w2"], ab)
        call(_mm, "mm_dw2_own", u, dx3_halves[1], mode="tn", out_dtype=BF16, res=_Late(lambda: net.recv["w2"]),
             pair_sum_of="w2")
        call(_mm, "mm_dw13_sibling", hf_halves[0], dab, mode="tn", out_dtype=BF16, grad_of="w13", sibling_half=True)
        dhf = call(_mm, "mm_dhf", dab, full["w13"], mode="nt", out_dtype=F32)
        call(_mm, "mm_dw13_own", hf_halves[1], dab, mode="tn", out_dtype=BF16, res=_Late(lambda: net.recv["w13"]),
             pair_sum_of="w13")
    else:
        call(_mm, "mm_dw2", u, dx3_bf16, mode="tn", out_dtype=BF16, grad_of="w2")
        dab = call(_ff_down_bwd, "mm_du", dx3_bf16, full["w2"], ab)
        call(_mm, "mm_dw13", hf, dab, mode="tn", out_dtype=BF16, grad_of="w13")
        dhf = call(_mm, "mm_dhf", dab, full["w13"], mode="nt", out_dtype=F32)
    dx2, gs["ln_ffn_w"], dx2_bf16 = _rms_bwd(x2, small["ln_ffn_w"], dhf, dx3, "rms_ffn_bwd", bf16_copy=True)
    doc = call(_mm, "mm_doc", dx2_bf16, full["wo_cross"], mode="nt", out_dtype=BF16)
    call(_mm, "mm_dwo", oc, dx2_bf16, mode="tn", out_dtype=BF16, grad_of="wo_cross")
    dqc, dkvc = _cross_bwd(qc, kvc, doc, "cross_bwd")
    call(_mm, "mm_dwq", hc, dqc, mode="tn", out_dtype=BF16, grad_of="wq_cross")
    dx1, gs["ln_cross_w"], dx1_bf16 = call(_norm_bwd_residual, "mm_dhc", dqc, full["wq_cross"], x1,
                                           small["ln_cross_w"], dx2)
    call(_mm, "mm_dwkv", mn, dkvc, mode="tn", out_dtype=BF16, grad_of="wkv_cross")
    dya, dyb, dga, dgb = call(_dmerged_gate_bwd, "mm_dmerged", dx1_bf16, full["w_out"], proj, ya, yb)
    call(_mm, "mm_dwout", merged, dx1_bf16, mode="tn", out_dtype=BF16, grad_of="w_out")
    call(_mm, "mm_dwa", o_att, dya, mode="tn", out_dtype=BF16, grad_of="w_branch_a")
    do_att = call(_mm, "mm_doatt", dya, full["w_branch_a"], mode="nt", out_dtype=F32)
    call(_mm, "mm_dwb", o_hg, dyb, mode="tn", out_dtype=BF16, grad_of="w_branch_b")
    do_hg = call(_mm, "mm_dohg", dyb, full["w_branch_b"], mode="nt", out_dtype=F32)
    dqh, dfh, dih, dgh, dlb, gs["hg_norm_w"] = call(
        _hg_bwd, "hgrn_bwd", proj, small["hg_lower_bounds"], small["hg_norm_w"], oraw, states, do_hg)
    gs["hg_lb"] = dlb
    do_gs, dl_gs = call(_attn_merge_bwd, "attn_merge_bwd", outs, lses, do_att)
    dqs, dks, dvs = zip(*[call(_attn_bwd, f"attn_bwd_g{g}", proj, g, lses[g], do_gs[g], dl_gs[g]) for g in range(3)])
    dproj = _concat_cols([*dqs, *dks, *dvs, dqh, dfh, dih, dgh, dga, dgb], "dproj_concat")
    if net.comm:
        h_sibling, h_own = h_halves
        call(_mm, "mm_dwin_sibling", h_sibling, dproj, mode="tn", out_dtype=BF16, grad_of="w_in", sibling_half=True)
    dmn = call(_mm, "mm_dmn", dkvc, full["wkv_cross"], mode="nt", out_dtype=F32)
    _, gs["ln_mem_w"] = _rms_bwd(mem, small["ln_mem_w"], dmn, None, "rms_mem_bwd")
    net.ahead.append(gs["ln_mem_w"])
    if net.comm:
        call(_mm, "mm_dwin_own", h_own, dproj, mode="tn", out_dtype=BF16, res=_Late(lambda: net.recv["w_in"]),
             pair_sum_of="w_in")
    else:
        call(_mm, "mm_dwin", h, dproj, mode="tn", out_dtype=BF16, grad_of="w_in")
    dh = call(_mm, "mm_dh", dproj, full["w_in"], mode="nt", out_dtype=F32)
    dx, gs["ln_mix_w"] = _rms_bwd(x, small["ln_mix_w"], dh, dx1, "rms_mix_bwd")
    return loss, dx, gs


WEIGHT_ORDER = ("ln_mix_w", "w_in", "hg_norm_w", "hg_lower_bounds", "w_branch_a", "w_branch_b", "w_out",
                "ln_cross_w", "ln_mem_w", "wq_cross", "wkv_cross", "wo_cross", "ln_ffn_w", "w1", "w3", "w2",
                "ln_final_w")


def kernel(x, mem, ln_mix_w, w_in, hg_norm_w, hg_lower_bounds, w_branch_a, w_branch_b, w_out, ln_cross_w, ln_mem_w, wq_cross, wkv_cross, wo_cross, ln_ffn_w, w1, w3, w2, ln_final_w, loss_target, m_ln_mix_w, m_w_in, m_hg_norm_w, m_hg_lower_bounds, m_w_branch_a, m_w_branch_b, m_w_out, m_ln_cross_w, m_ln_mem_w, m_wq_cross, m_wkv_cross, m_wo_cross, m_ln_ffn_w, m_w1, m_w3, m_w2, m_ln_final_w, v_ln_mix_w, v_w_in, v_hg_norm_w, v_hg_lower_bounds, v_w_branch_a, v_w_branch_b, v_w_out, v_ln_cross_w, v_ln_mem_w, v_wq_cross, v_wkv_cross, v_wo_cross, v_ln_ffn_w, v_w1, v_w3, v_w2, v_ln_final_w):
    args = dict(locals())
    w = {n: args[n] for n in WEIGHT_ORDER}
    m = {n: args["m_" + n] for n in WEIGHT_ORDER}
    v = {n: args["v_" + n] for n in WEIGHT_ORDER}
    shapes = {n: w[n].shape for n in WEIGHT_ORDER}
    mat = lambda a: a.reshape(a.shape[-2:])
    shard_shapes = {n: shapes[n][-2:] for n in BIG_WEIGHTS}

    pos = _mesh_scalars()

    def cast(f, token=None):
        return _cast_into_full({n: mat(w[n]) for n in BIG_WEIGHTS if WEIGHT_PLACE[n][0] == f}, f, pos,
                               FULL_SPECS, WEIGHT_PLACE, f"cast_{f}", token)

    net = _Net({"w_in": cast("w_in")}, pos, shard_shapes)
    net.start([(f"ring_a{i}", (*job, "a")) for i, job in enumerate(W_IN_PIECES)], "ring", "gather_start_w_in")
    rest = {f: cast(f, net.token) for f in FULL_SPECS if f != "w_in"}
    net.full.update(rest)
    small = {n: w[n].reshape(1, -1) for n in SMALL_ROWS}
    small["hg_norm_w"] = w["hg_norm_w"].reshape(1, HEAD_DIM)
    small["hg_lower_bounds"] = w["hg_lower_bounds"]
    x2d = x.reshape(SEQ, D_MODEL)
    h, *h_halves = _rms_fwd_halves(x2d, small["ln_mix_w"], pos, "rms_mix")
    for i, job in enumerate(W_IN_PIECES):
        net.wait(f"ring_a{i}", after=[*rest.values(), h] if i == 0 else ())
        net.start([(f"ring_b{i}", (*job, "b"))], "ring", f"gather_pass_on_w_in{i}")
    net.start(GATHER_GROUPS, "gather", "gather_start_rest")
    for i, job in enumerate(W_IN_PIECES):
        net.wait(f"ring_b{i}")
        net.run_comm(("d2d", [job]), f"gather_hand_over_w_in{i}")
    core_row = jnp.full((1, D_MODEL), lax.axis_index("c").astype(F32))
    loss, dx, gs = _local_step(net, x2d, h, mem.reshape(MEM_LEN, D_MODEL), loss_target.reshape(SEQ, D_MODEL), small,
                               h_halves, core_row)

    out_g, out_d, out_m, out_v = {}, {}, {}, {}
    net.last = dx
    for step in FINISH:
        if step[0] == "adamw":
            for n in step[1]:
                out_d[n], out_m[n], out_v[n], out_g[n] = net.call(_adamw, f"adamw_{n}", mat(w[n]), net.grads[n],
                                                                  mat(m[n]), mat(v[n]))
        elif step[0] == "wait":
            net.wait(step[1], after=list(out_d.values()))
        elif step[0] == "share_and_small":
            pad = lambda a: jnp.pad(a, ((0, 0), (0, D_MODEL - a.shape[1])))
            part = jnp.concatenate(
                [gs[n] for n in SMALL_ROWS]
                + [pad(jnp.concatenate([gs["hg_norm_w"][0], loss], axis=1)), pad(gs["hg_lb"]),
                   pad(gs["hg_norm_w"][1]) if len(gs["hg_norm_w"]) > 1 else jnp.zeros((1, D_MODEL), F32)], axis=0)
            rows = net.run_comm(("share", step[1]), "rs_sibling_share_and_gather_small",
                                extra=[_small_gather_comm(part)])[0][SMALL_ROWS_OF_ALL]
            sg, sd, sm, sv, loss_tot = _small_update(rows, _pack_small(w), _pack_small(m), _pack_small(v))
            for dst, packed in ((out_g, sg), (out_d, sd), (out_m, sm), (out_v, sv)):
                dst.update(_unpack_small(packed, shapes))
        else:
            net.step(step)

    result = [loss_tot[0, 0], dx.reshape(x.shape)]
    for group in (out_g, out_d, out_m, out_v):
        result += [group[n].reshape(shapes[n]) for n in WEIGHT_ORDER]
    return tuple(result)
```

```python
import math

import jax
import jax.numpy as jnp
from jax import lax
from jax.experimental import pallas as pl
from jax.experimental.pallas import tpu as pltpu

F32 = jnp.float32
BF16 = jnp.bfloat16
MESH = pl.DeviceIdType.MESH

D_MODEL = 2048
SEQ = 2048
HEAD_DIM = 128
MEM_LEN = 256
ATT_GROUPS = ((128, 1), (512, 4), (2048, 16))
ATT_HEADS = 4
ATT_WIDTH = 3 * ATT_HEADS * HEAD_DIM
ATT_OUT = ATT_HEADS * HEAD_DIM
ATT_BLOCK = 128
HG_HEADS = 8
HG_WIDTH = HG_HEADS * HEAD_DIM
HG_CHUNK = 64
IN_WIDTH = 3 * ATT_WIDTH + 4 * HG_WIDTH + 2 * D_MODEL
CROSS_HEADS = 4
CROSS_WIDTH = CROSS_HEADS * HEAD_DIM
D_FF = 5632
RMS_EPS = 1e-6
ADAM_LR = 0.001
ADAM_B1 = 0.9
ADAM_B2 = 0.999
ADAM_EPS = 1e-08
ADAM_WD = 0.01
ADAM_STEP = 10
N_CHIPS = 4
N_DEV = 8

VMEM_LIMIT_BYTES = 56 * 1024 * 1024
LANE = 128
MXU_WIDTH = 256
MM_TILE_CAP = 1536
TRANSPOSE_CHUNK = 512
ANY = pl.BlockSpec(memory_space=pl.ANY)


def _cparams(sem=None):
    return pltpu.CompilerParams(dimension_semantics=sem, vmem_limit_bytes=VMEM_LIMIT_BYTES)


def _div(n, cap, mult):
    best = None
    for d in range(mult, min(n, cap) + 1, mult):
        if n % d == 0:
            best = d
    assert best is not None, (n, cap, mult)
    return best


def _sigmoid(x):
    return 1.0 / (1.0 + jnp.exp(-x))


def _dot(a, b):
    return jnp.dot(a.astype(BF16), b.astype(BF16), preferred_element_type=F32)


def _dot_nt(a, b):
    return lax.dot_general(a.astype(BF16), b.astype(BF16), (((1,), (1,)), ((), ())),
                           preferred_element_type=F32)


def _dot_tn(a, b):
    return jnp.dot(a.astype(F32).T.astype(BF16), b.astype(BF16), preferred_element_type=F32)


def _dot_exact(a, b):
    return jnp.dot(a, b, precision=lax.Precision.HIGHEST, preferred_element_type=F32)


class _Carried:
    def __init__(self, arrays, fresh, n_sems, start, finish, mid=None, reads=None):
        self.arrays, self.fresh, self.n_sems, self.reads = arrays, fresh, n_sems, reads or {}
        self.start, self.mid, self.finish = start, mid, finish


class _Token:
    def __init__(self, array):
        self.array = array


TOKEN_SHAPE = (8, LANE)


def _carried_layout(carried):
    akeys = list(dict.fromkeys(k for cm in carried for k in cm.arrays))
    fkeys = [(ci, k) for ci, cm in enumerate(carried) for k in cm.fresh]
    arrays = [next(cm.arrays[k] for cm in carried if k in cm.arrays) for k in akeys]
    shapes = [jax.ShapeDtypeStruct(a.shape, a.dtype) for a in arrays] + [carried[ci].fresh[k] for ci, k in fkeys]
    sems = []
    for cm in carried:
        sems += [pltpu.SemaphoreType.DMA((cm.n_sems,)), pltpu.SemaphoreType.DMA((cm.n_sems,))]
    return akeys, fkeys, arrays, shapes, sems


def _carried_reads(carried):
    rkeys = list(dict.fromkeys(k for cm in carried for k in cm.reads))
    return rkeys, [next(cm.reads[k] for cm in carried if k in cm.reads) for k in rkeys]


def _carried_results(carried, akeys, fkeys, outs, rkeys=(), read_refs=()):
    shared = dict(zip(akeys, outs[:len(akeys)]))
    shared.update(zip(rkeys, read_refs))
    res = [{k: shared[k] for k in list(cm.arrays) + [r for r in cm.reads if r in shared]} for cm in carried]
    for (ci, k), o in zip(fkeys, outs[len(akeys):]):
        res[ci][k] = o
    return res


def _pcall(kern, *, name, grid, in_specs, out_specs, out_shape, args, scratch_shapes=(), semantics=None,
           carried=()):
    tokens = [c.array for c in carried if isinstance(c, _Token)]
    carried = [c for c in carried if not isinstance(c, _Token)]
    single = not isinstance(out_shape, (list, tuple))
    out_specs = [out_specs] if single else list(out_specs)
    out_shape = [out_shape] if single else list(out_shape)
    n_real, n_out, n_scr = len(in_specs), len(out_shape), len(scratch_shapes)
    in_specs = list(in_specs) + [pl.BlockSpec(TOKEN_SHAPE, lambda *_: (0, 0))] * len(tokens)
    args = list(args) + tokens
    n_in = len(in_specs)
    if not carried:
        def plain(*refs):
            kern(*refs[:n_real], *refs[n_in:])

        outs = pl.pallas_call(plain if tokens else kern, name=name, grid=grid, in_specs=in_specs,
                              out_specs=out_specs, out_shape=out_shape, scratch_shapes=list(scratch_shapes),
                              compiler_params=_cparams(semantics))(*args)
        return (outs[0] if single else list(outs)), []
    akeys, fkeys, arrays, shapes, sems = _carried_layout(carried)
    rkeys, reads = _carried_reads(carried)
    n_a, n_f, n_r = len(akeys), len(fkeys), len(rkeys)
    total = math.prod(grid)
    mid_step = min(total - 1, (17 * total) // 20)

    def wrapped(*refs):
        ins = refs[:n_real]
        r0 = n_in + n_a
        o0 = r0 + n_r
        outs = refs[o0:o0 + n_out]
        a0 = o0 + n_out
        s0 = a0 + n_a + n_f
        per = _carried_results(carried, akeys, fkeys, refs[a0:s0], rkeys, refs[r0:o0])
        scratch = refs[s0:s0 + n_scr]
        sem = refs[s0 + n_scr:]
        step = 0
        for d, g in enumerate(grid):
            step = step * g + pl.program_id(d)

        @pl.when(step == 0)
        def _():
            for ci, cm in enumerate(carried):
                cm.start(per[ci], sem[2 * ci], sem[2 * ci + 1])

        kern(*ins, *outs, *scratch)

        @pl.when(step == mid_step)
        def _():
            for ci, cm in enumerate(carried):
                if cm.mid is not None:
                    cm.mid(per[ci], sem[2 * ci], sem[2 * ci + 1])

        @pl.when(step == total - 1)
        def _():
            for ci, cm in enumerate(carried):
                cm.finish(per[ci], sem[2 * ci], sem[2 * ci + 1])

    outs = pl.pallas_call(
        wrapped, name=name, grid=grid,
        in_specs=list(in_specs) + [ANY] * (n_a + n_r), out_specs=out_specs + [ANY] * (n_a + n_f),
        out_shape=out_shape + shapes,
        input_output_aliases={n_in + i: n_out + i for i in range(n_a)},
        scratch_shapes=list(scratch_shapes) + sems,
        compiler_params=_cparams(("arbitrary",) * len(grid)),
    )(*args, *arrays, *reads)
    res = _carried_results(carried, akeys, fkeys, outs[n_out:])
    return (outs[0] if single else list(outs[:n_out])), res


def _run_comm(carried, name):
    carried = list(carried)
    akeys, fkeys, arrays, shapes, sems = _carried_layout(carried)
    rkeys, reads = _carried_reads(carried)
    n_a, n_f, n_r = len(akeys), len(fkeys), len(rkeys)

    def body(*refs):
        o0 = n_a + n_r
        per = _carried_results(carried, akeys, fkeys, refs[o0:o0 + n_a + n_f], rkeys, refs[n_a:o0])
        sem = refs[o0 + n_a + n_f:]
        for hook in ("start", "mid", "finish"):
            for ci, cm in enumerate(carried):
                fn = getattr(cm, hook)
                if fn is not None:
                    fn(per[ci], sem[2 * ci], sem[2 * ci + 1])

    outs = pl.pallas_call(
        body, name=name, in_specs=[ANY] * (n_a + n_r), out_specs=[ANY] * (n_a + n_f), out_shape=shapes,
        input_output_aliases={i: i for i in range(n_a)}, scratch_shapes=sems,
    )(*arrays, *reads)
    return _carried_results(carried, akeys, fkeys, outs)


HBM_SPEC = pl.BlockSpec(memory_space=pltpu.HBM)
SEM_SPEC = pl.BlockSpec(memory_space=pltpu.SEMAPHORE)
SPLIT_EFFECT = pltpu.SideEffectType.DATAFLOW_SIDE_EFFECTING


def _in_hbm(a):
    return pltpu.with_memory_space_constraint(a, pltpu.HBM)


def _split_start(items, name, after=None):
    items = list(items)
    akeys, fkeys, arrays, shapes, sems = _carried_layout(items)
    assert not fkeys
    rkeys, reads = _carried_reads(items)
    n_a, n_s = len(akeys), len(sems)
    n_after = n_a + (after is not None)
    n_in = n_after + len(rkeys)

    def body(*refs):
        per = _carried_results(items, akeys, [], refs[n_in:n_in + n_a], rkeys, refs[n_after:n_in])
        sem = refs[n_in + n_a:n_in + n_a + n_s]
        for ci, cm in enumerate(items):
            cm.start(per[ci], sem[2 * ci], sem[2 * ci + 1])
        token = refs[n_in + n_a + n_s]
        token[...] = jnp.zeros_like(token)

    outs = pl.pallas_call(
        body, name=name, in_specs=[HBM_SPEC] * n_a + [ANY] * (n_in - n_a),
        out_specs=[HBM_SPEC] * n_a + [SEM_SPEC] * n_s + [pl.BlockSpec(memory_space=pltpu.VMEM)],
        out_shape=[pltpu.HBM(s.shape, s.dtype) for s in shapes] + sems + [jax.ShapeDtypeStruct(TOKEN_SHAPE, F32)],
        input_output_aliases={i: i for i in range(n_a)},
        compiler_params=pltpu.CompilerParams(has_side_effects=SPLIT_EFFECT),
    )(*[_in_hbm(a) for a in arrays], *([after] if after is not None else []), *reads)
    res = _carried_results(items, akeys, [], outs[:n_a])
    sem_out = outs[n_a:n_a + n_s]
    return res, [(sem_out[2 * ci], sem_out[2 * ci + 1]) for ci in range(len(items))], outs[-1]


def _split_wait(items, sems, after, name):
    items = list(items)
    after = list(after) if isinstance(after, (list, tuple)) else [after]
    akeys, fkeys, arrays, shapes, _ = _carried_layout(items)
    rkeys, reads = _carried_reads(items)
    n_a, n_s = len(akeys), 2 * len(items)
    n_after = n_a + n_s + len(after)
    n_in = n_after + len(rkeys)

    def body(*refs):
        per = _carried_results(items, akeys, [], refs[n_in:], rkeys, refs[n_after:n_in])
        sem = refs[n_a:n_a + n_s]
        for ci, cm in enumerate(items):
            cm.finish(per[ci], sem[2 * ci], sem[2 * ci + 1])

    outs = pl.pallas_call(
        body, name=name, in_specs=[HBM_SPEC] * n_a + [SEM_SPEC] * n_s + [ANY] * (n_in - n_a - n_s),
        out_specs=[HBM_SPEC] * n_a, out_shape=[pltpu.HBM(s.shape, s.dtype) for s in shapes],
        input_output_aliases={i: i for i in range(n_a)},
        compiler_params=pltpu.CompilerParams(has_side_effects=SPLIT_EFFECT),
    )(*arrays, *[s for pair in sems for s in pair], *after, *reads)
    return _carried_results(items, akeys, [], outs)


def _mm(a, b, *, mode, out_dtype, name, res=None, carried=(), fused=None):
    if mode == "nn":
        (m, k), (k2, n) = a.shape, b.shape
    elif mode == "nt":
        (m, k), (n, k2) = a.shape, b.shape
    else:
        (k, m), (k2, n) = a.shape, b.shape
    assert k == k2, (name, a.shape, b.shape)
    tm = _div(m, MM_TILE_CAP, LANE)
    tn = _div(n, MM_TILE_CAP, MXU_WIDTH) if n % MXU_WIDTH == 0 else 0
    if tn < 1024:
        tn = _div(n, MM_TILE_CAP, LANE)
    out_shape = jax.ShapeDtypeStruct((m, n), out_dtype)

    if fused is not None:
        assert mode in ("nn", "nt") and res is None and k <= 2048
        tm, tn = fused["tile"]
        dot = _dot if mode == "nn" else _dot_nt
        n_x = len(fused["ins"])

        summed = [len(e) > 2 for e in fused["outs"]]

        def kern_fused(*refs):
            part = dot(refs[0][...], refs[1][...])
            outs = fused["post"](part, *[r[...] for r in refs[2:2 + n_x]])
            first_row_tile = pl.program_id(1) == 0
            for o_ref, val, acc in zip(refs[2 + n_x:], outs, summed):
                if not acc:
                    o_ref[...] = val.astype(o_ref.dtype)
                    continue

                @pl.when(first_row_tile)
                def _():
                    o_ref[...] = val

                @pl.when(jnp.logical_not(first_row_tile))
                def _():
                    o_ref[...] += val

        def tile_spec(shape, index=lambda i, j: (i, j)):
            return pl.BlockSpec(shape, lambda j, i: index(i, j))

        return _pcall(
            kern_fused, name=name, grid=(n // tn, m // tm),
            in_specs=[pl.BlockSpec((tm, k), lambda j, i: (i, 0)),
                      pl.BlockSpec((k, tn), lambda j, i: (0, j)) if mode == "nn"
                      else pl.BlockSpec((tn, k), lambda j, i: (j, 0))] + [tile_spec(*e[1:]) for e in fused["ins"]],
            out_specs=[tile_spec(*e[1:]) for e in fused["outs"]], out_shape=[e[0] for e in fused["outs"]],
            args=(a, b, *[e[0] for e in fused["ins"]]), semantics=("parallel", "arbitrary"), carried=carried)

    if mode == "tn":
        has_res_tn = res is not None

        def kern_tn(*refs):
            a_ref, b_ref, o_ref, at_ref = refs[0], refs[1], refs[-2], refs[-1]

            @pl.when(pl.program_id(1) == 0)
            def _():
                step = min(TRANSPOSE_CHUNK, k)
                for c0 in range(0, k, step):
                    at_ref[:, c0:c0 + step] = a_ref[c0:c0 + step, :].astype(F32).T.astype(BF16)

            part = jnp.dot(at_ref[...], b_ref[...].astype(BF16), preferred_element_type=F32)
            if has_res_tn:
                part = part + refs[2][...].astype(F32)
            o_ref[...] = part.astype(o_ref.dtype)

        o_spec = pl.BlockSpec((tm, tn), lambda i, j: (i, j))
        return _pcall(
            kern_tn, name=name, grid=(m // tm, n // tn),
            in_specs=[pl.BlockSpec((k, tm), lambda i, j: (0, i)),
                      pl.BlockSpec((k, tn), lambda i, j: (0, j))] + [o_spec] * has_res_tn,
            out_specs=o_spec, out_shape=out_shape, args=(a, b) + ((res,) if has_res_tn else ()),
            scratch_shapes=[pltpu.VMEM((tm, k), BF16)],
            semantics=("parallel", "arbitrary"), carried=carried)

    tk = k if k <= 2048 else _div(k, 3072, LANE)
    nk = k // tk
    a_spec = pl.BlockSpec((tm, tk), lambda i, j, kk: (i, kk))
    if mode == "nn":
        b_spec = pl.BlockSpec((tk, tn), lambda i, j, kk: (kk, j))
        dot = _dot
    else:
        b_spec = pl.BlockSpec((tn, tk), lambda i, j, kk: (j, kk))
        dot = _dot_nt
    o_spec = pl.BlockSpec((tm, tn), lambda i, j, kk: (i, j))
    in_specs = [a_spec, b_spec]
    args = [a, b]
    if res is not None:
        in_specs.append(o_spec)
        args.append(res)
    has_res = res is not None

    def kern(*refs):
        a_ref, b_ref = refs[0], refs[1]
        r_ref = refs[2] if has_res else None
        o_ref = refs[3] if has_res else refs[2]
        if nk == 1:
            part = dot(a_ref[...], b_ref[...])
            if has_res:
                part = part + r_ref[...]
            o_ref[...] = part.astype(o_ref.dtype)
            return
        @pl.when(pl.program_id(2) == 0)
        def _():
            o_ref[...] = r_ref[...] if has_res else jnp.zeros_like(o_ref)

        o_ref[...] += dot(a_ref[...], b_ref[...])

    assert nk == 1 or out_dtype == F32, name
    return _pcall(
        kern, name=name, grid=(m // tm, n // tn, nk),
        in_specs=in_specs, out_specs=o_spec, out_shape=out_shape, args=args,
        semantics=("parallel", "parallel", "arbitrary"), carried=carried)


ROW_BLOCK = 512


def _rms_fwd(x, g, name):
    t, d = x.shape
    tr = min(ROW_BLOCK, t)

    def kern(x_ref, g_ref, o_ref):
        xf = x_ref[...]
        r = lax.rsqrt(jnp.mean(xf * xf, axis=-1, keepdims=True) + RMS_EPS)
        o_ref[...] = (xf * r * g_ref[...]).astype(o_ref.dtype)

    return pl.pallas_call(
        kern, name=name, grid=(t // tr,),
        in_specs=[pl.BlockSpec((tr, d), lambda i: (i, 0)), pl.BlockSpec((1, d), lambda i: (0, 0))],
        out_specs=pl.BlockSpec((tr, d), lambda i: (i, 0)),
        out_shape=jax.ShapeDtypeStruct((t, d), BF16),
        compiler_params=_cparams(("parallel",)),
    )(x, g)


def _rms_fwd_halves(x, g, pos, name):
    t, d = x.shape
    tr = min(ROW_BLOCK, t)
    half = d // 2

    def kern(pos_ref, x_ref, g_ref, o_ref, sib_ref, own_ref):
        xf = x_ref[...]
        r = lax.rsqrt(jnp.mean(xf * xf, axis=-1, keepdims=True) + RMS_EPS)
        h = (xf * r * g_ref[...]).astype(o_ref.dtype)
        o_ref[...] = h
        is_south = pos_ref[0] == 0
        sib_ref[...] = jnp.where(is_south, h[:, half:], h[:, :half])
        own_ref[...] = jnp.where(is_south, h[:, :half], h[:, half:])

    row = pl.BlockSpec((tr, d), lambda i, pos_ref: (i, 0))
    part = pl.BlockSpec((tr, half), lambda i, pos_ref: (i, 0))
    return pl.pallas_call(
        kern, name=name,
        grid_spec=_grid_spec((t // tr,), [row, pl.BlockSpec((1, d), lambda i, pos_ref: (0, 0))], [row, part, part]),
        out_shape=[jax.ShapeDtypeStruct((t, d), BF16)] + [jax.ShapeDtypeStruct((t, half), BF16)] * 2,
        compiler_params=_cparams(("parallel",)),
    )(pos, x, g)


def _rms_bwd(x, g, dh, res, name, bf16_copy=False):
    t, d = x.shape
    tr = min(ROW_BLOCK, t)
    has_res = res is not None

    def kern(*refs):
        x_ref, g_ref, dh_ref = refs[:3]
        r_ref = refs[3] if has_res else None
        dx_ref, dg_ref = refs[3 + has_res], refs[4 + has_res]
        xf = x_ref[...]
        r = lax.rsqrt(jnp.mean(xf * xf, axis=-1, keepdims=True) + RMS_EPS)
        xn = xf * r
        dh_ = dh_ref[...]
        dhg = dh_ * g_ref[...]
        dx = r * (dhg - xn * jnp.mean(dhg * xn, axis=-1, keepdims=True))
        if has_res:
            dx = dx + r_ref[...]
        dx_ref[...] = dx
        if bf16_copy:
            refs[-1][...] = dx.astype(BF16)
        part = jnp.sum(dh_ * xn, axis=0, keepdims=True)

        @pl.when(pl.program_id(0) == 0)
        def _():
            dg_ref[...] = part

        @pl.when(pl.program_id(0) > 0)
        def _():
            dg_ref[...] += part

    row = pl.BlockSpec((tr, d), lambda i: (i, 0))
    vec = pl.BlockSpec((1, d), lambda i: (0, 0))
    in_specs = [row, vec, row] + ([row] if has_res else [])
    args = [x, g, dh] + ([res] if has_res else [])
    return pl.pallas_call(
        kern, name=name, grid=(t // tr,), in_specs=in_specs, out_specs=[row, vec] + [row] * bf16_copy,
        out_shape=[jax.ShapeDtypeStruct((t, d), F32), jax.ShapeDtypeStruct((1, d), F32)]
        + [jax.ShapeDtypeStruct((t, d), BF16)] * bf16_copy,
        compiler_params=_cparams(("arbitrary",)),
    )(*args)


def _loss_head(x3, g, target, name, core_row=None):
    t, d = x3.shape
    tr = ROW_BLOCK
    with_halves = core_row is not None

    def kern(x_ref, g_ref, t_ref, *rest):
        dx_ref, dg_ref, loss_ref, dxb_ref = rest[with_halves:with_halves + 4]
        xf = x_ref[...]
        r = lax.rsqrt(jnp.mean(xf * xf, axis=-1, keepdims=True) + RMS_EPS)
        xn = xf * r
        gg = g_ref[...]
        err = xn * gg - t_ref[...]
        lpart = 0.5 * jnp.sum(jnp.mean(err * err, axis=-1, keepdims=True), axis=0, keepdims=True)
        dy = err * (1.0 / d)
        dyg = dy * gg
        dx = r * (dyg - xn * jnp.mean(dyg * xn, axis=-1, keepdims=True))
        dx_ref[...] = dx
        dxb = dx.astype(BF16)
        dxb_ref[...] = dxb
        if with_halves:
            rest[-2][...], rest[-1][...] = _column_halves(dxb, rest[0][...])
        gpart = jnp.sum(dy * xn, axis=0, keepdims=True)
        lrow = jnp.broadcast_to(lpart, (1, LANE))

        @pl.when(pl.program_id(0) == 0)
        def _():
            dg_ref[...] = gpart
            loss_ref[...] = lrow

        @pl.when(pl.program_id(0) > 0)
        def _():
            dg_ref[...] += gpart
            loss_ref[...] += lrow

    row = pl.BlockSpec((tr, d), lambda i: (i, 0))
    vec = pl.BlockSpec((1, d), lambda i: (0, 0))
    part = pl.BlockSpec((tr, d // 2), lambda i: (i, 0))
    return pl.pallas_call(
        kern, name=name, grid=(t // tr,), in_specs=[row, vec, row] + [vec] * with_halves,
        out_specs=[row, vec, pl.BlockSpec((1, LANE), lambda i: (0, 0)), row] + [part] * (2 * with_halves),
        out_shape=[jax.ShapeDtypeStruct((t, d), F32), jax.ShapeDtypeStruct((1, d), F32),
                   jax.ShapeDtypeStruct((1, LANE), F32), jax.ShapeDtypeStruct((t, d), BF16)]
        + [jax.ShapeDtypeStruct((t, d // 2), BF16)] * (2 * with_halves),
        compiler_params=_cparams(("arbitrary",)),
    )(x3, g, target, *([core_row] if with_halves else []))


ATT_SCALE = HEAD_DIM ** -0.5
Q_BLOCK0, K_BLOCK0, V_BLOCK0 = 0, ATT_WIDTH // HEAD_DIM, 2 * ATT_WIDTH // HEAD_DIM


def _residue_rows(dil, r, n):
    if dil == 1:
        return pl.ds(n * ATT_BLOCK, ATT_BLOCK)
    return pl.ds(n * ATT_BLOCK * dil + r, ATT_BLOCK, stride=dil)


def _band_mask(with_prev):
    width = 2 * ATT_BLOCK if with_prev else ATT_BLOCK
    iq = lax.broadcasted_iota(jnp.int32, (ATT_BLOCK, width), 0)
    ik = lax.broadcasted_iota(jnp.int32, (ATT_BLOCK, width), 1)
    if not with_prev:
        return ik <= iq
    return ((ik < ATT_BLOCK) & (iq <= ik)) | ((ik >= ATT_BLOCK) & ((ik - ATT_BLOCK) <= iq))


def _band_keys(ref, dil, r, n):
    own = ref[_residue_rows(dil, r, n), :]
    if n == 0:
        return own
    return jnp.concatenate([ref[_residue_rows(dil, r, n - 1), :], own], axis=0)


def _attn_col_spec(base, grp):
    return pl.BlockSpec((SEQ, HEAD_DIM), lambda h: (0, base + grp * ATT_HEADS + h))


def _attn_fwd(proj, grp, name, carried=()):
    _, dil = ATT_GROUPS[grp]
    nb = SEQ // dil // ATT_BLOCK

    def kern(q_ref, k_ref, v_ref, o_ref, lse_ref):
        for r in range(dil):
            for n in range(nb):
                rows = _residue_rows(dil, r, n)
                s = _dot_nt(q_ref[rows, :], _band_keys(k_ref, dil, r, n)) * ATT_SCALE
                s = jnp.where(_band_mask(n > 0), s, -jnp.inf)
                m = jnp.max(s, axis=-1, keepdims=True)
                p = jnp.exp(s - m)
                l = jnp.sum(p, axis=-1, keepdims=True)
                o_ref[rows, :] = _dot(p / l, _band_keys(v_ref, dil, r, n))
                lse_ref[rows, :] = jnp.broadcast_to(m + jnp.log(l), (ATT_BLOCK, HEAD_DIM))

    out_spec = pl.BlockSpec((SEQ, HEAD_DIM), lambda h: (0, h))
    return _pcall(
        kern, name=name, grid=(ATT_HEADS,),
        in_specs=[_attn_col_spec(Q_BLOCK0, grp), _attn_col_spec(K_BLOCK0, grp), _attn_col_spec(V_BLOCK0, grp)],
        out_specs=[out_spec, out_spec],
        out_shape=[jax.ShapeDtypeStruct((SEQ, ATT_OUT), F32)] * 2, args=(proj, proj, proj),
        semantics=("parallel",), carried=carried)


def _attn_weights(l0, l1, l2):
    mx = jnp.maximum(jnp.maximum(l0, l1), l2)
    e0, e1, e2 = jnp.exp(l0 - mx), jnp.exp(l1 - mx), jnp.exp(l2 - mx)
    den = e0 + e1 + e2
    return e0 / den, e1 / den, e2 / den


def _attn_merge_fwd(outs, lses, name):
    tr = ROW_BLOCK

    def kern(o0, o1, o2, l0, l1, l2, out_ref):
        a0, a1, a2 = _attn_weights(l0[...], l1[...], l2[...])
        out_ref[...] = (a0 * o0[...] + a1 * o1[...] + a2 * o2[...]).astype(out_ref.dtype)

    spec = pl.BlockSpec((tr, ATT_OUT), lambda i: (i, 0))
    return pl.pallas_call(
        kern, name=name, grid=(SEQ // tr,), in_specs=[spec] * 6, out_specs=spec,
        out_shape=jax.ShapeDtypeStruct((SEQ, ATT_OUT), BF16),
        compiler_params=_cparams(("parallel",)),
    )(*outs, *lses)


def _attn_merge_bwd(outs, lses, do_att, name, carried=()):
    tr = ROW_BLOCK

    def kern(o0, o1, o2, l0, l1, l2, do_ref, d0, d1, d2, t0, t1, t2):
        alphas = _attn_weights(l0[...], l1[...], l2[...])
        do = do_ref[...]
        o_att = alphas[0] * o0[...] + alphas[1] * o1[...] + alphas[2] * o2[...]
        prod = do * o_att
        parts = []
        for h in range(ATT_HEADS):
            sl = slice(h * HEAD_DIM, (h + 1) * HEAD_DIM)
            tot = jnp.sum(prod[:, sl], axis=-1, keepdims=True)
            parts.append(jnp.broadcast_to(tot, (tr, HEAD_DIM)))
        dd = jnp.concatenate(parts, axis=1)
        for a, d_ref, t_ref in zip(alphas, (d0, d1, d2), (t0, t1, t2)):
            d_ref[...] = a * do
            t_ref[...] = -a * dd

    spec = pl.BlockSpec((tr, ATT_OUT), lambda i: (i, 0))
    res, cres = _pcall(
        kern, name=name, grid=(SEQ // tr,), in_specs=[spec] * 7, out_specs=[spec] * 6,
        out_shape=[jax.ShapeDtypeStruct((SEQ, ATT_OUT), F32)] * 6, args=(*outs, *lses, do_att),
        semantics=("parallel",), carried=carried)
    return (res[:3], res[3:]), cres


def _attn_bwd(proj, grp, lse, do_g, dl_g, name, carried=()):
    _, dil = ATT_GROUPS[grp]
    nb = SEQ // dil // ATT_BLOCK

    def kern(q_ref, k_ref, v_ref, do_ref, lse_ref, dl_ref, dq_ref, dk_ref, dv_ref, dq_acc, dk_acc, dv_acc):
        dk_acc[...] = jnp.zeros_like(dk_acc)
        dv_acc[...] = jnp.zeros_like(dv_acc)
        for r in range(dil):
            for n in range(nb):
                rows = _residue_rows(dil, r, n)
                q, do = q_ref[rows, :], do_ref[rows, :]
                kk, vv = _band_keys(k_ref, dil, r, n), _band_keys(v_ref, dil, r, n)
                s = _dot_nt(q, kk) * ATT_SCALE
                p = jnp.where(_band_mask(n > 0), jnp.exp(s - lse_ref[rows, :][:, :1]), 0.0)
                ds = p * (_dot_nt(do, vv) + dl_ref[rows, :][:, :1])
                dq_acc[rows, :] = _dot(ds, kk) * ATT_SCALE
                dk = _dot_tn(ds, q) * ATT_SCALE
                dv = _dot_tn(p, do)
                if n > 0:
                    prev = _residue_rows(dil, r, n - 1)
                    dk_acc[prev, :] += dk[:ATT_BLOCK]
                    dv_acc[prev, :] += dv[:ATT_BLOCK]
                    dk, dv = dk[ATT_BLOCK:], dv[ATT_BLOCK:]
                dk_acc[rows, :] += dk
                dv_acc[rows, :] += dv
        dq_ref[...] = dq_acc[...].astype(dq_ref.dtype)
        dk_ref[...] = dk_acc[...].astype(dk_ref.dtype)
        dv_ref[...] = dv_acc[...].astype(dv_ref.dtype)

    spec = pl.BlockSpec((SEQ, HEAD_DIM), lambda h: (0, h))
    return _pcall(
        kern, name=name, grid=(ATT_HEADS,),
        in_specs=[_attn_col_spec(Q_BLOCK0, grp), _attn_col_spec(K_BLOCK0, grp), _attn_col_spec(V_BLOCK0, grp),
                  spec, spec, spec],
        out_specs=[spec] * 3,
        out_shape=[jax.ShapeDtypeStruct((SEQ, ATT_OUT), BF16)] * 3, args=(proj, proj, proj, do_g, lse, dl_g),
        scratch_shapes=[pltpu.VMEM((SEQ, HEAD_DIM), F32)] * 3,
        semantics=("parallel",), carried=carried)


HG_HEADS_PER_STEP = 8
HG_BLOCK_W = 4 * HEAD_DIM
HG_BLOCKS = HG_HEADS_PER_STEP * HEAD_DIM // HG_BLOCK_W
HG_STEP_W = HG_HEADS_PER_STEP * HEAD_DIM
HG_Q_BLK = (3 * ATT_WIDTH) // HG_BLOCK_W
HG_N_CHUNKS = SEQ // HG_CHUNK
HG_MID = HG_CHUNK // 2


def _lower_bound(lb_ref, sl):
    l0, l1 = lb_ref[0:1, sl], lb_ref[1:2, sl]
    mx = jnp.maximum(l0, l1)
    e0, e1 = jnp.exp(l0 - mx), jnp.exp(l1 - mx)
    return e0 / (e0 + e1)


def _tri(lower):
    i = lax.broadcasted_iota(jnp.int32, (HG_CHUNK, HG_CHUNK), 0)
    j = lax.broadcasted_iota(jnp.int32, (HG_CHUNK, HG_CHUNK), 1)
    return (i >= j) if lower else (i <= j)


def _head_mean(x):
    parts = []
    for hd in range(x.shape[1] // HEAD_DIM):
        m = jnp.mean(x[:, hd * HEAD_DIM:(hd + 1) * HEAD_DIM], axis=-1, keepdims=True)
        parts.append(jnp.broadcast_to(m, (x.shape[0], HEAD_DIM)))
    return jnp.concatenate(parts, axis=1)


def _hg_chunk_terms(qh, fh, lb):
    sig = _sigmoid(fh)
    f = lb + (1.0 - lb) * sig
    k = 1.0 - f
    b = _dot_exact(_tri(True).astype(F32), jnp.log(f))
    bl = b[HG_CHUNK - 1:HG_CHUNK, :]
    br = b[HG_MID:HG_MID + 1, :]
    sq = _sigmoid(qh)
    q = qh * sq
    return dict(sig=sig, f=f, k=k, b=b, bl=bl, br=br, sq=sq, q=q,
                e1=jnp.exp(bl - b), e2=jnp.exp(b), e3=jnp.exp(b - br), e4=jnp.exp(br - b))


def _hg_fwd(proj, lbw, normw, name, carried=()):
    def in_blks(off):
        return [pl.BlockSpec((HG_CHUNK, HG_BLOCK_W), lambda hp, n, b=b: (n, HG_Q_BLK + off + hp * HG_BLOCKS + b))
                for b in range(HG_BLOCKS)]

    def kern(*refs):
        q_refs, f_refs, i_refs, g_refs = (refs[k * HG_BLOCKS:(k + 1) * HG_BLOCKS] for k in range(4))
        lb_ref, nw_ref, oraw_ref, ohg_ref, st_ref, state = refs[4 * HG_BLOCKS:]

        @pl.when(pl.program_id(1) == 0)
        def _():
            state[...] = jnp.zeros_like(state)

        causal = _tri(True)
        wide = lambda rs: jnp.concatenate([r[...] for r in rs], axis=1)
        t = _hg_chunk_terms(wide(q_refs), wide(f_refs), _lower_bound(lb_ref, slice(None)))
        v, gh = wide(i_refs), wide(g_refs)
        kd, qb, qr, kr = t["k"] * t["e1"], t["q"] * t["e2"], t["q"] * t["e3"], t["k"] * t["e4"]
        decay = jnp.exp(t["bl"])
        outs = []
        for hd in range(HG_HEADS_PER_STEP):
            sl = slice(hd * HEAD_DIM, (hd + 1) * HEAD_DIM)
            st = state[hd]
            st_ref[0, hd] = st
            a = jnp.where(causal, _dot_nt(qr[:, sl], kr[:, sl]), 0.0)
            outs.append(_dot_nt(qb[:, sl], st) + _dot(a, v[:, sl]))
            state[hd] = st * decay[:, sl] + _dot_tn(v[:, sl], kd[:, sl])
        o = jnp.concatenate(outs, axis=1)
        oraw_ref[...] = o
        r = lax.rsqrt(_head_mean(o * o) + RMS_EPS)
        nw = jnp.tile(nw_ref[...], (1, HG_HEADS_PER_STEP))
        ohg_ref[...] = (o * r * nw * (gh * _sigmoid(gh))).astype(ohg_ref.dtype)

    out_blk = pl.BlockSpec((HG_CHUNK, HG_STEP_W), lambda hp, n: (n, hp))
    return _pcall(
        kern, name=name, grid=(HG_HEADS // HG_HEADS_PER_STEP, HG_N_CHUNKS),
        in_specs=[*in_blks(0), *in_blks(2), *in_blks(4), *in_blks(6),
                  pl.BlockSpec((2, HG_STEP_W), lambda hp, n: (0, hp)),
                  pl.BlockSpec((1, HEAD_DIM), lambda hp, n: (0, 0))],
        out_specs=[out_blk, out_blk,
                   pl.BlockSpec((1, HG_HEADS_PER_STEP, HEAD_DIM, HEAD_DIM), lambda hp, n: (n, hp, 0, 0))],
        out_shape=[jax.ShapeDtypeStruct((SEQ, HG_WIDTH), F32), jax.ShapeDtypeStruct((SEQ, HG_WIDTH), BF16),
                   jax.ShapeDtypeStruct((HG_N_CHUNKS, HG_HEADS, HEAD_DIM, HEAD_DIM), F32)],
        args=(*[proj] * (4 * HG_BLOCKS), lbw, normw),
        scratch_shapes=[pltpu.VMEM((HG_HEADS_PER_STEP, HEAD_DIM, HEAD_DIM), F32)],
        semantics=("parallel", "arbitrary"), carried=carried)


def _hg_bwd(proj, lbw, normw, oraw, states, do_hg, name, carried=()):
    last = HG_N_CHUNKS - 1

    def in_blks(off):
        return [pl.BlockSpec((HG_CHUNK, HG_BLOCK_W),
                             lambda hp, n, b=b: (last - n, HG_Q_BLK + off + hp * HG_BLOCKS + b))
                for b in range(HG_BLOCKS)]

    blk = pl.BlockSpec((HG_CHUNK, HG_STEP_W), lambda hp, n: (last - n, hp))

    def kern(*refs):
        q_refs, f_refs, i_refs, g_refs = (refs[k * HG_BLOCKS:(k + 1) * HG_BLOCKS] for k in range(4))
        (lb_ref, nw_ref, oraw_ref, st_ref, do_ref, dq_ref, df_ref, di_ref, dg_ref, dlb_ref, dnw_ref,
         dstate) = refs[4 * HG_BLOCKS:]
        first = pl.program_id(1) == 0

        @pl.when(first)
        def _():
            dstate[...] = jnp.zeros_like(dstate)

        causal = _tri(True)
        wide = lambda rs: jnp.concatenate([r[...] for r in rs], axis=1)
        cat = lambda parts: jnp.concatenate(parts, axis=1)
        qh, fh, v, gh = wide(q_refs), wide(f_refs), wide(i_refs), wide(g_refs)
        o, dout = oraw_ref[...], do_ref[...]
        nw = jnp.tile(nw_ref[...], (1, HG_HEADS_PER_STEP))
        sgg = _sigmoid(gh)
        r = lax.rsqrt(_head_mean(o * o) + RMS_EPS)
        xn = o * r
        dg_ref[...] = (dout * xn * nw * (sgg * (1.0 + gh * (1.0 - sgg)))).astype(dg_ref.dtype)
        don = dout * (gh * sgg)
        dnw_wide = jnp.sum(don * xn, axis=0, keepdims=True)
        dnw_tot = dnw_wide[:, :HEAD_DIM]
        for hd in range(1, HG_HEADS_PER_STEP):
            dnw_tot = dnw_tot + dnw_wide[:, hd * HEAD_DIM:(hd + 1) * HEAD_DIM]
        tt = don * nw
        do = r * (tt - xn * _head_mean(tt * xn))
        lb = _lower_bound(lb_ref, slice(None))
        t = _hg_chunk_terms(qh, fh, lb)
        k, q = t["k"], t["q"]
        kd, qb, qr, kr = k * t["e1"], q * t["e2"], q * t["e3"], k * t["e4"]
        decay = jnp.exp(t["bl"])
        dqb, dqr, dkr, dkd, dv, ddecay = [], [], [], [], [], []
        for hd in range(HG_HEADS_PER_STEP):
            sl = slice(hd * HEAD_DIM, (hd + 1) * HEAD_DIM)
            st = st_ref[0, hd]
            dstn = dstate[hd]
            a = jnp.where(causal, _dot_nt(qr[:, sl], kr[:, sl]), 0.0)
            da = jnp.where(causal, _dot_nt(do[:, sl], v[:, sl]), 0.0)
            dqb.append(_dot(do[:, sl], st))
            dv.append(_dot_tn(a, do[:, sl]) + _dot_nt(kd[:, sl], dstn))
            dqr.append(_dot(da, kr[:, sl]))
            dkr.append(_dot_tn(da, qr[:, sl]))
            dkd.append(_dot(v[:, sl], dstn))
            ddecay.append(jnp.sum(dstn * st, axis=0, keepdims=True))
            dstate[hd] = dstn * decay[:, sl] + _dot_tn(do[:, sl], qb[:, sl])
        dqb, dqr, dkr, dkd, dv, ddecay = cat(dqb), cat(dqr), cat(dkr), cat(dkd), cat(dv), cat(ddecay)
        dq = dqb * t["e2"] + dqr * t["e3"]
        dk = dkd * t["e1"] + dkr * t["e4"]
        db = dqb * qb + dqr * qr - dkr * kr - dkd * kd
        dbl = jnp.sum(dkd * kd, axis=0, keepdims=True) + ddecay * decay
        dbr = jnp.sum(dkr * kr - dqr * qr, axis=0, keepdims=True)
        rows = lax.broadcasted_iota(jnp.int32, db.shape, 0)
        dlf = _dot_exact(_tri(False).astype(F32), db) + dbl + jnp.where(rows <= HG_MID, dbr, 0.0)
        df = dlf / t["f"] - dk
        sig, sq = t["sig"], t["sq"]
        df_ref[...] = (df * (1.0 - lb) * sig * (1.0 - sig)).astype(df_ref.dtype)
        dlb_row = jnp.sum(df * (1.0 - sig), axis=0, keepdims=True)
        dq_ref[...] = (dq * (sq * (1.0 + qh * (1.0 - sq)))).astype(dq_ref.dtype)
        di_ref[...] = dv.astype(di_ref.dtype)
        dnw_blk = jnp.broadcast_to(dnw_tot, (8, HEAD_DIM))

        @pl.when(first)
        def _():
            dlb_ref[...] = dlb_row
            dnw_ref[...] = dnw_blk

        @pl.when(jnp.logical_not(first))
        def _():
            dlb_ref[...] += dlb_row
            dnw_ref[...] += dnw_blk

    n_hp = HG_HEADS // HG_HEADS_PER_STEP
    outs, cres = _pcall(
        kern, name=name, grid=(n_hp, HG_N_CHUNKS),
        in_specs=[*in_blks(0), *in_blks(2), *in_blks(4), *in_blks(6),
                  pl.BlockSpec((2, HG_STEP_W), lambda hp, n: (0, hp)),
                  pl.BlockSpec((1, HEAD_DIM), lambda hp, n: (0, 0)),
                  blk,
                  pl.BlockSpec((1, HG_HEADS_PER_STEP, HEAD_DIM, HEAD_DIM), lambda hp, n: (last - n, hp, 0, 0)),
                  blk],
        out_specs=[blk, blk, blk, blk,
                   pl.BlockSpec((1, HG_STEP_W), lambda hp, n: (0, hp)),
                   pl.BlockSpec((8, HEAD_DIM), lambda hp, n: (hp, 0))],
        out_shape=[jax.ShapeDtypeStruct((SEQ, HG_WIDTH), BF16)] * 4
        + [jax.ShapeDtypeStruct((1, HG_WIDTH), F32), jax.ShapeDtypeStruct((8 * n_hp, HEAD_DIM), F32)],
        args=(*[proj] * (4 * HG_BLOCKS), lbw, normw, oraw, states, do_hg),
        scratch_shapes=[pltpu.VMEM((HG_HEADS_PER_STEP, HEAD_DIM, HEAD_DIM), F32)],
        semantics=("parallel", "arbitrary"), carried=carried)
    dqh, dfh, dih, dgh, dlb, dnw = outs
    return (dqh, dfh, dih, dgh, dlb, [dnw[8 * i:8 * i + 1] for i in range(n_hp)]), cres


GATE_BLOCK_W = 512
GATE_A_BLK = (3 * ATT_WIDTH + 4 * HG_WIDTH) // GATE_BLOCK_W
GATE_B_BLK = GATE_A_BLK + D_MODEL // GATE_BLOCK_W


GATE_TILE = (1024, GATE_BLOCK_W)


def _gate_ins(proj, ya, yb=None):
    ins = [(proj, GATE_TILE, lambda i, j: (i, GATE_A_BLK + j)), (proj, GATE_TILE, lambda i, j: (i, GATE_B_BLK + j)),
           (ya, GATE_TILE)]
    return ins + ([(yb, GATE_TILE)] if yb is not None else [])


def _branch_b_gate(o_hg, w_b, proj, ya, name, carried=()):
    def post(yb, ga, gb, ya_):
        return yb, _sigmoid(ga) * ya_ + _sigmoid(gb) * yb

    return _mm(o_hg, w_b, mode="nn", out_dtype=F32, name=name, carried=carried, fused=dict(
        tile=GATE_TILE, ins=_gate_ins(proj, ya), post=post,
        outs=[(jax.ShapeDtypeStruct((SEQ, D_MODEL), F32), GATE_TILE), (jax.ShapeDtypeStruct((SEQ, D_MODEL), BF16), GATE_TILE)]))


def _dmerged_gate_bwd(dx1, w_out, proj, ya, yb, name, carried=()):
    def post(dm, ga, gb, ya_, yb_):
        sa, sb = _sigmoid(ga), _sigmoid(gb)
        return dm * sa, dm * sb, dm * ya_ * sa * (1.0 - sa), dm * yb_ * sb * (1.0 - sb)

    return _mm(dx1, w_out, mode="nt", out_dtype=BF16, name=name, carried=carried, fused=dict(
        tile=GATE_TILE, ins=_gate_ins(proj, ya, yb), post=post,
        outs=[(jax.ShapeDtypeStruct((SEQ, D_MODEL), BF16), GATE_TILE)] * 4))


NORM_TILE = (512, D_MODEL)


def _column_halves(v, core_row):
    half = v.shape[1] // 2
    south = core_row[:, :half] < 0.5
    lo, hi = v[:, :half], v[:, half:]
    return jnp.where(south, hi, lo), jnp.where(south, lo, hi)


def _residual_norm(a, w, x, g, name, carried=(), core_row=None):
    vec = ((1, D_MODEL), lambda i, j: (0, j))
    half_tile = (NORM_TILE[0], D_MODEL // 2)

    def post(part, x_, g_, *core):
        xn = part + x_
        r = lax.rsqrt(jnp.mean(xn * xn, axis=-1, keepdims=True) + RMS_EPS)
        hn = (xn * r * g_).astype(BF16)
        return (xn, hn) + (_column_halves(hn, core[0]) if core else ())

    wide = [(jax.ShapeDtypeStruct((SEQ, D_MODEL), F32), NORM_TILE), (jax.ShapeDtypeStruct((SEQ, D_MODEL), BF16), NORM_TILE)]
    halves = [(jax.ShapeDtypeStruct((SEQ, D_MODEL // 2), BF16), half_tile)] * 2
    return _mm(a, w, mode="nn", out_dtype=F32, name=name, carried=carried, fused=dict(
        tile=NORM_TILE, post=post, ins=[(x, NORM_TILE), (g, *vec)] + ([(core_row, *vec)] if core_row is not None else []),
        outs=wide + (halves if core_row is not None else [])))


def _norm_bwd_residual(dq, w, x, g, res, name, carried=()):
    def post(dh, x_, g_, res_):
        r = lax.rsqrt(jnp.mean(x_ * x_, axis=-1, keepdims=True) + RMS_EPS)
        xn = x_ * r
        dhg = dh * g_
        dx = r * (dhg - xn * jnp.mean(dhg * xn, axis=-1, keepdims=True)) + res_
        return dx, jnp.sum(dh * xn, axis=0, keepdims=True), dx

    vec = ((1, D_MODEL), lambda i, j: (0, j))
    return _mm(dq, w, mode="nt", out_dtype=F32, name=name, carried=carried, fused=dict(
        tile=NORM_TILE, ins=[(x, NORM_TILE), (g, *vec), (res, NORM_TILE)], post=post,
        outs=[(jax.ShapeDtypeStruct((SEQ, D_MODEL), F32), NORM_TILE), (jax.ShapeDtypeStruct((1, D_MODEL), F32), *vec),
              (jax.ShapeDtypeStruct((SEQ, D_MODEL), BF16), NORM_TILE)]))


FF_SHARD = D_FF // N_CHIPS


FF_TILE_ROWS = 512


def _swiglu_tile(ab):
    a, b = ab[:, :FF_SHARD], ab[:, FF_SHARD:]
    return a * _sigmoid(a) * b


def _swiglu_grad_tile(du, ab):
    a, b = ab[:, :FF_SHARD], ab[:, FF_SHARD:]
    sg = _sigmoid(a)
    return jnp.concatenate([du * b * (sg * (1.0 + a * (1.0 - sg))), du * (a * sg)], axis=1)


def _ff_up(hf, w13, name, carried=()):
    wide, narrow = (FF_TILE_ROWS, 2 * FF_SHARD), (FF_TILE_ROWS, FF_SHARD)
    return _mm(hf, w13, mode="nn", out_dtype=F32, name=name, carried=carried, fused=dict(
        tile=wide, ins=[],
        outs=[(jax.ShapeDtypeStruct((SEQ, 2 * D_FF), F32), wide), (jax.ShapeDtypeStruct((SEQ, D_FF), BF16), narrow)],
        post=lambda p: (p, _swiglu_tile(p))))


def _ff_down_bwd(dx3, w2, ab, name, carried=()):
    wide, narrow = (FF_TILE_ROWS, 2 * FF_SHARD), (FF_TILE_ROWS, FF_SHARD)
    out, res = _mm(dx3, w2, mode="nt", out_dtype=BF16, name=name, carried=carried, fused=dict(
        tile=narrow, ins=[(ab, wide)], outs=[(jax.ShapeDtypeStruct((SEQ, 2 * D_FF), BF16), wide)],
        post=lambda du, ab_: (_swiglu_grad_tile(du, ab_),)))
    return out[0], res


CROSS_ROWS = 512


def _cross_fwd(qc, kvc, name, carried=()):
    def kern(q_ref, k_ref, v_ref, o_ref):
        s = _dot_nt(q_ref[...], k_ref[...]) * ATT_SCALE
        m = jnp.max(s, axis=-1, keepdims=True)
        e = jnp.exp(s - m)
        p = e / jnp.sum(e, axis=-1, keepdims=True)
        o_ref[...] = _dot(p, v_ref[...]).astype(o_ref.dtype)

    qblk = pl.BlockSpec((CROSS_ROWS, HEAD_DIM), lambda h, i: (i, h))
    return _pcall(
        kern, name=name, grid=(CROSS_HEADS, SEQ // CROSS_ROWS),
        in_specs=[qblk, pl.BlockSpec((MEM_LEN, HEAD_DIM), lambda h, i: (0, h)),
                  pl.BlockSpec((MEM_LEN, HEAD_DIM), lambda h, i: (0, CROSS_HEADS + h))],
        out_specs=qblk, out_shape=jax.ShapeDtypeStruct((SEQ, CROSS_WIDTH), BF16), args=(qc, kvc, kvc),
        semantics=("parallel", "parallel"), carried=carried)


def _cross_bwd(qc, kvc, doc, name):
    def kern(q_ref, k_ref, v_ref, do_ref, dq_ref, dk_ref, dv_ref):
        q, k, v, do = q_ref[...], k_ref[...], v_ref[...], do_ref[...]
        s = _dot_nt(q, k) * ATT_SCALE
        m = jnp.max(s, axis=-1, keepdims=True)
        e = jnp.exp(s - m)
        p = e / jnp.sum(e, axis=-1, keepdims=True)
        dp = _dot_nt(do, v)
        ds = p * (dp - jnp.sum(dp * p, axis=-1, keepdims=True))
        dq_ref[...] = (_dot(ds, k) * ATT_SCALE).astype(dq_ref.dtype)
        dk = _dot_tn(ds, q) * ATT_SCALE
        dv = _dot_tn(p, do)

        @pl.when(pl.program_id(1) == 0)
        def _():
            dk_ref[...] = dk
            dv_ref[...] = dv

        @pl.when(pl.program_id(1) > 0)
        def _():
            dk_ref[...] += dk
            dv_ref[...] += dv

    qblk = pl.BlockSpec((CROSS_ROWS, HEAD_DIM), lambda h, i: (i, h))
    kblk = pl.BlockSpec((MEM_LEN, HEAD_DIM), lambda h, i: (0, h))
    dq, dk, dv = pl.pallas_call(
        kern, name=name, grid=(CROSS_HEADS, SEQ // CROSS_ROWS),
        in_specs=[qblk, kblk, pl.BlockSpec((MEM_LEN, HEAD_DIM), lambda h, i: (0, CROSS_HEADS + h)), qblk],
        out_specs=[qblk, kblk, kblk],
        out_shape=[jax.ShapeDtypeStruct((SEQ, CROSS_WIDTH), BF16),
                   jax.ShapeDtypeStruct((MEM_LEN, CROSS_WIDTH), F32),
                   jax.ShapeDtypeStruct((MEM_LEN, CROSS_WIDTH), F32)],
        compiler_params=_cparams(("parallel", "arbitrary")),
    )(qc, kvc, kvc, doc)
    return dq, jnp.concatenate([dk, dv], axis=1)


FULL_SPECS = {
    "w_in": ("col", D_MODEL, IN_WIDTH),
    "w_branch_a": ("col", ATT_OUT, D_MODEL),
    "w_branch_b": ("col", HG_WIDTH, D_MODEL),
    "w_out": ("row", D_MODEL, D_MODEL),
    "wq_cross": ("row", D_MODEL, CROSS_WIDTH),
    "wkv_cross": ("row", D_MODEL, 2 * CROSS_WIDTH),
    "wo_cross": ("col", CROSS_WIDTH, D_MODEL),
    "w13": ("col", D_MODEL, 2 * D_FF),
    "w2": ("row", D_FF, D_MODEL),
}
WEIGHT_PLACE = {
    "w_in": ("w_in", 0), "w_branch_a": ("w_branch_a", 0), "w_branch_b": ("w_branch_b", 0),
    "w_out": ("w_out", 0), "wq_cross": ("wq_cross", 0), "wkv_cross": ("wkv_cross", 0),
    "wo_cross": ("wo_cross", 0), "w1": ("w13", 0), "w3": ("w13", FF_SHARD), "w2": ("w2", 0),
}
BIG_WEIGHTS = tuple(WEIGHT_PLACE)
EW_BLOCK_ELEMS = 512 * 1024


def _position():
    return lax.axis_index("x"), lax.axis_index("y"), lax.axis_index("c")


def _other_chips(x, y):
    return [(1 - x, y), (x, 1 - y), (1 - x, 1 - y)]


def _half(ref, kind, h):
    r, c = ref.shape
    if kind == "col":
        return ref.at[pl.ds(h * (r // 2), r // 2), :]
    return ref.at[:, pl.ds(h * (c // 2), c // 2)]


def _shard_of(ref, kind, start, size):
    return ref.at[:, pl.ds(start, size)] if kind == "col" else ref.at[pl.ds(start, size), :]


def _rows_of(ref, r0, nrows):
    return ref if nrows is None else ref.at[pl.ds(r0, nrows), :]


def _half_shape(kind, rows, cols):
    return (rows // 2, cols) if kind == "col" else (rows, cols // 2)


def _slot_shape(spec):
    kind, rows, cols = spec
    hr, hc = _half_shape(kind, rows, cols)
    return (hr, hc // N_CHIPS) if kind == "col" else (hr // N_CHIPS, hc)


def _remote(src, dst, send_sem, recv_sem, device):
    return pltpu.make_async_remote_copy(src_ref=src, dst_ref=dst, send_sem=send_sem, recv_sem=recv_sem,
                                        device_id=device, device_id_type=MESH)


def _gather_ici_comm(fulls, jobs, specs):
    def piece(refs, job, chip, c):
        f, r0, nr = job
        kind, rows, cols = specs[f]
        stride = (cols if kind == "col" else rows) // N_CHIPS
        return _rows_of(_half(_shard_of(refs[f], kind, chip * stride, stride), kind, c), r0, nr)

    def start(refs, ss, rs):
        x, y, c = _position()
        j = 2 * x + y
        for q, job in enumerate(jobs):
            for p, (px, py) in enumerate(_other_chips(x, y)):
                _remote(piece(refs, job, j, c), piece(refs, job, j, c), ss.at[3 * q + p], rs.at[3 * q + p],
                        (px, py, c)).start()

    def finish(refs, ss, rs):
        x, y, c = _position()
        j = 2 * x + y
        for q, job in enumerate(jobs):
            for p, (px, py) in enumerate(_other_chips(x, y)):
                _remote(piece(refs, job, j, c), piece(refs, job, 2 * px + py, c), ss.at[3 * q + p],
                        rs.at[3 * q + p], (px, py, c)).wait_recv()
        for q, job in enumerate(jobs):
            for p, (px, py) in enumerate(_other_chips(x, y)):
                _remote(piece(refs, job, j, c), piece(refs, job, j, c), ss.at[3 * q + p], rs.at[3 * q + p],
                        (px, py, c)).wait_send()

    names = list(dict.fromkeys(job[0] for job in jobs))
    return _Carried({f: fulls[f] for f in names}, {}, 3 * len(jobs), start, finish)


def _gather_ring_comm(fulls, f, r0, nr, phase, specs):
    kind, _, cols = specs[f]
    assert kind == "col" and nr % 32 == 0
    stride = cols // N_CHIPS
    half = nr // 2

    def rows(refs, chip, c, lo, n):
        return _rows_of(_half(_shard_of(refs[f], kind, chip * stride, stride), kind, c), r0 + lo, n)

    def copies(refs, ss, rs):
        x, y, c = _position()
        me, nx, ny, dg = 2 * x + y, 2 * (1 - x) + y, 2 * x + (1 - y), 2 * (1 - x) + (1 - y)
        to_x, to_y = (1 - x, y, c), (x, 1 - y, c)
        if phase == "a":
            mine = rows(refs, me, c, 0, nr)
            return [(_remote(mine, mine, ss.at[0], rs.at[0], to_x), rows(refs, nx, c, 0, nr)),
                    (_remote(mine, mine, ss.at[1], rs.at[1], to_y), rows(refs, ny, c, 0, nr))]
        up, low = rows(refs, ny, c, half, half), rows(refs, nx, c, 0, half)
        return [(_remote(up, up, ss.at[0], rs.at[0], to_x), rows(refs, dg, c, half, half)),
                (_remote(low, low, ss.at[1], rs.at[1], to_y), rows(refs, dg, c, 0, half))]

    def start(refs, ss, rs):
        for cp, _ in copies(refs, ss, rs):
            cp.start()

    def finish(refs, ss, rs):
        x, y, c = _position()
        mine = copies(refs, ss, rs)
        for i, (_, landing) in enumerate(mine):
            _remote(landing, landing, ss.at[i], rs.at[i], (x, y, c)).wait_recv()
        for cp, _ in mine:
            cp.wait_send()

    return _Carried({f: fulls[f]}, {}, 2, start, finish)


def _gather_d2d_comm(fulls, jobs, specs):
    def rect(refs, job, h):
        f, r0, nr = job
        assert nr is None or specs[f][0] == "col"
        return _rows_of(_half(refs[f], specs[f][0], h), r0, nr)

    def start(refs, ss, rs):
        x, y, c = _position()
        for q, job in enumerate(jobs):
            _remote(rect(refs, job, c), rect(refs, job, c), ss.at[q], rs.at[q], (x, y, 1 - c)).start()

    def finish(refs, ss, rs):
        x, y, c = _position()
        for q, job in enumerate(jobs):
            _remote(rect(refs, job, 1 - c), rect(refs, job, 1 - c), ss.at[q], rs.at[q], (x, y, 1 - c)).wait_recv()
        for q, job in enumerate(jobs):
            _remote(rect(refs, job, c), rect(refs, job, c), ss.at[q], rs.at[q], (x, y, 1 - c)).wait_send()

    names = list(dict.fromkeys(job[0] for job in jobs))
    return _Carried({f: fulls[f] for f in names}, {}, len(jobs), start, finish)


def _pairx_comm(grads, names, specs, whole=False, recv=None):
    def copies(refs, ss, rs):
        x, y, c = _position()
        src = (lambda f: refs[("g", f)]) if whole else (lambda f: _half(refs[("g", f)], specs[f][0], 1 - c))
        return [_remote(src(f), refs[("r", f)], ss.at[i], rs.at[i], (x, y, 1 - c)) for i, f in enumerate(names)]

    def start(refs, ss, rs):
        for cp in copies(refs, ss, rs):
            cp.start()

    def finish(refs, ss, rs):
        for cp in copies(refs, ss, rs):
            cp.wait_recv()
        for cp in copies(refs, ss, rs):
            cp.wait_send()

    reads = {("g", f): grads[f] for f in names}
    if recv is not None:
        return _Carried({("r", f): recv[f] for f in names}, {}, len(names), start, finish, reads=reads)
    fresh = {("r", f): jax.ShapeDtypeStruct(_half_shape(*specs[f]), BF16) for f in names}
    return _Carried({}, fresh, len(names), start, finish, reads=reads)


def _chipx_comm(pair_sums, slots, jobs, specs):
    def copies(refs, ss, rs):
        x, y, c = _position()
        out = []
        for q, (f, r0, nr) in enumerate(jobs):
            kind = specs[f][0]
            width = _slot_shape(specs[f])[1 if kind == "col" else 0]
            for p, (px, py) in enumerate(_other_chips(x, y)):
                src = _rows_of(_shard_of(refs[("p", f)], kind, (2 * px + py) * width, width), r0, nr)
                dst = _rows_of(refs[("s", f)].at[p], r0, nr)
                out.append(_remote(src, dst, ss.at[3 * q + p], rs.at[3 * q + p], (px, py, c)))
        return out

    def start(refs, ss, rs):
        for cp in copies(refs, ss, rs):
            cp.start()

    def finish(refs, ss, rs):
        for cp in copies(refs, ss, rs):
            cp.wait_recv()
        for cp in copies(refs, ss, rs):
            cp.wait_send()

    names = list(dict.fromkeys(job[0] for job in jobs))
    arrays = {("p", f): pair_sums[f] for f in names}
    arrays.update({("s", f): slots[f] for f in names})
    return _Carried(arrays, {}, 3 * len(jobs), start, finish)


def _share_comm(grads, wnames, specs, place):
    def start(refs, ss, rs):
        x, y, c = _position()
        for i, w in enumerate(wnames):
            kind = specs[place[w][0]][0]
            _remote(_half(refs[w], kind, c), _half(refs[w], kind, c), ss.at[i], rs.at[i], (x, y, 1 - c)).start()

    def finish(refs, ss, rs):
        x, y, c = _position()
        for i, w in enumerate(wnames):
            kind = specs[place[w][0]][0]
            _remote(_half(refs[w], kind, 1 - c), _half(refs[w], kind, 1 - c), ss.at[i], rs.at[i],
                    (x, y, 1 - c)).wait_recv()
        for i, w in enumerate(wnames):
            kind = specs[place[w][0]][0]
            _remote(_half(refs[w], kind, c), _half(refs[w], kind, c), ss.at[i], rs.at[i], (x, y, 1 - c)).wait_send()

    return _Carried({w: grads[w] for w in wnames}, {}, len(wnames), start, finish)


SMALL_IN, SMALL_ROWS_OF_ALL = ("small", "mine"), ("small", "all")


def _small_gather_comm(v):
    flips = [(fx, fy, fc) for fx in (0, 1) for fy in (0, 1) for fc in (0, 1)][1:]

    def copies(refs, ss, rs):
        x, y, c = _position()
        v_ref, out_ref = refs[SMALL_IN], refs[SMALL_ROWS_OF_ALL]
        me = 4 * x + 2 * y + c
        pairs = []
        for i, fl in enumerate(flips):
            px, py, pc = (1 - a if f else a for a, f in zip((x, y, c), fl))
            pairs.append((_remote(v_ref, out_ref.at[me], ss.at[i], rs.at[i], (px, py, pc)),
                          _remote(v_ref, out_ref.at[4 * px + 2 * py + pc], ss.at[i], rs.at[i], (px, py, pc))))
        return pairs, pltpu.make_async_copy(v_ref, out_ref.at[me], ss.at[N_DEV - 1])

    def start(refs, ss, rs):
        pairs, local = copies(refs, ss, rs)
        local.start()
        for mine, _ in pairs:
            mine.start()

    def finish(refs, ss, rs):
        pairs, local = copies(refs, ss, rs)
        for _, theirs in pairs:
            theirs.wait_recv()
        for mine, _ in pairs:
            mine.wait_send()
        local.wait()

    fresh = {SMALL_ROWS_OF_ALL: jax.ShapeDtypeStruct((N_DEV,) + v.shape, F32)}
    return _Carried({}, fresh, N_DEV, start, finish, reads={SMALL_IN: v})


def _ew_block(rows, cols, elems=EW_BLOCK_ELEMS):
    tc = cols if cols <= 4096 else _div(cols, 2048, LANE)
    tr = _div(rows, max(16, elems // tc), 16)
    return tr, tc


def _mesh_scalars():
    x, y, c = _position()
    return jnp.stack([c, 2 * x + y]).astype(jnp.int32)


def _grid_spec(grid, in_specs, out_specs):
    return pltpu.PrefetchScalarGridSpec(num_scalar_prefetch=1, grid=grid, in_specs=in_specs, out_specs=out_specs)


def _cast_into_full(parts, fname, pos, specs, place, name, token=None):
    kind, rows, cols = specs[fname]
    ws = [w for w in place if place[w][0] == fname]
    if kind == "col":
        stride = cols // N_CHIPS
        hr = rows // 2
        tr = _div(hr, max(16, EW_BLOCK_ELEMS // stride), 16)
        nrb = hr // tr
        in_specs = [pl.BlockSpec((tr, parts[w].shape[1]), lambda i, pos_ref: (i + pos_ref[0] * nrb, 0)) for w in ws]
        out_spec = pl.BlockSpec((tr, stride), lambda i, pos_ref: (i + pos_ref[0] * nrb, pos_ref[1]))
    else:
        stride = rows // N_CHIPS
        hc = cols // 2
        tr = _div(stride, max(16, EW_BLOCK_ELEMS // hc), 16)
        nrb = stride // tr
        in_specs = [pl.BlockSpec((tr, hc), lambda i, pos_ref: (i, pos_ref[0])) for w in ws]
        out_spec = pl.BlockSpec((tr, hc), lambda i, pos_ref: (i + pos_ref[1] * nrb, pos_ref[0]))

    def kern(pos_ref, *refs):
        o_ref = refs[-1]
        for w, r in zip(ws, refs[:len(ws)]):
            off = place[w][1] if kind == "col" else 0
            o_ref[:, off:off + r.shape[1]] = r[...].astype(o_ref.dtype)

    tokens = [] if token is None else [token]
    in_specs = in_specs + [pl.BlockSpec(TOKEN_SHAPE, lambda i, pos_ref: (0, 0))] * len(tokens)
    return pl.pallas_call(
        kern, name=name, grid_spec=_grid_spec((nrb,), in_specs, out_spec),
        out_shape=jax.ShapeDtypeStruct((rows, cols), BF16),
        compiler_params=_cparams(("parallel",)),
    )(pos, *[parts[w] for w in ws], *tokens)


def _pair_sum(grad, recv, pos, spec, name):
    kind, rows, cols = spec
    hr, hc = _half_shape(kind, rows, cols)
    tr, tc = _ew_block(hr, hc, 2 * EW_BLOCK_ELEMS)
    nrb, ncb = hr // tr, hc // tc
    blk = pl.BlockSpec((tr, tc), lambda i, jj, pos_ref: (i, jj))
    if kind == "col":
        mine = pl.BlockSpec((tr, tc), lambda i, jj, pos_ref: (i + pos_ref[0] * nrb, jj))
    else:
        mine = pl.BlockSpec((tr, tc), lambda i, jj, pos_ref: (i, jj + pos_ref[0] * ncb))

    def kern(pos_ref, g_ref, r_ref, o_ref, slots_ref):
        o_ref[...] = (g_ref[...].astype(F32) + r_ref[...].astype(F32)).astype(o_ref.dtype)

    return pl.pallas_call(
        kern, name=name, grid_spec=_grid_spec((nrb, ncb), [mine, blk], [blk, ANY]),
        out_shape=[jax.ShapeDtypeStruct((hr, hc), BF16),
                   jax.ShapeDtypeStruct((N_CHIPS - 1,) + _slot_shape(spec), BF16)],
        compiler_params=_cparams(("parallel", "parallel")),
    )(pos, grad, recv)


def _chip_sum(pair_sum, slots, pos, fname, shard_shapes, specs, place, name):
    kind, rows, cols = specs[fname]
    sr, sc = _slot_shape(specs[fname])
    ws = [w for w in place if place[w][0] == fname]
    n_slots = N_CHIPS - 1
    tr = _div(sr, max(16, EW_BLOCK_ELEMS // sc), 16)
    nrb = sr // tr
    slot = pl.BlockSpec((n_slots, tr, sc), lambda i, pos_ref: (0, i, 0))
    if kind == "col":
        own = pl.BlockSpec((tr, sc), lambda i, pos_ref: (i, pos_ref[1]))
        out_specs = [pl.BlockSpec((tr, shard_shapes[w][1]), lambda i, pos_ref: (i + pos_ref[0] * nrb, 0)) for w in ws]
    else:
        own = pl.BlockSpec((tr, sc), lambda i, pos_ref: (i + pos_ref[1] * nrb, 0))
        out_specs = [pl.BlockSpec((tr, sc), lambda i, pos_ref: (i, pos_ref[0])) for w in ws]

    def kern(pos_ref, own_ref, slot_ref, *out_refs):
        tot = own_ref[...].astype(F32)
        for s in range(n_slots):
            tot = tot + slot_ref[s].astype(F32)
        for w, o_ref in zip(ws, out_refs):
            off = place[w][1] if kind == "col" else 0
            o_ref[...] = tot[:, off:off + o_ref.shape[1]]

    outs = pl.pallas_call(
        kern, name=name, grid_spec=_grid_spec((nrb,), [own, slot], out_specs),
        out_shape=[jax.ShapeDtypeStruct(shard_shapes[w], F32) for w in ws],
        compiler_params=_cparams(("parallel",)),
    )(pos, pair_sum, slots)
    return dict(zip(ws, outs))


def _adam_math(w, g, m, v):
    m2 = ADAM_B1 * m + (1.0 - ADAM_B1) * g
    v2 = ADAM_B2 * v + (1.0 - ADAM_B2) * (g * g)
    m_hat = m2 / (1.0 - ADAM_B1 ** ADAM_STEP)
    v_hat = v2 / (1.0 - ADAM_B2 ** ADAM_STEP)
    delta = -ADAM_LR * (m_hat / (jnp.sqrt(v_hat) + ADAM_EPS) + ADAM_WD * w)
    return delta, m2, v2


def _adamw(w, g, m, v, name, carried=()):
    rows, cols = w.shape
    tr, tc = _ew_block(rows, cols)

    def kern(w_ref, g_ref, m_ref, v_ref, d_ref, m2_ref, v2_ref, g_out_ref):
        g_ = g_ref[...]
        d_ref[...], m2_ref[...], v2_ref[...] = _adam_math(w_ref[...], g_, m_ref[...], v_ref[...])
        g_out_ref[...] = g_

    blk = pl.BlockSpec((tr, tc), lambda i, j: (i, j))
    return _pcall(
        kern, name=name, grid=(rows // tr, cols // tc), in_specs=[blk] * 4, out_specs=[blk] * 4,
        out_shape=[jax.ShapeDtypeStruct((rows, cols), F32)] * 4, args=(w, g, m, v),
        semantics=("parallel", "parallel"), carried=carried)


SMALL_ROWS = ("ln_mix_w", "ln_cross_w", "ln_mem_w", "ln_ffn_w", "ln_final_w")
ROW_HG_NORM, ROW_LB0, ROW_LB1 = 5, 6, 7
LOSS_LANE0 = HEAD_DIM


def _pack_small(vals):
    rows = [vals[n].reshape(1, D_MODEL) for n in SMALL_ROWS]
    pad = lambda a: jnp.pad(a, ((0, 0), (0, D_MODEL - a.shape[1])))
    rows.append(pad(vals["hg_norm_w"].reshape(1, HEAD_DIM)))
    rows.append(pad(vals["hg_lower_bounds"].reshape(2, HG_WIDTH)))
    return jnp.concatenate(rows, axis=0)


def _small_update(gathered, w, m, v, name="small_update"):
    def kern(g_ref, w_ref, m_ref, v_ref, grad_ref, d_ref, m2_ref, v2_ref, loss_ref):
        tot = g_ref[0]
        for s in range(1, N_DEV):
            tot = tot + g_ref[s]
        wv = w_ref[...]
        row = lax.broadcasted_iota(jnp.int32, (8, D_MODEL), 0)
        lane = lax.broadcasted_iota(jnp.int32, (8, D_MODEL), 1)
        l0, l1 = wv[ROW_LB0:ROW_LB0 + 1], wv[ROW_LB1:ROW_LB1 + 1]
        mx = jnp.maximum(l0, l1)
        e0, e1 = jnp.exp(l0 - mx), jnp.exp(l1 - mx)
        p0 = e0 / (e0 + e1)
        dlog = tot[ROW_LB0:ROW_LB0 + 1] * p0 * (1.0 - p0)
        tot = jnp.where(row == ROW_HG_NORM, tot + tot[ROW_LB1:ROW_LB1 + 1], tot)
        grad = jnp.where(row == ROW_LB0, dlog, jnp.where(row == ROW_LB1, -dlog, tot))
        grad = jnp.where((row == ROW_HG_NORM) & (lane >= HEAD_DIM), 0.0, grad)
        grad = jnp.where((row >= ROW_LB0) & (lane >= HG_WIDTH), 0.0, grad)
        grad_ref[...] = grad
        d_ref[...], m2_ref[...], v2_ref[...] = _adam_math(wv, grad, m_ref[...], v_ref[...])
        loss_ref[...] = tot[ROW_HG_NORM:ROW_HG_NORM + 1, LOSS_LANE0:LOSS_LANE0 + LANE]

    full = pl.BlockSpec((8, D_MODEL), lambda: (0, 0))
    return pl.pallas_call(
        kern, name=name,
        in_specs=[pl.BlockSpec((N_DEV, 8, D_MODEL), lambda: (0, 0, 0)), full, full, full],
        out_specs=[full, full, full, full, pl.BlockSpec((1, LANE), lambda: (0, 0))],
        out_shape=[jax.ShapeDtypeStruct((8, D_MODEL), F32)] * 4 + [jax.ShapeDtypeStruct((1, LANE), F32)],
        compiler_params=_cparams(),
    )(gathered, w, m, v)


def _unpack_small(p, shapes):
    out = {n: p[i].reshape(shapes[n]) for i, n in enumerate(SMALL_ROWS)}
    out["hg_norm_w"] = p[ROW_HG_NORM, :HEAD_DIM].reshape(shapes["hg_norm_w"])
    out["hg_lower_bounds"] = p[ROW_LB0:ROW_LB1 + 1, :HG_WIDTH].reshape(shapes["hg_lower_bounds"])
    return out


def _concat_cols(pieces, name):
    rows = pieces[0].shape[0]
    widths = [p.shape[1] for p in pieces]
    tr = ROW_BLOCK // 2

    def kern(*refs):
        o_ref, off = refs[-1], 0
        for r, w in zip(refs[:-1], widths):
            o_ref[:, off:off + w] = r[...]
            off += w

    return pl.pallas_call(
        kern, name=name, grid=(rows // tr,),
        in_specs=[pl.BlockSpec((tr, w), lambda i: (i, 0)) for w in widths],
        out_specs=pl.BlockSpec((tr, sum(widths)), lambda i: (i, 0)),
        out_shape=jax.ShapeDtypeStruct((rows, sum(widths)), pieces[0].dtype),
        compiler_params=_cparams(("parallel",)),
    )(*pieces)


WHOLE = lambda f: (f, 0, None)
MID_MATRICES = ("w_branch_a", "w_branch_b", "w_out", "wq_cross", "wkv_cross", "wo_cross")
MID_WEIGHTS = MID_MATRICES
W_IN_PIECES = [("w_in", r0, 512) for r0 in range(0, D_MODEL // 2, 512)]
W13_PIECES = [("w13", r0, 512) for r0 in range(0, D_MODEL // 2, 512)]
GATHER_GROUPS = [("mid", [WHOLE(f) for f in MID_MATRICES]), ("w13a", W13_PIECES[:1]), ("w13b", W13_PIECES[1:]),
                 ("w2", [WHOLE("w2")])]
OTHER_WEIGHTS = ["w1", "w3", "w2"] + list(MID_WEIGHTS)
BEFORE = {
    "hgrn_fwd": [("wait", "mid")],
    "mm_out": [("wait", "w13a")],
    "mm_o": [("wait", "w13b")],
    "mm_w2": [("wait", "w2"), ("run", ("d2d", [WHOLE("w2")]), "gather_hand_over_w2")],
    "mm_dwin_own": [("wait", "rs_w2"), ("chip_sum", "w2"), ("wait", "rs_w13"), ("chip_sum", "w13"),
                    ("wait", "pair_w_in")],
    "mm_dh": [("wait", "rs_mid")] + [("chip_sum", f) for f in MID_MATRICES],
}
CARRY = {
    "hgrn_fwd": [("d2d", [WHOLE(f) for f in MID_MATRICES])],
    "mm_out": [("d2d", W13_PIECES[:1])],
    "mm_o": [("d2d", W13_PIECES[1:])],
    "mm_du": [("pairx_whole", ["w2"])],
    "mm_dhf": [("pairx_whole", ["w13"])],
    "attn_bwd_g0": [("pairx", list(MID_MATRICES))],
    "mm_dh": [("share", OTHER_WEIGHTS)],
}
AFTER = {
    "mm_dw2_own": [("start", "rs_w2", [WHOLE("w2")])],
    "mm_dw13_own": [("start", "rs_w13", [WHOLE("w13")])],
    "attn_bwd_g0": [("pair_sum", f) for f in MID_MATRICES] + [("start", "rs_mid", [WHOLE(f) for f in MID_MATRICES])],
    "mm_dwin_sibling": [("start_pairx", "pair_w_in", ["w_in"])],
    "mm_dwin_own": [("start", "rs_w_in", [WHOLE("w_in")])],
}
FINISH = [
    ("adamw", OTHER_WEIGHTS), ("wait", "rs_w_in"), ("chip_sum", "w_in"), ("share_and_small", ["w_in"]),
    ("adamw", ["w_in"]),
]


class _Late:
    def __init__(self, read):
        self.read = read


class _Net:
    def __init__(self, full, pos=None, shard_shapes=None, comm=True, specs=FULL_SPECS, place=WEIGHT_PLACE):
        self.full, self.pos, self.shard_shapes, self.comm = dict(full), pos, shard_shapes, comm
        self.specs, self.place = specs, place
        self.gw, self.recv, self.psum, self.slots, self.grads = {}, {}, {}, {}, {}
        self.gw_sibling = {}
        self.pending, self.token, self.last = {}, None, None
        self.ahead = []

    def _make(self, kind, arg):
        if kind == "gather":
            return _gather_ici_comm(self.full, arg, self.specs)
        if kind == "ring":
            return _gather_ring_comm(self.full, *arg, self.specs)
        if kind == "d2d":
            return _gather_d2d_comm(self.full, arg, self.specs)
        if kind == "pairx":
            return _pairx_comm(self.gw, arg, self.specs)
        if kind == "pairx_whole":
            return _pairx_comm(self.gw_sibling, arg, self.specs, whole=True)
        if kind == "pairx_split":
            return _pairx_comm(self.gw_sibling, arg, self.specs, whole=True, recv=self.recv)
        if kind == "chipx":
            return _chipx_comm(self.psum, self.slots, arg, self.specs)
        assert kind == "share"
        return _share_comm(self.grads, arg, self.specs, self.place)

    def _store(self, kind, res):
        if kind in ("gather", "ring", "d2d"):
            self.full.update(res)
        elif kind in ("pairx", "pairx_whole", "pairx_split"):
            for (tag, f), a in res.items():
                (self.gw if tag == "g" else self.recv)[f] = a
        elif kind == "chipx":
            for (tag, f), a in res.items():
                (self.psum if tag == "p" else self.slots)[f] = a
        else:
            self.grads.update(res)

    def run_comm(self, item, name, extra=()):
        kind, arg = item
        res = _run_comm([self._make(kind, arg), *extra], name)
        self._store(kind, res[0])
        return res[1:]

    @staticmethod
    def _others(after, items):
        own = [a for cm in items for a in cm.arrays.values()]
        return [a for a in after if a is not None and all(a is not o for o in own)]

    def start(self, groups, kind, name):
        items = [self._make(kind, jobs) for _, jobs in groups]
        after = self._others([self.last], items)
        res, sems, token = _split_start(items, name, after=after[0] if after else None)
        for (group, jobs), r, s in zip(groups, res, sems):
            self._store(kind, r)
            self.pending[group] = (kind, jobs, s)
        self.token = self.last = token

    def wait(self, group, after=()):
        kind, jobs, sems = self.pending.pop(group)
        item = self._make(kind, jobs)
        res = _split_wait([item], [sems], self._others([self.last, *after], [item]), f"wait_{group}")[0]
        self._store(kind, res)

    def step(self, step):
        if step[0] == "wait":
            self.wait(step[1], after=self.ahead)
            self.ahead = []
        elif step[0] == "start":
            self.start([(step[1], step[2])], "chipx", f"start_{step[1]}")
        elif step[0] == "start_pairx":
            for f in step[2]:
                self.recv[f] = lax.empty(_half_shape(*self.specs[f]), BF16)
            self.start([(step[1], step[2])], "pairx_split", f"start_{step[1]}")
        elif step[0] == "pair_sum":
            f = step[1]
            self.psum[f], self.slots[f] = _pair_sum(self.gw[f], self.recv[f], self.pos, self.specs[f],
                                                    f"rs_pair_sum_{f}")
        elif step[0] == "chip_sum":
            f = step[1]
            sums = _chip_sum(self.psum[f], self.slots[f], self.pos, f, self.shard_shapes,
                             self.specs, self.place, f"rs_chip_sum_{f}")
            self.grads.update(sums)
            self.ahead += list(sums.values())
        else:
            assert step[0] == "run"
            self.run_comm(step[1], step[2])

    def call(self, fn, name, *args, grad_of=None, sibling_half=False, pair_sum_of=None, **kw):
        for step in (BEFORE.get(name, []) if self.comm else []):
            self.step(step)
        self.ahead = []
        late = lambda a: a.read() if isinstance(a, _Late) else a
        args = [late(a) for a in args]
        kw = {key: late(val) for key, val in kw.items()}
        items = CARRY.get(name, []) if self.comm else []
        carried = [self._make(k, a) for k, a in items]
        if self.token is not None:
            carried.append(_Token(self.token))
            self.token = None
        out, res = fn(*args, name=name, carried=carried, **kw)
        if grad_of is not None:
            (self.gw_sibling if sibling_half else self.gw)[grad_of] = out
        if pair_sum_of is not None:
            self.psum[pair_sum_of] = out
            self.slots[pair_sum_of] = lax.empty((N_CHIPS - 1,) + _slot_shape(self.specs[pair_sum_of]), BF16)
        self.last = jax.tree.leaves(out)[0]
        for (kind, _), r in zip(items, res):
            self._store(kind, r)
        for step in (AFTER.get(name, []) if self.comm else []):
            self.step(step)
        return out


def _local_step(net, x, h, mem, target, small, h_halves=None, core_row=None):
    full, call = net.full, net.call
    proj = call(_mm, "mm_proj", h, full["w_in"], mode="nn", out_dtype=F32)
    att = [call(_attn_fwd, f"attn_fwd_g{g}", proj, g) for g in range(3)]
    outs, lses = [a[0] for a in att], [a[1] for a in att]
    o_att = _attn_merge_fwd(outs, lses, "attn_merge")
    oraw, o_hg, states = call(_hg_fwd, "hgrn_fwd", proj, small["hg_lower_bounds"], small["hg_norm_w"])
    ya = call(_mm, "mm_branch_a", o_att, full["w_branch_a"], mode="nn", out_dtype=F32)
    yb, merged = call(_branch_b_gate, "mm_branch_b", o_hg, full["w_branch_b"], proj, ya)
    x1, hc = call(_residual_norm, "mm_out", merged, full["w_out"], x, small["ln_cross_w"])

    mn = _rms_fwd(mem, small["ln_mem_w"], "rms_mem")
    qc = call(_mm, "mm_q", hc, full["wq_cross"], mode="nn", out_dtype=F32)
    kvc = call(_mm, "mm_kv", mn, full["wkv_cross"], mode="nn", out_dtype=F32)
    oc = call(_cross_fwd, "cross_fwd", qc, kvc)
    x2, hf, *hf_halves = call(_residual_norm, "mm_o", oc, full["wo_cross"], x1, small["ln_ffn_w"], core_row=core_row)

    ab, u = call(_ff_up, "mm_w13", hf, full["w13"])
    x3 = call(_mm, "mm_w2", u, _Late(lambda: full["w2"]), mode="nn", out_dtype=F32, res=x2)

    dx3, dg_final, loss, dx3_bf16, *dx3_halves = _loss_head(x3, small["ln_final_w"], target, "loss_head", core_row)

    gs = {"ln_final_w": dg_final}
    if net.comm:
        call(_mm, "mm_dw2_sibling", u, dx3_halves[0], mode="tn", out_dtype=BF16, grad_of="w2", sibling_half=True)
        dab = call(_ff_down_bwd, "mm_du", dx3_bf16, full["w2"], ab)
        call(_mm, "mm_dw2_own", u, dx3_halves[1], mode="tn", out_dtype=BF16, res=_Late(lambda: net.recv["w2"]),
             pair_sum_of="w2")
        call(_mm, "mm_dw13_sibling", hf_halves[0], dab, mode="tn", out_dtype=BF16, grad_of="w13", sibling_half=True)
        dhf = call(_mm, "mm_dhf", dab, full["w13"], mode="nt", out_dtype=F32)
        call(_mm, "mm_dw13_own", hf_halves[1], dab, mode="tn", out_dtype=BF16, res=_Late(lambda: net.recv["w13"]),
             pair_sum_of="w13")
    else:
        call(_mm, "mm_dw2", u, dx3_bf16, mode="tn", out_dtype=BF16, grad_of="w2")
        dab = call(_ff_down_bwd, "mm_du", dx3_bf16, full["w2"], ab)
        call(_mm, "mm_dw13", hf, dab, mode="tn", out_dtype=BF16, grad_of="w13")
        dhf = call(_mm, "mm_dhf", dab, full["w13"], mode="nt", out_dtype=F32)
    dx2, gs["ln_ffn_w"], dx2_bf16 = _rms_bwd(x2, small["ln_ffn_w"], dhf, dx3, "rms_ffn_bwd", bf16_copy=True)
    doc = call(_mm, "mm_doc", dx2_bf16, full["wo_cross"], mode="nt", out_dtype=BF16)
    call(_mm, "mm_dwo", oc, dx2_bf16, mode="tn", out_dtype=BF16, grad_of="wo_cross")
    dqc, dkvc = _cross_bwd(qc, kvc, doc, "cross_bwd")
    call(_mm, "mm_dwq", hc, dqc, mode="tn", out_dtype=BF16, grad_of="wq_cross")
    dx1, gs["ln_cross_w"], dx1_bf16 = call(_norm_bwd_residual, "mm_dhc", dqc, full["wq_cross"], x1,
                                           small["ln_cross_w"], dx2)
    call(_mm, "mm_dwkv", mn, dkvc, mode="tn", out_dtype=BF16, grad_of="wkv_cross")
    dya, dyb, dga, dgb = call(_dmerged_gate_bwd, "mm_dmerged", dx1_bf16, full["w_out"], proj, ya, yb)
    call(_mm, "mm_dwout", merged, dx1_bf16, mode="tn", out_dtype=BF16, grad_of="w_out")
    call(_mm, "mm_dwa", o_att, dya, mode="tn", out_dtype=BF16, grad_of="w_branch_a")
    do_att = call(_mm, "mm_doatt", dya, full["w_branch_a"], mode="nt", out_dtype=F32)
    call(_mm, "mm_dwb", o_hg, dyb, mode="tn", out_dtype=BF16, grad_of="w_branch_b")
    do_hg = call(_mm, "mm_dohg", dyb, full["w_branch_b"], mode="nt", out_dtype=F32)
    dqh, dfh, dih, dgh, dlb, gs["hg_norm_w"] = call(
        _hg_bwd, "hgrn_bwd", proj, small["hg_lower_bounds"], small["hg_norm_w"], oraw, states, do_hg)
    gs["hg_lb"] = dlb
    do_gs, dl_gs = call(_attn_merge_bwd, "attn_merge_bwd", outs, lses, do_att)
    dqs, dks, dvs = zip(*[call(_attn_bwd, f"attn_bwd_g{g}", proj, g, lses[g], do_gs[g], dl_gs[g]) for g in range(3)])
    dproj = _concat_cols([*dqs, *dks, *dvs, dqh, dfh, dih, dgh, dga, dgb], "dproj_concat")
    if net.comm:
        h_sibling, h_own = h_halves
        call(_mm, "mm_dwin_sibling", h_sibling, dproj, mode="tn", out_dtype=BF16, grad_of="w_in", sibling_half=True)
    dmn = call(_mm, "mm_dmn", dkvc, full["wkv_cross"], mode="nt", out_dtype=F32)
    _, gs["ln_mem_w"] = _rms_bwd(mem, small["ln_mem_w"], dmn, None, "rms_mem_bwd")
    net.ahead.append(gs["ln_mem_w"])
    if net.comm:
        call(_mm, "mm_dwin_own", h_own, dproj, mode="tn", out_dtype=BF16, res=_Late(lambda: net.recv["w_in"]),
             pair_sum_of="w_in")
    else:
        call(_mm, "mm_dwin", h, dproj, mode="tn", out_dtype=BF16, grad_of="w_in")
    dh = call(_mm, "mm_dh", dproj, full["w_in"], mode="nt", out_dtype=F32)
    dx, gs["ln_mix_w"] = _rms_bwd(x, small["ln_mix_w"], dh, dx1, "rms_mix_bwd")
    return loss, dx, gs


WEIGHT_ORDER = ("ln_mix_w", "w_in", "hg_norm_w", "hg_lower_bounds", "w_branch_a", "w_branch_b", "w_out",
                "ln_cross_w", "ln_mem_w", "wq_cross", "wkv_cross", "wo_cross", "ln_ffn_w", "w1", "w3", "w2",
                "ln_final_w")


def kernel(x, mem, ln_mix_w, w_in, hg_norm_w, hg_lower_bounds, w_branch_a, w_branch_b, w_out, ln_cross_w, ln_mem_w, wq_cross, wkv_cross, wo_cross, ln_ffn_w, w1, w3, w2, ln_final_w, loss_target, m_ln_mix_w, m_w_in, m_hg_norm_w, m_hg_lower_bounds, m_w_branch_a, m_w_branch_b, m_w_out, m_ln_cross_w, m_ln_mem_w, m_wq_cross, m_wkv_cross, m_wo_cross, m_ln_ffn_w, m_w1, m_w3, m_w2, m_ln_final_w, v_ln_mix_w, v_w_in, v_hg_norm_w, v_hg_lower_bounds, v_w_branch_a, v_w_branch_b, v_w_out, v_ln_cross_w, v_ln_mem_w, v_wq_cross, v_wkv_cross, v_wo_cross, v_ln_ffn_w, v_w1, v_w3, v_w2, v_ln_final_w):
    args = dict(locals())
    w = {n: args[n] for n in WEIGHT_ORDER}
    m = {n: args["m_" + n] for n in WEIGHT_ORDER}
    v = {n: args["v_" + n] for n in WEIGHT_ORDER}
    shapes = {n: w[n].shape for n in WEIGHT_ORDER}
    mat = lambda a: a.reshape(a.shape[-2:])
    shard_shapes = {n: shapes[n][-2:] for n in BIG_WEIGHTS}

    pos = _mesh_scalars()

    def cast(f, token=None):
        return _cast_into_full({n: mat(w[n]) for n in BIG_WEIGHTS if WEIGHT_PLACE[n][0] == f}, f, pos,
                               FULL_SPECS, WEIGHT_PLACE, f"cast_{f}", token)

    net = _Net({"w_in": cast("w_in")}, pos, shard_shapes)
    net.start([(f"ring_a{i}", (*job, "a")) for i, job in enumerate(W_IN_PIECES)], "ring", "gather_start_w_in")
    rest = {f: cast(f, net.token) for f in FULL_SPECS if f != "w_in"}
    net.full.update(rest)
    small = {n: w[n].reshape(1, -1) for n in SMALL_ROWS}
    small["hg_norm_w"] = w["hg_norm_w"].reshape(1, HEAD_DIM)
    small["hg_lower_bounds"] = w["hg_lower_bounds"]
    x2d = x.reshape(SEQ, D_MODEL)
    h, *h_halves = _rms_fwd_halves(x2d, small["ln_mix_w"], pos, "rms_mix")
    for i, job in enumerate(W_IN_PIECES):
        net.wait(f"ring_a{i}", after=[*rest.values(), h] if i == 0 else ())
        net.start([(f"ring_b{i}", (*job, "b"))], "ring", f"gather_pass_on_w_in{i}")
    net.start(GATHER_GROUPS, "gather", "gather_start_rest")
    for i, job in enumerate(W_IN_PIECES):
        net.wait(f"ring_b{i}")
        net.run_comm(("d2d", [job]), f"gather_hand_over_w_in{i}")
    core_row = jnp.full((1, D_MODEL), lax.axis_index("c").astype(F32))
    loss, dx, gs = _local_step(net, x2d, h, mem.reshape(MEM_LEN, D_MODEL), loss_target.reshape(SEQ, D_MODEL), small,
                               h_halves, core_row)

    out_g, out_d, out_m, out_v = {}, {}, {}, {}
    net.last = dx
    for step in FINISH:
        if step[0] == "adamw":
            for n in step[1]:
                out_d[n], out_m[n], out_v[n], out_g[n] = net.call(_adamw, f"adamw_{n}", mat(w[n]), net.grads[n],
                                                                  mat(m[n]), mat(v[n]))
        elif step[0] == "wait":
            net.wait(step[1], after=list(out_d.values()))
        elif step[0] == "share_and_small":
            pad = lambda a: jnp.pad(a, ((0, 0), (0, D_MODEL - a.shape[1])))
            part = jnp.concatenate(
                [gs[n] for n in SMALL_ROWS]
                + [pad(jnp.concatenate([gs["hg_norm_w"][0], loss], axis=1)), pad(gs["hg_lb"]),
                   pad(gs["hg_norm_w"][1]) if len(gs["hg_norm_w"]) > 1 else jnp.zeros((1, D_MODEL), F32)], axis=0)
            rows = net.run_comm(("share", step[1]), "rs_sibling_share_and_gather_small",
                                extra=[_small_gather_comm(part)])[0][SMALL_ROWS_OF_ALL]
            sg, sd, sm, sv, loss_tot = _small_update(rows, _pack_small(w), _pack_small(m), _pack_small(v))
            for dst, packed in ((out_g, sg), (out_d, sd), (out_m, sm), (out_v, sv)):
                dst.update(_unpack_small(packed, shapes))
        else:
            net.step(step)

    result = [loss_tot[0, 0], dx.reshape(x.shape)]
    for group in (out_g, out_d, out_m, out_v):
        result += [group[n].reshape(shapes[n]) for n in WEIGHT_ORDER]
    return tuple(result)
```

```python
import math

import jax
import jax.numpy as jnp
from jax import lax
from jax.experimental import pallas as pl
from jax.experimental.pallas import tpu as pltpu

F32 = jnp.float32
BF16 = jnp.bfloat16
MESH = pl.DeviceIdType.MESH

D_MODEL = 2048
SEQ = 2048
HEAD_DIM = 128
MEM_LEN = 256
ATT_GROUPS = ((128, 1), (512, 4), (2048, 16))
ATT_HEADS = 4
ATT_WIDTH = 3 * ATT_HEADS * HEAD_DIM
ATT_OUT = ATT_HEADS * HEAD_DIM
ATT_BLOCK = 128
HG_HEADS = 8
HG_WIDTH = HG_HEADS * HEAD_DIM
HG_CHUNK = 64
IN_WIDTH = 3 * ATT_WIDTH + 4 * HG_WIDTH + 2 * D_MODEL
CROSS_HEADS = 4
CROSS_WIDTH = CROSS_HEADS * HEAD_DIM
D_FF = 5632
RMS_EPS = 1e-6
ADAM_LR = 0.001
ADAM_B1 = 0.9
ADAM_B2 = 0.999
ADAM_EPS = 1e-08
ADAM_WD = 0.01
ADAM_STEP = 10
N_CHIPS = 4
N_DEV = 8

VMEM_LIMIT_BYTES = 56 * 1024 * 1024
LANE = 128
MXU_WIDTH = 256
MM_TILE_CAP = 1536
TRANSPOSE_CHUNK = 512
ANY = pl.BlockSpec(memory_space=pl.ANY)


def _cparams(sem=None):
    return pltpu.CompilerParams(dimension_semantics=sem, vmem_limit_bytes=VMEM_LIMIT_BYTES)


def _div(n, cap, mult):
    best = None
    for d in range(mult, min(n, cap) + 1, mult):
        if n % d == 0:
            best = d
    assert best is not None, (n, cap, mult)
    return best


def _sigmoid(x):
    return 1.0 / (1.0 + jnp.exp(-x))


def _dot(a, b):
    return jnp.dot(a.astype(BF16), b.astype(BF16), preferred_element_type=F32)


def _dot_nt(a, b):
    return lax.dot_general(a.astype(BF16), b.astype(BF16), (((1,), (1,)), ((), ())),
                           preferred_element_type=F32)


def _dot_tn(a, b):
    return jnp.dot(a.astype(F32).T.astype(BF16), b.astype(BF16), preferred_element_type=F32)


def _dot_exact(a, b):
    return jnp.dot(a, b, precision=lax.Precision.HIGHEST, preferred_element_type=F32)


class _Carried:
    def __init__(self, arrays, fresh, n_sems, start, finish, mid=None, reads=None):
        self.arrays, self.fresh, self.n_sems, self.reads = arrays, fresh, n_sems, reads or {}
        self.start, self.mid, self.finish = start, mid, finish


class _Token:
    def __init__(self, array):
        self.array = array


TOKEN_SHAPE = (8, LANE)


def _carried_layout(carried):
    akeys = list(dict.fromkeys(k for cm in carried for k in cm.arrays))
    fkeys = [(ci, k) for ci, cm in enumerate(carried) for k in cm.fresh]
    arrays = [next(cm.arrays[k] for cm in carried if k in cm.arrays) for k in akeys]
    shapes = [jax.ShapeDtypeStruct(a.shape, a.dtype) for a in arrays] + [carried[ci].fresh[k] for ci, k in fkeys]
    sems = []
    for cm in carried:
        sems += [pltpu.SemaphoreType.DMA((cm.n_sems,)), pltpu.SemaphoreType.DMA((cm.n_sems,))]
    return akeys, fkeys, arrays, shapes, sems


def _carried_reads(carried):
    rkeys = list(dict.fromkeys(k for cm in carried for k in cm.reads))
    return rkeys, [next(cm.reads[k] for cm in carried if k in cm.reads) for k in rkeys]


def _carried_results(carried, akeys, fkeys, outs, rkeys=(), read_refs=()):
    shared = dict(zip(akeys, outs[:len(akeys)]))
    shared.update(zip(rkeys, read_refs))
    res = [{k: shared[k] for k in list(cm.arrays) + [r for r in cm.reads if r in shared]} for cm in carried]
    for (ci, k), o in zip(fkeys, outs[len(akeys):]):
        res[ci][k] = o
    return res


def _pcall(kern, *, name, grid, in_specs, out_specs, out_shape, args, scratch_shapes=(), semantics=None,
           carried=()):
    tokens = [c.array for c in carried if isinstance(c, _Token)]
    carried = [c for c in carried if not isinstance(c, _Token)]
    single = not isinstance(out_shape, (list, tuple))
    out_specs = [out_specs] if single else list(out_specs)
    out_shape = [out_shape] if single else list(out_shape)
    n_real, n_out, n_scr = len(in_specs), len(out_shape), len(scratch_shapes)
    in_specs = list(in_specs) + [pl.BlockSpec(TOKEN_SHAPE, lambda *_: (0, 0))] * len(tokens)
    args = list(args) + tokens
    n_in = len(in_specs)
    if not carried:
        def plain(*refs):
            kern(*refs[:n_real], *refs[n_in:])

        outs = pl.pallas_call(plain if tokens else kern, name=name, grid=grid, in_specs=in_specs,
                              out_specs=out_specs, out_shape=out_shape, scratch_shapes=list(scratch_shapes),
                              compiler_params=_cparams(semantics))(*args)
        return (outs[0] if single else list(outs)), []
    akeys, fkeys, arrays, shapes, sems = _carried_layout(carried)
    rkeys, reads = _carried_reads(carried)
    n_a, n_f, n_r = len(akeys), len(fkeys), len(rkeys)
    total = math.prod(grid)
    mid_step = min(total - 1, (17 * total) // 20)

    def wrapped(*refs):
        ins = refs[:n_real]
        r0 = n_in + n_a
        o0 = r0 + n_r
        outs = refs[o0:o0 + n_out]
        a0 = o0 + n_out
        s0 = a0 + n_a + n_f
        per = _carried_results(carried, akeys, fkeys, refs[a0:s0], rkeys, refs[r0:o0])
        scratch = refs[s0:s0 + n_scr]
        sem = refs[s0 + n_scr:]
        step = 0
        for d, g in enumerate(grid):
            step = step * g + pl.program_id(d)

        @pl.when(step == 0)
        def _():
            for ci, cm in enumerate(carried):
                cm.start(per[ci], sem[2 * ci], sem[2 * ci + 1])

        kern(*ins, *outs, *scratch)

        @pl.when(step == mid_step)
        def _():
            for ci, cm in enumerate(carried):
                if cm.mid is not None:
                    cm.mid(per[ci], sem[2 * ci], sem[2 * ci + 1])

        @pl.when(step == total - 1)
        def _():
            for ci, cm in enumerate(carried):
                cm.finish(per[ci], sem[2 * ci], sem[2 * ci + 1])

    outs = pl.pallas_call(
        wrapped, name=name, grid=grid,
        in_specs=list(in_specs) + [ANY] * (n_a + n_r), out_specs=out_specs + [ANY] * (n_a + n_f),
        out_shape=out_shape + shapes,
        input_output_aliases={n_in + i: n_out + i for i in range(n_a)},
        scratch_shapes=list(scratch_shapes) + sems,
        compiler_params=_cparams(("arbitrary",) * len(grid)),
    )(*args, *arrays, *reads)
    res = _carried_results(carried, akeys, fkeys, outs[n_out:])
    return (outs[0] if single else list(outs[:n_out])), res


def _run_comm(carried, name):
    carried = list(carried)
    akeys, fkeys, arrays, shapes, sems = _carried_layout(carried)
    rkeys, reads = _carried_reads(carried)
    n_a, n_f, n_r = len(akeys), len(fkeys), len(rkeys)

    def body(*refs):
        o0 = n_a + n_r
        per = _carried_results(carried, akeys, fkeys, refs[o0:o0 + n_a + n_f], rkeys, refs[n_a:o0])
        sem = refs[o0 + n_a + n_f:]
        for hook in ("start", "mid", "finish"):
            for ci, cm in enumerate(carried):
                fn = getattr(cm, hook)
                if fn is not None:
                    fn(per[ci], sem[2 * ci], sem[2 * ci + 1])

    outs = pl.pallas_call(
        body, name=name, in_specs=[ANY] * (n_a + n_r), out_specs=[ANY] * (n_a + n_f), out_shape=shapes,
        input_output_aliases={i: i for i in range(n_a)}, scratch_shapes=sems,
    )(*arrays, *reads)
    return _carried_results(carried, akeys, fkeys, outs)


HBM_SPEC = pl.BlockSpec(memory_space=pltpu.HBM)
SEM_SPEC = pl.BlockSpec(memory_space=pltpu.SEMAPHORE)
SPLIT_EFFECT = pltpu.SideEffectType.DATAFLOW_SIDE_EFFECTING


def _in_hbm(a):
    return pltpu.with_memory_space_constraint(a, pltpu.HBM)


def _split_start(items, name, after=None):
    items = list(items)
    akeys, fkeys, arrays, shapes, sems = _carried_layout(items)
    assert not fkeys
    rkeys, reads = _carried_reads(items)
    n_a, n_s = len(akeys), len(sems)
    n_after = n_a + (after is not None)
    n_in = n_after + len(rkeys)

    def body(*refs):
        per = _carried_results(items, akeys, [], refs[n_in:n_in + n_a], rkeys, refs[n_after:n_in])
        sem = refs[n_in + n_a:n_in + n_a + n_s]
        for ci, cm in enumerate(items):
            cm.start(per[ci], sem[2 * ci], sem[2 * ci + 1])
        token = refs[n_in + n_a + n_s]
        token[...] = jnp.zeros_like(token)

    outs = pl.pallas_call(
        body, name=name, in_specs=[HBM_SPEC] * n_a + [ANY] * (n_in - n_a),
        out_specs=[HBM_SPEC] * n_a + [SEM_SPEC] * n_s + [pl.BlockSpec(memory_space=pltpu.VMEM)],
        out_shape=[pltpu.HBM(s.shape, s.dtype) for s in shapes] + sems + [jax.ShapeDtypeStruct(TOKEN_SHAPE, F32)],
        input_output_aliases={i: i for i in range(n_a)},
        compiler_params=pltpu.CompilerParams(has_side_effects=SPLIT_EFFECT),
    )(*[_in_hbm(a) for a in arrays], *([after] if after is not None else []), *reads)
    res = _carried_results(items, akeys, [], outs[:n_a])
    sem_out = outs[n_a:n_a + n_s]
    return res, [(sem_out[2 * ci], sem_out[2 * ci + 1]) for ci in range(len(items))], outs[-1]


def _split_wait(items, sems, after, name):
    items = list(items)
    after = list(after) if isinstance(after, (list, tuple)) else [after]
    akeys, fkeys, arrays, shapes, _ = _carried_layout(items)
    rkeys, reads = _carried_reads(items)
    n_a, n_s = len(akeys), 2 * len(items)
    n_after = n_a + n_s + len(after)
    n_in = n_after + len(rkeys)

    def body(*refs):
        per = _carried_results(items, akeys, [], refs[n_in:], rkeys, refs[n_after:n_in])
        sem = refs[n_a:n_a + n_s]
        for ci, cm in enumerate(items):
            cm.finish(per[ci], sem[2 * ci], sem[2 * ci + 1])

    outs = pl.pallas_call(
        body, name=name, in_specs=[HBM_SPEC] * n_a + [SEM_SPEC] * n_s + [ANY] * (n_in - n_a - n_s),
        out_specs=[HBM_SPEC] * n_a, out_shape=[pltpu.HBM(s.shape, s.dtype) for s in shapes],
        input_output_aliases={i: i for i in range(n_a)},
        compiler_params=pltpu.CompilerParams(has_side_effects=SPLIT_EFFECT),
    )(*arrays, *[s for pair in sems for s in pair], *after, *reads)
    return _carried_results(items, akeys, [], outs)


def _mm(a, b, *, mode, out_dtype, name, res=None, carried=(), fused=None):
    if mode == "nn":
        (m, k), (k2, n) = a.shape, b.shape
    elif mode == "nt":
        (m, k), (n, k2) = a.shape, b.shape
    else:
        (k, m), (k2, n) = a.shape, b.shape
    assert k == k2, (name, a.shape, b.shape)
    tm = _div(m, MM_TILE_CAP, LANE)
    tn = _div(n, MM_TILE_CAP, MXU_WIDTH) if n % MXU_WIDTH == 0 else 0
    if tn < 1024:
        tn = _div(n, MM_TILE_CAP, LANE)
    out_shape = jax.ShapeDtypeStruct((m, n), out_dtype)

    if fused is not None:
        assert mode in ("nn", "nt") and res is None and k <= 2048
        tm, tn = fused["tile"]
        dot = _dot if mode == "nn" else _dot_nt
        n_x = len(fused["ins"])

        summed = [len(e) > 2 for e in fused["outs"]]

        def kern_fused(*refs):
            part = dot(refs[0][...], refs[1][...])
            outs = fused["post"](part, *[r[...] for r in refs[2:2 + n_x]])
            first_row_tile = pl.program_id(1) == 0
            for o_ref, val, acc in zip(refs[2 + n_x:], outs, summed):
                if not acc:
                    o_ref[...] = val.astype(o_ref.dtype)
                    continue

                @pl.when(first_row_tile)
                def _():
                    o_ref[...] = val

                @pl.when(jnp.logical_not(first_row_tile))
                def _():
                    o_ref[...] += val

        def tile_spec(shape, index=lambda i, j: (i, j)):
            return pl.BlockSpec(shape, lambda j, i: index(i, j))

        return _pcall(
            kern_fused, name=name, grid=(n // tn, m // tm),
            in_specs=[pl.BlockSpec((tm, k), lambda j, i: (i, 0)),
                      pl.BlockSpec((k, tn), lambda j, i: (0, j)) if mode == "nn"
                      else pl.BlockSpec((tn, k), lambda j, i: (j, 0))] + [tile_spec(*e[1:]) for e in fused["ins"]],
            out_specs=[tile_spec(*e[1:]) for e in fused["outs"]], out_shape=[e[0] for e in fused["outs"]],
            args=(a, b, *[e[0] for e in fused["ins"]]), semantics=("parallel", "arbitrary"), carried=carried)

    if mode == "tn":
        has_res_tn = res is not None

        def kern_tn(*refs):
            a_ref, b_ref, o_ref, at_ref = refs[0], refs[1], refs[-2], refs[-1]

            @pl.when(pl.program_id(1) == 0)
            def _():
                step = min(TRANSPOSE_CHUNK, k)
                for c0 in range(0, k, step):
                    at_ref[:, c0:c0 + step] = a_ref[c0:c0 + step, :].T

            part = jnp.dot(at_ref[...], b_ref[...].astype(BF16), preferred_element_type=F32)
            if has_res_tn:
                part = part + refs[2][...].astype(F32)
            o_ref[...] = part.astype(o_ref.dtype)

        o_spec = pl.BlockSpec((tm, tn), lambda i, j: (i, j))
        return _pcall(
            kern_tn, name=name, grid=(m // tm, n // tn),
            in_specs=[pl.BlockSpec((k, tm), lambda i, j: (0, i)),
                      pl.BlockSpec((k, tn), lambda i, j: (0, j))] + [o_spec] * has_res_tn,
            out_specs=o_spec, out_shape=out_shape, args=(a, b) + ((res,) if has_res_tn else ()),
            scratch_shapes=[pltpu.VMEM((tm, k), BF16)],
            semantics=("parallel", "arbitrary"), carried=carried)

    tk = k if k <= 2048 else _div(k, 3072, LANE)
    nk = k // tk
    a_spec = pl.BlockSpec((tm, tk), lambda i, j, kk: (i, kk))
    if mode == "nn":
        b_spec = pl.BlockSpec((tk, tn), lambda i, j, kk: (kk, j))
        dot = _dot
    else:
        b_spec = pl.BlockSpec((tn, tk), lambda i, j, kk: (j, kk))
        dot = _dot_nt
    o_spec = pl.BlockSpec((tm, tn), lambda i, j, kk: (i, j))
    in_specs = [a_spec, b_spec]
    args = [a, b]
    if res is not None:
        in_specs.append(o_spec)
        args.append(res)
    has_res = res is not None

    def kern(*refs):
        a_ref, b_ref = refs[0], refs[1]
        r_ref = refs[2] if has_res else None
        o_ref = refs[3] if has_res else refs[2]
        if nk == 1:
            part = dot(a_ref[...], b_ref[...])
            if has_res:
                part = part + r_ref[...]
            o_ref[...] = part.astype(o_ref.dtype)
            return
        @pl.when(pl.program_id(2) == 0)
        def _():
            o_ref[...] = r_ref[...] if has_res else jnp.zeros_like(o_ref)

        o_ref[...] += dot(a_ref[...], b_ref[...])

    assert nk == 1 or out_dtype == F32, name
    return _pcall(
        kern, name=name, grid=(m // tm, n // tn, nk),
        in_specs=in_specs, out_specs=o_spec, out_shape=out_shape, args=args,
        semantics=("parallel", "parallel", "arbitrary"), carried=carried)


ROW_BLOCK = 512


def _rms_fwd(x, g, name):
    t, d = x.shape
    tr = min(ROW_BLOCK, t)

    def kern(x_ref, g_ref, o_ref):
        xf = x_ref[...]
        r = lax.rsqrt(jnp.mean(xf * xf, axis=-1, keepdims=True) + RMS_EPS)
        o_ref[...] = (xf * r * g_ref[...]).astype(o_ref.dtype)

    return pl.pallas_call(
        kern, name=name, grid=(t // tr,),
        in_specs=[pl.BlockSpec((tr, d), lambda i: (i, 0)), pl.BlockSpec((1, d), lambda i: (0, 0))],
        out_specs=pl.BlockSpec((tr, d), lambda i: (i, 0)),
        out_shape=jax.ShapeDtypeStruct((t, d), BF16),
        compiler_params=_cparams(("parallel",)),
    )(x, g)


def _rms_fwd_halves(x, g, pos, name):
    t, d = x.shape
    tr = min(ROW_BLOCK, t)
    half = d // 2

    def kern(pos_ref, x_ref, g_ref, o_ref, sib_ref, own_ref):
        xf = x_ref[...]
        r = lax.rsqrt(jnp.mean(xf * xf, axis=-1, keepdims=True) + RMS_EPS)
        h = (xf * r * g_ref[...]).astype(o_ref.dtype)
        o_ref[...] = h
        is_south = pos_ref[0] == 0
        sib_ref[...] = jnp.where(is_south, h[:, half:], h[:, :half])
        own_ref[...] = jnp.where(is_south, h[:, :half], h[:, half:])

    row = pl.BlockSpec((tr, d), lambda i, pos_ref: (i, 0))
    part = pl.BlockSpec((tr, half), lambda i, pos_ref: (i, 0))
    return pl.pallas_call(
        kern, name=name,
        grid_spec=_grid_spec((t // tr,), [row, pl.BlockSpec((1, d), lambda i, pos_ref: (0, 0))], [row, part, part]),
        out_shape=[jax.ShapeDtypeStruct((t, d), BF16)] + [jax.ShapeDtypeStruct((t, half), BF16)] * 2,
        compiler_params=_cparams(("parallel",)),
    )(pos, x, g)


def _rms_bwd(x, g, dh, res, name, bf16_copy=False):
    t, d = x.shape
    tr = min(ROW_BLOCK, t)
    has_res = res is not None

    def kern(*refs):
        x_ref, g_ref, dh_ref = refs[:3]
        r_ref = refs[3] if has_res else None
        dx_ref, dg_ref = refs[3 + has_res], refs[4 + has_res]
        xf = x_ref[...]
        r = lax.rsqrt(jnp.mean(xf * xf, axis=-1, keepdims=True) + RMS_EPS)
        xn = xf * r
        dh_ = dh_ref[...]
        dhg = dh_ * g_ref[...]
        dx = r * (dhg - xn * jnp.mean(dhg * xn, axis=-1, keepdims=True))
        if has_res:
            dx = dx + r_ref[...]
        dx_ref[...] = dx
        if bf16_copy:
            refs[-1][...] = dx.astype(BF16)
        part = jnp.sum(dh_ * xn, axis=0, keepdims=True)

        @pl.when(pl.program_id(0) == 0)
        def _():
            dg_ref[...] = part

        @pl.when(pl.program_id(0) > 0)
        def _():
            dg_ref[...] += part

    row = pl.BlockSpec((tr, d), lambda i: (i, 0))
    vec = pl.BlockSpec((1, d), lambda i: (0, 0))
    in_specs = [row, vec, row] + ([row] if has_res else [])
    args = [x, g, dh] + ([res] if has_res else [])
    return pl.pallas_call(
        kern, name=name, grid=(t // tr,), in_specs=in_specs, out_specs=[row, vec] + [row] * bf16_copy,
        out_shape=[jax.ShapeDtypeStruct((t, d), F32), jax.ShapeDtypeStruct((1, d), F32)]
        + [jax.ShapeDtypeStruct((t, d), BF16)] * bf16_copy,
        compiler_params=_cparams(("arbitrary",)),
    )(*args)


def _loss_head(x3, g, target, name, core_row=None):
    t, d = x3.shape
    tr = ROW_BLOCK
    with_halves = core_row is not None

    def kern(x_ref, g_ref, t_ref, *rest):
        dx_ref, dg_ref, loss_ref, dxb_ref = rest[with_halves:with_halves + 4]
        xf = x_ref[...]
        r = lax.rsqrt(jnp.mean(xf * xf, axis=-1, keepdims=True) + RMS_EPS)
        xn = xf * r
        gg = g_ref[...]
        err = xn * gg - t_ref[...]
        lpart = 0.5 * jnp.sum(jnp.mean(err * err, axis=-1, keepdims=True), axis=0, keepdims=True)
        dy = err * (1.0 / d)
        dyg = dy * gg
        dx = r * (dyg - xn * jnp.mean(dyg * xn, axis=-1, keepdims=True))
        dx_ref[...] = dx
        dxb = dx.astype(BF16)
        dxb_ref[...] = dxb
        if with_halves:
            rest[-2][...], rest[-1][...] = _column_halves(dxb, rest[0][...])
        gpart = jnp.sum(dy * xn, axis=0, keepdims=True)
        lrow = jnp.broadcast_to(lpart, (1, LANE))

        @pl.when(pl.program_id(0) == 0)
        def _():
            dg_ref[...] = gpart
            loss_ref[...] = lrow

        @pl.when(pl.program_id(0) > 0)
        def _():
            dg_ref[...] += gpart
            loss_ref[...] += lrow

    row = pl.BlockSpec((tr, d), lambda i: (i, 0))
    vec = pl.BlockSpec((1, d), lambda i: (0, 0))
    part = pl.BlockSpec((tr, d // 2), lambda i: (i, 0))
    return pl.pallas_call(
        kern, name=name, grid=(t // tr,), in_specs=[row, vec, row] + [vec] * with_halves,
        out_specs=[row, vec, pl.BlockSpec((1, LANE), lambda i: (0, 0)), row] + [part] * (2 * with_halves),
        out_shape=[jax.ShapeDtypeStruct((t, d), F32), jax.ShapeDtypeStruct((1, d), F32),
                   jax.ShapeDtypeStruct((1, LANE), F32), jax.ShapeDtypeStruct((t, d), BF16)]
        + [jax.ShapeDtypeStruct((t, d // 2), BF16)] * (2 * with_halves),
        compiler_params=_cparams(("arbitrary",)),
    )(x3, g, target, *([core_row] if with_halves else []))


ATT_SCALE = HEAD_DIM ** -0.5
Q_BLOCK0, K_BLOCK0, V_BLOCK0 = 0, ATT_WIDTH // HEAD_DIM, 2 * ATT_WIDTH // HEAD_DIM


def _residue_rows(dil, r, n):
    if dil == 1:
        return pl.ds(n * ATT_BLOCK, ATT_BLOCK)
    return pl.ds(n * ATT_BLOCK * dil + r, ATT_BLOCK, stride=dil)


def _band_mask(with_prev):
    width = 2 * ATT_BLOCK if with_prev else ATT_BLOCK
    iq = lax.broadcasted_iota(jnp.int32, (ATT_BLOCK, width), 0)
    ik = lax.broadcasted_iota(jnp.int32, (ATT_BLOCK, width), 1)
    if not with_prev:
        return ik <= iq
    return ((ik < ATT_BLOCK) & (iq <= ik)) | ((ik >= ATT_BLOCK) & ((ik - ATT_BLOCK) <= iq))


def _band_keys(ref, dil, r, n):
    own = ref[_residue_rows(dil, r, n), :]
    if n == 0:
        return own
    return jnp.concatenate([ref[_residue_rows(dil, r, n - 1), :], own], axis=0)


def _attn_col_spec(base, grp):
    return pl.BlockSpec((SEQ, HEAD_DIM), lambda h: (0, base + grp * ATT_HEADS + h))


def _attn_fwd(proj, grp, name, carried=()):
    _, dil = ATT_GROUPS[grp]
    nb = SEQ // dil // ATT_BLOCK

    def kern(q_ref, k_ref, v_ref, o_ref, lse_ref):
        for r in range(dil):
            for n in range(nb):
                rows = _residue_rows(dil, r, n)
                s = _dot_nt(q_ref[rows, :], _band_keys(k_ref, dil, r, n)) * ATT_SCALE
                s = jnp.where(_band_mask(n > 0), s, -jnp.inf)
                m = jnp.max(s, axis=-1, keepdims=True)
                p = jnp.exp(s - m)
                l = jnp.sum(p, axis=-1, keepdims=True)
                o_ref[rows, :] = _dot(p / l, _band_keys(v_ref, dil, r, n))
                lse_ref[rows, :] = jnp.broadcast_to(m + jnp.log(l), (ATT_BLOCK, HEAD_DIM))

    out_spec = pl.BlockSpec((SEQ, HEAD_DIM), lambda h: (0, h))
    return _pcall(
        kern, name=name, grid=(ATT_HEADS,),
        in_specs=[_attn_col_spec(Q_BLOCK0, grp), _attn_col_spec(K_BLOCK0, grp), _attn_col_spec(V_BLOCK0, grp)],
        out_specs=[out_spec, out_spec],
        out_shape=[jax.ShapeDtypeStruct((SEQ, ATT_OUT), F32)] * 2, args=(proj, proj, proj),
        semantics=("parallel",), carried=carried)


def _attn_weights(l0, l1, l2):
    mx = jnp.maximum(jnp.maximum(l0, l1), l2)
    e0, e1, e2 = jnp.exp(l0 - mx), jnp.exp(l1 - mx), jnp.exp(l2 - mx)
    den = e0 + e1 + e2
    return e0 / den, e1 / den, e2 / den


def _attn_merge_fwd(outs, lses, name):
    tr = ROW_BLOCK

    def kern(o0, o1, o2, l0, l1, l2, out_ref):
        a0, a1, a2 = _attn_weights(l0[...], l1[...], l2[...])
        out_ref[...] = (a0 * o0[...] + a1 * o1[...] + a2 * o2[...]).astype(out_ref.dtype)

    spec = pl.BlockSpec((tr, ATT_OUT), lambda i: (i, 0))
    return pl.pallas_call(
        kern, name=name, grid=(SEQ // tr,), in_specs=[spec] * 6, out_specs=spec,
        out_shape=jax.ShapeDtypeStruct((SEQ, ATT_OUT), BF16),
        compiler_params=_cparams(("parallel",)),
    )(*outs, *lses)


def _attn_merge_bwd(outs, lses, do_att, name, carried=()):
    tr = ROW_BLOCK

    def kern(o0, o1, o2, l0, l1, l2, do_ref, d0, d1, d2, t0, t1, t2):
        alphas = _attn_weights(l0[...], l1[...], l2[...])
        do = do_ref[...]
        o_att = alphas[0] * o0[...] + alphas[1] * o1[...] + alphas[2] * o2[...]
        prod = do * o_att
        parts = []
        for h in range(ATT_HEADS):
            sl = slice(h * HEAD_DIM, (h + 1) * HEAD_DIM)
            tot = jnp.sum(prod[:, sl], axis=-1, keepdims=True)
            parts.append(jnp.broadcast_to(tot, (tr, HEAD_DIM)))
        dd = jnp.concatenate(parts, axis=1)
        for a, d_ref, t_ref in zip(alphas, (d0, d1, d2), (t0, t1, t2)):
            d_ref[...] = a * do
            t_ref[...] = -a * dd

    spec = pl.BlockSpec((tr, ATT_OUT), lambda i: (i, 0))
    res, cres = _pcall(
        kern, name=name, grid=(SEQ // tr,), in_specs=[spec] * 7, out_specs=[spec] * 6,
        out_shape=[jax.ShapeDtypeStruct((SEQ, ATT_OUT), F32)] * 6, args=(*outs, *lses, do_att),
        semantics=("parallel",), carried=carried)
    return (res[:3], res[3:]), cres


def _attn_bwd(proj, grp, lse, do_g, dl_g, name, carried=()):
    _, dil = ATT_GROUPS[grp]
    nb = SEQ // dil // ATT_BLOCK

    def kern(q_ref, k_ref, v_ref, do_ref, lse_ref, dl_ref, dq_ref, dk_ref, dv_ref, dq_acc, dk_acc, dv_acc):
        dk_acc[...] = jnp.zeros_like(dk_acc)
        dv_acc[...] = jnp.zeros_like(dv_acc)
        for r in range(dil):
            for n in range(nb):
                rows = _residue_rows(dil, r, n)
                q, do = q_ref[rows, :], do_ref[rows, :]
                kk, vv = _band_keys(k_ref, dil, r, n), _band_keys(v_ref, dil, r, n)
                s = _dot_nt(q, kk) * ATT_SCALE
                p = jnp.where(_band_mask(n > 0), jnp.exp(s - lse_ref[rows, :][:, :1]), 0.0)
                ds = p * (_dot_nt(do, vv) + dl_ref[rows, :][:, :1])
                dq_acc[rows, :] = _dot(ds, kk) * ATT_SCALE
                dk = _dot_tn(ds, q) * ATT_SCALE
                dv = _dot_tn(p, do)
                if n > 0:
                    prev = _residue_rows(dil, r, n - 1)
                    dk_acc[prev, :] += dk[:ATT_BLOCK]
                    dv_acc[prev, :] += dv[:ATT_BLOCK]
                    dk, dv = dk[ATT_BLOCK:], dv[ATT_BLOCK:]
                dk_acc[rows, :] += dk
                dv_acc[rows, :] += dv
        dq_ref[...] = dq_acc[...].astype(dq_ref.dtype)
        dk_ref[...] = dk_acc[...].astype(dk_ref.dtype)
        dv_ref[...] = dv_acc[...].astype(dv_ref.dtype)

    spec = pl.BlockSpec((SEQ, HEAD_DIM), lambda h: (0, h))
    return _pcall(
        kern, name=name, grid=(ATT_HEADS,),
        in_specs=[_attn_col_spec(Q_BLOCK0, grp), _attn_col_spec(K_BLOCK0, grp), _attn_col_spec(V_BLOCK0, grp),
                  spec, spec, spec],
        out_specs=[spec] * 3,
        out_shape=[jax.ShapeDtypeStruct((SEQ, ATT_OUT), BF16)] * 3, args=(proj, proj, proj, do_g, lse, dl_g),
        scratch_shapes=[pltpu.VMEM((SEQ, HEAD_DIM), F32)] * 3,
        semantics=("parallel",), carried=carried)


HG_HEADS_PER_STEP = 8
HG_BLOCK_W = 4 * HEAD_DIM
HG_BLOCKS = HG_HEADS_PER_STEP * HEAD_DIM // HG_BLOCK_W
HG_STEP_W = HG_HEADS_PER_STEP * HEAD_DIM
HG_Q_BLK = (3 * ATT_WIDTH) // HG_BLOCK_W
HG_N_CHUNKS = SEQ // HG_CHUNK
HG_MID = HG_CHUNK // 2


def _lower_bound(lb_ref, sl):
    l0, l1 = lb_ref[0:1, sl], lb_ref[1:2, sl]
    mx = jnp.maximum(l0, l1)
    e0, e1 = jnp.exp(l0 - mx), jnp.exp(l1 - mx)
    return e0 / (e0 + e1)


def _tri(lower):
    i = lax.broadcasted_iota(jnp.int32, (HG_CHUNK, HG_CHUNK), 0)
    j = lax.broadcasted_iota(jnp.int32, (HG_CHUNK, HG_CHUNK), 1)
    return (i >= j) if lower else (i <= j)


def _head_mean(x):
    parts = []
    for hd in range(x.shape[1] // HEAD_DIM):
        m = jnp.mean(x[:, hd * HEAD_DIM:(hd + 1) * HEAD_DIM], axis=-1, keepdims=True)
        parts.append(jnp.broadcast_to(m, (x.shape[0], HEAD_DIM)))
    return jnp.concatenate(parts, axis=1)


def _hg_chunk_terms(qh, fh, lb):
    sig = _sigmoid(fh)
    f = lb + (1.0 - lb) * sig
    k = 1.0 - f
    b = _dot_exact(_tri(True).astype(F32), jnp.log(f))
    bl = b[HG_CHUNK - 1:HG_CHUNK, :]
    br = b[HG_MID:HG_MID + 1, :]
    sq = _sigmoid(qh)
    q = qh * sq
    return dict(sig=sig, f=f, k=k, b=b, bl=bl, br=br, sq=sq, q=q,
                e1=jnp.exp(bl - b), e2=jnp.exp(b), e3=jnp.exp(b - br), e4=jnp.exp(br - b))


def _hg_fwd(proj, lbw, normw, name, carried=()):
    def in_blks(off):
        return [pl.BlockSpec((HG_CHUNK, HG_BLOCK_W), lambda hp, n, b=b: (n, HG_Q_BLK + off + hp * HG_BLOCKS + b))
                for b in range(HG_BLOCKS)]

    def kern(*refs):
        q_refs, f_refs, i_refs, g_refs = (refs[k * HG_BLOCKS:(k + 1) * HG_BLOCKS] for k in range(4))
        lb_ref, nw_ref, oraw_ref, ohg_ref, st_ref, state = refs[4 * HG_BLOCKS:]

        @pl.when(pl.program_id(1) == 0)
        def _():
            state[...] = jnp.zeros_like(state)

        causal = _tri(True)
        wide = lambda rs: jnp.concatenate([r[...] for r in rs], axis=1)
        t = _hg_chunk_terms(wide(q_refs), wide(f_refs), _lower_bound(lb_ref, slice(None)))
        v, gh = wide(i_refs), wide(g_refs)
        kd, qb, qr, kr = t["k"] * t["e1"], t["q"] * t["e2"], t["q"] * t["e3"], t["k"] * t["e4"]
        decay = jnp.exp(t["bl"])
        outs = []
        for hd in range(HG_HEADS_PER_STEP):
            sl = slice(hd * HEAD_DIM, (hd + 1) * HEAD_DIM)
            st = state[hd]
            st_ref[0, hd] = st
            a = jnp.where(causal, _dot_nt(qr[:, sl], kr[:, sl]), 0.0)
            outs.append(_dot_nt(qb[:, sl], st) + _dot(a, v[:, sl]))
            state[hd] = st * decay[:, sl] + _dot_tn(v[:, sl], kd[:, sl])
        o = jnp.concatenate(outs, axis=1)
        oraw_ref[...] = o
        r = lax.rsqrt(_head_mean(o * o) + RMS_EPS)
        nw = jnp.tile(nw_ref[...], (1, HG_HEADS_PER_STEP))
        ohg_ref[...] = (o * r * nw * (gh * _sigmoid(gh))).astype(ohg_ref.dtype)

    out_blk = pl.BlockSpec((HG_CHUNK, HG_STEP_W), lambda hp, n: (n, hp))
    return _pcall(
        kern, name=name, grid=(HG_HEADS // HG_HEADS_PER_STEP, HG_N_CHUNKS),
        in_specs=[*in_blks(0), *in_blks(2), *in_blks(4), *in_blks(6),
                  pl.BlockSpec((2, HG_STEP_W), lambda hp, n: (0, hp)),
                  pl.BlockSpec((1, HEAD_DIM), lambda hp, n: (0, 0))],
        out_specs=[out_blk, out_blk,
                   pl.BlockSpec((1, HG_HEADS_PER_STEP, HEAD_DIM, HEAD_DIM), lambda hp, n: (n, hp, 0, 0))],
        out_shape=[jax.ShapeDtypeStruct((SEQ, HG_WIDTH), F32), jax.ShapeDtypeStruct((SEQ, HG_WIDTH), BF16),
                   jax.ShapeDtypeStruct((HG_N_CHUNKS, HG_HEADS, HEAD_DIM, HEAD_DIM), F32)],
        args=(*[proj] * (4 * HG_BLOCKS), lbw, normw),
        scratch_shapes=[pltpu.VMEM((HG_HEADS_PER_STEP, HEAD_DIM, HEAD_DIM), F32)],
        semantics=("parallel", "arbitrary"), carried=carried)


def _hg_bwd(proj, lbw, normw, oraw, states, do_hg, name, carried=()):
    last = HG_N_CHUNKS - 1

    def in_blks(off):
        return [pl.BlockSpec((HG_CHUNK, HG_BLOCK_W),
                             lambda hp, n, b=b: (last - n, HG_Q_BLK + off + hp * HG_BLOCKS + b))
                for b in range(HG_BLOCKS)]

    blk = pl.BlockSpec((HG_CHUNK, HG_STEP_W), lambda hp, n: (last - n, hp))

    def kern(*refs):
        q_refs, f_refs, i_refs, g_refs = (refs[k * HG_BLOCKS:(k + 1) * HG_BLOCKS] for k in range(4))
        (lb_ref, nw_ref, oraw_ref, st_ref, do_ref, dq_ref, df_ref, di_ref, dg_ref, dlb_ref, dnw_ref,
         dstate) = refs[4 * HG_BLOCKS:]
        first = pl.program_id(1) == 0

        @pl.when(first)
        def _():
            dstate[...] = jnp.zeros_like(dstate)

        causal = _tri(True)
        wide = lambda rs: jnp.concatenate([r[...] for r in rs], axis=1)
        cat = lambda parts: jnp.concatenate(parts, axis=1)
        qh, fh, v, gh = wide(q_refs), wide(f_refs), wide(i_refs), wide(g_refs)
        o, dout = oraw_ref[...], do_ref[...]
        nw = jnp.tile(nw_ref[...], (1, HG_HEADS_PER_STEP))
        sgg = _sigmoid(gh)
        r = lax.rsqrt(_head_mean(o * o) + RMS_EPS)
        xn = o * r
        dg_ref[...] = (dout * xn * nw * (sgg * (1.0 + gh * (1.0 - sgg)))).astype(dg_ref.dtype)
        don = dout * (gh * sgg)
        dnw_wide = jnp.sum(don * xn, axis=0, keepdims=True)
        dnw_tot = dnw_wide[:, :HEAD_DIM]
        for hd in range(1, HG_HEADS_PER_STEP):
            dnw_tot = dnw_tot + dnw_wide[:, hd * HEAD_DIM:(hd + 1) * HEAD_DIM]
        tt = don * nw
        do = r * (tt - xn * _head_mean(tt * xn))
        lb = _lower_bound(lb_ref, slice(None))
        t = _hg_chunk_terms(qh, fh, lb)
        k, q = t["k"], t["q"]
        kd, qb, qr, kr = k * t["e1"], q * t["e2"], q * t["e3"], k * t["e4"]
        decay = jnp.exp(t["bl"])
        dqb, dqr, dkr, dkd, dv, ddecay = [], [], [], [], [], []
        for hd in range(HG_HEADS_PER_STEP):
            sl = slice(hd * HEAD_DIM, (hd + 1) * HEAD_DIM)
            st = st_ref[0, hd]
            dstn = dstate[hd]
            a = jnp.where(causal, _dot_nt(qr[:, sl], kr[:, sl]), 0.0)
            da = jnp.where(causal, _dot_nt(do[:, sl], v[:, sl]), 0.0)
            dqb.append(_dot(do[:, sl], st))
            dv.append(_dot_tn(a, do[:, sl]) + _dot_nt(kd[:, sl], dstn))
            dqr.append(_dot(da, kr[:, sl]))
            dkr.append(_dot_tn(da, qr[:, sl]))
            dkd.append(_dot(v[:, sl], dstn))
            ddecay.append(jnp.sum(dstn * st, axis=0, keepdims=True))
            dstate[hd] = dstn * decay[:, sl] + _dot_tn(do[:, sl], qb[:, sl])
        dqb, dqr, dkr, dkd, dv, ddecay = cat(dqb), cat(dqr), cat(dkr), cat(dkd), cat(dv), cat(ddecay)
        dq = dqb * t["e2"] + dqr * t["e3"]
        dk = dkd * t["e1"] + dkr * t["e4"]
        db = dqb * qb + dqr * qr - dkr * kr - dkd * kd
        dbl = jnp.sum(dkd * kd, axis=0, keepdims=True) + ddecay * decay
        dbr = jnp.sum(dkr * kr - dqr * qr, axis=0, keepdims=True)
        rows = lax.broadcasted_iota(jnp.int32, db.shape, 0)
        dlf = _dot_exact(_tri(False).astype(F32), db) + dbl + jnp.where(rows <= HG_MID, dbr, 0.0)
        df = dlf / t["f"] - dk
        sig, sq = t["sig"], t["sq"]
        df_ref[...] = (df * (1.0 - lb) * sig * (1.0 - sig)).astype(df_ref.dtype)
        dlb_row = jnp.sum(df * (1.0 - sig), axis=0, keepdims=True)
        dq_ref[...] = (dq * (sq * (1.0 + qh * (1.0 - sq)))).astype(dq_ref.dtype)
        di_ref[...] = dv.astype(di_ref.dtype)
        dnw_blk = jnp.broadcast_to(dnw_tot, (8, HEAD_DIM))

        @pl.when(first)
        def _():
            dlb_ref[...] = dlb_row
            dnw_ref[...] = dnw_blk

        @pl.when(jnp.logical_not(first))
        def _():
            dlb_ref[...] += dlb_row
            dnw_ref[...] += dnw_blk

    n_hp = HG_HEADS // HG_HEADS_PER_STEP
    outs, cres = _pcall(
        kern, name=name, grid=(n_hp, HG_N_CHUNKS),
        in_specs=[*in_blks(0), *in_blks(2), *in_blks(4), *in_blks(6),
                  pl.BlockSpec((2, HG_STEP_W), lambda hp, n: (0, hp)),
                  pl.BlockSpec((1, HEAD_DIM), lambda hp, n: (0, 0)),
                  blk,
                  pl.BlockSpec((1, HG_HEADS_PER_STEP, HEAD_DIM, HEAD_DIM), lambda hp, n: (last - n, hp, 0, 0)),
                  blk],
        out_specs=[blk, blk, blk, blk,
                   pl.BlockSpec((1, HG_STEP_W), lambda hp, n: (0, hp)),
                   pl.BlockSpec((8, HEAD_DIM), lambda hp, n: (hp, 0))],
        out_shape=[jax.ShapeDtypeStruct((SEQ, HG_WIDTH), BF16)] * 4
        + [jax.ShapeDtypeStruct((1, HG_WIDTH), F32), jax.ShapeDtypeStruct((8 * n_hp, HEAD_DIM), F32)],
        args=(*[proj] * (4 * HG_BLOCKS), lbw, normw, oraw, states, do_hg),
        scratch_shapes=[pltpu.VMEM((HG_HEADS_PER_STEP, HEAD_DIM, HEAD_DIM), F32)],
        semantics=("parallel", "arbitrary"), carried=carried)
    dqh, dfh, dih, dgh, dlb, dnw = outs
    return (dqh, dfh, dih, dgh, dlb, [dnw[8 * i:8 * i + 1] for i in range(n_hp)]), cres


GATE_BLOCK_W = 512
GATE_A_BLK = (3 * ATT_WIDTH + 4 * HG_WIDTH) // GATE_BLOCK_W
GATE_B_BLK = GATE_A_BLK + D_MODEL // GATE_BLOCK_W


GATE_TILE = (1024, GATE_BLOCK_W)


def _gate_ins(proj, ya, yb=None):
    ins = [(proj, GATE_TILE, lambda i, j: (i, GATE_A_BLK + j)), (proj, GATE_TILE, lambda i, j: (i, GATE_B_BLK + j)),
           (ya, GATE_TILE)]
    return ins + ([(yb, GATE_TILE)] if yb is not None else [])


def _branch_b_gate(o_hg, w_b, proj, ya, name, carried=()):
    def post(yb, ga, gb, ya_):
        return yb, _sigmoid(ga) * ya_ + _sigmoid(gb) * yb

    return _mm(o_hg, w_b, mode="nn", out_dtype=F32, name=name, carried=carried, fused=dict(
        tile=GATE_TILE, ins=_gate_ins(proj, ya), post=post,
        outs=[(jax.ShapeDtypeStruct((SEQ, D_MODEL), F32), GATE_TILE), (jax.ShapeDtypeStruct((SEQ, D_MODEL), BF16), GATE_TILE)]))


def _dmerged_gate_bwd(dx1, w_out, proj, ya, yb, name, carried=()):
    def post(dm, ga, gb, ya_, yb_):
        sa, sb = _sigmoid(ga), _sigmoid(gb)
        return dm * sa, dm * sb, dm * ya_ * sa * (1.0 - sa), dm * yb_ * sb * (1.0 - sb)

    return _mm(dx1, w_out, mode="nt", out_dtype=BF16, name=name, carried=carried, fused=dict(
        tile=GATE_TILE, ins=_gate_ins(proj, ya, yb), post=post,
        outs=[(jax.ShapeDtypeStruct((SEQ, D_MODEL), BF16), GATE_TILE)] * 4))


NORM_TILE = (512, D_MODEL)


def _column_halves(v, core_row):
    half = v.shape[1] // 2
    south = core_row[:, :half] < 0.5
    lo, hi = v[:, :half], v[:, half:]
    return jnp.where(south, hi, lo), jnp.where(south, lo, hi)


def _residual_norm(a, w, x, g, name, carried=(), core_row=None):
    vec = ((1, D_MODEL), lambda i, j: (0, j))
    half_tile = (NORM_TILE[0], D_MODEL // 2)

    def post(part, x_, g_, *core):
        xn = part + x_
        r = lax.rsqrt(jnp.mean(xn * xn, axis=-1, keepdims=True) + RMS_EPS)
        hn = (xn * r * g_).astype(BF16)
        return (xn, hn) + (_column_halves(hn, core[0]) if core else ())

    wide = [(jax.ShapeDtypeStruct((SEQ, D_MODEL), F32), NORM_TILE), (jax.ShapeDtypeStruct((SEQ, D_MODEL), BF16), NORM_TILE)]
    halves = [(jax.ShapeDtypeStruct((SEQ, D_MODEL // 2), BF16), half_tile)] * 2
    return _mm(a, w, mode="nn", out_dtype=F32, name=name, carried=carried, fused=dict(
        tile=NORM_TILE, post=post, ins=[(x, NORM_TILE), (g, *vec)] + ([(core_row, *vec)] if core_row is not None else []),
        outs=wide + (halves if core_row is not None else [])))


def _norm_bwd_residual(dq, w, x, g, res, name, carried=()):
    def post(dh, x_, g_, res_):
        r = lax.rsqrt(jnp.mean(x_ * x_, axis=-1, keepdims=True) + RMS_EPS)
        xn = x_ * r
        dhg = dh * g_
        dx = r * (dhg - xn * jnp.mean(dhg * xn, axis=-1, keepdims=True)) + res_
        return dx, jnp.sum(dh * xn, axis=0, keepdims=True), dx

    vec = ((1, D_MODEL), lambda i, j: (0, j))
    return _mm(dq, w, mode="nt", out_dtype=F32, name=name, carried=carried, fused=dict(
        tile=NORM_TILE, ins=[(x, NORM_TILE), (g, *vec), (res, NORM_TILE)], post=post,
        outs=[(jax.ShapeDtypeStruct((SEQ, D_MODEL), F32), NORM_TILE), (jax.ShapeDtypeStruct((1, D_MODEL), F32), *vec),
              (jax.ShapeDtypeStruct((SEQ, D_MODEL), BF16), NORM_TILE)]))


FF_SHARD = D_FF // N_CHIPS


FF_TILE_ROWS = 512


def _swiglu_tile(ab):
    a, b = ab[:, :FF_SHARD], ab[:, FF_SHARD:]
    return a * _sigmoid(a) * b


def _swiglu_grad_tile(du, ab):
    a, b = ab[:, :FF_SHARD], ab[:, FF_SHARD:]
    sg = _sigmoid(a)
    return jnp.concatenate([du * b * (sg * (1.0 + a * (1.0 - sg))), du * (a * sg)], axis=1)


def _ff_up(hf, w13, name, carried=()):
    wide, narrow = (FF_TILE_ROWS, 2 * FF_SHARD), (FF_TILE_ROWS, FF_SHARD)
    return _mm(hf, w13, mode="nn", out_dtype=F32, name=name, carried=carried, fused=dict(
        tile=wide, ins=[],
        outs=[(jax.ShapeDtypeStruct((SEQ, 2 * D_FF), F32), wide), (jax.ShapeDtypeStruct((SEQ, D_FF), BF16), narrow)],
        post=lambda p: (p, _swiglu_tile(p))))


def _ff_down_bwd(dx3, w2, ab, name, carried=()):
    wide, narrow = (FF_TILE_ROWS, 2 * FF_SHARD), (FF_TILE_ROWS, FF_SHARD)
    out, res = _mm(dx3, w2, mode="nt", out_dtype=BF16, name=name, carried=carried, fused=dict(
        tile=narrow, ins=[(ab, wide)], outs=[(jax.ShapeDtypeStruct((SEQ, 2 * D_FF), BF16), wide)],
        post=lambda du, ab_: (_swiglu_grad_tile(du, ab_),)))
    return out[0], res


CROSS_ROWS = 512


def _cross_fwd(qc, kvc, name, carried=()):
    def kern(q_ref, k_ref, v_ref, o_ref):
        s = _dot_nt(q_ref[...], k_ref[...]) * ATT_SCALE
        m = jnp.max(s, axis=-1, keepdims=True)
        e = jnp.exp(s - m)
        p = e / jnp.sum(e, axis=-1, keepdims=True)
        o_ref[...] = _dot(p, v_ref[...]).astype(o_ref.dtype)

    qblk = pl.BlockSpec((CROSS_ROWS, HEAD_DIM), lambda h, i: (i, h))
    return _pcall(
        kern, name=name, grid=(CROSS_HEADS, SEQ // CROSS_ROWS),
        in_specs=[qblk, pl.BlockSpec((MEM_LEN, HEAD_DIM), lambda h, i: (0, h)),
                  pl.BlockSpec((MEM_LEN, HEAD_DIM), lambda h, i: (0, CROSS_HEADS + h))],
        out_specs=qblk, out_shape=jax.ShapeDtypeStruct((SEQ, CROSS_WIDTH), BF16), args=(qc, kvc, kvc),
        semantics=("parallel", "parallel"), carried=carried)


def _cross_bwd(qc, kvc, doc, name):
    def kern(q_ref, k_ref, v_ref, do_ref, dq_ref, dk_ref, dv_ref):
        q, k, v, do = q_ref[...], k_ref[...], v_ref[...], do_ref[...]
        s = _dot_nt(q, k) * ATT_SCALE
        m = jnp.max(s, axis=-1, keepdims=True)
        e = jnp.exp(s - m)
        p = e / jnp.sum(e, axis=-1, keepdims=True)
        dp = _dot_nt(do, v)
        ds = p * (dp - jnp.sum(dp * p, axis=-1, keepdims=True))
        dq_ref[...] = (_dot(ds, k) * ATT_SCALE).astype(dq_ref.dtype)
        dk = _dot_tn(ds, q) * ATT_SCALE
        dv = _dot_tn(p, do)

        @pl.when(pl.program_id(1) == 0)
        def _():
            dk_ref[...] = dk
            dv_ref[...] = dv

        @pl.when(pl.program_id(1) > 0)
        def _():
            dk_ref[...] += dk
            dv_ref[...] += dv

    qblk = pl.BlockSpec((CROSS_ROWS, HEAD_DIM), lambda h, i: (i, h))
    kblk = pl.BlockSpec((MEM_LEN, HEAD_DIM), lambda h, i: (0, h))
    dq, dk, dv = pl.pallas_call(
        kern, name=name, grid=(CROSS_HEADS, SEQ // CROSS_ROWS),
        in_specs=[qblk, kblk, pl.BlockSpec((MEM_LEN, HEAD_DIM), lambda h, i: (0, CROSS_HEADS + h)), qblk],
        out_specs=[qblk, kblk, kblk],
        out_shape=[jax.ShapeDtypeStruct((SEQ, CROSS_WIDTH), BF16),
                   jax.ShapeDtypeStruct((MEM_LEN, CROSS_WIDTH), F32),
                   jax.ShapeDtypeStruct((MEM_LEN, CROSS_WIDTH), F32)],
        compiler_params=_cparams(("parallel", "arbitrary")),
    )(qc, kvc, kvc, doc)
    return dq, jnp.concatenate([dk, dv], axis=1)


FULL_SPECS = {
    "w_in": ("col", D_MODEL, IN_WIDTH),
    "w_branch_a": ("col", ATT_OUT, D_MODEL),
    "w_branch_b": ("col", HG_WIDTH, D_MODEL),
    "w_out": ("row", D_MODEL, D_MODEL),
    "wq_cross": ("row", D_MODEL, CROSS_WIDTH),
    "wkv_cross": ("row", D_MODEL, 2 * CROSS_WIDTH),
    "wo_cross": ("col", CROSS_WIDTH, D_MODEL),
    "w13": ("col", D_MODEL, 2 * D_FF),
    "w2": ("row", D_FF, D_MODEL),
}
WEIGHT_PLACE = {
    "w_in": ("w_in", 0), "w_branch_a": ("w_branch_a", 0), "w_branch_b": ("w_branch_b", 0),
    "w_out": ("w_out", 0), "wq_cross": ("wq_cross", 0), "wkv_cross": ("wkv_cross", 0),
    "wo_cross": ("wo_cross", 0), "w1": ("w13", 0), "w3": ("w13", FF_SHARD), "w2": ("w2", 0),
}
BIG_WEIGHTS = tuple(WEIGHT_PLACE)
EW_BLOCK_ELEMS = 512 * 1024


def _position():
    return lax.axis_index("x"), lax.axis_index("y"), lax.axis_index("c")


def _other_chips(x, y):
    return [(1 - x, y), (x, 1 - y), (1 - x, 1 - y)]


def _half(ref, kind, h):
    r, c = ref.shape
    if kind == "col":
        return ref.at[pl.ds(h * (r // 2), r // 2), :]
    return ref.at[:, pl.ds(h * (c // 2), c // 2)]


def _shard_of(ref, kind, start, size):
    return ref.at[:, pl.ds(start, size)] if kind == "col" else ref.at[pl.ds(start, size), :]


def _rows_of(ref, r0, nrows):
    return ref if nrows is None else ref.at[pl.ds(r0, nrows), :]


def _half_shape(kind, rows, cols):
    return (rows // 2, cols) if kind == "col" else (rows, cols // 2)


def _slot_shape(spec):
    kind, rows, cols = spec
    hr, hc = _half_shape(kind, rows, cols)
    return (hr, hc // N_CHIPS) if kind == "col" else (hr // N_CHIPS, hc)


def _remote(src, dst, send_sem, recv_sem, device):
    return pltpu.make_async_remote_copy(src_ref=src, dst_ref=dst, send_sem=send_sem, recv_sem=recv_sem,
                                        device_id=device, device_id_type=MESH)


def _gather_ici_comm(fulls, jobs, specs):
    def piece(refs, job, chip, c):
        f, r0, nr = job
        kind, rows, cols = specs[f]
        stride = (cols if kind == "col" else rows) // N_CHIPS
        return _rows_of(_half(_shard_of(refs[f], kind, chip * stride, stride), kind, c), r0, nr)

    def start(refs, ss, rs):
        x, y, c = _position()
        j = 2 * x + y
        for q, job in enumerate(jobs):
            for p, (px, py) in enumerate(_other_chips(x, y)):
                _remote(piece(refs, job, j, c), piece(refs, job, j, c), ss.at[3 * q + p], rs.at[3 * q + p],
                        (px, py, c)).start()

    def finish(refs, ss, rs):
        x, y, c = _position()
        j = 2 * x + y
        for q, job in enumerate(jobs):
            for p, (px, py) in enumerate(_other_chips(x, y)):
                _remote(piece(refs, job, j, c), piece(refs, job, 2 * px + py, c), ss.at[3 * q + p],
                        rs.at[3 * q + p], (px, py, c)).wait_recv()
        for q, job in enumerate(jobs):
            for p, (px, py) in enumerate(_other_chips(x, y)):
                _remote(piece(refs, job, j, c), piece(refs, job, j, c), ss.at[3 * q + p], rs.at[3 * q + p],
                        (px, py, c)).wait_send()

    names = list(dict.fromkeys(job[0] for job in jobs))
    return _Carried({f: fulls[f] for f in names}, {}, 3 * len(jobs), start, finish)


def _gather_ring_comm(fulls, f, r0, nr, phase, specs):
    kind, _, cols = specs[f]
    assert kind == "col" and nr % 32 == 0
    stride = cols // N_CHIPS
    half = nr // 2

    def rows(refs, chip, c, lo, n):
        return _rows_of(_half(_shard_of(refs[f], kind, chip * stride, stride), kind, c), r0 + lo, n)

    def copies(refs, ss, rs):
        x, y, c = _position()
        me, nx, ny, dg = 2 * x + y, 2 * (1 - x) + y, 2 * x + (1 - y), 2 * (1 - x) + (1 - y)
        to_x, to_y = (1 - x, y, c), (x, 1 - y, c)
        if phase == "a":
            mine = rows(refs, me, c, 0, nr)
            return [(_remote(mine, mine, ss.at[0], rs.at[0], to_x), rows(refs, nx, c, 0, nr)),
                    (_remote(mine, mine, ss.at[1], rs.at[1], to_y), rows(refs, ny, c, 0, nr))]
        up, low = rows(refs, ny, c, half, half), rows(refs, nx, c, 0, half)
        return [(_remote(up, up, ss.at[0], rs.at[0], to_x), rows(refs, dg, c, half, half)),
                (_remote(low, low, ss.at[1], rs.at[1], to_y), rows(refs, dg, c, 0, half))]

    def start(refs, ss, rs):
        for cp, _ in copies(refs, ss, rs):
            cp.start()

    def finish(refs, ss, rs):
        x, y, c = _position()
        mine = copies(refs, ss, rs)
        for i, (_, landing) in enumerate(mine):
            _remote(landing, landing, ss.at[i], rs.at[i], (x, y, c)).wait_recv()
        for cp, _ in mine:
            cp.wait_send()

    return _Carried({f: fulls[f]}, {}, 2, start, finish)


def _gather_d2d_comm(fulls, jobs, specs):
    def rect(refs, job, h):
        f, r0, nr = job
        assert nr is None or specs[f][0] == "col"
        return _rows_of(_half(refs[f], specs[f][0], h), r0, nr)

    def start(refs, ss, rs):
        x, y, c = _position()
        for q, job in enumerate(jobs):
            _remote(rect(refs, job, c), rect(refs, job, c), ss.at[q], rs.at[q], (x, y, 1 - c)).start()

    def finish(refs, ss, rs):
        x, y, c = _position()
        for q, job in enumerate(jobs):
            _remote(rect(refs, job, 1 - c), rect(refs, job, 1 - c), ss.at[q], rs.at[q], (x, y, 1 - c)).wait_recv()
        for q, job in enumerate(jobs):
            _remote(rect(refs, job, c), rect(refs, job, c), ss.at[q], rs.at[q], (x, y, 1 - c)).wait_send()

    names = list(dict.fromkeys(job[0] for job in jobs))
    return _Carried({f: fulls[f] for f in names}, {}, len(jobs), start, finish)


def _pairx_comm(grads, names, specs, whole=False, recv=None):
    def copies(refs, ss, rs):
        x, y, c = _position()
        src = (lambda f: refs[("g", f)]) if whole else (lambda f: _half(refs[("g", f)], specs[f][0], 1 - c))
        return [_remote(src(f), refs[("r", f)], ss.at[i], rs.at[i], (x, y, 1 - c)) for i, f in enumerate(names)]

    def start(refs, ss, rs):
        for cp in copies(refs, ss, rs):
            cp.start()

    def finish(refs, ss, rs):
        for cp in copies(refs, ss, rs):
            cp.wait_recv()
        for cp in copies(refs, ss, rs):
            cp.wait_send()

    reads = {("g", f): grads[f] for f in names}
    if recv is not None:
        return _Carried({("r", f): recv[f] for f in names}, {}, len(names), start, finish, reads=reads)
    fresh = {("r", f): jax.ShapeDtypeStruct(_half_shape(*specs[f]), BF16) for f in names}
    return _Carried({}, fresh, len(names), start, finish, reads=reads)


def _chipx_comm(pair_sums, slots, jobs, specs):
    def copies(refs, ss, rs):
        x, y, c = _position()
        out = []
        for q, (f, r0, nr) in enumerate(jobs):
            kind = specs[f][0]
            width = _slot_shape(specs[f])[1 if kind == "col" else 0]
            for p, (px, py) in enumerate(_other_chips(x, y)):
                src = _rows_of(_shard_of(refs[("p", f)], kind, (2 * px + py) * width, width), r0, nr)
                dst = _rows_of(refs[("s", f)].at[p], r0, nr)
                out.append(_remote(src, dst, ss.at[3 * q + p], rs.at[3 * q + p], (px, py, c)))
        return out

    def start(refs, ss, rs):
        for cp in copies(refs, ss, rs):
            cp.start()

    def finish(refs, ss, rs):
        for cp in copies(refs, ss, rs):
            cp.wait_recv()
        for cp in copies(refs, ss, rs):
            cp.wait_send()

    names = list(dict.fromkeys(job[0] for job in jobs))
    arrays = {("p", f): pair_sums[f] for f in names}
    arrays.update({("s", f): slots[f] for f in names})
    return _Carried(arrays, {}, 3 * len(jobs), start, finish)


def _share_comm(grads, wnames, specs, place):
    def start(refs, ss, rs):
        x, y, c = _position()
        for i, w in enumerate(wnames):
            kind = specs[place[w][0]][0]
            _remote(_half(refs[w], kind, c), _half(refs[w], kind, c), ss.at[i], rs.at[i], (x, y, 1 - c)).start()

    def finish(refs, ss, rs):
        x, y, c = _position()
        for i, w in enumerate(wnames):
            kind = specs[place[w][0]][0]
            _remote(_half(refs[w], kind, 1 - c), _half(refs[w], kind, 1 - c), ss.at[i], rs.at[i],
                    (x, y, 1 - c)).wait_recv()
        for i, w in enumerate(wnames):
            kind = specs[place[w][0]][0]
            _remote(_half(refs[w], kind, c), _half(refs[w], kind, c), ss.at[i], rs.at[i], (x, y, 1 - c)).wait_send()

    return _Carried({w: grads[w] for w in wnames}, {}, len(wnames), start, finish)


SMALL_IN, SMALL_ROWS_OF_ALL = ("small", "mine"), ("small", "all")


def _small_gather_comm(v):
    flips = [(fx, fy, fc) for fx in (0, 1) for fy in (0, 1) for fc in (0, 1)][1:]

    def copies(refs, ss, rs):
        x, y, c = _position()
        v_ref, out_ref = refs[SMALL_IN], refs[SMALL_ROWS_OF_ALL]
        me = 4 * x + 2 * y + c
        pairs = []
        for i, fl in enumerate(flips):
            px, py, pc = (1 - a if f else a for a, f in zip((x, y, c), fl))
            pairs.append((_remote(v_ref, out_ref.at[me], ss.at[i], rs.at[i], (px, py, pc)),
                          _remote(v_ref, out_ref.at[4 * px + 2 * py + pc], ss.at[i], rs.at[i], (px, py, pc))))
        return pairs, pltpu.make_async_copy(v_ref, out_ref.at[me], ss.at[N_DEV - 1])

    def start(refs, ss, rs):
        pairs, local = copies(refs, ss, rs)
        local.start()
        for mine, _ in pairs:
            mine.start()

    def finish(refs, ss, rs):
        pairs, local = copies(refs, ss, rs)
        for _, theirs in pairs:
            theirs.wait_recv()
        for mine, _ in pairs:
            mine.wait_send()
        local.wait()

    fresh = {SMALL_ROWS_OF_ALL: jax.ShapeDtypeStruct((N_DEV,) + v.shape, F32)}
    return _Carried({}, fresh, N_DEV, start, finish, reads={SMALL_IN: v})


def _ew_block(rows, cols, elems=EW_BLOCK_ELEMS):
    tc = cols if cols <= 4096 else _div(cols, 2048, LANE)
    tr = _div(rows, max(16, elems // tc), 16)
    return tr, tc


def _mesh_scalars():
    x, y, c = _position()
    return jnp.stack([c, 2 * x + y]).astype(jnp.int32)


def _grid_spec(grid, in_specs, out_specs):
    return pltpu.PrefetchScalarGridSpec(num_scalar_prefetch=1, grid=grid, in_specs=in_specs, out_specs=out_specs)


def _cast_into_full(parts, fname, pos, specs, place, name, token=None):
    kind, rows, cols = specs[fname]
    ws = [w for w in place if place[w][0] == fname]
    if kind == "col":
        stride = cols // N_CHIPS
        hr = rows // 2
        tr = _div(hr, max(16, EW_BLOCK_ELEMS // stride), 16)
        nrb = hr // tr
        in_specs = [pl.BlockSpec((tr, parts[w].shape[1]), lambda i, pos_ref: (i + pos_ref[0] * nrb, 0)) for w in ws]
        out_spec = pl.BlockSpec((tr, stride), lambda i, pos_ref: (i + pos_ref[0] * nrb, pos_ref[1]))
    else:
        stride = rows // N_CHIPS
        hc = cols // 2
        tr = _div(stride, max(16, EW_BLOCK_ELEMS // hc), 16)
        nrb = stride // tr
        in_specs = [pl.BlockSpec((tr, hc), lambda i, pos_ref: (i, pos_ref[0])) for w in ws]
        out_spec = pl.BlockSpec((tr, hc), lambda i, pos_ref: (i + pos_ref[1] * nrb, pos_ref[0]))

    def kern(pos_ref, *refs):
        o_ref = refs[-1]
        for w, r in zip(ws, refs[:len(ws)]):
            off = place[w][1] if kind == "col" else 0
            o_ref[:, off:off + r.shape[1]] = r[...].astype(o_ref.dtype)

    tokens = [] if token is None else [token]
    in_specs = in_specs + [pl.BlockSpec(TOKEN_SHAPE, lambda i, pos_ref: (0, 0))] * len(tokens)
    return pl.pallas_call(
        kern, name=name, grid_spec=_grid_spec((nrb,), in_specs, out_spec),
        out_shape=jax.ShapeDtypeStruct((rows, cols), BF16),
        compiler_params=_cparams(("parallel",)),
    )(pos, *[parts[w] for w in ws], *tokens)


def _pair_sum(grad, recv, pos, spec, name):
    kind, rows, cols = spec
    hr, hc = _half_shape(kind, rows, cols)
    tr, tc = _ew_block(hr, hc, 2 * EW_BLOCK_ELEMS)
    nrb, ncb = hr // tr, hc // tc
    blk = pl.BlockSpec((tr, tc), lambda i, jj, pos_ref: (i, jj))
    if kind == "col":
        mine = pl.BlockSpec((tr, tc), lambda i, jj, pos_ref: (i + pos_ref[0] * nrb, jj))
    else:
        mine = pl.BlockSpec((tr, tc), lambda i, jj, pos_ref: (i, jj + pos_ref[0] * ncb))

    def kern(pos_ref, g_ref, r_ref, o_ref, slots_ref):
        o_ref[...] = (g_ref[...].astype(F32) + r_ref[...].astype(F32)).astype(o_ref.dtype)

    return pl.pallas_call(
        kern, name=name, grid_spec=_grid_spec((nrb, ncb), [mine, blk], [blk, ANY]),
        out_shape=[jax.ShapeDtypeStruct((hr, hc), BF16),
                   jax.ShapeDtypeStruct((N_CHIPS - 1,) + _slot_shape(spec), BF16)],
        compiler_params=_cparams(("parallel", "parallel")),
    )(pos, grad, recv)


def _chip_sum(pair_sum, slots, pos, fname, shard_shapes, specs, place, name):
    kind, rows, cols = specs[fname]
    sr, sc = _slot_shape(specs[fname])
    ws = [w for w in place if place[w][0] == fname]
    n_slots = N_CHIPS - 1
    tr = _div(sr, max(16, EW_BLOCK_ELEMS // sc), 16)
    nrb = sr // tr
    slot = pl.BlockSpec((n_slots, tr, sc), lambda i, pos_ref: (0, i, 0))
    if kind == "col":
        own = pl.BlockSpec((tr, sc), lambda i, pos_ref: (i, pos_ref[1]))
        out_specs = [pl.BlockSpec((tr, shard_shapes[w][1]), lambda i, pos_ref: (i + pos_ref[0] * nrb, 0)) for w in ws]
    else:
        own = pl.BlockSpec((tr, sc), lambda i, pos_ref: (i + pos_ref[1] * nrb, 0))
        out_specs = [pl.BlockSpec((tr, sc), lambda i, pos_ref: (i, pos_ref[0])) for w in ws]

    def kern(pos_ref, own_ref, slot_ref, *out_refs):
        tot = own_ref[...].astype(F32)
        for s in range(n_slots):
            tot = tot + slot_ref[s].astype(F32)
        for w, o_ref in zip(ws, out_refs):
            off = place[w][1] if kind == "col" else 0
            o_ref[...] = tot[:, off:off + o_ref.shape[1]]

    outs = pl.pallas_call(
        kern, name=name, grid_spec=_grid_spec((nrb,), [own, slot], out_specs),
        out_shape=[jax.ShapeDtypeStruct(shard_shapes[w], F32) for w in ws],
        compiler_params=_cparams(("parallel",)),
    )(pos, pair_sum, slots)
    return dict(zip(ws, outs))


def _adam_math(w, g, m, v):
    m2 = ADAM_B1 * m + (1.0 - ADAM_B1) * g
    v2 = ADAM_B2 * v + (1.0 - ADAM_B2) * (g * g)
    m_hat = m2 / (1.0 - ADAM_B1 ** ADAM_STEP)
    v_hat = v2 / (1.0 - ADAM_B2 ** ADAM_STEP)
    delta = -ADAM_LR * (m_hat / (jnp.sqrt(v_hat) + ADAM_EPS) + ADAM_WD * w)
    return delta, m2, v2


def _adamw(w, g, m, v, name, carried=()):
    rows, cols = w.shape
    tr, tc = _ew_block(rows, cols)

    def kern(w_ref, g_ref, m_ref, v_ref, d_ref, m2_ref, v2_ref, g_out_ref):
        g_ = g_ref[...]
        d_ref[...], m2_ref[...], v2_ref[...] = _adam_math(w_ref[...], g_, m_ref[...], v_ref[...])
        g_out_ref[...] = g_

    blk = pl.BlockSpec((tr, tc), lambda i, j: (i, j))
    return _pcall(
        kern, name=name, grid=(rows // tr, cols // tc), in_specs=[blk] * 4, out_specs=[blk] * 4,
        out_shape=[jax.ShapeDtypeStruct((rows, cols), F32)] * 4, args=(w, g, m, v),
        semantics=("parallel", "parallel"), carried=carried)


SMALL_ROWS = ("ln_mix_w", "ln_cross_w", "ln_mem_w", "ln_ffn_w", "ln_final_w")
ROW_HG_NORM, ROW_LB0, ROW_LB1 = 5, 6, 7
LOSS_LANE0 = HEAD_DIM


def _pack_small(vals):
    rows = [vals[n].reshape(1, D_MODEL) for n in SMALL_ROWS]
    pad = lambda a: jnp.pad(a, ((0, 0), (0, D_MODEL - a.shape[1])))
    rows.append(pad(vals["hg_norm_w"].reshape(1, HEAD_DIM)))
    rows.append(pad(vals["hg_lower_bounds"].reshape(2, HG_WIDTH)))
    return jnp.concatenate(rows, axis=0)


def _small_update(gathered, w, m, v, name="small_update"):
    def kern(g_ref, w_ref, m_ref, v_ref, grad_ref, d_ref, m2_ref, v2_ref, loss_ref):
        tot = g_ref[0]
        for s in range(1, N_DEV):
            tot = tot + g_ref[s]
        wv = w_ref[...]
        row = lax.broadcasted_iota(jnp.int32, (8, D_MODEL), 0)
        lane = lax.broadcasted_iota(jnp.int32, (8, D_MODEL), 1)
        l0, l1 = wv[ROW_LB0:ROW_LB0 + 1], wv[ROW_LB1:ROW_LB1 + 1]
        mx = jnp.maximum(l0, l1)
        e0, e1 = jnp.exp(l0 - mx), jnp.exp(l1 - mx)
        p0 = e0 / (e0 + e1)
        dlog = tot[ROW_LB0:ROW_LB0 + 1] * p0 * (1.0 - p0)
        tot = jnp.where(row == ROW_HG_NORM, tot + tot[ROW_LB1:ROW_LB1 + 1], tot)
        grad = jnp.where(row == ROW_LB0, dlog, jnp.where(row == ROW_LB1, -dlog, tot))
        grad = jnp.where((row == ROW_HG_NORM) & (lane >= HEAD_DIM), 0.0, grad)
        grad = jnp.where((row >= ROW_LB0) & (lane >= HG_WIDTH), 0.0, grad)
        grad_ref[...] = grad
        d_ref[...], m2_ref[...], v2_ref[...] = _adam_math(wv, grad, m_ref[...], v_ref[...])
        loss_ref[...] = tot[ROW_HG_NORM:ROW_HG_NORM + 1, LOSS_LANE0:LOSS_LANE0 + LANE]

    full = pl.BlockSpec((8, D_MODEL), lambda: (0, 0))
    return pl.pallas_call(
        kern, name=name,
        in_specs=[pl.BlockSpec((N_DEV, 8, D_MODEL), lambda: (0, 0, 0)), full, full, full],
        out_specs=[full, full, full, full, pl.BlockSpec((1, LANE), lambda: (0, 0))],
        out_shape=[jax.ShapeDtypeStruct((8, D_MODEL), F32)] * 4 + [jax.ShapeDtypeStruct((1, LANE), F32)],
        compiler_params=_cparams(),
    )(gathered, w, m, v)


def _unpack_small(p, shapes):
    out = {n: p[i].reshape(shapes[n]) for i, n in enumerate(SMALL_ROWS)}
    out["hg_norm_w"] = p[ROW_HG_NORM, :HEAD_DIM].reshape(shapes["hg_norm_w"])
    out["hg_lower_bounds"] = p[ROW_LB0:ROW_LB1 + 1, :HG_WIDTH].reshape(shapes["hg_lower_bounds"])
    return out


def _concat_cols(pieces, name):
    rows = pieces[0].shape[0]
    widths = [p.shape[1] for p in pieces]
    tr = ROW_BLOCK // 2

    def kern(*refs):
        o_ref, off = refs[-1], 0
        for r, w in zip(refs[:-1], widths):
            o_ref[:, off:off + w] = r[...]
            off += w

    return pl.pallas_call(
        kern, name=name, grid=(rows // tr,),
        in_specs=[pl.BlockSpec((tr, w), lambda i: (i, 0)) for w in widths],
        out_specs=pl.BlockSpec((tr, sum(widths)), lambda i: (i, 0)),
        out_shape=jax.ShapeDtypeStruct((rows, sum(widths)), pieces[0].dtype),
        compiler_params=_cparams(("parallel",)),
    )(*pieces)


WHOLE = lambda f: (f, 0, None)
MID_MATRICES = ("w_branch_a", "w_branch_b", "w_out", "wq_cross", "wkv_cross", "wo_cross")
MID_WEIGHTS = MID_MATRICES
W_IN_PIECES = [("w_in", r0, 512) for r0 in range(0, D_MODEL // 2, 512)]
W13_PIECES = [("w13", r0, 512) for r0 in range(0, D_MODEL // 2, 512)]
GATHER_GROUPS = [("mid", [WHOLE(f) for f in MID_MATRICES]), ("w13a", W13_PIECES[:1]), ("w13b", W13_PIECES[1:]),
                 ("w2", [WHOLE("w2")])]
OTHER_WEIGHTS = ["w1", "w3", "w2"] + list(MID_WEIGHTS)
BEFORE = {
    "hgrn_fwd": [("wait", "mid")],
    "mm_out": [("wait", "w13a")],
    "mm_o": [("wait", "w13b")],
    "mm_w2": [("wait", "w2"), ("run", ("d2d", [WHOLE("w2")]), "gather_hand_over_w2")],
    "mm_dwin_own": [("wait", "rs_w2"), ("chip_sum", "w2"), ("wait", "rs_w13"), ("chip_sum", "w13"),
                    ("wait", "pair_w_in")],
    "mm_dh": [("wait", "rs_mid")] + [("chip_sum", f) for f in MID_MATRICES],
}
CARRY = {
    "hgrn_fwd": [("d2d", [WHOLE(f) for f in MID_MATRICES])],
    "mm_out": [("d2d", W13_PIECES[:1])],
    "mm_o": [("d2d", W13_PIECES[1:])],
    "mm_du": [("pairx_whole", ["w2"])],
    "mm_dhf": [("pairx_whole", ["w13"])],
    "attn_bwd_g0": [("pairx", list(MID_MATRICES))],
    "mm_dh": [("share", OTHER_WEIGHTS)],
}
AFTER = {
    "mm_dw2_own": [("start", "rs_w2", [WHOLE("w2")])],
    "mm_dw13_own": [("start", "rs_w13", [WHOLE("w13")])],
    "attn_bwd_g0": [("pair_sum", f) for f in MID_MATRICES] + [("start", "rs_mid", [WHOLE(f) for f in MID_MATRICES])],
    "mm_dwin_sibling": [("start_pairx", "pair_w_in", ["w_in"])],
    "mm_dwin_own": [("start", "rs_w_in", [WHOLE("w_in")])],
}
FINISH = [
    ("adamw", OTHER_WEIGHTS), ("wait", "rs_w_in"), ("chip_sum", "w_in"), ("share_and_small", ["w_in"]),
    ("adamw", ["w_in"]),
]


class _Late:
    def __init__(self, read):
        self.read = read


class _Net:
    def __init__(self, full, pos=None, shard_shapes=None, comm=True, specs=FULL_SPECS, place=WEIGHT_PLACE):
        self.full, self.pos, self.shard_shapes, self.comm = dict(full), pos, shard_shapes, comm
        self.specs, self.place = specs, place
        self.gw, self.recv, self.psum, self.slots, self.grads = {}, {}, {}, {}, {}
        self.gw_sibling = {}
        self.pending, self.token, self.last = {}, None, None
        self.ahead = []

    def _make(self, kind, arg):
        if kind == "gather":
            return _gather_ici_comm(self.full, arg, self.specs)
        if kind == "ring":
            return _gather_ring_comm(self.full, *arg, self.specs)
        if kind == "d2d":
            return _gather_d2d_comm(self.full, arg, self.specs)
        if kind == "pairx":
            return _pairx_comm(self.gw, arg, self.specs)
        if kind == "pairx_whole":
            return _pairx_comm(self.gw_sibling, arg, self.specs, whole=True)
        if kind == "pairx_split":
            return _pairx_comm(self.gw_sibling, arg, self.specs, whole=True, recv=self.recv)
        if kind == "chipx":
            return _chipx_comm(self.psum, self.slots, arg, self.specs)
        assert kind == "share"
        return _share_comm(self.grads, arg, self.specs, self.place)

    def _store(self, kind, res):
        if kind in ("gather", "ring", "d2d"):
            self.full.update(res)
        elif kind in ("pairx", "pairx_whole", "pairx_split"):
            for (tag, f), a in res.items():
                (self.gw if tag == "g" else self.recv)[f] = a
        elif kind == "chipx":
            for (tag, f), a in res.items():
                (self.psum if tag == "p" else self.slots)[f] = a
        else:
            self.grads.update(res)

    def run_comm(self, item, name, extra=()):
        kind, arg = item
        res = _run_comm([self._make(kind, arg), *extra], name)
        self._store(kind, res[0])
        return res[1:]

    @staticmethod
    def _others(after, items):
        own = [a for cm in items for a in cm.arrays.values()]
        return [a for a in after if a is not None and all(a is not o for o in own)]

    def start(self, groups, kind, name):
        items = [self._make(kind, jobs) for _, jobs in groups]
        after = self._others([self.last], items)
        res, sems, token = _split_start(items, name, after=after[0] if after else None)
        for (group, jobs), r, s in zip(groups, res, sems):
            self._store(kind, r)
            self.pending[group] = (kind, jobs, s)
        self.token = self.last = token

    def wait(self, group, after=()):
        kind, jobs, sems = self.pending.pop(group)
        item = self._make(kind, jobs)
        res = _split_wait([item], [sems], self._others([self.last, *after], [item]), f"wait_{group}")[0]
        self._store(kind, res)

    def step(self, step):
        if step[0] == "wait":
            self.wait(step[1], after=self.ahead)
            self.ahead = []
        elif step[0] == "start":
            self.start([(step[1], step[2])], "chipx", f"start_{step[1]}")
        elif step[0] == "start_pairx":
            for f in step[2]:
                self.recv[f] = lax.empty(_half_shape(*self.specs[f]), BF16)
            self.start([(step[1], step[2])], "pairx_split", f"start_{step[1]}")
        elif step[0] == "pair_sum":
            f = step[1]
            self.psum[f], self.slots[f] = _pair_sum(self.gw[f], self.recv[f], self.pos, self.specs[f],
                                                    f"rs_pair_sum_{f}")
        elif step[0] == "chip_sum":
            f = step[1]
            sums = _chip_sum(self.psum[f], self.slots[f], self.pos, f, self.shard_shapes,
                             self.specs, self.place, f"rs_chip_sum_{f}")
            self.grads.update(sums)
            self.ahead += list(sums.values())
        else:
            assert step[0] == "run"
            self.run_comm(step[1], step[2])

    def call(self, fn, name, *args, grad_of=None, sibling_half=False, pair_sum_of=None, **kw):
        for step in (BEFORE.get(name, []) if self.comm else []):
            self.step(step)
        self.ahead = []
        late = lambda a: a.read() if isinstance(a, _Late) else a
        args = [late(a) for a in args]
        kw = {key: late(val) for key, val in kw.items()}
        items = CARRY.get(name, []) if self.comm else []
        carried = [self._make(k, a) for k, a in items]
        if self.token is not None:
            carried.append(_Token(self.token))
            self.token = None
        out, res = fn(*args, name=name, carried=carried, **kw)
        if grad_of is not None:
            (self.gw_sibling if sibling_half else self.gw)[grad_of] = out
        if pair_sum_of is not None:
            self.psum[pair_sum_of] = out
            self.slots[pair_sum_of] = lax.empty((N_CHIPS - 1,) + _slot_shape(self.specs[pair_sum_of]), BF16)
        self.last = jax.tree.leaves(out)[0]
        for (kind, _), r in zip(items, res):
            self._store(kind, r)
        for step in (AFTER.get(name, []) if self.comm else []):
            self.step(step)
        return out


def _local_step(net, x, h, mem, target, small, h_halves=None, core_row=None):
    full, call = net.full, net.call
    proj = call(_mm, "mm_proj", h, full["w_in"], mode="nn", out_dtype=F32)
    att = [call(_attn_fwd, f"attn_fwd_g{g}", proj, g) for g in range(3)]
    outs, lses = [a[0] for a in att], [a[1] for a in att]
    o_att = _attn_merge_fwd(outs, lses, "attn_merge")
    oraw, o_hg, states = call(_hg_fwd, "hgrn_fwd", proj, small["hg_lower_bounds"], small["hg_norm_w"])
    ya = call(_mm, "mm_branch_a", o_att, full["w_branch_a"], mode="nn", out_dtype=F32)
    yb, merged = call(_branch_b_gate, "mm_branch_b", o_hg, full["w_branch_b"], proj, ya)
    x1, hc = call(_residual_norm, "mm_out", merged, full["w_out"], x, small["ln_cross_w"])

    mn = _rms_fwd(mem, small["ln_mem_w"], "rms_mem")
    qc = call(_mm, "mm_q", hc, full["wq_cross"], mode="nn", out_dtype=F32)
    kvc = call(_mm, "mm_kv", mn, full["wkv_cross"], mode="nn", out_dtype=F32)
    oc = call(_cross_fwd, "cross_fwd", qc, kvc)
    x2, hf, *hf_halves = call(_residual_norm, "mm_o", oc, full["wo_cross"], x1, small["ln_ffn_w"], core_row=core_row)

    ab, u = call(_ff_up, "mm_w13", hf, full["w13"])
    x3 = call(_mm, "mm_w2", u, _Late(lambda: full["w2"]), mode="nn", out_dtype=F32, res=x2)

    dx3, dg_final, loss, dx3_bf16, *dx3_halves = _loss_head(x3, small["ln_final_w"], target, "loss_head", core_row)

    gs = {"ln_final_w": dg_final}
    if net.comm:
        call(_mm, "mm_dw2_sibling", u, dx3_halves[0], mode="tn", out_dtype=BF16, grad_of="w2", sibling_half=True)
        dab = call(_ff_down_bwd, "mm_du", dx3_bf16, full["w2"], ab)
        call(_mm, "mm_dw2_own", u, dx3_halves[1], mode="tn", out_dtype=BF16, res=_Late(lambda: net.recv["w2"]),
             pair_sum_of="w2")
        call(_mm, "mm_dw13_sibling", hf_halves[0], dab, mode="tn", out_dtype=BF16, grad_of="w13", sibling_half=True)
        dhf = call(_mm, "mm_dhf", dab, full["w13"], mode="nt", out_dtype=F32)
        call(_mm, "mm_dw13_own", hf_halves[1], dab, mode="tn", out_dtype=BF16, res=_Late(lambda: net.recv["w13"]),
             pair_sum_of="w13")
    else:
        call(_mm, "mm_dw2", u, dx3_bf16, mode="tn", out_dtype=BF16, grad_of="w2")
        dab = call(_ff_down_bwd, "mm_du", dx3_bf16, full["w2"], ab)
        call(_mm, "mm_dw13", hf, dab, mode="tn", out_dtype=BF16, grad_of="w13")
        dhf = call(_mm, "mm_dhf", dab, full["w13"], mode="nt", out_dtype=F32)
    dx2, gs["ln_ffn_w"], dx2_bf16 = _rms_bwd(x2, small["ln_ffn_w"], dhf, dx3, "rms_ffn_bwd", bf16_copy=True)
    doc = call(_mm, "mm_doc", dx2_bf16, full["wo_cross"], mode="nt", out_dtype=BF16)
    call(_mm, "mm_dwo", oc, dx2_bf16, mode="tn", out_dtype=BF16, grad_of="wo_cross")
    dqc, dkvc = _cross_bwd(qc, kvc, doc, "cross_bwd")
    call(_mm, "mm_dwq", hc, dqc, mode="tn", out_dtype=BF16, grad_of="wq_cross")
    dx1, gs["ln_cross_w"], dx1_bf16 = call(_norm_bwd_residual, "mm_dhc", dqc, full["wq_cross"], x1,
                                           small["ln_cross_w"], dx2)
    call(_mm, "mm_dwkv", mn, dkvc, mode="tn", out_dtype=BF16, grad_of="wkv_cross")
    dya, dyb, dga, dgb = call(_dmerged_gate_bwd, "mm_dmerged", dx1_bf16, full["w_out"], proj, ya, yb)
    call(_mm, "mm_dwout", merged, dx1_bf16, mode="tn", out_dtype=BF16, grad_of="w_out")
    call(_mm, "mm_dwa", o_att, dya, mode="tn", out_dtype=BF16, grad_of="w_branch_a")
    do_att = call(_mm, "mm_doatt", dya, full["w_branch_a"], mode="nt", out_dtype=F32)
    call(_mm, "mm_dwb", o_hg, dyb, mode="tn", out_dtype=BF16, grad_of="w_branch_b")
    do_hg = call(_mm, "mm_dohg", dyb, full["w_branch_b"], mode="nt", out_dtype=F32)
    dqh, dfh, dih, dgh, dlb, gs["hg_norm_w"] = call(
        _hg_bwd, "hgrn_bwd", proj, small["hg_lower_bounds"], small["hg_norm_w"], oraw, states, do_hg)
    gs["hg_lb"] = dlb
    do_gs, dl_gs = call(_attn_merge_bwd, "attn_merge_bwd", outs, lses, do_att)
    dqs, dks, dvs = zip(*[call(_attn_bwd, f"attn_bwd_g{g}", proj, g, lses[g], do_gs[g], dl_gs[g]) for g in range(3)])
    dproj = _concat_cols([*dqs, *dks, *dvs, dqh, dfh, dih, dgh, dga, dgb], "dproj_concat")
    if net.comm:
        h_sibling, h_own = h_halves
        call(_mm, "mm_dwin_sibling", h_sibling, dproj, mode="tn", out_dtype=BF16, grad_of="w_in", sibling_half=True)
    dmn = call(_mm, "mm_dmn", dkvc, full["wkv_cross"], mode="nt", out_dtype=F32)
    _, gs["ln_mem_w"] = _rms_bwd(mem, small["ln_mem_w"], dmn, None, "rms_mem_bwd")
    net.ahead.append(gs["ln_mem_w"])
    if net.comm:
        call(_mm, "mm_dwin_own", h_own, dproj, mode="tn", out_dtype=BF16, res=_Late(lambda: net.recv["w_in"]),
             pair_sum_of="w_in")
    else:
        call(_mm, "mm_dwin", h, dproj, mode="tn", out_dtype=BF16, grad_of="w_in")
    dh = call(_mm, "mm_dh", dproj, full["w_in"], mode="nt", out_dtype=F32)
    dx, gs["ln_mix_w"] = _rms_bwd(x, small["ln_mix_w"], dh, dx1, "rms_mix_bwd")
    return loss, dx, gs


WEIGHT_ORDER = ("ln_mix_w", "w_in", "hg_norm_w", "hg_lower_bounds", "w_branch_a", "w_branch_b", "w_out",
                "ln_cross_w", "ln_mem_w", "wq_cross", "wkv_cross", "wo_cross", "ln_ffn_w", "w1", "w3", "w2",
                "ln_final_w")


def kernel(x, mem, ln_mix_w, w_in, hg_norm_w, hg_lower_bounds, w_branch_a, w_branch_b, w_out, ln_cross_w, ln_mem_w, wq_cross, wkv_cross, wo_cross, ln_ffn_w, w1, w3, w2, ln_final_w, loss_target, m_ln_mix_w, m_w_in, m_hg_norm_w, m_hg_lower_bounds, m_w_branch_a, m_w_branch_b, m_w_out, m_ln_cross_w, m_ln_mem_w, m_wq_cross, m_wkv_cross, m_wo_cross, m_ln_ffn_w, m_w1, m_w3, m_w2, m_ln_final_w, v_ln_mix_w, v_w_in, v_hg_norm_w, v_hg_lower_bounds, v_w_branch_a, v_w_branch_b, v_w_out, v_ln_cross_w, v_ln_mem_w, v_wq_cross, v_wkv_cross, v_wo_cross, v_ln_ffn_w, v_w1, v_w3, v_w2, v_ln_final_w):
    args = dict(locals())
    w = {n: args[n] for n in WEIGHT_ORDER}
    m = {n: args["m_" + n] for n in WEIGHT_ORDER}
    v = {n: args["v_" + n] for n in WEIGHT_ORDER}
    shapes = {n: w[n].shape for n in WEIGHT_ORDER}
    mat = lambda a: a.reshape(a.shape[-2:])
    shard_shapes = {n: shapes[n][-2:] for n in BIG_WEIGHTS}

    pos = _mesh_scalars()

    def cast(f, token=None):
        return _cast_into_full({n: mat(w[n]) for n in BIG_WEIGHTS if WEIGHT_PLACE[n][0] == f}, f, pos,
                               FULL_SPECS, WEIGHT_PLACE, f"cast_{f}", token)

    net = _Net({"w_in": cast("w_in")}, pos, shard_shapes)
    net.start([(f"ring_a{i}", (*job, "a")) for i, job in enumerate(W_IN_PIECES)], "ring", "gather_start_w_in")
    rest = {f: cast(f, net.token) for f in FULL_SPECS if f != "w_in"}
    net.full.update(rest)
    small = {n: w[n].reshape(1, -1) for n in SMALL_ROWS}
    small["hg_norm_w"] = w["hg_norm_w"].reshape(1, HEAD_DIM)
    small["hg_lower_bounds"] = w["hg_lower_bounds"]
    x2d = x.reshape(SEQ, D_MODEL)
    h, *h_halves = _rms_fwd_halves(x2d, small["ln_mix_w"], pos, "rms_mix")
    for i, job in enumerate(W_IN_PIECES):
        net.wait(f"ring_a{i}", after=[*rest.values(), h] if i == 0 else ())
        net.start([(f"ring_b{i}", (*job, "b"))], "ring", f"gather_pass_on_w_in{i}")
    net.start(GATHER_GROUPS, "gather", "gather_start_rest")
    for i, job in enumerate(W_IN_PIECES):
        net.wait(f"ring_b{i}")
        net.run_comm(("d2d", [job]), f"gather_hand_over_w_in{i}")
    core_row = jnp.full((1, D_MODEL), lax.axis_index("c").astype(F32))
    loss, dx, gs = _local_step(net, x2d, h, mem.reshape(MEM_LEN, D_MODEL), loss_target.reshape(SEQ, D_MODEL), small,
                               h_halves, core_row)

    out_g, out_d, out_m, out_v = {}, {}, {}, {}
    net.last = dx
    for step in FINISH:
        if step[0] == "adamw":
            for n in step[1]:
                out_d[n], out_m[n], out_v[n], out_g[n] = net.call(_adamw, f"adamw_{n}", mat(w[n]), net.grads[n],
                                                                  mat(m[n]), mat(v[n]))
        elif step[0] == "wait":
            net.wait(step[1], after=list(out_d.values()))
        elif step[0] == "share_and_small":
            pad = lambda a: jnp.pad(a, ((0, 0), (0, D_MODEL - a.shape[1])))
            part = jnp.concatenate(
                [gs[n] for n in SMALL_ROWS]
                + [pad(jnp.concatenate([gs["hg_norm_w"][0], loss], axis=1)), pad(gs["hg_lb"]),
                   pad(gs["hg_norm_w"][1]) if len(gs["hg_norm_w"]) > 1 else jnp.zeros((1, D_MODEL), F32)], axis=0)
            rows = net.run_comm(("share", step[1]), "rs_sibling_share_and_gather_small",
                                extra=[_small_gather_comm(part)])[0][SMALL_ROWS_OF_ALL]
            sg, sd, sm, sv, loss_tot = _small_update(rows, _pack_small(w), _pack_small(m), _pack_small(v))
            for dst, packed in ((out_g, sg), (out_d, sd), (out_m, sm), (out_v, sv)):
                dst.update(_unpack_small(packed, shapes))
        else:
            net.step(step)

    result = [loss_tot[0, 0], dx.reshape(x.shape)]
    for group in (out_g, out_d, out_m, out_v):
        result += [group[n].reshape(shapes[n]) for n in WEIGHT_ORDER]
    return tuple(result)
```

```python
import math

import jax
import jax.numpy as jnp
from jax import lax
from jax.experimental import pallas as pl
from jax.experimental.pallas import tpu as pltpu

F32 = jnp.float32
BF16 = jnp.bfloat16
MESH = pl.DeviceIdType.MESH

D_MODEL = 2048
SEQ = 2048
HEAD_DIM = 128
MEM_LEN = 256
ATT_GROUPS = ((128, 1), (512, 4), (2048, 16))
ATT_HEADS = 4
ATT_WIDTH = 3 * ATT_HEADS * HEAD_DIM
ATT_OUT = ATT_HEADS * HEAD_DIM
ATT_BLOCK = 128
HG_HEADS = 8
HG_WIDTH = HG_HEADS * HEAD_DIM
HG_CHUNK = 64
IN_WIDTH = 3 * ATT_WIDTH + 4 * HG_WIDTH + 2 * D_MODEL
CROSS_HEADS = 4
CROSS_WIDTH = CROSS_HEADS * HEAD_DIM
D_FF = 5632
RMS_EPS = 1e-6
ADAM_LR = 0.001
ADAM_B1 = 0.9
ADAM_B2 = 0.999
ADAM_EPS = 1e-08
ADAM_WD = 0.01
ADAM_STEP = 10
N_CHIPS = 4
N_DEV = 8

VMEM_LIMIT_BYTES = 56 * 1024 * 1024
LANE = 128
MXU_WIDTH = 256
MM_TILE_CAP = 1536
TRANSPOSE_CHUNK = 512
ANY = pl.BlockSpec(memory_space=pl.ANY)


def _cparams(sem=None):
    return pltpu.CompilerParams(dimension_semantics=sem, vmem_limit_bytes=VMEM_LIMIT_BYTES)


def _div(n, cap, mult):
    best = None
    for d in range(mult, min(n, cap) + 1, mult):
        if n % d == 0:
            best = d
    assert best is not None, (n, cap, mult)
    return best


def _sigmoid(x):
    return 1.0 / (1.0 + jnp.exp(-x))


def _dot(a, b):
    return jnp.dot(a.astype(BF16), b.astype(BF16), preferred_element_type=F32)


def _dot_nt(a, b):
    return lax.dot_general(a.astype(BF16), b.astype(BF16), (((1,), (1,)), ((), ())),
                           preferred_element_type=F32)


def _dot_tn(a, b):
    return jnp.dot(a.astype(F32).T.astype(BF16), b.astype(BF16), preferred_element_type=F32)


def _dot_exact(a, b):
    return jnp.dot(a, b, precision=lax.Precision.HIGHEST, preferred_element_type=F32)


class _Carried:
    def __init__(self, arrays, fresh, n_sems, start, finish, mid=None, reads=None):
        self.arrays, self.fresh, self.n_sems, self.reads = arrays, fresh, n_sems, reads or {}
        self.start, self.mid, self.finish = start, mid, finish


class _Token:
    def __init__(self, array):
        self.array = array


TOKEN_SHAPE = (8, LANE)
ADAM_RING = 3


def _carried_layout(carried):
    akeys = list(dict.fromkeys(k for cm in carried for k in cm.arrays))
    fkeys = [(ci, k) for ci, cm in enumerate(carried) for k in cm.fresh]
    arrays = [next(cm.arrays[k] for cm in carried if k in cm.arrays) for k in akeys]
    shapes = [jax.ShapeDtypeStruct(a.shape, a.dtype) for a in arrays] + [carried[ci].fresh[k] for ci, k in fkeys]
    sems = []
    for cm in carried:
        sems += [pltpu.SemaphoreType.DMA((cm.n_sems,)), pltpu.SemaphoreType.DMA((cm.n_sems,))]
    return akeys, fkeys, arrays, shapes, sems


def _carried_reads(carried):
    rkeys = list(dict.fromkeys(k for cm in carried for k in cm.reads))
    return rkeys, [next(cm.reads[k] for cm in carried if k in cm.reads) for k in rkeys]


def _carried_results(carried, akeys, fkeys, outs, rkeys=(), read_refs=()):
    shared = dict(zip(akeys, outs[:len(akeys)]))
    shared.update(zip(rkeys, read_refs))
    res = [{k: shared[k] for k in list(cm.arrays) + [r for r in cm.reads if r in shared]} for cm in carried]
    for (ci, k), o in zip(fkeys, outs[len(akeys):]):
        res[ci][k] = o
    return res


def _pcall(kern, *, name, grid, in_specs, out_specs, out_shape, args, scratch_shapes=(), semantics=None,
           carried=()):
    tokens = [c.array for c in carried if isinstance(c, _Token)]
    carried = [c for c in carried if not isinstance(c, _Token)]
    single = not isinstance(out_shape, (list, tuple))
    out_specs = [out_specs] if single else list(out_specs)
    out_shape = [out_shape] if single else list(out_shape)
    n_real, n_out, n_scr = len(in_specs), len(out_shape), len(scratch_shapes)
    in_specs = list(in_specs) + [pl.BlockSpec(TOKEN_SHAPE, lambda *_: (0, 0))] * len(tokens)
    args = list(args) + tokens
    n_in = len(in_specs)
    if not carried:
        def plain(*refs):
            kern(*refs[:n_real], *refs[n_in:])

        outs = pl.pallas_call(plain if tokens else kern, name=name, grid=grid, in_specs=in_specs,
                              out_specs=out_specs, out_shape=out_shape, scratch_shapes=list(scratch_shapes),
                              compiler_params=_cparams(semantics))(*args)
        return (outs[0] if single else list(outs)), []
    akeys, fkeys, arrays, shapes, sems = _carried_layout(carried)
    rkeys, reads = _carried_reads(carried)
    n_a, n_f, n_r = len(akeys), len(fkeys), len(rkeys)
    total = math.prod(grid)
    mid_step = min(total - 1, (17 * total) // 20)

    def wrapped(*refs):
        ins = refs[:n_real]
        r0 = n_in + n_a
        o0 = r0 + n_r
        outs = refs[o0:o0 + n_out]
        a0 = o0 + n_out
        s0 = a0 + n_a + n_f
        per = _carried_results(carried, akeys, fkeys, refs[a0:s0], rkeys, refs[r0:o0])
        scratch = refs[s0:s0 + n_scr]
        sem = refs[s0 + n_scr:]
        step = 0
        for d, g in enumerate(grid):
            step = step * g + pl.program_id(d)

        @pl.when(step == 0)
        def _():
            for ci, cm in enumerate(carried):
                cm.start(per[ci], sem[2 * ci], sem[2 * ci + 1])

        kern(*ins, *outs, *scratch)

        @pl.when(step == mid_step)
        def _():
            for ci, cm in enumerate(carried):
                if cm.mid is not None:
                    cm.mid(per[ci], sem[2 * ci], sem[2 * ci + 1])

        @pl.when(step == total - 1)
        def _():
            for ci, cm in enumerate(carried):
                cm.finish(per[ci], sem[2 * ci], sem[2 * ci + 1])

    outs = pl.pallas_call(
        wrapped, name=name, grid=grid,
        in_specs=list(in_specs) + [ANY] * (n_a + n_r), out_specs=out_specs + [ANY] * (n_a + n_f),
        out_shape=out_shape + shapes,
        input_output_aliases={n_in + i: n_out + i for i in range(n_a)},
        scratch_shapes=list(scratch_shapes) + sems,
        compiler_params=_cparams(("arbitrary",) * len(grid)),
    )(*args, *arrays, *reads)
    res = _carried_results(carried, akeys, fkeys, outs[n_out:])
    return (outs[0] if single else list(outs[:n_out])), res


def _run_comm(carried, name):
    carried = list(carried)
    akeys, fkeys, arrays, shapes, sems = _carried_layout(carried)
    rkeys, reads = _carried_reads(carried)
    n_a, n_f, n_r = len(akeys), len(fkeys), len(rkeys)

    def body(*refs):
        o0 = n_a + n_r
        per = _carried_results(carried, akeys, fkeys, refs[o0:o0 + n_a + n_f], rkeys, refs[n_a:o0])
        sem = refs[o0 + n_a + n_f:]
        for hook in ("start", "mid", "finish"):
            for ci, cm in enumerate(carried):
                fn = getattr(cm, hook)
                if fn is not None:
                    fn(per[ci], sem[2 * ci], sem[2 * ci + 1])

    outs = pl.pallas_call(
        body, name=name, in_specs=[ANY] * (n_a + n_r), out_specs=[ANY] * (n_a + n_f), out_shape=shapes,
        input_output_aliases={i: i for i in range(n_a)}, scratch_shapes=sems,
    )(*arrays, *reads)
    return _carried_results(carried, akeys, fkeys, outs)


HBM_SPEC = pl.BlockSpec(memory_space=pltpu.HBM)
SEM_SPEC = pl.BlockSpec(memory_space=pltpu.SEMAPHORE)
SPLIT_EFFECT = pltpu.SideEffectType.DATAFLOW_SIDE_EFFECTING


def _in_hbm(a):
    return pltpu.with_memory_space_constraint(a, pltpu.HBM)


def _split_start(items, name, after=None):
    items = list(items)
    akeys, fkeys, arrays, shapes, sems = _carried_layout(items)
    assert not fkeys
    rkeys, reads = _carried_reads(items)
    n_a, n_s = len(akeys), len(sems)
    n_after = n_a + (after is not None)
    n_in = n_after + len(rkeys)

    def body(*refs):
        per = _carried_results(items, akeys, [], refs[n_in:n_in + n_a], rkeys, refs[n_after:n_in])
        sem = refs[n_in + n_a:n_in + n_a + n_s]
        for ci, cm in enumerate(items):
            cm.start(per[ci], sem[2 * ci], sem[2 * ci + 1])
        token = refs[n_in + n_a + n_s]
        token[...] = jnp.zeros_like(token)

    outs = pl.pallas_call(
        body, name=name, in_specs=[HBM_SPEC] * n_a + [ANY] * (n_in - n_a),
        out_specs=[HBM_SPEC] * n_a + [SEM_SPEC] * n_s + [pl.BlockSpec(memory_space=pltpu.VMEM)],
        out_shape=[pltpu.HBM(s.shape, s.dtype) for s in shapes] + sems + [jax.ShapeDtypeStruct(TOKEN_SHAPE, F32)],
        input_output_aliases={i: i for i in range(n_a)},
        compiler_params=pltpu.CompilerParams(has_side_effects=SPLIT_EFFECT),
    )(*[_in_hbm(a) for a in arrays], *([after] if after is not None else []), *reads)
    res = _carried_results(items, akeys, [], outs[:n_a])
    sem_out = outs[n_a:n_a + n_s]
    return res, [(sem_out[2 * ci], sem_out[2 * ci + 1]) for ci in range(len(items))], outs[-1]


def _split_wait(items, sems, after, name):
    items = list(items)
    after = list(after) if isinstance(after, (list, tuple)) else [after]
    akeys, fkeys, arrays, shapes, _ = _carried_layout(items)
    rkeys, reads = _carried_reads(items)
    n_a, n_s = len(akeys), 2 * len(items)
    n_after = n_a + n_s + len(after)
    n_in = n_after + len(rkeys)

    def body(*refs):
        per = _carried_results(items, akeys, [], refs[n_in:], rkeys, refs[n_after:n_in])
        sem = refs[n_a:n_a + n_s]
        for ci, cm in enumerate(items):
            cm.finish(per[ci], sem[2 * ci], sem[2 * ci + 1])

    outs = pl.pallas_call(
        body, name=name, in_specs=[HBM_SPEC] * n_a + [SEM_SPEC] * n_s + [ANY] * (n_in - n_a - n_s),
        out_specs=[HBM_SPEC] * n_a, out_shape=[pltpu.HBM(s.shape, s.dtype) for s in shapes],
        input_output_aliases={i: i for i in range(n_a)},
        compiler_params=pltpu.CompilerParams(has_side_effects=SPLIT_EFFECT),
    )(*arrays, *[s for pair in sems for s in pair], *after, *reads)
    return _carried_results(items, akeys, [], outs)


def _mm(a, b, *, mode, out_dtype, name, res=None, carried=(), fused=None):
    if mode == "nn":
        (m, k), (k2, n) = a.shape, b.shape
    elif mode == "nt":
        (m, k), (n, k2) = a.shape, b.shape
    else:
        (k, m), (k2, n) = a.shape, b.shape
    assert k == k2, (name, a.shape, b.shape)
    tm = _div(m, MM_TILE_CAP, LANE)
    tn = _div(n, MM_TILE_CAP, MXU_WIDTH) if n % MXU_WIDTH == 0 else 0
    if tn < 1024:
        tn = _div(n, MM_TILE_CAP, LANE)
    out_shape = jax.ShapeDtypeStruct((m, n), out_dtype)

    if fused is not None:
        assert mode in ("nn", "nt") and res is None and k <= 2048
        tm, tn = fused["tile"]
        dot = _dot if mode == "nn" else _dot_nt
        n_x = len(fused["ins"])

        summed = [len(e) > 2 for e in fused["outs"]]

        def kern_fused(*refs):
            part = dot(refs[0][...], refs[1][...])
            outs = fused["post"](part, *[r[...] for r in refs[2:2 + n_x]])
            first_row_tile = pl.program_id(1) == 0
            for o_ref, val, acc in zip(refs[2 + n_x:], outs, summed):
                if not acc:
                    o_ref[...] = val.astype(o_ref.dtype)
                    continue

                @pl.when(first_row_tile)
                def _():
                    o_ref[...] = val

                @pl.when(jnp.logical_not(first_row_tile))
                def _():
                    o_ref[...] += val

        def tile_spec(shape, index=lambda i, j: (i, j)):
            return pl.BlockSpec(shape, lambda j, i: index(i, j))

        return _pcall(
            kern_fused, name=name, grid=(n // tn, m // tm),
            in_specs=[pl.BlockSpec((tm, k), lambda j, i: (i, 0)),
                      pl.BlockSpec((k, tn), lambda j, i: (0, j)) if mode == "nn"
                      else pl.BlockSpec((tn, k), lambda j, i: (j, 0))] + [tile_spec(*e[1:]) for e in fused["ins"]],
            out_specs=[tile_spec(*e[1:]) for e in fused["outs"]], out_shape=[e[0] for e in fused["outs"]],
            args=(a, b, *[e[0] for e in fused["ins"]]), semantics=("parallel", "arbitrary"), carried=carried)

    if mode == "tn":
        has_res_tn = res is not None

        def kern_tn(*refs):
            a_ref, b_ref, o_ref, at_ref = refs[0], refs[1], refs[-2], refs[-1]

            @pl.when(pl.program_id(1) == 0)
            def _():
                step = min(TRANSPOSE_CHUNK, k)
                for c0 in range(0, k, step):
                    at_ref[:, c0:c0 + step] = a_ref[c0:c0 + step, :].T

            part = jnp.dot(at_ref[...], b_ref[...].astype(BF16), preferred_element_type=F32)
            if has_res_tn:
                part = part + refs[2][...].astype(F32)
            o_ref[...] = part.astype(o_ref.dtype)

        o_spec = pl.BlockSpec((tm, tn), lambda i, j: (i, j))
        return _pcall(
            kern_tn, name=name, grid=(m // tm, n // tn),
            in_specs=[pl.BlockSpec((k, tm), lambda i, j: (0, i)),
                      pl.BlockSpec((k, tn), lambda i, j: (0, j))] + [o_spec] * has_res_tn,
            out_specs=o_spec, out_shape=out_shape, args=(a, b) + ((res,) if has_res_tn else ()),
            scratch_shapes=[pltpu.VMEM((tm, k), BF16)],
            semantics=("parallel", "arbitrary"), carried=carried)

    tk = k if k <= 2048 else _div(k, 3072, LANE)
    nk = k // tk
    a_spec = pl.BlockSpec((tm, tk), lambda i, j, kk: (i, kk))
    if mode == "nn":
        b_spec = pl.BlockSpec((tk, tn), lambda i, j, kk: (kk, j))
        dot = _dot
    else:
        b_spec = pl.BlockSpec((tn, tk), lambda i, j, kk: (j, kk))
        dot = _dot_nt
    o_spec = pl.BlockSpec((tm, tn), lambda i, j, kk: (i, j))
    in_specs = [a_spec, b_spec]
    args = [a, b]
    if res is not None:
        in_specs.append(o_spec)
        args.append(res)
    has_res = res is not None

    def kern(*refs):
        a_ref, b_ref = refs[0], refs[1]
        r_ref = refs[2] if has_res else None
        o_ref = refs[3] if has_res else refs[2]
        if nk == 1:
            part = dot(a_ref[...], b_ref[...])
            if has_res:
                part = part + r_ref[...]
            o_ref[...] = part.astype(o_ref.dtype)
            return
        @pl.when(pl.program_id(2) == 0)
        def _():
            o_ref[...] = r_ref[...] if has_res else jnp.zeros_like(o_ref)

        o_ref[...] += dot(a_ref[...], b_ref[...])

    assert nk == 1 or out_dtype == F32, name
    return _pcall(
        kern, name=name, grid=(m // tm, n // tn, nk),
        in_specs=in_specs, out_specs=o_spec, out_shape=out_shape, args=args,
        semantics=("parallel", "parallel", "arbitrary"), carried=carried)


ROW_BLOCK = 512


def _rms_fwd(x, g, name):
    t, d = x.shape
    tr = min(ROW_BLOCK, t)

    def kern(x_ref, g_ref, o_ref):
        xf = x_ref[...]
        r = lax.rsqrt(jnp.mean(xf * xf, axis=-1, keepdims=True) + RMS_EPS)
        o_ref[...] = (xf * r * g_ref[...]).astype(o_ref.dtype)

    return pl.pallas_call(
        kern, name=name, grid=(t // tr,),
        in_specs=[pl.BlockSpec((tr, d), lambda i: (i, 0)), pl.BlockSpec((1, d), lambda i: (0, 0))],
        out_specs=pl.BlockSpec((tr, d), lambda i: (i, 0)),
        out_shape=jax.ShapeDtypeStruct((t, d), BF16),
        compiler_params=_cparams(("parallel",)),
    )(x, g)


def _rms_fwd_halves(x, g, pos, name):
    t, d = x.shape
    tr = min(ROW_BLOCK, t)
    half = d // 2

    def kern(pos_ref, x_ref, g_ref, o_ref, sib_ref, own_ref):
        xf = x_ref[...]
        r = lax.rsqrt(jnp.mean(xf * xf, axis=-1, keepdims=True) + RMS_EPS)
        h = (xf * r * g_ref[...]).astype(o_ref.dtype)
        o_ref[...] = h
        is_south = pos_ref[0] == 0
        sib_ref[...] = jnp.where(is_south, h[:, half:], h[:, :half])
        own_ref[...] = jnp.where(is_south, h[:, :half], h[:, half:])

    row = pl.BlockSpec((tr, d), lambda i, pos_ref: (i, 0))
    part = pl.BlockSpec((tr, half), lambda i, pos_ref: (i, 0))
    return pl.pallas_call(
        kern, name=name,
        grid_spec=_grid_spec((t // tr,), [row, pl.BlockSpec((1, d), lambda i, pos_ref: (0, 0))], [row, part, part]),
        out_shape=[jax.ShapeDtypeStruct((t, d), BF16)] + [jax.ShapeDtypeStruct((t, half), BF16)] * 2,
        compiler_params=_cparams(("parallel",)),
    )(pos, x, g)


def _rms_bwd(x, g, dh, res, name, bf16_copy=False):
    t, d = x.shape
    tr = min(ROW_BLOCK, t)
    has_res = res is not None

    def kern(*refs):
        x_ref, g_ref, dh_ref = refs[:3]
        r_ref = refs[3] if has_res else None
        dx_ref, dg_ref = refs[3 + has_res], refs[4 + has_res]
        xf = x_ref[...]
        r = lax.rsqrt(jnp.mean(xf * xf, axis=-1, keepdims=True) + RMS_EPS)
        xn = xf * r
        dh_ = dh_ref[...]
        dhg = dh_ * g_ref[...]
        dx = r * (dhg - xn * jnp.mean(dhg * xn, axis=-1, keepdims=True))
        if has_res:
            dx = dx + r_ref[...]
        dx_ref[...] = dx
        if bf16_copy:
            refs[-1][...] = dx.astype(BF16)
        part = jnp.sum(dh_ * xn, axis=0, keepdims=True)

        @pl.when(pl.program_id(0) == 0)
        def _():
            dg_ref[...] = part

        @pl.when(pl.program_id(0) > 0)
        def _():
            dg_ref[...] += part

    row = pl.BlockSpec((tr, d), lambda i: (i, 0))
    vec = pl.BlockSpec((1, d), lambda i: (0, 0))
    in_specs = [row, vec, row] + ([row] if has_res else [])
    args = [x, g, dh] + ([res] if has_res else [])
    return pl.pallas_call(
        kern, name=name, grid=(t // tr,), in_specs=in_specs, out_specs=[row, vec] + [row] * bf16_copy,
        out_shape=[jax.ShapeDtypeStruct((t, d), F32), jax.ShapeDtypeStruct((1, d), F32)]
        + [jax.ShapeDtypeStruct((t, d), BF16)] * bf16_copy,
        compiler_params=_cparams(("arbitrary",)),
    )(*args)


def _loss_head(x3, g, target, name, core_row=None):
    t, d = x3.shape
    tr = ROW_BLOCK
    with_halves = core_row is not None

    def kern(x_ref, g_ref, t_ref, *rest):
        dx_ref, dg_ref, loss_ref, dxb_ref = rest[with_halves:with_halves + 4]
        xf = x_ref[...]
        r = lax.rsqrt(jnp.mean(xf * xf, axis=-1, keepdims=True) + RMS_EPS)
        xn = xf * r
        gg = g_ref[...]
        err = xn * gg - t_ref[...]
        lpart = 0.5 * jnp.sum(jnp.mean(err * err, axis=-1, keepdims=True), axis=0, keepdims=True)
        dy = err * (1.0 / d)
        dyg = dy * gg
        dx = r * (dyg - xn * jnp.mean(dyg * xn, axis=-1, keepdims=True))
        dx_ref[...] = dx
        dxb = dx.astype(BF16)
        dxb_ref[...] = dxb
        if with_halves:
            rest[-2][...], rest[-1][...] = _column_halves(dxb, rest[0][...])
        gpart = jnp.sum(dy * xn, axis=0, keepdims=True)
        lrow = jnp.broadcast_to(lpart, (1, LANE))

        @pl.when(pl.program_id(0) == 0)
        def _():
            dg_ref[...] = gpart
            loss_ref[...] = lrow

        @pl.when(pl.program_id(0) > 0)
        def _():
            dg_ref[...] += gpart
            loss_ref[...] += lrow

    row = pl.BlockSpec((tr, d), lambda i: (i, 0))
    vec = pl.BlockSpec((1, d), lambda i: (0, 0))
    part = pl.BlockSpec((tr, d // 2), lambda i: (i, 0))
    return pl.pallas_call(
        kern, name=name, grid=(t // tr,), in_specs=[row, vec, row] + [vec] * with_halves,
        out_specs=[row, vec, pl.BlockSpec((1, LANE), lambda i: (0, 0)), row] + [part] * (2 * with_halves),
        out_shape=[jax.ShapeDtypeStruct((t, d), F32), jax.ShapeDtypeStruct((1, d), F32),
                   jax.ShapeDtypeStruct((1, LANE), F32), jax.ShapeDtypeStruct((t, d), BF16)]
        + [jax.ShapeDtypeStruct((t, d // 2), BF16)] * (2 * with_halves),
        compiler_params=_cparams(("arbitrary",)),
    )(x3, g, target, *([core_row] if with_halves else []))


ATT_SCALE = HEAD_DIM ** -0.5
Q_BLOCK0, K_BLOCK0, V_BLOCK0 = 0, ATT_WIDTH // HEAD_DIM, 2 * ATT_WIDTH // HEAD_DIM


def _residue_rows(dil, r, n):
    if dil == 1:
        return pl.ds(n * ATT_BLOCK, ATT_BLOCK)
    return pl.ds(n * ATT_BLOCK * dil + r, ATT_BLOCK, stride=dil)


def _band_mask(with_prev):
    width = 2 * ATT_BLOCK if with_prev else ATT_BLOCK
    iq = lax.broadcasted_iota(jnp.int32, (ATT_BLOCK, width), 0)
    ik = lax.broadcasted_iota(jnp.int32, (ATT_BLOCK, width), 1)
    if not with_prev:
        return ik <= iq
    return ((ik < ATT_BLOCK) & (iq <= ik)) | ((ik >= ATT_BLOCK) & ((ik - ATT_BLOCK) <= iq))


def _band_keys(ref, dil, r, n):
    own = ref[_residue_rows(dil, r, n), :]
    if n == 0:
        return own
    return jnp.concatenate([ref[_residue_rows(dil, r, n - 1), :], own], axis=0)


def _attn_col_spec(base, grp):
    return pl.BlockSpec((SEQ, HEAD_DIM), lambda h: (0, base + grp * ATT_HEADS + h))


def _attn_fwd(proj, grp, name, carried=()):
    _, dil = ATT_GROUPS[grp]
    nb = SEQ // dil // ATT_BLOCK

    def kern(q_ref, k_ref, v_ref, o_ref, lse_ref):
        for r in range(dil):
            for n in range(nb):
                rows = _residue_rows(dil, r, n)
                s = _dot_nt(q_ref[rows, :], _band_keys(k_ref, dil, r, n)) * ATT_SCALE
                s = jnp.where(_band_mask(n > 0), s, -jnp.inf)
                m = jnp.max(s, axis=-1, keepdims=True)
                p = jnp.exp(s - m)
                l = jnp.sum(p, axis=-1, keepdims=True)
                o_ref[rows, :] = _dot(p / l, _band_keys(v_ref, dil, r, n))
                lse_ref[rows, :] = jnp.broadcast_to(m + jnp.log(l), (ATT_BLOCK, HEAD_DIM))

    out_spec = pl.BlockSpec((SEQ, HEAD_DIM), lambda h: (0, h))
    return _pcall(
        kern, name=name, grid=(ATT_HEADS,),
        in_specs=[_attn_col_spec(Q_BLOCK0, grp), _attn_col_spec(K_BLOCK0, grp), _attn_col_spec(V_BLOCK0, grp)],
        out_specs=[out_spec, out_spec],
        out_shape=[jax.ShapeDtypeStruct((SEQ, ATT_OUT), F32)] * 2, args=(proj, proj, proj),
        semantics=("parallel",), carried=carried)


def _attn_weights(l0, l1, l2):
    mx = jnp.maximum(jnp.maximum(l0, l1), l2)
    e0, e1, e2 = jnp.exp(l0 - mx), jnp.exp(l1 - mx), jnp.exp(l2 - mx)
    den = e0 + e1 + e2
    return e0 / den, e1 / den, e2 / den


def _attn_merge_fwd(outs, lses, name):
    tr = ROW_BLOCK

    def kern(o0, o1, o2, l0, l1, l2, out_ref):
        a0, a1, a2 = _attn_weights(l0[...], l1[...], l2[...])
        out_ref[...] = (a0 * o0[...] + a1 * o1[...] + a2 * o2[...]).astype(out_ref.dtype)

    spec = pl.BlockSpec((tr, ATT_OUT), lambda i: (i, 0))
    return pl.pallas_call(
        kern, name=name, grid=(SEQ // tr,), in_specs=[spec] * 6, out_specs=spec,
        out_shape=jax.ShapeDtypeStruct((SEQ, ATT_OUT), BF16),
        compiler_params=_cparams(("parallel",)),
    )(*outs, *lses)


def _attn_merge_bwd(outs, lses, do_att, name, carried=()):
    tr = ROW_BLOCK

    def kern(o0, o1, o2, l0, l1, l2, do_ref, d0, d1, d2, t0, t1, t2):
        alphas = _attn_weights(l0[...], l1[...], l2[...])
        do = do_ref[...]
        o_att = alphas[0] * o0[...] + alphas[1] * o1[...] + alphas[2] * o2[...]
        prod = do * o_att
        parts = []
        for h in range(ATT_HEADS):
            sl = slice(h * HEAD_DIM, (h + 1) * HEAD_DIM)
            tot = jnp.sum(prod[:, sl], axis=-1, keepdims=True)
            parts.append(jnp.broadcast_to(tot, (tr, HEAD_DIM)))
        dd = jnp.concatenate(parts, axis=1)
        for a, d_ref, t_ref in zip(alphas, (d0, d1, d2), (t0, t1, t2)):
            d_ref[...] = a * do
            t_ref[...] = -a * dd

    spec = pl.BlockSpec((tr, ATT_OUT), lambda i: (i, 0))
    res, cres = _pcall(
        kern, name=name, grid=(SEQ // tr,), in_specs=[spec] * 7, out_specs=[spec] * 6,
        out_shape=[jax.ShapeDtypeStruct((SEQ, ATT_OUT), F32)] * 6, args=(*outs, *lses, do_att),
        semantics=("parallel",), carried=carried)
    return (res[:3], res[3:]), cres


def _attn_bwd(proj, grp, lse, do_g, dl_g, name, carried=()):
    _, dil = ATT_GROUPS[grp]
    nb = SEQ // dil // ATT_BLOCK

    def kern(q_ref, k_ref, v_ref, do_ref, lse_ref, dl_ref, dq_ref, dk_ref, dv_ref, dq_acc, dk_acc, dv_acc):
        dk_acc[...] = jnp.zeros_like(dk_acc)
        dv_acc[...] = jnp.zeros_like(dv_acc)
        for r in range(dil):
            for n in range(nb):
                rows = _residue_rows(dil, r, n)
                q, do = q_ref[rows, :], do_ref[rows, :]
                kk, vv = _band_keys(k_ref, dil, r, n), _band_keys(v_ref, dil, r, n)
                s = _dot_nt(q, kk) * ATT_SCALE
                p = jnp.where(_band_mask(n > 0), jnp.exp(s - lse_ref[rows, :][:, :1]), 0.0)
                ds = p * (_dot_nt(do, vv) + dl_ref[rows, :][:, :1])
                dq_acc[rows, :] = _dot(ds, kk) * ATT_SCALE
                dk = _dot_tn(ds, q) * ATT_SCALE
                dv = _dot_tn(p, do)
                if n > 0:
                    prev = _residue_rows(dil, r, n - 1)
                    dk_acc[prev, :] += dk[:ATT_BLOCK]
                    dv_acc[prev, :] += dv[:ATT_BLOCK]
                    dk, dv = dk[ATT_BLOCK:], dv[ATT_BLOCK:]
                dk_acc[rows, :] += dk
                dv_acc[rows, :] += dv
        dq_ref[...] = dq_acc[...].astype(dq_ref.dtype)
        dk_ref[...] = dk_acc[...].astype(dk_ref.dtype)
        dv_ref[...] = dv_acc[...].astype(dv_ref.dtype)

    spec = pl.BlockSpec((SEQ, HEAD_DIM), lambda h: (0, h))
    return _pcall(
        kern, name=name, grid=(ATT_HEADS,),
        in_specs=[_attn_col_spec(Q_BLOCK0, grp), _attn_col_spec(K_BLOCK0, grp), _attn_col_spec(V_BLOCK0, grp),
                  spec, spec, spec],
        out_specs=[spec] * 3,
        out_shape=[jax.ShapeDtypeStruct((SEQ, ATT_OUT), BF16)] * 3, args=(proj, proj, proj, do_g, lse, dl_g),
        scratch_shapes=[pltpu.VMEM((SEQ, HEAD_DIM), F32)] * 3,
        semantics=("parallel",), carried=carried)


HG_HEADS_PER_STEP = 8
HG_BLOCK_W = 4 * HEAD_DIM
HG_BLOCKS = HG_HEADS_PER_STEP * HEAD_DIM // HG_BLOCK_W
HG_STEP_W = HG_HEADS_PER_STEP * HEAD_DIM
HG_Q_BLK = (3 * ATT_WIDTH) // HG_BLOCK_W
HG_N_CHUNKS = SEQ // HG_CHUNK
HG_MID = HG_CHUNK // 2


def _lower_bound(lb_ref, sl):
    l0, l1 = lb_ref[0:1, sl], lb_ref[1:2, sl]
    mx = jnp.maximum(l0, l1)
    e0, e1 = jnp.exp(l0 - mx), jnp.exp(l1 - mx)
    return e0 / (e0 + e1)


def _tri(lower):
    i = lax.broadcasted_iota(jnp.int32, (HG_CHUNK, HG_CHUNK), 0)
    j = lax.broadcasted_iota(jnp.int32, (HG_CHUNK, HG_CHUNK), 1)
    return (i >= j) if lower else (i <= j)


def _head_mean(x):
    parts = []
    for hd in range(x.shape[1] // HEAD_DIM):
        m = jnp.mean(x[:, hd * HEAD_DIM:(hd + 1) * HEAD_DIM], axis=-1, keepdims=True)
        parts.append(jnp.broadcast_to(m, (x.shape[0], HEAD_DIM)))
    return jnp.concatenate(parts, axis=1)


def _hg_chunk_terms(qh, fh, lb):
    sig = _sigmoid(fh)
    f = lb + (1.0 - lb) * sig
    k = 1.0 - f
    b = _dot_exact(_tri(True).astype(F32), jnp.log(f))
    bl = b[HG_CHUNK - 1:HG_CHUNK, :]
    br = b[HG_MID:HG_MID + 1, :]
    sq = _sigmoid(qh)
    q = qh * sq
    return dict(sig=sig, f=f, k=k, b=b, bl=bl, br=br, sq=sq, q=q,
                e1=jnp.exp(bl - b), e2=jnp.exp(b), e3=jnp.exp(b - br), e4=jnp.exp(br - b))


def _hg_fwd(proj, lbw, normw, name, carried=()):
    def in_blks(off):
        return [pl.BlockSpec((HG_CHUNK, HG_BLOCK_W), lambda hp, n, b=b: (n, HG_Q_BLK + off + hp * HG_BLOCKS + b))
                for b in range(HG_BLOCKS)]

    def kern(*refs):
        q_refs, f_refs, i_refs, g_refs = (refs[k * HG_BLOCKS:(k + 1) * HG_BLOCKS] for k in range(4))
        lb_ref, nw_ref, oraw_ref, ohg_ref, st_ref, state = refs[4 * HG_BLOCKS:]

        @pl.when(pl.program_id(1) == 0)
        def _():
            state[...] = jnp.zeros_like(state)

        causal = _tri(True)
        wide = lambda rs: jnp.concatenate([r[...] for r in rs], axis=1)
        t = _hg_chunk_terms(wide(q_refs), wide(f_refs), _lower_bound(lb_ref, slice(None)))
        v, gh = wide(i_refs), wide(g_refs)
        kd, qb, qr, kr = t["k"] * t["e1"], t["q"] * t["e2"], t["q"] * t["e3"], t["k"] * t["e4"]
        decay = jnp.exp(t["bl"])
        outs = []
        for hd in range(HG_HEADS_PER_STEP):
            sl = slice(hd * HEAD_DIM, (hd + 1) * HEAD_DIM)
            st = state[hd]
            st_ref[0, hd] = st
            a = jnp.where(causal, _dot_nt(qr[:, sl], kr[:, sl]), 0.0)
            outs.append(_dot_nt(qb[:, sl], st) + _dot(a, v[:, sl]))
            state[hd] = st * decay[:, sl] + _dot_tn(v[:, sl], kd[:, sl])
        o = jnp.concatenate(outs, axis=1)
        oraw_ref[...] = o
        r = lax.rsqrt(_head_mean(o * o) + RMS_EPS)
        nw = jnp.tile(nw_ref[...], (1, HG_HEADS_PER_STEP))
        ohg_ref[...] = (o * r * nw * (gh * _sigmoid(gh))).astype(ohg_ref.dtype)

    out_blk = pl.BlockSpec((HG_CHUNK, HG_STEP_W), lambda hp, n: (n, hp))
    return _pcall(
        kern, name=name, grid=(HG_HEADS // HG_HEADS_PER_STEP, HG_N_CHUNKS),
        in_specs=[*in_blks(0), *in_blks(2), *in_blks(4), *in_blks(6),
                  pl.BlockSpec((2, HG_STEP_W), lambda hp, n: (0, hp)),
                  pl.BlockSpec((1, HEAD_DIM), lambda hp, n: (0, 0))],
        out_specs=[out_blk, out_blk,
                   pl.BlockSpec((1, HG_HEADS_PER_STEP, HEAD_DIM, HEAD_DIM), lambda hp, n: (n, hp, 0, 0))],
        out_shape=[jax.ShapeDtypeStruct((SEQ, HG_WIDTH), F32), jax.ShapeDtypeStruct((SEQ, HG_WIDTH), BF16),
                   jax.ShapeDtypeStruct((HG_N_CHUNKS, HG_HEADS, HEAD_DIM, HEAD_DIM), F32)],
        args=(*[proj] * (4 * HG_BLOCKS), lbw, normw),
        scratch_shapes=[pltpu.VMEM((HG_HEADS_PER_STEP, HEAD_DIM, HEAD_DIM), F32)],
        semantics=("parallel", "arbitrary"), carried=carried)


def _hg_bwd(proj, lbw, normw, oraw, states, do_hg, name, carried=()):
    last = HG_N_CHUNKS - 1

    def in_blks(off):
        return [pl.BlockSpec((HG_CHUNK, HG_BLOCK_W),
                             lambda hp, n, b=b: (last - n, HG_Q_BLK + off + hp * HG_BLOCKS + b))
                for b in range(HG_BLOCKS)]

    blk = pl.BlockSpec((HG_CHUNK, HG_STEP_W), lambda hp, n: (last - n, hp))

    def kern(*refs):
        q_refs, f_refs, i_refs, g_refs = (refs[k * HG_BLOCKS:(k + 1) * HG_BLOCKS] for k in range(4))
        (lb_ref, nw_ref, oraw_ref, st_ref, do_ref, dq_ref, df_ref, di_ref, dg_ref, dlb_ref, dnw_ref,
         dstate) = refs[4 * HG_BLOCKS:]
        first = pl.program_id(1) == 0

        @pl.when(first)
        def _():
            dstate[...] = jnp.zeros_like(dstate)

        causal = _tri(True)
        wide = lambda rs: jnp.concatenate([r[...] for r in rs], axis=1)
        cat = lambda parts: jnp.concatenate(parts, axis=1)
        qh, fh, v, gh = wide(q_refs), wide(f_refs), wide(i_refs), wide(g_refs)
        o, dout = oraw_ref[...], do_ref[...]
        nw = jnp.tile(nw_ref[...], (1, HG_HEADS_PER_STEP))
        sgg = _sigmoid(gh)
        r = lax.rsqrt(_head_mean(o * o) + RMS_EPS)
        xn = o * r
        dg_ref[...] = (dout * xn * nw * (sgg * (1.0 + gh * (1.0 - sgg)))).astype(dg_ref.dtype)
        don = dout * (gh * sgg)
        dnw_wide = jnp.sum(don * xn, axis=0, keepdims=True)
        dnw_tot = dnw_wide[:, :HEAD_DIM]
        for hd in range(1, HG_HEADS_PER_STEP):
            dnw_tot = dnw_tot + dnw_wide[:, hd * HEAD_DIM:(hd + 1) * HEAD_DIM]
        tt = don * nw
        do = r * (tt - xn * _head_mean(tt * xn))
        lb = _lower_bound(lb_ref, slice(None))
        t = _hg_chunk_terms(qh, fh, lb)
        k, q = t["k"], t["q"]
        kd, qb, qr, kr = k * t["e1"], q * t["e2"], q * t["e3"], k * t["e4"]
        decay = jnp.exp(t["bl"])
        dqb, dqr, dkr, dkd, dv, ddecay = [], [], [], [], [], []
        for hd in range(HG_HEADS_PER_STEP):
            sl = slice(hd * HEAD_DIM, (hd + 1) * HEAD_DIM)
            st = st_ref[0, hd]
            dstn = dstate[hd]
            a = jnp.where(causal, _dot_nt(qr[:, sl], kr[:, sl]), 0.0)
            da = jnp.where(causal, _dot_nt(do[:, sl], v[:, sl]), 0.0)
            dqb.append(_dot(do[:, sl], st))
            dv.append(_dot_tn(a, do[:, sl]) + _dot_nt(kd[:, sl], dstn))
            dqr.append(_dot(da, kr[:, sl]))
            dkr.append(_dot_tn(da, qr[:, sl]))
            dkd.append(_dot(v[:, sl], dstn))
            ddecay.append(jnp.sum(dstn * st, axis=0, keepdims=True))
            dstate[hd] = dstn * decay[:, sl] + _dot_tn(do[:, sl], qb[:, sl])
        dqb, dqr, dkr, dkd, dv, ddecay = cat(dqb), cat(dqr), cat(dkr), cat(dkd), cat(dv), cat(ddecay)
        dq = dqb * t["e2"] + dqr * t["e3"]
        dk = dkd * t["e1"] + dkr * t["e4"]
        db = dqb * qb + dqr * qr - dkr * kr - dkd * kd
        dbl = jnp.sum(dkd * kd, axis=0, keepdims=True) + ddecay * decay
        dbr = jnp.sum(dkr * kr - dqr * qr, axis=0, keepdims=True)
        rows = lax.broadcasted_iota(jnp.int32, db.shape, 0)
        dlf = _dot_exact(_tri(False).astype(F32), db) + dbl + jnp.where(rows <= HG_MID, dbr, 0.0)
        df = dlf / t["f"] - dk
        sig, sq = t["sig"], t["sq"]
        df_ref[...] = (df * (1.0 - lb) * sig * (1.0 - sig)).astype(df_ref.dtype)
        dlb_row = jnp.sum(df * (1.0 - sig), axis=0, keepdims=True)
        dq_ref[...] = (dq * (sq * (1.0 + qh * (1.0 - sq)))).astype(dq_ref.dtype)
        di_ref[...] = dv.astype(di_ref.dtype)
        dnw_blk = jnp.broadcast_to(dnw_tot, (8, HEAD_DIM))

        @pl.when(first)
        def _():
            dlb_ref[...] = dlb_row
            dnw_ref[...] = dnw_blk

        @pl.when(jnp.logical_not(first))
        def _():
            dlb_ref[...] += dlb_row
            dnw_ref[...] += dnw_blk

    n_hp = HG_HEADS // HG_HEADS_PER_STEP
    outs, cres = _pcall(
        kern, name=name, grid=(n_hp, HG_N_CHUNKS),
        in_specs=[*in_blks(0), *in_blks(2), *in_blks(4), *in_blks(6),
                  pl.BlockSpec((2, HG_STEP_W), lambda hp, n: (0, hp)),
                  pl.BlockSpec((1, HEAD_DIM), lambda hp, n: (0, 0)),
                  blk,
                  pl.BlockSpec((1, HG_HEADS_PER_STEP, HEAD_DIM, HEAD_DIM), lambda hp, n: (last - n, hp, 0, 0)),
                  blk],
        out_specs=[blk, blk, blk, blk,
                   pl.BlockSpec((1, HG_STEP_W), lambda hp, n: (0, hp)),
                   pl.BlockSpec((8, HEAD_DIM), lambda hp, n: (hp, 0))],
        out_shape=[jax.ShapeDtypeStruct((SEQ, HG_WIDTH), BF16)] * 4
        + [jax.ShapeDtypeStruct((1, HG_WIDTH), F32), jax.ShapeDtypeStruct((8 * n_hp, HEAD_DIM), F32)],
        args=(*[proj] * (4 * HG_BLOCKS), lbw, normw, oraw, states, do_hg),
        scratch_shapes=[pltpu.VMEM((HG_HEADS_PER_STEP, HEAD_DIM, HEAD_DIM), F32)],
        semantics=("parallel", "arbitrary"), carried=carried)
    dqh, dfh, dih, dgh, dlb, dnw = outs
    return (dqh, dfh, dih, dgh, dlb, [dnw[8 * i:8 * i + 1] for i in range(n_hp)]), cres


GATE_BLOCK_W = 512
GATE_A_BLK = (3 * ATT_WIDTH + 4 * HG_WIDTH) // GATE_BLOCK_W
GATE_B_BLK = GATE_A_BLK + D_MODEL // GATE_BLOCK_W


GATE_TILE = (1024, GATE_BLOCK_W)


def _gate_ins(proj, ya, yb=None):
    ins = [(proj, GATE_TILE, lambda i, j: (i, GATE_A_BLK + j)), (proj, GATE_TILE, lambda i, j: (i, GATE_B_BLK + j)),
           (ya, GATE_TILE)]
    return ins + ([(yb, GATE_TILE)] if yb is not None else [])


def _branch_b_gate(o_hg, w_b, proj, ya, name, carried=()):
    def post(yb, ga, gb, ya_):
        return yb, _sigmoid(ga) * ya_ + _sigmoid(gb) * yb

    return _mm(o_hg, w_b, mode="nn", out_dtype=F32, name=name, carried=carried, fused=dict(
        tile=GATE_TILE, ins=_gate_ins(proj, ya), post=post,
        outs=[(jax.ShapeDtypeStruct((SEQ, D_MODEL), F32), GATE_TILE), (jax.ShapeDtypeStruct((SEQ, D_MODEL), BF16), GATE_TILE)]))


def _dmerged_gate_bwd(dx1, w_out, proj, ya, yb, name, carried=()):
    def post(dm, ga, gb, ya_, yb_):
        sa, sb = _sigmoid(ga), _sigmoid(gb)
        return dm * sa, dm * sb, dm * ya_ * sa * (1.0 - sa), dm * yb_ * sb * (1.0 - sb)

    return _mm(dx1, w_out, mode="nt", out_dtype=BF16, name=name, carried=carried, fused=dict(
        tile=GATE_TILE, ins=_gate_ins(proj, ya, yb), post=post,
        outs=[(jax.ShapeDtypeStruct((SEQ, D_MODEL), BF16), GATE_TILE)] * 4))


NORM_TILE = (512, D_MODEL)


def _column_halves(v, core_row):
    half = v.shape[1] // 2
    south = core_row[:, :half] < 0.5
    lo, hi = v[:, :half], v[:, half:]
    return jnp.where(south, hi, lo), jnp.where(south, lo, hi)


def _residual_norm(a, w, x, g, name, carried=(), core_row=None):
    vec = ((1, D_MODEL), lambda i, j: (0, j))
    half_tile = (NORM_TILE[0], D_MODEL // 2)

    def post(part, x_, g_, *core):
        xn = part + x_
        r = lax.rsqrt(jnp.mean(xn * xn, axis=-1, keepdims=True) + RMS_EPS)
        hn = (xn * r * g_).astype(BF16)
        return (xn, hn) + (_column_halves(hn, core[0]) if core else ())

    wide = [(jax.ShapeDtypeStruct((SEQ, D_MODEL), F32), NORM_TILE), (jax.ShapeDtypeStruct((SEQ, D_MODEL), BF16), NORM_TILE)]
    halves = [(jax.ShapeDtypeStruct((SEQ, D_MODEL // 2), BF16), half_tile)] * 2
    return _mm(a, w, mode="nn", out_dtype=F32, name=name, carried=carried, fused=dict(
        tile=NORM_TILE, post=post, ins=[(x, NORM_TILE), (g, *vec)] + ([(core_row, *vec)] if core_row is not None else []),
        outs=wide + (halves if core_row is not None else [])))


def _norm_bwd_residual(dq, w, x, g, res, name, carried=()):
    def post(dh, x_, g_, res_):
        r = lax.rsqrt(jnp.mean(x_ * x_, axis=-1, keepdims=True) + RMS_EPS)
        xn = x_ * r
        dhg = dh * g_
        dx = r * (dhg - xn * jnp.mean(dhg * xn, axis=-1, keepdims=True)) + res_
        return dx, jnp.sum(dh * xn, axis=0, keepdims=True), dx

    vec = ((1, D_MODEL), lambda i, j: (0, j))
    return _mm(dq, w, mode="nt", out_dtype=F32, name=name, carried=carried, fused=dict(
        tile=NORM_TILE, ins=[(x, NORM_TILE), (g, *vec), (res, NORM_TILE)], post=post,
        outs=[(jax.ShapeDtypeStruct((SEQ, D_MODEL), F32), NORM_TILE), (jax.ShapeDtypeStruct((1, D_MODEL), F32), *vec),
              (jax.ShapeDtypeStruct((SEQ, D_MODEL), BF16), NORM_TILE)]))


FF_SHARD = D_FF // N_CHIPS


FF_TILE_ROWS = 512


def _swiglu_tile(ab):
    a, b = ab[:, :FF_SHARD], ab[:, FF_SHARD:]
    return a * _sigmoid(a) * b


def _swiglu_grad_tile(du, ab):
    a, b = ab[:, :FF_SHARD], ab[:, FF_SHARD:]
    sg = _sigmoid(a)
    return jnp.concatenate([du * b * (sg * (1.0 + a * (1.0 - sg))), du * (a * sg)], axis=1)


def _ff_up(hf, w13, name, carried=()):
    wide, narrow = (FF_TILE_ROWS, 2 * FF_SHARD), (FF_TILE_ROWS, FF_SHARD)
    return _mm(hf, w13, mode="nn", out_dtype=F32, name=name, carried=carried, fused=dict(
        tile=wide, ins=[],
        outs=[(jax.ShapeDtypeStruct((SEQ, 2 * D_FF), F32), wide), (jax.ShapeDtypeStruct((SEQ, D_FF), BF16), narrow)],
        post=lambda p: (p, _swiglu_tile(p))))


def _ff_down_bwd(dx3, w2, ab, name, carried=()):
    wide, narrow = (FF_TILE_ROWS, 2 * FF_SHARD), (FF_TILE_ROWS, FF_SHARD)
    out, res = _mm(dx3, w2, mode="nt", out_dtype=BF16, name=name, carried=carried, fused=dict(
        tile=narrow, ins=[(ab, wide)], outs=[(jax.ShapeDtypeStruct((SEQ, 2 * D_FF), BF16), wide)],
        post=lambda du, ab_: (_swiglu_grad_tile(du, ab_),)))
    return out[0], res


CROSS_ROWS = 512


def _cross_fwd(qc, kvc, name, carried=()):
    def kern(q_ref, k_ref, v_ref, o_ref):
        s = _dot_nt(q_ref[...], k_ref[...]) * ATT_SCALE
        m = jnp.max(s, axis=-1, keepdims=True)
        e = jnp.exp(s - m)
        p = e / jnp.sum(e, axis=-1, keepdims=True)
        o_ref[...] = _dot(p, v_ref[...]).astype(o_ref.dtype)

    qblk = pl.BlockSpec((CROSS_ROWS, HEAD_DIM), lambda h, i: (i, h))
    return _pcall(
        kern, name=name, grid=(CROSS_HEADS, SEQ // CROSS_ROWS),
        in_specs=[qblk, pl.BlockSpec((MEM_LEN, HEAD_DIM), lambda h, i: (0, h)),
                  pl.BlockSpec((MEM_LEN, HEAD_DIM), lambda h, i: (0, CROSS_HEADS + h))],
        out_specs=qblk, out_shape=jax.ShapeDtypeStruct((SEQ, CROSS_WIDTH), BF16), args=(qc, kvc, kvc),
        semantics=("parallel", "parallel"), carried=carried)


def _cross_bwd(qc, kvc, doc, name):
    def kern(q_ref, k_ref, v_ref, do_ref, dq_ref, dk_ref, dv_ref):
        q, k, v, do = q_ref[...], k_ref[...], v_ref[...], do_ref[...]
        s = _dot_nt(q, k) * ATT_SCALE
        m = jnp.max(s, axis=-1, keepdims=True)
        e = jnp.exp(s - m)
        p = e / jnp.sum(e, axis=-1, keepdims=True)
        dp = _dot_nt(do, v)
        ds = p * (dp - jnp.sum(dp * p, axis=-1, keepdims=True))
        dq_ref[...] = (_dot(ds, k) * ATT_SCALE).astype(dq_ref.dtype)
        dk = _dot_tn(ds, q) * ATT_SCALE
        dv = _dot_tn(p, do)

        @pl.when(pl.program_id(1) == 0)
        def _():
            dk_ref[...] = dk
            dv_ref[...] = dv

        @pl.when(pl.program_id(1) > 0)
        def _():
            dk_ref[...] += dk
            dv_ref[...] += dv

    qblk = pl.BlockSpec((CROSS_ROWS, HEAD_DIM), lambda h, i: (i, h))
    kblk = pl.BlockSpec((MEM_LEN, HEAD_DIM), lambda h, i: (0, h))
    dq, dk, dv = pl.pallas_call(
        kern, name=name, grid=(CROSS_HEADS, SEQ // CROSS_ROWS),
        in_specs=[qblk, kblk, pl.BlockSpec((MEM_LEN, HEAD_DIM), lambda h, i: (0, CROSS_HEADS + h)), qblk],
        out_specs=[qblk, kblk, kblk],
        out_shape=[jax.ShapeDtypeStruct((SEQ, CROSS_WIDTH), BF16),
                   jax.ShapeDtypeStruct((MEM_LEN, CROSS_WIDTH), F32),
                   jax.ShapeDtypeStruct((MEM_LEN, CROSS_WIDTH), F32)],
        compiler_params=_cparams(("parallel", "arbitrary")),
    )(qc, kvc, kvc, doc)
    return dq, jnp.concatenate([dk, dv], axis=1)


FULL_SPECS = {
    "w_in": ("col", D_MODEL, IN_WIDTH),
    "w_branch_a": ("col", ATT_OUT, D_MODEL),
    "w_branch_b": ("col", HG_WIDTH, D_MODEL),
    "w_out": ("row", D_MODEL, D_MODEL),
    "wq_cross": ("row", D_MODEL, CROSS_WIDTH),
    "wkv_cross": ("row", D_MODEL, 2 * CROSS_WIDTH),
    "wo_cross": ("col", CROSS_WIDTH, D_MODEL),
    "w13": ("col", D_MODEL, 2 * D_FF),
    "w2": ("row", D_FF, D_MODEL),
}
WEIGHT_PLACE = {
    "w_in": ("w_in", 0), "w_branch_a": ("w_branch_a", 0), "w_branch_b": ("w_branch_b", 0),
    "w_out": ("w_out", 0), "wq_cross": ("wq_cross", 0), "wkv_cross": ("wkv_cross", 0),
    "wo_cross": ("wo_cross", 0), "w1": ("w13", 0), "w3": ("w13", FF_SHARD), "w2": ("w2", 0),
}
BIG_WEIGHTS = tuple(WEIGHT_PLACE)
EW_BLOCK_ELEMS = 512 * 1024


def _position():
    return lax.axis_index("x"), lax.axis_index("y"), lax.axis_index("c")


def _other_chips(x, y):
    return [(1 - x, y), (x, 1 - y), (1 - x, 1 - y)]


def _half(ref, kind, h):
    r, c = ref.shape
    if kind == "col":
        return ref.at[pl.ds(h * (r // 2), r // 2), :]
    return ref.at[:, pl.ds(h * (c // 2), c // 2)]


def _shard_of(ref, kind, start, size):
    return ref.at[:, pl.ds(start, size)] if kind == "col" else ref.at[pl.ds(start, size), :]


def _rows_of(ref, r0, nrows):
    return ref if nrows is None else ref.at[pl.ds(r0, nrows), :]


def _half_shape(kind, rows, cols):
    return (rows // 2, cols) if kind == "col" else (rows, cols // 2)


def _slot_shape(spec):
    kind, rows, cols = spec
    hr, hc = _half_shape(kind, rows, cols)
    return (hr, hc // N_CHIPS) if kind == "col" else (hr // N_CHIPS, hc)


def _remote(src, dst, send_sem, recv_sem, device):
    return pltpu.make_async_remote_copy(src_ref=src, dst_ref=dst, send_sem=send_sem, recv_sem=recv_sem,
                                        device_id=device, device_id_type=MESH)


def _gather_ici_comm(fulls, jobs, specs):
    def piece(refs, job, chip, c):
        f, r0, nr = job
        kind, rows, cols = specs[f]
        stride = (cols if kind == "col" else rows) // N_CHIPS
        return _rows_of(_half(_shard_of(refs[f], kind, chip * stride, stride), kind, c), r0, nr)

    def start(refs, ss, rs):
        x, y, c = _position()
        j = 2 * x + y
        for q, job in enumerate(jobs):
            for p, (px, py) in enumerate(_other_chips(x, y)):
                _remote(piece(refs, job, j, c), piece(refs, job, j, c), ss.at[3 * q + p], rs.at[3 * q + p],
                        (px, py, c)).start()

    def finish(refs, ss, rs):
        x, y, c = _position()
        j = 2 * x + y
        for q, job in enumerate(jobs):
            for p, (px, py) in enumerate(_other_chips(x, y)):
                _remote(piece(refs, job, j, c), piece(refs, job, 2 * px + py, c), ss.at[3 * q + p],
                        rs.at[3 * q + p], (px, py, c)).wait_recv()
        for q, job in enumerate(jobs):
            for p, (px, py) in enumerate(_other_chips(x, y)):
                _remote(piece(refs, job, j, c), piece(refs, job, j, c), ss.at[3 * q + p], rs.at[3 * q + p],
                        (px, py, c)).wait_send()

    names = list(dict.fromkeys(job[0] for job in jobs))
    return _Carried({f: fulls[f] for f in names}, {}, 3 * len(jobs), start, finish)


def _gather_ring_comm(fulls, f, r0, nr, phase, specs):
    kind, _, cols = specs[f]
    assert kind == "col" and nr % 32 == 0
    stride = cols // N_CHIPS
    half = nr // 2

    def rows(refs, chip, c, lo, n):
        return _rows_of(_half(_shard_of(refs[f], kind, chip * stride, stride), kind, c), r0 + lo, n)

    def copies(refs, ss, rs):
        x, y, c = _position()
        me, nx, ny, dg = 2 * x + y, 2 * (1 - x) + y, 2 * x + (1 - y), 2 * (1 - x) + (1 - y)
        to_x, to_y = (1 - x, y, c), (x, 1 - y, c)
        if phase == "a":
            mine = rows(refs, me, c, 0, nr)
            return [(_remote(mine, mine, ss.at[0], rs.at[0], to_x), rows(refs, nx, c, 0, nr)),
                    (_remote(mine, mine, ss.at[1], rs.at[1], to_y), rows(refs, ny, c, 0, nr))]
        up, low = rows(refs, ny, c, half, half), rows(refs, nx, c, 0, half)
        return [(_remote(up, up, ss.at[0], rs.at[0], to_x), rows(refs, dg, c, half, half)),
                (_remote(low, low, ss.at[1], rs.at[1], to_y), rows(refs, dg, c, 0, half))]

    def start(refs, ss, rs):
        for cp, _ in copies(refs, ss, rs):
            cp.start()

    def finish(refs, ss, rs):
        x, y, c = _position()
        mine = copies(refs, ss, rs)
        for i, (_, landing) in enumerate(mine):
            _remote(landing, landing, ss.at[i], rs.at[i], (x, y, c)).wait_recv()
        for cp, _ in mine:
            cp.wait_send()

    return _Carried({f: fulls[f]}, {}, 2, start, finish)


def _gather_d2d_comm(fulls, jobs, specs):
    def rect(refs, job, h):
        f, r0, nr = job
        assert nr is None or specs[f][0] == "col"
        return _rows_of(_half(refs[f], specs[f][0], h), r0, nr)

    def start(refs, ss, rs):
        x, y, c = _position()
        for q, job in enumerate(jobs):
            _remote(rect(refs, job, c), rect(refs, job, c), ss.at[q], rs.at[q], (x, y, 1 - c)).start()

    def finish(refs, ss, rs):
        x, y, c = _position()
        for q, job in enumerate(jobs):
            _remote(rect(refs, job, 1 - c), rect(refs, job, 1 - c), ss.at[q], rs.at[q], (x, y, 1 - c)).wait_recv()
        for q, job in enumerate(jobs):
            _remote(rect(refs, job, c), rect(refs, job, c), ss.at[q], rs.at[q], (x, y, 1 - c)).wait_send()

    names = list(dict.fromkeys(job[0] for job in jobs))
    return _Carried({f: fulls[f] for f in names}, {}, len(jobs), start, finish)


def _pairx_comm(grads, names, specs, whole=False, recv=None):
    def copies(refs, ss, rs):
        x, y, c = _position()
        src = (lambda f: refs[("g", f)]) if whole else (lambda f: _half(refs[("g", f)], specs[f][0], 1 - c))
        return [_remote(src(f), refs[("r", f)], ss.at[i], rs.at[i], (x, y, 1 - c)) for i, f in enumerate(names)]

    def start(refs, ss, rs):
        for cp in copies(refs, ss, rs):
            cp.start()

    def finish(refs, ss, rs):
        for cp in copies(refs, ss, rs):
            cp.wait_recv()
        for cp in copies(refs, ss, rs):
            cp.wait_send()

    reads = {("g", f): grads[f] for f in names}
    if recv is not None:
        return _Carried({("r", f): recv[f] for f in names}, {}, len(names), start, finish, reads=reads)
    fresh = {("r", f): jax.ShapeDtypeStruct(_half_shape(*specs[f]), BF16) for f in names}
    return _Carried({}, fresh, len(names), start, finish, reads=reads)


def _chipx_comm(pair_sums, slots, jobs, specs):
    def copies(refs, ss, rs):
        x, y, c = _position()
        out = []
        for q, (f, r0, nr) in enumerate(jobs):
            kind = specs[f][0]
            width = _slot_shape(specs[f])[1 if kind == "col" else 0]
            for p, (px, py) in enumerate(_other_chips(x, y)):
                src = _rows_of(_shard_of(refs[("p", f)], kind, (2 * px + py) * width, width), r0, nr)
                dst = _rows_of(refs[("s", f)].at[p], r0, nr)
                out.append(_remote(src, dst, ss.at[3 * q + p], rs.at[3 * q + p], (px, py, c)))
        return out

    def start(refs, ss, rs):
        for cp in copies(refs, ss, rs):
            cp.start()

    def finish(refs, ss, rs):
        for cp in copies(refs, ss, rs):
            cp.wait_recv()
        for cp in copies(refs, ss, rs):
            cp.wait_send()

    names = list(dict.fromkeys(job[0] for job in jobs))
    arrays = {("p", f): pair_sums[f] for f in names}
    arrays.update({("s", f): slots[f] for f in names})
    return _Carried(arrays, {}, 3 * len(jobs), start, finish)


def _share_comm(grads, wnames, specs, place):
    def start(refs, ss, rs):
        x, y, c = _position()
        for i, w in enumerate(wnames):
            kind = specs[place[w][0]][0]
            _remote(_half(refs[w], kind, c), _half(refs[w], kind, c), ss.at[i], rs.at[i], (x, y, 1 - c)).start()

    def finish(refs, ss, rs):
        x, y, c = _position()
        for i, w in enumerate(wnames):
            kind = specs[place[w][0]][0]
            _remote(_half(refs[w], kind, 1 - c), _half(refs[w], kind, 1 - c), ss.at[i], rs.at[i],
                    (x, y, 1 - c)).wait_recv()
        for i, w in enumerate(wnames):
            kind = specs[place[w][0]][0]
            _remote(_half(refs[w], kind, c), _half(refs[w], kind, c), ss.at[i], rs.at[i], (x, y, 1 - c)).wait_send()

    return _Carried({w: grads[w] for w in wnames}, {}, len(wnames), start, finish)


SMALL_IN, SMALL_ROWS_OF_ALL = ("small", "mine"), ("small", "all")


def _small_gather_comm(v):
    flips = [(fx, fy, fc) for fx in (0, 1) for fy in (0, 1) for fc in (0, 1)][1:]

    def copies(refs, ss, rs):
        x, y, c = _position()
        v_ref, out_ref = refs[SMALL_IN], refs[SMALL_ROWS_OF_ALL]
        me = 4 * x + 2 * y + c
        pairs = []
        for i, fl in enumerate(flips):
            px, py, pc = (1 - a if f else a for a, f in zip((x, y, c), fl))
            pairs.append((_remote(v_ref, out_ref.at[me], ss.at[i], rs.at[i], (px, py, pc)),
                          _remote(v_ref, out_ref.at[4 * px + 2 * py + pc], ss.at[i], rs.at[i], (px, py, pc))))
        return pairs, pltpu.make_async_copy(v_ref, out_ref.at[me], ss.at[N_DEV - 1])

    def start(refs, ss, rs):
        pairs, local = copies(refs, ss, rs)
        local.start()
        for mine, _ in pairs:
            mine.start()

    def finish(refs, ss, rs):
        pairs, local = copies(refs, ss, rs)
        for _, theirs in pairs:
            theirs.wait_recv()
        for mine, _ in pairs:
            mine.wait_send()
        local.wait()

    fresh = {SMALL_ROWS_OF_ALL: jax.ShapeDtypeStruct((N_DEV,) + v.shape, F32)}
    return _Carried({}, fresh, N_DEV, start, finish, reads={SMALL_IN: v})


def _ew_block(rows, cols, elems=EW_BLOCK_ELEMS):
    tc = cols if cols <= 4096 else _div(cols, 2048, LANE)
    tr = _div(rows, max(16, elems // tc), 16)
    return tr, tc


def _mesh_scalars():
    x, y, c = _position()
    return jnp.stack([c, 2 * x + y]).astype(jnp.int32)


def _grid_spec(grid, in_specs, out_specs):
    return pltpu.PrefetchScalarGridSpec(num_scalar_prefetch=1, grid=grid, in_specs=in_specs, out_specs=out_specs)


def _cast_into_full(parts, fname, pos, specs, place, name, token=None):
    kind, rows, cols = specs[fname]
    ws = [w for w in place if place[w][0] == fname]
    if kind == "col":
        stride = cols // N_CHIPS
        hr = rows // 2
        tr = _div(hr, max(16, EW_BLOCK_ELEMS // stride), 16)
        nrb = hr // tr
        in_specs = [pl.BlockSpec((tr, parts[w].shape[1]), lambda i, pos_ref: (i + pos_ref[0] * nrb, 0)) for w in ws]
        out_spec = pl.BlockSpec((tr, stride), lambda i, pos_ref: (i + pos_ref[0] * nrb, pos_ref[1]))
    else:
        stride = rows // N_CHIPS
        hc = cols // 2
        tr = _div(stride, max(16, EW_BLOCK_ELEMS // hc), 16)
        nrb = stride // tr
        in_specs = [pl.BlockSpec((tr, hc), lambda i, pos_ref: (i, pos_ref[0])) for w in ws]
        out_spec = pl.BlockSpec((tr, hc), lambda i, pos_ref: (i + pos_ref[1] * nrb, pos_ref[0]))

    def kern(pos_ref, *refs):
        o_ref = refs[-1]
        for w, r in zip(ws, refs[:len(ws)]):
            off = place[w][1] if kind == "col" else 0
            o_ref[:, off:off + r.shape[1]] = r[...].astype(o_ref.dtype)

    tokens = [] if token is None else [token]
    in_specs = in_specs + [pl.BlockSpec(TOKEN_SHAPE, lambda i, pos_ref: (0, 0))] * len(tokens)
    return pl.pallas_call(
        kern, name=name, grid_spec=_grid_spec((nrb,), in_specs, out_spec),
        out_shape=jax.ShapeDtypeStruct((rows, cols), BF16),
        compiler_params=_cparams(("parallel",)),
    )(pos, *[parts[w] for w in ws], *tokens)


def _pair_sum(grad, recv, pos, spec, name):
    kind, rows, cols = spec
    hr, hc = _half_shape(kind, rows, cols)
    tr, tc = _ew_block(hr, hc, 2 * EW_BLOCK_ELEMS)
    nrb, ncb = hr // tr, hc // tc
    blk = pl.BlockSpec((tr, tc), lambda i, jj, pos_ref: (i, jj))
    if kind == "col":
        mine = pl.BlockSpec((tr, tc), lambda i, jj, pos_ref: (i + pos_ref[0] * nrb, jj))
    else:
        mine = pl.BlockSpec((tr, tc), lambda i, jj, pos_ref: (i, jj + pos_ref[0] * ncb))

    def kern(pos_ref, g_ref, r_ref, o_ref, slots_ref):
        o_ref[...] = (g_ref[...].astype(F32) + r_ref[...].astype(F32)).astype(o_ref.dtype)

    return pl.pallas_call(
        kern, name=name, grid_spec=_grid_spec((nrb, ncb), [mine, blk], [blk, ANY]),
        out_shape=[jax.ShapeDtypeStruct((hr, hc), BF16),
                   jax.ShapeDtypeStruct((N_CHIPS - 1,) + _slot_shape(spec), BF16)],
        compiler_params=_cparams(("parallel", "parallel")),
    )(pos, grad, recv)


def _chip_sum(pair_sum, slots, pos, fname, shard_shapes, specs, place, name):
    kind, rows, cols = specs[fname]
    sr, sc = _slot_shape(specs[fname])
    ws = [w for w in place if place[w][0] == fname]
    n_slots = N_CHIPS - 1
    tr = _div(sr, max(16, EW_BLOCK_ELEMS // sc), 16)
    nrb = sr // tr
    slot = pl.BlockSpec((n_slots, tr, sc), lambda i, pos_ref: (0, i, 0))
    if kind == "col":
        own = pl.BlockSpec((tr, sc), lambda i, pos_ref: (i, pos_ref[1]))
        out_specs = [pl.BlockSpec((tr, shard_shapes[w][1]), lambda i, pos_ref: (i + pos_ref[0] * nrb, 0)) for w in ws]
    else:
        own = pl.BlockSpec((tr, sc), lambda i, pos_ref: (i + pos_ref[1] * nrb, 0))
        out_specs = [pl.BlockSpec((tr, sc), lambda i, pos_ref: (i, pos_ref[0])) for w in ws]

    def kern(pos_ref, own_ref, slot_ref, *out_refs):
        tot = own_ref[...].astype(F32)
        for s in range(n_slots):
            tot = tot + slot_ref[s].astype(F32)
        for w, o_ref in zip(ws, out_refs):
            off = place[w][1] if kind == "col" else 0
            o_ref[...] = tot[:, off:off + o_ref.shape[1]]

    outs = pl.pallas_call(
        kern, name=name, grid_spec=_grid_spec((nrb,), [own, slot], out_specs),
        out_shape=[jax.ShapeDtypeStruct(shard_shapes[w], F32) for w in ws],
        compiler_params=_cparams(("parallel",)),
    )(pos, pair_sum, slots)
    return dict(zip(ws, outs))


def _adam_math(w, g, m, v):
    m2 = ADAM_B1 * m + (1.0 - ADAM_B1) * g
    v2 = ADAM_B2 * v + (1.0 - ADAM_B2) * (g * g)
    m_hat = m2 / (1.0 - ADAM_B1 ** ADAM_STEP)
    v_hat = v2 / (1.0 - ADAM_B2 ** ADAM_STEP)
    delta = -ADAM_LR * (m_hat / (jnp.sqrt(v_hat) + ADAM_EPS) + ADAM_WD * w)
    return delta, m2, v2


def _adamw(w, g, m, v, name, carried=()):
    rows, cols = w.shape
    tr, tc = _ew_block(rows, cols)
    assert tc == cols, (name, cols)
    steps = rows // tr

    def kern(w_hbm, g_hbm, m_hbm, v_hbm, d_ref, m2_ref, v2_ref, g_out_ref, buf, sem):
        s = pl.program_id(0)

        def copies(t):
            slot = t % ADAM_RING
            return [pltpu.make_async_copy(src.at[pl.ds(t * tr, tr), :], buf.at[k, slot], sem.at[k, slot])
                    for k, src in enumerate((w_hbm, g_hbm, m_hbm, v_hbm))]

        @pl.when(s == 0)
        def _():
            for t in range(min(ADAM_RING - 1, steps)):
                for cp in copies(t):
                    cp.start()

        @pl.when(s + (ADAM_RING - 1) < steps)
        def _():
            for cp in copies(s + (ADAM_RING - 1)):
                cp.start()

        for cp in copies(s):
            cp.wait()
        slot = s % ADAM_RING
        g_ = buf[1, slot]
        d_ref[...], m2_ref[...], v2_ref[...] = _adam_math(buf[0, slot], g_, buf[2, slot], buf[3, slot])
        g_out_ref[...] = g_

    blk = pl.BlockSpec((tr, cols), lambda i: (i, 0))
    return _pcall(
        kern, name=name, grid=(steps,), in_specs=[ANY] * 4, out_specs=[blk] * 4,
        out_shape=[jax.ShapeDtypeStruct((rows, cols), F32)] * 4, args=(w, g, m, v),
        scratch_shapes=[pltpu.VMEM((4, ADAM_RING, tr, cols), F32), pltpu.SemaphoreType.DMA((4, ADAM_RING))],
        semantics=("arbitrary",), carried=carried)


SMALL_ROWS = ("ln_mix_w", "ln_cross_w", "ln_mem_w", "ln_ffn_w", "ln_final_w")
ROW_HG_NORM, ROW_LB0, ROW_LB1 = 5, 6, 7
LOSS_LANE0 = HEAD_DIM


def _pack_small(vals):
    rows = [vals[n].reshape(1, D_MODEL) for n in SMALL_ROWS]
    pad = lambda a: jnp.pad(a, ((0, 0), (0, D_MODEL - a.shape[1])))
    rows.append(pad(vals["hg_norm_w"].reshape(1, HEAD_DIM)))
    rows.append(pad(vals["hg_lower_bounds"].reshape(2, HG_WIDTH)))
    return jnp.concatenate(rows, axis=0)


def _small_update(gathered, w, m, v, name="small_update"):
    def kern(g_ref, w_ref, m_ref, v_ref, grad_ref, d_ref, m2_ref, v2_ref, loss_ref):
        tot = g_ref[0]
        for s in range(1, N_DEV):
            tot = tot + g_ref[s]
        wv = w_ref[...]
        row = lax.broadcasted_iota(jnp.int32, (8, D_MODEL), 0)
        lane = lax.broadcasted_iota(jnp.int32, (8, D_MODEL), 1)
        l0, l1 = wv[ROW_LB0:ROW_LB0 + 1], wv[ROW_LB1:ROW_LB1 + 1]
        mx = jnp.maximum(l0, l1)
        e0, e1 = jnp.exp(l0 - mx), jnp.exp(l1 - mx)
        p0 = e0 / (e0 + e1)
        dlog = tot[ROW_LB0:ROW_LB0 + 1] * p0 * (1.0 - p0)
        tot = jnp.where(row == ROW_HG_NORM, tot + tot[ROW_LB1:ROW_LB1 + 1], tot)
        grad = jnp.where(row == ROW_LB0, dlog, jnp.where(row == ROW_LB1, -dlog, tot))
        grad = jnp.where((row == ROW_HG_NORM) & (lane >= HEAD_DIM), 0.0, grad)
        grad = jnp.where((row >= ROW_LB0) & (lane >= HG_WIDTH), 0.0, grad)
        grad_ref[...] = grad
        d_ref[...], m2_ref[...], v2_ref[...] = _adam_math(wv, grad, m_ref[...], v_ref[...])
        loss_ref[...] = tot[ROW_HG_NORM:ROW_HG_NORM + 1, LOSS_LANE0:LOSS_LANE0 + LANE]

    full = pl.BlockSpec((8, D_MODEL), lambda: (0, 0))
    return pl.pallas_call(
        kern, name=name,
        in_specs=[pl.BlockSpec((N_DEV, 8, D_MODEL), lambda: (0, 0, 0)), full, full, full],
        out_specs=[full, full, full, full, pl.BlockSpec((1, LANE), lambda: (0, 0))],
        out_shape=[jax.ShapeDtypeStruct((8, D_MODEL), F32)] * 4 + [jax.ShapeDtypeStruct((1, LANE), F32)],
        compiler_params=_cparams(),
    )(gathered, w, m, v)


def _unpack_small(p, shapes):
    out = {n: p[i].reshape(shapes[n]) for i, n in enumerate(SMALL_ROWS)}
    out["hg_norm_w"] = p[ROW_HG_NORM, :HEAD_DIM].reshape(shapes["hg_norm_w"])
    out["hg_lower_bounds"] = p[ROW_LB0:ROW_LB1 + 1, :HG_WIDTH].reshape(shapes["hg_lower_bounds"])
    return out


def _concat_cols(pieces, name):
    rows = pieces[0].shape[0]
    widths = [p.shape[1] for p in pieces]
    tr = ROW_BLOCK // 2

    def kern(*refs):
        o_ref, off = refs[-1], 0
        for r, w in zip(refs[:-1], widths):
            o_ref[:, off:off + w] = r[...]
            off += w

    return pl.pallas_call(
        kern, name=name, grid=(rows // tr,),
        in_specs=[pl.BlockSpec((tr, w), lambda i: (i, 0)) for w in widths],
        out_specs=pl.BlockSpec((tr, sum(widths)), lambda i: (i, 0)),
        out_shape=jax.ShapeDtypeStruct((rows, sum(widths)), pieces[0].dtype),
        compiler_params=_cparams(("parallel",)),
    )(*pieces)


WHOLE = lambda f: (f, 0, None)
MID_MATRICES = ("w_branch_a", "w_branch_b", "w_out", "wq_cross", "wkv_cross", "wo_cross")
MID_WEIGHTS = MID_MATRICES
W_IN_PIECES = [("w_in", r0, 512) for r0 in range(0, D_MODEL // 2, 512)]
W13_PIECES = [("w13", r0, 512) for r0 in range(0, D_MODEL // 2, 512)]
GATHER_GROUPS = [("mid", [WHOLE(f) for f in MID_MATRICES]), ("w13a", W13_PIECES[:1]), ("w13b", W13_PIECES[1:]),
                 ("w2", [WHOLE("w2")])]
OTHER_WEIGHTS = ["w1", "w3", "w2"] + list(MID_WEIGHTS)
BEFORE = {
    "hgrn_fwd": [("wait", "mid")],
    "mm_out": [("wait", "w13a")],
    "mm_o": [("wait", "w13b")],
    "mm_w2": [("wait", "w2"), ("run", ("d2d", [WHOLE("w2")]), "gather_hand_over_w2")],
    "mm_dwin_own": [("wait", "rs_w2"), ("chip_sum", "w2"), ("wait", "rs_w13"), ("chip_sum", "w13"),
                    ("wait", "pair_w_in")],
    "mm_dh": [("wait", "rs_mid")] + [("chip_sum", f) for f in MID_MATRICES],
}
CARRY = {
    "hgrn_fwd": [("d2d", [WHOLE(f) for f in MID_MATRICES])],
    "mm_out": [("d2d", W13_PIECES[:1])],
    "mm_o": [("d2d", W13_PIECES[1:])],
    "mm_du": [("pairx_whole", ["w2"])],
    "mm_dhf": [("pairx_whole", ["w13"])],
    "attn_bwd_g0": [("pairx", list(MID_MATRICES))],
    "mm_dh": [("share", OTHER_WEIGHTS)],
}
AFTER = {
    "mm_dw2_own": [("start", "rs_w2", [WHOLE("w2")])],
    "mm_dw13_own": [("start", "rs_w13", [WHOLE("w13")])],
    "attn_bwd_g0": [("pair_sum", f) for f in MID_MATRICES] + [("start", "rs_mid", [WHOLE(f) for f in MID_MATRICES])],
    "mm_dwin_sibling": [("start_pairx", "pair_w_in", ["w_in"])],
    "mm_dwin_own": [("start", "rs_w_in", [WHOLE("w_in")])],
}
FINISH = [
    ("adamw", OTHER_WEIGHTS), ("wait", "rs_w_in"), ("chip_sum", "w_in"), ("share_and_small", ["w_in"]),
    ("adamw", ["w_in"]),
]


class _Late:
    def __init__(self, read):
        self.read = read


class _Net:
    def __init__(self, full, pos=None, shard_shapes=None, comm=True, specs=FULL_SPECS, place=WEIGHT_PLACE):
        self.full, self.pos, self.shard_shapes, self.comm = dict(full), pos, shard_shapes, comm
        self.specs, self.place = specs, place
        self.gw, self.recv, self.psum, self.slots, self.grads = {}, {}, {}, {}, {}
        self.gw_sibling = {}
        self.pending, self.token, self.last = {}, None, None
        self.ahead = []

    def _make(self, kind, arg):
        if kind == "gather":
            return _gather_ici_comm(self.full, arg, self.specs)
        if kind == "ring":
            return _gather_ring_comm(self.full, *arg, self.specs)
        if kind == "d2d":
            return _gather_d2d_comm(self.full, arg, self.specs)
        if kind == "pairx":
            return _pairx_comm(self.gw, arg, self.specs)
        if kind == "pairx_whole":
            return _pairx_comm(self.gw_sibling, arg, self.specs, whole=True)
        if kind == "pairx_split":
            return _pairx_comm(self.gw_sibling, arg, self.specs, whole=True, recv=self.recv)
        if kind == "chipx":
            return _chipx_comm(self.psum, self.slots, arg, self.specs)
        assert kind == "share"
        return _share_comm(self.grads, arg, self.specs, self.place)

    def _store(self, kind, res):
        if kind in ("gather", "ring", "d2d"):
            self.full.update(res)
        elif kind in ("pairx", "pairx_whole", "pairx_split"):
            for (tag, f), a in res.items():
                (self.gw if tag == "g" else self.recv)[f] = a
        elif kind == "chipx":
            for (tag, f), a in res.items():
                (self.psum if tag == "p" else self.slots)[f] = a
        else:
            self.grads.update(res)

    def run_comm(self, item, name, extra=()):
        kind, arg = item
        res = _run_comm([self._make(kind, arg), *extra], name)
        self._store(kind, res[0])
        return res[1:]

    @staticmethod
    def _others(after, items):
        own = [a for cm in items for a in cm.arrays.values()]
        return [a for a in after if a is not None and all(a is not o for o in own)]

    def start(self, groups, kind, name):
        items = [self._make(kind, jobs) for _, jobs in groups]
        after = self._others([self.last], items)
        res, sems, token = _split_start(items, name, after=after[0] if after else None)
        for (group, jobs), r, s in zip(groups, res, sems):
            self._store(kind, r)
            self.pending[group] = (kind, jobs, s)
        self.token = self.last = token

    def wait(self, group, after=()):
        kind, jobs, sems = self.pending.pop(group)
        item = self._make(kind, jobs)
        res = _split_wait([item], [sems], self._others([self.last, *after], [item]), f"wait_{group}")[0]
        self._store(kind, res)

    def step(self, step):
        if step[0] == "wait":
            self.wait(step[1], after=self.ahead)
            self.ahead = []
        elif step[0] == "start":
            self.start([(step[1], step[2])], "chipx", f"start_{step[1]}")
        elif step[0] == "start_pairx":
            for f in step[2]:
                self.recv[f] = lax.empty(_half_shape(*self.specs[f]), BF16)
            self.start([(step[1], step[2])], "pairx_split", f"start_{step[1]}")
        elif step[0] == "pair_sum":
            f = step[1]
            self.psum[f], self.slots[f] = _pair_sum(self.gw[f], self.recv[f], self.pos, self.specs[f],
                                                    f"rs_pair_sum_{f}")
        elif step[0] == "chip_sum":
            f = step[1]
            sums = _chip_sum(self.psum[f], self.slots[f], self.pos, f, self.shard_shapes,
                             self.specs, self.place, f"rs_chip_sum_{f}")
            self.grads.update(sums)
            self.ahead += list(sums.values())
        else:
            assert step[0] == "run"
            self.run_comm(step[1], step[2])

    def call(self, fn, name, *args, grad_of=None, sibling_half=False, pair_sum_of=None, **kw):
        for step in (BEFORE.get(name, []) if self.comm else []):
            self.step(step)
        self.ahead = []
        late = lambda a: a.read() if isinstance(a, _Late) else a
        args = [late(a) for a in args]
        kw = {key: late(val) for key, val in kw.items()}
        items = CARRY.get(name, []) if self.comm else []
        carried = [self._make(k, a) for k, a in items]
        if self.token is not None:
            carried.append(_Token(self.token))
            self.token = None
        out, res = fn(*args, name=name, carried=carried, **kw)
        if grad_of is not None:
            (self.gw_sibling if sibling_half else self.gw)[grad_of] = out
        if pair_sum_of is not None:
            self.psum[pair_sum_of] = out
            self.slots[pair_sum_of] = lax.empty((N_CHIPS - 1,) + _slot_shape(self.specs[pair_sum_of]), BF16)
        self.last = jax.tree.leaves(out)[0]
        for (kind, _), r in zip(items, res):
            self._store(kind, r)
        for step in (AFTER.get(name, []) if self.comm else []):
            self.step(step)
        return out


def _local_step(net, x, h, mem, target, small, h_halves=None, core_row=None):
    full, call = net.full, net.call
    proj = call(_mm, "mm_proj", h, full["w_in"], mode="nn", out_dtype=F32)
    att = [call(_attn_fwd, f"attn_fwd_g{g}", proj, g) for g in range(3)]
    outs, lses = [a[0] for a in att], [a[1] for a in att]
    o_att = _attn_merge_fwd(outs, lses, "attn_merge")
    oraw, o_hg, states = call(_hg_fwd, "hgrn_fwd", proj, small["hg_lower_bounds"], small["hg_norm_w"])
    ya = call(_mm, "mm_branch_a", o_att, full["w_branch_a"], mode="nn", out_dtype=F32)
    yb, merged = call(_branch_b_gate, "mm_branch_b", o_hg, full["w_branch_b"], proj, ya)
    x1, hc = call(_residual_norm, "mm_out", merged, full["w_out"], x, small["ln_cross_w"])

    mn = _rms_fwd(mem, small["ln_mem_w"], "rms_mem")
    qc = call(_mm, "mm_q", hc, full["wq_cross"], mode="nn", out_dtype=F32)
    kvc = call(_mm, "mm_kv", mn, full["wkv_cross"], mode="nn", out_dtype=F32)
    oc = call(_cross_fwd, "cross_fwd", qc, kvc)
    x2, hf, *hf_halves = call(_residual_norm, "mm_o", oc, full["wo_cross"], x1, small["ln_ffn_w"], core_row=core_row)

    ab, u = call(_ff_up, "mm_w13", hf, full["w13"])
    x3 = call(_mm, "mm_w2", u, _Late(lambda: full["w2"]), mode="nn", out_dtype=F32, res=x2)

    dx3, dg_final, loss, dx3_bf16, *dx3_halves = _loss_head(x3, small["ln_final_w"], target, "loss_head", core_row)

    gs = {"ln_final_w": dg_final}
    if net.comm:
        call(_mm, "mm_dw2_sibling", u, dx3_halves[0], mode="tn", out_dtype=BF16, grad_of="w2", sibling_half=True)
        dab = call(_ff_down_bwd, "mm_du", dx3_bf16, full["w2"], ab)
        call(_mm, "mm_dw2_own", u, dx3_halves[1], mode="tn", out_dtype=BF16, res=_Late(lambda: net.recv["w2"]),
             pair_sum_of="w2")
        call(_mm, "mm_dw13_sibling", hf_halves[0], dab, mode="tn", out_dtype=BF16, grad_of="w13", sibling_half=True)
        dhf = call(_mm, "mm_dhf", dab, full["w13"], mode="nt", out_dtype=F32)
        call(_mm, "mm_dw13_own", hf_halves[1], dab, mode="tn", out_dtype=BF16, res=_Late(lambda: net.recv["w13"]),
             pair_sum_of="w13")
    else:
        call(_mm, "mm_dw2", u, dx3_bf16, mode="tn", out_dtype=BF16, grad_of="w2")
        dab = call(_ff_down_bwd, "mm_du", dx3_bf16, full["w2"], ab)
        call(_mm, "mm_dw13", hf, dab, mode="tn", out_dtype=BF16, grad_of="w13")
        dhf = call(_mm, "mm_dhf", dab, full["w13"], mode="nt", out_dtype=F32)
    dx2, gs["ln_ffn_w"], dx2_bf16 = _rms_bwd(x2, small["ln_ffn_w"], dhf, dx3, "rms_ffn_bwd", bf16_copy=True)
    doc = call(_mm, "mm_doc", dx2_bf16, full["wo_cross"], mode="nt", out_dtype=BF16)
    call(_mm, "mm_dwo", oc, dx2_bf16, mode="tn", out_dtype=BF16, grad_of="wo_cross")
    dqc, dkvc = _cross_bwd(qc, kvc, doc, "cross_bwd")
    call(_mm, "mm_dwq", hc, dqc, mode="tn", out_dtype=BF16, grad_of="wq_cross")
    dx1, gs["ln_cross_w"], dx1_bf16 = call(_norm_bwd_residual, "mm_dhc", dqc, full["wq_cross"], x1,
                                           small["ln_cross_w"], dx2)
    call(_mm, "mm_dwkv", mn, dkvc, mode="tn", out_dtype=BF16, grad_of="wkv_cross")
    dya, dyb, dga, dgb = call(_dmerged_gate_bwd, "mm_dmerged", dx1_bf16, full["w_out"], proj, ya, yb)
    call(_mm, "mm_dwout", merged, dx1_bf16, mode="tn", out_dtype=BF16, grad_of="w_out")
    call(_mm, "mm_dwa", o_att, dya, mode="tn", out_dtype=BF16, grad_of="w_branch_a")
    do_att = call(_mm, "mm_doatt", dya, full["w_branch_a"], mode="nt", out_dtype=F32)
    call(_mm, "mm_dwb", o_hg, dyb, mode="tn", out_dtype=BF16, grad_of="w_branch_b")
    do_hg = call(_mm, "mm_dohg", dyb, full["w_branch_b"], mode="nt", out_dtype=F32)
    dqh, dfh, dih, dgh, dlb, gs["hg_norm_w"] = call(
        _hg_bwd, "hgrn_bwd", proj, small["hg_lower_bounds"], small["hg_norm_w"], oraw, states, do_hg)
    gs["hg_lb"] = dlb
    do_gs, dl_gs = call(_attn_merge_bwd, "attn_merge_bwd", outs, lses, do_att)
    dqs, dks, dvs = zip(*[call(_attn_bwd, f"attn_bwd_g{g}", proj, g, lses[g], do_gs[g], dl_gs[g]) for g in range(3)])
    dproj = _concat_cols([*dqs, *dks, *dvs, dqh, dfh, dih, dgh, dga, dgb], "dproj_concat")
    if net.comm:
        h_sibling, h_own = h_halves
        call(_mm, "mm_dwin_sibling", h_sibling, dproj, mode="tn", out_dtype=BF16, grad_of="w_in", sibling_half=True)
    dmn = call(_mm, "mm_dmn", dkvc, full["wkv_cross"], mode="nt", out_dtype=F32)
    _, gs["ln_mem_w"] = _rms_bwd(mem, small["ln_mem_w"], dmn, None, "rms_mem_bwd")
    net.ahead.append(gs["ln_mem_w"])
    if net.comm:
        call(_mm, "mm_dwin_own", h_own, dproj, mode="tn", out_dtype=BF16, res=_Late(lambda: net.recv["w_in"]),
             pair_sum_of="w_in")
    else:
        call(_mm, "mm_dwin", h, dproj, mode="tn", out_dtype=BF16, grad_of="w_in")
    dh = call(_mm, "mm_dh", dproj, full["w_in"], mode="nt", out_dtype=F32)
    dx, gs["ln_mix_w"] = _rms_bwd(x, small["ln_mix_w"], dh, dx1, "rms_mix_bwd")
    return loss, dx, gs


WEIGHT_ORDER = ("ln_mix_w", "w_in", "hg_norm_w", "hg_lower_bounds", "w_branch_a", "w_branch_b", "w_out",
                "ln_cross_w", "ln_mem_w", "wq_cross", "wkv_cross", "wo_cross", "ln_ffn_w", "w1", "w3", "w2",
                "ln_final_w")


def kernel(x, mem, ln_mix_w, w_in, hg_norm_w, hg_lower_bounds, w_branch_a, w_branch_b, w_out, ln_cross_w, ln_mem_w, wq_cross, wkv_cross, wo_cross, ln_ffn_w, w1, w3, w2, ln_final_w, loss_target, m_ln_mix_w, m_w_in, m_hg_norm_w, m_hg_lower_bounds, m_w_branch_a, m_w_branch_b, m_w_out, m_ln_cross_w, m_ln_mem_w, m_wq_cross, m_wkv_cross, m_wo_cross, m_ln_ffn_w, m_w1, m_w3, m_w2, m_ln_final_w, v_ln_mix_w, v_w_in, v_hg_norm_w, v_hg_lower_bounds, v_w_branch_a, v_w_branch_b, v_w_out, v_ln_cross_w, v_ln_mem_w, v_wq_cross, v_wkv_cross, v_wo_cross, v_ln_ffn_w, v_w1, v_w3, v_w2, v_ln_final_w):
    args = dict(locals())
    w = {n: args[n] for n in WEIGHT_ORDER}
    m = {n: args["m_" + n] for n in WEIGHT_ORDER}
    v = {n: args["v_" + n] for n in WEIGHT_ORDER}
    shapes = {n: w[n].shape for n in WEIGHT_ORDER}
    mat = lambda a: a.reshape(a.shape[-2:])
    shard_shapes = {n: shapes[n][-2:] for n in BIG_WEIGHTS}

    pos = _mesh_scalars()

    def cast(f, token=None):
        return _cast_into_full({n: mat(w[n]) for n in BIG_WEIGHTS if WEIGHT_PLACE[n][0] == f}, f, pos,
                               FULL_SPECS, WEIGHT_PLACE, f"cast_{f}", token)

    net = _Net({"w_in": cast("w_in")}, pos, shard_shapes)
    net.start([(f"ring_a{i}", (*job, "a")) for i, job in enumerate(W_IN_PIECES)], "ring", "gather_start_w_in")
    rest = {f: cast(f, net.token) for f in FULL_SPECS if f != "w_in"}
    net.full.update(rest)
    small = {n: w[n].reshape(1, -1) for n in SMALL_ROWS}
    small["hg_norm_w"] = w["hg_norm_w"].reshape(1, HEAD_DIM)
    small["hg_lower_bounds"] = w["hg_lower_bounds"]
    x2d = x.reshape(SEQ, D_MODEL)
    h, *h_halves = _rms_fwd_halves(x2d, small["ln_mix_w"], pos, "rms_mix")
    for i, job in enumerate(W_IN_PIECES):
        net.wait(f"ring_a{i}", after=[*rest.values(), h] if i == 0 else ())
        net.start([(f"ring_b{i}", (*job, "b"))], "ring", f"gather_pass_on_w_in{i}")
    net.start(GATHER_GROUPS, "gather", "gather_start_rest")
    for i, job in enumerate(W_IN_PIECES):
        net.wait(f"ring_b{i}")
        net.run_comm(("d2d", [job]), f"gather_hand_over_w_in{i}")
    core_row = jnp.full((1, D_MODEL), lax.axis_index("c").astype(F32))
    loss, dx, gs = _local_step(net, x2d, h, mem.reshape(MEM_LEN, D_MODEL), loss_target.reshape(SEQ, D_MODEL), small,
                               h_halves, core_row)

    out_g, out_d, out_m, out_v = {}, {}, {}, {}
    net.last = dx
    for step in FINISH:
        if step[0] == "adamw":
            for n in step[1]:
                out_d[n], out_m[n], out_v[n], out_g[n] = net.call(_adamw, f"adamw_{n}", mat(w[n]), net.grads[n],
                                                                  mat(m[n]), mat(v[n]))
        elif step[0] == "wait":
            net.wait(step[1], after=list(out_d.values()))
        elif step[0] == "share_and_small":
            pad = lambda a: jnp.pad(a, ((0, 0), (0, D_MODEL - a.shape[1])))
            part = jnp.concatenate(
                [gs[n] for n in SMALL_ROWS]
                + [pad(jnp.concatenate([gs["hg_norm_w"][0], loss], axis=1)), pad(gs["hg_lb"]),
                   pad(gs["hg_norm_w"][1]) if len(gs["hg_norm_w"]) > 1 else jnp.zeros((1, D_MODEL), F32)], axis=0)
            rows = net.run_comm(("share", step[1]), "rs_sibling_share_and_gather_small",
                                extra=[_small_gather_comm(part)])[0][SMALL_ROWS_OF_ALL]
            sg, sd, sm, sv, loss_tot = _small_update(rows, _pack_small(w), _pack_small(m), _pack_small(v))
            for dst, packed in ((out_g, sg), (out_d, sd), (out_m, sm), (out_v, sv)):
                dst.update(_unpack_small(packed, shapes))
        else:
            net.step(step)

    result = [loss_tot[0, 0], dx.reshape(x.shape)]
    for group in (out_g, out_d, out_m, out_v):
        result += [group[n].reshape(shapes[n]) for n in WEIGHT_ORDER]
    return tuple(result)
```

```python
import math

import jax
import jax.numpy as jnp
from jax import lax
from jax.experimental import pallas as pl
from jax.experimental.pallas import tpu as pltpu

F32 = jnp.float32
BF16 = jnp.bfloat16
MESH = pl.DeviceIdType.MESH

D_MODEL = 2048
SEQ = 2048
HEAD_DIM = 128
MEM_LEN = 256
ATT_GROUPS = ((128, 1), (512, 4), (2048, 16))
ATT_HEADS = 4
ATT_WIDTH = 3 * ATT_HEADS * HEAD_DIM
ATT_OUT = ATT_HEADS * HEAD_DIM
ATT_BLOCK = 128
HG_HEADS = 8
HG_WIDTH = HG_HEADS * HEAD_DIM
HG_CHUNK = 64
IN_WIDTH = 3 * ATT_WIDTH + 4 * HG_WIDTH + 2 * D_MODEL
CROSS_HEADS = 4
CROSS_WIDTH = CROSS_HEADS * HEAD_DIM
D_FF = 5632
RMS_EPS = 1e-6
ADAM_LR = 0.001
ADAM_B1 = 0.9
ADAM_B2 = 0.999
ADAM_EPS = 1e-08
ADAM_WD = 0.01
ADAM_STEP = 10
N_CHIPS = 4
N_DEV = 8

VMEM_LIMIT_BYTES = 56 * 1024 * 1024
LANE = 128
MXU_WIDTH = 256
MM_TILE_CAP = 1536
TRANSPOSE_CHUNK = 512
ANY = pl.BlockSpec(memory_space=pl.ANY)


def _cparams(sem=None):
    return pltpu.CompilerParams(dimension_semantics=sem, vmem_limit_bytes=VMEM_LIMIT_BYTES)


def _div(n, cap, mult):
    best = None
    for d in range(mult, min(n, cap) + 1, mult):
        if n % d == 0:
            best = d
    assert best is not None, (n, cap, mult)
    return best


def _sigmoid(x):
    return 1.0 / (1.0 + jnp.exp(-x))


def _dot(a, b):
    return jnp.dot(a.astype(BF16), b.astype(BF16), preferred_element_type=F32)


def _dot_nt(a, b):
    return lax.dot_general(a.astype(BF16), b.astype(BF16), (((1,), (1,)), ((), ())),
                           preferred_element_type=F32)


def _dot_tn(a, b):
    return jnp.dot(a.astype(F32).T.astype(BF16), b.astype(BF16), preferred_element_type=F32)


def _dot_exact(a, b):
    return jnp.dot(a, b, precision=lax.Precision.HIGHEST, preferred_element_type=F32)


class _Carried:
    def __init__(self, arrays, fresh, n_sems, start, finish, mid=None, reads=None):
        self.arrays, self.fresh, self.n_sems, self.reads = arrays, fresh, n_sems, reads or {}
        self.start, self.mid, self.finish = start, mid, finish


class _Token:
    def __init__(self, array):
        self.array = array


TOKEN_SHAPE = (8, LANE)
ADAM_RING = 4


def _carried_layout(carried):
    akeys = list(dict.fromkeys(k for cm in carried for k in cm.arrays))
    fkeys = [(ci, k) for ci, cm in enumerate(carried) for k in cm.fresh]
    arrays = [next(cm.arrays[k] for cm in carried if k in cm.arrays) for k in akeys]
    shapes = [jax.ShapeDtypeStruct(a.shape, a.dtype) for a in arrays] + [carried[ci].fresh[k] for ci, k in fkeys]
    sems = []
    for cm in carried:
        sems += [pltpu.SemaphoreType.DMA((cm.n_sems,)), pltpu.SemaphoreType.DMA((cm.n_sems,))]
    return akeys, fkeys, arrays, shapes, sems


def _carried_reads(carried):
    rkeys = list(dict.fromkeys(k for cm in carried for k in cm.reads))
    return rkeys, [next(cm.reads[k] for cm in carried if k in cm.reads) for k in rkeys]


def _carried_results(carried, akeys, fkeys, outs, rkeys=(), read_refs=()):
    shared = dict(zip(akeys, outs[:len(akeys)]))
    shared.update(zip(rkeys, read_refs))
    res = [{k: shared[k] for k in list(cm.arrays) + [r for r in cm.reads if r in shared]} for cm in carried]
    for (ci, k), o in zip(fkeys, outs[len(akeys):]):
        res[ci][k] = o
    return res


def _pcall(kern, *, name, grid, in_specs, out_specs, out_shape, args, scratch_shapes=(), semantics=None,
           carried=()):
    tokens = [c.array for c in carried if isinstance(c, _Token)]
    carried = [c for c in carried if not isinstance(c, _Token)]
    single = not isinstance(out_shape, (list, tuple))
    out_specs = [out_specs] if single else list(out_specs)
    out_shape = [out_shape] if single else list(out_shape)
    n_real, n_out, n_scr = len(in_specs), len(out_shape), len(scratch_shapes)
    in_specs = list(in_specs) + [pl.BlockSpec(TOKEN_SHAPE, lambda *_: (0, 0))] * len(tokens)
    args = list(args) + tokens
    n_in = len(in_specs)
    if not carried:
        def plain(*refs):
            kern(*refs[:n_real], *refs[n_in:])

        outs = pl.pallas_call(plain if tokens else kern, name=name, grid=grid, in_specs=in_specs,
                              out_specs=out_specs, out_shape=out_shape, scratch_shapes=list(scratch_shapes),
                              compiler_params=_cparams(semantics))(*args)
        return (outs[0] if single else list(outs)), []
    akeys, fkeys, arrays, shapes, sems = _carried_layout(carried)
    rkeys, reads = _carried_reads(carried)
    n_a, n_f, n_r = len(akeys), len(fkeys), len(rkeys)
    total = math.prod(grid)
    mid_step = min(total - 1, (17 * total) // 20)

    def wrapped(*refs):
        ins = refs[:n_real]
        r0 = n_in + n_a
        o0 = r0 + n_r
        outs = refs[o0:o0 + n_out]
        a0 = o0 + n_out
        s0 = a0 + n_a + n_f
        per = _carried_results(carried, akeys, fkeys, refs[a0:s0], rkeys, refs[r0:o0])
        scratch = refs[s0:s0 + n_scr]
        sem = refs[s0 + n_scr:]
        step = 0
        for d, g in enumerate(grid):
            step = step * g + pl.program_id(d)

        @pl.when(step == 0)
        def _():
            for ci, cm in enumerate(carried):
                cm.start(per[ci], sem[2 * ci], sem[2 * ci + 1])

        kern(*ins, *outs, *scratch)

        @pl.when(step == mid_step)
        def _():
            for ci, cm in enumerate(carried):
                if cm.mid is not None:
                    cm.mid(per[ci], sem[2 * ci], sem[2 * ci + 1])

        @pl.when(step == total - 1)
        def _():
            for ci, cm in enumerate(carried):
                cm.finish(per[ci], sem[2 * ci], sem[2 * ci + 1])

    outs = pl.pallas_call(
        wrapped, name=name, grid=grid,
        in_specs=list(in_specs) + [ANY] * (n_a + n_r), out_specs=out_specs + [ANY] * (n_a + n_f),
        out_shape=out_shape + shapes,
        input_output_aliases={n_in + i: n_out + i for i in range(n_a)},
        scratch_shapes=list(scratch_shapes) + sems,
        compiler_params=_cparams(("arbitrary",) * len(grid)),
    )(*args, *arrays, *reads)
    res = _carried_results(carried, akeys, fkeys, outs[n_out:])
    return (outs[0] if single else list(outs[:n_out])), res


def _run_comm(carried, name):
    carried = list(carried)
    akeys, fkeys, arrays, shapes, sems = _carried_layout(carried)
    rkeys, reads = _carried_reads(carried)
    n_a, n_f, n_r = len(akeys), len(fkeys), len(rkeys)

    def body(*refs):
        o0 = n_a + n_r
        per = _carried_results(carried, akeys, fkeys, refs[o0:o0 + n_a + n_f], rkeys, refs[n_a:o0])
        sem = refs[o0 + n_a + n_f:]
        for hook in ("start", "mid", "finish"):
            for ci, cm in enumerate(carried):
                fn = getattr(cm, hook)
                if fn is not None:
                    fn(per[ci], sem[2 * ci], sem[2 * ci + 1])

    outs = pl.pallas_call(
        body, name=name, in_specs=[ANY] * (n_a + n_r), out_specs=[ANY] * (n_a + n_f), out_shape=shapes,
        input_output_aliases={i: i for i in range(n_a)}, scratch_shapes=sems,
    )(*arrays, *reads)
    return _carried_results(carried, akeys, fkeys, outs)


HBM_SPEC = pl.BlockSpec(memory_space=pltpu.HBM)
SEM_SPEC = pl.BlockSpec(memory_space=pltpu.SEMAPHORE)
SPLIT_EFFECT = pltpu.SideEffectType.DATAFLOW_SIDE_EFFECTING


def _in_hbm(a):
    return pltpu.with_memory_space_constraint(a, pltpu.HBM)


def _split_start(items, name, after=None):
    items = list(items)
    akeys, fkeys, arrays, shapes, sems = _carried_layout(items)
    assert not fkeys
    rkeys, reads = _carried_reads(items)
    n_a, n_s = len(akeys), len(sems)
    n_after = n_a + (after is not None)
    n_in = n_after + len(rkeys)

    def body(*refs):
        per = _carried_results(items, akeys, [], refs[n_in:n_in + n_a], rkeys, refs[n_after:n_in])
        sem = refs[n_in + n_a:n_in + n_a + n_s]
        for ci, cm in enumerate(items):
            cm.start(per[ci], sem[2 * ci], sem[2 * ci + 1])
        token = refs[n_in + n_a + n_s]
        token[...] = jnp.zeros_like(token)

    outs = pl.pallas_call(
        body, name=name, in_specs=[HBM_SPEC] * n_a + [ANY] * (n_in - n_a),
        out_specs=[HBM_SPEC] * n_a + [SEM_SPEC] * n_s + [pl.BlockSpec(memory_space=pltpu.VMEM)],
        out_shape=[pltpu.HBM(s.shape, s.dtype) for s in shapes] + sems + [jax.ShapeDtypeStruct(TOKEN_SHAPE, F32)],
        input_output_aliases={i: i for i in range(n_a)},
        compiler_params=pltpu.CompilerParams(has_side_effects=SPLIT_EFFECT),
    )(*[_in_hbm(a) for a in arrays], *([after] if after is not None else []), *reads)
    res = _carried_results(items, akeys, [], outs[:n_a])
    sem_out = outs[n_a:n_a + n_s]
    return res, [(sem_out[2 * ci], sem_out[2 * ci + 1]) for ci in range(len(items))], outs[-1]


def _split_wait(items, sems, after, name):
    items = list(items)
    after = list(after) if isinstance(after, (list, tuple)) else [after]
    akeys, fkeys, arrays, shapes, _ = _carried_layout(items)
    rkeys, reads = _carried_reads(items)
    n_a, n_s = len(akeys), 2 * len(items)
    n_after = n_a + n_s + len(after)
    n_in = n_after + len(rkeys)

    def body(*refs):
        per = _carried_results(items, akeys, [], refs[n_in:], rkeys, refs[n_after:n_in])
        sem = refs[n_a:n_a + n_s]
        for ci, cm in enumerate(items):
            cm.finish(per[ci], sem[2 * ci], sem[2 * ci + 1])

    outs = pl.pallas_call(
        body, name=name, in_specs=[HBM_SPEC] * n_a + [SEM_SPEC] * n_s + [ANY] * (n_in - n_a - n_s),
        out_specs=[HBM_SPEC] * n_a, out_shape=[pltpu.HBM(s.shape, s.dtype) for s in shapes],
        input_output_aliases={i: i for i in range(n_a)},
        compiler_params=pltpu.CompilerParams(has_side_effects=SPLIT_EFFECT),
    )(*arrays, *[s for pair in sems for s in pair], *after, *reads)
    return _carried_results(items, akeys, [], outs)


def _mm(a, b, *, mode, out_dtype, name, res=None, carried=(), fused=None):
    if mode == "nn":
        (m, k), (k2, n) = a.shape, b.shape
    elif mode == "nt":
        (m, k), (n, k2) = a.shape, b.shape
    else:
        (k, m), (k2, n) = a.shape, b.shape
    assert k == k2, (name, a.shape, b.shape)
    tm = _div(m, MM_TILE_CAP, LANE)
    tn = _div(n, MM_TILE_CAP, MXU_WIDTH) if n % MXU_WIDTH == 0 else 0
    if tn < 1024:
        tn = _div(n, MM_TILE_CAP, LANE)
    out_shape = jax.ShapeDtypeStruct((m, n), out_dtype)

    if fused is not None:
        assert mode in ("nn", "nt") and res is None and k <= 2048
        tm, tn = fused["tile"]
        dot = _dot if mode == "nn" else _dot_nt
        n_x = len(fused["ins"])

        summed = [len(e) > 2 for e in fused["outs"]]

        def kern_fused(*refs):
            part = dot(refs[0][...], refs[1][...])
            outs = fused["post"](part, *[r[...] for r in refs[2:2 + n_x]])
            first_row_tile = pl.program_id(1) == 0
            for o_ref, val, acc in zip(refs[2 + n_x:], outs, summed):
                if not acc:
                    o_ref[...] = val.astype(o_ref.dtype)
                    continue

                @pl.when(first_row_tile)
                def _():
                    o_ref[...] = val

                @pl.when(jnp.logical_not(first_row_tile))
                def _():
                    o_ref[...] += val

        def tile_spec(shape, index=lambda i, j: (i, j)):
            return pl.BlockSpec(shape, lambda j, i: index(i, j))

        return _pcall(
            kern_fused, name=name, grid=(n // tn, m // tm),
            in_specs=[pl.BlockSpec((tm, k), lambda j, i: (i, 0)),
                      pl.BlockSpec((k, tn), lambda j, i: (0, j)) if mode == "nn"
                      else pl.BlockSpec((tn, k), lambda j, i: (j, 0))] + [tile_spec(*e[1:]) for e in fused["ins"]],
            out_specs=[tile_spec(*e[1:]) for e in fused["outs"]], out_shape=[e[0] for e in fused["outs"]],
            args=(a, b, *[e[0] for e in fused["ins"]]), semantics=("parallel", "arbitrary"), carried=carried)

    if mode == "tn":
        has_res_tn = res is not None

        def kern_tn(*refs):
            a_ref, b_ref, o_ref, at_ref = refs[0], refs[1], refs[-2], refs[-1]

            @pl.when(pl.program_id(1) == 0)
            def _():
                step = min(TRANSPOSE_CHUNK, k)
                for c0 in range(0, k, step):
                    at_ref[:, c0:c0 + step] = a_ref[c0:c0 + step, :].T

            part = jnp.dot(at_ref[...], b_ref[...].astype(BF16), preferred_element_type=F32)
            if has_res_tn:
                part = part + refs[2][...].astype(F32)
            o_ref[...] = part.astype(o_ref.dtype)

        o_spec = pl.BlockSpec((tm, tn), lambda i, j: (i, j))
        return _pcall(
            kern_tn, name=name, grid=(m // tm, n // tn),
            in_specs=[pl.BlockSpec((k, tm), lambda i, j: (0, i)),
                      pl.BlockSpec((k, tn), lambda i, j: (0, j))] + [o_spec] * has_res_tn,
            out_specs=o_spec, out_shape=out_shape, args=(a, b) + ((res,) if has_res_tn else ()),
            scratch_shapes=[pltpu.VMEM((tm, k), BF16)],
            semantics=("parallel", "arbitrary"), carried=carried)

    tk = k if k <= 2048 else _div(k, 3072, LANE)
    nk = k // tk
    a_spec = pl.BlockSpec((tm, tk), lambda i, j, kk: (i, kk))
    if mode == "nn":
        b_spec = pl.BlockSpec((tk, tn), lambda i, j, kk: (kk, j))
        dot = _dot
    else:
        b_spec = pl.BlockSpec((tn, tk), lambda i, j, kk: (j, kk))
        dot = _dot_nt
    o_spec = pl.BlockSpec((tm, tn), lambda i, j, kk: (i, j))
    in_specs = [a_spec, b_spec]
    args = [a, b]
    if res is not None:
        in_specs.append(o_spec)
        args.append(res)
    has_res = res is not None

    def kern(*refs):
        a_ref, b_ref = refs[0], refs[1]
        r_ref = refs[2] if has_res else None
        o_ref = refs[3] if has_res else refs[2]
        if nk == 1:
            part = dot(a_ref[...], b_ref[...])
            if has_res:
                part = part + r_ref[...]
            o_ref[...] = part.astype(o_ref.dtype)
            return
        @pl.when(pl.program_id(2) == 0)
        def _():
            o_ref[...] = r_ref[...] if has_res else jnp.zeros_like(o_ref)

        o_ref[...] += dot(a_ref[...], b_ref[...])

    assert nk == 1 or out_dtype == F32, name
    return _pcall(
        kern, name=name, grid=(m // tm, n // tn, nk),
        in_specs=in_specs, out_specs=o_spec, out_shape=out_shape, args=args,
        semantics=("parallel", "parallel", "arbitrary"), carried=carried)


ROW_BLOCK = 512


def _rms_fwd(x, g, name):
    t, d = x.shape
    tr = min(ROW_BLOCK, t)

    def kern(x_ref, g_ref, o_ref):
        xf = x_ref[...]
        r = lax.rsqrt(jnp.mean(xf * xf, axis=-1, keepdims=True) + RMS_EPS)
        o_ref[...] = (xf * r * g_ref[...]).astype(o_ref.dtype)

    return pl.pallas_call(
        kern, name=name, grid=(t // tr,),
        in_specs=[pl.BlockSpec((tr, d), lambda i: (i, 0)), pl.BlockSpec((1, d), lambda i: (0, 0))],
        out_specs=pl.BlockSpec((tr, d), lambda i: (i, 0)),
        out_shape=jax.ShapeDtypeStruct((t, d), BF16),
        compiler_params=_cparams(("parallel",)),
    )(x, g)


def _rms_fwd_halves(x, g, pos, name):
    t, d = x.shape
    tr = min(ROW_BLOCK, t)
    half = d // 2

    def kern(pos_ref, x_ref, g_ref, o_ref, sib_ref, own_ref):
        xf = x_ref[...]
        r = lax.rsqrt(jnp.mean(xf * xf, axis=-1, keepdims=True) + RMS_EPS)
        h = (xf * r * g_ref[...]).astype(o_ref.dtype)
        o_ref[...] = h
        is_south = pos_ref[0] == 0
        sib_ref[...] = jnp.where(is_south, h[:, half:], h[:, :half])
        own_ref[...] = jnp.where(is_south, h[:, :half], h[:, half:])

    row = pl.BlockSpec((tr, d), lambda i, pos_ref: (i, 0))
    part = pl.BlockSpec((tr, half), lambda i, pos_ref: (i, 0))
    return pl.pallas_call(
        kern, name=name,
        grid_spec=_grid_spec((t // tr,), [row, pl.BlockSpec((1, d), lambda i, pos_ref: (0, 0))], [row, part, part]),
        out_shape=[jax.ShapeDtypeStruct((t, d), BF16)] + [jax.ShapeDtypeStruct((t, half), BF16)] * 2,
        compiler_params=_cparams(("parallel",)),
    )(pos, x, g)


def _rms_bwd(x, g, dh, res, name, bf16_copy=False):
    t, d = x.shape
    tr = min(ROW_BLOCK, t)
    has_res = res is not None

    def kern(*refs):
        x_ref, g_ref, dh_ref = refs[:3]
        r_ref = refs[3] if has_res else None
        dx_ref, dg_ref = refs[3 + has_res], refs[4 + has_res]
        xf = x_ref[...]
        r = lax.rsqrt(jnp.mean(xf * xf, axis=-1, keepdims=True) + RMS_EPS)
        xn = xf * r
        dh_ = dh_ref[...]
        dhg = dh_ * g_ref[...]
        dx = r * (dhg - xn * jnp.mean(dhg * xn, axis=-1, keepdims=True))
        if has_res:
            dx = dx + r_ref[...]
        dx_ref[...] = dx
        if bf16_copy:
            refs[-1][...] = dx.astype(BF16)
        part = jnp.sum(dh_ * xn, axis=0, keepdims=True)

        @pl.when(pl.program_id(0) == 0)
        def _():
            dg_ref[...] = part

        @pl.when(pl.program_id(0) > 0)
        def _():
            dg_ref[...] += part

    row = pl.BlockSpec((tr, d), lambda i: (i, 0))
    vec = pl.BlockSpec((1, d), lambda i: (0, 0))
    in_specs = [row, vec, row] + ([row] if has_res else [])
    args = [x, g, dh] + ([res] if has_res else [])
    return pl.pallas_call(
        kern, name=name, grid=(t // tr,), in_specs=in_specs, out_specs=[row, vec] + [row] * bf16_copy,
        out_shape=[jax.ShapeDtypeStruct((t, d), F32), jax.ShapeDtypeStruct((1, d), F32)]
        + [jax.ShapeDtypeStruct((t, d), BF16)] * bf16_copy,
        compiler_params=_cparams(("arbitrary",)),
    )(*args)


def _loss_head(x3, g, target, name, core_row=None):
    t, d = x3.shape
    tr = ROW_BLOCK
    with_halves = core_row is not None

    def kern(x_ref, g_ref, t_ref, *rest):
        dx_ref, dg_ref, loss_ref, dxb_ref = rest[with_halves:with_halves + 4]
        xf = x_ref[...]
        r = lax.rsqrt(jnp.mean(xf * xf, axis=-1, keepdims=True) + RMS_EPS)
        xn = xf * r
        gg = g_ref[...]
        err = xn * gg - t_ref[...]
        lpart = 0.5 * jnp.sum(jnp.mean(err * err, axis=-1, keepdims=True), axis=0, keepdims=True)
        dy = err * (1.0 / d)
        dyg = dy * gg
        dx = r * (dyg - xn * jnp.mean(dyg * xn, axis=-1, keepdims=True))
        dx_ref[...] = dx
        dxb = dx.astype(BF16)
        dxb_ref[...] = dxb
        if with_halves:
            rest[-2][...], rest[-1][...] = _column_halves(dxb, rest[0][...])
        gpart = jnp.sum(dy * xn, axis=0, keepdims=True)
        lrow = jnp.broadcast_to(lpart, (1, LANE))

        @pl.when(pl.program_id(0) == 0)
        def _():
            dg_ref[...] = gpart
            loss_ref[...] = lrow

        @pl.when(pl.program_id(0) > 0)
        def _():
            dg_ref[...] += gpart
            loss_ref[...] += lrow

    row = pl.BlockSpec((tr, d), lambda i: (i, 0))
    vec = pl.BlockSpec((1, d), lambda i: (0, 0))
    part = pl.BlockSpec((tr, d // 2), lambda i: (i, 0))
    return pl.pallas_call(
        kern, name=name, grid=(t // tr,), in_specs=[row, vec, row] + [vec] * with_halves,
        out_specs=[row, vec, pl.BlockSpec((1, LANE), lambda i: (0, 0)), row] + [part] * (2 * with_halves),
        out_shape=[jax.ShapeDtypeStruct((t, d), F32), jax.ShapeDtypeStruct((1, d), F32),
                   jax.ShapeDtypeStruct((1, LANE), F32), jax.ShapeDtypeStruct((t, d), BF16)]
        + [jax.ShapeDtypeStruct((t, d // 2), BF16)] * (2 * with_halves),
        compiler_params=_cparams(("arbitrary",)),
    )(x3, g, target, *([core_row] if with_halves else []))


ATT_SCALE = HEAD_DIM ** -0.5
Q_BLOCK0, K_BLOCK0, V_BLOCK0 = 0, ATT_WIDTH // HEAD_DIM, 2 * ATT_WIDTH // HEAD_DIM


def _residue_rows(dil, r, n):
    if dil == 1:
        return pl.ds(n * ATT_BLOCK, ATT_BLOCK)
    return pl.ds(n * ATT_BLOCK * dil + r, ATT_BLOCK, stride=dil)


def _band_mask(with_prev):
    width = 2 * ATT_BLOCK if with_prev else ATT_BLOCK
    iq = lax.broadcasted_iota(jnp.int32, (ATT_BLOCK, width), 0)
    ik = lax.broadcasted_iota(jnp.int32, (ATT_BLOCK, width), 1)
    if not with_prev:
        return ik <= iq
    return ((ik < ATT_BLOCK) & (iq <= ik)) | ((ik >= ATT_BLOCK) & ((ik - ATT_BLOCK) <= iq))


def _band_keys(ref, dil, r, n):
    own = ref[_residue_rows(dil, r, n), :]
    if n == 0:
        return own
    return jnp.concatenate([ref[_residue_rows(dil, r, n - 1), :], own], axis=0)


def _attn_col_spec(base, grp):
    return pl.BlockSpec((SEQ, HEAD_DIM), lambda h: (0, base + grp * ATT_HEADS + h))


def _attn_fwd(proj, grp, name, carried=()):
    _, dil = ATT_GROUPS[grp]
    nb = SEQ // dil // ATT_BLOCK

    def kern(q_ref, k_ref, v_ref, o_ref, lse_ref):
        for r in range(dil):
            for n in range(nb):
                rows = _residue_rows(dil, r, n)
                s = _dot_nt(q_ref[rows, :], _band_keys(k_ref, dil, r, n)) * ATT_SCALE
                s = jnp.where(_band_mask(n > 0), s, -jnp.inf)
                m = jnp.max(s, axis=-1, keepdims=True)
                p = jnp.exp(s - m)
                l = jnp.sum(p, axis=-1, keepdims=True)
                o_ref[rows, :] = _dot(p / l, _band_keys(v_ref, dil, r, n))
                lse_ref[rows, :] = jnp.broadcast_to(m + jnp.log(l), (ATT_BLOCK, HEAD_DIM))

    out_spec = pl.BlockSpec((SEQ, HEAD_DIM), lambda h: (0, h))
    return _pcall(
        kern, name=name, grid=(ATT_HEADS,),
        in_specs=[_attn_col_spec(Q_BLOCK0, grp), _attn_col_spec(K_BLOCK0, grp), _attn_col_spec(V_BLOCK0, grp)],
        out_specs=[out_spec, out_spec],
        out_shape=[jax.ShapeDtypeStruct((SEQ, ATT_OUT), F32)] * 2, args=(proj, proj, proj),
        semantics=("parallel",), carried=carried)


def _attn_weights(l0, l1, l2):
    mx = jnp.maximum(jnp.maximum(l0, l1), l2)
    e0, e1, e2 = jnp.exp(l0 - mx), jnp.exp(l1 - mx), jnp.exp(l2 - mx)
    den = e0 + e1 + e2
    return e0 / den, e1 / den, e2 / den


def _attn_merge_fwd(outs, lses, name):
    tr = ROW_BLOCK

    def kern(o0, o1, o2, l0, l1, l2, out_ref):
        a0, a1, a2 = _attn_weights(l0[...], l1[...], l2[...])
        out_ref[...] = (a0 * o0[...] + a1 * o1[...] + a2 * o2[...]).astype(out_ref.dtype)

    spec = pl.BlockSpec((tr, ATT_OUT), lambda i: (i, 0))
    return pl.pallas_call(
        kern, name=name, grid=(SEQ // tr,), in_specs=[spec] * 6, out_specs=spec,
        out_shape=jax.ShapeDtypeStruct((SEQ, ATT_OUT), BF16),
        compiler_params=_cparams(("parallel",)),
    )(*outs, *lses)


def _attn_merge_bwd(outs, lses, do_att, name, carried=()):
    tr = ROW_BLOCK

    def kern(o0, o1, o2, l0, l1, l2, do_ref, d0, d1, d2, t0, t1, t2):
        alphas = _attn_weights(l0[...], l1[...], l2[...])
        do = do_ref[...]
        o_att = alphas[0] * o0[...] + alphas[1] * o1[...] + alphas[2] * o2[...]
        prod = do * o_att
        parts = []
        for h in range(ATT_HEADS):
            sl = slice(h * HEAD_DIM, (h + 1) * HEAD_DIM)
            tot = jnp.sum(prod[:, sl], axis=-1, keepdims=True)
            parts.append(jnp.broadcast_to(tot, (tr, HEAD_DIM)))
        dd = jnp.concatenate(parts, axis=1)
        for a, d_ref, t_ref in zip(alphas, (d0, d1, d2), (t0, t1, t2)):
            d_ref[...] = a * do
            t_ref[...] = -a * dd

    spec = pl.BlockSpec((tr, ATT_OUT), lambda i: (i, 0))
    res, cres = _pcall(
        kern, name=name, grid=(SEQ // tr,), in_specs=[spec] * 7, out_specs=[spec] * 6,
        out_shape=[jax.ShapeDtypeStruct((SEQ, ATT_OUT), F32)] * 6, args=(*outs, *lses, do_att),
        semantics=("parallel",), carried=carried)
    return (res[:3], res[3:]), cres


def _attn_bwd(proj, grp, lse, do_g, dl_g, name, carried=()):
    _, dil = ATT_GROUPS[grp]
    nb = SEQ // dil // ATT_BLOCK

    def kern(q_ref, k_ref, v_ref, do_ref, lse_ref, dl_ref, dq_ref, dk_ref, dv_ref, dq_acc, dk_acc, dv_acc):
        dk_acc[...] = jnp.zeros_like(dk_acc)
        dv_acc[...] = jnp.zeros_like(dv_acc)
        for r in range(dil):
            for n in range(nb):
                rows = _residue_rows(dil, r, n)
                q, do = q_ref[rows, :], do_ref[rows, :]
                kk, vv = _band_keys(k_ref, dil, r, n), _band_keys(v_ref, dil, r, n)
                s = _dot_nt(q, kk) * ATT_SCALE
                p = jnp.where(_band_mask(n > 0), jnp.exp(s - lse_ref[rows, :][:, :1]), 0.0)
                ds = p * (_dot_nt(do, vv) + dl_ref[rows, :][:, :1])
                dq_acc[rows, :] = _dot(ds, kk) * ATT_SCALE
                dk = _dot_tn(ds, q) * ATT_SCALE
                dv = _dot_tn(p, do)
                if n > 0:
                    prev = _residue_rows(dil, r, n - 1)
                    dk_acc[prev, :] += dk[:ATT_BLOCK]
                    dv_acc[prev, :] += dv[:ATT_BLOCK]
                    dk, dv = dk[ATT_BLOCK:], dv[ATT_BLOCK:]
                dk_acc[rows, :] += dk
                dv_acc[rows, :] += dv
        dq_ref[...] = dq_acc[...].astype(dq_ref.dtype)
        dk_ref[...] = dk_acc[...].astype(dk_ref.dtype)
        dv_ref[...] = dv_acc[...].astype(dv_ref.dtype)

    spec = pl.BlockSpec((SEQ, HEAD_DIM), lambda h: (0, h))
    return _pcall(
        kern, name=name, grid=(ATT_HEADS,),
        in_specs=[_attn_col_spec(Q_BLOCK0, grp), _attn_col_spec(K_BLOCK0, grp), _attn_col_spec(V_BLOCK0, grp),
                  spec, spec, spec],
        out_specs=[spec] * 3,
        out_shape=[jax.ShapeDtypeStruct((SEQ, ATT_OUT), BF16)] * 3, args=(proj, proj, proj, do_g, lse, dl_g),
        scratch_shapes=[pltpu.VMEM((SEQ, HEAD_DIM), F32)] * 3,
        semantics=("parallel",), carried=carried)


HG_HEADS_PER_STEP = 8
HG_BLOCK_W = 4 * HEAD_DIM
HG_BLOCKS = HG_HEADS_PER_STEP * HEAD_DIM // HG_BLOCK_W
HG_STEP_W = HG_HEADS_PER_STEP * HEAD_DIM
HG_Q_BLK = (3 * ATT_WIDTH) // HG_BLOCK_W
HG_N_CHUNKS = SEQ // HG_CHUNK
HG_MID = HG_CHUNK // 2


def _lower_bound(lb_ref, sl):
    l0, l1 = lb_ref[0:1, sl], lb_ref[1:2, sl]
    mx = jnp.maximum(l0, l1)
    e0, e1 = jnp.exp(l0 - mx), jnp.exp(l1 - mx)
    return e0 / (e0 + e1)


def _tri(lower):
    i = lax.broadcasted_iota(jnp.int32, (HG_CHUNK, HG_CHUNK), 0)
    j = lax.broadcasted_iota(jnp.int32, (HG_CHUNK, HG_CHUNK), 1)
    return (i >= j) if lower else (i <= j)


def _head_mean(x):
    parts = []
    for hd in range(x.shape[1] // HEAD_DIM):
        m = jnp.mean(x[:, hd * HEAD_DIM:(hd + 1) * HEAD_DIM], axis=-1, keepdims=True)
        parts.append(jnp.broadcast_to(m, (x.shape[0], HEAD_DIM)))
    return jnp.concatenate(parts, axis=1)


def _hg_chunk_terms(qh, fh, lb):
    sig = _sigmoid(fh)
    f = lb + (1.0 - lb) * sig
    k = 1.0 - f
    b = _dot_exact(_tri(True).astype(F32), jnp.log(f))
    bl = b[HG_CHUNK - 1:HG_CHUNK, :]
    br = b[HG_MID:HG_MID + 1, :]
    sq = _sigmoid(qh)
    q = qh * sq
    return dict(sig=sig, f=f, k=k, b=b, bl=bl, br=br, sq=sq, q=q,
                e1=jnp.exp(bl - b), e2=jnp.exp(b), e3=jnp.exp(b - br), e4=jnp.exp(br - b))


def _hg_fwd(proj, lbw, normw, name, carried=()):
    def in_blks(off):
        return [pl.BlockSpec((HG_CHUNK, HG_BLOCK_W), lambda hp, n, b=b: (n, HG_Q_BLK + off + hp * HG_BLOCKS + b))
                for b in range(HG_BLOCKS)]

    def kern(*refs):
        q_refs, f_refs, i_refs, g_refs = (refs[k * HG_BLOCKS:(k + 1) * HG_BLOCKS] for k in range(4))
        lb_ref, nw_ref, oraw_ref, ohg_ref, st_ref, state = refs[4 * HG_BLOCKS:]

        @pl.when(pl.program_id(1) == 0)
        def _():
            state[...] = jnp.zeros_like(state)

        causal = _tri(True)
        wide = lambda rs: jnp.concatenate([r[...] for r in rs], axis=1)
        t = _hg_chunk_terms(wide(q_refs), wide(f_refs), _lower_bound(lb_ref, slice(None)))
        v, gh = wide(i_refs), wide(g_refs)
        kd, qb, qr, kr = t["k"] * t["e1"], t["q"] * t["e2"], t["q"] * t["e3"], t["k"] * t["e4"]
        decay = jnp.exp(t["bl"])
        outs = []
        for hd in range(HG_HEADS_PER_STEP):
            sl = slice(hd * HEAD_DIM, (hd + 1) * HEAD_DIM)
            st = state[hd]
            st_ref[0, hd] = st
            a = jnp.where(causal, _dot_nt(qr[:, sl], kr[:, sl]), 0.0)
            outs.append(_dot_nt(qb[:, sl], st) + _dot(a, v[:, sl]))
            state[hd] = st * decay[:, sl] + _dot_tn(v[:, sl], kd[:, sl])
        o = jnp.concatenate(outs, axis=1)
        oraw_ref[...] = o
        r = lax.rsqrt(_head_mean(o * o) + RMS_EPS)
        nw = jnp.tile(nw_ref[...], (1, HG_HEADS_PER_STEP))
        ohg_ref[...] = (o * r * nw * (gh * _sigmoid(gh))).astype(ohg_ref.dtype)

    out_blk = pl.BlockSpec((HG_CHUNK, HG_STEP_W), lambda hp, n: (n, hp))
    return _pcall(
        kern, name=name, grid=(HG_HEADS // HG_HEADS_PER_STEP, HG_N_CHUNKS),
        in_specs=[*in_blks(0), *in_blks(2), *in_blks(4), *in_blks(6),
                  pl.BlockSpec((2, HG_STEP_W), lambda hp, n: (0, hp)),
                  pl.BlockSpec((1, HEAD_DIM), lambda hp, n: (0, 0))],
        out_specs=[out_blk, out_blk,
                   pl.BlockSpec((1, HG_HEADS_PER_STEP, HEAD_DIM, HEAD_DIM), lambda hp, n: (n, hp, 0, 0))],
        out_shape=[jax.ShapeDtypeStruct((SEQ, HG_WIDTH), F32), jax.ShapeDtypeStruct((SEQ, HG_WIDTH), BF16),
                   jax.ShapeDtypeStruct((HG_N_CHUNKS, HG_HEADS, HEAD_DIM, HEAD_DIM), F32)],
        args=(*[proj] * (4 * HG_BLOCKS), lbw, normw),
        scratch_shapes=[pltpu.VMEM((HG_HEADS_PER_STEP, HEAD_DIM, HEAD_DIM), F32)],
        semantics=("parallel", "arbitrary"), carried=carried)


def _hg_bwd(proj, lbw, normw, oraw, states, do_hg, name, carried=()):
    last = HG_N_CHUNKS - 1

    def in_blks(off):
        return [pl.BlockSpec((HG_CHUNK, HG_BLOCK_W),
                             lambda hp, n, b=b: (last - n, HG_Q_BLK + off + hp * HG_BLOCKS + b))
                for b in range(HG_BLOCKS)]

    blk = pl.BlockSpec((HG_CHUNK, HG_STEP_W), lambda hp, n: (last - n, hp))

    def kern(*refs):
        q_refs, f_refs, i_refs, g_refs = (refs[k * HG_BLOCKS:(k + 1) * HG_BLOCKS] for k in range(4))
        (lb_ref, nw_ref, oraw_ref, st_ref, do_ref, dq_ref, df_ref, di_ref, dg_ref, dlb_ref, dnw_ref,
         dstate) = refs[4 * HG_BLOCKS:]
        first = pl.program_id(1) == 0

        @pl.when(first)
        def _():
            dstate[...] = jnp.zeros_like(dstate)

        causal = _tri(True)
        wide = lambda rs: jnp.concatenate([r[...] for r in rs], axis=1)
        cat = lambda parts: jnp.concatenate(parts, axis=1)
        qh, fh, v, gh = wide(q_refs), wide(f_refs), wide(i_refs), wide(g_refs)
        o, dout = oraw_ref[...], do_ref[...]
        nw = jnp.tile(nw_ref[...], (1, HG_HEADS_PER_STEP))
        sgg = _sigmoid(gh)
        r = lax.rsqrt(_head_mean(o * o) + RMS_EPS)
        xn = o * r
        dg_ref[...] = (dout * xn * nw * (sgg * (1.0 + gh * (1.0 - sgg)))).astype(dg_ref.dtype)
        don = dout * (gh * sgg)
        dnw_wide = jnp.sum(don * xn, axis=0, keepdims=True)
        dnw_tot = dnw_wide[:, :HEAD_DIM]
        for hd in range(1, HG_HEADS_PER_STEP):
            dnw_tot = dnw_tot + dnw_wide[:, hd * HEAD_DIM:(hd + 1) * HEAD_DIM]
        tt = don * nw
        do = r * (tt - xn * _head_mean(tt * xn))
        lb = _lower_bound(lb_ref, slice(None))
        t = _hg_chunk_terms(qh, fh, lb)
        k, q = t["k"], t["q"]
        kd, qb, qr, kr = k * t["e1"], q * t["e2"], q * t["e3"], k * t["e4"]
        decay = jnp.exp(t["bl"])
        dqb, dqr, dkr, dkd, dv, ddecay = [], [], [], [], [], []
        for hd in range(HG_HEADS_PER_STEP):
            sl = slice(hd * HEAD_DIM, (hd + 1) * HEAD_DIM)
            st = st_ref[0, hd]
            dstn = dstate[hd]
            a = jnp.where(causal, _dot_nt(qr[:, sl], kr[:, sl]), 0.0)
            da = jnp.where(causal, _dot_nt(do[:, sl], v[:, sl]), 0.0)
            dqb.append(_dot(do[:, sl], st))
            dv.append(_dot_tn(a, do[:, sl]) + _dot_nt(kd[:, sl], dstn))
            dqr.append(_dot(da, kr[:, sl]))
            dkr.append(_dot_tn(da, qr[:, sl]))
            dkd.append(_dot(v[:, sl], dstn))
            ddecay.append(jnp.sum(dstn * st, axis=0, keepdims=True))
            dstate[hd] = dstn * decay[:, sl] + _dot_tn(do[:, sl], qb[:, sl])
        dqb, dqr, dkr, dkd, dv, ddecay = cat(dqb), cat(dqr), cat(dkr), cat(dkd), cat(dv), cat(ddecay)
        dq = dqb * t["e2"] + dqr * t["e3"]
        dk = dkd * t["e1"] + dkr * t["e4"]
        db = dqb * qb + dqr * qr - dkr * kr - dkd * kd
        dbl = jnp.sum(dkd * kd, axis=0, keepdims=True) + ddecay * decay
        dbr = jnp.sum(dkr * kr - dqr * qr, axis=0, keepdims=True)
        rows = lax.broadcasted_iota(jnp.int32, db.shape, 0)
        dlf = _dot_exact(_tri(False).astype(F32), db) + dbl + jnp.where(rows <= HG_MID, dbr, 0.0)
        df = dlf / t["f"] - dk
        sig, sq = t["sig"], t["sq"]
        df_ref[...] = (df * (1.0 - lb) * sig * (1.0 - sig)).astype(df_ref.dtype)
        dlb_row = jnp.sum(df * (1.0 - sig), axis=0, keepdims=True)
        dq_ref[...] = (dq * (sq * (1.0 + qh * (1.0 - sq)))).astype(dq_ref.dtype)
        di_ref[...] = dv.astype(di_ref.dtype)
        dnw_blk = jnp.broadcast_to(dnw_tot, (8, HEAD_DIM))

        @pl.when(first)
        def _():
            dlb_ref[...] = dlb_row
            dnw_ref[...] = dnw_blk

        @pl.when(jnp.logical_not(first))
        def _():
            dlb_ref[...] += dlb_row
            dnw_ref[...] += dnw_blk

    n_hp = HG_HEADS // HG_HEADS_PER_STEP
    outs, cres = _pcall(
        kern, name=name, grid=(n_hp, HG_N_CHUNKS),
        in_specs=[*in_blks(0), *in_blks(2), *in_blks(4), *in_blks(6),
                  pl.BlockSpec((2, HG_STEP_W), lambda hp, n: (0, hp)),
                  pl.BlockSpec((1, HEAD_DIM), lambda hp, n: (0, 0)),
                  blk,
                  pl.BlockSpec((1, HG_HEADS_PER_STEP, HEAD_DIM, HEAD_DIM), lambda hp, n: (last - n, hp, 0, 0)),
                  blk],
        out_specs=[blk, blk, blk, blk,
                   pl.BlockSpec((1, HG_STEP_W), lambda hp, n: (0, hp)),
                   pl.BlockSpec((8, HEAD_DIM), lambda hp, n: (hp, 0))],
        out_shape=[jax.ShapeDtypeStruct((SEQ, HG_WIDTH), BF16)] * 4
        + [jax.ShapeDtypeStruct((1, HG_WIDTH), F32), jax.ShapeDtypeStruct((8 * n_hp, HEAD_DIM), F32)],
        args=(*[proj] * (4 * HG_BLOCKS), lbw, normw, oraw, states, do_hg),
        scratch_shapes=[pltpu.VMEM((HG_HEADS_PER_STEP, HEAD_DIM, HEAD_DIM), F32)],
        semantics=("parallel", "arbitrary"), carried=carried)
    dqh, dfh, dih, dgh, dlb, dnw = outs
    return (dqh, dfh, dih, dgh, dlb, [dnw[8 * i:8 * i + 1] for i in range(n_hp)]), cres


GATE_BLOCK_W = 512
GATE_A_BLK = (3 * ATT_WIDTH + 4 * HG_WIDTH) // GATE_BLOCK_W
GATE_B_BLK = GATE_A_BLK + D_MODEL // GATE_BLOCK_W


GATE_TILE = (1024, GATE_BLOCK_W)


def _gate_ins(proj, ya, yb=None):
    ins = [(proj, GATE_TILE, lambda i, j: (i, GATE_A_BLK + j)), (proj, GATE_TILE, lambda i, j: (i, GATE_B_BLK + j)),
           (ya, GATE_TILE)]
    return ins + ([(yb, GATE_TILE)] if yb is not None else [])


def _branch_b_gate(o_hg, w_b, proj, ya, name, carried=()):
    def post(yb, ga, gb, ya_):
        return yb, _sigmoid(ga) * ya_ + _sigmoid(gb) * yb

    return _mm(o_hg, w_b, mode="nn", out_dtype=F32, name=name, carried=carried, fused=dict(
        tile=GATE_TILE, ins=_gate_ins(proj, ya), post=post,
        outs=[(jax.ShapeDtypeStruct((SEQ, D_MODEL), F32), GATE_TILE), (jax.ShapeDtypeStruct((SEQ, D_MODEL), BF16), GATE_TILE)]))


def _dmerged_gate_bwd(dx1, w_out, proj, ya, yb, name, carried=()):
    def post(dm, ga, gb, ya_, yb_):
        sa, sb = _sigmoid(ga), _sigmoid(gb)
        return dm * sa, dm * sb, dm * ya_ * sa * (1.0 - sa), dm * yb_ * sb * (1.0 - sb)

    return _mm(dx1, w_out, mode="nt", out_dtype=BF16, name=name, carried=carried, fused=dict(
        tile=GATE_TILE, ins=_gate_ins(proj, ya, yb), post=post,
        outs=[(jax.ShapeDtypeStruct((SEQ, D_MODEL), BF16), GATE_TILE)] * 4))


NORM_TILE = (512, D_MODEL)


def _column_halves(v, core_row):
    half = v.shape[1] // 2
    south = core_row[:, :half] < 0.5
    lo, hi = v[:, :half], v[:, half:]
    return jnp.where(south, hi, lo), jnp.where(south, lo, hi)


def _residual_norm(a, w, x, g, name, carried=(), core_row=None):
    vec = ((1, D_MODEL), lambda i, j: (0, j))
    half_tile = (NORM_TILE[0], D_MODEL // 2)

    def post(part, x_, g_, *core):
        xn = part + x_
        r = lax.rsqrt(jnp.mean(xn * xn, axis=-1, keepdims=True) + RMS_EPS)
        hn = (xn * r * g_).astype(BF16)
        return (xn, hn) + (_column_halves(hn, core[0]) if core else ())

    wide = [(jax.ShapeDtypeStruct((SEQ, D_MODEL), F32), NORM_TILE), (jax.ShapeDtypeStruct((SEQ, D_MODEL), BF16), NORM_TILE)]
    halves = [(jax.ShapeDtypeStruct((SEQ, D_MODEL // 2), BF16), half_tile)] * 2
    return _mm(a, w, mode="nn", out_dtype=F32, name=name, carried=carried, fused=dict(
        tile=NORM_TILE, post=post, ins=[(x, NORM_TILE), (g, *vec)] + ([(core_row, *vec)] if core_row is not None else []),
        outs=wide + (halves if core_row is not None else [])))


def _norm_bwd_residual(dq, w, x, g, res, name, carried=()):
    def post(dh, x_, g_, res_):
        r = lax.rsqrt(jnp.mean(x_ * x_, axis=-1, keepdims=True) + RMS_EPS)
        xn = x_ * r
        dhg = dh * g_
        dx = r * (dhg - xn * jnp.mean(dhg * xn, axis=-1, keepdims=True)) + res_
        return dx, jnp.sum(dh * xn, axis=0, keepdims=True), dx

    vec = ((1, D_MODEL), lambda i, j: (0, j))
    return _mm(dq, w, mode="nt", out_dtype=F32, name=name, carried=carried, fused=dict(
        tile=NORM_TILE, ins=[(x, NORM_TILE), (g, *vec), (res, NORM_TILE)], post=post,
        outs=[(jax.ShapeDtypeStruct((SEQ, D_MODEL), F32), NORM_TILE), (jax.ShapeDtypeStruct((1, D_MODEL), F32), *vec),
              (jax.ShapeDtypeStruct((SEQ, D_MODEL), BF16), NORM_TILE)]))


FF_SHARD = D_FF // N_CHIPS


FF_TILE_ROWS = 512


def _swiglu_tile(ab):
    a, b = ab[:, :FF_SHARD], ab[:, FF_SHARD:]
    return a * _sigmoid(a) * b


def _swiglu_grad_tile(du, ab):
    a, b = ab[:, :FF_SHARD], ab[:, FF_SHARD:]
    sg = _sigmoid(a)
    return jnp.concatenate([du * b * (sg * (1.0 + a * (1.0 - sg))), du * (a * sg)], axis=1)


def _ff_up(hf, w13, name, carried=()):
    wide, narrow = (FF_TILE_ROWS, 2 * FF_SHARD), (FF_TILE_ROWS, FF_SHARD)
    return _mm(hf, w13, mode="nn", out_dtype=F32, name=name, carried=carried, fused=dict(
        tile=wide, ins=[],
        outs=[(jax.ShapeDtypeStruct((SEQ, 2 * D_FF), F32), wide), (jax.ShapeDtypeStruct((SEQ, D_FF), BF16), narrow)],
        post=lambda p: (p, _swiglu_tile(p))))


def _ff_down_bwd(dx3, w2, ab, name, carried=()):
    wide, narrow = (FF_TILE_ROWS, 2 * FF_SHARD), (FF_TILE_ROWS, FF_SHARD)
    out, res = _mm(dx3, w2, mode="nt", out_dtype=BF16, name=name, carried=carried, fused=dict(
        tile=narrow, ins=[(ab, wide)], outs=[(jax.ShapeDtypeStruct((SEQ, 2 * D_FF), BF16), wide)],
        post=lambda du, ab_: (_swiglu_grad_tile(du, ab_),)))
    return out[0], res


CROSS_ROWS = 512


def _cross_fwd(qc, kvc, name, carried=()):
    def kern(q_ref, k_ref, v_ref, o_ref):
        s = _dot_nt(q_ref[...], k_ref[...]) * ATT_SCALE
        m = jnp.max(s, axis=-1, keepdims=True)
        e = jnp.exp(s - m)
        p = e / jnp.sum(e, axis=-1, keepdims=True)
        o_ref[...] = _dot(p, v_ref[...]).astype(o_ref.dtype)

    qblk = pl.BlockSpec((CROSS_ROWS, HEAD_DIM), lambda h, i: (i, h))
    return _pcall(
        kern, name=name, grid=(CROSS_HEADS, SEQ // CROSS_ROWS),
        in_specs=[qblk, pl.BlockSpec((MEM_LEN, HEAD_DIM), lambda h, i: (0, h)),
                  pl.BlockSpec((MEM_LEN, HEAD_DIM), lambda h, i: (0, CROSS_HEADS + h))],
        out_specs=qblk, out_shape=jax.ShapeDtypeStruct((SEQ, CROSS_WIDTH), BF16), args=(qc, kvc, kvc),
        semantics=("parallel", "parallel"), carried=carried)


def _cross_bwd(qc, kvc, doc, name):
    def kern(q_ref, k_ref, v_ref, do_ref, dq_ref, dk_ref, dv_ref):
        q, k, v, do = q_ref[...], k_ref[...], v_ref[...], do_ref[...]
        s = _dot_nt(q, k) * ATT_SCALE
        m = jnp.max(s, axis=-1, keepdims=True)
        e = jnp.exp(s - m)
        p = e / jnp.sum(e, axis=-1, keepdims=True)
        dp = _dot_nt(do, v)
        ds = p * (dp - jnp.sum(dp * p, axis=-1, keepdims=True))
        dq_ref[...] = (_dot(ds, k) * ATT_SCALE).astype(dq_ref.dtype)
        dk = _dot_tn(ds, q) * ATT_SCALE
        dv = _dot_tn(p, do)

        @pl.when(pl.program_id(1) == 0)
        def _():
            dk_ref[...] = dk
            dv_ref[...] = dv

        @pl.when(pl.program_id(1) > 0)
        def _():
            dk_ref[...] += dk
            dv_ref[...] += dv

    qblk = pl.BlockSpec((CROSS_ROWS, HEAD_DIM), lambda h, i: (i, h))
    kblk = pl.BlockSpec((MEM_LEN, HEAD_DIM), lambda h, i: (0, h))
    dq, dk, dv = pl.pallas_call(
        kern, name=name, grid=(CROSS_HEADS, SEQ // CROSS_ROWS),
        in_specs=[qblk, kblk, pl.BlockSpec((MEM_LEN, HEAD_DIM), lambda h, i: (0, CROSS_HEADS + h)), qblk],
        out_specs=[qblk, kblk, kblk],
        out_shape=[jax.ShapeDtypeStruct((SEQ, CROSS_WIDTH), BF16),
                   jax.ShapeDtypeStruct((MEM_LEN, CROSS_WIDTH), F32),
                   jax.ShapeDtypeStruct((MEM_LEN, CROSS_WIDTH), F32)],
        compiler_params=_cparams(("parallel", "arbitrary")),
    )(qc, kvc, kvc, doc)
    return dq, jnp.concatenate([dk, dv], axis=1)


FULL_SPECS = {
    "w_in": ("col", D_MODEL, IN_WIDTH),
    "w_branch_a": ("col", ATT_OUT, D_MODEL),
    "w_branch_b": ("col", HG_WIDTH, D_MODEL),
    "w_out": ("row", D_MODEL, D_MODEL),
    "wq_cross": ("row", D_MODEL, CROSS_WIDTH),
    "wkv_cross": ("row", D_MODEL, 2 * CROSS_WIDTH),
    "wo_cross": ("col", CROSS_WIDTH, D_MODEL),
    "w13": ("col", D_MODEL, 2 * D_FF),
    "w2": ("row", D_FF, D_MODEL),
}
WEIGHT_PLACE = {
    "w_in": ("w_in", 0), "w_branch_a": ("w_branch_a", 0), "w_branch_b": ("w_branch_b", 0),
    "w_out": ("w_out", 0), "wq_cross": ("wq_cross", 0), "wkv_cross": ("wkv_cross", 0),
    "wo_cross": ("wo_cross", 0), "w1": ("w13", 0), "w3": ("w13", FF_SHARD), "w2": ("w2", 0),
}
BIG_WEIGHTS = tuple(WEIGHT_PLACE)
EW_BLOCK_ELEMS = 512 * 1024


def _position():
    return lax.axis_index("x"), lax.axis_index("y"), lax.axis_index("c")


def _other_chips(x, y):
    return [(1 - x, y), (x, 1 - y), (1 - x, 1 - y)]


def _half(ref, kind, h):
    r, c = ref.shape
    if kind == "col":
        return ref.at[pl.ds(h * (r // 2), r // 2), :]
    return ref.at[:, pl.ds(h * (c // 2), c // 2)]


def _shard_of(ref, kind, start, size):
    return ref.at[:, pl.ds(start, size)] if kind == "col" else ref.at[pl.ds(start, size), :]


def _rows_of(ref, r0, nrows):
    return ref if nrows is None else ref.at[pl.ds(r0, nrows), :]


def _half_shape(kind, rows, cols):
    return (rows // 2, cols) if kind == "col" else (rows, cols // 2)


def _slot_shape(spec):
    kind, rows, cols = spec
    hr, hc = _half_shape(kind, rows, cols)
    return (hr, hc // N_CHIPS) if kind == "col" else (hr // N_CHIPS, hc)


def _remote(src, dst, send_sem, recv_sem, device):
    return pltpu.make_async_remote_copy(src_ref=src, dst_ref=dst, send_sem=send_sem, recv_sem=recv_sem,
                                        device_id=device, device_id_type=MESH)


def _gather_ici_comm(fulls, jobs, specs):
    def piece(refs, job, chip, c):
        f, r0, nr = job
        kind, rows, cols = specs[f]
        stride = (cols if kind == "col" else rows) // N_CHIPS
        return _rows_of(_half(_shard_of(refs[f], kind, chip * stride, stride), kind, c), r0, nr)

    def start(refs, ss, rs):
        x, y, c = _position()
        j = 2 * x + y
        for q, job in enumerate(jobs):
            for p, (px, py) in enumerate(_other_chips(x, y)):
                _remote(piece(refs, job, j, c), piece(refs, job, j, c), ss.at[3 * q + p], rs.at[3 * q + p],
                        (px, py, c)).start()

    def finish(refs, ss, rs):
        x, y, c = _position()
        j = 2 * x + y
        for q, job in enumerate(jobs):
            for p, (px, py) in enumerate(_other_chips(x, y)):
                _remote(piece(refs, job, j, c), piece(refs, job, 2 * px + py, c), ss.at[3 * q + p],
                        rs.at[3 * q + p], (px, py, c)).wait_recv()
        for q, job in enumerate(jobs):
            for p, (px, py) in enumerate(_other_chips(x, y)):
                _remote(piece(refs, job, j, c), piece(refs, job, j, c), ss.at[3 * q + p], rs.at[3 * q + p],
                        (px, py, c)).wait_send()

    names = list(dict.fromkeys(job[0] for job in jobs))
    return _Carried({f: fulls[f] for f in names}, {}, 3 * len(jobs), start, finish)


def _gather_ring_comm(fulls, f, r0, nr, phase, specs):
    kind, _, cols = specs[f]
    assert kind == "col" and nr % 32 == 0
    stride = cols // N_CHIPS
    half = nr // 2

    def rows(refs, chip, c, lo, n):
        return _rows_of(_half(_shard_of(refs[f], kind, chip * stride, stride), kind, c), r0 + lo, n)

    def copies(refs, ss, rs):
        x, y, c = _position()
        me, nx, ny, dg = 2 * x + y, 2 * (1 - x) + y, 2 * x + (1 - y), 2 * (1 - x) + (1 - y)
        to_x, to_y = (1 - x, y, c), (x, 1 - y, c)
        if phase == "a":
            mine = rows(refs, me, c, 0, nr)
            return [(_remote(mine, mine, ss.at[0], rs.at[0], to_x), rows(refs, nx, c, 0, nr)),
                    (_remote(mine, mine, ss.at[1], rs.at[1], to_y), rows(refs, ny, c, 0, nr))]
        up, low = rows(refs, ny, c, half, half), rows(refs, nx, c, 0, half)
        return [(_remote(up, up, ss.at[0], rs.at[0], to_x), rows(refs, dg, c, half, half)),
                (_remote(low, low, ss.at[1], rs.at[1], to_y), rows(refs, dg, c, 0, half))]

    def start(refs, ss, rs):
        for cp, _ in copies(refs, ss, rs):
            cp.start()

    def finish(refs, ss, rs):
        x, y, c = _position()
        mine = copies(refs, ss, rs)
        for i, (_, landing) in enumerate(mine):
            _remote(landing, landing, ss.at[i], rs.at[i], (x, y, c)).wait_recv()
        for cp, _ in mine:
            cp.wait_send()

    return _Carried({f: fulls[f]}, {}, 2, start, finish)


def _gather_d2d_comm(fulls, jobs, specs):
    def rect(refs, job, h):
        f, r0, nr = job
        assert nr is None or specs[f][0] == "col"
        return _rows_of(_half(refs[f], specs[f][0], h), r0, nr)

    def start(refs, ss, rs):
        x, y, c = _position()
        for q, job in enumerate(jobs):
            _remote(rect(refs, job, c), rect(refs, job, c), ss.at[q], rs.at[q], (x, y, 1 - c)).start()

    def finish(refs, ss, rs):
        x, y, c = _position()
        for q, job in enumerate(jobs):
            _remote(rect(refs, job, 1 - c), rect(refs, job, 1 - c), ss.at[q], rs.at[q], (x, y, 1 - c)).wait_recv()
        for q, job in enumerate(jobs):
            _remote(rect(refs, job, c), rect(refs, job, c), ss.at[q], rs.at[q], (x, y, 1 - c)).wait_send()

    names = list(dict.fromkeys(job[0] for job in jobs))
    return _Carried({f: fulls[f] for f in names}, {}, len(jobs), start, finish)


def _pairx_comm(grads, names, specs, whole=False, recv=None):
    def copies(refs, ss, rs):
        x, y, c = _position()
        src = (lambda f: refs[("g", f)]) if whole else (lambda f: _half(refs[("g", f)], specs[f][0], 1 - c))
        return [_remote(src(f), refs[("r", f)], ss.at[i], rs.at[i], (x, y, 1 - c)) for i, f in enumerate(names)]

    def start(refs, ss, rs):
        for cp in copies(refs, ss, rs):
            cp.start()

    def finish(refs, ss, rs):
        for cp in copies(refs, ss, rs):
            cp.wait_recv()
        for cp in copies(refs, ss, rs):
            cp.wait_send()

    reads = {("g", f): grads[f] for f in names}
    if recv is not None:
        return _Carried({("r", f): recv[f] for f in names}, {}, len(names), start, finish, reads=reads)
    fresh = {("r", f): jax.ShapeDtypeStruct(_half_shape(*specs[f]), BF16) for f in names}
    return _Carried({}, fresh, len(names), start, finish, reads=reads)


def _chipx_comm(pair_sums, slots, jobs, specs):
    def copies(refs, ss, rs):
        x, y, c = _position()
        out = []
        for q, (f, r0, nr) in enumerate(jobs):
            kind = specs[f][0]
            width = _slot_shape(specs[f])[1 if kind == "col" else 0]
            for p, (px, py) in enumerate(_other_chips(x, y)):
                src = _rows_of(_shard_of(refs[("p", f)], kind, (2 * px + py) * width, width), r0, nr)
                dst = _rows_of(refs[("s", f)].at[p], r0, nr)
                out.append(_remote(src, dst, ss.at[3 * q + p], rs.at[3 * q + p], (px, py, c)))
        return out

    def start(refs, ss, rs):
        for cp in copies(refs, ss, rs):
            cp.start()

    def finish(refs, ss, rs):
        for cp in copies(refs, ss, rs):
            cp.wait_recv()
        for cp in copies(refs, ss, rs):
            cp.wait_send()

    names = list(dict.fromkeys(job[0] for job in jobs))
    arrays = {("p", f): pair_sums[f] for f in names}
    arrays.update({("s", f): slots[f] for f in names})
    return _Carried(arrays, {}, 3 * len(jobs), start, finish)


def _share_comm(grads, wnames, specs, place):
    def start(refs, ss, rs):
        x, y, c = _position()
        for i, w in enumerate(wnames):
            kind = specs[place[w][0]][0]
            _remote(_half(refs[w], kind, c), _half(refs[w], kind, c), ss.at[i], rs.at[i], (x, y, 1 - c)).start()

    def finish(refs, ss, rs):
        x, y, c = _position()
        for i, w in enumerate(wnames):
            kind = specs[place[w][0]][0]
            _remote(_half(refs[w], kind, 1 - c), _half(refs[w], kind, 1 - c), ss.at[i], rs.at[i],
                    (x, y, 1 - c)).wait_recv()
        for i, w in enumerate(wnames):
            kind = specs[place[w][0]][0]
            _remote(_half(refs[w], kind, c), _half(refs[w], kind, c), ss.at[i], rs.at[i], (x, y, 1 - c)).wait_send()

    return _Carried({w: grads[w] for w in wnames}, {}, len(wnames), start, finish)


SMALL_IN, SMALL_ROWS_OF_ALL = ("small", "mine"), ("small", "all")


def _small_gather_comm(v):
    flips = [(fx, fy, fc) for fx in (0, 1) for fy in (0, 1) for fc in (0, 1)][1:]

    def copies(refs, ss, rs):
        x, y, c = _position()
        v_ref, out_ref = refs[SMALL_IN], refs[SMALL_ROWS_OF_ALL]
        me = 4 * x + 2 * y + c
        pairs = []
        for i, fl in enumerate(flips):
            px, py, pc = (1 - a if f else a for a, f in zip((x, y, c), fl))
            pairs.append((_remote(v_ref, out_ref.at[me], ss.at[i], rs.at[i], (px, py, pc)),
                          _remote(v_ref, out_ref.at[4 * px + 2 * py + pc], ss.at[i], rs.at[i], (px, py, pc))))
        return pairs, pltpu.make_async_copy(v_ref, out_ref.at[me], ss.at[N_DEV - 1])

    def start(refs, ss, rs):
        pairs, local = copies(refs, ss, rs)
        local.start()
        for mine, _ in pairs:
            mine.start()

    def finish(refs, ss, rs):
        pairs, local = copies(refs, ss, rs)
        for _, theirs in pairs:
            theirs.wait_recv()
        for mine, _ in pairs:
            mine.wait_send()
        local.wait()

    fresh = {SMALL_ROWS_OF_ALL: jax.ShapeDtypeStruct((N_DEV,) + v.shape, F32)}
    return _Carried({}, fresh, N_DEV, start, finish, reads={SMALL_IN: v})


def _ew_block(rows, cols, elems=EW_BLOCK_ELEMS):
    tc = cols if cols <= 4096 else _div(cols, 2048, LANE)
    tr = _div(rows, max(16, elems // tc), 16)
    return tr, tc


def _mesh_scalars():
    x, y, c = _position()
    return jnp.stack([c, 2 * x + y]).astype(jnp.int32)


def _grid_spec(grid, in_specs, out_specs):
    return pltpu.PrefetchScalarGridSpec(num_scalar_prefetch=1, grid=grid, in_specs=in_specs, out_specs=out_specs)


def _cast_into_full(parts, fname, pos, specs, place, name, token=None):
    kind, rows, cols = specs[fname]
    ws = [w for w in place if place[w][0] == fname]
    if kind == "col":
        stride = cols // N_CHIPS
        hr = rows // 2
        tr = _div(hr, max(16, EW_BLOCK_ELEMS // stride), 16)
        nrb = hr // tr
        in_specs = [pl.BlockSpec((tr, parts[w].shape[1]), lambda i, pos_ref: (i + pos_ref[0] * nrb, 0)) for w in ws]
        out_spec = pl.BlockSpec((tr, stride), lambda i, pos_ref: (i + pos_ref[0] * nrb, pos_ref[1]))
    else:
        stride = rows // N_CHIPS
        hc = cols // 2
        tr = _div(stride, max(16, EW_BLOCK_ELEMS // hc), 16)
        nrb = stride // tr
        in_specs = [pl.BlockSpec((tr, hc), lambda i, pos_ref: (i, pos_ref[0])) for w in ws]
        out_spec = pl.BlockSpec((tr, hc), lambda i, pos_ref: (i + pos_ref[1] * nrb, pos_ref[0]))

    def kern(pos_ref, *refs):
        o_ref = refs[-1]
        for w, r in zip(ws, refs[:len(ws)]):
            off = place[w][1] if kind == "col" else 0
            o_ref[:, off:off + r.shape[1]] = r[...].astype(o_ref.dtype)

    tokens = [] if token is None else [token]
    in_specs = in_specs + [pl.BlockSpec(TOKEN_SHAPE, lambda i, pos_ref: (0, 0))] * len(tokens)
    return pl.pallas_call(
        kern, name=name, grid_spec=_grid_spec((nrb,), in_specs, out_spec),
        out_shape=jax.ShapeDtypeStruct((rows, cols), BF16),
        compiler_params=_cparams(("parallel",)),
    )(pos, *[parts[w] for w in ws], *tokens)


def _pair_sum(grad, recv, pos, spec, name):
    kind, rows, cols = spec
    hr, hc = _half_shape(kind, rows, cols)
    tr, tc = _ew_block(hr, hc, 2 * EW_BLOCK_ELEMS)
    nrb, ncb = hr // tr, hc // tc
    blk = pl.BlockSpec((tr, tc), lambda i, jj, pos_ref: (i, jj))
    if kind == "col":
        mine = pl.BlockSpec((tr, tc), lambda i, jj, pos_ref: (i + pos_ref[0] * nrb, jj))
    else:
        mine = pl.BlockSpec((tr, tc), lambda i, jj, pos_ref: (i, jj + pos_ref[0] * ncb))

    def kern(pos_ref, g_ref, r_ref, o_ref, slots_ref):
        o_ref[...] = (g_ref[...].astype(F32) + r_ref[...].astype(F32)).astype(o_ref.dtype)

    return pl.pallas_call(
        kern, name=name, grid_spec=_grid_spec((nrb, ncb), [mine, blk], [blk, ANY]),
        out_shape=[jax.ShapeDtypeStruct((hr, hc), BF16),
                   jax.ShapeDtypeStruct((N_CHIPS - 1,) + _slot_shape(spec), BF16)],
        compiler_params=_cparams(("parallel", "parallel")),
    )(pos, grad, recv)


def _chip_sum(pair_sum, slots, pos, fname, shard_shapes, specs, place, name):
    kind, rows, cols = specs[fname]
    sr, sc = _slot_shape(specs[fname])
    ws = [w for w in place if place[w][0] == fname]
    n_slots = N_CHIPS - 1
    tr = _div(sr, max(16, EW_BLOCK_ELEMS // sc), 16)
    nrb = sr // tr
    slot = pl.BlockSpec((n_slots, tr, sc), lambda i, pos_ref: (0, i, 0))
    if kind == "col":
        own = pl.BlockSpec((tr, sc), lambda i, pos_ref: (i, pos_ref[1]))
        out_specs = [pl.BlockSpec((tr, shard_shapes[w][1]), lambda i, pos_ref: (i + pos_ref[0] * nrb, 0)) for w in ws]
    else:
        own = pl.BlockSpec((tr, sc), lambda i, pos_ref: (i + pos_ref[1] * nrb, 0))
        out_specs = [pl.BlockSpec((tr, sc), lambda i, pos_ref: (i, pos_ref[0])) for w in ws]

    def kern(pos_ref, own_ref, slot_ref, *out_refs):
        tot = own_ref[...].astype(F32)
        for s in range(n_slots):
            tot = tot + slot_ref[s].astype(F32)
        for w, o_ref in zip(ws, out_refs):
            off = place[w][1] if kind == "col" else 0
            o_ref[...] = tot[:, off:off + o_ref.shape[1]]

    outs = pl.pallas_call(
        kern, name=name, grid_spec=_grid_spec((nrb,), [own, slot], out_specs),
        out_shape=[jax.ShapeDtypeStruct(shard_shapes[w], F32) for w in ws],
        compiler_params=_cparams(("parallel",)),
    )(pos, pair_sum, slots)
    return dict(zip(ws, outs))


def _adam_math(w, g, m, v):
    m2 = ADAM_B1 * m + (1.0 - ADAM_B1) * g
    v2 = ADAM_B2 * v + (1.0 - ADAM_B2) * (g * g)
    m_hat = m2 / (1.0 - ADAM_B1 ** ADAM_STEP)
    v_hat = v2 / (1.0 - ADAM_B2 ** ADAM_STEP)
    delta = -ADAM_LR * (m_hat / (jnp.sqrt(v_hat) + ADAM_EPS) + ADAM_WD * w)
    return delta, m2, v2


def _adamw(w, g, m, v, name, carried=()):
    rows, cols = w.shape
    tr, tc = _ew_block(rows, cols)
    assert tc == cols, (name, cols)
    steps = rows // tr

    def kern(w_hbm, g_hbm, m_hbm, v_hbm, d_ref, m2_ref, v2_ref, g_out_ref, buf, sem):
        s = pl.program_id(0)

        def copies(t):
            slot = t % ADAM_RING
            return [pltpu.make_async_copy(src.at[pl.ds(t * tr, tr), :], buf.at[k, slot], sem.at[k, slot])
                    for k, src in enumerate((w_hbm, g_hbm, m_hbm, v_hbm))]

        @pl.when(s == 0)
        def _():
            for t in range(min(ADAM_RING - 1, steps)):
                for cp in copies(t):
                    cp.start()

        @pl.when(s + (ADAM_RING - 1) < steps)
        def _():
            for cp in copies(s + (ADAM_RING - 1)):
                cp.start()

        for cp in copies(s):
            cp.wait()
        slot = s % ADAM_RING
        g_ = buf[1, slot]
        d_ref[...], m2_ref[...], v2_ref[...] = _adam_math(buf[0, slot], g_, buf[2, slot], buf[3, slot])
        g_out_ref[...] = g_

    blk = pl.BlockSpec((tr, cols), lambda i: (i, 0))
    return _pcall(
        kern, name=name, grid=(steps,), in_specs=[ANY] * 4, out_specs=[blk] * 4,
        out_shape=[jax.ShapeDtypeStruct((rows, cols), F32)] * 4, args=(w, g, m, v),
        scratch_shapes=[pltpu.VMEM((4, ADAM_RING, tr, cols), F32), pltpu.SemaphoreType.DMA((4, ADAM_RING))],
        semantics=("arbitrary",), carried=carried)


SMALL_ROWS = ("ln_mix_w", "ln_cross_w", "ln_mem_w", "ln_ffn_w", "ln_final_w")
ROW_HG_NORM, ROW_LB0, ROW_LB1 = 5, 6, 7
LOSS_LANE0 = HEAD_DIM


def _pack_small(vals):
    rows = [vals[n].reshape(1, D_MODEL) for n in SMALL_ROWS]
    pad = lambda a: jnp.pad(a, ((0, 0), (0, D_MODEL - a.shape[1])))
    rows.append(pad(vals["hg_norm_w"].reshape(1, HEAD_DIM)))
    rows.append(pad(vals["hg_lower_bounds"].reshape(2, HG_WIDTH)))
    return jnp.concatenate(rows, axis=0)


def _small_update(gathered, w, m, v, name="small_update"):
    def kern(g_ref, w_ref, m_ref, v_ref, grad_ref, d_ref, m2_ref, v2_ref, loss_ref):
        tot = g_ref[0]
        for s in range(1, N_DEV):
            tot = tot + g_ref[s]
        wv = w_ref[...]
        row = lax.broadcasted_iota(jnp.int32, (8, D_MODEL), 0)
        lane = lax.broadcasted_iota(jnp.int32, (8, D_MODEL), 1)
        l0, l1 = wv[ROW_LB0:ROW_LB0 + 1], wv[ROW_LB1:ROW_LB1 + 1]
        mx = jnp.maximum(l0, l1)
        e0, e1 = jnp.exp(l0 - mx), jnp.exp(l1 - mx)
        p0 = e0 / (e0 + e1)
        dlog = tot[ROW_LB0:ROW_LB0 + 1] * p0 * (1.0 - p0)
        tot = jnp.where(row == ROW_HG_NORM, tot + tot[ROW_LB1:ROW_LB1 + 1], tot)
        grad = jnp.where(row == ROW_LB0, dlog, jnp.where(row == ROW_LB1, -dlog, tot))
        grad = jnp.where((row == ROW_HG_NORM) & (lane >= HEAD_DIM), 0.0, grad)
        grad = jnp.where((row >= ROW_LB0) & (lane >= HG_WIDTH), 0.0, grad)
        grad_ref[...] = grad
        d_ref[...], m2_ref[...], v2_ref[...] = _adam_math(wv, grad, m_ref[...], v_ref[...])
        loss_ref[...] = tot[ROW_HG_NORM:ROW_HG_NORM + 1, LOSS_LANE0:LOSS_LANE0 + LANE]

    full = pl.BlockSpec((8, D_MODEL), lambda: (0, 0))
    return pl.pallas_call(
        kern, name=name,
        in_specs=[pl.BlockSpec((N_DEV, 8, D_MODEL), lambda: (0, 0, 0)), full, full, full],
        out_specs=[full, full, full, full, pl.BlockSpec((1, LANE), lambda: (0, 0))],
        out_shape=[jax.ShapeDtypeStruct((8, D_MODEL), F32)] * 4 + [jax.ShapeDtypeStruct((1, LANE), F32)],
        compiler_params=_cparams(),
    )(gathered, w, m, v)


def _unpack_small(p, shapes):
    out = {n: p[i].reshape(shapes[n]) for i, n in enumerate(SMALL_ROWS)}
    out["hg_norm_w"] = p[ROW_HG_NORM, :HEAD_DIM].reshape(shapes["hg_norm_w"])
    out["hg_lower_bounds"] = p[ROW_LB0:ROW_LB1 + 1, :HG_WIDTH].reshape(shapes["hg_lower_bounds"])
    return out


def _concat_cols(pieces, name):
    rows = pieces[0].shape[0]
    widths = [p.shape[1] for p in pieces]
    tr = ROW_BLOCK // 2

    def kern(*refs):
        o_ref, off = refs[-1], 0
        for r, w in zip(refs[:-1], widths):
            o_ref[:, off:off + w] = r[...]
            off += w

    return pl.pallas_call(
        kern, name=name, grid=(rows // tr,),
        in_specs=[pl.BlockSpec((tr, w), lambda i: (i, 0)) for w in widths],
        out_specs=pl.BlockSpec((tr, sum(widths)), lambda i: (i, 0)),
        out_shape=jax.ShapeDtypeStruct((rows, sum(widths)), pieces[0].dtype),
        compiler_params=_cparams(("parallel",)),
    )(*pieces)


WHOLE = lambda f: (f, 0, None)
MID_MATRICES = ("w_branch_a", "w_branch_b", "w_out", "wq_cross", "wkv_cross", "wo_cross")
MID_WEIGHTS = MID_MATRICES
W_IN_PIECES = [("w_in", r0, 512) for r0 in range(0, D_MODEL // 2, 512)]
W13_PIECES = [("w13", r0, 512) for r0 in range(0, D_MODEL // 2, 512)]
GATHER_GROUPS = [("mid", [WHOLE(f) for f in MID_MATRICES]), ("w13a", W13_PIECES[:1]), ("w13b", W13_PIECES[1:]),
                 ("w2", [WHOLE("w2")])]
OTHER_WEIGHTS = ["w1", "w3", "w2"] + list(MID_WEIGHTS)
BEFORE = {
    "hgrn_fwd": [("wait", "mid")],
    "mm_out": [("wait", "w13a")],
    "mm_o": [("wait", "w13b")],
    "mm_w2": [("wait", "w2"), ("run", ("d2d", [WHOLE("w2")]), "gather_hand_over_w2")],
    "mm_dwin_own": [("wait", "rs_w2"), ("chip_sum", "w2"), ("wait", "rs_w13"), ("chip_sum", "w13"),
                    ("wait", "pair_w_in")],
    "mm_dh": [("wait", "rs_mid")] + [("chip_sum", f) for f in MID_MATRICES],
}
CARRY = {
    "hgrn_fwd": [("d2d", [WHOLE(f) for f in MID_MATRICES])],
    "mm_out": [("d2d", W13_PIECES[:1])],
    "mm_o": [("d2d", W13_PIECES[1:])],
    "mm_du": [("pairx_whole", ["w2"])],
    "mm_dhf": [("pairx_whole", ["w13"])],
    "attn_bwd_g0": [("pairx", list(MID_MATRICES))],
    "mm_dh": [("share", OTHER_WEIGHTS)],
}
AFTER = {
    "mm_dw2_own": [("start", "rs_w2", [WHOLE("w2")])],
    "mm_dw13_own": [("start", "rs_w13", [WHOLE("w13")])],
    "attn_bwd_g0": [("pair_sum", f) for f in MID_MATRICES] + [("start", "rs_mid", [WHOLE(f) for f in MID_MATRICES])],
    "mm_dwin_sibling": [("start_pairx", "pair_w_in", ["w_in"])],
    "mm_dwin_own": [("start", "rs_w_in", [WHOLE("w_in")])],
}
FINISH = [
    ("adamw", OTHER_WEIGHTS), ("wait", "rs_w_in"), ("chip_sum", "w_in"), ("share_and_small", ["w_in"]),
    ("adamw", ["w_in"]),
]


class _Late:
    def __init__(self, read):
        self.read = read


class _Net:
    def __init__(self, full, pos=None, shard_shapes=None, comm=True, specs=FULL_SPECS, place=WEIGHT_PLACE):
        self.full, self.pos, self.shard_shapes, self.comm = dict(full), pos, shard_shapes, comm
        self.specs, self.place = specs, place
        self.gw, self.recv, self.psum, self.slots, self.grads = {}, {}, {}, {}, {}
        self.gw_sibling = {}
        self.pending, self.token, self.last = {}, None, None
        self.ahead = []

    def _make(self, kind, arg):
        if kind == "gather":
            return _gather_ici_comm(self.full, arg, self.specs)
        if kind == "ring":
            return _gather_ring_comm(self.full, *arg, self.specs)
        if kind == "d2d":
            return _gather_d2d_comm(self.full, arg, self.specs)
        if kind == "pairx":
            return _pairx_comm(self.gw, arg, self.specs)
        if kind == "pairx_whole":
            return _pairx_comm(self.gw_sibling, arg, self.specs, whole=True)
        if kind == "pairx_split":
            return _pairx_comm(self.gw_sibling, arg, self.specs, whole=True, recv=self.recv)
        if kind == "chipx":
            return _chipx_comm(self.psum, self.slots, arg, self.specs)
        assert kind == "share"
        return _share_comm(self.grads, arg, self.specs, self.place)

    def _store(self, kind, res):
        if kind in ("gather", "ring", "d2d"):
            self.full.update(res)
        elif kind in ("pairx", "pairx_whole", "pairx_split"):
            for (tag, f), a in res.items():
                (self.gw if tag == "g" else self.recv)[f] = a
        elif kind == "chipx":
            for (tag, f), a in res.items():
                (self.psum if tag == "p" else self.slots)[f] = a
        else:
            self.grads.update(res)

    def run_comm(self, item, name, extra=()):
        kind, arg = item
        res = _run_comm([self._make(kind, arg), *extra], name)
        self._store(kind, res[0])
        return res[1:]

    @staticmethod
    def _others(after, items):
        own = [a for cm in items for a in cm.arrays.values()]
        return [a for a in after if a is not None and all(a is not o for o in own)]

    def start(self, groups, kind, name):
        items = [self._make(kind, jobs) for _, jobs in groups]
        after = self._others([self.last], items)
        res, sems, token = _split_start(items, name, after=after[0] if after else None)
        for (group, jobs), r, s in zip(groups, res, sems):
            self._store(kind, r)
            self.pending[group] = (kind, jobs, s)
        self.token = self.last = token

    def wait(self, group, after=()):
        kind, jobs, sems = self.pending.pop(group)
        item = self._make(kind, jobs)
        res = _split_wait([item], [sems], self._others([self.last, *after], [item]), f"wait_{group}")[0]
        self._store(kind, res)

    def step(self, step):
        if step[0] == "wait":
            self.wait(step[1], after=self.ahead)
            self.ahead = []
        elif step[0] == "start":
            self.start([(step[1], step[2])], "chipx", f"start_{step[1]}")
        elif step[0] == "start_pairx":
            for f in step[2]:
                self.recv[f] = lax.empty(_half_shape(*self.specs[f]), BF16)
            self.start([(step[1], step[2])], "pairx_split", f"start_{step[1]}")
        elif step[0] == "pair_sum":
            f = step[1]
            self.psum[f], self.slots[f] = _pair_sum(self.gw[f], self.recv[f], self.pos, self.specs[f],
                                                    f"rs_pair_sum_{f}")
        elif step[0] == "chip_sum":
            f = step[1]
            sums = _chip_sum(self.psum[f], self.slots[f], self.pos, f, self.shard_shapes,
                             self.specs, self.place, f"rs_chip_sum_{f}")
            self.grads.update(sums)
            self.ahead += list(sums.values())
        else:
            assert step[0] == "run"
            self.run_comm(step[1], step[2])

    def call(self, fn, name, *args, grad_of=None, sibling_half=False, pair_sum_of=None, **kw):
        for step in (BEFORE.get(name, []) if self.comm else []):
            self.step(step)
        self.ahead = []
        late = lambda a: a.read() if isinstance(a, _Late) else a
        args = [late(a) for a in args]
        kw = {key: late(val) for key, val in kw.items()}
        items = CARRY.get(name, []) if self.comm else []
        carried = [self._make(k, a) for k, a in items]
        if self.token is not None:
            carried.append(_Token(self.token))
            self.token = None
        out, res = fn(*args, name=name, carried=carried, **kw)
        if grad_of is not None:
            (self.gw_sibling if sibling_half else self.gw)[grad_of] = out
        if pair_sum_of is not None:
            self.psum[pair_sum_of] = out
            self.slots[pair_sum_of] = lax.empty((N_CHIPS - 1,) + _slot_shape(self.specs[pair_sum_of]), BF16)
        self.last = jax.tree.leaves(out)[0]
        for (kind, _), r in zip(items, res):
            self._store(kind, r)
        for step in (AFTER.get(name, []) if self.comm else []):
            self.step(step)
        return out


def _local_step(net, x, h, mem, target, small, h_halves=None, core_row=None):
    full, call = net.full, net.call
    proj = call(_mm, "mm_proj", h, full["w_in"], mode="nn", out_dtype=F32)
    att = [call(_attn_fwd, f"attn_fwd_g{g}", proj, g) for g in range(3)]
    outs, lses = [a[0] for a in att], [a[1] for a in att]
    o_att = _attn_merge_fwd(outs, lses, "attn_merge")
    oraw, o_hg, states = call(_hg_fwd, "hgrn_fwd", proj, small["hg_lower_bounds"], small["hg_norm_w"])
    ya = call(_mm, "mm_branch_a", o_att, full["w_branch_a"], mode="nn", out_dtype=F32)
    yb, merged = call(_branch_b_gate, "mm_branch_b", o_hg, full["w_branch_b"], proj, ya)
    x1, hc = call(_residual_norm, "mm_out", merged, full["w_out"], x, small["ln_cross_w"])

    mn = _rms_fwd(mem, small["ln_mem_w"], "rms_mem")
    qc = call(_mm, "mm_q", hc, full["wq_cross"], mode="nn", out_dtype=F32)
    kvc = call(_mm, "mm_kv", mn, full["wkv_cross"], mode="nn", out_dtype=F32)
    oc = call(_cross_fwd, "cross_fwd", qc, kvc)
    x2, hf, *hf_halves = call(_residual_norm, "mm_o", oc, full["wo_cross"], x1, small["ln_ffn_w"], core_row=core_row)

    ab, u = call(_ff_up, "mm_w13", hf, full["w13"])
    x3 = call(_mm, "mm_w2", u, _Late(lambda: full["w2"]), mode="nn", out_dtype=F32, res=x2)

    dx3, dg_final, loss, dx3_bf16, *dx3_halves = _loss_head(x3, small["ln_final_w"], target, "loss_head", core_row)

    gs = {"ln_final_w": dg_final}
    if net.comm:
        call(_mm, "mm_dw2_sibling", u, dx3_halves[0], mode="tn", out_dtype=BF16, grad_of="w2", sibling_half=True)
        dab = call(_ff_down_bwd, "mm_du", dx3_bf16, full["w2"], ab)
        call(_mm, "mm_dw2_own", u, dx3_halves[1], mode="tn", out_dtype=BF16, res=_Late(lambda: net.recv["w2"]),
             pair_sum_of="w2")
        call(_mm, "mm_dw13_sibling", hf_halves[0], dab, mode="tn", out_dtype=BF16, grad_of="w13", sibling_half=True)
        dhf = call(_mm, "mm_dhf", dab, full["w13"], mode="nt", out_dtype=F32)
        call(_mm, "mm_dw13_own", hf_halves[1], dab, mode="tn", out_dtype=BF16, res=_Late(lambda: net.recv["w13"]),
             pair_sum_of="w13")
    else:
        call(_mm, "mm_dw2", u, dx3_bf16, mode="tn", out_dtype=BF16, grad_of="w2")
        dab = call(_ff_down_bwd, "mm_du", dx3_bf16, full["w2"], ab)
        call(_mm, "mm_dw13", hf, dab, mode="tn", out_dtype=BF16, grad_of="w13")
        dhf = call(_mm, "mm_dhf", dab, full["w13"], mode="nt", out_dtype=F32)
    dx2, gs["ln_ffn_w"], dx2_bf16 = _rms_bwd(x2, small["ln_ffn_w"], dhf, dx3, "rms_ffn_bwd", bf16_copy=True)
    doc = call(_mm, "mm_doc", dx2_bf16, full["wo_cross"], mode="nt", out_dtype=BF16)
    call(_mm, "mm_dwo", oc, dx2_bf16, mode="tn", out_dtype=BF16, grad_of="wo_cross")
    dqc, dkvc = _cross_bwd(qc, kvc, doc, "cross_bwd")
    call(_mm, "mm_dwq", hc, dqc, mode="tn", out_dtype=BF16, grad_of="wq_cross")
    dx1, gs["ln_cross_w"], dx1_bf16 = call(_norm_bwd_residual, "mm_dhc", dqc, full["wq_cross"], x1,
                                           small["ln_cross_w"], dx2)
    call(_mm, "mm_dwkv", mn, dkvc, mode="tn", out_dtype=BF16, grad_of="wkv_cross")
    dya, dyb, dga, dgb = call(_dmerged_gate_bwd, "mm_dmerged", dx1_bf16, full["w_out"], proj, ya, yb)
    call(_mm, "mm_dwout", merged, dx1_bf16, mode="tn", out_dtype=BF16, grad_of="w_out")
    call(_mm, "mm_dwa", o_att, dya, mode="tn", out_dtype=BF16, grad_of="w_branch_a")
    do_att = call(_mm, "mm_doatt", dya, full["w_branch_a"], mode="nt", out_dtype=F32)
    call(_mm, "mm_dwb", o_hg, dyb, mode="tn", out_dtype=BF16, grad_of="w_branch_b")
    do_hg = call(_mm, "mm_dohg", dyb, full["w_branch_b"], mode="nt", out_dtype=F32)
    dqh, dfh, dih, dgh, dlb, gs["hg_norm_w"] = call(
        _hg_bwd, "hgrn_bwd", proj, small["hg_lower_bounds"], small["hg_norm_w"], oraw, states, do_hg)
    gs["hg_lb"] = dlb
    do_gs, dl_gs = call(_attn_merge_bwd, "attn_merge_bwd", outs, lses, do_att)
    dqs, dks, dvs = zip(*[call(_attn_bwd, f"attn_bwd_g{g}", proj, g, lses[g], do_gs[g], dl_gs[g]) for g in range(3)])
    dproj = _concat_cols([*dqs, *dks, *dvs, dqh, dfh, dih, dgh, dga, dgb], "dproj_concat")
    if net.comm:
        h_sibling, h_own = h_halves
        call(_mm, "mm_dwin_sibling", h_sibling, dproj, mode="tn", out_dtype=BF16, grad_of="w_in", sibling_half=True)
    dmn = call(_mm, "mm_dmn", dkvc, full["wkv_cross"], mode="nt", out_dtype=F32)
    _, gs["ln_mem_w"] = _rms_bwd(mem, small["ln_mem_w"], dmn, None, "rms_mem_bwd")
    net.ahead.append(gs["ln_mem_w"])
    if net.comm:
        call(_mm, "mm_dwin_own", h_own, dproj, mode="tn", out_dtype=BF16, res=_Late(lambda: net.recv["w_in"]),
             pair_sum_of="w_in")
    else:
        call(_mm, "mm_dwin", h, dproj, mode="tn", out_dtype=BF16, grad_of="w_in")
    dh = call(_mm, "mm_dh", dproj, full["w_in"], mode="nt", out_dtype=F32)
    dx, gs["ln_mix_w"] = _rms_bwd(x, small["ln_mix_w"], dh, dx1, "rms_mix_bwd")
    return loss, dx, gs


WEIGHT_ORDER = ("ln_mix_w", "w_in", "hg_norm_w", "hg_lower_bounds", "w_branch_a", "w_branch_b", "w_out",
                "ln_cross_w", "ln_mem_w", "wq_cross", "wkv_cross", "wo_cross", "ln_ffn_w", "w1", "w3", "w2",
                "ln_final_w")


def kernel(x, mem, ln_mix_w, w_in, hg_norm_w, hg_lower_bounds, w_branch_a, w_branch_b, w_out, ln_cross_w, ln_mem_w, wq_cross, wkv_cross, wo_cross, ln_ffn_w, w1, w3, w2, ln_final_w, loss_target, m_ln_mix_w, m_w_in, m_hg_norm_w, m_hg_lower_bounds, m_w_branch_a, m_w_branch_b, m_w_out, m_ln_cross_w, m_ln_mem_w, m_wq_cross, m_wkv_cross, m_wo_cross, m_ln_ffn_w, m_w1, m_w3, m_w2, m_ln_final_w, v_ln_mix_w, v_w_in, v_hg_norm_w, v_hg_lower_bounds, v_w_branch_a, v_w_branch_b, v_w_out, v_ln_cross_w, v_ln_mem_w, v_wq_cross, v_wkv_cross, v_wo_cross, v_ln_ffn_w, v_w1, v_w3, v_w2, v_ln_final_w):
    args = dict(locals())
    w = {n: args[n] for n in WEIGHT_ORDER}
    m = {n: args["m_" + n] for n in WEIGHT_ORDER}
    v = {n: args["v_" + n] for n in WEIGHT_ORDER}
    shapes = {n: w[n].shape for n in WEIGHT_ORDER}
    mat = lambda a: a.reshape(a.shape[-2:])
    shard_shapes = {n: shapes[n][-2:] for n in BIG_WEIGHTS}

    pos = _mesh_scalars()

    def cast(f, token=None):
        return _cast_into_full({n: mat(w[n]) for n in BIG_WEIGHTS if WEIGHT_PLACE[n][0] == f}, f, pos,
                               FULL_SPECS, WEIGHT_PLACE, f"cast_{f}", token)

    net = _Net({"w_in": cast("w_in")}, pos, shard_shapes)
    net.start([(f"ring_a{i}", (*job, "a")) for i, job in enumerate(W_IN_PIECES)], "ring", "gather_start_w_in")
    rest = {f: cast(f, net.token) for f in FULL_SPECS if f != "w_in"}
    net.full.update(rest)
    small = {n: w[n].reshape(1, -1) for n in SMALL_ROWS}
    small["hg_norm_w"] = w["hg_norm_w"].reshape(1, HEAD_DIM)
    small["hg_lower_bounds"] = w["hg_lower_bounds"]
    x2d = x.reshape(SEQ, D_MODEL)
    h, *h_halves = _rms_fwd_halves(x2d, small["ln_mix_w"], pos, "rms_mix")
    for i, job in enumerate(W_IN_PIECES):
        net.wait(f"ring_a{i}", after=[*rest.values(), h] if i == 0 else ())
        net.start([(f"ring_b{i}", (*job, "b"))], "ring", f"gather_pass_on_w_in{i}")
    net.start(GATHER_GROUPS, "gather", "gather_start_rest")
    for i, job in enumerate(W_IN_PIECES):
        net.wait(f"ring_b{i}")
        net.run_comm(("d2d", [job]), f"gather_hand_over_w_in{i}")
    core_row = jnp.full((1, D_MODEL), lax.axis_index("c").astype(F32))
    loss, dx, gs = _local_step(net, x2d, h, mem.reshape(MEM_LEN, D_MODEL), loss_target.reshape(SEQ, D_MODEL), small,
                               h_halves, core_row)

    out_g, out_d, out_m, out_v = {}, {}, {}, {}
    net.last = dx
    for step in FINISH:
        if step[0] == "adamw":
            for n in step[1]:
                out_d[n], out_m[n], out_v[n], out_g[n] = net.call(_adamw, f"adamw_{n}", mat(w[n]), net.grads[n],
                                                                  mat(m[n]), mat(v[n]))
        elif step[0] == "wait":
            net.wait(step[1], after=list(out_d.values()))
        elif step[0] == "share_and_small":
            pad = lambda a: jnp.pad(a, ((0, 0), (0, D_MODEL - a.shape[1])))
            part = jnp.concatenate(
                [gs[n] for n in SMALL_ROWS]
                + [pad(jnp.concatenate([gs["hg_norm_w"][0], loss], axis=1)), pad(gs["hg_lb"]),
                   pad(gs["hg_norm_w"][1]) if len(gs["hg_norm_w"]) > 1 else jnp.zeros((1, D_MODEL), F32)], axis=0)
            rows = net.run_comm(("share", step[1]), "rs_sibling_share_and_gather_small",
                                extra=[_small_gather_comm(part)])[0][SMALL_ROWS_OF_ALL]
            sg, sd, sm, sv, loss_tot = _small_update(rows, _pack_small(w), _pack_small(m), _pack_small(v))
            for dst, packed in ((out_g, sg), (out_d, sd), (out_m, sm), (out_v, sv)):
                dst.update(_unpack_small(packed, shapes))
        else:
            net.step(step)

    result = [loss_tot[0, 0], dx.reshape(x.shape)]
    for group in (out_g, out_d, out_m, out_v):
        result += [group[n].reshape(shapes[n]) for n in WEIGHT_ORDER]
    return tuple(result)
```
